```python
import math
import jax, jax.numpy as jnp
from jax import lax
import numpy as np

D_MODEL = 1024
BATCH = 8
SEQ = 4096
DEPTH = 1

SB_HEADS = 8
SB_HEAD_DIM = 64
SB_WIDTH = SB_HEADS * SB_HEAD_DIM
Q_BLOCK = 128
CONV_WIDTH = D_MODEL // 2
CONV_K = 3
MEM_LEN = 256
MEM_HEADS = 4
MEM_HEAD_DIM = D_MODEL // MEM_HEADS
FFN_HIDDEN = -(-8 * D_MODEL // (3 * 256)) * 256
EPS = 1e-6

IN_SPLITS = (SB_WIDTH, SB_WIDTH, SB_WIDTH, CONV_WIDTH, CONV_WIDTH, CONV_WIDTH, D_MODEL, D_MODEL)
IN_WIDTH = sum(IN_SPLITS)

kernel_name = "hybrid_stickbreak_shortconv_memxattn_swiglu"


def rms_norm(x, g):
    xf = x.astype(jnp.float32)
    var = jnp.mean(xf * xf, axis=-1, keepdims=True)
    return (xf * lax.rsqrt(var + EPS) * g.astype(jnp.float32)).astype(x.dtype)


def stick_breaking_attention(q, k, v):
    seq = q.shape[2]
    scale = 1.0 / math.sqrt(q.shape[-1])
    outs = []
    for blk in range(seq // Q_BLOCK):
        t0 = blk * Q_BLOCK
        n_keys = t0 + Q_BLOCK
        qb = q[:, :, t0:n_keys].astype(jnp.float32)
        kb = k[:, :, :n_keys].astype(jnp.float32)
        vb = v[:, :, :n_keys].astype(jnp.float32)
        z = jnp.einsum('bhqd,bhkd->bhqk', qb, kb) * scale
        q_pos = t0 + jnp.arange(Q_BLOCK)[:, None]
        k_pos = jnp.arange(n_keys)[None, :]
        mask = k_pos < q_pos
        log_fail = jnp.where(mask, jax.nn.log_sigmoid(-z), 0.0)
        log_later = lax.cumsum(log_fail, axis=3, reverse=True) - log_fail
        weights = jnp.where(mask, jnp.exp(jax.nn.log_sigmoid(z) + log_later), 0.0)
        outs.append(jnp.einsum('bhqk,bhkd->bhqd', weights, vb))
    return jnp.concatenate(outs, axis=2).astype(v.dtype)


def causal_depthwise_conv(u, w):
    c = u.shape[-1]
    return lax.conv_general_dilated(
        u, w[:, None, :].astype(u.dtype), window_strides=(1,), padding=((CONV_K - 1, 0),),
        dimension_numbers=('NWC', 'WIO', 'NWC'), feature_group_count=c)


def memory_cross_attention(h, m, w_q, w_kv, w_o):
    b, s, _ = h.shape
    mlen = m.shape[1]
    q = (h @ w_q).reshape(b, s, MEM_HEADS, MEM_HEAD_DIM)
    kv = (m @ w_kv).reshape(b, mlen, 2, MEM_HEADS, MEM_HEAD_DIM)
    k, v = kv[:, :, 0], kv[:, :, 1]
    scores = jnp.einsum('bshd,bmhd->bhsm', q.astype(jnp.float32), k.astype(jnp.float32))
    probs = jax.nn.softmax(scores / math.sqrt(MEM_HEAD_DIM), axis=-1)
    o = jnp.einsum('bhsm,bmhd->bshd', probs, v.astype(jnp.float32)).astype(h.dtype)
    return o.reshape(b, s, D_MODEL) @ w_o


def _fwd_setup_inputs(seed: int = 0) -> dict:
    key = jax.random.key(seed)
    ks = jax.random.split(key, 20)
    f32 = jnp.float32

    def w(k, shape, fan_in):
        return jax.random.normal(k, shape, f32) * (fan_in ** -0.5)

    def gain(k, shape):
        return 1.0 + 0.05 * jax.random.normal(k, shape, f32)

    return {
        "x": jax.random.normal(ks[0], (BATCH, SEQ, D_MODEL), f32),
        "mem": jax.random.normal(ks[1], (BATCH, MEM_LEN, D_MODEL), f32),
        "norm_mix": gain(ks[2], (DEPTH, D_MODEL)),
        "w_in": w(ks[3], (DEPTH, D_MODEL, IN_WIDTH), D_MODEL),
        "conv_w": w(ks[4], (DEPTH, CONV_K, CONV_WIDTH), CONV_K),
        "w_branch_a": w(ks[5], (DEPTH, SB_WIDTH, D_MODEL), SB_WIDTH),
        "w_branch_b": w(ks[6], (DEPTH, CONV_WIDTH, D_MODEL), CONV_WIDTH),
        "w_mix_out": w(ks[7], (DEPTH, D_MODEL, D_MODEL), D_MODEL),
        "norm_mem_q": gain(ks[8], (DEPTH, D_MODEL)),
        "norm_mem_kv": gain(ks[9], (DEPTH, D_MODEL)),
        "w_mem_q": w(ks[10], (DEPTH, D_MODEL, D_MODEL), D_MODEL),
        "w_mem_kv": w(ks[11], (DEPTH, D_MODEL, 2 * D_MODEL), D_MODEL),
        "w_mem_o": w(ks[12], (DEPTH, D_MODEL, D_MODEL), D_MODEL),
        "norm_ffn": gain(ks[13], (DEPTH, D_MODEL)),
        "w_ffn_in": w(ks[14], (DEPTH, D_MODEL, 2 * FFN_HIDDEN), D_MODEL),
        "w_ffn_out": w(ks[15], (DEPTH, FFN_HIDDEN, D_MODEL), FFN_HIDDEN),
        "norm_final": gain(ks[16], (D_MODEL,)),
    }


def _fwd_reference(x, mem, norm_mix, w_in, conv_w, w_branch_a, w_branch_b, w_mix_out,
              norm_mem_q, norm_mem_kv, w_mem_q, w_mem_kv, w_mem_o,
              norm_ffn, w_ffn_in, w_ffn_out, norm_final):
    b, s, _ = x.shape
    split_points = list(np.cumsum(IN_SPLITS)[:-1])
    for l in range(DEPTH):
        h = rms_norm(x, norm_mix[l])
        proj = h @ w_in[l]
        q_a, k_a, v_a, u_b, gate_b, gate_c, g_a, g_b = jnp.split(proj, split_points, axis=-1)

        def heads(t):
            return t.reshape(b, s, SB_HEADS, SB_HEAD_DIM).transpose(0, 2, 1, 3)

        o_a = stick_breaking_attention(heads(q_a), heads(k_a), heads(v_a))
        o_a = o_a.transpose(0, 2, 1, 3).reshape(b, s, SB_WIDTH)
        y_b = gate_b * causal_depthwise_conv(gate_c * u_b, conv_w[l])

        branch_a = o_a @ w_branch_a[l]
        branch_b = y_b @ w_branch_b[l]
        merged = jax.nn.sigmoid(g_a) * branch_a + jax.nn.sigmoid(g_b) * branch_b
        x = x + merged @ w_mix_out[l]

        x = x + memory_cross_attention(rms_norm(x, norm_mem_q[l]), rms_norm(mem, norm_mem_kv[l]),
                                       w_mem_q[l], w_mem_kv[l], w_mem_o[l])

        hf = rms_norm(x, norm_ffn[l])
        gate, up = jnp.split(hf @ w_ffn_in[l], 2, axis=-1)
        x = x + (jax.nn.silu(gate) * up) @ w_ffn_out[l]
    return rms_norm(x, norm_final)


import jax as _jax
import jax.numpy as _jnp

TWIN_FORMAT = 'train_step'
FWD_PARAMS = ['x', 'mem', 'norm_mix', 'w_in', 'conv_w', 'w_branch_a', 'w_branch_b', 'w_mix_out', 'norm_mem_q', 'norm_mem_kv', 'w_mem_q', 'w_mem_kv', 'w_mem_o', 'norm_ffn', 'w_ffn_in', 'w_ffn_out', 'norm_final']
TWIN_WEIGHTS = ['norm_mix', 'w_in', 'conv_w', 'w_branch_a', 'w_branch_b', 'w_mix_out', 'norm_mem_q', 'norm_mem_kv', 'w_mem_q', 'w_mem_kv', 'w_mem_o', 'norm_ffn', 'w_ffn_in', 'w_ffn_out', 'norm_final']
TWIN_DIFF_INPUT = 'x'
TWIN_INPUTS = ['x', 'mem', 'norm_mix', 'w_in', 'conv_w', 'w_branch_a', 'w_branch_b', 'w_mix_out', 'norm_mem_q', 'norm_mem_kv', 'w_mem_q', 'w_mem_kv', 'w_mem_o', 'norm_ffn', 'w_ffn_in', 'w_ffn_out', 'norm_final', 'loss_target', 'm_norm_mix', 'm_w_in', 'm_conv_w', 'm_w_branch_a', 'm_w_branch_b', 'm_w_mix_out', 'm_norm_mem_q', 'm_norm_mem_kv', 'm_w_mem_q', 'm_w_mem_kv', 'm_w_mem_o', 'm_norm_ffn', 'm_w_ffn_in', 'm_w_ffn_out', 'm_norm_final', 'v_norm_mix', 'v_w_in', 'v_conv_w', 'v_w_branch_a', 'v_w_branch_b', 'v_w_mix_out', 'v_norm_mem_q', 'v_norm_mem_kv', 'v_w_mem_q', 'v_w_mem_kv', 'v_w_mem_o', 'v_norm_ffn', 'v_w_ffn_in', 'v_w_ffn_out', 'v_norm_final']
TWIN_OUTPUTS = ['loss', 'grad_x', 'grad_norm_mix', 'grad_w_in', 'grad_conv_w', 'grad_w_branch_a', 'grad_w_branch_b', 'grad_w_mix_out', 'grad_norm_mem_q', 'grad_norm_mem_kv', 'grad_w_mem_q', 'grad_w_mem_kv', 'grad_w_mem_o', 'grad_norm_ffn', 'grad_w_ffn_in', 'grad_w_ffn_out', 'grad_norm_final', 'delta_norm_mix', 'delta_w_in', 'delta_conv_w', 'delta_w_branch_a', 'delta_w_branch_b', 'delta_w_mix_out', 'delta_norm_mem_q', 'delta_norm_mem_kv', 'delta_w_mem_q', 'delta_w_mem_kv', 'delta_w_mem_o', 'delta_norm_ffn', 'delta_w_ffn_in', 'delta_w_ffn_out', 'delta_norm_final', 'new_m_norm_mix', 'new_m_w_in', 'new_m_conv_w', 'new_m_w_branch_a', 'new_m_w_branch_b', 'new_m_w_mix_out', 'new_m_norm_mem_q', 'new_m_norm_mem_kv', 'new_m_w_mem_q', 'new_m_w_mem_kv', 'new_m_w_mem_o', 'new_m_norm_ffn', 'new_m_w_ffn_in', 'new_m_w_ffn_out', 'new_m_norm_final', 'new_v_norm_mix', 'new_v_w_in', 'new_v_conv_w', 'new_v_w_branch_a', 'new_v_w_branch_b', 'new_v_w_mix_out', 'new_v_norm_mem_q', 'new_v_norm_mem_kv', 'new_v_w_mem_q', 'new_v_w_mem_kv', 'new_v_w_mem_o', 'new_v_norm_ffn', 'new_v_w_ffn_in', 'new_v_w_ffn_out', 'new_v_norm_final']
TWIN_LEAF_KINDS = {'loss': 'loss', 'grad_x': 'grad_x', 'grad_norm_mix': 'grad_w', 'grad_w_in': 'grad_w', 'grad_conv_w': 'grad_w', 'grad_w_branch_a': 'grad_w', 'grad_w_branch_b': 'grad_w', 'grad_w_mix_out': 'grad_w', 'grad_norm_mem_q': 'grad_w', 'grad_norm_mem_kv': 'grad_w', 'grad_w_mem_q': 'grad_w', 'grad_w_mem_kv': 'grad_w', 'grad_w_mem_o': 'grad_w', 'grad_norm_ffn': 'grad_w', 'grad_w_ffn_in': 'grad_w', 'grad_w_ffn_out': 'grad_w', 'grad_norm_final': 'grad_w', 'delta_norm_mix': 'delta_w', 'delta_w_in': 'delta_w', 'delta_conv_w': 'delta_w', 'delta_w_branch_a': 'delta_w', 'delta_w_branch_b': 'delta_w', 'delta_w_mix_out': 'delta_w', 'delta_norm_mem_q': 'delta_w', 'delta_norm_mem_kv': 'delta_w', 'delta_w_mem_q': 'delta_w', 'delta_w_mem_kv': 'delta_w', 'delta_w_mem_o': 'delta_w', 'delta_norm_ffn': 'delta_w', 'delta_w_ffn_in': 'delta_w', 'delta_w_ffn_out': 'delta_w', 'delta_norm_final': 'delta_w', 'new_m_norm_mix': 'new_m', 'new_m_w_in': 'new_m', 'new_m_conv_w': 'new_m', 'new_m_w_branch_a': 'new_m', 'new_m_w_branch_b': 'new_m', 'new_m_w_mix_out': 'new_m', 'new_m_norm_mem_q': 'new_m', 'new_m_norm_mem_kv': 'new_m', 'new_m_w_mem_q': 'new_m', 'new_m_w_mem_kv': 'new_m', 'new_m_w_mem_o': 'new_m', 'new_m_norm_ffn': 'new_m', 'new_m_w_ffn_in': 'new_m', 'new_m_w_ffn_out': 'new_m', 'new_m_norm_final': 'new_m', 'new_v_norm_mix': 'new_v', 'new_v_w_in': 'new_v', 'new_v_conv_w': 'new_v', 'new_v_w_branch_a': 'new_v', 'new_v_w_branch_b': 'new_v', 'new_v_w_mix_out': 'new_v', 'new_v_norm_mem_q': 'new_v', 'new_v_norm_mem_kv': 'new_v', 'new_v_w_mem_q': 'new_v', 'new_v_w_mem_kv': 'new_v', 'new_v_w_mem_o': 'new_v', 'new_v_norm_ffn': 'new_v', 'new_v_w_ffn_in': 'new_v', 'new_v_w_ffn_out': 'new_v', 'new_v_norm_final': 'new_v'}


def _forward(args):
    return _fwd_reference(*[args[k] for k in FWD_PARAMS])


def _output_shape():
    def fwd():
        inp = _fwd_setup_inputs(0)
        return _fwd_reference(*[inp[k] for k in FWD_PARAMS])
    out = _jax.eval_shape(fwd)
    return out.shape, out.dtype

N_MICROBATCH = 1
ADAM_LR = 0.001
ADAM_B1 = 0.9
ADAM_B2 = 0.999
ADAM_EPS = 1e-08
ADAM_WD = 0.01
ADAM_STEP = 10
PER_EXAMPLE_BATCH_AXIS = {'x': 0, 'mem': 0, 'loss_target': 0}
SHARED_INPUTS = []
_WEIGHT_DTYPES = {'norm_mix': _jnp.float32, 'w_in': _jnp.float32, 'conv_w': _jnp.float32, 'w_branch_a': _jnp.float32, 'w_branch_b': _jnp.float32, 'w_mix_out': _jnp.float32, 'norm_mem_q': _jnp.float32, 'norm_mem_kv': _jnp.float32, 'w_mem_q': _jnp.float32, 'w_mem_kv': _jnp.float32, 'w_mem_o': _jnp.float32, 'norm_ffn': _jnp.float32, 'w_ffn_in': _jnp.float32, 'w_ffn_out': _jnp.float32, 'norm_final': _jnp.float32}
MOMENT_SCALE = {'norm_mix': 1.742411e-01, 'w_in': 7.784543e-02, 'conv_w': 1.263474e-01, 'w_branch_a': 5.877875e-02, 'w_branch_b': 8.955358e-02, 'w_mix_out': 1.081019e-01, 'norm_mem_q': 1.777432e-02, 'norm_mem_kv': 2.587272e-02, 'w_mem_q': 1.736425e-02, 'w_mem_kv': 1.759471e-02, 'w_mem_o': 1.782132e-02, 'norm_ffn': 1.227492e-01, 'w_ffn_in': 4.941001e-02, 'w_ffn_out': 8.111990e-02, 'norm_final': 3.210093e+01}


def _to_microbatches(a, axis):
    t = _jnp.moveaxis(a, axis, 0)
    t = t.reshape((N_MICROBATCH, t.shape[0] // N_MICROBATCH) + t.shape[1:])
    return _jnp.moveaxis(t, 1, axis + 1)


def setup_inputs(seed: int = 0) -> dict:
    inp = _fwd_setup_inputs(seed)
    key = _jax.random.fold_in(_jax.random.key(seed), 7919)
    shape, _ = _output_shape()
    out = dict(inp)
    out["loss_target"] = _jax.random.normal(_jax.random.fold_in(key, 0), shape, _jnp.float32)
    for i, name in enumerate(TWIN_WEIGHTS):
        w = inp[name].astype(_jnp.float32)
        if MOMENT_SCALE is None:
            s = _jnp.sqrt(_jnp.mean(_jnp.square(w)) + 1e-30)
        else:
            s = MOMENT_SCALE[name]
        km, kv = _jax.random.split(_jax.random.fold_in(key, i + 1))
        out[name] = w
        out["m_" + name] = s * _jax.random.normal(km, w.shape, _jnp.float32)
        out["v_" + name] = (s * s) * _jax.random.uniform(kv, w.shape, _jnp.float32, 0.5, 1.5)
    if N_MICROBATCH > 1:
        for name, axis in PER_EXAMPLE_BATCH_AXIS.items():
            out[name] = _to_microbatches(out[name], axis)
    return {'x': out['x'], 'mem': out['mem'], 'norm_mix': out['norm_mix'], 'w_in': out['w_in'], 'conv_w': out['conv_w'], 'w_branch_a': out['w_branch_a'], 'w_branch_b': out['w_branch_b'], 'w_mix_out': out['w_mix_out'], 'norm_mem_q': out['norm_mem_q'], 'norm_mem_kv': out['norm_mem_kv'], 'w_mem_q': out['w_mem_q'], 'w_mem_kv': out['w_mem_kv'], 'w_mem_o': out['w_mem_o'], 'norm_ffn': out['norm_ffn'], 'w_ffn_in': out['w_ffn_in'], 'w_ffn_out': out['w_ffn_out'], 'norm_final': out['norm_final'], 'loss_target': out['loss_target'], 'm_norm_mix': out['m_norm_mix'], 'm_w_in': out['m_w_in'], 'm_conv_w': out['m_conv_w'], 'm_w_branch_a': out['m_w_branch_a'], 'm_w_branch_b': out['m_w_branch_b'], 'm_w_mix_out': out['m_w_mix_out'], 'm_norm_mem_q': out['m_norm_mem_q'], 'm_norm_mem_kv': out['m_norm_mem_kv'], 'm_w_mem_q': out['m_w_mem_q'], 'm_w_mem_kv': out['m_w_mem_kv'], 'm_w_mem_o': out['m_w_mem_o'], 'm_norm_ffn': out['m_norm_ffn'], 'm_w_ffn_in': out['m_w_ffn_in'], 'm_w_ffn_out': out['m_w_ffn_out'], 'm_norm_final': out['m_norm_final'], 'v_norm_mix': out['v_norm_mix'], 'v_w_in': out['v_w_in'], 'v_conv_w': out['v_conv_w'], 'v_w_branch_a': out['v_w_branch_a'], 'v_w_branch_b': out['v_w_branch_b'], 'v_w_mix_out': out['v_w_mix_out'], 'v_norm_mem_q': out['v_norm_mem_q'], 'v_norm_mem_kv': out['v_norm_mem_kv'], 'v_w_mem_q': out['v_w_mem_q'], 'v_w_mem_kv': out['v_w_mem_kv'], 'v_w_mem_o': out['v_w_mem_o'], 'v_norm_ffn': out['v_norm_ffn'], 'v_w_ffn_in': out['v_w_ffn_in'], 'v_w_ffn_out': out['v_w_ffn_out'], 'v_norm_final': out['v_norm_final']}


def _loss(weights, diff, rest, loss_target):
    with _jax.named_scope("forward"):
        args = {**rest, TWIN_DIFF_INPUT: diff, **{k: w.astype(_WEIGHT_DTYPES[k]) for k, w in weights.items()}}
        y = _forward(args)
    with _jax.named_scope("loss_head"):
        err = _jnp.square(y.astype(_jnp.float32) - loss_target)
        return 0.5 * _jnp.sum(_jnp.mean(err, axis=-1)) if err.ndim else 0.5 * err


def _adamw(w, g, m, v):
    m = ADAM_B1 * m + (1.0 - ADAM_B1) * g
    v = ADAM_B2 * v + (1.0 - ADAM_B2) * _jnp.square(g)
    m_hat = m / (1.0 - ADAM_B1 ** ADAM_STEP)
    v_hat = v / (1.0 - ADAM_B2 ** ADAM_STEP)
    delta = -ADAM_LR * (m_hat / (_jnp.sqrt(v_hat) + ADAM_EPS) + ADAM_WD * w)
    return delta, m, v


def reference(x, mem, norm_mix, w_in, conv_w, w_branch_a, w_branch_b, w_mix_out, norm_mem_q, norm_mem_kv, w_mem_q, w_mem_kv, w_mem_o, norm_ffn, w_ffn_in, w_ffn_out, norm_final, loss_target, m_norm_mix, m_w_in, m_conv_w, m_w_branch_a, m_w_branch_b, m_w_mix_out, m_norm_mem_q, m_norm_mem_kv, m_w_mem_q, m_w_mem_kv, m_w_mem_o, m_norm_ffn, m_w_ffn_in, m_w_ffn_out, m_norm_final, v_norm_mix, v_w_in, v_conv_w, v_w_branch_a, v_w_branch_b, v_w_mix_out, v_norm_mem_q, v_norm_mem_kv, v_w_mem_q, v_w_mem_kv, v_w_mem_o, v_norm_ffn, v_w_ffn_in, v_w_ffn_out, v_norm_final):
    given = dict(x=x, mem=mem, norm_mix=norm_mix, w_in=w_in, conv_w=conv_w, w_branch_a=w_branch_a, w_branch_b=w_branch_b, w_mix_out=w_mix_out, norm_mem_q=norm_mem_q, norm_mem_kv=norm_mem_kv, w_mem_q=w_mem_q, w_mem_kv=w_mem_kv, w_mem_o=w_mem_o, norm_ffn=norm_ffn, w_ffn_in=w_ffn_in, w_ffn_out=w_ffn_out, norm_final=norm_final, loss_target=loss_target, m_norm_mix=m_norm_mix, m_w_in=m_w_in, m_conv_w=m_conv_w, m_w_branch_a=m_w_branch_a, m_w_branch_b=m_w_branch_b, m_w_mix_out=m_w_mix_out, m_norm_mem_q=m_norm_mem_q, m_norm_mem_kv=m_norm_mem_kv, m_w_mem_q=m_w_mem_q, m_w_mem_kv=m_w_mem_kv, m_w_mem_o=m_w_mem_o, m_norm_ffn=m_norm_ffn, m_w_ffn_in=m_w_ffn_in, m_w_ffn_out=m_w_ffn_out, m_norm_final=m_norm_final, v_norm_mix=v_norm_mix, v_w_in=v_w_in, v_conv_w=v_conv_w, v_w_branch_a=v_w_branch_a, v_w_branch_b=v_w_branch_b, v_w_mix_out=v_w_mix_out, v_norm_mem_q=v_norm_mem_q, v_norm_mem_kv=v_norm_mem_kv, v_w_mem_q=v_w_mem_q, v_w_mem_kv=v_w_mem_kv, v_w_mem_o=v_w_mem_o, v_norm_ffn=v_norm_ffn, v_w_ffn_in=v_w_ffn_in, v_w_ffn_out=v_w_ffn_out, v_norm_final=v_norm_final)
    weights = {n: given[n] for n in TWIN_WEIGHTS}
    shared = {n: given[n] for n in SHARED_INPUTS}
    per_example = {n: given[n] for n in ['x', 'mem']}
    grad_fn = _jax.value_and_grad(_loss, argnums=(0, 1))

    def one_microbatch(ex, loss_target):
        ex = dict(ex)
        diff = ex.pop(TWIN_DIFF_INPUT)
        return grad_fn(weights, diff, {**shared, **ex}, loss_target)

    if N_MICROBATCH == 1:
        loss, (grad_w, grad_x) = one_microbatch(per_example, given["loss_target"])
    else:
        def body(carry, xs):
            loss_sum, grad_sum = carry
            l_k, (gw_k, gx_k) = one_microbatch(xs[0], xs[1])
            with _jax.named_scope("update"):
                return (loss_sum + l_k, _jax.tree.map(_jnp.add, grad_sum, gw_k)), gx_k

        init = (_jnp.zeros((), _jnp.float32), _jax.tree.map(_jnp.zeros_like, weights))
        (loss, grad_w), grad_x = _jax.lax.scan(body, init, (per_example, given["loss_target"]))
    with _jax.named_scope("update"):
        delta_w, new_m, new_v = {}, {}, {}
        for n in TWIN_WEIGHTS:
            delta_w[n], new_m[n], new_v[n] = _adamw(weights[n], grad_w[n], given["m_" + n], given["v_" + n])
    return (loss, grad_x, *[grad_w[n] for n in TWIN_WEIGHTS], *[delta_w[n] for n in TWIN_WEIGHTS],
            *[new_m[n] for n in TWIN_WEIGHTS], *[new_v[n] for n in TWIN_WEIGHTS])
```

```python
import functools
import math

import jax
import jax.numpy as jnp
from jax import lax
from jax.experimental import pallas as pl
from jax.experimental.pallas import tpu as pltpu

F32 = jnp.float32
BF16 = jnp.bfloat16
MESH = pl.DeviceIdType.MESH

N_DEV = 8
N_CHIP = 4
NORM_EPS = 1e-6
SB_HEADS = 8
SB_HEAD_DIM = 64
SB_WIDTH = SB_HEADS * SB_HEAD_DIM
CONV_WIDTH = 512
MEM_HEADS = 4
ADAM_LR = 0.001
ADAM_B1 = 0.9
ADAM_B2 = 0.999
ADAM_EPS = 1e-08
ADAM_WD = 0.01
ADAM_STEP = 10

LANES = 128
VMEM_LIMIT_BYTES = 52 * 1024 * 1024
SB_TILE = 256

ANY = pl.BlockSpec(memory_space=pl.ANY)


def _params(n_grid):
    return pltpu.CompilerParams(dimension_semantics=("arbitrary",) * n_grid, vmem_limit_bytes=VMEM_LIMIT_BYTES)


def _bdot(a, b, dims):
    return lax.dot_general(a.astype(BF16), b.astype(BF16), (dims, ((), ())), preferred_element_type=F32)


NN = ((1,), (0,))
NT = ((1,), (1,))
TN = ((0,), (0,))


def _mm_body(dims, n_k, has_add, *refs):
    if has_add:
        a_ref, b_ref, add_ref, o_ref, acc_ref = refs
    else:
        a_ref, b_ref, o_ref, acc_ref = refs
        add_ref = None
    k = pl.program_id(2)
    part = _bdot(a_ref[...], b_ref[...], dims)

    def finish(total):
        if add_ref is not None:
            total = total + add_ref[...]
        o_ref[...] = total.astype(o_ref.dtype)

    if n_k == 1:
        finish(part)
    else:
        @pl.when(k == 0)
        def _():
            acc_ref[...] = part

        @pl.when(jnp.logical_and(k > 0, k < n_k - 1))
        def _():
            acc_ref[...] += part

        @pl.when(k == n_k - 1)
        def _():
            finish(acc_ref[...] + part)


def _mm_nn(a, w3, *, name, out_dtype=BF16, add=None, tm=1024, tk=1024, tn=None):
    m, kk = a.shape
    j, _, n = w3.shape
    tm, tk, tn = min(tm, m), min(tk, kk), n if tn is None else tn
    n_k, n_t = kk // tk, n // tn
    in_specs = [pl.BlockSpec((tm, tk), lambda i, jj, k: (i, k)),
                pl.BlockSpec((None, tk, tn), lambda i, jj, k: (jj // n_t, k, jj % n_t))]
    args = [a, w3]
    if add is not None:
        in_specs.append(pl.BlockSpec((tm, tn), lambda i, jj, k: (i, jj)))
        args.append(add)
    return pl.pallas_call(
        functools.partial(_mm_body, NN, n_k, add is not None), name=name,
        grid=(m // tm, j * n_t, n_k), in_specs=in_specs,
        out_specs=pl.BlockSpec((tm, tn), lambda i, jj, k: (i, jj)),
        out_shape=jax.ShapeDtypeStruct((m, j * n), out_dtype),
        scratch_shapes=[pltpu.VMEM((tm, tn), F32)], compiler_params=_params(3))(*args)


def _mm_nt(dy, w3, *, name, out_dtype=BF16, tm=1024, tn=1024, tc=None):
    m = dy.shape[0]
    j, kk, n = w3.shape
    tm, tn, tc = min(tm, m), min(tn, kk), n if tc is None else tc
    n_c = n // tc
    return pl.pallas_call(
        functools.partial(_mm_body, NT, j * n_c, False), name=name,
        grid=(m // tm, kk // tn, j * n_c),
        in_specs=[pl.BlockSpec((tm, tc), lambda i, q, jj: (i, jj)),
                  pl.BlockSpec((None, tn, tc), lambda i, q, jj: (jj // n_c, q, jj % n_c))],
        out_specs=pl.BlockSpec((tm, tn), lambda i, q, jj: (i, q)),
        out_shape=jax.ShapeDtypeStruct((m, kk), out_dtype),
        scratch_shapes=[pltpu.VMEM((tm, tn), F32)], compiler_params=_params(3))(dy, w3)


def _mm_tn(a, dy, n, *, name, out_dtype=BF16, tm=1024, tk=1024, tn=None):
    t, kk = a.shape
    j = dy.shape[1] // n
    tm, tk, tn = min(tm, kk), min(tk, t), n if tn is None else tn
    n_t = n // tn
    return pl.pallas_call(
        functools.partial(_mm_body, TN, t // tk, False), name=name,
        grid=(kk // tm, j * n_t, t // tk),
        in_specs=[pl.BlockSpec((tk, tm), lambda i, jj, k: (k, i)),
                  pl.BlockSpec((tk, tn), lambda i, jj, k: (k, jj))],
        out_specs=pl.BlockSpec((None, tm, tn), lambda i, jj, k: (jj // n_t, i, jj % n_t)),
        out_shape=jax.ShapeDtypeStruct((j, kk, n), out_dtype),
        scratch_shapes=[pltpu.VMEM((tm, tn), F32)], compiler_params=_params(3))(a, dy)


def _rows(body, ins, outs, *, n_rows, tm, name):
    tm = min(tm, n_rows)
    n_steps = n_rows // tm
    in_specs, args = [], []
    for arr, kind, width, block in ins:
        if kind == "row":
            in_specs.append(pl.BlockSpec((tm, width), functools.partial(lambda i, b: (i, b), b=block)))
        elif kind == "prev":
            in_specs.append(pl.BlockSpec((tm, width), functools.partial(lambda i, b: (jnp.maximum(i - 1, 0), b), b=block)))
        elif kind == "next":
            in_specs.append(pl.BlockSpec((tm, width), functools.partial(lambda i, b: (jnp.minimum(i + 1, n_steps - 1), b), b=block)))
        else:
            in_specs.append(pl.BlockSpec(arr.shape, functools.partial(lambda i, nd: (0,) * nd, nd=arr.ndim)))
        args.append(arr)
    out_specs, out_shape = [], []
    for shape, dtype, kind in outs:
        if kind == "row":
            out_specs.append(pl.BlockSpec((tm, shape[1]), lambda i: (i, 0)))
        else:
            out_specs.append(pl.BlockSpec(shape, functools.partial(lambda i, nd: (0,) * nd, nd=len(shape))))
        out_shape.append(jax.ShapeDtypeStruct(shape, dtype))

    def kern(*refs):
        body(pl.program_id(0), n_steps, *refs)

    return pl.pallas_call(kern, name=name, grid=(n_steps,), in_specs=in_specs, out_specs=out_specs,
                          out_shape=out_shape, compiler_params=_params(1))(*args)


def _acc_rows(i, ref, value):
    @pl.when(i == 0)
    def _():
        ref[...] = jnp.zeros_like(ref)
    ref[...] += jnp.broadcast_to(value, ref.shape)


def _rms_fwd(x, g, *, name, tm=512):
    s, d = x.shape

    def body(i, n, x_ref, g_ref, h_ref):
        xv = x_ref[...]
        r = lax.rsqrt(jnp.mean(xv * xv, axis=-1, keepdims=True) + NORM_EPS)
        h_ref[...] = (xv * r * g_ref[...]).astype(BF16)

    return _rows(body, [(x, "row", d, 0), (g, "full", 0, 0)], [((s, d), BF16, "row")], n_rows=s, tm=tm, name=name)[0]


def _rms_bwd(x, g, dh, dres, *, name, tm=512):
    s, d = x.shape

    def body(i, n, x_ref, g_ref, dh_ref, *rest):
        if dres is None:
            dx_ref, dxb_ref, dg_ref = rest
        else:
            dres_ref, dx_ref, dxb_ref, dg_ref = rest
        xv = x_ref[...]
        r = lax.rsqrt(jnp.mean(xv * xv, axis=-1, keepdims=True) + NORM_EPS)
        xhat = xv * r
        dhv = dh_ref[...].astype(F32)
        dxhat = dhv * g_ref[...]
        dx = r * (dxhat - xhat * jnp.mean(dxhat * xhat, axis=-1, keepdims=True))
        if dres is not None:
            dx = dx + dres_ref[...]
        dx_ref[...] = dx
        dxb_ref[...] = dx.astype(BF16)
        _acc_rows(i, dg_ref, jnp.sum(dhv * xhat, axis=0, keepdims=True))

    ins = [(x, "row", d, 0), (g, "full", 0, 0), (dh, "row", d, 0)]
    if dres is not None:
        ins.append((dres, "row", d, 0))
    return _rows(body, ins, [((s, d), F32, "row"), ((s, d), BF16, "row"), ((8, d), F32, "acc")],
                 n_rows=s, tm=tm, name=name)


def _loss_bwd(x, g, target, *, name, tm=512):
    s, d = x.shape

    def body(i, n, x_ref, g_ref, t_ref, dx_ref, dxb_ref, dg_ref, loss_ref):
        xv = x_ref[...]
        gv = g_ref[...]
        r = lax.rsqrt(jnp.mean(xv * xv, axis=-1, keepdims=True) + NORM_EPS)
        xhat = xv * r
        err = xhat * gv - t_ref[...]
        part = 0.5 * jnp.sum(jnp.mean(err * err, axis=-1, keepdims=True), axis=0, keepdims=True)
        dy = err * (1.0 / d)
        dxhat = dy * gv
        dx = r * (dxhat - xhat * jnp.mean(dxhat * xhat, axis=-1, keepdims=True))
        dx_ref[...] = dx
        dxb_ref[...] = dx.astype(BF16)
        _acc_rows(i, dg_ref, jnp.sum(dy * xhat, axis=0, keepdims=True))
        _acc_rows(i, loss_ref, part)

    return _rows(body, [(x, "row", d, 0), (g, "full", 0, 0), (target, "row", d, 0)],
                 [((s, d), F32, "row"), ((s, d), BF16, "row"), ((8, d), F32, "acc"), ((8, LANES), F32, "acc")],
                 n_rows=s, tm=tm, name=name)


def _sigmoid(v):
    return 1.0 / (1.0 + jnp.exp(-v))


def _swiglu_fwd(gu, *, name, tm=512):
    s, two_f = gu.shape
    f = two_f // 2

    def body(i, n, gu_ref, act_ref):
        gate = gu_ref[:, :f].astype(F32)
        up = gu_ref[:, f:].astype(F32)
        act_ref[...] = (gate * _sigmoid(gate) * up).astype(BF16)

    return _rows(body, [(gu, "row", two_f, 0)], [((s, f), BF16, "row")], n_rows=s, tm=tm, name=name)[0]


def _swiglu_bwd(dact, gu, *, name, tm=512):
    s, two_f = gu.shape
    f = two_f // 2

    def body(i, n, dact_ref, gu_ref, dgu_ref):
        gate = gu_ref[:, :f].astype(F32)
        up = gu_ref[:, f:].astype(F32)
        da = dact_ref[...].astype(F32)
        sg = _sigmoid(gate)
        silu = gate * sg
        dgu_ref[:, :f] = (da * up * (sg + silu * (1.0 - sg))).astype(BF16)
        dgu_ref[:, f:] = (da * silu).astype(BF16)

    return _rows(body, [(dact, "row", f, 0), (gu, "row", two_f, 0)], [((s, two_f), BF16, "row")],
                 n_rows=s, tm=tm, name=name)[0]


def _gates_fwd(br_a, br_b, proj, *, name, tm=512):
    s, d = br_a.shape

    def body(i, n, a_ref, b_ref, ga_ref, gb_ref, o_ref):
        o_ref[...] = (_sigmoid(ga_ref[...].astype(F32)) * a_ref[...].astype(F32)
                      + _sigmoid(gb_ref[...].astype(F32)) * b_ref[...].astype(F32)).astype(BF16)

    return _rows(body, [(br_a, "row", d, 0), (br_b, "row", d, 0), (proj, "row", d, 3), (proj, "row", d, 4)],
                 [((s, d), BF16, "row")], n_rows=s, tm=tm, name=name)[0]


def _gates_bwd(dmerged, br_a, br_b, proj, *, name, tm=512):
    s, d = br_a.shape

    def body(i, n, dm_ref, a_ref, b_ref, ga_ref, gb_ref, da_ref, db_ref, dg_ref):
        dm = dm_ref[...].astype(F32)
        sa = _sigmoid(ga_ref[...].astype(F32))
        sb = _sigmoid(gb_ref[...].astype(F32))
        da_ref[...] = (dm * sa).astype(BF16)
        db_ref[...] = (dm * sb).astype(BF16)
        dg_ref[:, :d] = (dm * a_ref[...].astype(F32) * sa * (1.0 - sa)).astype(BF16)
        dg_ref[:, d:] = (dm * b_ref[...].astype(F32) * sb * (1.0 - sb)).astype(BF16)

    return _rows(body, [(dmerged, "row", d, 0), (br_a, "row", d, 0), (br_b, "row", d, 0),
                        (proj, "row", d, 3), (proj, "row", d, 4)],
                 [((s, d), BF16, "row"), ((s, d), BF16, "row"), ((s, 2 * d), BF16, "row")],
                 n_rows=s, tm=tm, name=name)


def _shift_down(cur, prev, k, first):
    tm = cur.shape[0]
    row = lax.broadcasted_iota(jnp.int32, cur.shape, 0)
    out = jnp.where(row >= k, pltpu.roll(cur, k, 0), pltpu.roll(prev, k, 0))
    return jnp.where(jnp.logical_and(first, row < k), 0.0, out)


def _shift_up(cur, nxt, k, last):
    tm = cur.shape[0]
    row = lax.broadcasted_iota(jnp.int32, cur.shape, 0)
    out = jnp.where(row < tm - k, pltpu.roll(cur, tm - k, 0), pltpu.roll(nxt, tm - k, 0))
    return jnp.where(jnp.logical_and(last, row >= tm - k), 0.0, out)


def _conv_fwd(proj, conv_w, *, name, tm=512):
    s = proj.shape[0]
    c = CONV_WIDTH

    def body(i, n, u_ref, gb_ref, gc_ref, up_ref, gcp_ref, w_ref, y_ref):
        cu = gc_ref[...].astype(F32) * u_ref[...].astype(F32)
        cup = gcp_ref[...].astype(F32) * up_ref[...].astype(F32)
        first = i == 0
        y = (w_ref[0:1, :] * _shift_down(cu, cup, 2, first) + w_ref[1:2, :] * _shift_down(cu, cup, 1, first)
             + w_ref[2:3, :] * cu)
        y_ref[...] = (gb_ref[...].astype(F32) * y).astype(BF16)

    return _rows(body, [(proj, "row", c, 3), (proj, "row", c, 4), (proj, "row", c, 5),
                        (proj, "prev", c, 3), (proj, "prev", c, 5), (conv_w, "full", 0, 0)],
                 [((s, c), BF16, "row")], n_rows=s, tm=tm, name=name)[0]


def _conv_bwd(dy_b, proj, conv_w, *, name, tm=512):
    s = proj.shape[0]
    c = CONV_WIDTH

    def body(i, n, dy_ref, u_ref, gb_ref, gc_ref, up_ref, gcp_ref, dyn_ref, gbn_ref, w_ref, d_ref, dw_ref):
        first, last = i == 0, i == n - 1
        u = u_ref[...].astype(F32)
        gb = gb_ref[...].astype(F32)
        gc = gc_ref[...].astype(F32)
        cu = gc * u
        cup = gcp_ref[...].astype(F32) * up_ref[...].astype(F32)
        cu1 = _shift_down(cu, cup, 1, first)
        cu2 = _shift_down(cu, cup, 2, first)
        conv = w_ref[0:1, :] * cu2 + w_ref[1:2, :] * cu1 + w_ref[2:3, :] * cu
        dy = dy_ref[...].astype(F32)
        dyc = dy * gb
        dycn = dyn_ref[...].astype(F32) * gbn_ref[...].astype(F32)
        dcu = (w_ref[2:3, :] * dyc + w_ref[1:2, :] * _shift_up(dyc, dycn, 1, last)
               + w_ref[0:1, :] * _shift_up(dyc, dycn, 2, last))
        d_ref[:, 0:c] = (dcu * gc).astype(BF16)
        d_ref[:, c:2 * c] = (dy * conv).astype(BF16)
        d_ref[:, 2 * c:3 * c] = (dcu * u).astype(BF16)
        row = lax.broadcasted_iota(jnp.int32, (8, c), 0)
        dw = (jnp.where(row == 0, jnp.sum(dyc * cu2, axis=0, keepdims=True), 0.0)
              + jnp.where(row == 1, jnp.sum(dyc * cu1, axis=0, keepdims=True), 0.0)
              + jnp.where(row == 2, jnp.sum(dyc * cu, axis=0, keepdims=True), 0.0))

        @pl.when(first)
        def _():
            dw_ref[...] = jnp.zeros_like(dw_ref)
        dw_ref[...] += dw

    return _rows(body, [(dy_b, "row", c, 0), (proj, "row", c, 3), (proj, "row", c, 4), (proj, "row", c, 5),
                        (proj, "prev", c, 3), (proj, "prev", c, 5), (dy_b, "next", c, 0), (proj, "next", c, 4),
                        (conv_w, "full", 0, 0)],
                 [((s, 3 * c), BF16, "row"), ((8, c), F32, "acc")], n_rows=s, tm=tm, name=name)


def _mem_probs(q, k, scale):
    sc = _bdot(q, k, NT) * scale
    sc = sc - jnp.max(sc, axis=-1, keepdims=True)
    p = jnp.exp(sc)
    return p / jnp.sum(p, axis=-1, keepdims=True)


def _memattn_fwd(qm, kv, *, name, tm=512):
    s, d = qm.shape
    hd = d // MEM_HEADS
    scale = 1.0 / math.sqrt(hd)

    def body(i, n, q_ref, kv_ref, o_ref):
        for h in range(MEM_HEADS):
            cols = slice(h * hd, (h + 1) * hd)
            p = _mem_probs(q_ref[:, cols], kv_ref[:, cols], scale)
            o_ref[:, cols] = _bdot(p, kv_ref[:, d + h * hd:d + (h + 1) * hd], NN).astype(BF16)

    return _rows(body, [(qm, "row", d, 0), (kv, "full", 0, 0)], [((s, d), BF16, "row")], n_rows=s, tm=tm, name=name)[0]


def _memattn_bwd(dom, qm, kv, *, name, tm=512):
    s, d = qm.shape
    hd = d // MEM_HEADS
    scale = 1.0 / math.sqrt(hd)

    def body(i, n, do_ref, q_ref, kv_ref, dq_ref, dkv_ref):
        @pl.when(i == 0)
        def _():
            dkv_ref[...] = jnp.zeros_like(dkv_ref)
        for h in range(MEM_HEADS):
            cols = slice(h * hd, (h + 1) * hd)
            vcols = slice(d + h * hd, d + (h + 1) * hd)
            q, k, v, do = q_ref[:, cols], kv_ref[:, cols], kv_ref[:, vcols], do_ref[:, cols]
            p = _mem_probs(q, k, scale)
            dp = _bdot(do, v, NT)
            ds = p * (dp - jnp.sum(dp * p, axis=-1, keepdims=True)) * scale
            dq_ref[:, cols] = _bdot(ds, k, NN).astype(BF16)
            dkv_ref[:, cols] += _bdot(ds, q, TN)
            dkv_ref[:, vcols] += _bdot(p, do, TN)

    return _rows(body, [(dom, "row", d, 0), (qm, "row", d, 0), (kv, "full", 0, 0)],
                 [((s, d), BF16, "row"), (kv.shape, F32, "acc")], n_rows=s, tm=tm, name=name)


def _sb_consts(t):
    row = lax.broadcasted_iota(jnp.int32, (t, t), 0)
    col = lax.broadcasted_iota(jnp.int32, (t, t), 1)
    lane = lax.broadcasted_iota(jnp.int32, (t, LANES), 1)
    return row, col, lane < SB_HEAD_DIM


def _log_fail(z):
    return jnp.minimum(-z, 0.0) - jnp.log(1.0 + jnp.exp(-jnp.abs(z)))


def _tri_sum(v, tri):
    hi = v.astype(BF16)
    lo = (v - hi.astype(F32)).astype(BF16)
    return _bdot(hi, tri, NN) + _bdot(lo, tri, NN)


def _sb_fwd(proj, *, name):
    s = proj.shape[0]
    t = SB_TILE
    n_q = s // t
    scale = 1.0 / math.sqrt(SB_HEAD_DIM)
    k_blk, v_blk = SB_WIDTH // LANES, 2 * SB_WIDTH // LANES

    def body(q_ref, k_ref, v_ref, o_ref, c_ref, acc_ref):
        i = pl.program_id(1)
        row, col, head0 = _sb_consts(t)
        later = (row > col).astype(BF16)
        valid = col < row
        qs = q_ref[...] * scale
        q2 = (jnp.where(head0, qs, 0), jnp.where(head0, 0, qs))

        def tile(kb, carry, diag):
            kt = k_ref[pl.ds(pl.multiple_of(kb * t, t), t), :]
            vt = v_ref[pl.ds(pl.multiple_of(kb * t, t), t), :]
            new = []
            for h in range(2):
                z = _bdot(q2[h], kt, NT)
                lf = _log_fail(z)
                if diag:
                    lf = jnp.where(valid, lf, 0.0)
                cum = _tri_sum(lf, later)
                w = jnp.exp(z + lf + cum + carry[h])
                if diag:
                    w = jnp.where(valid, w, 0.0)
                acc_ref[h] += _bdot(w, vt, NN)
                new.append(carry[h] + cum[:, 0:1] + lf[:, 0:1])
            return tuple(new)

        acc_ref[...] = jnp.zeros_like(acc_ref)
        zero = jnp.zeros((t, 1), F32)
        carry = tile(i, (zero, zero), True)
        carry = lax.fori_loop(0, i, lambda n, c: tile(i - 1 - n, c, False), carry)
        o_ref[...] = jnp.where(head0, acc_ref[0], acc_ref[1]).astype(BF16)
        c_ref[...] = jnp.where(lax.broadcasted_iota(jnp.int32, (t, 2), 1) == 0, carry[0], carry[1])

    return pl.pallas_call(
        body, name=name, grid=(SB_HEADS // 2, n_q),
        in_specs=[pl.BlockSpec((t, LANES), lambda p, i: (i, p)),
                  pl.BlockSpec((s, LANES), lambda p, i: (0, k_blk + p)),
                  pl.BlockSpec((s, LANES), lambda p, i: (0, v_blk + p))],
        out_specs=[pl.BlockSpec((t, LANES), lambda p, i: (i, p)),
                   pl.BlockSpec((None, t, 2), lambda p, i: (p, i, 0))],
        out_shape=[jax.ShapeDtypeStruct((s, SB_WIDTH), BF16), jax.ShapeDtypeStruct((SB_HEADS // 2, s, 2), F32)],
        scratch_shapes=[pltpu.VMEM((2, t, LANES), F32)], compiler_params=_params(2))(proj, proj, proj)


def _sb_bwd(proj, do_a, ctot, *, name):
    s = proj.shape[0]
    t = SB_TILE
    n_q = s // t
    scale = 1.0 / math.sqrt(SB_HEAD_DIM)
    k_blk, v_blk = SB_WIDTH // LANES, 2 * SB_WIDTH // LANES

    def body(q_ref, k_ref, v_ref, do_ref, c_ref, dq_ref, dk_ref, dv_ref, dq_acc, dk_acc, dv_acc):
        i = pl.program_id(1)
        row, col, head0 = _sb_consts(t)
        upto = (row <= col).astype(BF16)
        before = (row < col).astype(BF16)
        valid = col < row
        qs = q_ref[...] * scale
        q2 = (jnp.where(head0, qs, 0), jnp.where(head0, 0, qs))
        do = do_ref[...]
        do2 = (jnp.where(head0, do, 0), jnp.where(head0, 0, do))
        ctot2 = (c_ref[:, 0:1], c_ref[:, 1:2])

        @pl.when(i == 0)
        def _():
            dk_acc[...] = jnp.zeros_like(dk_acc)
            dv_acc[...] = jnp.zeros_like(dv_acc)
        dq_acc[...] = jnp.zeros_like(dq_acc)

        def tile(kb, carry, diag):
            rows = pl.ds(pl.multiple_of(kb * t, t), t)
            kt = k_ref[rows, :]
            vt = v_ref[rows, :]
            new = []
            for h in range(2):
                lf_before, g_before = carry[2 * h], carry[2 * h + 1]
                z = _bdot(q2[h], kt, NT)
                lf = _log_fail(z)
                if diag:
                    lf = jnp.where(valid, lf, 0.0)
                cum = _tri_sum(lf, upto)
                log_later = ctot2[h] - lf_before - cum
                beta = jnp.exp(z + lf)
                w = beta * jnp.exp(log_later)
                if diag:
                    w = jnp.where(valid, w, 0.0)
                g = w * _bdot(do2[h], vt, NT)
                g_sum = g_before + _bdot(g, before, NN)
                dz = g * jnp.exp(lf) - beta * g_sum
                if diag:
                    dz = jnp.where(valid, dz, 0.0)
                dq_acc[h] += _bdot(dz, kt, NN)
                dk_acc[rows, :] += _bdot(dz, q2[h], TN)
                dv_acc[rows, :] += _bdot(w, do2[h], TN)
                t_last = slice(t - 1, t)
                new += [lf_before + cum[:, t_last], g_sum[:, t_last] + g[:, t_last]]
            return tuple(new)

        zero = jnp.zeros((t, 1), F32)
        carry = lax.fori_loop(0, i, lambda n, c: tile(n, c, False), (zero,) * 4)
        tile(i, carry, True)
        dq_ref[...] = (jnp.where(head0, dq_acc[0], dq_acc[1]) * scale).astype(BF16)

        @pl.when(i == n_q - 1)
        def _():
            dk_ref[...] = dk_acc[...].astype(BF16)
            dv_ref[...] = dv_acc[...].astype(BF16)

    n_pair = SB_HEADS // 2
    outs = pl.pallas_call(
        body, name=name, grid=(n_pair, n_q),
        in_specs=[pl.BlockSpec((t, LANES), lambda p, i: (i, p)),
                  pl.BlockSpec((s, LANES), lambda p, i: (0, k_blk + p)),
                  pl.BlockSpec((s, LANES), lambda p, i: (0, v_blk + p)),
                  pl.BlockSpec((t, LANES), lambda p, i: (i, p)),
                  pl.BlockSpec((None, t, 2), lambda p, i: (p, i, 0))],
        out_specs=[pl.BlockSpec((t, LANES), lambda p, i: (i, p)),
                   pl.BlockSpec((s, LANES), lambda p, i: (0, p)),
                   pl.BlockSpec((s, LANES), lambda p, i: (0, p))],
        out_shape=[jax.ShapeDtypeStruct((s, SB_WIDTH), BF16)] * 3,
        scratch_shapes=[pltpu.VMEM((2, t, LANES), F32), pltpu.VMEM((s, LANES), F32), pltpu.VMEM((s, LANES), F32)],
        compiler_params=_params(2))(proj, proj, proj, do_a, ctot)
    return jnp.concatenate(outs, axis=1)


def _local_step(x, mem, target, gains, conv_w, w):
    g_mix, g_memq, g_memkv, g_ffn, g_fin = gains
    n_in, n_br, n_kv, n_fi = w["in"].shape[2], w["a"].shape[2], w["kv"].shape[2], w["fi"].shape[2]
    d = x.shape[1]

    h0 = _rms_fwd(x, g_mix, name="rms_mix")
    proj = _mm_nn(h0, w["in"], name="mm_in")
    o_a, ctot = _sb_fwd(proj, name="sb_fwd")
    y_b = _conv_fwd(proj, conv_w, name="conv_fwd")
    br_a = _mm_nn(o_a, w["a"], name="mm_branch_a")
    br_b = _mm_nn(y_b, w["b"], name="mm_branch_b")
    merged = _gates_fwd(br_a, br_b, proj, name="gates_fwd")
    x1 = _mm_nn(merged, w["mix"], name="mm_mix", out_dtype=F32, add=x)
    hq = _rms_fwd(x1, g_memq, name="rms_memq")
    qm = _mm_nn(hq, w["mq"], name="mm_memq")
    mn = _rms_fwd(mem, g_memkv, name="rms_memkv")
    kv = _mm_nn(mn, w["kv"], name="mm_memkv")
    om = _memattn_fwd(qm, kv, name="memattn_fwd")
    x2 = _mm_nn(om, w["mo"], name="mm_memo", out_dtype=F32, add=x1)
    hf = _rms_fwd(x2, g_ffn, name="rms_ffn")
    gu = _mm_nn(hf, w["fi"], name="mm_ffn_in", tn=512)
    act = _swiglu_fwd(gu, name="swiglu_fwd")
    x3 = _mm_nn(act, w["fo"], name="mm_ffn_out", out_dtype=F32, add=x2, tk=1408)

    dx3, dx3b, dg_fin, loss = _loss_bwd(x3, g_fin, target, name="loss_bwd")

    gw = {}
    gw["fo"] = _mm_tn(act, dx3b, d, name="mm_d_w_ffn_out", tm=1408)
    dact = _mm_nt(dx3b, w["fo"], name="mm_d_act", tn=1408)
    dgu = _swiglu_bwd(dact, gu, name="swiglu_bwd")
    gw["fi"] = _mm_tn(hf, dgu, n_fi, name="mm_d_w_ffn_in", tn=512)
    dhf = _mm_nt(dgu, w["fi"], name="mm_d_hf", out_dtype=F32, tc=512)
    dx2, dx2b, dg_ffn = _rms_bwd(x2, g_ffn, dhf, dx3, name="rms_ffn_bwd")

    gw["mo"] = _mm_tn(om, dx2b, d, name="mm_d_w_memo")
    dom = _mm_nt(dx2b, w["mo"], name="mm_d_om")
    dqm, dkv = _memattn_bwd(dom, qm, kv, name="memattn_bwd")
    gw["mq"] = _mm_tn(hq, dqm, d, name="mm_d_w_memq")
    dhq = _mm_nt(dqm, w["mq"], name="mm_d_hq", out_dtype=F32)
    dx1, dx1b, dg_memq = _rms_bwd(x1, g_memq, dhq, dx2, name="rms_memq_bwd")
    gw["kv"] = _mm_tn(mn, dkv, n_kv, name="mm_d_w_memkv")
    dmn = _mm_nt(dkv, w["kv"], name="mm_d_mn", out_dtype=F32)
    _, _, dg_memkv = _rms_bwd(mem, g_memkv, dmn, None, name="rms_memkv_bwd")

    gw["mix"] = _mm_tn(merged, dx1b, d, name="mm_d_w_mix")
    dmerged = _mm_nt(dx1b, w["mix"], name="mm_d_merged")
    dbr_a, dbr_b, dgab = _gates_bwd(dmerged, br_a, br_b, proj, name="gates_bwd")
    gw["a"] = _mm_tn(o_a, dbr_a, n_br, name="mm_d_w_branch_a")
    do_a = _mm_nt(dbr_a, w["a"], name="mm_d_o_a")
    gw["b"] = _mm_tn(y_b, dbr_b, n_br, name="mm_d_w_branch_b")
    dy_b = _mm_nt(dbr_b, w["b"], name="mm_d_y_b")
    dconv, dconv_w = _conv_bwd(dy_b, proj, conv_w, name="conv_bwd")
    dqkv = _sb_bwd(proj, do_a, ctot, name="sb_bwd")
    dproj = jnp.concatenate([dqkv, dconv, dgab], axis=1)
    gw["in"] = _mm_tn(h0, dproj, n_in, name="mm_d_w_in")
    dh0 = _mm_nt(dproj, w["in"], name="mm_d_h0", out_dtype=F32)
    dx0, _, dg_mix = _rms_bwd(x, g_mix, dh0, dx1, name="rms_mix_bwd")

    small = (dg_mix, dg_memq, dg_memkv, dg_ffn, dg_fin, dconv_w, loss)
    return dx0, gw, small


def _place():
    x, y, c = lax.axis_index("x"), lax.axis_index("y"), lax.axis_index("c")
    return x, y, c


def _all_gather(shards, *, name):
    n = len(shards)

    def body(*refs):
        ins, outs = refs[:n], refs[n:2 * n]
        send_sems, recv_sems, local_sems = refs[2 * n:]
        x, y, c = _place()
        me, sibling = (x, y, c), (x, y, 1 - c)
        chips = [(1 - x, y), (x, 1 - y), (1 - x, 1 - y)]

        def slot(px, py, pc):
            return 4 * px + 2 * py + pc

        def copy(a, k, block, to, src=None):
            dst = outs[a].at[slot(*block)]
            return pltpu.make_async_remote_copy(
                src_ref=dst if src is None else src, dst_ref=dst, send_sem=send_sems.at[a * 7 + k],
                recv_sem=recv_sems.at[a * 7 + k], device_id=to, device_id_type=MESH)

        mine, first, passed = [], [], []
        for a in range(n):
            cp = pltpu.make_async_copy(ins[a], outs[a].at[slot(*me)], local_sems.at[a])
            cp.start()
            mine.append(cp)
            first.append(copy(a, 0, me, sibling, src=ins[a]))
            first += [copy(a, 1 + j, me, (*chip, c), src=ins[a]) for j, chip in enumerate(chips)]
        for cp in first:
            cp.start()
        for a in range(n):
            for j, chip in enumerate(chips):
                copy(a, 1 + j, (*chip, c), me).wait_recv()
                fwd = copy(a, 4 + j, (*chip, c), sibling)
                fwd.start()
                passed.append(fwd)
        for a in range(n):
            copy(a, 0, sibling, me).wait_recv()
            for j, chip in enumerate(chips):
                copy(a, 4 + j, (*chip, 1 - c), me).wait_recv()
        for cp in first + passed:
            cp.wait_send()
        for cp in mine:
            cp.wait()

    return pl.pallas_call(
        body, name=name, in_specs=[ANY] * n, out_specs=[ANY] * n,
        out_shape=[jax.ShapeDtypeStruct((N_DEV,) + s.shape, s.dtype) for s in shards],
        scratch_shapes=[pltpu.SemaphoreType.DMA((7 * n,)), pltpu.SemaphoreType.DMA((7 * n,)),
                        pltpu.SemaphoreType.DMA((n,))],
        )(*shards)


def _exchange_sibling(parts, *, name):
    n = len(parts)

    def body(*refs):
        ins, outs = refs[:n], refs[n:2 * n]
        send_sems, recv_sems = refs[2 * n:]
        x, y, c = _place()
        sibling = (x, y, 1 - c)
        copies = []
        for a in range(n):
            for q in range(N_CHIP):
                copies.append(pltpu.make_async_remote_copy(
                    src_ref=ins[a].at[2 * q + 1 - c], dst_ref=outs[a].at[q], send_sem=send_sems.at[a * N_CHIP + q],
                    recv_sem=recv_sems.at[a * N_CHIP + q], device_id=sibling, device_id_type=MESH))
        for cp in copies:
            cp.start()
        for cp in copies:
            cp.wait_recv()
        for cp in copies:
            cp.wait_send()

    return pl.pallas_call(
        body, name=name, in_specs=[ANY] * n, out_specs=[ANY] * n,
        out_shape=[jax.ShapeDtypeStruct((N_CHIP,) + p.shape[1:], p.dtype) for p in parts],
        scratch_shapes=[pltpu.SemaphoreType.DMA((N_CHIP * n,)), pltpu.SemaphoreType.DMA((N_CHIP * n,))],
        )(*parts)


def _exchange_chips(parts, *, name):
    n = len(parts)

    def body(*refs):
        ins, outs = refs[:n], refs[n:2 * n]
        send_sems, recv_sems = refs[2 * n:]
        x, y, c = _place()
        chips = [(1 - x, y), (x, 1 - y), (1 - x, 1 - y)]
        copies = []
        for a in range(n):
            for j, (px, py) in enumerate(chips):
                copies.append(pltpu.make_async_remote_copy(
                    src_ref=ins[a].at[2 * px + py], dst_ref=outs[a].at[j], send_sem=send_sems.at[a * 3 + j],
                    recv_sem=recv_sems.at[a * 3 + j], device_id=(px, py, c), device_id_type=MESH))
        for cp in copies:
            cp.start()
        for cp in copies:
            cp.wait_recv()
        for cp in copies:
            cp.wait_send()

    return pl.pallas_call(
        body, name=name, in_specs=[ANY] * n, out_specs=[ANY] * n,
        out_shape=[jax.ShapeDtypeStruct((3,) + p.shape[1:], p.dtype) for p in parts],
        scratch_shapes=[pltpu.SemaphoreType.DMA((3 * n,)), pltpu.SemaphoreType.DMA((3 * n,))],
        )(*parts)


def _row_tile(a, target=512):
    tm = min(a, target)
    while a % tm:
        tm -= 8
    return tm


def _sum_with_sibling(part, recv, core, *, name):
    _, a, b = part.shape
    tm = _row_tile(a)

    def body(core_ref, p_ref, r_ref, o_ref):
        o_ref[...] = (p_ref[...].astype(F32) + r_ref[...].astype(F32)).astype(o_ref.dtype)

    return pl.pallas_call(
        body, name=name,
        grid_spec=pltpu.PrefetchScalarGridSpec(
            num_scalar_prefetch=1, grid=(N_CHIP, a // tm),
            in_specs=[pl.BlockSpec((None, tm, b), lambda q, i, core_ref: (2 * q + core_ref[0], i, 0)),
                      pl.BlockSpec((None, tm, b), lambda q, i, core_ref: (q, i, 0))],
            out_specs=pl.BlockSpec((None, tm, b), lambda q, i, core_ref: (q, i, 0))),
        out_shape=jax.ShapeDtypeStruct((N_CHIP, a, b), part.dtype), compiler_params=_params(2))(core, part, recv)


def _adam_math(wv, g, m, v):
    m = ADAM_B1 * m + (1.0 - ADAM_B1) * g
    v = ADAM_B2 * v + (1.0 - ADAM_B2) * (g * g)
    m_hat = m / (1.0 - ADAM_B1 ** ADAM_STEP)
    v_hat = v / (1.0 - ADAM_B2 ** ADAM_STEP)
    delta = -ADAM_LR * (m_hat / (jnp.sqrt(v_hat) + ADAM_EPS) + ADAM_WD * wv)
    return delta, m, v


def _adam_sharded(wv, m, v, own, recv, chip, *, name):
    a, b = wv.shape
    tm = _row_tile(a)

    def body(chip_ref, w_ref, m_ref, v_ref, own_ref, recv_ref, g_ref, d_ref, nm_ref, nv_ref):
        g = own_ref[...].astype(F32)
        for j in range(3):
            g = g + recv_ref[j].astype(F32)
        delta, nm, nv = _adam_math(w_ref[...], g, m_ref[...], v_ref[...])
        g_ref[...] = g
        d_ref[...] = delta
        nm_ref[...] = nm
        nv_ref[...] = nv

    tile = pl.BlockSpec((tm, b), lambda i, chip_ref: (i, 0))
    return pl.pallas_call(
        body, name=name,
        grid_spec=pltpu.PrefetchScalarGridSpec(
            num_scalar_prefetch=1, grid=(a // tm,),
            in_specs=[tile, tile, tile,
                      pl.BlockSpec((None, tm, b), lambda i, chip_ref: (chip_ref[0], i, 0)),
                      pl.BlockSpec((3, tm, b), lambda i, chip_ref: (0, i, 0))],
            out_specs=[tile] * 4),
        out_shape=[jax.ShapeDtypeStruct((a, b), F32)] * 4, compiler_params=_params(1))(chip, wv, m, v, own, recv)


def _sum_devices(gathered, *, name):
    _, r, c = gathered.shape

    def body(g_ref, o_ref):
        total = g_ref[0]
        for j in range(1, N_DEV):
            total = total + g_ref[j]
        o_ref[...] = total

    return pl.pallas_call(body, name=name, out_shape=jax.ShapeDtypeStruct((r, c), F32))(gathered)


def _adam_small(wv, g, m, v, *, name):
    def body(w_ref, g_ref, m_ref, v_ref, d_ref, nm_ref, nv_ref):
        delta, nm, nv = _adam_math(w_ref[...], g_ref[...], m_ref[...], v_ref[...])
        d_ref[...] = delta
        nm_ref[...] = nm
        nv_ref[...] = nv

    return pl.pallas_call(body, name=name, out_shape=[jax.ShapeDtypeStruct(wv.shape, F32)] * 3)(wv, g, m, v)


BIG = ("in", "a", "b", "mix", "mq", "kv", "mo", "fi", "fo")
SMALL_ROWS = 16


def kernel(x, mem, norm_mix, w_in, conv_w, w_branch_a, w_branch_b, w_mix_out, norm_mem_q, norm_mem_kv, w_mem_q, w_mem_kv, w_mem_o, norm_ffn, w_ffn_in, w_ffn_out, norm_final, loss_target, m_norm_mix, m_w_in, m_conv_w, m_w_branch_a, m_w_branch_b, m_w_mix_out, m_norm_mem_q, m_norm_mem_kv, m_w_mem_q, m_w_mem_kv, m_w_mem_o, m_norm_ffn, m_w_ffn_in, m_w_ffn_out, m_norm_final, v_norm_mix, v_w_in, v_conv_w, v_w_branch_a, v_w_branch_b, v_w_mix_out, v_norm_mem_q, v_norm_mem_kv, v_w_mem_q, v_w_mem_kv, v_w_mem_o, v_norm_ffn, v_w_ffn_in, v_w_ffn_out, v_norm_final):
    d = x.shape[-1]
    xi, yi, ci = lax.axis_index("x"), lax.axis_index("y"), lax.axis_index("c")
    core = jnp.reshape(ci, (1,)).astype(jnp.int32)
    chip = jnp.reshape(2 * xi + yi, (1,)).astype(jnp.int32)
    dev = 4 * xi + 2 * yi + ci

    big_w = dict(zip(BIG, (w_in, w_branch_a, w_branch_b, w_mix_out, w_mem_q, w_mem_kv, w_mem_o, w_ffn_in, w_ffn_out)))
    big_m = dict(zip(BIG, (m_w_in, m_w_branch_a, m_w_branch_b, m_w_mix_out, m_w_mem_q, m_w_mem_kv, m_w_mem_o, m_w_ffn_in, m_w_ffn_out)))
    big_v = dict(zip(BIG, (v_w_in, v_w_branch_a, v_w_branch_b, v_w_mix_out, v_w_mem_q, v_w_mem_kv, v_w_mem_o, v_w_ffn_in, v_w_ffn_out)))
    row_sharded = ("mix", "mq", "mo", "fo")

    conv_pad = jnp.zeros((8, LANES), F32).at[:3, :conv_w.shape[-1]].set(conv_w[0])
    shards = [big_w[k][0].astype(BF16) for k in BIG] + [conv_pad]
    gathered = _all_gather(shards, name="gather_weights")
    w = dict(zip(BIG, gathered[:-1]))
    for k in row_sharded:
        w[k] = w[k].reshape(1, -1, w[k].shape[-1])
    n_fi = w["fi"].shape[-1]
    w["fi"] = jnp.transpose(w["fi"], (1, 0, 2)).reshape(1, d, N_DEV * n_fi)
    conv_full = jnp.transpose(gathered[-1][:, :3, :conv_w.shape[-1]], (1, 0, 2)).reshape(3, CONV_WIDTH)

    gains = (norm_mix, norm_mem_q, norm_mem_kv, norm_ffn, norm_final.reshape(1, d))
    dx0, gw, small = _local_step(x[0], mem[0], loss_target[0], gains, conv_full, w)
    for k in row_sharded:
        gw[k] = gw[k].reshape(N_DEV, -1, gw[k].shape[-1])
    gw["fi"] = jnp.transpose(gw["fi"].reshape(d, N_DEV, n_fi), (1, 0, 2))

    parts = [gw[k] for k in BIG]
    from_sibling = _exchange_sibling(parts, name="grads_to_sibling")
    chip_sums = [_sum_with_sibling(p, r, core, name="sum_with_sibling_" + k) for k, p, r in zip(BIG, parts, from_sibling)]
    from_chips = _exchange_chips(chip_sums, name="grads_to_chips")

    grads, deltas, new_m, new_v = {}, {}, {}, {}
    for k, own, recv in zip(BIG, chip_sums, from_chips):
        lead = big_w[k].shape
        g, dl, nm, nv = _adam_sharded(big_w[k][0], big_m[k][0], big_v[k][0], own, recv, chip, name="adam_" + k)
        grads[k], deltas[k], new_m[k], new_v[k] = (t.reshape(lead) for t in (g, dl, nm, nv))

    dg_mix, dg_memq, dg_memkv, dg_ffn, dg_fin, dconv_w, loss = small
    conv_rows = jnp.zeros((3, d), F32).at[:, :CONV_WIDTH].set(dconv_w[:3])
    block = jnp.concatenate([dg_mix[:1], dg_memq[:1], dg_memkv[:1], dg_ffn[:1], dg_fin[:1], conv_rows,
                             jnp.broadcast_to(loss[:1, :1], (1, d)), jnp.zeros((SMALL_ROWS - 9, d), F32)], axis=0)
    total = _sum_devices(_all_gather([block], name="gather_small")[0], name="sum_small")
    n_conv = conv_w.shape[-1]
    g_conv = lax.dynamic_slice(total[5:8, :CONV_WIDTH], (0, dev * n_conv), (3, n_conv))
    small_w = [norm_mix, norm_mem_q, norm_mem_kv, norm_ffn, norm_final.reshape(1, d), conv_w[0]]
    small_m = [m_norm_mix, m_norm_mem_q, m_norm_mem_kv, m_norm_ffn, m_norm_final.reshape(1, d), m_conv_w[0]]
    small_v = [v_norm_mix, v_norm_mem_q, v_norm_mem_kv, v_norm_ffn, v_norm_final.reshape(1, d), v_conv_w[0]]
    small_g = [total[0:1], total[1:2], total[2:3], total[3:4], total[4:5], g_conv]
    small_names = ["norm_mix", "norm_mem_q", "norm_mem_kv", "norm_ffn", "norm_final", "conv_w"]
    sg, sd, sm, sv = {}, {}, {}, {}
    for nme, wv, g, m, v in zip(small_names, small_w, small_g, small_m, small_v):
        dl, nm, nv = _adam_small(wv, g, m, v, name="adam_" + nme)
        shape = norm_final.shape if nme == "norm_final" else (conv_w.shape if nme == "conv_w" else wv.shape)
        sg[nme], sd[nme], sm[nme], sv[nme] = (t.reshape(shape) for t in (g, dl, nm, nv))

    def ordered(big, sml):
        return (sml["norm_mix"], big["in"], sml["conv_w"], big["a"], big["b"], big["mix"], sml["norm_mem_q"],
                sml["norm_mem_kv"], big["mq"], big["kv"], big["mo"], sml["norm_ffn"], big["fi"], big["fo"],
                sml["norm_final"])

    loss_out = total[8, 0]
    grad_x = dx0.reshape(x.shape)
    return (loss_out, grad_x, *ordered(grads, sg), *ordered(deltas, sd), *ordered(new_m, sm), *ordered(new_v, sv))
```

```python
import functools
import math

import jax
import jax.numpy as jnp
from jax import lax
from jax.experimental import pallas as pl
from jax.experimental.pallas import tpu as pltpu

F32 = jnp.float32
BF16 = jnp.bfloat16
MESH = pl.DeviceIdType.MESH

N_DEV = 8
N_CHIP = 4
NORM_EPS = 1e-6
SB_HEADS = 8
SB_HEAD_DIM = 64
SB_WIDTH = SB_HEADS * SB_HEAD_DIM
CONV_WIDTH = 512
MEM_HEADS = 4
ADAM_LR = 0.001
ADAM_B1 = 0.9
ADAM_B2 = 0.999
ADAM_EPS = 1e-08
ADAM_WD = 0.01
ADAM_STEP = 10

LANES = 128
VMEM_LIMIT_BYTES = 52 * 1024 * 1024
SB_TILE = 256
SB_DEAD = 110.0

ANY = pl.BlockSpec(memory_space=pl.ANY)


def _params(n_grid):
    return pltpu.CompilerParams(dimension_semantics=("arbitrary",) * n_grid, vmem_limit_bytes=VMEM_LIMIT_BYTES)


def _bdot(a, b, dims):
    return lax.dot_general(a.astype(BF16), b.astype(BF16), (dims, ((), ())), preferred_element_type=F32)


NN = ((1,), (0,))
NT = ((1,), (1,))
TN = ((0,), (0,))


def _mm_body(dims, n_k, has_add, *refs):
    if has_add:
        a_ref, b_ref, add_ref, o_ref, acc_ref = refs
    else:
        a_ref, b_ref, o_ref, acc_ref = refs
        add_ref = None
    k = pl.program_id(2)
    part = _bdot(a_ref[...], b_ref[...], dims)

    def finish(total):
        if add_ref is not None:
            total = total + add_ref[...]
        o_ref[...] = total.astype(o_ref.dtype)

    if n_k == 1:
        finish(part)
    else:
        @pl.when(k == 0)
        def _():
            acc_ref[...] = part

        @pl.when(jnp.logical_and(k > 0, k < n_k - 1))
        def _():
            acc_ref[...] += part

        @pl.when(k == n_k - 1)
        def _():
            finish(acc_ref[...] + part)


def _mm_nn(a, w3, *, name, out_dtype=BF16, add=None, tm=1024, tk=1024, tn=None):
    m, kk = a.shape
    j, _, n = w3.shape
    tm, tk, tn = min(tm, m), min(tk, kk), n if tn is None else tn
    n_k, n_t = kk // tk, n // tn
    in_specs = [pl.BlockSpec((tm, tk), lambda i, jj, k: (i, k)),
                pl.BlockSpec((None, tk, tn), lambda i, jj, k: (jj // n_t, k, jj % n_t))]
    args = [a, w3]
    if add is not None:
        in_specs.append(pl.BlockSpec((tm, tn), lambda i, jj, k: (i, jj)))
        args.append(add)
    return pl.pallas_call(
        functools.partial(_mm_body, NN, n_k, add is not None), name=name,
        grid=(m // tm, j * n_t, n_k), in_specs=in_specs,
        out_specs=pl.BlockSpec((tm, tn), lambda i, jj, k: (i, jj)),
        out_shape=jax.ShapeDtypeStruct((m, j * n), out_dtype),
        scratch_shapes=[pltpu.VMEM((tm, tn), F32)], compiler_params=_params(3))(*args)


def _mm_nt(dy, w3, *, name, out_dtype=BF16, tm=1024, tn=1024, tc=None):
    m = dy.shape[0]
    j, kk, n = w3.shape
    tm, tn, tc = min(tm, m), min(tn, kk), n if tc is None else tc
    n_c = n // tc
    return pl.pallas_call(
        functools.partial(_mm_body, NT, j * n_c, False), name=name,
        grid=(m // tm, kk // tn, j * n_c),
        in_specs=[pl.BlockSpec((tm, tc), lambda i, q, jj: (i, jj)),
                  pl.BlockSpec((None, tn, tc), lambda i, q, jj: (jj // n_c, q, jj % n_c))],
        out_specs=pl.BlockSpec((tm, tn), lambda i, q, jj: (i, q)),
        out_shape=jax.ShapeDtypeStruct((m, kk), out_dtype),
        scratch_shapes=[pltpu.VMEM((tm, tn), F32)], compiler_params=_params(3))(dy, w3)


def _mm_tn(a, dy, n, *, name, out_dtype=BF16, tm=1024, tk=1024, tn=None):
    t, kk = a.shape
    j = dy.shape[1] // n
    tm, tk, tn = min(tm, kk), min(tk, t), n if tn is None else tn
    n_t = n // tn
    return pl.pallas_call(
        functools.partial(_mm_body, TN, t // tk, False), name=name,
        grid=(kk // tm, j * n_t, t // tk),
        in_specs=[pl.BlockSpec((tk, tm), lambda i, jj, k: (k, i)),
                  pl.BlockSpec((tk, tn), lambda i, jj, k: (k, jj))],
        out_specs=pl.BlockSpec((None, tm, tn), lambda i, jj, k: (jj // n_t, i, jj % n_t)),
        out_shape=jax.ShapeDtypeStruct((j, kk, n), out_dtype),
        scratch_shapes=[pltpu.VMEM((tm, tn), F32)], compiler_params=_params(3))(a, dy)


def _rows(body, ins, outs, *, n_rows, tm, name):
    tm = min(tm, n_rows)
    n_steps = n_rows // tm
    in_specs, args = [], []
    for arr, kind, width, block in ins:
        if kind == "row":
            in_specs.append(pl.BlockSpec((tm, width), functools.partial(lambda i, b: (i, b), b=block)))
        elif kind == "prev":
            in_specs.append(pl.BlockSpec((tm, width), functools.partial(lambda i, b: (jnp.maximum(i - 1, 0), b), b=block)))
        elif kind == "next":
            in_specs.append(pl.BlockSpec((tm, width), functools.partial(lambda i, b: (jnp.minimum(i + 1, n_steps - 1), b), b=block)))
        else:
            in_specs.append(pl.BlockSpec(arr.shape, functools.partial(lambda i, nd: (0,) * nd, nd=arr.ndim)))
        args.append(arr)
    out_specs, out_shape = [], []
    for shape, dtype, kind in outs:
        if kind == "row":
            out_specs.append(pl.BlockSpec((tm, shape[1]), lambda i: (i, 0)))
        else:
            out_specs.append(pl.BlockSpec(shape, functools.partial(lambda i, nd: (0,) * nd, nd=len(shape))))
        out_shape.append(jax.ShapeDtypeStruct(shape, dtype))

    def kern(*refs):
        body(pl.program_id(0), n_steps, *refs)

    return pl.pallas_call(kern, name=name, grid=(n_steps,), in_specs=in_specs, out_specs=out_specs,
                          out_shape=out_shape, compiler_params=_params(1))(*args)


def _acc_rows(i, ref, value):
    @pl.when(i == 0)
    def _():
        ref[...] = jnp.zeros_like(ref)
    ref[...] += jnp.broadcast_to(value, ref.shape)


def _rms_fwd(x, g, *, name, tm=512):
    s, d = x.shape

    def body(i, n, x_ref, g_ref, h_ref):
        xv = x_ref[...]
        r = lax.rsqrt(jnp.mean(xv * xv, axis=-1, keepdims=True) + NORM_EPS)
        h_ref[...] = (xv * r * g_ref[...]).astype(BF16)

    return _rows(body, [(x, "row", d, 0), (g, "full", 0, 0)], [((s, d), BF16, "row")], n_rows=s, tm=tm, name=name)[0]


def _rms_bwd(x, g, dh, dres, *, name, tm=512):
    s, d = x.shape

    def body(i, n, x_ref, g_ref, dh_ref, *rest):
        if dres is None:
            dx_ref, dxb_ref, dg_ref = rest
        else:
            dres_ref, dx_ref, dxb_ref, dg_ref = rest
        xv = x_ref[...]
        r = lax.rsqrt(jnp.mean(xv * xv, axis=-1, keepdims=True) + NORM_EPS)
        xhat = xv * r
        dhv = dh_ref[...].astype(F32)
        dxhat = dhv * g_ref[...]
        dx = r * (dxhat - xhat * jnp.mean(dxhat * xhat, axis=-1, keepdims=True))
        if dres is not None:
            dx = dx + dres_ref[...]
        dx_ref[...] = dx
        dxb_ref[...] = dx.astype(BF16)
        _acc_rows(i, dg_ref, jnp.sum(dhv * xhat, axis=0, keepdims=True))

    ins = [(x, "row", d, 0), (g, "full", 0, 0), (dh, "row", d, 0)]
    if dres is not None:
        ins.append((dres, "row", d, 0))
    return _rows(body, ins, [((s, d), F32, "row"), ((s, d), BF16, "row"), ((8, d), F32, "acc")],
                 n_rows=s, tm=tm, name=name)


def _loss_bwd(x, g, target, *, name, tm=512):
    s, d = x.shape

    def body(i, n, x_ref, g_ref, t_ref, dx_ref, dxb_ref, dg_ref, loss_ref):
        xv = x_ref[...]
        gv = g_ref[...]
        r = lax.rsqrt(jnp.mean(xv * xv, axis=-1, keepdims=True) + NORM_EPS)
        xhat = xv * r
        err = xhat * gv - t_ref[...]
        part = 0.5 * jnp.sum(jnp.mean(err * err, axis=-1, keepdims=True), axis=0, keepdims=True)
        dy = err * (1.0 / d)
        dxhat = dy * gv
        dx = r * (dxhat - xhat * jnp.mean(dxhat * xhat, axis=-1, keepdims=True))
        dx_ref[...] = dx
        dxb_ref[...] = dx.astype(BF16)
        _acc_rows(i, dg_ref, jnp.sum(dy * xhat, axis=0, keepdims=True))
        _acc_rows(i, loss_ref, part)

    return _rows(body, [(x, "row", d, 0), (g, "full", 0, 0), (target, "row", d, 0)],
                 [((s, d), F32, "row"), ((s, d), BF16, "row"), ((8, d), F32, "acc"), ((8, LANES), F32, "acc")],
                 n_rows=s, tm=tm, name=name)


def _sigmoid(v):
    return 1.0 / (1.0 + jnp.exp(-v))


def _swiglu_fwd(gu, *, name, tm=512):
    s, two_f = gu.shape
    f = two_f // 2

    def body(i, n, gu_ref, act_ref):
        gate = gu_ref[:, :f].astype(F32)
        up = gu_ref[:, f:].astype(F32)
        act_ref[...] = (gate * _sigmoid(gate) * up).astype(BF16)

    return _rows(body, [(gu, "row", two_f, 0)], [((s, f), BF16, "row")], n_rows=s, tm=tm, name=name)[0]


def _swiglu_bwd(dact, gu, *, name, tm=512):
    s, two_f = gu.shape
    f = two_f // 2

    def body(i, n, dact_ref, gu_ref, dgu_ref):
        gate = gu_ref[:, :f].astype(F32)
        up = gu_ref[:, f:].astype(F32)
        da = dact_ref[...].astype(F32)
        sg = _sigmoid(gate)
        silu = gate * sg
        dgu_ref[:, :f] = (da * up * (sg + silu * (1.0 - sg))).astype(BF16)
        dgu_ref[:, f:] = (da * silu).astype(BF16)

    return _rows(body, [(dact, "row", f, 0), (gu, "row", two_f, 0)], [((s, two_f), BF16, "row")],
                 n_rows=s, tm=tm, name=name)[0]


def _gates_fwd(br_a, br_b, proj, *, name, tm=512):
    s, d = br_a.shape

    def body(i, n, a_ref, b_ref, ga_ref, gb_ref, o_ref):
        o_ref[...] = (_sigmoid(ga_ref[...].astype(F32)) * a_ref[...].astype(F32)
                      + _sigmoid(gb_ref[...].astype(F32)) * b_ref[...].astype(F32)).astype(BF16)

    return _rows(body, [(br_a, "row", d, 0), (br_b, "row", d, 0), (proj, "row", d, 3), (proj, "row", d, 4)],
                 [((s, d), BF16, "row")], n_rows=s, tm=tm, name=name)[0]


def _gates_bwd(dmerged, br_a, br_b, proj, *, name, tm=512):
    s, d = br_a.shape

    def body(i, n, dm_ref, a_ref, b_ref, ga_ref, gb_ref, da_ref, db_ref, dg_ref):
        dm = dm_ref[...].astype(F32)
        sa = _sigmoid(ga_ref[...].astype(F32))
        sb = _sigmoid(gb_ref[...].astype(F32))
        da_ref[...] = (dm * sa).astype(BF16)
        db_ref[...] = (dm * sb).astype(BF16)
        dg_ref[:, :d] = (dm * a_ref[...].astype(F32) * sa * (1.0 - sa)).astype(BF16)
        dg_ref[:, d:] = (dm * b_ref[...].astype(F32) * sb * (1.0 - sb)).astype(BF16)

    return _rows(body, [(dmerged, "row", d, 0), (br_a, "row", d, 0), (br_b, "row", d, 0),
                        (proj, "row", d, 3), (proj, "row", d, 4)],
                 [((s, d), BF16, "row"), ((s, d), BF16, "row"), ((s, 2 * d), BF16, "row")],
                 n_rows=s, tm=tm, name=name)


def _shift_down(cur, prev, k, first):
    tm = cur.shape[0]
    row = lax.broadcasted_iota(jnp.int32, cur.shape, 0)
    out = jnp.where(row >= k, pltpu.roll(cur, k, 0), pltpu.roll(prev, k, 0))
    return jnp.where(jnp.logical_and(first, row < k), 0.0, out)


def _shift_up(cur, nxt, k, last):
    tm = cur.shape[0]
    row = lax.broadcasted_iota(jnp.int32, cur.shape, 0)
    out = jnp.where(row < tm - k, pltpu.roll(cur, tm - k, 0), pltpu.roll(nxt, tm - k, 0))
    return jnp.where(jnp.logical_and(last, row >= tm - k), 0.0, out)


def _conv_fwd(proj, conv_w, *, name, tm=512):
    s = proj.shape[0]
    c = CONV_WIDTH

    def body(i, n, u_ref, gb_ref, gc_ref, up_ref, gcp_ref, w_ref, y_ref):
        cu = gc_ref[...].astype(F32) * u_ref[...].astype(F32)
        cup = gcp_ref[...].astype(F32) * up_ref[...].astype(F32)
        first = i == 0
        y = (w_ref[0:1, :] * _shift_down(cu, cup, 2, first) + w_ref[1:2, :] * _shift_down(cu, cup, 1, first)
             + w_ref[2:3, :] * cu)
        y_ref[...] = (gb_ref[...].astype(F32) * y).astype(BF16)

    return _rows(body, [(proj, "row", c, 3), (proj, "row", c, 4), (proj, "row", c, 5),
                        (proj, "prev", c, 3), (proj, "prev", c, 5), (conv_w, "full", 0, 0)],
                 [((s, c), BF16, "row")], n_rows=s, tm=tm, name=name)[0]


def _conv_bwd(dy_b, proj, conv_w, *, name, tm=512):
    s = proj.shape[0]
    c = CONV_WIDTH

    def body(i, n, dy_ref, u_ref, gb_ref, gc_ref, up_ref, gcp_ref, dyn_ref, gbn_ref, w_ref, d_ref, dw_ref):
        first, last = i == 0, i == n - 1
        u = u_ref[...].astype(F32)
        gb = gb_ref[...].astype(F32)
        gc = gc_ref[...].astype(F32)
        cu = gc * u
        cup = gcp_ref[...].astype(F32) * up_ref[...].astype(F32)
        cu1 = _shift_down(cu, cup, 1, first)
        cu2 = _shift_down(cu, cup, 2, first)
        conv = w_ref[0:1, :] * cu2 + w_ref[1:2, :] * cu1 + w_ref[2:3, :] * cu
        dy = dy_ref[...].astype(F32)
        dyc = dy * gb
        dycn = dyn_ref[...].astype(F32) * gbn_ref[...].astype(F32)
        dcu = (w_ref[2:3, :] * dyc + w_ref[1:2, :] * _shift_up(dyc, dycn, 1, last)
               + w_ref[0:1, :] * _shift_up(dyc, dycn, 2, last))
        d_ref[:, 0:c] = (dcu * gc).astype(BF16)
        d_ref[:, c:2 * c] = (dy * conv).astype(BF16)
        d_ref[:, 2 * c:3 * c] = (dcu * u).astype(BF16)
        row = lax.broadcasted_iota(jnp.int32, (8, c), 0)
        dw = (jnp.where(row == 0, jnp.sum(dyc * cu2, axis=0, keepdims=True), 0.0)
              + jnp.where(row == 1, jnp.sum(dyc * cu1, axis=0, keepdims=True), 0.0)
              + jnp.where(row == 2, jnp.sum(dyc * cu, axis=0, keepdims=True), 0.0))

        @pl.when(first)
        def _():
            dw_ref[...] = jnp.zeros_like(dw_ref)
        dw_ref[...] += dw

    return _rows(body, [(dy_b, "row", c, 0), (proj, "row", c, 3), (proj, "row", c, 4), (proj, "row", c, 5),
                        (proj, "prev", c, 3), (proj, "prev", c, 5), (dy_b, "next", c, 0), (proj, "next", c, 4),
                        (conv_w, "full", 0, 0)],
                 [((s, 3 * c), BF16, "row"), ((8, c), F32, "acc")], n_rows=s, tm=tm, name=name)


def _mem_probs(q, k, scale):
    sc = _bdot(q, k, NT) * scale
    sc = sc - jnp.max(sc, axis=-1, keepdims=True)
    p = jnp.exp(sc)
    return p / jnp.sum(p, axis=-1, keepdims=True)


def _memattn_fwd(qm, kv, *, name, tm=512):
    s, d = qm.shape
    hd = d // MEM_HEADS
    scale = 1.0 / math.sqrt(hd)

    def body(i, n, q_ref, kv_ref, o_ref):
        for h in range(MEM_HEADS):
            cols = slice(h * hd, (h + 1) * hd)
            p = _mem_probs(q_ref[:, cols], kv_ref[:, cols], scale)
            o_ref[:, cols] = _bdot(p, kv_ref[:, d + h * hd:d + (h + 1) * hd], NN).astype(BF16)

    return _rows(body, [(qm, "row", d, 0), (kv, "full", 0, 0)], [((s, d), BF16, "row")], n_rows=s, tm=tm, name=name)[0]


def _memattn_bwd(dom, qm, kv, *, name, tm=512):
    s, d = qm.shape
    hd = d // MEM_HEADS
    scale = 1.0 / math.sqrt(hd)

    def body(i, n, do_ref, q_ref, kv_ref, dq_ref, dkv_ref):
        @pl.when(i == 0)
        def _():
            dkv_ref[...] = jnp.zeros_like(dkv_ref)
        for h in range(MEM_HEADS):
            cols = slice(h * hd, (h + 1) * hd)
            vcols = slice(d + h * hd, d + (h + 1) * hd)
            q, k, v, do = q_ref[:, cols], kv_ref[:, cols], kv_ref[:, vcols], do_ref[:, cols]
            p = _mem_probs(q, k, scale)
            dp = _bdot(do, v, NT)
            ds = p * (dp - jnp.sum(dp * p, axis=-1, keepdims=True)) * scale
            dq_ref[:, cols] = _bdot(ds, k, NN).astype(BF16)
            dkv_ref[:, cols] += _bdot(ds, q, TN)
            dkv_ref[:, vcols] += _bdot(p, do, TN)

    return _rows(body, [(dom, "row", d, 0), (qm, "row", d, 0), (kv, "full", 0, 0)],
                 [((s, d), BF16, "row"), (kv.shape, F32, "acc")], n_rows=s, tm=tm, name=name)


def _sb_consts(t):
    row = lax.broadcasted_iota(jnp.int32, (t, t), 0)
    col = lax.broadcasted_iota(jnp.int32, (t, t), 1)
    lane = lax.broadcasted_iota(jnp.int32, (t, LANES), 1)
    return row, col, lane < SB_HEAD_DIM


def _log_fail(z):
    return jnp.minimum(-z, 0.0) - jnp.log(1.0 + jnp.exp(-jnp.abs(z)))


def _tri_sum(v, tri):
    hi = v.astype(BF16)
    lo = (v - hi.astype(F32)).astype(BF16)
    return _bdot(hi, tri, NN) + _bdot(lo, tri, NN)


def _sb_fwd(proj, *, name):
    s = proj.shape[0]
    t = SB_TILE
    n_q = s // t
    scale = 1.0 / math.sqrt(SB_HEAD_DIM)
    k_blk, v_blk = SB_WIDTH // LANES, 2 * SB_WIDTH // LANES

    def body(q_ref, k_ref, v_ref, o_ref, c_ref, first_ref, acc_ref):
        i = pl.program_id(1)
        row, col, head0 = _sb_consts(t)
        later = (row > col).astype(BF16)
        valid = col < row
        qs = q_ref[...] * scale
        q2 = (jnp.where(head0, qs, 0), jnp.where(head0, 0, qs))

        def tile(kb, carry, diag):
            kt = k_ref[pl.ds(pl.multiple_of(kb * t, t), t), :]
            vt = v_ref[pl.ds(pl.multiple_of(kb * t, t), t), :]
            new = []
            for h in range(2):
                z = _bdot(q2[h], kt, NT)
                lf = _log_fail(z)
                if diag:
                    lf = jnp.where(valid, lf, 0.0)
                cum = _tri_sum(lf, later)
                w = jnp.exp(z + lf + cum + carry[h])
                if diag:
                    w = jnp.where(valid, w, 0.0)
                acc_ref[h] += _bdot(w, vt, NN)
                new.append(carry[h] + cum[:, 0:1] + lf[:, 0:1])
            return tuple(new)

        acc_ref[...] = jnp.zeros_like(acc_ref)
        zero = jnp.zeros((t, 1), F32)
        def alive(carry):
            return (jnp.maximum(jnp.max(carry[0]), jnp.max(carry[1])) > -SB_DEAD).astype(jnp.int32)

        def step(state):
            kb, _, c0, c1 = state
            new = tile(kb, (c0, c1), False)
            return kb - 1, alive(new), new[0], new[1]

        carry = tile(i, (zero, zero), True)
        kb, _, c0, c1 = lax.while_loop(lambda st: jnp.logical_and(st[0] >= 0, st[1] > 0), step,
                                       (i - 1, alive(carry), carry[0], carry[1]))
        o_ref[...] = jnp.where(head0, acc_ref[0], acc_ref[1]).astype(BF16)
        c_ref[...] = jnp.where(lax.broadcasted_iota(jnp.int32, (t, 2), 1) == 0, c0, c1)
        first_ref[pl.program_id(0), i] = (kb + 1).astype(F32)

    return pl.pallas_call(
        body, name=name, grid=(SB_HEADS // 2, n_q),
        in_specs=[pl.BlockSpec((t, LANES), lambda p, i: (i, p)),
                  pl.BlockSpec((s, LANES), lambda p, i: (0, k_blk + p)),
                  pl.BlockSpec((s, LANES), lambda p, i: (0, v_blk + p))],
        out_specs=[pl.BlockSpec((t, LANES), lambda p, i: (i, p)),
                   pl.BlockSpec((None, t, 2), lambda p, i: (p, i, 0)),
                   pl.BlockSpec(memory_space=pltpu.SMEM)],
        out_shape=[jax.ShapeDtypeStruct((s, SB_WIDTH), BF16), jax.ShapeDtypeStruct((SB_HEADS // 2, s, 2), F32),
                   jax.ShapeDtypeStruct((SB_HEADS // 2, n_q), F32)],
        scratch_shapes=[pltpu.VMEM((2, t, LANES), F32)], compiler_params=_params(2))(proj, proj, proj)


def _sb_bwd(proj, do_a, ctot, first, *, name):
    s = proj.shape[0]
    t = SB_TILE
    n_q = s // t
    scale = 1.0 / math.sqrt(SB_HEAD_DIM)
    k_blk, v_blk = SB_WIDTH // LANES, 2 * SB_WIDTH // LANES

    def body(q_ref, k_ref, v_ref, do_ref, c_ref, first_ref, dq_ref, dk_ref, dv_ref, dq_acc, dk_acc, dv_acc):
        i = pl.program_id(1)
        kb0 = jnp.clip(first_ref[pl.program_id(0), i].astype(jnp.int32), 0, i)
        row, col, head0 = _sb_consts(t)
        upto = (row <= col).astype(BF16)
        before = (row < col).astype(BF16)
        valid = col < row
        qs = q_ref[...] * scale
        q2 = (jnp.where(head0, qs, 0), jnp.where(head0, 0, qs))
        do = do_ref[...]
        do2 = (jnp.where(head0, do, 0), jnp.where(head0, 0, do))
        ctot2 = (c_ref[:, 0:1], c_ref[:, 1:2])

        @pl.when(i == 0)
        def _():
            dk_acc[...] = jnp.zeros_like(dk_acc)
            dv_acc[...] = jnp.zeros_like(dv_acc)
        dq_acc[...] = jnp.zeros_like(dq_acc)

        def tile(kb, carry, diag):
            rows = pl.ds(pl.multiple_of(kb * t, t), t)
            kt = k_ref[rows, :]
            vt = v_ref[rows, :]
            new = []
            for h in range(2):
                lf_before, g_before = carry[2 * h], carry[2 * h + 1]
                z = _bdot(q2[h], kt, NT)
                lf = _log_fail(z)
                if diag:
                    lf = jnp.where(valid, lf, 0.0)
                cum = _tri_sum(lf, upto)
                log_later = ctot2[h] - lf_before - cum
                beta = jnp.exp(z + lf)
                w = beta * jnp.exp(log_later)
                if diag:
                    w = jnp.where(valid, w, 0.0)
                g = w * _bdot(do2[h], vt, NT)
                g_sum = g_before + _bdot(g, before, NN)
                dz = g * jnp.exp(lf) - beta * g_sum
                if diag:
                    dz = jnp.where(valid, dz, 0.0)
                dq_acc[h] += _bdot(dz, kt, NN)
                dk_acc[rows, :] += _bdot(dz, q2[h], TN)
                dv_acc[rows, :] += _bdot(w, do2[h], TN)
                t_last = slice(t - 1, t)
                new += [lf_before + cum[:, t_last], g_sum[:, t_last] + g[:, t_last]]
            return tuple(new)

        zero = jnp.zeros((t, 1), F32)
        carry = lax.fori_loop(kb0, i, lambda n, c: tile(n, c, False), (zero,) * 4)
        tile(i, carry, True)
        dq_ref[...] = (jnp.where(head0, dq_acc[0], dq_acc[1]) * scale).astype(BF16)

        @pl.when(i == n_q - 1)
        def _():
            dk_ref[...] = dk_acc[...].astype(BF16)
            dv_ref[...] = dv_acc[...].astype(BF16)

    n_pair = SB_HEADS // 2
    outs = pl.pallas_call(
        body, name=name, grid=(n_pair, n_q),
        in_specs=[pl.BlockSpec((t, LANES), lambda p, i: (i, p)),
                  pl.BlockSpec((s, LANES), lambda p, i: (0, k_blk + p)),
                  pl.BlockSpec((s, LANES), lambda p, i: (0, v_blk + p)),
                  pl.BlockSpec((t, LANES), lambda p, i: (i, p)),
                  pl.BlockSpec((None, t, 2), lambda p, i: (p, i, 0)),
                  pl.BlockSpec(memory_space=pltpu.SMEM)],
        out_specs=[pl.BlockSpec((t, LANES), lambda p, i: (i, p)),
                   pl.BlockSpec((s, LANES), lambda p, i: (0, p)),
                   pl.BlockSpec((s, LANES), lambda p, i: (0, p))],
        out_shape=[jax.ShapeDtypeStruct((s, SB_WIDTH), BF16)] * 3,
        scratch_shapes=[pltpu.VMEM((2, t, LANES), F32), pltpu.VMEM((s, LANES), F32), pltpu.VMEM((s, LANES), F32)],
        compiler_params=_params(2))(proj, proj, proj, do_a, ctot, first)
    return jnp.concatenate(outs, axis=1)


def _local_step(x, mem, target, gains, conv_w, w):
    g_mix, g_memq, g_memkv, g_ffn, g_fin = gains
    n_in, n_br, n_kv, n_fi = w["in"].shape[2], w["a"].shape[2], w["kv"].shape[2], w["fi"].shape[2]
    d = x.shape[1]

    h0 = _rms_fwd(x, g_mix, name="rms_mix")
    proj = _mm_nn(h0, w["in"], name="mm_in")
    o_a, ctot, first = _sb_fwd(proj, name="sb_fwd")
    y_b = _conv_fwd(proj, conv_w, name="conv_fwd")
    br_a = _mm_nn(o_a, w["a"], name="mm_branch_a")
    br_b = _mm_nn(y_b, w["b"], name="mm_branch_b")
    merged = _gates_fwd(br_a, br_b, proj, name="gates_fwd")
    x1 = _mm_nn(merged, w["mix"], name="mm_mix", out_dtype=F32, add=x)
    hq = _rms_fwd(x1, g_memq, name="rms_memq")
    qm = _mm_nn(hq, w["mq"], name="mm_memq")
    mn = _rms_fwd(mem, g_memkv, name="rms_memkv")
    kv = _mm_nn(mn, w["kv"], name="mm_memkv")
    om = _memattn_fwd(qm, kv, name="memattn_fwd")
    x2 = _mm_nn(om, w["mo"], name="mm_memo", out_dtype=F32, add=x1)
    hf = _rms_fwd(x2, g_ffn, name="rms_ffn")
    gu = _mm_nn(hf, w["fi"], name="mm_ffn_in", tn=512)
    act = _swiglu_fwd(gu, name="swiglu_fwd")
    x3 = _mm_nn(act, w["fo"], name="mm_ffn_out", out_dtype=F32, add=x2, tk=1408)

    dx3, dx3b, dg_fin, loss = _loss_bwd(x3, g_fin, target, name="loss_bwd")

    gw = {}
    gw["fo"] = _mm_tn(act, dx3b, d, name="mm_d_w_ffn_out", tm=1408)
    dact = _mm_nt(dx3b, w["fo"], name="mm_d_act", tn=1408)
    dgu = _swiglu_bwd(dact, gu, name="swiglu_bwd")
    gw["fi"] = _mm_tn(hf, dgu, n_fi, name="mm_d_w_ffn_in", tn=512)
    dhf = _mm_nt(dgu, w["fi"], name="mm_d_hf", out_dtype=F32, tc=512)
    dx2, dx2b, dg_ffn = _rms_bwd(x2, g_ffn, dhf, dx3, name="rms_ffn_bwd")

    gw["mo"] = _mm_tn(om, dx2b, d, name="mm_d_w_memo")
    dom = _mm_nt(dx2b, w["mo"], name="mm_d_om")
    dqm, dkv = _memattn_bwd(dom, qm, kv, name="memattn_bwd")
    gw["mq"] = _mm_tn(hq, dqm, d, name="mm_d_w_memq")
    dhq = _mm_nt(dqm, w["mq"], name="mm_d_hq", out_dtype=F32)
    dx1, dx1b, dg_memq = _rms_bwd(x1, g_memq, dhq, dx2, name="rms_memq_bwd")
    gw["kv"] = _mm_tn(mn, dkv, n_kv, name="mm_d_w_memkv")
    dmn = _mm_nt(dkv, w["kv"], name="mm_d_mn", out_dtype=F32)
    _, _, dg_memkv = _rms_bwd(mem, g_memkv, dmn, None, name="rms_memkv_bwd")

    gw["mix"] = _mm_tn(merged, dx1b, d, name="mm_d_w_mix")
    dmerged = _mm_nt(dx1b, w["mix"], name="mm_d_merged")
    dbr_a, dbr_b, dgab = _gates_bwd(dmerged, br_a, br_b, proj, name="gates_bwd")
    gw["a"] = _mm_tn(o_a, dbr_a, n_br, name="mm_d_w_branch_a")
    do_a = _mm_nt(dbr_a, w["a"], name="mm_d_o_a")
    gw["b"] = _mm_tn(y_b, dbr_b, n_br, name="mm_d_w_branch_b")
    dy_b = _mm_nt(dbr_b, w["b"], name="mm_d_y_b")
    dconv, dconv_w = _conv_bwd(dy_b, proj, conv_w, name="conv_bwd")
    dqkv = _sb_bwd(proj, do_a, ctot, first, name="sb_bwd")
    dproj = jnp.concatenate([dqkv, dconv, dgab], axis=1)
    gw["in"] = _mm_tn(h0, dproj, n_in, name="mm_d_w_in")
    dh0 = _mm_nt(dproj, w["in"], name="mm_d_h0", out_dtype=F32)
    dx0, _, dg_mix = _rms_bwd(x, g_mix, dh0, dx1, name="rms_mix_bwd")

    small = (dg_mix, dg_memq, dg_memkv, dg_ffn, dg_fin, dconv_w, loss)
    return dx0, gw, small


def _place():
    x, y, c = lax.axis_index("x"), lax.axis_index("y"), lax.axis_index("c")
    return x, y, c


def _all_gather(shards, *, name):
    n = len(shards)

    def body(*refs):
        ins, outs = refs[:n], refs[n:2 * n]
        send_sems, recv_sems, local_sems = refs[2 * n:]
        x, y, c = _place()
        me, sibling = (x, y, c), (x, y, 1 - c)
        chips = [(1 - x, y), (x, 1 - y), (1 - x, 1 - y)]

        def slot(px, py, pc):
            return 4 * px + 2 * py + pc

        def copy(a, k, block, to, src=None):
            dst = outs[a].at[slot(*block)]
            return pltpu.make_async_remote_copy(
                src_ref=dst if src is None else src, dst_ref=dst, send_sem=send_sems.at[a * 7 + k],
                recv_sem=recv_sems.at[a * 7 + k], device_id=to, device_id_type=MESH)

        mine, first, passed = [], [], []
        for a in range(n):
            cp = pltpu.make_async_copy(ins[a], outs[a].at[slot(*me)], local_sems.at[a])
            cp.start()
            mine.append(cp)
            first.append(copy(a, 0, me, sibling, src=ins[a]))
            first += [copy(a, 1 + j, me, (*chip, c), src=ins[a]) for j, chip in enumerate(chips)]
        for cp in first:
            cp.start()
        for a in range(n):
            for j, chip in enumerate(chips):
                copy(a, 1 + j, (*chip, c), me).wait_recv()
                fwd = copy(a, 4 + j, (*chip, c), sibling)
                fwd.start()
                passed.append(fwd)
        for a in range(n):
            copy(a, 0, sibling, me).wait_recv()
            for j, chip in enumerate(chips):
                copy(a, 4 + j, (*chip, 1 - c), me).wait_recv()
        for cp in first + passed:
            cp.wait_send()
        for cp in mine:
            cp.wait()

    return pl.pallas_call(
        body, name=name, in_specs=[ANY] * n, out_specs=[ANY] * n,
        out_shape=[jax.ShapeDtypeStruct((N_DEV,) + s.shape, s.dtype) for s in shards],
        scratch_shapes=[pltpu.SemaphoreType.DMA((7 * n,)), pltpu.SemaphoreType.DMA((7 * n,)),
                        pltpu.SemaphoreType.DMA((n,))],
        )(*shards)


def _exchange_sibling(parts, *, name):
    n = len(parts)

    def body(*refs):
        ins, outs = refs[:n], refs[n:2 * n]
        send_sems, recv_sems = refs[2 * n:]
        x, y, c = _place()
        sibling = (x, y, 1 - c)
        copies = []
        for a in range(n):
            for q in range(N_CHIP):
                copies.append(pltpu.make_async_remote_copy(
                    src_ref=ins[a].at[2 * q + 1 - c], dst_ref=outs[a].at[q], send_sem=send_sems.at[a * N_CHIP + q],
                    recv_sem=recv_sems.at[a * N_CHIP + q], device_id=sibling, device_id_type=MESH))
        for cp in copies:
            cp.start()
        for cp in copies:
            cp.wait_recv()
        for cp in copies:
            cp.wait_send()

    return pl.pallas_call(
        body, name=name, in_specs=[ANY] * n, out_specs=[ANY] * n,
        out_shape=[jax.ShapeDtypeStruct((N_CHIP,) + p.shape[1:], p.dtype) for p in parts],
        scratch_shapes=[pltpu.SemaphoreType.DMA((N_CHIP * n,)), pltpu.SemaphoreType.DMA((N_CHIP * n,))],
        )(*parts)


def _exchange_chips(parts, *, name):
    n = len(parts)

    def body(*refs):
        ins, outs = refs[:n], refs[n:2 * n]
        send_sems, recv_sems = refs[2 * n:]
        x, y, c = _place()
        chips = [(1 - x, y), (x, 1 - y), (1 - x, 1 - y)]
        copies = []
        for a in range(n):
            for j, (px, py) in enumerate(chips):
                copies.append(pltpu.make_async_remote_copy(
                    src_ref=ins[a].at[2 * px + py], dst_ref=outs[a].at[j], send_sem=send_sems.at[a * 3 + j],
                    recv_sem=recv_sems.at[a * 3 + j], device_id=(px, py, c), device_id_type=MESH))
        for cp in copies:
            cp.start()
        for cp in copies:
            cp.wait_recv()
        for cp in copies:
            cp.wait_send()

    return pl.pallas_call(
        body, name=name, in_specs=[ANY] * n, out_specs=[ANY] * n,
        out_shape=[jax.ShapeDtypeStruct((3,) + p.shape[1:], p.dtype) for p in parts],
        scratch_shapes=[pltpu.SemaphoreType.DMA((3 * n,)), pltpu.SemaphoreType.DMA((3 * n,))],
        )(*parts)


def _row_tile(a, target=512):
    tm = min(a, target)
    while a % tm:
        tm -= 8
    return tm


def _sum_with_sibling(part, recv, core, *, name):
    _, a, b = part.shape
    tm = _row_tile(a)

    def body(core_ref, p_ref, r_ref, o_ref):
        o_ref[...] = (p_ref[...].astype(F32) + r_ref[...].astype(F32)).astype(o_ref.dtype)

    return pl.pallas_call(
        body, name=name,
        grid_spec=pltpu.PrefetchScalarGridSpec(
            num_scalar_prefetch=1, grid=(N_CHIP, a // tm),
            in_specs=[pl.BlockSpec((None, tm, b), lambda q, i, core_ref: (2 * q + core_ref[0], i, 0)),
                      pl.BlockSpec((None, tm, b), lambda q, i, core_ref: (q, i, 0))],
            out_specs=pl.BlockSpec((None, tm, b), lambda q, i, core_ref: (q, i, 0))),
        out_shape=jax.ShapeDtypeStruct((N_CHIP, a, b), part.dtype), compiler_params=_params(2))(core, part, recv)


def _adam_math(wv, g, m, v):
    m = ADAM_B1 * m + (1.0 - ADAM_B1) * g
    v = ADAM_B2 * v + (1.0 - ADAM_B2) * (g * g)
    m_hat = m / (1.0 - ADAM_B1 ** ADAM_STEP)
    v_hat = v / (1.0 - ADAM_B2 ** ADAM_STEP)
    delta = -ADAM_LR * (m_hat / (jnp.sqrt(v_hat) + ADAM_EPS) + ADAM_WD * wv)
    return delta, m, v


def _adam_sharded(wv, m, v, own, recv, chip, *, name):
    a, b = wv.shape
    tm = _row_tile(a)

    def body(chip_ref, w_ref, m_ref, v_ref, own_ref, recv_ref, g_ref, d_ref, nm_ref, nv_ref):
        g = own_ref[...].astype(F32)
        for j in range(3):
            g = g + recv_ref[j].astype(F32)
        delta, nm, nv = _adam_math(w_ref[...], g, m_ref[...], v_ref[...])
        g_ref[...] = g
        d_ref[...] = delta
        nm_ref[...] = nm
        nv_ref[...] = nv

    tile = pl.BlockSpec((tm, b), lambda i, chip_ref: (i, 0))
    return pl.pallas_call(
        body, name=name,
        grid_spec=pltpu.PrefetchScalarGridSpec(
            num_scalar_prefetch=1, grid=(a // tm,),
            in_specs=[tile, tile, tile,
                      pl.BlockSpec((None, tm, b), lambda i, chip_ref: (chip_ref[0], i, 0)),
                      pl.BlockSpec((3, tm, b), lambda i, chip_ref: (0, i, 0))],
            out_specs=[tile] * 4),
        out_shape=[jax.ShapeDtypeStruct((a, b), F32)] * 4, compiler_params=_params(1))(chip, wv, m, v, own, recv)


def _sum_devices(gathered, *, name):
    _, r, c = gathered.shape

    def body(g_ref, o_ref):
        total = g_ref[0]
        for j in range(1, N_DEV):
            total = total + g_ref[j]
        o_ref[...] = total

    return pl.pallas_call(body, name=name, out_shape=jax.ShapeDtypeStruct((r, c), F32))(gathered)


def _adam_small(wv, g, m, v, *, name):
    def body(w_ref, g_ref, m_ref, v_ref, d_ref, nm_ref, nv_ref):
        delta, nm, nv = _adam_math(w_ref[...], g_ref[...], m_ref[...], v_ref[...])
        d_ref[...] = delta
        nm_ref[...] = nm
        nv_ref[...] = nv

    return pl.pallas_call(body, name=name, out_shape=[jax.ShapeDtypeStruct(wv.shape, F32)] * 3)(wv, g, m, v)


BIG = ("in", "a", "b", "mix", "mq", "kv", "mo", "fi", "fo")
SMALL_ROWS = 16


def kernel(x, mem, norm_mix, w_in, conv_w, w_branch_a, w_branch_b, w_mix_out, norm_mem_q, norm_mem_kv, w_mem_q, w_mem_kv, w_mem_o, norm_ffn, w_ffn_in, w_ffn_out, norm_final, loss_target, m_norm_mix, m_w_in, m_conv_w, m_w_branch_a, m_w_branch_b, m_w_mix_out, m_norm_mem_q, m_norm_mem_kv, m_w_mem_q, m_w_mem_kv, m_w_mem_o, m_norm_ffn, m_w_ffn_in, m_w_ffn_out, m_norm_final, v_norm_mix, v_w_in, v_conv_w, v_w_branch_a, v_w_branch_b, v_w_mix_out, v_norm_mem_q, v_norm_mem_kv, v_w_mem_q, v_w_mem_kv, v_w_mem_o, v_norm_ffn, v_w_ffn_in, v_w_ffn_out, v_norm_final):
    d = x.shape[-1]
    xi, yi, ci = lax.axis_index("x"), lax.axis_index("y"), lax.axis_index("c")
    core = jnp.reshape(ci, (1,)).astype(jnp.int32)
    chip = jnp.reshape(2 * xi + yi, (1,)).astype(jnp.int32)
    dev = 4 * xi + 2 * yi + ci

    big_w = dict(zip(BIG, (w_in, w_branch_a, w_branch_b, w_mix_out, w_mem_q, w_mem_kv, w_mem_o, w_ffn_in, w_ffn_out)))
    big_m = dict(zip(BIG, (m_w_in, m_w_branch_a, m_w_branch_b, m_w_mix_out, m_w_mem_q, m_w_mem_kv, m_w_mem_o, m_w_ffn_in, m_w_ffn_out)))
    big_v = dict(zip(BIG, (v_w_in, v_w_branch_a, v_w_branch_b, v_w_mix_out, v_w_mem_q, v_w_mem_kv, v_w_mem_o, v_w_ffn_in, v_w_ffn_out)))
    row_sharded = ("mix", "mq", "mo", "fo")

    conv_pad = jnp.zeros((8, LANES), F32).at[:3, :conv_w.shape[-1]].set(conv_w[0])
    shards = [big_w[k][0].astype(BF16) for k in BIG] + [conv_pad]
    gathered = _all_gather(shards, name="gather_weights")
    w = dict(zip(BIG, gathered[:-1]))
    for k in row_sharded:
        w[k] = w[k].reshape(1, -1, w[k].shape[-1])
    n_fi = w["fi"].shape[-1]
    w["fi"] = jnp.transpose(w["fi"], (1, 0, 2)).reshape(1, d, N_DEV * n_fi)
    conv_full = jnp.transpose(gathered[-1][:, :3, :conv_w.shape[-1]], (1, 0, 2)).reshape(3, CONV_WIDTH)

    gains = (norm_mix, norm_mem_q, norm_mem_kv, norm_ffn, norm_final.reshape(1, d))
    dx0, gw, small = _local_step(x[0], mem[0], loss_target[0], gains, conv_full, w)
    for k in row_sharded:
        gw[k] = gw[k].reshape(N_DEV, -1, gw[k].shape[-1])
    gw["fi"] = jnp.transpose(gw["fi"].reshape(d, N_DEV, n_fi), (1, 0, 2))

    parts = [gw[k] for k in BIG]
    from_sibling = _exchange_sibling(parts, name="grads_to_sibling")
    chip_sums = [_sum_with_sibling(p, r, core, name="sum_with_sibling_" + k) for k, p, r in zip(BIG, parts, from_sibling)]
    from_chips = _exchange_chips(chip_sums, name="grads_to_chips")

    grads, deltas, new_m, new_v = {}, {}, {}, {}
    for k, own, recv in zip(BIG, chip_sums, from_chips):
        lead = big_w[k].shape
        g, dl, nm, nv = _adam_sharded(big_w[k][0], big_m[k][0], big_v[k][0], own, recv, chip, name="adam_" + k)
        grads[k], deltas[k], new_m[k], new_v[k] = (t.reshape(lead) for t in (g, dl, nm, nv))

    dg_mix, dg_memq, dg_memkv, dg_ffn, dg_fin, dconv_w, loss = small
    conv_rows = jnp.zeros((3, d), F32).at[:, :CONV_WIDTH].set(dconv_w[:3])
    block = jnp.concatenate([dg_mix[:1], dg_memq[:1], dg_memkv[:1], dg_ffn[:1], dg_fin[:1], conv_rows,
                             jnp.broadcast_to(loss[:1, :1], (1, d)), jnp.zeros((SMALL_ROWS - 9, d), F32)], axis=0)
    total = _sum_devices(_all_gather([block], name="gather_small")[0], name="sum_small")
    n_conv = conv_w.shape[-1]
    g_conv = lax.dynamic_slice(total[5:8, :CONV_WIDTH], (0, dev * n_conv), (3, n_conv))
    small_w = [norm_mix, norm_mem_q, norm_mem_kv, norm_ffn, norm_final.reshape(1, d), conv_w[0]]
    small_m = [m_norm_mix, m_norm_mem_q, m_norm_mem_kv, m_norm_ffn, m_norm_final.reshape(1, d), m_conv_w[0]]
    small_v = [v_norm_mix, v_norm_mem_q, v_norm_mem_kv, v_norm_ffn, v_norm_final.reshape(1, d), v_conv_w[0]]
    small_g = [total[0:1], total[1:2], total[2:3], total[3:4], total[4:5], g_conv]
    small_names = ["norm_mix", "norm_mem_q", "norm_mem_kv", "norm_ffn", "norm_final", "conv_w"]
    sg, sd, sm, sv = {}, {}, {}, {}
    for nme, wv, g, m, v in zip(small_names, small_w, small_g, small_m, small_v):
        dl, nm, nv = _adam_small(wv, g, m, v, name="adam_" + nme)
        shape = norm_final.shape if nme == "norm_final" else (conv_w.shape if nme == "conv_w" else wv.shape)
        sg[nme], sd[nme], sm[nme], sv[nme] = (t.reshape(shape) for t in (g, dl, nm, nv))

    def ordered(big, sml):
        return (sml["norm_mix"], big["in"], sml["conv_w"], big["a"], big["b"], big["mix"], sml["norm_mem_q"],
                sml["norm_mem_kv"], big["mq"], big["kv"], big["mo"], sml["norm_ffn"], big["fi"], big["fo"],
                sml["norm_final"])

    loss_out = total[8, 0]
    grad_x = dx0.reshape(x.shape)
    return (loss_out, grad_x, *ordered(grads, sg), *ordered(deltas, sd), *ordered(new_m, sm), *ordered(new_v, sv))
```

```python
import functools
import math

import jax
import jax.numpy as jnp
from jax import lax
from jax.experimental import pallas as pl
from jax.experimental.pallas import tpu as pltpu

F32 = jnp.float32
BF16 = jnp.bfloat16
MESH = pl.DeviceIdType.MESH

N_DEV = 8
N_CHIP = 4
NORM_EPS = 1e-6
SB_HEADS = 8
SB_HEAD_DIM = 64
SB_WIDTH = SB_HEADS * SB_HEAD_DIM
CONV_WIDTH = 512
MEM_HEADS = 4
ADAM_LR = 0.001
ADAM_B1 = 0.9
ADAM_B2 = 0.999
ADAM_EPS = 1e-08
ADAM_WD = 0.01
ADAM_STEP = 10

LANES = 128
VMEM_LIMIT_BYTES = 52 * 1024 * 1024
SB_TILE = 256
SB_DEAD = 110.0

ANY = pl.BlockSpec(memory_space=pl.ANY)


def _params(n_grid):
    return pltpu.CompilerParams(dimension_semantics=("arbitrary",) * n_grid, vmem_limit_bytes=VMEM_LIMIT_BYTES)


def _bdot(a, b, dims):
    return lax.dot_general(a.astype(BF16), b.astype(BF16), (dims, ((), ())), preferred_element_type=F32)


NN = ((1,), (0,))
NT = ((1,), (1,))
TN = ((0,), (0,))


class _Comm:
    def __init__(self, ins, outs, n_sems, start, finish):
        self.ins, self.outs, self.n_sems, self.start, self.finish = ins, outs, n_sems, start, finish

    def sem_shapes(self):
        return [pltpu.SemaphoreType.DMA((k,)) for k in self.n_sems]


def _place():
    return lax.axis_index("x"), lax.axis_index("y"), lax.axis_index("c")


def _gather_comm(shards):
    n = len(shards)

    def copies(ins, outs, sems):
        send_sems, recv_sems, _ = sems
        x, y, c = _place()
        chips = [(1 - x, y), (x, 1 - y), (1 - x, 1 - y)]

        def copy(a, k, block, to, from_shard=False):
            dst = outs[a].at[4 * block[0] + 2 * block[1] + block[2]]
            return pltpu.make_async_remote_copy(
                src_ref=ins[a] if from_shard else dst, dst_ref=dst, send_sem=send_sems.at[a * 7 + k],
                recv_sem=recv_sems.at[a * 7 + k], device_id=to, device_id_type=MESH)

        me, sibling = (x, y, c), (x, y, 1 - c)
        own = [[copy(a, 0, me, sibling, True)] + [copy(a, 1 + j, me, (*chip, c), True) for j, chip in enumerate(chips)]
               for a in range(n)]
        landed = [[copy(a, 1 + j, (*chip, c), me) for j, chip in enumerate(chips)] for a in range(n)]
        passed = [[copy(a, 4 + j, (*chip, c), sibling) for j, chip in enumerate(chips)] for a in range(n)]
        from_sibling = [[copy(a, 0, sibling, me)] + [copy(a, 4 + j, (*chip, 1 - c), me) for j, chip in enumerate(chips)]
                        for a in range(n)]
        local = [pltpu.make_async_copy(ins[a], outs[a].at[4 * x + 2 * y + c], sems[2].at[a]) for a in range(n)]
        return own, landed, passed, from_sibling, local

    def start(ins, outs, sems):
        own, _, _, _, local = copies(ins, outs, sems)
        for a in range(n):
            local[a].start()
            for cp in own[a]:
                cp.start()

    def finish(ins, outs, sems):
        own, landed, passed, from_sibling, local = copies(ins, outs, sems)
        for a in range(n):
            for arrived, onward in zip(landed[a], passed[a]):
                arrived.wait_recv()
                onward.start()
        for a in range(n):
            for cp in from_sibling[a]:
                cp.wait_recv()
        for a in range(n):
            for cp in own[a] + passed[a]:
                cp.wait_send()
            local[a].wait()

    outs = [jax.ShapeDtypeStruct((N_DEV,) + s.shape, s.dtype) for s in shards]
    return _Comm(list(shards), outs, (7 * n, 7 * n, n), start, finish)


def _sibling_comm(parts):
    n = len(parts)

    def copies(ins, outs, sems):
        x, y, c = _place()
        return [pltpu.make_async_remote_copy(
            src_ref=ins[a].at[2 * q + 1 - c], dst_ref=outs[a].at[q], send_sem=sems[0].at[a * N_CHIP + q],
            recv_sem=sems[1].at[a * N_CHIP + q], device_id=(x, y, 1 - c), device_id_type=MESH)
            for a in range(n) for q in range(N_CHIP)]

    def start(ins, outs, sems):
        for cp in copies(ins, outs, sems):
            cp.start()

    def finish(ins, outs, sems):
        cps = copies(ins, outs, sems)
        for cp in cps:
            cp.wait_recv()
        for cp in cps:
            cp.wait_send()

    outs = [jax.ShapeDtypeStruct((N_CHIP,) + p.shape[1:], p.dtype) for p in parts]
    return _Comm(list(parts), outs, (N_CHIP * n, N_CHIP * n), start, finish)


def _chips_comm(parts):
    n = len(parts)

    def copies(ins, outs, sems):
        x, y, c = _place()
        chips = [(1 - x, y), (x, 1 - y), (1 - x, 1 - y)]
        return [pltpu.make_async_remote_copy(
            src_ref=ins[a].at[2 * px + py], dst_ref=outs[a].at[j], send_sem=sems[0].at[a * 3 + j],
            recv_sem=sems[1].at[a * 3 + j], device_id=(px, py, c), device_id_type=MESH)
            for a in range(n) for j, (px, py) in enumerate(chips)]

    def start(ins, outs, sems):
        for cp in copies(ins, outs, sems):
            cp.start()

    def finish(ins, outs, sems):
        cps = copies(ins, outs, sems)
        for cp in cps:
            cp.wait_recv()
        for cp in cps:
            cp.wait_send()

    outs = [jax.ShapeDtypeStruct((3,) + p.shape[1:], p.dtype) for p in parts]
    return _Comm(list(parts), outs, (3 * n, 3 * n), start, finish)


def _exchange(comm, *, name):
    n_ci, n_co = len(comm.ins), len(comm.outs)

    def kern(*refs):
        c_ins, c_outs, sems = refs[:n_ci], refs[n_ci:n_ci + n_co], refs[n_ci + n_co:]
        comm.start(c_ins, c_outs, sems)
        comm.finish(c_ins, c_outs, sems)

    return pl.pallas_call(kern, name=name, in_specs=[ANY] * n_ci, out_specs=[ANY] * n_co, out_shape=comm.outs,
                          scratch_shapes=comm.sem_shapes())(*comm.ins)


def _call(body, *, name, grid, in_specs, out_specs, out_shape, scratch, args, plan=None):
    comm = plan.comm(name) if plan is not None else None
    if comm is None:
        return list(pl.pallas_call(functools.partial(body), name=name, grid=grid, in_specs=in_specs,
                                   out_specs=out_specs, out_shape=out_shape, scratch_shapes=scratch,
                                   compiler_params=_params(len(grid)))(*args))
    n_in, n_out, n_scr, n_ci, n_co = len(in_specs), len(out_specs), len(scratch), len(comm.ins), len(comm.outs)

    def kern(*refs):
        ins, c_ins, refs = refs[:n_in], refs[n_in:n_in + n_ci], refs[n_in + n_ci:]
        outs, c_outs, refs = refs[:n_out], refs[n_out:n_out + n_co], refs[n_out + n_co:]
        scr, sems = refs[:n_scr], refs[n_scr:]
        ids = [pl.program_id(ax) for ax in range(len(grid))]
        first = functools.reduce(jnp.logical_and, [i == 0 for i in ids])
        last = functools.reduce(jnp.logical_and, [i == g - 1 for i, g in zip(ids, grid)])

        @pl.when(first)
        def _():
            comm.start(c_ins, c_outs, sems)
        body(*ins, *outs, *scr)

        @pl.when(last)
        def _():
            comm.finish(c_ins, c_outs, sems)

    res = pl.pallas_call(kern, name=name, grid=grid, in_specs=list(in_specs) + [ANY] * n_ci,
                         out_specs=list(out_specs) + [ANY] * n_co, out_shape=list(out_shape) + comm.outs,
                         scratch_shapes=list(scratch) + comm.sem_shapes(),
                         compiler_params=_params(len(grid)))(*args, *comm.ins)
    plan.landed(name, list(res[n_out:]))
    return list(res[:n_out])


def _mm_body(dims, n_k, has_add, *refs):
    if has_add:
        a_ref, b_ref, add_ref, o_ref, acc_ref = refs
    else:
        a_ref, b_ref, o_ref, acc_ref = refs
        add_ref = None
    k = pl.program_id(2)
    part = _bdot(a_ref[...], b_ref[...], dims)

    def finish(total):
        if add_ref is not None:
            total = total + add_ref[...]
        o_ref[...] = total.astype(o_ref.dtype)

    if n_k == 1:
        finish(part)
    else:
        @pl.when(k == 0)
        def _():
            acc_ref[...] = part

        @pl.when(jnp.logical_and(k > 0, k < n_k - 1))
        def _():
            acc_ref[...] += part

        @pl.when(k == n_k - 1)
        def _():
            finish(acc_ref[...] + part)


def _mm_nn(a, w3, *, name, out_dtype=BF16, add=None, tm=1024, tk=1024, tn=None, plan=None):
    m, kk = a.shape
    j, _, n = w3.shape
    tm, tk, tn = min(tm, m), min(tk, kk), n if tn is None else tn
    n_k, n_t = kk // tk, n // tn
    in_specs = [pl.BlockSpec((tm, tk), lambda i, jj, k: (i, k)),
                pl.BlockSpec((None, tk, tn), lambda i, jj, k: (jj // n_t, k, jj % n_t))]
    args = [a, w3]
    if add is not None:
        in_specs.append(pl.BlockSpec((tm, tn), lambda i, jj, k: (i, jj)))
        args.append(add)
    return _call(
        functools.partial(_mm_body, NN, n_k, add is not None), name=name,
        grid=(m // tm, j * n_t, n_k), in_specs=in_specs,
        out_specs=[pl.BlockSpec((tm, tn), lambda i, jj, k: (i, jj))],
        out_shape=[jax.ShapeDtypeStruct((m, j * n), out_dtype)],
        scratch=[pltpu.VMEM((tm, tn), F32)], args=args, plan=plan)[0]


def _mm_nt(dy, w3, *, name, out_dtype=BF16, tm=1024, tn=1024, tc=None, plan=None):
    m = dy.shape[0]
    j, kk, n = w3.shape
    tm, tn, tc = min(tm, m), min(tn, kk), n if tc is None else tc
    n_c = n // tc
    return _call(
        functools.partial(_mm_body, NT, j * n_c, False), name=name,
        grid=(m // tm, kk // tn, j * n_c),
        in_specs=[pl.BlockSpec((tm, tc), lambda i, q, jj: (i, jj)),
                  pl.BlockSpec((None, tn, tc), lambda i, q, jj: (jj // n_c, q, jj % n_c))],
        out_specs=[pl.BlockSpec((tm, tn), lambda i, q, jj: (i, q))],
        out_shape=[jax.ShapeDtypeStruct((m, kk), out_dtype)],
        scratch=[pltpu.VMEM((tm, tn), F32)], args=[dy, w3], plan=plan)[0]


def _mm_tn(a, dy, n, *, name, out_dtype=BF16, tm=1024, tk=1024, tn=None, plan=None):
    t, kk = a.shape
    j = dy.shape[1] // n
    tm, tk, tn = min(tm, kk), min(tk, t), n if tn is None else tn
    n_t = n // tn
    return _call(
        functools.partial(_mm_body, TN, t // tk, False), name=name,
        grid=(kk // tm, j * n_t, t // tk),
        in_specs=[pl.BlockSpec((tk, tm), lambda i, jj, k: (k, i)),
                  pl.BlockSpec((tk, tn), lambda i, jj, k: (k, jj))],
        out_specs=[pl.BlockSpec((None, tm, tn), lambda i, jj, k: (jj // n_t, i, jj % n_t))],
        out_shape=[jax.ShapeDtypeStruct((j, kk, n), out_dtype)],
        scratch=[pltpu.VMEM((tm, tn), F32)], args=[a, dy], plan=plan)[0]


def _rows(body, ins, outs, *, n_rows, tm, name, plan=None):
    tm = min(tm, n_rows)
    n_steps = n_rows // tm
    in_specs, args = [], []
    for arr, kind, width, block in ins:
        if kind == "row":
            in_specs.append(pl.BlockSpec((tm, width), functools.partial(lambda i, b: (i, b), b=block)))
        elif kind == "prev":
            in_specs.append(pl.BlockSpec((tm, width), functools.partial(lambda i, b: (jnp.maximum(i - 1, 0), b), b=block)))
        elif kind == "next":
            in_specs.append(pl.BlockSpec((tm, width), functools.partial(lambda i, b: (jnp.minimum(i + 1, n_steps - 1), b), b=block)))
        else:
            in_specs.append(pl.BlockSpec(arr.shape, functools.partial(lambda i, nd: (0,) * nd, nd=arr.ndim)))
        args.append(arr)
    out_specs, out_shape = [], []
    for shape, dtype, kind in outs:
        if kind == "row":
            out_specs.append(pl.BlockSpec((tm, shape[1]), lambda i: (i, 0)))
        else:
            out_specs.append(pl.BlockSpec(shape, functools.partial(lambda i, nd: (0,) * nd, nd=len(shape))))
        out_shape.append(jax.ShapeDtypeStruct(shape, dtype))

    def kern(*refs):
        body(pl.program_id(0), n_steps, *refs)

    return _call(kern, name=name, grid=(n_steps,), in_specs=in_specs, out_specs=out_specs, out_shape=out_shape,
                 scratch=[], args=args, plan=plan)


def _acc_rows(i, ref, value):
    @pl.when(i == 0)
    def _():
        ref[...] = jnp.zeros_like(ref)
    ref[...] += jnp.broadcast_to(value, ref.shape)


def _rms_fwd(x, g, *, name, tm=512):
    s, d = x.shape

    def body(i, n, x_ref, g_ref, h_ref):
        xv = x_ref[...]
        r = lax.rsqrt(jnp.mean(xv * xv, axis=-1, keepdims=True) + NORM_EPS)
        h_ref[...] = (xv * r * g_ref[...]).astype(BF16)

    return _rows(body, [(x, "row", d, 0), (g, "full", 0, 0)], [((s, d), BF16, "row")], n_rows=s, tm=tm, name=name)[0]


def _rms_bwd(x, g, dh, dres, *, name, tm=512, plan=None):
    s, d = x.shape

    def body(i, n, x_ref, g_ref, dh_ref, *rest):
        if dres is None:
            dx_ref, dxb_ref, dg_ref = rest
        else:
            dres_ref, dx_ref, dxb_ref, dg_ref = rest
        xv = x_ref[...]
        r = lax.rsqrt(jnp.mean(xv * xv, axis=-1, keepdims=True) + NORM_EPS)
        xhat = xv * r
        dhv = dh_ref[...].astype(F32)
        dxhat = dhv * g_ref[...]
        dx = r * (dxhat - xhat * jnp.mean(dxhat * xhat, axis=-1, keepdims=True))
        if dres is not None:
            dx = dx + dres_ref[...]
        dx_ref[...] = dx
        dxb_ref[...] = dx.astype(BF16)
        _acc_rows(i, dg_ref, jnp.sum(dhv * xhat, axis=0, keepdims=True))

    ins = [(x, "row", d, 0), (g, "full", 0, 0), (dh, "row", d, 0)]
    if dres is not None:
        ins.append((dres, "row", d, 0))
    return _rows(body, ins, [((s, d), F32, "row"), ((s, d), BF16, "row"), ((8, d), F32, "acc")],
                 n_rows=s, tm=tm, name=name, plan=plan)


def _loss_bwd(x, g, target, *, name, tm=512):
    s, d = x.shape

    def body(i, n, x_ref, g_ref, t_ref, dx_ref, dxb_ref, dg_ref, loss_ref):
        xv = x_ref[...]
        gv = g_ref[...]
        r = lax.rsqrt(jnp.mean(xv * xv, axis=-1, keepdims=True) + NORM_EPS)
        xhat = xv * r
        err = xhat * gv - t_ref[...]
        part = 0.5 * jnp.sum(jnp.mean(err * err, axis=-1, keepdims=True), axis=0, keepdims=True)
        dy = err * (1.0 / d)
        dxhat = dy * gv
        dx = r * (dxhat - xhat * jnp.mean(dxhat * xhat, axis=-1, keepdims=True))
        dx_ref[...] = dx
        dxb_ref[...] = dx.astype(BF16)
        _acc_rows(i, dg_ref, jnp.sum(dy * xhat, axis=0, keepdims=True))
        _acc_rows(i, loss_ref, part)

    return _rows(body, [(x, "row", d, 0), (g, "full", 0, 0), (target, "row", d, 0)],
                 [((s, d), F32, "row"), ((s, d), BF16, "row"), ((8, d), F32, "acc"), ((8, LANES), F32, "acc")],
                 n_rows=s, tm=tm, name=name)


def _sigmoid(v):
    return 1.0 / (1.0 + jnp.exp(-v))


def _swiglu_fwd(gu, *, name, tm=512):
    s, two_f = gu.shape
    f = two_f // 2

    def body(i, n, gu_ref, act_ref):
        gate = gu_ref[:, :f].astype(F32)
        up = gu_ref[:, f:].astype(F32)
        act_ref[...] = (gate * _sigmoid(gate) * up).astype(BF16)

    return _rows(body, [(gu, "row", two_f, 0)], [((s, f), BF16, "row")], n_rows=s, tm=tm, name=name)[0]


def _swiglu_bwd(dact, gu, *, name, tm=512):
    s, two_f = gu.shape
    f = two_f // 2

    def body(i, n, dact_ref, gu_ref, dgu_ref):
        gate = gu_ref[:, :f].astype(F32)
        up = gu_ref[:, f:].astype(F32)
        da = dact_ref[...].astype(F32)
        sg = _sigmoid(gate)
        silu = gate * sg
        dgu_ref[:, :f] = (da * up * (sg + silu * (1.0 - sg))).astype(BF16)
        dgu_ref[:, f:] = (da * silu).astype(BF16)

    return _rows(body, [(dact, "row", f, 0), (gu, "row", two_f, 0)], [((s, two_f), BF16, "row")],
                 n_rows=s, tm=tm, name=name)[0]


def _gates_fwd(br_a, br_b, proj, *, name, tm=512):
    s, d = br_a.shape

    def body(i, n, a_ref, b_ref, ga_ref, gb_ref, o_ref):
        o_ref[...] = (_sigmoid(ga_ref[...].astype(F32)) * a_ref[...].astype(F32)
                      + _sigmoid(gb_ref[...].astype(F32)) * b_ref[...].astype(F32)).astype(BF16)

    return _rows(body, [(br_a, "row", d, 0), (br_b, "row", d, 0), (proj, "row", d, 3), (proj, "row", d, 4)],
                 [((s, d), BF16, "row")], n_rows=s, tm=tm, name=name)[0]


def _gates_bwd(dmerged, br_a, br_b, proj, *, name, tm=512):
    s, d = br_a.shape

    def body(i, n, dm_ref, a_ref, b_ref, ga_ref, gb_ref, da_ref, db_ref, dg_ref):
        dm = dm_ref[...].astype(F32)
        sa = _sigmoid(ga_ref[...].astype(F32))
        sb = _sigmoid(gb_ref[...].astype(F32))
        da_ref[...] = (dm * sa).astype(BF16)
        db_ref[...] = (dm * sb).astype(BF16)
        dg_ref[:, :d] = (dm * a_ref[...].astype(F32) * sa * (1.0 - sa)).astype(BF16)
        dg_ref[:, d:] = (dm * b_ref[...].astype(F32) * sb * (1.0 - sb)).astype(BF16)

    return _rows(body, [(dmerged, "row", d, 0), (br_a, "row", d, 0), (br_b, "row", d, 0),
                        (proj, "row", d, 3), (proj, "row", d, 4)],
                 [((s, d), BF16, "row"), ((s, d), BF16, "row"), ((s, 2 * d), BF16, "row")],
                 n_rows=s, tm=tm, name=name)


def _shift_down(cur, prev, k, first):
    row = lax.broadcasted_iota(jnp.int32, cur.shape, 0)
    out = jnp.where(row >= k, pltpu.roll(cur, k, 0), pltpu.roll(prev, k, 0))
    return jnp.where(jnp.logical_and(first, row < k), 0.0, out)


def _shift_up(cur, nxt, k, last):
    tm = cur.shape[0]
    row = lax.broadcasted_iota(jnp.int32, cur.shape, 0)
    out = jnp.where(row < tm - k, pltpu.roll(cur, tm - k, 0), pltpu.roll(nxt, tm - k, 0))
    return jnp.where(jnp.logical_and(last, row >= tm - k), 0.0, out)


def _conv_fwd(proj, conv_w, *, name, tm=512):
    s = proj.shape[0]
    c = CONV_WIDTH

    def body(i, n, u_ref, gb_ref, gc_ref, up_ref, gcp_ref, w_ref, y_ref):
        cu = gc_ref[...].astype(F32) * u_ref[...].astype(F32)
        cup = gcp_ref[...].astype(F32) * up_ref[...].astype(F32)
        first = i == 0
        y = (w_ref[0:1, :] * _shift_down(cu, cup, 2, first) + w_ref[1:2, :] * _shift_down(cu, cup, 1, first)
             + w_ref[2:3, :] * cu)
        y_ref[...] = (gb_ref[...].astype(F32) * y).astype(BF16)

    return _rows(body, [(proj, "row", c, 3), (proj, "row", c, 4), (proj, "row", c, 5),
                        (proj, "prev", c, 3), (proj, "prev", c, 5), (conv_w, "full", 0, 0)],
                 [((s, c), BF16, "row")], n_rows=s, tm=tm, name=name)[0]


def _conv_bwd(dy_b, proj, conv_w, *, name, tm=512, plan=None):
    s = proj.shape[0]
    c = CONV_WIDTH

    def body(i, n, dy_ref, u_ref, gb_ref, gc_ref, up_ref, gcp_ref, dyn_ref, gbn_ref, w_ref, d_ref, dw_ref):
        first, last = i == 0, i == n - 1
        u = u_ref[...].astype(F32)
        gb = gb_ref[...].astype(F32)
        gc = gc_ref[...].astype(F32)
        cu = gc * u
        cup = gcp_ref[...].astype(F32) * up_ref[...].astype(F32)
        cu1 = _shift_down(cu, cup, 1, first)
        cu2 = _shift_down(cu, cup, 2, first)
        conv = w_ref[0:1, :] * cu2 + w_ref[1:2, :] * cu1 + w_ref[2:3, :] * cu
        dy = dy_ref[...].astype(F32)
        dyc = dy * gb
        dycn = dyn_ref[...].astype(F32) * gbn_ref[...].astype(F32)
        dcu = (w_ref[2:3, :] * dyc + w_ref[1:2, :] * _shift_up(dyc, dycn, 1, last)
               + w_ref[0:1, :] * _shift_up(dyc, dycn, 2, last))
        d_ref[:, 0:c] = (dcu * gc).astype(BF16)
        d_ref[:, c:2 * c] = (dy * conv).astype(BF16)
        d_ref[:, 2 * c:3 * c] = (dcu * u).astype(BF16)
        row = lax.broadcasted_iota(jnp.int32, (8, c), 0)
        dw = (jnp.where(row == 0, jnp.sum(dyc * cu2, axis=0, keepdims=True), 0.0)
              + jnp.where(row == 1, jnp.sum(dyc * cu1, axis=0, keepdims=True), 0.0)
              + jnp.where(row == 2, jnp.sum(dyc * cu, axis=0, keepdims=True), 0.0))

        @pl.when(first)
        def _():
            dw_ref[...] = jnp.zeros_like(dw_ref)
        dw_ref[...] += dw

    return _rows(body, [(dy_b, "row", c, 0), (proj, "row", c, 3), (proj, "row", c, 4), (proj, "row", c, 5),
                        (proj, "prev", c, 3), (proj, "prev", c, 5), (dy_b, "next", c, 0), (proj, "next", c, 4),
                        (conv_w, "full", 0, 0)],
                 [((s, 3 * c), BF16, "row"), ((8, c), F32, "acc")], n_rows=s, tm=tm, name=name, plan=plan)


def _mem_probs(q, k, scale):
    sc = _bdot(q, k, NT) * scale
    sc = sc - jnp.max(sc, axis=-1, keepdims=True)
    p = jnp.exp(sc)
    return p / jnp.sum(p, axis=-1, keepdims=True)


def _memattn_fwd(qm, kv, *, name, tm=512):
    s, d = qm.shape
    hd = d // MEM_HEADS
    scale = 1.0 / math.sqrt(hd)

    def body(i, n, q_ref, kv_ref, o_ref):
        for h in range(MEM_HEADS):
            cols = slice(h * hd, (h + 1) * hd)
            p = _mem_probs(q_ref[:, cols], kv_ref[:, cols], scale)
            o_ref[:, cols] = _bdot(p, kv_ref[:, d + h * hd:d + (h + 1) * hd], NN).astype(BF16)

    return _rows(body, [(qm, "row", d, 0), (kv, "full", 0, 0)], [((s, d), BF16, "row")], n_rows=s, tm=tm, name=name)[0]


def _memattn_bwd(dom, qm, kv, *, name, tm=512):
    s, d = qm.shape
    hd = d // MEM_HEADS
    scale = 1.0 / math.sqrt(hd)

    def body(i, n, do_ref, q_ref, kv_ref, dq_ref, dkv_ref):
        @pl.when(i == 0)
        def _():
            dkv_ref[...] = jnp.zeros_like(dkv_ref)
        for h in range(MEM_HEADS):
            cols = slice(h * hd, (h + 1) * hd)
            vcols = slice(d + h * hd, d + (h + 1) * hd)
            q, k, v, do = q_ref[:, cols], kv_ref[:, cols], kv_ref[:, vcols], do_ref[:, cols]
            p = _mem_probs(q, k, scale)
            dp = _bdot(do, v, NT)
            ds = p * (dp - jnp.sum(dp * p, axis=-1, keepdims=True)) * scale
            dq_ref[:, cols] = _bdot(ds, k, NN).astype(BF16)
            dkv_ref[:, cols] += _bdot(ds, q, TN)
            dkv_ref[:, vcols] += _bdot(p, do, TN)

    return _rows(body, [(dom, "row", d, 0), (qm, "row", d, 0), (kv, "full", 0, 0)],
                 [((s, d), BF16, "row"), (kv.shape, F32, "acc")], n_rows=s, tm=tm, name=name)


def _sb_consts(t):
    row = lax.broadcasted_iota(jnp.int32, (t, t), 0)
    col = lax.broadcasted_iota(jnp.int32, (t, t), 1)
    lane = lax.broadcasted_iota(jnp.int32, (t, LANES), 1)
    return row, col, lane < SB_HEAD_DIM


def _log_fail(z):
    return jnp.minimum(-z, 0.0) - jnp.log(1.0 + jnp.exp(-jnp.abs(z)))


def _tri_sum(v, tri):
    hi = v.astype(BF16)
    lo = (v - hi.astype(F32)).astype(BF16)
    return _bdot(hi, tri, NN) + _bdot(lo, tri, NN)


def _sb_fwd(proj, *, name, plan=None):
    s = proj.shape[0]
    t = SB_TILE
    n_q = s // t
    scale = 1.0 / math.sqrt(SB_HEAD_DIM)
    k_blk, v_blk = SB_WIDTH // LANES, 2 * SB_WIDTH // LANES

    def body(q_ref, k_ref, v_ref, o_ref, c_ref, first_ref, acc_ref):
        i = pl.program_id(1)
        row, col, head0 = _sb_consts(t)
        later = (row > col).astype(BF16)
        valid = col < row
        qs = q_ref[...] * scale
        q2 = (jnp.where(head0, qs, 0), jnp.where(head0, 0, qs))

        def tile(kb, carry, diag):
            kt = k_ref[pl.ds(pl.multiple_of(kb * t, t), t), :]
            vt = v_ref[pl.ds(pl.multiple_of(kb * t, t), t), :]
            new = []
            for h in range(2):
                z = _bdot(q2[h], kt, NT)
                lf = _log_fail(z)
                if diag:
                    lf = jnp.where(valid, lf, 0.0)
                cum = _tri_sum(lf, later)
                w = jnp.exp(z + lf + cum + carry[h])
                if diag:
                    w = jnp.where(valid, w, 0.0)
                acc_ref[h] += _bdot(w, vt, NN)
                new.append(carry[h] + cum[:, 0:1] + lf[:, 0:1])
            return tuple(new)

        acc_ref[...] = jnp.zeros_like(acc_ref)
        zero = jnp.zeros((t, 1), F32)

        def alive(carry):
            return (jnp.maximum(jnp.max(carry[0]), jnp.max(carry[1])) > -SB_DEAD).astype(jnp.int32)

        def step(state):
            kb, _, c0, c1 = state
            new = tile(kb, (c0, c1), False)
            return kb - 1, alive(new), new[0], new[1]

        carry = tile(i, (zero, zero), True)
        kb, _, c0, c1 = lax.while_loop(lambda st: jnp.logical_and(st[0] >= 0, st[1] > 0), step,
                                       (i - 1, alive(carry), carry[0], carry[1]))
        o_ref[...] = jnp.where(head0, acc_ref[0], acc_ref[1]).astype(BF16)
        c_ref[...] = jnp.where(lax.broadcasted_iota(jnp.int32, (t, 2), 1) == 0, c0, c1)
        first_ref[pl.program_id(0), i] = (kb + 1).astype(F32)

    return _call(
        body, name=name, grid=(SB_HEADS // 2, n_q),
        in_specs=[pl.BlockSpec((t, LANES), lambda p, i: (i, p)),
                  pl.BlockSpec((s, LANES), lambda p, i: (0, k_blk + p)),
                  pl.BlockSpec((s, LANES), lambda p, i: (0, v_blk + p))],
        out_specs=[pl.BlockSpec((t, LANES), lambda p, i: (i, p)),
                   pl.BlockSpec((None, t, 2), lambda p, i: (p, i, 0)),
                   pl.BlockSpec(memory_space=pltpu.SMEM)],
        out_shape=[jax.ShapeDtypeStruct((s, SB_WIDTH), BF16), jax.ShapeDtypeStruct((SB_HEADS // 2, s, 2), F32),
                   jax.ShapeDtypeStruct((SB_HEADS // 2, n_q), F32)],
        scratch=[pltpu.VMEM((2, t, LANES), F32)], args=[proj, proj, proj], plan=plan)


def _sb_bwd(proj, do_a, ctot, first, *, name, plan=None):
    s = proj.shape[0]
    t = SB_TILE
    n_q = s // t
    scale = 1.0 / math.sqrt(SB_HEAD_DIM)
    k_blk, v_blk = SB_WIDTH // LANES, 2 * SB_WIDTH // LANES

    def body(q_ref, k_ref, v_ref, do_ref, c_ref, first_ref, dq_ref, dk_ref, dv_ref, dq_acc, dk_acc, dv_acc):
        i = pl.program_id(1)
        kb0 = jnp.clip(first_ref[pl.program_id(0), i].astype(jnp.int32), 0, i)
        row, col, head0 = _sb_consts(t)
        upto = (row <= col).astype(BF16)
        before = (row < col).astype(BF16)
        valid = col < row
        qs = q_ref[...] * scale
        q2 = (jnp.where(head0, qs, 0), jnp.where(head0, 0, qs))
        do = do_ref[...]
        do2 = (jnp.where(head0, do, 0), jnp.where(head0, 0, do))
        ctot2 = (c_ref[:, 0:1], c_ref[:, 1:2])

        @pl.when(i == 0)
        def _():
            dk_acc[...] = jnp.zeros_like(dk_acc)
            dv_acc[...] = jnp.zeros_like(dv_acc)
        dq_acc[...] = jnp.zeros_like(dq_acc)

        def tile(kb, carry, diag):
            rows = pl.ds(pl.multiple_of(kb * t, t), t)
            kt = k_ref[rows, :]
            vt = v_ref[rows, :]
            new = []
            for h in range(2):
                lf_before, g_before = carry[2 * h], carry[2 * h + 1]
                z = _bdot(q2[h], kt, NT)
                lf = _log_fail(z)
                if diag:
                    lf = jnp.where(valid, lf, 0.0)
                cum = _tri_sum(lf, upto)
                log_later = ctot2[h] - lf_before - cum
                beta = jnp.exp(z + lf)
                w = beta * jnp.exp(log_later)
                if diag:
                    w = jnp.where(valid, w, 0.0)
                g = w * _bdot(do2[h], vt, NT)
                g_sum = g_before + _bdot(g, before, NN)
                dz = g * jnp.exp(lf) - beta * g_sum
                if diag:
                    dz = jnp.where(valid, dz, 0.0)
                dq_acc[h] += _bdot(dz, kt, NN)
                dk_acc[rows, :] += _bdot(dz, q2[h], TN)
                dv_acc[rows, :] += _bdot(w, do2[h], TN)
                t_last = slice(t - 1, t)
                new += [lf_before + cum[:, t_last], g_sum[:, t_last] + g[:, t_last]]
            return tuple(new)

        zero = jnp.zeros((t, 1), F32)
        carry = lax.fori_loop(kb0, i, lambda n, c: tile(n, c, False), (zero,) * 4)
        tile(i, carry, True)
        dq_ref[...] = (jnp.where(head0, dq_acc[0], dq_acc[1]) * scale).astype(BF16)

        @pl.when(i == n_q - 1)
        def _():
            dk_ref[...] = dk_acc[...].astype(BF16)
            dv_ref[...] = dv_acc[...].astype(BF16)

    outs = _call(
        body, name=name, grid=(SB_HEADS // 2, n_q),
        in_specs=[pl.BlockSpec((t, LANES), lambda p, i: (i, p)),
                  pl.BlockSpec((s, LANES), lambda p, i: (0, k_blk + p)),
                  pl.BlockSpec((s, LANES), lambda p, i: (0, v_blk + p)),
                  pl.BlockSpec((t, LANES), lambda p, i: (i, p)),
                  pl.BlockSpec((None, t, 2), lambda p, i: (p, i, 0)),
                  pl.BlockSpec(memory_space=pltpu.SMEM)],
        out_specs=[pl.BlockSpec((t, LANES), lambda p, i: (i, p)),
                   pl.BlockSpec((s, LANES), lambda p, i: (0, p)),
                   pl.BlockSpec((s, LANES), lambda p, i: (0, p))],
        out_shape=[jax.ShapeDtypeStruct((s, SB_WIDTH), BF16)] * 3,
        scratch=[pltpu.VMEM((2, t, LANES), F32), pltpu.VMEM((s, LANES), F32), pltpu.VMEM((s, LANES), F32)],
        args=[proj, proj, proj, do_a, ctot, first], plan=plan)
    return jnp.concatenate(outs, axis=1)


def _local_step(x, mem, target, gains, conv_w, plan):
    g_mix, g_memq, g_memkv, g_ffn, g_fin = gains
    d = x.shape[1]

    h0 = _rms_fwd(x, g_mix, name="rms_mix")
    w_in = plan.weight("in")
    proj = _mm_nn(h0, w_in, name="mm_in", plan=plan)
    o_a, ctot, first = _sb_fwd(proj, name="sb_fwd", plan=plan)
    y_b = _conv_fwd(proj, conv_w, name="conv_fwd")
    w_a, w_b, w_mix = plan.weight("a"), plan.weight("b"), plan.weight("mix")
    br_a = _mm_nn(o_a, w_a, name="mm_branch_a")
    br_b = _mm_nn(y_b, w_b, name="mm_branch_b")
    merged = _gates_fwd(br_a, br_b, proj, name="gates_fwd")
    x1 = _mm_nn(merged, w_mix, name="mm_mix", out_dtype=F32, add=x)
    hq = _rms_fwd(x1, g_memq, name="rms_memq")
    w_mq, w_kv, w_mo = plan.weight("mq"), plan.weight("kv"), plan.weight("mo")
    qm = _mm_nn(hq, w_mq, name="mm_memq")
    mn = _rms_fwd(mem, g_memkv, name="rms_memkv")
    kv = _mm_nn(mn, w_kv, name="mm_memkv")
    om = _memattn_fwd(qm, kv, name="memattn_fwd")
    x2 = _mm_nn(om, w_mo, name="mm_memo", out_dtype=F32, add=x1)
    hf = _rms_fwd(x2, g_ffn, name="rms_ffn")
    w_fi = plan.weight("fi")
    gu = _mm_nn(hf, w_fi, name="mm_ffn_in", tn=512, plan=plan)
    act = _swiglu_fwd(gu, name="swiglu_fwd")
    w_fo = plan.weight("fo")
    x3 = _mm_nn(act, w_fo, name="mm_ffn_out", out_dtype=F32, add=x2, tk=1408)

    dx3, dx3b, dg_fin, loss = _loss_bwd(x3, g_fin, target, name="loss_bwd")

    plan.grad("fo", _mm_tn(act, dx3b, d, name="mm_d_w_ffn_out", tm=1408))
    dact = _mm_nt(dx3b, w_fo, name="mm_d_act", tn=1408)
    dgu = _swiglu_bwd(dact, gu, name="swiglu_bwd")
    plan.grad("fi", _mm_tn(hf, dgu, w_fi.shape[2], name="mm_d_w_ffn_in", tn=512))
    dhf = _mm_nt(dgu, w_fi, name="mm_d_hf", out_dtype=F32, tc=512, plan=plan)
    dx2, dx2b, dg_ffn = _rms_bwd(x2, g_ffn, dhf, dx3, name="rms_ffn_bwd")

    plan.grad("mo", _mm_tn(om, dx2b, d, name="mm_d_w_memo"))
    dom = _mm_nt(dx2b, w_mo, name="mm_d_om")
    dqm, dkv = _memattn_bwd(dom, qm, kv, name="memattn_bwd")
    plan.grad("mq", _mm_tn(hq, dqm, d, name="mm_d_w_memq"))
    dhq = _mm_nt(dqm, w_mq, name="mm_d_hq", out_dtype=F32)
    dx1, dx1b, dg_memq = _rms_bwd(x1, g_memq, dhq, dx2, name="rms_memq_bwd")
    plan.grad("kv", _mm_tn(mn, dkv, w_kv.shape[2], name="mm_d_w_memkv"))
    dmn = _mm_nt(dkv, w_kv, name="mm_d_mn", out_dtype=F32)
    _, _, dg_memkv = _rms_bwd(mem, g_memkv, dmn, None, name="rms_memkv_bwd")

    plan.grad("mix", _mm_tn(merged, dx1b, d, name="mm_d_w_mix"))
    dmerged = _mm_nt(dx1b, w_mix, name="mm_d_merged", plan=plan)
    dbr_a, dbr_b, dgab = _gates_bwd(dmerged, br_a, br_b, proj, name="gates_bwd")
    plan.grad("a", _mm_tn(o_a, dbr_a, d, name="mm_d_w_branch_a"))
    do_a = _mm_nt(dbr_a, w_a, name="mm_d_o_a")
    plan.grad("b", _mm_tn(y_b, dbr_b, d, name="mm_d_w_branch_b"))
    dy_b = _mm_nt(dbr_b, w_b, name="mm_d_y_b")
    dconv, dconv_w = _conv_bwd(dy_b, proj, conv_w, name="conv_bwd", plan=plan)
    dqkv = _sb_bwd(proj, do_a, ctot, first, name="sb_bwd", plan=plan)
    dproj = jnp.concatenate([dqkv, dconv, dgab], axis=1)
    plan.grad("in", _mm_tn(h0, dproj, w_in.shape[2], name="mm_d_w_in"))
    dh0 = _mm_nt(dproj, w_in, name="mm_d_h0", out_dtype=F32, plan=plan)
    dx0, _, dg_mix = _rms_bwd(x, g_mix, dh0, dx1, name="rms_mix_bwd", plan=plan)

    return dx0, (dg_mix, dg_memq, dg_memkv, dg_ffn, dg_fin, dconv_w, loss)


def _row_tile(a, target=512):
    tm = min(a, target)
    while a % tm:
        tm -= 8
    return tm


def _sum_with_sibling(part, recv, core, *, name):
    _, a, b = part.shape
    tm = _row_tile(a)

    def body(core_ref, p_ref, r_ref, o_ref):
        o_ref[...] = (p_ref[...].astype(F32) + r_ref[...].astype(F32)).astype(o_ref.dtype)

    return pl.pallas_call(
        body, name=name,
        grid_spec=pltpu.PrefetchScalarGridSpec(
            num_scalar_prefetch=1, grid=(N_CHIP, a // tm),
            in_specs=[pl.BlockSpec((None, tm, b), lambda q, i, core_ref: (2 * q + core_ref[0], i, 0)),
                      pl.BlockSpec((None, tm, b), lambda q, i, core_ref: (q, i, 0))],
            out_specs=pl.BlockSpec((None, tm, b), lambda q, i, core_ref: (q, i, 0))),
        out_shape=jax.ShapeDtypeStruct((N_CHIP, a, b), part.dtype), compiler_params=_params(2))(core, part, recv)


def _adam_math(wv, g, m, v):
    m = ADAM_B1 * m + (1.0 - ADAM_B1) * g
    v = ADAM_B2 * v + (1.0 - ADAM_B2) * (g * g)
    m_hat = m / (1.0 - ADAM_B1 ** ADAM_STEP)
    v_hat = v / (1.0 - ADAM_B2 ** ADAM_STEP)
    delta = -ADAM_LR * (m_hat / (jnp.sqrt(v_hat) + ADAM_EPS) + ADAM_WD * wv)
    return delta, m, v


def _adam_sharded(wv, m, v, own, recv, chip, *, name):
    a, b = wv.shape
    tm = _row_tile(a)

    def body(chip_ref, w_ref, m_ref, v_ref, own_ref, recv_ref, g_ref, d_ref, nm_ref, nv_ref):
        g = own_ref[...].astype(F32)
        for j in range(3):
            g = g + recv_ref[j].astype(F32)
        delta, nm, nv = _adam_math(w_ref[...], g, m_ref[...], v_ref[...])
        g_ref[...] = g
        d_ref[...] = delta
        nm_ref[...] = nm
        nv_ref[...] = nv

    tile = pl.BlockSpec((tm, b), lambda i, chip_ref: (i, 0))
    return pl.pallas_call(
        body, name=name,
        grid_spec=pltpu.PrefetchScalarGridSpec(
            num_scalar_prefetch=1, grid=(a // tm,),
            in_specs=[tile, tile, tile,
                      pl.BlockSpec((None, tm, b), lambda i, chip_ref: (chip_ref[0], i, 0)),
                      pl.BlockSpec((3, tm, b), lambda i, chip_ref: (0, i, 0))],
            out_specs=[tile] * 4),
        out_shape=[jax.ShapeDtypeStruct((a, b), F32)] * 4, compiler_params=_params(1))(chip, wv, m, v, own, recv)


def _sum_devices(gathered, *, name):
    _, r, c = gathered.shape

    def body(g_ref, o_ref):
        total = g_ref[0]
        for j in range(1, N_DEV):
            total = total + g_ref[j]
        o_ref[...] = total

    return pl.pallas_call(body, name=name, out_shape=jax.ShapeDtypeStruct((r, c), F32))(gathered)


def _adam_small(wv, g, m, v, *, name):
    def body(w_ref, g_ref, m_ref, v_ref, d_ref, nm_ref, nv_ref):
        delta, nm, nv = _adam_math(w_ref[...], g_ref[...], m_ref[...], v_ref[...])
        d_ref[...] = delta
        nm_ref[...] = nm
        nv_ref[...] = nv

    return pl.pallas_call(body, name=name, out_shape=[jax.ShapeDtypeStruct(wv.shape, F32)] * 3)(wv, g, m, v)


BIG = ("in", "a", "b", "mix", "mq", "kv", "mo", "fi", "fo")
ROW_SHARDED = ("mix", "mq", "mo", "fo")
UNSHARDED = ("a", "b", "fi")
SMALL_ROWS = 16


class _Plan:
    GATHER_ON = {"mm_in": ("a", "b", "mix", "kv"), "sb_fwd": ("mq", "mo", "fi"), "mm_ffn_in": ("fo",)}
    SIBLING_ON = {"mm_d_hf": ("fo", "fi"), "mm_d_merged": ("mo", "mq", "kv"), "conv_bwd": ("mix", "a", "b"),
                  "mm_d_h0": ("in",)}
    CHIPS_ON = {"sb_bwd": ("fo", "fi", "mo", "mq", "kv", "mix", "a", "b"), "rms_mix_bwd": ("in",)}

    def __init__(self, shards, core):
        self.shards, self.core = shards, core
        self.w, self.parts, self.chip_sums, self.from_chips = {}, {}, {}, {}

    def comm(self, name):
        if name in self.GATHER_ON:
            return _gather_comm([self.shards[k] for k in self.GATHER_ON[name]])
        if name in self.SIBLING_ON:
            return _sibling_comm([self.parts[k] for k in self.SIBLING_ON[name]])
        if name in self.CHIPS_ON:
            return _chips_comm([self.chip_sums[k] for k in self.CHIPS_ON[name]])
        return None

    def landed(self, name, outs):
        if name in self.GATHER_ON:
            for k, o in zip(self.GATHER_ON[name], outs):
                self.set_weight(k, o)
        elif name in self.SIBLING_ON:
            for k, o in zip(self.SIBLING_ON[name], outs):
                self.chip_sums[k] = _sum_with_sibling(self.parts[k], o, self.core, name="sum_with_sibling_" + k)
        else:
            for k, o in zip(self.CHIPS_ON[name], outs):
                self.from_chips[k] = o

    def set_weight(self, k, gathered):
        _, a, b = gathered.shape
        if k in ROW_SHARDED:
            gathered = gathered.reshape(1, N_DEV * a, b)
        elif k in UNSHARDED:
            gathered = jnp.transpose(gathered, (1, 0, 2)).reshape(1, a, N_DEV * b)
        self.w[k] = gathered

    def weight(self, k):
        return self.w[k]

    def grad(self, k, g):
        _, a, b = g.shape
        if k in ROW_SHARDED:
            g = g.reshape(N_DEV, a // N_DEV, b)
        elif k in UNSHARDED:
            g = jnp.transpose(g.reshape(a, N_DEV, b // N_DEV), (1, 0, 2))
        self.parts[k] = g


def kernel(x, mem, norm_mix, w_in, conv_w, w_branch_a, w_branch_b, w_mix_out, norm_mem_q, norm_mem_kv, w_mem_q, w_mem_kv, w_mem_o, norm_ffn, w_ffn_in, w_ffn_out, norm_final, loss_target, m_norm_mix, m_w_in, m_conv_w, m_w_branch_a, m_w_branch_b, m_w_mix_out, m_norm_mem_q, m_norm_mem_kv, m_w_mem_q, m_w_mem_kv, m_w_mem_o, m_norm_ffn, m_w_ffn_in, m_w_ffn_out, m_norm_final, v_norm_mix, v_w_in, v_conv_w, v_w_branch_a, v_w_branch_b, v_w_mix_out, v_norm_mem_q, v_norm_mem_kv, v_w_mem_q, v_w_mem_kv, v_w_mem_o, v_norm_ffn, v_w_ffn_in, v_w_ffn_out, v_norm_final):
    d = x.shape[-1]
    xi, yi, ci = lax.axis_index("x"), lax.axis_index("y"), lax.axis_index("c")
    core = jnp.reshape(ci, (1,)).astype(jnp.int32)
    chip = jnp.reshape(2 * xi + yi, (1,)).astype(jnp.int32)
    dev = 4 * xi + 2 * yi + ci

    big_w = dict(zip(BIG, (w_in, w_branch_a, w_branch_b, w_mix_out, w_mem_q, w_mem_kv, w_mem_o, w_ffn_in, w_ffn_out)))
    big_m = dict(zip(BIG, (m_w_in, m_w_branch_a, m_w_branch_b, m_w_mix_out, m_w_mem_q, m_w_mem_kv, m_w_mem_o, m_w_ffn_in, m_w_ffn_out)))
    big_v = dict(zip(BIG, (v_w_in, v_w_branch_a, v_w_branch_b, v_w_mix_out, v_w_mem_q, v_w_mem_kv, v_w_mem_o, v_w_ffn_in, v_w_ffn_out)))

    plan = _Plan({k: big_w[k][0].astype(BF16) for k in BIG}, core)
    n_conv = conv_w.shape[-1]
    conv_pad = jnp.zeros((8, LANES), F32).at[:3, :n_conv].set(conv_w[0])
    w_in_all, conv_all = _exchange(_gather_comm([plan.shards["in"], conv_pad]), name="gather_first")
    plan.set_weight("in", w_in_all)
    conv_full = jnp.transpose(conv_all[:, :3, :n_conv], (1, 0, 2)).reshape(3, CONV_WIDTH)

    gains = (norm_mix, norm_mem_q, norm_mem_kv, norm_ffn, norm_final.reshape(1, d))
    dx0, small = _local_step(x[0], mem[0], loss_target[0], gains, conv_full, plan)

    grads, deltas, new_m, new_v = {}, {}, {}, {}
    for k in BIG:
        lead = big_w[k].shape
        g, dl, nm, nv = _adam_sharded(big_w[k][0], big_m[k][0], big_v[k][0], plan.chip_sums[k], plan.from_chips[k],
                                      chip, name="adam_" + k)
        grads[k], deltas[k], new_m[k], new_v[k] = (t.reshape(lead) for t in (g, dl, nm, nv))

    dg_mix, dg_memq, dg_memkv, dg_ffn, dg_fin, dconv_w, loss = small
    conv_rows = jnp.zeros((3, d), F32).at[:, :CONV_WIDTH].set(dconv_w[:3])
    block = jnp.concatenate([dg_mix[:1], dg_memq[:1], dg_memkv[:1], dg_ffn[:1], dg_fin[:1], conv_rows,
                             jnp.broadcast_to(loss[:1, :1], (1, d)), jnp.zeros((SMALL_ROWS - 9, d), F32)], axis=0)
    total = _sum_devices(_exchange(_gather_comm([block]), name="gather_small")[0], name="sum_small")
    g_conv = lax.dynamic_slice(total[5:8, :CONV_WIDTH], (0, dev * n_conv), (3, n_conv))
    small_w = [norm_mix, norm_mem_q, norm_mem_kv, norm_ffn, norm_final.reshape(1, d), conv_w[0]]
    small_m = [m_norm_mix, m_norm_mem_q, m_norm_mem_kv, m_norm_ffn, m_norm_final.reshape(1, d), m_conv_w[0]]
    small_v = [v_norm_mix, v_norm_mem_q, v_norm_mem_kv, v_norm_ffn, v_norm_final.reshape(1, d), v_conv_w[0]]
    small_g = [total[0:1], total[1:2], total[2:3], total[3:4], total[4:5], g_conv]
    small_names = ["norm_mix", "norm_mem_q", "norm_mem_kv", "norm_ffn", "norm_final", "conv_w"]
    sg, sd, sm, sv = {}, {}, {}, {}
    for nme, wv, g, m, v in zip(small_names, small_w, small_g, small_m, small_v):
        dl, nm, nv = _adam_small(wv, g, m, v, name="adam_" + nme)
        shape = norm_final.shape if nme == "norm_final" else (conv_w.shape if nme == "conv_w" else wv.shape)
        sg[nme], sd[nme], sm[nme], sv[nme] = (t.reshape(shape) for t in (g, dl, nm, nv))

    def ordered(big, sml):
        return (sml["norm_mix"], big["in"], sml["conv_w"], big["a"], big["b"], big["mix"], sml["norm_mem_q"],
                sml["norm_mem_kv"], big["mq"], big["kv"], big["mo"], sml["norm_ffn"], big["fi"], big["fo"],
                sml["norm_final"])

    loss_out = total[8, 0]
    grad_x = dx0.reshape(x.shape)
    return (loss_out, grad_x, *ordered(grads, sg), *ordered(deltas, sd), *ordered(new_m, sm), *ordered(new_v, sv))
```

```python
import functools
import math

import jax
import jax.numpy as jnp
from jax import lax
from jax.experimental import pallas as pl
from jax.experimental.pallas import tpu as pltpu

F32 = jnp.float32
BF16 = jnp.bfloat16
MESH = pl.DeviceIdType.MESH

N_DEV = 8
N_CHIP = 4
NORM_EPS = 1e-6
SB_HEADS = 8
SB_HEAD_DIM = 64
SB_WIDTH = SB_HEADS * SB_HEAD_DIM
CONV_WIDTH = 512
MEM_HEADS = 4
ADAM_LR = 0.001
ADAM_B1 = 0.9
ADAM_B2 = 0.999
ADAM_EPS = 1e-08
ADAM_WD = 0.01
ADAM_STEP = 10

LANES = 128
VMEM_LIMIT_BYTES = 52 * 1024 * 1024
SB_TILE = 256
SB_DEAD = 110.0

ANY = pl.BlockSpec(memory_space=pl.ANY)


def _params(n_grid):
    return pltpu.CompilerParams(dimension_semantics=("arbitrary",) * n_grid, vmem_limit_bytes=VMEM_LIMIT_BYTES)


def _bdot(a, b, dims):
    return lax.dot_general(a.astype(BF16), b.astype(BF16), (dims, ((), ())), preferred_element_type=F32)


NN = ((1,), (0,))
NT = ((1,), (1,))
TN = ((0,), (0,))


class _Comm:
    def __init__(self, ins, outs, n_sems, start, finish):
        self.ins, self.outs, self.n_sems, self.start, self.finish = ins, outs, n_sems, start, finish

    def sem_shapes(self):
        return [pltpu.SemaphoreType.DMA((k,)) for k in self.n_sems]


def _place():
    return lax.axis_index("x"), lax.axis_index("y"), lax.axis_index("c")


def _gather_comm(shards):
    n = len(shards)

    def copies(ins, outs, sems):
        send_sems, recv_sems, _ = sems
        x, y, c = _place()
        chips = [(1 - x, y), (x, 1 - y), (1 - x, 1 - y)]

        def copy(a, k, block, to, from_shard=False):
            dst = outs[a].at[4 * block[0] + 2 * block[1] + block[2]]
            return pltpu.make_async_remote_copy(
                src_ref=ins[a] if from_shard else dst, dst_ref=dst, send_sem=send_sems.at[a * 7 + k],
                recv_sem=recv_sems.at[a * 7 + k], device_id=to, device_id_type=MESH)

        me, sibling = (x, y, c), (x, y, 1 - c)
        own = [[copy(a, 0, me, sibling, True)] + [copy(a, 1 + j, me, (*chip, c), True) for j, chip in enumerate(chips)]
               for a in range(n)]
        landed = [[copy(a, 1 + j, (*chip, c), me) for j, chip in enumerate(chips)] for a in range(n)]
        passed = [[copy(a, 4 + j, (*chip, c), sibling) for j, chip in enumerate(chips)] for a in range(n)]
        from_sibling = [[copy(a, 0, sibling, me)] + [copy(a, 4 + j, (*chip, 1 - c), me) for j, chip in enumerate(chips)]
                        for a in range(n)]
        local = [pltpu.make_async_copy(ins[a], outs[a].at[4 * x + 2 * y + c], sems[2].at[a]) for a in range(n)]
        return own, landed, passed, from_sibling, local

    def start(ins, outs, sems):
        own, _, _, _, local = copies(ins, outs, sems)
        for a in range(n):
            local[a].start()
            for cp in own[a]:
                cp.start()

    def finish(ins, outs, sems):
        own, landed, passed, from_sibling, local = copies(ins, outs, sems)
        for a in range(n):
            for arrived, onward in zip(landed[a], passed[a]):
                arrived.wait_recv()
                onward.start()
        for a in range(n):
            for cp in from_sibling[a]:
                cp.wait_recv()
        for a in range(n):
            for cp in own[a] + passed[a]:
                cp.wait_send()
            local[a].wait()

    outs = [jax.ShapeDtypeStruct((N_DEV,) + s.shape, s.dtype) for s in shards]
    return _Comm(list(shards), outs, (7 * n, 7 * n, n), start, finish)


def _sibling_comm(parts):
    n = len(parts)

    def copies(ins, outs, sems):
        x, y, c = _place()
        return [pltpu.make_async_remote_copy(
            src_ref=ins[a].at[2 * q + 1 - c], dst_ref=outs[a].at[q], send_sem=sems[0].at[a * N_CHIP + q],
            recv_sem=sems[1].at[a * N_CHIP + q], device_id=(x, y, 1 - c), device_id_type=MESH)
            for a in range(n) for q in range(N_CHIP)]

    def start(ins, outs, sems):
        for cp in copies(ins, outs, sems):
            cp.start()

    def finish(ins, outs, sems):
        cps = copies(ins, outs, sems)
        for cp in cps:
            cp.wait_recv()
        for cp in cps:
            cp.wait_send()

    outs = [jax.ShapeDtypeStruct((N_CHIP,) + p.shape[1:], p.dtype) for p in parts]
    return _Comm(list(parts), outs, (N_CHIP * n, N_CHIP * n), start, finish)


def _chips_comm(parts):
    n = len(parts)

    def copies(ins, outs, sems):
        x, y, c = _place()
        chips = [(1 - x, y), (x, 1 - y), (1 - x, 1 - y)]
        return [pltpu.make_async_remote_copy(
            src_ref=ins[a].at[2 * px + py], dst_ref=outs[a].at[j], send_sem=sems[0].at[a * 3 + j],
            recv_sem=sems[1].at[a * 3 + j], device_id=(px, py, c), device_id_type=MESH)
            for a in range(n) for j, (px, py) in enumerate(chips)]

    def start(ins, outs, sems):
        for cp in copies(ins, outs, sems):
            cp.start()

    def finish(ins, outs, sems):
        cps = copies(ins, outs, sems)
        for cp in cps:
            cp.wait_recv()
        for cp in cps:
            cp.wait_send()

    outs = [jax.ShapeDtypeStruct((3,) + p.shape[1:], p.dtype) for p in parts]
    return _Comm(list(parts), outs, (3 * n, 3 * n), start, finish)


def _exchange(comm, *, name):
    n_ci, n_co = len(comm.ins), len(comm.outs)

    def kern(*refs):
        c_ins, c_outs, sems = refs[:n_ci], refs[n_ci:n_ci + n_co], refs[n_ci + n_co:]
        comm.start(c_ins, c_outs, sems)
        comm.finish(c_ins, c_outs, sems)

    return pl.pallas_call(kern, name=name, in_specs=[ANY] * n_ci, out_specs=[ANY] * n_co, out_shape=comm.outs,
                          scratch_shapes=comm.sem_shapes())(*comm.ins)


def _call(body, *, name, grid, in_specs, out_specs, out_shape, scratch, args, plan=None):
    comm = plan.comm(name) if plan is not None else None
    if comm is None:
        return list(pl.pallas_call(functools.partial(body), name=name, grid=grid, in_specs=in_specs,
                                   out_specs=out_specs, out_shape=out_shape, scratch_shapes=scratch,
                                   compiler_params=_params(len(grid)))(*args))
    n_in, n_out, n_scr, n_ci, n_co = len(in_specs), len(out_specs), len(scratch), len(comm.ins), len(comm.outs)

    def kern(*refs):
        ins, c_ins, refs = refs[:n_in], refs[n_in:n_in + n_ci], refs[n_in + n_ci:]
        outs, c_outs, refs = refs[:n_out], refs[n_out:n_out + n_co], refs[n_out + n_co:]
        scr, sems = refs[:n_scr], refs[n_scr:]
        ids = [pl.program_id(ax) for ax in range(len(grid))]
        first = functools.reduce(jnp.logical_and, [i == 0 for i in ids])
        last = functools.reduce(jnp.logical_and, [i == g - 1 for i, g in zip(ids, grid)])

        @pl.when(first)
        def _():
            comm.start(c_ins, c_outs, sems)
        body(*ins, *outs, *scr)

        @pl.when(last)
        def _():
            comm.finish(c_ins, c_outs, sems)

    res = pl.pallas_call(kern, name=name, grid=grid, in_specs=list(in_specs) + [ANY] * n_ci,
                         out_specs=list(out_specs) + [ANY] * n_co, out_shape=list(out_shape) + comm.outs,
                         scratch_shapes=list(scratch) + comm.sem_shapes(),
                         compiler_params=_params(len(grid)))(*args, *comm.ins)
    plan.landed(name, list(res[n_out:]))
    return list(res[:n_out])


def _mm_body(dims, has_add, *refs):
    if has_add:
        a_ref, b_ref, add_ref, o_ref = refs
        total = _bdot(a_ref[...], b_ref[...], dims) + add_ref[...]
    else:
        a_ref, b_ref, o_ref = refs
        total = _bdot(a_ref[...], b_ref[...], dims)
    o_ref[...] = total.astype(o_ref.dtype)


def _mm_nt_body(j, n, dy_ref, w_ref, o_ref):
    total = _bdot(dy_ref[:, 0:n], w_ref[0], NT)
    for jj in range(1, j):
        total = total + _bdot(dy_ref[:, jj * n:(jj + 1) * n], w_ref[jj], NT)
    o_ref[...] = total.astype(o_ref.dtype)


def _mm_nn(a, w3, *, name, out_dtype=BF16, add=None, tm=1024, tn=None, plan=None):
    m, kk = a.shape
    j, _, n = w3.shape
    tm, tn = min(tm, m), n if tn is None else tn
    n_t = n // tn
    in_specs = [pl.BlockSpec((tm, kk), lambda i, jj: (i, 0)),
                pl.BlockSpec((None, kk, tn), lambda i, jj: (jj // n_t, 0, jj % n_t))]
    args = [a, w3]
    if add is not None:
        in_specs.append(pl.BlockSpec((tm, tn), lambda i, jj: (i, jj)))
        args.append(add)
    return _call(
        functools.partial(_mm_body, NN, add is not None), name=name,
        grid=(m // tm, j * n_t), in_specs=in_specs,
        out_specs=[pl.BlockSpec((tm, tn), lambda i, jj: (i, jj))],
        out_shape=[jax.ShapeDtypeStruct((m, j * n), out_dtype)], scratch=[], args=args, plan=plan)[0]


def _mm_nt(dy, w3, *, name, out_dtype=BF16, tm=512, tn=1024, plan=None):
    m = dy.shape[0]
    j, kk, n = w3.shape
    tm, tn = min(tm, m), min(tn, kk)
    return _call(
        functools.partial(_mm_nt_body, j, n), name=name,
        grid=(m // tm, kk // tn),
        in_specs=[pl.BlockSpec((tm, j * n), lambda i, q: (i, 0)),
                  pl.BlockSpec((j, tn, n), lambda i, q: (0, q, 0))],
        out_specs=[pl.BlockSpec((tm, tn), lambda i, q: (i, q))],
        out_shape=[jax.ShapeDtypeStruct((m, kk), out_dtype)], scratch=[], args=[dy, w3], plan=plan)[0]


def _mm_tn(a, dy, n, *, name, out_dtype=BF16, tm=512, tn=None, plan=None):
    t, kk = a.shape
    j = dy.shape[1] // n
    tm, tn = min(tm, kk), n if tn is None else tn
    n_t = n // tn
    return _call(
        functools.partial(_mm_body, TN, False), name=name,
        grid=(kk // tm, j * n_t),
        in_specs=[pl.BlockSpec((t, tm), lambda i, jj: (0, i)),
                  pl.BlockSpec((t, tn), lambda i, jj: (0, jj))],
        out_specs=[pl.BlockSpec((None, tm, tn), lambda i, jj: (jj // n_t, i, jj % n_t))],
        out_shape=[jax.ShapeDtypeStruct((j, kk, n), out_dtype)], scratch=[], args=[a, dy], plan=plan)[0]


def _rows(body, ins, outs, *, n_rows, tm, name, plan=None):
    tm = min(tm, n_rows)
    n_steps = n_rows // tm
    in_specs, args = [], []
    for arr, kind, width, block in ins:
        if kind == "row":
            in_specs.append(pl.BlockSpec((tm, width), functools.partial(lambda i, b: (i, b), b=block)))
        elif kind == "prev":
            in_specs.append(pl.BlockSpec((tm, width), functools.partial(lambda i, b: (jnp.maximum(i - 1, 0), b), b=block)))
        elif kind == "next":
            in_specs.append(pl.BlockSpec((tm, width), functools.partial(lambda i, b: (jnp.minimum(i + 1, n_steps - 1), b), b=block)))
        else:
            in_specs.append(pl.BlockSpec(arr.shape, functools.partial(lambda i, nd: (0,) * nd, nd=arr.ndim)))
        args.append(arr)
    out_specs, out_shape = [], []
    for shape, dtype, kind in outs:
        if kind == "row":
            out_specs.append(pl.BlockSpec((tm, shape[1]), lambda i: (i, 0)))
        else:
            out_specs.append(pl.BlockSpec(shape, functools.partial(lambda i, nd: (0,) * nd, nd=len(shape))))
        out_shape.append(jax.ShapeDtypeStruct(shape, dtype))

    def kern(*refs):
        body(pl.program_id(0), n_steps, *refs)

    return _call(kern, name=name, grid=(n_steps,), in_specs=in_specs, out_specs=out_specs, out_shape=out_shape,
                 scratch=[], args=args, plan=plan)


def _acc_rows(i, ref, value):
    @pl.when(i == 0)
    def _():
        ref[...] = jnp.zeros_like(ref)
    ref[...] += jnp.broadcast_to(value, ref.shape)


def _rms_fwd(x, g, *, name, tm=512):
    s, d = x.shape

    def body(i, n, x_ref, g_ref, h_ref):
        xv = x_ref[...]
        r = lax.rsqrt(jnp.mean(xv * xv, axis=-1, keepdims=True) + NORM_EPS)
        h_ref[...] = (xv * r * g_ref[...]).astype(BF16)

    return _rows(body, [(x, "row", d, 0), (g, "full", 0, 0)], [((s, d), BF16, "row")], n_rows=s, tm=tm, name=name)[0]


def _rms_bwd(x, g, dh, dres, *, name, tm=512, plan=None):
    s, d = x.shape

    def body(i, n, x_ref, g_ref, dh_ref, *rest):
        if dres is None:
            dx_ref, dxb_ref, dg_ref = rest
        else:
            dres_ref, dx_ref, dxb_ref, dg_ref = rest
        xv = x_ref[...]
        r = lax.rsqrt(jnp.mean(xv * xv, axis=-1, keepdims=True) + NORM_EPS)
        xhat = xv * r
        dhv = dh_ref[...].astype(F32)
        dxhat = dhv * g_ref[...]
        dx = r * (dxhat - xhat * jnp.mean(dxhat * xhat, axis=-1, keepdims=True))
        if dres is not None:
            dx = dx + dres_ref[...]
        dx_ref[...] = dx
        dxb_ref[...] = dx.astype(BF16)
        _acc_rows(i, dg_ref, jnp.sum(dhv * xhat, axis=0, keepdims=True))

    ins = [(x, "row", d, 0), (g, "full", 0, 0), (dh, "row", d, 0)]
    if dres is not None:
        ins.append((dres, "row", d, 0))
    return _rows(body, ins, [((s, d), F32, "row"), ((s, d), BF16, "row"), ((8, d), F32, "acc")],
                 n_rows=s, tm=tm, name=name, plan=plan)


def _loss_bwd(x, g, target, *, name, tm=512):
    s, d = x.shape

    def body(i, n, x_ref, g_ref, t_ref, dx_ref, dxb_ref, dg_ref, loss_ref):
        xv = x_ref[...]
        gv = g_ref[...]
        r = lax.rsqrt(jnp.mean(xv * xv, axis=-1, keepdims=True) + NORM_EPS)
        xhat = xv * r
        err = xhat * gv - t_ref[...]
        part = 0.5 * jnp.sum(jnp.mean(err * err, axis=-1, keepdims=True), axis=0, keepdims=True)
        dy = err * (1.0 / d)
        dxhat = dy * gv
        dx = r * (dxhat - xhat * jnp.mean(dxhat * xhat, axis=-1, keepdims=True))
        dx_ref[...] = dx
        dxb_ref[...] = dx.astype(BF16)
        _acc_rows(i, dg_ref, jnp.sum(dy * xhat, axis=0, keepdims=True))
        _acc_rows(i, loss_ref, part)

    return _rows(body, [(x, "row", d, 0), (g, "full", 0, 0), (target, "row", d, 0)],
                 [((s, d), F32, "row"), ((s, d), BF16, "row"), ((8, d), F32, "acc"), ((8, LANES), F32, "acc")],
                 n_rows=s, tm=tm, name=name)


def _sigmoid(v):
    return 1.0 / (1.0 + jnp.exp(-v))


def _swiglu_fwd(gu, *, name, tm=512):
    s, two_f = gu.shape
    f = two_f // 2

    def body(i, n, gu_ref, act_ref):
        gate = gu_ref[:, :f].astype(F32)
        up = gu_ref[:, f:].astype(F32)
        act_ref[...] = (gate * _sigmoid(gate) * up).astype(BF16)

    return _rows(body, [(gu, "row", two_f, 0)], [((s, f), BF16, "row")], n_rows=s, tm=tm, name=name)[0]


def _swiglu_bwd(dact, gu, *, name, tm=512):
    s, two_f = gu.shape
    f = two_f // 2

    def body(i, n, dact_ref, gu_ref, dgu_ref):
        gate = gu_ref[:, :f].astype(F32)
        up = gu_ref[:, f:].astype(F32)
        da = dact_ref[...].astype(F32)
        sg = _sigmoid(gate)
        silu = gate * sg
        dgu_ref[:, :f] = (da * up * (sg + silu * (1.0 - sg))).astype(BF16)
        dgu_ref[:, f:] = (da * silu).astype(BF16)

    return _rows(body, [(dact, "row", f, 0), (gu, "row", two_f, 0)], [((s, two_f), BF16, "row")],
                 n_rows=s, tm=tm, name=name)[0]


def _gates_fwd(br_a, br_b, proj, *, name, tm=512):
    s, d = br_a.shape

    def body(i, n, a_ref, b_ref, ga_ref, gb_ref, o_ref):
        o_ref[...] = (_sigmoid(ga_ref[...].astype(F32)) * a_ref[...].astype(F32)
                      + _sigmoid(gb_ref[...].astype(F32)) * b_ref[...].astype(F32)).astype(BF16)

    return _rows(body, [(br_a, "row", d, 0), (br_b, "row", d, 0), (proj, "row", d, 3), (proj, "row", d, 4)],
                 [((s, d), BF16, "row")], n_rows=s, tm=tm, name=name)[0]


def _gates_bwd(dmerged, br_a, br_b, proj, *, name, tm=512):
    s, d = br_a.shape

    def body(i, n, dm_ref, a_ref, b_ref, ga_ref, gb_ref, da_ref, db_ref, dg_ref):
        dm = dm_ref[...].astype(F32)
        sa = _sigmoid(ga_ref[...].astype(F32))
        sb = _sigmoid(gb_ref[...].astype(F32))
        da_ref[...] = (dm * sa).astype(BF16)
        db_ref[...] = (dm * sb).astype(BF16)
        dg_ref[:, :d] = (dm * a_ref[...].astype(F32) * sa * (1.0 - sa)).astype(BF16)
        dg_ref[:, d:] = (dm * b_ref[...].astype(F32) * sb * (1.0 - sb)).astype(BF16)

    return _rows(body, [(dmerged, "row", d, 0), (br_a, "row", d, 0), (br_b, "row", d, 0),
                        (proj, "row", d, 3), (proj, "row", d, 4)],
                 [((s, d), BF16, "row"), ((s, d), BF16, "row"), ((s, 2 * d), BF16, "row")],
                 n_rows=s, tm=tm, name=name)


def _shift_down(cur, prev, k, first):
    row = lax.broadcasted_iota(jnp.int32, cur.shape, 0)
    out = jnp.where(row >= k, pltpu.roll(cur, k, 0), pltpu.roll(prev, k, 0))
    return jnp.where(jnp.logical_and(first, row < k), 0.0, out)


def _shift_up(cur, nxt, k, last):
    tm = cur.shape[0]
    row = lax.broadcasted_iota(jnp.int32, cur.shape, 0)
    out = jnp.where(row < tm - k, pltpu.roll(cur, tm - k, 0), pltpu.roll(nxt, tm - k, 0))
    return jnp.where(jnp.logical_and(last, row >= tm - k), 0.0, out)


def _conv_fwd(proj, conv_w, *, name, tm=512):
    s = proj.shape[0]
    c = CONV_WIDTH

    def body(i, n, u_ref, gb_ref, gc_ref, up_ref, gcp_ref, w_ref, y_ref):
        cu = gc_ref[...].astype(F32) * u_ref[...].astype(F32)
        cup = gcp_ref[...].astype(F32) * up_ref[...].astype(F32)
        first = i == 0
        y = (w_ref[0:1, :] * _shift_down(cu, cup, 2, first) + w_ref[1:2, :] * _shift_down(cu, cup, 1, first)
             + w_ref[2:3, :] * cu)
        y_ref[...] = (gb_ref[...].astype(F32) * y).astype(BF16)

    return _rows(body, [(proj, "row", c, 3), (proj, "row", c, 4), (proj, "row", c, 5),
                        (proj, "prev", c, 3), (proj, "prev", c, 5), (conv_w, "full", 0, 0)],
                 [((s, c), BF16, "row")], n_rows=s, tm=tm, name=name)[0]


def _conv_bwd(dy_b, proj, conv_w, *, name, tm=512, plan=None):
    s = proj.shape[0]
    c = CONV_WIDTH

    def body(i, n, dy_ref, u_ref, gb_ref, gc_ref, up_ref, gcp_ref, dyn_ref, gbn_ref, w_ref, d_ref, dw_ref):
        first, last = i == 0, i == n - 1
        u = u_ref[...].astype(F32)
        gb = gb_ref[...].astype(F32)
        gc = gc_ref[...].astype(F32)
        cu = gc * u
        cup = gcp_ref[...].astype(F32) * up_ref[...].astype(F32)
        cu1 = _shift_down(cu, cup, 1, first)
        cu2 = _shift_down(cu, cup, 2, first)
        conv = w_ref[0:1, :] * cu2 + w_ref[1:2, :] * cu1 + w_ref[2:3, :] * cu
        dy = dy_ref[...].astype(F32)
        dyc = dy * gb
        dycn = dyn_ref[...].astype(F32) * gbn_ref[...].astype(F32)
        dcu = (w_ref[2:3, :] * dyc + w_ref[1:2, :] * _shift_up(dyc, dycn, 1, last)
               + w_ref[0:1, :] * _shift_up(dyc, dycn, 2, last))
        d_ref[:, 0:c] = (dcu * gc).astype(BF16)
        d_ref[:, c:2 * c] = (dy * conv).astype(BF16)
        d_ref[:, 2 * c:3 * c] = (dcu * u).astype(BF16)
        row = lax.broadcasted_iota(jnp.int32, (8, c), 0)
        dw = (jnp.where(row == 0, jnp.sum(dyc * cu2, axis=0, keepdims=True), 0.0)
              + jnp.where(row == 1, jnp.sum(dyc * cu1, axis=0, keepdims=True), 0.0)
              + jnp.where(row == 2, jnp.sum(dyc * cu, axis=0, keepdims=True), 0.0))

        @pl.when(first)
        def _():
            dw_ref[...] = jnp.zeros_like(dw_ref)
        dw_ref[...] += dw

    return _rows(body, [(dy_b, "row", c, 0), (proj, "row", c, 3), (proj, "row", c, 4), (proj, "row", c, 5),
                        (proj, "prev", c, 3), (proj, "prev", c, 5), (dy_b, "next", c, 0), (proj, "next", c, 4),
                        (conv_w, "full", 0, 0)],
                 [((s, 3 * c), BF16, "row"), ((8, c), F32, "acc")], n_rows=s, tm=tm, name=name, plan=plan)


def _mem_probs(q, k, scale):
    sc = _bdot(q, k, NT) * scale
    sc = sc - jnp.max(sc, axis=-1, keepdims=True)
    p = jnp.exp(sc)
    return p / jnp.sum(p, axis=-1, keepdims=True)


def _memattn_fwd(qm, kv, *, name, tm=512):
    s, d = qm.shape
    hd = d // MEM_HEADS
    scale = 1.0 / math.sqrt(hd)

    def body(i, n, q_ref, kv_ref, o_ref):
        for h in range(MEM_HEADS):
            cols = slice(h * hd, (h + 1) * hd)
            p = _mem_probs(q_ref[:, cols], kv_ref[:, cols], scale)
            o_ref[:, cols] = _bdot(p, kv_ref[:, d + h * hd:d + (h + 1) * hd], NN).astype(BF16)

    return _rows(body, [(qm, "row", d, 0), (kv, "full", 0, 0)], [((s, d), BF16, "row")], n_rows=s, tm=tm, name=name)[0]


def _memattn_bwd(dom, qm, kv, *, name, tm=512):
    s, d = qm.shape
    hd = d // MEM_HEADS
    scale = 1.0 / math.sqrt(hd)

    def body(i, n, do_ref, q_ref, kv_ref, dq_ref, dkv_ref):
        @pl.when(i == 0)
        def _():
            dkv_ref[...] = jnp.zeros_like(dkv_ref)
        for h in range(MEM_HEADS):
            cols = slice(h * hd, (h + 1) * hd)
            vcols = slice(d + h * hd, d + (h + 1) * hd)
            q, k, v, do = q_ref[:, cols], kv_ref[:, cols], kv_ref[:, vcols], do_ref[:, cols]
            p = _mem_probs(q, k, scale)
            dp = _bdot(do, v, NT)
            ds = p * (dp - jnp.sum(dp * p, axis=-1, keepdims=True)) * scale
            dq_ref[:, cols] = _bdot(ds, k, NN).astype(BF16)
            dkv_ref[:, cols] += _bdot(ds, q, TN)
            dkv_ref[:, vcols] += _bdot(p, do, TN)

    return _rows(body, [(dom, "row", d, 0), (qm, "row", d, 0), (kv, "full", 0, 0)],
                 [((s, d), BF16, "row"), (kv.shape, F32, "acc")], n_rows=s, tm=tm, name=name)


def _sb_consts(t):
    row = lax.broadcasted_iota(jnp.int32, (t, t), 0)
    col = lax.broadcasted_iota(jnp.int32, (t, t), 1)
    lane = lax.broadcasted_iota(jnp.int32, (t, LANES), 1)
    return row, col, lane < SB_HEAD_DIM


def _log_fail(z):
    return jnp.minimum(-z, 0.0) - jnp.log(1.0 + jnp.exp(-jnp.abs(z)))


def _tri_sum(v, tri):
    hi = v.astype(BF16)
    lo = (v - hi.astype(F32)).astype(BF16)
    return _bdot(hi, tri, NN) + _bdot(lo, tri, NN)


def _sb_fwd(proj, *, name, plan=None):
    s = proj.shape[0]
    t = SB_TILE
    n_q = s // t
    scale = 1.0 / math.sqrt(SB_HEAD_DIM)
    k_blk, v_blk = SB_WIDTH // LANES, 2 * SB_WIDTH // LANES

    def body(q_ref, k_ref, v_ref, o_ref, c_ref, first_ref, acc_ref):
        i = pl.program_id(1)
        row, col, head0 = _sb_consts(t)
        later = (row > col).astype(BF16)
        valid = col < row
        qs = q_ref[...] * scale
        q2 = (jnp.where(head0, qs, 0), jnp.where(head0, 0, qs))

        def tile(kb, carry, diag):
            kt = k_ref[pl.ds(pl.multiple_of(kb * t, t), t), :]
            vt = v_ref[pl.ds(pl.multiple_of(kb * t, t), t), :]
            heads = range(2)
            z = [_bdot(q2[h], kt, NT) for h in heads]
            lf = [_log_fail(z[h]) for h in heads]
            if diag:
                lf = [jnp.where(valid, lf[h], 0.0) for h in heads]
            cum = [_tri_sum(lf[h], later) for h in heads]
            w = [jnp.exp(z[h] + lf[h] + cum[h] + carry[h]) for h in heads]
            if diag:
                w = [jnp.where(valid, w[h], 0.0) for h in heads]
            for h in heads:
                acc_ref[h] += _bdot(w[h], vt, NN)
            return tuple(carry[h] + cum[h][:, 0:1] + lf[h][:, 0:1] for h in heads)

        acc_ref[...] = jnp.zeros_like(acc_ref)
        zero = jnp.zeros((t, 1), F32)

        def alive(carry):
            return (jnp.maximum(jnp.max(carry[0]), jnp.max(carry[1])) > -SB_DEAD).astype(jnp.int32)

        def step(state):
            kb, _, c0, c1 = state
            new = tile(kb, (c0, c1), False)
            return kb - 1, alive(new), new[0], new[1]

        carry = tile(i, (zero, zero), True)
        kb, _, c0, c1 = lax.while_loop(lambda st: jnp.logical_and(st[0] >= 0, st[1] > 0), step,
                                       (i - 1, alive(carry), carry[0], carry[1]))
        o_ref[...] = jnp.where(head0, acc_ref[0], acc_ref[1]).astype(BF16)
        c_ref[...] = jnp.where(lax.broadcasted_iota(jnp.int32, (t, 2), 1) == 0, c0, c1)
        first_ref[pl.program_id(0), i] = (kb + 1).astype(F32)

    return _call(
        body, name=name, grid=(SB_HEADS // 2, n_q),
        in_specs=[pl.BlockSpec((t, LANES), lambda p, i: (i, p)),
                  pl.BlockSpec((s, LANES), lambda p, i: (0, k_blk + p)),
                  pl.BlockSpec((s, LANES), lambda p, i: (0, v_blk + p))],
        out_specs=[pl.BlockSpec((t, LANES), lambda p, i: (i, p)),
                   pl.BlockSpec((None, t, 2), lambda p, i: (p, i, 0)),
                   pl.BlockSpec(memory_space=pltpu.SMEM)],
        out_shape=[jax.ShapeDtypeStruct((s, SB_WIDTH), BF16), jax.ShapeDtypeStruct((SB_HEADS // 2, s, 2), F32),
                   jax.ShapeDtypeStruct((SB_HEADS // 2, n_q), F32)],
        scratch=[pltpu.VMEM((2, t, LANES), F32)], args=[proj, proj, proj], plan=plan)


def _sb_bwd(proj, do_a, ctot, first, *, name, plan=None):
    s = proj.shape[0]
    t = SB_TILE
    n_q = s // t
    scale = 1.0 / math.sqrt(SB_HEAD_DIM)
    k_blk, v_blk = SB_WIDTH // LANES, 2 * SB_WIDTH // LANES

    def body(q_ref, k_ref, v_ref, do_ref, c_ref, first_ref, dq_ref, dk_ref, dv_ref, dq_acc, dk_acc, dv_acc):
        i = pl.program_id(1)
        kb0 = jnp.clip(first_ref[pl.program_id(0), i].astype(jnp.int32), 0, i)
        row, col, head0 = _sb_consts(t)
        upto = (row <= col).astype(BF16)
        before = (row < col).astype(BF16)
        valid = col < row
        qs = q_ref[...] * scale
        q2 = (jnp.where(head0, qs, 0), jnp.where(head0, 0, qs))
        do = do_ref[...]
        do2 = (jnp.where(head0, do, 0), jnp.where(head0, 0, do))
        ctot2 = (c_ref[:, 0:1], c_ref[:, 1:2])

        @pl.when(i == 0)
        def _():
            dk_acc[...] = jnp.zeros_like(dk_acc)
            dv_acc[...] = jnp.zeros_like(dv_acc)
        dq_acc[...] = jnp.zeros_like(dq_acc)

        def tile(kb, carry, diag):
            rows = pl.ds(pl.multiple_of(kb * t, t), t)
            kt = k_ref[rows, :]
            vt = v_ref[rows, :]
            heads = range(2)
            lf_before, g_before = carry[0::2], carry[1::2]
            z = [_bdot(q2[h], kt, NT) for h in heads]
            dw = [_bdot(do2[h], vt, NT) for h in heads]
            lf = [_log_fail(z[h]) for h in heads]
            if diag:
                lf = [jnp.where(valid, lf[h], 0.0) for h in heads]
            cum = [_tri_sum(lf[h], upto) for h in heads]
            beta = [jnp.exp(z[h] + lf[h]) for h in heads]
            w = [beta[h] * jnp.exp(ctot2[h] - lf_before[h] - cum[h]) for h in heads]
            if diag:
                w = [jnp.where(valid, w[h], 0.0) for h in heads]
            g = [w[h] * dw[h] for h in heads]
            g_sum = [g_before[h] + _bdot(g[h], before, NN) for h in heads]
            for h in heads:
                dv_acc[rows, :] += _bdot(w[h], do2[h], TN)
            dz = [g[h] * jnp.exp(lf[h]) - beta[h] * g_sum[h] for h in heads]
            if diag:
                dz = [jnp.where(valid, dz[h], 0.0) for h in heads]
            for h in heads:
                dq_acc[h] += _bdot(dz[h], kt, NN)
                dk_acc[rows, :] += _bdot(dz[h], q2[h], TN)
            t_last = slice(t - 1, t)
            new = []
            for h in heads:
                new += [lf_before[h] + cum[h][:, t_last], g_sum[h][:, t_last] + g[h][:, t_last]]
            return tuple(new)

        zero = jnp.zeros((t, 1), F32)
        carry = lax.fori_loop(kb0, i, lambda n, c: tile(n, c, False), (zero,) * 4)
        tile(i, carry, True)
        dq_ref[...] = (jnp.where(head0, dq_acc[0], dq_acc[1]) * scale).astype(BF16)

        @pl.when(i == n_q - 1)
        def _():
            dk_ref[...] = dk_acc[...].astype(BF16)
            dv_ref[...] = dv_acc[...].astype(BF16)

    outs = _call(
        body, name=name, grid=(SB_HEADS // 2, n_q),
        in_specs=[pl.BlockSpec((t, LANES), lambda p, i: (i, p)),
                  pl.BlockSpec((s, LANES), lambda p, i: (0, k_blk + p)),
                  pl.BlockSpec((s, LANES), lambda p, i: (0, v_blk + p)),
                  pl.BlockSpec((t, LANES), lambda p, i: (i, p)),
                  pl.BlockSpec((None, t, 2), lambda p, i: (p, i, 0)),
                  pl.BlockSpec(memory_space=pltpu.SMEM)],
        out_specs=[pl.BlockSpec((t, LANES), lambda p, i: (i, p)),
                   pl.BlockSpec((s, LANES), lambda p, i: (0, p)),
                   pl.BlockSpec((s, LANES), lambda p, i: (0, p))],
        out_shape=[jax.ShapeDtypeStruct((s, SB_WIDTH), BF16)] * 3,
        scratch=[pltpu.VMEM((2, t, LANES), F32), pltpu.VMEM((s, LANES), F32), pltpu.VMEM((s, LANES), F32)],
        args=[proj, proj, proj, do_a, ctot, first], plan=plan)
    return jnp.concatenate(outs, axis=1)


def _local_step(x, mem, target, gains, conv_w, plan):
    g_mix, g_memq, g_memkv, g_ffn, g_fin = gains
    d = x.shape[1]

    h0 = _rms_fwd(x, g_mix, name="rms_mix")
    w_in = plan.weight("in")
    proj = _mm_nn(h0, w_in, name="mm_in", plan=plan)
    o_a, ctot, first = _sb_fwd(proj, name="sb_fwd", plan=plan)
    y_b = _conv_fwd(proj, conv_w, name="conv_fwd")
    w_a, w_b, w_mix = plan.weight("a"), plan.weight("b"), plan.weight("mix")
    br_a = _mm_nn(o_a, w_a, name="mm_branch_a")
    br_b = _mm_nn(y_b, w_b, name="mm_branch_b")
    merged = _gates_fwd(br_a, br_b, proj, name="gates_fwd")
    x1 = _mm_nn(merged, w_mix, name="mm_mix", out_dtype=F32, add=x)
    hq = _rms_fwd(x1, g_memq, name="rms_memq")
    w_mq, w_kv, w_mo = plan.weight("mq"), plan.weight("kv"), plan.weight("mo")
    qm = _mm_nn(hq, w_mq, name="mm_memq")
    mn = _rms_fwd(mem, g_memkv, name="rms_memkv")
    kv = _mm_nn(mn, w_kv, name="mm_memkv")
    om = _memattn_fwd(qm, kv, name="memattn_fwd")
    x2 = _mm_nn(om, w_mo, name="mm_memo", out_dtype=F32, add=x1)
    hf = _rms_fwd(x2, g_ffn, name="rms_ffn")
    w_fi = plan.weight("fi")
    gu = _mm_nn(hf, w_fi, name="mm_ffn_in", tn=512, plan=plan)
    act = _swiglu_fwd(gu, name="swiglu_fwd")
    w_fo = plan.weight("fo")
    x3 = _mm_nn(act, w_fo, name="mm_ffn_out", out_dtype=F32, add=x2, tm=512)

    dx3, dx3b, dg_fin, loss = _loss_bwd(x3, g_fin, target, name="loss_bwd")

    plan.grad("fo", _mm_tn(act, dx3b, d, name="mm_d_w_ffn_out", tm=256))
    dact = _mm_nt(dx3b, w_fo, name="mm_d_act", tn=1408)
    dgu = _swiglu_bwd(dact, gu, name="swiglu_bwd")
    plan.grad("fi", _mm_tn(hf, dgu, w_fi.shape[2], name="mm_d_w_ffn_in", tn=512))
    dhf = _mm_nt(dgu, w_fi, name="mm_d_hf", out_dtype=F32, plan=plan)
    dx2, dx2b, dg_ffn = _rms_bwd(x2, g_ffn, dhf, dx3, name="rms_ffn_bwd")

    plan.grad("mo", _mm_tn(om, dx2b, d, name="mm_d_w_memo"))
    dom = _mm_nt(dx2b, w_mo, name="mm_d_om")
    dqm, dkv = _memattn_bwd(dom, qm, kv, name="memattn_bwd")
    plan.grad("mq", _mm_tn(hq, dqm, d, name="mm_d_w_memq"))
    dhq = _mm_nt(dqm, w_mq, name="mm_d_hq", out_dtype=F32)
    dx1, dx1b, dg_memq = _rms_bwd(x1, g_memq, dhq, dx2, name="rms_memq_bwd")
    plan.grad("kv", _mm_tn(mn, dkv, w_kv.shape[2], name="mm_d_w_memkv"))
    dmn = _mm_nt(dkv, w_kv, name="mm_d_mn", out_dtype=F32)
    _, _, dg_memkv = _rms_bwd(mem, g_memkv, dmn, None, name="rms_memkv_bwd")

    plan.grad("mix", _mm_tn(merged, dx1b, d, name="mm_d_w_mix"))
    dmerged = _mm_nt(dx1b, w_mix, name="mm_d_merged", plan=plan)
    dbr_a, dbr_b, dgab = _gates_bwd(dmerged, br_a, br_b, proj, name="gates_bwd")
    plan.grad("a", _mm_tn(o_a, dbr_a, d, name="mm_d_w_branch_a"))
    do_a = _mm_nt(dbr_a, w_a, name="mm_d_o_a")
    plan.grad("b", _mm_tn(y_b, dbr_b, d, name="mm_d_w_branch_b"))
    dy_b = _mm_nt(dbr_b, w_b, name="mm_d_y_b")
    dconv, dconv_w = _conv_bwd(dy_b, proj, conv_w, name="conv_bwd", plan=plan)
    dqkv = _sb_bwd(proj, do_a, ctot, first, name="sb_bwd", plan=plan)
    dproj = jnp.concatenate([dqkv, dconv, dgab], axis=1)
    plan.grad("in", _mm_tn(h0, dproj, w_in.shape[2], name="mm_d_w_in"))
    dh0 = _mm_nt(dproj, w_in, name="mm_d_h0", out_dtype=F32, plan=plan)
    dx0, _, dg_mix = _rms_bwd(x, g_mix, dh0, dx1, name="rms_mix_bwd", plan=plan)

    return dx0, (dg_mix, dg_memq, dg_memkv, dg_ffn, dg_fin, dconv_w, loss)


def _row_tile(a, target=512):
    tm = min(a, target)
    while a % tm:
        tm -= 8
    return tm


def _sum_with_sibling(part, recv, core, *, name):
    _, a, b = part.shape
    tm = _row_tile(a)

    def body(core_ref, p_ref, r_ref, o_ref):
        o_ref[...] = (p_ref[...].astype(F32) + r_ref[...].astype(F32)).astype(o_ref.dtype)

    return pl.pallas_call(
        body, name=name,
        grid_spec=pltpu.PrefetchScalarGridSpec(
            num_scalar_prefetch=1, grid=(N_CHIP, a // tm),
            in_specs=[pl.BlockSpec((None, tm, b), lambda q, i, core_ref: (2 * q + core_ref[0], i, 0)),
                      pl.BlockSpec((None, tm, b), lambda q, i, core_ref: (q, i, 0))],
            out_specs=pl.BlockSpec((None, tm, b), lambda q, i, core_ref: (q, i, 0))),
        out_shape=jax.ShapeDtypeStruct((N_CHIP, a, b), part.dtype), compiler_params=_params(2))(core, part, recv)


def _adam_math(wv, g, m, v):
    m = ADAM_B1 * m + (1.0 - ADAM_B1) * g
    v = ADAM_B2 * v + (1.0 - ADAM_B2) * (g * g)
    m_hat = m / (1.0 - ADAM_B1 ** ADAM_STEP)
    v_hat = v / (1.0 - ADAM_B2 ** ADAM_STEP)
    delta = -ADAM_LR * (m_hat / (jnp.sqrt(v_hat) + ADAM_EPS) + ADAM_WD * wv)
    return delta, m, v


def _adam_sharded(wv, m, v, own, recv, chip, *, name):
    a, b = wv.shape
    tm = _row_tile(a)

    def body(chip_ref, w_ref, m_ref, v_ref, own_ref, recv_ref, g_ref, d_ref, nm_ref, nv_ref):
        g = own_ref[...].astype(F32)
        for j in range(3):
            g = g + recv_ref[j].astype(F32)
        delta, nm, nv = _adam_math(w_ref[...], g, m_ref[...], v_ref[...])
        g_ref[...] = g
        d_ref[...] = delta
        nm_ref[...] = nm
        nv_ref[...] = nv

    tile = pl.BlockSpec((tm, b), lambda i, chip_ref: (i, 0))
    return pl.pallas_call(
        body, name=name,
        grid_spec=pltpu.PrefetchScalarGridSpec(
            num_scalar_prefetch=1, grid=(a // tm,),
            in_specs=[tile, tile, tile,
                      pl.BlockSpec((None, tm, b), lambda i, chip_ref: (chip_ref[0], i, 0)),
                      pl.BlockSpec((3, tm, b), lambda i, chip_ref: (0, i, 0))],
            out_specs=[tile] * 4),
        out_shape=[jax.ShapeDtypeStruct((a, b), F32)] * 4, compiler_params=_params(1))(chip, wv, m, v, own, recv)


def _sum_devices(gathered, *, name):
    _, r, c = gathered.shape

    def body(g_ref, o_ref):
        total = g_ref[0]
        for j in range(1, N_DEV):
            total = total + g_ref[j]
        o_ref[...] = total

    return pl.pallas_call(body, name=name, out_shape=jax.ShapeDtypeStruct((r, c), F32))(gathered)


def _adam_small(wv, g, m, v, *, name):
    def body(w_ref, g_ref, m_ref, v_ref, d_ref, nm_ref, nv_ref):
        delta, nm, nv = _adam_math(w_ref[...], g_ref[...], m_ref[...], v_ref[...])
        d_ref[...] = delta
        nm_ref[...] = nm
        nv_ref[...] = nv

    return pl.pallas_call(body, name=name, out_shape=[jax.ShapeDtypeStruct(wv.shape, F32)] * 3)(wv, g, m, v)


BIG = ("in", "a", "b", "mix", "mq", "kv", "mo", "fi", "fo")
ROW_SHARDED = ("mix", "mq", "mo", "fo")
UNSHARDED = ("a", "b", "fi")
SMALL_ROWS = 16


class _Plan:
    GATHER_ON = {"mm_in": ("a", "b", "mix", "kv"), "sb_fwd": ("mq", "mo", "fi"), "mm_ffn_in": ("fo",)}
    SIBLING_ON = {"mm_d_hf": ("fo", "fi"), "mm_d_merged": ("mo", "mq", "kv"), "conv_bwd": ("mix", "a", "b"),
                  "mm_d_h0": ("in",)}
    CHIPS_ON = {"sb_bwd": ("fo", "fi", "mo", "mq", "kv", "mix", "a", "b"), "rms_mix_bwd": ("in",)}

    def __init__(self, shards, core):
        self.shards, self.core = shards, core
        self.w, self.parts, self.chip_sums, self.from_chips = {}, {}, {}, {}

    def comm(self, name):
        if name in self.GATHER_ON:
            return _gather_comm([self.shards[k] for k in self.GATHER_ON[name]])
        if name in self.SIBLING_ON:
            return _sibling_comm([self.parts[k] for k in self.SIBLING_ON[name]])
        if name in self.CHIPS_ON:
            return _chips_comm([self.chip_sums[k] for k in self.CHIPS_ON[name]])
        return None

    def landed(self, name, outs):
        if name in self.GATHER_ON:
            for k, o in zip(self.GATHER_ON[name], outs):
                self.set_weight(k, o)
        elif name in self.SIBLING_ON:
            for k, o in zip(self.SIBLING_ON[name], outs):
                self.chip_sums[k] = _sum_with_sibling(self.parts[k], o, self.core, name="sum_with_sibling_" + k)
        else:
            for k, o in zip(self.CHIPS_ON[name], outs):
                self.from_chips[k] = o

    def set_weight(self, k, gathered):
        _, a, b = gathered.shape
        if k in ROW_SHARDED:
            gathered = gathered.reshape(1, N_DEV * a, b)
        elif k in UNSHARDED:
            gathered = jnp.transpose(gathered, (1, 0, 2)).reshape(1, a, N_DEV * b)
        self.w[k] = gathered

    def weight(self, k):
        return self.w[k]

    def grad(self, k, g):
        _, a, b = g.shape
        if k in ROW_SHARDED:
            g = g.reshape(N_DEV, a // N_DEV, b)
        elif k in UNSHARDED:
            g = jnp.transpose(g.reshape(a, N_DEV, b // N_DEV), (1, 0, 2))
        self.parts[k] = g


def kernel(x, mem, norm_mix, w_in, conv_w, w_branch_a, w_branch_b, w_mix_out, norm_mem_q, norm_mem_kv, w_mem_q, w_mem_kv, w_mem_o, norm_ffn, w_ffn_in, w_ffn_out, norm_final, loss_target, m_norm_mix, m_w_in, m_conv_w, m_w_branch_a, m_w_branch_b, m_w_mix_out, m_norm_mem_q, m_norm_mem_kv, m_w_mem_q, m_w_mem_kv, m_w_mem_o, m_norm_ffn, m_w_ffn_in, m_w_ffn_out, m_norm_final, v_norm_mix, v_w_in, v_conv_w, v_w_branch_a, v_w_branch_b, v_w_mix_out, v_norm_mem_q, v_norm_mem_kv, v_w_mem_q, v_w_mem_kv, v_w_mem_o, v_norm_ffn, v_w_ffn_in, v_w_ffn_out, v_norm_final):
    d = x.shape[-1]
    xi, yi, ci = lax.axis_index("x"), lax.axis_index("y"), lax.axis_index("c")
    core = jnp.reshape(ci, (1,)).astype(jnp.int32)
    chip = jnp.reshape(2 * xi + yi, (1,)).astype(jnp.int32)
    dev = 4 * xi + 2 * yi + ci

    big_w = dict(zip(BIG, (w_in, w_branch_a, w_branch_b, w_mix_out, w_mem_q, w_mem_kv, w_mem_o, w_ffn_in, w_ffn_out)))
    big_m = dict(zip(BIG, (m_w_in, m_w_branch_a, m_w_branch_b, m_w_mix_out, m_w_mem_q, m_w_mem_kv, m_w_mem_o, m_w_ffn_in, m_w_ffn_out)))
    big_v = dict(zip(BIG, (v_w_in, v_w_branch_a, v_w_branch_b, v_w_mix_out, v_w_mem_q, v_w_mem_kv, v_w_mem_o, v_w_ffn_in, v_w_ffn_out)))

    plan = _Plan({k: big_w[k][0].astype(BF16) for k in BIG}, core)
    n_conv = conv_w.shape[-1]
    conv_pad = jnp.zeros((8, LANES), F32).at[:3, :n_conv].set(conv_w[0])
    w_in_all, conv_all = _exchange(_gather_comm([plan.shards["in"], conv_pad]), name="gather_first")
    plan.set_weight("in", w_in_all)
    conv_full = jnp.transpose(conv_all[:, :3, :n_conv], (1, 0, 2)).reshape(3, CONV_WIDTH)

    gains = (norm_mix, norm_mem_q, norm_mem_kv, norm_ffn, norm_final.reshape(1, d))
    dx0, small = _local_step(x[0], mem[0], loss_target[0], gains, conv_full, plan)

    grads, deltas, new_m, new_v = {}, {}, {}, {}
    for k in BIG:
        lead = big_w[k].shape
        g, dl, nm, nv = _adam_sharded(big_w[k][0], big_m[k][0], big_v[k][0], plan.chip_sums[k], plan.from_chips[k],
                                      chip, name="adam_" + k)
        grads[k], deltas[k], new_m[k], new_v[k] = (t.reshape(lead) for t in (g, dl, nm, nv))

    dg_mix, dg_memq, dg_memkv, dg_ffn, dg_fin, dconv_w, loss = small
    conv_rows = jnp.zeros((3, d), F32).at[:, :CONV_WIDTH].set(dconv_w[:3])
    block = jnp.concatenate([dg_mix[:1], dg_memq[:1], dg_memkv[:1], dg_ffn[:1], dg_fin[:1], conv_rows,
                             jnp.broadcast_to(loss[:1, :1], (1, d)), jnp.zeros((SMALL_ROWS - 9, d), F32)], axis=0)
    total = _sum_devices(_exchange(_gather_comm([block]), name="gather_small")[0], name="sum_small")
    g_conv = lax.dynamic_slice(total[5:8, :CONV_WIDTH], (0, dev * n_conv), (3, n_conv))
    small_w = [norm_mix, norm_mem_q, norm_mem_kv, norm_ffn, norm_final.reshape(1, d), conv_w[0]]
    small_m = [m_norm_mix, m_norm_mem_q, m_norm_mem_kv, m_norm_ffn, m_norm_final.reshape(1, d), m_conv_w[0]]
    small_v = [v_norm_mix, v_norm_mem_q, v_norm_mem_kv, v_norm_ffn, v_norm_final.reshape(1, d), v_conv_w[0]]
    small_g = [total[0:1], total[1:2], total[2:3], total[3:4], total[4:5], g_conv]
    small_names = ["norm_mix", "norm_mem_q", "norm_mem_kv", "norm_ffn", "norm_final", "conv_w"]
    sg, sd, sm, sv = {}, {}, {}, {}
    for nme, wv, g, m, v in zip(small_names, small_w, small_g, small_m, small_v):
        dl, nm, nv = _adam_small(wv, g, m, v, name="adam_" + nme)
        shape = norm_final.shape if nme == "norm_final" else (conv_w.shape if nme == "conv_w" else wv.shape)
        sg[nme], sd[nme], sm[nme], sv[nme] = (t.reshape(shape) for t in (g, dl, nm, nv))

    def ordered(big, sml):
        return (sml["norm_mix"], big["in"], sml["conv_w"], big["a"], big["b"], big["mix"], sml["norm_mem_q"],
                sml["norm_mem_kv"], big["mq"], big["kv"], big["mo"], sml["norm_ffn"], big["fi"], big["fo"],
                sml["norm_final"])

    loss_out = total[8, 0]
    grad_x = dx0.reshape(x.shape)
    return (loss_out, grad_x, *ordered(grads, sg), *ordered(deltas, sd), *ordered(new_m, sm), *ordered(new_v, sv))
```

```python
import functools
import math

import jax
import jax.numpy as jnp
from jax import lax
from jax.experimental import pallas as pl
from jax.experimental.pallas import tpu as pltpu

F32 = jnp.float32
BF16 = jnp.bfloat16
MESH = pl.DeviceIdType.MESH

N_DEV = 8
N_CHIP = 4
NORM_EPS = 1e-6
SB_HEADS = 8
SB_HEAD_DIM = 64
SB_WIDTH = SB_HEADS * SB_HEAD_DIM
CONV_WIDTH = 512
MEM_HEADS = 4
ADAM_LR = 0.001
ADAM_B1 = 0.9
ADAM_B2 = 0.999
ADAM_EPS = 1e-08
ADAM_WD = 0.01
ADAM_STEP = 10

LANES = 128
VMEM_LIMIT_BYTES = 52 * 1024 * 1024
SB_TILE = 256
SB_DEAD = 110.0

ANY = pl.BlockSpec(memory_space=pl.ANY)


def _params(n_grid):
    return pltpu.CompilerParams(dimension_semantics=("arbitrary",) * n_grid, vmem_limit_bytes=VMEM_LIMIT_BYTES)


def _bdot(a, b, dims):
    return lax.dot_general(a.astype(BF16), b.astype(BF16), (dims, ((), ())), preferred_element_type=F32)


NN = ((1,), (0,))
NT = ((1,), (1,))
TN = ((0,), (0,))


class _Comm:
    def __init__(self, ins, outs, n_sems, start, finish):
        self.ins, self.outs, self.n_sems, self.start, self.finish = ins, outs, n_sems, start, finish

    def sem_shapes(self):
        return [pltpu.SemaphoreType.DMA((k,)) for k in self.n_sems]


def _place():
    return lax.axis_index("x"), lax.axis_index("y"), lax.axis_index("c")


def _gather_comm(shards):
    n = len(shards)

    def copies(ins, outs, sems):
        send_sems, recv_sems, _ = sems
        x, y, c = _place()
        chips = [(1 - x, y), (x, 1 - y), (1 - x, 1 - y)]

        def copy(a, k, block, to, from_shard=False):
            dst = outs[a].at[4 * block[0] + 2 * block[1] + block[2]]
            return pltpu.make_async_remote_copy(
                src_ref=ins[a] if from_shard else dst, dst_ref=dst, send_sem=send_sems.at[a * 7 + k],
                recv_sem=recv_sems.at[a * 7 + k], device_id=to, device_id_type=MESH)

        me, sibling = (x, y, c), (x, y, 1 - c)
        own = [[copy(a, 0, me, sibling, True)] + [copy(a, 1 + j, me, (*chip, c), True) for j, chip in enumerate(chips)]
               for a in range(n)]
        landed = [[copy(a, 1 + j, (*chip, c), me) for j, chip in enumerate(chips)] for a in range(n)]
        passed = [[copy(a, 4 + j, (*chip, c), sibling) for j, chip in enumerate(chips)] for a in range(n)]
        from_sibling = [[copy(a, 0, sibling, me)] + [copy(a, 4 + j, (*chip, 1 - c), me) for j, chip in enumerate(chips)]
                        for a in range(n)]
        local = [pltpu.make_async_copy(ins[a], outs[a].at[4 * x + 2 * y + c], sems[2].at[a]) for a in range(n)]
        return own, landed, passed, from_sibling, local

    def start(ins, outs, sems):
        own, _, _, _, local = copies(ins, outs, sems)
        for a in range(n):
            local[a].start()
            for cp in own[a]:
                cp.start()

    def finish(ins, outs, sems):
        own, landed, passed, from_sibling, local = copies(ins, outs, sems)
        for a in range(n):
            for arrived, onward in zip(landed[a], passed[a]):
                arrived.wait_recv()
                onward.start()
        for a in range(n):
            for cp in from_sibling[a]:
                cp.wait_recv()
        for a in range(n):
            for cp in own[a] + passed[a]:
                cp.wait_send()
            local[a].wait()

    outs = [jax.ShapeDtypeStruct((N_DEV,) + s.shape, s.dtype) for s in shards]
    return _Comm(list(shards), outs, (7 * n, 7 * n, n), start, finish)


def _sibling_comm(parts):
    n = len(parts)

    def copies(ins, outs, sems):
        x, y, c = _place()
        return [pltpu.make_async_remote_copy(
            src_ref=ins[a].at[2 * q + 1 - c], dst_ref=outs[a].at[q], send_sem=sems[0].at[a * N_CHIP + q],
            recv_sem=sems[1].at[a * N_CHIP + q], device_id=(x, y, 1 - c), device_id_type=MESH)
            for a in range(n) for q in range(N_CHIP)]

    def start(ins, outs, sems):
        for cp in copies(ins, outs, sems):
            cp.start()

    def finish(ins, outs, sems):
        cps = copies(ins, outs, sems)
        for cp in cps:
            cp.wait_recv()
        for cp in cps:
            cp.wait_send()

    outs = [jax.ShapeDtypeStruct((N_CHIP,) + p.shape[1:], p.dtype) for p in parts]
    return _Comm(list(parts), outs, (N_CHIP * n, N_CHIP * n), start, finish)


def _chips_comm(parts):
    n = len(parts)

    def copies(ins, outs, sems):
        x, y, c = _place()
        chips = [(1 - x, y), (x, 1 - y), (1 - x, 1 - y)]
        return [pltpu.make_async_remote_copy(
            src_ref=ins[a].at[2 * px + py], dst_ref=outs[a].at[j], send_sem=sems[0].at[a * 3 + j],
            recv_sem=sems[1].at[a * 3 + j], device_id=(px, py, c), device_id_type=MESH)
            for a in range(n) for j, (px, py) in enumerate(chips)]

    def start(ins, outs, sems):
        for cp in copies(ins, outs, sems):
            cp.start()

    def finish(ins, outs, sems):
        cps = copies(ins, outs, sems)
        for cp in cps:
            cp.wait_recv()
        for cp in cps:
            cp.wait_send()

    outs = [jax.ShapeDtypeStruct((3,) + p.shape[1:], p.dtype) for p in parts]
    return _Comm(list(parts), outs, (3 * n, 3 * n), start, finish)


def _exchange(comm, *, name):
    n_ci, n_co = len(comm.ins), len(comm.outs)

    def kern(*refs):
        c_ins, c_outs, sems = refs[:n_ci], refs[n_ci:n_ci + n_co], refs[n_ci + n_co:]
        comm.start(c_ins, c_outs, sems)
        comm.finish(c_ins, c_outs, sems)

    return pl.pallas_call(kern, name=name, in_specs=[ANY] * n_ci, out_specs=[ANY] * n_co, out_shape=comm.outs,
                          scratch_shapes=comm.sem_shapes())(*comm.ins)


def _call(body, *, name, grid, in_specs, out_specs, out_shape, scratch, args, plan=None):
    comm = plan.comm(name) if plan is not None else None
    if comm is None:
        return list(pl.pallas_call(functools.partial(body), name=name, grid=grid, in_specs=in_specs,
                                   out_specs=out_specs, out_shape=out_shape, scratch_shapes=scratch,
                                   compiler_params=_params(len(grid)))(*args))
    n_in, n_out, n_scr, n_ci, n_co = len(in_specs), len(out_specs), len(scratch), len(comm.ins), len(comm.outs)

    def kern(*refs):
        ins, c_ins, refs = refs[:n_in], refs[n_in:n_in + n_ci], refs[n_in + n_ci:]
        outs, c_outs, refs = refs[:n_out], refs[n_out:n_out + n_co], refs[n_out + n_co:]
        scr, sems = refs[:n_scr], refs[n_scr:]
        ids = [pl.program_id(ax) for ax in range(len(grid))]
        first = functools.reduce(jnp.logical_and, [i == 0 for i in ids])
        last = functools.reduce(jnp.logical_and, [i == g - 1 for i, g in zip(ids, grid)])

        @pl.when(first)
        def _():
            comm.start(c_ins, c_outs, sems)
        body(*ins, *outs, *scr)

        @pl.when(last)
        def _():
            comm.finish(c_ins, c_outs, sems)

    res = pl.pallas_call(kern, name=name, grid=grid, in_specs=list(in_specs) + [ANY] * n_ci,
                         out_specs=list(out_specs) + [ANY] * n_co, out_shape=list(out_shape) + comm.outs,
                         scratch_shapes=list(scratch) + comm.sem_shapes(),
                         compiler_params=_params(len(grid)))(*args, *comm.ins)
    plan.landed(name, list(res[n_out:]))
    return list(res[:n_out])


def _mm_body(dims, has_add, *refs):
    if has_add:
        a_ref, b_ref, add_ref, o_ref = refs
        total = _bdot(a_ref[...], b_ref[...], dims) + add_ref[...]
    else:
        a_ref, b_ref, o_ref = refs
        total = _bdot(a_ref[...], b_ref[...], dims)
    o_ref[...] = total.astype(o_ref.dtype)


def _mm_nt_body(j, n, dy_ref, w_ref, o_ref):
    total = _bdot(dy_ref[:, 0:n], w_ref[0], NT)
    for jj in range(1, j):
        total = total + _bdot(dy_ref[:, jj * n:(jj + 1) * n], w_ref[jj], NT)
    o_ref[...] = total.astype(o_ref.dtype)


def _mm_nn(a, w3, *, name, out_dtype=BF16, add=None, tm=1024, tn=None, out3=False, plan=None):
    m, kk = a.shape
    j, _, n = w3.shape
    tm, tn = min(tm, m), n if tn is None else tn
    n_t = n // tn
    in_specs = [pl.BlockSpec((tm, kk), lambda i, jj: (i, 0)),
                pl.BlockSpec((None, kk, tn), lambda i, jj: (jj // n_t, 0, jj % n_t))]
    args = [a, w3]
    if add is not None:
        in_specs.append(pl.BlockSpec((tm, tn), lambda i, jj: (i, jj)))
        args.append(add)
    if out3:
        out_spec = pl.BlockSpec((None, tm, tn), lambda i, jj: (jj // n_t, i, jj % n_t))
        out_shape = jax.ShapeDtypeStruct((j, m, n), out_dtype)
    else:
        out_spec = pl.BlockSpec((tm, tn), lambda i, jj: (i, jj))
        out_shape = jax.ShapeDtypeStruct((m, j * n), out_dtype)
    return _call(
        functools.partial(_mm_body, NN, add is not None), name=name, grid=(m // tm, j * n_t), in_specs=in_specs,
        out_specs=[out_spec], out_shape=[out_shape], scratch=[], args=args, plan=plan)[0]


def _mm_gathering(a, shard, *, name, out3=False, tm=1024):
    m, kk = a.shape
    _, n = shard.shape
    tm = min(tm, m)
    n_i = m // tm

    def body(a_ref, shard_ref, o_ref, w_all, w_vmem, send_sems, recv_sems, copy_sems):
        jj, i = pl.program_id(0), pl.program_id(1)
        x, y, c = _place()
        me, sibling = (x, y, c), (x, y, 1 - c)
        chips = [(1 - x, y), (x, 1 - y), (1 - x, 1 - y)]

        def rows(block):
            return w_all.at[4 * block[0] + 2 * block[1] + block[2]]

        def remote(k, block, to, from_shard=False):
            return pltpu.make_async_remote_copy(
                src_ref=shard_ref if from_shard else rows(block), dst_ref=rows(block), send_sem=send_sems.at[k],
                recv_sem=recv_sems.at[k], device_id=to, device_id_type=MESH)

        def load(src):
            cp = pltpu.make_async_copy(src, w_vmem, copy_sems.at[1])
            cp.start()
            cp.wait()

        own = [remote(0, me, sibling, True)] + [remote(1 + j, me, (*chip, c), True) for j, chip in enumerate(chips)]
        passed = [remote(4 + j, (*chip, c), sibling) for j, chip in enumerate(chips)]
        local = pltpu.make_async_copy(shard_ref, rows(me), copy_sems.at[0])

        @pl.when(jnp.logical_and(i == 0, jj == 0))
        def _():
            local.start()
            for cp in own:
                cp.start()
            load(shard_ref)

        @pl.when(jnp.logical_and(i == 0, jj == 1))
        def _():
            remote(0, sibling, me).wait_recv()
            load(rows(sibling))

        for j, chip in enumerate(chips):
            @pl.when(jnp.logical_and(i == 0, jj == 2 + j))
            def _():
                remote(1 + j, (*chip, c), me).wait_recv()
                passed[j].start()
                load(rows((*chip, c)))

            @pl.when(jnp.logical_and(i == 0, jj == 5 + j))
            def _():
                remote(4 + j, (*chip, 1 - c), me).wait_recv()
                load(rows((*chip, 1 - c)))

        o_ref[...] = _bdot(a_ref[...], w_vmem[...], NN).astype(o_ref.dtype)

        @pl.when(jnp.logical_and(i == n_i - 1, jj == N_DEV - 1))
        def _():
            for cp in own + passed:
                cp.wait_send()
            local.wait()

    def swept(jj):
        x, y, c = _place()
        flips = sum(jnp.where(jj == k, f, 0) for k, f in enumerate(SWEEP_FLIPS))
        return jnp.bitwise_xor(4 * x + 2 * y + c, flips)

    if out3:
        out_spec = pl.BlockSpec((None, tm, n), lambda jj, i: (swept(jj), i, 0))
        out_shape = jax.ShapeDtypeStruct((N_DEV, m, n), BF16)
    else:
        out_spec = pl.BlockSpec((tm, n), lambda jj, i: (i, swept(jj)))
        out_shape = jax.ShapeDtypeStruct((m, N_DEV * n), BF16)
    return pl.pallas_call(
        body, name=name, grid=(N_DEV, n_i),
        in_specs=[pl.BlockSpec((tm, kk), lambda jj, i: (i, 0)), ANY], out_specs=[out_spec, ANY],
        scratch_shapes=[pltpu.VMEM((kk, n), shard.dtype), pltpu.SemaphoreType.DMA((7,)),
                        pltpu.SemaphoreType.DMA((7,)), pltpu.SemaphoreType.DMA((2,))],
        out_shape=[out_shape, jax.ShapeDtypeStruct((N_DEV, kk, n), shard.dtype)],
        compiler_params=_params(2))(a, shard)


SWEEP_FLIPS = (0b000, 0b001, 0b100, 0b010, 0b110, 0b101, 0b011, 0b111)


def _mm_nn_a3(a3, w3, add, *, name, tm=512):
    j, m, n = a3.shape
    nn = w3.shape[2]
    tm = min(tm, m)

    def body(a_ref, w_ref, add_ref, o_ref):
        total = add_ref[...]
        for jj in range(j):
            total = total + _bdot(a_ref[jj], w_ref[jj], NN)
        o_ref[...] = total

    return _call(body, name=name, grid=(m // tm,),
                 in_specs=[pl.BlockSpec((j, tm, n), lambda i: (0, i, 0)), pl.BlockSpec(w3.shape, lambda i: (0, 0, 0)),
                           pl.BlockSpec((tm, nn), lambda i: (i, 0))],
                 out_specs=[pl.BlockSpec((tm, nn), lambda i: (i, 0))],
                 out_shape=[jax.ShapeDtypeStruct((m, nn), F32)], scratch=[], args=[a3, w3, add])[0]


def _mm_nt_out3(dy, w3, *, name, tm=1024):
    m, nn = dy.shape
    j, n, _ = w3.shape
    tm = min(tm, m)
    return _call(functools.partial(_mm_body, NT, False), name=name, grid=(m // tm, j),
                 in_specs=[pl.BlockSpec((tm, nn), lambda i, jj: (i, 0)),
                           pl.BlockSpec((None, n, nn), lambda i, jj: (jj, 0, 0))],
                 out_specs=[pl.BlockSpec((None, tm, n), lambda i, jj: (jj, i, 0))],
                 out_shape=[jax.ShapeDtypeStruct((j, m, n), BF16)], scratch=[], args=[dy, w3])[0]


def _mm_nt_dy3(dy3, w3, *, name, out_dtype=F32, tm=512, plan=None):
    j, m, n = dy3.shape
    kk = w3.shape[1]
    tm = min(tm, m)

    def body(dy_ref, w_ref, o_ref):
        total = _bdot(dy_ref[0], w_ref[0], NT)
        for jj in range(1, j):
            total = total + _bdot(dy_ref[jj], w_ref[jj], NT)
        o_ref[...] = total.astype(o_ref.dtype)

    return _call(body, name=name, grid=(m // tm,),
                 in_specs=[pl.BlockSpec((j, tm, n), lambda i: (0, i, 0)), pl.BlockSpec(w3.shape, lambda i: (0, 0, 0))],
                 out_specs=[pl.BlockSpec((tm, kk), lambda i: (i, 0))],
                 out_shape=[jax.ShapeDtypeStruct((m, kk), out_dtype)], scratch=[], args=[dy3, w3], plan=plan)[0]


def _mm_tn_a3(a3, dy, *, name):
    j, t, n = a3.shape
    nn = dy.shape[1]
    return _call(functools.partial(_mm_body, TN, False), name=name, grid=(j,),
                 in_specs=[pl.BlockSpec((None, t, n), lambda jj: (jj, 0, 0)), pl.BlockSpec((t, nn), lambda jj: (0, 0))],
                 out_specs=[pl.BlockSpec((None, n, nn), lambda jj: (jj, 0, 0))],
                 out_shape=[jax.ShapeDtypeStruct((j, n, nn), BF16)], scratch=[], args=[a3, dy])[0]


def _mm_tn_dy3(a, dy3, *, name, tm=512):
    t, kk = a.shape
    j, _, n = dy3.shape
    tm = min(tm, kk)
    return _call(functools.partial(_mm_body, TN, False), name=name, grid=(kk // tm, j),
                 in_specs=[pl.BlockSpec((t, tm), lambda i, jj: (0, i)), pl.BlockSpec((None, t, n), lambda i, jj: (jj, 0, 0))],
                 out_specs=[pl.BlockSpec((None, tm, n), lambda i, jj: (jj, i, 0))],
                 out_shape=[jax.ShapeDtypeStruct((j, kk, n), BF16)], scratch=[], args=[a, dy3])[0]


def _mm_nt(dy, w3, *, name, out_dtype=BF16, tm=512, tn=1024, plan=None):
    m = dy.shape[0]
    j, kk, n = w3.shape
    tm, tn = min(tm, m), min(tn, kk)
    return _call(
        functools.partial(_mm_nt_body, j, n), name=name,
        grid=(m // tm, kk // tn),
        in_specs=[pl.BlockSpec((tm, j * n), lambda i, q: (i, 0)),
                  pl.BlockSpec((j, tn, n), lambda i, q: (0, q, 0))],
        out_specs=[pl.BlockSpec((tm, tn), lambda i, q: (i, q))],
        out_shape=[jax.ShapeDtypeStruct((m, kk), out_dtype)], scratch=[], args=[dy, w3], plan=plan)[0]


def _mm_tn(a, dy, n, *, name, out_dtype=BF16, tm=512, tn=None, plan=None):
    t, kk = a.shape
    j = dy.shape[1] // n
    tm, tn = min(tm, kk), n if tn is None else tn
    n_t = n // tn
    return _call(
        functools.partial(_mm_body, TN, False), name=name,
        grid=(kk // tm, j * n_t),
        in_specs=[pl.BlockSpec((t, tm), lambda i, jj: (0, i)),
                  pl.BlockSpec((t, tn), lambda i, jj: (0, jj))],
        out_specs=[pl.BlockSpec((None, tm, tn), lambda i, jj: (jj // n_t, i, jj % n_t))],
        out_shape=[jax.ShapeDtypeStruct((j, kk, n), out_dtype)], scratch=[], args=[a, dy], plan=plan)[0]


def _rows(body, ins, outs, *, n_rows, tm, name, plan=None):
    tm = min(tm, n_rows)
    n_steps = n_rows // tm
    in_specs, args = [], []
    for arr, kind, width, block in ins:
        if kind == "row":
            in_specs.append(pl.BlockSpec((tm, width), functools.partial(lambda i, b: (i, b), b=block)))
        elif kind == "prev":
            in_specs.append(pl.BlockSpec((tm, width), functools.partial(lambda i, b: (jnp.maximum(i - 1, 0), b), b=block)))
        elif kind == "next":
            in_specs.append(pl.BlockSpec((tm, width), functools.partial(lambda i, b: (jnp.minimum(i + 1, n_steps - 1), b), b=block)))
        else:
            in_specs.append(pl.BlockSpec(arr.shape, functools.partial(lambda i, nd: (0,) * nd, nd=arr.ndim)))
        args.append(arr)
    out_specs, out_shape = [], []
    for shape, dtype, kind in outs:
        if kind == "row":
            out_specs.append(pl.BlockSpec((tm, shape[1]), lambda i: (i, 0)))
        else:
            out_specs.append(pl.BlockSpec(shape, functools.partial(lambda i, nd: (0,) * nd, nd=len(shape))))
        out_shape.append(jax.ShapeDtypeStruct(shape, dtype))

    def kern(*refs):
        body(pl.program_id(0), n_steps, *refs)

    return _call(kern, name=name, grid=(n_steps,), in_specs=in_specs, out_specs=out_specs, out_shape=out_shape,
                 scratch=[], args=args, plan=plan)


def _acc_rows(i, ref, value):
    @pl.when(i == 0)
    def _():
        ref[...] = jnp.zeros_like(ref)
    ref[...] += jnp.broadcast_to(value, ref.shape)


def _rms_fwd(x, g, *, name, tm=512):
    s, d = x.shape

    def body(i, n, x_ref, g_ref, h_ref):
        xv = x_ref[...]
        r = lax.rsqrt(jnp.mean(xv * xv, axis=-1, keepdims=True) + NORM_EPS)
        h_ref[...] = (xv * r * g_ref[...]).astype(BF16)

    return _rows(body, [(x, "row", d, 0), (g, "full", 0, 0)], [((s, d), BF16, "row")], n_rows=s, tm=tm, name=name)[0]


def _rms_bwd(x, g, dh, dres, *, name, tm=512, plan=None):
    s, d = x.shape

    def body(i, n, x_ref, g_ref, dh_ref, *rest):
        if dres is None:
            dx_ref, dxb_ref, dg_ref = rest
        else:
            dres_ref, dx_ref, dxb_ref, dg_ref = rest
        xv = x_ref[...]
        r = lax.rsqrt(jnp.mean(xv * xv, axis=-1, keepdims=True) + NORM_EPS)
        xhat = xv * r
        dhv = dh_ref[...].astype(F32)
        dxhat = dhv * g_ref[...]
        dx = r * (dxhat - xhat * jnp.mean(dxhat * xhat, axis=-1, keepdims=True))
        if dres is not None:
            dx = dx + dres_ref[...]
        dx_ref[...] = dx
        dxb_ref[...] = dx.astype(BF16)
        _acc_rows(i, dg_ref, jnp.sum(dhv * xhat, axis=0, keepdims=True))

    ins = [(x, "row", d, 0), (g, "full", 0, 0), (dh, "row", d, 0)]
    if dres is not None:
        ins.append((dres, "row", d, 0))
    return _rows(body, ins, [((s, d), F32, "row"), ((s, d), BF16, "row"), ((8, d), F32, "acc")],
                 n_rows=s, tm=tm, name=name, plan=plan)


def _loss_bwd(x, g, target, *, name, tm=512):
    s, d = x.shape

    def body(i, n, x_ref, g_ref, t_ref, dx_ref, dxb_ref, dg_ref, loss_ref):
        xv = x_ref[...]
        gv = g_ref[...]
        r = lax.rsqrt(jnp.mean(xv * xv, axis=-1, keepdims=True) + NORM_EPS)
        xhat = xv * r
        err = xhat * gv - t_ref[...]
        part = 0.5 * jnp.sum(jnp.mean(err * err, axis=-1, keepdims=True), axis=0, keepdims=True)
        dy = err * (1.0 / d)
        dxhat = dy * gv
        dx = r * (dxhat - xhat * jnp.mean(dxhat * xhat, axis=-1, keepdims=True))
        dx_ref[...] = dx
        dxb_ref[...] = dx.astype(BF16)
        _acc_rows(i, dg_ref, jnp.sum(dy * xhat, axis=0, keepdims=True))
        _acc_rows(i, loss_ref, part)

    return _rows(body, [(x, "row", d, 0), (g, "full", 0, 0), (target, "row", d, 0)],
                 [((s, d), F32, "row"), ((s, d), BF16, "row"), ((8, d), F32, "acc"), ((8, LANES), F32, "acc")],
                 n_rows=s, tm=tm, name=name)


def _sigmoid(v):
    return 1.0 / (1.0 + jnp.exp(-v))


def _swiglu_fwd(gu3, *, name, tm=1024):
    j2, s, n = gu3.shape
    j = j2 // 2
    tm = min(tm, s)

    def body(gu_ref, act_ref):
        gate = gu_ref[0].astype(F32)
        up = gu_ref[1].astype(F32)
        act_ref[...] = (gate * _sigmoid(gate) * up).astype(BF16)

    return _call(body, name=name, grid=(j, s // tm),
                 in_specs=[pl.BlockSpec((2, None, tm, n), lambda jj, i: (0, jj, i, 0))],
                 out_specs=[pl.BlockSpec((None, tm, n), lambda jj, i: (jj, i, 0))],
                 out_shape=[jax.ShapeDtypeStruct((j, s, n), BF16)], scratch=[], args=[gu3.reshape(2, j, s, n)])[0]


def _swiglu_bwd(dact3, gu3, *, name, tm=1024):
    j2, s, n = gu3.shape
    j = j2 // 2
    tm = min(tm, s)

    def body(dact_ref, gu_ref, dgu_ref):
        gate = gu_ref[0].astype(F32)
        up = gu_ref[1].astype(F32)
        da = dact_ref[...].astype(F32)
        sg = _sigmoid(gate)
        silu = gate * sg
        dgu_ref[0] = (da * up * (sg + silu * (1.0 - sg))).astype(BF16)
        dgu_ref[1] = (da * silu).astype(BF16)

    out = _call(body, name=name, grid=(j, s // tm),
                in_specs=[pl.BlockSpec((None, tm, n), lambda jj, i: (jj, i, 0)),
                          pl.BlockSpec((2, None, tm, n), lambda jj, i: (0, jj, i, 0))],
                out_specs=[pl.BlockSpec((2, None, tm, n), lambda jj, i: (0, jj, i, 0))],
                out_shape=[jax.ShapeDtypeStruct((2, j, s, n), BF16)], scratch=[],
                args=[dact3, gu3.reshape(2, j, s, n)])[0]
    return out.reshape(j2, s, n)


def _gates_fwd(br_a, br_b, proj, *, name, tm=512):
    s, d = br_a.shape

    def body(i, n, a_ref, b_ref, ga_ref, gb_ref, o_ref):
        o_ref[...] = (_sigmoid(ga_ref[...].astype(F32)) * a_ref[...].astype(F32)
                      + _sigmoid(gb_ref[...].astype(F32)) * b_ref[...].astype(F32)).astype(BF16)

    return _rows(body, [(br_a, "row", d, 0), (br_b, "row", d, 0), (proj, "row", d, 3), (proj, "row", d, 4)],
                 [((s, d), BF16, "row")], n_rows=s, tm=tm, name=name)[0]


def _gates_bwd(dmerged, br_a, br_b, proj, *, name, tm=512):
    s, d = br_a.shape

    def body(i, n, dm_ref, a_ref, b_ref, ga_ref, gb_ref, da_ref, db_ref, dg_ref):
        dm = dm_ref[...].astype(F32)
        sa = _sigmoid(ga_ref[...].astype(F32))
        sb = _sigmoid(gb_ref[...].astype(F32))
        da_ref[...] = (dm * sa).astype(BF16)
        db_ref[...] = (dm * sb).astype(BF16)
        dg_ref[:, :d] = (dm * a_ref[...].astype(F32) * sa * (1.0 - sa)).astype(BF16)
        dg_ref[:, d:] = (dm * b_ref[...].astype(F32) * sb * (1.0 - sb)).astype(BF16)

    return _rows(body, [(dmerged, "row", d, 0), (br_a, "row", d, 0), (br_b, "row", d, 0),
                        (proj, "row", d, 3), (proj, "row", d, 4)],
                 [((s, d), BF16, "row"), ((s, d), BF16, "row"), ((s, 2 * d), BF16, "row")],
                 n_rows=s, tm=tm, name=name)


def _shift_down(cur, prev, k, first):
    row = lax.broadcasted_iota(jnp.int32, cur.shape, 0)
    out = jnp.where(row >= k, pltpu.roll(cur, k, 0), pltpu.roll(prev, k, 0))
    return jnp.where(jnp.logical_and(first, row < k), 0.0, out)


def _shift_up(cur, nxt, k, last):
    tm = cur.shape[0]
    row = lax.broadcasted_iota(jnp.int32, cur.shape, 0)
    out = jnp.where(row < tm - k, pltpu.roll(cur, tm - k, 0), pltpu.roll(nxt, tm - k, 0))
    return jnp.where(jnp.logical_and(last, row >= tm - k), 0.0, out)


def _conv_fwd(proj, conv_w, *, name, tm=512):
    s = proj.shape[0]
    c = CONV_WIDTH

    def body(i, n, u_ref, gb_ref, gc_ref, up_ref, gcp_ref, w_ref, y_ref):
        cu = gc_ref[...].astype(F32) * u_ref[...].astype(F32)
        cup = gcp_ref[...].astype(F32) * up_ref[...].astype(F32)
        first = i == 0
        y = (w_ref[0:1, :] * _shift_down(cu, cup, 2, first) + w_ref[1:2, :] * _shift_down(cu, cup, 1, first)
             + w_ref[2:3, :] * cu)
        y_ref[...] = (gb_ref[...].astype(F32) * y).astype(BF16)

    return _rows(body, [(proj, "row", c, 3), (proj, "row", c, 4), (proj, "row", c, 5),
                        (proj, "prev", c, 3), (proj, "prev", c, 5), (conv_w, "full", 0, 0)],
                 [((s, c), BF16, "row")], n_rows=s, tm=tm, name=name)[0]


def _conv_bwd(dy_b, proj, conv_w, *, name, tm=512, plan=None):
    s = proj.shape[0]
    c = CONV_WIDTH

    def body(i, n, dy_ref, u_ref, gb_ref, gc_ref, up_ref, gcp_ref, dyn_ref, gbn_ref, w_ref, d_ref, dw_ref):
        first, last = i == 0, i == n - 1
        u = u_ref[...].astype(F32)
        gb = gb_ref[...].astype(F32)
        gc = gc_ref[...].astype(F32)
        cu = gc * u
        cup = gcp_ref[...].astype(F32) * up_ref[...].astype(F32)
        cu1 = _shift_down(cu, cup, 1, first)
        cu2 = _shift_down(cu, cup, 2, first)
        conv = w_ref[0:1, :] * cu2 + w_ref[1:2, :] * cu1 + w_ref[2:3, :] * cu
        dy = dy_ref[...].astype(F32)
        dyc = dy * gb
        dycn = dyn_ref[...].astype(F32) * gbn_ref[...].astype(F32)
        dcu = (w_ref[2:3, :] * dyc + w_ref[1:2, :] * _shift_up(dyc, dycn, 1, last)
               + w_ref[0:1, :] * _shift_up(dyc, dycn, 2, last))
        d_ref[:, 0:c] = (dcu * gc).astype(BF16)
        d_ref[:, c:2 * c] = (dy * conv).astype(BF16)
        d_ref[:, 2 * c:3 * c] = (dcu * u).astype(BF16)
        row = lax.broadcasted_iota(jnp.int32, (8, c), 0)
        dw = (jnp.where(row == 0, jnp.sum(dyc * cu2, axis=0, keepdims=True), 0.0)
              + jnp.where(row == 1, jnp.sum(dyc * cu1, axis=0, keepdims=True), 0.0)
              + jnp.where(row == 2, jnp.sum(dyc * cu, axis=0, keepdims=True), 0.0))

        @pl.when(first)
        def _():
            dw_ref[...] = jnp.zeros_like(dw_ref)
        dw_ref[...] += dw

    return _rows(body, [(dy_b, "row", c, 0), (proj, "row", c, 3), (proj, "row", c, 4), (proj, "row", c, 5),
                        (proj, "prev", c, 3), (proj, "prev", c, 5), (dy_b, "next", c, 0), (proj, "next", c, 4),
                        (conv_w, "full", 0, 0)],
                 [((s, 3 * c), BF16, "row"), ((8, c), F32, "acc")], n_rows=s, tm=tm, name=name, plan=plan)


def _mem_probs(q, k, scale):
    sc = _bdot(q, k, NT) * scale
    sc = sc - jnp.max(sc, axis=-1, keepdims=True)
    p = jnp.exp(sc)
    return p / jnp.sum(p, axis=-1, keepdims=True)


def _memattn_fwd(qm, kv, *, name, tm=512):
    s, d = qm.shape
    hd = d // MEM_HEADS
    scale = 1.0 / math.sqrt(hd)

    def body(i, n, q_ref, kv_ref, o_ref):
        for h in range(MEM_HEADS):
            cols = slice(h * hd, (h + 1) * hd)
            p = _mem_probs(q_ref[:, cols], kv_ref[:, cols], scale)
            o_ref[:, cols] = _bdot(p, kv_ref[:, d + h * hd:d + (h + 1) * hd], NN).astype(BF16)

    return _rows(body, [(qm, "row", d, 0), (kv, "full", 0, 0)], [((s, d), BF16, "row")], n_rows=s, tm=tm, name=name)[0]


def _memattn_bwd(dom, qm, kv, *, name, tm=512):
    s, d = qm.shape
    hd = d // MEM_HEADS
    scale = 1.0 / math.sqrt(hd)

    def body(i, n, do_ref, q_ref, kv_ref, dq_ref, dkv_ref):
        @pl.when(i == 0)
        def _():
            dkv_ref[...] = jnp.zeros_like(dkv_ref)
        for h in range(MEM_HEADS):
            cols = slice(h * hd, (h + 1) * hd)
            vcols = slice(d + h * hd, d + (h + 1) * hd)
            q, k, v, do = q_ref[:, cols], kv_ref[:, cols], kv_ref[:, vcols], do_ref[:, cols]
            p = _mem_probs(q, k, scale)
            dp = _bdot(do, v, NT)
            ds = p * (dp - jnp.sum(dp * p, axis=-1, keepdims=True)) * scale
            dq_ref[:, cols] = _bdot(ds, k, NN).astype(BF16)
            dkv_ref[:, cols] += _bdot(ds, q, TN)
            dkv_ref[:, vcols] += _bdot(p, do, TN)

    return _rows(body, [(dom, "row", d, 0), (qm, "row", d, 0), (kv, "full", 0, 0)],
                 [((s, d), BF16, "row"), (kv.shape, F32, "acc")], n_rows=s, tm=tm, name=name)


def _sb_consts(t):
    row = lax.broadcasted_iota(jnp.int32, (t, t), 0)
    col = lax.broadcasted_iota(jnp.int32, (t, t), 1)
    lane = lax.broadcasted_iota(jnp.int32, (t, LANES), 1)
    return row, col, lane < SB_HEAD_DIM


def _log_fail(z):
    return jnp.minimum(-z, 0.0) - jnp.log(1.0 + jnp.exp(-jnp.abs(z)))


def _tri_sum(v, tri):
    hi = v.astype(BF16)
    lo = (v - hi.astype(F32)).astype(BF16)
    return _bdot(hi, tri, NN) + _bdot(lo, tri, NN)


def _sb_fwd(proj, *, name, plan=None):
    s = proj.shape[0]
    t = SB_TILE
    n_q = s // t
    scale = 1.0 / math.sqrt(SB_HEAD_DIM)
    k_blk, v_blk = SB_WIDTH // LANES, 2 * SB_WIDTH // LANES

    def body(q_ref, k_ref, v_ref, o_ref, c_ref, first_ref, acc_ref):
        i = pl.program_id(1)
        row, col, head0 = _sb_consts(t)
        later = (row > col).astype(BF16)
        valid = col < row
        qs = q_ref[...] * scale
        q2 = (jnp.where(head0, qs, 0), jnp.where(head0, 0, qs))

        def tile(kb, carry, diag):
            kt = k_ref[pl.ds(pl.multiple_of(kb * t, t), t), :]
            vt = v_ref[pl.ds(pl.multiple_of(kb * t, t), t), :]
            heads = range(2)
            z = [_bdot(q2[h], kt, NT) for h in heads]
            lf = [_log_fail(z[h]) for h in heads]
            if diag:
                lf = [jnp.where(valid, lf[h], 0.0) for h in heads]
            cum = [_tri_sum(lf[h], later) for h in heads]
            w = [jnp.exp(z[h] + lf[h] + cum[h] + carry[h]) for h in heads]
            if diag:
                w = [jnp.where(valid, w[h], 0.0) for h in heads]
            for h in heads:
                acc_ref[h] += _bdot(w[h], vt, NN)
            return tuple(carry[h] + cum[h][:, 0:1] + lf[h][:, 0:1] for h in heads)

        acc_ref[...] = jnp.zeros_like(acc_ref)
        zero = jnp.zeros((t, 1), F32)

        def alive(carry):
            return (jnp.maximum(jnp.max(carry[0]), jnp.max(carry[1])) > -SB_DEAD).astype(jnp.int32)

        def step(state):
            kb, _, c0, c1 = state
            new = tile(kb, (c0, c1), False)
            return kb - 1, alive(new), new[0], new[1]

        carry = tile(i, (zero, zero), True)
        kb, _, c0, c1 = lax.while_loop(lambda st: jnp.logical_and(st[0] >= 0, st[1] > 0), step,
                                       (i - 1, alive(carry), carry[0], carry[1]))
        o_ref[...] = jnp.where(head0, acc_ref[0], acc_ref[1]).astype(BF16)
        c_ref[...] = jnp.where(lax.broadcasted_iota(jnp.int32, (t, 2), 1) == 0, c0, c1)
        first_ref[pl.program_id(0), i] = (kb + 1).astype(F32)

    return _call(
        body, name=name, grid=(SB_HEADS // 2, n_q),
        in_specs=[pl.BlockSpec((t, LANES), lambda p, i: (i, p)),
                  pl.BlockSpec((s, LANES), lambda p, i: (0, k_blk + p)),
                  pl.BlockSpec((s, LANES), lambda p, i: (0, v_blk + p))],
        out_specs=[pl.BlockSpec((t, LANES), lambda p, i: (i, p)),
                   pl.BlockSpec((None, t, 2), lambda p, i: (p, i, 0)),
                   pl.BlockSpec(memory_space=pltpu.SMEM)],
        out_shape=[jax.ShapeDtypeStruct((s, SB_WIDTH), BF16), jax.ShapeDtypeStruct((SB_HEADS // 2, s, 2), F32),
                   jax.ShapeDtypeStruct((SB_HEADS // 2, n_q), F32)],
        scratch=[pltpu.VMEM((2, t, LANES), F32)], args=[proj, proj, proj], plan=plan)


def _sb_bwd(proj, do_a, ctot, first, *, name, plan=None):
    s = proj.shape[0]
    t = SB_TILE
    n_q = s // t
    scale = 1.0 / math.sqrt(SB_HEAD_DIM)
    k_blk, v_blk = SB_WIDTH // LANES, 2 * SB_WIDTH // LANES

    def body(q_ref, k_ref, v_ref, do_ref, c_ref, first_ref, dq_ref, dk_ref, dv_ref, dq_acc, dk_acc, dv_acc):
        i = pl.program_id(1)
        kb0 = jnp.clip(first_ref[pl.program_id(0), i].astype(jnp.int32), 0, i)
        row, col, head0 = _sb_consts(t)
        upto = (row <= col).astype(BF16)
        before = (row < col).astype(BF16)
        valid = col < row
        qs = q_ref[...] * scale
        q2 = (jnp.where(head0, qs, 0), jnp.where(head0, 0, qs))
        do = do_ref[...]
        do2 = (jnp.where(head0, do, 0), jnp.where(head0, 0, do))
        ctot2 = (c_ref[:, 0:1], c_ref[:, 1:2])

        @pl.when(i == 0)
        def _():
            dk_acc[...] = jnp.zeros_like(dk_acc)
            dv_acc[...] = jnp.zeros_like(dv_acc)
        dq_acc[...] = jnp.zeros_like(dq_acc)

        def tile(kb, carry, diag):
            rows = pl.ds(pl.multiple_of(kb * t, t), t)
            kt = k_ref[rows, :]
            vt = v_ref[rows, :]
            heads = range(2)
            lf_before, g_before = carry[0::2], carry[1::2]
            z = [_bdot(q2[h], kt, NT) for h in heads]
            dw = [_bdot(do2[h], vt, NT) for h in heads]
            lf = [_log_fail(z[h]) for h in heads]
            if diag:
                lf = [jnp.where(valid, lf[h], 0.0) for h in heads]
            cum = [_tri_sum(lf[h], upto) for h in heads]
            beta = [jnp.exp(z[h] + lf[h]) for h in heads]
            w = [beta[h] * jnp.exp(ctot2[h] - lf_before[h] - cum[h]) for h in heads]
            if diag:
                w = [jnp.where(valid, w[h], 0.0) for h in heads]
            g = [w[h] * dw[h] for h in heads]
            g_sum = [g_before[h] + _bdot(g[h], before, NN) for h in heads]
            for h in heads:
                dv_acc[rows, :] += _bdot(w[h], do2[h], TN)
            dz = [g[h] * jnp.exp(lf[h]) - beta[h] * g_sum[h] for h in heads]
            if diag:
                dz = [jnp.where(valid, dz[h], 0.0) for h in heads]
            for h in heads:
                dq_acc[h] += _bdot(dz[h], kt, NN)
                dk_acc[rows, :] += _bdot(dz[h], q2[h], TN)
            t_last = slice(t - 1, t)
            new = []
            for h in heads:
                new += [lf_before[h] + cum[h][:, t_last], g_sum[h][:, t_last] + g[h][:, t_last]]
            return tuple(new)

        zero = jnp.zeros((t, 1), F32)
        carry = lax.fori_loop(kb0, i, lambda n, c: tile(n, c, False), (zero,) * 4)
        tile(i, carry, True)
        dq_ref[...] = (jnp.where(head0, dq_acc[0], dq_acc[1]) * scale).astype(BF16)

        @pl.when(i == n_q - 1)
        def _():
            dk_ref[...] = dk_acc[...].astype(BF16)
            dv_ref[...] = dv_acc[...].astype(BF16)

    outs = _call(
        body, name=name, grid=(SB_HEADS // 2, n_q),
        in_specs=[pl.BlockSpec((t, LANES), lambda p, i: (i, p)),
                  pl.BlockSpec((s, LANES), lambda p, i: (0, k_blk + p)),
                  pl.BlockSpec((s, LANES), lambda p, i: (0, v_blk + p)),
                  pl.BlockSpec((t, LANES), lambda p, i: (i, p)),
                  pl.BlockSpec((None, t, 2), lambda p, i: (p, i, 0)),
                  pl.BlockSpec(memory_space=pltpu.SMEM)],
        out_specs=[pl.BlockSpec((t, LANES), lambda p, i: (i, p)),
                   pl.BlockSpec((s, LANES), lambda p, i: (0, p)),
                   pl.BlockSpec((s, LANES), lambda p, i: (0, p))],
        out_shape=[jax.ShapeDtypeStruct((s, SB_WIDTH), BF16)] * 3,
        scratch=[pltpu.VMEM((2, t, LANES), F32), pltpu.VMEM((s, LANES), F32), pltpu.VMEM((s, LANES), F32)],
        args=[proj, proj, proj, do_a, ctot, first], plan=plan)
    return jnp.concatenate(outs, axis=1)


def _mm_gathered(a, key, plan, *, name, out3=False):
    src = plan.gathering(key)
    if src is None:
        return _mm_nn(a, plan.weight(key), name=name, out3=out3)
    out, w_all = _mm_gathering(a, src, name=name, out3=out3)
    plan.set_weight(key, w_all)
    return out


def _local_step(x, mem, target, gains, plan):
    g_mix, g_memq, g_memkv, g_ffn, g_fin = gains
    d = x.shape[1]

    h0 = _rms_fwd(x, g_mix, name="rms_mix")
    proj = _mm_gathered(h0, "in", plan, name="mm_in")
    w_in = plan.weight("in")
    o_a, ctot, first = _sb_fwd(proj, name="sb_fwd", plan=plan)
    conv_w = plan.weight("conv")
    y_b = _conv_fwd(proj, conv_w, name="conv_fwd")
    w_a, w_b, w_mix = plan.weight("a"), plan.weight("b"), plan.weight("mix")
    br_a = _mm_nn(o_a, w_a, name="mm_branch_a")
    br_b = _mm_nn(y_b, w_b, name="mm_branch_b")
    merged = _gates_fwd(br_a, br_b, proj, name="gates_fwd")
    x1 = _mm_nn(merged, w_mix, name="mm_mix", out_dtype=F32, add=x)
    hq = _rms_fwd(x1, g_memq, name="rms_memq")
    w_mq, w_kv, w_mo = plan.weight("mq"), plan.weight("kv"), plan.weight("mo")
    qm = _mm_nn(hq, w_mq, name="mm_memq")
    mn = _rms_fwd(mem, g_memkv, name="rms_memkv")
    kv = _mm_nn(mn, w_kv, name="mm_memkv")
    om = _memattn_fwd(qm, kv, name="memattn_fwd")
    x2 = _mm_nn(om, w_mo, name="mm_memo", out_dtype=F32, add=x1)
    hf = _rms_fwd(x2, g_ffn, name="rms_ffn")
    gu = _mm_gathered(hf, "fi", plan, name="mm_ffn_in", out3=True)
    w_fi, w_fo = plan.weight("fi"), plan.weight("fo")
    act = _swiglu_fwd(gu, name="swiglu_fwd")
    x3 = _mm_nn_a3(act, w_fo, x2, name="mm_ffn_out")

    dx3, dx3b, dg_fin, loss = _loss_bwd(x3, g_fin, target, name="loss_bwd")

    plan.grad("fo", _mm_tn_a3(act, dx3b, name="mm_d_w_ffn_out"))
    dact = _mm_nt_out3(dx3b, w_fo, name="mm_d_act")
    dgu = _swiglu_bwd(dact, gu, name="swiglu_bwd")
    plan.grad("fi", _mm_tn_dy3(hf, dgu, name="mm_d_w_ffn_in"))
    dhf = _mm_nt_dy3(dgu, w_fi, name="mm_d_hf", plan=plan)
    dx2, dx2b, dg_ffn = _rms_bwd(x2, g_ffn, dhf, dx3, name="rms_ffn_bwd")

    plan.grad("mo", _mm_tn(om, dx2b, d, name="mm_d_w_memo"))
    dom = _mm_nt(dx2b, w_mo, name="mm_d_om")
    dqm, dkv = _memattn_bwd(dom, qm, kv, name="memattn_bwd")
    plan.grad("mq", _mm_tn(hq, dqm, d, name="mm_d_w_memq"))
    dhq = _mm_nt(dqm, w_mq, name="mm_d_hq", out_dtype=F32)
    dx1, dx1b, dg_memq = _rms_bwd(x1, g_memq, dhq, dx2, name="rms_memq_bwd")
    plan.grad("kv", _mm_tn(mn, dkv, w_kv.shape[2], name="mm_d_w_memkv"))
    dmn = _mm_nt(dkv, w_kv, name="mm_d_mn", out_dtype=F32)
    _, _, dg_memkv = _rms_bwd(mem, g_memkv, dmn, None, name="rms_memkv_bwd")

    plan.grad("mix", _mm_tn(merged, dx1b, d, name="mm_d_w_mix"))
    dmerged = _mm_nt(dx1b, w_mix, name="mm_d_merged", plan=plan)
    dbr_a, dbr_b, dgab = _gates_bwd(dmerged, br_a, br_b, proj, name="gates_bwd")
    plan.grad("a", _mm_tn(o_a, dbr_a, d, name="mm_d_w_branch_a"))
    do_a = _mm_nt(dbr_a, w_a, name="mm_d_o_a")
    plan.grad("b", _mm_tn(y_b, dbr_b, d, name="mm_d_w_branch_b"))
    dy_b = _mm_nt(dbr_b, w_b, name="mm_d_y_b")
    dconv, dconv_w = _conv_bwd(dy_b, proj, conv_w, name="conv_bwd", plan=plan)
    dqkv = _sb_bwd(proj, do_a, ctot, first, name="sb_bwd", plan=plan)
    dproj = jnp.concatenate([dqkv, dconv, dgab], axis=1)
    plan.grad("in", _mm_tn(h0, dproj, w_in.shape[2], name="mm_d_w_in"))
    dh0 = _mm_nt(dproj, w_in, name="mm_d_h0", out_dtype=F32, plan=plan)
    dx0, _, dg_mix = _rms_bwd(x, g_mix, dh0, dx1, name="rms_mix_bwd", plan=plan)

    return dx0, (dg_mix, dg_memq, dg_memkv, dg_ffn, dg_fin, dconv_w, loss)


def _row_tile(a, target=512):
    tm = min(a, target)
    while a % tm:
        tm -= 8
    return tm


def _sum_with_sibling(part, recv, core, *, name):
    _, a, b = part.shape
    tm = _row_tile(a)

    def body(core_ref, p_ref, r_ref, o_ref):
        o_ref[...] = (p_ref[...].astype(F32) + r_ref[...].astype(F32)).astype(o_ref.dtype)

    return pl.pallas_call(
        body, name=name,
        grid_spec=pltpu.PrefetchScalarGridSpec(
            num_scalar_prefetch=1, grid=(N_CHIP, a // tm),
            in_specs=[pl.BlockSpec((None, tm, b), lambda q, i, core_ref: (2 * q + core_ref[0], i, 0)),
                      pl.BlockSpec((None, tm, b), lambda q, i, core_ref: (q, i, 0))],
            out_specs=pl.BlockSpec((None, tm, b), lambda q, i, core_ref: (q, i, 0))),
        out_shape=jax.ShapeDtypeStruct((N_CHIP, a, b), part.dtype), compiler_params=_params(2))(core, part, recv)


def _adam_math(wv, g, m, v):
    m = ADAM_B1 * m + (1.0 - ADAM_B1) * g
    v = ADAM_B2 * v + (1.0 - ADAM_B2) * (g * g)
    m_hat = m / (1.0 - ADAM_B1 ** ADAM_STEP)
    v_hat = v / (1.0 - ADAM_B2 ** ADAM_STEP)
    delta = -ADAM_LR * (m_hat / (jnp.sqrt(v_hat) + ADAM_EPS) + ADAM_WD * wv)
    return delta, m, v


def _adam_sharded(wv, m, v, own, recv, chip, *, name):
    a, b = wv.shape
    tm = _row_tile(a)

    def body(chip_ref, w_ref, m_ref, v_ref, own_ref, recv_ref, g_ref, d_ref, nm_ref, nv_ref):
        g = own_ref[...].astype(F32)
        for j in range(3):
            g = g + recv_ref[j].astype(F32)
        delta, nm, nv = _adam_math(w_ref[...], g, m_ref[...], v_ref[...])
        g_ref[...] = g
        d_ref[...] = delta
        nm_ref[...] = nm
        nv_ref[...] = nv

    tile = pl.BlockSpec((tm, b), lambda i, chip_ref: (i, 0))
    return pl.pallas_call(
        body, name=name,
        grid_spec=pltpu.PrefetchScalarGridSpec(
            num_scalar_prefetch=1, grid=(a // tm,),
            in_specs=[tile, tile, tile,
                      pl.BlockSpec((None, tm, b), lambda i, chip_ref: (chip_ref[0], i, 0)),
                      pl.BlockSpec((3, tm, b), lambda i, chip_ref: (0, i, 0))],
            out_specs=[tile] * 4),
        out_shape=[jax.ShapeDtypeStruct((a, b), F32)] * 4, compiler_params=_params(1))(chip, wv, m, v, own, recv)


def _sum_devices(gathered, *, name):
    _, r, c = gathered.shape

    def body(g_ref, o_ref):
        total = g_ref[0]
        for j in range(1, N_DEV):
            total = total + g_ref[j]
        o_ref[...] = total

    return pl.pallas_call(body, name=name, out_shape=jax.ShapeDtypeStruct((r, c), F32))(gathered)


def _adam_small(wv, g, m, v, *, name):
    def body(w_ref, g_ref, m_ref, v_ref, d_ref, nm_ref, nv_ref):
        delta, nm, nv = _adam_math(w_ref[...], g_ref[...], m_ref[...], v_ref[...])
        d_ref[...] = delta
        nm_ref[...] = nm
        nv_ref[...] = nv

    return pl.pallas_call(body, name=name, out_shape=[jax.ShapeDtypeStruct(wv.shape, F32)] * 3)(wv, g, m, v)


BIG = ("in", "a", "b", "mix", "mq", "kv", "mo", "fi", "fo")
ROW_SHARDED = ("mix", "mq", "mo")
UNSHARDED = ("a", "b")
FFN_GROUPS = 4
SMALL_ROWS = 16


class _Plan:
    FUSED = ("in", "fi")
    GATHER_ON = {"sb_fwd": ("a", "b", "mix", "kv", "mq", "mo", "fo", "conv")}
    SIBLING_ON = {"mm_d_hf": ("fo", "fi"), "mm_d_merged": ("mo", "mq", "kv"), "conv_bwd": ("mix", "a", "b"),
                  "mm_d_h0": ("in",)}
    CHIPS_ON = {"sb_bwd": ("fo", "fi", "mo", "mq", "kv", "mix", "a", "b"), "rms_mix_bwd": ("in",)}

    def __init__(self, shards, core):
        self.shards, self.core = shards, core
        self.w, self.parts, self.chip_sums, self.from_chips = {}, {}, {}, {}

    def gathering(self, k):
        return self.shards[k] if k in self.FUSED else None

    def comm(self, name):
        if name in self.GATHER_ON:
            return _gather_comm([self.shards[k] for k in self.GATHER_ON[name]])
        if name in self.SIBLING_ON:
            return _sibling_comm([self.parts[k] for k in self.SIBLING_ON[name]])
        if name in self.CHIPS_ON:
            return _chips_comm([self.chip_sums[k] for k in self.CHIPS_ON[name]])
        return None

    def landed(self, name, outs):
        if name in self.GATHER_ON:
            for k, o in zip(self.GATHER_ON[name], outs):
                self.set_weight(k, o)
        elif name in self.SIBLING_ON:
            for k, o in zip(self.SIBLING_ON[name], outs):
                self.chip_sums[k] = _sum_with_sibling(self.parts[k], o, self.core, name="sum_with_sibling_" + k)
        else:
            for k, o in zip(self.CHIPS_ON[name], outs):
                self.from_chips[k] = o

    def set_weight(self, k, gathered):
        _, a, b = gathered.shape
        if k in ROW_SHARDED:
            gathered = gathered.reshape(1, N_DEV * a, b)
        elif k in UNSHARDED:
            gathered = jnp.transpose(gathered, (1, 0, 2)).reshape(1, a, N_DEV * b)
        elif k == "fo":
            gathered = gathered.reshape(FFN_GROUPS, N_DEV * a // FFN_GROUPS, b)
        elif k == "conv":
            n_conv = CONV_WIDTH // N_DEV
            gathered = jnp.transpose(gathered[:, :3, :n_conv], (1, 0, 2)).reshape(3, CONV_WIDTH)
        self.w[k] = gathered

    def weight(self, k):
        return self.w[k]

    def grad(self, k, g):
        _, a, b = g.shape
        if k in ROW_SHARDED:
            g = g.reshape(N_DEV, a // N_DEV, b)
        elif k in UNSHARDED:
            g = jnp.transpose(g.reshape(a, N_DEV, b // N_DEV), (1, 0, 2))
        elif k == "fo":
            g = g.reshape(N_DEV, FFN_GROUPS * a // N_DEV, b)
        self.parts[k] = g


def kernel(x, mem, norm_mix, w_in, conv_w, w_branch_a, w_branch_b, w_mix_out, norm_mem_q, norm_mem_kv, w_mem_q, w_mem_kv, w_mem_o, norm_ffn, w_ffn_in, w_ffn_out, norm_final, loss_target, m_norm_mix, m_w_in, m_conv_w, m_w_branch_a, m_w_branch_b, m_w_mix_out, m_norm_mem_q, m_norm_mem_kv, m_w_mem_q, m_w_mem_kv, m_w_mem_o, m_norm_ffn, m_w_ffn_in, m_w_ffn_out, m_norm_final, v_norm_mix, v_w_in, v_conv_w, v_w_branch_a, v_w_branch_b, v_w_mix_out, v_norm_mem_q, v_norm_mem_kv, v_w_mem_q, v_w_mem_kv, v_w_mem_o, v_norm_ffn, v_w_ffn_in, v_w_ffn_out, v_norm_final):
    d = x.shape[-1]
    xi, yi, ci = lax.axis_index("x"), lax.axis_index("y"), lax.axis_index("c")
    core = jnp.reshape(ci, (1,)).astype(jnp.int32)
    chip = jnp.reshape(2 * xi + yi, (1,)).astype(jnp.int32)
    dev = 4 * xi + 2 * yi + ci

    big_w = dict(zip(BIG, (w_in, w_branch_a, w_branch_b, w_mix_out, w_mem_q, w_mem_kv, w_mem_o, w_ffn_in, w_ffn_out)))
    big_m = dict(zip(BIG, (m_w_in, m_w_branch_a, m_w_branch_b, m_w_mix_out, m_w_mem_q, m_w_mem_kv, m_w_mem_o, m_w_ffn_in, m_w_ffn_out)))
    big_v = dict(zip(BIG, (v_w_in, v_w_branch_a, v_w_branch_b, v_w_mix_out, v_w_mem_q, v_w_mem_kv, v_w_mem_o, v_w_ffn_in, v_w_ffn_out)))

    shards = {k: big_w[k][0].astype(BF16) for k in BIG}
    n_conv = conv_w.shape[-1]
    shards["conv"] = jnp.zeros((8, LANES), F32).at[:3, :n_conv].set(conv_w[0])
    plan = _Plan(shards, core)

    gains = (norm_mix, norm_mem_q, norm_mem_kv, norm_ffn, norm_final.reshape(1, d))
    dx0, small = _local_step(x[0], mem[0], loss_target[0], gains, plan)

    grads, deltas, new_m, new_v = {}, {}, {}, {}
    for k in BIG:
        lead = big_w[k].shape
        g, dl, nm, nv = _adam_sharded(big_w[k][0], big_m[k][0], big_v[k][0], plan.chip_sums[k], plan.from_chips[k],
                                      chip, name="adam_" + k)
        grads[k], deltas[k], new_m[k], new_v[k] = (t.reshape(lead) for t in (g, dl, nm, nv))

    dg_mix, dg_memq, dg_memkv, dg_ffn, dg_fin, dconv_w, loss = small
    conv_rows = jnp.zeros((3, d), F32).at[:, :CONV_WIDTH].set(dconv_w[:3])
    block = jnp.concatenate([dg_mix[:1], dg_memq[:1], dg_memkv[:1], dg_ffn[:1], dg_fin[:1], conv_rows,
                             jnp.broadcast_to(loss[:1, :1], (1, d)), jnp.zeros((SMALL_ROWS - 9, d), F32)], axis=0)
    total = _sum_devices(_exchange(_gather_comm([block]), name="gather_small")[0], name="sum_small")
    g_conv = lax.dynamic_slice(total[5:8, :CONV_WIDTH], (0, dev * n_conv), (3, n_conv))
    small_w = [norm_mix, norm_mem_q, norm_mem_kv, norm_ffn, norm_final.reshape(1, d), conv_w[0]]
    small_m = [m_norm_mix, m_norm_mem_q, m_norm_mem_kv, m_norm_ffn, m_norm_final.reshape(1, d), m_conv_w[0]]
    small_v = [v_norm_mix, v_norm_mem_q, v_norm_mem_kv, v_norm_ffn, v_norm_final.reshape(1, d), v_conv_w[0]]
    small_g = [total[0:1], total[1:2], total[2:3], total[3:4], total[4:5], g_conv]
    small_names = ["norm_mix", "norm_mem_q", "norm_mem_kv", "norm_ffn", "norm_final", "conv_w"]
    sg, sd, sm, sv = {}, {}, {}, {}
    for nme, wv, g, m, v in zip(small_names, small_w, small_g, small_m, small_v):
        dl, nm, nv = _adam_small(wv, g, m, v, name="adam_" + nme)
        shape = norm_final.shape if nme == "norm_final" else (conv_w.shape if nme == "conv_w" else wv.shape)
        sg[nme], sd[nme], sm[nme], sv[nme] = (t.reshape(shape) for t in (g, dl, nm, nv))

    def ordered(big, sml):
        return (sml["norm_mix"], big["in"], sml["conv_w"], big["a"], big["b"], big["mix"], sml["norm_mem_q"],
                sml["norm_mem_kv"], big["mq"], big["kv"], big["mo"], sml["norm_ffn"], big["fi"], big["fo"],
                sml["norm_final"])

    loss_out = total[8, 0]
    grad_x = dx0.reshape(x.shape)
    return (loss_out, grad_x, *ordered(grads, sg), *ordered(deltas, sd), *ordered(new_m, sm), *ordered(new_v, sv))
```

```python
import functools
import math

import jax
import jax.numpy as jnp
from jax import lax
from jax.experimental import pallas as pl
from jax.experimental.pallas import tpu as pltpu

F32 = jnp.float32
BF16 = jnp.bfloat16
MESH = pl.DeviceIdType.MESH

N_DEV = 8
N_CHIP = 4
NORM_EPS = 1e-6
SB_HEADS = 8
SB_HEAD_DIM = 64
SB_WIDTH = SB_HEADS * SB_HEAD_DIM
CONV_WIDTH = 512
MEM_HEADS = 4
ADAM_LR = 0.001
ADAM_B1 = 0.9
ADAM_B2 = 0.999
ADAM_EPS = 1e-08
ADAM_WD = 0.01
ADAM_STEP = 10

LANES = 128
VMEM_LIMIT_BYTES = 52 * 1024 * 1024
SB_TILE = 256
SB_DEAD = 110.0

ANY = pl.BlockSpec(memory_space=pl.ANY)


def _params(n_grid):
    return pltpu.CompilerParams(dimension_semantics=("arbitrary",) * n_grid, vmem_limit_bytes=VMEM_LIMIT_BYTES)


def _bdot(a, b, dims):
    return lax.dot_general(a.astype(BF16), b.astype(BF16), (dims, ((), ())), preferred_element_type=F32)


NN = ((1,), (0,))
NT = ((1,), (1,))
TN = ((0,), (0,))


class _Comm:
    def __init__(self, ins, outs, n_sems, start, finish):
        self.ins, self.outs, self.n_sems, self.start, self.finish = ins, outs, n_sems, start, finish

    def sem_shapes(self):
        return [pltpu.SemaphoreType.DMA((k,)) for k in self.n_sems]


def _place():
    return lax.axis_index("x"), lax.axis_index("y"), lax.axis_index("c")


def _gather_comm(shards):
    n = len(shards)

    def copies(ins, outs, sems):
        send_sems, recv_sems, _ = sems
        x, y, c = _place()
        chips = [(1 - x, y), (x, 1 - y), (1 - x, 1 - y)]

        def copy(a, k, block, to, from_shard=False):
            dst = outs[a].at[4 * block[0] + 2 * block[1] + block[2]]
            return pltpu.make_async_remote_copy(
                src_ref=ins[a] if from_shard else dst, dst_ref=dst, send_sem=send_sems.at[a * 7 + k],
                recv_sem=recv_sems.at[a * 7 + k], device_id=to, device_id_type=MESH)

        me, sibling = (x, y, c), (x, y, 1 - c)
        own = [[copy(a, 0, me, sibling, True)] + [copy(a, 1 + j, me, (*chip, c), True) for j, chip in enumerate(chips)]
               for a in range(n)]
        landed = [[copy(a, 1 + j, (*chip, c), me) for j, chip in enumerate(chips)] for a in range(n)]
        passed = [[copy(a, 4 + j, (*chip, c), sibling) for j, chip in enumerate(chips)] for a in range(n)]
        from_sibling = [[copy(a, 0, sibling, me)] + [copy(a, 4 + j, (*chip, 1 - c), me) for j, chip in enumerate(chips)]
                        for a in range(n)]
        local = [pltpu.make_async_copy(ins[a], outs[a].at[4 * x + 2 * y + c], sems[2].at[a]) for a in range(n)]
        return own, landed, passed, from_sibling, local

    def start(ins, outs, sems):
        own, _, _, _, local = copies(ins, outs, sems)
        for a in range(n):
            local[a].start()
            for cp in own[a]:
                cp.start()

    def finish(ins, outs, sems):
        own, landed, passed, from_sibling, local = copies(ins, outs, sems)
        for a in range(n):
            for arrived, onward in zip(landed[a], passed[a]):
                arrived.wait_recv()
                onward.start()
        for a in range(n):
            for cp in from_sibling[a]:
                cp.wait_recv()
        for a in range(n):
            for cp in own[a] + passed[a]:
                cp.wait_send()
            local[a].wait()

    outs = [jax.ShapeDtypeStruct((N_DEV,) + s.shape, s.dtype) for s in shards]
    return _Comm(list(shards), outs, (7 * n, 7 * n, n), start, finish)


def _sibling_comm(parts):
    n = len(parts)

    def copies(ins, outs, sems):
        x, y, c = _place()
        return [pltpu.make_async_remote_copy(
            src_ref=ins[a].at[2 * q + 1 - c], dst_ref=outs[a].at[q], send_sem=sems[0].at[a * N_CHIP + q],
            recv_sem=sems[1].at[a * N_CHIP + q], device_id=(x, y, 1 - c), device_id_type=MESH)
            for a in range(n) for q in range(N_CHIP)]

    def start(ins, outs, sems):
        for cp in copies(ins, outs, sems):
            cp.start()

    def finish(ins, outs, sems):
        cps = copies(ins, outs, sems)
        for cp in cps:
            cp.wait_recv()
        for cp in cps:
            cp.wait_send()

    outs = [jax.ShapeDtypeStruct((N_CHIP,) + p.shape[1:], p.dtype) for p in parts]
    return _Comm(list(parts), outs, (N_CHIP * n, N_CHIP * n), start, finish)


def _chips_comm(parts):
    n = len(parts)

    def copies(ins, outs, sems):
        x, y, c = _place()
        chips = [(1 - x, y), (x, 1 - y), (1 - x, 1 - y)]
        return [pltpu.make_async_remote_copy(
            src_ref=ins[a].at[2 * px + py], dst_ref=outs[a].at[j], send_sem=sems[0].at[a * 3 + j],
            recv_sem=sems[1].at[a * 3 + j], device_id=(px, py, c), device_id_type=MESH)
            for a in range(n) for j, (px, py) in enumerate(chips)]

    def start(ins, outs, sems):
        for cp in copies(ins, outs, sems):
            cp.start()

    def finish(ins, outs, sems):
        cps = copies(ins, outs, sems)
        for cp in cps:
            cp.wait_recv()
        for cp in cps:
            cp.wait_send()

    outs = [jax.ShapeDtypeStruct((3,) + p.shape[1:], p.dtype) for p in parts]
    return _Comm(list(parts), outs, (3 * n, 3 * n), start, finish)


def _join_comms(comms):
    if len(comms) == 1:
        return comms[0]

    def split(refs, counts):
        out, at = [], 0
        for n in counts:
            out.append(refs[at:at + n])
            at += n
        return out

    def each(method):
        def run(ins, outs, sems):
            parts = zip(comms, split(ins, [len(c.ins) for c in comms]), split(outs, [len(c.outs) for c in comms]),
                        split(sems, [len(c.n_sems) for c in comms]))
            for c, c_ins, c_outs, c_sems in parts:
                getattr(c, method)(c_ins, c_outs, c_sems)
        return run

    return _Comm([a for c in comms for a in c.ins], [o for c in comms for o in c.outs],
                 tuple(k for c in comms for k in c.n_sems), each("start"), each("finish"))


def _exchange(comm, *, name):
    n_ci, n_co = len(comm.ins), len(comm.outs)

    def kern(*refs):
        c_ins, c_outs, sems = refs[:n_ci], refs[n_ci:n_ci + n_co], refs[n_ci + n_co:]
        comm.start(c_ins, c_outs, sems)
        comm.finish(c_ins, c_outs, sems)

    return pl.pallas_call(kern, name=name, in_specs=[ANY] * n_ci, out_specs=[ANY] * n_co, out_shape=comm.outs,
                          scratch_shapes=comm.sem_shapes())(*comm.ins)


def _call(body, *, name, grid, in_specs, out_specs, out_shape, scratch, args, plan=None):
    comm = plan.comm(name) if plan is not None else None
    if comm is None:
        return list(pl.pallas_call(functools.partial(body), name=name, grid=grid, in_specs=in_specs,
                                   out_specs=out_specs, out_shape=out_shape, scratch_shapes=scratch,
                                   compiler_params=_params(len(grid)))(*args))
    n_in, n_out, n_scr, n_ci, n_co = len(in_specs), len(out_specs), len(scratch), len(comm.ins), len(comm.outs)

    def kern(*refs):
        ins, c_ins, refs = refs[:n_in], refs[n_in:n_in + n_ci], refs[n_in + n_ci:]
        outs, c_outs, refs = refs[:n_out], refs[n_out:n_out + n_co], refs[n_out + n_co:]
        scr, sems = refs[:n_scr], refs[n_scr:]
        ids = [pl.program_id(ax) for ax in range(len(grid))]
        first = functools.reduce(jnp.logical_and, [i == 0 for i in ids])
        last = functools.reduce(jnp.logical_and, [i == g - 1 for i, g in zip(ids, grid)])

        @pl.when(first)
        def _():
            comm.start(c_ins, c_outs, sems)
        body(*ins, *outs, *scr)

        @pl.when(last)
        def _():
            comm.finish(c_ins, c_outs, sems)

    res = pl.pallas_call(kern, name=name, grid=grid, in_specs=list(in_specs) + [ANY] * n_ci,
                         out_specs=list(out_specs) + [ANY] * n_co, out_shape=list(out_shape) + comm.outs,
                         scratch_shapes=list(scratch) + comm.sem_shapes(),
                         compiler_params=_params(len(grid)))(*args, *comm.ins)
    plan.landed(name, list(res[n_out:]))
    return list(res[:n_out])


def _mm_body(dims, has_add, *refs):
    if has_add:
        a_ref, b_ref, add_ref, o_ref = refs
        total = _bdot(a_ref[...], b_ref[...], dims) + add_ref[...]
    else:
        a_ref, b_ref, o_ref = refs
        total = _bdot(a_ref[...], b_ref[...], dims)
    o_ref[...] = total.astype(o_ref.dtype)


def _mm_nt_body(j, n, dy_ref, w_ref, o_ref):
    total = _bdot(dy_ref[:, 0:n], w_ref[0], NT)
    for jj in range(1, j):
        total = total + _bdot(dy_ref[:, jj * n:(jj + 1) * n], w_ref[jj], NT)
    o_ref[...] = total.astype(o_ref.dtype)


def _mm_nn(a, w3, *, name, out_dtype=BF16, add=None, tm=1024, tn=None, out3=False, plan=None):
    m, kk = a.shape
    j, _, n = w3.shape
    tm, tn = min(tm, m), n if tn is None else tn
    n_t = n // tn
    in_specs = [pl.BlockSpec((tm, kk), lambda i, jj: (i, 0)),
                pl.BlockSpec((None, kk, tn), lambda i, jj: (jj // n_t, 0, jj % n_t))]
    args = [a, w3]
    if add is not None:
        in_specs.append(pl.BlockSpec((tm, tn), lambda i, jj: (i, jj)))
        args.append(add)
    if out3:
        out_spec = pl.BlockSpec((None, tm, tn), lambda i, jj: (jj // n_t, i, jj % n_t))
        out_shape = jax.ShapeDtypeStruct((j, m, n), out_dtype)
    else:
        out_spec = pl.BlockSpec((tm, tn), lambda i, jj: (i, jj))
        out_shape = jax.ShapeDtypeStruct((m, j * n), out_dtype)
    return _call(
        functools.partial(_mm_body, NN, add is not None), name=name, grid=(m // tm, j * n_t), in_specs=in_specs,
        out_specs=[out_spec], out_shape=[out_shape], scratch=[], args=args, plan=plan)[0]


def _mm_gathering(a, shard, *, name, out3=False, tm=1024):
    m, kk = a.shape
    _, n = shard.shape
    tm = min(tm, m)
    n_i = m // tm

    def body(a_ref, shard_ref, o_ref, w_all, w_vmem, send_sems, recv_sems, copy_sems):
        jj, i = pl.program_id(0), pl.program_id(1)
        x, y, c = _place()
        me, sibling = (x, y, c), (x, y, 1 - c)
        chips = [(jnp.bitwise_xor(x, c), jnp.bitwise_xor(y, 1 - c)), (jnp.bitwise_xor(x, 1 - c), jnp.bitwise_xor(y, c)),
                 (1 - x, 1 - y)]
        sibling_chips = [chips[1], chips[0], chips[2]]

        def rows(block):
            return w_all.at[4 * block[0] + 2 * block[1] + block[2]]

        def remote(k, block, to, from_shard=False):
            return pltpu.make_async_remote_copy(
                src_ref=shard_ref if from_shard else rows(block), dst_ref=rows(block), send_sem=send_sems.at[k],
                recv_sem=recv_sems.at[k], device_id=to, device_id_type=MESH)

        def load(src):
            cp = pltpu.make_async_copy(src, w_vmem, copy_sems.at[1])
            cp.start()
            cp.wait()

        own = [remote(0, me, sibling, True)] + [remote(1 + j, me, (*chip, c), True) for j, chip in enumerate(chips)]
        passed = [remote(4 + j, (*chip, c), sibling) for j, chip in enumerate(chips)]
        local = pltpu.make_async_copy(shard_ref, rows(me), copy_sems.at[0])

        @pl.when(jnp.logical_and(i == 0, jj == 0))
        def _():
            local.start()
            own[0].start()
            own[1].start()
            load(shard_ref)

        @pl.when(jnp.logical_and(i == 0, jj == 1))
        def _():
            remote(0, sibling, me).wait_recv()
            load(rows(sibling))

        for j, chip in enumerate(chips):
            @pl.when(jnp.logical_and(i == 0, jj == 2 + 2 * j))
            def _():
                if j < 2:
                    own[1 + j].wait_send()
                    own[2 + j].start()
                remote(1 + j, (*chip, c), me).wait_recv()
                passed[j].start()
                load(rows((*chip, c)))

            @pl.when(jnp.logical_and(i == 0, jj == 3 + 2 * j))
            def _():
                block = (*sibling_chips[j], 1 - c)
                remote(4 + j, block, me).wait_recv()
                load(rows(block))

        o_ref[...] = _bdot(a_ref[...], w_vmem[...], NN).astype(o_ref.dtype)

        @pl.when(jnp.logical_and(i == n_i - 1, jj == N_DEV - 1))
        def _():
            for cp in [own[0], own[3]] + passed:
                cp.wait_send()
            local.wait()

    def swept(jj):
        x, y, c = _place()
        first, second = 2 + 2 * c, 4 - 2 * c
        flips = (0b000, 0b001, first, second + 1, second, first + 1, 0b110, 0b111)
        return jnp.bitwise_xor(4 * x + 2 * y + c, sum(jnp.where(jj == k, f, 0) for k, f in enumerate(flips)))

    if out3:
        out_spec = pl.BlockSpec((None, tm, n), lambda jj, i: (swept(jj), i, 0))
        out_shape = jax.ShapeDtypeStruct((N_DEV, m, n), BF16)
    else:
        out_spec = pl.BlockSpec((tm, n), lambda jj, i: (i, swept(jj)))
        out_shape = jax.ShapeDtypeStruct((m, N_DEV * n), BF16)
    return pl.pallas_call(
        body, name=name, grid=(N_DEV, n_i),
        in_specs=[pl.BlockSpec((tm, kk), lambda jj, i: (i, 0)), ANY], out_specs=[out_spec, ANY],
        scratch_shapes=[pltpu.VMEM((kk, n), shard.dtype), pltpu.SemaphoreType.DMA((7,)),
                        pltpu.SemaphoreType.DMA((7,)), pltpu.SemaphoreType.DMA((2,))],
        out_shape=[out_shape, jax.ShapeDtypeStruct((N_DEV, kk, n), shard.dtype)],
        compiler_params=_params(2))(a, shard)


def _sigmoid(v):
    return 1.0 / (1.0 + jnp.exp(-v))


def _ffn_out(gu3, w3, add, *, name, tm=512):
    j2, m, n = gu3.shape
    j = j2 // 2
    nn = w3.shape[2]
    tm = min(tm, m)

    def body(gu_ref, w_ref, add_ref, o_ref, act_ref):
        total = add_ref[...]
        for jj in range(j):
            gate = gu_ref[0, jj].astype(F32)
            act = (gate * _sigmoid(gate) * gu_ref[1, jj].astype(F32)).astype(BF16)
            act_ref[jj] = act
            total = total + _bdot(act, w_ref[jj], NN)
        o_ref[...] = total

    return _call(body, name=name, grid=(m // tm,),
                 in_specs=[pl.BlockSpec((2, j, tm, n), lambda i: (0, 0, i, 0)), pl.BlockSpec(w3.shape, lambda i: (0, 0, 0)),
                           pl.BlockSpec((tm, nn), lambda i: (i, 0))],
                 out_specs=[pl.BlockSpec((tm, nn), lambda i: (i, 0)), pl.BlockSpec((j, tm, n), lambda i: (0, i, 0))],
                 out_shape=[jax.ShapeDtypeStruct((m, nn), F32), jax.ShapeDtypeStruct((j, m, n), BF16)], scratch=[],
                 args=[gu3.reshape(2, j, m, n), w3, add])


def _ffn_out_bwd(dy, w3, gu3, *, name, tm=1024):
    m, nn = dy.shape
    j, n, _ = w3.shape
    tm = min(tm, m)

    def body(dy_ref, w_ref, gu_ref, dgu_ref):
        da = _bdot(dy_ref[...], w_ref[...], NT)
        gate = gu_ref[0].astype(F32)
        up = gu_ref[1].astype(F32)
        sg = _sigmoid(gate)
        silu = gate * sg
        dgu_ref[0] = (da * up * (sg + silu * (1.0 - sg))).astype(BF16)
        dgu_ref[1] = (da * silu).astype(BF16)

    out = _call(body, name=name, grid=(m // tm, j),
                in_specs=[pl.BlockSpec((tm, nn), lambda i, jj: (i, 0)),
                          pl.BlockSpec((None, n, nn), lambda i, jj: (jj, 0, 0)),
                          pl.BlockSpec((2, None, tm, n), lambda i, jj: (0, jj, i, 0))],
                out_specs=[pl.BlockSpec((2, None, tm, n), lambda i, jj: (0, jj, i, 0))],
                out_shape=[jax.ShapeDtypeStruct((2, j, m, n), BF16)], scratch=[],
                args=[dy, w3, gu3.reshape(2, j, m, n)])[0]
    return out.reshape(2 * j, m, n)


def _mm_nt_dy3(dy3, w3, *, name, out_dtype=F32, tm=512, plan=None):
    j, m, n = dy3.shape
    kk = w3.shape[1]
    tm = min(tm, m)

    def body(dy_ref, w_ref, o_ref):
        total = _bdot(dy_ref[0], w_ref[0], NT)
        for jj in range(1, j):
            total = total + _bdot(dy_ref[jj], w_ref[jj], NT)
        o_ref[...] = total.astype(o_ref.dtype)

    return _call(body, name=name, grid=(m // tm,),
                 in_specs=[pl.BlockSpec((j, tm, n), lambda i: (0, i, 0)), pl.BlockSpec(w3.shape, lambda i: (0, 0, 0))],
                 out_specs=[pl.BlockSpec((tm, kk), lambda i: (i, 0))],
                 out_shape=[jax.ShapeDtypeStruct((m, kk), out_dtype)], scratch=[], args=[dy3, w3], plan=plan)[0]


def _mm_tn_a3(a3, dy, *, name):
    j, t, n = a3.shape
    nn = dy.shape[1]
    return _call(functools.partial(_mm_body, TN, False), name=name, grid=(j,),
                 in_specs=[pl.BlockSpec((None, t, n), lambda jj: (jj, 0, 0)), pl.BlockSpec((t, nn), lambda jj: (0, 0))],
                 out_specs=[pl.BlockSpec((None, n, nn), lambda jj: (jj, 0, 0))],
                 out_shape=[jax.ShapeDtypeStruct((j, n, nn), BF16)], scratch=[], args=[a3, dy])[0]


def _mm_tn_dy3(a, dy3, *, name, tm=512):
    t, kk = a.shape
    j, _, n = dy3.shape
    tm = min(tm, kk)
    return _call(functools.partial(_mm_body, TN, False), name=name, grid=(kk // tm, j),
                 in_specs=[pl.BlockSpec((t, tm), lambda i, jj: (0, i)), pl.BlockSpec((None, t, n), lambda i, jj: (jj, 0, 0))],
                 out_specs=[pl.BlockSpec((None, tm, n), lambda i, jj: (jj, i, 0))],
                 out_shape=[jax.ShapeDtypeStruct((j, kk, n), BF16)], scratch=[], args=[a, dy3])[0]


def _mm_nt(dy, w3, *, name, out_dtype=BF16, tm=512, tn=1024, plan=None):
    m = dy.shape[0]
    j, kk, n = w3.shape
    tm, tn = min(tm, m), min(tn, kk)
    return _call(
        functools.partial(_mm_nt_body, j, n), name=name,
        grid=(m // tm, kk // tn),
        in_specs=[pl.BlockSpec((tm, j * n), lambda i, q: (i, 0)),
                  pl.BlockSpec((j, tn, n), lambda i, q: (0, q, 0))],
        out_specs=[pl.BlockSpec((tm, tn), lambda i, q: (i, q))],
        out_shape=[jax.ShapeDtypeStruct((m, kk), out_dtype)], scratch=[], args=[dy, w3], plan=plan)[0]


def _mm_tn(a, dy, n, *, name, out_dtype=BF16, tm=512, tn=None, k_tiles=None, plan=None):
    t, kk = a.shape
    j = dy.shape[1] // n
    tm, tn = min(tm, kk), n if tn is None else tn
    n_t = n // tn
    first, count = (0, kk // tm) if k_tiles is None else k_tiles
    return _call(
        functools.partial(_mm_body, TN, False), name=name,
        grid=(count, j * n_t),
        in_specs=[pl.BlockSpec((t, tm), lambda i, jj: (0, first + i)),
                  pl.BlockSpec((t, tn), lambda i, jj: (0, jj))],
        out_specs=[pl.BlockSpec((None, tm, tn), lambda i, jj: (jj // n_t, i, jj % n_t))],
        out_shape=[jax.ShapeDtypeStruct((j, count * tm, n), out_dtype)], scratch=[], args=[a, dy], plan=plan)[0]


def _rows(body, ins, outs, *, n_rows, tm, name, plan=None):
    tm = min(tm, n_rows)
    n_steps = n_rows // tm
    in_specs, args = [], []
    for arr, kind, width, block in ins:
        if kind == "row":
            in_specs.append(pl.BlockSpec((tm, width), functools.partial(lambda i, b: (i, b), b=block)))
        elif kind == "prev":
            in_specs.append(pl.BlockSpec((tm, width), functools.partial(lambda i, b: (jnp.maximum(i - 1, 0), b), b=block)))
        elif kind == "next":
            in_specs.append(pl.BlockSpec((tm, width), functools.partial(lambda i, b: (jnp.minimum(i + 1, n_steps - 1), b), b=block)))
        else:
            in_specs.append(pl.BlockSpec(arr.shape, functools.partial(lambda i, nd: (0,) * nd, nd=arr.ndim)))
        args.append(arr)
    out_specs, out_shape = [], []
    for shape, dtype, kind in outs:
        if kind == "row":
            out_specs.append(pl.BlockSpec((tm, shape[1]), lambda i: (i, 0)))
        else:
            out_specs.append(pl.BlockSpec(shape, functools.partial(lambda i, nd: (0,) * nd, nd=len(shape))))
        out_shape.append(jax.ShapeDtypeStruct(shape, dtype))

    def kern(*refs):
        body(pl.program_id(0), n_steps, *refs)

    return _call(kern, name=name, grid=(n_steps,), in_specs=in_specs, out_specs=out_specs, out_shape=out_shape,
                 scratch=[], args=args, plan=plan)


def _acc_rows(i, ref, value):
    @pl.when(i == 0)
    def _():
        ref[...] = jnp.zeros_like(ref)
    ref[...] += jnp.broadcast_to(value, ref.shape)


def _rms_fwd(x, g, *, name, tm=512):
    s, d = x.shape

    def body(i, n, x_ref, g_ref, h_ref):
        xv = x_ref[...]
        r = lax.rsqrt(jnp.mean(xv * xv, axis=-1, keepdims=True) + NORM_EPS)
        h_ref[...] = (xv * r * g_ref[...]).astype(BF16)

    return _rows(body, [(x, "row", d, 0), (g, "full", 0, 0)], [((s, d), BF16, "row")], n_rows=s, tm=tm, name=name)[0]


def _rms_bwd(x, g, dh, dres, *, name, tm=512, plan=None):
    s, d = x.shape

    def body(i, n, x_ref, g_ref, dh_ref, *rest):
        if dres is None:
            dx_ref, dxb_ref, dg_ref = rest
        else:
            dres_ref, dx_ref, dxb_ref, dg_ref = rest
        xv = x_ref[...]
        r = lax.rsqrt(jnp.mean(xv * xv, axis=-1, keepdims=True) + NORM_EPS)
        xhat = xv * r
        dhv = dh_ref[...].astype(F32)
        dxhat = dhv * g_ref[...]
        dx = r * (dxhat - xhat * jnp.mean(dxhat * xhat, axis=-1, keepdims=True))
        if dres is not None:
            dx = dx + dres_ref[...]
        dx_ref[...] = dx
        dxb_ref[...] = dx.astype(BF16)
        _acc_rows(i, dg_ref, jnp.sum(dhv * xhat, axis=0, keepdims=True))

    ins = [(x, "row", d, 0), (g, "full", 0, 0), (dh, "row", d, 0)]
    if dres is not None:
        ins.append((dres, "row", d, 0))
    return _rows(body, ins, [((s, d), F32, "row"), ((s, d), BF16, "row"), ((8, d), F32, "acc")],
                 n_rows=s, tm=tm, name=name, plan=plan)


def _loss_bwd(x, g, target, *, name, tm=512):
    s, d = x.shape

    def body(i, n, x_ref, g_ref, t_ref, dx_ref, dxb_ref, dg_ref, loss_ref):
        xv = x_ref[...]
        gv = g_ref[...]
        r = lax.rsqrt(jnp.mean(xv * xv, axis=-1, keepdims=True) + NORM_EPS)
        xhat = xv * r
        err = xhat * gv - t_ref[...]
        part = 0.5 * jnp.sum(jnp.mean(err * err, axis=-1, keepdims=True), axis=0, keepdims=True)
        dy = err * (1.0 / d)
        dxhat = dy * gv
        dx = r * (dxhat - xhat * jnp.mean(dxhat * xhat, axis=-1, keepdims=True))
        dx_ref[...] = dx
        dxb_ref[...] = dx.astype(BF16)
        _acc_rows(i, dg_ref, jnp.sum(dy * xhat, axis=0, keepdims=True))
        _acc_rows(i, loss_ref, part)

    return _rows(body, [(x, "row", d, 0), (g, "full", 0, 0), (target, "row", d, 0)],
                 [((s, d), F32, "row"), ((s, d), BF16, "row"), ((8, d), F32, "acc"), ((8, LANES), F32, "acc")],
                 n_rows=s, tm=tm, name=name)


def _gates_fwd(br_a, br_b, proj, *, name, tm=512):
    s, d = br_a.shape

    def body(i, n, a_ref, b_ref, ga_ref, gb_ref, o_ref):
        o_ref[...] = (_sigmoid(ga_ref[...].astype(F32)) * a_ref[...].astype(F32)
                      + _sigmoid(gb_ref[...].astype(F32)) * b_ref[...].astype(F32)).astype(BF16)

    return _rows(body, [(br_a, "row", d, 0), (br_b, "row", d, 0), (proj, "row", d, 3), (proj, "row", d, 4)],
                 [((s, d), BF16, "row")], n_rows=s, tm=tm, name=name)[0]


def _gates_bwd(dmerged, br_a, br_b, proj, *, name, tm=512):
    s, d = br_a.shape

    def body(i, n, dm_ref, a_ref, b_ref, ga_ref, gb_ref, da_ref, db_ref, dg_ref):
        dm = dm_ref[...].astype(F32)
        sa = _sigmoid(ga_ref[...].astype(F32))
        sb = _sigmoid(gb_ref[...].astype(F32))
        da_ref[...] = (dm * sa).astype(BF16)
        db_ref[...] = (dm * sb).astype(BF16)
        dg_ref[:, :d] = (dm * a_ref[...].astype(F32) * sa * (1.0 - sa)).astype(BF16)
        dg_ref[:, d:] = (dm * b_ref[...].astype(F32) * sb * (1.0 - sb)).astype(BF16)

    return _rows(body, [(dmerged, "row", d, 0), (br_a, "row", d, 0), (br_b, "row", d, 0),
                        (proj, "row", d, 3), (proj, "row", d, 4)],
                 [((s, d), BF16, "row"), ((s, d), BF16, "row"), ((s, 2 * d), BF16, "row")],
                 n_rows=s, tm=tm, name=name)


def _shift_down(cur, prev, k, first):
    row = lax.broadcasted_iota(jnp.int32, cur.shape, 0)
    out = jnp.where(row >= k, pltpu.roll(cur, k, 0), pltpu.roll(prev, k, 0))
    return jnp.where(jnp.logical_and(first, row < k), 0.0, out)


def _shift_up(cur, nxt, k, last):
    tm = cur.shape[0]
    row = lax.broadcasted_iota(jnp.int32, cur.shape, 0)
    out = jnp.where(row < tm - k, pltpu.roll(cur, tm - k, 0), pltpu.roll(nxt, tm - k, 0))
    return jnp.where(jnp.logical_and(last, row >= tm - k), 0.0, out)


def _conv_fwd(proj, conv_w, *, name, tm=512):
    s = proj.shape[0]
    c = CONV_WIDTH

    def body(i, n, u_ref, gb_ref, gc_ref, up_ref, gcp_ref, w_ref, y_ref):
        cu = gc_ref[...].astype(F32) * u_ref[...].astype(F32)
        cup = gcp_ref[...].astype(F32) * up_ref[...].astype(F32)
        first = i == 0
        y = (w_ref[0:1, :] * _shift_down(cu, cup, 2, first) + w_ref[1:2, :] * _shift_down(cu, cup, 1, first)
             + w_ref[2:3, :] * cu)
        y_ref[...] = (gb_ref[...].astype(F32) * y).astype(BF16)

    return _rows(body, [(proj, "row", c, 3), (proj, "row", c, 4), (proj, "row", c, 5),
                        (proj, "prev", c, 3), (proj, "prev", c, 5), (conv_w, "full", 0, 0)],
                 [((s, c), BF16, "row")], n_rows=s, tm=tm, name=name)[0]


def _conv_bwd(dy_b, proj, conv_w, *, name, tm=512, plan=None):
    s = proj.shape[0]
    c = CONV_WIDTH

    def body(i, n, dy_ref, u_ref, gb_ref, gc_ref, up_ref, gcp_ref, dyn_ref, gbn_ref, w_ref, d_ref, dw_ref):
        first, last = i == 0, i == n - 1
        u = u_ref[...].astype(F32)
        gb = gb_ref[...].astype(F32)
        gc = gc_ref[...].astype(F32)
        cu = gc * u
        cup = gcp_ref[...].astype(F32) * up_ref[...].astype(F32)
        cu1 = _shift_down(cu, cup, 1, first)
        cu2 = _shift_down(cu, cup, 2, first)
        conv = w_ref[0:1, :] * cu2 + w_ref[1:2, :] * cu1 + w_ref[2:3, :] * cu
        dy = dy_ref[...].astype(F32)
        dyc = dy * gb
        dycn = dyn_ref[...].astype(F32) * gbn_ref[...].astype(F32)
        dcu = (w_ref[2:3, :] * dyc + w_ref[1:2, :] * _shift_up(dyc, dycn, 1, last)
               + w_ref[0:1, :] * _shift_up(dyc, dycn, 2, last))
        d_ref[:, 0:c] = (dcu * gc).astype(BF16)
        d_ref[:, c:2 * c] = (dy * conv).astype(BF16)
        d_ref[:, 2 * c:3 * c] = (dcu * u).astype(BF16)
        row = lax.broadcasted_iota(jnp.int32, (8, c), 0)
        dw = (jnp.where(row == 0, jnp.sum(dyc * cu2, axis=0, keepdims=True), 0.0)
              + jnp.where(row == 1, jnp.sum(dyc * cu1, axis=0, keepdims=True), 0.0)
              + jnp.where(row == 2, jnp.sum(dyc * cu, axis=0, keepdims=True), 0.0))

        @pl.when(first)
        def _():
            dw_ref[...] = jnp.zeros_like(dw_ref)
        dw_ref[...] += dw

    return _rows(body, [(dy_b, "row", c, 0), (proj, "row", c, 3), (proj, "row", c, 4), (proj, "row", c, 5),
                        (proj, "prev", c, 3), (proj, "prev", c, 5), (dy_b, "next", c, 0), (proj, "next", c, 4),
                        (conv_w, "full", 0, 0)],
                 [((s, 3 * c), BF16, "row"), ((8, c), F32, "acc")], n_rows=s, tm=tm, name=name, plan=plan)


def _mem_probs(q, k, scale):
    sc = _bdot(q, k, NT) * scale
    sc = sc - jnp.max(sc, axis=-1, keepdims=True)
    p = jnp.exp(sc)
    return p / jnp.sum(p, axis=-1, keepdims=True)


def _memattn_fwd(qm, kv, *, name, tm=512):
    s, d = qm.shape
    hd = d // MEM_HEADS
    scale = 1.0 / math.sqrt(hd)

    def body(i, n, q_ref, kv_ref, o_ref):
        for h in range(MEM_HEADS):
            cols = slice(h * hd, (h + 1) * hd)
            p = _mem_probs(q_ref[:, cols], kv_ref[:, cols], scale)
            o_ref[:, cols] = _bdot(p, kv_ref[:, d + h * hd:d + (h + 1) * hd], NN).astype(BF16)

    return _rows(body, [(qm, "row", d, 0), (kv, "full", 0, 0)], [((s, d), BF16, "row")], n_rows=s, tm=tm, name=name)[0]


def _memattn_bwd(dom, qm, kv, *, name, tm=512):
    s, d = qm.shape
    hd = d // MEM_HEADS
    scale = 1.0 / math.sqrt(hd)

    def body(i, n, do_ref, q_ref, kv_ref, dq_ref, dkv_ref):
        @pl.when(i == 0)
        def _():
            dkv_ref[...] = jnp.zeros_like(dkv_ref)
        for h in range(MEM_HEADS):
            cols = slice(h * hd, (h + 1) * hd)
            vcols = slice(d + h * hd, d + (h + 1) * hd)
            q, k, v, do = q_ref[:, cols], kv_ref[:, cols], kv_ref[:, vcols], do_ref[:, cols]
            p = _mem_probs(q, k, scale)
            dp = _bdot(do, v, NT)
            ds = p * (dp - jnp.sum(dp * p, axis=-1, keepdims=True)) * scale
            dq_ref[:, cols] = _bdot(ds, k, NN).astype(BF16)
            dkv_ref[:, cols] += _bdot(ds, q, TN)
            dkv_ref[:, vcols] += _bdot(p, do, TN)

    return _rows(body, [(dom, "row", d, 0), (qm, "row", d, 0), (kv, "full", 0, 0)],
                 [((s, d), BF16, "row"), (kv.shape, F32, "acc")], n_rows=s, tm=tm, name=name)


def _sb_consts(t):
    row = lax.broadcasted_iota(jnp.int32, (t, t), 0)
    col = lax.broadcasted_iota(jnp.int32, (t, t), 1)
    lane = lax.broadcasted_iota(jnp.int32, (t, LANES), 1)
    return row, col, lane < SB_HEAD_DIM


def _log_fail(z):
    return jnp.minimum(-z, 0.0) - jnp.log(1.0 + jnp.exp(-jnp.abs(z)))


def _tri_sum(v, tri):
    hi = v.astype(BF16)
    lo = (v - hi.astype(F32)).astype(BF16)
    return _bdot(hi, tri, NN) + _bdot(lo, tri, NN)


def _sb_fwd(proj, *, name, plan=None):
    s = proj.shape[0]
    t = SB_TILE
    n_q = s // t
    scale = 1.0 / math.sqrt(SB_HEAD_DIM)
    k_blk, v_blk = SB_WIDTH // LANES, 2 * SB_WIDTH // LANES

    def body(q_ref, k_ref, v_ref, o_ref, c_ref, first_ref, acc_ref):
        i = pl.program_id(1)
        row, col, head0 = _sb_consts(t)
        later = (row > col).astype(BF16)
        valid = col < row
        qs = q_ref[...] * scale
        q2 = (jnp.where(head0, qs, 0), jnp.where(head0, 0, qs))

        def tile(kb, carry, diag):
            kt = k_ref[pl.ds(pl.multiple_of(kb * t, t), t), :]
            vt = v_ref[pl.ds(pl.multiple_of(kb * t, t), t), :]
            heads = range(2)
            z = [_bdot(q2[h], kt, NT) for h in heads]
            lf = [_log_fail(z[h]) for h in heads]
            if diag:
                lf = [jnp.where(valid, lf[h], 0.0) for h in heads]
            cum = [_tri_sum(lf[h], later) for h in heads]
            w = [jnp.exp(z[h] + lf[h] + cum[h] + carry[h]) for h in heads]
            if diag:
                w = [jnp.where(valid, w[h], 0.0) for h in heads]
            for h in heads:
                acc_ref[h] += _bdot(w[h], vt, NN)
            return tuple(carry[h] + cum[h][:, 0:1] + lf[h][:, 0:1] for h in heads)

        acc_ref[...] = jnp.zeros_like(acc_ref)
        zero = jnp.zeros((t, 1), F32)

        def alive(carry):
            return (jnp.maximum(jnp.max(carry[0]), jnp.max(carry[1])) > -SB_DEAD).astype(jnp.int32)

        def step(state):
            kb, _, c0, c1 = state
            new = tile(kb, (c0, c1), False)
            return kb - 1, alive(new), new[0], new[1]

        carry = tile(i, (zero, zero), True)
        kb, _, c0, c1 = lax.while_loop(lambda st: jnp.logical_and(st[0] >= 0, st[1] > 0), step,
                                       (i - 1, alive(carry), carry[0], carry[1]))
        o_ref[...] = jnp.where(head0, acc_ref[0], acc_ref[1]).astype(BF16)
        c_ref[...] = jnp.where(lax.broadcasted_iota(jnp.int32, (t, 2), 1) == 0, c0, c1)
        first_ref[pl.program_id(0), i] = (kb + 1).astype(F32)

    return _call(
        body, name=name, grid=(SB_HEADS // 2, n_q),
        in_specs=[pl.BlockSpec((t, LANES), lambda p, i: (i, p)),
                  pl.BlockSpec((s, LANES), lambda p, i: (0, k_blk + p)),
                  pl.BlockSpec((s, LANES), lambda p, i: (0, v_blk + p))],
        out_specs=[pl.BlockSpec((t, LANES), lambda p, i: (i, p)),
                   pl.BlockSpec((None, t, 2), lambda p, i: (p, i, 0)),
                   pl.BlockSpec(memory_space=pltpu.SMEM)],
        out_shape=[jax.ShapeDtypeStruct((s, SB_WIDTH), BF16), jax.ShapeDtypeStruct((SB_HEADS // 2, s, 2), F32),
                   jax.ShapeDtypeStruct((SB_HEADS // 2, n_q), F32)],
        scratch=[pltpu.VMEM((2, t, LANES), F32)], args=[proj, proj, proj], plan=plan)


def _sb_bwd(proj, do_a, ctot, first, *, name, plan=None):
    s = proj.shape[0]
    t = SB_TILE
    n_q = s // t
    scale = 1.0 / math.sqrt(SB_HEAD_DIM)
    k_blk, v_blk = SB_WIDTH // LANES, 2 * SB_WIDTH // LANES

    def body(q_ref, k_ref, v_ref, do_ref, c_ref, first_ref, dq_ref, dk_ref, dv_ref, dq_acc, dk_acc, dv_acc):
        i = pl.program_id(1)
        kb0 = jnp.clip(first_ref[pl.program_id(0), i].astype(jnp.int32), 0, i)
        row, col, head0 = _sb_consts(t)
        upto = (row <= col).astype(BF16)
        before = (row < col).astype(BF16)
        valid = col < row
        qs = q_ref[...] * scale
        q2 = (jnp.where(head0, qs, 0), jnp.where(head0, 0, qs))
        do = do_ref[...]
        do2 = (jnp.where(head0, do, 0), jnp.where(head0, 0, do))
        ctot2 = (c_ref[:, 0:1], c_ref[:, 1:2])

        @pl.when(i == 0)
        def _():
            dk_acc[...] = jnp.zeros_like(dk_acc)
            dv_acc[...] = jnp.zeros_like(dv_acc)
        dq_acc[...] = jnp.zeros_like(dq_acc)

        def tile(kb, carry, diag):
            rows = pl.ds(pl.multiple_of(kb * t, t), t)
            kt = k_ref[rows, :]
            vt = v_ref[rows, :]
            heads = range(2)
            lf_before, g_before = carry[0::2], carry[1::2]
            z = [_bdot(q2[h], kt, NT) for h in heads]
            dw = [_bdot(do2[h], vt, NT) for h in heads]
            lf = [_log_fail(z[h]) for h in heads]
            if diag:
                lf = [jnp.where(valid, lf[h], 0.0) for h in heads]
            cum = [_tri_sum(lf[h], upto) for h in heads]
            beta = [jnp.exp(z[h] + lf[h]) for h in heads]
            w = [beta[h] * jnp.exp(ctot2[h] - lf_before[h] - cum[h]) for h in heads]
            if diag:
                w = [jnp.where(valid, w[h], 0.0) for h in heads]
            g = [w[h] * dw[h] for h in heads]
            g_sum = [g_before[h] + _bdot(g[h], before, NN) for h in heads]
            for h in heads:
                dv_acc[rows, :] += _bdot(w[h], do2[h], TN)
            dz = [g[h] * jnp.exp(lf[h]) - beta[h] * g_sum[h] for h in heads]
            if diag:
                dz = [jnp.where(valid, dz[h], 0.0) for h in heads]
            for h in heads:
                dq_acc[h] += _bdot(dz[h], kt, NN)
                dk_acc[rows, :] += _bdot(dz[h], q2[h], TN)
            t_last = slice(t - 1, t)
            new = []
            for h in heads:
                new += [lf_before[h] + cum[h][:, t_last], g_sum[h][:, t_last] + g[h][:, t_last]]
            return tuple(new)

        zero = jnp.zeros((t, 1), F32)
        carry = lax.fori_loop(kb0, i, lambda n, c: tile(n, c, False), (zero,) * 4)
        tile(i, carry, True)
        dq_ref[...] = (jnp.where(head0, dq_acc[0], dq_acc[1]) * scale).astype(BF16)

        @pl.when(i == n_q - 1)
        def _():
            dk_ref[...] = dk_acc[...].astype(BF16)
            dv_ref[...] = dv_acc[...].astype(BF16)

    outs = _call(
        body, name=name, grid=(SB_HEADS // 2, n_q),
        in_specs=[pl.BlockSpec((t, LANES), lambda p, i: (i, p)),
                  pl.BlockSpec((s, LANES), lambda p, i: (0, k_blk + p)),
                  pl.BlockSpec((s, LANES), lambda p, i: (0, v_blk + p)),
                  pl.BlockSpec((t, LANES), lambda p, i: (i, p)),
                  pl.BlockSpec((None, t, 2), lambda p, i: (p, i, 0)),
                  pl.BlockSpec(memory_space=pltpu.SMEM)],
        out_specs=[pl.BlockSpec((t, LANES), lambda p, i: (i, p)),
                   pl.BlockSpec((s, LANES), lambda p, i: (0, p)),
                   pl.BlockSpec((s, LANES), lambda p, i: (0, p))],
        out_shape=[jax.ShapeDtypeStruct((s, SB_WIDTH), BF16)] * 3,
        scratch=[pltpu.VMEM((2, t, LANES), F32), pltpu.VMEM((s, LANES), F32), pltpu.VMEM((s, LANES), F32)],
        args=[proj, proj, proj, do_a, ctot, first], plan=plan)
    return jnp.concatenate(outs, axis=1)


def _mm_gathered(a, key, plan, *, name, out3=False):
    src = plan.gathering(key)
    if src is None:
        return _mm_nn(a, plan.weight(key), name=name, out3=out3)
    out, w_all = _mm_gathering(a, src, name=name, out3=out3)
    plan.set_weight(key, w_all)
    return out


def _local_step(x, mem, target, gains, plan):
    g_mix, g_memq, g_memkv, g_ffn, g_fin = gains
    d = x.shape[1]

    h0 = _rms_fwd(x, g_mix, name="rms_mix")
    proj = _mm_gathered(h0, "in", plan, name="mm_in")
    w_in = plan.weight("in")
    o_a, ctot, first = _sb_fwd(proj, name="sb_fwd", plan=plan)
    conv_w = plan.weight("conv")
    y_b = _conv_fwd(proj, conv_w, name="conv_fwd")
    w_a, w_b, w_mix = plan.weight("a"), plan.weight("b"), plan.weight("mix")
    br_a = _mm_nn(o_a, w_a, name="mm_branch_a")
    br_b = _mm_nn(y_b, w_b, name="mm_branch_b")
    merged = _gates_fwd(br_a, br_b, proj, name="gates_fwd")
    x1 = _mm_nn(merged, w_mix, name="mm_mix", out_dtype=F32, add=x)
    hq = _rms_fwd(x1, g_memq, name="rms_memq")
    w_mq, w_kv, w_mo = plan.weight("mq"), plan.weight("kv"), plan.weight("mo")
    qm = _mm_nn(hq, w_mq, name="mm_memq")
    mn = _rms_fwd(mem, g_memkv, name="rms_memkv")
    kv = _mm_nn(mn, w_kv, name="mm_memkv")
    om = _memattn_fwd(qm, kv, name="memattn_fwd")
    x2 = _mm_nn(om, w_mo, name="mm_memo", out_dtype=F32, add=x1)
    hf = _rms_fwd(x2, g_ffn, name="rms_ffn")
    gu = _mm_gathered(hf, "fi", plan, name="mm_ffn_in", out3=True)
    w_fi, w_fo = plan.weight("fi"), plan.weight("fo")
    x3, act = _ffn_out(gu, w_fo, x2, name="mm_ffn_out")

    dx3, dx3b, dg_fin, loss = _loss_bwd(x3, g_fin, target, name="loss_bwd")

    plan.grad("fo", _mm_tn_a3(act, dx3b, name="mm_d_w_ffn_out"))
    dgu = _ffn_out_bwd(dx3b, w_fo, gu, name="mm_d_act")
    plan.grad("fi", _mm_tn_dy3(hf, dgu, name="mm_d_w_ffn_in"))
    dhf = _mm_nt_dy3(dgu, w_fi, name="mm_d_hf", plan=plan)
    dx2, dx2b, dg_ffn = _rms_bwd(x2, g_ffn, dhf, dx3, name="rms_ffn_bwd")

    plan.grad("mo", _mm_tn(om, dx2b, d, name="mm_d_w_memo"))
    dom = _mm_nt(dx2b, w_mo, name="mm_d_om")
    dqm, dkv = _memattn_bwd(dom, qm, kv, name="memattn_bwd")
    plan.grad("mq", _mm_tn(hq, dqm, d, name="mm_d_w_memq"))
    dhq = _mm_nt(dqm, w_mq, name="mm_d_hq", out_dtype=F32)
    dx1, dx1b, dg_memq = _rms_bwd(x1, g_memq, dhq, dx2, name="rms_memq_bwd")
    plan.grad("kv", _mm_tn(mn, dkv, w_kv.shape[2], name="mm_d_w_memkv"))
    dmn = _mm_nt(dkv, w_kv, name="mm_d_mn", out_dtype=F32)
    _, _, dg_memkv = _rms_bwd(mem, g_memkv, dmn, None, name="rms_memkv_bwd")

    plan.grad("mix", _mm_tn(merged, dx1b, d, name="mm_d_w_mix"))
    dmerged = _mm_nt(dx1b, w_mix, name="mm_d_merged", plan=plan)
    dbr_a, dbr_b, dgab = _gates_bwd(dmerged, br_a, br_b, proj, name="gates_bwd")
    plan.grad("a", _mm_tn(o_a, dbr_a, d, name="mm_d_w_branch_a"))
    do_a = _mm_nt(dbr_a, w_a, name="mm_d_o_a")
    plan.grad("b", _mm_tn(y_b, dbr_b, d, name="mm_d_w_branch_b"))
    dy_b = _mm_nt(dbr_b, w_b, name="mm_d_y_b")
    dconv, dconv_w = _conv_bwd(dy_b, proj, conv_w, name="conv_bwd", plan=plan)
    dqkv = _sb_bwd(proj, do_a, ctot, first, name="sb_bwd", plan=plan)
    dproj = jnp.concatenate([dqkv, dconv, dgab], axis=1)
    plan.grad("in0", _mm_tn(h0, dproj, w_in.shape[2], name="mm_d_w_in0", k_tiles=(0, 1)))
    plan.grad("in1", _mm_tn(h0, dproj, w_in.shape[2], name="mm_d_w_in1", k_tiles=(1, 1), plan=plan))
    dh0 = _mm_nt(dproj, w_in, name="mm_d_h0", out_dtype=F32, plan=plan)
    dx0, _, dg_mix = _rms_bwd(x, g_mix, dh0, dx1, name="rms_mix_bwd", plan=plan)

    return dx0, (dg_mix, dg_memq, dg_memkv, dg_ffn, dg_fin, dconv_w, loss)


def _row_tile(a, target=512):
    tm = min(a, target)
    while a % tm:
        tm -= 8
    return tm


def _sum_with_sibling(part, recv, core, *, name):
    _, a, b = part.shape
    tm = _row_tile(a)

    def body(core_ref, p_ref, r_ref, o_ref):
        o_ref[...] = (p_ref[...].astype(F32) + r_ref[...].astype(F32)).astype(o_ref.dtype)

    return pl.pallas_call(
        body, name=name,
        grid_spec=pltpu.PrefetchScalarGridSpec(
            num_scalar_prefetch=1, grid=(N_CHIP, a // tm),
            in_specs=[pl.BlockSpec((None, tm, b), lambda q, i, core_ref: (2 * q + core_ref[0], i, 0)),
                      pl.BlockSpec((None, tm, b), lambda q, i, core_ref: (q, i, 0))],
            out_specs=pl.BlockSpec((None, tm, b), lambda q, i, core_ref: (q, i, 0))),
        out_shape=jax.ShapeDtypeStruct((N_CHIP, a, b), part.dtype), compiler_params=_params(2))(core, part, recv)


def _adam_math(wv, g, m, v):
    m = ADAM_B1 * m + (1.0 - ADAM_B1) * g
    v = ADAM_B2 * v + (1.0 - ADAM_B2) * (g * g)
    m_hat = m / (1.0 - ADAM_B1 ** ADAM_STEP)
    v_hat = v / (1.0 - ADAM_B2 ** ADAM_STEP)
    delta = -ADAM_LR * (m_hat / (jnp.sqrt(v_hat) + ADAM_EPS) + ADAM_WD * wv)
    return delta, m, v


def _adam_sharded(wv, m, v, own, recv, chip, *, name):
    a, b = wv.shape
    tm = _row_tile(a)

    def body(chip_ref, w_ref, m_ref, v_ref, own_ref, recv_ref, g_ref, d_ref, nm_ref, nv_ref):
        g = own_ref[...].astype(F32)
        for j in range(3):
            g = g + recv_ref[j].astype(F32)
        delta, nm, nv = _adam_math(w_ref[...], g, m_ref[...], v_ref[...])
        g_ref[...] = g
        d_ref[...] = delta
        nm_ref[...] = nm
        nv_ref[...] = nv

    tile = pl.BlockSpec((tm, b), lambda i, chip_ref: (i, 0))
    return pl.pallas_call(
        body, name=name,
        grid_spec=pltpu.PrefetchScalarGridSpec(
            num_scalar_prefetch=1, grid=(a // tm,),
            in_specs=[tile, tile, tile,
                      pl.BlockSpec((None, tm, b), lambda i, chip_ref: (chip_ref[0], i, 0)),
                      pl.BlockSpec((3, tm, b), lambda i, chip_ref: (0, i, 0))],
            out_specs=[tile] * 4),
        out_shape=[jax.ShapeDtypeStruct((a, b), F32)] * 4, compiler_params=_params(1))(chip, wv, m, v, own, recv)


def _sum_devices(gathered, *, name):
    _, r, c = gathered.shape

    def body(g_ref, o_ref):
        total = g_ref[0]
        for j in range(1, N_DEV):
            total = total + g_ref[j]
        o_ref[...] = total

    return pl.pallas_call(body, name=name, out_shape=jax.ShapeDtypeStruct((r, c), F32))(gathered)


def _adam_small(wv, g, m, v, *, name):
    def body(w_ref, g_ref, m_ref, v_ref, d_ref, nm_ref, nv_ref):
        delta, nm, nv = _adam_math(w_ref[...], g_ref[...], m_ref[...], v_ref[...])
        d_ref[...] = delta
        nm_ref[...] = nm
        nv_ref[...] = nv

    return pl.pallas_call(body, name=name, out_shape=[jax.ShapeDtypeStruct(wv.shape, F32)] * 3)(wv, g, m, v)


BIG = ("in", "a", "b", "mix", "mq", "kv", "mo", "fi", "fo")
ROW_SHARDED = ("mix", "mq", "mo")
UNSHARDED = ("a", "b")
FFN_GROUPS = 4
SMALL_ROWS = 16


class _Plan:
    FUSED = ("in", "fi")
    GATHER_ON = {"sb_fwd": ("a", "b", "mix", "kv", "mq", "mo", "fo", "conv")}
    SIBLING_ON = {"mm_d_hf": ("fo", "fi"), "mm_d_merged": ("mo", "mq", "kv"), "conv_bwd": ("mix", "a", "b"),
                  "mm_d_w_in1": ("in0",), "mm_d_h0": ("in1",)}
    CHIPS_ON = {"sb_bwd": ("fo", "fi", "mo", "mq", "kv", "mix", "a", "b"), "mm_d_h0": ("in0",),
                "rms_mix_bwd": ("in1",)}

    def __init__(self, shards, core):
        self.shards, self.core = shards, core
        self.w, self.parts, self.chip_sums, self.from_chips = {}, {}, {}, {}

    def gathering(self, k):
        return self.shards[k] if k in self.FUSED else None

    def comm(self, name):
        comms = []
        if name in self.GATHER_ON:
            comms.append(_gather_comm([self.shards[k] for k in self.GATHER_ON[name]]))
        if name in self.SIBLING_ON:
            comms.append(_sibling_comm([self.parts[k] for k in self.SIBLING_ON[name]]))
        if name in self.CHIPS_ON:
            comms.append(_chips_comm([self.chip_sums[k] for k in self.CHIPS_ON[name]]))
        return _join_comms(comms) if comms else None

    def landed(self, name, outs):
        outs = list(outs)
        for k in self.GATHER_ON.get(name, ()):
            self.set_weight(k, outs.pop(0))
        for k in self.SIBLING_ON.get(name, ()):
            self.chip_sums[k] = _sum_with_sibling(self.parts[k], outs.pop(0), self.core, name="sum_with_sibling_" + k)
        for k in self.CHIPS_ON.get(name, ()):
            self.from_chips[k] = outs.pop(0)

    def set_weight(self, k, gathered):
        _, a, b = gathered.shape
        if k in ROW_SHARDED:
            gathered = gathered.reshape(1, N_DEV * a, b)
        elif k in UNSHARDED:
            gathered = jnp.transpose(gathered, (1, 0, 2)).reshape(1, a, N_DEV * b)
        elif k == "fo":
            gathered = gathered.reshape(FFN_GROUPS, N_DEV * a // FFN_GROUPS, b)
        elif k == "conv":
            n_conv = CONV_WIDTH // N_DEV
            gathered = jnp.transpose(gathered[:, :3, :n_conv], (1, 0, 2)).reshape(3, CONV_WIDTH)
        self.w[k] = gathered

    def weight(self, k):
        return self.w[k]

    def grad(self, k, g):
        _, a, b = g.shape
        if k in ROW_SHARDED:
            g = g.reshape(N_DEV, a // N_DEV, b)
        elif k in UNSHARDED:
            g = jnp.transpose(g.reshape(a, N_DEV, b // N_DEV), (1, 0, 2))
        elif k == "fo":
            g = g.reshape(N_DEV, FFN_GROUPS * a // N_DEV, b)
        self.parts[k] = g


def kernel(x, mem, norm_mix, w_in, conv_w, w_branch_a, w_branch_b, w_mix_out, norm_mem_q, norm_mem_kv, w_mem_q, w_mem_kv, w_mem_o, norm_ffn, w_ffn_in, w_ffn_out, norm_final, loss_target, m_norm_mix, m_w_in, m_conv_w, m_w_branch_a, m_w_branch_b, m_w_mix_out, m_norm_mem_q, m_norm_mem_kv, m_w_mem_q, m_w_mem_kv, m_w_mem_o, m_norm_ffn, m_w_ffn_in, m_w_ffn_out, m_norm_final, v_norm_mix, v_w_in, v_conv_w, v_w_branch_a, v_w_branch_b, v_w_mix_out, v_norm_mem_q, v_norm_mem_kv, v_w_mem_q, v_w_mem_kv, v_w_mem_o, v_norm_ffn, v_w_ffn_in, v_w_ffn_out, v_norm_final):
    d = x.shape[-1]
    xi, yi, ci = lax.axis_index("x"), lax.axis_index("y"), lax.axis_index("c")
    core = jnp.reshape(ci, (1,)).astype(jnp.int32)
    chip = jnp.reshape(2 * xi + yi, (1,)).astype(jnp.int32)
    dev = 4 * xi + 2 * yi + ci

    big_w = dict(zip(BIG, (w_in, w_branch_a, w_branch_b, w_mix_out, w_mem_q, w_mem_kv, w_mem_o, w_ffn_in, w_ffn_out)))
    big_m = dict(zip(BIG, (m_w_in, m_w_branch_a, m_w_branch_b, m_w_mix_out, m_w_mem_q, m_w_mem_kv, m_w_mem_o, m_w_ffn_in, m_w_ffn_out)))
    big_v = dict(zip(BIG, (v_w_in, v_w_branch_a, v_w_branch_b, v_w_mix_out, v_w_mem_q, v_w_mem_kv, v_w_mem_o, v_w_ffn_in, v_w_ffn_out)))

    shards = {k: big_w[k][0].astype(BF16) for k in BIG}
    n_conv = conv_w.shape[-1]
    shards["conv"] = jnp.zeros((8, LANES), F32).at[:3, :n_conv].set(conv_w[0])
    plan = _Plan(shards, core)

    gains = (norm_mix, norm_mem_q, norm_mem_kv, norm_ffn, norm_final.reshape(1, d))
    dx0, small = _local_step(x[0], mem[0], loss_target[0], gains, plan)

    grads, deltas, new_m, new_v = {}, {}, {}, {}
    for k in BIG:
        lead = big_w[k].shape
        wv, mv, vv = big_w[k][0], big_m[k][0], big_v[k][0]
        if k == "in":
            half = wv.shape[0] // 2
            lo = _adam_sharded(wv[:half], mv[:half], vv[:half], plan.chip_sums["in0"], plan.from_chips["in0"], chip,
                               name="adam_in0")
            hi = _adam_sharded(wv[half:], mv[half:], vv[half:], plan.chip_sums["in1"], plan.from_chips["in1"], chip,
                               name="adam_in1")
            outs = [jnp.concatenate(pair, axis=0) for pair in zip(lo, hi)]
        else:
            outs = _adam_sharded(wv, mv, vv, plan.chip_sums[k], plan.from_chips[k], chip, name="adam_" + k)
        grads[k], deltas[k], new_m[k], new_v[k] = (t.reshape(lead) for t in outs)

    dg_mix, dg_memq, dg_memkv, dg_ffn, dg_fin, dconv_w, loss = small
    conv_rows = jnp.zeros((3, d), F32).at[:, :CONV_WIDTH].set(dconv_w[:3])
    block = jnp.concatenate([dg_mix[:1], dg_memq[:1], dg_memkv[:1], dg_ffn[:1], dg_fin[:1], conv_rows,
                             jnp.broadcast_to(loss[:1, :1], (1, d)), jnp.zeros((SMALL_ROWS - 9, d), F32)], axis=0)
    total = _sum_devices(_exchange(_gather_comm([block]), name="gather_small")[0], name="sum_small")
    g_conv = lax.dynamic_slice(total[5:8, :CONV_WIDTH], (0, dev * n_conv), (3, n_conv))
    small_w = [norm_mix, norm_mem_q, norm_mem_kv, norm_ffn, norm_final.reshape(1, d), conv_w[0]]
    small_m = [m_norm_mix, m_norm_mem_q, m_norm_mem_kv, m_norm_ffn, m_norm_final.reshape(1, d), m_conv_w[0]]
    small_v = [v_norm_mix, v_norm_mem_q, v_norm_mem_kv, v_norm_ffn, v_norm_final.reshape(1, d), v_conv_w[0]]
    small_g = [total[0:1], total[1:2], total[2:3], total[3:4], total[4:5], g_conv]
    small_names = ["norm_mix", "norm_mem_q", "norm_mem_kv", "norm_ffn", "norm_final", "conv_w"]
    sg, sd, sm, sv = {}, {}, {}, {}
    for nme, wv, g, m, v in zip(small_names, small_w, small_g, small_m, small_v):
        dl, nm, nv = _adam_small(wv, g, m, v, name="adam_" + nme)
        shape = norm_final.shape if nme == "norm_final" else (conv_w.shape if nme == "conv_w" else wv.shape)
        sg[nme], sd[nme], sm[nme], sv[nme] = (t.reshape(shape) for t in (g, dl, nm, nv))

    def ordered(big, sml):
        return (sml["norm_mix"], big["in"], sml["conv_w"], big["a"], big["b"], big["mix"], sml["norm_mem_q"],
                sml["norm_mem_kv"], big["mq"], big["kv"], big["mo"], sml["norm_ffn"], big["fi"], big["fo"],
                sml["norm_final"])

    loss_out = total[8, 0]
    grad_x = dx0.reshape(x.shape)
    return (loss_out, grad_x, *ordered(grads, sg), *ordered(deltas, sd), *ordered(new_m, sm), *ordered(new_v, sv))
```

```python
import functools
import math

import jax
import jax.numpy as jnp
from jax import lax
from jax.experimental import pallas as pl
from jax.experimental.pallas import tpu as pltpu

F32 = jnp.float32
BF16 = jnp.bfloat16
MESH = pl.DeviceIdType.MESH

N_DEV = 8
N_CHIP = 4
NORM_EPS = 1e-6
SB_HEADS = 8
SB_HEAD_DIM = 64
SB_WIDTH = SB_HEADS * SB_HEAD_DIM
CONV_WIDTH = 512
MEM_HEADS = 4
ADAM_LR = 0.001
ADAM_B1 = 0.9
ADAM_B2 = 0.999
ADAM_EPS = 1e-08
ADAM_WD = 0.01
ADAM_STEP = 10

LANES = 128
VMEM_LIMIT_BYTES = 52 * 1024 * 1024
SB_TILE = 256
SB_DEAD = 110.0

ANY = pl.BlockSpec(memory_space=pl.ANY)


def _params(n_grid):
    return pltpu.CompilerParams(dimension_semantics=("arbitrary",) * n_grid, vmem_limit_bytes=VMEM_LIMIT_BYTES)


def _bdot(a, b, dims):
    return lax.dot_general(a.astype(BF16), b.astype(BF16), (dims, ((), ())), preferred_element_type=F32)


NN = ((1,), (0,))
NT = ((1,), (1,))
TN = ((0,), (0,))


class _Comm:
    def __init__(self, ins, outs, n_sems, start, finish):
        self.ins, self.outs, self.n_sems, self.start, self.finish = ins, outs, n_sems, start, finish

    def sem_shapes(self):
        return [pltpu.SemaphoreType.DMA((k,)) for k in self.n_sems]


def _place():
    return lax.axis_index("x"), lax.axis_index("y"), lax.axis_index("c")


def _gather_comm(shards):
    n = len(shards)

    def copies(ins, outs, sems):
        send_sems, recv_sems, _ = sems
        x, y, c = _place()
        chips = [(1 - x, y), (x, 1 - y), (1 - x, 1 - y)]

        def copy(a, k, block, to, from_shard=False):
            dst = outs[a].at[4 * block[0] + 2 * block[1] + block[2]]
            return pltpu.make_async_remote_copy(
                src_ref=ins[a] if from_shard else dst, dst_ref=dst, send_sem=send_sems.at[a * 7 + k],
                recv_sem=recv_sems.at[a * 7 + k], device_id=to, device_id_type=MESH)

        me, sibling = (x, y, c), (x, y, 1 - c)
        own = [[copy(a, 0, me, sibling, True)] + [copy(a, 1 + j, me, (*chip, c), True) for j, chip in enumerate(chips)]
               for a in range(n)]
        landed = [[copy(a, 1 + j, (*chip, c), me) for j, chip in enumerate(chips)] for a in range(n)]
        passed = [[copy(a, 4 + j, (*chip, c), sibling) for j, chip in enumerate(chips)] for a in range(n)]
        from_sibling = [[copy(a, 0, sibling, me)] + [copy(a, 4 + j, (*chip, 1 - c), me) for j, chip in enumerate(chips)]
                        for a in range(n)]
        local = [pltpu.make_async_copy(ins[a], outs[a].at[4 * x + 2 * y + c], sems[2].at[a]) for a in range(n)]
        return own, landed, passed, from_sibling, local

    def start(ins, outs, sems):
        own, _, _, _, local = copies(ins, outs, sems)
        for a in range(n):
            local[a].start()
            for cp in own[a]:
                cp.start()

    def finish(ins, outs, sems):
        own, landed, passed, from_sibling, local = copies(ins, outs, sems)
        for a in range(n):
            for arrived, onward in zip(landed[a], passed[a]):
                arrived.wait_recv()
                onward.start()
        for a in range(n):
            for cp in from_sibling[a]:
                cp.wait_recv()
        for a in range(n):
            for cp in own[a] + passed[a]:
                cp.wait_send()
            local[a].wait()

    outs = [jax.ShapeDtypeStruct((N_DEV,) + s.shape, s.dtype) for s in shards]
    return _Comm(list(shards), outs, (7 * n, 7 * n, n), start, finish)


def _sibling_comm(parts):
    n = len(parts)

    def copies(ins, outs, sems):
        x, y, c = _place()
        return [pltpu.make_async_remote_copy(
            src_ref=ins[a].at[2 * q + 1 - c], dst_ref=outs[a].at[q], send_sem=sems[0].at[a * N_CHIP + q],
            recv_sem=sems[1].at[a * N_CHIP + q], device_id=(x, y, 1 - c), device_id_type=MESH)
            for a in range(n) for q in range(N_CHIP)]

    def start(ins, outs, sems):
        for cp in copies(ins, outs, sems):
            cp.start()

    def finish(ins, outs, sems):
        cps = copies(ins, outs, sems)
        for cp in cps:
            cp.wait_recv()
        for cp in cps:
            cp.wait_send()

    outs = [jax.ShapeDtypeStruct((N_CHIP,) + p.shape[1:], p.dtype) for p in parts]
    return _Comm(list(parts), outs, (N_CHIP * n, N_CHIP * n), start, finish)


def _chips_comm(parts):
    n = len(parts)

    def copies(ins, outs, sems):
        x, y, c = _place()
        chips = [(1 - x, y), (x, 1 - y), (1 - x, 1 - y)]
        return [pltpu.make_async_remote_copy(
            src_ref=ins[a].at[2 * px + py], dst_ref=outs[a].at[j], send_sem=sems[0].at[a * 3 + j],
            recv_sem=sems[1].at[a * 3 + j], device_id=(px, py, c), device_id_type=MESH)
            for a in range(n) for j, (px, py) in enumerate(chips)]

    def start(ins, outs, sems):
        for cp in copies(ins, outs, sems):
            cp.start()

    def finish(ins, outs, sems):
        cps = copies(ins, outs, sems)
        for cp in cps:
            cp.wait_recv()
        for cp in cps:
            cp.wait_send()

    outs = [jax.ShapeDtypeStruct((3,) + p.shape[1:], p.dtype) for p in parts]
    return _Comm(list(parts), outs, (3 * n, 3 * n), start, finish)


def _join_comms(comms):
    if len(comms) == 1:
        return comms[0]

    def split(refs, counts):
        out, at = [], 0
        for n in counts:
            out.append(refs[at:at + n])
            at += n
        return out

    def each(method):
        def run(ins, outs, sems):
            parts = zip(comms, split(ins, [len(c.ins) for c in comms]), split(outs, [len(c.outs) for c in comms]),
                        split(sems, [len(c.n_sems) for c in comms]))
            for c, c_ins, c_outs, c_sems in parts:
                getattr(c, method)(c_ins, c_outs, c_sems)
        return run

    return _Comm([a for c in comms for a in c.ins], [o for c in comms for o in c.outs],
                 tuple(k for c in comms for k in c.n_sems), each("start"), each("finish"))


def _exchange(comm, *, name):
    n_ci, n_co = len(comm.ins), len(comm.outs)

    def kern(*refs):
        c_ins, c_outs, sems = refs[:n_ci], refs[n_ci:n_ci + n_co], refs[n_ci + n_co:]
        comm.start(c_ins, c_outs, sems)
        comm.finish(c_ins, c_outs, sems)

    return pl.pallas_call(kern, name=name, in_specs=[ANY] * n_ci, out_specs=[ANY] * n_co, out_shape=comm.outs,
                          scratch_shapes=comm.sem_shapes())(*comm.ins)


def _call(body, *, name, grid, in_specs, out_specs, out_shape, scratch, args, plan=None):
    comm = plan.comm(name) if plan is not None else None
    if comm is None:
        return list(pl.pallas_call(functools.partial(body), name=name, grid=grid, in_specs=in_specs,
                                   out_specs=out_specs, out_shape=out_shape, scratch_shapes=scratch,
                                   compiler_params=_params(len(grid)))(*args))
    n_in, n_out, n_scr, n_ci, n_co = len(in_specs), len(out_specs), len(scratch), len(comm.ins), len(comm.outs)

    def kern(*refs):
        ins, c_ins, refs = refs[:n_in], refs[n_in:n_in + n_ci], refs[n_in + n_ci:]
        outs, c_outs, refs = refs[:n_out], refs[n_out:n_out + n_co], refs[n_out + n_co:]
        scr, sems = refs[:n_scr], refs[n_scr:]
        ids = [pl.program_id(ax) for ax in range(len(grid))]
        first = functools.reduce(jnp.logical_and, [i == 0 for i in ids])
        last = functools.reduce(jnp.logical_and, [i == g - 1 for i, g in zip(ids, grid)])

        @pl.when(first)
        def _():
            comm.start(c_ins, c_outs, sems)
        body(*ins, *outs, *scr)

        @pl.when(last)
        def _():
            comm.finish(c_ins, c_outs, sems)

    res = pl.pallas_call(kern, name=name, grid=grid, in_specs=list(in_specs) + [ANY] * n_ci,
                         out_specs=list(out_specs) + [ANY] * n_co, out_shape=list(out_shape) + comm.outs,
                         scratch_shapes=list(scratch) + comm.sem_shapes(),
                         compiler_params=_params(len(grid)))(*args, *comm.ins)
    plan.landed(name, list(res[n_out:]))
    return list(res[:n_out])


def _mm_body(dims, has_add, *refs):
    if has_add:
        a_ref, b_ref, add_ref, o_ref = refs
        total = _bdot(a_ref[...], b_ref[...], dims) + add_ref[...]
    else:
        a_ref, b_ref, o_ref = refs
        total = _bdot(a_ref[...], b_ref[...], dims)
    o_ref[...] = total.astype(o_ref.dtype)


def _mm_nt_body(j, n, dy_ref, w_ref, o_ref):
    total = _bdot(dy_ref[:, 0:n], w_ref[0], NT)
    for jj in range(1, j):
        total = total + _bdot(dy_ref[:, jj * n:(jj + 1) * n], w_ref[jj], NT)
    o_ref[...] = total.astype(o_ref.dtype)


def _mm_nn(a, w3, *, name, out_dtype=BF16, add=None, tm=1024, tn=None, out3=False, w_t=False, plan=None):
    m, kk = a.shape
    j, n = w3.shape[0], w3.shape[1 if w_t else 2]
    tm, tn = min(tm, m), n if tn is None else tn
    n_t = n // tn
    in_specs = [pl.BlockSpec((tm, kk), lambda i, jj: (i, 0)),
                pl.BlockSpec((None, tn, kk), lambda i, jj: (jj // n_t, jj % n_t, 0)) if w_t else
                pl.BlockSpec((None, kk, tn), lambda i, jj: (jj // n_t, 0, jj % n_t))]
    args = [a, w3]
    if add is not None:
        in_specs.append(pl.BlockSpec((tm, tn), lambda i, jj: (i, jj)))
        args.append(add)
    if out3:
        out_spec = pl.BlockSpec((None, tm, tn), lambda i, jj: (jj // n_t, i, jj % n_t))
        out_shape = jax.ShapeDtypeStruct((j, m, n), out_dtype)
    else:
        out_spec = pl.BlockSpec((tm, tn), lambda i, jj: (i, jj))
        out_shape = jax.ShapeDtypeStruct((m, j * n), out_dtype)
    return _call(
        functools.partial(_mm_body, NT if w_t else NN, add is not None), name=name, grid=(m // tm, j * n_t),
        in_specs=in_specs, out_specs=[out_spec], out_shape=[out_shape], scratch=[], args=args, plan=plan)[0]


def _mm_gathering(a, shard, *, name, out3=False, w_t=False, tm=1024):
    m, kk = a.shape
    n = shard.shape[0 if w_t else 1]
    tm = min(tm, m)
    n_i = m // tm

    def body(a_ref, shard_ref, o_ref, w_all, w_vmem, send_sems, recv_sems, copy_sems):
        jj, i = pl.program_id(0), pl.program_id(1)
        x, y, c = _place()
        me, sibling = (x, y, c), (x, y, 1 - c)
        chips = [(jnp.bitwise_xor(x, c), jnp.bitwise_xor(y, 1 - c)), (jnp.bitwise_xor(x, 1 - c), jnp.bitwise_xor(y, c)),
                 (1 - x, 1 - y)]
        sibling_chips = [chips[1], chips[0], chips[2]]

        def rows(block):
            return w_all.at[4 * block[0] + 2 * block[1] + block[2]]

        def remote(k, block, to, from_shard=False):
            return pltpu.make_async_remote_copy(
                src_ref=shard_ref if from_shard else rows(block), dst_ref=rows(block), send_sem=send_sems.at[k],
                recv_sem=recv_sems.at[k], device_id=to, device_id_type=MESH)

        def load(src):
            cp = pltpu.make_async_copy(src, w_vmem, copy_sems.at[1])
            cp.start()
            cp.wait()

        own = [remote(0, me, sibling, True)] + [remote(1 + j, me, (*chip, c), True) for j, chip in enumerate(chips)]
        passed = [remote(4 + j, (*chip, c), sibling) for j, chip in enumerate(chips)]
        local = pltpu.make_async_copy(shard_ref, rows(me), copy_sems.at[0])

        @pl.when(jnp.logical_and(i == 0, jj == 0))
        def _():
            local.start()
            own[0].start()
            own[1].start()
            load(shard_ref)

        @pl.when(jnp.logical_and(i == 0, jj == 1))
        def _():
            remote(0, sibling, me).wait_recv()
            load(rows(sibling))

        for j, chip in enumerate(chips):
            @pl.when(jnp.logical_and(i == 0, jj == 2 + 2 * j))
            def _():
                if j < 2:
                    own[1 + j].wait_send()
                    own[2 + j].start()
                remote(1 + j, (*chip, c), me).wait_recv()
                passed[j].start()
                load(rows((*chip, c)))

            @pl.when(jnp.logical_and(i == 0, jj == 3 + 2 * j))
            def _():
                block = (*sibling_chips[j], 1 - c)
                remote(4 + j, block, me).wait_recv()
                load(rows(block))

        o_ref[...] = _bdot(a_ref[...], w_vmem[...], NT if w_t else NN).astype(o_ref.dtype)

        @pl.when(jnp.logical_and(i == n_i - 1, jj == N_DEV - 1))
        def _():
            for cp in [own[0], own[3]] + passed:
                cp.wait_send()
            local.wait()

    def swept(jj):
        x, y, c = _place()
        first, second = 2 + 2 * c, 4 - 2 * c
        flips = (0b000, 0b001, first, second + 1, second, first + 1, 0b110, 0b111)
        return jnp.bitwise_xor(4 * x + 2 * y + c, sum(jnp.where(jj == k, f, 0) for k, f in enumerate(flips)))

    if out3:
        out_spec = pl.BlockSpec((None, tm, n), lambda jj, i: (swept(jj), i, 0))
        out_shape = jax.ShapeDtypeStruct((N_DEV, m, n), BF16)
    else:
        out_spec = pl.BlockSpec((tm, n), lambda jj, i: (i, swept(jj)))
        out_shape = jax.ShapeDtypeStruct((m, N_DEV * n), BF16)
    return pl.pallas_call(
        body, name=name, grid=(N_DEV, n_i),
        in_specs=[pl.BlockSpec((tm, kk), lambda jj, i: (i, 0)), ANY], out_specs=[out_spec, ANY],
        scratch_shapes=[pltpu.VMEM(shard.shape, shard.dtype), pltpu.SemaphoreType.DMA((7,)),
                        pltpu.SemaphoreType.DMA((7,)), pltpu.SemaphoreType.DMA((2,))],
        out_shape=[out_shape, jax.ShapeDtypeStruct((N_DEV,) + shard.shape, shard.dtype)],
        compiler_params=_params(2))(a, shard)


def _sigmoid(v):
    return 1.0 / (1.0 + jnp.exp(-v))


def _ffn_out_loss(gu3, w3, add, g, target, *, name, tm=256):
    j2, m, n = gu3.shape
    j = j2 // 2
    nn = w3.shape[2]
    tm = min(tm, m)

    def body(gu_ref, w_ref, add_ref, g_ref, t_ref, dx_ref, dxb_ref, dg_ref, loss_ref, act_ref):
        i = pl.program_id(0)
        xv = add_ref[...]
        for jj in range(j):
            gate = gu_ref[0, jj].astype(F32)
            act = (gate * _sigmoid(gate) * gu_ref[1, jj].astype(F32)).astype(BF16)
            act_ref[jj] = act
            xv = xv + _bdot(act, w_ref[jj], NN)
        gv = g_ref[...]
        r = lax.rsqrt(jnp.mean(xv * xv, axis=-1, keepdims=True) + NORM_EPS)
        xhat = xv * r
        err = xhat * gv - t_ref[...]
        _acc_rows(i, loss_ref, 0.5 * jnp.sum(jnp.mean(err * err, axis=-1, keepdims=True), axis=0, keepdims=True))
        dy = err * (1.0 / nn)
        dxhat = dy * gv
        dx = r * (dxhat - xhat * jnp.mean(dxhat * xhat, axis=-1, keepdims=True))
        dx_ref[...] = dx
        dxb_ref[...] = dx.astype(BF16)
        _acc_rows(i, dg_ref, jnp.sum(dy * xhat, axis=0, keepdims=True))

    row = pl.BlockSpec((tm, nn), lambda i: (i, 0))
    return _call(body, name=name, grid=(m // tm,),
                 in_specs=[pl.BlockSpec((2, j, tm, n), lambda i: (0, 0, i, 0)), pl.BlockSpec(w3.shape, lambda i: (0, 0, 0)),
                           row, pl.BlockSpec(g.shape, lambda i: (0, 0)), row],
                 out_specs=[row, row, pl.BlockSpec((8, nn), lambda i: (0, 0)), pl.BlockSpec((8, LANES), lambda i: (0, 0)),
                            pl.BlockSpec((j, tm, n), lambda i: (0, i, 0))],
                 out_shape=[jax.ShapeDtypeStruct((m, nn), F32), jax.ShapeDtypeStruct((m, nn), BF16),
                            jax.ShapeDtypeStruct((8, nn), F32), jax.ShapeDtypeStruct((8, LANES), F32),
                            jax.ShapeDtypeStruct((j, m, n), BF16)],
                 scratch=[], args=[gu3.reshape(2, j, m, n), w3, add, g, target])


def _ffn_out_bwd(dy, w3, gu3, *, name, tm=1024):
    m, nn = dy.shape
    j, n, _ = w3.shape
    tm = min(tm, m)

    def body(dy_ref, w_ref, gu_ref, dgu_ref):
        da = _bdot(dy_ref[...], w_ref[...], NT)
        gate = gu_ref[0].astype(F32)
        up = gu_ref[1].astype(F32)
        sg = _sigmoid(gate)
        silu = gate * sg
        dgu_ref[0] = (da * up * (sg + silu * (1.0 - sg))).astype(BF16)
        dgu_ref[1] = (da * silu).astype(BF16)

    out = _call(body, name=name, grid=(m // tm, j),
                in_specs=[pl.BlockSpec((tm, nn), lambda i, jj: (i, 0)),
                          pl.BlockSpec((None, n, nn), lambda i, jj: (jj, 0, 0)),
                          pl.BlockSpec((2, None, tm, n), lambda i, jj: (0, jj, i, 0))],
                out_specs=[pl.BlockSpec((2, None, tm, n), lambda i, jj: (0, jj, i, 0))],
                out_shape=[jax.ShapeDtypeStruct((2, j, m, n), BF16)], scratch=[],
                args=[dy, w3, gu3.reshape(2, j, m, n)])[0]
    return out.reshape(2 * j, m, n)


def _rms_fwd_tail(xv, g_ref, h_ref):
    r = lax.rsqrt(jnp.mean(xv * xv, axis=-1, keepdims=True) + NORM_EPS)
    h_ref[...] = (xv * r * g_ref[...]).astype(BF16)


def _rms_bwd_tail(i, dh, x_ref, g_ref, dres_ref, dx_ref, dxb_ref, dg_ref):
    xv = x_ref[...]
    r = lax.rsqrt(jnp.mean(xv * xv, axis=-1, keepdims=True) + NORM_EPS)
    xhat = xv * r
    dxhat = dh * g_ref[...]
    dx = r * (dxhat - xhat * jnp.mean(dxhat * xhat, axis=-1, keepdims=True))
    if dres_ref is not None:
        dx = dx + dres_ref[...]
    dx_ref[...] = dx
    dxb_ref[...] = dx.astype(BF16)
    _acc_rows(i, dg_ref, jnp.sum(dh * xhat, axis=0, keepdims=True))


def _mm_nt_rms(dy, w3, x, g, dres, *, name, dy3=False, w_nn=False, tm=512, plan=None):
    j = w3.shape[0]
    m, kk = x.shape
    n = dy.shape[2] if dy3 else dy.shape[1] // j
    tm = min(tm, m)

    def body(dy_ref, w_ref, x_ref, g_ref, *rest):
        dres_ref = rest[0] if dres is not None else None
        dx_ref, dxb_ref, dg_ref = rest[-3:]
        dh = None
        for jj in range(j):
            piece = dy_ref[jj] if dy3 else dy_ref[:, jj * n:(jj + 1) * n]
            part = _bdot(piece, w_ref[jj], NN if w_nn else NT)
            dh = part if dh is None else dh + part
        _rms_bwd_tail(pl.program_id(0), dh, x_ref, g_ref, dres_ref, dx_ref, dxb_ref, dg_ref)

    row = pl.BlockSpec((tm, kk), lambda i: (i, 0))
    in_specs = [pl.BlockSpec((j, tm, n), lambda i: (0, i, 0)) if dy3 else pl.BlockSpec((tm, j * n), lambda i: (i, 0)),
                pl.BlockSpec(w3.shape, lambda i: (0, 0, 0)), row, pl.BlockSpec(g.shape, lambda i: (0, 0))]
    args = [dy, w3, x, g]
    if dres is not None:
        in_specs.append(row)
        args.append(dres)
    return _call(body, name=name, grid=(m // tm,), in_specs=in_specs,
                 out_specs=[row, row, pl.BlockSpec((8, kk), lambda i: (0, 0))],
                 out_shape=[jax.ShapeDtypeStruct((m, kk), F32), jax.ShapeDtypeStruct((m, kk), BF16),
                            jax.ShapeDtypeStruct((8, kk), F32)], scratch=[], args=args, plan=plan)


def _mm_nn_rms(a, w, add, g, *, name, tm=1024):
    m, kk = a.shape
    nn = w.shape[1]
    tm = min(tm, m)

    def body(a_ref, w_ref, add_ref, g_ref, x_ref, h_ref):
        xv = _bdot(a_ref[...], w_ref[...], NN) + add_ref[...]
        x_ref[...] = xv
        _rms_fwd_tail(xv, g_ref, h_ref)

    row = pl.BlockSpec((tm, nn), lambda i: (i, 0))
    return _call(body, name=name, grid=(m // tm,),
                 in_specs=[pl.BlockSpec((tm, kk), lambda i: (i, 0)), pl.BlockSpec(w.shape, lambda i: (0, 0)), row,
                           pl.BlockSpec(g.shape, lambda i: (0, 0))],
                 out_specs=[row, row], out_shape=[jax.ShapeDtypeStruct((m, nn), F32), jax.ShapeDtypeStruct((m, nn), BF16)],
                 scratch=[], args=[a, w, add, g])


def _mix_out(br_a, br_b, proj, w, x, g, *, name, tm=512):
    s, d = br_a.shape
    tm = min(tm, s)

    def body(a_ref, b_ref, ga_ref, gb_ref, w_ref, x_ref, g_ref, x1_ref, h_ref, merged_ref):
        merged = (_sigmoid(ga_ref[...].astype(F32)) * a_ref[...].astype(F32)
                  + _sigmoid(gb_ref[...].astype(F32)) * b_ref[...].astype(F32)).astype(BF16)
        merged_ref[...] = merged
        xv = _bdot(merged, w_ref[...], NN) + x_ref[...]
        x1_ref[...] = xv
        _rms_fwd_tail(xv, g_ref, h_ref)

    row = pl.BlockSpec((tm, d), lambda i: (i, 0))
    return _call(body, name=name, grid=(s // tm,),
                 in_specs=[row, row, pl.BlockSpec((tm, d), lambda i: (i, 3)), pl.BlockSpec((tm, d), lambda i: (i, 4)),
                           pl.BlockSpec(w.shape, lambda i: (0, 0)), row, pl.BlockSpec(g.shape, lambda i: (0, 0))],
                 out_specs=[row, row, row],
                 out_shape=[jax.ShapeDtypeStruct((s, d), F32), jax.ShapeDtypeStruct((s, d), BF16),
                            jax.ShapeDtypeStruct((s, d), BF16)],
                 scratch=[], args=[br_a, br_b, proj, proj, w, x, g])


def _mm_tn_a3(a3, dy, *, name):
    j, t, n = a3.shape
    nn = dy.shape[1]
    return _call(functools.partial(_mm_body, TN, False), name=name, grid=(j,),
                 in_specs=[pl.BlockSpec((None, t, n), lambda jj: (jj, 0, 0)), pl.BlockSpec((t, nn), lambda jj: (0, 0))],
                 out_specs=[pl.BlockSpec((None, n, nn), lambda jj: (jj, 0, 0))],
                 out_shape=[jax.ShapeDtypeStruct((j, n, nn), BF16)], scratch=[], args=[a3, dy])[0]


def _mm_tn_dy3(a, dy3, *, name, tm=512):
    t, kk = a.shape
    j, _, n = dy3.shape
    tm = min(tm, kk)
    return _call(functools.partial(_mm_body, TN, False), name=name, grid=(kk // tm, j),
                 in_specs=[pl.BlockSpec((t, tm), lambda i, jj: (0, i)), pl.BlockSpec((None, t, n), lambda i, jj: (jj, 0, 0))],
                 out_specs=[pl.BlockSpec((None, tm, n), lambda i, jj: (jj, i, 0))],
                 out_shape=[jax.ShapeDtypeStruct((j, kk, n), BF16)], scratch=[], args=[a, dy3])[0]


def _mm_nt(dy, w3, *, name, out_dtype=BF16, tm=512, tn=1024, plan=None):
    m = dy.shape[0]
    j, kk, n = w3.shape
    tm, tn = min(tm, m), min(tn, kk)
    return _call(
        functools.partial(_mm_nt_body, j, n), name=name,
        grid=(m // tm, kk // tn),
        in_specs=[pl.BlockSpec((tm, j * n), lambda i, q: (i, 0)),
                  pl.BlockSpec((j, tn, n), lambda i, q: (0, q, 0))],
        out_specs=[pl.BlockSpec((tm, tn), lambda i, q: (i, q))],
        out_shape=[jax.ShapeDtypeStruct((m, kk), out_dtype)], scratch=[], args=[dy, w3], plan=plan)[0]


def _mm_tn(a, dy, n, *, name, out_dtype=BF16, tm=512, tn=None, k_tiles=None, plan=None):
    t, kk = a.shape
    j = dy.shape[1] // n
    tm, tn = min(tm, kk), n if tn is None else tn
    n_t = n // tn
    first, count = (0, kk // tm) if k_tiles is None else k_tiles
    return _call(
        functools.partial(_mm_body, TN, False), name=name,
        grid=(count, j * n_t),
        in_specs=[pl.BlockSpec((t, tm), lambda i, jj: (0, first + i)),
                  pl.BlockSpec((t, tn), lambda i, jj: (0, jj))],
        out_specs=[pl.BlockSpec((None, tm, tn), lambda i, jj: (jj // n_t, i, jj % n_t))],
        out_shape=[jax.ShapeDtypeStruct((j, count * tm, n), out_dtype)], scratch=[], args=[a, dy], plan=plan)[0]


def _rows(body, ins, outs, *, n_rows, tm, name, plan=None):
    tm = min(tm, n_rows)
    n_steps = n_rows // tm
    in_specs, args = [], []
    for arr, kind, width, block in ins:
        if kind == "row":
            in_specs.append(pl.BlockSpec((tm, width), functools.partial(lambda i, b: (i, b), b=block)))
        elif kind == "prev":
            in_specs.append(pl.BlockSpec((tm, width), functools.partial(lambda i, b: (jnp.maximum(i - 1, 0), b), b=block)))
        elif kind == "next":
            in_specs.append(pl.BlockSpec((tm, width), functools.partial(lambda i, b: (jnp.minimum(i + 1, n_steps - 1), b), b=block)))
        else:
            in_specs.append(pl.BlockSpec(arr.shape, functools.partial(lambda i, nd: (0,) * nd, nd=arr.ndim)))
        args.append(arr)
    out_specs, out_shape = [], []
    for shape, dtype, kind in outs:
        if kind == "row":
            out_specs.append(pl.BlockSpec((tm, shape[1]), lambda i: (i, 0)))
        else:
            out_specs.append(pl.BlockSpec(shape, functools.partial(lambda i, nd: (0,) * nd, nd=len(shape))))
        out_shape.append(jax.ShapeDtypeStruct(shape, dtype))

    def kern(*refs):
        body(pl.program_id(0), n_steps, *refs)

    return _call(kern, name=name, grid=(n_steps,), in_specs=in_specs, out_specs=out_specs, out_shape=out_shape,
                 scratch=[], args=args, plan=plan)


def _acc_rows(i, ref, value):
    @pl.when(i == 0)
    def _():
        ref[...] = jnp.zeros_like(ref)
    ref[...] += jnp.broadcast_to(value, ref.shape)


def _rms_fwd(x, g, *, name, tm=512):
    s, d = x.shape

    def body(i, n, x_ref, g_ref, h_ref):
        _rms_fwd_tail(x_ref[...], g_ref, h_ref)

    return _rows(body, [(x, "row", d, 0), (g, "full", 0, 0)], [((s, d), BF16, "row")], n_rows=s, tm=tm, name=name)[0]


def _rms_bwd(x, g, dh, dres, *, name, tm=512, plan=None):
    s, d = x.shape

    def body(i, n, x_ref, g_ref, dh_ref, dres_ref, dx_ref, dxb_ref, dg_ref):
        _rms_bwd_tail(i, dh_ref[...].astype(F32), x_ref, g_ref, dres_ref, dx_ref, dxb_ref, dg_ref)

    return _rows(body, [(x, "row", d, 0), (g, "full", 0, 0), (dh, "row", d, 0), (dres, "row", d, 0)],
                 [((s, d), F32, "row"), ((s, d), BF16, "row"), ((8, d), F32, "acc")],
                 n_rows=s, tm=tm, name=name, plan=plan)


def _gates_bwd(dmerged, br_a, br_b, proj, *, name, tm=512):
    s, d = br_a.shape

    def body(i, n, dm_ref, a_ref, b_ref, ga_ref, gb_ref, da_ref, db_ref, dg_ref):
        dm = dm_ref[...].astype(F32)
        sa = _sigmoid(ga_ref[...].astype(F32))
        sb = _sigmoid(gb_ref[...].astype(F32))
        da_ref[...] = (dm * sa).astype(BF16)
        db_ref[...] = (dm * sb).astype(BF16)
        dg_ref[:, :d] = (dm * a_ref[...].astype(F32) * sa * (1.0 - sa)).astype(BF16)
        dg_ref[:, d:] = (dm * b_ref[...].astype(F32) * sb * (1.0 - sb)).astype(BF16)

    return _rows(body, [(dmerged, "row", d, 0), (br_a, "row", d, 0), (br_b, "row", d, 0),
                        (proj, "row", d, 3), (proj, "row", d, 4)],
                 [((s, d), BF16, "row"), ((s, d), BF16, "row"), ((s, 2 * d), BF16, "row")],
                 n_rows=s, tm=tm, name=name)


def _shift_down(cur, prev, k, first):
    row = lax.broadcasted_iota(jnp.int32, cur.shape, 0)
    out = jnp.where(row >= k, pltpu.roll(cur, k, 0), pltpu.roll(prev, k, 0))
    return jnp.where(jnp.logical_and(first, row < k), 0.0, out)


def _shift_up(cur, nxt, k, last):
    tm = cur.shape[0]
    row = lax.broadcasted_iota(jnp.int32, cur.shape, 0)
    out = jnp.where(row < tm - k, pltpu.roll(cur, tm - k, 0), pltpu.roll(nxt, tm - k, 0))
    return jnp.where(jnp.logical_and(last, row >= tm - k), 0.0, out)


def _conv_fwd(proj, conv_w, *, name, tm=512):
    s = proj.shape[0]
    c = CONV_WIDTH

    def body(i, n, u_ref, gb_ref, gc_ref, up_ref, gcp_ref, w_ref, y_ref):
        cu = gc_ref[...].astype(F32) * u_ref[...].astype(F32)
        cup = gcp_ref[...].astype(F32) * up_ref[...].astype(F32)
        first = i == 0
        y = (w_ref[0:1, :] * _shift_down(cu, cup, 2, first) + w_ref[1:2, :] * _shift_down(cu, cup, 1, first)
             + w_ref[2:3, :] * cu)
        y_ref[...] = (gb_ref[...].astype(F32) * y).astype(BF16)

    return _rows(body, [(proj, "row", c, 3), (proj, "row", c, 4), (proj, "row", c, 5),
                        (proj, "prev", c, 3), (proj, "prev", c, 5), (conv_w, "full", 0, 0)],
                 [((s, c), BF16, "row")], n_rows=s, tm=tm, name=name)[0]


def _conv_bwd(dy_b, proj, conv_w, *, name, tm=512, plan=None):
    s = proj.shape[0]
    c = CONV_WIDTH

    def body(i, n, dy_ref, u_ref, gb_ref, gc_ref, up_ref, gcp_ref, dyn_ref, gbn_ref, w_ref, d_ref, dw_ref):
        first, last = i == 0, i == n - 1
        u = u_ref[...].astype(F32)
        gb = gb_ref[...].astype(F32)
        gc = gc_ref[...].astype(F32)
        cu = gc * u
        cup = gcp_ref[...].astype(F32) * up_ref[...].astype(F32)
        cu1 = _shift_down(cu, cup, 1, first)
        cu2 = _shift_down(cu, cup, 2, first)
        conv = w_ref[0:1, :] * cu2 + w_ref[1:2, :] * cu1 + w_ref[2:3, :] * cu
        dy = dy_ref[...].astype(F32)
        dyc = dy * gb
        dycn = dyn_ref[...].astype(F32) * gbn_ref[...].astype(F32)
        dcu = (w_ref[2:3, :] * dyc + w_ref[1:2, :] * _shift_up(dyc, dycn, 1, last)
               + w_ref[0:1, :] * _shift_up(dyc, dycn, 2, last))
        d_ref[:, 0:c] = (dcu * gc).astype(BF16)
        d_ref[:, c:2 * c] = (dy * conv).astype(BF16)
        d_ref[:, 2 * c:3 * c] = (dcu * u).astype(BF16)
        row = lax.broadcasted_iota(jnp.int32, (8, c), 0)
        dw = (jnp.where(row == 0, jnp.sum(dyc * cu2, axis=0, keepdims=True), 0.0)
              + jnp.where(row == 1, jnp.sum(dyc * cu1, axis=0, keepdims=True), 0.0)
              + jnp.where(row == 2, jnp.sum(dyc * cu, axis=0, keepdims=True), 0.0))

        @pl.when(first)
        def _():
            dw_ref[...] = jnp.zeros_like(dw_ref)
        dw_ref[...] += dw

    return _rows(body, [(dy_b, "row", c, 0), (proj, "row", c, 3), (proj, "row", c, 4), (proj, "row", c, 5),
                        (proj, "prev", c, 3), (proj, "prev", c, 5), (dy_b, "next", c, 0), (proj, "next", c, 4),
                        (conv_w, "full", 0, 0)],
                 [((s, 3 * c), BF16, "row"), ((8, c), F32, "acc")], n_rows=s, tm=tm, name=name, plan=plan)


def _mem_probs(q, k, scale):
    sc = _bdot(q, k, NT) * scale
    sc = sc - jnp.max(sc, axis=-1, keepdims=True)
    p = jnp.exp(sc)
    return p / jnp.sum(p, axis=-1, keepdims=True)


def _memattn_fwd(qm, kv, *, name, tm=512):
    s, d = qm.shape
    hd = d // MEM_HEADS
    scale = 1.0 / math.sqrt(hd)

    def body(i, n, q_ref, kv_ref, o_ref):
        for h in range(MEM_HEADS):
            cols = slice(h * hd, (h + 1) * hd)
            p = _mem_probs(q_ref[:, cols], kv_ref[:, cols], scale)
            o_ref[:, cols] = _bdot(p, kv_ref[:, d + h * hd:d + (h + 1) * hd], NN).astype(BF16)

    return _rows(body, [(qm, "row", d, 0), (kv, "full", 0, 0)], [((s, d), BF16, "row")], n_rows=s, tm=tm, name=name)[0]


def _memattn_bwd(dom, qm, kv, *, name, tm=512):
    s, d = qm.shape
    hd = d // MEM_HEADS
    scale = 1.0 / math.sqrt(hd)

    def body(i, n, do_ref, q_ref, kv_ref, dq_ref, dkv_ref):
        @pl.when(i == 0)
        def _():
            dkv_ref[...] = jnp.zeros_like(dkv_ref)
        for h in range(MEM_HEADS):
            cols = slice(h * hd, (h + 1) * hd)
            vcols = slice(d + h * hd, d + (h + 1) * hd)
            q, k, v, do = q_ref[:, cols], kv_ref[:, cols], kv_ref[:, vcols], do_ref[:, cols]
            p = _mem_probs(q, k, scale)
            dp = _bdot(do, v, NT)
            ds = p * (dp - jnp.sum(dp * p, axis=-1, keepdims=True)) * scale
            dq_ref[:, cols] = _bdot(ds, k, NN).astype(BF16)
            dkv_ref[:, cols] += _bdot(ds, q, TN)
            dkv_ref[:, vcols] += _bdot(p, do, TN)

    return _rows(body, [(dom, "row", d, 0), (qm, "row", d, 0), (kv, "full", 0, 0)],
                 [((s, d), BF16, "row"), (kv.shape, F32, "acc")], n_rows=s, tm=tm, name=name)


def _sb_consts(t):
    row = lax.broadcasted_iota(jnp.int32, (t, t), 0)
    col = lax.broadcasted_iota(jnp.int32, (t, t), 1)
    lane = lax.broadcasted_iota(jnp.int32, (t, LANES), 1)
    return row, col, lane < SB_HEAD_DIM


def _log_fail(z):
    return jnp.minimum(-z, 0.0) - jnp.log(1.0 + jnp.exp(-jnp.abs(z)))


def _tri_sum(v, tri):
    hi = v.astype(BF16)
    lo = (v - hi.astype(F32)).astype(BF16)
    return _bdot(hi, tri, NN) + _bdot(lo, tri, NN)


def _sb_fwd(proj, *, name, plan=None):
    s = proj.shape[0]
    t = SB_TILE
    n_q = s // t
    scale = 1.0 / math.sqrt(SB_HEAD_DIM)
    k_blk, v_blk = SB_WIDTH // LANES, 2 * SB_WIDTH // LANES

    def body(q_ref, k_ref, v_ref, o_ref, c_ref, first_ref, acc_ref):
        i = pl.program_id(1)
        row, col, head0 = _sb_consts(t)
        later = (row > col).astype(BF16)
        valid = col < row
        qs = q_ref[...] * scale
        q2 = (jnp.where(head0, qs, 0), jnp.where(head0, 0, qs))

        def tile(kb, carry, diag):
            kt = k_ref[pl.ds(pl.multiple_of(kb * t, t), t), :]
            vt = v_ref[pl.ds(pl.multiple_of(kb * t, t), t), :]
            heads = range(2)
            z = [_bdot(q2[h], kt, NT) for h in heads]
            lf = [_log_fail(z[h]) for h in heads]
            if diag:
                lf = [jnp.where(valid, lf[h], 0.0) for h in heads]
            cum = [_tri_sum(lf[h], later) for h in heads]
            w = [jnp.exp(z[h] + lf[h] + cum[h] + carry[h]) for h in heads]
            if diag:
                w = [jnp.where(valid, w[h], 0.0) for h in heads]
            for h in heads:
                acc_ref[h] += _bdot(w[h], vt, NN)
            return tuple(carry[h] + cum[h][:, 0:1] + lf[h][:, 0:1] for h in heads)

        acc_ref[...] = jnp.zeros_like(acc_ref)
        zero = jnp.zeros((t, 1), F32)

        def alive(carry):
            return (jnp.maximum(jnp.max(carry[0]), jnp.max(carry[1])) > -SB_DEAD).astype(jnp.int32)

        def step(state):
            kb, _, c0, c1 = state
            new = tile(kb, (c0, c1), False)
            return kb - 1, alive(new), new[0], new[1]

        carry = tile(i, (zero, zero), True)
        kb, _, c0, c1 = lax.while_loop(lambda st: jnp.logical_and(st[0] >= 0, st[1] > 0), step,
                                       (i - 1, alive(carry), carry[0], carry[1]))
        o_ref[...] = jnp.where(head0, acc_ref[0], acc_ref[1]).astype(BF16)
        c_ref[...] = jnp.where(lax.broadcasted_iota(jnp.int32, (t, 2), 1) == 0, c0, c1)
        first_ref[pl.program_id(0), i] = (kb + 1).astype(F32)

    return _call(
        body, name=name, grid=(SB_HEADS // 2, n_q),
        in_specs=[pl.BlockSpec((t, LANES), lambda p, i: (i, p)),
                  pl.BlockSpec((s, LANES), lambda p, i: (0, k_blk + p)),
                  pl.BlockSpec((s, LANES), lambda p, i: (0, v_blk + p))],
        out_specs=[pl.BlockSpec((t, LANES), lambda p, i: (i, p)),
                   pl.BlockSpec((None, t, 2), lambda p, i: (p, i, 0)),
                   pl.BlockSpec(memory_space=pltpu.SMEM)],
        out_shape=[jax.ShapeDtypeStruct((s, SB_WIDTH), BF16), jax.ShapeDtypeStruct((SB_HEADS // 2, s, 2), F32),
                   jax.ShapeDtypeStruct((SB_HEADS // 2, n_q), F32)],
        scratch=[pltpu.VMEM((2, t, LANES), F32)], args=[proj, proj, proj], plan=plan)


def _sb_bwd(proj, do_a, ctot, first, *, name, plan=None):
    s = proj.shape[0]
    t = SB_TILE
    n_q = s // t
    scale = 1.0 / math.sqrt(SB_HEAD_DIM)
    k_blk, v_blk = SB_WIDTH // LANES, 2 * SB_WIDTH // LANES

    def body(q_ref, k_ref, v_ref, do_ref, c_ref, first_ref, dq_ref, dk_ref, dv_ref, dq_acc, dk_acc, dv_acc):
        i = pl.program_id(1)
        kb0 = jnp.clip(first_ref[pl.program_id(0), i].astype(jnp.int32), 0, i)
        row, col, head0 = _sb_consts(t)
        upto = (row <= col).astype(BF16)
        before = (row < col).astype(BF16)
        valid = col < row
        qs = q_ref[...] * scale
        q2 = (jnp.where(head0, qs, 0), jnp.where(head0, 0, qs))
        do = do_ref[...]
        do2 = (jnp.where(head0, do, 0), jnp.where(head0, 0, do))
        ctot2 = (c_ref[:, 0:1], c_ref[:, 1:2])

        @pl.when(i == 0)
        def _():
            dk_acc[...] = jnp.zeros_like(dk_acc)
            dv_acc[...] = jnp.zeros_like(dv_acc)
        dq_acc[...] = jnp.zeros_like(dq_acc)

        def tile(kb, carry, diag):
            rows = pl.ds(pl.multiple_of(kb * t, t), t)
            kt = k_ref[rows, :]
            vt = v_ref[rows, :]
            heads = range(2)
            lf_before, g_before = carry[0::2], carry[1::2]
            z = [_bdot(q2[h], kt, NT) for h in heads]
            dw = [_bdot(do2[h], vt, NT) for h in heads]
            lf = [_log_fail(z[h]) for h in heads]
            if diag:
                lf = [jnp.where(valid, lf[h], 0.0) for h in heads]
            cum = [_tri_sum(lf[h], upto) for h in heads]
            beta = [jnp.exp(z[h] + lf[h]) for h in heads]
            w = [beta[h] * jnp.exp(ctot2[h] - lf_before[h] - cum[h]) for h in heads]
            if diag:
                w = [jnp.where(valid, w[h], 0.0) for h in heads]
            g = [w[h] * dw[h] for h in heads]
            g_sum = [g_before[h] + _bdot(g[h], before, NN) for h in heads]
            for h in heads:
                dv_acc[rows, :] += _bdot(w[h], do2[h], TN)
            dz = [g[h] * jnp.exp(lf[h]) - beta[h] * g_sum[h] for h in heads]
            if diag:
                dz = [jnp.where(valid, dz[h], 0.0) for h in heads]
            for h in heads:
                dq_acc[h] += _bdot(dz[h], kt, NN)
                dk_acc[rows, :] += _bdot(dz[h], q2[h], TN)
            t_last = slice(t - 1, t)
            new = []
            for h in heads:
                new += [lf_before[h] + cum[h][:, t_last], g_sum[h][:, t_last] + g[h][:, t_last]]
            return tuple(new)

        zero = jnp.zeros((t, 1), F32)
        carry = lax.fori_loop(kb0, i, lambda n, c: tile(n, c, False), (zero,) * 4)
        tile(i, carry, True)
        dq_ref[...] = (jnp.where(head0, dq_acc[0], dq_acc[1]) * scale).astype(BF16)

        @pl.when(i == n_q - 1)
        def _():
            dk_ref[...] = dk_acc[...].astype(BF16)
            dv_ref[...] = dv_acc[...].astype(BF16)

    outs = _call(
        body, name=name, grid=(SB_HEADS // 2, n_q),
        in_specs=[pl.BlockSpec((t, LANES), lambda p, i: (i, p)),
                  pl.BlockSpec((s, LANES), lambda p, i: (0, k_blk + p)),
                  pl.BlockSpec((s, LANES), lambda p, i: (0, v_blk + p)),
                  pl.BlockSpec((t, LANES), lambda p, i: (i, p)),
                  pl.BlockSpec((None, t, 2), lambda p, i: (p, i, 0)),
                  pl.BlockSpec(memory_space=pltpu.SMEM)],
        out_specs=[pl.BlockSpec((t, LANES), lambda p, i: (i, p)),
                   pl.BlockSpec((s, LANES), lambda p, i: (0, p)),
                   pl.BlockSpec((s, LANES), lambda p, i: (0, p))],
        out_shape=[jax.ShapeDtypeStruct((s, SB_WIDTH), BF16)] * 3,
        scratch=[pltpu.VMEM((2, t, LANES), F32), pltpu.VMEM((s, LANES), F32), pltpu.VMEM((s, LANES), F32)],
        args=[proj, proj, proj, do_a, ctot, first], plan=plan)
    return jnp.concatenate(outs, axis=1)


def _mm_gathered(a, key, plan, *, name, out3=False, w_t=False):
    src = plan.gathering(key)
    if src is None:
        return _mm_nn(a, plan.weight(key), name=name, out3=out3, w_t=w_t)
    out, w_all = _mm_gathering(a, src, name=name, out3=out3, w_t=w_t)
    plan.set_weight(key, w_all)
    return out


def _local_step(x, mem, target, gains, plan):
    g_mix, g_memq, g_memkv, g_ffn, g_fin = gains
    d = x.shape[1]

    h0 = _rms_fwd(x, g_mix, name="rms_mix")
    proj = _mm_gathered(h0, "in", plan, name="mm_in")
    w_in = plan.weight("in")
    o_a, ctot, first = _sb_fwd(proj, name="sb_fwd", plan=plan)
    conv_w = plan.weight("conv")
    y_b = _conv_fwd(proj, conv_w, name="conv_fwd")
    w_a, w_b, w_mix = plan.weight("a"), plan.weight("b"), plan.weight("mix")
    br_a = _mm_nn(o_a, w_a, name="mm_branch_a")
    br_b = _mm_nn(y_b, w_b, name="mm_branch_b")
    x1, hq, merged = _mix_out(br_a, br_b, proj, w_mix[0], x, g_memq, name="mm_mix")
    w_mq, w_kv, w_mo = plan.weight("mq"), plan.weight("kv"), plan.weight("mo")
    qm = _mm_nn(hq, w_mq, name="mm_memq")
    mn = _rms_fwd(mem, g_memkv, name="rms_memkv")
    kv = _mm_nn(mn, w_kv, name="mm_memkv")
    om = _memattn_fwd(qm, kv, name="memattn_fwd")
    x2, hf = _mm_nn_rms(om, w_mo[0], x1, g_ffn, name="mm_memo")
    gu = _mm_gathered(hf, "fi", plan, name="mm_ffn_in", out3=True, w_t=True)
    w_fi, w_fo = plan.weight("fi"), plan.weight("fo")
    dx3, dx3b, dg_fin, loss, act = _ffn_out_loss(gu, w_fo, x2, g_fin, target, name="mm_ffn_out")

    plan.grad("fo", _mm_tn_a3(act, dx3b, name="mm_d_w_ffn_out"))
    dgu = _ffn_out_bwd(dx3b, w_fo, gu, name="mm_d_act")
    plan.grad("fi", _mm_tn_a3(dgu, hf, name="mm_d_w_ffn_in"))
    dx2, dx2b, dg_ffn = _mm_nt_rms(dgu, w_fi, x2, g_ffn, dx3, name="mm_d_hf", dy3=True, w_nn=True, tm=256, plan=plan)

    plan.grad("mo", _mm_tn(om, dx2b, d, name="mm_d_w_memo"))
    dom = _mm_nt(dx2b, w_mo, name="mm_d_om")
    dqm, dkv = _memattn_bwd(dom, qm, kv, name="memattn_bwd")
    plan.grad("mq", _mm_tn(hq, dqm, d, name="mm_d_w_memq"))
    dx1, dx1b, dg_memq = _mm_nt_rms(dqm, w_mq, x1, g_memq, dx2, name="mm_d_hq")
    plan.grad("kv", _mm_tn(mn, dkv, w_kv.shape[2], name="mm_d_w_memkv"))
    _, _, dg_memkv = _mm_nt_rms(dkv, w_kv, mem, g_memkv, None, name="mm_d_mn")

    plan.grad("mix", _mm_tn(merged, dx1b, d, name="mm_d_w_mix"))
    dmerged = _mm_nt(dx1b, w_mix, name="mm_d_merged", plan=plan)
    dbr_a, dbr_b, dgab = _gates_bwd(dmerged, br_a, br_b, proj, name="gates_bwd")
    plan.grad("a", _mm_tn(o_a, dbr_a, d, name="mm_d_w_branch_a"))
    do_a = _mm_nt(dbr_a, w_a, name="mm_d_o_a")
    plan.grad("b", _mm_tn(y_b, dbr_b, d, name="mm_d_w_branch_b"))
    dy_b = _mm_nt(dbr_b, w_b, name="mm_d_y_b")
    dconv, dconv_w = _conv_bwd(dy_b, proj, conv_w, name="conv_bwd", plan=plan)
    dqkv = _sb_bwd(proj, do_a, ctot, first, name="sb_bwd", plan=plan)
    dproj = jnp.concatenate([dqkv, dconv, dgab], axis=1)
    plan.grad("in0", _mm_tn(h0, dproj, w_in.shape[2], name="mm_d_w_in0", k_tiles=(0, 1)))
    plan.grad("in1", _mm_tn(h0, dproj, w_in.shape[2], name="mm_d_w_in1", k_tiles=(1, 1), plan=plan))
    dh0 = _mm_nt(dproj, w_in, name="mm_d_h0", out_dtype=F32, plan=plan)
    dx0, _, dg_mix = _rms_bwd(x, g_mix, dh0, dx1, name="rms_mix_bwd", plan=plan)

    return dx0, (dg_mix, dg_memq, dg_memkv, dg_ffn, dg_fin, dconv_w, loss)


def _row_tile(a, target=512):
    tm = min(a, target)
    while a % tm:
        tm -= 8
    return tm


def _sum_with_sibling(part, recv, core, *, name):
    _, a, b = part.shape
    tm = _row_tile(a)

    def body(core_ref, p_ref, r_ref, o_ref):
        o_ref[...] = (p_ref[...].astype(F32) + r_ref[...].astype(F32)).astype(o_ref.dtype)

    return pl.pallas_call(
        body, name=name,
        grid_spec=pltpu.PrefetchScalarGridSpec(
            num_scalar_prefetch=1, grid=(N_CHIP, a // tm),
            in_specs=[pl.BlockSpec((None, tm, b), lambda q, i, core_ref: (2 * q + core_ref[0], i, 0)),
                      pl.BlockSpec((None, tm, b), lambda q, i, core_ref: (q, i, 0))],
            out_specs=pl.BlockSpec((None, tm, b), lambda q, i, core_ref: (q, i, 0))),
        out_shape=jax.ShapeDtypeStruct((N_CHIP, a, b), part.dtype), compiler_params=_params(2))(core, part, recv)


def _adam_math(wv, g, m, v):
    m = ADAM_B1 * m + (1.0 - ADAM_B1) * g
    v = ADAM_B2 * v + (1.0 - ADAM_B2) * (g * g)
    m_hat = m / (1.0 - ADAM_B1 ** ADAM_STEP)
    v_hat = v / (1.0 - ADAM_B2 ** ADAM_STEP)
    delta = -ADAM_LR * (m_hat / (jnp.sqrt(v_hat) + ADAM_EPS) + ADAM_WD * wv)
    return delta, m, v


def _adam_sharded(wv, m, v, own, recv, chip, *, name):
    a, b = wv.shape
    tm = _row_tile(a)

    def body(chip_ref, w_ref, m_ref, v_ref, own_ref, recv_ref, g_ref, d_ref, nm_ref, nv_ref):
        g = own_ref[...].astype(F32)
        for j in range(3):
            g = g + recv_ref[j].astype(F32)
        delta, nm, nv = _adam_math(w_ref[...], g, m_ref[...], v_ref[...])
        g_ref[...] = g
        d_ref[...] = delta
        nm_ref[...] = nm
        nv_ref[...] = nv

    tile = pl.BlockSpec((tm, b), lambda i, chip_ref: (i, 0))
    return pl.pallas_call(
        body, name=name,
        grid_spec=pltpu.PrefetchScalarGridSpec(
            num_scalar_prefetch=1, grid=(a // tm,),
            in_specs=[tile, tile, tile,
                      pl.BlockSpec((None, tm, b), lambda i, chip_ref: (chip_ref[0], i, 0)),
                      pl.BlockSpec((3, tm, b), lambda i, chip_ref: (0, i, 0))],
            out_specs=[tile] * 4),
        out_shape=[jax.ShapeDtypeStruct((a, b), F32)] * 4, compiler_params=_params(1))(chip, wv, m, v, own, recv)


def _sum_devices(gathered, *, name):
    _, r, c = gathered.shape

    def body(g_ref, o_ref):
        total = g_ref[0]
        for j in range(1, N_DEV):
            total = total + g_ref[j]
        o_ref[...] = total

    return pl.pallas_call(body, name=name, out_shape=jax.ShapeDtypeStruct((r, c), F32))(gathered)


def _adam_small(wv, g, m, v, *, name):
    def body(w_ref, g_ref, m_ref, v_ref, d_ref, nm_ref, nv_ref):
        delta, nm, nv = _adam_math(w_ref[...], g_ref[...], m_ref[...], v_ref[...])
        d_ref[...] = delta
        nm_ref[...] = nm
        nv_ref[...] = nv

    return pl.pallas_call(body, name=name, out_shape=[jax.ShapeDtypeStruct(wv.shape, F32)] * 3)(wv, g, m, v)


BIG = ("in", "a", "b", "mix", "mq", "kv", "mo", "fi", "fo")
ROW_SHARDED = ("mix", "mq", "mo")
UNSHARDED = ("a", "b")
FFN_GROUPS = 4
SMALL_ROWS = 16


class _Plan:
    FUSED = ("in", "fi")
    GATHER_ON = {"sb_fwd": ("a", "b", "mix", "kv", "mq", "mo", "fo", "conv")}
    SIBLING_ON = {"mm_d_hf": ("fo", "fi"), "mm_d_merged": ("mo", "mq", "kv"), "conv_bwd": ("mix", "a", "b"),
                  "mm_d_w_in1": ("in0",), "mm_d_h0": ("in1",)}
    CHIPS_ON = {"sb_bwd": ("fo", "fi", "mo", "mq", "kv", "mix", "a", "b"), "mm_d_h0": ("in0",),
                "rms_mix_bwd": ("in1",)}

    def __init__(self, shards, core):
        self.shards, self.core = shards, core
        self.w, self.parts, self.chip_sums, self.from_chips = {}, {}, {}, {}

    def gathering(self, k):
        return self.shards[k] if k in self.FUSED else None

    def comm(self, name):
        comms = []
        if name in self.GATHER_ON:
            comms.append(_gather_comm([self.shards[k] for k in self.GATHER_ON[name]]))
        if name in self.SIBLING_ON:
            comms.append(_sibling_comm([self.parts[k] for k in self.SIBLING_ON[name]]))
        if name in self.CHIPS_ON:
            comms.append(_chips_comm([self.chip_sums[k] for k in self.CHIPS_ON[name]]))
        return _join_comms(comms) if comms else None

    def landed(self, name, outs):
        outs = list(outs)
        for k in self.GATHER_ON.get(name, ()):
            self.set_weight(k, outs.pop(0))
        for k in self.SIBLING_ON.get(name, ()):
            self.chip_sums[k] = _sum_with_sibling(self.parts[k], outs.pop(0), self.core, name="sum_with_sibling_" + k)
        for k in self.CHIPS_ON.get(name, ()):
            self.from_chips[k] = outs.pop(0)

    def set_weight(self, k, gathered):
        _, a, b = gathered.shape
        if k in ROW_SHARDED:
            gathered = gathered.reshape(1, N_DEV * a, b)
        elif k in UNSHARDED:
            gathered = jnp.transpose(gathered, (1, 0, 2)).reshape(1, a, N_DEV * b)
        elif k == "fo":
            gathered = gathered.reshape(FFN_GROUPS, N_DEV * a // FFN_GROUPS, b)
        elif k == "conv":
            n_conv = CONV_WIDTH // N_DEV
            gathered = jnp.transpose(gathered[:, :3, :n_conv], (1, 0, 2)).reshape(3, CONV_WIDTH)
        self.w[k] = gathered

    def weight(self, k):
        return self.w[k]

    def grad(self, k, g):
        _, a, b = g.shape
        if k in ROW_SHARDED:
            g = g.reshape(N_DEV, a // N_DEV, b)
        elif k in UNSHARDED:
            g = jnp.transpose(g.reshape(a, N_DEV, b // N_DEV), (1, 0, 2))
        elif k == "fo":
            g = g.reshape(N_DEV, FFN_GROUPS * a // N_DEV, b)
        self.parts[k] = g


def kernel(x, mem, norm_mix, w_in, conv_w, w_branch_a, w_branch_b, w_mix_out, norm_mem_q, norm_mem_kv, w_mem_q, w_mem_kv, w_mem_o, norm_ffn, w_ffn_in, w_ffn_out, norm_final, loss_target, m_norm_mix, m_w_in, m_conv_w, m_w_branch_a, m_w_branch_b, m_w_mix_out, m_norm_mem_q, m_norm_mem_kv, m_w_mem_q, m_w_mem_kv, m_w_mem_o, m_norm_ffn, m_w_ffn_in, m_w_ffn_out, m_norm_final, v_norm_mix, v_w_in, v_conv_w, v_w_branch_a, v_w_branch_b, v_w_mix_out, v_norm_mem_q, v_norm_mem_kv, v_w_mem_q, v_w_mem_kv, v_w_mem_o, v_norm_ffn, v_w_ffn_in, v_w_ffn_out, v_norm_final):
    d = x.shape[-1]
    xi, yi, ci = lax.axis_index("x"), lax.axis_index("y"), lax.axis_index("c")
    core = jnp.reshape(ci, (1,)).astype(jnp.int32)
    chip = jnp.reshape(2 * xi + yi, (1,)).astype(jnp.int32)
    dev = 4 * xi + 2 * yi + ci

    big_w = dict(zip(BIG, (w_in, w_branch_a, w_branch_b, w_mix_out, w_mem_q, w_mem_kv, w_mem_o, w_ffn_in, w_ffn_out)))
    big_m = dict(zip(BIG, (m_w_in, m_w_branch_a, m_w_branch_b, m_w_mix_out, m_w_mem_q, m_w_mem_kv, m_w_mem_o, m_w_ffn_in, m_w_ffn_out)))
    big_v = dict(zip(BIG, (v_w_in, v_w_branch_a, v_w_branch_b, v_w_mix_out, v_w_mem_q, v_w_mem_kv, v_w_mem_o, v_w_ffn_in, v_w_ffn_out)))

    flip = lambda t, k: jnp.transpose(t) if k == "fi" else t
    shards = {k: flip(big_w[k][0], k).astype(BF16) for k in BIG}
    n_conv = conv_w.shape[-1]
    shards["conv"] = jnp.zeros((8, LANES), F32).at[:3, :n_conv].set(conv_w[0])
    plan = _Plan(shards, core)

    gains = (norm_mix, norm_mem_q, norm_mem_kv, norm_ffn, norm_final.reshape(1, d))
    dx0, small = _local_step(x[0], mem[0], loss_target[0], gains, plan)

    grads, deltas, new_m, new_v = {}, {}, {}, {}
    for k in BIG:
        lead = big_w[k].shape
        wv, mv, vv = flip(big_w[k][0], k), flip(big_m[k][0], k), flip(big_v[k][0], k)
        if k == "in":
            half = wv.shape[0] // 2
            lo = _adam_sharded(wv[:half], mv[:half], vv[:half], plan.chip_sums["in0"], plan.from_chips["in0"], chip,
                               name="adam_in0")
            hi = _adam_sharded(wv[half:], mv[half:], vv[half:], plan.chip_sums["in1"], plan.from_chips["in1"], chip,
                               name="adam_in1")
            outs = [jnp.concatenate(pair, axis=0) for pair in zip(lo, hi)]
        else:
            outs = _adam_sharded(wv, mv, vv, plan.chip_sums[k], plan.from_chips[k], chip, name="adam_" + k)
        grads[k], deltas[k], new_m[k], new_v[k] = (flip(t, k).reshape(lead) for t in outs)

    dg_mix, dg_memq, dg_memkv, dg_ffn, dg_fin, dconv_w, loss = small
    conv_rows = jnp.zeros((3, d), F32).at[:, :CONV_WIDTH].set(dconv_w[:3])
    block = jnp.concatenate([dg_mix[:1], dg_memq[:1], dg_memkv[:1], dg_ffn[:1], dg_fin[:1], conv_rows,
                             jnp.broadcast_to(loss[:1, :1], (1, d)), jnp.zeros((SMALL_ROWS - 9, d), F32)], axis=0)
    total = _sum_devices(_exchange(_gather_comm([block]), name="gather_small")[0], name="sum_small")
    g_conv = lax.dynamic_slice(total[5:8, :CONV_WIDTH], (0, dev * n_conv), (3, n_conv))
    small_w = [norm_mix, norm_mem_q, norm_mem_kv, norm_ffn, norm_final.reshape(1, d), conv_w[0]]
    small_m = [m_norm_mix, m_norm_mem_q, m_norm_mem_kv, m_norm_ffn, m_norm_final.reshape(1, d), m_conv_w[0]]
    small_v = [v_norm_mix, v_norm_mem_q, v_norm_mem_kv, v_norm_ffn, v_norm_final.reshape(1, d), v_conv_w[0]]
    small_g = [total[0:1], total[1:2], total[2:3], total[3:4], total[4:5], g_conv]
    small_names = ["norm_mix", "norm_mem_q", "norm_mem_kv", "norm_ffn", "norm_final", "conv_w"]
    sg, sd, sm, sv = {}, {}, {}, {}
    for nme, wv, g, m, v in zip(small_names, small_w, small_g, small_m, small_v):
        dl, nm, nv = _adam_small(wv, g, m, v, name="adam_" + nme)
        shape = norm_final.shape if nme == "norm_final" else (conv_w.shape if nme == "conv_w" else wv.shape)
        sg[nme], sd[nme], sm[nme], sv[nme] = (t.reshape(shape) for t in (g, dl, nm, nv))

    def ordered(big, sml):
        return (sml["norm_mix"], big["in"], sml["conv_w"], big["a"], big["b"], big["mix"], sml["norm_mem_q"],
                sml["norm_mem_kv"], big["mq"], big["kv"], big["mo"], sml["norm_ffn"], big["fi"], big["fo"],
                sml["norm_final"])

    loss_out = total[8, 0]
    grad_x = dx0.reshape(x.shape)
    return (loss_out, grad_x, *ordered(grads, sg), *ordered(deltas, sd), *ordered(new_m, sm), *ordered(new_v, sv))
```

```python
import functools
import math

import jax
import jax.numpy as jnp
from jax import lax
from jax.experimental import pallas as pl
from jax.experimental.pallas import tpu as pltpu

F32 = jnp.float32
BF16 = jnp.bfloat16
MESH = pl.DeviceIdType.MESH

N_DEV = 8
N_CHIP = 4
NORM_EPS = 1e-6
SB_HEADS = 8
SB_HEAD_DIM = 64
SB_WIDTH = SB_HEADS * SB_HEAD_DIM
CONV_WIDTH = 512
MEM_HEADS = 4
ADAM_LR = 0.001
ADAM_B1 = 0.9
ADAM_B2 = 0.999
ADAM_EPS = 1e-08
ADAM_WD = 0.01
ADAM_STEP = 10

LANES = 128
VMEM_LIMIT_BYTES = 52 * 1024 * 1024
SB_TILE = 256
SB_DEAD = 110.0

ANY = pl.BlockSpec(memory_space=pl.ANY)


def _params(n_grid):
    return pltpu.CompilerParams(dimension_semantics=("arbitrary",) * n_grid, vmem_limit_bytes=VMEM_LIMIT_BYTES)


def _bdot(a, b, dims):
    return lax.dot_general(a.astype(BF16), b.astype(BF16), (dims, ((), ())), preferred_element_type=F32)


NN = ((1,), (0,))
NT = ((1,), (1,))
TN = ((0,), (0,))


class _Comm:
    def __init__(self, ins, outs, n_sems, start, finish):
        self.ins, self.outs, self.n_sems, self.start, self.finish = ins, outs, n_sems, start, finish

    def sem_shapes(self):
        return [pltpu.SemaphoreType.DMA((k,)) for k in self.n_sems]


def _place():
    return lax.axis_index("x"), lax.axis_index("y"), lax.axis_index("c")


def _gather_comm(shards):
    n = len(shards)

    def copies(ins, outs, sems):
        send_sems, recv_sems, _ = sems
        x, y, c = _place()
        chips = [(1 - x, y), (x, 1 - y), (1 - x, 1 - y)]

        def copy(a, k, block, to, from_shard=False):
            dst = outs[a].at[4 * block[0] + 2 * block[1] + block[2]]
            return pltpu.make_async_remote_copy(
                src_ref=ins[a] if from_shard else dst, dst_ref=dst, send_sem=send_sems.at[a * 7 + k],
                recv_sem=recv_sems.at[a * 7 + k], device_id=to, device_id_type=MESH)

        me, sibling = (x, y, c), (x, y, 1 - c)
        own = [[copy(a, 0, me, sibling, True)] + [copy(a, 1 + j, me, (*chip, c), True) for j, chip in enumerate(chips)]
               for a in range(n)]
        landed = [[copy(a, 1 + j, (*chip, c), me) for j, chip in enumerate(chips)] for a in range(n)]
        passed = [[copy(a, 4 + j, (*chip, c), sibling) for j, chip in enumerate(chips)] for a in range(n)]
        from_sibling = [[copy(a, 0, sibling, me)] + [copy(a, 4 + j, (*chip, 1 - c), me) for j, chip in enumerate(chips)]
                        for a in range(n)]
        local = [pltpu.make_async_copy(ins[a], outs[a].at[4 * x + 2 * y + c], sems[2].at[a]) for a in range(n)]
        return own, landed, passed, from_sibling, local

    def start(ins, outs, sems):
        own, _, _, _, local = copies(ins, outs, sems)
        for a in range(n):
            local[a].start()
            for cp in own[a]:
                cp.start()

    def finish(ins, outs, sems):
        own, landed, passed, from_sibling, local = copies(ins, outs, sems)
        for a in range(n):
            for arrived, onward in zip(landed[a], passed[a]):
                arrived.wait_recv()
                onward.start()
        for a in range(n):
            for cp in from_sibling[a]:
                cp.wait_recv()
        for a in range(n):
            for cp in own[a] + passed[a]:
                cp.wait_send()
            local[a].wait()

    outs = [jax.ShapeDtypeStruct((N_DEV,) + s.shape, s.dtype) for s in shards]
    return _Comm(list(shards), outs, (7 * n, 7 * n, n), start, finish)


def _sibling_comm(parts):
    n = len(parts)

    def copies(ins, outs, sems):
        x, y, c = _place()
        return [pltpu.make_async_remote_copy(
            src_ref=ins[a].at[2 * q + 1 - c], dst_ref=outs[a].at[q], send_sem=sems[0].at[a * N_CHIP + q],
            recv_sem=sems[1].at[a * N_CHIP + q], device_id=(x, y, 1 - c), device_id_type=MESH)
            for a in range(n) for q in range(N_CHIP)]

    def start(ins, outs, sems):
        for cp in copies(ins, outs, sems):
            cp.start()

    def finish(ins, outs, sems):
        cps = copies(ins, outs, sems)
        for cp in cps:
            cp.wait_recv()
        for cp in cps:
            cp.wait_send()

    outs = [jax.ShapeDtypeStruct((N_CHIP,) + p.shape[1:], p.dtype) for p in parts]
    return _Comm(list(parts), outs, (N_CHIP * n, N_CHIP * n), start, finish)


def _chips_comm(parts):
    n = len(parts)

    def copies(ins, outs, sems):
        x, y, c = _place()
        chips = [(1 - x, y), (x, 1 - y), (1 - x, 1 - y)]
        return [pltpu.make_async_remote_copy(
            src_ref=ins[a].at[2 * px + py], dst_ref=outs[a].at[j], send_sem=sems[0].at[a * 3 + j],
            recv_sem=sems[1].at[a * 3 + j], device_id=(px, py, c), device_id_type=MESH)
            for a in range(n) for j, (px, py) in enumerate(chips)]

    def start(ins, outs, sems):
        for cp in copies(ins, outs, sems):
            cp.start()

    def finish(ins, outs, sems):
        cps = copies(ins, outs, sems)
        for cp in cps:
            cp.wait_recv()
        for cp in cps:
            cp.wait_send()

    outs = [jax.ShapeDtypeStruct((3,) + p.shape[1:], p.dtype) for p in parts]
    return _Comm(list(parts), outs, (3 * n, 3 * n), start, finish)


def _join_comms(comms):
    if len(comms) == 1:
        return comms[0]

    def split(refs, counts):
        out, at = [], 0
        for n in counts:
            out.append(refs[at:at + n])
            at += n
        return out

    def each(method):
        def run(ins, outs, sems):
            parts = zip(comms, split(ins, [len(c.ins) for c in comms]), split(outs, [len(c.outs) for c in comms]),
                        split(sems, [len(c.n_sems) for c in comms]))
            for c, c_ins, c_outs, c_sems in parts:
                getattr(c, method)(c_ins, c_outs, c_sems)
        return run

    return _Comm([a for c in comms for a in c.ins], [o for c in comms for o in c.outs],
                 tuple(k for c in comms for k in c.n_sems), each("start"), each("finish"))


def _exchange(comm, *, name):
    n_ci, n_co = len(comm.ins), len(comm.outs)

    def kern(*refs):
        c_ins, c_outs, sems = refs[:n_ci], refs[n_ci:n_ci + n_co], refs[n_ci + n_co:]
        comm.start(c_ins, c_outs, sems)
        comm.finish(c_ins, c_outs, sems)

    return pl.pallas_call(kern, name=name, in_specs=[ANY] * n_ci, out_specs=[ANY] * n_co, out_shape=comm.outs,
                          scratch_shapes=comm.sem_shapes())(*comm.ins)


def _call(body, *, name, grid, in_specs, out_specs, out_shape, scratch, args, plan=None):
    comm = plan.comm(name) if plan is not None else None
    if comm is None:
        return list(pl.pallas_call(functools.partial(body), name=name, grid=grid, in_specs=in_specs,
                                   out_specs=out_specs, out_shape=out_shape, scratch_shapes=scratch,
                                   compiler_params=_params(len(grid)))(*args))
    n_in, n_out, n_scr, n_ci, n_co = len(in_specs), len(out_specs), len(scratch), len(comm.ins), len(comm.outs)

    def kern(*refs):
        ins, c_ins, refs = refs[:n_in], refs[n_in:n_in + n_ci], refs[n_in + n_ci:]
        outs, c_outs, refs = refs[:n_out], refs[n_out:n_out + n_co], refs[n_out + n_co:]
        scr, sems = refs[:n_scr], refs[n_scr:]
        ids = [pl.program_id(ax) for ax in range(len(grid))]
        first = functools.reduce(jnp.logical_and, [i == 0 for i in ids])
        last = functools.reduce(jnp.logical_and, [i == g - 1 for i, g in zip(ids, grid)])

        @pl.when(first)
        def _():
            comm.start(c_ins, c_outs, sems)
        body(*ins, *outs, *scr)

        @pl.when(last)
        def _():
            comm.finish(c_ins, c_outs, sems)

    res = pl.pallas_call(kern, name=name, grid=grid, in_specs=list(in_specs) + [ANY] * n_ci,
                         out_specs=list(out_specs) + [ANY] * n_co, out_shape=list(out_shape) + comm.outs,
                         scratch_shapes=list(scratch) + comm.sem_shapes(),
                         compiler_params=_params(len(grid)))(*args, *comm.ins)
    plan.landed(name, list(res[n_out:]))
    return list(res[:n_out])


def _mm_body(dims, has_add, *refs):
    if has_add:
        a_ref, b_ref, add_ref, o_ref = refs
        total = _bdot(a_ref[...], b_ref[...], dims) + add_ref[...]
    else:
        a_ref, b_ref, o_ref = refs
        total = _bdot(a_ref[...], b_ref[...], dims)
    o_ref[...] = total.astype(o_ref.dtype)


def _mm_nt_body(j, n, dy_ref, w_ref, o_ref):
    total = _bdot(dy_ref[:, 0:n], w_ref[0], NT)
    for jj in range(1, j):
        total = total + _bdot(dy_ref[:, jj * n:(jj + 1) * n], w_ref[jj], NT)
    o_ref[...] = total.astype(o_ref.dtype)


def _mm_nn(a, w3, *, name, out_dtype=BF16, add=None, tm=1024, tn=None, out3=False, w_t=False, plan=None):
    m, kk = a.shape
    j, n = w3.shape[0], w3.shape[1 if w_t else 2]
    tm, tn = min(tm, m), n if tn is None else tn
    n_t = n // tn
    in_specs = [pl.BlockSpec((tm, kk), lambda i, jj: (i, 0)),
                pl.BlockSpec((None, tn, kk), lambda i, jj: (jj // n_t, jj % n_t, 0)) if w_t else
                pl.BlockSpec((None, kk, tn), lambda i, jj: (jj // n_t, 0, jj % n_t))]
    args = [a, w3]
    if add is not None:
        in_specs.append(pl.BlockSpec((tm, tn), lambda i, jj: (i, jj)))
        args.append(add)
    if out3:
        out_spec = pl.BlockSpec((None, tm, tn), lambda i, jj: (jj // n_t, i, jj % n_t))
        out_shape = jax.ShapeDtypeStruct((j, m, n), out_dtype)
    else:
        out_spec = pl.BlockSpec((tm, tn), lambda i, jj: (i, jj))
        out_shape = jax.ShapeDtypeStruct((m, j * n), out_dtype)
    return _call(
        functools.partial(_mm_body, NT if w_t else NN, add is not None), name=name, grid=(m // tm, j * n_t),
        in_specs=in_specs, out_specs=[out_spec], out_shape=[out_shape], scratch=[], args=args, plan=plan)[0]


def _mm_gathering(a, shard, *, name, out3=False, w_t=False, tm=1024):
    m, kk = a.shape
    n = shard.shape[0 if w_t else 1]
    tm = min(tm, m)
    n_i = m // tm

    def body(a_ref, shard_ref, o_ref, w_all, w_vmem, send_sems, recv_sems, copy_sems):
        jj, i = pl.program_id(0), pl.program_id(1)
        x, y, c = _place()
        me, sibling = (x, y, c), (x, y, 1 - c)
        chips = [(jnp.bitwise_xor(x, c), jnp.bitwise_xor(y, 1 - c)), (jnp.bitwise_xor(x, 1 - c), jnp.bitwise_xor(y, c)),
                 (1 - x, 1 - y)]
        sibling_chips = [chips[1], chips[0], chips[2]]

        def rows(block):
            return w_all.at[4 * block[0] + 2 * block[1] + block[2]]

        def remote(k, block, to, from_shard=False):
            return pltpu.make_async_remote_copy(
                src_ref=shard_ref if from_shard else rows(block), dst_ref=rows(block), send_sem=send_sems.at[k],
                recv_sem=recv_sems.at[k], device_id=to, device_id_type=MESH)

        def load(src):
            cp = pltpu.make_async_copy(src, w_vmem, copy_sems.at[1])
            cp.start()
            cp.wait()

        own = [remote(0, me, sibling, True)] + [remote(1 + j, me, (*chip, c), True) for j, chip in enumerate(chips)]
        passed = [remote(4 + j, (*chip, c), sibling) for j, chip in enumerate(chips)]
        local = pltpu.make_async_copy(shard_ref, rows(me), copy_sems.at[0])

        @pl.when(jnp.logical_and(i == 0, jj == 0))
        def _():
            local.start()
            own[0].start()
            own[1].start()
            load(shard_ref)

        @pl.when(jnp.logical_and(i == 0, jj == 1))
        def _():
            remote(0, sibling, me).wait_recv()
            load(rows(sibling))

        for j, chip in enumerate(chips):
            @pl.when(jnp.logical_and(i == 0, jj == 2 + 2 * j))
            def _():
                if j < 2:
                    own[1 + j].wait_send()
                    own[2 + j].start()
                remote(1 + j, (*chip, c), me).wait_recv()
                passed[j].start()
                load(rows((*chip, c)))

            @pl.when(jnp.logical_and(i == 0, jj == 3 + 2 * j))
            def _():
                block = (*sibling_chips[j], 1 - c)
                remote(4 + j, block, me).wait_recv()
                load(rows(block))

        o_ref[...] = _bdot(a_ref[...], w_vmem[...], NT if w_t else NN).astype(o_ref.dtype)

        @pl.when(jnp.logical_and(i == n_i - 1, jj == N_DEV - 1))
        def _():
            for cp in [own[0], own[3]] + passed:
                cp.wait_send()
            local.wait()

    def swept(jj):
        x, y, c = _place()
        first, second = 2 + 2 * c, 4 - 2 * c
        flips = (0b000, 0b001, first, second + 1, second, first + 1, 0b110, 0b111)
        return jnp.bitwise_xor(4 * x + 2 * y + c, sum(jnp.where(jj == k, f, 0) for k, f in enumerate(flips)))

    if out3:
        out_spec = pl.BlockSpec((None, tm, n), lambda jj, i: (swept(jj), i, 0))
        out_shape = jax.ShapeDtypeStruct((N_DEV, m, n), BF16)
    else:
        out_spec = pl.BlockSpec((tm, n), lambda jj, i: (i, swept(jj)))
        out_shape = jax.ShapeDtypeStruct((m, N_DEV * n), BF16)
    return pl.pallas_call(
        body, name=name, grid=(N_DEV, n_i),
        in_specs=[pl.BlockSpec((tm, kk), lambda jj, i: (i, 0)), ANY], out_specs=[out_spec, ANY],
        scratch_shapes=[pltpu.VMEM(shard.shape, shard.dtype), pltpu.SemaphoreType.DMA((7,)),
                        pltpu.SemaphoreType.DMA((7,)), pltpu.SemaphoreType.DMA((2,))],
        out_shape=[out_shape, jax.ShapeDtypeStruct((N_DEV,) + shard.shape, shard.dtype)],
        compiler_params=_params(2))(a, shard)


def _sigmoid(v):
    return 1.0 / (1.0 + jnp.exp(-v))


def _ffn_out_loss(gu3, w3, add, g, target, *, name, tm=256):
    j2, m, n = gu3.shape
    j = j2 // 2
    nn = w3.shape[2]
    tm = min(tm, m)

    def body(gu_ref, w_ref, add_ref, g_ref, t_ref, dx_ref, dxb_ref, dg_ref, loss_ref, act_ref):
        i = pl.program_id(0)
        xv = add_ref[...]
        for jj in range(j):
            gate = gu_ref[0, jj].astype(F32)
            act = (gate * _sigmoid(gate) * gu_ref[1, jj].astype(F32)).astype(BF16)
            act_ref[jj] = act
            xv = xv + _bdot(act, w_ref[jj], NN)
        gv = g_ref[...]
        r = lax.rsqrt(jnp.mean(xv * xv, axis=-1, keepdims=True) + NORM_EPS)
        xhat = xv * r
        err = xhat * gv - t_ref[...]
        _acc_rows(i, loss_ref, 0.5 * jnp.sum(jnp.mean(err * err, axis=-1, keepdims=True), axis=0, keepdims=True))
        dy = err * (1.0 / nn)
        dxhat = dy * gv
        dx = r * (dxhat - xhat * jnp.mean(dxhat * xhat, axis=-1, keepdims=True))
        dx_ref[...] = dx
        dxb_ref[...] = dx.astype(BF16)
        _acc_rows(i, dg_ref, jnp.sum(dy * xhat, axis=0, keepdims=True))

    row = pl.BlockSpec((tm, nn), lambda i: (i, 0))
    return _call(body, name=name, grid=(m // tm,),
                 in_specs=[pl.BlockSpec((2, j, tm, n), lambda i: (0, 0, i, 0)), pl.BlockSpec(w3.shape, lambda i: (0, 0, 0)),
                           row, pl.BlockSpec(g.shape, lambda i: (0, 0)), row],
                 out_specs=[row, row, pl.BlockSpec((8, nn), lambda i: (0, 0)), pl.BlockSpec((8, LANES), lambda i: (0, 0)),
                            pl.BlockSpec((j, tm, n), lambda i: (0, i, 0))],
                 out_shape=[jax.ShapeDtypeStruct((m, nn), F32), jax.ShapeDtypeStruct((m, nn), BF16),
                            jax.ShapeDtypeStruct((8, nn), F32), jax.ShapeDtypeStruct((8, LANES), F32),
                            jax.ShapeDtypeStruct((j, m, n), BF16)],
                 scratch=[], args=[gu3.reshape(2, j, m, n), w3, add, g, target])


def _ffn_out_bwd(dy, w3, gu3, *, name, tm=1024):
    m, nn = dy.shape
    j, n, _ = w3.shape
    tm = min(tm, m)

    def body(dy_ref, w_ref, gu_ref, dgu_ref):
        da = _bdot(dy_ref[...], w_ref[...], NT)
        gate = gu_ref[0].astype(F32)
        up = gu_ref[1].astype(F32)
        sg = _sigmoid(gate)
        silu = gate * sg
        dgu_ref[0] = (da * up * (sg + silu * (1.0 - sg))).astype(BF16)
        dgu_ref[1] = (da * silu).astype(BF16)

    out = _call(body, name=name, grid=(m // tm, j),
                in_specs=[pl.BlockSpec((tm, nn), lambda i, jj: (i, 0)),
                          pl.BlockSpec((None, n, nn), lambda i, jj: (jj, 0, 0)),
                          pl.BlockSpec((2, None, tm, n), lambda i, jj: (0, jj, i, 0))],
                out_specs=[pl.BlockSpec((2, None, tm, n), lambda i, jj: (0, jj, i, 0))],
                out_shape=[jax.ShapeDtypeStruct((2, j, m, n), BF16)], scratch=[],
                args=[dy, w3, gu3.reshape(2, j, m, n)])[0]
    return out.reshape(2 * j, m, n)


def _rms_fwd_tail(xv, g_ref, h_ref):
    r = lax.rsqrt(jnp.mean(xv * xv, axis=-1, keepdims=True) + NORM_EPS)
    h_ref[...] = (xv * r * g_ref[...]).astype(BF16)


def _rms_bwd_tail(i, dh, x_ref, g_ref, dres_ref, dx_ref, dxb_ref, dg_ref):
    xv = x_ref[...]
    r = lax.rsqrt(jnp.mean(xv * xv, axis=-1, keepdims=True) + NORM_EPS)
    xhat = xv * r
    dxhat = dh * g_ref[...]
    dx = r * (dxhat - xhat * jnp.mean(dxhat * xhat, axis=-1, keepdims=True))
    if dres_ref is not None:
        dx = dx + dres_ref[...]
    dx_ref[...] = dx
    dxb_ref[...] = dx.astype(BF16)
    _acc_rows(i, dg_ref, jnp.sum(dh * xhat, axis=0, keepdims=True))


def _mm_nt_rms(dy, w3, x, g, dres, *, name, dy3=False, w_nn=False, tm=512, plan=None):
    j = w3.shape[0]
    m, kk = x.shape
    n = dy.shape[2] if dy3 else dy.shape[1] // j
    tm = min(tm, m)

    def body(dy_ref, w_ref, x_ref, g_ref, *rest):
        dres_ref = rest[0] if dres is not None else None
        dx_ref, dxb_ref, dg_ref = rest[-3:]
        dh = None
        for jj in range(j):
            piece = dy_ref[jj] if dy3 else dy_ref[:, jj * n:(jj + 1) * n]
            part = _bdot(piece, w_ref[jj], NN if w_nn else NT)
            dh = part if dh is None else dh + part
        _rms_bwd_tail(pl.program_id(0), dh, x_ref, g_ref, dres_ref, dx_ref, dxb_ref, dg_ref)

    row = pl.BlockSpec((tm, kk), lambda i: (i, 0))
    in_specs = [pl.BlockSpec((j, tm, n), lambda i: (0, i, 0)) if dy3 else pl.BlockSpec((tm, j * n), lambda i: (i, 0)),
                pl.BlockSpec(w3.shape, lambda i: (0, 0, 0)), row, pl.BlockSpec(g.shape, lambda i: (0, 0))]
    args = [dy, w3, x, g]
    if dres is not None:
        in_specs.append(row)
        args.append(dres)
    return _call(body, name=name, grid=(m // tm,), in_specs=in_specs,
                 out_specs=[row, row, pl.BlockSpec((8, kk), lambda i: (0, 0))],
                 out_shape=[jax.ShapeDtypeStruct((m, kk), F32), jax.ShapeDtypeStruct((m, kk), BF16),
                            jax.ShapeDtypeStruct((8, kk), F32)], scratch=[], args=args, plan=plan)


def _mix_out(br_a, br_b, proj, w, x, g, *, name, tm=512, plan=None):
    s, d = br_a.shape
    tm = min(tm, s)

    def body(a_ref, b_ref, ga_ref, gb_ref, w_ref, x_ref, g_ref, x1_ref, h_ref, merged_ref):
        merged = (_sigmoid(ga_ref[...].astype(F32)) * a_ref[...].astype(F32)
                  + _sigmoid(gb_ref[...].astype(F32)) * b_ref[...].astype(F32)).astype(BF16)
        merged_ref[...] = merged
        xv = _bdot(merged, w_ref[...], NN) + x_ref[...]
        x1_ref[...] = xv
        _rms_fwd_tail(xv, g_ref, h_ref)

    row = pl.BlockSpec((tm, d), lambda i: (i, 0))
    return _call(body, name=name, grid=(s // tm,),
                 in_specs=[row, row, pl.BlockSpec((tm, d), lambda i: (i, 3)), pl.BlockSpec((tm, d), lambda i: (i, 4)),
                           pl.BlockSpec(w.shape, lambda i: (0, 0)), row, pl.BlockSpec(g.shape, lambda i: (0, 0))],
                 out_specs=[row, row, row],
                 out_shape=[jax.ShapeDtypeStruct((s, d), F32), jax.ShapeDtypeStruct((s, d), BF16),
                            jax.ShapeDtypeStruct((s, d), BF16)],
                 scratch=[], args=[br_a, br_b, proj, proj, w, x, g], plan=plan)


def _mm_tn_a3(a3, dy, *, name):
    j, t, n = a3.shape
    nn = dy.shape[1]
    return _call(functools.partial(_mm_body, TN, False), name=name, grid=(j,),
                 in_specs=[pl.BlockSpec((None, t, n), lambda jj: (jj, 0, 0)), pl.BlockSpec((t, nn), lambda jj: (0, 0))],
                 out_specs=[pl.BlockSpec((None, n, nn), lambda jj: (jj, 0, 0))],
                 out_shape=[jax.ShapeDtypeStruct((j, n, nn), BF16)], scratch=[], args=[a3, dy])[0]


def _mm_nt(dy, w3, *, name, out_dtype=BF16, tm=512, tn=1024, plan=None):
    m = dy.shape[0]
    j, kk, n = w3.shape
    tm, tn = min(tm, m), min(tn, kk)
    return _call(
        functools.partial(_mm_nt_body, j, n), name=name,
        grid=(m // tm, kk // tn),
        in_specs=[pl.BlockSpec((tm, j * n), lambda i, q: (i, 0)),
                  pl.BlockSpec((j, tn, n), lambda i, q: (0, q, 0))],
        out_specs=[pl.BlockSpec((tm, tn), lambda i, q: (i, q))],
        out_shape=[jax.ShapeDtypeStruct((m, kk), out_dtype)], scratch=[], args=[dy, w3], plan=plan)[0]


def _mm_tn(a, dy, n, *, name, out_dtype=BF16, tm=512, tn=None, k_tiles=None, plan=None):
    t, kk = a.shape
    j = dy.shape[1] // n
    tm, tn = min(tm, kk), n if tn is None else tn
    n_t = n // tn
    first, count = (0, kk // tm) if k_tiles is None else k_tiles
    return _call(
        functools.partial(_mm_body, TN, False), name=name,
        grid=(count, j * n_t),
        in_specs=[pl.BlockSpec((t, tm), lambda i, jj: (0, first + i)),
                  pl.BlockSpec((t, tn), lambda i, jj: (0, jj))],
        out_specs=[pl.BlockSpec((None, tm, tn), lambda i, jj: (jj // n_t, i, jj % n_t))],
        out_shape=[jax.ShapeDtypeStruct((j, count * tm, n), out_dtype)], scratch=[], args=[a, dy], plan=plan)[0]


def _rows(body, ins, outs, *, n_rows, tm, name, plan=None):
    tm = min(tm, n_rows)
    n_steps = n_rows // tm
    in_specs, args = [], []
    for arr, kind, width, block in ins:
        if kind == "row":
            in_specs.append(pl.BlockSpec((tm, width), functools.partial(lambda i, b: (i, b), b=block)))
        elif kind == "prev":
            in_specs.append(pl.BlockSpec((tm, width), functools.partial(lambda i, b: (jnp.maximum(i - 1, 0), b), b=block)))
        elif kind == "next":
            in_specs.append(pl.BlockSpec((tm, width), functools.partial(lambda i, b: (jnp.minimum(i + 1, n_steps - 1), b), b=block)))
        else:
            in_specs.append(pl.BlockSpec(arr.shape, functools.partial(lambda i, nd: (0,) * nd, nd=arr.ndim)))
        args.append(arr)
    out_specs, out_shape = [], []
    for shape, dtype, kind in outs:
        if kind == "row":
            out_specs.append(pl.BlockSpec((tm, shape[1]), lambda i: (i, 0)))
        else:
            out_specs.append(pl.BlockSpec(shape, functools.partial(lambda i, nd: (0,) * nd, nd=len(shape))))
        out_shape.append(jax.ShapeDtypeStruct(shape, dtype))

    def kern(*refs):
        body(pl.program_id(0), n_steps, *refs)

    return _call(kern, name=name, grid=(n_steps,), in_specs=in_specs, out_specs=out_specs, out_shape=out_shape,
                 scratch=[], args=args, plan=plan)


def _acc_rows(i, ref, value):
    @pl.when(i == 0)
    def _():
        ref[...] = jnp.zeros_like(ref)
    ref[...] += jnp.broadcast_to(value, ref.shape)


def _rms_fwd(x, g, *, name, tm=512):
    s, d = x.shape

    def body(i, n, x_ref, g_ref, h_ref):
        _rms_fwd_tail(x_ref[...], g_ref, h_ref)

    return _rows(body, [(x, "row", d, 0), (g, "full", 0, 0)], [((s, d), BF16, "row")], n_rows=s, tm=tm, name=name)[0]


def _rms_bwd(x, g, dh, dres, *, name, tm=512, plan=None):
    s, d = x.shape

    def body(i, n, x_ref, g_ref, dh_ref, dres_ref, dx_ref, dxb_ref, dg_ref):
        _rms_bwd_tail(i, dh_ref[...].astype(F32), x_ref, g_ref, dres_ref, dx_ref, dxb_ref, dg_ref)

    return _rows(body, [(x, "row", d, 0), (g, "full", 0, 0), (dh, "row", d, 0), (dres, "row", d, 0)],
                 [((s, d), F32, "row"), ((s, d), BF16, "row"), ((8, d), F32, "acc")],
                 n_rows=s, tm=tm, name=name, plan=plan)


def _mix_out_bwd(dx1b, w, br_a, br_b, proj, *, name, tm=512, plan=None):
    s, d = br_a.shape
    tm = min(tm, s)

    def body(dy_ref, w_ref, a_ref, b_ref, ga_ref, gb_ref, da_ref, db_ref, dg_ref):
        dm = _bdot(dy_ref[...], w_ref[...], NT)
        sa = _sigmoid(ga_ref[...].astype(F32))
        sb = _sigmoid(gb_ref[...].astype(F32))
        da_ref[...] = (dm * sa).astype(BF16)
        db_ref[...] = (dm * sb).astype(BF16)
        dg_ref[:, :d] = (dm * a_ref[...].astype(F32) * sa * (1.0 - sa)).astype(BF16)
        dg_ref[:, d:] = (dm * b_ref[...].astype(F32) * sb * (1.0 - sb)).astype(BF16)

    row = pl.BlockSpec((tm, d), lambda i: (i, 0))
    return _call(body, name=name, grid=(s // tm,),
                 in_specs=[row, pl.BlockSpec(w.shape, lambda i: (0, 0)), row, row,
                           pl.BlockSpec((tm, d), lambda i: (i, 3)), pl.BlockSpec((tm, d), lambda i: (i, 4))],
                 out_specs=[row, row, pl.BlockSpec((tm, 2 * d), lambda i: (i, 0))],
                 out_shape=[jax.ShapeDtypeStruct((s, d), BF16), jax.ShapeDtypeStruct((s, d), BF16),
                            jax.ShapeDtypeStruct((s, 2 * d), BF16)],
                 scratch=[], args=[dx1b, w, br_a, br_b, proj, proj], plan=plan)


def _shift_down(cur, prev, k, first):
    row = lax.broadcasted_iota(jnp.int32, cur.shape, 0)
    out = jnp.where(row >= k, pltpu.roll(cur, k, 0), pltpu.roll(prev, k, 0))
    return jnp.where(jnp.logical_and(first, row < k), 0.0, out)


def _shift_up(cur, nxt, k, last):
    tm = cur.shape[0]
    row = lax.broadcasted_iota(jnp.int32, cur.shape, 0)
    out = jnp.where(row < tm - k, pltpu.roll(cur, tm - k, 0), pltpu.roll(nxt, tm - k, 0))
    return jnp.where(jnp.logical_and(last, row >= tm - k), 0.0, out)


def _conv_fwd(proj, conv_w, *, name, tm=512):
    s = proj.shape[0]
    c = CONV_WIDTH

    def body(i, n, u_ref, gb_ref, gc_ref, up_ref, gcp_ref, w_ref, y_ref):
        cu = gc_ref[...].astype(F32) * u_ref[...].astype(F32)
        cup = gcp_ref[...].astype(F32) * up_ref[...].astype(F32)
        first = i == 0
        y = (w_ref[0:1, :] * _shift_down(cu, cup, 2, first) + w_ref[1:2, :] * _shift_down(cu, cup, 1, first)
             + w_ref[2:3, :] * cu)
        y_ref[...] = (gb_ref[...].astype(F32) * y).astype(BF16)

    return _rows(body, [(proj, "row", c, 3), (proj, "row", c, 4), (proj, "row", c, 5),
                        (proj, "prev", c, 3), (proj, "prev", c, 5), (conv_w, "full", 0, 0)],
                 [((s, c), BF16, "row")], n_rows=s, tm=tm, name=name)[0]


def _conv_bwd(dy_b, proj, conv_w, *, name, tm=512, plan=None):
    s = proj.shape[0]
    c = CONV_WIDTH

    def body(i, n, dy_ref, u_ref, gb_ref, gc_ref, up_ref, gcp_ref, dyn_ref, gbn_ref, w_ref, d_ref, dw_ref):
        first, last = i == 0, i == n - 1
        u = u_ref[...].astype(F32)
        gb = gb_ref[...].astype(F32)
        gc = gc_ref[...].astype(F32)
        cu = gc * u
        cup = gcp_ref[...].astype(F32) * up_ref[...].astype(F32)
        cu1 = _shift_down(cu, cup, 1, first)
        cu2 = _shift_down(cu, cup, 2, first)
        conv = w_ref[0:1, :] * cu2 + w_ref[1:2, :] * cu1 + w_ref[2:3, :] * cu
        dy = dy_ref[...].astype(F32)
        dyc = dy * gb
        dycn = dyn_ref[...].astype(F32) * gbn_ref[...].astype(F32)
        dcu = (w_ref[2:3, :] * dyc + w_ref[1:2, :] * _shift_up(dyc, dycn, 1, last)
               + w_ref[0:1, :] * _shift_up(dyc, dycn, 2, last))
        d_ref[:, 0:c] = (dcu * gc).astype(BF16)
        d_ref[:, c:2 * c] = (dy * conv).astype(BF16)
        d_ref[:, 2 * c:3 * c] = (dcu * u).astype(BF16)
        row = lax.broadcasted_iota(jnp.int32, (8, c), 0)
        dw = (jnp.where(row == 0, jnp.sum(dyc * cu2, axis=0, keepdims=True), 0.0)
              + jnp.where(row == 1, jnp.sum(dyc * cu1, axis=0, keepdims=True), 0.0)
              + jnp.where(row == 2, jnp.sum(dyc * cu, axis=0, keepdims=True), 0.0))

        @pl.when(first)
        def _():
            dw_ref[...] = jnp.zeros_like(dw_ref)
        dw_ref[...] += dw

    return _rows(body, [(dy_b, "row", c, 0), (proj, "row", c, 3), (proj, "row", c, 4), (proj, "row", c, 5),
                        (proj, "prev", c, 3), (proj, "prev", c, 5), (dy_b, "next", c, 0), (proj, "next", c, 4),
                        (conv_w, "full", 0, 0)],
                 [((s, 3 * c), BF16, "row"), ((8, c), F32, "acc")], n_rows=s, tm=tm, name=name, plan=plan)


def _mem_probs(q, k, scale):
    sc = _bdot(q, k, NT) * scale
    sc = sc - jnp.max(sc, axis=-1, keepdims=True)
    p = jnp.exp(sc)
    return p / jnp.sum(p, axis=-1, keepdims=True)


def _mem_sublayer(hq, w_q, kv, w_o, x, g, *, name, tm=512):
    s, d = hq.shape
    hd = d // MEM_HEADS
    scale = 1.0 / math.sqrt(hd)
    tm = min(tm, s)

    def body(hq_ref, wq_ref, kv_ref, wo_ref, x_ref, g_ref, q_ref, o_ref, x2_ref, h_ref):
        q_ref[...] = _bdot(hq_ref[...], wq_ref[...], NN).astype(BF16)
        for h in range(MEM_HEADS):
            cols = slice(h * hd, (h + 1) * hd)
            p = _mem_probs(q_ref[:, cols], kv_ref[:, cols], scale)
            o_ref[:, cols] = _bdot(p, kv_ref[:, d + h * hd:d + (h + 1) * hd], NN).astype(BF16)
        xv = _bdot(o_ref[...], wo_ref[...], NN) + x_ref[...]
        x2_ref[...] = xv
        _rms_fwd_tail(xv, g_ref, h_ref)

    row = pl.BlockSpec((tm, d), lambda i: (i, 0))
    whole = lambda a: pl.BlockSpec(a.shape, lambda i: (0,) * a.ndim)
    return _call(body, name=name, grid=(s // tm,),
                 in_specs=[row, whole(w_q), whole(kv), whole(w_o), row, whole(g)], out_specs=[row] * 4,
                 out_shape=[jax.ShapeDtypeStruct((s, d), BF16), jax.ShapeDtypeStruct((s, d), BF16),
                            jax.ShapeDtypeStruct((s, d), F32), jax.ShapeDtypeStruct((s, d), BF16)],
                 scratch=[], args=[hq, w_q, kv, w_o, x, g])


def _mem_sublayer_bwd(dx2b, dx2, x, g, qm, kv, w_q, w_o, *, name, tm=512):
    s, d = qm.shape
    hd = d // MEM_HEADS
    scale = 1.0 / math.sqrt(hd)
    tm = min(tm, s)

    def body(dyb_ref, dres_ref, x_ref, g_ref, q_ref, kv_ref, wq_ref, wo_ref, dx_ref, dxb_ref, dg_ref, dq_ref, dkv_ref):
        i = pl.program_id(0)

        @pl.when(i == 0)
        def _():
            dkv_ref[...] = jnp.zeros_like(dkv_ref)
        dom = _bdot(dyb_ref[...], wo_ref[...], NT).astype(BF16)
        for h in range(MEM_HEADS):
            cols = slice(h * hd, (h + 1) * hd)
            vcols = slice(d + h * hd, d + (h + 1) * hd)
            q, k, v, do = q_ref[:, cols], kv_ref[:, cols], kv_ref[:, vcols], dom[:, cols]
            p = _mem_probs(q, k, scale)
            dp = _bdot(do, v, NT)
            ds = p * (dp - jnp.sum(dp * p, axis=-1, keepdims=True)) * scale
            dq_ref[:, cols] = _bdot(ds, k, NN).astype(BF16)
            dkv_ref[:, cols] += _bdot(ds, q, TN)
            dkv_ref[:, vcols] += _bdot(p, do, TN)
        dh = _bdot(dq_ref[...], wq_ref[...], NT)
        _rms_bwd_tail(i, dh, x_ref, g_ref, dres_ref, dx_ref, dxb_ref, dg_ref)

    row = pl.BlockSpec((tm, d), lambda i: (i, 0))
    whole = lambda a: pl.BlockSpec(a.shape, lambda i: (0,) * a.ndim)
    return _call(body, name=name, grid=(s // tm,),
                 in_specs=[row, row, row, whole(g), row, whole(kv), whole(w_q), whole(w_o)],
                 out_specs=[row, row, pl.BlockSpec((8, d), lambda i: (0, 0)), row, whole(kv)],
                 out_shape=[jax.ShapeDtypeStruct((s, d), F32), jax.ShapeDtypeStruct((s, d), BF16),
                            jax.ShapeDtypeStruct((8, d), F32), jax.ShapeDtypeStruct((s, d), BF16),
                            jax.ShapeDtypeStruct(kv.shape, F32)],
                 scratch=[], args=[dx2b, dx2, x, g, qm, kv, w_q, w_o])


def _sb_consts(t):
    row = lax.broadcasted_iota(jnp.int32, (t, t), 0)
    col = lax.broadcasted_iota(jnp.int32, (t, t), 1)
    lane = lax.broadcasted_iota(jnp.int32, (t, LANES), 1)
    return row, col, lane < SB_HEAD_DIM


def _log_fail(z):
    return jnp.minimum(-z, 0.0) - jnp.log(1.0 + jnp.exp(-jnp.abs(z)))


def _tri_sum(v, tri):
    hi = v.astype(BF16)
    lo = (v - hi.astype(F32)).astype(BF16)
    return _bdot(hi, tri, NN) + _bdot(lo, tri, NN)


def _sb_fwd(proj, *, name, plan=None):
    s = proj.shape[0]
    t = SB_TILE
    n_q = s // t
    scale = 1.0 / math.sqrt(SB_HEAD_DIM)
    k_blk, v_blk = SB_WIDTH // LANES, 2 * SB_WIDTH // LANES

    def body(q_ref, k_ref, v_ref, o_ref, c_ref, first_ref, acc_ref):
        i = pl.program_id(1)
        row, col, head0 = _sb_consts(t)
        later = (row > col).astype(BF16)
        valid = col < row
        qs = q_ref[...] * scale
        q2 = (jnp.where(head0, qs, 0), jnp.where(head0, 0, qs))

        def tile(kb, carry, diag):
            kt = k_ref[pl.ds(pl.multiple_of(kb * t, t), t), :]
            vt = v_ref[pl.ds(pl.multiple_of(kb * t, t), t), :]
            heads = range(2)
            z = [_bdot(q2[h], kt, NT) for h in heads]
            lf = [_log_fail(z[h]) for h in heads]
            if diag:
                lf = [jnp.where(valid, lf[h], 0.0) for h in heads]
            cum = [_tri_sum(lf[h], later) for h in heads]
            w = [jnp.exp(z[h] + lf[h] + cum[h] + carry[h]) for h in heads]
            if diag:
                w = [jnp.where(valid, w[h], 0.0) for h in heads]
            for h in heads:
                acc_ref[h] += _bdot(w[h], vt, NN)
            return tuple(carry[h] + cum[h][:, 0:1] + lf[h][:, 0:1] for h in heads)

        acc_ref[...] = jnp.zeros_like(acc_ref)
        zero = jnp.zeros((t, 1), F32)

        def alive(carry):
            return (jnp.maximum(jnp.max(carry[0]), jnp.max(carry[1])) > -SB_DEAD).astype(jnp.int32)

        def step(state):
            kb, _, c0, c1 = state
            new = tile(kb, (c0, c1), False)
            return kb - 1, alive(new), new[0], new[1]

        carry = tile(i, (zero, zero), True)
        kb, _, c0, c1 = lax.while_loop(lambda st: jnp.logical_and(st[0] >= 0, st[1] > 0), step,
                                       (i - 1, alive(carry), carry[0], carry[1]))
        o_ref[...] = jnp.where(head0, acc_ref[0], acc_ref[1]).astype(BF16)
        c_ref[...] = jnp.where(lax.broadcasted_iota(jnp.int32, (t, 2), 1) == 0, c0, c1)
        first_ref[pl.program_id(0), i] = (kb + 1).astype(F32)

    return _call(
        body, name=name, grid=(SB_HEADS // 2, n_q),
        in_specs=[pl.BlockSpec((t, LANES), lambda p, i: (i, p)),
                  pl.BlockSpec((s, LANES), lambda p, i: (0, k_blk + p)),
                  pl.BlockSpec((s, LANES), lambda p, i: (0, v_blk + p))],
        out_specs=[pl.BlockSpec((t, LANES), lambda p, i: (i, p)),
                   pl.BlockSpec((None, t, 2), lambda p, i: (p, i, 0)),
                   pl.BlockSpec(memory_space=pltpu.SMEM)],
        out_shape=[jax.ShapeDtypeStruct((s, SB_WIDTH), BF16), jax.ShapeDtypeStruct((SB_HEADS // 2, s, 2), F32),
                   jax.ShapeDtypeStruct((SB_HEADS // 2, n_q), F32)],
        scratch=[pltpu.VMEM((2, t, LANES), F32)], args=[proj, proj, proj], plan=plan)


def _sb_bwd(proj, do_a, ctot, first, *, name, plan=None):
    s = proj.shape[0]
    t = SB_TILE
    n_q = s // t
    scale = 1.0 / math.sqrt(SB_HEAD_DIM)
    k_blk, v_blk = SB_WIDTH // LANES, 2 * SB_WIDTH // LANES

    def body(q_ref, k_ref, v_ref, do_ref, c_ref, first_ref, dq_ref, dk_ref, dv_ref, dq_acc, dk_acc, dv_acc):
        i = pl.program_id(1)
        kb0 = jnp.clip(first_ref[pl.program_id(0), i].astype(jnp.int32), 0, i)
        row, col, head0 = _sb_consts(t)
        upto = (row <= col).astype(BF16)
        before = (row < col).astype(BF16)
        valid = col < row
        qs = q_ref[...] * scale
        q2 = (jnp.where(head0, qs, 0), jnp.where(head0, 0, qs))
        do = do_ref[...]
        do2 = (jnp.where(head0, do, 0), jnp.where(head0, 0, do))
        ctot2 = (c_ref[:, 0:1], c_ref[:, 1:2])

        @pl.when(i == 0)
        def _():
            dk_acc[...] = jnp.zeros_like(dk_acc)
            dv_acc[...] = jnp.zeros_like(dv_acc)
        dq_acc[...] = jnp.zeros_like(dq_acc)

        def tile(kb, carry, diag):
            rows = pl.ds(pl.multiple_of(kb * t, t), t)
            kt = k_ref[rows, :]
            vt = v_ref[rows, :]
            heads = range(2)
            lf_before, g_before = carry[0::2], carry[1::2]
            z = [_bdot(q2[h], kt, NT) for h in heads]
            dw = [_bdot(do2[h], vt, NT) for h in heads]
            lf = [_log_fail(z[h]) for h in heads]
            if diag:
                lf = [jnp.where(valid, lf[h], 0.0) for h in heads]
            cum = [_tri_sum(lf[h], upto) for h in heads]
            beta = [jnp.exp(z[h] + lf[h]) for h in heads]
            w = [beta[h] * jnp.exp(ctot2[h] - lf_before[h] - cum[h]) for h in heads]
            if diag:
                w = [jnp.where(valid, w[h], 0.0) for h in heads]
            g = [w[h] * dw[h] for h in heads]
            g_sum = [g_before[h] + _bdot(g[h], before, NN) for h in heads]
            for h in heads:
                dv_acc[rows, :] += _bdot(w[h], do2[h], TN)
            dz = [g[h] * jnp.exp(lf[h]) - beta[h] * g_sum[h] for h in heads]
            if diag:
                dz = [jnp.where(valid, dz[h], 0.0) for h in heads]
            for h in heads:
                dq_acc[h] += _bdot(dz[h], kt, NN)
                dk_acc[rows, :] += _bdot(dz[h], q2[h], TN)
            t_last = slice(t - 1, t)
            new = []
            for h in heads:
                new += [lf_before[h] + cum[h][:, t_last], g_sum[h][:, t_last] + g[h][:, t_last]]
            return tuple(new)

        zero = jnp.zeros((t, 1), F32)
        carry = lax.fori_loop(kb0, i, lambda n, c: tile(n, c, False), (zero,) * 4)
        tile(i, carry, True)
        dq_ref[...] = (jnp.where(head0, dq_acc[0], dq_acc[1]) * scale).astype(BF16)

        @pl.when(i == n_q - 1)
        def _():
            dk_ref[...] = dk_acc[...].astype(BF16)
            dv_ref[...] = dv_acc[...].astype(BF16)

    outs = _call(
        body, name=name, grid=(SB_HEADS // 2, n_q),
        in_specs=[pl.BlockSpec((t, LANES), lambda p, i: (i, p)),
                  pl.BlockSpec((s, LANES), lambda p, i: (0, k_blk + p)),
                  pl.BlockSpec((s, LANES), lambda p, i: (0, v_blk + p)),
                  pl.BlockSpec((t, LANES), lambda p, i: (i, p)),
                  pl.BlockSpec((None, t, 2), lambda p, i: (p, i, 0)),
                  pl.BlockSpec(memory_space=pltpu.SMEM)],
        out_specs=[pl.BlockSpec((t, LANES), lambda p, i: (i, p)),
                   pl.BlockSpec((s, LANES), lambda p, i: (0, p)),
                   pl.BlockSpec((s, LANES), lambda p, i: (0, p))],
        out_shape=[jax.ShapeDtypeStruct((s, SB_WIDTH), BF16)] * 3,
        scratch=[pltpu.VMEM((2, t, LANES), F32), pltpu.VMEM((s, LANES), F32), pltpu.VMEM((s, LANES), F32)],
        args=[proj, proj, proj, do_a, ctot, first], plan=plan)
    return jnp.concatenate(outs, axis=1)


def _mm_gathered(a, key, plan, *, name, out3=False, w_t=False):
    src = plan.gathering(key)
    if src is None:
        return _mm_nn(a, plan.weight(key), name=name, out3=out3, w_t=w_t)
    out, w_all = _mm_gathering(a, src, name=name, out3=out3, w_t=w_t)
    plan.set_weight(key, w_all)
    return out


def _local_step(x, mem, target, gains, plan):
    g_mix, g_memq, g_memkv, g_ffn, g_fin = gains
    d = x.shape[1]

    h0 = _rms_fwd(x, g_mix, name="rms_mix")
    proj = _mm_gathered(h0, "in", plan, name="mm_in")
    w_in = plan.weight("in")
    o_a, ctot, first = _sb_fwd(proj, name="sb_fwd", plan=plan)
    conv_w = plan.weight("conv")
    y_b = _conv_fwd(proj, conv_w, name="conv_fwd")
    w_a, w_b, w_mix = plan.weight("a"), plan.weight("b"), plan.weight("mix")
    br_a = _mm_nn(o_a, w_a, name="mm_branch_a")
    br_b = _mm_nn(y_b, w_b, name="mm_branch_b")
    x1, hq, merged = _mix_out(br_a, br_b, proj, w_mix[0], x, g_memq, name="mm_mix", plan=plan)
    w_mq, w_kv, w_mo = plan.weight("mq")[0], plan.weight("kv"), plan.weight("mo")[0]
    mn = _rms_fwd(mem, g_memkv, name="rms_memkv")
    kv = _mm_nn(mn, w_kv, name="mm_memkv")
    qm, om, x2, hf = _mem_sublayer(hq, w_mq, kv, w_mo, x1, g_ffn, name="mem_sublayer")
    gu = _mm_gathered(hf, "fi", plan, name="mm_ffn_in", out3=True, w_t=True)
    w_fi, w_fo = plan.weight("fi"), plan.weight("fo")
    dx3, dx3b, dg_fin, loss, act = _ffn_out_loss(gu, w_fo, x2, g_fin, target, name="mm_ffn_out")

    plan.grad("fo", _mm_tn_a3(act, dx3b, name="mm_d_w_ffn_out"))
    dgu = _ffn_out_bwd(dx3b, w_fo, gu, name="mm_d_act")
    plan.grad("fi", _mm_tn_a3(dgu, hf, name="mm_d_w_ffn_in"))
    dx2, dx2b, dg_ffn = _mm_nt_rms(dgu, w_fi, x2, g_ffn, dx3, name="mm_d_hf", dy3=True, w_nn=True, tm=256, plan=plan)

    plan.grad("mo", _mm_tn(om, dx2b, d, name="mm_d_w_memo"))
    dx1, dx1b, dg_memq, dqm, dkv = _mem_sublayer_bwd(dx2b, dx2, x1, g_memq, qm, kv, w_mq, w_mo, name="mem_sublayer_bwd")
    plan.grad("mq", _mm_tn(hq, dqm, d, name="mm_d_w_memq"))
    plan.grad("kv", _mm_tn(mn, dkv, w_kv.shape[2], name="mm_d_w_memkv"))
    _, _, dg_memkv = _mm_nt_rms(dkv, w_kv, mem, g_memkv, None, name="mm_d_mn")

    plan.grad("mix", _mm_tn(merged, dx1b, d, name="mm_d_w_mix"))
    dbr_a, dbr_b, dgab = _mix_out_bwd(dx1b, w_mix[0], br_a, br_b, proj, name="mm_d_merged", plan=plan)
    plan.grad("a", _mm_tn(o_a, dbr_a, d, name="mm_d_w_branch_a"))
    do_a = _mm_nt(dbr_a, w_a, name="mm_d_o_a")
    plan.grad("b", _mm_tn(y_b, dbr_b, d, name="mm_d_w_branch_b"))
    dy_b = _mm_nt(dbr_b, w_b, name="mm_d_y_b")
    dconv, dconv_w = _conv_bwd(dy_b, proj, conv_w, name="conv_bwd", plan=plan)
    dqkv = _sb_bwd(proj, do_a, ctot, first, name="sb_bwd", plan=plan)
    dproj = jnp.concatenate([dqkv, dconv, dgab], axis=1)
    plan.grad("in0", _mm_tn(h0, dproj, w_in.shape[2], name="mm_d_w_in0", k_tiles=(0, 1)))
    plan.grad("in1", _mm_tn(h0, dproj, w_in.shape[2], name="mm_d_w_in1", k_tiles=(1, 1), plan=plan))
    dh0 = _mm_nt(dproj, w_in, name="mm_d_h0", out_dtype=F32, plan=plan)
    dx0, _, dg_mix = _rms_bwd(x, g_mix, dh0, dx1, name="rms_mix_bwd", plan=plan)

    return dx0, (dg_mix, dg_memq, dg_memkv, dg_ffn, dg_fin, dconv_w, loss)


def _row_tile(a, target=512):
    tm = min(a, target)
    while a % tm:
        tm -= 8
    return tm


def _sum_with_sibling(part, recv, core, *, name):
    _, a, b = part.shape
    tm = _row_tile(a)

    def body(core_ref, p_ref, r_ref, o_ref):
        o_ref[...] = (p_ref[...].astype(F32) + r_ref[...].astype(F32)).astype(o_ref.dtype)

    return pl.pallas_call(
        body, name=name,
        grid_spec=pltpu.PrefetchScalarGridSpec(
            num_scalar_prefetch=1, grid=(N_CHIP, a // tm),
            in_specs=[pl.BlockSpec((None, tm, b), lambda q, i, core_ref: (2 * q + core_ref[0], i, 0)),
                      pl.BlockSpec((None, tm, b), lambda q, i, core_ref: (q, i, 0))],
            out_specs=pl.BlockSpec((None, tm, b), lambda q, i, core_ref: (q, i, 0))),
        out_shape=jax.ShapeDtypeStruct((N_CHIP, a, b), part.dtype), compiler_params=_params(2))(core, part, recv)


def _adam_math(wv, g, m, v):
    m = ADAM_B1 * m + (1.0 - ADAM_B1) * g
    v = ADAM_B2 * v + (1.0 - ADAM_B2) * (g * g)
    m_hat = m / (1.0 - ADAM_B1 ** ADAM_STEP)
    v_hat = v / (1.0 - ADAM_B2 ** ADAM_STEP)
    delta = -ADAM_LR * (m_hat / (jnp.sqrt(v_hat) + ADAM_EPS) + ADAM_WD * wv)
    return delta, m, v


def _adam_sharded(wv, m, v, own, recv, chip, *, name):
    a, b = wv.shape
    tm = _row_tile(a)

    def body(chip_ref, w_ref, m_ref, v_ref, own_ref, recv_ref, g_ref, d_ref, nm_ref, nv_ref):
        g = own_ref[...].astype(F32)
        for j in range(3):
            g = g + recv_ref[j].astype(F32)
        delta, nm, nv = _adam_math(w_ref[...], g, m_ref[...], v_ref[...])
        g_ref[...] = g
        d_ref[...] = delta
        nm_ref[...] = nm
        nv_ref[...] = nv

    tile = pl.BlockSpec((tm, b), lambda i, chip_ref: (i, 0))
    return pl.pallas_call(
        body, name=name,
        grid_spec=pltpu.PrefetchScalarGridSpec(
            num_scalar_prefetch=1, grid=(a // tm,),
            in_specs=[tile, tile, tile,
                      pl.BlockSpec((None, tm, b), lambda i, chip_ref: (chip_ref[0], i, 0)),
                      pl.BlockSpec((3, tm, b), lambda i, chip_ref: (0, i, 0))],
            out_specs=[tile] * 4),
        out_shape=[jax.ShapeDtypeStruct((a, b), F32)] * 4, compiler_params=_params(1))(chip, wv, m, v, own, recv)


def _sum_devices(gathered, *, name):
    _, r, c = gathered.shape

    def body(g_ref, o_ref):
        total = g_ref[0]
        for j in range(1, N_DEV):
            total = total + g_ref[j]
        o_ref[...] = total

    return pl.pallas_call(body, name=name, out_shape=jax.ShapeDtypeStruct((r, c), F32))(gathered)


def _adam_small(wv, g, m, v, *, name):
    def body(w_ref, g_ref, m_ref, v_ref, d_ref, nm_ref, nv_ref):
        delta, nm, nv = _adam_math(w_ref[...], g_ref[...], m_ref[...], v_ref[...])
        d_ref[...] = delta
        nm_ref[...] = nm
        nv_ref[...] = nv

    return pl.pallas_call(body, name=name, out_shape=[jax.ShapeDtypeStruct(wv.shape, F32)] * 3)(wv, g, m, v)


BIG = ("in", "a", "b", "mix", "mq", "kv", "mo", "fi", "fo")
ROW_SHARDED = ("mix", "mq", "mo")
UNSHARDED = ("a", "b")
FFN_GROUPS = 4
SMALL_ROWS = 16


class _Plan:
    FUSED = ("in",)
    GATHER_ON = {"sb_fwd": ("a", "b", "mix", "kv", "mq", "mo", "conv", "fi"), "mm_mix": ("fo",)}
    SIBLING_ON = {"mm_d_hf": ("fo", "fi"), "mm_d_merged": ("mo", "mq", "kv"), "conv_bwd": ("mix", "a", "b"),
                  "mm_d_w_in1": ("in0",), "mm_d_h0": ("in1",)}
    CHIPS_ON = {"sb_bwd": ("fo", "fi", "mo", "mq", "kv", "mix", "a", "b"), "mm_d_h0": ("in0",),
                "rms_mix_bwd": ("in1",)}

    def __init__(self, shards, core):
        self.shards, self.core = shards, core
        self.w, self.parts, self.chip_sums, self.from_chips = {}, {}, {}, {}

    def gathering(self, k):
        return self.shards[k] if k in self.FUSED else None

    def comm(self, name):
        comms = []
        if name in self.GATHER_ON:
            comms.append(_gather_comm([self.shards[k] for k in self.GATHER_ON[name]]))
        if name in self.SIBLING_ON:
            comms.append(_sibling_comm([self.parts[k] for k in self.SIBLING_ON[name]]))
        if name in self.CHIPS_ON:
            comms.append(_chips_comm([self.chip_sums[k] for k in self.CHIPS_ON[name]]))
        return _join_comms(comms) if comms else None

    def landed(self, name, outs):
        outs = list(outs)
        for k in self.GATHER_ON.get(name, ()):
            self.set_weight(k, outs.pop(0))
        for k in self.SIBLING_ON.get(name, ()):
            self.chip_sums[k] = _sum_with_sibling(self.parts[k], outs.pop(0), self.core, name="sum_with_sibling_" + k)
        for k in self.CHIPS_ON.get(name, ()):
            self.from_chips[k] = outs.pop(0)

    def set_weight(self, k, gathered):
        _, a, b = gathered.shape
        if k in ROW_SHARDED:
            gathered = gathered.reshape(1, N_DEV * a, b)
        elif k in UNSHARDED:
            gathered = jnp.transpose(gathered, (1, 0, 2)).reshape(1, a, N_DEV * b)
        elif k == "fo":
            gathered = gathered.reshape(FFN_GROUPS, N_DEV * a // FFN_GROUPS, b)
        elif k == "conv":
            n_conv = CONV_WIDTH // N_DEV
            gathered = jnp.transpose(gathered[:, :3, :n_conv], (1, 0, 2)).reshape(3, CONV_WIDTH)
        self.w[k] = gathered

    def weight(self, k):
        return self.w[k]

    def grad(self, k, g):
        _, a, b = g.shape
        if k in ROW_SHARDED:
            g = g.reshape(N_DEV, a // N_DEV, b)
        elif k in UNSHARDED:
            g = jnp.transpose(g.reshape(a, N_DEV, b // N_DEV), (1, 0, 2))
        elif k == "fo":
            g = g.reshape(N_DEV, FFN_GROUPS * a // N_DEV, b)
        self.parts[k] = g


def kernel(x, mem, norm_mix, w_in, conv_w, w_branch_a, w_branch_b, w_mix_out, norm_mem_q, norm_mem_kv, w_mem_q, w_mem_kv, w_mem_o, norm_ffn, w_ffn_in, w_ffn_out, norm_final, loss_target, m_norm_mix, m_w_in, m_conv_w, m_w_branch_a, m_w_branch_b, m_w_mix_out, m_norm_mem_q, m_norm_mem_kv, m_w_mem_q, m_w_mem_kv, m_w_mem_o, m_norm_ffn, m_w_ffn_in, m_w_ffn_out, m_norm_final, v_norm_mix, v_w_in, v_conv_w, v_w_branch_a, v_w_branch_b, v_w_mix_out, v_norm_mem_q, v_norm_mem_kv, v_w_mem_q, v_w_mem_kv, v_w_mem_o, v_norm_ffn, v_w_ffn_in, v_w_ffn_out, v_norm_final):
    d = x.shape[-1]
    xi, yi, ci = lax.axis_index("x"), lax.axis_index("y"), lax.axis_index("c")
    core = jnp.reshape(ci, (1,)).astype(jnp.int32)
    chip = jnp.reshape(2 * xi + yi, (1,)).astype(jnp.int32)
    dev = 4 * xi + 2 * yi + ci

    big_w = dict(zip(BIG, (w_in, w_branch_a, w_branch_b, w_mix_out, w_mem_q, w_mem_kv, w_mem_o, w_ffn_in, w_ffn_out)))
    big_m = dict(zip(BIG, (m_w_in, m_w_branch_a, m_w_branch_b, m_w_mix_out, m_w_mem_q, m_w_mem_kv, m_w_mem_o, m_w_ffn_in, m_w_ffn_out)))
    big_v = dict(zip(BIG, (v_w_in, v_w_branch_a, v_w_branch_b, v_w_mix_out, v_w_mem_q, v_w_mem_kv, v_w_mem_o, v_w_ffn_in, v_w_ffn_out)))

    flip = lambda t, k: jnp.transpose(t) if k == "fi" else t
    shards = {k: flip(big_w[k][0], k).astype(BF16) for k in BIG}
    n_conv = conv_w.shape[-1]
    shards["conv"] = jnp.zeros((8, LANES), F32).at[:3, :n_conv].set(conv_w[0])
    plan = _Plan(shards, core)

    gains = (norm_mix, norm_mem_q, norm_mem_kv, norm_ffn, norm_final.reshape(1, d))
    dx0, small = _local_step(x[0], mem[0], loss_target[0], gains, plan)

    grads, deltas, new_m, new_v = {}, {}, {}, {}
    for k in BIG:
        lead = big_w[k].shape
        wv, mv, vv = flip(big_w[k][0], k), flip(big_m[k][0], k), flip(big_v[k][0], k)
        if k == "in":
            half = wv.shape[0] // 2
            lo = _adam_sharded(wv[:half], mv[:half], vv[:half], plan.chip_sums["in0"], plan.from_chips["in0"], chip,
                               name="adam_in0")
            hi = _adam_sharded(wv[half:], mv[half:], vv[half:], plan.chip_sums["in1"], plan.from_chips["in1"], chip,
                               name="adam_in1")
            outs = [jnp.concatenate(pair, axis=0) for pair in zip(lo, hi)]
        else:
            outs = _adam_sharded(wv, mv, vv, plan.chip_sums[k], plan.from_chips[k], chip, name="adam_" + k)
        grads[k], deltas[k], new_m[k], new_v[k] = (flip(t, k).reshape(lead) for t in outs)

    dg_mix, dg_memq, dg_memkv, dg_ffn, dg_fin, dconv_w, loss = small
    conv_rows = jnp.zeros((3, d), F32).at[:, :CONV_WIDTH].set(dconv_w[:3])
    block = jnp.concatenate([dg_mix[:1], dg_memq[:1], dg_memkv[:1], dg_ffn[:1], dg_fin[:1], conv_rows,
                             jnp.broadcast_to(loss[:1, :1], (1, d)), jnp.zeros((SMALL_ROWS - 9, d), F32)], axis=0)
    total = _sum_devices(_exchange(_gather_comm([block]), name="gather_small")[0], name="sum_small")
    g_conv = lax.dynamic_slice(total[5:8, :CONV_WIDTH], (0, dev * n_conv), (3, n_conv))
    small_w = [norm_mix, norm_mem_q, norm_mem_kv, norm_ffn, norm_final.reshape(1, d), conv_w[0]]
    small_m = [m_norm_mix, m_norm_mem_q, m_norm_mem_kv, m_norm_ffn, m_norm_final.reshape(1, d), m_conv_w[0]]
    small_v = [v_norm_mix, v_norm_mem_q, v_norm_mem_kv, v_norm_ffn, v_norm_final.reshape(1, d), v_conv_w[0]]
    small_g = [total[0:1], total[1:2], total[2:3], total[3:4], total[4:5], g_conv]
    small_names = ["norm_mix", "norm_mem_q", "norm_mem_kv", "norm_ffn", "norm_final", "conv_w"]
    sg, sd, sm, sv = {}, {}, {}, {}
    for nme, wv, g, m, v in zip(small_names, small_w, small_g, small_m, small_v):
        dl, nm, nv = _adam_small(wv, g, m, v, name="adam_" + nme)
        shape = norm_final.shape if nme == "norm_final" else (conv_w.shape if nme == "conv_w" else wv.shape)
        sg[nme], sd[nme], sm[nme], sv[nme] = (t.reshape(shape) for t in (g, dl, nm, nv))

    def ordered(big, sml):
        return (sml["norm_mix"], big["in"], sml["conv_w"], big["a"], big["b"], big["mix"], sml["norm_mem_q"],
                sml["norm_mem_kv"], big["mq"], big["kv"], big["mo"], sml["norm_ffn"], big["fi"], big["fo"],
                sml["norm_final"])

    loss_out = total[8, 0]
    grad_x = dx0.reshape(x.shape)
    return (loss_out, grad_x, *ordered(grads, sg), *ordered(deltas, sd), *ordered(new_m, sm), *ordered(new_v, sv))
```

```python
import functools
import math

import jax
import jax.numpy as jnp
from jax import lax
from jax.experimental import pallas as pl
from jax.experimental.pallas import tpu as pltpu

F32 = jnp.float32
BF16 = jnp.bfloat16
MESH = pl.DeviceIdType.MESH

N_DEV = 8
N_CHIP = 4
NORM_EPS = 1e-6
SB_HEADS = 8
SB_HEAD_DIM = 64
SB_WIDTH = SB_HEADS * SB_HEAD_DIM
CONV_WIDTH = 512
MEM_HEADS = 4
ADAM_LR = 0.001
ADAM_B1 = 0.9
ADAM_B2 = 0.999
ADAM_EPS = 1e-08
ADAM_WD = 0.01
ADAM_STEP = 10

LANES = 128
VMEM_LIMIT_BYTES = 52 * 1024 * 1024
SB_TILE = 256
SB_DEAD = 159.0
SB_CLAMP = 126.0
LOG2_E = 1.4426950408889634

ANY = pl.BlockSpec(memory_space=pl.ANY)


def _params(n_grid):
    return pltpu.CompilerParams(dimension_semantics=("arbitrary",) * n_grid, vmem_limit_bytes=VMEM_LIMIT_BYTES)


def _bdot(a, b, dims):
    return lax.dot_general(a.astype(BF16), b.astype(BF16), (dims, ((), ())), preferred_element_type=F32)


NN = ((1,), (0,))
NT = ((1,), (1,))
TN = ((0,), (0,))


class _Comm:
    def __init__(self, ins, outs, n_sems, start, finish):
        self.ins, self.outs, self.n_sems, self.start, self.finish = ins, outs, n_sems, start, finish

    def sem_shapes(self):
        return [pltpu.SemaphoreType.DMA((k,)) for k in self.n_sems]


def _place():
    return lax.axis_index("x"), lax.axis_index("y"), lax.axis_index("c")


def _gather_comm(shards):
    n = len(shards)

    def copies(ins, outs, sems):
        send_sems, recv_sems, _ = sems
        x, y, c = _place()
        chips = [(1 - x, y), (x, 1 - y), (1 - x, 1 - y)]

        def copy(a, k, block, to, from_shard=False):
            dst = outs[a].at[4 * block[0] + 2 * block[1] + block[2]]
            return pltpu.make_async_remote_copy(
                src_ref=ins[a] if from_shard else dst, dst_ref=dst, send_sem=send_sems.at[a * 7 + k],
                recv_sem=recv_sems.at[a * 7 + k], device_id=to, device_id_type=MESH)

        me, sibling = (x, y, c), (x, y, 1 - c)
        own = [[copy(a, 0, me, sibling, True)] + [copy(a, 1 + j, me, (*chip, c), True) for j, chip in enumerate(chips)]
               for a in range(n)]
        landed = [[copy(a, 1 + j, (*chip, c), me) for j, chip in enumerate(chips)] for a in range(n)]
        passed = [[copy(a, 4 + j, (*chip, c), sibling) for j, chip in enumerate(chips)] for a in range(n)]
        from_sibling = [[copy(a, 0, sibling, me)] + [copy(a, 4 + j, (*chip, 1 - c), me) for j, chip in enumerate(chips)]
                        for a in range(n)]
        local = [pltpu.make_async_copy(ins[a], outs[a].at[4 * x + 2 * y + c], sems[2].at[a]) for a in range(n)]
        return own, landed, passed, from_sibling, local

    def start(ins, outs, sems):
        own, _, _, _, local = copies(ins, outs, sems)
        for a in range(n):
            local[a].start()
            for cp in own[a]:
                cp.start()

    def finish(ins, outs, sems):
        own, landed, passed, from_sibling, local = copies(ins, outs, sems)
        for a in range(n):
            for arrived, onward in zip(landed[a], passed[a]):
                arrived.wait_recv()
                onward.start()
        for a in range(n):
            for cp in from_sibling[a]:
                cp.wait_recv()
        for a in range(n):
            for cp in own[a] + passed[a]:
                cp.wait_send()
            local[a].wait()

    outs = [jax.ShapeDtypeStruct((N_DEV,) + s.shape, s.dtype) for s in shards]
    return _Comm(list(shards), outs, (7 * n, 7 * n, n), start, finish)


def _sibling_comm(parts):
    n = len(parts)

    def copies(ins, outs, sems):
        x, y, c = _place()
        return [pltpu.make_async_remote_copy(
            src_ref=ins[a].at[2 * q + 1 - c], dst_ref=outs[a].at[q], send_sem=sems[0].at[a * N_CHIP + q],
            recv_sem=sems[1].at[a * N_CHIP + q], device_id=(x, y, 1 - c), device_id_type=MESH)
            for a in range(n) for q in range(N_CHIP)]

    def start(ins, outs, sems):
        for cp in copies(ins, outs, sems):
            cp.start()

    def finish(ins, outs, sems):
        cps = copies(ins, outs, sems)
        for cp in cps:
            cp.wait_recv()
        for cp in cps:
            cp.wait_send()

    outs = [jax.ShapeDtypeStruct((N_CHIP,) + p.shape[1:], p.dtype) for p in parts]
    return _Comm(list(parts), outs, (N_CHIP * n, N_CHIP * n), start, finish)


def _chips_comm(parts):
    n = len(parts)

    def copies(ins, outs, sems):
        x, y, c = _place()
        chips = [(1 - x, y), (x, 1 - y), (1 - x, 1 - y)]
        return [pltpu.make_async_remote_copy(
            src_ref=ins[a].at[2 * px + py], dst_ref=outs[a].at[j], send_sem=sems[0].at[a * 3 + j],
            recv_sem=sems[1].at[a * 3 + j], device_id=(px, py, c), device_id_type=MESH)
            for a in range(n) for j, (px, py) in enumerate(chips)]

    def start(ins, outs, sems):
        for cp in copies(ins, outs, sems):
            cp.start()

    def finish(ins, outs, sems):
        cps = copies(ins, outs, sems)
        for cp in cps:
            cp.wait_recv()
        for cp in cps:
            cp.wait_send()

    outs = [jax.ShapeDtypeStruct((3,) + p.shape[1:], p.dtype) for p in parts]
    return _Comm(list(parts), outs, (3 * n, 3 * n), start, finish)


def _join_comms(comms):
    if len(comms) == 1:
        return comms[0]

    def split(refs, counts):
        out, at = [], 0
        for n in counts:
            out.append(refs[at:at + n])
            at += n
        return out

    def each(method):
        def run(ins, outs, sems):
            parts = zip(comms, split(ins, [len(c.ins) for c in comms]), split(outs, [len(c.outs) for c in comms]),
                        split(sems, [len(c.n_sems) for c in comms]))
            for c, c_ins, c_outs, c_sems in parts:
                getattr(c, method)(c_ins, c_outs, c_sems)
        return run

    return _Comm([a for c in comms for a in c.ins], [o for c in comms for o in c.outs],
                 tuple(k for c in comms for k in c.n_sems), each("start"), each("finish"))


def _exchange(comm, *, name):
    n_ci, n_co = len(comm.ins), len(comm.outs)

    def kern(*refs):
        c_ins, c_outs, sems = refs[:n_ci], refs[n_ci:n_ci + n_co], refs[n_ci + n_co:]
        comm.start(c_ins, c_outs, sems)
        comm.finish(c_ins, c_outs, sems)

    return pl.pallas_call(kern, name=name, in_specs=[ANY] * n_ci, out_specs=[ANY] * n_co, out_shape=comm.outs,
                          scratch_shapes=comm.sem_shapes())(*comm.ins)


def _call(body, *, name, grid, in_specs, out_specs, out_shape, scratch, args, plan=None):
    comm = plan.comm(name) if plan is not None else None
    if comm is None:
        return list(pl.pallas_call(functools.partial(body), name=name, grid=grid, in_specs=in_specs,
                                   out_specs=out_specs, out_shape=out_shape, scratch_shapes=scratch,
                                   compiler_params=_params(len(grid)))(*args))
    n_in, n_out, n_scr, n_ci, n_co = len(in_specs), len(out_specs), len(scratch), len(comm.ins), len(comm.outs)

    def kern(*refs):
        ins, c_ins, refs = refs[:n_in], refs[n_in:n_in + n_ci], refs[n_in + n_ci:]
        outs, c_outs, refs = refs[:n_out], refs[n_out:n_out + n_co], refs[n_out + n_co:]
        scr, sems = refs[:n_scr], refs[n_scr:]
        ids = [pl.program_id(ax) for ax in range(len(grid))]
        first = functools.reduce(jnp.logical_and, [i == 0 for i in ids])
        last = functools.reduce(jnp.logical_and, [i == g - 1 for i, g in zip(ids, grid)])

        @pl.when(first)
        def _():
            comm.start(c_ins, c_outs, sems)
        body(*ins, *outs, *scr)

        @pl.when(last)
        def _():
            comm.finish(c_ins, c_outs, sems)

    res = pl.pallas_call(kern, name=name, grid=grid, in_specs=list(in_specs) + [ANY] * n_ci,
                         out_specs=list(out_specs) + [ANY] * n_co, out_shape=list(out_shape) + comm.outs,
                         scratch_shapes=list(scratch) + comm.sem_shapes(),
                         compiler_params=_params(len(grid)))(*args, *comm.ins)
    plan.landed(name, list(res[n_out:]))
    return list(res[:n_out])


def _mm_body(dims, has_add, *refs):
    if has_add:
        a_ref, b_ref, add_ref, o_ref = refs
        total = _bdot(a_ref[...], b_ref[...], dims) + add_ref[...]
    else:
        a_ref, b_ref, o_ref = refs
        total = _bdot(a_ref[...], b_ref[...], dims)
    o_ref[...] = total.astype(o_ref.dtype)


def _mm_nt_body(j, n, dy_ref, w_ref, o_ref):
    total = _bdot(dy_ref[:, 0:n], w_ref[0], NT)
    for jj in range(1, j):
        total = total + _bdot(dy_ref[:, jj * n:(jj + 1) * n], w_ref[jj], NT)
    o_ref[...] = total.astype(o_ref.dtype)


def _mm_nn(a, w3, *, name, out_dtype=BF16, add=None, tm=1024, tn=None, out3=False, w_t=False, plan=None):
    m, kk = a.shape
    j, n = w3.shape[0], w3.shape[1 if w_t else 2]
    tm, tn = min(tm, m), n if tn is None else tn
    n_t = n // tn
    in_specs = [pl.BlockSpec((tm, kk), lambda i, jj: (i, 0)),
                pl.BlockSpec((None, tn, kk), lambda i, jj: (jj // n_t, jj % n_t, 0)) if w_t else
                pl.BlockSpec((None, kk, tn), lambda i, jj: (jj // n_t, 0, jj % n_t))]
    args = [a, w3]
    if add is not None:
        in_specs.append(pl.BlockSpec((tm, tn), lambda i, jj: (i, jj)))
        args.append(add)
    if out3:
        out_spec = pl.BlockSpec((None, tm, tn), lambda i, jj: (jj // n_t, i, jj % n_t))
        out_shape = jax.ShapeDtypeStruct((j, m, n), out_dtype)
    else:
        out_spec = pl.BlockSpec((tm, tn), lambda i, jj: (i, jj))
        out_shape = jax.ShapeDtypeStruct((m, j * n), out_dtype)
    return _call(
        functools.partial(_mm_body, NT if w_t else NN, add is not None), name=name, grid=(m // tm, j * n_t),
        in_specs=in_specs, out_specs=[out_spec], out_shape=[out_shape], scratch=[], args=args, plan=plan)[0]


def _mm_gathering(a, shard, *, name, out3=False, w_t=False, tm=1024):
    m, kk = a.shape
    n = shard.shape[0 if w_t else 1]
    tm = min(tm, m)
    n_i = m // tm

    def body(a_ref, shard_ref, o_ref, w_all, w_vmem, send_sems, recv_sems, copy_sems):
        jj, i = pl.program_id(0), pl.program_id(1)
        x, y, c = _place()
        me, sibling = (x, y, c), (x, y, 1 - c)
        chips = [(jnp.bitwise_xor(x, c), jnp.bitwise_xor(y, 1 - c)), (jnp.bitwise_xor(x, 1 - c), jnp.bitwise_xor(y, c)),
                 (1 - x, 1 - y)]
        sibling_chips = [chips[1], chips[0], chips[2]]

        def rows(block):
            return w_all.at[4 * block[0] + 2 * block[1] + block[2]]

        def remote(k, block, to, from_shard=False):
            return pltpu.make_async_remote_copy(
                src_ref=shard_ref if from_shard else rows(block), dst_ref=rows(block), send_sem=send_sems.at[k],
                recv_sem=recv_sems.at[k], device_id=to, device_id_type=MESH)

        def load(src):
            cp = pltpu.make_async_copy(src, w_vmem, copy_sems.at[1])
            cp.start()
            cp.wait()

        own = [remote(0, me, sibling, True)] + [remote(1 + j, me, (*chip, c), True) for j, chip in enumerate(chips)]
        passed = [remote(4 + j, (*chip, c), sibling) for j, chip in enumerate(chips)]
        local = pltpu.make_async_copy(shard_ref, rows(me), copy_sems.at[0])

        @pl.when(jnp.logical_and(i == 0, jj == 0))
        def _():
            local.start()
            own[0].start()
            own[1].start()
            load(shard_ref)

        @pl.when(jnp.logical_and(i == 0, jj == 1))
        def _():
            remote(0, sibling, me).wait_recv()
            load(rows(sibling))

        for j, chip in enumerate(chips):
            @pl.when(jnp.logical_and(i == 0, jj == 2 + 2 * j))
            def _():
                if j < 2:
                    own[1 + j].wait_send()
                    own[2 + j].start()
                remote(1 + j, (*chip, c), me).wait_recv()
                passed[j].start()
                load(rows((*chip, c)))

            @pl.when(jnp.logical_and(i == 0, jj == 3 + 2 * j))
            def _():
                block = (*sibling_chips[j], 1 - c)
                remote(4 + j, block, me).wait_recv()
                load(rows(block))

        o_ref[...] = _bdot(a_ref[...], w_vmem[...], NT if w_t else NN).astype(o_ref.dtype)

        @pl.when(jnp.logical_and(i == n_i - 1, jj == N_DEV - 1))
        def _():
            for cp in [own[0], own[3]] + passed:
                cp.wait_send()
            local.wait()

    def swept(jj):
        x, y, c = _place()
        first, second = 2 + 2 * c, 4 - 2 * c
        flips = (0b000, 0b001, first, second + 1, second, first + 1, 0b110, 0b111)
        return jnp.bitwise_xor(4 * x + 2 * y + c, sum(jnp.where(jj == k, f, 0) for k, f in enumerate(flips)))

    if out3:
        out_spec = pl.BlockSpec((None, tm, n), lambda jj, i: (swept(jj), i, 0))
        out_shape = jax.ShapeDtypeStruct((N_DEV, m, n), BF16)
    else:
        out_spec = pl.BlockSpec((tm, n), lambda jj, i: (i, swept(jj)))
        out_shape = jax.ShapeDtypeStruct((m, N_DEV * n), BF16)
    return pl.pallas_call(
        body, name=name, grid=(N_DEV, n_i),
        in_specs=[pl.BlockSpec((tm, kk), lambda jj, i: (i, 0)), ANY], out_specs=[out_spec, ANY],
        scratch_shapes=[pltpu.VMEM(shard.shape, shard.dtype), pltpu.SemaphoreType.DMA((7,)),
                        pltpu.SemaphoreType.DMA((7,)), pltpu.SemaphoreType.DMA((2,))],
        out_shape=[out_shape, jax.ShapeDtypeStruct((N_DEV,) + shard.shape, shard.dtype)],
        compiler_params=_params(2))(a, shard)


def _sigmoid(v):
    return 1.0 / (1.0 + jnp.exp(-v))


def _ffn_out_loss(gu3, w3, add, g, target, *, name, tm=256):
    j2, m, n = gu3.shape
    j = j2 // 2
    nn = w3.shape[2]
    tm = min(tm, m)

    def body(gu_ref, w_ref, add_ref, g_ref, t_ref, dx_ref, dxb_ref, dg_ref, loss_ref, act_ref):
        i = pl.program_id(0)
        xv = add_ref[...]
        for jj in range(j):
            gate = gu_ref[0, jj].astype(F32)
            act = (gate * _sigmoid(gate) * gu_ref[1, jj].astype(F32)).astype(BF16)
            act_ref[jj] = act
            xv = xv + _bdot(act, w_ref[jj], NN)
        gv = g_ref[...]
        r = lax.rsqrt(jnp.mean(xv * xv, axis=-1, keepdims=True) + NORM_EPS)
        xhat = xv * r
        err = xhat * gv - t_ref[...]
        _acc_rows(i, loss_ref, 0.5 * jnp.sum(jnp.mean(err * err, axis=-1, keepdims=True), axis=0, keepdims=True))
        dy = err * (1.0 / nn)
        dxhat = dy * gv
        dx = r * (dxhat - xhat * jnp.mean(dxhat * xhat, axis=-1, keepdims=True))
        dx_ref[...] = dx
        dxb_ref[...] = dx.astype(BF16)
        _acc_rows(i, dg_ref, jnp.sum(dy * xhat, axis=0, keepdims=True))

    row = pl.BlockSpec((tm, nn), lambda i: (i, 0))
    return _call(body, name=name, grid=(m // tm,),
                 in_specs=[pl.BlockSpec((2, j, tm, n), lambda i: (0, 0, i, 0)), pl.BlockSpec(w3.shape, lambda i: (0, 0, 0)),
                           row, pl.BlockSpec(g.shape, lambda i: (0, 0)), row],
                 out_specs=[row, row, pl.BlockSpec((8, nn), lambda i: (0, 0)), pl.BlockSpec((8, LANES), lambda i: (0, 0)),
                            pl.BlockSpec((j, tm, n), lambda i: (0, i, 0))],
                 out_shape=[jax.ShapeDtypeStruct((m, nn), F32), jax.ShapeDtypeStruct((m, nn), BF16),
                            jax.ShapeDtypeStruct((8, nn), F32), jax.ShapeDtypeStruct((8, LANES), F32),
                            jax.ShapeDtypeStruct((j, m, n), BF16)],
                 scratch=[], args=[gu3.reshape(2, j, m, n), w3, add, g, target])


def _ffn_out_bwd(dy, w3, gu3, *, name, tm=1024):
    m, nn = dy.shape
    j, n, _ = w3.shape
    tm = min(tm, m)

    def body(dy_ref, w_ref, gu_ref, dgu_ref):
        da = _bdot(dy_ref[...], w_ref[...], NT)
        gate = gu_ref[0].astype(F32)
        up = gu_ref[1].astype(F32)
        sg = _sigmoid(gate)
        silu = gate * sg
        dgu_ref[0] = (da * up * (sg + silu * (1.0 - sg))).astype(BF16)
        dgu_ref[1] = (da * silu).astype(BF16)

    out = _call(body, name=name, grid=(m // tm, j),
                in_specs=[pl.BlockSpec((tm, nn), lambda i, jj: (i, 0)),
                          pl.BlockSpec((None, n, nn), lambda i, jj: (jj, 0, 0)),
                          pl.BlockSpec((2, None, tm, n), lambda i, jj: (0, jj, i, 0))],
                out_specs=[pl.BlockSpec((2, None, tm, n), lambda i, jj: (0, jj, i, 0))],
                out_shape=[jax.ShapeDtypeStruct((2, j, m, n), BF16)], scratch=[],
                args=[dy, w3, gu3.reshape(2, j, m, n)])[0]
    return out.reshape(2 * j, m, n)


def _rms_fwd_tail(xv, g_ref, h_ref):
    r = lax.rsqrt(jnp.mean(xv * xv, axis=-1, keepdims=True) + NORM_EPS)
    h_ref[...] = (xv * r * g_ref[...]).astype(BF16)


def _rms_bwd_tail(i, dh, x_ref, g_ref, dres_ref, dx_ref, dxb_ref, dg_ref):
    xv = x_ref[...]
    r = lax.rsqrt(jnp.mean(xv * xv, axis=-1, keepdims=True) + NORM_EPS)
    xhat = xv * r
    dxhat = dh * g_ref[...]
    dx = r * (dxhat - xhat * jnp.mean(dxhat * xhat, axis=-1, keepdims=True))
    if dres_ref is not None:
        dx = dx + dres_ref[...]
    dx_ref[...] = dx
    dxb_ref[...] = dx.astype(BF16)
    _acc_rows(i, dg_ref, jnp.sum(dh * xhat, axis=0, keepdims=True))


def _mm_nt_rms(dy, w3, x, g, dres, *, name, dy3=False, w_nn=False, tm=512, plan=None):
    j = w3.shape[0]
    m, kk = x.shape
    n = dy.shape[2] if dy3 else dy.shape[1] // j
    tm = min(tm, m)

    def body(dy_ref, w_ref, x_ref, g_ref, *rest):
        dres_ref = rest[0] if dres is not None else None
        dx_ref, dxb_ref, dg_ref = rest[-3:]
        dh = None
        for jj in range(j):
            piece = dy_ref[jj] if dy3 else dy_ref[:, jj * n:(jj + 1) * n]
            part = _bdot(piece, w_ref[jj], NN if w_nn else NT)
            dh = part if dh is None else dh + part
        _rms_bwd_tail(pl.program_id(0), dh, x_ref, g_ref, dres_ref, dx_ref, dxb_ref, dg_ref)

    row = pl.BlockSpec((tm, kk), lambda i: (i, 0))
    in_specs = [pl.BlockSpec((j, tm, n), lambda i: (0, i, 0)) if dy3 else pl.BlockSpec((tm, j * n), lambda i: (i, 0)),
                pl.BlockSpec(w3.shape, lambda i: (0, 0, 0)), row, pl.BlockSpec(g.shape, lambda i: (0, 0))]
    args = [dy, w3, x, g]
    if dres is not None:
        in_specs.append(row)
        args.append(dres)
    return _call(body, name=name, grid=(m // tm,), in_specs=in_specs,
                 out_specs=[row, row, pl.BlockSpec((8, kk), lambda i: (0, 0))],
                 out_shape=[jax.ShapeDtypeStruct((m, kk), F32), jax.ShapeDtypeStruct((m, kk), BF16),
                            jax.ShapeDtypeStruct((8, kk), F32)], scratch=[], args=args, plan=plan)


def _mix_out(br_a, br_b, proj, w, x, g, *, name, tm=512, plan=None):
    s, d = br_a.shape
    tm = min(tm, s)

    def body(a_ref, b_ref, ga_ref, gb_ref, w_ref, x_ref, g_ref, x1_ref, h_ref, merged_ref):
        merged = (_sigmoid(ga_ref[...].astype(F32)) * a_ref[...].astype(F32)
                  + _sigmoid(gb_ref[...].astype(F32)) * b_ref[...].astype(F32)).astype(BF16)
        merged_ref[...] = merged
        xv = _bdot(merged, w_ref[...], NN) + x_ref[...]
        x1_ref[...] = xv
        _rms_fwd_tail(xv, g_ref, h_ref)

    row = pl.BlockSpec((tm, d), lambda i: (i, 0))
    return _call(body, name=name, grid=(s // tm,),
                 in_specs=[row, row, pl.BlockSpec((tm, d), lambda i: (i, 3)), pl.BlockSpec((tm, d), lambda i: (i, 4)),
                           pl.BlockSpec(w.shape, lambda i: (0, 0)), row, pl.BlockSpec(g.shape, lambda i: (0, 0))],
                 out_specs=[row, row, row],
                 out_shape=[jax.ShapeDtypeStruct((s, d), F32), jax.ShapeDtypeStruct((s, d), BF16),
                            jax.ShapeDtypeStruct((s, d), BF16)],
                 scratch=[], args=[br_a, br_b, proj, proj, w, x, g], plan=plan)


def _mm_tn_a3(a3, dy, *, name):
    j, t, n = a3.shape
    nn = dy.shape[1]
    return _call(functools.partial(_mm_body, TN, False), name=name, grid=(j,),
                 in_specs=[pl.BlockSpec((None, t, n), lambda jj: (jj, 0, 0)), pl.BlockSpec((t, nn), lambda jj: (0, 0))],
                 out_specs=[pl.BlockSpec((None, n, nn), lambda jj: (jj, 0, 0))],
                 out_shape=[jax.ShapeDtypeStruct((j, n, nn), BF16)], scratch=[], args=[a3, dy])[0]


def _mm_nt(dy, w3, *, name, out_dtype=BF16, tm=512, tn=1024, plan=None):
    m = dy.shape[0]
    j, kk, n = w3.shape
    tm, tn = min(tm, m), min(tn, kk)
    return _call(
        functools.partial(_mm_nt_body, j, n), name=name,
        grid=(m // tm, kk // tn),
        in_specs=[pl.BlockSpec((tm, j * n), lambda i, q: (i, 0)),
                  pl.BlockSpec((j, tn, n), lambda i, q: (0, q, 0))],
        out_specs=[pl.BlockSpec((tm, tn), lambda i, q: (i, q))],
        out_shape=[jax.ShapeDtypeStruct((m, kk), out_dtype)], scratch=[], args=[dy, w3], plan=plan)[0]


def _mm_tn(a, dy, n, *, name, out_dtype=BF16, tm=512, tn=None, k_tiles=None, plan=None):
    t, kk = a.shape
    j = dy.shape[1] // n
    tm, tn = min(tm, kk), n if tn is None else tn
    n_t = n // tn
    first, count = (0, kk // tm) if k_tiles is None else k_tiles
    return _call(
        functools.partial(_mm_body, TN, False), name=name,
        grid=(count, j * n_t),
        in_specs=[pl.BlockSpec((t, tm), lambda i, jj: (0, first + i)),
                  pl.BlockSpec((t, tn), lambda i, jj: (0, jj))],
        out_specs=[pl.BlockSpec((None, tm, tn), lambda i, jj: (jj // n_t, i, jj % n_t))],
        out_shape=[jax.ShapeDtypeStruct((j, count * tm, n), out_dtype)], scratch=[], args=[a, dy], plan=plan)[0]


def _rows(body, ins, outs, *, n_rows, tm, name, plan=None):
    tm = min(tm, n_rows)
    n_steps = n_rows // tm
    in_specs, args = [], []
    for arr, kind, width, block in ins:
        if kind == "row":
            in_specs.append(pl.BlockSpec((tm, width), functools.partial(lambda i, b: (i, b), b=block)))
        elif kind == "prev":
            in_specs.append(pl.BlockSpec((tm, width), functools.partial(lambda i, b: (jnp.maximum(i - 1, 0), b), b=block)))
        elif kind == "next":
            in_specs.append(pl.BlockSpec((tm, width), functools.partial(lambda i, b: (jnp.minimum(i + 1, n_steps - 1), b), b=block)))
        else:
            in_specs.append(pl.BlockSpec(arr.shape, functools.partial(lambda i, nd: (0,) * nd, nd=arr.ndim)))
        args.append(arr)
    out_specs, out_shape = [], []
    for shape, dtype, kind in outs:
        if kind == "row":
            out_specs.append(pl.BlockSpec((tm, shape[1]), lambda i: (i, 0)))
        else:
            out_specs.append(pl.BlockSpec(shape, functools.partial(lambda i, nd: (0,) * nd, nd=len(shape))))
        out_shape.append(jax.ShapeDtypeStruct(shape, dtype))

    def kern(*refs):
        body(pl.program_id(0), n_steps, *refs)

    return _call(kern, name=name, grid=(n_steps,), in_specs=in_specs, out_specs=out_specs, out_shape=out_shape,
                 scratch=[], args=args, plan=plan)


def _acc_rows(i, ref, value):
    @pl.when(i == 0)
    def _():
        ref[...] = jnp.zeros_like(ref)
    ref[...] += jnp.broadcast_to(value, ref.shape)


def _rms_fwd(x, g, *, name, tm=512):
    s, d = x.shape

    def body(i, n, x_ref, g_ref, h_ref):
        _rms_fwd_tail(x_ref[...], g_ref, h_ref)

    return _rows(body, [(x, "row", d, 0), (g, "full", 0, 0)], [((s, d), BF16, "row")], n_rows=s, tm=tm, name=name)[0]


def _rms_bwd(x, g, dh, dres, *, name, tm=512, plan=None):
    s, d = x.shape

    def body(i, n, x_ref, g_ref, dh_ref, dres_ref, dx_ref, dxb_ref, dg_ref):
        _rms_bwd_tail(i, dh_ref[...].astype(F32), x_ref, g_ref, dres_ref, dx_ref, dxb_ref, dg_ref)

    return _rows(body, [(x, "row", d, 0), (g, "full", 0, 0), (dh, "row", d, 0), (dres, "row", d, 0)],
                 [((s, d), F32, "row"), ((s, d), BF16, "row"), ((8, d), F32, "acc")],
                 n_rows=s, tm=tm, name=name, plan=plan)


def _mix_out_bwd(dx1b, w, br_a, br_b, proj, *, name, tm=512, plan=None):
    s, d = br_a.shape
    tm = min(tm, s)

    def body(dy_ref, w_ref, a_ref, b_ref, ga_ref, gb_ref, da_ref, db_ref, dg_ref):
        dm = _bdot(dy_ref[...], w_ref[...], NT)
        sa = _sigmoid(ga_ref[...].astype(F32))
        sb = _sigmoid(gb_ref[...].astype(F32))
        da_ref[...] = (dm * sa).astype(BF16)
        db_ref[...] = (dm * sb).astype(BF16)
        dg_ref[:, :d] = (dm * a_ref[...].astype(F32) * sa * (1.0 - sa)).astype(BF16)
        dg_ref[:, d:] = (dm * b_ref[...].astype(F32) * sb * (1.0 - sb)).astype(BF16)

    row = pl.BlockSpec((tm, d), lambda i: (i, 0))
    return _call(body, name=name, grid=(s // tm,),
                 in_specs=[row, pl.BlockSpec(w.shape, lambda i: (0, 0)), row, row,
                           pl.BlockSpec((tm, d), lambda i: (i, 3)), pl.BlockSpec((tm, d), lambda i: (i, 4))],
                 out_specs=[row, row, pl.BlockSpec((tm, 2 * d), lambda i: (i, 0))],
                 out_shape=[jax.ShapeDtypeStruct((s, d), BF16), jax.ShapeDtypeStruct((s, d), BF16),
                            jax.ShapeDtypeStruct((s, 2 * d), BF16)],
                 scratch=[], args=[dx1b, w, br_a, br_b, proj, proj], plan=plan)


def _shift_down(cur, prev, k, first):
    row = lax.broadcasted_iota(jnp.int32, cur.shape, 0)
    out = jnp.where(row >= k, pltpu.roll(cur, k, 0), pltpu.roll(prev, k, 0))
    return jnp.where(jnp.logical_and(first, row < k), 0.0, out)


def _shift_up(cur, nxt, k, last):
    tm = cur.shape[0]
    row = lax.broadcasted_iota(jnp.int32, cur.shape, 0)
    out = jnp.where(row < tm - k, pltpu.roll(cur, tm - k, 0), pltpu.roll(nxt, tm - k, 0))
    return jnp.where(jnp.logical_and(last, row >= tm - k), 0.0, out)


def _conv_fwd(proj, conv_w, *, name, tm=512):
    s = proj.shape[0]
    c = CONV_WIDTH

    def body(i, n, u_ref, gb_ref, gc_ref, up_ref, gcp_ref, w_ref, y_ref):
        cu = gc_ref[...].astype(F32) * u_ref[...].astype(F32)
        cup = gcp_ref[...].astype(F32) * up_ref[...].astype(F32)
        first = i == 0
        y = (w_ref[0:1, :] * _shift_down(cu, cup, 2, first) + w_ref[1:2, :] * _shift_down(cu, cup, 1, first)
             + w_ref[2:3, :] * cu)
        y_ref[...] = (gb_ref[...].astype(F32) * y).astype(BF16)

    return _rows(body, [(proj, "row", c, 3), (proj, "row", c, 4), (proj, "row", c, 5),
                        (proj, "prev", c, 3), (proj, "prev", c, 5), (conv_w, "full", 0, 0)],
                 [((s, c), BF16, "row")], n_rows=s, tm=tm, name=name)[0]


def _conv_bwd(dy_b, proj, conv_w, *, name, tm=512, plan=None):
    s = proj.shape[0]
    c = CONV_WIDTH

    def body(i, n, dy_ref, u_ref, gb_ref, gc_ref, up_ref, gcp_ref, dyn_ref, gbn_ref, w_ref, d_ref, dw_ref):
        first, last = i == 0, i == n - 1
        u = u_ref[...].astype(F32)
        gb = gb_ref[...].astype(F32)
        gc = gc_ref[...].astype(F32)
        cu = gc * u
        cup = gcp_ref[...].astype(F32) * up_ref[...].astype(F32)
        cu1 = _shift_down(cu, cup, 1, first)
        cu2 = _shift_down(cu, cup, 2, first)
        conv = w_ref[0:1, :] * cu2 + w_ref[1:2, :] * cu1 + w_ref[2:3, :] * cu
        dy = dy_ref[...].astype(F32)
        dyc = dy * gb
        dycn = dyn_ref[...].astype(F32) * gbn_ref[...].astype(F32)
        dcu = (w_ref[2:3, :] * dyc + w_ref[1:2, :] * _shift_up(dyc, dycn, 1, last)
               + w_ref[0:1, :] * _shift_up(dyc, dycn, 2, last))
        d_ref[:, 0:c] = (dcu * gc).astype(BF16)
        d_ref[:, c:2 * c] = (dy * conv).astype(BF16)
        d_ref[:, 2 * c:3 * c] = (dcu * u).astype(BF16)
        row = lax.broadcasted_iota(jnp.int32, (8, c), 0)
        dw = (jnp.where(row == 0, jnp.sum(dyc * cu2, axis=0, keepdims=True), 0.0)
              + jnp.where(row == 1, jnp.sum(dyc * cu1, axis=0, keepdims=True), 0.0)
              + jnp.where(row == 2, jnp.sum(dyc * cu, axis=0, keepdims=True), 0.0))

        @pl.when(first)
        def _():
            dw_ref[...] = jnp.zeros_like(dw_ref)
        dw_ref[...] += dw

    return _rows(body, [(dy_b, "row", c, 0), (proj, "row", c, 3), (proj, "row", c, 4), (proj, "row", c, 5),
                        (proj, "prev", c, 3), (proj, "prev", c, 5), (dy_b, "next", c, 0), (proj, "next", c, 4),
                        (conv_w, "full", 0, 0)],
                 [((s, 3 * c), BF16, "row"), ((8, c), F32, "acc")], n_rows=s, tm=tm, name=name, plan=plan)


def _mem_probs(q, k, scale):
    sc = _bdot(q, k, NT) * scale
    sc = sc - jnp.max(sc, axis=-1, keepdims=True)
    p = jnp.exp(sc)
    return p / jnp.sum(p, axis=-1, keepdims=True)


def _mem_sublayer(hq, w_q, kv, w_o, x, g, *, name, tm=512):
    s, d = hq.shape
    hd = d // MEM_HEADS
    scale = 1.0 / math.sqrt(hd)
    tm = min(tm, s)

    def body(hq_ref, wq_ref, kv_ref, wo_ref, x_ref, g_ref, q_ref, o_ref, x2_ref, h_ref):
        q_ref[...] = _bdot(hq_ref[...], wq_ref[...], NN).astype(BF16)
        for h in range(MEM_HEADS):
            cols = slice(h * hd, (h + 1) * hd)
            p = _mem_probs(q_ref[:, cols], kv_ref[:, cols], scale)
            o_ref[:, cols] = _bdot(p, kv_ref[:, d + h * hd:d + (h + 1) * hd], NN).astype(BF16)
        xv = _bdot(o_ref[...], wo_ref[...], NN) + x_ref[...]
        x2_ref[...] = xv
        _rms_fwd_tail(xv, g_ref, h_ref)

    row = pl.BlockSpec((tm, d), lambda i: (i, 0))
    whole = lambda a: pl.BlockSpec(a.shape, lambda i: (0,) * a.ndim)
    return _call(body, name=name, grid=(s // tm,),
                 in_specs=[row, whole(w_q), whole(kv), whole(w_o), row, whole(g)], out_specs=[row] * 4,
                 out_shape=[jax.ShapeDtypeStruct((s, d), BF16), jax.ShapeDtypeStruct((s, d), BF16),
                            jax.ShapeDtypeStruct((s, d), F32), jax.ShapeDtypeStruct((s, d), BF16)],
                 scratch=[], args=[hq, w_q, kv, w_o, x, g])


def _mem_sublayer_bwd(dx2b, dx2, x, g, qm, kv, w_q, w_o, *, name, tm=512):
    s, d = qm.shape
    hd = d // MEM_HEADS
    scale = 1.0 / math.sqrt(hd)
    tm = min(tm, s)

    def body(dyb_ref, dres_ref, x_ref, g_ref, q_ref, kv_ref, wq_ref, wo_ref, dx_ref, dxb_ref, dg_ref, dq_ref, dkv_ref):
        i = pl.program_id(0)

        @pl.when(i == 0)
        def _():
            dkv_ref[...] = jnp.zeros_like(dkv_ref)
        dom = _bdot(dyb_ref[...], wo_ref[...], NT).astype(BF16)
        for h in range(MEM_HEADS):
            cols = slice(h * hd, (h + 1) * hd)
            vcols = slice(d + h * hd, d + (h + 1) * hd)
            q, k, v, do = q_ref[:, cols], kv_ref[:, cols], kv_ref[:, vcols], dom[:, cols]
            p = _mem_probs(q, k, scale)
            dp = _bdot(do, v, NT)
            ds = p * (dp - jnp.sum(dp * p, axis=-1, keepdims=True)) * scale
            dq_ref[:, cols] = _bdot(ds, k, NN).astype(BF16)
            dkv_ref[:, cols] += _bdot(ds, q, TN)
            dkv_ref[:, vcols] += _bdot(p, do, TN)
        dh = _bdot(dq_ref[...], wq_ref[...], NT)
        _rms_bwd_tail(i, dh, x_ref, g_ref, dres_ref, dx_ref, dxb_ref, dg_ref)

    row = pl.BlockSpec((tm, d), lambda i: (i, 0))
    whole = lambda a: pl.BlockSpec(a.shape, lambda i: (0,) * a.ndim)
    return _call(body, name=name, grid=(s // tm,),
                 in_specs=[row, row, row, whole(g), row, whole(kv), whole(w_q), whole(w_o)],
                 out_specs=[row, row, pl.BlockSpec((8, d), lambda i: (0, 0)), row, whole(kv)],
                 out_shape=[jax.ShapeDtypeStruct((s, d), F32), jax.ShapeDtypeStruct((s, d), BF16),
                            jax.ShapeDtypeStruct((8, d), F32), jax.ShapeDtypeStruct((s, d), BF16),
                            jax.ShapeDtypeStruct(kv.shape, F32)],
                 scratch=[], args=[dx2b, dx2, x, g, qm, kv, w_q, w_o])


def _sb_consts(t):
    row = lax.broadcasted_iota(jnp.int32, (t, t), 0)
    col = lax.broadcasted_iota(jnp.int32, (t, t), 1)
    lane = lax.broadcasted_iota(jnp.int32, (t, LANES), 1)
    return row, col, lane < SB_HEAD_DIM


def _sb_logits(q, k):
    z2 = jnp.minimum(_bdot(q, k, NT) * LOG2_E, SB_CLAMP)
    return z2, jnp.exp2(z2)


def _tri_sum(v, tri):
    hi = v.astype(BF16)
    lo = (v - hi.astype(F32)).astype(BF16)
    return _bdot(hi, tri, NN) + _bdot(lo, tri, NN)


def _sb_fwd(proj, *, name, plan=None):
    s = proj.shape[0]
    t = SB_TILE
    n_q = s // t
    scale = 1.0 / math.sqrt(SB_HEAD_DIM)
    k_blk, v_blk = SB_WIDTH // LANES, 2 * SB_WIDTH // LANES

    def body(q_ref, k_ref, v_ref, o_ref, c_ref, first_ref, acc_ref, c_scr):
        i = pl.program_id(1)
        row, col, head0 = _sb_consts(t)
        later = (row > col).astype(BF16)
        valid = col < row
        qs = q_ref[...] * scale
        q2 = (jnp.where(head0, qs, 0), jnp.where(head0, 0, qs))

        def tiles(kbs, diag_first, carry):
            kt = [k_ref[pl.ds(pl.multiple_of(kb * t, t), t), :] for kb in kbs]
            vt = [v_ref[pl.ds(pl.multiple_of(kb * t, t), t), :] for kb in kbs]
            jobs = [(n, h) for n in range(len(kbs)) for h in range(2)]
            masked = lambda n: diag_first and n == 0
            zs = {(n, h): _sb_logits(q2[h], kt[n]) for n, h in jobs}
            fail = {j: jnp.log2(1.0 + zs[j][1]) for j in jobs}
            fail = {j: jnp.where(valid, fail[j], 0.0) if masked(j[0]) else fail[j] for j in jobs}
            cum = {j: _tri_sum(fail[j], later) for j in jobs}
            run, before = list(carry), {}
            for n, h in jobs:
                before[n, h] = run[h]
                run[h] = run[h] + cum[n, h][:, 0:1] + fail[n, h][:, 0:1]
            w = {j: jnp.exp2(zs[j][0] - fail[j] - cum[j] - before[j]) for j in jobs}
            w = {j: jnp.where(valid, w[j], 0.0) if masked(j[0]) else w[j] for j in jobs}
            for n, h in jobs:
                acc_ref[h] += _bdot(w[n, h], vt[n], NN)
            return tuple(run)

        acc_ref[...] = jnp.zeros_like(acc_ref)
        zero = jnp.zeros((t, 1), F32)

        def alive(carry):
            return (jnp.minimum(jnp.min(carry[0]), jnp.min(carry[1])) < SB_DEAD).astype(jnp.int32)

        def step(state):
            kb, _, c0, c1 = state
            new = tiles([kb], False, (c0, c1))
            return kb - 1, alive(new), new[0], new[1]

        @pl.when(i == 0)
        def _():
            c_scr[0], c_scr[1] = tiles([i], True, (zero, zero))

        @pl.when(i > 0)
        def _():
            c_scr[0], c_scr[1] = tiles([i, i - 1], True, (zero, zero))
        carry = (c_scr[0], c_scr[1])
        kb, _, c0, c1 = lax.while_loop(lambda st: jnp.logical_and(st[0] >= 0, st[1] > 0), step,
                                       (i - 2, alive(carry), carry[0], carry[1]))
        kb = jnp.maximum(kb, -1)
        o_ref[...] = jnp.where(head0, acc_ref[0], acc_ref[1]).astype(BF16)
        c_ref[...] = jnp.where(lax.broadcasted_iota(jnp.int32, (t, 2), 1) == 0, c0, c1)
        first_ref[pl.program_id(0), i] = (kb + 1).astype(F32)

    return _call(
        body, name=name, grid=(SB_HEADS // 2, n_q),
        in_specs=[pl.BlockSpec((t, LANES), lambda p, i: (i, p)),
                  pl.BlockSpec((s, LANES), lambda p, i: (0, k_blk + p)),
                  pl.BlockSpec((s, LANES), lambda p, i: (0, v_blk + p))],
        out_specs=[pl.BlockSpec((t, LANES), lambda p, i: (i, p)),
                   pl.BlockSpec((None, t, 2), lambda p, i: (p, i, 0)),
                   pl.BlockSpec(memory_space=pltpu.SMEM)],
        out_shape=[jax.ShapeDtypeStruct((s, SB_WIDTH), BF16), jax.ShapeDtypeStruct((SB_HEADS // 2, s, 2), F32),
                   jax.ShapeDtypeStruct((SB_HEADS // 2, n_q), F32)],
        scratch=[pltpu.VMEM((2, t, LANES), F32), pltpu.VMEM((2, t, 1), F32)], args=[proj, proj, proj], plan=plan)


def _sb_bwd(proj, do_a, ctot, first, *, name, plan=None):
    s = proj.shape[0]
    t = SB_TILE
    n_q = s // t
    scale = 1.0 / math.sqrt(SB_HEAD_DIM)
    k_blk, v_blk = SB_WIDTH // LANES, 2 * SB_WIDTH // LANES

    def body(q_ref, k_ref, v_ref, do_ref, c_ref, first_ref, dq_ref, dk_ref, dv_ref, dq_acc, dk_acc, dv_acc):
        i = pl.program_id(1)
        kb0 = jnp.clip(first_ref[pl.program_id(0), i].astype(jnp.int32), 0, i)
        row, col, head0 = _sb_consts(t)
        upto = (row <= col).astype(BF16)
        before = (row < col).astype(BF16)
        valid = col < row
        qs = q_ref[...] * scale
        q2 = (jnp.where(head0, qs, 0), jnp.where(head0, 0, qs))
        do = do_ref[...]
        do2 = (jnp.where(head0, do, 0), jnp.where(head0, 0, do))
        ctot2 = (c_ref[:, 0:1], c_ref[:, 1:2])

        @pl.when(i == 0)
        def _():
            dk_acc[...] = jnp.zeros_like(dk_acc)
            dv_acc[...] = jnp.zeros_like(dv_acc)
        dq_acc[...] = jnp.zeros_like(dq_acc)

        def tiles(kbs, diag_last, carry):
            rows = [pl.ds(pl.multiple_of(kb * t, t), t) for kb in kbs]
            kt = [k_ref[r, :] for r in rows]
            vt = [v_ref[r, :] for r in rows]
            jobs = [(n, h) for n in range(len(kbs)) for h in range(2)]
            masked = lambda n: diag_last and n == len(kbs) - 1
            t_last = slice(t - 1, t)
            zs = {(n, h): _sb_logits(q2[h], kt[n]) for n, h in jobs}
            dw = {(n, h): _bdot(do2[h], vt[n], NT) for n, h in jobs}
            fail = {j: jnp.log2(1.0 + zs[j][1]) for j in jobs}
            fail = {j: jnp.where(valid, fail[j], 0.0) if masked(j[0]) else fail[j] for j in jobs}
            cum = {j: _tri_sum(fail[j], upto) for j in jobs}
            miss = {j: jnp.exp2(-fail[j]) for j in jobs}
            beta = {j: zs[j][1] * miss[j] for j in jobs}
            fail_run, fail_before = list(carry[0::2]), {}
            for n, h in jobs:
                fail_before[n, h] = fail_run[h]
                fail_run[h] = fail_run[h] + cum[n, h][:, t_last]
            w = {(n, h): beta[n, h] * jnp.exp2(fail_before[n, h] + cum[n, h] - ctot2[h]) for n, h in jobs}
            w = {j: jnp.where(valid, w[j], 0.0) if masked(j[0]) else w[j] for j in jobs}
            g = {j: w[j] * dw[j] for j in jobs}
            g_local = {j: _bdot(g[j], before, NN) for j in jobs}
            for n, h in jobs:
                dv_acc[rows[n], :] += _bdot(w[n, h], do2[h], TN)
            g_run, dz = list(carry[1::2]), {}
            for n, h in jobs:
                g_sum = g_run[h] + g_local[n, h]
                dz[n, h] = g[n, h] * miss[n, h] - beta[n, h] * g_sum
                g_run[h] = g_sum[:, t_last] + g[n, h][:, t_last]
            dz = {j: jnp.where(valid, dz[j], 0.0) if masked(j[0]) else dz[j] for j in jobs}
            for n, h in jobs:
                dq_acc[h] += _bdot(dz[n, h], kt[n], NN)
                dk_acc[rows[n], :] += _bdot(dz[n, h], q2[h], TN)
            return fail_run[0], g_run[0], fail_run[1], g_run[1]

        zero = jnp.zeros((t, 1), F32)
        carry = lax.fori_loop(kb0, i - 1, lambda n, c: tiles([n], False, c), (zero,) * 4)

        @pl.when(i == 0)
        def _():
            tiles([i], True, carry)

        @pl.when(i > 0)
        def _():
            tiles([i - 1, i], True, carry)
        dq_ref[...] = (jnp.where(head0, dq_acc[0], dq_acc[1]) * scale).astype(BF16)

        @pl.when(i == n_q - 1)
        def _():
            dk_ref[...] = dk_acc[...].astype(BF16)
            dv_ref[...] = dv_acc[...].astype(BF16)

    outs = _call(
        body, name=name, grid=(SB_HEADS // 2, n_q),
        in_specs=[pl.BlockSpec((t, LANES), lambda p, i: (i, p)),
                  pl.BlockSpec((s, LANES), lambda p, i: (0, k_blk + p)),
                  pl.BlockSpec((s, LANES), lambda p, i: (0, v_blk + p)),
                  pl.BlockSpec((t, LANES), lambda p, i: (i, p)),
                  pl.BlockSpec((None, t, 2), lambda p, i: (p, i, 0)),
                  pl.BlockSpec(memory_space=pltpu.SMEM)],
        out_specs=[pl.BlockSpec((t, LANES), lambda p, i: (i, p)),
                   pl.BlockSpec((s, LANES), lambda p, i: (0, p)),
                   pl.BlockSpec((s, LANES), lambda p, i: (0, p))],
        out_shape=[jax.ShapeDtypeStruct((s, SB_WIDTH), BF16)] * 3,
        scratch=[pltpu.VMEM((2, t, LANES), F32), pltpu.VMEM((s, LANES), F32), pltpu.VMEM((s, LANES), F32)],
        args=[proj, proj, proj, do_a, ctot, first], plan=plan)
    return jnp.concatenate(outs, axis=1)


def _mm_gathered(a, key, plan, *, name, out3=False, w_t=False):
    src = plan.gathering(key)
    if src is None:
        return _mm_nn(a, plan.weight(key), name=name, out3=out3, w_t=w_t)
    out, w_all = _mm_gathering(a, src, name=name, out3=out3, w_t=w_t)
    plan.set_weight(key, w_all)
    return out


def _local_step(x, mem, target, gains, plan):
    g_mix, g_memq, g_memkv, g_ffn, g_fin = gains
    d = x.shape[1]

    h0 = _rms_fwd(x, g_mix, name="rms_mix")
    proj = _mm_gathered(h0, "in", plan, name="mm_in")
    w_in = plan.weight("in")
    o_a, ctot, first = _sb_fwd(proj, name="sb_fwd", plan=plan)
    conv_w = plan.weight("conv")
    y_b = _conv_fwd(proj, conv_w, name="conv_fwd")
    w_a, w_b, w_mix = plan.weight("a"), plan.weight("b"), plan.weight("mix")
    br_a = _mm_nn(o_a, w_a, name="mm_branch_a")
    br_b = _mm_nn(y_b, w_b, name="mm_branch_b")
    x1, hq, merged = _mix_out(br_a, br_b, proj, w_mix[0], x, g_memq, name="mm_mix", plan=plan)
    w_mq, w_kv, w_mo = plan.weight("mq")[0], plan.weight("kv"), plan.weight("mo")[0]
    mn = _rms_fwd(mem, g_memkv, name="rms_memkv")
    kv = _mm_nn(mn, w_kv, name="mm_memkv")
    qm, om, x2, hf = _mem_sublayer(hq, w_mq, kv, w_mo, x1, g_ffn, name="mem_sublayer")
    gu = _mm_gathered(hf, "fi", plan, name="mm_ffn_in", out3=True, w_t=True)
    w_fi, w_fo = plan.weight("fi"), plan.weight("fo")
    dx3, dx3b, dg_fin, loss, act = _ffn_out_loss(gu, w_fo, x2, g_fin, target, name="mm_ffn_out")

    plan.grad("fo", _mm_tn_a3(act, dx3b, name="mm_d_w_ffn_out"))
    dgu = _ffn_out_bwd(dx3b, w_fo, gu, name="mm_d_act")
    plan.grad("fi", _mm_tn_a3(dgu, hf, name="mm_d_w_ffn_in"))
    dx2, dx2b, dg_ffn = _mm_nt_rms(dgu, w_fi, x2, g_ffn, dx3, name="mm_d_hf", dy3=True, w_nn=True, tm=256, plan=plan)

    plan.grad("mo", _mm_tn(om, dx2b, d, name="mm_d_w_memo"))
    dx1, dx1b, dg_memq, dqm, dkv = _mem_sublayer_bwd(dx2b, dx2, x1, g_memq, qm, kv, w_mq, w_mo, name="mem_sublayer_bwd")
    plan.grad("mq", _mm_tn(hq, dqm, d, name="mm_d_w_memq"))
    plan.grad("kv", _mm_tn(mn, dkv, w_kv.shape[2], name="mm_d_w_memkv"))
    _, _, dg_memkv = _mm_nt_rms(dkv, w_kv, mem, g_memkv, None, name="mm_d_mn")

    plan.grad("mix", _mm_tn(merged, dx1b, d, name="mm_d_w_mix"))
    dbr_a, dbr_b, dgab = _mix_out_bwd(dx1b, w_mix[0], br_a, br_b, proj, name="mm_d_merged", plan=plan)
    plan.grad("a", _mm_tn(o_a, dbr_a, d, name="mm_d_w_branch_a"))
    do_a = _mm_nt(dbr_a, w_a, name="mm_d_o_a")
    plan.grad("b", _mm_tn(y_b, dbr_b, d, name="mm_d_w_branch_b"))
    dy_b = _mm_nt(dbr_b, w_b, name="mm_d_y_b")
    dconv, dconv_w = _conv_bwd(dy_b, proj, conv_w, name="conv_bwd", plan=plan)
    dqkv = _sb_bwd(proj, do_a, ctot, first, name="sb_bwd", plan=plan)
    dproj = jnp.concatenate([dqkv, dconv, dgab], axis=1)
    plan.grad("in0", _mm_tn(h0, dproj, w_in.shape[2], name="mm_d_w_in0", k_tiles=(0, 1)))
    plan.grad("in1", _mm_tn(h0, dproj, w_in.shape[2], name="mm_d_w_in1", k_tiles=(1, 1), plan=plan))
    dh0 = _mm_nt(dproj, w_in, name="mm_d_h0", out_dtype=F32, plan=plan)
    dx0, _, dg_mix = _rms_bwd(x, g_mix, dh0, dx1, name="rms_mix_bwd", plan=plan)

    return dx0, (dg_mix, dg_memq, dg_memkv, dg_ffn, dg_fin, dconv_w, loss)


def _row_tile(a, target=512):
    tm = min(a, target)
    while a % tm:
        tm -= 8
    return tm


def _sum_with_sibling(part, recv, core, *, name):
    _, a, b = part.shape
    tm = _row_tile(a)

    def body(core_ref, p_ref, r_ref, o_ref):
        o_ref[...] = (p_ref[...].astype(F32) + r_ref[...].astype(F32)).astype(o_ref.dtype)

    return pl.pallas_call(
        body, name=name,
        grid_spec=pltpu.PrefetchScalarGridSpec(
            num_scalar_prefetch=1, grid=(N_CHIP, a // tm),
            in_specs=[pl.BlockSpec((None, tm, b), lambda q, i, core_ref: (2 * q + core_ref[0], i, 0)),
                      pl.BlockSpec((None, tm, b), lambda q, i, core_ref: (q, i, 0))],
            out_specs=pl.BlockSpec((None, tm, b), lambda q, i, core_ref: (q, i, 0))),
        out_shape=jax.ShapeDtypeStruct((N_CHIP, a, b), part.dtype), compiler_params=_params(2))(core, part, recv)


def _adam_math(wv, g, m, v):
    m = ADAM_B1 * m + (1.0 - ADAM_B1) * g
    v = ADAM_B2 * v + (1.0 - ADAM_B2) * (g * g)
    m_hat = m / (1.0 - ADAM_B1 ** ADAM_STEP)
    v_hat = v / (1.0 - ADAM_B2 ** ADAM_STEP)
    delta = -ADAM_LR * (m_hat / (jnp.sqrt(v_hat) + ADAM_EPS) + ADAM_WD * wv)
    return delta, m, v


def _adam_sharded(wv, m, v, own, recv, chip, *, name):
    a, b = wv.shape
    tm = _row_tile(a)

    def body(chip_ref, w_ref, m_ref, v_ref, own_ref, recv_ref, g_ref, d_ref, nm_ref, nv_ref):
        g = own_ref[...].astype(F32)
        for j in range(3):
            g = g + recv_ref[j].astype(F32)
        delta, nm, nv = _adam_math(w_ref[...], g, m_ref[...], v_ref[...])
        g_ref[...] = g
        d_ref[...] = delta
        nm_ref[...] = nm
        nv_ref[...] = nv

    tile = pl.BlockSpec((tm, b), lambda i, chip_ref: (i, 0))
    return pl.pallas_call(
        body, name=name,
        grid_spec=pltpu.PrefetchScalarGridSpec(
            num_scalar_prefetch=1, grid=(a // tm,),
            in_specs=[tile, tile, tile,
                      pl.BlockSpec((None, tm, b), lambda i, chip_ref: (chip_ref[0], i, 0)),
                      pl.BlockSpec((3, tm, b), lambda i, chip_ref: (0, i, 0))],
            out_specs=[tile] * 4),
        out_shape=[jax.ShapeDtypeStruct((a, b), F32)] * 4, compiler_params=_params(1))(chip, wv, m, v, own, recv)


def _sum_devices(gathered, *, name):
    _, r, c = gathered.shape

    def body(g_ref, o_ref):
        total = g_ref[0]
        for j in range(1, N_DEV):
            total = total + g_ref[j]
        o_ref[...] = total

    return pl.pallas_call(body, name=name, out_shape=jax.ShapeDtypeStruct((r, c), F32))(gathered)


def _adam_small(wv, g, m, v, *, name):
    def body(w_ref, g_ref, m_ref, v_ref, d_ref, nm_ref, nv_ref):
        delta, nm, nv = _adam_math(w_ref[...], g_ref[...], m_ref[...], v_ref[...])
        d_ref[...] = delta
        nm_ref[...] = nm
        nv_ref[...] = nv

    return pl.pallas_call(body, name=name, out_shape=[jax.ShapeDtypeStruct(wv.shape, F32)] * 3)(wv, g, m, v)


BIG = ("in", "a", "b", "mix", "mq", "kv", "mo", "fi", "fo")
ROW_SHARDED = ("mix", "mq", "mo")
UNSHARDED = ("a", "b")
FFN_GROUPS = 4
SMALL_ROWS = 16


class _Plan:
    FUSED = ("in",)
    GATHER_ON = {"sb_fwd": ("a", "b", "mix", "kv", "mq", "mo", "conv", "fi"), "mm_mix": ("fo",)}
    SIBLING_ON = {"mm_d_hf": ("fo", "fi"), "mm_d_merged": ("mo", "mq", "kv"), "conv_bwd": ("mix", "a", "b"),
                  "mm_d_w_in1": ("in0",), "mm_d_h0": ("in1",)}
    CHIPS_ON = {"sb_bwd": ("fo", "fi", "mo", "mq", "kv", "mix", "a", "b"), "mm_d_h0": ("in0",),
                "rms_mix_bwd": ("in1",)}

    def __init__(self, shards, core):
        self.shards, self.core = shards, core
        self.w, self.parts, self.chip_sums, self.from_chips = {}, {}, {}, {}

    def gathering(self, k):
        return self.shards[k] if k in self.FUSED else None

    def comm(self, name):
        comms = []
        if name in self.GATHER_ON:
            comms.append(_gather_comm([self.shards[k] for k in self.GATHER_ON[name]]))
        if name in self.SIBLING_ON:
            comms.append(_sibling_comm([self.parts[k] for k in self.SIBLING_ON[name]]))
        if name in self.CHIPS_ON:
            comms.append(_chips_comm([self.chip_sums[k] for k in self.CHIPS_ON[name]]))
        return _join_comms(comms) if comms else None

    def landed(self, name, outs):
        outs = list(outs)
        for k in self.GATHER_ON.get(name, ()):
            self.set_weight(k, outs.pop(0))
        for k in self.SIBLING_ON.get(name, ()):
            self.chip_sums[k] = _sum_with_sibling(self.parts[k], outs.pop(0), self.core, name="sum_with_sibling_" + k)
        for k in self.CHIPS_ON.get(name, ()):
            self.from_chips[k] = outs.pop(0)

    def set_weight(self, k, gathered):
        _, a, b = gathered.shape
        if k in ROW_SHARDED:
            gathered = gathered.reshape(1, N_DEV * a, b)
        elif k in UNSHARDED:
            gathered = jnp.transpose(gathered, (1, 0, 2)).reshape(1, a, N_DEV * b)
        elif k == "fo":
            gathered = gathered.reshape(FFN_GROUPS, N_DEV * a // FFN_GROUPS, b)
        elif k == "conv":
            n_conv = CONV_WIDTH // N_DEV
            gathered = jnp.transpose(gathered[:, :3, :n_conv], (1, 0, 2)).reshape(3, CONV_WIDTH)
        self.w[k] = gathered

    def weight(self, k):
        return self.w[k]

    def grad(self, k, g):
        _, a, b = g.shape
        if k in ROW_SHARDED:
            g = g.reshape(N_DEV, a // N_DEV, b)
        elif k in UNSHARDED:
            g = jnp.transpose(g.reshape(a, N_DEV, b // N_DEV), (1, 0, 2))
        elif k == "fo":
            g = g.reshape(N_DEV, FFN_GROUPS * a // N_DEV, b)
        self.parts[k] = g


def kernel(x, mem, norm_mix, w_in, conv_w, w_branch_a, w_branch_b, w_mix_out, norm_mem_q, norm_mem_kv, w_mem_q, w_mem_kv, w_mem_o, norm_ffn, w_ffn_in, w_ffn_out, norm_final, loss_target, m_norm_mix, m_w_in, m_conv_w, m_w_branch_a, m_w_branch_b, m_w_mix_out, m_norm_mem_q, m_norm_mem_kv, m_w_mem_q, m_w_mem_kv, m_w_mem_o, m_norm_ffn, m_w_ffn_in, m_w_ffn_out, m_norm_final, v_norm_mix, v_w_in, v_conv_w, v_w_branch_a, v_w_branch_b, v_w_mix_out, v_norm_mem_q, v_norm_mem_kv, v_w_mem_q, v_w_mem_kv, v_w_mem_o, v_norm_ffn, v_w_ffn_in, v_w_ffn_out, v_norm_final):
    d = x.shape[-1]
    xi, yi, ci = lax.axis_index("x"), lax.axis_index("y"), lax.axis_index("c")
    core = jnp.reshape(ci, (1,)).astype(jnp.int32)
    chip = jnp.reshape(2 * xi + yi, (1,)).astype(jnp.int32)
    dev = 4 * xi + 2 * yi + ci

    big_w = dict(zip(BIG, (w_in, w_branch_a, w_branch_b, w_mix_out, w_mem_q, w_mem_kv, w_mem_o, w_ffn_in, w_ffn_out)))
    big_m = dict(zip(BIG, (m_w_in, m_w_branch_a, m_w_branch_b, m_w_mix_out, m_w_mem_q, m_w_mem_kv, m_w_mem_o, m_w_ffn_in, m_w_ffn_out)))
    big_v = dict(zip(BIG, (v_w_in, v_w_branch_a, v_w_branch_b, v_w_mix_out, v_w_mem_q, v_w_mem_kv, v_w_mem_o, v_w_ffn_in, v_w_ffn_out)))

    flip = lambda t, k: jnp.transpose(t) if k == "fi" else t
    shards = {k: flip(big_w[k][0], k).astype(BF16) for k in BIG}
    n_conv = conv_w.shape[-1]
    shards["conv"] = jnp.zeros((8, LANES), F32).at[:3, :n_conv].set(conv_w[0])
    plan = _Plan(shards, core)

    gains = (norm_mix, norm_mem_q, norm_mem_kv, norm_ffn, norm_final.reshape(1, d))
    dx0, small = _local_step(x[0], mem[0], loss_target[0], gains, plan)

    grads, deltas, new_m, new_v = {}, {}, {}, {}
    for k in BIG:
        lead = big_w[k].shape
        wv, mv, vv = flip(big_w[k][0], k), flip(big_m[k][0], k), flip(big_v[k][0], k)
        if k == "in":
            half = wv.shape[0] // 2
            lo = _adam_sharded(wv[:half], mv[:half], vv[:half], plan.chip_sums["in0"], plan.from_chips["in0"], chip,
                               name="adam_in0")
            hi = _adam_sharded(wv[half:], mv[half:], vv[half:], plan.chip_sums["in1"], plan.from_chips["in1"], chip,
                               name="adam_in1")
            outs = [jnp.concatenate(pair, axis=0) for pair in zip(lo, hi)]
        else:
            outs = _adam_sharded(wv, mv, vv, plan.chip_sums[k], plan.from_chips[k], chip, name="adam_" + k)
        grads[k], deltas[k], new_m[k], new_v[k] = (flip(t, k).reshape(lead) for t in outs)

    dg_mix, dg_memq, dg_memkv, dg_ffn, dg_fin, dconv_w, loss = small
    conv_rows = jnp.zeros((3, d), F32).at[:, :CONV_WIDTH].set(dconv_w[:3])
    block = jnp.concatenate([dg_mix[:1], dg_memq[:1], dg_memkv[:1], dg_ffn[:1], dg_fin[:1], conv_rows,
                             jnp.broadcast_to(loss[:1, :1], (1, d)), jnp.zeros((SMALL_ROWS - 9, d), F32)], axis=0)
    total = _sum_devices(_exchange(_gather_comm([block]), name="gather_small")[0], name="sum_small")
    g_conv = lax.dynamic_slice(total[5:8, :CONV_WIDTH], (0, dev * n_conv), (3, n_conv))
    small_w = [norm_mix, norm_mem_q, norm_mem_kv, norm_ffn, norm_final.reshape(1, d), conv_w[0]]
    small_m = [m_norm_mix, m_norm_mem_q, m_norm_mem_kv, m_norm_ffn, m_norm_final.reshape(1, d), m_conv_w[0]]
    small_v = [v_norm_mix, v_norm_mem_q, v_norm_mem_kv, v_norm_ffn, v_norm_final.reshape(1, d), v_conv_w[0]]
    small_g = [total[0:1], total[1:2], total[2:3], total[3:4], total[4:5], g_conv]
    small_names = ["norm_mix", "norm_mem_q", "norm_mem_kv", "norm_ffn", "norm_final", "conv_w"]
    sg, sd, sm, sv = {}, {}, {}, {}
    for nme, wv, g, m, v in zip(small_names, small_w, small_g, small_m, small_v):
        dl, nm, nv = _adam_small(wv, g, m, v, name="adam_" + nme)
        shape = norm_final.shape if nme == "norm_final" else (conv_w.shape if nme == "conv_w" else wv.shape)
        sg[nme], sd[nme], sm[nme], sv[nme] = (t.reshape(shape) for t in (g, dl, nm, nv))

    def ordered(big, sml):
        return (sml["norm_mix"], big["in"], sml["conv_w"], big["a"], big["b"], big["mix"], sml["norm_mem_q"],
                sml["norm_mem_kv"], big["mq"], big["kv"], big["mo"], sml["norm_ffn"], big["fi"], big["fo"],
                sml["norm_final"])

    loss_out = total[8, 0]
    grad_x = dx0.reshape(x.shape)
    return (loss_out, grad_x, *ordered(grads, sg), *ordered(deltas, sd), *ordered(new_m, sm), *ordered(new_v, sv))
```

```python
import functools
import math

import jax
import jax.numpy as jnp
from jax import lax
from jax.experimental import pallas as pl
from jax.experimental.pallas import tpu as pltpu

F32 = jnp.float32
BF16 = jnp.bfloat16
MESH = pl.DeviceIdType.MESH

N_DEV = 8
N_CHIP = 4
NORM_EPS = 1e-6
SB_HEADS = 8
SB_HEAD_DIM = 64
SB_WIDTH = SB_HEADS * SB_HEAD_DIM
CONV_WIDTH = 512
MEM_HEADS = 4
ADAM_LR = 0.001
ADAM_B1 = 0.9
ADAM_B2 = 0.999
ADAM_EPS = 1e-08
ADAM_WD = 0.01
ADAM_STEP = 10

LANES = 128
VMEM_LIMIT_BYTES = 52 * 1024 * 1024
SB_TILE = 256
SB_DEAD = 159.0
SB_CLAMP = 126.0
LOG2_E = 1.4426950408889634

ANY = pl.BlockSpec(memory_space=pl.ANY)


def _params(n_grid):
    return pltpu.CompilerParams(dimension_semantics=("arbitrary",) * n_grid, vmem_limit_bytes=VMEM_LIMIT_BYTES)


def _bdot(a, b, dims):
    return lax.dot_general(a.astype(BF16), b.astype(BF16), (dims, ((), ())), preferred_element_type=F32)


NN = ((1,), (0,))
NT = ((1,), (1,))
TN = ((0,), (0,))


class _Comm:
    def __init__(self, ins, outs, n_sems, start, finish):
        self.ins, self.outs, self.n_sems, self.start, self.finish = ins, outs, n_sems, start, finish

    def sem_shapes(self):
        return [pltpu.SemaphoreType.DMA((k,)) for k in self.n_sems]


def _place():
    return lax.axis_index("x"), lax.axis_index("y"), lax.axis_index("c")


def _gather_comm(shards):
    n = len(shards)

    def copies(ins, outs, sems):
        send_sems, recv_sems, _ = sems
        x, y, c = _place()
        chips = [(1 - x, y), (x, 1 - y), (1 - x, 1 - y)]

        def copy(a, k, block, to, from_shard=False):
            dst = outs[a].at[4 * block[0] + 2 * block[1] + block[2]]
            return pltpu.make_async_remote_copy(
                src_ref=ins[a] if from_shard else dst, dst_ref=dst, send_sem=send_sems.at[a * 7 + k],
                recv_sem=recv_sems.at[a * 7 + k], device_id=to, device_id_type=MESH)

        me, sibling = (x, y, c), (x, y, 1 - c)
        own = [[copy(a, 0, me, sibling, True)] + [copy(a, 1 + j, me, (*chip, c), True) for j, chip in enumerate(chips)]
               for a in range(n)]
        landed = [[copy(a, 1 + j, (*chip, c), me) for j, chip in enumerate(chips)] for a in range(n)]
        passed = [[copy(a, 4 + j, (*chip, c), sibling) for j, chip in enumerate(chips)] for a in range(n)]
        from_sibling = [[copy(a, 0, sibling, me)] + [copy(a, 4 + j, (*chip, 1 - c), me) for j, chip in enumerate(chips)]
                        for a in range(n)]
        local = [pltpu.make_async_copy(ins[a], outs[a].at[4 * x + 2 * y + c], sems[2].at[a]) for a in range(n)]
        return own, landed, passed, from_sibling, local

    def start(ins, outs, sems):
        own, _, _, _, local = copies(ins, outs, sems)
        for a in range(n):
            local[a].start()
            for cp in own[a]:
                cp.start()

    def finish(ins, outs, sems):
        own, landed, passed, from_sibling, local = copies(ins, outs, sems)
        for a in range(n):
            for arrived, onward in zip(landed[a], passed[a]):
                arrived.wait_recv()
                onward.start()
        for a in range(n):
            for cp in from_sibling[a]:
                cp.wait_recv()
        for a in range(n):
            for cp in own[a] + passed[a]:
                cp.wait_send()
            local[a].wait()

    outs = [jax.ShapeDtypeStruct((N_DEV,) + s.shape, s.dtype) for s in shards]
    return _Comm(list(shards), outs, (7 * n, 7 * n, n), start, finish)


def _sibling_comm(parts):
    n = len(parts)

    def copies(ins, outs, sems):
        x, y, c = _place()
        return [pltpu.make_async_remote_copy(
            src_ref=ins[a].at[2 * q + 1 - c], dst_ref=outs[a].at[q], send_sem=sems[0].at[a * N_CHIP + q],
            recv_sem=sems[1].at[a * N_CHIP + q], device_id=(x, y, 1 - c), device_id_type=MESH)
            for a in range(n) for q in range(N_CHIP)]

    def start(ins, outs, sems):
        for cp in copies(ins, outs, sems):
            cp.start()

    def finish(ins, outs, sems):
        cps = copies(ins, outs, sems)
        for cp in cps:
            cp.wait_recv()
        for cp in cps:
            cp.wait_send()

    outs = [jax.ShapeDtypeStruct((N_CHIP,) + p.shape[1:], p.dtype) for p in parts]
    return _Comm(list(parts), outs, (N_CHIP * n, N_CHIP * n), start, finish)


def _chips_comm(parts):
    n = len(parts)

    def copies(ins, outs, sems):
        x, y, c = _place()
        chips = [(1 - x, y), (x, 1 - y), (1 - x, 1 - y)]
        return [pltpu.make_async_remote_copy(
            src_ref=ins[a].at[2 * px + py], dst_ref=outs[a].at[j], send_sem=sems[0].at[a * 3 + j],
            recv_sem=sems[1].at[a * 3 + j], device_id=(px, py, c), device_id_type=MESH)
            for a in range(n) for j, (px, py) in enumerate(chips)]

    def start(ins, outs, sems):
        for cp in copies(ins, outs, sems):
            cp.start()

    def finish(ins, outs, sems):
        cps = copies(ins, outs, sems)
        for cp in cps:
            cp.wait_recv()
        for cp in cps:
            cp.wait_send()

    outs = [jax.ShapeDtypeStruct((3,) + p.shape[1:], p.dtype) for p in parts]
    return _Comm(list(parts), outs, (3 * n, 3 * n), start, finish)


def _join_comms(comms):
    if len(comms) == 1:
        return comms[0]

    def split(refs, counts):
        out, at = [], 0
        for n in counts:
            out.append(refs[at:at + n])
            at += n
        return out

    def each(method):
        def run(ins, outs, sems):
            parts = zip(comms, split(ins, [len(c.ins) for c in comms]), split(outs, [len(c.outs) for c in comms]),
                        split(sems, [len(c.n_sems) for c in comms]))
            for c, c_ins, c_outs, c_sems in parts:
                getattr(c, method)(c_ins, c_outs, c_sems)
        return run

    return _Comm([a for c in comms for a in c.ins], [o for c in comms for o in c.outs],
                 tuple(k for c in comms for k in c.n_sems), each("start"), each("finish"))


def _exchange(comm, *, name):
    n_ci, n_co = len(comm.ins), len(comm.outs)

    def kern(*refs):
        c_ins, c_outs, sems = refs[:n_ci], refs[n_ci:n_ci + n_co], refs[n_ci + n_co:]
        comm.start(c_ins, c_outs, sems)
        comm.finish(c_ins, c_outs, sems)

    return pl.pallas_call(kern, name=name, in_specs=[ANY] * n_ci, out_specs=[ANY] * n_co, out_shape=comm.outs,
                          scratch_shapes=comm.sem_shapes())(*comm.ins)


def _call(body, *, name, grid, in_specs, out_specs, out_shape, scratch, args, plan=None):
    comm = plan.comm(name) if plan is not None else None
    if comm is None:
        return list(pl.pallas_call(functools.partial(body), name=name, grid=grid, in_specs=in_specs,
                                   out_specs=out_specs, out_shape=out_shape, scratch_shapes=scratch,
                                   compiler_params=_params(len(grid)))(*args))
    n_in, n_out, n_scr, n_ci, n_co = len(in_specs), len(out_specs), len(scratch), len(comm.ins), len(comm.outs)

    def kern(*refs):
        ins, c_ins, refs = refs[:n_in], refs[n_in:n_in + n_ci], refs[n_in + n_ci:]
        outs, c_outs, refs = refs[:n_out], refs[n_out:n_out + n_co], refs[n_out + n_co:]
        scr, sems = refs[:n_scr], refs[n_scr:]
        ids = [pl.program_id(ax) for ax in range(len(grid))]
        first = functools.reduce(jnp.logical_and, [i == 0 for i in ids])
        last = functools.reduce(jnp.logical_and, [i == g - 1 for i, g in zip(ids, grid)])

        @pl.when(first)
        def _():
            comm.start(c_ins, c_outs, sems)
        body(*ins, *outs, *scr)

        @pl.when(last)
        def _():
            comm.finish(c_ins, c_outs, sems)

    res = pl.pallas_call(kern, name=name, grid=grid, in_specs=list(in_specs) + [ANY] * n_ci,
                         out_specs=list(out_specs) + [ANY] * n_co, out_shape=list(out_shape) + comm.outs,
                         scratch_shapes=list(scratch) + comm.sem_shapes(),
                         compiler_params=_params(len(grid)))(*args, *comm.ins)
    plan.landed(name, list(res[n_out:]))
    return list(res[:n_out])


def _mm_body(dims, has_add, *refs):
    if has_add:
        a_ref, b_ref, add_ref, o_ref = refs
        total = _bdot(a_ref[...], b_ref[...], dims) + add_ref[...]
    else:
        a_ref, b_ref, o_ref = refs
        total = _bdot(a_ref[...], b_ref[...], dims)
    o_ref[...] = total.astype(o_ref.dtype)


def _mm_nt_body(j, n, dy_ref, w_ref, o_ref):
    total = _bdot(dy_ref[:, 0:n], w_ref[0], NT)
    for jj in range(1, j):
        total = total + _bdot(dy_ref[:, jj * n:(jj + 1) * n], w_ref[jj], NT)
    o_ref[...] = total.astype(o_ref.dtype)


def _mm_nn(a, w3, *, name, out_dtype=BF16, add=None, tm=1024, tn=None, out3=False, w_t=False, plan=None):
    m, kk = a.shape
    j, n = w3.shape[0], w3.shape[1 if w_t else 2]
    tm, tn = min(tm, m), n if tn is None else tn
    n_t = n // tn
    in_specs = [pl.BlockSpec((tm, kk), lambda i, jj: (i, 0)),
                pl.BlockSpec((None, tn, kk), lambda i, jj: (jj // n_t, jj % n_t, 0)) if w_t else
                pl.BlockSpec((None, kk, tn), lambda i, jj: (jj // n_t, 0, jj % n_t))]
    args = [a, w3]
    if add is not None:
        in_specs.append(pl.BlockSpec((tm, tn), lambda i, jj: (i, jj)))
        args.append(add)
    if out3:
        out_spec = pl.BlockSpec((None, tm, tn), lambda i, jj: (jj // n_t, i, jj % n_t))
        out_shape = jax.ShapeDtypeStruct((j, m, n), out_dtype)
    else:
        out_spec = pl.BlockSpec((tm, tn), lambda i, jj: (i, jj))
        out_shape = jax.ShapeDtypeStruct((m, j * n), out_dtype)
    return _call(
        functools.partial(_mm_body, NT if w_t else NN, add is not None), name=name, grid=(m // tm, j * n_t),
        in_specs=in_specs, out_specs=[out_spec], out_shape=[out_shape], scratch=[], args=args, plan=plan)[0]


def _mm_gathering(a, shard, *, name, out3=False, w_t=False, tm=1024):
    m, kk = a.shape
    n = shard.shape[0 if w_t else 1]
    tm = min(tm, m)
    n_i = m // tm

    def body(a_ref, shard_ref, o_ref, w_all, w_vmem, send_sems, recv_sems, copy_sems):
        jj, i = pl.program_id(0), pl.program_id(1)
        x, y, c = _place()
        me, sibling = (x, y, c), (x, y, 1 - c)
        chips = [(jnp.bitwise_xor(x, c), jnp.bitwise_xor(y, 1 - c)), (jnp.bitwise_xor(x, 1 - c), jnp.bitwise_xor(y, c)),
                 (1 - x, 1 - y)]
        sibling_chips = [chips[1], chips[0], chips[2]]

        def rows(block):
            return w_all.at[4 * block[0] + 2 * block[1] + block[2]]

        def remote(k, block, to, from_shard=False):
            return pltpu.make_async_remote_copy(
                src_ref=shard_ref if from_shard else rows(block), dst_ref=rows(block), send_sem=send_sems.at[k],
                recv_sem=recv_sems.at[k], device_id=to, device_id_type=MESH)

        def load(src):
            cp = pltpu.make_async_copy(src, w_vmem, copy_sems.at[1])
            cp.start()
            cp.wait()

        own = [remote(0, me, sibling, True)] + [remote(1 + j, me, (*chip, c), True) for j, chip in enumerate(chips)]
        passed = [remote(4 + j, (*chip, c), sibling) for j, chip in enumerate(chips)]
        local = pltpu.make_async_copy(shard_ref, rows(me), copy_sems.at[0])

        @pl.when(jnp.logical_and(i == 0, jj == 0))
        def _():
            local.start()
            own[0].start()
            own[1].start()
            load(shard_ref)

        @pl.when(jnp.logical_and(i == 0, jj == 1))
        def _():
            remote(0, sibling, me).wait_recv()
            load(rows(sibling))

        for j, chip in enumerate(chips):
            @pl.when(jnp.logical_and(i == 0, jj == 2 + 2 * j))
            def _():
                if j < 2:
                    own[1 + j].wait_send()
                    own[2 + j].start()
                remote(1 + j, (*chip, c), me).wait_recv()
                passed[j].start()
                load(rows((*chip, c)))

            @pl.when(jnp.logical_and(i == 0, jj == 3 + 2 * j))
            def _():
                block = (*sibling_chips[j], 1 - c)
                remote(4 + j, block, me).wait_recv()
                load(rows(block))

        o_ref[...] = _bdot(a_ref[...], w_vmem[...], NT if w_t else NN).astype(o_ref.dtype)

        @pl.when(jnp.logical_and(i == n_i - 1, jj == N_DEV - 1))
        def _():
            for cp in [own[0], own[3]] + passed:
                cp.wait_send()
            local.wait()

    def swept(jj):
        x, y, c = _place()
        first, second = 2 + 2 * c, 4 - 2 * c
        flips = (0b000, 0b001, first, second + 1, second, first + 1, 0b110, 0b111)
        return jnp.bitwise_xor(4 * x + 2 * y + c, sum(jnp.where(jj == k, f, 0) for k, f in enumerate(flips)))

    if out3:
        out_spec = pl.BlockSpec((None, tm, n), lambda jj, i: (swept(jj), i, 0))
        out_shape = jax.ShapeDtypeStruct((N_DEV, m, n), BF16)
    else:
        out_spec = pl.BlockSpec((tm, n), lambda jj, i: (i, swept(jj)))
        out_shape = jax.ShapeDtypeStruct((m, N_DEV * n), BF16)
    return pl.pallas_call(
        body, name=name, grid=(N_DEV, n_i),
        in_specs=[pl.BlockSpec((tm, kk), lambda jj, i: (i, 0)), ANY], out_specs=[out_spec, ANY],
        scratch_shapes=[pltpu.VMEM(shard.shape, shard.dtype), pltpu.SemaphoreType.DMA((7,)),
                        pltpu.SemaphoreType.DMA((7,)), pltpu.SemaphoreType.DMA((2,))],
        out_shape=[out_shape, jax.ShapeDtypeStruct((N_DEV,) + shard.shape, shard.dtype)],
        compiler_params=_params(2))(a, shard)


def _sigmoid(v):
    return 0.5 * jnp.tanh(0.5 * v) + 0.5


def _ffn_out_loss(gu3, w3, add, g, target, *, name, tm=256):
    j2, m, n = gu3.shape
    j = j2 // 2
    nn = w3.shape[2]
    tm = min(tm, m)

    def body(gu_ref, w_ref, add_ref, g_ref, t_ref, dx_ref, dxb_ref, dg_ref, loss_ref, act_ref):
        i = pl.program_id(0)
        xv = add_ref[...]
        for jj in range(j):
            gate = gu_ref[0, jj].astype(F32)
            act = (gate * _sigmoid(gate) * gu_ref[1, jj].astype(F32)).astype(BF16)
            act_ref[jj] = act
            xv = xv + _bdot(act, w_ref[jj], NN)
        gv = g_ref[...]
        r = lax.rsqrt(jnp.mean(xv * xv, axis=-1, keepdims=True) + NORM_EPS)
        xhat = xv * r
        err = xhat * gv - t_ref[...]
        _acc_rows(i, loss_ref, 0.5 * jnp.sum(jnp.mean(err * err, axis=-1, keepdims=True), axis=0, keepdims=True))
        dy = err * (1.0 / nn)
        dxhat = dy * gv
        dx = r * (dxhat - xhat * jnp.mean(dxhat * xhat, axis=-1, keepdims=True))
        dx_ref[...] = dx
        dxb_ref[...] = dx.astype(BF16)
        _acc_rows(i, dg_ref, jnp.sum(dy * xhat, axis=0, keepdims=True))

    row = pl.BlockSpec((tm, nn), lambda i: (i, 0))
    return _call(body, name=name, grid=(m // tm,),
                 in_specs=[pl.BlockSpec((2, j, tm, n), lambda i: (0, 0, i, 0)), pl.BlockSpec(w3.shape, lambda i: (0, 0, 0)),
                           row, pl.BlockSpec(g.shape, lambda i: (0, 0)), row],
                 out_specs=[row, row, pl.BlockSpec((8, nn), lambda i: (0, 0)), pl.BlockSpec((8, LANES), lambda i: (0, 0)),
                            pl.BlockSpec((j, tm, n), lambda i: (0, i, 0))],
                 out_shape=[jax.ShapeDtypeStruct((m, nn), F32), jax.ShapeDtypeStruct((m, nn), BF16),
                            jax.ShapeDtypeStruct((8, nn), F32), jax.ShapeDtypeStruct((8, LANES), F32),
                            jax.ShapeDtypeStruct((j, m, n), BF16)],
                 scratch=[], args=[gu3.reshape(2, j, m, n), w3, add, g, target])


def _ffn_out_bwd(dy, w3, gu3, *, name, tm=1024):
    m, nn = dy.shape
    j, n, _ = w3.shape
    tm = min(tm, m)

    def body(dy_ref, w_ref, gu_ref, dgu_ref):
        da = _bdot(dy_ref[...], w_ref[...], NT)
        gate = gu_ref[0].astype(F32)
        up = gu_ref[1].astype(F32)
        sg = _sigmoid(gate)
        silu = gate * sg
        dgu_ref[0] = (da * up * (sg + silu * (1.0 - sg))).astype(BF16)
        dgu_ref[1] = (da * silu).astype(BF16)

    out = _call(body, name=name, grid=(m // tm, j),
                in_specs=[pl.BlockSpec((tm, nn), lambda i, jj: (i, 0)),
                          pl.BlockSpec((None, n, nn), lambda i, jj: (jj, 0, 0)),
                          pl.BlockSpec((2, None, tm, n), lambda i, jj: (0, jj, i, 0))],
                out_specs=[pl.BlockSpec((2, None, tm, n), lambda i, jj: (0, jj, i, 0))],
                out_shape=[jax.ShapeDtypeStruct((2, j, m, n), BF16)], scratch=[],
                args=[dy, w3, gu3.reshape(2, j, m, n)])[0]
    return out.reshape(2 * j, m, n)


def _rms_fwd_tail(xv, g_ref, h_ref):
    r = lax.rsqrt(jnp.mean(xv * xv, axis=-1, keepdims=True) + NORM_EPS)
    h_ref[...] = (xv * r * g_ref[...]).astype(BF16)


def _rms_bwd_tail(i, dh, x_ref, g_ref, dres_ref, dx_ref, dxb_ref, dg_ref):
    xv = x_ref[...]
    r = lax.rsqrt(jnp.mean(xv * xv, axis=-1, keepdims=True) + NORM_EPS)
    xhat = xv * r
    dxhat = dh * g_ref[...]
    dx = r * (dxhat - xhat * jnp.mean(dxhat * xhat, axis=-1, keepdims=True))
    if dres_ref is not None:
        dx = dx + dres_ref[...]
    dx_ref[...] = dx
    dxb_ref[...] = dx.astype(BF16)
    _acc_rows(i, dg_ref, jnp.sum(dh * xhat, axis=0, keepdims=True))


def _mm_nt_rms(dy, w3, x, g, dres, *, name, dy3=False, w_nn=False, tm=512, plan=None):
    j = w3.shape[0]
    m, kk = x.shape
    n = dy.shape[2] if dy3 else dy.shape[1] // j
    tm = min(tm, m)

    def body(dy_ref, w_ref, x_ref, g_ref, *rest):
        dres_ref = rest[0] if dres is not None else None
        dx_ref, dxb_ref, dg_ref = rest[-3:]
        dh = None
        for jj in range(j):
            piece = dy_ref[jj] if dy3 else dy_ref[:, jj * n:(jj + 1) * n]
            part = _bdot(piece, w_ref[jj], NN if w_nn else NT)
            dh = part if dh is None else dh + part
        _rms_bwd_tail(pl.program_id(0), dh, x_ref, g_ref, dres_ref, dx_ref, dxb_ref, dg_ref)

    row = pl.BlockSpec((tm, kk), lambda i: (i, 0))
    in_specs = [pl.BlockSpec((j, tm, n), lambda i: (0, i, 0)) if dy3 else pl.BlockSpec((tm, j * n), lambda i: (i, 0)),
                pl.BlockSpec(w3.shape, lambda i: (0, 0, 0)), row, pl.BlockSpec(g.shape, lambda i: (0, 0))]
    args = [dy, w3, x, g]
    if dres is not None:
        in_specs.append(row)
        args.append(dres)
    return _call(body, name=name, grid=(m // tm,), in_specs=in_specs,
                 out_specs=[row, row, pl.BlockSpec((8, kk), lambda i: (0, 0))],
                 out_shape=[jax.ShapeDtypeStruct((m, kk), F32), jax.ShapeDtypeStruct((m, kk), BF16),
                            jax.ShapeDtypeStruct((8, kk), F32)], scratch=[], args=args, plan=plan)


def _mix_out(br_a, br_b, proj, w, x, g, *, name, tm=512, plan=None):
    s, d = br_a.shape
    tm = min(tm, s)

    def body(a_ref, b_ref, ga_ref, gb_ref, w_ref, x_ref, g_ref, x1_ref, h_ref, merged_ref):
        merged = (_sigmoid(ga_ref[...].astype(F32)) * a_ref[...].astype(F32)
                  + _sigmoid(gb_ref[...].astype(F32)) * b_ref[...].astype(F32)).astype(BF16)
        merged_ref[...] = merged
        xv = _bdot(merged, w_ref[...], NN) + x_ref[...]
        x1_ref[...] = xv
        _rms_fwd_tail(xv, g_ref, h_ref)

    row = pl.BlockSpec((tm, d), lambda i: (i, 0))
    return _call(body, name=name, grid=(s // tm,),
                 in_specs=[row, row, pl.BlockSpec((tm, d), lambda i: (i, 3)), pl.BlockSpec((tm, d), lambda i: (i, 4)),
                           pl.BlockSpec(w.shape, lambda i: (0, 0)), row, pl.BlockSpec(g.shape, lambda i: (0, 0))],
                 out_specs=[row, row, row],
                 out_shape=[jax.ShapeDtypeStruct((s, d), F32), jax.ShapeDtypeStruct((s, d), BF16),
                            jax.ShapeDtypeStruct((s, d), BF16)],
                 scratch=[], args=[br_a, br_b, proj, proj, w, x, g], plan=plan)


def _mm_tn_a3(a3, dy, *, name):
    j, t, n = a3.shape
    nn = dy.shape[1]
    return _call(functools.partial(_mm_body, TN, False), name=name, grid=(j,),
                 in_specs=[pl.BlockSpec((None, t, n), lambda jj: (jj, 0, 0)), pl.BlockSpec((t, nn), lambda jj: (0, 0))],
                 out_specs=[pl.BlockSpec((None, n, nn), lambda jj: (jj, 0, 0))],
                 out_shape=[jax.ShapeDtypeStruct((j, n, nn), BF16)], scratch=[], args=[a3, dy])[0]


def _mm_nt(dy, w3, *, name, out_dtype=BF16, tm=512, tn=1024, plan=None):
    m = dy.shape[0]
    j, kk, n = w3.shape
    tm, tn = min(tm, m), min(tn, kk)
    return _call(
        functools.partial(_mm_nt_body, j, n), name=name,
        grid=(m // tm, kk // tn),
        in_specs=[pl.BlockSpec((tm, j * n), lambda i, q: (i, 0)),
                  pl.BlockSpec((j, tn, n), lambda i, q: (0, q, 0))],
        out_specs=[pl.BlockSpec((tm, tn), lambda i, q: (i, q))],
        out_shape=[jax.ShapeDtypeStruct((m, kk), out_dtype)], scratch=[], args=[dy, w3], plan=plan)[0]


def _mm_tn(a, dy, n, *, name, out_dtype=BF16, tm=512, tn=None, k_tiles=None, plan=None):
    t, kk = a.shape
    j = dy.shape[1] // n
    tm, tn = min(tm, kk), n if tn is None else tn
    n_t = n // tn
    first, count = (0, kk // tm) if k_tiles is None else k_tiles
    return _call(
        functools.partial(_mm_body, TN, False), name=name,
        grid=(count, j * n_t),
        in_specs=[pl.BlockSpec((t, tm), lambda i, jj: (0, first + i)),
                  pl.BlockSpec((t, tn), lambda i, jj: (0, jj))],
        out_specs=[pl.BlockSpec((None, tm, tn), lambda i, jj: (jj // n_t, i, jj % n_t))],
        out_shape=[jax.ShapeDtypeStruct((j, count * tm, n), out_dtype)], scratch=[], args=[a, dy], plan=plan)[0]


def _rows(body, ins, outs, *, n_rows, tm, name, plan=None):
    tm = min(tm, n_rows)
    n_steps = n_rows // tm
    in_specs, args = [], []
    for arr, kind, width, block in ins:
        if kind == "row":
            in_specs.append(pl.BlockSpec((tm, width), functools.partial(lambda i, b: (i, b), b=block)))
        elif kind == "prev":
            in_specs.append(pl.BlockSpec((tm, width), functools.partial(lambda i, b: (jnp.maximum(i - 1, 0), b), b=block)))
        elif kind == "next":
            in_specs.append(pl.BlockSpec((tm, width), functools.partial(lambda i, b: (jnp.minimum(i + 1, n_steps - 1), b), b=block)))
        else:
            in_specs.append(pl.BlockSpec(arr.shape, functools.partial(lambda i, nd: (0,) * nd, nd=arr.ndim)))
        args.append(arr)
    out_specs, out_shape = [], []
    for shape, dtype, kind in outs:
        if kind == "row":
            out_specs.append(pl.BlockSpec((tm, shape[1]), lambda i: (i, 0)))
        else:
            out_specs.append(pl.BlockSpec(shape, functools.partial(lambda i, nd: (0,) * nd, nd=len(shape))))
        out_shape.append(jax.ShapeDtypeStruct(shape, dtype))

    def kern(*refs):
        body(pl.program_id(0), n_steps, *refs)

    return _call(kern, name=name, grid=(n_steps,), in_specs=in_specs, out_specs=out_specs, out_shape=out_shape,
                 scratch=[], args=args, plan=plan)


def _acc_rows(i, ref, value):
    @pl.when(i == 0)
    def _():
        ref[...] = jnp.zeros_like(ref)
    ref[...] += jnp.broadcast_to(value, ref.shape)


def _rms_fwd(x, g, *, name, tm=512):
    s, d = x.shape

    def body(i, n, x_ref, g_ref, h_ref):
        _rms_fwd_tail(x_ref[...], g_ref, h_ref)

    return _rows(body, [(x, "row", d, 0), (g, "full", 0, 0)], [((s, d), BF16, "row")], n_rows=s, tm=tm, name=name)[0]


def _rms_bwd(x, g, dh, dres, *, name, tm=512, plan=None):
    s, d = x.shape

    def body(i, n, x_ref, g_ref, dh_ref, dres_ref, dx_ref, dxb_ref, dg_ref):
        _rms_bwd_tail(i, dh_ref[...].astype(F32), x_ref, g_ref, dres_ref, dx_ref, dxb_ref, dg_ref)

    return _rows(body, [(x, "row", d, 0), (g, "full", 0, 0), (dh, "row", d, 0), (dres, "row", d, 0)],
                 [((s, d), F32, "row"), ((s, d), BF16, "row"), ((8, d), F32, "acc")],
                 n_rows=s, tm=tm, name=name, plan=plan)


def _mix_out_bwd(dx1b, w, br_a, br_b, proj, *, name, tm=512, plan=None):
    s, d = br_a.shape
    tm = min(tm, s)

    def body(dy_ref, w_ref, a_ref, b_ref, ga_ref, gb_ref, da_ref, db_ref, dg_ref):
        dm = _bdot(dy_ref[...], w_ref[...], NT)
        sa = _sigmoid(ga_ref[...].astype(F32))
        sb = _sigmoid(gb_ref[...].astype(F32))
        da_ref[...] = (dm * sa).astype(BF16)
        db_ref[...] = (dm * sb).astype(BF16)
        dg_ref[:, :d] = (dm * a_ref[...].astype(F32) * sa * (1.0 - sa)).astype(BF16)
        dg_ref[:, d:] = (dm * b_ref[...].astype(F32) * sb * (1.0 - sb)).astype(BF16)

    row = pl.BlockSpec((tm, d), lambda i: (i, 0))
    return _call(body, name=name, grid=(s // tm,),
                 in_specs=[row, pl.BlockSpec(w.shape, lambda i: (0, 0)), row, row,
                           pl.BlockSpec((tm, d), lambda i: (i, 3)), pl.BlockSpec((tm, d), lambda i: (i, 4))],
                 out_specs=[row, row, pl.BlockSpec((tm, 2 * d), lambda i: (i, 0))],
                 out_shape=[jax.ShapeDtypeStruct((s, d), BF16), jax.ShapeDtypeStruct((s, d), BF16),
                            jax.ShapeDtypeStruct((s, 2 * d), BF16)],
                 scratch=[], args=[dx1b, w, br_a, br_b, proj, proj], plan=plan)


def _shift_down(cur, prev, k, first):
    row = lax.broadcasted_iota(jnp.int32, cur.shape, 0)
    out = jnp.where(row >= k, pltpu.roll(cur, k, 0), pltpu.roll(prev, k, 0))
    return jnp.where(jnp.logical_and(first, row < k), 0.0, out)


def _shift_up(cur, nxt, k, last):
    tm = cur.shape[0]
    row = lax.broadcasted_iota(jnp.int32, cur.shape, 0)
    out = jnp.where(row < tm - k, pltpu.roll(cur, tm - k, 0), pltpu.roll(nxt, tm - k, 0))
    return jnp.where(jnp.logical_and(last, row >= tm - k), 0.0, out)


def _conv_fwd(proj, conv_w, *, name, tm=512):
    s = proj.shape[0]
    c = CONV_WIDTH

    def body(i, n, u_ref, gb_ref, gc_ref, up_ref, gcp_ref, w_ref, y_ref):
        cu = gc_ref[...].astype(F32) * u_ref[...].astype(F32)
        cup = gcp_ref[...].astype(F32) * up_ref[...].astype(F32)
        first = i == 0
        y = (w_ref[0:1, :] * _shift_down(cu, cup, 2, first) + w_ref[1:2, :] * _shift_down(cu, cup, 1, first)
             + w_ref[2:3, :] * cu)
        y_ref[...] = (gb_ref[...].astype(F32) * y).astype(BF16)

    return _rows(body, [(proj, "row", c, 3), (proj, "row", c, 4), (proj, "row", c, 5),
                        (proj, "prev", c, 3), (proj, "prev", c, 5), (conv_w, "full", 0, 0)],
                 [((s, c), BF16, "row")], n_rows=s, tm=tm, name=name)[0]


def _conv_bwd(dy_b, proj, conv_w, *, name, tm=512, plan=None):
    s = proj.shape[0]
    c = CONV_WIDTH

    def body(i, n, dy_ref, u_ref, gb_ref, gc_ref, up_ref, gcp_ref, dyn_ref, gbn_ref, w_ref, d_ref, dw_ref):
        first, last = i == 0, i == n - 1
        u = u_ref[...].astype(F32)
        gb = gb_ref[...].astype(F32)
        gc = gc_ref[...].astype(F32)
        cu = gc * u
        cup = gcp_ref[...].astype(F32) * up_ref[...].astype(F32)
        cu1 = _shift_down(cu, cup, 1, first)
        cu2 = _shift_down(cu, cup, 2, first)
        conv = w_ref[0:1, :] * cu2 + w_ref[1:2, :] * cu1 + w_ref[2:3, :] * cu
        dy = dy_ref[...].astype(F32)
        dyc = dy * gb
        dycn = dyn_ref[...].astype(F32) * gbn_ref[...].astype(F32)
        dcu = (w_ref[2:3, :] * dyc + w_ref[1:2, :] * _shift_up(dyc, dycn, 1, last)
               + w_ref[0:1, :] * _shift_up(dyc, dycn, 2, last))
        d_ref[:, 0:c] = (dcu * gc).astype(BF16)
        d_ref[:, c:2 * c] = (dy * conv).astype(BF16)
        d_ref[:, 2 * c:3 * c] = (dcu * u).astype(BF16)
        row = lax.broadcasted_iota(jnp.int32, (8, c), 0)
        dw = (jnp.where(row == 0, jnp.sum(dyc * cu2, axis=0, keepdims=True), 0.0)
              + jnp.where(row == 1, jnp.sum(dyc * cu1, axis=0, keepdims=True), 0.0)
              + jnp.where(row == 2, jnp.sum(dyc * cu, axis=0, keepdims=True), 0.0))

        @pl.when(first)
        def _():
            dw_ref[...] = jnp.zeros_like(dw_ref)
        dw_ref[...] += dw

    return _rows(body, [(dy_b, "row", c, 0), (proj, "row", c, 3), (proj, "row", c, 4), (proj, "row", c, 5),
                        (proj, "prev", c, 3), (proj, "prev", c, 5), (dy_b, "next", c, 0), (proj, "next", c, 4),
                        (conv_w, "full", 0, 0)],
                 [((s, 3 * c), BF16, "row"), ((8, c), F32, "acc")], n_rows=s, tm=tm, name=name, plan=plan)


def _mem_probs(q, k, scale):
    sc = _bdot(q, k, NT) * scale
    sc = sc - jnp.max(sc, axis=-1, keepdims=True)
    p = jnp.exp(sc)
    return p / jnp.sum(p, axis=-1, keepdims=True)


def _mem_sublayer(hq, w_q, kv, w_o, x, g, *, name, tm=512, plan=None):
    s, d = hq.shape
    hd = d // MEM_HEADS
    scale = 1.0 / math.sqrt(hd)
    tm = min(tm, s)

    def body(hq_ref, wq_ref, kv_ref, wo_ref, x_ref, g_ref, q_ref, o_ref, x2_ref, h_ref):
        q_ref[...] = _bdot(hq_ref[...], wq_ref[...], NN).astype(BF16)
        for h in range(MEM_HEADS):
            cols = slice(h * hd, (h + 1) * hd)
            p = _mem_probs(q_ref[:, cols], kv_ref[:, cols], scale)
            o_ref[:, cols] = _bdot(p, kv_ref[:, d + h * hd:d + (h + 1) * hd], NN).astype(BF16)
        xv = _bdot(o_ref[...], wo_ref[...], NN) + x_ref[...]
        x2_ref[...] = xv
        _rms_fwd_tail(xv, g_ref, h_ref)

    row = pl.BlockSpec((tm, d), lambda i: (i, 0))
    whole = lambda a: pl.BlockSpec(a.shape, lambda i: (0,) * a.ndim)
    return _call(body, name=name, grid=(s // tm,),
                 in_specs=[row, whole(w_q), whole(kv), whole(w_o), row, whole(g)], out_specs=[row] * 4,
                 out_shape=[jax.ShapeDtypeStruct((s, d), BF16), jax.ShapeDtypeStruct((s, d), BF16),
                            jax.ShapeDtypeStruct((s, d), F32), jax.ShapeDtypeStruct((s, d), BF16)],
                 scratch=[], args=[hq, w_q, kv, w_o, x, g], plan=plan)


def _mem_sublayer_bwd(dx2b, dx2, x, g, qm, kv, w_q, w_o, *, name, tm=512):
    s, d = qm.shape
    hd = d // MEM_HEADS
    scale = 1.0 / math.sqrt(hd)
    tm = min(tm, s)

    def body(dyb_ref, dres_ref, x_ref, g_ref, q_ref, kv_ref, wq_ref, wo_ref, dx_ref, dxb_ref, dg_ref, dq_ref, dkv_ref):
        i = pl.program_id(0)

        @pl.when(i == 0)
        def _():
            dkv_ref[...] = jnp.zeros_like(dkv_ref)
        dom = _bdot(dyb_ref[...], wo_ref[...], NT).astype(BF16)
        for h in range(MEM_HEADS):
            cols = slice(h * hd, (h + 1) * hd)
            vcols = slice(d + h * hd, d + (h + 1) * hd)
            q, k, v, do = q_ref[:, cols], kv_ref[:, cols], kv_ref[:, vcols], dom[:, cols]
            p = _mem_probs(q, k, scale)
            dp = _bdot(do, v, NT)
            ds = p * (dp - jnp.sum(dp * p, axis=-1, keepdims=True)) * scale
            dq_ref[:, cols] = _bdot(ds, k, NN).astype(BF16)
            dkv_ref[:, cols] += _bdot(ds, q, TN)
            dkv_ref[:, vcols] += _bdot(p, do, TN)
        dh = _bdot(dq_ref[...], wq_ref[...], NT)
        _rms_bwd_tail(i, dh, x_ref, g_ref, dres_ref, dx_ref, dxb_ref, dg_ref)

    row = pl.BlockSpec((tm, d), lambda i: (i, 0))
    whole = lambda a: pl.BlockSpec(a.shape, lambda i: (0,) * a.ndim)
    return _call(body, name=name, grid=(s // tm,),
                 in_specs=[row, row, row, whole(g), row, whole(kv), whole(w_q), whole(w_o)],
                 out_specs=[row, row, pl.BlockSpec((8, d), lambda i: (0, 0)), row, whole(kv)],
                 out_shape=[jax.ShapeDtypeStruct((s, d), F32), jax.ShapeDtypeStruct((s, d), BF16),
                            jax.ShapeDtypeStruct((8, d), F32), jax.ShapeDtypeStruct((s, d), BF16),
                            jax.ShapeDtypeStruct(kv.shape, F32)],
                 scratch=[], args=[dx2b, dx2, x, g, qm, kv, w_q, w_o])


def _sb_consts(t):
    row = lax.broadcasted_iota(jnp.int32, (t, t), 0)
    col = lax.broadcasted_iota(jnp.int32, (t, t), 1)
    lane = lax.broadcasted_iota(jnp.int32, (t, LANES), 1)
    return row, col, lane < SB_HEAD_DIM


def _sb_logits(q, k):
    z2 = jnp.minimum(_bdot(q, k, NT) * LOG2_E, SB_CLAMP)
    return z2, jnp.exp2(z2)


def _tri_sum(v, tri):
    hi = v.astype(BF16)
    lo = (v - hi.astype(F32)).astype(BF16)
    return _bdot(hi, tri, NN) + _bdot(lo, tri, NN)


def _sb_fwd(proj, *, name, plan=None):
    s = proj.shape[0]
    t = SB_TILE
    n_q = s // t
    scale = 1.0 / math.sqrt(SB_HEAD_DIM)
    k_blk, v_blk = SB_WIDTH // LANES, 2 * SB_WIDTH // LANES

    def body(q_ref, k_ref, v_ref, o_ref, c_ref, first_ref, acc_ref, c_scr):
        i = pl.program_id(1)
        row, col, head0 = _sb_consts(t)
        later = (row > col).astype(BF16)
        valid = col < row
        qs = q_ref[...] * scale
        q2 = (jnp.where(head0, qs, 0), jnp.where(head0, 0, qs))

        def tiles(kbs, diag_first, carry):
            kt = [k_ref[pl.ds(pl.multiple_of(kb * t, t), t), :] for kb in kbs]
            vt = [v_ref[pl.ds(pl.multiple_of(kb * t, t), t), :] for kb in kbs]
            jobs = [(n, h) for n in range(len(kbs)) for h in range(2)]
            masked = lambda n: diag_first and n == 0
            zs = {(n, h): _sb_logits(q2[h], kt[n]) for n, h in jobs}
            fail = {j: jnp.log2(1.0 + zs[j][1]) for j in jobs}
            fail = {j: jnp.where(valid, fail[j], 0.0) if masked(j[0]) else fail[j] for j in jobs}
            cum = {j: _tri_sum(fail[j], later) for j in jobs}
            run, before = list(carry), {}
            for n, h in jobs:
                before[n, h] = run[h]
                run[h] = run[h] + cum[n, h][:, 0:1] + fail[n, h][:, 0:1]
            w = {j: jnp.exp2(zs[j][0] - fail[j] - cum[j] - before[j]) for j in jobs}
            w = {j: jnp.where(valid, w[j], 0.0) if masked(j[0]) else w[j] for j in jobs}
            for n, h in jobs:
                acc_ref[h] += _bdot(w[n, h], vt[n], NN)
            return tuple(run)

        acc_ref[...] = jnp.zeros_like(acc_ref)
        zero = jnp.zeros((t, 1), F32)

        def alive(carry):
            return (jnp.minimum(jnp.min(carry[0]), jnp.min(carry[1])) < SB_DEAD).astype(jnp.int32)

        def step(state):
            kb, _, c0, c1 = state
            new = tiles([kb], False, (c0, c1))
            return kb - 1, alive(new), new[0], new[1]

        @pl.when(i == 0)
        def _():
            c_scr[0], c_scr[1] = tiles([i], True, (zero, zero))

        @pl.when(i > 0)
        def _():
            c_scr[0], c_scr[1] = tiles([i, i - 1], True, (zero, zero))
        carry = (c_scr[0], c_scr[1])
        kb, _, c0, c1 = lax.while_loop(lambda st: jnp.logical_and(st[0] >= 0, st[1] > 0), step,
                                       (i - 2, alive(carry), carry[0], carry[1]))
        kb = jnp.maximum(kb, -1)
        o_ref[...] = jnp.where(head0, acc_ref[0], acc_ref[1]).astype(BF16)
        c_ref[...] = jnp.where(lax.broadcasted_iota(jnp.int32, (t, 2), 1) == 0, c0, c1)
        first_ref[pl.program_id(0), i] = (kb + 1).astype(F32)

    return _call(
        body, name=name, grid=(SB_HEADS // 2, n_q),
        in_specs=[pl.BlockSpec((t, LANES), lambda p, i: (i, p)),
                  pl.BlockSpec((s, LANES), lambda p, i: (0, k_blk + p)),
                  pl.BlockSpec((s, LANES), lambda p, i: (0, v_blk + p))],
        out_specs=[pl.BlockSpec((t, LANES), lambda p, i: (i, p)),
                   pl.BlockSpec((None, t, 2), lambda p, i: (p, i, 0)),
                   pl.BlockSpec(memory_space=pltpu.SMEM)],
        out_shape=[jax.ShapeDtypeStruct((s, SB_WIDTH), BF16), jax.ShapeDtypeStruct((SB_HEADS // 2, s, 2), F32),
                   jax.ShapeDtypeStruct((SB_HEADS // 2, n_q), F32)],
        scratch=[pltpu.VMEM((2, t, LANES), F32), pltpu.VMEM((2, t, 1), F32)], args=[proj, proj, proj], plan=plan)


def _sb_bwd(proj, do_a, ctot, first, *, name, plan=None):
    s = proj.shape[0]
    t = SB_TILE
    n_q = s // t
    scale = 1.0 / math.sqrt(SB_HEAD_DIM)
    k_blk, v_blk = SB_WIDTH // LANES, 2 * SB_WIDTH // LANES

    def body(q_ref, k_ref, v_ref, do_ref, c_ref, first_ref, dq_ref, dk_ref, dv_ref, dq_acc, dk_acc, dv_acc):
        i = pl.program_id(1)
        kb0 = jnp.clip(first_ref[pl.program_id(0), i].astype(jnp.int32), 0, i)
        row, col, head0 = _sb_consts(t)
        upto = (row <= col).astype(BF16)
        before = (row < col).astype(BF16)
        valid = col < row
        qs = q_ref[...] * scale
        q2 = (jnp.where(head0, qs, 0), jnp.where(head0, 0, qs))
        do = do_ref[...]
        do2 = (jnp.where(head0, do, 0), jnp.where(head0, 0, do))
        ctot2 = (c_ref[:, 0:1], c_ref[:, 1:2])

        @pl.when(i == 0)
        def _():
            dk_acc[...] = jnp.zeros_like(dk_acc)
            dv_acc[...] = jnp.zeros_like(dv_acc)
        dq_acc[...] = jnp.zeros_like(dq_acc)

        def tiles(kbs, diag_last, carry):
            rows = [pl.ds(pl.multiple_of(kb * t, t), t) for kb in kbs]
            kt = [k_ref[r, :] for r in rows]
            vt = [v_ref[r, :] for r in rows]
            jobs = [(n, h) for n in range(len(kbs)) for h in range(2)]
            masked = lambda n: diag_last and n == len(kbs) - 1
            t_last = slice(t - 1, t)
            zs = {(n, h): _sb_logits(q2[h], kt[n]) for n, h in jobs}
            dw = {(n, h): _bdot(do2[h], vt[n], NT) for n, h in jobs}
            fail = {j: jnp.log2(1.0 + zs[j][1]) for j in jobs}
            fail = {j: jnp.where(valid, fail[j], 0.0) if masked(j[0]) else fail[j] for j in jobs}
            cum = {j: _tri_sum(fail[j], upto) for j in jobs}
            miss = {j: jnp.exp2(-fail[j]) for j in jobs}
            beta = {j: zs[j][1] * miss[j] for j in jobs}
            fail_run, fail_before = list(carry[0::2]), {}
            for n, h in jobs:
                fail_before[n, h] = fail_run[h]
                fail_run[h] = fail_run[h] + cum[n, h][:, t_last]
            w = {(n, h): beta[n, h] * jnp.exp2(fail_before[n, h] + cum[n, h] - ctot2[h]) for n, h in jobs}
            w = {j: jnp.where(valid, w[j], 0.0) if masked(j[0]) else w[j] for j in jobs}
            g = {j: w[j] * dw[j] for j in jobs}
            g_local = {j: _bdot(g[j], before, NN) for j in jobs}
            for n, h in jobs:
                dv_acc[rows[n], :] += _bdot(w[n, h], do2[h], TN)
            g_run, dz = list(carry[1::2]), {}
            for n, h in jobs:
                g_sum = g_run[h] + g_local[n, h]
                dz[n, h] = g[n, h] * miss[n, h] - beta[n, h] * g_sum
                g_run[h] = g_sum[:, t_last] + g[n, h][:, t_last]
            dz = {j: jnp.where(valid, dz[j], 0.0) if masked(j[0]) else dz[j] for j in jobs}
            for n, h in jobs:
                dq_acc[h] += _bdot(dz[n, h], kt[n], NN)
                dk_acc[rows[n], :] += _bdot(dz[n, h], q2[h], TN)
            return fail_run[0], g_run[0], fail_run[1], g_run[1]

        zero = jnp.zeros((t, 1), F32)
        carry = lax.fori_loop(kb0, i - 1, lambda n, c: tiles([n], False, c), (zero,) * 4)

        @pl.when(i == 0)
        def _():
            tiles([i], True, carry)

        @pl.when(i > 0)
        def _():
            tiles([i - 1, i], True, carry)
        dq_ref[...] = (jnp.where(head0, dq_acc[0], dq_acc[1]) * scale).astype(BF16)

        @pl.when(i == n_q - 1)
        def _():
            dk_ref[...] = dk_acc[...].astype(BF16)
            dv_ref[...] = dv_acc[...].astype(BF16)

    outs = _call(
        body, name=name, grid=(SB_HEADS // 2, n_q),
        in_specs=[pl.BlockSpec((t, LANES), lambda p, i: (i, p)),
                  pl.BlockSpec((s, LANES), lambda p, i: (0, k_blk + p)),
                  pl.BlockSpec((s, LANES), lambda p, i: (0, v_blk + p)),
                  pl.BlockSpec((t, LANES), lambda p, i: (i, p)),
                  pl.BlockSpec((None, t, 2), lambda p, i: (p, i, 0)),
                  pl.BlockSpec(memory_space=pltpu.SMEM)],
        out_specs=[pl.BlockSpec((t, LANES), lambda p, i: (i, p)),
                   pl.BlockSpec((s, LANES), lambda p, i: (0, p)),
                   pl.BlockSpec((s, LANES), lambda p, i: (0, p))],
        out_shape=[jax.ShapeDtypeStruct((s, SB_WIDTH), BF16)] * 3,
        scratch=[pltpu.VMEM((2, t, LANES), F32), pltpu.VMEM((s, LANES), F32), pltpu.VMEM((s, LANES), F32)],
        args=[proj, proj, proj, do_a, ctot, first], plan=plan)
    return jnp.concatenate(outs, axis=1)


def _mm_gathered(a, key, plan, *, name, out3=False, w_t=False):
    src = plan.gathering(key)
    if src is None:
        return _mm_nn(a, plan.weight(key), name=name, out3=out3, w_t=w_t)
    out, w_all = _mm_gathering(a, src, name=name, out3=out3, w_t=w_t)
    plan.set_weight(key, w_all)
    return out


def _local_step(x, mem, target, gains, plan):
    g_mix, g_memq, g_memkv, g_ffn, g_fin = gains
    d = x.shape[1]

    h0 = _rms_fwd(x, g_mix, name="rms_mix")
    proj = _mm_gathered(h0, "in", plan, name="mm_in")
    w_in = plan.weight("in")
    o_a, ctot, first = _sb_fwd(proj, name="sb_fwd", plan=plan)
    conv_w = plan.weight("conv")
    y_b = _conv_fwd(proj, conv_w, name="conv_fwd")
    w_a, w_b, w_mix = plan.weight("a"), plan.weight("b"), plan.weight("mix")
    br_a = _mm_nn(o_a, w_a, name="mm_branch_a")
    br_b = _mm_nn(y_b, w_b, name="mm_branch_b")
    x1, hq, merged = _mix_out(br_a, br_b, proj, w_mix[0], x, g_memq, name="mm_mix", plan=plan)
    w_mq, w_kv, w_mo = plan.weight("mq")[0], plan.weight("kv"), plan.weight("mo")[0]
    mn = _rms_fwd(mem, g_memkv, name="rms_memkv")
    kv = _mm_nn(mn, w_kv, name="mm_memkv")
    qm, om, x2, hf = _mem_sublayer(hq, w_mq, kv, w_mo, x1, g_ffn, name="mem_sublayer", plan=plan)
    gu = _mm_gathered(hf, "fi", plan, name="mm_ffn_in", out3=True, w_t=True)
    w_fi, w_fo = plan.weight("fi"), plan.weight("fo")
    dx3, dx3b, dg_fin, loss, act = _ffn_out_loss(gu, w_fo, x2, g_fin, target, name="mm_ffn_out")

    plan.grad("fo", _mm_tn_a3(act, dx3b, name="mm_d_w_ffn_out"))
    dgu = _ffn_out_bwd(dx3b, w_fo, gu, name="mm_d_act")
    plan.grad("fi", _mm_tn_a3(dgu, hf, name="mm_d_w_ffn_in"))
    dx2, dx2b, dg_ffn = _mm_nt_rms(dgu, w_fi, x2, g_ffn, dx3, name="mm_d_hf", dy3=True, w_nn=True, tm=256, plan=plan)

    plan.grad("mo", _mm_tn(om, dx2b, d, name="mm_d_w_memo"))
    dx1, dx1b, dg_memq, dqm, dkv = _mem_sublayer_bwd(dx2b, dx2, x1, g_memq, qm, kv, w_mq, w_mo, name="mem_sublayer_bwd")
    plan.grad("mq", _mm_tn(hq, dqm, d, name="mm_d_w_memq"))
    plan.grad("kv", _mm_tn(mn, dkv, w_kv.shape[2], name="mm_d_w_memkv"))
    _, _, dg_memkv = _mm_nt_rms(dkv, w_kv, mem, g_memkv, None, name="mm_d_mn")

    plan.grad("mix", _mm_tn(merged, dx1b, d, name="mm_d_w_mix"))
    dbr_a, dbr_b, dgab = _mix_out_bwd(dx1b, w_mix[0], br_a, br_b, proj, name="mm_d_merged", plan=plan)
    plan.grad("a", _mm_tn(o_a, dbr_a, d, name="mm_d_w_branch_a"))
    do_a = _mm_nt(dbr_a, w_a, name="mm_d_o_a")
    plan.grad("b", _mm_tn(y_b, dbr_b, d, name="mm_d_w_branch_b"))
    dy_b = _mm_nt(dbr_b, w_b, name="mm_d_y_b")
    dconv, dconv_w = _conv_bwd(dy_b, proj, conv_w, name="conv_bwd", plan=plan)
    dqkv = _sb_bwd(proj, do_a, ctot, first, name="sb_bwd", plan=plan)
    dproj = jnp.concatenate([dqkv, dconv, dgab], axis=1)
    tile_in = d // IN_SPLIT[1]
    plan.grad("in0", _mm_tn(h0, dproj, w_in.shape[2], name="mm_d_w_in0", tm=tile_in, k_tiles=(0, IN_SPLIT[0])))
    plan.grad("in1", _mm_tn(h0, dproj, w_in.shape[2], name="mm_d_w_in1", tm=tile_in,
                            k_tiles=(IN_SPLIT[0], IN_SPLIT[1] - IN_SPLIT[0]), plan=plan))
    dh0 = _mm_nt(dproj, w_in, name="mm_d_h0", out_dtype=F32, plan=plan)
    dx0, _, dg_mix = _rms_bwd(x, g_mix, dh0, dx1, name="rms_mix_bwd", plan=plan)

    return dx0, (dg_mix, dg_memq, dg_memkv, dg_ffn, dg_fin, dconv_w, loss)


def _row_tile(a, target=512):
    tm = min(a, target)
    while a % tm:
        tm -= 8
    return tm


def _sum_with_sibling(part, recv, core, *, name):
    _, a, b = part.shape
    tm = _row_tile(a)

    def body(core_ref, p_ref, r_ref, o_ref):
        o_ref[...] = (p_ref[...].astype(F32) + r_ref[...].astype(F32)).astype(o_ref.dtype)

    return pl.pallas_call(
        body, name=name,
        grid_spec=pltpu.PrefetchScalarGridSpec(
            num_scalar_prefetch=1, grid=(N_CHIP, a // tm),
            in_specs=[pl.BlockSpec((None, tm, b), lambda q, i, core_ref: (2 * q + core_ref[0], i, 0)),
                      pl.BlockSpec((None, tm, b), lambda q, i, core_ref: (q, i, 0))],
            out_specs=pl.BlockSpec((None, tm, b), lambda q, i, core_ref: (q, i, 0))),
        out_shape=jax.ShapeDtypeStruct((N_CHIP, a, b), part.dtype), compiler_params=_params(2))(core, part, recv)


def _adam_math(wv, g, m, v):
    m = ADAM_B1 * m + (1.0 - ADAM_B1) * g
    v = ADAM_B2 * v + (1.0 - ADAM_B2) * (g * g)
    m_hat = m / (1.0 - ADAM_B1 ** ADAM_STEP)
    v_hat = v / (1.0 - ADAM_B2 ** ADAM_STEP)
    delta = -ADAM_LR * (m_hat / (jnp.sqrt(v_hat) + ADAM_EPS) + ADAM_WD * wv)
    return delta, m, v


def _adam_sharded(wv, m, v, own, recv, chip, *, name):
    a, b = wv.shape
    tm = _row_tile(a)

    def body(chip_ref, w_ref, m_ref, v_ref, own_ref, recv_ref, g_ref, d_ref, nm_ref, nv_ref):
        g = own_ref[...].astype(F32)
        for j in range(3):
            g = g + recv_ref[j].astype(F32)
        delta, nm, nv = _adam_math(w_ref[...], g, m_ref[...], v_ref[...])
        g_ref[...] = g
        d_ref[...] = delta
        nm_ref[...] = nm
        nv_ref[...] = nv

    tile = pl.BlockSpec((tm, b), lambda i, chip_ref: (i, 0))
    return pl.pallas_call(
        body, name=name,
        grid_spec=pltpu.PrefetchScalarGridSpec(
            num_scalar_prefetch=1, grid=(a // tm,),
            in_specs=[tile, tile, tile,
                      pl.BlockSpec((None, tm, b), lambda i, chip_ref: (chip_ref[0], i, 0)),
                      pl.BlockSpec((3, tm, b), lambda i, chip_ref: (0, i, 0))],
            out_specs=[tile] * 4),
        out_shape=[jax.ShapeDtypeStruct((a, b), F32)] * 4, compiler_params=_params(1))(chip, wv, m, v, own, recv)


def _sum_devices(gathered, *, name):
    _, r, c = gathered.shape

    def body(g_ref, o_ref):
        total = g_ref[0]
        for j in range(1, N_DEV):
            total = total + g_ref[j]
        o_ref[...] = total

    return pl.pallas_call(body, name=name, out_shape=jax.ShapeDtypeStruct((r, c), F32))(gathered)


def _adam_small(wv, g, m, v, *, name):
    def body(w_ref, g_ref, m_ref, v_ref, d_ref, nm_ref, nv_ref):
        delta, nm, nv = _adam_math(w_ref[...], g_ref[...], m_ref[...], v_ref[...])
        d_ref[...] = delta
        nm_ref[...] = nm
        nv_ref[...] = nv

    return pl.pallas_call(body, name=name, out_shape=[jax.ShapeDtypeStruct(wv.shape, F32)] * 3)(wv, g, m, v)


BIG = ("in", "a", "b", "mix", "mq", "kv", "mo", "fi", "fo")
ROW_SHARDED = ("mix", "mq", "mo")
UNSHARDED = ("a", "b")
FFN_GROUPS = 4
IN_SPLIT = (3, 4)
SMALL_ROWS = 16


class _Plan:
    FUSED = ("in",)
    GATHER_ON = {"sb_fwd": ("a", "b", "mix", "kv", "mq", "mo", "conv", "fi0"), "mm_mix": ("fo",),
                 "mem_sublayer": ("fi1",)}
    SIBLING_ON = {"mm_d_hf": ("fo", "fi"), "mm_d_merged": ("mo", "mq", "kv"), "conv_bwd": ("mix", "a", "b"),
                  "mm_d_w_in1": ("in0",), "mm_d_h0": ("in1",)}
    CHIPS_ON = {"sb_bwd": ("fo", "fi", "mo", "mq", "kv", "mix", "a", "b"), "mm_d_h0": ("in0",),
                "rms_mix_bwd": ("in1",)}

    def __init__(self, shards, core):
        self.shards, self.core = shards, core
        self.w, self.parts, self.chip_sums, self.from_chips = {}, {}, {}, {}

    def gathering(self, k):
        return self.shards[k] if k in self.FUSED else None

    def comm(self, name):
        comms = []
        if name in self.GATHER_ON:
            comms.append(_gather_comm([self.shards[k] for k in self.GATHER_ON[name]]))
        if name in self.SIBLING_ON:
            comms.append(_sibling_comm([self.parts[k] for k in self.SIBLING_ON[name]]))
        if name in self.CHIPS_ON:
            comms.append(_chips_comm([self.chip_sums[k] for k in self.CHIPS_ON[name]]))
        return _join_comms(comms) if comms else None

    def landed(self, name, outs):
        outs = list(outs)
        for k in self.GATHER_ON.get(name, ()):
            self.set_weight(k, outs.pop(0))
        for k in self.SIBLING_ON.get(name, ()):
            self.chip_sums[k] = _sum_with_sibling(self.parts[k], outs.pop(0), self.core, name="sum_with_sibling_" + k)
        for k in self.CHIPS_ON.get(name, ()):
            self.from_chips[k] = outs.pop(0)

    def set_weight(self, k, gathered):
        _, a, b = gathered.shape
        if k in ROW_SHARDED:
            gathered = gathered.reshape(1, N_DEV * a, b)
        elif k in UNSHARDED:
            gathered = jnp.transpose(gathered, (1, 0, 2)).reshape(1, a, N_DEV * b)
        elif k == "fo":
            gathered = gathered.reshape(FFN_GROUPS, N_DEV * a // FFN_GROUPS, b)
        elif k == "conv":
            n_conv = CONV_WIDTH // N_DEV
            gathered = jnp.transpose(gathered[:, :3, :n_conv], (1, 0, 2)).reshape(3, CONV_WIDTH)
        self.w[k] = gathered
        if k == "fi1":
            self.w["fi"] = jnp.concatenate([self.w["fi0"], gathered], axis=2)

    def weight(self, k):
        return self.w[k]

    def grad(self, k, g):
        _, a, b = g.shape
        if k in ROW_SHARDED:
            g = g.reshape(N_DEV, a // N_DEV, b)
        elif k in UNSHARDED:
            g = jnp.transpose(g.reshape(a, N_DEV, b // N_DEV), (1, 0, 2))
        elif k == "fo":
            g = g.reshape(N_DEV, FFN_GROUPS * a // N_DEV, b)
        self.parts[k] = g


def kernel(x, mem, norm_mix, w_in, conv_w, w_branch_a, w_branch_b, w_mix_out, norm_mem_q, norm_mem_kv, w_mem_q, w_mem_kv, w_mem_o, norm_ffn, w_ffn_in, w_ffn_out, norm_final, loss_target, m_norm_mix, m_w_in, m_conv_w, m_w_branch_a, m_w_branch_b, m_w_mix_out, m_norm_mem_q, m_norm_mem_kv, m_w_mem_q, m_w_mem_kv, m_w_mem_o, m_norm_ffn, m_w_ffn_in, m_w_ffn_out, m_norm_final, v_norm_mix, v_w_in, v_conv_w, v_w_branch_a, v_w_branch_b, v_w_mix_out, v_norm_mem_q, v_norm_mem_kv, v_w_mem_q, v_w_mem_kv, v_w_mem_o, v_norm_ffn, v_w_ffn_in, v_w_ffn_out, v_norm_final):
    d = x.shape[-1]
    xi, yi, ci = lax.axis_index("x"), lax.axis_index("y"), lax.axis_index("c")
    core = jnp.reshape(ci, (1,)).astype(jnp.int32)
    chip = jnp.reshape(2 * xi + yi, (1,)).astype(jnp.int32)
    dev = 4 * xi + 2 * yi + ci

    big_w = dict(zip(BIG, (w_in, w_branch_a, w_branch_b, w_mix_out, w_mem_q, w_mem_kv, w_mem_o, w_ffn_in, w_ffn_out)))
    big_m = dict(zip(BIG, (m_w_in, m_w_branch_a, m_w_branch_b, m_w_mix_out, m_w_mem_q, m_w_mem_kv, m_w_mem_o, m_w_ffn_in, m_w_ffn_out)))
    big_v = dict(zip(BIG, (v_w_in, v_w_branch_a, v_w_branch_b, v_w_mix_out, v_w_mem_q, v_w_mem_kv, v_w_mem_o, v_w_ffn_in, v_w_ffn_out)))

    flip = lambda t, k: jnp.transpose(t) if k == "fi" else t
    shards = {k: flip(big_w[k][0], k).astype(BF16) for k in BIG}
    shards["fi0"], shards["fi1"] = shards["fi"][:, :d // 2], shards["fi"][:, d // 2:]
    n_conv = conv_w.shape[-1]
    shards["conv"] = jnp.zeros((8, LANES), F32).at[:3, :n_conv].set(conv_w[0])
    plan = _Plan(shards, core)

    gains = (norm_mix, norm_mem_q, norm_mem_kv, norm_ffn, norm_final.reshape(1, d))
    dx0, small = _local_step(x[0], mem[0], loss_target[0], gains, plan)

    grads, deltas, new_m, new_v = {}, {}, {}, {}
    for k in BIG:
        lead = big_w[k].shape
        wv, mv, vv = flip(big_w[k][0], k), flip(big_m[k][0], k), flip(big_v[k][0], k)
        if k == "in":
            half = wv.shape[0] * IN_SPLIT[0] // IN_SPLIT[1]
            lo = _adam_sharded(wv[:half], mv[:half], vv[:half], plan.chip_sums["in0"], plan.from_chips["in0"], chip,
                               name="adam_in0")
            hi = _adam_sharded(wv[half:], mv[half:], vv[half:], plan.chip_sums["in1"], plan.from_chips["in1"], chip,
                               name="adam_in1")
            outs = [jnp.concatenate(pair, axis=0) for pair in zip(lo, hi)]
        else:
            outs = _adam_sharded(wv, mv, vv, plan.chip_sums[k], plan.from_chips[k], chip, name="adam_" + k)
        grads[k], deltas[k], new_m[k], new_v[k] = (flip(t, k).reshape(lead) for t in outs)

    dg_mix, dg_memq, dg_memkv, dg_ffn, dg_fin, dconv_w, loss = small
    conv_rows = jnp.zeros((3, d), F32).at[:, :CONV_WIDTH].set(dconv_w[:3])
    block = jnp.concatenate([dg_mix[:1], dg_memq[:1], dg_memkv[:1], dg_ffn[:1], dg_fin[:1], conv_rows,
                             jnp.broadcast_to(loss[:1, :1], (1, d)), jnp.zeros((SMALL_ROWS - 9, d), F32)], axis=0)
    total = _sum_devices(_exchange(_gather_comm([block]), name="gather_small")[0], name="sum_small")
    g_conv = lax.dynamic_slice(total[5:8, :CONV_WIDTH], (0, dev * n_conv), (3, n_conv))
    small_w = [norm_mix, norm_mem_q, norm_mem_kv, norm_ffn, norm_final.reshape(1, d), conv_w[0]]
    small_m = [m_norm_mix, m_norm_mem_q, m_norm_mem_kv, m_norm_ffn, m_norm_final.reshape(1, d), m_conv_w[0]]
    small_v = [v_norm_mix, v_norm_mem_q, v_norm_mem_kv, v_norm_ffn, v_norm_final.reshape(1, d), v_conv_w[0]]
    small_g = [total[0:1], total[1:2], total[2:3], total[3:4], total[4:5], g_conv]
    small_names = ["norm_mix", "norm_mem_q", "norm_mem_kv", "norm_ffn", "norm_final", "conv_w"]
    sg, sd, sm, sv = {}, {}, {}, {}
    for nme, wv, g, m, v in zip(small_names, small_w, small_g, small_m, small_v):
        dl, nm, nv = _adam_small(wv, g, m, v, name="adam_" + nme)
        shape = norm_final.shape if nme == "norm_final" else (conv_w.shape if nme == "conv_w" else wv.shape)
        sg[nme], sd[nme], sm[nme], sv[nme] = (t.reshape(shape) for t in (g, dl, nm, nv))

    def ordered(big, sml):
        return (sml["norm_mix"], big["in"], sml["conv_w"], big["a"], big["b"], big["mix"], sml["norm_mem_q"],
                sml["norm_mem_kv"], big["mq"], big["kv"], big["mo"], sml["norm_ffn"], big["fi"], big["fo"],
                sml["norm_final"])

    loss_out = total[8, 0]
    grad_x = dx0.reshape(x.shape)
    return (loss_out, grad_x, *ordered(grads, sg), *ordered(deltas, sd), *ordered(new_m, sm), *ordered(new_v, sv))
```

```python
import functools
import math

import jax
import jax.numpy as jnp
from jax import lax
from jax.experimental import pallas as pl
from jax.experimental.pallas import tpu as pltpu

F32 = jnp.float32
BF16 = jnp.bfloat16
MESH = pl.DeviceIdType.MESH

N_DEV = 8
N_CHIP = 4
NORM_EPS = 1e-6
SB_HEADS = 8
SB_HEAD_DIM = 64
SB_WIDTH = SB_HEADS * SB_HEAD_DIM
CONV_WIDTH = 512
MEM_HEADS = 4
ADAM_LR = 0.001
ADAM_B1 = 0.9
ADAM_B2 = 0.999
ADAM_EPS = 1e-08
ADAM_WD = 0.01
ADAM_STEP = 10

LANES = 128
VMEM_LIMIT_BYTES = 52 * 1024 * 1024
SB_TILE = 256
SB_DEAD = 159.0
SB_CLAMP = 126.0
LOG2_E = 1.4426950408889634

ANY = pl.BlockSpec(memory_space=pl.ANY)


def _params(n_grid):
    return pltpu.CompilerParams(dimension_semantics=("arbitrary",) * n_grid, vmem_limit_bytes=VMEM_LIMIT_BYTES)


def _bdot(a, b, dims):
    return lax.dot_general(a.astype(BF16), b.astype(BF16), (dims, ((), ())), preferred_element_type=F32)


NN = ((1,), (0,))
NT = ((1,), (1,))
TN = ((0,), (0,))


class _Comm:
    def __init__(self, ins, outs, n_sems, start, finish):
        self.ins, self.outs, self.n_sems, self.start, self.finish = ins, outs, n_sems, start, finish

    def sem_shapes(self):
        return [pltpu.SemaphoreType.DMA((k,)) for k in self.n_sems]


def _place():
    return lax.axis_index("x"), lax.axis_index("y"), lax.axis_index("c")


def _gather_comm(shards):
    n = len(shards)

    def copies(ins, outs, sems):
        send_sems, recv_sems, _ = sems
        x, y, c = _place()
        chips = [(1 - x, y), (x, 1 - y), (1 - x, 1 - y)]

        def copy(a, k, block, to, from_shard=False):
            dst = outs[a].at[4 * block[0] + 2 * block[1] + block[2]]
            return pltpu.make_async_remote_copy(
                src_ref=ins[a] if from_shard else dst, dst_ref=dst, send_sem=send_sems.at[a * 7 + k],
                recv_sem=recv_sems.at[a * 7 + k], device_id=to, device_id_type=MESH)

        me, sibling = (x, y, c), (x, y, 1 - c)
        own = [[copy(a, 0, me, sibling, True)] + [copy(a, 1 + j, me, (*chip, c), True) for j, chip in enumerate(chips)]
               for a in range(n)]
        landed = [[copy(a, 1 + j, (*chip, c), me) for j, chip in enumerate(chips)] for a in range(n)]
        passed = [[copy(a, 4 + j, (*chip, c), sibling) for j, chip in enumerate(chips)] for a in range(n)]
        from_sibling = [[copy(a, 0, sibling, me)] + [copy(a, 4 + j, (*chip, 1 - c), me) for j, chip in enumerate(chips)]
                        for a in range(n)]
        local = [pltpu.make_async_copy(ins[a], outs[a].at[4 * x + 2 * y + c], sems[2].at[a]) for a in range(n)]
        return own, landed, passed, from_sibling, local

    def start(ins, outs, sems):
        own, _, _, _, local = copies(ins, outs, sems)
        for a in range(n):
            local[a].start()
            for cp in own[a]:
                cp.start()

    def finish(ins, outs, sems):
        own, landed, passed, from_sibling, local = copies(ins, outs, sems)
        for a in range(n):
            for arrived, onward in zip(landed[a], passed[a]):
                arrived.wait_recv()
                onward.start()
        for a in range(n):
            for cp in from_sibling[a]:
                cp.wait_recv()
        for a in range(n):
            for cp in own[a] + passed[a]:
                cp.wait_send()
            local[a].wait()

    outs = [jax.ShapeDtypeStruct((N_DEV,) + s.shape, s.dtype) for s in shards]
    return _Comm(list(shards), outs, (7 * n, 7 * n, n), start, finish)


def _sibling_comm(parts):
    n = len(parts)

    def copies(ins, outs, sems):
        x, y, c = _place()
        return [pltpu.make_async_remote_copy(
            src_ref=ins[a].at[2 * q + 1 - c], dst_ref=outs[a].at[q], send_sem=sems[0].at[a * N_CHIP + q],
            recv_sem=sems[1].at[a * N_CHIP + q], device_id=(x, y, 1 - c), device_id_type=MESH)
            for a in range(n) for q in range(N_CHIP)]

    def start(ins, outs, sems):
        for cp in copies(ins, outs, sems):
            cp.start()

    def finish(ins, outs, sems):
        cps = copies(ins, outs, sems)
        for cp in cps:
            cp.wait_recv()
        for cp in cps:
            cp.wait_send()

    outs = [jax.ShapeDtypeStruct((N_CHIP,) + p.shape[1:], p.dtype) for p in parts]
    return _Comm(list(parts), outs, (N_CHIP * n, N_CHIP * n), start, finish)


def _chips_comm(parts):
    n = len(parts)

    def copies(ins, outs, sems):
        x, y, c = _place()
        chips = [(1 - x, y), (x, 1 - y), (1 - x, 1 - y)]
        return [pltpu.make_async_remote_copy(
            src_ref=ins[a].at[2 * px + py], dst_ref=outs[a].at[j], send_sem=sems[0].at[a * 3 + j],
            recv_sem=sems[1].at[a * 3 + j], device_id=(px, py, c), device_id_type=MESH)
            for a in range(n) for j, (px, py) in enumerate(chips)]

    def start(ins, outs, sems):
        for cp in copies(ins, outs, sems):
            cp.start()

    def finish(ins, outs, sems):
        cps = copies(ins, outs, sems)
        for cp in cps:
            cp.wait_recv()
        for cp in cps:
            cp.wait_send()

    outs = [jax.ShapeDtypeStruct((3,) + p.shape[1:], p.dtype) for p in parts]
    return _Comm(list(parts), outs, (3 * n, 3 * n), start, finish)


def _join_comms(comms):
    if len(comms) == 1:
        return comms[0]

    def split(refs, counts):
        out, at = [], 0
        for n in counts:
            out.append(refs[at:at + n])
            at += n
        return out

    def each(method):
        def run(ins, outs, sems):
            parts = zip(comms, split(ins, [len(c.ins) for c in comms]), split(outs, [len(c.outs) for c in comms]),
                        split(sems, [len(c.n_sems) for c in comms]))
            for c, c_ins, c_outs, c_sems in parts:
                getattr(c, method)(c_ins, c_outs, c_sems)
        return run

    return _Comm([a for c in comms for a in c.ins], [o for c in comms for o in c.outs],
                 tuple(k for c in comms for k in c.n_sems), each("start"), each("finish"))


def _exchange(comm, *, name):
    n_ci, n_co = len(comm.ins), len(comm.outs)

    def kern(*refs):
        c_ins, c_outs, sems = refs[:n_ci], refs[n_ci:n_ci + n_co], refs[n_ci + n_co:]
        comm.start(c_ins, c_outs, sems)
        comm.finish(c_ins, c_outs, sems)

    return pl.pallas_call(kern, name=name, in_specs=[ANY] * n_ci, out_specs=[ANY] * n_co, out_shape=comm.outs,
                          scratch_shapes=comm.sem_shapes())(*comm.ins)


def _call(body, *, name, grid, in_specs, out_specs, out_shape, scratch, args, plan=None):
    comm = plan.comm(name) if plan is not None else None
    if comm is None:
        return list(pl.pallas_call(functools.partial(body), name=name, grid=grid, in_specs=in_specs,
                                   out_specs=out_specs, out_shape=out_shape, scratch_shapes=scratch,
                                   compiler_params=_params(len(grid)))(*args))
    n_in, n_out, n_scr, n_ci, n_co = len(in_specs), len(out_specs), len(scratch), len(comm.ins), len(comm.outs)

    def kern(*refs):
        ins, c_ins, refs = refs[:n_in], refs[n_in:n_in + n_ci], refs[n_in + n_ci:]
        outs, c_outs, refs = refs[:n_out], refs[n_out:n_out + n_co], refs[n_out + n_co:]
        scr, sems = refs[:n_scr], refs[n_scr:]
        ids = [pl.program_id(ax) for ax in range(len(grid))]
        first = functools.reduce(jnp.logical_and, [i == 0 for i in ids])
        last = functools.reduce(jnp.logical_and, [i == g - 1 for i, g in zip(ids, grid)])

        @pl.when(first)
        def _():
            comm.start(c_ins, c_outs, sems)
        body(*ins, *outs, *scr)

        @pl.when(last)
        def _():
            comm.finish(c_ins, c_outs, sems)

    res = pl.pallas_call(kern, name=name, grid=grid, in_specs=list(in_specs) + [ANY] * n_ci,
                         out_specs=list(out_specs) + [ANY] * n_co, out_shape=list(out_shape) + comm.outs,
                         scratch_shapes=list(scratch) + comm.sem_shapes(),
                         compiler_params=_params(len(grid)))(*args, *comm.ins)
    plan.landed(name, list(res[n_out:]))
    return list(res[:n_out])


def _mm_body(dims, has_add, *refs):
    if has_add:
        a_ref, b_ref, add_ref, o_ref = refs
        total = _bdot(a_ref[...], b_ref[...], dims) + add_ref[...]
    else:
        a_ref, b_ref, o_ref = refs
        total = _bdot(a_ref[...], b_ref[...], dims)
    o_ref[...] = total.astype(o_ref.dtype)


def _mm_nt_body(j, n, dy_ref, w_ref, o_ref):
    total = _bdot(dy_ref[:, 0:n], w_ref[0], NT)
    for jj in range(1, j):
        total = total + _bdot(dy_ref[:, jj * n:(jj + 1) * n], w_ref[jj], NT)
    o_ref[...] = total.astype(o_ref.dtype)


def _mm_nn(a, w3, *, name, out_dtype=BF16, add=None, tm=1024, tn=None, out3=False, w_t=False, plan=None):
    m, kk = a.shape
    j, n = w3.shape[0], w3.shape[1 if w_t else 2]
    tm, tn = min(tm, m), n if tn is None else tn
    n_t = n // tn
    in_specs = [pl.BlockSpec((tm, kk), lambda i, jj: (i, 0)),
                pl.BlockSpec((None, tn, kk), lambda i, jj: (jj // n_t, jj % n_t, 0)) if w_t else
                pl.BlockSpec((None, kk, tn), lambda i, jj: (jj // n_t, 0, jj % n_t))]
    args = [a, w3]
    if add is not None:
        in_specs.append(pl.BlockSpec((tm, tn), lambda i, jj: (i, jj)))
        args.append(add)
    if out3:
        out_spec = pl.BlockSpec((None, tm, tn), lambda i, jj: (jj // n_t, i, jj % n_t))
        out_shape = jax.ShapeDtypeStruct((j, m, n), out_dtype)
    else:
        out_spec = pl.BlockSpec((tm, tn), lambda i, jj: (i, jj))
        out_shape = jax.ShapeDtypeStruct((m, j * n), out_dtype)
    return _call(
        functools.partial(_mm_body, NT if w_t else NN, add is not None), name=name, grid=(m // tm, j * n_t),
        in_specs=in_specs, out_specs=[out_spec], out_shape=[out_shape], scratch=[], args=args, plan=plan)[0]


def _mm_gathering(a, shard, *, name, out3=False, w_t=False, tm=1024):
    m, kk = a.shape
    n = shard.shape[0 if w_t else 1]
    tm = min(tm, m)
    n_i = m // tm

    def body(a_ref, shard_ref, o_ref, w_all, w_vmem, send_sems, recv_sems, copy_sems):
        jj, i = pl.program_id(0), pl.program_id(1)
        x, y, c = _place()
        me, sibling = (x, y, c), (x, y, 1 - c)
        chips = [(jnp.bitwise_xor(x, c), jnp.bitwise_xor(y, 1 - c)), (jnp.bitwise_xor(x, 1 - c), jnp.bitwise_xor(y, c)),
                 (1 - x, 1 - y)]
        sibling_chips = [chips[1], chips[0], chips[2]]

        def rows(block):
            return w_all.at[4 * block[0] + 2 * block[1] + block[2]]

        def remote(k, block, to, from_shard=False):
            return pltpu.make_async_remote_copy(
                src_ref=shard_ref if from_shard else rows(block), dst_ref=rows(block), send_sem=send_sems.at[k],
                recv_sem=recv_sems.at[k], device_id=to, device_id_type=MESH)

        def load(src):
            cp = pltpu.make_async_copy(src, w_vmem, copy_sems.at[1])
            cp.start()
            cp.wait()

        own = [remote(0, me, sibling, True)] + [remote(1 + j, me, (*chip, c), True) for j, chip in enumerate(chips)]
        passed = [remote(4 + j, (*chip, c), sibling) for j, chip in enumerate(chips)]
        local = pltpu.make_async_copy(shard_ref, rows(me), copy_sems.at[0])

        @pl.when(jnp.logical_and(i == 0, jj == 0))
        def _():
            local.start()
            own[0].start()
            own[1].start()
            load(shard_ref)

        @pl.when(jnp.logical_and(i == 0, jj == 1))
        def _():
            remote(0, sibling, me).wait_recv()
            load(rows(sibling))

        for j, chip in enumerate(chips):
            @pl.when(jnp.logical_and(i == 0, jj == 2 + 2 * j))
            def _():
                if j < 2:
                    own[1 + j].wait_send()
                    own[2 + j].start()
                remote(1 + j, (*chip, c), me).wait_recv()
                passed[j].start()
                load(rows((*chip, c)))

            @pl.when(jnp.logical_and(i == 0, jj == 3 + 2 * j))
            def _():
                block = (*sibling_chips[j], 1 - c)
                remote(4 + j, block, me).wait_recv()
                load(rows(block))

        o_ref[...] = _bdot(a_ref[...], w_vmem[...], NT if w_t else NN).astype(o_ref.dtype)

        @pl.when(jnp.logical_and(i == n_i - 1, jj == N_DEV - 1))
        def _():
            for cp in [own[0], own[3]] + passed:
                cp.wait_send()
            local.wait()

    def swept(jj):
        x, y, c = _place()
        first, second = 2 + 2 * c, 4 - 2 * c
        flips = (0b000, 0b001, first, second + 1, second, first + 1, 0b110, 0b111)
        return jnp.bitwise_xor(4 * x + 2 * y + c, sum(jnp.where(jj == k, f, 0) for k, f in enumerate(flips)))

    if out3:
        out_spec = pl.BlockSpec((None, tm, n), lambda jj, i: (swept(jj), i, 0))
        out_shape = jax.ShapeDtypeStruct((N_DEV, m, n), BF16)
    else:
        out_spec = pl.BlockSpec((tm, n), lambda jj, i: (i, swept(jj)))
        out_shape = jax.ShapeDtypeStruct((m, N_DEV * n), BF16)
    return pl.pallas_call(
        body, name=name, grid=(N_DEV, n_i),
        in_specs=[pl.BlockSpec((tm, kk), lambda jj, i: (i, 0)), ANY], out_specs=[out_spec, ANY],
        scratch_shapes=[pltpu.VMEM(shard.shape, shard.dtype), pltpu.SemaphoreType.DMA((7,)),
                        pltpu.SemaphoreType.DMA((7,)), pltpu.SemaphoreType.DMA((2,))],
        out_shape=[out_shape, jax.ShapeDtypeStruct((N_DEV,) + shard.shape, shard.dtype)],
        compiler_params=_params(2))(a, shard)


def _sigmoid(v):
    return 0.5 * jnp.tanh(0.5 * v) + 0.5


def _ffn_out_loss(gu3, w3, add, g, target, *, name, tm=256):
    j2, m, n = gu3.shape
    j = j2 // 2
    nn = w3.shape[2]
    tm = min(tm, m)

    def body(gu_ref, w_ref, add_ref, g_ref, t_ref, dx_ref, dxb_ref, dg_ref, loss_ref, act_ref):
        i = pl.program_id(0)
        xv = add_ref[...]
        for jj in range(j):
            gate = gu_ref[0, jj].astype(F32)
            act = (gate * _sigmoid(gate) * gu_ref[1, jj].astype(F32)).astype(BF16)
            act_ref[jj] = act
            xv = xv + _bdot(act, w_ref[jj], NN)
        gv = g_ref[...]
        r = lax.rsqrt(jnp.mean(xv * xv, axis=-1, keepdims=True) + NORM_EPS)
        xhat = xv * r
        err = xhat * gv - t_ref[...]
        _acc_rows(i, loss_ref, 0.5 * jnp.sum(jnp.mean(err * err, axis=-1, keepdims=True), axis=0, keepdims=True))
        dy = err * (1.0 / nn)
        dxhat = dy * gv
        dx = r * (dxhat - xhat * jnp.mean(dxhat * xhat, axis=-1, keepdims=True))
        dx_ref[...] = dx
        dxb_ref[...] = dx.astype(BF16)
        _acc_rows(i, dg_ref, jnp.sum(dy * xhat, axis=0, keepdims=True))

    row = pl.BlockSpec((tm, nn), lambda i: (i, 0))
    return _call(body, name=name, grid=(m // tm,),
                 in_specs=[pl.BlockSpec((2, j, tm, n), lambda i: (0, 0, i, 0)), pl.BlockSpec(w3.shape, lambda i: (0, 0, 0)),
                           row, pl.BlockSpec(g.shape, lambda i: (0, 0)), row],
                 out_specs=[row, row, pl.BlockSpec((8, nn), lambda i: (0, 0)), pl.BlockSpec((8, LANES), lambda i: (0, 0)),
                            pl.BlockSpec((j, tm, n), lambda i: (0, i, 0))],
                 out_shape=[jax.ShapeDtypeStruct((m, nn), F32), jax.ShapeDtypeStruct((m, nn), BF16),
                            jax.ShapeDtypeStruct((8, nn), F32), jax.ShapeDtypeStruct((8, LANES), F32),
                            jax.ShapeDtypeStruct((j, m, n), BF16)],
                 scratch=[], args=[gu3.reshape(2, j, m, n), w3, add, g, target])


def _ffn_out_bwd(dy, w3, gu3, *, name, tm=1024):
    m, nn = dy.shape
    j, n, _ = w3.shape
    tm = min(tm, m)

    def body(dy_ref, w_ref, gu_ref, dgu_ref):
        da = _bdot(dy_ref[...], w_ref[...], NT)
        gate = gu_ref[0].astype(F32)
        up = gu_ref[1].astype(F32)
        sg = _sigmoid(gate)
        silu = gate * sg
        dgu_ref[0] = (da * up * (sg + silu * (1.0 - sg))).astype(BF16)
        dgu_ref[1] = (da * silu).astype(BF16)

    out = _call(body, name=name, grid=(m // tm, j),
                in_specs=[pl.BlockSpec((tm, nn), lambda i, jj: (i, 0)),
                          pl.BlockSpec((None, n, nn), lambda i, jj: (jj, 0, 0)),
                          pl.BlockSpec((2, None, tm, n), lambda i, jj: (0, jj, i, 0))],
                out_specs=[pl.BlockSpec((2, None, tm, n), lambda i, jj: (0, jj, i, 0))],
                out_shape=[jax.ShapeDtypeStruct((2, j, m, n), BF16)], scratch=[],
                args=[dy, w3, gu3.reshape(2, j, m, n)])[0]
    return out.reshape(2 * j, m, n)


def _rms_fwd_tail(xv, g_ref, h_ref):
    r = lax.rsqrt(jnp.mean(xv * xv, axis=-1, keepdims=True) + NORM_EPS)
    h_ref[...] = (xv * r * g_ref[...]).astype(BF16)


def _rms_bwd_tail(i, dh, x_ref, g_ref, dres_ref, dx_ref, dxb_ref, dg_ref):
    xv = x_ref[...]
    r = lax.rsqrt(jnp.mean(xv * xv, axis=-1, keepdims=True) + NORM_EPS)
    xhat = xv * r
    dxhat = dh * g_ref[...]
    dx = r * (dxhat - xhat * jnp.mean(dxhat * xhat, axis=-1, keepdims=True))
    if dres_ref is not None:
        dx = dx + dres_ref[...]
    dx_ref[...] = dx
    dxb_ref[...] = dx.astype(BF16)
    _acc_rows(i, dg_ref, jnp.sum(dh * xhat, axis=0, keepdims=True))


def _mm_nt_rms(dy, w3, x, g, dres, *, name, dy3=False, w_nn=False, tm=512, plan=None):
    j = w3.shape[0]
    m, kk = x.shape
    n = dy.shape[2] if dy3 else dy.shape[1] // j
    tm = min(tm, m)

    def body(dy_ref, w_ref, x_ref, g_ref, *rest):
        dres_ref = rest[0] if dres is not None else None
        dx_ref, dxb_ref, dg_ref = rest[-3:]
        dh = None
        for jj in range(j):
            piece = dy_ref[jj] if dy3 else dy_ref[:, jj * n:(jj + 1) * n]
            part = _bdot(piece, w_ref[jj], NN if w_nn else NT)
            dh = part if dh is None else dh + part
        _rms_bwd_tail(pl.program_id(0), dh, x_ref, g_ref, dres_ref, dx_ref, dxb_ref, dg_ref)

    row = pl.BlockSpec((tm, kk), lambda i: (i, 0))
    in_specs = [pl.BlockSpec((j, tm, n), lambda i: (0, i, 0)) if dy3 else pl.BlockSpec((tm, j * n), lambda i: (i, 0)),
                pl.BlockSpec(w3.shape, lambda i: (0, 0, 0)), row, pl.BlockSpec(g.shape, lambda i: (0, 0))]
    args = [dy, w3, x, g]
    if dres is not None:
        in_specs.append(row)
        args.append(dres)
    return _call(body, name=name, grid=(m // tm,), in_specs=in_specs,
                 out_specs=[row, row, pl.BlockSpec((8, kk), lambda i: (0, 0))],
                 out_shape=[jax.ShapeDtypeStruct((m, kk), F32), jax.ShapeDtypeStruct((m, kk), BF16),
                            jax.ShapeDtypeStruct((8, kk), F32)], scratch=[], args=args, plan=plan)


def _mix_out(br_a, br_b, proj, w, x, g, *, name, tm=512, plan=None):
    s, d = br_a.shape
    tm = min(tm, s)

    def body(a_ref, b_ref, ga_ref, gb_ref, w_ref, x_ref, g_ref, x1_ref, h_ref, merged_ref):
        merged = (_sigmoid(ga_ref[...].astype(F32)) * a_ref[...].astype(F32)
                  + _sigmoid(gb_ref[...].astype(F32)) * b_ref[...].astype(F32)).astype(BF16)
        merged_ref[...] = merged
        xv = _bdot(merged, w_ref[...], NN) + x_ref[...]
        x1_ref[...] = xv
        _rms_fwd_tail(xv, g_ref, h_ref)

    row = pl.BlockSpec((tm, d), lambda i: (i, 0))
    return _call(body, name=name, grid=(s // tm,),
                 in_specs=[row, row, pl.BlockSpec((tm, d), lambda i: (i, 3)), pl.BlockSpec((tm, d), lambda i: (i, 4)),
                           pl.BlockSpec(w.shape, lambda i: (0, 0)), row, pl.BlockSpec(g.shape, lambda i: (0, 0))],
                 out_specs=[row, row, row],
                 out_shape=[jax.ShapeDtypeStruct((s, d), F32), jax.ShapeDtypeStruct((s, d), BF16),
                            jax.ShapeDtypeStruct((s, d), BF16)],
                 scratch=[], args=[br_a, br_b, proj, proj, w, x, g], plan=plan)


def _mm_tn_a3(a3, dy, *, name):
    j, t, n = a3.shape
    nn = dy.shape[1]
    return _call(functools.partial(_mm_body, TN, False), name=name, grid=(j,),
                 in_specs=[pl.BlockSpec((None, t, n), lambda jj: (jj, 0, 0)), pl.BlockSpec((t, nn), lambda jj: (0, 0))],
                 out_specs=[pl.BlockSpec((None, n, nn), lambda jj: (jj, 0, 0))],
                 out_shape=[jax.ShapeDtypeStruct((j, n, nn), BF16)], scratch=[], args=[a3, dy])[0]


def _mm_nt(dy, w3, *, name, out_dtype=BF16, tm=512, tn=1024, plan=None):
    m = dy.shape[0]
    j, kk, n = w3.shape
    tm, tn = min(tm, m), min(tn, kk)
    return _call(
        functools.partial(_mm_nt_body, j, n), name=name,
        grid=(m // tm, kk // tn),
        in_specs=[pl.BlockSpec((tm, j * n), lambda i, q: (i, 0)),
                  pl.BlockSpec((j, tn, n), lambda i, q: (0, q, 0))],
        out_specs=[pl.BlockSpec((tm, tn), lambda i, q: (i, q))],
        out_shape=[jax.ShapeDtypeStruct((m, kk), out_dtype)], scratch=[], args=[dy, w3], plan=plan)[0]


def _mm_tn(a, dy, n, *, name, out_dtype=BF16, tm=512, tn=None, k_tiles=None, plan=None):
    t, kk = a.shape
    j = dy.shape[1] // n
    tm, tn = min(tm, kk), n if tn is None else tn
    n_t = n // tn
    first, count = (0, kk // tm) if k_tiles is None else k_tiles
    return _call(
        functools.partial(_mm_body, TN, False), name=name,
        grid=(count, j * n_t),
        in_specs=[pl.BlockSpec((t, tm), lambda i, jj: (0, first + i)),
                  pl.BlockSpec((t, tn), lambda i, jj: (0, jj))],
        out_specs=[pl.BlockSpec((None, tm, tn), lambda i, jj: (jj // n_t, i, jj % n_t))],
        out_shape=[jax.ShapeDtypeStruct((j, count * tm, n), out_dtype)], scratch=[], args=[a, dy], plan=plan)[0]


def _rows(body, ins, outs, *, n_rows, tm, name, plan=None):
    tm = min(tm, n_rows)
    n_steps = n_rows // tm
    in_specs, args = [], []
    for arr, kind, width, block in ins:
        if kind == "row":
            in_specs.append(pl.BlockSpec((tm, width), functools.partial(lambda i, b: (i, b), b=block)))
        elif kind == "prev":
            in_specs.append(pl.BlockSpec((tm, width), functools.partial(lambda i, b: (jnp.maximum(i - 1, 0), b), b=block)))
        elif kind == "next":
            in_specs.append(pl.BlockSpec((tm, width), functools.partial(lambda i, b: (jnp.minimum(i + 1, n_steps - 1), b), b=block)))
        else:
            in_specs.append(pl.BlockSpec(arr.shape, functools.partial(lambda i, nd: (0,) * nd, nd=arr.ndim)))
        args.append(arr)
    out_specs, out_shape = [], []
    for shape, dtype, kind in outs:
        if kind == "row":
            out_specs.append(pl.BlockSpec((tm, shape[1]), lambda i: (i, 0)))
        else:
            out_specs.append(pl.BlockSpec(shape, functools.partial(lambda i, nd: (0,) * nd, nd=len(shape))))
        out_shape.append(jax.ShapeDtypeStruct(shape, dtype))

    def kern(*refs):
        body(pl.program_id(0), n_steps, *refs)

    return _call(kern, name=name, grid=(n_steps,), in_specs=in_specs, out_specs=out_specs, out_shape=out_shape,
                 scratch=[], args=args, plan=plan)


def _acc_rows(i, ref, value):
    @pl.when(i == 0)
    def _():
        ref[...] = jnp.zeros_like(ref)
    ref[...] += jnp.broadcast_to(value, ref.shape)


def _rms_fwd(x, g, *, name, tm=512):
    s, d = x.shape

    def body(i, n, x_ref, g_ref, h_ref):
        _rms_fwd_tail(x_ref[...], g_ref, h_ref)

    return _rows(body, [(x, "row", d, 0), (g, "full", 0, 0)], [((s, d), BF16, "row")], n_rows=s, tm=tm, name=name)[0]


def _rms_bwd(x, g, dh, dres, *, name, tm=512, plan=None):
    s, d = x.shape

    def body(i, n, x_ref, g_ref, dh_ref, dres_ref, dx_ref, dxb_ref, dg_ref):
        _rms_bwd_tail(i, dh_ref[...].astype(F32), x_ref, g_ref, dres_ref, dx_ref, dxb_ref, dg_ref)

    return _rows(body, [(x, "row", d, 0), (g, "full", 0, 0), (dh, "row", d, 0), (dres, "row", d, 0)],
                 [((s, d), F32, "row"), ((s, d), BF16, "row"), ((8, d), F32, "acc")],
                 n_rows=s, tm=tm, name=name, plan=plan)


def _mix_out_bwd(dx1b, w, br_a, br_b, proj, *, name, tm=512, plan=None):
    s, d = br_a.shape
    tm = min(tm, s)

    def body(dy_ref, w_ref, a_ref, b_ref, ga_ref, gb_ref, da_ref, db_ref, dg_ref):
        dm = _bdot(dy_ref[...], w_ref[...], NT)
        sa = _sigmoid(ga_ref[...].astype(F32))
        sb = _sigmoid(gb_ref[...].astype(F32))
        da_ref[...] = (dm * sa).astype(BF16)
        db_ref[...] = (dm * sb).astype(BF16)
        dg_ref[:, :d] = (dm * a_ref[...].astype(F32) * sa * (1.0 - sa)).astype(BF16)
        dg_ref[:, d:] = (dm * b_ref[...].astype(F32) * sb * (1.0 - sb)).astype(BF16)

    row = pl.BlockSpec((tm, d), lambda i: (i, 0))
    return _call(body, name=name, grid=(s // tm,),
                 in_specs=[row, pl.BlockSpec(w.shape, lambda i: (0, 0)), row, row,
                           pl.BlockSpec((tm, d), lambda i: (i, 3)), pl.BlockSpec((tm, d), lambda i: (i, 4))],
                 out_specs=[row, row, pl.BlockSpec((tm, 2 * d), lambda i: (i, 0))],
                 out_shape=[jax.ShapeDtypeStruct((s, d), BF16), jax.ShapeDtypeStruct((s, d), BF16),
                            jax.ShapeDtypeStruct((s, 2 * d), BF16)],
                 scratch=[], args=[dx1b, w, br_a, br_b, proj, proj], plan=plan)


def _shift_down(cur, prev, k, first):
    row = lax.broadcasted_iota(jnp.int32, cur.shape, 0)
    out = jnp.where(row >= k, pltpu.roll(cur, k, 0), pltpu.roll(prev, k, 0))
    return jnp.where(jnp.logical_and(first, row < k), 0.0, out)


def _shift_up(cur, nxt, k, last):
    tm = cur.shape[0]
    row = lax.broadcasted_iota(jnp.int32, cur.shape, 0)
    out = jnp.where(row < tm - k, pltpu.roll(cur, tm - k, 0), pltpu.roll(nxt, tm - k, 0))
    return jnp.where(jnp.logical_and(last, row >= tm - k), 0.0, out)


def _conv_fwd(proj, conv_w, *, name, tm=512):
    s = proj.shape[0]
    c = CONV_WIDTH

    def body(i, n, u_ref, gb_ref, gc_ref, up_ref, gcp_ref, w_ref, y_ref):
        cu = gc_ref[...].astype(F32) * u_ref[...].astype(F32)
        cup = gcp_ref[...].astype(F32) * up_ref[...].astype(F32)
        first = i == 0
        y = (w_ref[0:1, :] * _shift_down(cu, cup, 2, first) + w_ref[1:2, :] * _shift_down(cu, cup, 1, first)
             + w_ref[2:3, :] * cu)
        y_ref[...] = (gb_ref[...].astype(F32) * y).astype(BF16)

    return _rows(body, [(proj, "row", c, 3), (proj, "row", c, 4), (proj, "row", c, 5),
                        (proj, "prev", c, 3), (proj, "prev", c, 5), (conv_w, "full", 0, 0)],
                 [((s, c), BF16, "row")], n_rows=s, tm=tm, name=name)[0]


def _conv_bwd(dy_b, proj, conv_w, *, name, tm=512, plan=None):
    s = proj.shape[0]
    c = CONV_WIDTH

    def body(i, n, dy_ref, u_ref, gb_ref, gc_ref, up_ref, gcp_ref, dyn_ref, gbn_ref, w_ref, d_ref, dw_ref):
        first, last = i == 0, i == n - 1
        u = u_ref[...].astype(F32)
        gb = gb_ref[...].astype(F32)
        gc = gc_ref[...].astype(F32)
        cu = gc * u
        cup = gcp_ref[...].astype(F32) * up_ref[...].astype(F32)
        cu1 = _shift_down(cu, cup, 1, first)
        cu2 = _shift_down(cu, cup, 2, first)
        conv = w_ref[0:1, :] * cu2 + w_ref[1:2, :] * cu1 + w_ref[2:3, :] * cu
        dy = dy_ref[...].astype(F32)
        dyc = dy * gb
        dycn = dyn_ref[...].astype(F32) * gbn_ref[...].astype(F32)
        dcu = (w_ref[2:3, :] * dyc + w_ref[1:2, :] * _shift_up(dyc, dycn, 1, last)
               + w_ref[0:1, :] * _shift_up(dyc, dycn, 2, last))
        d_ref[:, 0:c] = (dcu * gc).astype(BF16)
        d_ref[:, c:2 * c] = (dy * conv).astype(BF16)
        d_ref[:, 2 * c:3 * c] = (dcu * u).astype(BF16)
        row = lax.broadcasted_iota(jnp.int32, (8, c), 0)
        dw = (jnp.where(row == 0, jnp.sum(dyc * cu2, axis=0, keepdims=True), 0.0)
              + jnp.where(row == 1, jnp.sum(dyc * cu1, axis=0, keepdims=True), 0.0)
              + jnp.where(row == 2, jnp.sum(dyc * cu, axis=0, keepdims=True), 0.0))

        @pl.when(first)
        def _():
            dw_ref[...] = jnp.zeros_like(dw_ref)
        dw_ref[...] += dw

    return _rows(body, [(dy_b, "row", c, 0), (proj, "row", c, 3), (proj, "row", c, 4), (proj, "row", c, 5),
                        (proj, "prev", c, 3), (proj, "prev", c, 5), (dy_b, "next", c, 0), (proj, "next", c, 4),
                        (conv_w, "full", 0, 0)],
                 [((s, 3 * c), BF16, "row"), ((8, c), F32, "acc")], n_rows=s, tm=tm, name=name, plan=plan)


def _mem_probs(q, k, scale):
    sc = _bdot(q, k, NT) * scale
    sc = sc - jnp.max(sc, axis=-1, keepdims=True)
    p = jnp.exp(sc)
    return p / jnp.sum(p, axis=-1, keepdims=True)


def _mem_sublayer(hq, w_q, kv, w_o, x, g, *, name, tm=512, plan=None):
    s, d = hq.shape
    hd = d // MEM_HEADS
    scale = 1.0 / math.sqrt(hd)
    tm = min(tm, s)

    def body(hq_ref, wq_ref, kv_ref, wo_ref, x_ref, g_ref, q_ref, o_ref, x2_ref, h_ref):
        q_ref[...] = _bdot(hq_ref[...], wq_ref[...], NN).astype(BF16)
        for h in range(MEM_HEADS):
            cols = slice(h * hd, (h + 1) * hd)
            p = _mem_probs(q_ref[:, cols], kv_ref[:, cols], scale)
            o_ref[:, cols] = _bdot(p, kv_ref[:, d + h * hd:d + (h + 1) * hd], NN).astype(BF16)
        xv = _bdot(o_ref[...], wo_ref[...], NN) + x_ref[...]
        x2_ref[...] = xv
        _rms_fwd_tail(xv, g_ref, h_ref)

    row = pl.BlockSpec((tm, d), lambda i: (i, 0))
    whole = lambda a: pl.BlockSpec(a.shape, lambda i: (0,) * a.ndim)
    return _call(body, name=name, grid=(s // tm,),
                 in_specs=[row, whole(w_q), whole(kv), whole(w_o), row, whole(g)], out_specs=[row] * 4,
                 out_shape=[jax.ShapeDtypeStruct((s, d), BF16), jax.ShapeDtypeStruct((s, d), BF16),
                            jax.ShapeDtypeStruct((s, d), F32), jax.ShapeDtypeStruct((s, d), BF16)],
                 scratch=[], args=[hq, w_q, kv, w_o, x, g], plan=plan)


def _mem_sublayer_bwd(dx2b, dx2, x, g, qm, kv, w_q, w_o, *, name, tm=512, plan=None):
    s, d = qm.shape
    hd = d // MEM_HEADS
    scale = 1.0 / math.sqrt(hd)
    tm = min(tm, s)

    def body(dyb_ref, dres_ref, x_ref, g_ref, q_ref, kv_ref, wq_ref, wo_ref, dx_ref, dxb_ref, dg_ref, dq_ref, dkv_ref):
        i = pl.program_id(0)

        @pl.when(i == 0)
        def _():
            dkv_ref[...] = jnp.zeros_like(dkv_ref)
        dom = _bdot(dyb_ref[...], wo_ref[...], NT).astype(BF16)
        for h in range(MEM_HEADS):
            cols = slice(h * hd, (h + 1) * hd)
            vcols = slice(d + h * hd, d + (h + 1) * hd)
            q, k, v, do = q_ref[:, cols], kv_ref[:, cols], kv_ref[:, vcols], dom[:, cols]
            p = _mem_probs(q, k, scale)
            dp = _bdot(do, v, NT)
            ds = p * (dp - jnp.sum(dp * p, axis=-1, keepdims=True)) * scale
            dq_ref[:, cols] = _bdot(ds, k, NN).astype(BF16)
            dkv_ref[:, cols] += _bdot(ds, q, TN)
            dkv_ref[:, vcols] += _bdot(p, do, TN)
        dh = _bdot(dq_ref[...], wq_ref[...], NT)
        _rms_bwd_tail(i, dh, x_ref, g_ref, dres_ref, dx_ref, dxb_ref, dg_ref)

    row = pl.BlockSpec((tm, d), lambda i: (i, 0))
    whole = lambda a: pl.BlockSpec(a.shape, lambda i: (0,) * a.ndim)
    return _call(body, name=name, grid=(s // tm,),
                 in_specs=[row, row, row, whole(g), row, whole(kv), whole(w_q), whole(w_o)],
                 out_specs=[row, row, pl.BlockSpec((8, d), lambda i: (0, 0)), row, whole(kv)],
                 out_shape=[jax.ShapeDtypeStruct((s, d), F32), jax.ShapeDtypeStruct((s, d), BF16),
                            jax.ShapeDtypeStruct((8, d), F32), jax.ShapeDtypeStruct((s, d), BF16),
                            jax.ShapeDtypeStruct(kv.shape, F32)],
                 scratch=[], args=[dx2b, dx2, x, g, qm, kv, w_q, w_o], plan=plan)


def _sb_consts(t):
    row = lax.broadcasted_iota(jnp.int32, (t, t), 0)
    col = lax.broadcasted_iota(jnp.int32, (t, t), 1)
    lane = lax.broadcasted_iota(jnp.int32, (t, LANES), 1)
    return row, col, lane < SB_HEAD_DIM


def _sb_logits(q, k):
    z2 = jnp.minimum(_bdot(q, k, NT) * LOG2_E, SB_CLAMP)
    return z2, jnp.exp2(z2)


def _tri_sum(v, tri):
    hi = v.astype(BF16)
    lo = (v - hi.astype(F32)).astype(BF16)
    return _bdot(hi, tri, NN) + _bdot(lo, tri, NN)


def _sb_fwd(proj, *, name, plan=None):
    s = proj.shape[0]
    t = SB_TILE
    n_q = s // t
    scale = 1.0 / math.sqrt(SB_HEAD_DIM)
    k_blk, v_blk = SB_WIDTH // LANES, 2 * SB_WIDTH // LANES

    def body(q_ref, k_ref, v_ref, o_ref, c_ref, first_ref, acc_ref, c_scr):
        i = pl.program_id(1)
        row, col, head0 = _sb_consts(t)
        later = (row > col).astype(BF16)
        valid = col < row
        qs = q_ref[...] * scale
        q2 = (jnp.where(head0, qs, 0), jnp.where(head0, 0, qs))

        def tiles(kbs, diag_first, carry):
            kt = [k_ref[pl.ds(pl.multiple_of(kb * t, t), t), :] for kb in kbs]
            vt = [v_ref[pl.ds(pl.multiple_of(kb * t, t), t), :] for kb in kbs]
            jobs = [(n, h) for n in range(len(kbs)) for h in range(2)]
            masked = lambda n: diag_first and n == 0
            zs = {(n, h): _sb_logits(q2[h], kt[n]) for n, h in jobs}
            fail = {j: jnp.log2(1.0 + zs[j][1]) for j in jobs}
            fail = {j: jnp.where(valid, fail[j], 0.0) if masked(j[0]) else fail[j] for j in jobs}
            cum = {j: _tri_sum(fail[j], later) for j in jobs}
            run, before = list(carry), {}
            for n, h in jobs:
                before[n, h] = run[h]
                run[h] = run[h] + cum[n, h][:, 0:1] + fail[n, h][:, 0:1]
            w = {j: jnp.exp2(zs[j][0] - fail[j] - cum[j] - before[j]) for j in jobs}
            w = {j: jnp.where(valid, w[j], 0.0) if masked(j[0]) else w[j] for j in jobs}
            for n, h in jobs:
                acc_ref[h] += _bdot(w[n, h], vt[n], NN)
            return tuple(run)

        acc_ref[...] = jnp.zeros_like(acc_ref)
        zero = jnp.zeros((t, 1), F32)

        def alive(carry):
            return (jnp.minimum(jnp.min(carry[0]), jnp.min(carry[1])) < SB_DEAD).astype(jnp.int32)

        def step(state):
            kb, _, c0, c1 = state
            new = tiles([kb], False, (c0, c1))
            return kb - 1, alive(new), new[0], new[1]

        @pl.when(i == 0)
        def _():
            c_scr[0], c_scr[1] = tiles([i], True, (zero, zero))

        @pl.when(i > 0)
        def _():
            c_scr[0], c_scr[1] = tiles([i, i - 1], True, (zero, zero))
        carry = (c_scr[0], c_scr[1])
        kb, _, c0, c1 = lax.while_loop(lambda st: jnp.logical_and(st[0] >= 0, st[1] > 0), step,
                                       (i - 2, alive(carry), carry[0], carry[1]))
        kb = jnp.maximum(kb, -1)
        o_ref[...] = jnp.where(head0, acc_ref[0], acc_ref[1]).astype(BF16)
        c_ref[...] = jnp.where(lax.broadcasted_iota(jnp.int32, (t, 2), 1) == 0, c0, c1)
        first_ref[pl.program_id(0), i] = (kb + 1).astype(F32)

    return _call(
        body, name=name, grid=(SB_HEADS // 2, n_q),
        in_specs=[pl.BlockSpec((t, LANES), lambda p, i: (i, p)),
                  pl.BlockSpec((s, LANES), lambda p, i: (0, k_blk + p)),
                  pl.BlockSpec((s, LANES), lambda p, i: (0, v_blk + p))],
        out_specs=[pl.BlockSpec((t, LANES), lambda p, i: (i, p)),
                   pl.BlockSpec((None, t, 2), lambda p, i: (p, i, 0)),
                   pl.BlockSpec(memory_space=pltpu.SMEM)],
        out_shape=[jax.ShapeDtypeStruct((s, SB_WIDTH), BF16), jax.ShapeDtypeStruct((SB_HEADS // 2, s, 2), F32),
                   jax.ShapeDtypeStruct((SB_HEADS // 2, n_q), F32)],
        scratch=[pltpu.VMEM((2, t, LANES), F32), pltpu.VMEM((2, t, 1), F32)], args=[proj, proj, proj], plan=plan)


def _sb_bwd(proj, do_a, ctot, first, *, name, plan=None):
    s = proj.shape[0]
    t = SB_TILE
    n_q = s // t
    scale = 1.0 / math.sqrt(SB_HEAD_DIM)
    k_blk, v_blk = SB_WIDTH // LANES, 2 * SB_WIDTH // LANES

    def body(q_ref, k_ref, v_ref, do_ref, c_ref, first_ref, dq_ref, dk_ref, dv_ref, dq_acc, dk_acc, dv_acc):
        i = pl.program_id(1)
        kb0 = jnp.clip(first_ref[pl.program_id(0), i].astype(jnp.int32), 0, i)
        row, col, head0 = _sb_consts(t)
        upto = (row <= col).astype(BF16)
        before = (row < col).astype(BF16)
        valid = col < row
        qs = q_ref[...] * scale
        q2 = (jnp.where(head0, qs, 0), jnp.where(head0, 0, qs))
        do = do_ref[...]
        do2 = (jnp.where(head0, do, 0), jnp.where(head0, 0, do))
        ctot2 = (c_ref[:, 0:1], c_ref[:, 1:2])

        @pl.when(i == 0)
        def _():
            dk_acc[...] = jnp.zeros_like(dk_acc)
            dv_acc[...] = jnp.zeros_like(dv_acc)
        dq_acc[...] = jnp.zeros_like(dq_acc)

        def tiles(kbs, diag_last, carry):
            rows = [pl.ds(pl.multiple_of(kb * t, t), t) for kb in kbs]
            kt = [k_ref[r, :] for r in rows]
            vt = [v_ref[r, :] for r in rows]
            jobs = [(n, h) for n in range(len(kbs)) for h in range(2)]
            masked = lambda n: diag_last and n == len(kbs) - 1
            t_last = slice(t - 1, t)
            zs = {(n, h): _sb_logits(q2[h], kt[n]) for n, h in jobs}
            dw = {(n, h): _bdot(do2[h], vt[n], NT) for n, h in jobs}
            fail = {j: jnp.log2(1.0 + zs[j][1]) for j in jobs}
            fail = {j: jnp.where(valid, fail[j], 0.0) if masked(j[0]) else fail[j] for j in jobs}
            cum = {j: _tri_sum(fail[j], upto) for j in jobs}
            miss = {j: jnp.exp2(-fail[j]) for j in jobs}
            beta = {j: zs[j][1] * miss[j] for j in jobs}
            fail_run, fail_before = list(carry[0::2]), {}
            for n, h in jobs:
                fail_before[n, h] = fail_run[h]
                fail_run[h] = fail_run[h] + cum[n, h][:, t_last]
            w = {(n, h): beta[n, h] * jnp.exp2(fail_before[n, h] + cum[n, h] - ctot2[h]) for n, h in jobs}
            w = {j: jnp.where(valid, w[j], 0.0) if masked(j[0]) else w[j] for j in jobs}
            g = {j: w[j] * dw[j] for j in jobs}
            g_local = {j: _bdot(g[j], before, NN) for j in jobs}
            for n, h in jobs:
                dv_acc[rows[n], :] += _bdot(w[n, h], do2[h], TN)
            g_run, dz = list(carry[1::2]), {}
            for n, h in jobs:
                g_sum = g_run[h] + g_local[n, h]
                dz[n, h] = g[n, h] * miss[n, h] - beta[n, h] * g_sum
                g_run[h] = g_sum[:, t_last] + g[n, h][:, t_last]
            dz = {j: jnp.where(valid, dz[j], 0.0) if masked(j[0]) else dz[j] for j in jobs}
            for n, h in jobs:
                dq_acc[h] += _bdot(dz[n, h], kt[n], NN)
                dk_acc[rows[n], :] += _bdot(dz[n, h], q2[h], TN)
            return fail_run[0], g_run[0], fail_run[1], g_run[1]

        zero = jnp.zeros((t, 1), F32)
        carry = lax.fori_loop(kb0, i - 1, lambda n, c: tiles([n], False, c), (zero,) * 4)

        @pl.when(i == 0)
        def _():
            tiles([i], True, carry)

        @pl.when(i > 0)
        def _():
            tiles([i - 1, i], True, carry)
        dq_ref[...] = (jnp.where(head0, dq_acc[0], dq_acc[1]) * scale).astype(BF16)

        @pl.when(i == n_q - 1)
        def _():
            dk_ref[...] = dk_acc[...].astype(BF16)
            dv_ref[...] = dv_acc[...].astype(BF16)

    outs = _call(
        body, name=name, grid=(SB_HEADS // 2, n_q),
        in_specs=[pl.BlockSpec((t, LANES), lambda p, i: (i, p)),
                  pl.BlockSpec((s, LANES), lambda p, i: (0, k_blk + p)),
                  pl.BlockSpec((s, LANES), lambda p, i: (0, v_blk + p)),
                  pl.BlockSpec((t, LANES), lambda p, i: (i, p)),
                  pl.BlockSpec((None, t, 2), lambda p, i: (p, i, 0)),
                  pl.BlockSpec(memory_space=pltpu.SMEM)],
        out_specs=[pl.BlockSpec((t, LANES), lambda p, i: (i, p)),
                   pl.BlockSpec((s, LANES), lambda p, i: (0, p)),
                   pl.BlockSpec((s, LANES), lambda p, i: (0, p))],
        out_shape=[jax.ShapeDtypeStruct((s, SB_WIDTH), BF16)] * 3,
        scratch=[pltpu.VMEM((2, t, LANES), F32), pltpu.VMEM((s, LANES), F32), pltpu.VMEM((s, LANES), F32)],
        args=[proj, proj, proj, do_a, ctot, first], plan=plan)
    return jnp.concatenate(outs, axis=1)


def _mm_gathered(a, key, plan, *, name, out3=False, w_t=False):
    src = plan.gathering(key)
    if src is None:
        return _mm_nn(a, plan.weight(key), name=name, out3=out3, w_t=w_t)
    out, w_all = _mm_gathering(a, src, name=name, out3=out3, w_t=w_t)
    plan.set_weight(key, w_all)
    return out


def _local_step(x, mem, target, gains, plan):
    g_mix, g_memq, g_memkv, g_ffn, g_fin = gains
    d = x.shape[1]

    h0 = _rms_fwd(x, g_mix, name="rms_mix")
    proj = _mm_gathered(h0, "in", plan, name="mm_in")
    w_in = plan.weight("in")
    o_a, ctot, first = _sb_fwd(proj, name="sb_fwd", plan=plan)
    conv_w = plan.weight("conv")
    y_b = _conv_fwd(proj, conv_w, name="conv_fwd")
    w_a, w_b, w_mix = plan.weight("a"), plan.weight("b"), plan.weight("mix")
    br_a = _mm_nn(o_a, w_a, name="mm_branch_a")
    br_b = _mm_nn(y_b, w_b, name="mm_branch_b")
    x1, hq, merged = _mix_out(br_a, br_b, proj, w_mix[0], x, g_memq, name="mm_mix", plan=plan)
    w_mq, w_kv, w_mo = plan.weight("mq")[0], plan.weight("kv"), plan.weight("mo")[0]
    mn = _rms_fwd(mem, g_memkv, name="rms_memkv")
    kv = _mm_nn(mn, w_kv, name="mm_memkv")
    qm, om, x2, hf = _mem_sublayer(hq, w_mq, kv, w_mo, x1, g_ffn, name="mem_sublayer", plan=plan)
    gu = _mm_gathered(hf, "fi", plan, name="mm_ffn_in", out3=True, w_t=True)
    w_fi, w_fo = plan.weight("fi"), plan.weight("fo")
    dx3, dx3b, dg_fin, loss, act = _ffn_out_loss(gu, w_fo, x2, g_fin, target, name="mm_ffn_out")

    plan.grad("fo", _mm_tn_a3(act, dx3b, name="mm_d_w_ffn_out"))
    dgu = _ffn_out_bwd(dx3b, w_fo, gu, name="mm_d_act")
    plan.grad("fi", _mm_tn_a3(dgu, hf, name="mm_d_w_ffn_in"))
    dx2, dx2b, dg_ffn = _mm_nt_rms(dgu, w_fi, x2, g_ffn, dx3, name="mm_d_hf", dy3=True, w_nn=True, tm=256, plan=plan)

    plan.grad("mo", _mm_tn(om, dx2b, d, name="mm_d_w_memo"))
    dx1, dx1b, dg_memq, dqm, dkv = _mem_sublayer_bwd(dx2b, dx2, x1, g_memq, qm, kv, w_mq, w_mo, name="mem_sublayer_bwd",
                                                    plan=plan)
    plan.grad("mq", _mm_tn(hq, dqm, d, name="mm_d_w_memq"))
    plan.grad("kv", _mm_tn(mn, dkv, w_kv.shape[2], name="mm_d_w_memkv"))
    _, _, dg_memkv = _mm_nt_rms(dkv, w_kv, mem, g_memkv, None, name="mm_d_mn")

    plan.grad("mix", _mm_tn(merged, dx1b, d, name="mm_d_w_mix"))
    dbr_a, dbr_b, dgab = _mix_out_bwd(dx1b, w_mix[0], br_a, br_b, proj, name="mm_d_merged", plan=plan)
    plan.grad("a", _mm_tn(o_a, dbr_a, d, name="mm_d_w_branch_a"))
    do_a = _mm_nt(dbr_a, w_a, name="mm_d_o_a")
    plan.grad("b", _mm_tn(y_b, dbr_b, d, name="mm_d_w_branch_b"))
    dy_b = _mm_nt(dbr_b, w_b, name="mm_d_y_b")
    dconv, dconv_w = _conv_bwd(dy_b, proj, conv_w, name="conv_bwd", plan=plan)
    dqkv = _sb_bwd(proj, do_a, ctot, first, name="sb_bwd", plan=plan)
    dproj = jnp.concatenate([dqkv, dconv, dgab], axis=1)
    rows_in1 = d // IN_SPLIT[1] * (IN_SPLIT[1] - IN_SPLIT[0])
    plan.grad("in0", _mm_tn(h0, dproj, w_in.shape[2], name="mm_d_w_in0", tm=d - rows_in1, k_tiles=(0, 1)))
    plan.grad("in1", _mm_tn(h0, dproj, w_in.shape[2], name="mm_d_w_in1", tm=rows_in1,
                            k_tiles=(d // rows_in1 - 1, 1), plan=plan))
    dh0 = _mm_nt(dproj, w_in, name="mm_d_h0", out_dtype=F32, plan=plan)
    dx0, _, dg_mix = _rms_bwd(x, g_mix, dh0, dx1, name="rms_mix_bwd", plan=plan)

    return dx0, (dg_mix, dg_memq, dg_memkv, dg_ffn, dg_fin, dconv_w, loss)


def _row_tile(a, target=512):
    tm = min(a, target)
    while a % tm:
        tm -= 8
    return tm


def _sum_with_sibling(part, recv, core, *, name):
    _, a, b = part.shape
    tm = _row_tile(a)

    def body(core_ref, p_ref, r_ref, o_ref):
        o_ref[...] = (p_ref[...].astype(F32) + r_ref[...].astype(F32)).astype(o_ref.dtype)

    return pl.pallas_call(
        body, name=name,
        grid_spec=pltpu.PrefetchScalarGridSpec(
            num_scalar_prefetch=1, grid=(N_CHIP, a // tm),
            in_specs=[pl.BlockSpec((None, tm, b), lambda q, i, core_ref: (2 * q + core_ref[0], i, 0)),
                      pl.BlockSpec((None, tm, b), lambda q, i, core_ref: (q, i, 0))],
            out_specs=pl.BlockSpec((None, tm, b), lambda q, i, core_ref: (q, i, 0))),
        out_shape=jax.ShapeDtypeStruct((N_CHIP, a, b), part.dtype), compiler_params=_params(2))(core, part, recv)


def _adam_math(wv, g, m, v):
    m = ADAM_B1 * m + (1.0 - ADAM_B1) * g
    v = ADAM_B2 * v + (1.0 - ADAM_B2) * (g * g)
    m_hat = m / (1.0 - ADAM_B1 ** ADAM_STEP)
    v_hat = v / (1.0 - ADAM_B2 ** ADAM_STEP)
    delta = -ADAM_LR * (m_hat / (jnp.sqrt(v_hat) + ADAM_EPS) + ADAM_WD * wv)
    return delta, m, v


def _adam_sharded(wv, m, v, own, recv, chip, *, name):
    a, b = wv.shape
    tm = _row_tile(a)

    def body(chip_ref, w_ref, m_ref, v_ref, own_ref, recv_ref, g_ref, d_ref, nm_ref, nv_ref):
        g = own_ref[...].astype(F32)
        for j in range(3):
            g = g + recv_ref[j].astype(F32)
        delta, nm, nv = _adam_math(w_ref[...], g, m_ref[...], v_ref[...])
        g_ref[...] = g
        d_ref[...] = delta
        nm_ref[...] = nm
        nv_ref[...] = nv

    tile = pl.BlockSpec((tm, b), lambda i, chip_ref: (i, 0))
    return pl.pallas_call(
        body, name=name,
        grid_spec=pltpu.PrefetchScalarGridSpec(
            num_scalar_prefetch=1, grid=(a // tm,),
            in_specs=[tile, tile, tile,
                      pl.BlockSpec((None, tm, b), lambda i, chip_ref: (chip_ref[0], i, 0)),
                      pl.BlockSpec((3, tm, b), lambda i, chip_ref: (0, i, 0))],
            out_specs=[tile] * 4),
        out_shape=[jax.ShapeDtypeStruct((a, b), F32)] * 4, compiler_params=_params(1))(chip, wv, m, v, own, recv)


def _sum_devices(gathered, *, name):
    _, r, c = gathered.shape

    def body(g_ref, o_ref):
        total = g_ref[0]
        for j in range(1, N_DEV):
            total = total + g_ref[j]
        o_ref[...] = total

    return pl.pallas_call(body, name=name, out_shape=jax.ShapeDtypeStruct((r, c), F32))(gathered)


def _adam_small(wv, g, m, v, *, name):
    def body(w_ref, g_ref, m_ref, v_ref, d_ref, nm_ref, nv_ref):
        delta, nm, nv = _adam_math(w_ref[...], g_ref[...], m_ref[...], v_ref[...])
        d_ref[...] = delta
        nm_ref[...] = nm
        nv_ref[...] = nv

    return pl.pallas_call(body, name=name, out_shape=[jax.ShapeDtypeStruct(wv.shape, F32)] * 3)(wv, g, m, v)


BIG = ("in", "a", "b", "mix", "mq", "kv", "mo", "fi", "fo")
ROW_SHARDED = ("mix", "mq", "mo")
UNSHARDED = ("a", "b")
FFN_GROUPS = 4
IN_SPLIT = (3, 4)
SMALL_ROWS = 16


class _Plan:
    FUSED = ("in",)
    GATHER_ON = {"sb_fwd": ("a", "b", "mix", "kv", "mq", "mo", "conv", "fi0"), "mm_mix": ("fo",),
                 "mem_sublayer": ("fi1",)}
    SIBLING_ON = {"mm_d_hf": ("fo", "fi"), "mm_d_merged": ("mo", "mq", "kv"), "conv_bwd": ("mix", "a", "b"),
                  "mm_d_w_in1": ("in0",), "mm_d_h0": ("in1",)}
    CHIPS_ON = {"mem_sublayer_bwd": ("fo",), "sb_bwd": ("fi", "mo", "mq", "kv", "mix", "a", "b"), "mm_d_h0": ("in0",),
                "rms_mix_bwd": ("in1",)}

    def __init__(self, shards, core):
        self.shards, self.core = shards, core
        self.w, self.parts, self.chip_sums, self.from_chips = {}, {}, {}, {}

    def gathering(self, k):
        return self.shards[k] if k in self.FUSED else None

    def comm(self, name):
        comms = []
        if name in self.GATHER_ON:
            comms.append(_gather_comm([self.shards[k] for k in self.GATHER_ON[name]]))
        if name in self.SIBLING_ON:
            comms.append(_sibling_comm([self.parts[k] for k in self.SIBLING_ON[name]]))
        if name in self.CHIPS_ON:
            comms.append(_chips_comm([self.chip_sums[k] for k in self.CHIPS_ON[name]]))
        return _join_comms(comms) if comms else None

    def landed(self, name, outs):
        outs = list(outs)
        for k in self.GATHER_ON.get(name, ()):
            self.set_weight(k, outs.pop(0))
        for k in self.SIBLING_ON.get(name, ()):
            self.chip_sums[k] = _sum_with_sibling(self.parts[k], outs.pop(0), self.core, name="sum_with_sibling_" + k)
        for k in self.CHIPS_ON.get(name, ()):
            self.from_chips[k] = outs.pop(0)

    def set_weight(self, k, gathered):
        _, a, b = gathered.shape
        if k in ROW_SHARDED:
            gathered = gathered.reshape(1, N_DEV * a, b)
        elif k in UNSHARDED:
            gathered = jnp.transpose(gathered, (1, 0, 2)).reshape(1, a, N_DEV * b)
        elif k == "fo":
            gathered = gathered.reshape(FFN_GROUPS, N_DEV * a // FFN_GROUPS, b)
        elif k == "conv":
            n_conv = CONV_WIDTH // N_DEV
            gathered = jnp.transpose(gathered[:, :3, :n_conv], (1, 0, 2)).reshape(3, CONV_WIDTH)
        self.w[k] = gathered
        if k == "fi1":
            self.w["fi"] = jnp.concatenate([self.w["fi0"], gathered], axis=2)

    def weight(self, k):
        return self.w[k]

    def grad(self, k, g):
        _, a, b = g.shape
        if k in ROW_SHARDED:
            g = g.reshape(N_DEV, a // N_DEV, b)
        elif k in UNSHARDED:
            g = jnp.transpose(g.reshape(a, N_DEV, b // N_DEV), (1, 0, 2))
        elif k == "fo":
            g = g.reshape(N_DEV, FFN_GROUPS * a // N_DEV, b)
        self.parts[k] = g


def kernel(x, mem, norm_mix, w_in, conv_w, w_branch_a, w_branch_b, w_mix_out, norm_mem_q, norm_mem_kv, w_mem_q, w_mem_kv, w_mem_o, norm_ffn, w_ffn_in, w_ffn_out, norm_final, loss_target, m_norm_mix, m_w_in, m_conv_w, m_w_branch_a, m_w_branch_b, m_w_mix_out, m_norm_mem_q, m_norm_mem_kv, m_w_mem_q, m_w_mem_kv, m_w_mem_o, m_norm_ffn, m_w_ffn_in, m_w_ffn_out, m_norm_final, v_norm_mix, v_w_in, v_conv_w, v_w_branch_a, v_w_branch_b, v_w_mix_out, v_norm_mem_q, v_norm_mem_kv, v_w_mem_q, v_w_mem_kv, v_w_mem_o, v_norm_ffn, v_w_ffn_in, v_w_ffn_out, v_norm_final):
    d = x.shape[-1]
    xi, yi, ci = lax.axis_index("x"), lax.axis_index("y"), lax.axis_index("c")
    core = jnp.reshape(ci, (1,)).astype(jnp.int32)
    chip = jnp.reshape(2 * xi + yi, (1,)).astype(jnp.int32)
    dev = 4 * xi + 2 * yi + ci

    big_w = dict(zip(BIG, (w_in, w_branch_a, w_branch_b, w_mix_out, w_mem_q, w_mem_kv, w_mem_o, w_ffn_in, w_ffn_out)))
    big_m = dict(zip(BIG, (m_w_in, m_w_branch_a, m_w_branch_b, m_w_mix_out, m_w_mem_q, m_w_mem_kv, m_w_mem_o, m_w_ffn_in, m_w_ffn_out)))
    big_v = dict(zip(BIG, (v_w_in, v_w_branch_a, v_w_branch_b, v_w_mix_out, v_w_mem_q, v_w_mem_kv, v_w_mem_o, v_w_ffn_in, v_w_ffn_out)))

    flip = lambda t, k: jnp.transpose(t) if k == "fi" else t
    shards = {k: flip(big_w[k][0], k).astype(BF16) for k in BIG}
    shards["fi0"], shards["fi1"] = shards["fi"][:, :d // 2], shards["fi"][:, d // 2:]
    n_conv = conv_w.shape[-1]
    shards["conv"] = jnp.zeros((8, LANES), F32).at[:3, :n_conv].set(conv_w[0])
    plan = _Plan(shards, core)

    gains = (norm_mix, norm_mem_q, norm_mem_kv, norm_ffn, norm_final.reshape(1, d))
    dx0, small = _local_step(x[0], mem[0], loss_target[0], gains, plan)

    grads, deltas, new_m, new_v = {}, {}, {}, {}
    for k in BIG:
        lead = big_w[k].shape
        wv, mv, vv = flip(big_w[k][0], k), flip(big_m[k][0], k), flip(big_v[k][0], k)
        if k == "in":
            half = wv.shape[0] * IN_SPLIT[0] // IN_SPLIT[1]
            lo = _adam_sharded(wv[:half], mv[:half], vv[:half], plan.chip_sums["in0"], plan.from_chips["in0"], chip,
                               name="adam_in0")
            hi = _adam_sharded(wv[half:], mv[half:], vv[half:], plan.chip_sums["in1"], plan.from_chips["in1"], chip,
                               name="adam_in1")
            outs = [jnp.concatenate(pair, axis=0) for pair in zip(lo, hi)]
        else:
            outs = _adam_sharded(wv, mv, vv, plan.chip_sums[k], plan.from_chips[k], chip, name="adam_" + k)
        grads[k], deltas[k], new_m[k], new_v[k] = (flip(t, k).reshape(lead) for t in outs)

    dg_mix, dg_memq, dg_memkv, dg_ffn, dg_fin, dconv_w, loss = small
    conv_rows = jnp.zeros((3, d), F32).at[:, :CONV_WIDTH].set(dconv_w[:3])
    block = jnp.concatenate([dg_mix[:1], dg_memq[:1], dg_memkv[:1], dg_ffn[:1], dg_fin[:1], conv_rows,
                             jnp.broadcast_to(loss[:1, :1], (1, d)), jnp.zeros((SMALL_ROWS - 9, d), F32)], axis=0)
    total = _sum_devices(_exchange(_gather_comm([block]), name="gather_small")[0], name="sum_small")
    g_conv = lax.dynamic_slice(total[5:8, :CONV_WIDTH], (0, dev * n_conv), (3, n_conv))
    small_w = [norm_mix, norm_mem_q, norm_mem_kv, norm_ffn, norm_final.reshape(1, d), conv_w[0]]
    small_m = [m_norm_mix, m_norm_mem_q, m_norm_mem_kv, m_norm_ffn, m_norm_final.reshape(1, d), m_conv_w[0]]
    small_v = [v_norm_mix, v_norm_mem_q, v_norm_mem_kv, v_norm_ffn, v_norm_final.reshape(1, d), v_conv_w[0]]
    small_g = [total[0:1], total[1:2], total[2:3], total[3:4], total[4:5], g_conv]
    small_names = ["norm_mix", "norm_mem_q", "norm_mem_kv", "norm_ffn", "norm_final", "conv_w"]
    sg, sd, sm, sv = {}, {}, {}, {}
    for nme, wv, g, m, v in zip(small_names, small_w, small_g, small_m, small_v):
        dl, nm, nv = _adam_small(wv, g, m, v, name="adam_" + nme)
        shape = norm_final.shape if nme == "norm_final" else (conv_w.shape if nme == "conv_w" else wv.shape)
        sg[nme], sd[nme], sm[nme], sv[nme] = (t.reshape(shape) for t in (g, dl, nm, nv))

    def ordered(big, sml):
        return (sml["norm_mix"], big["in"], sml["conv_w"], big["a"], big["b"], big["mix"], sml["norm_mem_q"],
                sml["norm_mem_kv"], big["mq"], big["kv"], big["mo"], sml["norm_ffn"], big["fi"], big["fo"],
                sml["norm_final"])

    loss_out = total[8, 0]
    grad_x = dx0.reshape(x.shape)
    return (loss_out, grad_x, *ordered(grads, sg), *ordered(deltas, sd), *ordered(new_m, sm), *ordered(new_v, sv))
```

```python
import functools
import math

import jax
import jax.numpy as jnp
from jax import lax
from jax.experimental import pallas as pl
from jax.experimental.pallas import tpu as pltpu

F32 = jnp.float32
BF16 = jnp.bfloat16
MESH = pl.DeviceIdType.MESH

N_DEV = 8
N_CHIP = 4
NORM_EPS = 1e-6
SB_HEADS = 8
SB_HEAD_DIM = 64
SB_WIDTH = SB_HEADS * SB_HEAD_DIM
CONV_WIDTH = 512
MEM_HEADS = 4
ADAM_LR = 0.001
ADAM_B1 = 0.9
ADAM_B2 = 0.999
ADAM_EPS = 1e-08
ADAM_WD = 0.01
ADAM_STEP = 10

LANES = 128
VMEM_LIMIT_BYTES = 52 * 1024 * 1024
SB_TILE = 256
SB_DEAD = 159.0
SB_CLAMP = 126.0
LOG2_E = 1.4426950408889634

ANY = pl.BlockSpec(memory_space=pl.ANY)


def _params(n_grid):
    return pltpu.CompilerParams(dimension_semantics=("arbitrary",) * n_grid, vmem_limit_bytes=VMEM_LIMIT_BYTES)


def _bdot(a, b, dims):
    return lax.dot_general(a.astype(BF16), b.astype(BF16), (dims, ((), ())), preferred_element_type=F32)


NN = ((1,), (0,))
NT = ((1,), (1,))
TN = ((0,), (0,))


class _Comm:
    def __init__(self, ins, outs, n_sems, start, finish):
        self.ins, self.outs, self.n_sems, self.start, self.finish = ins, outs, n_sems, start, finish

    def sem_shapes(self):
        return [pltpu.SemaphoreType.DMA((k,)) for k in self.n_sems]


def _place():
    return lax.axis_index("x"), lax.axis_index("y"), lax.axis_index("c")


def _gather_comm(shards):
    n = len(shards)

    def copies(ins, outs, sems):
        send_sems, recv_sems, _ = sems
        x, y, c = _place()
        chips = [(1 - x, y), (x, 1 - y), (1 - x, 1 - y)]

        def copy(a, k, block, to, from_shard=False):
            dst = outs[a].at[4 * block[0] + 2 * block[1] + block[2]]
            return pltpu.make_async_remote_copy(
                src_ref=ins[a] if from_shard else dst, dst_ref=dst, send_sem=send_sems.at[a * 7 + k],
                recv_sem=recv_sems.at[a * 7 + k], device_id=to, device_id_type=MESH)

        me, sibling = (x, y, c), (x, y, 1 - c)
        own = [[copy(a, 0, me, sibling, True)] + [copy(a, 1 + j, me, (*chip, c), True) for j, chip in enumerate(chips)]
               for a in range(n)]
        landed = [[copy(a, 1 + j, (*chip, c), me) for j, chip in enumerate(chips)] for a in range(n)]
        passed = [[copy(a, 4 + j, (*chip, c), sibling) for j, chip in enumerate(chips)] for a in range(n)]
        from_sibling = [[copy(a, 0, sibling, me)] + [copy(a, 4 + j, (*chip, 1 - c), me) for j, chip in enumerate(chips)]
                        for a in range(n)]
        local = [pltpu.make_async_copy(ins[a], outs[a].at[4 * x + 2 * y + c], sems[2].at[a]) for a in range(n)]
        return own, landed, passed, from_sibling, local

    def start(ins, outs, sems):
        own, _, _, _, local = copies(ins, outs, sems)
        for a in range(n):
            local[a].start()
            for cp in own[a]:
                cp.start()

    def finish(ins, outs, sems):
        own, landed, passed, from_sibling, local = copies(ins, outs, sems)
        for a in range(n):
            for arrived, onward in zip(landed[a], passed[a]):
                arrived.wait_recv()
                onward.start()
        for a in range(n):
            for cp in from_sibling[a]:
                cp.wait_recv()
        for a in range(n):
            for cp in own[a] + passed[a]:
                cp.wait_send()
            local[a].wait()

    outs = [jax.ShapeDtypeStruct((N_DEV,) + s.shape, s.dtype) for s in shards]
    return _Comm(list(shards), outs, (7 * n, 7 * n, n), start, finish)


def _sibling_comm(parts):
    n = len(parts)

    def copies(ins, outs, sems):
        x, y, c = _place()
        return [pltpu.make_async_remote_copy(
            src_ref=ins[a].at[2 * q + 1 - c], dst_ref=outs[a].at[q], send_sem=sems[0].at[a * N_CHIP + q],
            recv_sem=sems[1].at[a * N_CHIP + q], device_id=(x, y, 1 - c), device_id_type=MESH)
            for a in range(n) for q in range(N_CHIP)]

    def start(ins, outs, sems):
        for cp in copies(ins, outs, sems):
            cp.start()

    def finish(ins, outs, sems):
        cps = copies(ins, outs, sems)
        for cp in cps:
            cp.wait_recv()
        for cp in cps:
            cp.wait_send()

    outs = [jax.ShapeDtypeStruct((N_CHIP,) + p.shape[1:], p.dtype) for p in parts]
    return _Comm(list(parts), outs, (N_CHIP * n, N_CHIP * n), start, finish)


def _chips_comm(parts):
    n = len(parts)

    def copies(ins, outs, sems):
        x, y, c = _place()
        chips = [(1 - x, y), (x, 1 - y), (1 - x, 1 - y)]
        return [pltpu.make_async_remote_copy(
            src_ref=ins[a].at[2 * px + py], dst_ref=outs[a].at[j], send_sem=sems[0].at[a * 3 + j],
            recv_sem=sems[1].at[a * 3 + j], device_id=(px, py, c), device_id_type=MESH)
            for a in range(n) for j, (px, py) in enumerate(chips)]

    def start(ins, outs, sems):
        for cp in copies(ins, outs, sems):
            cp.start()

    def finish(ins, outs, sems):
        cps = copies(ins, outs, sems)
        for cp in cps:
            cp.wait_recv()
        for cp in cps:
            cp.wait_send()

    outs = [jax.ShapeDtypeStruct((3,) + p.shape[1:], p.dtype) for p in parts]
    return _Comm(list(parts), outs, (3 * n, 3 * n), start, finish)


def _join_comms(comms):
    if len(comms) == 1:
        return comms[0]

    def split(refs, counts):
        out, at = [], 0
        for n in counts:
            out.append(refs[at:at + n])
            at += n
        return out

    def each(method):
        def run(ins, outs, sems):
            parts = zip(comms, split(ins, [len(c.ins) for c in comms]), split(outs, [len(c.outs) for c in comms]),
                        split(sems, [len(c.n_sems) for c in comms]))
            for c, c_ins, c_outs, c_sems in parts:
                getattr(c, method)(c_ins, c_outs, c_sems)
        return run

    return _Comm([a for c in comms for a in c.ins], [o for c in comms for o in c.outs],
                 tuple(k for c in comms for k in c.n_sems), each("start"), each("finish"))


def _exchange(comm, *, name):
    n_ci, n_co = len(comm.ins), len(comm.outs)

    def kern(*refs):
        c_ins, c_outs, sems = refs[:n_ci], refs[n_ci:n_ci + n_co], refs[n_ci + n_co:]
        comm.start(c_ins, c_outs, sems)
        comm.finish(c_ins, c_outs, sems)

    return pl.pallas_call(kern, name=name, in_specs=[ANY] * n_ci, out_specs=[ANY] * n_co, out_shape=comm.outs,
                          scratch_shapes=comm.sem_shapes())(*comm.ins)


def _call(body, *, name, grid, in_specs, out_specs, out_shape, scratch, args, plan=None):
    comm = plan.comm(name) if plan is not None else None
    if comm is None:
        return list(pl.pallas_call(functools.partial(body), name=name, grid=grid, in_specs=in_specs,
                                   out_specs=out_specs, out_shape=out_shape, scratch_shapes=scratch,
                                   compiler_params=_params(len(grid)))(*args))
    n_in, n_out, n_scr, n_ci, n_co = len(in_specs), len(out_specs), len(scratch), len(comm.ins), len(comm.outs)

    def kern(*refs):
        ins, c_ins, refs = refs[:n_in], refs[n_in:n_in + n_ci], refs[n_in + n_ci:]
        outs, c_outs, refs = refs[:n_out], refs[n_out:n_out + n_co], refs[n_out + n_co:]
        scr, sems = refs[:n_scr], refs[n_scr:]
        ids = [pl.program_id(ax) for ax in range(len(grid))]
        first = functools.reduce(jnp.logical_and, [i == 0 for i in ids])
        last = functools.reduce(jnp.logical_and, [i == g - 1 for i, g in zip(ids, grid)])

        @pl.when(first)
        def _():
            comm.start(c_ins, c_outs, sems)
        body(*ins, *outs, *scr)

        @pl.when(last)
        def _():
            comm.finish(c_ins, c_outs, sems)

    res = pl.pallas_call(kern, name=name, grid=grid, in_specs=list(in_specs) + [ANY] * n_ci,
                         out_specs=list(out_specs) + [ANY] * n_co, out_shape=list(out_shape) + comm.outs,
                         scratch_shapes=list(scratch) + comm.sem_shapes(),
                         compiler_params=_params(len(grid)))(*args, *comm.ins)
    plan.landed(name, list(res[n_out:]))
    return list(res[:n_out])


def _mm_body(dims, has_add, *refs):
    if has_add:
        a_ref, b_ref, add_ref, o_ref = refs
        total = _bdot(a_ref[...], b_ref[...], dims) + add_ref[...]
    else:
        a_ref, b_ref, o_ref = refs
        total = _bdot(a_ref[...], b_ref[...], dims)
    o_ref[...] = total.astype(o_ref.dtype)


def _mm_nt_body(j, n, dy_ref, w_ref, o_ref):
    total = _bdot(dy_ref[:, 0:n], w_ref[0], NT)
    for jj in range(1, j):
        total = total + _bdot(dy_ref[:, jj * n:(jj + 1) * n], w_ref[jj], NT)
    o_ref[...] = total.astype(o_ref.dtype)


def _mm_nn(a, w3, *, name, out_dtype=BF16, add=None, tm=1024, tn=None, out3=False, w_t=False, plan=None):
    m, kk = a.shape
    j, n = w3.shape[0], w3.shape[1 if w_t else 2]
    tm, tn = min(tm, m), n if tn is None else tn
    n_t = n // tn
    in_specs = [pl.BlockSpec((tm, kk), lambda i, jj: (i, 0)),
                pl.BlockSpec((None, tn, kk), lambda i, jj: (jj // n_t, jj % n_t, 0)) if w_t else
                pl.BlockSpec((None, kk, tn), lambda i, jj: (jj // n_t, 0, jj % n_t))]
    args = [a, w3]
    if add is not None:
        in_specs.append(pl.BlockSpec((tm, tn), lambda i, jj: (i, jj)))
        args.append(add)
    if out3:
        out_spec = pl.BlockSpec((None, tm, tn), lambda i, jj: (jj // n_t, i, jj % n_t))
        out_shape = jax.ShapeDtypeStruct((j, m, n), out_dtype)
    else:
        out_spec = pl.BlockSpec((tm, tn), lambda i, jj: (i, jj))
        out_shape = jax.ShapeDtypeStruct((m, j * n), out_dtype)
    return _call(
        functools.partial(_mm_body, NT if w_t else NN, add is not None), name=name, grid=(m // tm, j * n_t),
        in_specs=in_specs, out_specs=[out_spec], out_shape=[out_shape], scratch=[], args=args, plan=plan)[0]


def _mm_gathering(a, shard, *, name, out3=False, w_t=False, tm=1024):
    m, kk = a.shape
    n = shard.shape[0 if w_t else 1]
    tm = min(tm, m)
    n_i = m // tm

    def body(a_ref, shard_ref, o_ref, w_all, w_vmem, send_sems, recv_sems, copy_sems):
        jj, i = pl.program_id(0), pl.program_id(1)
        x, y, c = _place()
        me, sibling = (x, y, c), (x, y, 1 - c)
        chips = [(jnp.bitwise_xor(x, c), jnp.bitwise_xor(y, 1 - c)), (jnp.bitwise_xor(x, 1 - c), jnp.bitwise_xor(y, c)),
                 (1 - x, 1 - y)]
        sibling_chips = [chips[1], chips[0], chips[2]]

        def rows(block):
            return w_all.at[4 * block[0] + 2 * block[1] + block[2]]

        def remote(k, block, to, from_shard=False):
            return pltpu.make_async_remote_copy(
                src_ref=shard_ref if from_shard else rows(block), dst_ref=rows(block), send_sem=send_sems.at[k],
                recv_sem=recv_sems.at[k], device_id=to, device_id_type=MESH)

        def load(src):
            cp = pltpu.make_async_copy(src, w_vmem, copy_sems.at[1])
            cp.start()
            cp.wait()

        own = [remote(0, me, sibling, True)] + [remote(1 + j, me, (*chip, c), True) for j, chip in enumerate(chips)]
        passed = [remote(4 + j, (*chip, c), sibling) for j, chip in enumerate(chips)]
        local = pltpu.make_async_copy(shard_ref, rows(me), copy_sems.at[0])

        @pl.when(jnp.logical_and(i == 0, jj == 0))
        def _():
            local.start()
            own[0].start()
            own[1].start()
            load(shard_ref)

        @pl.when(jnp.logical_and(i == 0, jj == 1))
        def _():
            remote(0, sibling, me).wait_recv()
            load(rows(sibling))

        for j, chip in enumerate(chips):
            @pl.when(jnp.logical_and(i == 0, jj == 2 + 2 * j))
            def _():
                if j < 2:
                    own[1 + j].wait_send()
                    own[2 + j].start()
                remote(1 + j, (*chip, c), me).wait_recv()
                passed[j].start()
                load(rows((*chip, c)))

            @pl.when(jnp.logical_and(i == 0, jj == 3 + 2 * j))
            def _():
                block = (*sibling_chips[j], 1 - c)
                remote(4 + j, block, me).wait_recv()
                load(rows(block))

        o_ref[...] = _bdot(a_ref[...], w_vmem[...], NT if w_t else NN).astype(o_ref.dtype)

        @pl.when(jnp.logical_and(i == n_i - 1, jj == N_DEV - 1))
        def _():
            for cp in [own[0], own[3]] + passed:
                cp.wait_send()
            local.wait()

    def swept(jj):
        x, y, c = _place()
        first, second = 2 + 2 * c, 4 - 2 * c
        flips = (0b000, 0b001, first, second + 1, second, first + 1, 0b110, 0b111)
        return jnp.bitwise_xor(4 * x + 2 * y + c, sum(jnp.where(jj == k, f, 0) for k, f in enumerate(flips)))

    if out3:
        out_spec = pl.BlockSpec((None, tm, n), lambda jj, i: (swept(jj), i, 0))
        out_shape = jax.ShapeDtypeStruct((N_DEV, m, n), BF16)
    else:
        out_spec = pl.BlockSpec((tm, n), lambda jj, i: (i, swept(jj)))
        out_shape = jax.ShapeDtypeStruct((m, N_DEV * n), BF16)
    return pl.pallas_call(
        body, name=name, grid=(N_DEV, n_i),
        in_specs=[pl.BlockSpec((tm, kk), lambda jj, i: (i, 0)), ANY], out_specs=[out_spec, ANY],
        scratch_shapes=[pltpu.VMEM(shard.shape, shard.dtype), pltpu.SemaphoreType.DMA((7,)),
                        pltpu.SemaphoreType.DMA((7,)), pltpu.SemaphoreType.DMA((2,))],
        out_shape=[out_shape, jax.ShapeDtypeStruct((N_DEV,) + shard.shape, shard.dtype)],
        compiler_params=_params(2))(a, shard)


def _sigmoid(v):
    return 0.5 * jnp.tanh(0.5 * v) + 0.5


def _resident(w):
    return pl.BlockSpec(w.shape, lambda i: (0,) * w.ndim, pipeline_mode=pl.Buffered(1))


def _ffn_out_loss(gu3, w3, add, g, target, *, name, tm=512):
    j2, m, n = gu3.shape
    j = j2 // 2
    nn = w3.shape[2]
    tm = min(tm, m)

    def body(gu_ref, w_ref, add_ref, g_ref, t_ref, dx_ref, dxb_ref, dg_ref, loss_ref, act_ref):
        i = pl.program_id(0)
        xv = add_ref[...]
        for jj in range(j):
            gate = gu_ref[0, jj].astype(F32)
            act = (gate * _sigmoid(gate) * gu_ref[1, jj].astype(F32)).astype(BF16)
            act_ref[jj] = act
            xv = xv + _bdot(act, w_ref[jj], NN)
        gv = g_ref[...]
        r = lax.rsqrt(jnp.mean(xv * xv, axis=-1, keepdims=True) + NORM_EPS)
        xhat = xv * r
        err = xhat * gv - t_ref[...]
        _acc_rows(i, loss_ref, 0.5 * jnp.sum(jnp.mean(err * err, axis=-1, keepdims=True), axis=0, keepdims=True))
        dy = err * (1.0 / nn)
        dxhat = dy * gv
        dx = r * (dxhat - xhat * jnp.mean(dxhat * xhat, axis=-1, keepdims=True))
        dx_ref[...] = dx
        dxb_ref[...] = dx.astype(BF16)
        _acc_rows(i, dg_ref, jnp.sum(dy * xhat, axis=0, keepdims=True))

    row = pl.BlockSpec((tm, nn), lambda i: (i, 0))
    return _call(body, name=name, grid=(m // tm,),
                 in_specs=[pl.BlockSpec((2, j, tm, n), lambda i: (0, 0, i, 0)), _resident(w3),
                           row, pl.BlockSpec(g.shape, lambda i: (0, 0)), row],
                 out_specs=[row, row, pl.BlockSpec((8, nn), lambda i: (0, 0)), pl.BlockSpec((8, LANES), lambda i: (0, 0)),
                            pl.BlockSpec((j, tm, n), lambda i: (0, i, 0))],
                 out_shape=[jax.ShapeDtypeStruct((m, nn), F32), jax.ShapeDtypeStruct((m, nn), BF16),
                            jax.ShapeDtypeStruct((8, nn), F32), jax.ShapeDtypeStruct((8, LANES), F32),
                            jax.ShapeDtypeStruct((j, m, n), BF16)],
                 scratch=[], args=[gu3.reshape(2, j, m, n), w3, add, g, target])


def _ffn_out_bwd(dy, w3, gu3, *, name, tm=1024):
    m, nn = dy.shape
    j, n, _ = w3.shape
    tm = min(tm, m)

    def body(dy_ref, w_ref, gu_ref, dgu_ref):
        da = _bdot(dy_ref[...], w_ref[...], NT)
        gate = gu_ref[0].astype(F32)
        up = gu_ref[1].astype(F32)
        sg = _sigmoid(gate)
        silu = gate * sg
        dgu_ref[0] = (da * up * (sg + silu * (1.0 - sg))).astype(BF16)
        dgu_ref[1] = (da * silu).astype(BF16)

    out = _call(body, name=name, grid=(m // tm, j),
                in_specs=[pl.BlockSpec((tm, nn), lambda i, jj: (i, 0)),
                          pl.BlockSpec((None, n, nn), lambda i, jj: (jj, 0, 0)),
                          pl.BlockSpec((2, None, tm, n), lambda i, jj: (0, jj, i, 0))],
                out_specs=[pl.BlockSpec((2, None, tm, n), lambda i, jj: (0, jj, i, 0))],
                out_shape=[jax.ShapeDtypeStruct((2, j, m, n), BF16)], scratch=[],
                args=[dy, w3, gu3.reshape(2, j, m, n)])[0]
    return out.reshape(2 * j, m, n)


def _rms_fwd_tail(xv, g_ref, h_ref):
    r = lax.rsqrt(jnp.mean(xv * xv, axis=-1, keepdims=True) + NORM_EPS)
    h_ref[...] = (xv * r * g_ref[...]).astype(BF16)


def _rms_bwd_tail(i, dh, x_ref, g_ref, dres_ref, dx_ref, dxb_ref, dg_ref):
    xv = x_ref[...]
    r = lax.rsqrt(jnp.mean(xv * xv, axis=-1, keepdims=True) + NORM_EPS)
    xhat = xv * r
    dxhat = dh * g_ref[...]
    dx = r * (dxhat - xhat * jnp.mean(dxhat * xhat, axis=-1, keepdims=True))
    if dres_ref is not None:
        dx = dx + dres_ref[...]
    dx_ref[...] = dx
    dxb_ref[...] = dx.astype(BF16)
    _acc_rows(i, dg_ref, jnp.sum(dh * xhat, axis=0, keepdims=True))


def _mm_nt_rms(dy, w3, x, g, dres, *, name, dy3=False, w_nn=False, tm=512, plan=None):
    j = w3.shape[0]
    m, kk = x.shape
    n = dy.shape[2] if dy3 else dy.shape[1] // j
    tm = min(tm, m)

    def body(dy_ref, w_ref, x_ref, g_ref, *rest):
        dres_ref = rest[0] if dres is not None else None
        dx_ref, dxb_ref, dg_ref = rest[-3:]
        dh = None
        for jj in range(j):
            piece = dy_ref[jj] if dy3 else dy_ref[:, jj * n:(jj + 1) * n]
            part = _bdot(piece, w_ref[jj], NN if w_nn else NT)
            dh = part if dh is None else dh + part
        _rms_bwd_tail(pl.program_id(0), dh, x_ref, g_ref, dres_ref, dx_ref, dxb_ref, dg_ref)

    row = pl.BlockSpec((tm, kk), lambda i: (i, 0))
    in_specs = [pl.BlockSpec((j, tm, n), lambda i: (0, i, 0)) if dy3 else pl.BlockSpec((tm, j * n), lambda i: (i, 0)),
                _resident(w3), row, pl.BlockSpec(g.shape, lambda i: (0, 0))]
    args = [dy, w3, x, g]
    if dres is not None:
        in_specs.append(row)
        args.append(dres)
    return _call(body, name=name, grid=(m // tm,), in_specs=in_specs,
                 out_specs=[row, row, pl.BlockSpec((8, kk), lambda i: (0, 0))],
                 out_shape=[jax.ShapeDtypeStruct((m, kk), F32), jax.ShapeDtypeStruct((m, kk), BF16),
                            jax.ShapeDtypeStruct((8, kk), F32)], scratch=[], args=args, plan=plan)


def _mix_out(br_a, br_b, proj, w, x, g, *, name, tm=512, plan=None):
    s, d = br_a.shape
    tm = min(tm, s)

    def body(a_ref, b_ref, ga_ref, gb_ref, w_ref, x_ref, g_ref, x1_ref, h_ref, merged_ref):
        merged = (_sigmoid(ga_ref[...].astype(F32)) * a_ref[...].astype(F32)
                  + _sigmoid(gb_ref[...].astype(F32)) * b_ref[...].astype(F32)).astype(BF16)
        merged_ref[...] = merged
        xv = _bdot(merged, w_ref[...], NN) + x_ref[...]
        x1_ref[...] = xv
        _rms_fwd_tail(xv, g_ref, h_ref)

    row = pl.BlockSpec((tm, d), lambda i: (i, 0))
    return _call(body, name=name, grid=(s // tm,),
                 in_specs=[row, row, pl.BlockSpec((tm, d), lambda i: (i, 3)), pl.BlockSpec((tm, d), lambda i: (i, 4)),
                           pl.BlockSpec(w.shape, lambda i: (0, 0)), row, pl.BlockSpec(g.shape, lambda i: (0, 0))],
                 out_specs=[row, row, row],
                 out_shape=[jax.ShapeDtypeStruct((s, d), F32), jax.ShapeDtypeStruct((s, d), BF16),
                            jax.ShapeDtypeStruct((s, d), BF16)],
                 scratch=[], args=[br_a, br_b, proj, proj, w, x, g], plan=plan)


def _mm_tn_a3(a3, dy, *, name):
    j, t, n = a3.shape
    nn = dy.shape[1]
    return _call(functools.partial(_mm_body, TN, False), name=name, grid=(j,),
                 in_specs=[pl.BlockSpec((None, t, n), lambda jj: (jj, 0, 0)), pl.BlockSpec((t, nn), lambda jj: (0, 0))],
                 out_specs=[pl.BlockSpec((None, n, nn), lambda jj: (jj, 0, 0))],
                 out_shape=[jax.ShapeDtypeStruct((j, n, nn), BF16)], scratch=[], args=[a3, dy])[0]


def _mm_nt(dy, w3, *, name, out_dtype=BF16, tm=512, tn=1024, plan=None):
    m = dy.shape[0]
    j, kk, n = w3.shape
    tm, tn = min(tm, m), min(tn, kk)
    return _call(
        functools.partial(_mm_nt_body, j, n), name=name,
        grid=(m // tm, kk // tn),
        in_specs=[pl.BlockSpec((tm, j * n), lambda i, q: (i, 0)),
                  pl.BlockSpec((j, tn, n), lambda i, q: (0, q, 0))],
        out_specs=[pl.BlockSpec((tm, tn), lambda i, q: (i, q))],
        out_shape=[jax.ShapeDtypeStruct((m, kk), out_dtype)], scratch=[], args=[dy, w3], plan=plan)[0]


def _mm_tn(a, dy, n, *, name, out_dtype=BF16, tm=512, tn=None, k_tiles=None, plan=None):
    t, kk = a.shape
    j = dy.shape[1] // n
    tm, tn = min(tm, kk), n if tn is None else tn
    n_t = n // tn
    first, count = (0, kk // tm) if k_tiles is None else k_tiles
    return _call(
        functools.partial(_mm_body, TN, False), name=name,
        grid=(count, j * n_t),
        in_specs=[pl.BlockSpec((t, tm), lambda i, jj: (0, first + i)),
                  pl.BlockSpec((t, tn), lambda i, jj: (0, jj))],
        out_specs=[pl.BlockSpec((None, tm, tn), lambda i, jj: (jj // n_t, i, jj % n_t))],
        out_shape=[jax.ShapeDtypeStruct((j, count * tm, n), out_dtype)], scratch=[], args=[a, dy], plan=plan)[0]


def _rows(body, ins, outs, *, n_rows, tm, name, plan=None):
    tm = min(tm, n_rows)
    n_steps = n_rows // tm
    in_specs, args = [], []
    for arr, kind, width, block in ins:
        if kind == "row":
            in_specs.append(pl.BlockSpec((tm, width), functools.partial(lambda i, b: (i, b), b=block)))
        elif kind == "prev":
            in_specs.append(pl.BlockSpec((tm, width), functools.partial(lambda i, b: (jnp.maximum(i - 1, 0), b), b=block)))
        elif kind == "next":
            in_specs.append(pl.BlockSpec((tm, width), functools.partial(lambda i, b: (jnp.minimum(i + 1, n_steps - 1), b), b=block)))
        else:
            in_specs.append(pl.BlockSpec(arr.shape, functools.partial(lambda i, nd: (0,) * nd, nd=arr.ndim)))
        args.append(arr)
    out_specs, out_shape = [], []
    for shape, dtype, kind in outs:
        if kind == "row":
            out_specs.append(pl.BlockSpec((tm, shape[1]), lambda i: (i, 0)))
        else:
            out_specs.append(pl.BlockSpec(shape, functools.partial(lambda i, nd: (0,) * nd, nd=len(shape))))
        out_shape.append(jax.ShapeDtypeStruct(shape, dtype))

    def kern(*refs):
        body(pl.program_id(0), n_steps, *refs)

    return _call(kern, name=name, grid=(n_steps,), in_specs=in_specs, out_specs=out_specs, out_shape=out_shape,
                 scratch=[], args=args, plan=plan)


def _acc_rows(i, ref, value):
    @pl.when(i == 0)
    def _():
        ref[...] = jnp.zeros_like(ref)
    ref[...] += jnp.broadcast_to(value, ref.shape)


def _rms_fwd(x, g, *, name, tm=512):
    s, d = x.shape

    def body(i, n, x_ref, g_ref, h_ref):
        _rms_fwd_tail(x_ref[...], g_ref, h_ref)

    return _rows(body, [(x, "row", d, 0), (g, "full", 0, 0)], [((s, d), BF16, "row")], n_rows=s, tm=tm, name=name)[0]


def _rms_bwd(x, g, dh, dres, *, name, tm=512, plan=None):
    s, d = x.shape

    def body(i, n, x_ref, g_ref, dh_ref, dres_ref, dx_ref, dxb_ref, dg_ref):
        _rms_bwd_tail(i, dh_ref[...].astype(F32), x_ref, g_ref, dres_ref, dx_ref, dxb_ref, dg_ref)

    return _rows(body, [(x, "row", d, 0), (g, "full", 0, 0), (dh, "row", d, 0), (dres, "row", d, 0)],
                 [((s, d), F32, "row"), ((s, d), BF16, "row"), ((8, d), F32, "acc")],
                 n_rows=s, tm=tm, name=name, plan=plan)


def _mix_out_bwd(dx1b, w, br_a, br_b, proj, *, name, tm=512, plan=None):
    s, d = br_a.shape
    tm = min(tm, s)

    def body(dy_ref, w_ref, a_ref, b_ref, ga_ref, gb_ref, da_ref, db_ref, dg_ref):
        dm = _bdot(dy_ref[...], w_ref[...], NT)
        sa = _sigmoid(ga_ref[...].astype(F32))
        sb = _sigmoid(gb_ref[...].astype(F32))
        da_ref[...] = (dm * sa).astype(BF16)
        db_ref[...] = (dm * sb).astype(BF16)
        dg_ref[:, :d] = (dm * a_ref[...].astype(F32) * sa * (1.0 - sa)).astype(BF16)
        dg_ref[:, d:] = (dm * b_ref[...].astype(F32) * sb * (1.0 - sb)).astype(BF16)

    row = pl.BlockSpec((tm, d), lambda i: (i, 0))
    return _call(body, name=name, grid=(s // tm,),
                 in_specs=[row, pl.BlockSpec(w.shape, lambda i: (0, 0)), row, row,
                           pl.BlockSpec((tm, d), lambda i: (i, 3)), pl.BlockSpec((tm, d), lambda i: (i, 4))],
                 out_specs=[row, row, pl.BlockSpec((tm, 2 * d), lambda i: (i, 0))],
                 out_shape=[jax.ShapeDtypeStruct((s, d), BF16), jax.ShapeDtypeStruct((s, d), BF16),
                            jax.ShapeDtypeStruct((s, 2 * d), BF16)],
                 scratch=[], args=[dx1b, w, br_a, br_b, proj, proj], plan=plan)


def _shift_down(cur, prev, k, first):
    row = lax.broadcasted_iota(jnp.int32, cur.shape, 0)
    out = jnp.where(row >= k, pltpu.roll(cur, k, 0), pltpu.roll(prev, k, 0))
    return jnp.where(jnp.logical_and(first, row < k), 0.0, out)


def _shift_up(cur, nxt, k, last):
    tm = cur.shape[0]
    row = lax.broadcasted_iota(jnp.int32, cur.shape, 0)
    out = jnp.where(row < tm - k, pltpu.roll(cur, tm - k, 0), pltpu.roll(nxt, tm - k, 0))
    return jnp.where(jnp.logical_and(last, row >= tm - k), 0.0, out)


def _conv_fwd(proj, conv_w, *, name, tm=512):
    s = proj.shape[0]
    c = CONV_WIDTH

    def body(i, n, u_ref, gb_ref, gc_ref, up_ref, gcp_ref, w_ref, y_ref):
        cu = gc_ref[...].astype(F32) * u_ref[...].astype(F32)
        cup = gcp_ref[...].astype(F32) * up_ref[...].astype(F32)
        first = i == 0
        y = (w_ref[0:1, :] * _shift_down(cu, cup, 2, first) + w_ref[1:2, :] * _shift_down(cu, cup, 1, first)
             + w_ref[2:3, :] * cu)
        y_ref[...] = (gb_ref[...].astype(F32) * y).astype(BF16)

    return _rows(body, [(proj, "row", c, 3), (proj, "row", c, 4), (proj, "row", c, 5),
                        (proj, "prev", c, 3), (proj, "prev", c, 5), (conv_w, "full", 0, 0)],
                 [((s, c), BF16, "row")], n_rows=s, tm=tm, name=name)[0]


def _conv_bwd(dy_b, proj, conv_w, *, name, tm=512, plan=None):
    s = proj.shape[0]
    c = CONV_WIDTH

    def body(i, n, dy_ref, u_ref, gb_ref, gc_ref, up_ref, gcp_ref, dyn_ref, gbn_ref, w_ref, d_ref, dw_ref):
        first, last = i == 0, i == n - 1
        u = u_ref[...].astype(F32)
        gb = gb_ref[...].astype(F32)
        gc = gc_ref[...].astype(F32)
        cu = gc * u
        cup = gcp_ref[...].astype(F32) * up_ref[...].astype(F32)
        cu1 = _shift_down(cu, cup, 1, first)
        cu2 = _shift_down(cu, cup, 2, first)
        conv = w_ref[0:1, :] * cu2 + w_ref[1:2, :] * cu1 + w_ref[2:3, :] * cu
        dy = dy_ref[...].astype(F32)
        dyc = dy * gb
        dycn = dyn_ref[...].astype(F32) * gbn_ref[...].astype(F32)
        dcu = (w_ref[2:3, :] * dyc + w_ref[1:2, :] * _shift_up(dyc, dycn, 1, last)
               + w_ref[0:1, :] * _shift_up(dyc, dycn, 2, last))
        d_ref[:, 0:c] = (dcu * gc).astype(BF16)
        d_ref[:, c:2 * c] = (dy * conv).astype(BF16)
        d_ref[:, 2 * c:3 * c] = (dcu * u).astype(BF16)
        row = lax.broadcasted_iota(jnp.int32, (8, c), 0)
        dw = (jnp.where(row == 0, jnp.sum(dyc * cu2, axis=0, keepdims=True), 0.0)
              + jnp.where(row == 1, jnp.sum(dyc * cu1, axis=0, keepdims=True), 0.0)
              + jnp.where(row == 2, jnp.sum(dyc * cu, axis=0, keepdims=True), 0.0))

        @pl.when(first)
        def _():
            dw_ref[...] = jnp.zeros_like(dw_ref)
        dw_ref[...] += dw

    return _rows(body, [(dy_b, "row", c, 0), (proj, "row", c, 3), (proj, "row", c, 4), (proj, "row", c, 5),
                        (proj, "prev", c, 3), (proj, "prev", c, 5), (dy_b, "next", c, 0), (proj, "next", c, 4),
                        (conv_w, "full", 0, 0)],
                 [((s, 3 * c), BF16, "row"), ((8, c), F32, "acc")], n_rows=s, tm=tm, name=name, plan=plan)


def _mem_probs(q, k, scale):
    sc = _bdot(q, k, NT) * scale
    sc = sc - jnp.max(sc, axis=-1, keepdims=True)
    p = jnp.exp(sc)
    return p / jnp.sum(p, axis=-1, keepdims=True)


def _mem_sublayer(hq, w_q, kv, w_o, x, g, *, name, tm=512, plan=None):
    s, d = hq.shape
    hd = d // MEM_HEADS
    scale = 1.0 / math.sqrt(hd)
    tm = min(tm, s)

    def body(hq_ref, wq_ref, kv_ref, wo_ref, x_ref, g_ref, q_ref, o_ref, x2_ref, h_ref):
        q_ref[...] = _bdot(hq_ref[...], wq_ref[...], NN).astype(BF16)
        for h in range(MEM_HEADS):
            cols = slice(h * hd, (h + 1) * hd)
            p = _mem_probs(q_ref[:, cols], kv_ref[:, cols], scale)
            o_ref[:, cols] = _bdot(p, kv_ref[:, d + h * hd:d + (h + 1) * hd], NN).astype(BF16)
        xv = _bdot(o_ref[...], wo_ref[...], NN) + x_ref[...]
        x2_ref[...] = xv
        _rms_fwd_tail(xv, g_ref, h_ref)

    row = pl.BlockSpec((tm, d), lambda i: (i, 0))
    whole = lambda a: pl.BlockSpec(a.shape, lambda i: (0,) * a.ndim)
    return _call(body, name=name, grid=(s // tm,),
                 in_specs=[row, whole(w_q), whole(kv), whole(w_o), row, whole(g)], out_specs=[row] * 4,
                 out_shape=[jax.ShapeDtypeStruct((s, d), BF16), jax.ShapeDtypeStruct((s, d), BF16),
                            jax.ShapeDtypeStruct((s, d), F32), jax.ShapeDtypeStruct((s, d), BF16)],
                 scratch=[], args=[hq, w_q, kv, w_o, x, g], plan=plan)


def _mem_sublayer_bwd(dx2b, dx2, x, g, qm, kv, w_q, w_o, *, name, tm=512, plan=None):
    s, d = qm.shape
    hd = d // MEM_HEADS
    scale = 1.0 / math.sqrt(hd)
    tm = min(tm, s)

    def body(dyb_ref, dres_ref, x_ref, g_ref, q_ref, kv_ref, wq_ref, wo_ref, dx_ref, dxb_ref, dg_ref, dq_ref, dkv_ref):
        i = pl.program_id(0)

        @pl.when(i == 0)
        def _():
            dkv_ref[...] = jnp.zeros_like(dkv_ref)
        dom = _bdot(dyb_ref[...], wo_ref[...], NT).astype(BF16)
        for h in range(MEM_HEADS):
            cols = slice(h * hd, (h + 1) * hd)
            vcols = slice(d + h * hd, d + (h + 1) * hd)
            q, k, v, do = q_ref[:, cols], kv_ref[:, cols], kv_ref[:, vcols], dom[:, cols]
            p = _mem_probs(q, k, scale)
            dp = _bdot(do, v, NT)
            ds = p * (dp - jnp.sum(dp * p, axis=-1, keepdims=True)) * scale
            dq_ref[:, cols] = _bdot(ds, k, NN).astype(BF16)
            dkv_ref[:, cols] += _bdot(ds, q, TN)
            dkv_ref[:, vcols] += _bdot(p, do, TN)
        dh = _bdot(dq_ref[...], wq_ref[...], NT)
        _rms_bwd_tail(i, dh, x_ref, g_ref, dres_ref, dx_ref, dxb_ref, dg_ref)

    row = pl.BlockSpec((tm, d), lambda i: (i, 0))
    whole = lambda a: pl.BlockSpec(a.shape, lambda i: (0,) * a.ndim)
    return _call(body, name=name, grid=(s // tm,),
                 in_specs=[row, row, row, whole(g), row, whole(kv), whole(w_q), whole(w_o)],
                 out_specs=[row, row, pl.BlockSpec((8, d), lambda i: (0, 0)), row, whole(kv)],
                 out_shape=[jax.ShapeDtypeStruct((s, d), F32), jax.ShapeDtypeStruct((s, d), BF16),
                            jax.ShapeDtypeStruct((8, d), F32), jax.ShapeDtypeStruct((s, d), BF16),
                            jax.ShapeDtypeStruct(kv.shape, F32)],
                 scratch=[], args=[dx2b, dx2, x, g, qm, kv, w_q, w_o], plan=plan)


def _sb_consts(t):
    row = lax.broadcasted_iota(jnp.int32, (t, t), 0)
    col = lax.broadcasted_iota(jnp.int32, (t, t), 1)
    lane = lax.broadcasted_iota(jnp.int32, (t, LANES), 1)
    return row, col, lane < SB_HEAD_DIM


def _sb_logits(q, k):
    z2 = jnp.minimum(_bdot(q, k, NT) * LOG2_E, SB_CLAMP)
    return z2, jnp.exp2(z2)


def _tri_sum(v, tri):
    hi = v.astype(BF16)
    lo = (v - hi.astype(F32)).astype(BF16)
    return _bdot(hi, tri, NN) + _bdot(lo, tri, NN)


def _sb_fwd(proj, *, name, plan=None):
    s = proj.shape[0]
    t = SB_TILE
    n_q = s // t
    scale = 1.0 / math.sqrt(SB_HEAD_DIM)
    k_blk, v_blk = SB_WIDTH // LANES, 2 * SB_WIDTH // LANES

    def body(q_ref, k_ref, v_ref, o_ref, c_ref, first_ref, acc_ref, c_scr):
        i = pl.program_id(1)
        row, col, head0 = _sb_consts(t)
        later = (row > col).astype(BF16)
        valid = col < row
        qs = q_ref[...] * scale
        q2 = (jnp.where(head0, qs, 0), jnp.where(head0, 0, qs))

        def tiles(kbs, diag_first, carry):
            kt = [k_ref[pl.ds(pl.multiple_of(kb * t, t), t), :] for kb in kbs]
            vt = [v_ref[pl.ds(pl.multiple_of(kb * t, t), t), :] for kb in kbs]
            jobs = [(n, h) for n in range(len(kbs)) for h in range(2)]
            masked = lambda n: diag_first and n == 0
            zs = {(n, h): _sb_logits(q2[h], kt[n]) for n, h in jobs}
            fail = {j: jnp.log2(1.0 + zs[j][1]) for j in jobs}
            fail = {j: jnp.where(valid, fail[j], 0.0) if masked(j[0]) else fail[j] for j in jobs}
            cum = {j: _tri_sum(fail[j], later) for j in jobs}
            run, before = list(carry), {}
            for n, h in jobs:
                before[n, h] = run[h]
                run[h] = run[h] + cum[n, h][:, 0:1] + fail[n, h][:, 0:1]
            w = {j: jnp.exp2(zs[j][0] - fail[j] - cum[j] - before[j]) for j in jobs}
            w = {j: jnp.where(valid, w[j], 0.0) if masked(j[0]) else w[j] for j in jobs}
            for n, h in jobs:
                acc_ref[h] += _bdot(w[n, h], vt[n], NN)
            return tuple(run)

        acc_ref[...] = jnp.zeros_like(acc_ref)
        zero = jnp.zeros((t, 1), F32)

        def alive(carry):
            return (jnp.minimum(jnp.min(carry[0]), jnp.min(carry[1])) < SB_DEAD).astype(jnp.int32)

        def step(state):
            kb, _, c0, c1 = state
            new = tiles([kb], False, (c0, c1))
            return kb - 1, alive(new), new[0], new[1]

        @pl.when(i == 0)
        def _():
            c_scr[0], c_scr[1] = tiles([i], True, (zero, zero))

        @pl.when(i > 0)
        def _():
            c_scr[0], c_scr[1] = tiles([i, i - 1], True, (zero, zero))
        carry = (c_scr[0], c_scr[1])
        kb, _, c0, c1 = lax.while_loop(lambda st: jnp.logical_and(st[0] >= 0, st[1] > 0), step,
                                       (i - 2, alive(carry), carry[0], carry[1]))
        kb = jnp.maximum(kb, -1)
        o_ref[...] = jnp.where(head0, acc_ref[0], acc_ref[1]).astype(BF16)
        c_ref[...] = jnp.where(lax.broadcasted_iota(jnp.int32, (t, 2), 1) == 0, c0, c1)
        first_ref[pl.program_id(0), i] = (kb + 1).astype(F32)

    return _call(
        body, name=name, grid=(SB_HEADS // 2, n_q),
        in_specs=[pl.BlockSpec((t, LANES), lambda p, i: (i, p)),
                  pl.BlockSpec((s, LANES), lambda p, i: (0, k_blk + p)),
                  pl.BlockSpec((s, LANES), lambda p, i: (0, v_blk + p))],
        out_specs=[pl.BlockSpec((t, LANES), lambda p, i: (i, p)),
                   pl.BlockSpec((None, t, 2), lambda p, i: (p, i, 0)),
                   pl.BlockSpec(memory_space=pltpu.SMEM)],
        out_shape=[jax.ShapeDtypeStruct((s, SB_WIDTH), BF16), jax.ShapeDtypeStruct((SB_HEADS // 2, s, 2), F32),
                   jax.ShapeDtypeStruct((SB_HEADS // 2, n_q), F32)],
        scratch=[pltpu.VMEM((2, t, LANES), F32), pltpu.VMEM((2, t, 1), F32)], args=[proj, proj, proj], plan=plan)


def _sb_bwd(proj, do_a, ctot, first, *, name, plan=None):
    s = proj.shape[0]
    t = SB_TILE
    n_q = s // t
    scale = 1.0 / math.sqrt(SB_HEAD_DIM)
    k_blk, v_blk = SB_WIDTH // LANES, 2 * SB_WIDTH // LANES

    def body(q_ref, k_ref, v_ref, do_ref, c_ref, first_ref, dq_ref, dk_ref, dv_ref, dq_acc, dk_acc, dv_acc):
        i = pl.program_id(1)
        kb0 = jnp.clip(first_ref[pl.program_id(0), i].astype(jnp.int32), 0, i)
        row, col, head0 = _sb_consts(t)
        upto = (row <= col).astype(BF16)
        before = (row < col).astype(BF16)
        valid = col < row
        qs = q_ref[...] * scale
        q2 = (jnp.where(head0, qs, 0), jnp.where(head0, 0, qs))
        do = do_ref[...]
        do2 = (jnp.where(head0, do, 0), jnp.where(head0, 0, do))
        ctot2 = (c_ref[:, 0:1], c_ref[:, 1:2])

        @pl.when(i == 0)
        def _():
            dk_acc[...] = jnp.zeros_like(dk_acc)
            dv_acc[...] = jnp.zeros_like(dv_acc)
        dq_acc[...] = jnp.zeros_like(dq_acc)

        def tiles(kbs, diag_last, carry):
            rows = [pl.ds(pl.multiple_of(kb * t, t), t) for kb in kbs]
            kt = [k_ref[r, :] for r in rows]
            vt = [v_ref[r, :] for r in rows]
            jobs = [(n, h) for n in range(len(kbs)) for h in range(2)]
            masked = lambda n: diag_last and n == len(kbs) - 1
            t_last = slice(t - 1, t)
            zs = {(n, h): _sb_logits(q2[h], kt[n]) for n, h in jobs}
            dw = {(n, h): _bdot(do2[h], vt[n], NT) for n, h in jobs}
            fail = {j: jnp.log2(1.0 + zs[j][1]) for j in jobs}
            fail = {j: jnp.where(valid, fail[j], 0.0) if masked(j[0]) else fail[j] for j in jobs}
            cum = {j: _tri_sum(fail[j], upto) for j in jobs}
            miss = {j: jnp.exp2(-fail[j]) for j in jobs}
            beta = {j: zs[j][1] * miss[j] for j in jobs}
            fail_run, fail_before = list(carry[0::2]), {}
            for n, h in jobs:
                fail_before[n, h] = fail_run[h]
                fail_run[h] = fail_run[h] + cum[n, h][:, t_last]
            w = {(n, h): beta[n, h] * jnp.exp2(fail_before[n, h] + cum[n, h] - ctot2[h]) for n, h in jobs}
            w = {j: jnp.where(valid, w[j], 0.0) if masked(j[0]) else w[j] for j in jobs}
            g = {j: w[j] * dw[j] for j in jobs}
            g_local = {j: _bdot(g[j], before, NN) for j in jobs}
            for n, h in jobs:
                dv_acc[rows[n], :] += _bdot(w[n, h], do2[h], TN)
            g_run, dz = list(carry[1::2]), {}
            for n, h in jobs:
                g_sum = g_run[h] + g_local[n, h]
                dz[n, h] = g[n, h] * miss[n, h] - beta[n, h] * g_sum
                g_run[h] = g_sum[:, t_last] + g[n, h][:, t_last]
            dz = {j: jnp.where(valid, dz[j], 0.0) if masked(j[0]) else dz[j] for j in jobs}
            for n, h in jobs:
                dq_acc[h] += _bdot(dz[n, h], kt[n], NN)
                dk_acc[rows[n], :] += _bdot(dz[n, h], q2[h], TN)
            return fail_run[0], g_run[0], fail_run[1], g_run[1]

        zero = jnp.zeros((t, 1), F32)
        carry = lax.fori_loop(kb0, i - 1, lambda n, c: tiles([n], False, c), (zero,) * 4)

        @pl.when(i == 0)
        def _():
            tiles([i], True, carry)

        @pl.when(i > 0)
        def _():
            tiles([i - 1, i], True, carry)
        dq_ref[...] = (jnp.where(head0, dq_acc[0], dq_acc[1]) * scale).astype(BF16)

        @pl.when(i == n_q - 1)
        def _():
            dk_ref[...] = dk_acc[...].astype(BF16)
            dv_ref[...] = dv_acc[...].astype(BF16)

    outs = _call(
        body, name=name, grid=(SB_HEADS // 2, n_q),
        in_specs=[pl.BlockSpec((t, LANES), lambda p, i: (i, p)),
                  pl.BlockSpec((s, LANES), lambda p, i: (0, k_blk + p)),
                  pl.BlockSpec((s, LANES), lambda p, i: (0, v_blk + p)),
                  pl.BlockSpec((t, LANES), lambda p, i: (i, p)),
                  pl.BlockSpec((None, t, 2), lambda p, i: (p, i, 0)),
                  pl.BlockSpec(memory_space=pltpu.SMEM)],
        out_specs=[pl.BlockSpec((t, LANES), lambda p, i: (i, p)),
                   pl.BlockSpec((s, LANES), lambda p, i: (0, p)),
                   pl.BlockSpec((s, LANES), lambda p, i: (0, p))],
        out_shape=[jax.ShapeDtypeStruct((s, SB_WIDTH), BF16)] * 3,
        scratch=[pltpu.VMEM((2, t, LANES), F32), pltpu.VMEM((s, LANES), F32), pltpu.VMEM((s, LANES), F32)],
        args=[proj, proj, proj, do_a, ctot, first], plan=plan)
    return jnp.concatenate(outs, axis=1)


def _mm_gathered(a, key, plan, *, name, out3=False, w_t=False):
    src = plan.gathering(key)
    if src is None:
        return _mm_nn(a, plan.weight(key), name=name, out3=out3, w_t=w_t)
    out, w_all = _mm_gathering(a, src, name=name, out3=out3, w_t=w_t)
    plan.set_weight(key, w_all)
    return out


def _local_step(x, mem, target, gains, plan):
    g_mix, g_memq, g_memkv, g_ffn, g_fin = gains
    d = x.shape[1]

    h0 = _rms_fwd(x, g_mix, name="rms_mix")
    proj = _mm_gathered(h0, "in", plan, name="mm_in")
    w_in = plan.weight("in")
    o_a, ctot, first = _sb_fwd(proj, name="sb_fwd", plan=plan)
    conv_w = plan.weight("conv")
    y_b = _conv_fwd(proj, conv_w, name="conv_fwd")
    w_a, w_b, w_mix = plan.weight("a"), plan.weight("b"), plan.weight("mix")
    br_a = _mm_nn(o_a, w_a, name="mm_branch_a")
    br_b = _mm_nn(y_b, w_b, name="mm_branch_b")
    x1, hq, merged = _mix_out(br_a, br_b, proj, w_mix[0], x, g_memq, name="mm_mix", plan=plan)
    w_mq, w_kv, w_mo = plan.weight("mq")[0], plan.weight("kv"), plan.weight("mo")[0]
    mn = _rms_fwd(mem, g_memkv, name="rms_memkv")
    kv = _mm_nn(mn, w_kv, name="mm_memkv")
    qm, om, x2, hf = _mem_sublayer(hq, w_mq, kv, w_mo, x1, g_ffn, name="mem_sublayer", plan=plan)
    gu = _mm_gathered(hf, "fi", plan, name="mm_ffn_in", out3=True, w_t=True)
    w_fi, w_fo = plan.weight("fi"), plan.weight("fo")
    dx3, dx3b, dg_fin, loss, act = _ffn_out_loss(gu, w_fo, x2, g_fin, target, name="mm_ffn_out")

    plan.grad("fo", _mm_tn_a3(act, dx3b, name="mm_d_w_ffn_out"))
    dgu = _ffn_out_bwd(dx3b, w_fo, gu, name="mm_d_act")
    plan.grad("fi", _mm_tn_a3(dgu, hf, name="mm_d_w_ffn_in"))
    dx2, dx2b, dg_ffn = _mm_nt_rms(dgu, w_fi, x2, g_ffn, dx3, name="mm_d_hf", dy3=True, w_nn=True, plan=plan)

    plan.grad("mo", _mm_tn(om, dx2b, d, name="mm_d_w_memo"))
    dx1, dx1b, dg_memq, dqm, dkv = _mem_sublayer_bwd(dx2b, dx2, x1, g_memq, qm, kv, w_mq, w_mo, name="mem_sublayer_bwd",
                                                    plan=plan)
    plan.grad("mq", _mm_tn(hq, dqm, d, name="mm_d_w_memq"))
    plan.grad("kv", _mm_tn(mn, dkv, w_kv.shape[2], name="mm_d_w_memkv"))
    _, _, dg_memkv = _mm_nt_rms(dkv, w_kv, mem, g_memkv, None, name="mm_d_mn")

    plan.grad("mix", _mm_tn(merged, dx1b, d, name="mm_d_w_mix"))
    dbr_a, dbr_b, dgab = _mix_out_bwd(dx1b, w_mix[0], br_a, br_b, proj, name="mm_d_merged", plan=plan)
    plan.grad("a", _mm_tn(o_a, dbr_a, d, name="mm_d_w_branch_a"))
    do_a = _mm_nt(dbr_a, w_a, name="mm_d_o_a")
    plan.grad("b", _mm_tn(y_b, dbr_b, d, name="mm_d_w_branch_b"))
    dy_b = _mm_nt(dbr_b, w_b, name="mm_d_y_b")
    dconv, dconv_w = _conv_bwd(dy_b, proj, conv_w, name="conv_bwd", plan=plan)
    dqkv = _sb_bwd(proj, do_a, ctot, first, name="sb_bwd", plan=plan)
    dproj = jnp.concatenate([dqkv, dconv, dgab], axis=1)
    rows_in1 = d // IN_SPLIT[1] * (IN_SPLIT[1] - IN_SPLIT[0])
    plan.grad("in0", _mm_tn(h0, dproj, w_in.shape[2], name="mm_d_w_in0", tm=d - rows_in1, k_tiles=(0, 1)))
    plan.grad("in1", _mm_tn(h0, dproj, w_in.shape[2], name="mm_d_w_in1", tm=rows_in1,
                            k_tiles=(d // rows_in1 - 1, 1), plan=plan))
    dh0 = _mm_nt(dproj, w_in, name="mm_d_h0", out_dtype=F32, plan=plan)
    dx0, _, dg_mix = _rms_bwd(x, g_mix, dh0, dx1, name="rms_mix_bwd", plan=plan)

    return dx0, (dg_mix, dg_memq, dg_memkv, dg_ffn, dg_fin, dconv_w, loss)


def _row_tile(a, target=512):
    tm = min(a, target)
    while a % tm:
        tm -= 8
    return tm


def _sum_with_sibling(parts, recvs, core, *, name):
    n = len(parts)

    def body(core_ref, *refs):
        for p_ref, r_ref, o_ref in zip(refs[:n], refs[n:2 * n], refs[2 * n:]):
            o_ref[...] = (p_ref[...].astype(F32) + r_ref[...].astype(F32)).astype(o_ref.dtype)

    mine = [pl.BlockSpec((None,) + p.shape[1:], lambda q, core_ref: (2 * q + core_ref[0], 0, 0)) for p in parts]
    other = [pl.BlockSpec((None,) + p.shape[1:], lambda q, core_ref: (q, 0, 0)) for p in parts]
    return pl.pallas_call(
        body, name=name,
        grid_spec=pltpu.PrefetchScalarGridSpec(num_scalar_prefetch=1, grid=(N_CHIP,), in_specs=mine + other,
                                               out_specs=other),
        out_shape=[jax.ShapeDtypeStruct((N_CHIP,) + p.shape[1:], p.dtype) for p in parts],
        compiler_params=_params(1))(core, *parts, *recvs)


def _adam_math(wv, g, m, v):
    m = ADAM_B1 * m + (1.0 - ADAM_B1) * g
    v = ADAM_B2 * v + (1.0 - ADAM_B2) * (g * g)
    m_hat = m / (1.0 - ADAM_B1 ** ADAM_STEP)
    v_hat = v / (1.0 - ADAM_B2 ** ADAM_STEP)
    delta = -ADAM_LR * (m_hat / (jnp.sqrt(v_hat) + ADAM_EPS) + ADAM_WD * wv)
    return delta, m, v


def _adam_sharded(wv, m, v, own, recv, chip, *, name):
    a, b = wv.shape
    tm = _row_tile(a)

    def body(chip_ref, w_ref, m_ref, v_ref, own_ref, recv_ref, g_ref, d_ref, nm_ref, nv_ref):
        g = own_ref[...].astype(F32)
        for j in range(3):
            g = g + recv_ref[j].astype(F32)
        delta, nm, nv = _adam_math(w_ref[...], g, m_ref[...], v_ref[...])
        g_ref[...] = g
        d_ref[...] = delta
        nm_ref[...] = nm
        nv_ref[...] = nv

    tile = pl.BlockSpec((tm, b), lambda i, chip_ref: (i, 0))
    return pl.pallas_call(
        body, name=name,
        grid_spec=pltpu.PrefetchScalarGridSpec(
            num_scalar_prefetch=1, grid=(a // tm,),
            in_specs=[tile, tile, tile,
                      pl.BlockSpec((None, tm, b), lambda i, chip_ref: (chip_ref[0], i, 0)),
                      pl.BlockSpec((3, tm, b), lambda i, chip_ref: (0, i, 0))],
            out_specs=[tile] * 4),
        out_shape=[jax.ShapeDtypeStruct((a, b), F32)] * 4, compiler_params=_params(1))(chip, wv, m, v, own, recv)


def _sum_devices(gathered, *, name):
    _, r, c = gathered.shape

    def body(g_ref, o_ref):
        total = g_ref[0]
        for j in range(1, N_DEV):
            total = total + g_ref[j]
        o_ref[...] = total

    return pl.pallas_call(body, name=name, out_shape=jax.ShapeDtypeStruct((r, c), F32))(gathered)


def _adam_small(wv, g, m, v, *, name):
    def body(w_ref, g_ref, m_ref, v_ref, d_ref, nm_ref, nv_ref):
        delta, nm, nv = _adam_math(w_ref[...], g_ref[...], m_ref[...], v_ref[...])
        d_ref[...] = delta
        nm_ref[...] = nm
        nv_ref[...] = nv

    return pl.pallas_call(body, name=name, out_shape=[jax.ShapeDtypeStruct(wv.shape, F32)] * 3)(wv, g, m, v)


BIG = ("in", "a", "b", "mix", "mq", "kv", "mo", "fi", "fo")
ROW_SHARDED = ("mix", "mq", "mo")
UNSHARDED = ("a", "b")
FFN_GROUPS = 4
IN_SPLIT = (3, 4)
SMALL_ROWS = 16


class _Plan:
    FUSED = ("in",)
    GATHER_ON = {"sb_fwd": ("a", "b", "mix", "kv", "mq", "mo", "conv", "fi0"), "mm_mix": ("fo",),
                 "mem_sublayer": ("fi1",)}
    SIBLING_ON = {"mm_d_hf": ("fo", "fi"), "mm_d_merged": ("mo", "mq", "kv"), "conv_bwd": ("mix", "a", "b"),
                  "mm_d_w_in1": ("in0",), "mm_d_h0": ("in1",)}
    CHIPS_ON = {"mem_sublayer_bwd": ("fo",), "sb_bwd": ("fi", "mo", "mq", "kv", "mix", "a", "b"), "mm_d_h0": ("in0",),
                "rms_mix_bwd": ("in1",)}

    def __init__(self, shards, core):
        self.shards, self.core = shards, core
        self.w, self.parts, self.chip_sums, self.from_chips = {}, {}, {}, {}

    def gathering(self, k):
        return self.shards[k] if k in self.FUSED else None

    def comm(self, name):
        comms = []
        if name in self.GATHER_ON:
            comms.append(_gather_comm([self.shards[k] for k in self.GATHER_ON[name]]))
        if name in self.SIBLING_ON:
            comms.append(_sibling_comm([self.parts[k] for k in self.SIBLING_ON[name]]))
        if name in self.CHIPS_ON:
            comms.append(_chips_comm([self.chip_sums[k] for k in self.CHIPS_ON[name]]))
        return _join_comms(comms) if comms else None

    def landed(self, name, outs):
        outs = list(outs)
        for k in self.GATHER_ON.get(name, ()):
            self.set_weight(k, outs.pop(0))
        keys = self.SIBLING_ON.get(name, ())
        if keys:
            sums = _sum_with_sibling([self.parts[k] for k in keys], [outs.pop(0) for _ in keys], self.core,
                                     name="sum_with_sibling_" + "_".join(keys))
            self.chip_sums.update(zip(keys, sums))
        for k in self.CHIPS_ON.get(name, ()):
            self.from_chips[k] = outs.pop(0)

    def set_weight(self, k, gathered):
        _, a, b = gathered.shape
        if k in ROW_SHARDED:
            gathered = gathered.reshape(1, N_DEV * a, b)
        elif k in UNSHARDED:
            gathered = jnp.transpose(gathered, (1, 0, 2)).reshape(1, a, N_DEV * b)
        elif k == "fo":
            gathered = gathered.reshape(FFN_GROUPS, N_DEV * a // FFN_GROUPS, b)
        elif k == "conv":
            n_conv = CONV_WIDTH // N_DEV
            gathered = jnp.transpose(gathered[:, :3, :n_conv], (1, 0, 2)).reshape(3, CONV_WIDTH)
        self.w[k] = gathered
        if k == "fi1":
            self.w["fi"] = jnp.concatenate([self.w["fi0"], gathered], axis=2)

    def weight(self, k):
        return self.w[k]

    def grad(self, k, g):
        _, a, b = g.shape
        if k in ROW_SHARDED:
            g = g.reshape(N_DEV, a // N_DEV, b)
        elif k in UNSHARDED:
            g = jnp.transpose(g.reshape(a, N_DEV, b // N_DEV), (1, 0, 2))
        elif k == "fo":
            g = g.reshape(N_DEV, FFN_GROUPS * a // N_DEV, b)
        self.parts[k] = g


def kernel(x, mem, norm_mix, w_in, conv_w, w_branch_a, w_branch_b, w_mix_out, norm_mem_q, norm_mem_kv, w_mem_q, w_mem_kv, w_mem_o, norm_ffn, w_ffn_in, w_ffn_out, norm_final, loss_target, m_norm_mix, m_w_in, m_conv_w, m_w_branch_a, m_w_branch_b, m_w_mix_out, m_norm_mem_q, m_norm_mem_kv, m_w_mem_q, m_w_mem_kv, m_w_mem_o, m_norm_ffn, m_w_ffn_in, m_w_ffn_out, m_norm_final, v_norm_mix, v_w_in, v_conv_w, v_w_branch_a, v_w_branch_b, v_w_mix_out, v_norm_mem_q, v_norm_mem_kv, v_w_mem_q, v_w_mem_kv, v_w_mem_o, v_norm_ffn, v_w_ffn_in, v_w_ffn_out, v_norm_final):
    d = x.shape[-1]
    xi, yi, ci = lax.axis_index("x"), lax.axis_index("y"), lax.axis_index("c")
    chip = jnp.reshape(2 * xi + yi, (1,)).astype(jnp.int32)
    dev = 4 * xi + 2 * yi + ci

    big_w = dict(zip(BIG, (w_in, w_branch_a, w_branch_b, w_mix_out, w_mem_q, w_mem_kv, w_mem_o, w_ffn_in, w_ffn_out)))
    big_m = dict(zip(BIG, (m_w_in, m_w_branch_a, m_w_branch_b, m_w_mix_out, m_w_mem_q, m_w_mem_kv, m_w_mem_o, m_w_ffn_in, m_w_ffn_out)))
    big_v = dict(zip(BIG, (v_w_in, v_w_branch_a, v_w_branch_b, v_w_mix_out, v_w_mem_q, v_w_mem_kv, v_w_mem_o, v_w_ffn_in, v_w_ffn_out)))

    flip = lambda t, k: jnp.transpose(t) if k == "fi" else t
    shards = {k: flip(big_w[k][0], k).astype(BF16) for k in BIG}
    shards["fi0"], shards["fi1"] = shards["fi"][:, :d // 2], shards["fi"][:, d // 2:]
    n_conv = conv_w.shape[-1]
    shards["conv"] = jnp.zeros((8, LANES), F32).at[:3, :n_conv].set(conv_w[0])
    plan = _Plan(shards, jnp.reshape(ci, (1,)).astype(jnp.int32))

    gains = (norm_mix, norm_mem_q, norm_mem_kv, norm_ffn, norm_final.reshape(1, d))
    dx0, small = _local_step(x[0], mem[0], loss_target[0], gains, plan)

    grads, deltas, new_m, new_v = {}, {}, {}, {}
    for k in BIG:
        lead = big_w[k].shape
        wv, mv, vv = flip(big_w[k][0], k), flip(big_m[k][0], k), flip(big_v[k][0], k)
        if k == "in":
            half = wv.shape[0] * IN_SPLIT[0] // IN_SPLIT[1]
            lo = _adam_sharded(wv[:half], mv[:half], vv[:half], plan.chip_sums["in0"], plan.from_chips["in0"], chip,
                               name="adam_in0")
            hi = _adam_sharded(wv[half:], mv[half:], vv[half:], plan.chip_sums["in1"], plan.from_chips["in1"], chip,
                               name="adam_in1")
            outs = [jnp.concatenate(pair, axis=0) for pair in zip(lo, hi)]
        else:
            outs = _adam_sharded(wv, mv, vv, plan.chip_sums[k], plan.from_chips[k], chip, name="adam_" + k)
        grads[k], deltas[k], new_m[k], new_v[k] = (flip(t, k).reshape(lead) for t in outs)

    dg_mix, dg_memq, dg_memkv, dg_ffn, dg_fin, dconv_w, loss = small
    conv_rows = jnp.zeros((3, d), F32).at[:, :CONV_WIDTH].set(dconv_w[:3])
    block = jnp.concatenate([dg_mix[:1], dg_memq[:1], dg_memkv[:1], dg_ffn[:1], dg_fin[:1], conv_rows,
                             jnp.broadcast_to(loss[:1, :1], (1, d)), jnp.zeros((SMALL_ROWS - 9, d), F32)], axis=0)
    total = _sum_devices(_exchange(_gather_comm([block]), name="gather_small")[0], name="sum_small")
    g_conv = lax.dynamic_slice(total[5:8, :CONV_WIDTH], (0, dev * n_conv), (3, n_conv))
    small_w = [norm_mix, norm_mem_q, norm_mem_kv, norm_ffn, norm_final.reshape(1, d), conv_w[0]]
    small_m = [m_norm_mix, m_norm_mem_q, m_norm_mem_kv, m_norm_ffn, m_norm_final.reshape(1, d), m_conv_w[0]]
    small_v = [v_norm_mix, v_norm_mem_q, v_norm_mem_kv, v_norm_ffn, v_norm_final.reshape(1, d), v_conv_w[0]]
    small_g = [total[0:1], total[1:2], total[2:3], total[3:4], total[4:5], g_conv]
    small_names = ["norm_mix", "norm_mem_q", "norm_mem_kv", "norm_ffn", "norm_final", "conv_w"]
    sg, sd, sm, sv = {}, {}, {}, {}
    for nme, wv, g, m, v in zip(small_names, small_w, small_g, small_m, small_v):
        dl, nm, nv = _adam_small(wv, g, m, v, name="adam_" + nme)
        shape = norm_final.shape if nme == "norm_final" else (conv_w.shape if nme == "conv_w" else wv.shape)
        sg[nme], sd[nme], sm[nme], sv[nme] = (t.reshape(shape) for t in (g, dl, nm, nv))

    def ordered(big, sml):
        return (sml["norm_mix"], big["in"], sml["conv_w"], big["a"], big["b"], big["mix"], sml["norm_mem_q"],
                sml["norm_mem_kv"], big["mq"], big["kv"], big["mo"], sml["norm_ffn"], big["fi"], big["fo"],
                sml["norm_final"])

    loss_out = total[8, 0]
    grad_x = dx0.reshape(x.shape)
    return (loss_out, grad_x, *ordered(grads, sg), *ordered(deltas, sd), *ordered(new_m, sm), *ordered(new_v, sv))
```

```python
import functools
import math

import jax
import jax.numpy as jnp
from jax import lax
from jax.experimental import pallas as pl
from jax.experimental.pallas import tpu as pltpu

F32 = jnp.float32
BF16 = jnp.bfloat16
MESH = pl.DeviceIdType.MESH

N_DEV = 8
N_CHIP = 4
NORM_EPS = 1e-6
SB_HEADS = 8
SB_HEAD_DIM = 64
SB_WIDTH = SB_HEADS * SB_HEAD_DIM
CONV_WIDTH = 512
MEM_HEADS = 4
ADAM_LR = 0.001
ADAM_B1 = 0.9
ADAM_B2 = 0.999
ADAM_EPS = 1e-08
ADAM_WD = 0.01
ADAM_STEP = 10

LANES = 128
VMEM_LIMIT_BYTES = 52 * 1024 * 1024
SB_TILE = 256
SB_DEAD = 159.0
SB_CLAMP = 126.0
LOG2_E = 1.4426950408889634

ANY = pl.BlockSpec(memory_space=pl.ANY)


def _params(n_grid):
    return pltpu.CompilerParams(dimension_semantics=("arbitrary",) * n_grid, vmem_limit_bytes=VMEM_LIMIT_BYTES)


def _bdot(a, b, dims):
    return lax.dot_general(a.astype(BF16), b.astype(BF16), (dims, ((), ())), preferred_element_type=F32)


NN = ((1,), (0,))
NT = ((1,), (1,))
TN = ((0,), (0,))


class _Comm:
    def __init__(self, ins, outs, n_sems, start, finish):
        self.ins, self.outs, self.n_sems, self.start, self.finish = ins, outs, n_sems, start, finish

    def sem_shapes(self):
        return [pltpu.SemaphoreType.DMA((k,)) for k in self.n_sems]


def _place():
    return lax.axis_index("x"), lax.axis_index("y"), lax.axis_index("c")


def _gather_comm(shards):
    n = len(shards)

    def copies(ins, outs, sems):
        send_sems, recv_sems, _ = sems
        x, y, c = _place()
        chips = [(1 - x, y), (x, 1 - y), (1 - x, 1 - y)]

        def copy(a, k, block, to, from_shard=False):
            dst = outs[a].at[4 * block[0] + 2 * block[1] + block[2]]
            return pltpu.make_async_remote_copy(
                src_ref=ins[a] if from_shard else dst, dst_ref=dst, send_sem=send_sems.at[a * 7 + k],
                recv_sem=recv_sems.at[a * 7 + k], device_id=to, device_id_type=MESH)

        me, sibling = (x, y, c), (x, y, 1 - c)
        own = [[copy(a, 0, me, sibling, True)] + [copy(a, 1 + j, me, (*chip, c), True) for j, chip in enumerate(chips)]
               for a in range(n)]
        landed = [[copy(a, 1 + j, (*chip, c), me) for j, chip in enumerate(chips)] for a in range(n)]
        passed = [[copy(a, 4 + j, (*chip, c), sibling) for j, chip in enumerate(chips)] for a in range(n)]
        from_sibling = [[copy(a, 0, sibling, me)] + [copy(a, 4 + j, (*chip, 1 - c), me) for j, chip in enumerate(chips)]
                        for a in range(n)]
        local = [pltpu.make_async_copy(ins[a], outs[a].at[4 * x + 2 * y + c], sems[2].at[a]) for a in range(n)]
        return own, landed, passed, from_sibling, local

    def start(ins, outs, sems):
        own, _, _, _, local = copies(ins, outs, sems)
        for a in range(n):
            local[a].start()
            for cp in own[a]:
                cp.start()

    def finish(ins, outs, sems):
        own, landed, passed, from_sibling, local = copies(ins, outs, sems)
        for a in range(n):
            for arrived, onward in zip(landed[a], passed[a]):
                arrived.wait_recv()
                onward.start()
        for a in range(n):
            for cp in from_sibling[a]:
                cp.wait_recv()
        for a in range(n):
            for cp in own[a] + passed[a]:
                cp.wait_send()
            local[a].wait()

    outs = [jax.ShapeDtypeStruct((N_DEV,) + s.shape, s.dtype) for s in shards]
    return _Comm(list(shards), outs, (7 * n, 7 * n, n), start, finish)


def _sibling_comm(parts):
    n = len(parts)

    def copies(ins, outs, sems):
        x, y, c = _place()
        return [pltpu.make_async_remote_copy(
            src_ref=ins[a].at[2 * q + 1 - c], dst_ref=outs[a].at[q], send_sem=sems[0].at[a * N_CHIP + q],
            recv_sem=sems[1].at[a * N_CHIP + q], device_id=(x, y, 1 - c), device_id_type=MESH)
            for a in range(n) for q in range(N_CHIP)]

    def start(ins, outs, sems):
        for cp in copies(ins, outs, sems):
            cp.start()

    def finish(ins, outs, sems):
        cps = copies(ins, outs, sems)
        for cp in cps:
            cp.wait_recv()
        for cp in cps:
            cp.wait_send()

    outs = [jax.ShapeDtypeStruct((N_CHIP,) + p.shape[1:], p.dtype) for p in parts]
    return _Comm(list(parts), outs, (N_CHIP * n, N_CHIP * n), start, finish)


def _chips_comm(parts):
    n = len(parts)

    def copies(ins, outs, sems):
        x, y, c = _place()
        chips = [(1 - x, y), (x, 1 - y), (1 - x, 1 - y)]
        return [pltpu.make_async_remote_copy(
            src_ref=ins[a].at[2 * px + py], dst_ref=outs[a].at[j], send_sem=sems[0].at[a * 3 + j],
            recv_sem=sems[1].at[a * 3 + j], device_id=(px, py, c), device_id_type=MESH)
            for a in range(n) for j, (px, py) in enumerate(chips)]

    def start(ins, outs, sems):
        for cp in copies(ins, outs, sems):
            cp.start()

    def finish(ins, outs, sems):
        cps = copies(ins, outs, sems)
        for cp in cps:
            cp.wait_recv()
        for cp in cps:
            cp.wait_send()

    outs = [jax.ShapeDtypeStruct((3,) + p.shape[1:], p.dtype) for p in parts]
    return _Comm(list(parts), outs, (3 * n, 3 * n), start, finish)


def _join_comms(comms):
    if len(comms) == 1:
        return comms[0]

    def split(refs, counts):
        out, at = [], 0
        for n in counts:
            out.append(refs[at:at + n])
            at += n
        return out

    def each(method):
        def run(ins, outs, sems):
            parts = zip(comms, split(ins, [len(c.ins) for c in comms]), split(outs, [len(c.outs) for c in comms]),
                        split(sems, [len(c.n_sems) for c in comms]))
            for c, c_ins, c_outs, c_sems in parts:
                getattr(c, method)(c_ins, c_outs, c_sems)
        return run

    return _Comm([a for c in comms for a in c.ins], [o for c in comms for o in c.outs],
                 tuple(k for c in comms for k in c.n_sems), each("start"), each("finish"))


def _exchange(comm, *, name):
    n_ci, n_co = len(comm.ins), len(comm.outs)

    def kern(*refs):
        c_ins, c_outs, sems = refs[:n_ci], refs[n_ci:n_ci + n_co], refs[n_ci + n_co:]
        comm.start(c_ins, c_outs, sems)
        comm.finish(c_ins, c_outs, sems)

    return pl.pallas_call(kern, name=name, in_specs=[ANY] * n_ci, out_specs=[ANY] * n_co, out_shape=comm.outs,
                          scratch_shapes=comm.sem_shapes())(*comm.ins)


def _call(body, *, name, grid, in_specs, out_specs, out_shape, scratch, args, plan=None):
    comm = plan.comm(name) if plan is not None else None
    if comm is None:
        return list(pl.pallas_call(functools.partial(body), name=name, grid=grid, in_specs=in_specs,
                                   out_specs=out_specs, out_shape=out_shape, scratch_shapes=scratch,
                                   compiler_params=_params(len(grid)))(*args))
    n_in, n_out, n_scr, n_ci, n_co = len(in_specs), len(out_specs), len(scratch), len(comm.ins), len(comm.outs)

    def kern(*refs):
        ins, c_ins, refs = refs[:n_in], refs[n_in:n_in + n_ci], refs[n_in + n_ci:]
        outs, c_outs, refs = refs[:n_out], refs[n_out:n_out + n_co], refs[n_out + n_co:]
        scr, sems = refs[:n_scr], refs[n_scr:]
        ids = [pl.program_id(ax) for ax in range(len(grid))]
        first = functools.reduce(jnp.logical_and, [i == 0 for i in ids])
        last = functools.reduce(jnp.logical_and, [i == g - 1 for i, g in zip(ids, grid)])

        @pl.when(first)
        def _():
            comm.start(c_ins, c_outs, sems)
        body(*ins, *outs, *scr)

        @pl.when(last)
        def _():
            comm.finish(c_ins, c_outs, sems)

    res = pl.pallas_call(kern, name=name, grid=grid, in_specs=list(in_specs) + [ANY] * n_ci,
                         out_specs=list(out_specs) + [ANY] * n_co, out_shape=list(out_shape) + comm.outs,
                         scratch_shapes=list(scratch) + comm.sem_shapes(),
                         compiler_params=_params(len(grid)))(*args, *comm.ins)
    plan.landed(name, list(res[n_out:]))
    return list(res[:n_out])


def _mm_body(dims, has_add, *refs):
    if has_add:
        a_ref, b_ref, add_ref, o_ref = refs
        total = _bdot(a_ref[...], b_ref[...], dims) + add_ref[...]
    else:
        a_ref, b_ref, o_ref = refs
        total = _bdot(a_ref[...], b_ref[...], dims)
    o_ref[...] = total.astype(o_ref.dtype)


def _mm_nt_body(j, n, dy_ref, w_ref, o_ref):
    total = _bdot(dy_ref[:, 0:n], w_ref[0], NT)
    for jj in range(1, j):
        total = total + _bdot(dy_ref[:, jj * n:(jj + 1) * n], w_ref[jj], NT)
    o_ref[...] = total.astype(o_ref.dtype)


def _mm_nn(a, w3, *, name, out_dtype=BF16, add=None, tm=1024, tn=None, out3=False, w_t=False, plan=None):
    m, kk = a.shape
    j, n = w3.shape[0], w3.shape[1 if w_t else 2]
    tm, tn = min(tm, m), n if tn is None else tn
    n_t = n // tn
    in_specs = [pl.BlockSpec((tm, kk), lambda i, jj: (i, 0)),
                pl.BlockSpec((None, tn, kk), lambda i, jj: (jj // n_t, jj % n_t, 0)) if w_t else
                pl.BlockSpec((None, kk, tn), lambda i, jj: (jj // n_t, 0, jj % n_t))]
    args = [a, w3]
    if add is not None:
        in_specs.append(pl.BlockSpec((tm, tn), lambda i, jj: (i, jj)))
        args.append(add)
    if out3:
        out_spec = pl.BlockSpec((None, tm, tn), lambda i, jj: (jj // n_t, i, jj % n_t))
        out_shape = jax.ShapeDtypeStruct((j, m, n), out_dtype)
    else:
        out_spec = pl.BlockSpec((tm, tn), lambda i, jj: (i, jj))
        out_shape = jax.ShapeDtypeStruct((m, j * n), out_dtype)
    return _call(
        functools.partial(_mm_body, NT if w_t else NN, add is not None), name=name, grid=(m // tm, j * n_t),
        in_specs=in_specs, out_specs=[out_spec], out_shape=[out_shape], scratch=[], args=args, plan=plan)[0]


def _mm_gathering(a, shard, *, name, out3=False, w_t=False, tm=1024):
    m, kk = a.shape
    n = shard.shape[0 if w_t else 1]
    tm = min(tm, m)
    n_i = m // tm
    fetch_at = min(1, n_i - 1)

    def body(a_ref, shard_ref, o_ref, w_all, w_vmem, send_sems, recv_sems, copy_sems):
        jj, i = pl.program_id(0), pl.program_id(1)
        x, y, c = _place()
        me, sibling = (x, y, c), (x, y, 1 - c)
        chips = [(jnp.bitwise_xor(x, c), jnp.bitwise_xor(y, 1 - c)), (jnp.bitwise_xor(x, 1 - c), jnp.bitwise_xor(y, c)),
                 (1 - x, 1 - y)]
        sibling_chips = [chips[1], chips[0], chips[2]]

        def rows(block):
            return w_all.at[4 * block[0] + 2 * block[1] + block[2]]

        def remote(k, block, to, from_shard=False):
            return pltpu.make_async_remote_copy(
                src_ref=shard_ref if from_shard else rows(block), dst_ref=rows(block), send_sem=send_sems.at[k],
                recv_sem=recv_sems.at[k], device_id=to, device_id_type=MESH)

        def load(step, src):
            return pltpu.make_async_copy(src, w_vmem.at[step % 2], copy_sems.at[1 + step % 2])

        own = [remote(0, me, sibling, True)] + [remote(1 + j, me, (*chip, c), True) for j, chip in enumerate(chips)]
        passed = [remote(4 + j, (*chip, c), sibling) for j, chip in enumerate(chips)]
        local = pltpu.make_async_copy(shard_ref, rows(me), copy_sems.at[0])

        @pl.when(jnp.logical_and(i == 0, jj == 0))
        def _():
            local.start()
            own[0].start()
            own[1].start()
            load(0, shard_ref).start()

        def arrivals():
            yield 1, (lambda: remote(0, sibling, me).wait_recv()), sibling
            for j, chip in enumerate(chips):
                def landed(j=j, chip=chip):
                    if j < 2:
                        own[1 + j].wait_send()
                        own[2 + j].start()
                    remote(1 + j, (*chip, c), me).wait_recv()
                    passed[j].start()
                yield 2 + 2 * j, landed, (*chip, c)
                block = (*sibling_chips[j], 1 - c)
                yield 3 + 2 * j, (lambda j=j, block=block: remote(4 + j, block, me).wait_recv()), block

        for step, wait_for_it, block in arrivals():
            @pl.when(jnp.logical_and(i == fetch_at, jj == step - 1))
            def _():
                wait_for_it()
                load(step, rows(block)).start()

        for step in range(N_DEV):
            @pl.when(jnp.logical_and(i == 0, jj == step))
            def _():
                load(step, rows(me)).wait()

        o_ref[...] = _bdot(a_ref[...], w_vmem[lax.rem(jj, 2)], NT if w_t else NN).astype(o_ref.dtype)

        @pl.when(jnp.logical_and(i == n_i - 1, jj == N_DEV - 1))
        def _():
            for cp in [own[0], own[3]] + passed:
                cp.wait_send()
            local.wait()

    def swept(jj):
        x, y, c = _place()
        first, second = 2 + 2 * c, 4 - 2 * c
        flips = (0b000, 0b001, first, second + 1, second, first + 1, 0b110, 0b111)
        return jnp.bitwise_xor(4 * x + 2 * y + c, sum(jnp.where(jj == k, f, 0) for k, f in enumerate(flips)))

    if out3:
        out_spec = pl.BlockSpec((None, tm, n), lambda jj, i: (swept(jj), i, 0))
        out_shape = jax.ShapeDtypeStruct((N_DEV, m, n), BF16)
    else:
        out_spec = pl.BlockSpec((tm, n), lambda jj, i: (i, swept(jj)))
        out_shape = jax.ShapeDtypeStruct((m, N_DEV * n), BF16)
    return pl.pallas_call(
        body, name=name, grid=(N_DEV, n_i),
        in_specs=[pl.BlockSpec((tm, kk), lambda jj, i: (i, 0)), ANY], out_specs=[out_spec, ANY],
        scratch_shapes=[pltpu.VMEM((2,) + shard.shape, shard.dtype), pltpu.SemaphoreType.DMA((7,)),
                        pltpu.SemaphoreType.DMA((7,)), pltpu.SemaphoreType.DMA((3,))],
        out_shape=[out_shape, jax.ShapeDtypeStruct((N_DEV,) + shard.shape, shard.dtype)],
        compiler_params=_params(2))(a, shard)


def _sigmoid(v):
    return 0.5 * jnp.tanh(0.5 * v) + 0.5


def _resident(w):
    return pl.BlockSpec(w.shape, lambda i: (0,) * w.ndim, pipeline_mode=pl.Buffered(1))


def _ffn_out_loss(gu3, w3, add, g, target, *, name, tm=512):
    j2, m, n = gu3.shape
    j = j2 // 2
    nn = w3.shape[2]
    tm = min(tm, m)

    def body(gu_ref, w_ref, add_ref, g_ref, t_ref, dx_ref, dxb_ref, dg_ref, loss_ref, act_ref):
        i = pl.program_id(0)
        xv = add_ref[...]
        for jj in range(j):
            gate = gu_ref[0, jj].astype(F32)
            act = (gate * _sigmoid(gate) * gu_ref[1, jj].astype(F32)).astype(BF16)
            act_ref[jj] = act
            xv = xv + _bdot(act, w_ref[jj], NN)
        gv = g_ref[...]
        r = lax.rsqrt(jnp.mean(xv * xv, axis=-1, keepdims=True) + NORM_EPS)
        xhat = xv * r
        err = xhat * gv - t_ref[...]
        _acc_rows(i, loss_ref, 0.5 * jnp.sum(jnp.mean(err * err, axis=-1, keepdims=True), axis=0, keepdims=True))
        dy = err * (1.0 / nn)
        dxhat = dy * gv
        dx = r * (dxhat - xhat * jnp.mean(dxhat * xhat, axis=-1, keepdims=True))
        dx_ref[...] = dx
        dxb_ref[...] = dx.astype(BF16)
        _acc_rows(i, dg_ref, jnp.sum(dy * xhat, axis=0, keepdims=True))

    row = pl.BlockSpec((tm, nn), lambda i: (i, 0))
    return _call(body, name=name, grid=(m // tm,),
                 in_specs=[pl.BlockSpec((2, j, tm, n), lambda i: (0, 0, i, 0)), _resident(w3),
                           row, pl.BlockSpec(g.shape, lambda i: (0, 0)), row],
                 out_specs=[row, row, pl.BlockSpec((8, nn), lambda i: (0, 0)), pl.BlockSpec((8, LANES), lambda i: (0, 0)),
                            pl.BlockSpec((j, tm, n), lambda i: (0, i, 0))],
                 out_shape=[jax.ShapeDtypeStruct((m, nn), F32), jax.ShapeDtypeStruct((m, nn), BF16),
                            jax.ShapeDtypeStruct((8, nn), F32), jax.ShapeDtypeStruct((8, LANES), F32),
                            jax.ShapeDtypeStruct((j, m, n), BF16)],
                 scratch=[], args=[gu3.reshape(2, j, m, n), w3, add, g, target])


def _ffn_out_bwd(dy, w3, gu3, *, name, tm=1024):
    m, nn = dy.shape
    j, n, _ = w3.shape
    tm = min(tm, m)

    def body(dy_ref, w_ref, gu_ref, dgu_ref):
        da = _bdot(dy_ref[...], w_ref[...], NT)
        gate = gu_ref[0].astype(F32)
        up = gu_ref[1].astype(F32)
        sg = _sigmoid(gate)
        silu = gate * sg
        dgu_ref[0] = (da * up * (sg + silu * (1.0 - sg))).astype(BF16)
        dgu_ref[1] = (da * silu).astype(BF16)

    out = _call(body, name=name, grid=(m // tm, j),
                in_specs=[pl.BlockSpec((tm, nn), lambda i, jj: (i, 0)),
                          pl.BlockSpec((None, n, nn), lambda i, jj: (jj, 0, 0)),
                          pl.BlockSpec((2, None, tm, n), lambda i, jj: (0, jj, i, 0))],
                out_specs=[pl.BlockSpec((2, None, tm, n), lambda i, jj: (0, jj, i, 0))],
                out_shape=[jax.ShapeDtypeStruct((2, j, m, n), BF16)], scratch=[],
                args=[dy, w3, gu3.reshape(2, j, m, n)])[0]
    return out.reshape(2 * j, m, n)


def _rms_fwd_tail(xv, g_ref, h_ref):
    r = lax.rsqrt(jnp.mean(xv * xv, axis=-1, keepdims=True) + NORM_EPS)
    h_ref[...] = (xv * r * g_ref[...]).astype(BF16)


def _rms_bwd_tail(i, dh, x_ref, g_ref, dres_ref, dx_ref, dxb_ref, dg_ref):
    xv = x_ref[...]
    r = lax.rsqrt(jnp.mean(xv * xv, axis=-1, keepdims=True) + NORM_EPS)
    xhat = xv * r
    dxhat = dh * g_ref[...]
    dx = r * (dxhat - xhat * jnp.mean(dxhat * xhat, axis=-1, keepdims=True))
    if dres_ref is not None:
        dx = dx + dres_ref[...]
    dx_ref[...] = dx
    dxb_ref[...] = dx.astype(BF16)
    _acc_rows(i, dg_ref, jnp.sum(dh * xhat, axis=0, keepdims=True))


def _mm_nt_rms(dy, w3, x, g, dres, *, name, dy3=False, w_nn=False, tm=512, plan=None):
    j = w3.shape[0]
    m, kk = x.shape
    n = dy.shape[2] if dy3 else dy.shape[1] // j
    tm = min(tm, m)

    def body(dy_ref, w_ref, x_ref, g_ref, *rest):
        dres_ref = rest[0] if dres is not None else None
        dx_ref, dxb_ref, dg_ref = rest[-3:]
        dh = None
        for jj in range(j):
            piece = dy_ref[jj] if dy3 else dy_ref[:, jj * n:(jj + 1) * n]
            part = _bdot(piece, w_ref[jj], NN if w_nn else NT)
            dh = part if dh is None else dh + part
        _rms_bwd_tail(pl.program_id(0), dh, x_ref, g_ref, dres_ref, dx_ref, dxb_ref, dg_ref)

    row = pl.BlockSpec((tm, kk), lambda i: (i, 0))
    in_specs = [pl.BlockSpec((j, tm, n), lambda i: (0, i, 0)) if dy3 else pl.BlockSpec((tm, j * n), lambda i: (i, 0)),
                _resident(w3), row, pl.BlockSpec(g.shape, lambda i: (0, 0))]
    args = [dy, w3, x, g]
    if dres is not None:
        in_specs.append(row)
        args.append(dres)
    return _call(body, name=name, grid=(m // tm,), in_specs=in_specs,
                 out_specs=[row, row, pl.BlockSpec((8, kk), lambda i: (0, 0))],
                 out_shape=[jax.ShapeDtypeStruct((m, kk), F32), jax.ShapeDtypeStruct((m, kk), BF16),
                            jax.ShapeDtypeStruct((8, kk), F32)], scratch=[], args=args, plan=plan)


def _mix_out(br_a, br_b, proj, w, x, g, *, name, tm=512, plan=None):
    s, d = br_a.shape
    tm = min(tm, s)

    def body(a_ref, b_ref, ga_ref, gb_ref, w_ref, x_ref, g_ref, x1_ref, h_ref, merged_ref):
        merged = (_sigmoid(ga_ref[...].astype(F32)) * a_ref[...].astype(F32)
                  + _sigmoid(gb_ref[...].astype(F32)) * b_ref[...].astype(F32)).astype(BF16)
        merged_ref[...] = merged
        xv = _bdot(merged, w_ref[...], NN) + x_ref[...]
        x1_ref[...] = xv
        _rms_fwd_tail(xv, g_ref, h_ref)

    row = pl.BlockSpec((tm, d), lambda i: (i, 0))
    return _call(body, name=name, grid=(s // tm,),
                 in_specs=[row, row, pl.BlockSpec((tm, d), lambda i: (i, 3)), pl.BlockSpec((tm, d), lambda i: (i, 4)),
                           pl.BlockSpec(w.shape, lambda i: (0, 0)), row, pl.BlockSpec(g.shape, lambda i: (0, 0))],
                 out_specs=[row, row, row],
                 out_shape=[jax.ShapeDtypeStruct((s, d), F32), jax.ShapeDtypeStruct((s, d), BF16),
                            jax.ShapeDtypeStruct((s, d), BF16)],
                 scratch=[], args=[br_a, br_b, proj, proj, w, x, g], plan=plan)


def _mm_tn_a3(a3, dy, *, name):
    j, t, n = a3.shape
    nn = dy.shape[1]
    return _call(functools.partial(_mm_body, TN, False), name=name, grid=(j,),
                 in_specs=[pl.BlockSpec((None, t, n), lambda jj: (jj, 0, 0)), pl.BlockSpec((t, nn), lambda jj: (0, 0))],
                 out_specs=[pl.BlockSpec((None, n, nn), lambda jj: (jj, 0, 0))],
                 out_shape=[jax.ShapeDtypeStruct((j, n, nn), BF16)], scratch=[], args=[a3, dy])[0]


def _mm_nt(dy, w3, *, name, out_dtype=BF16, tm=512, tn=1024, plan=None):
    m = dy.shape[0]
    j, kk, n = w3.shape
    tm, tn = min(tm, m), min(tn, kk)
    return _call(
        functools.partial(_mm_nt_body, j, n), name=name,
        grid=(m // tm, kk // tn),
        in_specs=[pl.BlockSpec((tm, j * n), lambda i, q: (i, 0)),
                  pl.BlockSpec((j, tn, n), lambda i, q: (0, q, 0))],
        out_specs=[pl.BlockSpec((tm, tn), lambda i, q: (i, q))],
        out_shape=[jax.ShapeDtypeStruct((m, kk), out_dtype)], scratch=[], args=[dy, w3], plan=plan)[0]


def _mm_tn(a, dy, n, *, name, out_dtype=BF16, tm=512, tn=None, k_tiles=None, plan=None):
    t, kk = a.shape
    j = dy.shape[1] // n
    tm, tn = min(tm, kk), n if tn is None else tn
    n_t = n // tn
    first, count = (0, kk // tm) if k_tiles is None else k_tiles
    return _call(
        functools.partial(_mm_body, TN, False), name=name,
        grid=(count, j * n_t),
        in_specs=[pl.BlockSpec((t, tm), lambda i, jj: (0, first + i)),
                  pl.BlockSpec((t, tn), lambda i, jj: (0, jj))],
        out_specs=[pl.BlockSpec((None, tm, tn), lambda i, jj: (jj // n_t, i, jj % n_t))],
        out_shape=[jax.ShapeDtypeStruct((j, count * tm, n), out_dtype)], scratch=[], args=[a, dy], plan=plan)[0]


def _rows(body, ins, outs, *, n_rows, tm, name, plan=None):
    tm = min(tm, n_rows)
    n_steps = n_rows // tm
    in_specs, args = [], []
    for arr, kind, width, block in ins:
        if kind == "row":
            in_specs.append(pl.BlockSpec((tm, width), functools.partial(lambda i, b: (i, b), b=block)))
        elif kind == "prev":
            in_specs.append(pl.BlockSpec((tm, width), functools.partial(lambda i, b: (jnp.maximum(i - 1, 0), b), b=block)))
        elif kind == "next":
            in_specs.append(pl.BlockSpec((tm, width), functools.partial(lambda i, b: (jnp.minimum(i + 1, n_steps - 1), b), b=block)))
        else:
            in_specs.append(pl.BlockSpec(arr.shape, functools.partial(lambda i, nd: (0,) * nd, nd=arr.ndim)))
        args.append(arr)
    out_specs, out_shape = [], []
    for shape, dtype, kind in outs:
        if kind == "row":
            out_specs.append(pl.BlockSpec((tm, shape[1]), lambda i: (i, 0)))
        else:
            out_specs.append(pl.BlockSpec(shape, functools.partial(lambda i, nd: (0,) * nd, nd=len(shape))))
        out_shape.append(jax.ShapeDtypeStruct(shape, dtype))

    def kern(*refs):
        body(pl.program_id(0), n_steps, *refs)

    return _call(kern, name=name, grid=(n_steps,), in_specs=in_specs, out_specs=out_specs, out_shape=out_shape,
                 scratch=[], args=args, plan=plan)


def _acc_rows(i, ref, value):
    @pl.when(i == 0)
    def _():
        ref[...] = jnp.zeros_like(ref)
    ref[...] += jnp.broadcast_to(value, ref.shape)


def _rms_fwd(x, g, *, name, tm=512):
    s, d = x.shape

    def body(i, n, x_ref, g_ref, h_ref):
        _rms_fwd_tail(x_ref[...], g_ref, h_ref)

    return _rows(body, [(x, "row", d, 0), (g, "full", 0, 0)], [((s, d), BF16, "row")], n_rows=s, tm=tm, name=name)[0]


def _rms_bwd(x, g, dh, dres, *, name, tm=512, plan=None):
    s, d = x.shape

    def body(i, n, x_ref, g_ref, dh_ref, dres_ref, dx_ref, dxb_ref, dg_ref):
        _rms_bwd_tail(i, dh_ref[...].astype(F32), x_ref, g_ref, dres_ref, dx_ref, dxb_ref, dg_ref)

    return _rows(body, [(x, "row", d, 0), (g, "full", 0, 0), (dh, "row", d, 0), (dres, "row", d, 0)],
                 [((s, d), F32, "row"), ((s, d), BF16, "row"), ((8, d), F32, "acc")],
                 n_rows=s, tm=tm, name=name, plan=plan)


def _mix_out_bwd(dx1b, w, br_a, br_b, proj, *, name, tm=512, plan=None):
    s, d = br_a.shape
    tm = min(tm, s)

    def body(dy_ref, w_ref, a_ref, b_ref, ga_ref, gb_ref, da_ref, db_ref, dg_ref):
        dm = _bdot(dy_ref[...], w_ref[...], NT)
        sa = _sigmoid(ga_ref[...].astype(F32))
        sb = _sigmoid(gb_ref[...].astype(F32))
        da_ref[...] = (dm * sa).astype(BF16)
        db_ref[...] = (dm * sb).astype(BF16)
        dg_ref[:, :d] = (dm * a_ref[...].astype(F32) * sa * (1.0 - sa)).astype(BF16)
        dg_ref[:, d:] = (dm * b_ref[...].astype(F32) * sb * (1.0 - sb)).astype(BF16)

    row = pl.BlockSpec((tm, d), lambda i: (i, 0))
    return _call(body, name=name, grid=(s // tm,),
                 in_specs=[row, pl.BlockSpec(w.shape, lambda i: (0, 0)), row, row,
                           pl.BlockSpec((tm, d), lambda i: (i, 3)), pl.BlockSpec((tm, d), lambda i: (i, 4))],
                 out_specs=[row, row, pl.BlockSpec((tm, 2 * d), lambda i: (i, 0))],
                 out_shape=[jax.ShapeDtypeStruct((s, d), BF16), jax.ShapeDtypeStruct((s, d), BF16),
                            jax.ShapeDtypeStruct((s, 2 * d), BF16)],
                 scratch=[], args=[dx1b, w, br_a, br_b, proj, proj], plan=plan)


def _shift_down(cur, prev, k, first):
    row = lax.broadcasted_iota(jnp.int32, cur.shape, 0)
    out = jnp.where(row >= k, pltpu.roll(cur, k, 0), pltpu.roll(prev, k, 0))
    return jnp.where(jnp.logical_and(first, row < k), 0.0, out)


def _shift_up(cur, nxt, k, last):
    tm = cur.shape[0]
    row = lax.broadcasted_iota(jnp.int32, cur.shape, 0)
    out = jnp.where(row < tm - k, pltpu.roll(cur, tm - k, 0), pltpu.roll(nxt, tm - k, 0))
    return jnp.where(jnp.logical_and(last, row >= tm - k), 0.0, out)


def _conv_fwd(proj, conv_w, *, name, tm=512):
    s = proj.shape[0]
    c = CONV_WIDTH

    def body(i, n, u_ref, gb_ref, gc_ref, up_ref, gcp_ref, w_ref, y_ref):
        cu = gc_ref[...].astype(F32) * u_ref[...].astype(F32)
        cup = gcp_ref[...].astype(F32) * up_ref[...].astype(F32)
        first = i == 0
        y = (w_ref[0:1, :] * _shift_down(cu, cup, 2, first) + w_ref[1:2, :] * _shift_down(cu, cup, 1, first)
             + w_ref[2:3, :] * cu)
        y_ref[...] = (gb_ref[...].astype(F32) * y).astype(BF16)

    return _rows(body, [(proj, "row", c, 3), (proj, "row", c, 4), (proj, "row", c, 5),
                        (proj, "prev", c, 3), (proj, "prev", c, 5), (conv_w, "full", 0, 0)],
                 [((s, c), BF16, "row")], n_rows=s, tm=tm, name=name)[0]


def _conv_bwd(dy_b, proj, conv_w, *, name, tm=512, plan=None):
    s = proj.shape[0]
    c = CONV_WIDTH

    def body(i, n, dy_ref, u_ref, gb_ref, gc_ref, up_ref, gcp_ref, dyn_ref, gbn_ref, w_ref, d_ref, dw_ref):
        first, last = i == 0, i == n - 1
        u = u_ref[...].astype(F32)
        gb = gb_ref[...].astype(F32)
        gc = gc_ref[...].astype(F32)
        cu = gc * u
        cup = gcp_ref[...].astype(F32) * up_ref[...].astype(F32)
        cu1 = _shift_down(cu, cup, 1, first)
        cu2 = _shift_down(cu, cup, 2, first)
        conv = w_ref[0:1, :] * cu2 + w_ref[1:2, :] * cu1 + w_ref[2:3, :] * cu
        dy = dy_ref[...].astype(F32)
        dyc = dy * gb
        dycn = dyn_ref[...].astype(F32) * gbn_ref[...].astype(F32)
        dcu = (w_ref[2:3, :] * dyc + w_ref[1:2, :] * _shift_up(dyc, dycn, 1, last)
               + w_ref[0:1, :] * _shift_up(dyc, dycn, 2, last))
        d_ref[:, 0:c] = (dcu * gc).astype(BF16)
        d_ref[:, c:2 * c] = (dy * conv).astype(BF16)
        d_ref[:, 2 * c:3 * c] = (dcu * u).astype(BF16)
        row = lax.broadcasted_iota(jnp.int32, (8, c), 0)
        dw = (jnp.where(row == 0, jnp.sum(dyc * cu2, axis=0, keepdims=True), 0.0)
              + jnp.where(row == 1, jnp.sum(dyc * cu1, axis=0, keepdims=True), 0.0)
              + jnp.where(row == 2, jnp.sum(dyc * cu, axis=0, keepdims=True), 0.0))

        @pl.when(first)
        def _():
            dw_ref[...] = jnp.zeros_like(dw_ref)
        dw_ref[...] += dw

    return _rows(body, [(dy_b, "row", c, 0), (proj, "row", c, 3), (proj, "row", c, 4), (proj, "row", c, 5),
                        (proj, "prev", c, 3), (proj, "prev", c, 5), (dy_b, "next", c, 0), (proj, "next", c, 4),
                        (conv_w, "full", 0, 0)],
                 [((s, 3 * c), BF16, "row"), ((8, c), F32, "acc")], n_rows=s, tm=tm, name=name, plan=plan)


def _mem_probs(q, k, scale):
    sc = _bdot(q, k, NT) * scale
    sc = sc - jnp.max(sc, axis=-1, keepdims=True)
    p = jnp.exp(sc)
    return p / jnp.sum(p, axis=-1, keepdims=True)


def _mem_sublayer(hq, w_q, kv, w_o, x, g, *, name, tm=512, plan=None):
    s, d = hq.shape
    hd = d // MEM_HEADS
    scale = 1.0 / math.sqrt(hd)
    tm = min(tm, s)

    def body(hq_ref, wq_ref, kv_ref, wo_ref, x_ref, g_ref, q_ref, o_ref, x2_ref, h_ref):
        q_ref[...] = _bdot(hq_ref[...], wq_ref[...], NN).astype(BF16)
        for h in range(MEM_HEADS):
            cols = slice(h * hd, (h + 1) * hd)
            p = _mem_probs(q_ref[:, cols], kv_ref[:, cols], scale)
            o_ref[:, cols] = _bdot(p, kv_ref[:, d + h * hd:d + (h + 1) * hd], NN).astype(BF16)
        xv = _bdot(o_ref[...], wo_ref[...], NN) + x_ref[...]
        x2_ref[...] = xv
        _rms_fwd_tail(xv, g_ref, h_ref)

    row = pl.BlockSpec((tm, d), lambda i: (i, 0))
    whole = lambda a: pl.BlockSpec(a.shape, lambda i: (0,) * a.ndim)
    return _call(body, name=name, grid=(s // tm,),
                 in_specs=[row, whole(w_q), whole(kv), whole(w_o), row, whole(g)], out_specs=[row] * 4,
                 out_shape=[jax.ShapeDtypeStruct((s, d), BF16), jax.ShapeDtypeStruct((s, d), BF16),
                            jax.ShapeDtypeStruct((s, d), F32), jax.ShapeDtypeStruct((s, d), BF16)],
                 scratch=[], args=[hq, w_q, kv, w_o, x, g], plan=plan)


def _mem_sublayer_bwd(dx2b, dx2, x, g, qm, kv, w_q, w_o, *, name, tm=512, plan=None):
    s, d = qm.shape
    hd = d // MEM_HEADS
    scale = 1.0 / math.sqrt(hd)
    tm = min(tm, s)

    def body(dyb_ref, dres_ref, x_ref, g_ref, q_ref, kv_ref, wq_ref, wo_ref, dx_ref, dxb_ref, dg_ref, dq_ref, dkv_ref):
        i = pl.program_id(0)

        @pl.when(i == 0)
        def _():
            dkv_ref[...] = jnp.zeros_like(dkv_ref)
        dom = _bdot(dyb_ref[...], wo_ref[...], NT).astype(BF16)
        for h in range(MEM_HEADS):
            cols = slice(h * hd, (h + 1) * hd)
            vcols = slice(d + h * hd, d + (h + 1) * hd)
            q, k, v, do = q_ref[:, cols], kv_ref[:, cols], kv_ref[:, vcols], dom[:, cols]
            p = _mem_probs(q, k, scale)
            dp = _bdot(do, v, NT)
            ds = p * (dp - jnp.sum(dp * p, axis=-1, keepdims=True)) * scale
            dq_ref[:, cols] = _bdot(ds, k, NN).astype(BF16)
            dkv_ref[:, cols] += _bdot(ds, q, TN)
            dkv_ref[:, vcols] += _bdot(p, do, TN)
        dh = _bdot(dq_ref[...], wq_ref[...], NT)
        _rms_bwd_tail(i, dh, x_ref, g_ref, dres_ref, dx_ref, dxb_ref, dg_ref)

    row = pl.BlockSpec((tm, d), lambda i: (i, 0))
    whole = lambda a: pl.BlockSpec(a.shape, lambda i: (0,) * a.ndim)
    return _call(body, name=name, grid=(s // tm,),
                 in_specs=[row, row, row, whole(g), row, whole(kv), whole(w_q), whole(w_o)],
                 out_specs=[row, row, pl.BlockSpec((8, d), lambda i: (0, 0)), row, whole(kv)],
                 out_shape=[jax.ShapeDtypeStruct((s, d), F32), jax.ShapeDtypeStruct((s, d), BF16),
                            jax.ShapeDtypeStruct((8, d), F32), jax.ShapeDtypeStruct((s, d), BF16),
                            jax.ShapeDtypeStruct(kv.shape, F32)],
                 scratch=[], args=[dx2b, dx2, x, g, qm, kv, w_q, w_o], plan=plan)


def _sb_consts(t):
    row = lax.broadcasted_iota(jnp.int32, (t, t), 0)
    col = lax.broadcasted_iota(jnp.int32, (t, t), 1)
    lane = lax.broadcasted_iota(jnp.int32, (t, LANES), 1)
    return row, col, lane < SB_HEAD_DIM


def _sb_logits(q, k):
    z2 = jnp.minimum(_bdot(q, k, NT) * LOG2_E, SB_CLAMP)
    return z2, jnp.exp2(z2)


def _tri_sum(v, tri):
    hi = v.astype(BF16)
    lo = (v - hi.astype(F32)).astype(BF16)
    return _bdot(hi, tri, NN) + _bdot(lo, tri, NN)


def _sb_fwd(proj, *, name, plan=None):
    s = proj.shape[0]
    t = SB_TILE
    n_q = s // t
    scale = 1.0 / math.sqrt(SB_HEAD_DIM)
    k_blk, v_blk = SB_WIDTH // LANES, 2 * SB_WIDTH // LANES

    def body(q_ref, k_ref, v_ref, o_ref, c_ref, first_ref, acc_ref, c_scr):
        i = pl.program_id(1)
        row, col, head0 = _sb_consts(t)
        later = (row > col).astype(BF16)
        valid = col < row
        qs = q_ref[...] * scale
        q2 = (jnp.where(head0, qs, 0), jnp.where(head0, 0, qs))

        def tiles(kbs, diag_first, carry):
            kt = [k_ref[pl.ds(pl.multiple_of(kb * t, t), t), :] for kb in kbs]
            vt = [v_ref[pl.ds(pl.multiple_of(kb * t, t), t), :] for kb in kbs]
            jobs = [(n, h) for n in range(len(kbs)) for h in range(2)]
            masked = lambda n: diag_first and n == 0
            zs = {(n, h): _sb_logits(q2[h], kt[n]) for n, h in jobs}
            fail = {j: jnp.log2(1.0 + zs[j][1]) for j in jobs}
            fail = {j: jnp.where(valid, fail[j], 0.0) if masked(j[0]) else fail[j] for j in jobs}
            cum = {j: _tri_sum(fail[j], later) for j in jobs}
            run, before = list(carry), {}
            for n, h in jobs:
                before[n, h] = run[h]
                run[h] = run[h] + cum[n, h][:, 0:1] + fail[n, h][:, 0:1]
            w = {j: jnp.exp2(zs[j][0] - fail[j] - cum[j] - before[j]) for j in jobs}
            w = {j: jnp.where(valid, w[j], 0.0) if masked(j[0]) else w[j] for j in jobs}
            for n, h in jobs:
                acc_ref[h] += _bdot(w[n, h], vt[n], NN)
            return tuple(run)

        acc_ref[...] = jnp.zeros_like(acc_ref)
        zero = jnp.zeros((t, 1), F32)

        def alive(carry):
            return (jnp.minimum(jnp.min(carry[0]), jnp.min(carry[1])) < SB_DEAD).astype(jnp.int32)

        def step(state):
            kb, _, c0, c1 = state
            new = tiles([kb], False, (c0, c1))
            return kb - 1, alive(new), new[0], new[1]

        @pl.when(i == 0)
        def _():
            c_scr[0], c_scr[1] = tiles([i], True, (zero, zero))

        @pl.when(i > 0)
        def _():
            c_scr[0], c_scr[1] = tiles([i, i - 1], True, (zero, zero))
        carry = (c_scr[0], c_scr[1])
        kb, _, c0, c1 = lax.while_loop(lambda st: jnp.logical_and(st[0] >= 0, st[1] > 0), step,
                                       (i - 2, alive(carry), carry[0], carry[1]))
        kb = jnp.maximum(kb, -1)
        o_ref[...] = jnp.where(head0, acc_ref[0], acc_ref[1]).astype(BF16)
        c_ref[...] = jnp.where(lax.broadcasted_iota(jnp.int32, (t, 2), 1) == 0, c0, c1)
        first_ref[pl.program_id(0), i] = (kb + 1).astype(F32)

    return _call(
        body, name=name, grid=(SB_HEADS // 2, n_q),
        in_specs=[pl.BlockSpec((t, LANES), lambda p, i: (i, p)),
                  pl.BlockSpec((s, LANES), lambda p, i: (0, k_blk + p)),
                  pl.BlockSpec((s, LANES), lambda p, i: (0, v_blk + p))],
        out_specs=[pl.BlockSpec((t, LANES), lambda p, i: (i, p)),
                   pl.BlockSpec((None, t, 2), lambda p, i: (p, i, 0)),
                   pl.BlockSpec(memory_space=pltpu.SMEM)],
        out_shape=[jax.ShapeDtypeStruct((s, SB_WIDTH), BF16), jax.ShapeDtypeStruct((SB_HEADS // 2, s, 2), F32),
                   jax.ShapeDtypeStruct((SB_HEADS // 2, n_q), F32)],
        scratch=[pltpu.VMEM((2, t, LANES), F32), pltpu.VMEM((2, t, 1), F32)], args=[proj, proj, proj], plan=plan)


def _sb_bwd(proj, do_a, ctot, first, *, name, plan=None):
    s = proj.shape[0]
    t = SB_TILE
    n_q = s // t
    scale = 1.0 / math.sqrt(SB_HEAD_DIM)
    k_blk, v_blk = SB_WIDTH // LANES, 2 * SB_WIDTH // LANES

    def body(q_ref, k_ref, v_ref, do_ref, c_ref, first_ref, dq_ref, dk_ref, dv_ref, dq_acc, dk_acc, dv_acc):
        i = pl.program_id(1)
        kb0 = jnp.clip(first_ref[pl.program_id(0), i].astype(jnp.int32), 0, i)
        row, col, head0 = _sb_consts(t)
        upto = (row <= col).astype(BF16)
        before = (row < col).astype(BF16)
        valid = col < row
        qs = q_ref[...] * scale
        q2 = (jnp.where(head0, qs, 0), jnp.where(head0, 0, qs))
        do = do_ref[...]
        do2 = (jnp.where(head0, do, 0), jnp.where(head0, 0, do))
        ctot2 = (c_ref[:, 0:1], c_ref[:, 1:2])

        @pl.when(i == 0)
        def _():
            dk_acc[...] = jnp.zeros_like(dk_acc)
            dv_acc[...] = jnp.zeros_like(dv_acc)
        dq_acc[...] = jnp.zeros_like(dq_acc)

        def tiles(kbs, diag_last, carry):
            rows = [pl.ds(pl.multiple_of(kb * t, t), t) for kb in kbs]
            kt = [k_ref[r, :] for r in rows]
            vt = [v_ref[r, :] for r in rows]
            jobs = [(n, h) for n in range(len(kbs)) for h in range(2)]
            masked = lambda n: diag_last and n == len(kbs) - 1
            t_last = slice(t - 1, t)
            zs = {(n, h): _sb_logits(q2[h], kt[n]) for n, h in jobs}
            dw = {(n, h): _bdot(do2[h], vt[n], NT) for n, h in jobs}
            fail = {j: jnp.log2(1.0 + zs[j][1]) for j in jobs}
            fail = {j: jnp.where(valid, fail[j], 0.0) if masked(j[0]) else fail[j] for j in jobs}
            cum = {j: _tri_sum(fail[j], upto) for j in jobs}
            miss = {j: jnp.exp2(-fail[j]) for j in jobs}
            beta = {j: zs[j][1] * miss[j] for j in jobs}
            fail_run, fail_before = list(carry[0::2]), {}
            for n, h in jobs:
                fail_before[n, h] = fail_run[h]
                fail_run[h] = fail_run[h] + cum[n, h][:, t_last]
            w = {(n, h): beta[n, h] * jnp.exp2(fail_before[n, h] + cum[n, h] - ctot2[h]) for n, h in jobs}
            w = {j: jnp.where(valid, w[j], 0.0) if masked(j[0]) else w[j] for j in jobs}
            g = {j: w[j] * dw[j] for j in jobs}
            g_local = {j: _bdot(g[j], before, NN) for j in jobs}
            for n, h in jobs:
                dv_acc[rows[n], :] += _bdot(w[n, h], do2[h], TN)
            g_run, dz = list(carry[1::2]), {}
            for n, h in jobs:
                g_sum = g_run[h] + g_local[n, h]
                dz[n, h] = g[n, h] * miss[n, h] - beta[n, h] * g_sum
                g_run[h] = g_sum[:, t_last] + g[n, h][:, t_last]
            dz = {j: jnp.where(valid, dz[j], 0.0) if masked(j[0]) else dz[j] for j in jobs}
            for n, h in jobs:
                dq_acc[h] += _bdot(dz[n, h], kt[n], NN)
                dk_acc[rows[n], :] += _bdot(dz[n, h], q2[h], TN)
            return fail_run[0], g_run[0], fail_run[1], g_run[1]

        zero = jnp.zeros((t, 1), F32)
        carry = lax.fori_loop(kb0, i - 1, lambda n, c: tiles([n], False, c), (zero,) * 4)

        @pl.when(i == 0)
        def _():
            tiles([i], True, carry)

        @pl.when(i > 0)
        def _():
            tiles([i - 1, i], True, carry)
        dq_ref[...] = (jnp.where(head0, dq_acc[0], dq_acc[1]) * scale).astype(BF16)

        @pl.when(i == n_q - 1)
        def _():
            dk_ref[...] = dk_acc[...].astype(BF16)
            dv_ref[...] = dv_acc[...].astype(BF16)

    outs = _call(
        body, name=name, grid=(SB_HEADS // 2, n_q),
        in_specs=[pl.BlockSpec((t, LANES), lambda p, i: (i, p)),
                  pl.BlockSpec((s, LANES), lambda p, i: (0, k_blk + p)),
                  pl.BlockSpec((s, LANES), lambda p, i: (0, v_blk + p)),
                  pl.BlockSpec((t, LANES), lambda p, i: (i, p)),
                  pl.BlockSpec((None, t, 2), lambda p, i: (p, i, 0)),
                  pl.BlockSpec(memory_space=pltpu.SMEM)],
        out_specs=[pl.BlockSpec((t, LANES), lambda p, i: (i, p)),
                   pl.BlockSpec((s, LANES), lambda p, i: (0, p)),
                   pl.BlockSpec((s, LANES), lambda p, i: (0, p))],
        out_shape=[jax.ShapeDtypeStruct((s, SB_WIDTH), BF16)] * 3,
        scratch=[pltpu.VMEM((2, t, LANES), F32), pltpu.VMEM((s, LANES), F32), pltpu.VMEM((s, LANES), F32)],
        args=[proj, proj, proj, do_a, ctot, first], plan=plan)
    return jnp.concatenate(outs, axis=1)


def _mm_gathered(a, key, plan, *, name, out3=False, w_t=False):
    src = plan.gathering(key)
    if src is None:
        return _mm_nn(a, plan.weight(key), name=name, out3=out3, w_t=w_t)
    out, w_all = _mm_gathering(a, src, name=name, out3=out3, w_t=w_t)
    plan.set_weight(key, w_all)
    return out


def _local_step(x, mem, target, gains, plan):
    g_mix, g_memq, g_memkv, g_ffn, g_fin = gains
    d = x.shape[1]

    h0 = _rms_fwd(x, g_mix, name="rms_mix")
    proj = _mm_gathered(h0, "in", plan, name="mm_in")
    w_in = plan.weight("in")
    o_a, ctot, first = _sb_fwd(proj, name="sb_fwd", plan=plan)
    conv_w = plan.weight("conv")
    y_b = _conv_fwd(proj, conv_w, name="conv_fwd")
    w_a, w_b, w_mix = plan.weight("a"), plan.weight("b"), plan.weight("mix")
    br_a = _mm_nn(o_a, w_a, name="mm_branch_a")
    br_b = _mm_nn(y_b, w_b, name="mm_branch_b")
    x1, hq, merged = _mix_out(br_a, br_b, proj, w_mix[0], x, g_memq, name="mm_mix", plan=plan)
    w_mq, w_kv, w_mo = plan.weight("mq")[0], plan.weight("kv"), plan.weight("mo")[0]
    mn = _rms_fwd(mem, g_memkv, name="rms_memkv")
    kv = _mm_nn(mn, w_kv, name="mm_memkv")
    qm, om, x2, hf = _mem_sublayer(hq, w_mq, kv, w_mo, x1, g_ffn, name="mem_sublayer", plan=plan)
    gu = _mm_gathered(hf, "fi", plan, name="mm_ffn_in", out3=True, w_t=True)
    w_fi, w_fo = plan.weight("fi"), plan.weight("fo")
    dx3, dx3b, dg_fin, loss, act = _ffn_out_loss(gu, w_fo, x2, g_fin, target, name="mm_ffn_out")

    plan.grad("fo", _mm_tn_a3(act, dx3b, name="mm_d_w_ffn_out"))
    dgu = _ffn_out_bwd(dx3b, w_fo, gu, name="mm_d_act")
    plan.grad("fi", _mm_tn_a3(dgu, hf, name="mm_d_w_ffn_in"))
    dx2, dx2b, dg_ffn = _mm_nt_rms(dgu, w_fi, x2, g_ffn, dx3, name="mm_d_hf", dy3=True, w_nn=True, plan=plan)

    plan.grad("mo", _mm_tn(om, dx2b, d, name="mm_d_w_memo"))
    dx1, dx1b, dg_memq, dqm, dkv = _mem_sublayer_bwd(dx2b, dx2, x1, g_memq, qm, kv, w_mq, w_mo, name="mem_sublayer_bwd",
                                                    plan=plan)
    plan.grad("mq", _mm_tn(hq, dqm, d, name="mm_d_w_memq"))
    plan.grad("kv", _mm_tn(mn, dkv, w_kv.shape[2], name="mm_d_w_memkv"))
    _, _, dg_memkv = _mm_nt_rms(dkv, w_kv, mem, g_memkv, None, name="mm_d_mn")

    plan.grad("mix", _mm_tn(merged, dx1b, d, name="mm_d_w_mix"))
    dbr_a, dbr_b, dgab = _mix_out_bwd(dx1b, w_mix[0], br_a, br_b, proj, name="mm_d_merged", plan=plan)
    plan.grad("a", _mm_tn(o_a, dbr_a, d, name="mm_d_w_branch_a"))
    do_a = _mm_nt(dbr_a, w_a, name="mm_d_o_a")
    plan.grad("b", _mm_tn(y_b, dbr_b, d, name="mm_d_w_branch_b"))
    dy_b = _mm_nt(dbr_b, w_b, name="mm_d_y_b")
    dconv, dconv_w = _conv_bwd(dy_b, proj, conv_w, name="conv_bwd", plan=plan)
    dqkv = _sb_bwd(proj, do_a, ctot, first, name="sb_bwd", plan=plan)
    dproj = jnp.concatenate([dqkv, dconv, dgab], axis=1)
    rows_in1 = d // IN_SPLIT[1] * (IN_SPLIT[1] - IN_SPLIT[0])
    plan.grad("in0", _mm_tn(h0, dproj, w_in.shape[2], name="mm_d_w_in0", tm=d - rows_in1, k_tiles=(0, 1)))
    plan.grad("in1", _mm_tn(h0, dproj, w_in.shape[2], name="mm_d_w_in1", tm=rows_in1,
                            k_tiles=(d // rows_in1 - 1, 1), plan=plan))
    dh0 = _mm_nt(dproj, w_in, name="mm_d_h0", out_dtype=F32, plan=plan)
    dx0, _, dg_mix = _rms_bwd(x, g_mix, dh0, dx1, name="rms_mix_bwd", plan=plan)

    return dx0, (dg_mix, dg_memq, dg_memkv, dg_ffn, dg_fin, dconv_w, loss)


def _row_tile(a, target=512):
    tm = min(a, target)
    while a % tm:
        tm -= 8
    return tm


def _sum_with_sibling(parts, recvs, core, *, name):
    n = len(parts)

    def body(core_ref, *refs):
        for p_ref, r_ref, o_ref in zip(refs[:n], refs[n:2 * n], refs[2 * n:]):
            o_ref[...] = (p_ref[...].astype(F32) + r_ref[...].astype(F32)).astype(o_ref.dtype)

    mine = [pl.BlockSpec((None,) + p.shape[1:], lambda q, core_ref: (2 * q + core_ref[0], 0, 0)) for p in parts]
    other = [pl.BlockSpec((None,) + p.shape[1:], lambda q, core_ref: (q, 0, 0)) for p in parts]
    return pl.pallas_call(
        body, name=name,
        grid_spec=pltpu.PrefetchScalarGridSpec(num_scalar_prefetch=1, grid=(N_CHIP,), in_specs=mine + other,
                                               out_specs=other),
        out_shape=[jax.ShapeDtypeStruct((N_CHIP,) + p.shape[1:], p.dtype) for p in parts],
        compiler_params=_params(1))(core, *parts, *recvs)


def _adam_math(wv, g, m, v):
    m = ADAM_B1 * m + (1.0 - ADAM_B1) * g
    v = ADAM_B2 * v + (1.0 - ADAM_B2) * (g * g)
    m_hat = m / (1.0 - ADAM_B1 ** ADAM_STEP)
    v_hat = v / (1.0 - ADAM_B2 ** ADAM_STEP)
    delta = -ADAM_LR * (m_hat / (jnp.sqrt(v_hat) + ADAM_EPS) + ADAM_WD * wv)
    return delta, m, v


def _adam_sharded(wv, m, v, own, recv, chip, *, name):
    a, b = wv.shape
    tm = _row_tile(a)

    def body(chip_ref, w_ref, m_ref, v_ref, own_ref, recv_ref, g_ref, d_ref, nm_ref, nv_ref):
        g = own_ref[...].astype(F32)
        for j in range(3):
            g = g + recv_ref[j].astype(F32)
        delta, nm, nv = _adam_math(w_ref[...], g, m_ref[...], v_ref[...])
        g_ref[...] = g
        d_ref[...] = delta
        nm_ref[...] = nm
        nv_ref[...] = nv

    tile = pl.BlockSpec((tm, b), lambda i, chip_ref: (i, 0))
    return pl.pallas_call(
        body, name=name,
        grid_spec=pltpu.PrefetchScalarGridSpec(
            num_scalar_prefetch=1, grid=(a // tm,),
            in_specs=[tile, tile, tile,
                      pl.BlockSpec((None, tm, b), lambda i, chip_ref: (chip_ref[0], i, 0)),
                      pl.BlockSpec((3, tm, b), lambda i, chip_ref: (0, i, 0))],
            out_specs=[tile] * 4),
        out_shape=[jax.ShapeDtypeStruct((a, b), F32)] * 4, compiler_params=_params(1))(chip, wv, m, v, own, recv)


def _sum_devices(gathered, *, name):
    _, r, c = gathered.shape

    def body(g_ref, o_ref):
        total = g_ref[0]
        for j in range(1, N_DEV):
            total = total + g_ref[j]
        o_ref[...] = total

    return pl.pallas_call(body, name=name, out_shape=jax.ShapeDtypeStruct((r, c), F32))(gathered)


def _adam_small(wv, g, m, v, *, name):
    def body(w_ref, g_ref, m_ref, v_ref, d_ref, nm_ref, nv_ref):
        delta, nm, nv = _adam_math(w_ref[...], g_ref[...], m_ref[...], v_ref[...])
        d_ref[...] = delta
        nm_ref[...] = nm
        nv_ref[...] = nv

    return pl.pallas_call(body, name=name, out_shape=[jax.ShapeDtypeStruct(wv.shape, F32)] * 3)(wv, g, m, v)


BIG = ("in", "a", "b", "mix", "mq", "kv", "mo", "fi", "fo")
ROW_SHARDED = ("mix", "mq", "mo")
UNSHARDED = ("a", "b")
FFN_GROUPS = 4
IN_SPLIT = (3, 4)
SMALL_ROWS = 16


class _Plan:
    FUSED = ("in",)
    GATHER_ON = {"sb_fwd": ("a", "b", "mix", "kv", "mq", "mo", "conv", "fi0"), "mm_mix": ("fo",),
                 "mem_sublayer": ("fi1",)}
    SIBLING_ON = {"mm_d_hf": ("fo", "fi"), "mm_d_merged": ("mo", "mq", "kv"), "conv_bwd": ("mix", "a", "b"),
                  "mm_d_w_in1": ("in0",), "mm_d_h0": ("in1",)}
    CHIPS_ON = {"mem_sublayer_bwd": ("fo",), "sb_bwd": ("fi", "mo", "mq", "kv", "mix", "a", "b"), "mm_d_h0": ("in0",),
                "rms_mix_bwd": ("in1",)}

    def __init__(self, shards, core):
        self.shards, self.core = shards, core
        self.w, self.parts, self.chip_sums, self.from_chips = {}, {}, {}, {}

    def gathering(self, k):
        return self.shards[k] if k in self.FUSED else None

    def comm(self, name):
        comms = []
        if name in self.GATHER_ON:
            comms.append(_gather_comm([self.shards[k] for k in self.GATHER_ON[name]]))
        if name in self.SIBLING_ON:
            comms.append(_sibling_comm([self.parts[k] for k in self.SIBLING_ON[name]]))
        if name in self.CHIPS_ON:
            comms.append(_chips_comm([self.chip_sums[k] for k in self.CHIPS_ON[name]]))
        return _join_comms(comms) if comms else None

    def landed(self, name, outs):
        outs = list(outs)
        for k in self.GATHER_ON.get(name, ()):
            self.set_weight(k, outs.pop(0))
        keys = self.SIBLING_ON.get(name, ())
        if keys:
            sums = _sum_with_sibling([self.parts[k] for k in keys], [outs.pop(0) for _ in keys], self.core,
                                     name="sum_with_sibling_" + "_".join(keys))
            self.chip_sums.update(zip(keys, sums))
        for k in self.CHIPS_ON.get(name, ()):
            self.from_chips[k] = outs.pop(0)

    def set_weight(self, k, gathered):
        _, a, b = gathered.shape
        if k in ROW_SHARDED:
            gathered = gathered.reshape(1, N_DEV * a, b)
        elif k in UNSHARDED:
            gathered = jnp.transpose(gathered, (1, 0, 2)).reshape(1, a, N_DEV * b)
        elif k == "fo":
            gathered = gathered.reshape(FFN_GROUPS, N_DEV * a // FFN_GROUPS, b)
        elif k == "conv":
            n_conv = CONV_WIDTH // N_DEV
            gathered = jnp.transpose(gathered[:, :3, :n_conv], (1, 0, 2)).reshape(3, CONV_WIDTH)
        self.w[k] = gathered
        if k == "fi1":
            self.w["fi"] = jnp.concatenate([self.w["fi0"], gathered], axis=2)

    def weight(self, k):
        return self.w[k]

    def grad(self, k, g):
        _, a, b = g.shape
        if k in ROW_SHARDED:
            g = g.reshape(N_DEV, a // N_DEV, b)
        elif k in UNSHARDED:
            g = jnp.transpose(g.reshape(a, N_DEV, b // N_DEV), (1, 0, 2))
        elif k == "fo":
            g = g.reshape(N_DEV, FFN_GROUPS * a // N_DEV, b)
        self.parts[k] = g


def kernel(x, mem, norm_mix, w_in, conv_w, w_branch_a, w_branch_b, w_mix_out, norm_mem_q, norm_mem_kv, w_mem_q, w_mem_kv, w_mem_o, norm_ffn, w_ffn_in, w_ffn_out, norm_final, loss_target, m_norm_mix, m_w_in, m_conv_w, m_w_branch_a, m_w_branch_b, m_w_mix_out, m_norm_mem_q, m_norm_mem_kv, m_w_mem_q, m_w_mem_kv, m_w_mem_o, m_norm_ffn, m_w_ffn_in, m_w_ffn_out, m_norm_final, v_norm_mix, v_w_in, v_conv_w, v_w_branch_a, v_w_branch_b, v_w_mix_out, v_norm_mem_q, v_norm_mem_kv, v_w_mem_q, v_w_mem_kv, v_w_mem_o, v_norm_ffn, v_w_ffn_in, v_w_ffn_out, v_norm_final):
    d = x.shape[-1]
    xi, yi, ci = lax.axis_index("x"), lax.axis_index("y"), lax.axis_index("c")
    chip = jnp.reshape(2 * xi + yi, (1,)).astype(jnp.int32)
    dev = 4 * xi + 2 * yi + ci

    big_w = dict(zip(BIG, (w_in, w_branch_a, w_branch_b, w_mix_out, w_mem_q, w_mem_kv, w_mem_o, w_ffn_in, w_ffn_out)))
    big_m = dict(zip(BIG, (m_w_in, m_w_branch_a, m_w_branch_b, m_w_mix_out, m_w_mem_q, m_w_mem_kv, m_w_mem_o, m_w_ffn_in, m_w_ffn_out)))
    big_v = dict(zip(BIG, (v_w_in, v_w_branch_a, v_w_branch_b, v_w_mix_out, v_w_mem_q, v_w_mem_kv, v_w_mem_o, v_w_ffn_in, v_w_ffn_out)))

    flip = lambda t, k: jnp.transpose(t) if k == "fi" else t
    shards = {k: flip(big_w[k][0], k).astype(BF16) for k in BIG}
    shards["fi0"], shards["fi1"] = shards["fi"][:, :d // 2], shards["fi"][:, d // 2:]
    n_conv = conv_w.shape[-1]
    shards["conv"] = jnp.zeros((8, LANES), F32).at[:3, :n_conv].set(conv_w[0])
    plan = _Plan(shards, jnp.reshape(ci, (1,)).astype(jnp.int32))

    gains = (norm_mix, norm_mem_q, norm_mem_kv, norm_ffn, norm_final.reshape(1, d))
    dx0, small = _local_step(x[0], mem[0], loss_target[0], gains, plan)

    grads, deltas, new_m, new_v = {}, {}, {}, {}
    for k in BIG:
        lead = big_w[k].shape
        wv, mv, vv = flip(big_w[k][0], k), flip(big_m[k][0], k), flip(big_v[k][0], k)
        if k == "in":
            half = wv.shape[0] * IN_SPLIT[0] // IN_SPLIT[1]
            lo = _adam_sharded(wv[:half], mv[:half], vv[:half], plan.chip_sums["in0"], plan.from_chips["in0"], chip,
                               name="adam_in0")
            hi = _adam_sharded(wv[half:], mv[half:], vv[half:], plan.chip_sums["in1"], plan.from_chips["in1"], chip,
                               name="adam_in1")
            outs = [jnp.concatenate(pair, axis=0) for pair in zip(lo, hi)]
        else:
            outs = _adam_sharded(wv, mv, vv, plan.chip_sums[k], plan.from_chips[k], chip, name="adam_" + k)
        grads[k], deltas[k], new_m[k], new_v[k] = (flip(t, k).reshape(lead) for t in outs)

    dg_mix, dg_memq, dg_memkv, dg_ffn, dg_fin, dconv_w, loss = small
    conv_rows = jnp.zeros((3, d), F32).at[:, :CONV_WIDTH].set(dconv_w[:3])
    block = jnp.concatenate([dg_mix[:1], dg_memq[:1], dg_memkv[:1], dg_ffn[:1], dg_fin[:1], conv_rows,
                             jnp.broadcast_to(loss[:1, :1], (1, d)), jnp.zeros((SMALL_ROWS - 9, d), F32)], axis=0)
    total = _sum_devices(_exchange(_gather_comm([block]), name="gather_small")[0], name="sum_small")
    g_conv = lax.dynamic_slice(total[5:8, :CONV_WIDTH], (0, dev * n_conv), (3, n_conv))
    small_w = [norm_mix, norm_mem_q, norm_mem_kv, norm_ffn, norm_final.reshape(1, d), conv_w[0]]
    small_m = [m_norm_mix, m_norm_mem_q, m_norm_mem_kv, m_norm_ffn, m_norm_final.reshape(1, d), m_conv_w[0]]
    small_v = [v_norm_mix, v_norm_mem_q, v_norm_mem_kv, v_norm_ffn, v_norm_final.reshape(1, d), v_conv_w[0]]
    small_g = [total[0:1], total[1:2], total[2:3], total[3:4], total[4:5], g_conv]
    small_names = ["norm_mix", "norm_mem_q", "norm_mem_kv", "norm_ffn", "norm_final", "conv_w"]
    sg, sd, sm, sv = {}, {}, {}, {}
    for nme, wv, g, m, v in zip(small_names, small_w, small_g, small_m, small_v):
        dl, nm, nv = _adam_small(wv, g, m, v, name="adam_" + nme)
        shape = norm_final.shape if nme == "norm_final" else (conv_w.shape if nme == "conv_w" else wv.shape)
        sg[nme], sd[nme], sm[nme], sv[nme] = (t.reshape(shape) for t in (g, dl, nm, nv))

    def ordered(big, sml):
        return (sml["norm_mix"], big["in"], sml["conv_w"], big["a"], big["b"], big["mix"], sml["norm_mem_q"],
                sml["norm_mem_kv"], big["mq"], big["kv"], big["mo"], sml["norm_ffn"], big["fi"], big["fo"],
                sml["norm_final"])

    loss_out = total[8, 0]
    grad_x = dx0.reshape(x.shape)
    return (loss_out, grad_x, *ordered(grads, sg), *ordered(deltas, sd), *ordered(new_m, sm), *ordered(new_v, sv))
```

```python
import functools
import math

import jax
import jax.numpy as jnp
from jax import lax
from jax.experimental import pallas as pl
from jax.experimental.pallas import tpu as pltpu

F32 = jnp.float32
BF16 = jnp.bfloat16
MESH = pl.DeviceIdType.MESH

N_DEV = 8
N_CHIP = 4
NORM_EPS = 1e-6
SB_HEADS = 8
SB_HEAD_DIM = 64
SB_WIDTH = SB_HEADS * SB_HEAD_DIM
CONV_WIDTH = 512
MEM_HEADS = 4
ADAM_LR = 0.001
ADAM_B1 = 0.9
ADAM_B2 = 0.999
ADAM_EPS = 1e-08
ADAM_WD = 0.01
ADAM_STEP = 10

LANES = 128
VMEM_LIMIT_BYTES = 52 * 1024 * 1024
SB_TILE = 256
SB_DEAD = 159.0
SB_CLAMP = 126.0
LOG2_E = 1.4426950408889634

ANY = pl.BlockSpec(memory_space=pl.ANY)


def _params(n_grid):
    return pltpu.CompilerParams(dimension_semantics=("arbitrary",) * n_grid, vmem_limit_bytes=VMEM_LIMIT_BYTES)


def _bdot(a, b, dims):
    return lax.dot_general(a.astype(BF16), b.astype(BF16), (dims, ((), ())), preferred_element_type=F32)


NN = ((1,), (0,))
NT = ((1,), (1,))
TN = ((0,), (0,))


class _Comm:
    def __init__(self, ins, outs, n_sems, start, finish):
        self.ins, self.outs, self.n_sems, self.start, self.finish = ins, outs, n_sems, start, finish

    def sem_shapes(self):
        return [pltpu.SemaphoreType.DMA((k,)) for k in self.n_sems]


def _place():
    return lax.axis_index("x"), lax.axis_index("y"), lax.axis_index("c")


def _gather_comm(shards):
    n = len(shards)

    def copies(ins, outs, sems):
        send_sems, recv_sems, _ = sems
        x, y, c = _place()
        chips = [(1 - x, y), (x, 1 - y), (1 - x, 1 - y)]

        def copy(a, k, block, to, from_shard=False):
            dst = outs[a].at[4 * block[0] + 2 * block[1] + block[2]]
            return pltpu.make_async_remote_copy(
                src_ref=ins[a] if from_shard else dst, dst_ref=dst, send_sem=send_sems.at[a * 7 + k],
                recv_sem=recv_sems.at[a * 7 + k], device_id=to, device_id_type=MESH)

        me, sibling = (x, y, c), (x, y, 1 - c)
        own = [[copy(a, 0, me, sibling, True)] + [copy(a, 1 + j, me, (*chip, c), True) for j, chip in enumerate(chips)]
               for a in range(n)]
        landed = [[copy(a, 1 + j, (*chip, c), me) for j, chip in enumerate(chips)] for a in range(n)]
        passed = [[copy(a, 4 + j, (*chip, c), sibling) for j, chip in enumerate(chips)] for a in range(n)]
        from_sibling = [[copy(a, 0, sibling, me)] + [copy(a, 4 + j, (*chip, 1 - c), me) for j, chip in enumerate(chips)]
                        for a in range(n)]
        local = [pltpu.make_async_copy(ins[a], outs[a].at[4 * x + 2 * y + c], sems[2].at[a]) for a in range(n)]
        return own, landed, passed, from_sibling, local

    def start(ins, outs, sems):
        own, _, _, _, local = copies(ins, outs, sems)
        for a in range(n):
            local[a].start()
            for cp in own[a]:
                cp.start()

    def finish(ins, outs, sems):
        own, landed, passed, from_sibling, local = copies(ins, outs, sems)
        for a in range(n):
            for arrived, onward in zip(landed[a], passed[a]):
                arrived.wait_recv()
                onward.start()
        for a in range(n):
            for cp in from_sibling[a]:
                cp.wait_recv()
        for a in range(n):
            for cp in own[a] + passed[a]:
                cp.wait_send()
            local[a].wait()

    outs = [jax.ShapeDtypeStruct((N_DEV,) + s.shape, s.dtype) for s in shards]
    return _Comm(list(shards), outs, (7 * n, 7 * n, n), start, finish)


def _sibling_comm(parts):
    n = len(parts)

    def copies(ins, outs, sems):
        x, y, c = _place()
        return [pltpu.make_async_remote_copy(
            src_ref=ins[a].at[2 * q + 1 - c], dst_ref=outs[a].at[q], send_sem=sems[0].at[a * N_CHIP + q],
            recv_sem=sems[1].at[a * N_CHIP + q], device_id=(x, y, 1 - c), device_id_type=MESH)
            for a in range(n) for q in range(N_CHIP)]

    def start(ins, outs, sems):
        for cp in copies(ins, outs, sems):
            cp.start()

    def finish(ins, outs, sems):
        cps = copies(ins, outs, sems)
        for cp in cps:
            cp.wait_recv()
        for cp in cps:
            cp.wait_send()

    outs = [jax.ShapeDtypeStruct((N_CHIP,) + p.shape[1:], p.dtype) for p in parts]
    return _Comm(list(parts), outs, (N_CHIP * n, N_CHIP * n), start, finish)


def _chips_comm(parts):
    n = len(parts)

    def copies(ins, outs, sems):
        x, y, c = _place()
        chips = [(1 - x, y), (x, 1 - y), (1 - x, 1 - y)]
        return [pltpu.make_async_remote_copy(
            src_ref=ins[a].at[2 * px + py], dst_ref=outs[a].at[j], send_sem=sems[0].at[a * 3 + j],
            recv_sem=sems[1].at[a * 3 + j], device_id=(px, py, c), device_id_type=MESH)
            for a in range(n) for j, (px, py) in enumerate(chips)]

    def start(ins, outs, sems):
        for cp in copies(ins, outs, sems):
            cp.start()

    def finish(ins, outs, sems):
        cps = copies(ins, outs, sems)
        for cp in cps:
            cp.wait_recv()
        for cp in cps:
            cp.wait_send()

    outs = [jax.ShapeDtypeStruct((3,) + p.shape[1:], p.dtype) for p in parts]
    return _Comm(list(parts), outs, (3 * n, 3 * n), start, finish)


def _join_comms(comms):
    if len(comms) == 1:
        return comms[0]

    def split(refs, counts):
        out, at = [], 0
        for n in counts:
            out.append(refs[at:at + n])
            at += n
        return out

    def each(method):
        def run(ins, outs, sems):
            parts = zip(comms, split(ins, [len(c.ins) for c in comms]), split(outs, [len(c.outs) for c in comms]),
                        split(sems, [len(c.n_sems) for c in comms]))
            for c, c_ins, c_outs, c_sems in parts:
                getattr(c, method)(c_ins, c_outs, c_sems)
        return run

    return _Comm([a for c in comms for a in c.ins], [o for c in comms for o in c.outs],
                 tuple(k for c in comms for k in c.n_sems), each("start"), each("finish"))


def _exchange(comm, *, name):
    n_ci, n_co = len(comm.ins), len(comm.outs)

    def kern(*refs):
        c_ins, c_outs, sems = refs[:n_ci], refs[n_ci:n_ci + n_co], refs[n_ci + n_co:]
        comm.start(c_ins, c_outs, sems)
        comm.finish(c_ins, c_outs, sems)

    return pl.pallas_call(kern, name=name, in_specs=[ANY] * n_ci, out_specs=[ANY] * n_co, out_shape=comm.outs,
                          scratch_shapes=comm.sem_shapes())(*comm.ins)


def _call(body, *, name, grid, in_specs, out_specs, out_shape, scratch, args, plan=None):
    comm = plan.comm(name) if plan is not None else None
    if comm is None:
        return list(pl.pallas_call(functools.partial(body), name=name, grid=grid, in_specs=in_specs,
                                   out_specs=out_specs, out_shape=out_shape, scratch_shapes=scratch,
                                   compiler_params=_params(len(grid)))(*args))
    n_in, n_out, n_scr, n_ci, n_co = len(in_specs), len(out_specs), len(scratch), len(comm.ins), len(comm.outs)

    def kern(*refs):
        ins, c_ins, refs = refs[:n_in], refs[n_in:n_in + n_ci], refs[n_in + n_ci:]
        outs, c_outs, refs = refs[:n_out], refs[n_out:n_out + n_co], refs[n_out + n_co:]
        scr, sems = refs[:n_scr], refs[n_scr:]
        ids = [pl.program_id(ax) for ax in range(len(grid))]
        first = functools.reduce(jnp.logical_and, [i == 0 for i in ids])
        last = functools.reduce(jnp.logical_and, [i == g - 1 for i, g in zip(ids, grid)])

        @pl.when(first)
        def _():
            comm.start(c_ins, c_outs, sems)
        body(*ins, *outs, *scr)

        @pl.when(last)
        def _():
            comm.finish(c_ins, c_outs, sems)

    res = pl.pallas_call(kern, name=name, grid=grid, in_specs=list(in_specs) + [ANY] * n_ci,
                         out_specs=list(out_specs) + [ANY] * n_co, out_shape=list(out_shape) + comm.outs,
                         scratch_shapes=list(scratch) + comm.sem_shapes(),
                         compiler_params=_params(len(grid)))(*args, *comm.ins)
    plan.landed(name, list(res[n_out:]))
    return list(res[:n_out])


def _mm_body(dims, has_add, *refs):
    if has_add:
        a_ref, b_ref, add_ref, o_ref = refs
        total = _bdot(a_ref[...], b_ref[...], dims) + add_ref[...]
    else:
        a_ref, b_ref, o_ref = refs
        total = _bdot(a_ref[...], b_ref[...], dims)
    o_ref[...] = total.astype(o_ref.dtype)


def _mm_nt_body(j, n, dy_ref, w_ref, o_ref):
    total = _bdot(dy_ref[:, 0:n], w_ref[0], NT)
    for jj in range(1, j):
        total = total + _bdot(dy_ref[:, jj * n:(jj + 1) * n], w_ref[jj], NT)
    o_ref[...] = total.astype(o_ref.dtype)


def _mm_nn(a, w3, *, name, out_dtype=BF16, add=None, tm=1024, tn=None, out3=False, w_t=False, plan=None):
    m, kk = a.shape
    j, n = w3.shape[0], w3.shape[1 if w_t else 2]
    tm, tn = min(tm, m), n if tn is None else tn
    n_t = n // tn
    in_specs = [pl.BlockSpec((tm, kk), lambda i, jj: (i, 0)),
                pl.BlockSpec((None, tn, kk), lambda i, jj: (jj // n_t, jj % n_t, 0)) if w_t else
                pl.BlockSpec((None, kk, tn), lambda i, jj: (jj // n_t, 0, jj % n_t))]
    args = [a, w3]
    if add is not None:
        in_specs.append(pl.BlockSpec((tm, tn), lambda i, jj: (i, jj)))
        args.append(add)
    if out3:
        out_spec = pl.BlockSpec((None, tm, tn), lambda i, jj: (jj // n_t, i, jj % n_t))
        out_shape = jax.ShapeDtypeStruct((j, m, n), out_dtype)
    else:
        out_spec = pl.BlockSpec((tm, tn), lambda i, jj: (i, jj))
        out_shape = jax.ShapeDtypeStruct((m, j * n), out_dtype)
    return _call(
        functools.partial(_mm_body, NT if w_t else NN, add is not None), name=name, grid=(m // tm, j * n_t),
        in_specs=in_specs, out_specs=[out_spec], out_shape=[out_shape], scratch=[], args=args, plan=plan)[0]


def _mm_gathering(a, shard, *, name, out3=False, w_t=False, tm=1024):
    m, kk = a.shape
    n = shard.shape[0 if w_t else 1]
    tm = min(tm, m)
    n_i = m // tm
    fetch_at = min(1, n_i - 1)

    def body(a_ref, shard_ref, o_ref, w_all, w_vmem, send_sems, recv_sems, copy_sems):
        jj, i = pl.program_id(0), pl.program_id(1)
        x, y, c = _place()
        me, sibling = (x, y, c), (x, y, 1 - c)
        chips = [(jnp.bitwise_xor(x, c), jnp.bitwise_xor(y, 1 - c)), (jnp.bitwise_xor(x, 1 - c), jnp.bitwise_xor(y, c)),
                 (1 - x, 1 - y)]
        sibling_chips = [chips[1], chips[0], chips[2]]

        def rows(block):
            return w_all.at[4 * block[0] + 2 * block[1] + block[2]]

        def remote(k, block, to, from_shard=False):
            return pltpu.make_async_remote_copy(
                src_ref=shard_ref if from_shard else rows(block), dst_ref=rows(block), send_sem=send_sems.at[k],
                recv_sem=recv_sems.at[k], device_id=to, device_id_type=MESH)

        def load(step, src):
            return pltpu.make_async_copy(src, w_vmem.at[step % 2], copy_sems.at[1 + step % 2])

        own = [remote(0, me, sibling, True)] + [remote(1 + j, me, (*chip, c), True) for j, chip in enumerate(chips)]
        passed = [remote(4 + j, (*chip, c), sibling) for j, chip in enumerate(chips)]
        local = pltpu.make_async_copy(shard_ref, rows(me), copy_sems.at[0])

        @pl.when(jnp.logical_and(i == 0, jj == 0))
        def _():
            local.start()
            own[0].start()
            own[1].start()
            load(0, shard_ref).start()

        def arrivals():
            yield 1, (lambda: remote(0, sibling, me).wait_recv()), sibling
            for j, chip in enumerate(chips):
                def landed(j=j, chip=chip):
                    if j < 2:
                        own[1 + j].wait_send()
                        own[2 + j].start()
                    remote(1 + j, (*chip, c), me).wait_recv()
                    passed[j].start()
                yield 2 + 2 * j, landed, (*chip, c)
                block = (*sibling_chips[j], 1 - c)
                yield 3 + 2 * j, (lambda j=j, block=block: remote(4 + j, block, me).wait_recv()), block

        for step, wait_for_it, block in arrivals():
            @pl.when(jnp.logical_and(i == fetch_at, jj == step - 1))
            def _():
                wait_for_it()
                load(step, rows(block)).start()

        for step in range(N_DEV):
            @pl.when(jnp.logical_and(i == 0, jj == step))
            def _():
                load(step, rows(me)).wait()

        o_ref[...] = _bdot(a_ref[...], w_vmem[lax.rem(jj, 2)], NT if w_t else NN).astype(o_ref.dtype)

        @pl.when(jnp.logical_and(i == n_i - 1, jj == N_DEV - 1))
        def _():
            for cp in [own[0], own[3]] + passed:
                cp.wait_send()
            local.wait()

    def swept(jj):
        x, y, c = _place()
        first, second = 2 + 2 * c, 4 - 2 * c
        flips = (0b000, 0b001, first, second + 1, second, first + 1, 0b110, 0b111)
        return jnp.bitwise_xor(4 * x + 2 * y + c, sum(jnp.where(jj == k, f, 0) for k, f in enumerate(flips)))

    if out3:
        out_spec = pl.BlockSpec((None, tm, n), lambda jj, i: (swept(jj), i, 0))
        out_shape = jax.ShapeDtypeStruct((N_DEV, m, n), BF16)
    else:
        out_spec = pl.BlockSpec((tm, n), lambda jj, i: (i, swept(jj)))
        out_shape = jax.ShapeDtypeStruct((m, N_DEV * n), BF16)
    return pl.pallas_call(
        body, name=name, grid=(N_DEV, n_i),
        in_specs=[pl.BlockSpec((tm, kk), lambda jj, i: (i, 0)), ANY], out_specs=[out_spec, ANY],
        scratch_shapes=[pltpu.VMEM((2,) + shard.shape, shard.dtype), pltpu.SemaphoreType.DMA((7,)),
                        pltpu.SemaphoreType.DMA((7,)), pltpu.SemaphoreType.DMA((3,))],
        out_shape=[out_shape, jax.ShapeDtypeStruct((N_DEV,) + shard.shape, shard.dtype)],
        compiler_params=_params(2))(a, shard)


def _sigmoid(v):
    return 0.5 * jnp.tanh(0.5 * v) + 0.5


def _resident(w):
    return pl.BlockSpec(w.shape, lambda i: (0,) * w.ndim, pipeline_mode=pl.Buffered(1))


def _ffn_out_loss(gu3, w3, add, g, target, *, name, tm=512):
    j2, m, n = gu3.shape
    j = j2 // 2
    nn = w3.shape[2]
    tm = min(tm, m)

    def body(gu_ref, w_ref, add_ref, g_ref, t_ref, dx_ref, dxb_ref, dg_ref, loss_ref, act_ref):
        i = pl.program_id(0)
        xv = add_ref[...]
        for jj in range(j):
            gate = gu_ref[0, jj].astype(F32)
            act = (gate * _sigmoid(gate) * gu_ref[1, jj].astype(F32)).astype(BF16)
            act_ref[jj] = act
            xv = xv + _bdot(act, w_ref[jj], NN)
        gv = g_ref[...]
        r = lax.rsqrt(jnp.mean(xv * xv, axis=-1, keepdims=True) + NORM_EPS)
        xhat = xv * r
        err = xhat * gv - t_ref[...]
        _acc_rows(i, loss_ref, 0.5 * jnp.sum(jnp.mean(err * err, axis=-1, keepdims=True), axis=0, keepdims=True))
        dy = err * (1.0 / nn)
        dxhat = dy * gv
        dx = r * (dxhat - xhat * jnp.mean(dxhat * xhat, axis=-1, keepdims=True))
        dx_ref[...] = dx
        dxb_ref[...] = dx.astype(BF16)
        _acc_rows(i, dg_ref, jnp.sum(dy * xhat, axis=0, keepdims=True))

    row = pl.BlockSpec((tm, nn), lambda i: (i, 0))
    return _call(body, name=name, grid=(m // tm,),
                 in_specs=[pl.BlockSpec((2, j, tm, n), lambda i: (0, 0, i, 0)), _resident(w3),
                           row, pl.BlockSpec(g.shape, lambda i: (0, 0)), row],
                 out_specs=[row, row, pl.BlockSpec((8, nn), lambda i: (0, 0)), pl.BlockSpec((8, LANES), lambda i: (0, 0)),
                            pl.BlockSpec((j, tm, n), lambda i: (0, i, 0))],
                 out_shape=[jax.ShapeDtypeStruct((m, nn), F32), jax.ShapeDtypeStruct((m, nn), BF16),
                            jax.ShapeDtypeStruct((8, nn), F32), jax.ShapeDtypeStruct((8, LANES), F32),
                            jax.ShapeDtypeStruct((j, m, n), BF16)],
                 scratch=[], args=[gu3.reshape(2, j, m, n), w3, add, g, target])


def _ffn_out_bwd(dy, w3, gu3, *, name, tm=1024):
    m, nn = dy.shape
    j, n, _ = w3.shape
    tm = min(tm, m)

    def body(dy_ref, w_ref, gu_ref, dgu_ref):
        da = _bdot(dy_ref[...], w_ref[...], NT)
        gate = gu_ref[0].astype(F32)
        up = gu_ref[1].astype(F32)
        sg = _sigmoid(gate)
        silu = gate * sg
        dgu_ref[0] = (da * up * (sg + silu * (1.0 - sg))).astype(BF16)
        dgu_ref[1] = (da * silu).astype(BF16)

    out = _call(body, name=name, grid=(m // tm, j),
                in_specs=[pl.BlockSpec((tm, nn), lambda i, jj: (i, 0)),
                          pl.BlockSpec((None, n, nn), lambda i, jj: (jj, 0, 0)),
                          pl.BlockSpec((2, None, tm, n), lambda i, jj: (0, jj, i, 0))],
                out_specs=[pl.BlockSpec((2, None, tm, n), lambda i, jj: (0, jj, i, 0))],
                out_shape=[jax.ShapeDtypeStruct((2, j, m, n), BF16)], scratch=[],
                args=[dy, w3, gu3.reshape(2, j, m, n)])[0]
    return out.reshape(2 * j, m, n)


def _rms_fwd_tail(xv, g_ref, h_ref):
    r = lax.rsqrt(jnp.mean(xv * xv, axis=-1, keepdims=True) + NORM_EPS)
    h_ref[...] = (xv * r * g_ref[...]).astype(BF16)


def _rms_bwd_tail(i, dh, x_ref, g_ref, dres_ref, dx_ref, dxb_ref, dg_ref):
    xv = x_ref[...]
    r = lax.rsqrt(jnp.mean(xv * xv, axis=-1, keepdims=True) + NORM_EPS)
    xhat = xv * r
    dxhat = dh * g_ref[...]
    dx = r * (dxhat - xhat * jnp.mean(dxhat * xhat, axis=-1, keepdims=True))
    if dres_ref is not None:
        dx = dx + dres_ref[...]
    dx_ref[...] = dx
    dxb_ref[...] = dx.astype(BF16)
    _acc_rows(i, dg_ref, jnp.sum(dh * xhat, axis=0, keepdims=True))


def _mm_nt_rms(dy, w3, x, g, dres, *, name, dy3=False, w_nn=False, tm=512, plan=None):
    j = w3.shape[0]
    m, kk = x.shape
    n = dy.shape[2] if dy3 else dy.shape[1] // j
    tm = min(tm, m)

    def body(dy_ref, w_ref, x_ref, g_ref, *rest):
        dres_ref = rest[0] if dres is not None else None
        dx_ref, dxb_ref, dg_ref = rest[-3:]
        dh = None
        for jj in range(j):
            piece = dy_ref[jj] if dy3 else dy_ref[:, jj * n:(jj + 1) * n]
            part = _bdot(piece, w_ref[jj], NN if w_nn else NT)
            dh = part if dh is None else dh + part
        _rms_bwd_tail(pl.program_id(0), dh, x_ref, g_ref, dres_ref, dx_ref, dxb_ref, dg_ref)

    row = pl.BlockSpec((tm, kk), lambda i: (i, 0))
    in_specs = [pl.BlockSpec((j, tm, n), lambda i: (0, i, 0)) if dy3 else pl.BlockSpec((tm, j * n), lambda i: (i, 0)),
                _resident(w3), row, pl.BlockSpec(g.shape, lambda i: (0, 0))]
    args = [dy, w3, x, g]
    if dres is not None:
        in_specs.append(row)
        args.append(dres)
    return _call(body, name=name, grid=(m // tm,), in_specs=in_specs,
                 out_specs=[row, row, pl.BlockSpec((8, kk), lambda i: (0, 0))],
                 out_shape=[jax.ShapeDtypeStruct((m, kk), F32), jax.ShapeDtypeStruct((m, kk), BF16),
                            jax.ShapeDtypeStruct((8, kk), F32)], scratch=[], args=args, plan=plan)


def _mix_out(o_a, y_b, proj, w_a, w_b, w, x, g, *, name, tm=512, plan=None):
    s, c = o_a.shape
    d = w.shape[1]
    tm = min(tm, s)

    def body(oa_ref, yb_ref, ga_ref, gb_ref, wa_ref, wb_ref, w_ref, x_ref, g_ref, x1_ref, h_ref, merged_ref, a_ref, b_ref):
        a_ref[...] = _bdot(oa_ref[...], wa_ref[...], NN).astype(BF16)
        b_ref[...] = _bdot(yb_ref[...], wb_ref[...], NN).astype(BF16)
        merged = (_sigmoid(ga_ref[...].astype(F32)) * a_ref[...].astype(F32)
                  + _sigmoid(gb_ref[...].astype(F32)) * b_ref[...].astype(F32)).astype(BF16)
        merged_ref[...] = merged
        xv = _bdot(merged, w_ref[...], NN) + x_ref[...]
        x1_ref[...] = xv
        _rms_fwd_tail(xv, g_ref, h_ref)

    row = pl.BlockSpec((tm, d), lambda i: (i, 0))
    narrow = pl.BlockSpec((tm, c), lambda i: (i, 0))
    whole = lambda arr: pl.BlockSpec(arr.shape, lambda i: (0,) * arr.ndim)
    return _call(body, name=name, grid=(s // tm,),
                 in_specs=[narrow, narrow, pl.BlockSpec((tm, d), lambda i: (i, 3)), pl.BlockSpec((tm, d), lambda i: (i, 4)),
                           whole(w_a), whole(w_b), whole(w), row, whole(g)],
                 out_specs=[row] * 5,
                 out_shape=[jax.ShapeDtypeStruct((s, d), F32)] + [jax.ShapeDtypeStruct((s, d), BF16)] * 4,
                 scratch=[], args=[o_a, y_b, proj, proj, w_a, w_b, w, x, g], plan=plan)


def _mm_tn_a3(a3, dy, *, name):
    j, t, n = a3.shape
    nn = dy.shape[1]
    return _call(functools.partial(_mm_body, TN, False), name=name, grid=(j,),
                 in_specs=[pl.BlockSpec((None, t, n), lambda jj: (jj, 0, 0)), pl.BlockSpec((t, nn), lambda jj: (0, 0))],
                 out_specs=[pl.BlockSpec((None, n, nn), lambda jj: (jj, 0, 0))],
                 out_shape=[jax.ShapeDtypeStruct((j, n, nn), BF16)], scratch=[], args=[a3, dy])[0]


def _mm_nt(dy, w3, *, name, out_dtype=BF16, tm=512, tn=1024, plan=None):
    m = dy.shape[0]
    j, kk, n = w3.shape
    tm, tn = min(tm, m), min(tn, kk)
    return _call(
        functools.partial(_mm_nt_body, j, n), name=name,
        grid=(m // tm, kk // tn),
        in_specs=[pl.BlockSpec((tm, j * n), lambda i, q: (i, 0)),
                  pl.BlockSpec((j, tn, n), lambda i, q: (0, q, 0))],
        out_specs=[pl.BlockSpec((tm, tn), lambda i, q: (i, q))],
        out_shape=[jax.ShapeDtypeStruct((m, kk), out_dtype)], scratch=[], args=[dy, w3], plan=plan)[0]


def _mm_tn(a, dy, n, *, name, out_dtype=BF16, tm=512, tn=None, k_tiles=None, plan=None):
    t, kk = a.shape
    j = dy.shape[1] // n
    tm, tn = min(tm, kk), n if tn is None else tn
    n_t = n // tn
    first, count = (0, kk // tm) if k_tiles is None else k_tiles
    return _call(
        functools.partial(_mm_body, TN, False), name=name,
        grid=(count, j * n_t),
        in_specs=[pl.BlockSpec((t, tm), lambda i, jj: (0, first + i)),
                  pl.BlockSpec((t, tn), lambda i, jj: (0, jj))],
        out_specs=[pl.BlockSpec((None, tm, tn), lambda i, jj: (jj // n_t, i, jj % n_t))],
        out_shape=[jax.ShapeDtypeStruct((j, count * tm, n), out_dtype)], scratch=[], args=[a, dy], plan=plan)[0]


def _rows(body, ins, outs, *, n_rows, tm, name, plan=None):
    tm = min(tm, n_rows)
    n_steps = n_rows // tm
    in_specs, args = [], []
    for arr, kind, width, block in ins:
        if kind == "row":
            in_specs.append(pl.BlockSpec((tm, width), functools.partial(lambda i, b: (i, b), b=block)))
        elif kind == "prev":
            in_specs.append(pl.BlockSpec((tm, width), functools.partial(lambda i, b: (jnp.maximum(i - 1, 0), b), b=block)))
        elif kind == "next":
            in_specs.append(pl.BlockSpec((tm, width), functools.partial(lambda i, b: (jnp.minimum(i + 1, n_steps - 1), b), b=block)))
        else:
            in_specs.append(pl.BlockSpec(arr.shape, functools.partial(lambda i, nd: (0,) * nd, nd=arr.ndim)))
        args.append(arr)
    out_specs, out_shape = [], []
    for shape, dtype, kind in outs:
        if kind == "row":
            out_specs.append(pl.BlockSpec((tm, shape[1]), lambda i: (i, 0)))
        else:
            out_specs.append(pl.BlockSpec(shape, functools.partial(lambda i, nd: (0,) * nd, nd=len(shape))))
        out_shape.append(jax.ShapeDtypeStruct(shape, dtype))

    def kern(*refs):
        body(pl.program_id(0), n_steps, *refs)

    return _call(kern, name=name, grid=(n_steps,), in_specs=in_specs, out_specs=out_specs, out_shape=out_shape,
                 scratch=[], args=args, plan=plan)


def _acc_rows(i, ref, value):
    @pl.when(i == 0)
    def _():
        ref[...] = jnp.zeros_like(ref)
    ref[...] += jnp.broadcast_to(value, ref.shape)


def _rms_fwd(x, g, *, name, tm=512):
    s, d = x.shape

    def body(i, n, x_ref, g_ref, h_ref):
        _rms_fwd_tail(x_ref[...], g_ref, h_ref)

    return _rows(body, [(x, "row", d, 0), (g, "full", 0, 0)], [((s, d), BF16, "row")], n_rows=s, tm=tm, name=name)[0]


def _rms_bwd(x, g, dh, dres, *, name, tm=512, plan=None):
    s, d = x.shape

    def body(i, n, x_ref, g_ref, dh_ref, dres_ref, dx_ref, dxb_ref, dg_ref):
        _rms_bwd_tail(i, dh_ref[...].astype(F32), x_ref, g_ref, dres_ref, dx_ref, dxb_ref, dg_ref)

    return _rows(body, [(x, "row", d, 0), (g, "full", 0, 0), (dh, "row", d, 0), (dres, "row", d, 0)],
                 [((s, d), F32, "row"), ((s, d), BF16, "row"), ((8, d), F32, "acc")],
                 n_rows=s, tm=tm, name=name, plan=plan)


def _mix_out_bwd(dx1b, w, br_a, br_b, proj, w_a, w_b, *, name, tm=512, plan=None):
    s, d = br_a.shape
    c = w_a.shape[0]
    tm = min(tm, s)

    def body(dy_ref, w_ref, a_ref, b_ref, ga_ref, gb_ref, wa_ref, wb_ref, da_ref, db_ref, dg_ref, doa_ref, dyb_ref):
        dm = _bdot(dy_ref[...], w_ref[...], NT)
        sa = _sigmoid(ga_ref[...].astype(F32))
        sb = _sigmoid(gb_ref[...].astype(F32))
        da_ref[...] = (dm * sa).astype(BF16)
        db_ref[...] = (dm * sb).astype(BF16)
        dg_ref[:, :d] = (dm * a_ref[...].astype(F32) * sa * (1.0 - sa)).astype(BF16)
        dg_ref[:, d:] = (dm * b_ref[...].astype(F32) * sb * (1.0 - sb)).astype(BF16)
        doa_ref[...] = _bdot(da_ref[...], wa_ref[...], NT).astype(BF16)
        dyb_ref[...] = _bdot(db_ref[...], wb_ref[...], NT).astype(BF16)

    row = pl.BlockSpec((tm, d), lambda i: (i, 0))
    narrow = pl.BlockSpec((tm, c), lambda i: (i, 0))
    whole = lambda arr: pl.BlockSpec(arr.shape, lambda i: (0,) * arr.ndim)
    return _call(body, name=name, grid=(s // tm,),
                 in_specs=[row, whole(w), row, row, pl.BlockSpec((tm, d), lambda i: (i, 3)),
                           pl.BlockSpec((tm, d), lambda i: (i, 4)), whole(w_a), whole(w_b)],
                 out_specs=[row, row, pl.BlockSpec((tm, 2 * d), lambda i: (i, 0)), narrow, narrow],
                 out_shape=[jax.ShapeDtypeStruct((s, d), BF16), jax.ShapeDtypeStruct((s, d), BF16),
                            jax.ShapeDtypeStruct((s, 2 * d), BF16), jax.ShapeDtypeStruct((s, c), BF16),
                            jax.ShapeDtypeStruct((s, c), BF16)],
                 scratch=[], args=[dx1b, w, br_a, br_b, proj, proj, w_a, w_b], plan=plan)


def _shift_down(cur, prev, k, first):
    row = lax.broadcasted_iota(jnp.int32, cur.shape, 0)
    out = jnp.where(row >= k, pltpu.roll(cur, k, 0), pltpu.roll(prev, k, 0))
    return jnp.where(jnp.logical_and(first, row < k), 0.0, out)


def _shift_up(cur, nxt, k, last):
    tm = cur.shape[0]
    row = lax.broadcasted_iota(jnp.int32, cur.shape, 0)
    out = jnp.where(row < tm - k, pltpu.roll(cur, tm - k, 0), pltpu.roll(nxt, tm - k, 0))
    return jnp.where(jnp.logical_and(last, row >= tm - k), 0.0, out)


def _conv_fwd(proj, conv_w, *, name, tm=512):
    s = proj.shape[0]
    c = CONV_WIDTH

    def body(i, n, u_ref, gb_ref, gc_ref, up_ref, gcp_ref, w_ref, y_ref):
        cu = gc_ref[...].astype(F32) * u_ref[...].astype(F32)
        cup = gcp_ref[...].astype(F32) * up_ref[...].astype(F32)
        first = i == 0
        y = (w_ref[0:1, :] * _shift_down(cu, cup, 2, first) + w_ref[1:2, :] * _shift_down(cu, cup, 1, first)
             + w_ref[2:3, :] * cu)
        y_ref[...] = (gb_ref[...].astype(F32) * y).astype(BF16)

    return _rows(body, [(proj, "row", c, 3), (proj, "row", c, 4), (proj, "row", c, 5),
                        (proj, "prev", c, 3), (proj, "prev", c, 5), (conv_w, "full", 0, 0)],
                 [((s, c), BF16, "row")], n_rows=s, tm=tm, name=name)[0]


def _conv_bwd(dy_b, proj, conv_w, *, name, tm=512, plan=None):
    s = proj.shape[0]
    c = CONV_WIDTH

    def body(i, n, dy_ref, u_ref, gb_ref, gc_ref, up_ref, gcp_ref, dyn_ref, gbn_ref, w_ref, d_ref, dw_ref):
        first, last = i == 0, i == n - 1
        u = u_ref[...].astype(F32)
        gb = gb_ref[...].astype(F32)
        gc = gc_ref[...].astype(F32)
        cu = gc * u
        cup = gcp_ref[...].astype(F32) * up_ref[...].astype(F32)
        cu1 = _shift_down(cu, cup, 1, first)
        cu2 = _shift_down(cu, cup, 2, first)
        conv = w_ref[0:1, :] * cu2 + w_ref[1:2, :] * cu1 + w_ref[2:3, :] * cu
        dy = dy_ref[...].astype(F32)
        dyc = dy * gb
        dycn = dyn_ref[...].astype(F32) * gbn_ref[...].astype(F32)
        dcu = (w_ref[2:3, :] * dyc + w_ref[1:2, :] * _shift_up(dyc, dycn, 1, last)
               + w_ref[0:1, :] * _shift_up(dyc, dycn, 2, last))
        d_ref[:, 0:c] = (dcu * gc).astype(BF16)
        d_ref[:, c:2 * c] = (dy * conv).astype(BF16)
        d_ref[:, 2 * c:3 * c] = (dcu * u).astype(BF16)
        row = lax.broadcasted_iota(jnp.int32, (8, c), 0)
        dw = (jnp.where(row == 0, jnp.sum(dyc * cu2, axis=0, keepdims=True), 0.0)
              + jnp.where(row == 1, jnp.sum(dyc * cu1, axis=0, keepdims=True), 0.0)
              + jnp.where(row == 2, jnp.sum(dyc * cu, axis=0, keepdims=True), 0.0))

        @pl.when(first)
        def _():
            dw_ref[...] = jnp.zeros_like(dw_ref)
        dw_ref[...] += dw

    return _rows(body, [(dy_b, "row", c, 0), (proj, "row", c, 3), (proj, "row", c, 4), (proj, "row", c, 5),
                        (proj, "prev", c, 3), (proj, "prev", c, 5), (dy_b, "next", c, 0), (proj, "next", c, 4),
                        (conv_w, "full", 0, 0)],
                 [((s, 3 * c), BF16, "row"), ((8, c), F32, "acc")], n_rows=s, tm=tm, name=name, plan=plan)


def _mem_probs(q, k, scale):
    sc = _bdot(q, k, NT) * scale
    sc = sc - jnp.max(sc, axis=-1, keepdims=True)
    p = jnp.exp(sc)
    return p / jnp.sum(p, axis=-1, keepdims=True)


def _mem_sublayer(hq, w_q, kv, w_o, x, g, *, name, tm=512, plan=None):
    s, d = hq.shape
    hd = d // MEM_HEADS
    scale = 1.0 / math.sqrt(hd)
    tm = min(tm, s)

    def body(hq_ref, wq_ref, kv_ref, wo_ref, x_ref, g_ref, q_ref, o_ref, x2_ref, h_ref):
        q_ref[...] = _bdot(hq_ref[...], wq_ref[...], NN).astype(BF16)
        for h in range(MEM_HEADS):
            cols = slice(h * hd, (h + 1) * hd)
            p = _mem_probs(q_ref[:, cols], kv_ref[:, cols], scale)
            o_ref[:, cols] = _bdot(p, kv_ref[:, d + h * hd:d + (h + 1) * hd], NN).astype(BF16)
        xv = _bdot(o_ref[...], wo_ref[...], NN) + x_ref[...]
        x2_ref[...] = xv
        _rms_fwd_tail(xv, g_ref, h_ref)

    row = pl.BlockSpec((tm, d), lambda i: (i, 0))
    whole = lambda a: pl.BlockSpec(a.shape, lambda i: (0,) * a.ndim)
    return _call(body, name=name, grid=(s // tm,),
                 in_specs=[row, whole(w_q), whole(kv), whole(w_o), row, whole(g)], out_specs=[row] * 4,
                 out_shape=[jax.ShapeDtypeStruct((s, d), BF16), jax.ShapeDtypeStruct((s, d), BF16),
                            jax.ShapeDtypeStruct((s, d), F32), jax.ShapeDtypeStruct((s, d), BF16)],
                 scratch=[], args=[hq, w_q, kv, w_o, x, g], plan=plan)


def _mem_sublayer_bwd(dx2b, dx2, x, g, qm, kv, w_q, w_o, *, name, tm=512, plan=None):
    s, d = qm.shape
    hd = d // MEM_HEADS
    scale = 1.0 / math.sqrt(hd)
    tm = min(tm, s)

    def body(dyb_ref, dres_ref, x_ref, g_ref, q_ref, kv_ref, wq_ref, wo_ref, dx_ref, dxb_ref, dg_ref, dq_ref, dkv_ref):
        i = pl.program_id(0)

        @pl.when(i == 0)
        def _():
            dkv_ref[...] = jnp.zeros_like(dkv_ref)
        dom = _bdot(dyb_ref[...], wo_ref[...], NT).astype(BF16)
        for h in range(MEM_HEADS):
            cols = slice(h * hd, (h + 1) * hd)
            vcols = slice(d + h * hd, d + (h + 1) * hd)
            q, k, v, do = q_ref[:, cols], kv_ref[:, cols], kv_ref[:, vcols], dom[:, cols]
            p = _mem_probs(q, k, scale)
            dp = _bdot(do, v, NT)
            ds = p * (dp - jnp.sum(dp * p, axis=-1, keepdims=True)) * scale
            dq_ref[:, cols] = _bdot(ds, k, NN).astype(BF16)
            dkv_ref[:, cols] += _bdot(ds, q, TN)
            dkv_ref[:, vcols] += _bdot(p, do, TN)
        dh = _bdot(dq_ref[...], wq_ref[...], NT)
        _rms_bwd_tail(i, dh, x_ref, g_ref, dres_ref, dx_ref, dxb_ref, dg_ref)

    row = pl.BlockSpec((tm, d), lambda i: (i, 0))
    whole = lambda a: pl.BlockSpec(a.shape, lambda i: (0,) * a.ndim)
    return _call(body, name=name, grid=(s // tm,),
                 in_specs=[row, row, row, whole(g), row, whole(kv), whole(w_q), whole(w_o)],
                 out_specs=[row, row, pl.BlockSpec((8, d), lambda i: (0, 0)), row, whole(kv)],
                 out_shape=[jax.ShapeDtypeStruct((s, d), F32), jax.ShapeDtypeStruct((s, d), BF16),
                            jax.ShapeDtypeStruct((8, d), F32), jax.ShapeDtypeStruct((s, d), BF16),
                            jax.ShapeDtypeStruct(kv.shape, F32)],
                 scratch=[], args=[dx2b, dx2, x, g, qm, kv, w_q, w_o], plan=plan)


def _sb_consts(t):
    row = lax.broadcasted_iota(jnp.int32, (t, t), 0)
    col = lax.broadcasted_iota(jnp.int32, (t, t), 1)
    lane = lax.broadcasted_iota(jnp.int32, (t, LANES), 1)
    return row, col, lane < SB_HEAD_DIM


def _sb_logits(q, k):
    z2 = jnp.minimum(_bdot(q, k, NT) * LOG2_E, SB_CLAMP)
    return z2, jnp.exp2(z2)


def _tri_sum(v, tri):
    hi = v.astype(BF16)
    lo = (v - hi.astype(F32)).astype(BF16)
    return _bdot(hi, tri, NN) + _bdot(lo, tri, NN)


def _sb_fwd(proj, *, name, plan=None):
    s = proj.shape[0]
    t = SB_TILE
    n_q = s // t
    scale = 1.0 / math.sqrt(SB_HEAD_DIM)
    k_blk, v_blk = SB_WIDTH // LANES, 2 * SB_WIDTH // LANES

    def body(q_ref, k_ref, v_ref, o_ref, c_ref, first_ref, acc_ref, c_scr):
        i = pl.program_id(1)
        row, col, head0 = _sb_consts(t)
        later = (row > col).astype(BF16)
        valid = col < row
        qs = q_ref[...] * scale
        q2 = (jnp.where(head0, qs, 0), jnp.where(head0, 0, qs))

        def tiles(kbs, diag_first, carry):
            kt = [k_ref[pl.ds(pl.multiple_of(kb * t, t), t), :] for kb in kbs]
            vt = [v_ref[pl.ds(pl.multiple_of(kb * t, t), t), :] for kb in kbs]
            jobs = [(n, h) for n in range(len(kbs)) for h in range(2)]
            masked = lambda n: diag_first and n == 0
            zs = {(n, h): _sb_logits(q2[h], kt[n]) for n, h in jobs}
            fail = {j: jnp.log2(1.0 + zs[j][1]) for j in jobs}
            fail = {j: jnp.where(valid, fail[j], 0.0) if masked(j[0]) else fail[j] for j in jobs}
            cum = {j: _tri_sum(fail[j], later) for j in jobs}
            run, before = list(carry), {}
            for n, h in jobs:
                before[n, h] = run[h]
                run[h] = run[h] + cum[n, h][:, 0:1] + fail[n, h][:, 0:1]
            w = {j: jnp.exp2(zs[j][0] - fail[j] - cum[j] - before[j]) for j in jobs}
            w = {j: jnp.where(valid, w[j], 0.0) if masked(j[0]) else w[j] for j in jobs}
            for n, h in jobs:
                acc_ref[h] += _bdot(w[n, h], vt[n], NN)
            return tuple(run)

        acc_ref[...] = jnp.zeros_like(acc_ref)
        zero = jnp.zeros((t, 1), F32)

        def alive(carry):
            return (jnp.minimum(jnp.min(carry[0]), jnp.min(carry[1])) < SB_DEAD).astype(jnp.int32)

        def step(state):
            kb, _, c0, c1 = state
            new = tiles([kb], False, (c0, c1))
            return kb - 1, alive(new), new[0], new[1]

        @pl.when(i == 0)
        def _():
            c_scr[0], c_scr[1] = tiles([i], True, (zero, zero))

        @pl.when(i > 0)
        def _():
            c_scr[0], c_scr[1] = tiles([i, i - 1], True, (zero, zero))
        carry = (c_scr[0], c_scr[1])
        kb, _, c0, c1 = lax.while_loop(lambda st: jnp.logical_and(st[0] >= 0, st[1] > 0), step,
                                       (i - 2, alive(carry), carry[0], carry[1]))
        kb = jnp.maximum(kb, -1)
        o_ref[...] = jnp.where(head0, acc_ref[0], acc_ref[1]).astype(BF16)
        c_ref[...] = jnp.where(lax.broadcasted_iota(jnp.int32, (t, 2), 1) == 0, c0, c1)
        first_ref[pl.program_id(0), i] = (kb + 1).astype(F32)

    return _call(
        body, name=name, grid=(SB_HEADS // 2, n_q),
        in_specs=[pl.BlockSpec((t, LANES), lambda p, i: (i, p)),
                  pl.BlockSpec((s, LANES), lambda p, i: (0, k_blk + p)),
                  pl.BlockSpec((s, LANES), lambda p, i: (0, v_blk + p))],
        out_specs=[pl.BlockSpec((t, LANES), lambda p, i: (i, p)),
                   pl.BlockSpec((None, t, 2), lambda p, i: (p, i, 0)),
                   pl.BlockSpec(memory_space=pltpu.SMEM)],
        out_shape=[jax.ShapeDtypeStruct((s, SB_WIDTH), BF16), jax.ShapeDtypeStruct((SB_HEADS // 2, s, 2), F32),
                   jax.ShapeDtypeStruct((SB_HEADS // 2, n_q), F32)],
        scratch=[pltpu.VMEM((2, t, LANES), F32), pltpu.VMEM((2, t, 1), F32)], args=[proj, proj, proj], plan=plan)


def _sb_bwd(proj, do_a, ctot, first, *, name, plan=None):
    s = proj.shape[0]
    t = SB_TILE
    n_q = s // t
    scale = 1.0 / math.sqrt(SB_HEAD_DIM)
    k_blk, v_blk = SB_WIDTH // LANES, 2 * SB_WIDTH // LANES

    def body(q_ref, k_ref, v_ref, do_ref, c_ref, first_ref, dq_ref, dk_ref, dv_ref, dq_acc, dk_acc, dv_acc):
        i = pl.program_id(1)
        kb0 = jnp.clip(first_ref[pl.program_id(0), i].astype(jnp.int32), 0, i)
        row, col, head0 = _sb_consts(t)
        upto = (row <= col).astype(BF16)
        before = (row < col).astype(BF16)
        valid = col < row
        qs = q_ref[...] * scale
        q2 = (jnp.where(head0, qs, 0), jnp.where(head0, 0, qs))
        do = do_ref[...]
        do2 = (jnp.where(head0, do, 0), jnp.where(head0, 0, do))
        ctot2 = (c_ref[:, 0:1], c_ref[:, 1:2])

        @pl.when(i == 0)
        def _():
            dk_acc[...] = jnp.zeros_like(dk_acc)
            dv_acc[...] = jnp.zeros_like(dv_acc)
        dq_acc[...] = jnp.zeros_like(dq_acc)

        def tiles(kbs, diag_last, carry):
            rows = [pl.ds(pl.multiple_of(kb * t, t), t) for kb in kbs]
            kt = [k_ref[r, :] for r in rows]
            vt = [v_ref[r, :] for r in rows]
            jobs = [(n, h) for n in range(len(kbs)) for h in range(2)]
            masked = lambda n: diag_last and n == len(kbs) - 1
            t_last = slice(t - 1, t)
            zs = {(n, h): _sb_logits(q2[h], kt[n]) for n, h in jobs}
            dw = {(n, h): _bdot(do2[h], vt[n], NT) for n, h in jobs}
            fail = {j: jnp.log2(1.0 + zs[j][1]) for j in jobs}
            fail = {j: jnp.where(valid, fail[j], 0.0) if masked(j[0]) else fail[j] for j in jobs}
            cum = {j: _tri_sum(fail[j], upto) for j in jobs}
            miss = {j: jnp.exp2(-fail[j]) for j in jobs}
            beta = {j: zs[j][1] * miss[j] for j in jobs}
            fail_run, fail_before = list(carry[0::2]), {}
            for n, h in jobs:
                fail_before[n, h] = fail_run[h]
                fail_run[h] = fail_run[h] + cum[n, h][:, t_last]
            w = {(n, h): beta[n, h] * jnp.exp2(fail_before[n, h] + cum[n, h] - ctot2[h]) for n, h in jobs}
            w = {j: jnp.where(valid, w[j], 0.0) if masked(j[0]) else w[j] for j in jobs}
            g = {j: w[j] * dw[j] for j in jobs}
            g_local = {j: _bdot(g[j], before, NN) for j in jobs}
            for n, h in jobs:
                dv_acc[rows[n], :] += _bdot(w[n, h], do2[h], TN)
            g_run, dz = list(carry[1::2]), {}
            for n, h in jobs:
                g_sum = g_run[h] + g_local[n, h]
                dz[n, h] = g[n, h] * miss[n, h] - beta[n, h] * g_sum
                g_run[h] = g_sum[:, t_last] + g[n, h][:, t_last]
            dz = {j: jnp.where(valid, dz[j], 0.0) if masked(j[0]) else dz[j] for j in jobs}
            for n, h in jobs:
                dq_acc[h] += _bdot(dz[n, h], kt[n], NN)
                dk_acc[rows[n], :] += _bdot(dz[n, h], q2[h], TN)
            return fail_run[0], g_run[0], fail_run[1], g_run[1]

        zero = jnp.zeros((t, 1), F32)
        carry = lax.fori_loop(kb0, i - 1, lambda n, c: tiles([n], False, c), (zero,) * 4)

        @pl.when(i == 0)
        def _():
            tiles([i], True, carry)

        @pl.when(i > 0)
        def _():
            tiles([i - 1, i], True, carry)
        dq_ref[...] = (jnp.where(head0, dq_acc[0], dq_acc[1]) * scale).astype(BF16)

        @pl.when(i == n_q - 1)
        def _():
            dk_ref[...] = dk_acc[...].astype(BF16)
            dv_ref[...] = dv_acc[...].astype(BF16)

    outs = _call(
        body, name=name, grid=(SB_HEADS // 2, n_q),
        in_specs=[pl.BlockSpec((t, LANES), lambda p, i: (i, p)),
                  pl.BlockSpec((s, LANES), lambda p, i: (0, k_blk + p)),
                  pl.BlockSpec((s, LANES), lambda p, i: (0, v_blk + p)),
                  pl.BlockSpec((t, LANES), lambda p, i: (i, p)),
                  pl.BlockSpec((None, t, 2), lambda p, i: (p, i, 0)),
                  pl.BlockSpec(memory_space=pltpu.SMEM)],
        out_specs=[pl.BlockSpec((t, LANES), lambda p, i: (i, p)),
                   pl.BlockSpec((s, LANES), lambda p, i: (0, p)),
                   pl.BlockSpec((s, LANES), lambda p, i: (0, p))],
        out_shape=[jax.ShapeDtypeStruct((s, SB_WIDTH), BF16)] * 3,
        scratch=[pltpu.VMEM((2, t, LANES), F32), pltpu.VMEM((s, LANES), F32), pltpu.VMEM((s, LANES), F32)],
        args=[proj, proj, proj, do_a, ctot, first], plan=plan)
    return jnp.concatenate(outs, axis=1)


def _mm_gathered(a, key, plan, *, name, out3=False, w_t=False):
    src = plan.gathering(key)
    if src is None:
        return _mm_nn(a, plan.weight(key), name=name, out3=out3, w_t=w_t)
    out, w_all = _mm_gathering(a, src, name=name, out3=out3, w_t=w_t)
    plan.set_weight(key, w_all)
    return out


def _local_step(x, mem, target, gains, plan):
    g_mix, g_memq, g_memkv, g_ffn, g_fin = gains
    d = x.shape[1]

    h0 = _rms_fwd(x, g_mix, name="rms_mix")
    proj = _mm_gathered(h0, "in", plan, name="mm_in")
    w_in = plan.weight("in")
    o_a, ctot, first = _sb_fwd(proj, name="sb_fwd", plan=plan)
    conv_w = plan.weight("conv")
    y_b = _conv_fwd(proj, conv_w, name="conv_fwd")
    w_a, w_b, w_mix = plan.weight("a"), plan.weight("b"), plan.weight("mix")
    x1, hq, merged, br_a, br_b = _mix_out(o_a, y_b, proj, w_a[0], w_b[0], w_mix[0], x, g_memq, name="mm_mix", plan=plan)
    w_mq, w_kv, w_mo = plan.weight("mq")[0], plan.weight("kv"), plan.weight("mo")[0]
    mn = _rms_fwd(mem, g_memkv, name="rms_memkv")
    kv = _mm_nn(mn, w_kv, name="mm_memkv")
    qm, om, x2, hf = _mem_sublayer(hq, w_mq, kv, w_mo, x1, g_ffn, name="mem_sublayer", plan=plan)
    gu = _mm_gathered(hf, "fi", plan, name="mm_ffn_in", out3=True, w_t=True)
    w_fi, w_fo = plan.weight("fi"), plan.weight("fo")
    dx3, dx3b, dg_fin, loss, act = _ffn_out_loss(gu, w_fo, x2, g_fin, target, name="mm_ffn_out")

    plan.grad("fo", _mm_tn_a3(act, dx3b, name="mm_d_w_ffn_out"))
    dgu = _ffn_out_bwd(dx3b, w_fo, gu, name="mm_d_act")
    plan.grad("fi", _mm_tn_a3(dgu, hf, name="mm_d_w_ffn_in"))
    dx2, dx2b, dg_ffn = _mm_nt_rms(dgu, w_fi, x2, g_ffn, dx3, name="mm_d_hf", dy3=True, w_nn=True, plan=plan)

    plan.grad("mo", _mm_tn(om, dx2b, d, name="mm_d_w_memo"))
    dx1, dx1b, dg_memq, dqm, dkv = _mem_sublayer_bwd(dx2b, dx2, x1, g_memq, qm, kv, w_mq, w_mo, name="mem_sublayer_bwd",
                                                    plan=plan)
    plan.grad("mq", _mm_tn(hq, dqm, d, name="mm_d_w_memq"))
    plan.grad("kv", _mm_tn(mn, dkv, w_kv.shape[2], name="mm_d_w_memkv"))
    _, _, dg_memkv = _mm_nt_rms(dkv, w_kv, mem, g_memkv, None, name="mm_d_mn")

    plan.grad("mix", _mm_tn(merged, dx1b, d, name="mm_d_w_mix"))
    dbr_a, dbr_b, dgab, do_a, dy_b = _mix_out_bwd(dx1b, w_mix[0], br_a, br_b, proj, w_a[0], w_b[0], name="mm_d_merged",
                                                 plan=plan)
    plan.grad("a", _mm_tn(o_a, dbr_a, d, name="mm_d_w_branch_a"))
    plan.grad("b", _mm_tn(y_b, dbr_b, d, name="mm_d_w_branch_b"))
    dconv, dconv_w = _conv_bwd(dy_b, proj, conv_w, name="conv_bwd", plan=plan)
    dqkv = _sb_bwd(proj, do_a, ctot, first, name="sb_bwd", plan=plan)
    dproj = jnp.concatenate([dqkv, dconv, dgab], axis=1)
    rows_in1 = d // IN_SPLIT[1] * (IN_SPLIT[1] - IN_SPLIT[0])
    plan.grad("in0", _mm_tn(h0, dproj, w_in.shape[2], name="mm_d_w_in0", tm=d - rows_in1, k_tiles=(0, 1)))
    plan.grad("in1", _mm_tn(h0, dproj, w_in.shape[2], name="mm_d_w_in1", tm=rows_in1,
                            k_tiles=(d // rows_in1 - 1, 1), plan=plan))
    dh0 = _mm_nt(dproj, w_in, name="mm_d_h0", out_dtype=F32, plan=plan)
    dx0, _, dg_mix = _rms_bwd(x, g_mix, dh0, dx1, name="rms_mix_bwd", plan=plan)

    return dx0, (dg_mix, dg_memq, dg_memkv, dg_ffn, dg_fin, dconv_w, loss)


def _row_tile(a, target=512):
    tm = min(a, target)
    while a % tm:
        tm -= 8
    return tm


def _sum_with_sibling(parts, recvs, core, *, name):
    n = len(parts)

    def body(core_ref, *refs):
        for p_ref, r_ref, o_ref in zip(refs[:n], refs[n:2 * n], refs[2 * n:]):
            o_ref[...] = (p_ref[...].astype(F32) + r_ref[...].astype(F32)).astype(o_ref.dtype)

    mine = [pl.BlockSpec((None,) + p.shape[1:], lambda q, core_ref: (2 * q + core_ref[0], 0, 0)) for p in parts]
    other = [pl.BlockSpec((None,) + p.shape[1:], lambda q, core_ref: (q, 0, 0)) for p in parts]
    return pl.pallas_call(
        body, name=name,
        grid_spec=pltpu.PrefetchScalarGridSpec(num_scalar_prefetch=1, grid=(N_CHIP,), in_specs=mine + other,
                                               out_specs=other),
        out_shape=[jax.ShapeDtypeStruct((N_CHIP,) + p.shape[1:], p.dtype) for p in parts],
        compiler_params=_params(1))(core, *parts, *recvs)


def _adam_math(wv, g, m, v):
    m = ADAM_B1 * m + (1.0 - ADAM_B1) * g
    v = ADAM_B2 * v + (1.0 - ADAM_B2) * (g * g)
    m_hat = m / (1.0 - ADAM_B1 ** ADAM_STEP)
    v_hat = v / (1.0 - ADAM_B2 ** ADAM_STEP)
    delta = -ADAM_LR * (m_hat / (jnp.sqrt(v_hat) + ADAM_EPS) + ADAM_WD * wv)
    return delta, m, v


def _adam_sharded(wv, m, v, own, recv, chip, *, name):
    a, b = wv.shape
    tm = _row_tile(a)

    def body(chip_ref, w_ref, m_ref, v_ref, own_ref, recv_ref, g_ref, d_ref, nm_ref, nv_ref):
        g = own_ref[...].astype(F32)
        for j in range(3):
            g = g + recv_ref[j].astype(F32)
        delta, nm, nv = _adam_math(w_ref[...], g, m_ref[...], v_ref[...])
        g_ref[...] = g
        d_ref[...] = delta
        nm_ref[...] = nm
        nv_ref[...] = nv

    tile = pl.BlockSpec((tm, b), lambda i, chip_ref: (i, 0))
    return pl.pallas_call(
        body, name=name,
        grid_spec=pltpu.PrefetchScalarGridSpec(
            num_scalar_prefetch=1, grid=(a // tm,),
            in_specs=[tile, tile, tile,
                      pl.BlockSpec((None, tm, b), lambda i, chip_ref: (chip_ref[0], i, 0)),
                      pl.BlockSpec((3, tm, b), lambda i, chip_ref: (0, i, 0))],
            out_specs=[tile] * 4),
        out_shape=[jax.ShapeDtypeStruct((a, b), F32)] * 4, compiler_params=_params(1))(chip, wv, m, v, own, recv)


def _sum_devices(gathered, *, name):
    _, r, c = gathered.shape

    def body(g_ref, o_ref):
        total = g_ref[0]
        for j in range(1, N_DEV):
            total = total + g_ref[j]
        o_ref[...] = total

    return pl.pallas_call(body, name=name, out_shape=jax.ShapeDtypeStruct((r, c), F32))(gathered)


def _adam_small(wv, g, m, v, *, name):
    def body(w_ref, g_ref, m_ref, v_ref, d_ref, nm_ref, nv_ref):
        delta, nm, nv = _adam_math(w_ref[...], g_ref[...], m_ref[...], v_ref[...])
        d_ref[...] = delta
        nm_ref[...] = nm
        nv_ref[...] = nv

    return pl.pallas_call(body, name=name, out_shape=[jax.ShapeDtypeStruct(wv.shape, F32)] * 3)(wv, g, m, v)


BIG = ("in", "a", "b", "mix", "mq", "kv", "mo", "fi", "fo")
ROW_SHARDED = ("mix", "mq", "mo")
UNSHARDED = ("a", "b")
FFN_GROUPS = 4
IN_SPLIT = (3, 4)
SMALL_ROWS = 16


class _Plan:
    FUSED = ("in",)
    GATHER_ON = {"sb_fwd": ("a", "b", "mix", "kv", "mq", "mo", "conv", "fi0"), "mm_mix": ("fo",),
                 "mem_sublayer": ("fi1",)}
    SIBLING_ON = {"mm_d_hf": ("fo", "fi"), "mm_d_merged": ("mo", "mq", "kv"), "conv_bwd": ("mix", "a", "b"),
                  "mm_d_w_in1": ("in0",), "mm_d_h0": ("in1",)}
    CHIPS_ON = {"mem_sublayer_bwd": ("fo",), "sb_bwd": ("fi", "mo", "mq", "kv", "mix", "a", "b"), "mm_d_h0": ("in0",),
                "rms_mix_bwd": ("in1",)}

    def __init__(self, shards, core):
        self.shards, self.core = shards, core
        self.w, self.parts, self.chip_sums, self.from_chips = {}, {}, {}, {}

    def gathering(self, k):
        return self.shards[k] if k in self.FUSED else None

    def comm(self, name):
        comms = []
        if name in self.GATHER_ON:
            comms.append(_gather_comm([self.shards[k] for k in self.GATHER_ON[name]]))
        if name in self.SIBLING_ON:
            comms.append(_sibling_comm([self.parts[k] for k in self.SIBLING_ON[name]]))
        if name in self.CHIPS_ON:
            comms.append(_chips_comm([self.chip_sums[k] for k in self.CHIPS_ON[name]]))
        return _join_comms(comms) if comms else None

    def landed(self, name, outs):
        outs = list(outs)
        for k in self.GATHER_ON.get(name, ()):
            self.set_weight(k, outs.pop(0))
        keys = self.SIBLING_ON.get(name, ())
        if keys:
            sums = _sum_with_sibling([self.parts[k] for k in keys], [outs.pop(0) for _ in keys], self.core,
                                     name="sum_with_sibling_" + "_".join(keys))
            self.chip_sums.update(zip(keys, sums))
        for k in self.CHIPS_ON.get(name, ()):
            self.from_chips[k] = outs.pop(0)

    def set_weight(self, k, gathered):
        _, a, b = gathered.shape
        if k in ROW_SHARDED:
            gathered = gathered.reshape(1, N_DEV * a, b)
        elif k in UNSHARDED:
            gathered = jnp.transpose(gathered, (1, 0, 2)).reshape(1, a, N_DEV * b)
        elif k == "fo":
            gathered = gathered.reshape(FFN_GROUPS, N_DEV * a // FFN_GROUPS, b)
        elif k == "conv":
            n_conv = CONV_WIDTH // N_DEV
            gathered = jnp.transpose(gathered[:, :3, :n_conv], (1, 0, 2)).reshape(3, CONV_WIDTH)
        self.w[k] = gathered
        if k == "fi1":
            self.w["fi"] = jnp.concatenate([self.w["fi0"], gathered], axis=2)

    def weight(self, k):
        return self.w[k]

    def grad(self, k, g):
        _, a, b = g.shape
        if k in ROW_SHARDED:
            g = g.reshape(N_DEV, a // N_DEV, b)
        elif k in UNSHARDED:
            g = jnp.transpose(g.reshape(a, N_DEV, b // N_DEV), (1, 0, 2))
        elif k == "fo":
            g = g.reshape(N_DEV, FFN_GROUPS * a // N_DEV, b)
        self.parts[k] = g


def kernel(x, mem, norm_mix, w_in, conv_w, w_branch_a, w_branch_b, w_mix_out, norm_mem_q, norm_mem_kv, w_mem_q, w_mem_kv, w_mem_o, norm_ffn, w_ffn_in, w_ffn_out, norm_final, loss_target, m_norm_mix, m_w_in, m_conv_w, m_w_branch_a, m_w_branch_b, m_w_mix_out, m_norm_mem_q, m_norm_mem_kv, m_w_mem_q, m_w_mem_kv, m_w_mem_o, m_norm_ffn, m_w_ffn_in, m_w_ffn_out, m_norm_final, v_norm_mix, v_w_in, v_conv_w, v_w_branch_a, v_w_branch_b, v_w_mix_out, v_norm_mem_q, v_norm_mem_kv, v_w_mem_q, v_w_mem_kv, v_w_mem_o, v_norm_ffn, v_w_ffn_in, v_w_ffn_out, v_norm_final):
    d = x.shape[-1]
    xi, yi, ci = lax.axis_index("x"), lax.axis_index("y"), lax.axis_index("c")
    chip = jnp.reshape(2 * xi + yi, (1,)).astype(jnp.int32)
    dev = 4 * xi + 2 * yi + ci

    big_w = dict(zip(BIG, (w_in, w_branch_a, w_branch_b, w_mix_out, w_mem_q, w_mem_kv, w_mem_o, w_ffn_in, w_ffn_out)))
    big_m = dict(zip(BIG, (m_w_in, m_w_branch_a, m_w_branch_b, m_w_mix_out, m_w_mem_q, m_w_mem_kv, m_w_mem_o, m_w_ffn_in, m_w_ffn_out)))
    big_v = dict(zip(BIG, (v_w_in, v_w_branch_a, v_w_branch_b, v_w_mix_out, v_w_mem_q, v_w_mem_kv, v_w_mem_o, v_w_ffn_in, v_w_ffn_out)))

    flip = lambda t, k: jnp.transpose(t) if k == "fi" else t
    shards = {k: flip(big_w[k][0], k).astype(BF16) for k in BIG}
    shards["fi0"], shards["fi1"] = shards["fi"][:, :d // 2], shards["fi"][:, d // 2:]
    n_conv = conv_w.shape[-1]
    shards["conv"] = jnp.zeros((8, LANES), F32).at[:3, :n_conv].set(conv_w[0])
    plan = _Plan(shards, jnp.reshape(ci, (1,)).astype(jnp.int32))

    gains = (norm_mix, norm_mem_q, norm_mem_kv, norm_ffn, norm_final.reshape(1, d))
    dx0, small = _local_step(x[0], mem[0], loss_target[0], gains, plan)

    grads, deltas, new_m, new_v = {}, {}, {}, {}
    for k in BIG:
        lead = big_w[k].shape
        wv, mv, vv = flip(big_w[k][0], k), flip(big_m[k][0], k), flip(big_v[k][0], k)
        if k == "in":
            half = wv.shape[0] * IN_SPLIT[0] // IN_SPLIT[1]
            lo = _adam_sharded(wv[:half], mv[:half], vv[:half], plan.chip_sums["in0"], plan.from_chips["in0"], chip,
                               name="adam_in0")
            hi = _adam_sharded(wv[half:], mv[half:], vv[half:], plan.chip_sums["in1"], plan.from_chips["in1"], chip,
                               name="adam_in1")
            outs = [jnp.concatenate(pair, axis=0) for pair in zip(lo, hi)]
        else:
            outs = _adam_sharded(wv, mv, vv, plan.chip_sums[k], plan.from_chips[k], chip, name="adam_" + k)
        grads[k], deltas[k], new_m[k], new_v[k] = (flip(t, k).reshape(lead) for t in outs)

    dg_mix, dg_memq, dg_memkv, dg_ffn, dg_fin, dconv_w, loss = small
    conv_rows = jnp.zeros((3, d), F32).at[:, :CONV_WIDTH].set(dconv_w[:3])
    block = jnp.concatenate([dg_mix[:1], dg_memq[:1], dg_memkv[:1], dg_ffn[:1], dg_fin[:1], conv_rows,
                             jnp.broadcast_to(loss[:1, :1], (1, d)), jnp.zeros((SMALL_ROWS - 9, d), F32)], axis=0)
    total = _sum_devices(_exchange(_gather_comm([block]), name="gather_small")[0], name="sum_small")
    g_conv = lax.dynamic_slice(total[5:8, :CONV_WIDTH], (0, dev * n_conv), (3, n_conv))
    small_w = [norm_mix, norm_mem_q, norm_mem_kv, norm_ffn, norm_final.reshape(1, d), conv_w[0]]
    small_m = [m_norm_mix, m_norm_mem_q, m_norm_mem_kv, m_norm_ffn, m_norm_final.reshape(1, d), m_conv_w[0]]
    small_v = [v_norm_mix, v_norm_mem_q, v_norm_mem_kv, v_norm_ffn, v_norm_final.reshape(1, d), v_conv_w[0]]
    small_g = [total[0:1], total[1:2], total[2:3], total[3:4], total[4:5], g_conv]
    small_names = ["norm_mix", "norm_mem_q", "norm_mem_kv", "norm_ffn", "norm_final", "conv_w"]
    sg, sd, sm, sv = {}, {}, {}, {}
    for nme, wv, g, m, v in zip(small_names, small_w, small_g, small_m, small_v):
        dl, nm, nv = _adam_small(wv, g, m, v, name="adam_" + nme)
        shape = norm_final.shape if nme == "norm_final" else (conv_w.shape if nme == "conv_w" else wv.shape)
        sg[nme], sd[nme], sm[nme], sv[nme] = (t.reshape(shape) for t in (g, dl, nm, nv))

    def ordered(big, sml):
        return (sml["norm_mix"], big["in"], sml["conv_w"], big["a"], big["b"], big["mix"], sml["norm_mem_q"],
                sml["norm_mem_kv"], big["mq"], big["kv"], big["mo"], sml["norm_ffn"], big["fi"], big["fo"],
                sml["norm_final"])

    loss_out = total[8, 0]
    grad_x = dx0.reshape(x.shape)
    return (loss_out, grad_x, *ordered(grads, sg), *ordered(deltas, sd), *ordered(new_m, sm), *ordered(new_v, sv))
```

```python
import functools
import math

import jax
import jax.numpy as jnp
from jax import lax
from jax.experimental import pallas as pl
from jax.experimental.pallas import tpu as pltpu

F32 = jnp.float32
BF16 = jnp.bfloat16
MESH = pl.DeviceIdType.MESH

N_DEV = 8
N_CHIP = 4
NORM_EPS = 1e-6
SB_HEADS = 8
SB_HEAD_DIM = 64
SB_WIDTH = SB_HEADS * SB_HEAD_DIM
CONV_WIDTH = 512
MEM_HEADS = 4
ADAM_LR = 0.001
ADAM_B1 = 0.9
ADAM_B2 = 0.999
ADAM_EPS = 1e-08
ADAM_WD = 0.01
ADAM_STEP = 10

LANES = 128
VMEM_LIMIT_BYTES = 52 * 1024 * 1024
SB_TILE = 256
SB_STEP_HEADS = 4
SB_DEAD = 159.0
SB_CLAMP = 126.0
LOG2_E = 1.4426950408889634

ANY = pl.BlockSpec(memory_space=pl.ANY)


def _params(n_grid):
    return pltpu.CompilerParams(dimension_semantics=("arbitrary",) * n_grid, vmem_limit_bytes=VMEM_LIMIT_BYTES)


def _bdot(a, b, dims):
    return lax.dot_general(a.astype(BF16), b.astype(BF16), (dims, ((), ())), preferred_element_type=F32)


NN = ((1,), (0,))
NT = ((1,), (1,))
TN = ((0,), (0,))


class _Comm:
    def __init__(self, ins, outs, n_sems, start, finish):
        self.ins, self.outs, self.n_sems, self.start, self.finish = ins, outs, n_sems, start, finish

    def sem_shapes(self):
        return [pltpu.SemaphoreType.DMA((k,)) for k in self.n_sems]


def _place():
    return lax.axis_index("x"), lax.axis_index("y"), lax.axis_index("c")


def _gather_comm(shards):
    n = len(shards)

    def copies(ins, outs, sems):
        send_sems, recv_sems, _ = sems
        x, y, c = _place()
        chips = [(1 - x, y), (x, 1 - y), (1 - x, 1 - y)]

        def copy(a, k, block, to, from_shard=False):
            dst = outs[a].at[4 * block[0] + 2 * block[1] + block[2]]
            return pltpu.make_async_remote_copy(
                src_ref=ins[a] if from_shard else dst, dst_ref=dst, send_sem=send_sems.at[a * 7 + k],
                recv_sem=recv_sems.at[a * 7 + k], device_id=to, device_id_type=MESH)

        me, sibling = (x, y, c), (x, y, 1 - c)
        own = [[copy(a, 0, me, sibling, True)] + [copy(a, 1 + j, me, (*chip, c), True) for j, chip in enumerate(chips)]
               for a in range(n)]
        landed = [[copy(a, 1 + j, (*chip, c), me) for j, chip in enumerate(chips)] for a in range(n)]
        passed = [[copy(a, 4 + j, (*chip, c), sibling) for j, chip in enumerate(chips)] for a in range(n)]
        from_sibling = [[copy(a, 0, sibling, me)] + [copy(a, 4 + j, (*chip, 1 - c), me) for j, chip in enumerate(chips)]
                        for a in range(n)]
        local = [pltpu.make_async_copy(ins[a], outs[a].at[4 * x + 2 * y + c], sems[2].at[a]) for a in range(n)]
        return own, landed, passed, from_sibling, local

    def start(ins, outs, sems):
        own, _, _, _, local = copies(ins, outs, sems)
        for a in range(n):
            local[a].start()
            for cp in own[a]:
                cp.start()

    def finish(ins, outs, sems):
        own, landed, passed, from_sibling, local = copies(ins, outs, sems)
        for a in range(n):
            for arrived, onward in zip(landed[a], passed[a]):
                arrived.wait_recv()
                onward.start()
        for a in range(n):
            for cp in from_sibling[a]:
                cp.wait_recv()
        for a in range(n):
            for cp in own[a] + passed[a]:
                cp.wait_send()
            local[a].wait()

    outs = [jax.ShapeDtypeStruct((N_DEV,) + s.shape, s.dtype) for s in shards]
    return _Comm(list(shards), outs, (7 * n, 7 * n, n), start, finish)


def _sibling_comm(parts):
    n = len(parts)

    def copies(ins, outs, sems):
        x, y, c = _place()
        return [pltpu.make_async_remote_copy(
            src_ref=ins[a].at[2 * q + 1 - c], dst_ref=outs[a].at[q], send_sem=sems[0].at[a * N_CHIP + q],
            recv_sem=sems[1].at[a * N_CHIP + q], device_id=(x, y, 1 - c), device_id_type=MESH)
            for a in range(n) for q in range(N_CHIP)]

    def start(ins, outs, sems):
        for cp in copies(ins, outs, sems):
            cp.start()

    def finish(ins, outs, sems):
        cps = copies(ins, outs, sems)
        for cp in cps:
            cp.wait_recv()
        for cp in cps:
            cp.wait_send()

    outs = [jax.ShapeDtypeStruct((N_CHIP,) + p.shape[1:], p.dtype) for p in parts]
    return _Comm(list(parts), outs, (N_CHIP * n, N_CHIP * n), start, finish)


def _chips_comm(parts):
    n = len(parts)

    def copies(ins, outs, sems):
        x, y, c = _place()
        chips = [(1 - x, y), (x, 1 - y), (1 - x, 1 - y)]
        return [pltpu.make_async_remote_copy(
            src_ref=ins[a].at[2 * px + py], dst_ref=outs[a].at[j], send_sem=sems[0].at[a * 3 + j],
            recv_sem=sems[1].at[a * 3 + j], device_id=(px, py, c), device_id_type=MESH)
            for a in range(n) for j, (px, py) in enumerate(chips)]

    def start(ins, outs, sems):
        for cp in copies(ins, outs, sems):
            cp.start()

    def finish(ins, outs, sems):
        cps = copies(ins, outs, sems)
        for cp in cps:
            cp.wait_recv()
        for cp in cps:
            cp.wait_send()

    outs = [jax.ShapeDtypeStruct((3,) + p.shape[1:], p.dtype) for p in parts]
    return _Comm(list(parts), outs, (3 * n, 3 * n), start, finish)


def _join_comms(comms):
    if len(comms) == 1:
        return comms[0]

    def split(refs, counts):
        out, at = [], 0
        for n in counts:
            out.append(refs[at:at + n])
            at += n
        return out

    def each(method):
        def run(ins, outs, sems):
            parts = zip(comms, split(ins, [len(c.ins) for c in comms]), split(outs, [len(c.outs) for c in comms]),
                        split(sems, [len(c.n_sems) for c in comms]))
            for c, c_ins, c_outs, c_sems in parts:
                getattr(c, method)(c_ins, c_outs, c_sems)
        return run

    return _Comm([a for c in comms for a in c.ins], [o for c in comms for o in c.outs],
                 tuple(k for c in comms for k in c.n_sems), each("start"), each("finish"))


def _exchange(comm, *, name):
    n_ci, n_co = len(comm.ins), len(comm.outs)

    def kern(*refs):
        c_ins, c_outs, sems = refs[:n_ci], refs[n_ci:n_ci + n_co], refs[n_ci + n_co:]
        comm.start(c_ins, c_outs, sems)
        comm.finish(c_ins, c_outs, sems)

    return pl.pallas_call(kern, name=name, in_specs=[ANY] * n_ci, out_specs=[ANY] * n_co, out_shape=comm.outs,
                          scratch_shapes=comm.sem_shapes())(*comm.ins)


def _call(body, *, name, grid, in_specs, out_specs, out_shape, scratch, args, plan=None):
    comm = plan.comm(name) if plan is not None else None
    if comm is None:
        return list(pl.pallas_call(functools.partial(body), name=name, grid=grid, in_specs=in_specs,
                                   out_specs=out_specs, out_shape=out_shape, scratch_shapes=scratch,
                                   compiler_params=_params(len(grid)))(*args))
    n_in, n_out, n_scr, n_ci, n_co = len(in_specs), len(out_specs), len(scratch), len(comm.ins), len(comm.outs)

    def kern(*refs):
        ins, c_ins, refs = refs[:n_in], refs[n_in:n_in + n_ci], refs[n_in + n_ci:]
        outs, c_outs, refs = refs[:n_out], refs[n_out:n_out + n_co], refs[n_out + n_co:]
        scr, sems = refs[:n_scr], refs[n_scr:]
        ids = [pl.program_id(ax) for ax in range(len(grid))]
        first = functools.reduce(jnp.logical_and, [i == 0 for i in ids])
        last = functools.reduce(jnp.logical_and, [i == g - 1 for i, g in zip(ids, grid)])

        @pl.when(first)
        def _():
            comm.start(c_ins, c_outs, sems)
        body(*ins, *outs, *scr)

        @pl.when(last)
        def _():
            comm.finish(c_ins, c_outs, sems)

    res = pl.pallas_call(kern, name=name, grid=grid, in_specs=list(in_specs) + [ANY] * n_ci,
                         out_specs=list(out_specs) + [ANY] * n_co, out_shape=list(out_shape) + comm.outs,
                         scratch_shapes=list(scratch) + comm.sem_shapes(),
                         compiler_params=_params(len(grid)))(*args, *comm.ins)
    plan.landed(name, list(res[n_out:]))
    return list(res[:n_out])


def _mm_body(dims, has_add, *refs):
    if has_add:
        a_ref, b_ref, add_ref, o_ref = refs
        total = _bdot(a_ref[...], b_ref[...], dims) + add_ref[...]
    else:
        a_ref, b_ref, o_ref = refs
        total = _bdot(a_ref[...], b_ref[...], dims)
    o_ref[...] = total.astype(o_ref.dtype)


def _mm_nt_body(j, n, dy_ref, w_ref, o_ref):
    total = _bdot(dy_ref[:, 0:n], w_ref[0], NT)
    for jj in range(1, j):
        total = total + _bdot(dy_ref[:, jj * n:(jj + 1) * n], w_ref[jj], NT)
    o_ref[...] = total.astype(o_ref.dtype)


def _mm_nn(a, w3, *, name, out_dtype=BF16, add=None, tm=1024, tn=None, out3=False, w_t=False, plan=None):
    m, kk = a.shape
    j, n = w3.shape[0], w3.shape[1 if w_t else 2]
    tm, tn = min(tm, m), n if tn is None else tn
    n_t = n // tn
    in_specs = [pl.BlockSpec((tm, kk), lambda i, jj: (i, 0)),
                pl.BlockSpec((None, tn, kk), lambda i, jj: (jj // n_t, jj % n_t, 0)) if w_t else
                pl.BlockSpec((None, kk, tn), lambda i, jj: (jj // n_t, 0, jj % n_t))]
    args = [a, w3]
    if add is not None:
        in_specs.append(pl.BlockSpec((tm, tn), lambda i, jj: (i, jj)))
        args.append(add)
    if out3:
        out_spec = pl.BlockSpec((None, tm, tn), lambda i, jj: (jj // n_t, i, jj % n_t))
        out_shape = jax.ShapeDtypeStruct((j, m, n), out_dtype)
    else:
        out_spec = pl.BlockSpec((tm, tn), lambda i, jj: (i, jj))
        out_shape = jax.ShapeDtypeStruct((m, j * n), out_dtype)
    return _call(
        functools.partial(_mm_body, NT if w_t else NN, add is not None), name=name, grid=(m // tm, j * n_t),
        in_specs=in_specs, out_specs=[out_spec], out_shape=[out_shape], scratch=[], args=args, plan=plan)[0]


def _mm_gathering(a, shard, *, name, out3=False, w_t=False, tm=1024):
    m, kk = a.shape
    n = shard.shape[0 if w_t else 1]
    tm = min(tm, m)
    n_i = m // tm
    fetch_at = min(1, n_i - 1)

    def body(a_ref, shard_ref, o_ref, w_all, w_vmem, send_sems, recv_sems, copy_sems):
        jj, i = pl.program_id(0), pl.program_id(1)
        x, y, c = _place()
        me, sibling = (x, y, c), (x, y, 1 - c)
        chips = [(jnp.bitwise_xor(x, c), jnp.bitwise_xor(y, 1 - c)), (jnp.bitwise_xor(x, 1 - c), jnp.bitwise_xor(y, c)),
                 (1 - x, 1 - y)]
        sibling_chips = [chips[1], chips[0], chips[2]]

        def rows(block):
            return w_all.at[4 * block[0] + 2 * block[1] + block[2]]

        def remote(k, block, to, from_shard=False):
            return pltpu.make_async_remote_copy(
                src_ref=shard_ref if from_shard else rows(block), dst_ref=rows(block), send_sem=send_sems.at[k],
                recv_sem=recv_sems.at[k], device_id=to, device_id_type=MESH)

        def load(step, src):
            return pltpu.make_async_copy(src, w_vmem.at[step % 2], copy_sems.at[1 + step % 2])

        own = [remote(0, me, sibling, True)] + [remote(1 + j, me, (*chip, c), True) for j, chip in enumerate(chips)]
        passed = [remote(4 + j, (*chip, c), sibling) for j, chip in enumerate(chips)]
        local = pltpu.make_async_copy(shard_ref, rows(me), copy_sems.at[0])

        @pl.when(jnp.logical_and(i == 0, jj == 0))
        def _():
            local.start()
            own[0].start()
            own[1].start()
            load(0, shard_ref).start()

        def arrivals():
            yield 1, (lambda: remote(0, sibling, me).wait_recv()), sibling
            for j, chip in enumerate(chips):
                def landed(j=j, chip=chip):
                    if j < 2:
                        own[1 + j].wait_send()
                        own[2 + j].start()
                    remote(1 + j, (*chip, c), me).wait_recv()
                    passed[j].start()
                yield 2 + 2 * j, landed, (*chip, c)
                block = (*sibling_chips[j], 1 - c)
                yield 3 + 2 * j, (lambda j=j, block=block: remote(4 + j, block, me).wait_recv()), block

        for step, wait_for_it, block in arrivals():
            @pl.when(jnp.logical_and(i == fetch_at, jj == step - 1))
            def _():
                wait_for_it()
                load(step, rows(block)).start()

        for step in range(N_DEV):
            @pl.when(jnp.logical_and(i == 0, jj == step))
            def _():
                load(step, rows(me)).wait()

        o_ref[...] = _bdot(a_ref[...], w_vmem[lax.rem(jj, 2)], NT if w_t else NN).astype(o_ref.dtype)

        @pl.when(jnp.logical_and(i == n_i - 1, jj == N_DEV - 1))
        def _():
            for cp in [own[0], own[3]] + passed:
                cp.wait_send()
            local.wait()

    def swept(jj):
        x, y, c = _place()
        first, second = 2 + 2 * c, 4 - 2 * c
        flips = (0b000, 0b001, first, second + 1, second, first + 1, 0b110, 0b111)
        return jnp.bitwise_xor(4 * x + 2 * y + c, sum(jnp.where(jj == k, f, 0) for k, f in enumerate(flips)))

    if out3:
        out_spec = pl.BlockSpec((None, tm, n), lambda jj, i: (swept(jj), i, 0))
        out_shape = jax.ShapeDtypeStruct((N_DEV, m, n), BF16)
    else:
        out_spec = pl.BlockSpec((tm, n), lambda jj, i: (i, swept(jj)))
        out_shape = jax.ShapeDtypeStruct((m, N_DEV * n), BF16)
    return pl.pallas_call(
        body, name=name, grid=(N_DEV, n_i),
        in_specs=[pl.BlockSpec((tm, kk), lambda jj, i: (i, 0)), ANY], out_specs=[out_spec, ANY],
        scratch_shapes=[pltpu.VMEM((2,) + shard.shape, shard.dtype), pltpu.SemaphoreType.DMA((7,)),
                        pltpu.SemaphoreType.DMA((7,)), pltpu.SemaphoreType.DMA((3,))],
        out_shape=[out_shape, jax.ShapeDtypeStruct((N_DEV,) + shard.shape, shard.dtype)],
        compiler_params=_params(2))(a, shard)


def _sigmoid(v):
    return 0.5 * jnp.tanh(0.5 * v) + 0.5


def _resident(w):
    return pl.BlockSpec(w.shape, lambda i: (0,) * w.ndim, pipeline_mode=pl.Buffered(1))


def _ffn_out_loss(gu3, w3, add, g, target, *, name, tm=512):
    j2, m, n = gu3.shape
    j = j2 // 2
    nn = w3.shape[2]
    tm = min(tm, m)

    def body(gu_ref, w_ref, add_ref, g_ref, t_ref, dx_ref, dxb_ref, dg_ref, loss_ref, act_ref):
        i = pl.program_id(0)
        xv = add_ref[...]
        for jj in range(j):
            gate = gu_ref[0, jj].astype(F32)
            act = (gate * _sigmoid(gate) * gu_ref[1, jj].astype(F32)).astype(BF16)
            act_ref[jj] = act
            xv = xv + _bdot(act, w_ref[jj], NN)
        gv = g_ref[...]
        r = lax.rsqrt(jnp.mean(xv * xv, axis=-1, keepdims=True) + NORM_EPS)
        xhat = xv * r
        err = xhat * gv - t_ref[...]
        _acc_rows(i, loss_ref, 0.5 * jnp.sum(jnp.mean(err * err, axis=-1, keepdims=True), axis=0, keepdims=True))
        dy = err * (1.0 / nn)
        dxhat = dy * gv
        dx = r * (dxhat - xhat * jnp.mean(dxhat * xhat, axis=-1, keepdims=True))
        dx_ref[...] = dx
        dxb_ref[...] = dx.astype(BF16)
        _acc_rows(i, dg_ref, jnp.sum(dy * xhat, axis=0, keepdims=True))

    row = pl.BlockSpec((tm, nn), lambda i: (i, 0))
    return _call(body, name=name, grid=(m // tm,),
                 in_specs=[pl.BlockSpec((2, j, tm, n), lambda i: (0, 0, i, 0)), _resident(w3),
                           row, pl.BlockSpec(g.shape, lambda i: (0, 0)), row],
                 out_specs=[row, row, pl.BlockSpec((8, nn), lambda i: (0, 0)), pl.BlockSpec((8, LANES), lambda i: (0, 0)),
                            pl.BlockSpec((j, tm, n), lambda i: (0, i, 0))],
                 out_shape=[jax.ShapeDtypeStruct((m, nn), F32), jax.ShapeDtypeStruct((m, nn), BF16),
                            jax.ShapeDtypeStruct((8, nn), F32), jax.ShapeDtypeStruct((8, LANES), F32),
                            jax.ShapeDtypeStruct((j, m, n), BF16)],
                 scratch=[], args=[gu3.reshape(2, j, m, n), w3, add, g, target])


def _ffn_out_bwd(dy, w3, gu3, *, name, tm=1024):
    m, nn = dy.shape
    j, n, _ = w3.shape
    tm = min(tm, m)

    def body(dy_ref, w_ref, gu_ref, dgu_ref):
        da = _bdot(dy_ref[...], w_ref[...], NT)
        gate = gu_ref[0].astype(F32)
        up = gu_ref[1].astype(F32)
        sg = _sigmoid(gate)
        silu = gate * sg
        dgu_ref[0] = (da * up * (sg + silu * (1.0 - sg))).astype(BF16)
        dgu_ref[1] = (da * silu).astype(BF16)

    out = _call(body, name=name, grid=(m // tm, j),
                in_specs=[pl.BlockSpec((tm, nn), lambda i, jj: (i, 0)),
                          pl.BlockSpec((None, n, nn), lambda i, jj: (jj, 0, 0)),
                          pl.BlockSpec((2, None, tm, n), lambda i, jj: (0, jj, i, 0))],
                out_specs=[pl.BlockSpec((2, None, tm, n), lambda i, jj: (0, jj, i, 0))],
                out_shape=[jax.ShapeDtypeStruct((2, j, m, n), BF16)], scratch=[],
                args=[dy, w3, gu3.reshape(2, j, m, n)])[0]
    return out.reshape(2 * j, m, n)


def _rms_fwd_tail(xv, g_ref, h_ref):
    r = lax.rsqrt(jnp.mean(xv * xv, axis=-1, keepdims=True) + NORM_EPS)
    h_ref[...] = (xv * r * g_ref[...]).astype(BF16)


def _rms_bwd_tail(i, dh, x_ref, g_ref, dres_ref, dx_ref, dxb_ref, dg_ref):
    xv = x_ref[...]
    r = lax.rsqrt(jnp.mean(xv * xv, axis=-1, keepdims=True) + NORM_EPS)
    xhat = xv * r
    dxhat = dh * g_ref[...]
    dx = r * (dxhat - xhat * jnp.mean(dxhat * xhat, axis=-1, keepdims=True))
    if dres_ref is not None:
        dx = dx + dres_ref[...]
    dx_ref[...] = dx
    dxb_ref[...] = dx.astype(BF16)
    _acc_rows(i, dg_ref, jnp.sum(dh * xhat, axis=0, keepdims=True))


def _mm_nt_rms(dy, w3, x, g, dres, *, name, dy3=False, w_nn=False, tm=512, plan=None):
    j = w3.shape[0]
    m, kk = x.shape
    n = dy.shape[2] if dy3 else dy.shape[1] // j
    tm = min(tm, m)

    def body(dy_ref, w_ref, x_ref, g_ref, *rest):
        dres_ref = rest[0] if dres is not None else None
        dx_ref, dxb_ref, dg_ref = rest[-3:]
        dh = None
        for jj in range(j):
            piece = dy_ref[jj] if dy3 else dy_ref[:, jj * n:(jj + 1) * n]
            part = _bdot(piece, w_ref[jj], NN if w_nn else NT)
            dh = part if dh is None else dh + part
        _rms_bwd_tail(pl.program_id(0), dh, x_ref, g_ref, dres_ref, dx_ref, dxb_ref, dg_ref)

    row = pl.BlockSpec((tm, kk), lambda i: (i, 0))
    in_specs = [pl.BlockSpec((j, tm, n), lambda i: (0, i, 0)) if dy3 else pl.BlockSpec((tm, j * n), lambda i: (i, 0)),
                _resident(w3), row, pl.BlockSpec(g.shape, lambda i: (0, 0))]
    args = [dy, w3, x, g]
    if dres is not None:
        in_specs.append(row)
        args.append(dres)
    return _call(body, name=name, grid=(m // tm,), in_specs=in_specs,
                 out_specs=[row, row, pl.BlockSpec((8, kk), lambda i: (0, 0))],
                 out_shape=[jax.ShapeDtypeStruct((m, kk), F32), jax.ShapeDtypeStruct((m, kk), BF16),
                            jax.ShapeDtypeStruct((8, kk), F32)], scratch=[], args=args, plan=plan)


def _mix_out(o_a, y_b, proj, w_a, w_b, w, x, g, *, name, tm=512, plan=None):
    s, c = o_a.shape
    d = w.shape[1]
    tm = min(tm, s)

    def body(oa_ref, yb_ref, ga_ref, gb_ref, wa_ref, wb_ref, w_ref, x_ref, g_ref, x1_ref, h_ref, merged_ref, a_ref, b_ref):
        a_ref[...] = _bdot(oa_ref[...], wa_ref[...], NN).astype(BF16)
        b_ref[...] = _bdot(yb_ref[...], wb_ref[...], NN).astype(BF16)
        merged = (_sigmoid(ga_ref[...].astype(F32)) * a_ref[...].astype(F32)
                  + _sigmoid(gb_ref[...].astype(F32)) * b_ref[...].astype(F32)).astype(BF16)
        merged_ref[...] = merged
        xv = _bdot(merged, w_ref[...], NN) + x_ref[...]
        x1_ref[...] = xv
        _rms_fwd_tail(xv, g_ref, h_ref)

    row = pl.BlockSpec((tm, d), lambda i: (i, 0))
    narrow = pl.BlockSpec((tm, c), lambda i: (i, 0))
    whole = lambda arr: pl.BlockSpec(arr.shape, lambda i: (0,) * arr.ndim)
    return _call(body, name=name, grid=(s // tm,),
                 in_specs=[narrow, narrow, pl.BlockSpec((tm, d), lambda i: (i, 3)), pl.BlockSpec((tm, d), lambda i: (i, 4)),
                           whole(w_a), whole(w_b), whole(w), row, whole(g)],
                 out_specs=[row] * 5,
                 out_shape=[jax.ShapeDtypeStruct((s, d), F32)] + [jax.ShapeDtypeStruct((s, d), BF16)] * 4,
                 scratch=[], args=[o_a, y_b, proj, proj, w_a, w_b, w, x, g], plan=plan)


def _mm_tn_a3(a3, dy, *, name):
    j, t, n = a3.shape
    nn = dy.shape[1]
    return _call(functools.partial(_mm_body, TN, False), name=name, grid=(j,),
                 in_specs=[pl.BlockSpec((None, t, n), lambda jj: (jj, 0, 0)), pl.BlockSpec((t, nn), lambda jj: (0, 0))],
                 out_specs=[pl.BlockSpec((None, n, nn), lambda jj: (jj, 0, 0))],
                 out_shape=[jax.ShapeDtypeStruct((j, n, nn), BF16)], scratch=[], args=[a3, dy])[0]


def _mm_nt(dy, w3, *, name, out_dtype=BF16, tm=512, tn=1024, plan=None):
    m = dy.shape[0]
    j, kk, n = w3.shape
    tm, tn = min(tm, m), min(tn, kk)
    return _call(
        functools.partial(_mm_nt_body, j, n), name=name,
        grid=(m // tm, kk // tn),
        in_specs=[pl.BlockSpec((tm, j * n), lambda i, q: (i, 0)),
                  pl.BlockSpec((j, tn, n), lambda i, q: (0, q, 0))],
        out_specs=[pl.BlockSpec((tm, tn), lambda i, q: (i, q))],
        out_shape=[jax.ShapeDtypeStruct((m, kk), out_dtype)], scratch=[], args=[dy, w3], plan=plan)[0]


def _mm_tn(a, dy, n, *, name, out_dtype=BF16, tm=512, tn=None, k_tiles=None, plan=None):
    t, kk = a.shape
    j = dy.shape[1] // n
    tm, tn = min(tm, kk), n if tn is None else tn
    n_t = n // tn
    first, count = (0, kk // tm) if k_tiles is None else k_tiles
    return _call(
        functools.partial(_mm_body, TN, False), name=name,
        grid=(count, j * n_t),
        in_specs=[pl.BlockSpec((t, tm), lambda i, jj: (0, first + i)),
                  pl.BlockSpec((t, tn), lambda i, jj: (0, jj))],
        out_specs=[pl.BlockSpec((None, tm, tn), lambda i, jj: (jj // n_t, i, jj % n_t))],
        out_shape=[jax.ShapeDtypeStruct((j, count * tm, n), out_dtype)], scratch=[], args=[a, dy], plan=plan)[0]


def _rows(body, ins, outs, *, n_rows, tm, name, plan=None):
    tm = min(tm, n_rows)
    n_steps = n_rows // tm
    in_specs, args = [], []
    for arr, kind, width, block in ins:
        if kind == "row":
            in_specs.append(pl.BlockSpec((tm, width), functools.partial(lambda i, b: (i, b), b=block)))
        elif kind == "prev":
            in_specs.append(pl.BlockSpec((tm, width), functools.partial(lambda i, b: (jnp.maximum(i - 1, 0), b), b=block)))
        elif kind == "next":
            in_specs.append(pl.BlockSpec((tm, width), functools.partial(lambda i, b: (jnp.minimum(i + 1, n_steps - 1), b), b=block)))
        else:
            in_specs.append(pl.BlockSpec(arr.shape, functools.partial(lambda i, nd: (0,) * nd, nd=arr.ndim)))
        args.append(arr)
    out_specs, out_shape = [], []
    for shape, dtype, kind in outs:
        if kind == "row":
            out_specs.append(pl.BlockSpec((tm, shape[1]), lambda i: (i, 0)))
        else:
            out_specs.append(pl.BlockSpec(shape, functools.partial(lambda i, nd: (0,) * nd, nd=len(shape))))
        out_shape.append(jax.ShapeDtypeStruct(shape, dtype))

    def kern(*refs):
        body(pl.program_id(0), n_steps, *refs)

    return _call(kern, name=name, grid=(n_steps,), in_specs=in_specs, out_specs=out_specs, out_shape=out_shape,
                 scratch=[], args=args, plan=plan)


def _acc_rows(i, ref, value):
    @pl.when(i == 0)
    def _():
        ref[...] = jnp.zeros_like(ref)
    ref[...] += jnp.broadcast_to(value, ref.shape)


def _rms_fwd(x, g, *, name, tm=512):
    s, d = x.shape

    def body(i, n, x_ref, g_ref, h_ref):
        _rms_fwd_tail(x_ref[...], g_ref, h_ref)

    return _rows(body, [(x, "row", d, 0), (g, "full", 0, 0)], [((s, d), BF16, "row")], n_rows=s, tm=tm, name=name)[0]


def _rms_bwd(x, g, dh, dres, *, name, tm=512, plan=None):
    s, d = x.shape

    def body(i, n, x_ref, g_ref, dh_ref, dres_ref, dx_ref, dxb_ref, dg_ref):
        _rms_bwd_tail(i, dh_ref[...].astype(F32), x_ref, g_ref, dres_ref, dx_ref, dxb_ref, dg_ref)

    return _rows(body, [(x, "row", d, 0), (g, "full", 0, 0), (dh, "row", d, 0), (dres, "row", d, 0)],
                 [((s, d), F32, "row"), ((s, d), BF16, "row"), ((8, d), F32, "acc")],
                 n_rows=s, tm=tm, name=name, plan=plan)


def _mix_out_bwd(dx1b, w, br_a, br_b, proj, w_a, w_b, *, name, tm=512, plan=None):
    s, d = br_a.shape
    c = w_a.shape[0]
    tm = min(tm, s)

    def body(dy_ref, w_ref, a_ref, b_ref, ga_ref, gb_ref, wa_ref, wb_ref, da_ref, db_ref, dg_ref, doa_ref, dyb_ref):
        dm = _bdot(dy_ref[...], w_ref[...], NT)
        sa = _sigmoid(ga_ref[...].astype(F32))
        sb = _sigmoid(gb_ref[...].astype(F32))
        da_ref[...] = (dm * sa).astype(BF16)
        db_ref[...] = (dm * sb).astype(BF16)
        dg_ref[:, :d] = (dm * a_ref[...].astype(F32) * sa * (1.0 - sa)).astype(BF16)
        dg_ref[:, d:] = (dm * b_ref[...].astype(F32) * sb * (1.0 - sb)).astype(BF16)
        doa_ref[...] = _bdot(da_ref[...], wa_ref[...], NT).astype(BF16)
        dyb_ref[...] = _bdot(db_ref[...], wb_ref[...], NT).astype(BF16)

    row = pl.BlockSpec((tm, d), lambda i: (i, 0))
    narrow = pl.BlockSpec((tm, c), lambda i: (i, 0))
    whole = lambda arr: pl.BlockSpec(arr.shape, lambda i: (0,) * arr.ndim)
    return _call(body, name=name, grid=(s // tm,),
                 in_specs=[row, whole(w), row, row, pl.BlockSpec((tm, d), lambda i: (i, 3)),
                           pl.BlockSpec((tm, d), lambda i: (i, 4)), whole(w_a), whole(w_b)],
                 out_specs=[row, row, pl.BlockSpec((tm, 2 * d), lambda i: (i, 0)), narrow, narrow],
                 out_shape=[jax.ShapeDtypeStruct((s, d), BF16), jax.ShapeDtypeStruct((s, d), BF16),
                            jax.ShapeDtypeStruct((s, 2 * d), BF16), jax.ShapeDtypeStruct((s, c), BF16),
                            jax.ShapeDtypeStruct((s, c), BF16)],
                 scratch=[], args=[dx1b, w, br_a, br_b, proj, proj, w_a, w_b], plan=plan)


def _shift_down(cur, prev, k, first):
    row = lax.broadcasted_iota(jnp.int32, cur.shape, 0)
    out = jnp.where(row >= k, pltpu.roll(cur, k, 0), pltpu.roll(prev, k, 0))
    return jnp.where(jnp.logical_and(first, row < k), 0.0, out)


def _shift_up(cur, nxt, k, last):
    tm = cur.shape[0]
    row = lax.broadcasted_iota(jnp.int32, cur.shape, 0)
    out = jnp.where(row < tm - k, pltpu.roll(cur, tm - k, 0), pltpu.roll(nxt, tm - k, 0))
    return jnp.where(jnp.logical_and(last, row >= tm - k), 0.0, out)


def _conv_fwd(proj, conv_w, *, name, tm=512):
    s = proj.shape[0]
    c = CONV_WIDTH

    def body(i, n, u_ref, gb_ref, gc_ref, up_ref, gcp_ref, w_ref, y_ref):
        cu = gc_ref[...].astype(F32) * u_ref[...].astype(F32)
        cup = gcp_ref[...].astype(F32) * up_ref[...].astype(F32)
        first = i == 0
        y = (w_ref[0:1, :] * _shift_down(cu, cup, 2, first) + w_ref[1:2, :] * _shift_down(cu, cup, 1, first)
             + w_ref[2:3, :] * cu)
        y_ref[...] = (gb_ref[...].astype(F32) * y).astype(BF16)

    return _rows(body, [(proj, "row", c, 3), (proj, "row", c, 4), (proj, "row", c, 5),
                        (proj, "prev", c, 3), (proj, "prev", c, 5), (conv_w, "full", 0, 0)],
                 [((s, c), BF16, "row")], n_rows=s, tm=tm, name=name)[0]


def _conv_bwd(dy_b, proj, conv_w, *, name, tm=512, plan=None):
    s = proj.shape[0]
    c = CONV_WIDTH

    def body(i, n, dy_ref, u_ref, gb_ref, gc_ref, up_ref, gcp_ref, dyn_ref, gbn_ref, w_ref, d_ref, dw_ref):
        first, last = i == 0, i == n - 1
        u = u_ref[...].astype(F32)
        gb = gb_ref[...].astype(F32)
        gc = gc_ref[...].astype(F32)
        cu = gc * u
        cup = gcp_ref[...].astype(F32) * up_ref[...].astype(F32)
        cu1 = _shift_down(cu, cup, 1, first)
        cu2 = _shift_down(cu, cup, 2, first)
        conv = w_ref[0:1, :] * cu2 + w_ref[1:2, :] * cu1 + w_ref[2:3, :] * cu
        dy = dy_ref[...].astype(F32)
        dyc = dy * gb
        dycn = dyn_ref[...].astype(F32) * gbn_ref[...].astype(F32)
        dcu = (w_ref[2:3, :] * dyc + w_ref[1:2, :] * _shift_up(dyc, dycn, 1, last)
               + w_ref[0:1, :] * _shift_up(dyc, dycn, 2, last))
        d_ref[:, 0:c] = (dcu * gc).astype(BF16)
        d_ref[:, c:2 * c] = (dy * conv).astype(BF16)
        d_ref[:, 2 * c:3 * c] = (dcu * u).astype(BF16)
        row = lax.broadcasted_iota(jnp.int32, (8, c), 0)
        dw = (jnp.where(row == 0, jnp.sum(dyc * cu2, axis=0, keepdims=True), 0.0)
              + jnp.where(row == 1, jnp.sum(dyc * cu1, axis=0, keepdims=True), 0.0)
              + jnp.where(row == 2, jnp.sum(dyc * cu, axis=0, keepdims=True), 0.0))

        @pl.when(first)
        def _():
            dw_ref[...] = jnp.zeros_like(dw_ref)
        dw_ref[...] += dw

    return _rows(body, [(dy_b, "row", c, 0), (proj, "row", c, 3), (proj, "row", c, 4), (proj, "row", c, 5),
                        (proj, "prev", c, 3), (proj, "prev", c, 5), (dy_b, "next", c, 0), (proj, "next", c, 4),
                        (conv_w, "full", 0, 0)],
                 [((s, 3 * c), BF16, "row"), ((8, c), F32, "acc")], n_rows=s, tm=tm, name=name, plan=plan)


def _mem_probs(q, k, scale):
    sc = _bdot(q, k, NT) * scale
    sc = sc - jnp.max(sc, axis=-1, keepdims=True)
    p = jnp.exp(sc)
    return p / jnp.sum(p, axis=-1, keepdims=True)


def _mem_sublayer(hq, w_q, kv, w_o, x, g, *, name, tm=512, plan=None):
    s, d = hq.shape
    hd = d // MEM_HEADS
    scale = 1.0 / math.sqrt(hd)
    tm = min(tm, s)

    def body(hq_ref, wq_ref, kv_ref, wo_ref, x_ref, g_ref, q_ref, o_ref, x2_ref, h_ref):
        q_ref[...] = _bdot(hq_ref[...], wq_ref[...], NN).astype(BF16)
        for h in range(MEM_HEADS):
            cols = slice(h * hd, (h + 1) * hd)
            p = _mem_probs(q_ref[:, cols], kv_ref[:, cols], scale)
            o_ref[:, cols] = _bdot(p, kv_ref[:, d + h * hd:d + (h + 1) * hd], NN).astype(BF16)
        xv = _bdot(o_ref[...], wo_ref[...], NN) + x_ref[...]
        x2_ref[...] = xv
        _rms_fwd_tail(xv, g_ref, h_ref)

    row = pl.BlockSpec((tm, d), lambda i: (i, 0))
    whole = lambda a: pl.BlockSpec(a.shape, lambda i: (0,) * a.ndim)
    return _call(body, name=name, grid=(s // tm,),
                 in_specs=[row, whole(w_q), whole(kv), whole(w_o), row, whole(g)], out_specs=[row] * 4,
                 out_shape=[jax.ShapeDtypeStruct((s, d), BF16), jax.ShapeDtypeStruct((s, d), BF16),
                            jax.ShapeDtypeStruct((s, d), F32), jax.ShapeDtypeStruct((s, d), BF16)],
                 scratch=[], args=[hq, w_q, kv, w_o, x, g], plan=plan)


def _mem_sublayer_bwd(dx2b, dx2, x, g, qm, kv, w_q, w_o, *, name, tm=512, plan=None):
    s, d = qm.shape
    hd = d // MEM_HEADS
    scale = 1.0 / math.sqrt(hd)
    tm = min(tm, s)

    def body(dyb_ref, dres_ref, x_ref, g_ref, q_ref, kv_ref, wq_ref, wo_ref, dx_ref, dxb_ref, dg_ref, dq_ref, dkv_ref):
        i = pl.program_id(0)

        @pl.when(i == 0)
        def _():
            dkv_ref[...] = jnp.zeros_like(dkv_ref)
        dom = _bdot(dyb_ref[...], wo_ref[...], NT).astype(BF16)
        for h in range(MEM_HEADS):
            cols = slice(h * hd, (h + 1) * hd)
            vcols = slice(d + h * hd, d + (h + 1) * hd)
            q, k, v, do = q_ref[:, cols], kv_ref[:, cols], kv_ref[:, vcols], dom[:, cols]
            p = _mem_probs(q, k, scale)
            dp = _bdot(do, v, NT)
            ds = p * (dp - jnp.sum(dp * p, axis=-1, keepdims=True)) * scale
            dq_ref[:, cols] = _bdot(ds, k, NN).astype(BF16)
            dkv_ref[:, cols] += _bdot(ds, q, TN)
            dkv_ref[:, vcols] += _bdot(p, do, TN)
        dh = _bdot(dq_ref[...], wq_ref[...], NT)
        _rms_bwd_tail(i, dh, x_ref, g_ref, dres_ref, dx_ref, dxb_ref, dg_ref)

    row = pl.BlockSpec((tm, d), lambda i: (i, 0))
    whole = lambda a: pl.BlockSpec(a.shape, lambda i: (0,) * a.ndim)
    return _call(body, name=name, grid=(s // tm,),
                 in_specs=[row, row, row, whole(g), row, whole(kv), whole(w_q), whole(w_o)],
                 out_specs=[row, row, pl.BlockSpec((8, d), lambda i: (0, 0)), row, whole(kv)],
                 out_shape=[jax.ShapeDtypeStruct((s, d), F32), jax.ShapeDtypeStruct((s, d), BF16),
                            jax.ShapeDtypeStruct((8, d), F32), jax.ShapeDtypeStruct((s, d), BF16),
                            jax.ShapeDtypeStruct(kv.shape, F32)],
                 scratch=[], args=[dx2b, dx2, x, g, qm, kv, w_q, w_o], plan=plan)


def _sb_consts(t):
    row = lax.broadcasted_iota(jnp.int32, (t, t), 0)
    col = lax.broadcasted_iota(jnp.int32, (t, t), 1)
    lane = lax.broadcasted_iota(jnp.int32, (t, LANES), 1)
    return row, col, lane < SB_HEAD_DIM


def _sb_logits(q, k):
    z2 = jnp.minimum(_bdot(q, k, NT) * LOG2_E, SB_CLAMP)
    return z2, jnp.exp2(z2)


def _tri_sum(v, tri):
    hi = v.astype(BF16)
    lo = (v - hi.astype(F32)).astype(BF16)
    return _bdot(hi, tri, NN) + _bdot(lo, tri, NN)


def _sb_fwd(proj, *, name, plan=None):
    s = proj.shape[0]
    t, nh = SB_TILE, SB_STEP_HEADS
    n_q = s // t
    scale = 1.0 / math.sqrt(SB_HEAD_DIM)

    def body(q_ref, k_ref, v_ref, o_ref, c_ref, first_ref, acc_ref, c_scr):
        i = pl.program_id(1)
        row, col, head0 = _sb_consts(t)
        later = (row > col).astype(BF16)
        valid = col < row
        lanes = lambda h: slice((h // 2) * LANES, (h // 2 + 1) * LANES)
        q = [jnp.where(head0 == (h % 2 == 0), q_ref[:, lanes(h)] * scale, 0) for h in range(nh)]

        def tiles(kbs, diag_first, carry):
            rows = [pl.ds(pl.multiple_of(kb * t, t), t) for kb in kbs]
            jobs = [(n, h) for n in range(len(kbs)) for h in range(nh)]
            masked = lambda n: diag_first and n == 0
            zs = {(n, h): _sb_logits(q[h], k_ref[rows[n], lanes(h)]) for n, h in jobs}
            fail = {j: jnp.log2(1.0 + zs[j][1]) for j in jobs}
            fail = {j: jnp.where(valid, fail[j], 0.0) if masked(j[0]) else fail[j] for j in jobs}
            cum = {j: _tri_sum(fail[j], later) for j in jobs}
            run, before = list(carry), {}
            for n, h in jobs:
                before[n, h] = run[h]
                run[h] = run[h] + cum[n, h][:, 0:1] + fail[n, h][:, 0:1]
            w = {j: jnp.exp2(zs[j][0] - fail[j] - cum[j] - before[j]) for j in jobs}
            w = {j: jnp.where(valid, w[j], 0.0) if masked(j[0]) else w[j] for j in jobs}
            for n, h in jobs:
                acc_ref[h] += _bdot(w[n, h], v_ref[rows[n], lanes(h)], NN)
            return tuple(run)

        acc_ref[...] = jnp.zeros_like(acc_ref)
        zero = (jnp.zeros((t, 1), F32),) * nh

        def alive(carry):
            return (functools.reduce(jnp.minimum, [jnp.min(c) for c in carry]) < SB_DEAD).astype(jnp.int32)

        def step(state):
            new = tiles([state[0]], False, state[2:])
            return (state[0] - 1, alive(new)) + new

        @pl.when(i == 0)
        def _():
            for h, c in enumerate(tiles([i], True, zero)):
                c_scr[h] = c

        @pl.when(i > 0)
        def _():
            for h, c in enumerate(tiles([i, i - 1], True, zero)):
                c_scr[h] = c
        carry = tuple(c_scr[h] for h in range(nh))
        state = lax.while_loop(lambda st: jnp.logical_and(st[0] >= 0, st[1] > 0), step, (i - 2, alive(carry)) + carry)
        for b in range(nh // 2):
            o_ref[:, b * LANES:(b + 1) * LANES] = jnp.where(head0, acc_ref[2 * b], acc_ref[2 * b + 1]).astype(BF16)
        head = lax.broadcasted_iota(jnp.int32, (t, nh), 1)
        c_ref[...] = sum(jnp.where(head == h, state[2 + h], 0.0) for h in range(nh))
        first_ref[pl.program_id(0), i] = (jnp.maximum(state[0], -1) + 1).astype(F32)

    n_p, width = SB_HEADS // nh, nh * SB_HEAD_DIM
    k_blk, v_blk = SB_WIDTH // width, 2 * SB_WIDTH // width
    return _call(
        body, name=name, grid=(n_p, n_q),
        in_specs=[pl.BlockSpec((t, width), lambda p, i: (i, p)),
                  pl.BlockSpec((s, width), lambda p, i: (0, k_blk + p)),
                  pl.BlockSpec((s, width), lambda p, i: (0, v_blk + p))],
        out_specs=[pl.BlockSpec((t, width), lambda p, i: (i, p)),
                   pl.BlockSpec((None, t, nh), lambda p, i: (p, i, 0)),
                   pl.BlockSpec(memory_space=pltpu.SMEM)],
        out_shape=[jax.ShapeDtypeStruct((s, SB_WIDTH), BF16), jax.ShapeDtypeStruct((n_p, s, nh), F32),
                   jax.ShapeDtypeStruct((n_p, n_q), F32)],
        scratch=[pltpu.VMEM((nh, t, LANES), F32), pltpu.VMEM((nh, t, 1), F32)], args=[proj, proj, proj], plan=plan)


def _sb_bwd(proj, do_a, ctot, first, *, name, plan=None):
    s = proj.shape[0]
    t, nh = SB_TILE, SB_STEP_HEADS
    n_q = s // t
    scale = 1.0 / math.sqrt(SB_HEAD_DIM)

    def body(q_ref, k_ref, v_ref, do_ref, c_ref, first_ref, dq_ref, dk_ref, dv_ref, dq_acc, dk_acc, dv_acc):
        i = pl.program_id(1)
        kb0 = jnp.clip(first_ref[pl.program_id(0), i].astype(jnp.int32), 0, i)
        row, col, head0 = _sb_consts(t)
        upto = (row <= col).astype(BF16)
        before = (row < col).astype(BF16)
        valid = col < row
        lanes = lambda h: slice((h // 2) * LANES, (h // 2 + 1) * LANES)
        q2 = [jnp.where(head0 == (h % 2 == 0), q_ref[:, lanes(h)] * scale, 0) for h in range(nh)]
        do2 = [jnp.where(head0 == (h % 2 == 0), do_ref[:, lanes(h)], 0) for h in range(nh)]
        ctot2 = [c_ref[:, h:h + 1] for h in range(nh)]

        @pl.when(i == 0)
        def _():
            dk_acc[...] = jnp.zeros_like(dk_acc)
            dv_acc[...] = jnp.zeros_like(dv_acc)
        dq_acc[...] = jnp.zeros_like(dq_acc)

        def tiles(kbs, diag_last, carry):
            rows = [pl.ds(pl.multiple_of(kb * t, t), t) for kb in kbs]
            kt = {(n, h): k_ref[rows[n], lanes(h)] for n in range(len(kbs)) for h in range(nh)}
            jobs = list(kt)
            masked = lambda n: diag_last and n == len(kbs) - 1
            t_last = slice(t - 1, t)
            zs = {(n, h): _sb_logits(q2[h], kt[n, h]) for n, h in jobs}
            dw = {(n, h): _bdot(do2[h], v_ref[rows[n], lanes(h)], NT) for n, h in jobs}
            fail = {j: jnp.log2(1.0 + zs[j][1]) for j in jobs}
            fail = {j: jnp.where(valid, fail[j], 0.0) if masked(j[0]) else fail[j] for j in jobs}
            cum = {j: _tri_sum(fail[j], upto) for j in jobs}
            miss = {j: jnp.exp2(-fail[j]) for j in jobs}
            beta = {j: zs[j][1] * miss[j] for j in jobs}
            fail_run, fail_before = list(carry[0::2]), {}
            for n, h in jobs:
                fail_before[n, h] = fail_run[h]
                fail_run[h] = fail_run[h] + cum[n, h][:, t_last]
            w = {(n, h): beta[n, h] * jnp.exp2(fail_before[n, h] + cum[n, h] - ctot2[h]) for n, h in jobs}
            w = {j: jnp.where(valid, w[j], 0.0) if masked(j[0]) else w[j] for j in jobs}
            g = {j: w[j] * dw[j] for j in jobs}
            g_local = {j: _bdot(g[j], before, NN) for j in jobs}
            for n, h in jobs:
                dv_acc[rows[n], lanes(h)] += _bdot(w[n, h], do2[h], TN)
            g_run, dz = list(carry[1::2]), {}
            for n, h in jobs:
                g_sum = g_run[h] + g_local[n, h]
                dz[n, h] = g[n, h] * miss[n, h] - beta[n, h] * g_sum
                g_run[h] = g_sum[:, t_last] + g[n, h][:, t_last]
            dz = {j: jnp.where(valid, dz[j], 0.0) if masked(j[0]) else dz[j] for j in jobs}
            for n, h in jobs:
                dq_acc[h] += _bdot(dz[n, h], kt[n, h], NN)
                dk_acc[rows[n], lanes(h)] += _bdot(dz[n, h], q2[h], TN)
            return tuple(v for pair in zip(fail_run, g_run) for v in pair)

        zero = jnp.zeros((t, 1), F32)
        carry = lax.fori_loop(kb0, i - 1, lambda n, c: tiles([n], False, c), (zero,) * (2 * nh))

        @pl.when(i == 0)
        def _():
            tiles([i], True, carry)

        @pl.when(i > 0)
        def _():
            tiles([i - 1, i], True, carry)
        for b in range(nh // 2):
            dq_ref[:, b * LANES:(b + 1) * LANES] = (jnp.where(head0, dq_acc[2 * b], dq_acc[2 * b + 1])
                                                    * scale).astype(BF16)

        @pl.when(i == n_q - 1)
        def _():
            dk_ref[...] = dk_acc[...].astype(BF16)
            dv_ref[...] = dv_acc[...].astype(BF16)

    n_p, width = SB_HEADS // nh, nh * SB_HEAD_DIM
    k_blk, v_blk = SB_WIDTH // width, 2 * SB_WIDTH // width
    outs = _call(
        body, name=name, grid=(n_p, n_q),
        in_specs=[pl.BlockSpec((t, width), lambda p, i: (i, p)),
                  pl.BlockSpec((s, width), lambda p, i: (0, k_blk + p)),
                  pl.BlockSpec((s, width), lambda p, i: (0, v_blk + p)),
                  pl.BlockSpec((t, width), lambda p, i: (i, p)),
                  pl.BlockSpec((None, t, nh), lambda p, i: (p, i, 0)),
                  pl.BlockSpec(memory_space=pltpu.SMEM)],
        out_specs=[pl.BlockSpec((t, width), lambda p, i: (i, p)),
                   pl.BlockSpec((s, width), lambda p, i: (0, p)),
                   pl.BlockSpec((s, width), lambda p, i: (0, p))],
        out_shape=[jax.ShapeDtypeStruct((s, SB_WIDTH), BF16)] * 3,
        scratch=[pltpu.VMEM((nh, t, LANES), F32), pltpu.VMEM((s, width), F32), pltpu.VMEM((s, width), F32)],
        args=[proj, proj, proj, do_a, ctot, first], plan=plan)
    return jnp.concatenate(outs, axis=1)


def _mm_gathered(a, key, plan, *, name, out3=False, w_t=False):
    src = plan.gathering(key)
    if src is None:
        return _mm_nn(a, plan.weight(key), name=name, out3=out3, w_t=w_t, plan=plan)
    out, w_all = _mm_gathering(a, src, name=name, out3=out3, w_t=w_t)
    plan.set_weight(key, w_all)
    return out


def _local_step(x, mem, target, gains, plan):
    g_mix, g_memq, g_memkv, g_ffn, g_fin = gains
    d = x.shape[1]

    h0 = _rms_fwd(x, g_mix, name="rms_mix")
    proj = _mm_gathered(h0, "in", plan, name="mm_in")
    w_in = plan.weight("in")
    o_a, ctot, first = _sb_fwd(proj, name="sb_fwd", plan=plan)
    conv_w = plan.weight("conv")
    y_b = _conv_fwd(proj, conv_w, name="conv_fwd")
    w_a, w_b, w_mix = plan.weight("a"), plan.weight("b"), plan.weight("mix")
    x1, hq, merged, br_a, br_b = _mix_out(o_a, y_b, proj, w_a[0], w_b[0], w_mix[0], x, g_memq, name="mm_mix", plan=plan)
    w_mq, w_kv, w_mo = plan.weight("mq")[0], plan.weight("kv"), plan.weight("mo")[0]
    mn = _rms_fwd(mem, g_memkv, name="rms_memkv")
    kv = _mm_nn(mn, w_kv, name="mm_memkv")
    qm, om, x2, hf = _mem_sublayer(hq, w_mq, kv, w_mo, x1, g_ffn, name="mem_sublayer", plan=plan)
    gu = _mm_gathered(hf, "fi", plan, name="mm_ffn_in", out3=True, w_t=True)
    w_fi, w_fo = plan.weight("fi"), plan.weight("fo")
    dx3, dx3b, dg_fin, loss, act = _ffn_out_loss(gu, w_fo, x2, g_fin, target, name="mm_ffn_out")

    plan.grad("fo", _mm_tn_a3(act, dx3b, name="mm_d_w_ffn_out"))
    dgu = _ffn_out_bwd(dx3b, w_fo, gu, name="mm_d_act")
    plan.grad("fi", _mm_tn_a3(dgu, hf, name="mm_d_w_ffn_in"))
    dx2, dx2b, dg_ffn = _mm_nt_rms(dgu, w_fi, x2, g_ffn, dx3, name="mm_d_hf", dy3=True, w_nn=True, plan=plan)

    plan.grad("mo", _mm_tn(om, dx2b, d, name="mm_d_w_memo"))
    dx1, dx1b, dg_memq, dqm, dkv = _mem_sublayer_bwd(dx2b, dx2, x1, g_memq, qm, kv, w_mq, w_mo, name="mem_sublayer_bwd",
                                                    plan=plan)
    plan.grad("mq", _mm_tn(hq, dqm, d, name="mm_d_w_memq"))
    plan.grad("kv", _mm_tn(mn, dkv, w_kv.shape[2], name="mm_d_w_memkv"))
    _, _, dg_memkv = _mm_nt_rms(dkv, w_kv, mem, g_memkv, None, name="mm_d_mn")

    plan.grad("mix", _mm_tn(merged, dx1b, d, name="mm_d_w_mix"))
    dbr_a, dbr_b, dgab, do_a, dy_b = _mix_out_bwd(dx1b, w_mix[0], br_a, br_b, proj, w_a[0], w_b[0], name="mm_d_merged",
                                                 plan=plan)
    plan.grad("a", _mm_tn(o_a, dbr_a, d, name="mm_d_w_branch_a"))
    plan.grad("b", _mm_tn(y_b, dbr_b, d, name="mm_d_w_branch_b"))
    dconv, dconv_w = _conv_bwd(dy_b, proj, conv_w, name="conv_bwd", plan=plan)
    dqkv = _sb_bwd(proj, do_a, ctot, first, name="sb_bwd", plan=plan)
    dproj = jnp.concatenate([dqkv, dconv, dgab], axis=1)
    rows_in1 = d // IN_SPLIT[1] * (IN_SPLIT[1] - IN_SPLIT[0])
    plan.grad("in0", _mm_tn(h0, dproj, w_in.shape[2], name="mm_d_w_in0", tm=d - rows_in1, k_tiles=(0, 1)))
    plan.grad("in1", _mm_tn(h0, dproj, w_in.shape[2], name="mm_d_w_in1", tm=rows_in1,
                            k_tiles=(d // rows_in1 - 1, 1), plan=plan))
    dh0 = _mm_nt(dproj, w_in, name="mm_d_h0", out_dtype=F32, plan=plan)
    dx0, _, dg_mix = _rms_bwd(x, g_mix, dh0, dx1, name="rms_mix_bwd", plan=plan)

    return dx0, (dg_mix, dg_memq, dg_memkv, dg_ffn, dg_fin, dconv_w, loss)


def _row_tile(a, target=512):
    tm = min(a, target)
    while a % tm:
        tm -= 8
    return tm


def _sum_with_sibling(parts, recvs, core, *, name):
    n = len(parts)

    def body(core_ref, *refs):
        for p_ref, r_ref, o_ref in zip(refs[:n], refs[n:2 * n], refs[2 * n:]):
            o_ref[...] = (p_ref[...].astype(F32) + r_ref[...].astype(F32)).astype(o_ref.dtype)

    mine = [pl.BlockSpec((None,) + p.shape[1:], lambda q, core_ref: (2 * q + core_ref[0], 0, 0)) for p in parts]
    other = [pl.BlockSpec((None,) + p.shape[1:], lambda q, core_ref: (q, 0, 0)) for p in parts]
    return pl.pallas_call(
        body, name=name,
        grid_spec=pltpu.PrefetchScalarGridSpec(num_scalar_prefetch=1, grid=(N_CHIP,), in_specs=mine + other,
                                               out_specs=other),
        out_shape=[jax.ShapeDtypeStruct((N_CHIP,) + p.shape[1:], p.dtype) for p in parts],
        compiler_params=_params(1))(core, *parts, *recvs)


def _adam_math(wv, g, m, v):
    m = ADAM_B1 * m + (1.0 - ADAM_B1) * g
    v = ADAM_B2 * v + (1.0 - ADAM_B2) * (g * g)
    m_hat = m / (1.0 - ADAM_B1 ** ADAM_STEP)
    v_hat = v / (1.0 - ADAM_B2 ** ADAM_STEP)
    delta = -ADAM_LR * (m_hat / (jnp.sqrt(v_hat) + ADAM_EPS) + ADAM_WD * wv)
    return delta, m, v


def _adam_sharded(wv, m, v, own, recv, chip, *, name):
    a, b = wv.shape
    tm = _row_tile(a)

    def body(chip_ref, w_ref, m_ref, v_ref, own_ref, recv_ref, g_ref, d_ref, nm_ref, nv_ref):
        g = own_ref[...].astype(F32)
        for j in range(3):
            g = g + recv_ref[j].astype(F32)
        delta, nm, nv = _adam_math(w_ref[...], g, m_ref[...], v_ref[...])
        g_ref[...] = g
        d_ref[...] = delta
        nm_ref[...] = nm
        nv_ref[...] = nv

    tile = pl.BlockSpec((tm, b), lambda i, chip_ref: (i, 0))
    return pl.pallas_call(
        body, name=name,
        grid_spec=pltpu.PrefetchScalarGridSpec(
            num_scalar_prefetch=1, grid=(a // tm,),
            in_specs=[tile, tile, tile,
                      pl.BlockSpec((None, tm, b), lambda i, chip_ref: (chip_ref[0], i, 0)),
                      pl.BlockSpec((3, tm, b), lambda i, chip_ref: (0, i, 0))],
            out_specs=[tile] * 4),
        out_shape=[jax.ShapeDtypeStruct((a, b), F32)] * 4, compiler_params=_params(1))(chip, wv, m, v, own, recv)


def _sum_devices(gathered, *, name):
    _, r, c = gathered.shape

    def body(g_ref, o_ref):
        total = g_ref[0]
        for j in range(1, N_DEV):
            total = total + g_ref[j]
        o_ref[...] = total

    return pl.pallas_call(body, name=name, out_shape=jax.ShapeDtypeStruct((r, c), F32))(gathered)


def _adam_small(wv, g, m, v, *, name):
    def body(w_ref, g_ref, m_ref, v_ref, d_ref, nm_ref, nv_ref):
        delta, nm, nv = _adam_math(w_ref[...], g_ref[...], m_ref[...], v_ref[...])
        d_ref[...] = delta
        nm_ref[...] = nm
        nv_ref[...] = nv

    return pl.pallas_call(body, name=name, out_shape=[jax.ShapeDtypeStruct(wv.shape, F32)] * 3)(wv, g, m, v)


BIG = ("in", "a", "b", "mix", "mq", "kv", "mo", "fi", "fo")
ROW_SHARDED = ("mix", "mq", "mo")
UNSHARDED = ("a", "b")
FFN_GROUPS = 4
IN_SPLIT = (3, 4)
SMALL_ROWS = 16


class _Plan:
    FUSED = ("in",)
    GATHER_ON = {"sb_fwd": ("a", "b", "mix", "mq", "mo", "conv", "fi0"), "mm_mix": ("kv",), "mem_sublayer": ("fi1",),
                 "mm_ffn_in": ("fo",)}
    SIBLING_ON = {"mm_d_hf": ("fo", "fi"), "mm_d_merged": ("mo", "mq", "kv"), "conv_bwd": ("mix", "a", "b"),
                  "mm_d_w_in1": ("in0",), "mm_d_h0": ("in1",)}
    CHIPS_ON = {"mem_sublayer_bwd": ("fo",), "sb_bwd": ("fi", "mo", "mq", "kv", "mix", "a", "b"), "mm_d_h0": ("in0",),
                "rms_mix_bwd": ("in1",)}

    def __init__(self, shards, core):
        self.shards, self.core = shards, core
        self.w, self.parts, self.chip_sums, self.from_chips = {}, {}, {}, {}

    def gathering(self, k):
        return self.shards[k] if k in self.FUSED else None

    def comm(self, name):
        comms = []
        if name in self.GATHER_ON:
            comms.append(_gather_comm([self.shards[k] for k in self.GATHER_ON[name]]))
        if name in self.SIBLING_ON:
            comms.append(_sibling_comm([self.parts[k] for k in self.SIBLING_ON[name]]))
        if name in self.CHIPS_ON:
            comms.append(_chips_comm([self.chip_sums[k] for k in self.CHIPS_ON[name]]))
        return _join_comms(comms) if comms else None

    def landed(self, name, outs):
        outs = list(outs)
        for k in self.GATHER_ON.get(name, ()):
            self.set_weight(k, outs.pop(0))
        keys = self.SIBLING_ON.get(name, ())
        if keys:
            sums = _sum_with_sibling([self.parts[k] for k in keys], [outs.pop(0) for _ in keys], self.core,
                                     name="sum_with_sibling_" + "_".join(keys))
            self.chip_sums.update(zip(keys, sums))
        for k in self.CHIPS_ON.get(name, ()):
            self.from_chips[k] = outs.pop(0)

    def set_weight(self, k, gathered):
        _, a, b = gathered.shape
        if k in ROW_SHARDED:
            gathered = gathered.reshape(1, N_DEV * a, b)
        elif k in UNSHARDED:
            gathered = jnp.transpose(gathered, (1, 0, 2)).reshape(1, a, N_DEV * b)
        elif k == "fo":
            gathered = gathered.reshape(FFN_GROUPS, N_DEV * a // FFN_GROUPS, b)
        elif k == "conv":
            n_conv = CONV_WIDTH // N_DEV
            gathered = jnp.transpose(gathered[:, :3, :n_conv], (1, 0, 2)).reshape(3, CONV_WIDTH)
        self.w[k] = gathered
        if k == "fi1":
            self.w["fi"] = jnp.concatenate([self.w["fi0"], gathered], axis=2)

    def weight(self, k):
        return self.w[k]

    def grad(self, k, g):
        _, a, b = g.shape
        if k in ROW_SHARDED:
            g = g.reshape(N_DEV, a // N_DEV, b)
        elif k in UNSHARDED:
            g = jnp.transpose(g.reshape(a, N_DEV, b // N_DEV), (1, 0, 2))
        elif k == "fo":
            g = g.reshape(N_DEV, FFN_GROUPS * a // N_DEV, b)
        self.parts[k] = g


def kernel(x, mem, norm_mix, w_in, conv_w, w_branch_a, w_branch_b, w_mix_out, norm_mem_q, norm_mem_kv, w_mem_q, w_mem_kv, w_mem_o, norm_ffn, w_ffn_in, w_ffn_out, norm_final, loss_target, m_norm_mix, m_w_in, m_conv_w, m_w_branch_a, m_w_branch_b, m_w_mix_out, m_norm_mem_q, m_norm_mem_kv, m_w_mem_q, m_w_mem_kv, m_w_mem_o, m_norm_ffn, m_w_ffn_in, m_w_ffn_out, m_norm_final, v_norm_mix, v_w_in, v_conv_w, v_w_branch_a, v_w_branch_b, v_w_mix_out, v_norm_mem_q, v_norm_mem_kv, v_w_mem_q, v_w_mem_kv, v_w_mem_o, v_norm_ffn, v_w_ffn_in, v_w_ffn_out, v_norm_final):
    d = x.shape[-1]
    xi, yi, ci = lax.axis_index("x"), lax.axis_index("y"), lax.axis_index("c")
    chip = jnp.reshape(2 * xi + yi, (1,)).astype(jnp.int32)
    dev = 4 * xi + 2 * yi + ci

    big_w = dict(zip(BIG, (w_in, w_branch_a, w_branch_b, w_mix_out, w_mem_q, w_mem_kv, w_mem_o, w_ffn_in, w_ffn_out)))
    big_m = dict(zip(BIG, (m_w_in, m_w_branch_a, m_w_branch_b, m_w_mix_out, m_w_mem_q, m_w_mem_kv, m_w_mem_o, m_w_ffn_in, m_w_ffn_out)))
    big_v = dict(zip(BIG, (v_w_in, v_w_branch_a, v_w_branch_b, v_w_mix_out, v_w_mem_q, v_w_mem_kv, v_w_mem_o, v_w_ffn_in, v_w_ffn_out)))

    flip = lambda t, k: jnp.transpose(t) if k == "fi" else t
    shards = {k: flip(big_w[k][0], k).astype(BF16) for k in BIG}
    shards["fi0"], shards["fi1"] = shards["fi"][:, :d // 2], shards["fi"][:, d // 2:]
    n_conv = conv_w.shape[-1]
    shards["conv"] = jnp.zeros((8, LANES), F32).at[:3, :n_conv].set(conv_w[0])
    plan = _Plan(shards, jnp.reshape(ci, (1,)).astype(jnp.int32))

    gains = (norm_mix, norm_mem_q, norm_mem_kv, norm_ffn, norm_final.reshape(1, d))
    dx0, small = _local_step(x[0], mem[0], loss_target[0], gains, plan)

    grads, deltas, new_m, new_v = {}, {}, {}, {}
    for k in BIG:
        lead = big_w[k].shape
        wv, mv, vv = flip(big_w[k][0], k), flip(big_m[k][0], k), flip(big_v[k][0], k)
        if k == "in":
            half = wv.shape[0] * IN_SPLIT[0] // IN_SPLIT[1]
            lo = _adam_sharded(wv[:half], mv[:half], vv[:half], plan.chip_sums["in0"], plan.from_chips["in0"], chip,
                               name="adam_in0")
            hi = _adam_sharded(wv[half:], mv[half:], vv[half:], plan.chip_sums["in1"], plan.from_chips["in1"], chip,
                               name="adam_in1")
            outs = [jnp.concatenate(pair, axis=0) for pair in zip(lo, hi)]
        else:
            outs = _adam_sharded(wv, mv, vv, plan.chip_sums[k], plan.from_chips[k], chip, name="adam_" + k)
        grads[k], deltas[k], new_m[k], new_v[k] = (flip(t, k).reshape(lead) for t in outs)

    dg_mix, dg_memq, dg_memkv, dg_ffn, dg_fin, dconv_w, loss = small
    conv_rows = jnp.zeros((3, d), F32).at[:, :CONV_WIDTH].set(dconv_w[:3])
    block = jnp.concatenate([dg_mix[:1], dg_memq[:1], dg_memkv[:1], dg_ffn[:1], dg_fin[:1], conv_rows,
                             jnp.broadcast_to(loss[:1, :1], (1, d)), jnp.zeros((SMALL_ROWS - 9, d), F32)], axis=0)
    total = _sum_devices(_exchange(_gather_comm([block]), name="gather_small")[0], name="sum_small")
    g_conv = lax.dynamic_slice(total[5:8, :CONV_WIDTH], (0, dev * n_conv), (3, n_conv))
    small_w = [norm_mix, norm_mem_q, norm_mem_kv, norm_ffn, norm_final.reshape(1, d), conv_w[0]]
    small_m = [m_norm_mix, m_norm_mem_q, m_norm_mem_kv, m_norm_ffn, m_norm_final.reshape(1, d), m_conv_w[0]]
    small_v = [v_norm_mix, v_norm_mem_q, v_norm_mem_kv, v_norm_ffn, v_norm_final.reshape(1, d), v_conv_w[0]]
    small_g = [total[0:1], total[1:2], total[2:3], total[3:4], total[4:5], g_conv]
    small_names = ["norm_mix", "norm_mem_q", "norm_mem_kv", "norm_ffn", "norm_final", "conv_w"]
    sg, sd, sm, sv = {}, {}, {}, {}
    for nme, wv, g, m, v in zip(small_names, small_w, small_g, small_m, small_v):
        dl, nm, nv = _adam_small(wv, g, m, v, name="adam_" + nme)
        shape = norm_final.shape if nme == "norm_final" else (conv_w.shape if nme == "conv_w" else wv.shape)
        sg[nme], sd[nme], sm[nme], sv[nme] = (t.reshape(shape) for t in (g, dl, nm, nv))

    def ordered(big, sml):
        return (sml["norm_mix"], big["in"], sml["conv_w"], big["a"], big["b"], big["mix"], sml["norm_mem_q"],
                sml["norm_mem_kv"], big["mq"], big["kv"], big["mo"], sml["norm_ffn"], big["fi"], big["fo"],
                sml["norm_final"])

    loss_out = total[8, 0]
    grad_x = dx0.reshape(x.shape)
    return (loss_out, grad_x, *ordered(grads, sg), *ordered(deltas, sd), *ordered(new_m, sm), *ordered(new_v, sv))
```

```python
import functools
import math

import jax
import jax.numpy as jnp
from jax import lax
from jax.experimental import pallas as pl
from jax.experimental.pallas import tpu as pltpu

F32 = jnp.float32
BF16 = jnp.bfloat16
MESH = pl.DeviceIdType.MESH

N_DEV = 8
N_CHIP = 4
NORM_EPS = 1e-6
SB_HEADS = 8
SB_HEAD_DIM = 64
SB_WIDTH = SB_HEADS * SB_HEAD_DIM
CONV_WIDTH = 512
MEM_HEADS = 4
ADAM_LR = 0.001
ADAM_B1 = 0.9
ADAM_B2 = 0.999
ADAM_EPS = 1e-08
ADAM_WD = 0.01
ADAM_STEP = 10

LANES = 128
VMEM_LIMIT_BYTES = 52 * 1024 * 1024
SB_TILE = 256
SB_STEP_HEADS = 4
SB_DEAD = 159.0
SB_CLAMP = 126.0
LOG2_E = 1.4426950408889634

ANY = pl.BlockSpec(memory_space=pl.ANY)


def _params(n_grid):
    return pltpu.CompilerParams(dimension_semantics=("arbitrary",) * n_grid, vmem_limit_bytes=VMEM_LIMIT_BYTES)


def _bdot(a, b, dims):
    return lax.dot_general(a.astype(BF16), b.astype(BF16), (dims, ((), ())), preferred_element_type=F32)


NN = ((1,), (0,))
NT = ((1,), (1,))
TN = ((0,), (0,))


class _Comm:
    def __init__(self, ins, outs, n_sems, start, finish, relay=None):
        self.ins, self.outs, self.n_sems, self.start, self.finish = ins, outs, n_sems, start, finish
        self.relay = relay if relay is not None else (lambda ins, outs, sems: None)

    def sem_shapes(self):
        return [pltpu.SemaphoreType.DMA((k,)) for k in self.n_sems]


def _place():
    return lax.axis_index("x"), lax.axis_index("y"), lax.axis_index("c")


def _neighbours(x, y, c):
    return [(jnp.bitwise_xor(x, c), jnp.bitwise_xor(y, 1 - c)), (jnp.bitwise_xor(x, 1 - c), jnp.bitwise_xor(y, c)),
            (1 - x, 1 - y)]


def _gather_comm(shards):
    n = len(shards)

    def copies(ins, outs, sems):
        send_sems, recv_sems, _ = sems
        x, y, c = _place()
        chips = _neighbours(x, y, c)
        sibling_chips = [chips[1], chips[0], chips[2]]

        def copy(a, k, block, to, from_shard=False):
            dst = outs[a].at[4 * block[0] + 2 * block[1] + block[2]]
            return pltpu.make_async_remote_copy(
                src_ref=ins[a] if from_shard else dst, dst_ref=dst, send_sem=send_sems.at[a * 7 + k],
                recv_sem=recv_sems.at[a * 7 + k], device_id=to, device_id_type=MESH)

        me, sibling = (x, y, c), (x, y, 1 - c)
        own = [[copy(a, 0, me, sibling, True), copy(a, 1, me, (*chips[0], c), True), copy(a, 2, me, (*chips[1], c), True)]
               for a in range(n)]
        relayed = [copy(a, 3, (*chips[0], c), (*chips[1], c)) for a in range(n)]
        landed = [[copy(a, 1 + j, (*chip, c), me) for j, chip in enumerate(chips)] for a in range(n)]
        passed = [[copy(a, 4 + j, (*chip, c), sibling) for j, chip in enumerate(chips)] for a in range(n)]
        from_sibling = [[copy(a, 0, sibling, me)] + [copy(a, 4 + j, (*chip, 1 - c), me)
                                                     for j, chip in enumerate(sibling_chips)] for a in range(n)]
        local = [pltpu.make_async_copy(ins[a], outs[a].at[4 * x + 2 * y + c], sems[2].at[a]) for a in range(n)]
        return own, relayed, landed, passed, from_sibling, local

    def start(ins, outs, sems):
        own, _, _, _, _, local = copies(ins, outs, sems)
        for a in range(n):
            local[a].start()
            for cp in own[a]:
                cp.start()

    def relay(ins, outs, sems):
        _, relayed, landed, passed, _, _ = copies(ins, outs, sems)
        for a in range(n):
            landed[a][0].wait_recv()
            relayed[a].start()
            passed[a][0].start()

    def finish(ins, outs, sems):
        own, relayed, landed, passed, from_sibling, local = copies(ins, outs, sems)
        for a in range(n):
            for arrived, onward in zip(landed[a][1:], passed[a][1:]):
                arrived.wait_recv()
                onward.start()
        for a in range(n):
            for cp in from_sibling[a]:
                cp.wait_recv()
        for a in range(n):
            for cp in own[a] + [relayed[a]] + passed[a]:
                cp.wait_send()
            local[a].wait()

    outs = [jax.ShapeDtypeStruct((N_DEV,) + s.shape, s.dtype) for s in shards]
    return _Comm(list(shards), outs, (7 * n, 7 * n, n), start, finish, relay)


def _sibling_comm(parts):
    n = len(parts)

    def copies(ins, outs, sems):
        x, y, c = _place()
        return [pltpu.make_async_remote_copy(
            src_ref=ins[a].at[2 * q + 1 - c], dst_ref=outs[a].at[q], send_sem=sems[0].at[a * N_CHIP + q],
            recv_sem=sems[1].at[a * N_CHIP + q], device_id=(x, y, 1 - c), device_id_type=MESH)
            for a in range(n) for q in range(N_CHIP)]

    def start(ins, outs, sems):
        for cp in copies(ins, outs, sems):
            cp.start()

    def finish(ins, outs, sems):
        cps = copies(ins, outs, sems)
        for cp in cps:
            cp.wait_recv()
        for cp in cps:
            cp.wait_send()

    outs = [jax.ShapeDtypeStruct((N_CHIP,) + p.shape[1:], p.dtype) for p in parts]
    return _Comm(list(parts), outs, (N_CHIP * n, N_CHIP * n), start, finish)


def _chips_comm(parts):
    n = len(parts)

    def copies(ins, outs, sems):
        x, y, c = _place()
        chips = [(1 - x, y), (x, 1 - y), (1 - x, 1 - y)]
        return [pltpu.make_async_remote_copy(
            src_ref=ins[a].at[2 * px + py], dst_ref=outs[a].at[j], send_sem=sems[0].at[a * 3 + j],
            recv_sem=sems[1].at[a * 3 + j], device_id=(px, py, c), device_id_type=MESH)
            for a in range(n) for j, (px, py) in enumerate(chips)]

    def start(ins, outs, sems):
        for cp in copies(ins, outs, sems):
            cp.start()

    def finish(ins, outs, sems):
        cps = copies(ins, outs, sems)
        for cp in cps:
            cp.wait_recv()
        for cp in cps:
            cp.wait_send()

    outs = [jax.ShapeDtypeStruct((3,) + p.shape[1:], p.dtype) for p in parts]
    return _Comm(list(parts), outs, (3 * n, 3 * n), start, finish)


def _join_comms(comms):
    if len(comms) == 1:
        return comms[0]

    def split(refs, counts):
        out, at = [], 0
        for n in counts:
            out.append(refs[at:at + n])
            at += n
        return out

    def each(method):
        def run(ins, outs, sems):
            parts = zip(comms, split(ins, [len(c.ins) for c in comms]), split(outs, [len(c.outs) for c in comms]),
                        split(sems, [len(c.n_sems) for c in comms]))
            for c, c_ins, c_outs, c_sems in parts:
                getattr(c, method)(c_ins, c_outs, c_sems)
        return run

    return _Comm([a for c in comms for a in c.ins], [o for c in comms for o in c.outs],
                 tuple(k for c in comms for k in c.n_sems), each("start"), each("finish"), each("relay"))


def _exchange(comm, *, name):
    n_ci, n_co = len(comm.ins), len(comm.outs)

    def kern(*refs):
        c_ins, c_outs, sems = refs[:n_ci], refs[n_ci:n_ci + n_co], refs[n_ci + n_co:]
        comm.start(c_ins, c_outs, sems)
        comm.relay(c_ins, c_outs, sems)
        comm.finish(c_ins, c_outs, sems)

    return pl.pallas_call(kern, name=name, in_specs=[ANY] * n_ci, out_specs=[ANY] * n_co, out_shape=comm.outs,
                          scratch_shapes=comm.sem_shapes())(*comm.ins)


def _call(body, *, name, grid, in_specs, out_specs, out_shape, scratch, args, plan=None):
    comm = plan.comm(name) if plan is not None else None
    if comm is None:
        return list(pl.pallas_call(functools.partial(body), name=name, grid=grid, in_specs=in_specs,
                                   out_specs=out_specs, out_shape=out_shape, scratch_shapes=scratch,
                                   compiler_params=_params(len(grid)))(*args))
    n_in, n_out, n_scr, n_ci, n_co = len(in_specs), len(out_specs), len(scratch), len(comm.ins), len(comm.outs)

    def kern(*refs):
        ins, c_ins, refs = refs[:n_in], refs[n_in:n_in + n_ci], refs[n_in + n_ci:]
        outs, c_outs, refs = refs[:n_out], refs[n_out:n_out + n_co], refs[n_out + n_co:]
        scr, sems = refs[:n_scr], refs[n_scr:]
        ids = [pl.program_id(ax) for ax in range(len(grid))]
        step = functools.reduce(lambda at, ig: at * ig[1] + ig[0], zip(ids, grid), 0)
        n_steps = math.prod(grid)

        @pl.when(step == 0)
        def _():
            comm.start(c_ins, c_outs, sems)

        @pl.when(step == n_steps // 3)
        def _():
            comm.relay(c_ins, c_outs, sems)
        body(*ins, *outs, *scr)

        @pl.when(step == n_steps - 1)
        def _():
            comm.finish(c_ins, c_outs, sems)

    res = pl.pallas_call(kern, name=name, grid=grid, in_specs=list(in_specs) + [ANY] * n_ci,
                         out_specs=list(out_specs) + [ANY] * n_co, out_shape=list(out_shape) + comm.outs,
                         scratch_shapes=list(scratch) + comm.sem_shapes(),
                         compiler_params=_params(len(grid)))(*args, *comm.ins)
    plan.landed(name, list(res[n_out:]))
    return list(res[:n_out])


def _mm_body(dims, has_add, *refs):
    if has_add:
        a_ref, b_ref, add_ref, o_ref = refs
        total = _bdot(a_ref[...], b_ref[...], dims) + add_ref[...]
    else:
        a_ref, b_ref, o_ref = refs
        total = _bdot(a_ref[...], b_ref[...], dims)
    o_ref[...] = total.astype(o_ref.dtype)


def _mm_nt_body(j, n, dy_ref, w_ref, o_ref):
    total = _bdot(dy_ref[:, 0:n], w_ref[0], NT)
    for jj in range(1, j):
        total = total + _bdot(dy_ref[:, jj * n:(jj + 1) * n], w_ref[jj], NT)
    o_ref[...] = total.astype(o_ref.dtype)


def _mm_nn(a, w3, *, name, out_dtype=BF16, add=None, tm=1024, tn=None, out3=False, w_t=False, plan=None):
    m, kk = a.shape
    j, n = w3.shape[0], w3.shape[1 if w_t else 2]
    tm, tn = min(tm, m), n if tn is None else tn
    n_t = n // tn
    in_specs = [pl.BlockSpec((tm, kk), lambda i, jj: (i, 0)),
                pl.BlockSpec((None, tn, kk), lambda i, jj: (jj // n_t, jj % n_t, 0)) if w_t else
                pl.BlockSpec((None, kk, tn), lambda i, jj: (jj // n_t, 0, jj % n_t))]
    args = [a, w3]
    if add is not None:
        in_specs.append(pl.BlockSpec((tm, tn), lambda i, jj: (i, jj)))
        args.append(add)
    if out3:
        out_spec = pl.BlockSpec((None, tm, tn), lambda i, jj: (jj // n_t, i, jj % n_t))
        out_shape = jax.ShapeDtypeStruct((j, m, n), out_dtype)
    else:
        out_spec = pl.BlockSpec((tm, tn), lambda i, jj: (i, jj))
        out_shape = jax.ShapeDtypeStruct((m, j * n), out_dtype)
    return _call(
        functools.partial(_mm_body, NT if w_t else NN, add is not None), name=name, grid=(m // tm, j * n_t),
        in_specs=in_specs, out_specs=[out_spec], out_shape=[out_shape], scratch=[], args=args, plan=plan)[0]


def _mm_gathering(a, shard, *, name, out3=False, w_t=False, tm=1024):
    m, kk = a.shape
    n = shard.shape[0 if w_t else 1]
    tm = min(tm, m)
    n_i = m // tm
    fetch_at = min(1, n_i - 1)

    def body(a_ref, shard_ref, o_ref, w_all, w_vmem, send_sems, recv_sems, copy_sems):
        jj, i = pl.program_id(0), pl.program_id(1)
        x, y, c = _place()
        me, sibling = (x, y, c), (x, y, 1 - c)
        chips = _neighbours(x, y, c)
        sibling_chips = [chips[1], chips[0], chips[2]]

        def rows(block):
            return w_all.at[4 * block[0] + 2 * block[1] + block[2]]

        def remote(k, block, to, from_shard=False):
            return pltpu.make_async_remote_copy(
                src_ref=shard_ref if from_shard else rows(block), dst_ref=rows(block), send_sem=send_sems.at[k],
                recv_sem=recv_sems.at[k], device_id=to, device_id_type=MESH)

        def load(step, src):
            return pltpu.make_async_copy(src, w_vmem.at[step % 2], copy_sems.at[1 + step % 2])

        own = [remote(0, me, sibling, True), remote(1, me, (*chips[0], c), True), remote(2, me, (*chips[1], c), True),
               remote(3, (*chips[0], c), (*chips[1], c))]
        passed = [remote(4 + j, (*chip, c), sibling) for j, chip in enumerate(chips)]
        local = pltpu.make_async_copy(shard_ref, rows(me), copy_sems.at[0])

        @pl.when(jnp.logical_and(i == 0, jj == 0))
        def _():
            local.start()
            own[0].start()
            own[1].start()
            load(0, shard_ref).start()

        def arrivals():
            yield 1, (lambda: remote(0, sibling, me).wait_recv()), sibling
            for j, chip in enumerate(chips):
                def landed(j=j, chip=chip):
                    if j < 2:
                        own[1 + j].wait_send()
                        own[2 + j].start()
                    remote(1 + j, (*chip, c), me).wait_recv()
                    passed[j].start()
                yield 2 + 2 * j, landed, (*chip, c)
                block = (*sibling_chips[j], 1 - c)
                yield 3 + 2 * j, (lambda j=j, block=block: remote(4 + j, block, me).wait_recv()), block

        for step, wait_for_it, block in arrivals():
            @pl.when(jnp.logical_and(i == fetch_at, jj == step - 1))
            def _():
                wait_for_it()
                load(step, rows(block)).start()

        for step in range(N_DEV):
            @pl.when(jnp.logical_and(i == 0, jj == step))
            def _():
                load(step, rows(me)).wait()

        o_ref[...] = _bdot(a_ref[...], w_vmem[lax.rem(jj, 2)], NT if w_t else NN).astype(o_ref.dtype)

        @pl.when(jnp.logical_and(i == n_i - 1, jj == N_DEV - 1))
        def _():
            for cp in [own[0], own[3]] + passed:
                cp.wait_send()
            local.wait()

    def swept(jj):
        x, y, c = _place()
        first, second = 2 + 2 * c, 4 - 2 * c
        flips = (0b000, 0b001, first, second + 1, second, first + 1, 0b110, 0b111)
        return jnp.bitwise_xor(4 * x + 2 * y + c, sum(jnp.where(jj == k, f, 0) for k, f in enumerate(flips)))

    if out3:
        out_spec = pl.BlockSpec((None, tm, n), lambda jj, i: (swept(jj), i, 0))
        out_shape = jax.ShapeDtypeStruct((N_DEV, m, n), BF16)
    else:
        out_spec = pl.BlockSpec((tm, n), lambda jj, i: (i, swept(jj)))
        out_shape = jax.ShapeDtypeStruct((m, N_DEV * n), BF16)
    return pl.pallas_call(
        body, name=name, grid=(N_DEV, n_i),
        in_specs=[pl.BlockSpec((tm, kk), lambda jj, i: (i, 0)), ANY], out_specs=[out_spec, ANY],
        scratch_shapes=[pltpu.VMEM((2,) + shard.shape, shard.dtype), pltpu.SemaphoreType.DMA((7,)),
                        pltpu.SemaphoreType.DMA((7,)), pltpu.SemaphoreType.DMA((3,))],
        out_shape=[out_shape, jax.ShapeDtypeStruct((N_DEV,) + shard.shape, shard.dtype)],
        compiler_params=_params(2))(a, shard)


def _sigmoid(v):
    return 0.5 * jnp.tanh(0.5 * v) + 0.5


def _resident(w):
    return pl.BlockSpec(w.shape, lambda i: (0,) * w.ndim, pipeline_mode=pl.Buffered(1))


def _ffn_out_loss(gu3, w3, add, g, target, *, name, tm=512):
    j2, m, n = gu3.shape
    j = j2 // 2
    nn = w3.shape[2]
    tm = min(tm, m)

    def body(gu_ref, w_ref, add_ref, g_ref, t_ref, dx_ref, dxb_ref, dg_ref, loss_ref, act_ref):
        i = pl.program_id(0)
        xv = add_ref[...]
        for jj in range(j):
            gate = gu_ref[0, jj].astype(F32)
            act = (gate * _sigmoid(gate) * gu_ref[1, jj].astype(F32)).astype(BF16)
            act_ref[jj] = act
            xv = xv + _bdot(act, w_ref[jj], NN)
        gv = g_ref[...]
        r = lax.rsqrt(jnp.mean(xv * xv, axis=-1, keepdims=True) + NORM_EPS)
        xhat = xv * r
        err = xhat * gv - t_ref[...]
        _acc_rows(i, loss_ref, 0.5 * jnp.sum(jnp.mean(err * err, axis=-1, keepdims=True), axis=0, keepdims=True))
        dy = err * (1.0 / nn)
        dxhat = dy * gv
        dx = r * (dxhat - xhat * jnp.mean(dxhat * xhat, axis=-1, keepdims=True))
        dx_ref[...] = dx
        dxb_ref[...] = dx.astype(BF16)
        _acc_rows(i, dg_ref, jnp.sum(dy * xhat, axis=0, keepdims=True))

    row = pl.BlockSpec((tm, nn), lambda i: (i, 0))
    return _call(body, name=name, grid=(m // tm,),
                 in_specs=[pl.BlockSpec((2, j, tm, n), lambda i: (0, 0, i, 0)), _resident(w3),
                           row, pl.BlockSpec(g.shape, lambda i: (0, 0)), row],
                 out_specs=[row, row, pl.BlockSpec((8, nn), lambda i: (0, 0)), pl.BlockSpec((8, LANES), lambda i: (0, 0)),
                            pl.BlockSpec((j, tm, n), lambda i: (0, i, 0))],
                 out_shape=[jax.ShapeDtypeStruct((m, nn), F32), jax.ShapeDtypeStruct((m, nn), BF16),
                            jax.ShapeDtypeStruct((8, nn), F32), jax.ShapeDtypeStruct((8, LANES), F32),
                            jax.ShapeDtypeStruct((j, m, n), BF16)],
                 scratch=[], args=[gu3.reshape(2, j, m, n), w3, add, g, target])


def _ffn_out_bwd(dy, w3, gu3, *, name, tm=1024):
    m, nn = dy.shape
    j, n, _ = w3.shape
    tm = min(tm, m)

    def body(dy_ref, w_ref, gu_ref, dgu_ref):
        da = _bdot(dy_ref[...], w_ref[...], NT)
        gate = gu_ref[0].astype(F32)
        up = gu_ref[1].astype(F32)
        sg = _sigmoid(gate)
        silu = gate * sg
        dgu_ref[0] = (da * up * (sg + silu * (1.0 - sg))).astype(BF16)
        dgu_ref[1] = (da * silu).astype(BF16)

    out = _call(body, name=name, grid=(m // tm, j),
                in_specs=[pl.BlockSpec((tm, nn), lambda i, jj: (i, 0)),
                          pl.BlockSpec((None, n, nn), lambda i, jj: (jj, 0, 0)),
                          pl.BlockSpec((2, None, tm, n), lambda i, jj: (0, jj, i, 0))],
                out_specs=[pl.BlockSpec((2, None, tm, n), lambda i, jj: (0, jj, i, 0))],
                out_shape=[jax.ShapeDtypeStruct((2, j, m, n), BF16)], scratch=[],
                args=[dy, w3, gu3.reshape(2, j, m, n)])[0]
    return out.reshape(2 * j, m, n)


def _rms_fwd_tail(xv, g_ref, h_ref):
    r = lax.rsqrt(jnp.mean(xv * xv, axis=-1, keepdims=True) + NORM_EPS)
    h_ref[...] = (xv * r * g_ref[...]).astype(BF16)


def _rms_bwd_tail(i, dh, x_ref, g_ref, dres_ref, dx_ref, dxb_ref, dg_ref):
    xv = x_ref[...]
    r = lax.rsqrt(jnp.mean(xv * xv, axis=-1, keepdims=True) + NORM_EPS)
    xhat = xv * r
    dxhat = dh * g_ref[...]
    dx = r * (dxhat - xhat * jnp.mean(dxhat * xhat, axis=-1, keepdims=True))
    if dres_ref is not None:
        dx = dx + dres_ref[...]
    dx_ref[...] = dx
    dxb_ref[...] = dx.astype(BF16)
    _acc_rows(i, dg_ref, jnp.sum(dh * xhat, axis=0, keepdims=True))


def _mm_nt_rms(dy, w3, x, g, dres, *, name, dy3=False, w_nn=False, tm=512, plan=None):
    j = w3.shape[0]
    m, kk = x.shape
    n = dy.shape[2] if dy3 else dy.shape[1] // j
    tm = min(tm, m)

    def body(dy_ref, w_ref, x_ref, g_ref, *rest):
        dres_ref = rest[0] if dres is not None else None
        dx_ref, dxb_ref, dg_ref = rest[-3:]
        dh = None
        for jj in range(j):
            piece = dy_ref[jj] if dy3 else dy_ref[:, jj * n:(jj + 1) * n]
            part = _bdot(piece, w_ref[jj], NN if w_nn else NT)
            dh = part if dh is None else dh + part
        _rms_bwd_tail(pl.program_id(0), dh, x_ref, g_ref, dres_ref, dx_ref, dxb_ref, dg_ref)

    row = pl.BlockSpec((tm, kk), lambda i: (i, 0))
    in_specs = [pl.BlockSpec((j, tm, n), lambda i: (0, i, 0)) if dy3 else pl.BlockSpec((tm, j * n), lambda i: (i, 0)),
                _resident(w3), row, pl.BlockSpec(g.shape, lambda i: (0, 0))]
    args = [dy, w3, x, g]
    if dres is not None:
        in_specs.append(row)
        args.append(dres)
    return _call(body, name=name, grid=(m // tm,), in_specs=in_specs,
                 out_specs=[row, row, pl.BlockSpec((8, kk), lambda i: (0, 0))],
                 out_shape=[jax.ShapeDtypeStruct((m, kk), F32), jax.ShapeDtypeStruct((m, kk), BF16),
                            jax.ShapeDtypeStruct((8, kk), F32)], scratch=[], args=args, plan=plan)


def _mix_out(o_a, y_b, proj, w_a, w_b, w, x, g, *, name, tm=512, plan=None):
    s, c = o_a.shape
    d = w.shape[1]
    tm = min(tm, s)

    def body(oa_ref, yb_ref, ga_ref, gb_ref, wa_ref, wb_ref, w_ref, x_ref, g_ref, x1_ref, h_ref, merged_ref, a_ref, b_ref):
        a_ref[...] = _bdot(oa_ref[...], wa_ref[...], NN).astype(BF16)
        b_ref[...] = _bdot(yb_ref[...], wb_ref[...], NN).astype(BF16)
        merged = (_sigmoid(ga_ref[...].astype(F32)) * a_ref[...].astype(F32)
                  + _sigmoid(gb_ref[...].astype(F32)) * b_ref[...].astype(F32)).astype(BF16)
        merged_ref[...] = merged
        xv = _bdot(merged, w_ref[...], NN) + x_ref[...]
        x1_ref[...] = xv
        _rms_fwd_tail(xv, g_ref, h_ref)

    row = pl.BlockSpec((tm, d), lambda i: (i, 0))
    narrow = pl.BlockSpec((tm, c), lambda i: (i, 0))
    whole = lambda arr: pl.BlockSpec(arr.shape, lambda i: (0,) * arr.ndim)
    return _call(body, name=name, grid=(s // tm,),
                 in_specs=[narrow, narrow, pl.BlockSpec((tm, d), lambda i: (i, 3)), pl.BlockSpec((tm, d), lambda i: (i, 4)),
                           whole(w_a), whole(w_b), whole(w), row, whole(g)],
                 out_specs=[row] * 5,
                 out_shape=[jax.ShapeDtypeStruct((s, d), F32)] + [jax.ShapeDtypeStruct((s, d), BF16)] * 4,
                 scratch=[], args=[o_a, y_b, proj, proj, w_a, w_b, w, x, g], plan=plan)


def _mm_tn_a3(a3, dy, *, name):
    j, t, n = a3.shape
    nn = dy.shape[1]
    return _call(functools.partial(_mm_body, TN, False), name=name, grid=(j,),
                 in_specs=[pl.BlockSpec((None, t, n), lambda jj: (jj, 0, 0)), pl.BlockSpec((t, nn), lambda jj: (0, 0))],
                 out_specs=[pl.BlockSpec((None, n, nn), lambda jj: (jj, 0, 0))],
                 out_shape=[jax.ShapeDtypeStruct((j, n, nn), BF16)], scratch=[], args=[a3, dy])[0]


def _mm_nt(dy, w3, *, name, out_dtype=BF16, tm=512, tn=1024, plan=None):
    m = dy.shape[0]
    j, kk, n = w3.shape
    tm, tn = min(tm, m), min(tn, kk)
    return _call(
        functools.partial(_mm_nt_body, j, n), name=name,
        grid=(m // tm, kk // tn),
        in_specs=[pl.BlockSpec((tm, j * n), lambda i, q: (i, 0)),
                  pl.BlockSpec((j, tn, n), lambda i, q: (0, q, 0))],
        out_specs=[pl.BlockSpec((tm, tn), lambda i, q: (i, q))],
        out_shape=[jax.ShapeDtypeStruct((m, kk), out_dtype)], scratch=[], args=[dy, w3], plan=plan)[0]


def _mm_tn(a, dy, n, *, name, out_dtype=BF16, tm=512, tn=None, k_tiles=None, plan=None):
    t, kk = a.shape
    j = dy.shape[1] // n
    tm, tn = min(tm, kk), n if tn is None else tn
    n_t = n // tn
    first, count = (0, kk // tm) if k_tiles is None else k_tiles
    return _call(
        functools.partial(_mm_body, TN, False), name=name,
        grid=(count, j * n_t),
        in_specs=[pl.BlockSpec((t, tm), lambda i, jj: (0, first + i)),
                  pl.BlockSpec((t, tn), lambda i, jj: (0, jj))],
        out_specs=[pl.BlockSpec((None, tm, tn), lambda i, jj: (jj // n_t, i, jj % n_t))],
        out_shape=[jax.ShapeDtypeStruct((j, count * tm, n), out_dtype)], scratch=[], args=[a, dy], plan=plan)[0]


def _rows(body, ins, outs, *, n_rows, tm, name, plan=None):
    tm = min(tm, n_rows)
    n_steps = n_rows // tm
    in_specs, args = [], []
    for arr, kind, width, block in ins:
        if kind == "row":
            in_specs.append(pl.BlockSpec((tm, width), functools.partial(lambda i, b: (i, b), b=block)))
        elif kind == "prev":
            in_specs.append(pl.BlockSpec((tm, width), functools.partial(lambda i, b: (jnp.maximum(i - 1, 0), b), b=block)))
        elif kind == "next":
            in_specs.append(pl.BlockSpec((tm, width), functools.partial(lambda i, b: (jnp.minimum(i + 1, n_steps - 1), b), b=block)))
        else:
            in_specs.append(pl.BlockSpec(arr.shape, functools.partial(lambda i, nd: (0,) * nd, nd=arr.ndim)))
        args.append(arr)
    out_specs, out_shape = [], []
    for shape, dtype, kind in outs:
        if kind == "row":
            out_specs.append(pl.BlockSpec((tm, shape[1]), lambda i: (i, 0)))
        else:
            out_specs.append(pl.BlockSpec(shape, functools.partial(lambda i, nd: (0,) * nd, nd=len(shape))))
        out_shape.append(jax.ShapeDtypeStruct(shape, dtype))

    def kern(*refs):
        body(pl.program_id(0), n_steps, *refs)

    return _call(kern, name=name, grid=(n_steps,), in_specs=in_specs, out_specs=out_specs, out_shape=out_shape,
                 scratch=[], args=args, plan=plan)


def _acc_rows(i, ref, value):
    @pl.when(i == 0)
    def _():
        ref[...] = jnp.zeros_like(ref)
    ref[...] += jnp.broadcast_to(value, ref.shape)


def _rms_fwd(x, g, *, name, tm=512):
    s, d = x.shape

    def body(i, n, x_ref, g_ref, h_ref):
        _rms_fwd_tail(x_ref[...], g_ref, h_ref)

    return _rows(body, [(x, "row", d, 0), (g, "full", 0, 0)], [((s, d), BF16, "row")], n_rows=s, tm=tm, name=name)[0]


def _rms_bwd(x, g, dh, dres, *, name, tm=512, plan=None):
    s, d = x.shape

    def body(i, n, x_ref, g_ref, dh_ref, dres_ref, dx_ref, dxb_ref, dg_ref):
        _rms_bwd_tail(i, dh_ref[...].astype(F32), x_ref, g_ref, dres_ref, dx_ref, dxb_ref, dg_ref)

    return _rows(body, [(x, "row", d, 0), (g, "full", 0, 0), (dh, "row", d, 0), (dres, "row", d, 0)],
                 [((s, d), F32, "row"), ((s, d), BF16, "row"), ((8, d), F32, "acc")],
                 n_rows=s, tm=tm, name=name, plan=plan)


def _mix_out_bwd(dx1b, w, br_a, br_b, proj, w_a, w_b, *, name, tm=512, plan=None):
    s, d = br_a.shape
    c = w_a.shape[0]
    tm = min(tm, s)

    def body(dy_ref, w_ref, a_ref, b_ref, ga_ref, gb_ref, wa_ref, wb_ref, da_ref, db_ref, dg_ref, doa_ref, dyb_ref):
        dm = _bdot(dy_ref[...], w_ref[...], NT)
        sa = _sigmoid(ga_ref[...].astype(F32))
        sb = _sigmoid(gb_ref[...].astype(F32))
        da_ref[...] = (dm * sa).astype(BF16)
        db_ref[...] = (dm * sb).astype(BF16)
        dg_ref[:, :d] = (dm * a_ref[...].astype(F32) * sa * (1.0 - sa)).astype(BF16)
        dg_ref[:, d:] = (dm * b_ref[...].astype(F32) * sb * (1.0 - sb)).astype(BF16)
        doa_ref[...] = _bdot(da_ref[...], wa_ref[...], NT).astype(BF16)
        dyb_ref[...] = _bdot(db_ref[...], wb_ref[...], NT).astype(BF16)

    row = pl.BlockSpec((tm, d), lambda i: (i, 0))
    narrow = pl.BlockSpec((tm, c), lambda i: (i, 0))
    whole = lambda arr: pl.BlockSpec(arr.shape, lambda i: (0,) * arr.ndim)
    return _call(body, name=name, grid=(s // tm,),
                 in_specs=[row, whole(w), row, row, pl.BlockSpec((tm, d), lambda i: (i, 3)),
                           pl.BlockSpec((tm, d), lambda i: (i, 4)), whole(w_a), whole(w_b)],
                 out_specs=[row, row, pl.BlockSpec((tm, 2 * d), lambda i: (i, 0)), narrow, narrow],
                 out_shape=[jax.ShapeDtypeStruct((s, d), BF16), jax.ShapeDtypeStruct((s, d), BF16),
                            jax.ShapeDtypeStruct((s, 2 * d), BF16), jax.ShapeDtypeStruct((s, c), BF16),
                            jax.ShapeDtypeStruct((s, c), BF16)],
                 scratch=[], args=[dx1b, w, br_a, br_b, proj, proj, w_a, w_b], plan=plan)


def _shift_down(cur, prev, k, first):
    row = lax.broadcasted_iota(jnp.int32, cur.shape, 0)
    out = jnp.where(row >= k, pltpu.roll(cur, k, 0), pltpu.roll(prev, k, 0))
    return jnp.where(jnp.logical_and(first, row < k), 0.0, out)


def _shift_up(cur, nxt, k, last):
    tm = cur.shape[0]
    row = lax.broadcasted_iota(jnp.int32, cur.shape, 0)
    out = jnp.where(row < tm - k, pltpu.roll(cur, tm - k, 0), pltpu.roll(nxt, tm - k, 0))
    return jnp.where(jnp.logical_and(last, row >= tm - k), 0.0, out)


def _conv_fwd(proj, conv_w, *, name, tm=512):
    s = proj.shape[0]
    c = CONV_WIDTH

    def body(i, n, u_ref, gb_ref, gc_ref, up_ref, gcp_ref, w_ref, y_ref):
        cu = gc_ref[...].astype(F32) * u_ref[...].astype(F32)
        cup = gcp_ref[...].astype(F32) * up_ref[...].astype(F32)
        first = i == 0
        y = (w_ref[0:1, :] * _shift_down(cu, cup, 2, first) + w_ref[1:2, :] * _shift_down(cu, cup, 1, first)
             + w_ref[2:3, :] * cu)
        y_ref[...] = (gb_ref[...].astype(F32) * y).astype(BF16)

    return _rows(body, [(proj, "row", c, 3), (proj, "row", c, 4), (proj, "row", c, 5),
                        (proj, "prev", c, 3), (proj, "prev", c, 5), (conv_w, "full", 0, 0)],
                 [((s, c), BF16, "row")], n_rows=s, tm=tm, name=name)[0]


def _conv_bwd(dy_b, proj, conv_w, *, name, tm=512, plan=None):
    s = proj.shape[0]
    c = CONV_WIDTH

    def body(i, n, dy_ref, u_ref, gb_ref, gc_ref, up_ref, gcp_ref, dyn_ref, gbn_ref, w_ref, d_ref, dw_ref):
        first, last = i == 0, i == n - 1
        u = u_ref[...].astype(F32)
        gb = gb_ref[...].astype(F32)
        gc = gc_ref[...].astype(F32)
        cu = gc * u
        cup = gcp_ref[...].astype(F32) * up_ref[...].astype(F32)
        cu1 = _shift_down(cu, cup, 1, first)
        cu2 = _shift_down(cu, cup, 2, first)
        conv = w_ref[0:1, :] * cu2 + w_ref[1:2, :] * cu1 + w_ref[2:3, :] * cu
        dy = dy_ref[...].astype(F32)
        dyc = dy * gb
        dycn = dyn_ref[...].astype(F32) * gbn_ref[...].astype(F32)
        dcu = (w_ref[2:3, :] * dyc + w_ref[1:2, :] * _shift_up(dyc, dycn, 1, last)
               + w_ref[0:1, :] * _shift_up(dyc, dycn, 2, last))
        d_ref[:, 0:c] = (dcu * gc).astype(BF16)
        d_ref[:, c:2 * c] = (dy * conv).astype(BF16)
        d_ref[:, 2 * c:3 * c] = (dcu * u).astype(BF16)
        row = lax.broadcasted_iota(jnp.int32, (8, c), 0)
        dw = (jnp.where(row == 0, jnp.sum(dyc * cu2, axis=0, keepdims=True), 0.0)
              + jnp.where(row == 1, jnp.sum(dyc * cu1, axis=0, keepdims=True), 0.0)
              + jnp.where(row == 2, jnp.sum(dyc * cu, axis=0, keepdims=True), 0.0))

        @pl.when(first)
        def _():
            dw_ref[...] = jnp.zeros_like(dw_ref)
        dw_ref[...] += dw

    return _rows(body, [(dy_b, "row", c, 0), (proj, "row", c, 3), (proj, "row", c, 4), (proj, "row", c, 5),
                        (proj, "prev", c, 3), (proj, "prev", c, 5), (dy_b, "next", c, 0), (proj, "next", c, 4),
                        (conv_w, "full", 0, 0)],
                 [((s, 3 * c), BF16, "row"), ((8, c), F32, "acc")], n_rows=s, tm=tm, name=name, plan=plan)


def _mem_probs(q, k, scale):
    sc = _bdot(q, k, NT) * scale
    sc = sc - jnp.max(sc, axis=-1, keepdims=True)
    p = jnp.exp(sc)
    return p / jnp.sum(p, axis=-1, keepdims=True)


def _mem_sublayer(hq, w_q, kv, w_o, x, g, *, name, tm=512, plan=None):
    s, d = hq.shape
    hd = d // MEM_HEADS
    scale = 1.0 / math.sqrt(hd)
    tm = min(tm, s)

    def body(hq_ref, wq_ref, kv_ref, wo_ref, x_ref, g_ref, q_ref, o_ref, x2_ref, h_ref):
        q_ref[...] = _bdot(hq_ref[...], wq_ref[...], NN).astype(BF16)
        for h in range(MEM_HEADS):
            cols = slice(h * hd, (h + 1) * hd)
            p = _mem_probs(q_ref[:, cols], kv_ref[:, cols], scale)
            o_ref[:, cols] = _bdot(p, kv_ref[:, d + h * hd:d + (h + 1) * hd], NN).astype(BF16)
        xv = _bdot(o_ref[...], wo_ref[...], NN) + x_ref[...]
        x2_ref[...] = xv
        _rms_fwd_tail(xv, g_ref, h_ref)

    row = pl.BlockSpec((tm, d), lambda i: (i, 0))
    whole = lambda a: pl.BlockSpec(a.shape, lambda i: (0,) * a.ndim)
    return _call(body, name=name, grid=(s // tm,),
                 in_specs=[row, whole(w_q), whole(kv), whole(w_o), row, whole(g)], out_specs=[row] * 4,
                 out_shape=[jax.ShapeDtypeStruct((s, d), BF16), jax.ShapeDtypeStruct((s, d), BF16),
                            jax.ShapeDtypeStruct((s, d), F32), jax.ShapeDtypeStruct((s, d), BF16)],
                 scratch=[], args=[hq, w_q, kv, w_o, x, g], plan=plan)


def _mem_sublayer_bwd(dx2b, dx2, x, g, qm, kv, w_q, w_o, *, name, tm=512, plan=None):
    s, d = qm.shape
    hd = d // MEM_HEADS
    scale = 1.0 / math.sqrt(hd)
    tm = min(tm, s)

    def body(dyb_ref, dres_ref, x_ref, g_ref, q_ref, kv_ref, wq_ref, wo_ref, dx_ref, dxb_ref, dg_ref, dq_ref, dkv_ref):
        i = pl.program_id(0)

        @pl.when(i == 0)
        def _():
            dkv_ref[...] = jnp.zeros_like(dkv_ref)
        dom = _bdot(dyb_ref[...], wo_ref[...], NT).astype(BF16)
        for h in range(MEM_HEADS):
            cols = slice(h * hd, (h + 1) * hd)
            vcols = slice(d + h * hd, d + (h + 1) * hd)
            q, k, v, do = q_ref[:, cols], kv_ref[:, cols], kv_ref[:, vcols], dom[:, cols]
            p = _mem_probs(q, k, scale)
            dp = _bdot(do, v, NT)
            ds = p * (dp - jnp.sum(dp * p, axis=-1, keepdims=True)) * scale
            dq_ref[:, cols] = _bdot(ds, k, NN).astype(BF16)
            dkv_ref[:, cols] += _bdot(ds, q, TN)
            dkv_ref[:, vcols] += _bdot(p, do, TN)
        dh = _bdot(dq_ref[...], wq_ref[...], NT)
        _rms_bwd_tail(i, dh, x_ref, g_ref, dres_ref, dx_ref, dxb_ref, dg_ref)

    row = pl.BlockSpec((tm, d), lambda i: (i, 0))
    whole = lambda a: pl.BlockSpec(a.shape, lambda i: (0,) * a.ndim)
    return _call(body, name=name, grid=(s // tm,),
                 in_specs=[row, row, row, whole(g), row, whole(kv), whole(w_q), whole(w_o)],
                 out_specs=[row, row, pl.BlockSpec((8, d), lambda i: (0, 0)), row, whole(kv)],
                 out_shape=[jax.ShapeDtypeStruct((s, d), F32), jax.ShapeDtypeStruct((s, d), BF16),
                            jax.ShapeDtypeStruct((8, d), F32), jax.ShapeDtypeStruct((s, d), BF16),
                            jax.ShapeDtypeStruct(kv.shape, F32)],
                 scratch=[], args=[dx2b, dx2, x, g, qm, kv, w_q, w_o], plan=plan)


def _sb_consts(t):
    row = lax.broadcasted_iota(jnp.int32, (t, t), 0)
    col = lax.broadcasted_iota(jnp.int32, (t, t), 1)
    lane = lax.broadcasted_iota(jnp.int32, (t, LANES), 1)
    return row, col, lane < SB_HEAD_DIM


def _sb_logits(q, k):
    z2 = jnp.minimum(_bdot(q, k, NT) * LOG2_E, SB_CLAMP)
    return z2, jnp.exp2(z2)


def _tri_sum(v, tri):
    hi = v.astype(BF16)
    lo = (v - hi.astype(F32)).astype(BF16)
    return _bdot(hi, tri, NN) + _bdot(lo, tri, NN)


def _sb_fwd(proj, *, name, plan=None):
    s = proj.shape[0]
    t, nh = SB_TILE, SB_STEP_HEADS
    n_q = s // t
    scale = 1.0 / math.sqrt(SB_HEAD_DIM)

    def body(q_ref, k_ref, v_ref, o_ref, c_ref, first_ref, acc_ref, c_scr):
        i = pl.program_id(1)
        row, col, head0 = _sb_consts(t)
        later = (row > col).astype(BF16)
        valid = col < row
        lanes = lambda h: slice((h // 2) * LANES, (h // 2 + 1) * LANES)
        q = [jnp.where(head0 == (h % 2 == 0), q_ref[:, lanes(h)] * scale, 0) for h in range(nh)]

        def tiles(kbs, diag_first, carry):
            rows = [pl.ds(pl.multiple_of(kb * t, t), t) for kb in kbs]
            jobs = [(n, h) for n in range(len(kbs)) for h in range(nh)]
            masked = lambda n: diag_first and n == 0
            zs = {(n, h): _sb_logits(q[h], k_ref[rows[n], lanes(h)]) for n, h in jobs}
            fail = {j: jnp.log2(1.0 + zs[j][1]) for j in jobs}
            fail = {j: jnp.where(valid, fail[j], 0.0) if masked(j[0]) else fail[j] for j in jobs}
            cum = {j: _tri_sum(fail[j], later) for j in jobs}
            run, before = list(carry), {}
            for n, h in jobs:
                before[n, h] = run[h]
                run[h] = run[h] + cum[n, h][:, 0:1] + fail[n, h][:, 0:1]
            w = {j: jnp.exp2(zs[j][0] - fail[j] - cum[j] - before[j]) for j in jobs}
            w = {j: jnp.where(valid, w[j], 0.0) if masked(j[0]) else w[j] for j in jobs}
            for n, h in jobs:
                acc_ref[h] += _bdot(w[n, h], v_ref[rows[n], lanes(h)], NN)
            return tuple(run)

        acc_ref[...] = jnp.zeros_like(acc_ref)
        zero = (jnp.zeros((t, 1), F32),) * nh

        def alive(carry):
            return (functools.reduce(jnp.minimum, [jnp.min(c) for c in carry]) < SB_DEAD).astype(jnp.int32)

        def step(state):
            new = tiles([state[0]], False, state[2:])
            return (state[0] - 1, alive(new)) + new

        @pl.when(i == 0)
        def _():
            for h, c in enumerate(tiles([i], True, zero)):
                c_scr[h] = c

        @pl.when(i > 0)
        def _():
            for h, c in enumerate(tiles([i, i - 1], True, zero)):
                c_scr[h] = c
        carry = tuple(c_scr[h] for h in range(nh))
        state = lax.while_loop(lambda st: jnp.logical_and(st[0] >= 0, st[1] > 0), step, (i - 2, alive(carry)) + carry)
        for b in range(nh // 2):
            o_ref[:, b * LANES:(b + 1) * LANES] = jnp.where(head0, acc_ref[2 * b], acc_ref[2 * b + 1]).astype(BF16)
        head = lax.broadcasted_iota(jnp.int32, (t, nh), 1)
        c_ref[...] = sum(jnp.where(head == h, state[2 + h], 0.0) for h in range(nh))
        first_ref[pl.program_id(0), i] = (jnp.maximum(state[0], -1) + 1).astype(F32)

    n_p, width = SB_HEADS // nh, nh * SB_HEAD_DIM
    k_blk, v_blk = SB_WIDTH // width, 2 * SB_WIDTH // width
    return _call(
        body, name=name, grid=(n_p, n_q),
        in_specs=[pl.BlockSpec((t, width), lambda p, i: (i, p)),
                  pl.BlockSpec((s, width), lambda p, i: (0, k_blk + p)),
                  pl.BlockSpec((s, width), lambda p, i: (0, v_blk + p))],
        out_specs=[pl.BlockSpec((t, width), lambda p, i: (i, p)),
                   pl.BlockSpec((None, t, nh), lambda p, i: (p, i, 0)),
                   pl.BlockSpec(memory_space=pltpu.SMEM)],
        out_shape=[jax.ShapeDtypeStruct((s, SB_WIDTH), BF16), jax.ShapeDtypeStruct((n_p, s, nh), F32),
                   jax.ShapeDtypeStruct((n_p, n_q), F32)],
        scratch=[pltpu.VMEM((nh, t, LANES), F32), pltpu.VMEM((nh, t, 1), F32)], args=[proj, proj, proj], plan=plan)


def _sb_bwd(proj, do_a, ctot, first, *, name, plan=None):
    s = proj.shape[0]
    t, nh = SB_TILE, SB_STEP_HEADS
    n_q = s // t
    scale = 1.0 / math.sqrt(SB_HEAD_DIM)

    def body(q_ref, k_ref, v_ref, do_ref, c_ref, first_ref, dq_ref, dk_ref, dv_ref, dq_acc, dk_acc, dv_acc):
        i = pl.program_id(1)
        kb0 = jnp.clip(first_ref[pl.program_id(0), i].astype(jnp.int32), 0, i)
        row, col, head0 = _sb_consts(t)
        upto = (row <= col).astype(BF16)
        before = (row < col).astype(BF16)
        valid = col < row
        lanes = lambda h: slice((h // 2) * LANES, (h // 2 + 1) * LANES)
        q2 = [jnp.where(head0 == (h % 2 == 0), q_ref[:, lanes(h)] * scale, 0) for h in range(nh)]
        do2 = [jnp.where(head0 == (h % 2 == 0), do_ref[:, lanes(h)], 0) for h in range(nh)]
        ctot2 = [c_ref[:, h:h + 1] for h in range(nh)]

        @pl.when(i == 0)
        def _():
            dk_acc[...] = jnp.zeros_like(dk_acc)
            dv_acc[...] = jnp.zeros_like(dv_acc)
        dq_acc[...] = jnp.zeros_like(dq_acc)

        def tiles(kbs, diag_last, carry):
            rows = [pl.ds(pl.multiple_of(kb * t, t), t) for kb in kbs]
            kt = {(n, h): k_ref[rows[n], lanes(h)] for n in range(len(kbs)) for h in range(nh)}
            jobs = list(kt)
            masked = lambda n: diag_last and n == len(kbs) - 1
            t_last = slice(t - 1, t)
            zs = {(n, h): _sb_logits(q2[h], kt[n, h]) for n, h in jobs}
            dw = {(n, h): _bdot(do2[h], v_ref[rows[n], lanes(h)], NT) for n, h in jobs}
            fail = {j: jnp.log2(1.0 + zs[j][1]) for j in jobs}
            fail = {j: jnp.where(valid, fail[j], 0.0) if masked(j[0]) else fail[j] for j in jobs}
            cum = {j: _tri_sum(fail[j], upto) for j in jobs}
            miss = {j: jnp.exp2(-fail[j]) for j in jobs}
            beta = {j: zs[j][1] * miss[j] for j in jobs}
            fail_run, fail_before = list(carry[0::2]), {}
            for n, h in jobs:
                fail_before[n, h] = fail_run[h]
                fail_run[h] = fail_run[h] + cum[n, h][:, t_last]
            w = {(n, h): beta[n, h] * jnp.exp2(fail_before[n, h] + cum[n, h] - ctot2[h]) for n, h in jobs}
            w = {j: jnp.where(valid, w[j], 0.0) if masked(j[0]) else w[j] for j in jobs}
            g = {j: w[j] * dw[j] for j in jobs}
            g_local = {j: _bdot(g[j], before, NN) for j in jobs}
            for n, h in jobs:
                dv_acc[rows[n], lanes(h)] += _bdot(w[n, h], do2[h], TN)
            g_run, dz = list(carry[1::2]), {}
            for n, h in jobs:
                g_sum = g_run[h] + g_local[n, h]
                dz[n, h] = g[n, h] * miss[n, h] - beta[n, h] * g_sum
                g_run[h] = g_sum[:, t_last] + g[n, h][:, t_last]
            dz = {j: jnp.where(valid, dz[j], 0.0) if masked(j[0]) else dz[j] for j in jobs}
            for n, h in jobs:
                dq_acc[h] += _bdot(dz[n, h], kt[n, h], NN)
                dk_acc[rows[n], lanes(h)] += _bdot(dz[n, h], q2[h], TN)
            return tuple(v for pair in zip(fail_run, g_run) for v in pair)

        zero = jnp.zeros((t, 1), F32)
        carry = lax.fori_loop(kb0, i - 1, lambda n, c: tiles([n], False, c), (zero,) * (2 * nh))

        @pl.when(i == 0)
        def _():
            tiles([i], True, carry)

        @pl.when(i > 0)
        def _():
            tiles([i - 1, i], True, carry)
        for b in range(nh // 2):
            dq_ref[:, b * LANES:(b + 1) * LANES] = (jnp.where(head0, dq_acc[2 * b], dq_acc[2 * b + 1])
                                                    * scale).astype(BF16)

        @pl.when(i == n_q - 1)
        def _():
            dk_ref[...] = dk_acc[...].astype(BF16)
            dv_ref[...] = dv_acc[...].astype(BF16)

    n_p, width = SB_HEADS // nh, nh * SB_HEAD_DIM
    k_blk, v_blk = SB_WIDTH // width, 2 * SB_WIDTH // width
    outs = _call(
        body, name=name, grid=(n_p, n_q),
        in_specs=[pl.BlockSpec((t, width), lambda p, i: (i, p)),
                  pl.BlockSpec((s, width), lambda p, i: (0, k_blk + p)),
                  pl.BlockSpec((s, width), lambda p, i: (0, v_blk + p)),
                  pl.BlockSpec((t, width), lambda p, i: (i, p)),
                  pl.BlockSpec((None, t, nh), lambda p, i: (p, i, 0)),
                  pl.BlockSpec(memory_space=pltpu.SMEM)],
        out_specs=[pl.BlockSpec((t, width), lambda p, i: (i, p)),
                   pl.BlockSpec((s, width), lambda p, i: (0, p)),
                   pl.BlockSpec((s, width), lambda p, i: (0, p))],
        out_shape=[jax.ShapeDtypeStruct((s, SB_WIDTH), BF16)] * 3,
        scratch=[pltpu.VMEM((nh, t, LANES), F32), pltpu.VMEM((s, width), F32), pltpu.VMEM((s, width), F32)],
        args=[proj, proj, proj, do_a, ctot, first], plan=plan)
    return jnp.concatenate(outs, axis=1)


def _mm_gathered(a, key, plan, *, name, out3=False, w_t=False):
    src = plan.gathering(key)
    if src is None:
        return _mm_nn(a, plan.weight(key), name=name, out3=out3, w_t=w_t, plan=plan)
    out, w_all = _mm_gathering(a, src, name=name, out3=out3, w_t=w_t)
    plan.set_weight(key, w_all)
    return out


def _local_step(x, mem, target, gains, plan):
    g_mix, g_memq, g_memkv, g_ffn, g_fin = gains
    d = x.shape[1]

    h0 = _rms_fwd(x, g_mix, name="rms_mix")
    proj = _mm_gathered(h0, "in", plan, name="mm_in")
    w_in = plan.weight("in")
    o_a, ctot, first = _sb_fwd(proj, name="sb_fwd", plan=plan)
    conv_w = plan.weight("conv")
    y_b = _conv_fwd(proj, conv_w, name="conv_fwd")
    w_a, w_b, w_mix = plan.weight("a"), plan.weight("b"), plan.weight("mix")
    x1, hq, merged, br_a, br_b = _mix_out(o_a, y_b, proj, w_a[0], w_b[0], w_mix[0], x, g_memq, name="mm_mix", plan=plan)
    w_mq, w_kv, w_mo = plan.weight("mq")[0], plan.weight("kv"), plan.weight("mo")[0]
    mn = _rms_fwd(mem, g_memkv, name="rms_memkv")
    kv = _mm_nn(mn, w_kv, name="mm_memkv")
    qm, om, x2, hf = _mem_sublayer(hq, w_mq, kv, w_mo, x1, g_ffn, name="mem_sublayer", plan=plan)
    gu = _mm_gathered(hf, "fi", plan, name="mm_ffn_in", out3=True, w_t=True)
    w_fi, w_fo = plan.weight("fi"), plan.weight("fo")
    dx3, dx3b, dg_fin, loss, act = _ffn_out_loss(gu, w_fo, x2, g_fin, target, name="mm_ffn_out")

    plan.grad("fo", _mm_tn_a3(act, dx3b, name="mm_d_w_ffn_out"))
    dgu = _ffn_out_bwd(dx3b, w_fo, gu, name="mm_d_act")
    plan.grad("fi", _mm_tn_a3(dgu, hf, name="mm_d_w_ffn_in"))
    dx2, dx2b, dg_ffn = _mm_nt_rms(dgu, w_fi, x2, g_ffn, dx3, name="mm_d_hf", dy3=True, w_nn=True, plan=plan)

    plan.grad("mo", _mm_tn(om, dx2b, d, name="mm_d_w_memo"))
    dx1, dx1b, dg_memq, dqm, dkv = _mem_sublayer_bwd(dx2b, dx2, x1, g_memq, qm, kv, w_mq, w_mo, name="mem_sublayer_bwd",
                                                    plan=plan)
    plan.grad("mq", _mm_tn(hq, dqm, d, name="mm_d_w_memq"))
    plan.grad("kv", _mm_tn(mn, dkv, w_kv.shape[2], name="mm_d_w_memkv"))
    _, _, dg_memkv = _mm_nt_rms(dkv, w_kv, mem, g_memkv, None, name="mm_d_mn")

    plan.grad("mix", _mm_tn(merged, dx1b, d, name="mm_d_w_mix"))
    dbr_a, dbr_b, dgab, do_a, dy_b = _mix_out_bwd(dx1b, w_mix[0], br_a, br_b, proj, w_a[0], w_b[0], name="mm_d_merged",
                                                 plan=plan)
    plan.grad("a", _mm_tn(o_a, dbr_a, d, name="mm_d_w_branch_a"))
    plan.grad("b", _mm_tn(y_b, dbr_b, d, name="mm_d_w_branch_b"))
    dconv, dconv_w = _conv_bwd(dy_b, proj, conv_w, name="conv_bwd", plan=plan)
    dqkv = _sb_bwd(proj, do_a, ctot, first, name="sb_bwd", plan=plan)
    dproj = jnp.concatenate([dqkv, dconv, dgab], axis=1)
    rows_in1 = d // IN_SPLIT[1] * (IN_SPLIT[1] - IN_SPLIT[0])
    plan.grad("in0", _mm_tn(h0, dproj, w_in.shape[2], name="mm_d_w_in0", tm=d - rows_in1, k_tiles=(0, 1)))
    plan.grad("in1", _mm_tn(h0, dproj, w_in.shape[2], name="mm_d_w_in1", tm=rows_in1,
                            k_tiles=(d // rows_in1 - 1, 1), plan=plan))
    dh0 = _mm_nt(dproj, w_in, name="mm_d_h0", out_dtype=F32, plan=plan)
    dx0, _, dg_mix = _rms_bwd(x, g_mix, dh0, dx1, name="rms_mix_bwd", plan=plan)

    return dx0, (dg_mix, dg_memq, dg_memkv, dg_ffn, dg_fin, dconv_w, loss)


def _row_tile(a, target=512):
    tm = min(a, target)
    while a % tm:
        tm -= 8
    return tm


def _sum_with_sibling(parts, recvs, core, *, name):
    n = len(parts)

    def body(core_ref, *refs):
        for p_ref, r_ref, o_ref in zip(refs[:n], refs[n:2 * n], refs[2 * n:]):
            o_ref[...] = (p_ref[...].astype(F32) + r_ref[...].astype(F32)).astype(o_ref.dtype)

    mine = [pl.BlockSpec((None,) + p.shape[1:], lambda q, core_ref: (2 * q + core_ref[0], 0, 0)) for p in parts]
    other = [pl.BlockSpec((None,) + p.shape[1:], lambda q, core_ref: (q, 0, 0)) for p in parts]
    return pl.pallas_call(
        body, name=name,
        grid_spec=pltpu.PrefetchScalarGridSpec(num_scalar_prefetch=1, grid=(N_CHIP,), in_specs=mine + other,
                                               out_specs=other),
        out_shape=[jax.ShapeDtypeStruct((N_CHIP,) + p.shape[1:], p.dtype) for p in parts],
        compiler_params=_params(1))(core, *parts, *recvs)


def _adam_math(wv, g, m, v):
    m = ADAM_B1 * m + (1.0 - ADAM_B1) * g
    v = ADAM_B2 * v + (1.0 - ADAM_B2) * (g * g)
    m_hat = m / (1.0 - ADAM_B1 ** ADAM_STEP)
    v_hat = v / (1.0 - ADAM_B2 ** ADAM_STEP)
    delta = -ADAM_LR * (m_hat / (jnp.sqrt(v_hat) + ADAM_EPS) + ADAM_WD * wv)
    return delta, m, v


def _adam_sharded(wv, m, v, own, recv, chip, *, name):
    a, b = wv.shape
    tm = _row_tile(a)

    def body(chip_ref, w_ref, m_ref, v_ref, own_ref, recv_ref, g_ref, d_ref, nm_ref, nv_ref):
        g = own_ref[...].astype(F32)
        for j in range(3):
            g = g + recv_ref[j].astype(F32)
        delta, nm, nv = _adam_math(w_ref[...], g, m_ref[...], v_ref[...])
        g_ref[...] = g
        d_ref[...] = delta
        nm_ref[...] = nm
        nv_ref[...] = nv

    tile = pl.BlockSpec((tm, b), lambda i, chip_ref: (i, 0))
    return pl.pallas_call(
        body, name=name,
        grid_spec=pltpu.PrefetchScalarGridSpec(
            num_scalar_prefetch=1, grid=(a // tm,),
            in_specs=[tile, tile, tile,
                      pl.BlockSpec((None, tm, b), lambda i, chip_ref: (chip_ref[0], i, 0)),
                      pl.BlockSpec((3, tm, b), lambda i, chip_ref: (0, i, 0))],
            out_specs=[tile] * 4),
        out_shape=[jax.ShapeDtypeStruct((a, b), F32)] * 4, compiler_params=_params(1))(chip, wv, m, v, own, recv)


def _sum_devices(gathered, *, name):
    _, r, c = gathered.shape

    def body(g_ref, o_ref):
        total = g_ref[0]
        for j in range(1, N_DEV):
            total = total + g_ref[j]
        o_ref[...] = total

    return pl.pallas_call(body, name=name, out_shape=jax.ShapeDtypeStruct((r, c), F32))(gathered)


def _adam_small(wv, g, m, v, *, name):
    def body(w_ref, g_ref, m_ref, v_ref, d_ref, nm_ref, nv_ref):
        delta, nm, nv = _adam_math(w_ref[...], g_ref[...], m_ref[...], v_ref[...])
        d_ref[...] = delta
        nm_ref[...] = nm
        nv_ref[...] = nv

    return pl.pallas_call(body, name=name, out_shape=[jax.ShapeDtypeStruct(wv.shape, F32)] * 3)(wv, g, m, v)


BIG = ("in", "a", "b", "mix", "mq", "kv", "mo", "fi", "fo")
ROW_SHARDED = ("mix", "mq", "mo")
UNSHARDED = ("a", "b")
FFN_GROUPS = 4
IN_SPLIT = (3, 4)
SMALL_ROWS = 16


class _Plan:
    FUSED = ("in",)
    GATHER_ON = {"sb_fwd": ("a", "b", "mix", "mq", "mo", "conv", "fi0"), "mm_mix": ("kv",), "mem_sublayer": ("fi1",),
                 "mm_ffn_in": ("fo",)}
    SIBLING_ON = {"mm_d_hf": ("fo", "fi"), "mm_d_merged": ("mo", "mq", "kv"), "conv_bwd": ("mix", "a", "b"),
                  "mm_d_w_in1": ("in0",), "mm_d_h0": ("in1",)}
    CHIPS_ON = {"mem_sublayer_bwd": ("fo",), "sb_bwd": ("fi", "mo", "mq", "kv", "mix", "a", "b"), "mm_d_h0": ("in0",),
                "rms_mix_bwd": ("in1",)}

    def __init__(self, shards, core):
        self.shards, self.core = shards, core
        self.w, self.parts, self.chip_sums, self.from_chips = {}, {}, {}, {}

    def gathering(self, k):
        return self.shards[k] if k in self.FUSED else None

    def comm(self, name):
        comms = []
        if name in self.GATHER_ON:
            comms.append(_gather_comm([self.shards[k] for k in self.GATHER_ON[name]]))
        if name in self.SIBLING_ON:
            comms.append(_sibling_comm([self.parts[k] for k in self.SIBLING_ON[name]]))
        if name in self.CHIPS_ON:
            comms.append(_chips_comm([self.chip_sums[k] for k in self.CHIPS_ON[name]]))
        return _join_comms(comms) if comms else None

    def landed(self, name, outs):
        outs = list(outs)
        for k in self.GATHER_ON.get(name, ()):
            self.set_weight(k, outs.pop(0))
        keys = self.SIBLING_ON.get(name, ())
        if keys:
            sums = _sum_with_sibling([self.parts[k] for k in keys], [outs.pop(0) for _ in keys], self.core,
                                     name="sum_with_sibling_" + "_".join(keys))
            self.chip_sums.update(zip(keys, sums))
        for k in self.CHIPS_ON.get(name, ()):
            self.from_chips[k] = outs.pop(0)

    def set_weight(self, k, gathered):
        _, a, b = gathered.shape
        if k in ROW_SHARDED:
            gathered = gathered.reshape(1, N_DEV * a, b)
        elif k in UNSHARDED:
            gathered = jnp.transpose(gathered, (1, 0, 2)).reshape(1, a, N_DEV * b)
        elif k == "fo":
            gathered = gathered.reshape(FFN_GROUPS, N_DEV * a // FFN_GROUPS, b)
        elif k == "conv":
            n_conv = CONV_WIDTH // N_DEV
            gathered = jnp.transpose(gathered[:, :3, :n_conv], (1, 0, 2)).reshape(3, CONV_WIDTH)
        self.w[k] = gathered
        if k == "fi1":
            self.w["fi"] = jnp.concatenate([self.w["fi0"], gathered], axis=2)

    def weight(self, k):
        return self.w[k]

    def grad(self, k, g):
        _, a, b = g.shape
        if k in ROW_SHARDED:
            g = g.reshape(N_DEV, a // N_DEV, b)
        elif k in UNSHARDED:
            g = jnp.transpose(g.reshape(a, N_DEV, b // N_DEV), (1, 0, 2))
        elif k == "fo":
            g = g.reshape(N_DEV, FFN_GROUPS * a // N_DEV, b)
        self.parts[k] = g


def kernel(x, mem, norm_mix, w_in, conv_w, w_branch_a, w_branch_b, w_mix_out, norm_mem_q, norm_mem_kv, w_mem_q, w_mem_kv, w_mem_o, norm_ffn, w_ffn_in, w_ffn_out, norm_final, loss_target, m_norm_mix, m_w_in, m_conv_w, m_w_branch_a, m_w_branch_b, m_w_mix_out, m_norm_mem_q, m_norm_mem_kv, m_w_mem_q, m_w_mem_kv, m_w_mem_o, m_norm_ffn, m_w_ffn_in, m_w_ffn_out, m_norm_final, v_norm_mix, v_w_in, v_conv_w, v_w_branch_a, v_w_branch_b, v_w_mix_out, v_norm_mem_q, v_norm_mem_kv, v_w_mem_q, v_w_mem_kv, v_w_mem_o, v_norm_ffn, v_w_ffn_in, v_w_ffn_out, v_norm_final):
    d = x.shape[-1]
    xi, yi, ci = lax.axis_index("x"), lax.axis_index("y"), lax.axis_index("c")
    chip = jnp.reshape(2 * xi + yi, (1,)).astype(jnp.int32)
    dev = 4 * xi + 2 * yi + ci

    big_w = dict(zip(BIG, (w_in, w_branch_a, w_branch_b, w_mix_out, w_mem_q, w_mem_kv, w_mem_o, w_ffn_in, w_ffn_out)))
    big_m = dict(zip(BIG, (m_w_in, m_w_branch_a, m_w_branch_b, m_w_mix_out, m_w_mem_q, m_w_mem_kv, m_w_mem_o, m_w_ffn_in, m_w_ffn_out)))
    big_v = dict(zip(BIG, (v_w_in, v_w_branch_a, v_w_branch_b, v_w_mix_out, v_w_mem_q, v_w_mem_kv, v_w_mem_o, v_w_ffn_in, v_w_ffn_out)))

    flip = lambda t, k: jnp.transpose(t) if k == "fi" else t
    shards = {k: flip(big_w[k][0], k).astype(BF16) for k in BIG}
    shards["fi0"], shards["fi1"] = shards["fi"][:, :d // 2], shards["fi"][:, d // 2:]
    n_conv = conv_w.shape[-1]
    shards["conv"] = jnp.zeros((8, LANES), F32).at[:3, :n_conv].set(conv_w[0])
    plan = _Plan(shards, jnp.reshape(ci, (1,)).astype(jnp.int32))

    gains = (norm_mix, norm_mem_q, norm_mem_kv, norm_ffn, norm_final.reshape(1, d))
    dx0, small = _local_step(x[0], mem[0], loss_target[0], gains, plan)

    grads, deltas, new_m, new_v = {}, {}, {}, {}
    for k in BIG:
        lead = big_w[k].shape
        wv, mv, vv = flip(big_w[k][0], k), flip(big_m[k][0], k), flip(big_v[k][0], k)
        if k == "in":
            half = wv.shape[0] * IN_SPLIT[0] // IN_SPLIT[1]
            lo = _adam_sharded(wv[:half], mv[:half], vv[:half], plan.chip_sums["in0"], plan.from_chips["in0"], chip,
                               name="adam_in0")
            hi = _adam_sharded(wv[half:], mv[half:], vv[half:], plan.chip_sums["in1"], plan.from_chips["in1"], chip,
                               name="adam_in1")
            outs = [jnp.concatenate(pair, axis=0) for pair in zip(lo, hi)]
        else:
            outs = _adam_sharded(wv, mv, vv, plan.chip_sums[k], plan.from_chips[k], chip, name="adam_" + k)
        grads[k], deltas[k], new_m[k], new_v[k] = (flip(t, k).reshape(lead) for t in outs)

    dg_mix, dg_memq, dg_memkv, dg_ffn, dg_fin, dconv_w, loss = small
    conv_rows = jnp.zeros((3, d), F32).at[:, :CONV_WIDTH].set(dconv_w[:3])
    block = jnp.concatenate([dg_mix[:1], dg_memq[:1], dg_memkv[:1], dg_ffn[:1], dg_fin[:1], conv_rows,
                             jnp.broadcast_to(loss[:1, :1], (1, d)), jnp.zeros((SMALL_ROWS - 9, d), F32)], axis=0)
    total = _sum_devices(_exchange(_gather_comm([block]), name="gather_small")[0], name="sum_small")
    g_conv = lax.dynamic_slice(total[5:8, :CONV_WIDTH], (0, dev * n_conv), (3, n_conv))
    small_w = [norm_mix, norm_mem_q, norm_mem_kv, norm_ffn, norm_final.reshape(1, d), conv_w[0]]
    small_m = [m_norm_mix, m_norm_mem_q, m_norm_mem_kv, m_norm_ffn, m_norm_final.reshape(1, d), m_conv_w[0]]
    small_v = [v_norm_mix, v_norm_mem_q, v_norm_mem_kv, v_norm_ffn, v_norm_final.reshape(1, d), v_conv_w[0]]
    small_g = [total[0:1], total[1:2], total[2:3], total[3:4], total[4:5], g_conv]
    small_names = ["norm_mix", "norm_mem_q", "norm_mem_kv", "norm_ffn", "norm_final", "conv_w"]
    sg, sd, sm, sv = {}, {}, {}, {}
    for nme, wv, g, m, v in zip(small_names, small_w, small_g, small_m, small_v):
        dl, nm, nv = _adam_small(wv, g, m, v, name="adam_" + nme)
        shape = norm_final.shape if nme == "norm_final" else (conv_w.shape if nme == "conv_w" else wv.shape)
        sg[nme], sd[nme], sm[nme], sv[nme] = (t.reshape(shape) for t in (g, dl, nm, nv))

    def ordered(big, sml):
        return (sml["norm_mix"], big["in"], sml["conv_w"], big["a"], big["b"], big["mix"], sml["norm_mem_q"],
                sml["norm_mem_kv"], big["mq"], big["kv"], big["mo"], sml["norm_ffn"], big["fi"], big["fo"],
                sml["norm_final"])

    loss_out = total[8, 0]
    grad_x = dx0.reshape(x.shape)
    return (loss_out, grad_x, *ordered(grads, sg), *ordered(deltas, sd), *ordered(new_m, sm), *ordered(new_v, sv))
```

```python
import functools
import math

import jax
import jax.numpy as jnp
from jax import lax
from jax.experimental import pallas as pl
from jax.experimental.pallas import tpu as pltpu

F32 = jnp.float32
BF16 = jnp.bfloat16
MESH = pl.DeviceIdType.MESH

N_DEV = 8
N_CHIP = 4
NORM_EPS = 1e-6
SB_HEADS = 8
SB_HEAD_DIM = 64
SB_WIDTH = SB_HEADS * SB_HEAD_DIM
CONV_WIDTH = 512
MEM_HEADS = 4
ADAM_LR = 0.001
ADAM_B1 = 0.9
ADAM_B2 = 0.999
ADAM_EPS = 1e-08
ADAM_WD = 0.01
ADAM_STEP = 10

LANES = 128
VMEM_LIMIT_BYTES = 52 * 1024 * 1024
SB_TILE = 256
SB_STEP_HEADS = 4
SB_DEAD = 159.0
SB_CLAMP = 126.0
LOG2_E = 1.4426950408889634

ANY = pl.BlockSpec(memory_space=pl.ANY)


def _params(n_grid):
    return pltpu.CompilerParams(dimension_semantics=("arbitrary",) * n_grid, vmem_limit_bytes=VMEM_LIMIT_BYTES)


def _bdot(a, b, dims):
    return lax.dot_general(a.astype(BF16), b.astype(BF16), (dims, ((), ())), preferred_element_type=F32)


NN = ((1,), (0,))
NT = ((1,), (1,))
TN = ((0,), (0,))


class _Comm:
    def __init__(self, ins, outs, n_sems, start, finish):
        self.ins, self.outs, self.n_sems, self.start, self.finish = ins, outs, n_sems, start, finish

    def sem_shapes(self):
        return [pltpu.SemaphoreType.DMA((k,)) for k in self.n_sems]


def _place():
    return lax.axis_index("x"), lax.axis_index("y"), lax.axis_index("c")


def _neighbours(x, y, c):
    return [(jnp.bitwise_xor(x, c), jnp.bitwise_xor(y, 1 - c)), (jnp.bitwise_xor(x, 1 - c), jnp.bitwise_xor(y, c)),
            (1 - x, 1 - y)]


def _gather_comm(shards):
    n = len(shards)

    def copies(ins, outs, sems):
        send_sems, recv_sems, _ = sems
        x, y, c = _place()
        chips = [(1 - x, y), (x, 1 - y), (1 - x, 1 - y)]

        def copy(a, k, block, to, from_shard=False):
            dst = outs[a].at[4 * block[0] + 2 * block[1] + block[2]]
            return pltpu.make_async_remote_copy(
                src_ref=ins[a] if from_shard else dst, dst_ref=dst, send_sem=send_sems.at[a * 7 + k],
                recv_sem=recv_sems.at[a * 7 + k], device_id=to, device_id_type=MESH)

        me, sibling = (x, y, c), (x, y, 1 - c)
        own = [[copy(a, 0, me, sibling, True)] + [copy(a, 1 + j, me, (*chip, c), True) for j, chip in enumerate(chips)]
               for a in range(n)]
        landed = [[copy(a, 1 + j, (*chip, c), me) for j, chip in enumerate(chips)] for a in range(n)]
        passed = [[copy(a, 4 + j, (*chip, c), sibling) for j, chip in enumerate(chips)] for a in range(n)]
        from_sibling = [[copy(a, 0, sibling, me)] + [copy(a, 4 + j, (*chip, 1 - c), me) for j, chip in enumerate(chips)]
                        for a in range(n)]
        local = [pltpu.make_async_copy(ins[a], outs[a].at[4 * x + 2 * y + c], sems[2].at[a]) for a in range(n)]
        return own, landed, passed, from_sibling, local

    def start(ins, outs, sems):
        own, _, _, _, local = copies(ins, outs, sems)
        for a in range(n):
            local[a].start()
            for cp in own[a]:
                cp.start()

    def finish(ins, outs, sems):
        own, landed, passed, from_sibling, local = copies(ins, outs, sems)
        for a in range(n):
            for arrived, onward in zip(landed[a], passed[a]):
                arrived.wait_recv()
                onward.start()
        for a in range(n):
            for cp in from_sibling[a]:
                cp.wait_recv()
        for a in range(n):
            for cp in own[a] + passed[a]:
                cp.wait_send()
            local[a].wait()

    outs = [jax.ShapeDtypeStruct((N_DEV,) + s.shape, s.dtype) for s in shards]
    return _Comm(list(shards), outs, (7 * n, 7 * n, n), start, finish)


def _sibling_comm(parts):
    n = len(parts)

    def copies(ins, outs, sems):
        x, y, c = _place()
        return [pltpu.make_async_remote_copy(
            src_ref=ins[a].at[2 * q + 1 - c], dst_ref=outs[a].at[q], send_sem=sems[0].at[a * N_CHIP + q],
            recv_sem=sems[1].at[a * N_CHIP + q], device_id=(x, y, 1 - c), device_id_type=MESH)
            for a in range(n) for q in range(N_CHIP)]

    def start(ins, outs, sems):
        for cp in copies(ins, outs, sems):
            cp.start()

    def finish(ins, outs, sems):
        cps = copies(ins, outs, sems)
        for cp in cps:
            cp.wait_recv()
        for cp in cps:
            cp.wait_send()

    outs = [jax.ShapeDtypeStruct((N_CHIP,) + p.shape[1:], p.dtype) for p in parts]
    return _Comm(list(parts), outs, (N_CHIP * n, N_CHIP * n), start, finish)


def _chips_comm(parts):
    n = len(parts)

    def copies(ins, outs, sems):
        x, y, c = _place()
        chips = [(1 - x, y), (x, 1 - y), (1 - x, 1 - y)]
        return [pltpu.make_async_remote_copy(
            src_ref=ins[a].at[2 * px + py], dst_ref=outs[a].at[j], send_sem=sems[0].at[a * 3 + j],
            recv_sem=sems[1].at[a * 3 + j], device_id=(px, py, c), device_id_type=MESH)
            for a in range(n) for j, (px, py) in enumerate(chips)]

    def start(ins, outs, sems):
        for cp in copies(ins, outs, sems):
            cp.start()

    def finish(ins, outs, sems):
        cps = copies(ins, outs, sems)
        for cp in cps:
            cp.wait_recv()
        for cp in cps:
            cp.wait_send()

    outs = [jax.ShapeDtypeStruct((3,) + p.shape[1:], p.dtype) for p in parts]
    return _Comm(list(parts), outs, (3 * n, 3 * n), start, finish)


def _join_comms(comms):
    if len(comms) == 1:
        return comms[0]

    def split(refs, counts):
        out, at = [], 0
        for n in counts:
            out.append(refs[at:at + n])
            at += n
        return out

    def each(method):
        def run(ins, outs, sems):
            parts = zip(comms, split(ins, [len(c.ins) for c in comms]), split(outs, [len(c.outs) for c in comms]),
                        split(sems, [len(c.n_sems) for c in comms]))
            for c, c_ins, c_outs, c_sems in parts:
                getattr(c, method)(c_ins, c_outs, c_sems)
        return run

    return _Comm([a for c in comms for a in c.ins], [o for c in comms for o in c.outs],
                 tuple(k for c in comms for k in c.n_sems), each("start"), each("finish"))


def _exchange(comm, *, name):
    n_ci, n_co = len(comm.ins), len(comm.outs)

    def kern(*refs):
        c_ins, c_outs, sems = refs[:n_ci], refs[n_ci:n_ci + n_co], refs[n_ci + n_co:]
        comm.start(c_ins, c_outs, sems)
        comm.finish(c_ins, c_outs, sems)

    return pl.pallas_call(kern, name=name, in_specs=[ANY] * n_ci, out_specs=[ANY] * n_co, out_shape=comm.outs,
                          scratch_shapes=comm.sem_shapes())(*comm.ins)


def _call(body, *, name, grid, in_specs, out_specs, out_shape, scratch, args, plan=None):
    comm = plan.comm(name) if plan is not None else None
    if comm is None:
        return list(pl.pallas_call(functools.partial(body), name=name, grid=grid, in_specs=in_specs,
                                   out_specs=out_specs, out_shape=out_shape, scratch_shapes=scratch,
                                   compiler_params=_params(len(grid)))(*args))
    n_in, n_out, n_scr, n_ci, n_co = len(in_specs), len(out_specs), len(scratch), len(comm.ins), len(comm.outs)

    def kern(*refs):
        ins, c_ins, refs = refs[:n_in], refs[n_in:n_in + n_ci], refs[n_in + n_ci:]
        outs, c_outs, refs = refs[:n_out], refs[n_out:n_out + n_co], refs[n_out + n_co:]
        scr, sems = refs[:n_scr], refs[n_scr:]
        ids = [pl.program_id(ax) for ax in range(len(grid))]
        first = functools.reduce(jnp.logical_and, [i == 0 for i in ids])
        last = functools.reduce(jnp.logical_and, [i == g - 1 for i, g in zip(ids, grid)])

        @pl.when(first)
        def _():
            comm.start(c_ins, c_outs, sems)
        body(*ins, *outs, *scr)

        @pl.when(last)
        def _():
            comm.finish(c_ins, c_outs, sems)

    res = pl.pallas_call(kern, name=name, grid=grid, in_specs=list(in_specs) + [ANY] * n_ci,
                         out_specs=list(out_specs) + [ANY] * n_co, out_shape=list(out_shape) + comm.outs,
                         scratch_shapes=list(scratch) + comm.sem_shapes(),
                         compiler_params=_params(len(grid)))(*args, *comm.ins)
    plan.landed(name, list(res[n_out:]))
    return list(res[:n_out])


def _mm_body(dims, has_add, *refs):
    if has_add:
        a_ref, b_ref, add_ref, o_ref = refs
        total = _bdot(a_ref[...], b_ref[...], dims) + add_ref[...]
    else:
        a_ref, b_ref, o_ref = refs
        total = _bdot(a_ref[...], b_ref[...], dims)
    o_ref[...] = total.astype(o_ref.dtype)


def _mm_nt_body(j, n, dy_ref, w_ref, o_ref):
    total = _bdot(dy_ref[:, 0:n], w_ref[0], NT)
    for jj in range(1, j):
        total = total + _bdot(dy_ref[:, jj * n:(jj + 1) * n], w_ref[jj], NT)
    o_ref[...] = total.astype(o_ref.dtype)


def _mm_nn(a, w3, *, name, out_dtype=BF16, add=None, tm=1024, tn=None, out3=False, w_t=False, plan=None):
    m, kk = a.shape
    j, n = w3.shape[0], w3.shape[1 if w_t else 2]
    tm, tn = min(tm, m), n if tn is None else tn
    n_t = n // tn
    in_specs = [pl.BlockSpec((tm, kk), lambda i, jj: (i, 0)),
                pl.BlockSpec((None, tn, kk), lambda i, jj: (jj // n_t, jj % n_t, 0)) if w_t else
                pl.BlockSpec((None, kk, tn), lambda i, jj: (jj // n_t, 0, jj % n_t))]
    args = [a, w3]
    if add is not None:
        in_specs.append(pl.BlockSpec((tm, tn), lambda i, jj: (i, jj)))
        args.append(add)
    if out3:
        out_spec = pl.BlockSpec((None, tm, tn), lambda i, jj: (jj // n_t, i, jj % n_t))
        out_shape = jax.ShapeDtypeStruct((j, m, n), out_dtype)
    else:
        out_spec = pl.BlockSpec((tm, tn), lambda i, jj: (i, jj))
        out_shape = jax.ShapeDtypeStruct((m, j * n), out_dtype)
    return _call(
        functools.partial(_mm_body, NT if w_t else NN, add is not None), name=name, grid=(m // tm, j * n_t),
        in_specs=in_specs, out_specs=[out_spec], out_shape=[out_shape], scratch=[], args=args, plan=plan)[0]


def _mm_gathering(a, shard, *, name, out3=False, w_t=False, tm=1024):
    m, kk = a.shape
    n = shard.shape[0 if w_t else 1]
    tm = min(tm, m)
    n_i = m // tm
    fetch_at = min(1, n_i - 1)

    def body(a_ref, shard_ref, o_ref, w_all, w_vmem, send_sems, recv_sems, copy_sems):
        jj, i = pl.program_id(0), pl.program_id(1)
        x, y, c = _place()
        me, sibling = (x, y, c), (x, y, 1 - c)
        chips = _neighbours(x, y, c)
        sibling_chips = [chips[1], chips[0], chips[2]]

        def rows(block):
            return w_all.at[4 * block[0] + 2 * block[1] + block[2]]

        def remote(k, block, to, from_shard=False):
            return pltpu.make_async_remote_copy(
                src_ref=shard_ref if from_shard else rows(block), dst_ref=rows(block), send_sem=send_sems.at[k],
                recv_sem=recv_sems.at[k], device_id=to, device_id_type=MESH)

        def load(step, src):
            return pltpu.make_async_copy(src, w_vmem.at[step % 2], copy_sems.at[1 + step % 2])

        own = [remote(0, me, sibling, True), remote(1, me, (*chips[0], c), True), remote(2, me, (*chips[1], c), True),
               remote(3, (*chips[0], c), (*chips[1], c))]
        passed = [remote(4 + j, (*chip, c), sibling) for j, chip in enumerate(chips)]
        local = pltpu.make_async_copy(shard_ref, rows(me), copy_sems.at[0])

        @pl.when(jnp.logical_and(i == 0, jj == 0))
        def _():
            local.start()
            own[0].start()
            own[1].start()
            load(0, shard_ref).start()

        def arrivals():
            yield 1, (lambda: remote(0, sibling, me).wait_recv()), sibling
            for j, chip in enumerate(chips):
                def landed(j=j, chip=chip):
                    if j < 2:
                        own[1 + j].wait_send()
                        own[2 + j].start()
                    remote(1 + j, (*chip, c), me).wait_recv()
                    passed[j].start()
                yield 2 + 2 * j, landed, (*chip, c)
                block = (*sibling_chips[j], 1 - c)
                yield 3 + 2 * j, (lambda j=j, block=block: remote(4 + j, block, me).wait_recv()), block

        for step, wait_for_it, block in arrivals():
            @pl.when(jnp.logical_and(i == fetch_at, jj == step - 1))
            def _():
                wait_for_it()
                load(step, rows(block)).start()

        for step in range(N_DEV):
            @pl.when(jnp.logical_and(i == 0, jj == step))
            def _():
                load(step, rows(me)).wait()

        o_ref[...] = _bdot(a_ref[...], w_vmem[lax.rem(jj, 2)], NT if w_t else NN).astype(o_ref.dtype)

        @pl.when(jnp.logical_and(i == n_i - 1, jj == N_DEV - 1))
        def _():
            for cp in [own[0], own[3]] + passed:
                cp.wait_send()
            local.wait()

    def swept(jj):
        x, y, c = _place()
        first, second = 2 + 2 * c, 4 - 2 * c
        flips = (0b000, 0b001, first, second + 1, second, first + 1, 0b110, 0b111)
        return jnp.bitwise_xor(4 * x + 2 * y + c, sum(jnp.where(jj == k, f, 0) for k, f in enumerate(flips)))

    if out3:
        out_spec = pl.BlockSpec((None, tm, n), lambda jj, i: (swept(jj), i, 0))
        out_shape = jax.ShapeDtypeStruct((N_DEV, m, n), BF16)
    else:
        out_spec = pl.BlockSpec((tm, n), lambda jj, i: (i, swept(jj)))
        out_shape = jax.ShapeDtypeStruct((m, N_DEV * n), BF16)
    return pl.pallas_call(
        body, name=name, grid=(N_DEV, n_i),
        in_specs=[pl.BlockSpec((tm, kk), lambda jj, i: (i, 0)), ANY], out_specs=[out_spec, ANY],
        scratch_shapes=[pltpu.VMEM((2,) + shard.shape, shard.dtype), pltpu.SemaphoreType.DMA((7,)),
                        pltpu.SemaphoreType.DMA((7,)), pltpu.SemaphoreType.DMA((3,))],
        out_shape=[out_shape, jax.ShapeDtypeStruct((N_DEV,) + shard.shape, shard.dtype)],
        compiler_params=_params(2))(a, shard)


def _sigmoid(v):
    return 0.5 * jnp.tanh(0.5 * v) + 0.5


def _resident(w):
    return pl.BlockSpec(w.shape, lambda i: (0,) * w.ndim, pipeline_mode=pl.Buffered(1))


def _ffn_out_loss(gu3, w3, add, g, target, *, name, tm=512):
    j2, m, n = gu3.shape
    j = j2 // 2
    nn = w3.shape[2]
    tm = min(tm, m)

    def body(gu_ref, w_ref, add_ref, g_ref, t_ref, dx_ref, dxb_ref, dg_ref, loss_ref, act_ref):
        i = pl.program_id(0)
        xv = add_ref[...]
        for jj in range(j):
            gate = gu_ref[0, jj].astype(F32)
            act = (gate * _sigmoid(gate) * gu_ref[1, jj].astype(F32)).astype(BF16)
            act_ref[jj] = act
            xv = xv + _bdot(act, w_ref[jj], NN)
        gv = g_ref[...]
        r = lax.rsqrt(jnp.mean(xv * xv, axis=-1, keepdims=True) + NORM_EPS)
        xhat = xv * r
        err = xhat * gv - t_ref[...]
        _acc_rows(i, loss_ref, 0.5 * jnp.sum(jnp.mean(err * err, axis=-1, keepdims=True), axis=0, keepdims=True))
        dy = err * (1.0 / nn)
        dxhat = dy * gv
        dx = r * (dxhat - xhat * jnp.mean(dxhat * xhat, axis=-1, keepdims=True))
        dx_ref[...] = dx
        dxb_ref[...] = dx.astype(BF16)
        _acc_rows(i, dg_ref, jnp.sum(dy * xhat, axis=0, keepdims=True))

    row = pl.BlockSpec((tm, nn), lambda i: (i, 0))
    return _call(body, name=name, grid=(m // tm,),
                 in_specs=[pl.BlockSpec((2, j, tm, n), lambda i: (0, 0, i, 0)), _resident(w3),
                           row, pl.BlockSpec(g.shape, lambda i: (0, 0)), row],
                 out_specs=[row, row, pl.BlockSpec((8, nn), lambda i: (0, 0)), pl.BlockSpec((8, LANES), lambda i: (0, 0)),
                            pl.BlockSpec((j, tm, n), lambda i: (0, i, 0))],
                 out_shape=[jax.ShapeDtypeStruct((m, nn), F32), jax.ShapeDtypeStruct((m, nn), BF16),
                            jax.ShapeDtypeStruct((8, nn), F32), jax.ShapeDtypeStruct((8, LANES), F32),
                            jax.ShapeDtypeStruct((j, m, n), BF16)],
                 scratch=[], args=[gu3.reshape(2, j, m, n), w3, add, g, target])


def _ffn_out_bwd(dy, w3, gu3, *, name, tm=1024):
    m, nn = dy.shape
    j, n, _ = w3.shape
    tm = min(tm, m)

    def body(dy_ref, w_ref, gu_ref, dgu_ref):
        da = _bdot(dy_ref[...], w_ref[...], NT)
        gate = gu_ref[0].astype(F32)
        up = gu_ref[1].astype(F32)
        sg = _sigmoid(gate)
        silu = gate * sg
        dgu_ref[0] = (da * up * (sg + silu * (1.0 - sg))).astype(BF16)
        dgu_ref[1] = (da * silu).astype(BF16)

    out = _call(body, name=name, grid=(m // tm, j),
                in_specs=[pl.BlockSpec((tm, nn), lambda i, jj: (i, 0)),
                          pl.BlockSpec((None, n, nn), lambda i, jj: (jj, 0, 0)),
                          pl.BlockSpec((2, None, tm, n), lambda i, jj: (0, jj, i, 0))],
                out_specs=[pl.BlockSpec((2, None, tm, n), lambda i, jj: (0, jj, i, 0))],
                out_shape=[jax.ShapeDtypeStruct((2, j, m, n), BF16)], scratch=[],
                args=[dy, w3, gu3.reshape(2, j, m, n)])[0]
    return out.reshape(2 * j, m, n)


def _rms_fwd_tail(xv, g_ref, h_ref):
    r = lax.rsqrt(jnp.mean(xv * xv, axis=-1, keepdims=True) + NORM_EPS)
    h_ref[...] = (xv * r * g_ref[...]).astype(BF16)


def _rms_bwd_tail(i, dh, x_ref, g_ref, dres_ref, dx_ref, dxb_ref, dg_ref):
    xv = x_ref[...]
    r = lax.rsqrt(jnp.mean(xv * xv, axis=-1, keepdims=True) + NORM_EPS)
    xhat = xv * r
    dxhat = dh * g_ref[...]
    dx = r * (dxhat - xhat * jnp.mean(dxhat * xhat, axis=-1, keepdims=True))
    if dres_ref is not None:
        dx = dx + dres_ref[...]
    dx_ref[...] = dx
    dxb_ref[...] = dx.astype(BF16)
    _acc_rows(i, dg_ref, jnp.sum(dh * xhat, axis=0, keepdims=True))


def _mm_nt_rms(dy, w3, x, g, dres, *, name, dy3=False, w_nn=False, tm=512, plan=None):
    j = w3.shape[0]
    m, kk = x.shape
    n = dy.shape[2] if dy3 else dy.shape[1] // j
    tm = min(tm, m)

    def body(dy_ref, w_ref, x_ref, g_ref, *rest):
        dres_ref = rest[0] if dres is not None else None
        dx_ref, dxb_ref, dg_ref = rest[-3:]
        dh = None
        for jj in range(j):
            piece = dy_ref[jj] if dy3 else dy_ref[:, jj * n:(jj + 1) * n]
            part = _bdot(piece, w_ref[jj], NN if w_nn else NT)
            dh = part if dh is None else dh + part
        _rms_bwd_tail(pl.program_id(0), dh, x_ref, g_ref, dres_ref, dx_ref, dxb_ref, dg_ref)

    row = pl.BlockSpec((tm, kk), lambda i: (i, 0))
    in_specs = [pl.BlockSpec((j, tm, n), lambda i: (0, i, 0)) if dy3 else pl.BlockSpec((tm, j * n), lambda i: (i, 0)),
                _resident(w3), row, pl.BlockSpec(g.shape, lambda i: (0, 0))]
    args = [dy, w3, x, g]
    if dres is not None:
        in_specs.append(row)
        args.append(dres)
    return _call(body, name=name, grid=(m // tm,), in_specs=in_specs,
                 out_specs=[row, row, pl.BlockSpec((8, kk), lambda i: (0, 0))],
                 out_shape=[jax.ShapeDtypeStruct((m, kk), F32), jax.ShapeDtypeStruct((m, kk), BF16),
                            jax.ShapeDtypeStruct((8, kk), F32)], scratch=[], args=args, plan=plan)


def _mix_out(o_a, y_b, proj, w_a, w_b, w, x, g, *, name, tm=512, plan=None):
    s, c = o_a.shape
    d = w.shape[1]
    tm = min(tm, s)

    def body(oa_ref, yb_ref, ga_ref, gb_ref, wa_ref, wb_ref, w_ref, x_ref, g_ref, x1_ref, h_ref, merged_ref, a_ref, b_ref):
        a_ref[...] = _bdot(oa_ref[...], wa_ref[...], NN).astype(BF16)
        b_ref[...] = _bdot(yb_ref[...], wb_ref[...], NN).astype(BF16)
        merged = (_sigmoid(ga_ref[...].astype(F32)) * a_ref[...].astype(F32)
                  + _sigmoid(gb_ref[...].astype(F32)) * b_ref[...].astype(F32)).astype(BF16)
        merged_ref[...] = merged
        xv = _bdot(merged, w_ref[...], NN) + x_ref[...]
        x1_ref[...] = xv
        _rms_fwd_tail(xv, g_ref, h_ref)

    row = pl.BlockSpec((tm, d), lambda i: (i, 0))
    narrow = pl.BlockSpec((tm, c), lambda i: (i, 0))
    whole = lambda arr: pl.BlockSpec(arr.shape, lambda i: (0,) * arr.ndim)
    return _call(body, name=name, grid=(s // tm,),
                 in_specs=[narrow, narrow, pl.BlockSpec((tm, d), lambda i: (i, 3)), pl.BlockSpec((tm, d), lambda i: (i, 4)),
                           whole(w_a), whole(w_b), whole(w), row, whole(g)],
                 out_specs=[row] * 5,
                 out_shape=[jax.ShapeDtypeStruct((s, d), F32)] + [jax.ShapeDtypeStruct((s, d), BF16)] * 4,
                 scratch=[], args=[o_a, y_b, proj, proj, w_a, w_b, w, x, g], plan=plan)


def _mm_tn_a3(a3, dy, *, name):
    j, t, n = a3.shape
    nn = dy.shape[1]
    return _call(functools.partial(_mm_body, TN, False), name=name, grid=(j,),
                 in_specs=[pl.BlockSpec((None, t, n), lambda jj: (jj, 0, 0)), pl.BlockSpec((t, nn), lambda jj: (0, 0))],
                 out_specs=[pl.BlockSpec((None, n, nn), lambda jj: (jj, 0, 0))],
                 out_shape=[jax.ShapeDtypeStruct((j, n, nn), BF16)], scratch=[], args=[a3, dy])[0]


def _mm_nt(dy, w3, *, name, out_dtype=BF16, tm=512, tn=1024, plan=None):
    m = dy.shape[0]
    j, kk, n = w3.shape
    tm, tn = min(tm, m), min(tn, kk)
    return _call(
        functools.partial(_mm_nt_body, j, n), name=name,
        grid=(m // tm, kk // tn),
        in_specs=[pl.BlockSpec((tm, j * n), lambda i, q: (i, 0)),
                  pl.BlockSpec((j, tn, n), lambda i, q: (0, q, 0))],
        out_specs=[pl.BlockSpec((tm, tn), lambda i, q: (i, q))],
        out_shape=[jax.ShapeDtypeStruct((m, kk), out_dtype)], scratch=[], args=[dy, w3], plan=plan)[0]


def _mm_tn(a, dy, n, *, name, out_dtype=BF16, tm=512, tn=None, k_tiles=None, plan=None):
    t, kk = a.shape
    j = dy.shape[1] // n
    tm, tn = min(tm, kk), n if tn is None else tn
    n_t = n // tn
    first, count = (0, kk // tm) if k_tiles is None else k_tiles
    return _call(
        functools.partial(_mm_body, TN, False), name=name,
        grid=(count, j * n_t),
        in_specs=[pl.BlockSpec((t, tm), lambda i, jj: (0, first + i)),
                  pl.BlockSpec((t, tn), lambda i, jj: (0, jj))],
        out_specs=[pl.BlockSpec((None, tm, tn), lambda i, jj: (jj // n_t, i, jj % n_t))],
        out_shape=[jax.ShapeDtypeStruct((j, count * tm, n), out_dtype)], scratch=[], args=[a, dy], plan=plan)[0]


def _rows(body, ins, outs, *, n_rows, tm, name, plan=None):
    tm = min(tm, n_rows)
    n_steps = n_rows // tm
    in_specs, args = [], []
    for arr, kind, width, block in ins:
        if kind == "row":
            in_specs.append(pl.BlockSpec((tm, width), functools.partial(lambda i, b: (i, b), b=block)))
        elif kind == "prev":
            in_specs.append(pl.BlockSpec((tm, width), functools.partial(lambda i, b: (jnp.maximum(i - 1, 0), b), b=block)))
        elif kind == "next":
            in_specs.append(pl.BlockSpec((tm, width), functools.partial(lambda i, b: (jnp.minimum(i + 1, n_steps - 1), b), b=block)))
        else:
            in_specs.append(pl.BlockSpec(arr.shape, functools.partial(lambda i, nd: (0,) * nd, nd=arr.ndim)))
        args.append(arr)
    out_specs, out_shape = [], []
    for shape, dtype, kind in outs:
        if kind == "row":
            out_specs.append(pl.BlockSpec((tm, shape[1]), lambda i: (i, 0)))
        else:
            out_specs.append(pl.BlockSpec(shape, functools.partial(lambda i, nd: (0,) * nd, nd=len(shape))))
        out_shape.append(jax.ShapeDtypeStruct(shape, dtype))

    def kern(*refs):
        body(pl.program_id(0), n_steps, *refs)

    return _call(kern, name=name, grid=(n_steps,), in_specs=in_specs, out_specs=out_specs, out_shape=out_shape,
                 scratch=[], args=args, plan=plan)


def _acc_rows(i, ref, value):
    @pl.when(i == 0)
    def _():
        ref[...] = jnp.zeros_like(ref)
    ref[...] += jnp.broadcast_to(value, ref.shape)


def _rms_fwd(x, g, *, name, tm=512):
    s, d = x.shape

    def body(i, n, x_ref, g_ref, h_ref):
        _rms_fwd_tail(x_ref[...], g_ref, h_ref)

    return _rows(body, [(x, "row", d, 0), (g, "full", 0, 0)], [((s, d), BF16, "row")], n_rows=s, tm=tm, name=name)[0]


def _rms_bwd(x, g, dh, dres, *, name, tm=512, plan=None):
    s, d = x.shape

    def body(i, n, x_ref, g_ref, dh_ref, dres_ref, dx_ref, dxb_ref, dg_ref):
        _rms_bwd_tail(i, dh_ref[...].astype(F32), x_ref, g_ref, dres_ref, dx_ref, dxb_ref, dg_ref)

    return _rows(body, [(x, "row", d, 0), (g, "full", 0, 0), (dh, "row", d, 0), (dres, "row", d, 0)],
                 [((s, d), F32, "row"), ((s, d), BF16, "row"), ((8, d), F32, "acc")],
                 n_rows=s, tm=tm, name=name, plan=plan)


def _mix_out_bwd(dx1b, w, br_a, br_b, proj, w_a, w_b, *, name, tm=512, plan=None):
    s, d = br_a.shape
    c = w_a.shape[0]
    tm = min(tm, s)

    def body(dy_ref, w_ref, a_ref, b_ref, ga_ref, gb_ref, wa_ref, wb_ref, da_ref, db_ref, dg_ref, doa_ref, dyb_ref):
        dm = _bdot(dy_ref[...], w_ref[...], NT)
        sa = _sigmoid(ga_ref[...].astype(F32))
        sb = _sigmoid(gb_ref[...].astype(F32))
        da_ref[...] = (dm * sa).astype(BF16)
        db_ref[...] = (dm * sb).astype(BF16)
        dg_ref[:, :d] = (dm * a_ref[...].astype(F32) * sa * (1.0 - sa)).astype(BF16)
        dg_ref[:, d:] = (dm * b_ref[...].astype(F32) * sb * (1.0 - sb)).astype(BF16)
        doa_ref[...] = _bdot(da_ref[...], wa_ref[...], NT).astype(BF16)
        dyb_ref[...] = _bdot(db_ref[...], wb_ref[...], NT).astype(BF16)

    row = pl.BlockSpec((tm, d), lambda i: (i, 0))
    narrow = pl.BlockSpec((tm, c), lambda i: (i, 0))
    whole = lambda arr: pl.BlockSpec(arr.shape, lambda i: (0,) * arr.ndim)
    return _call(body, name=name, grid=(s // tm,),
                 in_specs=[row, whole(w), row, row, pl.BlockSpec((tm, d), lambda i: (i, 3)),
                           pl.BlockSpec((tm, d), lambda i: (i, 4)), whole(w_a), whole(w_b)],
                 out_specs=[row, row, pl.BlockSpec((tm, 2 * d), lambda i: (i, 0)), narrow, narrow],
                 out_shape=[jax.ShapeDtypeStruct((s, d), BF16), jax.ShapeDtypeStruct((s, d), BF16),
                            jax.ShapeDtypeStruct((s, 2 * d), BF16), jax.ShapeDtypeStruct((s, c), BF16),
                            jax.ShapeDtypeStruct((s, c), BF16)],
                 scratch=[], args=[dx1b, w, br_a, br_b, proj, proj, w_a, w_b], plan=plan)


def _shift_down(cur, prev, k, first):
    row = lax.broadcasted_iota(jnp.int32, cur.shape, 0)
    out = jnp.where(row >= k, pltpu.roll(cur, k, 0), pltpu.roll(prev, k, 0))
    return jnp.where(jnp.logical_and(first, row < k), 0.0, out)


def _shift_up(cur, nxt, k, last):
    tm = cur.shape[0]
    row = lax.broadcasted_iota(jnp.int32, cur.shape, 0)
    out = jnp.where(row < tm - k, pltpu.roll(cur, tm - k, 0), pltpu.roll(nxt, tm - k, 0))
    return jnp.where(jnp.logical_and(last, row >= tm - k), 0.0, out)


def _conv_fwd(proj, conv_w, *, name, tm=512):
    s = proj.shape[0]
    c = CONV_WIDTH

    def body(i, n, u_ref, gb_ref, gc_ref, up_ref, gcp_ref, w_ref, y_ref):
        cu = gc_ref[...].astype(F32) * u_ref[...].astype(F32)
        cup = gcp_ref[...].astype(F32) * up_ref[...].astype(F32)
        first = i == 0
        y = (w_ref[0:1, :] * _shift_down(cu, cup, 2, first) + w_ref[1:2, :] * _shift_down(cu, cup, 1, first)
             + w_ref[2:3, :] * cu)
        y_ref[...] = (gb_ref[...].astype(F32) * y).astype(BF16)

    return _rows(body, [(proj, "row", c, 3), (proj, "row", c, 4), (proj, "row", c, 5),
                        (proj, "prev", c, 3), (proj, "prev", c, 5), (conv_w, "full", 0, 0)],
                 [((s, c), BF16, "row")], n_rows=s, tm=tm, name=name)[0]


def _conv_bwd(dy_b, proj, conv_w, *, name, tm=512, plan=None):
    s = proj.shape[0]
    c = CONV_WIDTH

    def body(i, n, dy_ref, u_ref, gb_ref, gc_ref, up_ref, gcp_ref, dyn_ref, gbn_ref, w_ref, d_ref, dw_ref):
        first, last = i == 0, i == n - 1
        u = u_ref[...].astype(F32)
        gb = gb_ref[...].astype(F32)
        gc = gc_ref[...].astype(F32)
        cu = gc * u
        cup = gcp_ref[...].astype(F32) * up_ref[...].astype(F32)
        cu1 = _shift_down(cu, cup, 1, first)
        cu2 = _shift_down(cu, cup, 2, first)
        conv = w_ref[0:1, :] * cu2 + w_ref[1:2, :] * cu1 + w_ref[2:3, :] * cu
        dy = dy_ref[...].astype(F32)
        dyc = dy * gb
        dycn = dyn_ref[...].astype(F32) * gbn_ref[...].astype(F32)
        dcu = (w_ref[2:3, :] * dyc + w_ref[1:2, :] * _shift_up(dyc, dycn, 1, last)
               + w_ref[0:1, :] * _shift_up(dyc, dycn, 2, last))
        d_ref[:, 0:c] = (dcu * gc).astype(BF16)
        d_ref[:, c:2 * c] = (dy * conv).astype(BF16)
        d_ref[:, 2 * c:3 * c] = (dcu * u).astype(BF16)
        row = lax.broadcasted_iota(jnp.int32, (8, c), 0)
        dw = (jnp.where(row == 0, jnp.sum(dyc * cu2, axis=0, keepdims=True), 0.0)
              + jnp.where(row == 1, jnp.sum(dyc * cu1, axis=0, keepdims=True), 0.0)
              + jnp.where(row == 2, jnp.sum(dyc * cu, axis=0, keepdims=True), 0.0))

        @pl.when(first)
        def _():
            dw_ref[...] = jnp.zeros_like(dw_ref)
        dw_ref[...] += dw

    return _rows(body, [(dy_b, "row", c, 0), (proj, "row", c, 3), (proj, "row", c, 4), (proj, "row", c, 5),
                        (proj, "prev", c, 3), (proj, "prev", c, 5), (dy_b, "next", c, 0), (proj, "next", c, 4),
                        (conv_w, "full", 0, 0)],
                 [((s, 3 * c), BF16, "row"), ((8, c), F32, "acc")], n_rows=s, tm=tm, name=name, plan=plan)


def _mem_probs(q, k, scale):
    sc = _bdot(q, k, NT) * scale
    sc = sc - jnp.max(sc, axis=-1, keepdims=True)
    p = jnp.exp(sc)
    return p / jnp.sum(p, axis=-1, keepdims=True)


def _mem_sublayer(hq, w_q, kv, w_o, x, g, *, name, tm=512, plan=None):
    s, d = hq.shape
    hd = d // MEM_HEADS
    scale = 1.0 / math.sqrt(hd)
    tm = min(tm, s)

    def body(hq_ref, wq_ref, kv_ref, wo_ref, x_ref, g_ref, q_ref, o_ref, x2_ref, h_ref):
        q_ref[...] = _bdot(hq_ref[...], wq_ref[...], NN).astype(BF16)
        for h in range(MEM_HEADS):
            cols = slice(h * hd, (h + 1) * hd)
            p = _mem_probs(q_ref[:, cols], kv_ref[:, cols], scale)
            o_ref[:, cols] = _bdot(p, kv_ref[:, d + h * hd:d + (h + 1) * hd], NN).astype(BF16)
        xv = _bdot(o_ref[...], wo_ref[...], NN) + x_ref[...]
        x2_ref[...] = xv
        _rms_fwd_tail(xv, g_ref, h_ref)

    row = pl.BlockSpec((tm, d), lambda i: (i, 0))
    whole = lambda a: pl.BlockSpec(a.shape, lambda i: (0,) * a.ndim)
    return _call(body, name=name, grid=(s // tm,),
                 in_specs=[row, whole(w_q), whole(kv), whole(w_o), row, whole(g)], out_specs=[row] * 4,
                 out_shape=[jax.ShapeDtypeStruct((s, d), BF16), jax.ShapeDtypeStruct((s, d), BF16),
                            jax.ShapeDtypeStruct((s, d), F32), jax.ShapeDtypeStruct((s, d), BF16)],
                 scratch=[], args=[hq, w_q, kv, w_o, x, g], plan=plan)


def _mem_sublayer_bwd(dx2b, dx2, x, g, qm, kv, w_q, w_o, *, name, tm=512, plan=None):
    s, d = qm.shape
    hd = d // MEM_HEADS
    scale = 1.0 / math.sqrt(hd)
    tm = min(tm, s)

    def body(dyb_ref, dres_ref, x_ref, g_ref, q_ref, kv_ref, wq_ref, wo_ref, dx_ref, dxb_ref, dg_ref, dq_ref, dkv_ref):
        i = pl.program_id(0)

        @pl.when(i == 0)
        def _():
            dkv_ref[...] = jnp.zeros_like(dkv_ref)
        dom = _bdot(dyb_ref[...], wo_ref[...], NT).astype(BF16)
        for h in range(MEM_HEADS):
            cols = slice(h * hd, (h + 1) * hd)
            vcols = slice(d + h * hd, d + (h + 1) * hd)
            q, k, v, do = q_ref[:, cols], kv_ref[:, cols], kv_ref[:, vcols], dom[:, cols]
            p = _mem_probs(q, k, scale)
            dp = _bdot(do, v, NT)
            ds = p * (dp - jnp.sum(dp * p, axis=-1, keepdims=True)) * scale
            dq_ref[:, cols] = _bdot(ds, k, NN).astype(BF16)
            dkv_ref[:, cols] += _bdot(ds, q, TN)
            dkv_ref[:, vcols] += _bdot(p, do, TN)
        dh = _bdot(dq_ref[...], wq_ref[...], NT)
        _rms_bwd_tail(i, dh, x_ref, g_ref, dres_ref, dx_ref, dxb_ref, dg_ref)

    row = pl.BlockSpec((tm, d), lambda i: (i, 0))
    whole = lambda a: pl.BlockSpec(a.shape, lambda i: (0,) * a.ndim)
    return _call(body, name=name, grid=(s // tm,),
                 in_specs=[row, row, row, whole(g), row, whole(kv), whole(w_q), whole(w_o)],
                 out_specs=[row, row, pl.BlockSpec((8, d), lambda i: (0, 0)), row, whole(kv)],
                 out_shape=[jax.ShapeDtypeStruct((s, d), F32), jax.ShapeDtypeStruct((s, d), BF16),
                            jax.ShapeDtypeStruct((8, d), F32), jax.ShapeDtypeStruct((s, d), BF16),
                            jax.ShapeDtypeStruct(kv.shape, F32)],
                 scratch=[], args=[dx2b, dx2, x, g, qm, kv, w_q, w_o], plan=plan)


def _sb_consts(t):
    row = lax.broadcasted_iota(jnp.int32, (t, t), 0)
    col = lax.broadcasted_iota(jnp.int32, (t, t), 1)
    lane = lax.broadcasted_iota(jnp.int32, (t, LANES), 1)
    return row, col, lane < SB_HEAD_DIM


def _sb_logits(q, k):
    z2 = jnp.minimum(_bdot(q, k, NT) * LOG2_E, SB_CLAMP)
    return z2, jnp.exp2(z2)


def _tri_sum(v, tri):
    hi = v.astype(BF16)
    lo = (v - hi.astype(F32)).astype(BF16)
    return _bdot(hi, tri, NN) + _bdot(lo, tri, NN)


def _sb_fwd(proj, *, name, plan=None):
    s = proj.shape[0]
    t, nh = SB_TILE, SB_STEP_HEADS
    n_q = s // t
    scale = 1.0 / math.sqrt(SB_HEAD_DIM)

    def body(q_ref, k_ref, v_ref, o_ref, c_ref, first_ref, acc_ref, c_scr):
        i = pl.program_id(1)
        row, col, head0 = _sb_consts(t)
        later = (row > col).astype(BF16)
        valid = col < row
        lanes = lambda h: slice((h // 2) * LANES, (h // 2 + 1) * LANES)
        q = [jnp.where(head0 == (h % 2 == 0), q_ref[:, lanes(h)] * scale, 0) for h in range(nh)]

        def tiles(kbs, diag_first, carry):
            rows = [pl.ds(pl.multiple_of(kb * t, t), t) for kb in kbs]
            jobs = [(n, h) for n in range(len(kbs)) for h in range(nh)]
            masked = lambda n: diag_first and n == 0
            zs = {(n, h): _sb_logits(q[h], k_ref[rows[n], lanes(h)]) for n, h in jobs}
            fail = {j: jnp.log2(1.0 + zs[j][1]) for j in jobs}
            fail = {j: jnp.where(valid, fail[j], 0.0) if masked(j[0]) else fail[j] for j in jobs}
            cum = {j: _tri_sum(fail[j], later) for j in jobs}
            run, before = list(carry), {}
            for n, h in jobs:
                before[n, h] = run[h]
                run[h] = run[h] + cum[n, h][:, 0:1] + fail[n, h][:, 0:1]
            w = {j: jnp.exp2(zs[j][0] - fail[j] - cum[j] - before[j]) for j in jobs}
            w = {j: jnp.where(valid, w[j], 0.0) if masked(j[0]) else w[j] for j in jobs}
            for n, h in jobs:
                acc_ref[h] += _bdot(w[n, h], v_ref[rows[n], lanes(h)], NN)
            return tuple(run)

        acc_ref[...] = jnp.zeros_like(acc_ref)
        zero = (jnp.zeros((t, 1), F32),) * nh

        def alive(carry):
            return (functools.reduce(jnp.minimum, [jnp.min(c) for c in carry]) < SB_DEAD).astype(jnp.int32)

        def step(state):
            new = tiles([state[0]], False, state[2:])
            return (state[0] - 1, alive(new)) + new

        @pl.when(i == 0)
        def _():
            for h, c in enumerate(tiles([i], True, zero)):
                c_scr[h] = c

        @pl.when(i > 0)
        def _():
            for h, c in enumerate(tiles([i, i - 1], True, zero)):
                c_scr[h] = c
        carry = tuple(c_scr[h] for h in range(nh))
        state = lax.while_loop(lambda st: jnp.logical_and(st[0] >= 0, st[1] > 0), step, (i - 2, alive(carry)) + carry)
        for b in range(nh // 2):
            o_ref[:, b * LANES:(b + 1) * LANES] = jnp.where(head0, acc_ref[2 * b], acc_ref[2 * b + 1]).astype(BF16)
        head = lax.broadcasted_iota(jnp.int32, (t, nh), 1)
        c_ref[...] = sum(jnp.where(head == h, state[2 + h], 0.0) for h in range(nh))
        first_ref[pl.program_id(0), i] = (jnp.maximum(state[0], -1) + 1).astype(F32)

    n_p, width = SB_HEADS // nh, nh * SB_HEAD_DIM
    k_blk, v_blk = SB_WIDTH // width, 2 * SB_WIDTH // width
    return _call(
        body, name=name, grid=(n_p, n_q),
        in_specs=[pl.BlockSpec((t, width), lambda p, i: (i, p)),
                  pl.BlockSpec((s, width), lambda p, i: (0, k_blk + p)),
                  pl.BlockSpec((s, width), lambda p, i: (0, v_blk + p))],
        out_specs=[pl.BlockSpec((t, width), lambda p, i: (i, p)),
                   pl.BlockSpec((None, t, nh), lambda p, i: (p, i, 0)),
                   pl.BlockSpec(memory_space=pltpu.SMEM)],
        out_shape=[jax.ShapeDtypeStruct((s, SB_WIDTH), BF16), jax.ShapeDtypeStruct((n_p, s, nh), F32),
                   jax.ShapeDtypeStruct((n_p, n_q), F32)],
        scratch=[pltpu.VMEM((nh, t, LANES), F32), pltpu.VMEM((nh, t, 1), F32)], args=[proj, proj, proj], plan=plan)


def _sb_bwd(proj, do_a, ctot, first, *, name, plan=None):
    s = proj.shape[0]
    t, nh = SB_TILE, SB_STEP_HEADS
    n_q = s // t
    scale = 1.0 / math.sqrt(SB_HEAD_DIM)

    def body(q_ref, k_ref, v_ref, do_ref, c_ref, first_ref, dq_ref, dk_ref, dv_ref, dq_acc, dk_acc, dv_acc):
        i = pl.program_id(1)
        kb0 = jnp.clip(first_ref[pl.program_id(0), i].astype(jnp.int32), 0, i)
        row, col, head0 = _sb_consts(t)
        upto = (row <= col).astype(BF16)
        before = (row < col).astype(BF16)
        valid = col < row
        lanes = lambda h: slice((h // 2) * LANES, (h // 2 + 1) * LANES)
        q2 = [jnp.where(head0 == (h % 2 == 0), q_ref[:, lanes(h)] * scale, 0) for h in range(nh)]
        do2 = [jnp.where(head0 == (h % 2 == 0), do_ref[:, lanes(h)], 0) for h in range(nh)]
        ctot2 = [c_ref[:, h:h + 1] for h in range(nh)]

        @pl.when(i == 0)
        def _():
            dk_acc[...] = jnp.zeros_like(dk_acc)
            dv_acc[...] = jnp.zeros_like(dv_acc)
        dq_acc[...] = jnp.zeros_like(dq_acc)

        def tiles(kbs, diag_last, carry):
            rows = [pl.ds(pl.multiple_of(kb * t, t), t) for kb in kbs]
            kt = {(n, h): k_ref[rows[n], lanes(h)] for n in range(len(kbs)) for h in range(nh)}
            jobs = list(kt)
            masked = lambda n: diag_last and n == len(kbs) - 1
            t_last = slice(t - 1, t)
            zs = {(n, h): _sb_logits(q2[h], kt[n, h]) for n, h in jobs}
            dw = {(n, h): _bdot(do2[h], v_ref[rows[n], lanes(h)], NT) for n, h in jobs}
            fail = {j: jnp.log2(1.0 + zs[j][1]) for j in jobs}
            fail = {j: jnp.where(valid, fail[j], 0.0) if masked(j[0]) else fail[j] for j in jobs}
            cum = {j: _tri_sum(fail[j], upto) for j in jobs}
            miss = {j: jnp.exp2(-fail[j]) for j in jobs}
            beta = {j: zs[j][1] * miss[j] for j in jobs}
            fail_run, fail_before = list(carry[0::2]), {}
            for n, h in jobs:
                fail_before[n, h] = fail_run[h]
                fail_run[h] = fail_run[h] + cum[n, h][:, t_last]
            w = {(n, h): beta[n, h] * jnp.exp2(fail_before[n, h] + cum[n, h] - ctot2[h]) for n, h in jobs}
            w = {j: jnp.where(valid, w[j], 0.0) if masked(j[0]) else w[j] for j in jobs}
            g = {j: w[j] * dw[j] for j in jobs}
            g_local = {j: _bdot(g[j], before, NN) for j in jobs}
            for n, h in jobs:
                dv_acc[rows[n], lanes(h)] += _bdot(w[n, h], do2[h], TN)
            g_run, dz = list(carry[1::2]), {}
            for n, h in jobs:
                g_sum = g_run[h] + g_local[n, h]
                dz[n, h] = g[n, h] * miss[n, h] - beta[n, h] * g_sum
                g_run[h] = g_sum[:, t_last] + g[n, h][:, t_last]
            dz = {j: jnp.where(valid, dz[j], 0.0) if masked(j[0]) else dz[j] for j in jobs}
            for n, h in jobs:
                dq_acc[h] += _bdot(dz[n, h], kt[n, h], NN)
                dk_acc[rows[n], lanes(h)] += _bdot(dz[n, h], q2[h], TN)
            return tuple(v for pair in zip(fail_run, g_run) for v in pair)

        zero = jnp.zeros((t, 1), F32)
        carry = lax.fori_loop(kb0, i - 1, lambda n, c: tiles([n], False, c), (zero,) * (2 * nh))

        @pl.when(i == 0)
        def _():
            tiles([i], True, carry)

        @pl.when(i > 0)
        def _():
            tiles([i - 1, i], True, carry)
        for b in range(nh // 2):
            dq_ref[:, b * LANES:(b + 1) * LANES] = (jnp.where(head0, dq_acc[2 * b], dq_acc[2 * b + 1])
                                                    * scale).astype(BF16)

        @pl.when(i == n_q - 1)
        def _():
            dk_ref[...] = dk_acc[...].astype(BF16)
            dv_ref[...] = dv_acc[...].astype(BF16)

    n_p, width = SB_HEADS // nh, nh * SB_HEAD_DIM
    k_blk, v_blk = SB_WIDTH // width, 2 * SB_WIDTH // width
    outs = _call(
        body, name=name, grid=(n_p, n_q),
        in_specs=[pl.BlockSpec((t, width), lambda p, i: (i, p)),
                  pl.BlockSpec((s, width), lambda p, i: (0, k_blk + p)),
                  pl.BlockSpec((s, width), lambda p, i: (0, v_blk + p)),
                  pl.BlockSpec((t, width), lambda p, i: (i, p)),
                  pl.BlockSpec((None, t, nh), lambda p, i: (p, i, 0)),
                  pl.BlockSpec(memory_space=pltpu.SMEM)],
        out_specs=[pl.BlockSpec((t, width), lambda p, i: (i, p)),
                   pl.BlockSpec((s, width), lambda p, i: (0, p)),
                   pl.BlockSpec((s, width), lambda p, i: (0, p))],
        out_shape=[jax.ShapeDtypeStruct((s, SB_WIDTH), BF16)] * 3,
        scratch=[pltpu.VMEM((nh, t, LANES), F32), pltpu.VMEM((s, width), F32), pltpu.VMEM((s, width), F32)],
        args=[proj, proj, proj, do_a, ctot, first], plan=plan)
    return jnp.concatenate(outs, axis=1)


def _mm_gathered(a, key, plan, *, name, out3=False, w_t=False):
    src = plan.gathering(key)
    if src is None:
        return _mm_nn(a, plan.weight(key), name=name, out3=out3, w_t=w_t, plan=plan)
    out, w_all = _mm_gathering(a, src, name=name, out3=out3, w_t=w_t)
    plan.set_weight(key, w_all)
    return out


def _local_step(x, mem, target, gains, plan):
    g_mix, g_memq, g_memkv, g_ffn, g_fin = gains
    d = x.shape[1]

    h0 = _rms_fwd(x, g_mix, name="rms_mix")
    proj = _mm_gathered(h0, "in", plan, name="mm_in")
    w_in = plan.weight("in")
    o_a, ctot, first = _sb_fwd(proj, name="sb_fwd", plan=plan)
    conv_w = plan.weight("conv")
    y_b = _conv_fwd(proj, conv_w, name="conv_fwd")
    w_a, w_b, w_mix = plan.weight("a"), plan.weight("b"), plan.weight("mix")
    x1, hq, merged, br_a, br_b = _mix_out(o_a, y_b, proj, w_a[0], w_b[0], w_mix[0], x, g_memq, name="mm_mix", plan=plan)
    w_mq, w_kv, w_mo = plan.weight("mq")[0], plan.weight("kv"), plan.weight("mo")[0]
    mn = _rms_fwd(mem, g_memkv, name="rms_memkv")
    kv = _mm_nn(mn, w_kv, name="mm_memkv")
    qm, om, x2, hf = _mem_sublayer(hq, w_mq, kv, w_mo, x1, g_ffn, name="mem_sublayer", plan=plan)
    gu = _mm_gathered(hf, "fi", plan, name="mm_ffn_in", out3=True, w_t=True)
    w_fi, w_fo = plan.weight("fi"), plan.weight("fo")
    dx3, dx3b, dg_fin, loss, act = _ffn_out_loss(gu, w_fo, x2, g_fin, target, name="mm_ffn_out")

    plan.grad("fo", _mm_tn_a3(act, dx3b, name="mm_d_w_ffn_out"))
    dgu = _ffn_out_bwd(dx3b, w_fo, gu, name="mm_d_act")
    plan.grad("fi", _mm_tn_a3(dgu, hf, name="mm_d_w_ffn_in"))
    dx2, dx2b, dg_ffn = _mm_nt_rms(dgu, w_fi, x2, g_ffn, dx3, name="mm_d_hf", dy3=True, w_nn=True, plan=plan)

    plan.grad("mo", _mm_tn(om, dx2b, d, name="mm_d_w_memo"))
    dx1, dx1b, dg_memq, dqm, dkv = _mem_sublayer_bwd(dx2b, dx2, x1, g_memq, qm, kv, w_mq, w_mo, name="mem_sublayer_bwd",
                                                    plan=plan)
    plan.grad("mq", _mm_tn(hq, dqm, d, name="mm_d_w_memq"))
    plan.grad("kv", _mm_tn(mn, dkv, w_kv.shape[2], name="mm_d_w_memkv"))
    _, _, dg_memkv = _mm_nt_rms(dkv, w_kv, mem, g_memkv, None, name="mm_d_mn")

    plan.grad("mix", _mm_tn(merged, dx1b, d, name="mm_d_w_mix"))
    dbr_a, dbr_b, dgab, do_a, dy_b = _mix_out_bwd(dx1b, w_mix[0], br_a, br_b, proj, w_a[0], w_b[0], name="mm_d_merged",
                                                 plan=plan)
    plan.grad("a", _mm_tn(o_a, dbr_a, d, name="mm_d_w_branch_a"))
    plan.grad("b", _mm_tn(y_b, dbr_b, d, name="mm_d_w_branch_b"))
    dconv, dconv_w = _conv_bwd(dy_b, proj, conv_w, name="conv_bwd", plan=plan)
    dqkv = _sb_bwd(proj, do_a, ctot, first, name="sb_bwd", plan=plan)
    dproj = jnp.concatenate([dqkv, dconv, dgab], axis=1)
    rows_in1 = d // IN_SPLIT[1] * (IN_SPLIT[1] - IN_SPLIT[0])
    plan.grad("in0", _mm_tn(h0, dproj, w_in.shape[2], name="mm_d_w_in0", tm=d - rows_in1, k_tiles=(0, 1)))
    plan.grad("in1", _mm_tn(h0, dproj, w_in.shape[2], name="mm_d_w_in1", tm=rows_in1,
                            k_tiles=(d // rows_in1 - 1, 1), plan=plan))
    dh0 = _mm_nt(dproj, w_in, name="mm_d_h0", out_dtype=F32, plan=plan)
    dx0, _, dg_mix = _rms_bwd(x, g_mix, dh0, dx1, name="rms_mix_bwd", plan=plan)

    return dx0, (dg_mix, dg_memq, dg_memkv, dg_ffn, dg_fin, dconv_w, loss)


def _row_tile(a, target=512):
    tm = min(a, target)
    while a % tm:
        tm -= 8
    return tm


def _sum_with_sibling(parts, recvs, core, *, name):
    n = len(parts)

    def body(core_ref, *refs):
        for p_ref, r_ref, o_ref in zip(refs[:n], refs[n:2 * n], refs[2 * n:]):
            o_ref[...] = (p_ref[...].astype(F32) + r_ref[...].astype(F32)).astype(o_ref.dtype)

    mine = [pl.BlockSpec((None,) + p.shape[1:], lambda q, core_ref: (2 * q + core_ref[0], 0, 0)) for p in parts]
    other = [pl.BlockSpec((None,) + p.shape[1:], lambda q, core_ref: (q, 0, 0)) for p in parts]
    return pl.pallas_call(
        body, name=name,
        grid_spec=pltpu.PrefetchScalarGridSpec(num_scalar_prefetch=1, grid=(N_CHIP,), in_specs=mine + other,
                                               out_specs=other),
        out_shape=[jax.ShapeDtypeStruct((N_CHIP,) + p.shape[1:], p.dtype) for p in parts],
        compiler_params=_params(1))(core, *parts, *recvs)


def _adam_math(wv, g, m, v):
    m = ADAM_B1 * m + (1.0 - ADAM_B1) * g
    v = ADAM_B2 * v + (1.0 - ADAM_B2) * (g * g)
    m_hat = m / (1.0 - ADAM_B1 ** ADAM_STEP)
    v_hat = v / (1.0 - ADAM_B2 ** ADAM_STEP)
    delta = -ADAM_LR * (m_hat / (jnp.sqrt(v_hat) + ADAM_EPS) + ADAM_WD * wv)
    return delta, m, v


def _adam_sharded(wv, m, v, own, recv, chip, *, name):
    a, b = wv.shape
    tm = _row_tile(a)

    def body(chip_ref, w_ref, m_ref, v_ref, own_ref, recv_ref, g_ref, d_ref, nm_ref, nv_ref):
        g = own_ref[...].astype(F32)
        for j in range(3):
            g = g + recv_ref[j].astype(F32)
        delta, nm, nv = _adam_math(w_ref[...], g, m_ref[...], v_ref[...])
        g_ref[...] = g
        d_ref[...] = delta
        nm_ref[...] = nm
        nv_ref[...] = nv

    tile = pl.BlockSpec((tm, b), lambda i, chip_ref: (i, 0))
    return pl.pallas_call(
        body, name=name,
        grid_spec=pltpu.PrefetchScalarGridSpec(
            num_scalar_prefetch=1, grid=(a // tm,),
            in_specs=[tile, tile, tile,
                      pl.BlockSpec((None, tm, b), lambda i, chip_ref: (chip_ref[0], i, 0)),
                      pl.BlockSpec((3, tm, b), lambda i, chip_ref: (0, i, 0))],
            out_specs=[tile] * 4),
        out_shape=[jax.ShapeDtypeStruct((a, b), F32)] * 4, compiler_params=_params(1))(chip, wv, m, v, own, recv)


def _sum_devices(gathered, *, name):
    _, r, c = gathered.shape

    def body(g_ref, o_ref):
        total = g_ref[0]
        for j in range(1, N_DEV):
            total = total + g_ref[j]
        o_ref[...] = total

    return pl.pallas_call(body, name=name, out_shape=jax.ShapeDtypeStruct((r, c), F32))(gathered)


def _adam_small(wv, g, m, v, *, name):
    def body(w_ref, g_ref, m_ref, v_ref, d_ref, nm_ref, nv_ref):
        delta, nm, nv = _adam_math(w_ref[...], g_ref[...], m_ref[...], v_ref[...])
        d_ref[...] = delta
        nm_ref[...] = nm
        nv_ref[...] = nv

    return pl.pallas_call(body, name=name, out_shape=[jax.ShapeDtypeStruct(wv.shape, F32)] * 3)(wv, g, m, v)


BIG = ("in", "a", "b", "mix", "mq", "kv", "mo", "fi", "fo")
ROW_SHARDED = ("mix", "mq", "mo")
UNSHARDED = ("a", "b")
FFN_GROUPS = 4
IN_SPLIT = (3, 4)
SMALL_ROWS = 16


class _Plan:
    FUSED = ("in",)
    GATHER_ON = {"sb_fwd": ("a", "b", "mix", "mq", "mo", "conv", "fi0"), "mm_mix": ("kv",), "mem_sublayer": ("fi1",),
                 "mm_ffn_in": ("fo",)}
    SIBLING_ON = {"mm_d_hf": ("fo", "fi"), "mm_d_merged": ("mo", "mq", "kv"), "conv_bwd": ("mix", "a", "b"),
                  "mm_d_w_in1": ("in0",), "mm_d_h0": ("in1",)}
    CHIPS_ON = {"mem_sublayer_bwd": ("fo",), "sb_bwd": ("fi", "mo", "mq", "kv", "mix", "a", "b"), "mm_d_h0": ("in0",),
                "rms_mix_bwd": ("in1",)}

    def __init__(self, shards, core):
        self.shards, self.core = shards, core
        self.w, self.parts, self.chip_sums, self.from_chips = {}, {}, {}, {}

    def gathering(self, k):
        return self.shards[k] if k in self.FUSED else None

    def comm(self, name):
        comms = []
        if name in self.GATHER_ON:
            comms.append(_gather_comm([self.shards[k] for k in self.GATHER_ON[name]]))
        if name in self.SIBLING_ON:
            comms.append(_sibling_comm([self.parts[k] for k in self.SIBLING_ON[name]]))
        if name in self.CHIPS_ON:
            comms.append(_chips_comm([self.chip_sums[k] for k in self.CHIPS_ON[name]]))
        return _join_comms(comms) if comms else None

    def landed(self, name, outs):
        outs = list(outs)
        for k in self.GATHER_ON.get(name, ()):
            self.set_weight(k, outs.pop(0))
        keys = self.SIBLING_ON.get(name, ())
        if keys:
            sums = _sum_with_sibling([self.parts[k] for k in keys], [outs.pop(0) for _ in keys], self.core,
                                     name="sum_with_sibling_" + "_".join(keys))
            self.chip_sums.update(zip(keys, sums))
        for k in self.CHIPS_ON.get(name, ()):
            self.from_chips[k] = outs.pop(0)

    def set_weight(self, k, gathered):
        _, a, b = gathered.shape
        if k in ROW_SHARDED:
            gathered = gathered.reshape(1, N_DEV * a, b)
        elif k in UNSHARDED:
            gathered = jnp.transpose(gathered, (1, 0, 2)).reshape(1, a, N_DEV * b)
        elif k == "fo":
            gathered = gathered.reshape(FFN_GROUPS, N_DEV * a // FFN_GROUPS, b)
        elif k == "conv":
            n_conv = CONV_WIDTH // N_DEV
            gathered = jnp.transpose(gathered[:, :3, :n_conv], (1, 0, 2)).reshape(3, CONV_WIDTH)
        self.w[k] = gathered
        if k == "fi1":
            self.w["fi"] = jnp.concatenate([self.w["fi0"], gathered], axis=2)

    def weight(self, k):
        return self.w[k]

    def grad(self, k, g):
        _, a, b = g.shape
        if k in ROW_SHARDED:
            g = g.reshape(N_DEV, a // N_DEV, b)
        elif k in UNSHARDED:
            g = jnp.transpose(g.reshape(a, N_DEV, b // N_DEV), (1, 0, 2))
        elif k == "fo":
            g = g.reshape(N_DEV, FFN_GROUPS * a // N_DEV, b)
        self.parts[k] = g


def kernel(x, mem, norm_mix, w_in, conv_w, w_branch_a, w_branch_b, w_mix_out, norm_mem_q, norm_mem_kv, w_mem_q, w_mem_kv, w_mem_o, norm_ffn, w_ffn_in, w_ffn_out, norm_final, loss_target, m_norm_mix, m_w_in, m_conv_w, m_w_branch_a, m_w_branch_b, m_w_mix_out, m_norm_mem_q, m_norm_mem_kv, m_w_mem_q, m_w_mem_kv, m_w_mem_o, m_norm_ffn, m_w_ffn_in, m_w_ffn_out, m_norm_final, v_norm_mix, v_w_in, v_conv_w, v_w_branch_a, v_w_branch_b, v_w_mix_out, v_norm_mem_q, v_norm_mem_kv, v_w_mem_q, v_w_mem_kv, v_w_mem_o, v_norm_ffn, v_w_ffn_in, v_w_ffn_out, v_norm_final):
    d = x.shape[-1]
    xi, yi, ci = lax.axis_index("x"), lax.axis_index("y"), lax.axis_index("c")
    chip = jnp.reshape(2 * xi + yi, (1,)).astype(jnp.int32)
    dev = 4 * xi + 2 * yi + ci

    big_w = dict(zip(BIG, (w_in, w_branch_a, w_branch_b, w_mix_out, w_mem_q, w_mem_kv, w_mem_o, w_ffn_in, w_ffn_out)))
    big_m = dict(zip(BIG, (m_w_in, m_w_branch_a, m_w_branch_b, m_w_mix_out, m_w_mem_q, m_w_mem_kv, m_w_mem_o, m_w_ffn_in, m_w_ffn_out)))
    big_v = dict(zip(BIG, (v_w_in, v_w_branch_a, v_w_branch_b, v_w_mix_out, v_w_mem_q, v_w_mem_kv, v_w_mem_o, v_w_ffn_in, v_w_ffn_out)))

    flip = lambda t, k: jnp.transpose(t) if k == "fi" else t
    shards = {k: flip(big_w[k][0], k).astype(BF16) for k in BIG}
    shards["fi0"], shards["fi1"] = shards["fi"][:, :d // 2], shards["fi"][:, d // 2:]
    n_conv = conv_w.shape[-1]
    shards["conv"] = jnp.zeros((8, LANES), F32).at[:3, :n_conv].set(conv_w[0])
    plan = _Plan(shards, jnp.reshape(ci, (1,)).astype(jnp.int32))

    gains = (norm_mix, norm_mem_q, norm_mem_kv, norm_ffn, norm_final.reshape(1, d))
    dx0, small = _local_step(x[0], mem[0], loss_target[0], gains, plan)

    grads, deltas, new_m, new_v = {}, {}, {}, {}
    for k in BIG:
        lead = big_w[k].shape
        wv, mv, vv = flip(big_w[k][0], k), flip(big_m[k][0], k), flip(big_v[k][0], k)
        if k == "in":
            half = wv.shape[0] * IN_SPLIT[0] // IN_SPLIT[1]
            lo = _adam_sharded(wv[:half], mv[:half], vv[:half], plan.chip_sums["in0"], plan.from_chips["in0"], chip,
                               name="adam_in0")
            hi = _adam_sharded(wv[half:], mv[half:], vv[half:], plan.chip_sums["in1"], plan.from_chips["in1"], chip,
                               name="adam_in1")
            outs = [jnp.concatenate(pair, axis=0) for pair in zip(lo, hi)]
        else:
            outs = _adam_sharded(wv, mv, vv, plan.chip_sums[k], plan.from_chips[k], chip, name="adam_" + k)
        grads[k], deltas[k], new_m[k], new_v[k] = (flip(t, k).reshape(lead) for t in outs)

    dg_mix, dg_memq, dg_memkv, dg_ffn, dg_fin, dconv_w, loss = small
    conv_rows = jnp.zeros((3, d), F32).at[:, :CONV_WIDTH].set(dconv_w[:3])
    block = jnp.concatenate([dg_mix[:1], dg_memq[:1], dg_memkv[:1], dg_ffn[:1], dg_fin[:1], conv_rows,
                             jnp.broadcast_to(loss[:1, :1], (1, d)), jnp.zeros((SMALL_ROWS - 9, d), F32)], axis=0)
    total = _sum_devices(_exchange(_gather_comm([block]), name="gather_small")[0], name="sum_small")
    g_conv = lax.dynamic_slice(total[5:8, :CONV_WIDTH], (0, dev * n_conv), (3, n_conv))
    small_w = [norm_mix, norm_mem_q, norm_mem_kv, norm_ffn, norm_final.reshape(1, d), conv_w[0]]
    small_m = [m_norm_mix, m_norm_mem_q, m_norm_mem_kv, m_norm_ffn, m_norm_final.reshape(1, d), m_conv_w[0]]
    small_v = [v_norm_mix, v_norm_mem_q, v_norm_mem_kv, v_norm_ffn, v_norm_final.reshape(1, d), v_conv_w[0]]
    small_g = [total[0:1], total[1:2], total[2:3], total[3:4], total[4:5], g_conv]
    small_names = ["norm_mix", "norm_mem_q", "norm_mem_kv", "norm_ffn", "norm_final", "conv_w"]
    sg, sd, sm, sv = {}, {}, {}, {}
    for nme, wv, g, m, v in zip(small_names, small_w, small_g, small_m, small_v):
        dl, nm, nv = _adam_small(wv, g, m, v, name="adam_" + nme)
        shape = norm_final.shape if nme == "norm_final" else (conv_w.shape if nme == "conv_w" else wv.shape)
        sg[nme], sd[nme], sm[nme], sv[nme] = (t.reshape(shape) for t in (g, dl, nm, nv))

    def ordered(big, sml):
        return (sml["norm_mix"], big["in"], sml["conv_w"], big["a"], big["b"], big["mix"], sml["norm_mem_q"],
                sml["norm_mem_kv"], big["mq"], big["kv"], big["mo"], sml["norm_ffn"], big["fi"], big["fo"],
                sml["norm_final"])

    loss_out = total[8, 0]
    grad_x = dx0.reshape(x.shape)
    return (loss_out, grad_x, *ordered(grads, sg), *ordered(deltas, sd), *ordered(new_m, sm), *ordered(new_v, sv))
```

```python
import functools
import math

import jax
import jax.numpy as jnp
from jax import lax
from jax.experimental import pallas as pl
from jax.experimental.pallas import tpu as pltpu

F32 = jnp.float32
BF16 = jnp.bfloat16
MESH = pl.DeviceIdType.MESH

N_DEV = 8
N_CHIP = 4
NORM_EPS = 1e-6
SB_HEADS = 8
SB_HEAD_DIM = 64
SB_WIDTH = SB_HEADS * SB_HEAD_DIM
CONV_WIDTH = 512
MEM_HEADS = 4
ADAM_LR = 0.001
ADAM_B1 = 0.9
ADAM_B2 = 0.999
ADAM_EPS = 1e-08
ADAM_WD = 0.01
ADAM_STEP = 10

LANES = 128
VMEM_LIMIT_BYTES = 52 * 1024 * 1024
SB_TILE = 256
SB_STEP_HEADS = 4
SB_DEAD = 159.0
SB_CLAMP = 126.0
LOG2_E = 1.4426950408889634

ANY = pl.BlockSpec(memory_space=pl.ANY)


def _params(n_grid):
    return pltpu.CompilerParams(dimension_semantics=("arbitrary",) * n_grid, vmem_limit_bytes=VMEM_LIMIT_BYTES)


def _bdot(a, b, dims):
    return lax.dot_general(a.astype(BF16), b.astype(BF16), (dims, ((), ())), preferred_element_type=F32)


NN = ((1,), (0,))
NT = ((1,), (1,))
TN = ((0,), (0,))


class _Comm:
    def __init__(self, ins, outs, n_sems, start, finish):
        self.ins, self.outs, self.n_sems, self.start, self.finish = ins, outs, n_sems, start, finish

    def sem_shapes(self):
        return [pltpu.SemaphoreType.DMA((k,)) for k in self.n_sems]


def _place():
    return lax.axis_index("x"), lax.axis_index("y"), lax.axis_index("c")


def _neighbours(x, y, c):
    return [(jnp.bitwise_xor(x, c), jnp.bitwise_xor(y, 1 - c)), (jnp.bitwise_xor(x, 1 - c), jnp.bitwise_xor(y, c)),
            (1 - x, 1 - y)]


def _gather_comm(shards):
    n = len(shards)

    def copies(ins, outs, sems):
        send_sems, recv_sems, _ = sems
        x, y, c = _place()
        chips = [(1 - x, y), (x, 1 - y), (1 - x, 1 - y)]

        def copy(a, k, block, to, from_shard=False):
            dst = outs[a].at[4 * block[0] + 2 * block[1] + block[2]]
            return pltpu.make_async_remote_copy(
                src_ref=ins[a] if from_shard else dst, dst_ref=dst, send_sem=send_sems.at[a * 7 + k],
                recv_sem=recv_sems.at[a * 7 + k], device_id=to, device_id_type=MESH)

        me, sibling = (x, y, c), (x, y, 1 - c)
        own = [[copy(a, 0, me, sibling, True)] + [copy(a, 1 + j, me, (*chip, c), True) for j, chip in enumerate(chips)]
               for a in range(n)]
        landed = [[copy(a, 1 + j, (*chip, c), me) for j, chip in enumerate(chips)] for a in range(n)]
        passed = [[copy(a, 4 + j, (*chip, c), sibling) for j, chip in enumerate(chips)] for a in range(n)]
        from_sibling = [[copy(a, 0, sibling, me)] + [copy(a, 4 + j, (*chip, 1 - c), me) for j, chip in enumerate(chips)]
                        for a in range(n)]
        local = [pltpu.make_async_copy(ins[a], outs[a].at[4 * x + 2 * y + c], sems[2].at[a]) for a in range(n)]
        return own, landed, passed, from_sibling, local

    def start(ins, outs, sems):
        own, _, _, _, local = copies(ins, outs, sems)
        for a in range(n):
            local[a].start()
            for cp in own[a]:
                cp.start()

    def finish(ins, outs, sems):
        own, landed, passed, from_sibling, local = copies(ins, outs, sems)
        for a in range(n):
            for arrived, onward in zip(landed[a], passed[a]):
                arrived.wait_recv()
                onward.start()
        for a in range(n):
            for cp in from_sibling[a]:
                cp.wait_recv()
        for a in range(n):
            for cp in own[a] + passed[a]:
                cp.wait_send()
            local[a].wait()

    outs = [jax.ShapeDtypeStruct((N_DEV,) + s.shape, s.dtype) for s in shards]
    return _Comm(list(shards), outs, (7 * n, 7 * n, n), start, finish)


def _sibling_comm(parts):
    n = len(parts)

    def copies(ins, outs, sems):
        x, y, c = _place()
        return [pltpu.make_async_remote_copy(
            src_ref=ins[a].at[2 * q + 1 - c], dst_ref=outs[a].at[q], send_sem=sems[0].at[a * N_CHIP + q],
            recv_sem=sems[1].at[a * N_CHIP + q], device_id=(x, y, 1 - c), device_id_type=MESH)
            for a in range(n) for q in range(N_CHIP)]

    def start(ins, outs, sems):
        for cp in copies(ins, outs, sems):
            cp.start()

    def finish(ins, outs, sems):
        cps = copies(ins, outs, sems)
        for cp in cps:
            cp.wait_recv()
        for cp in cps:
            cp.wait_send()

    outs = [jax.ShapeDtypeStruct((N_CHIP,) + p.shape[1:], p.dtype) for p in parts]
    return _Comm(list(parts), outs, (N_CHIP * n, N_CHIP * n), start, finish)


def _chips_comm(parts):
    n = len(parts)

    def copies(ins, outs, sems):
        x, y, c = _place()
        chips = [(1 - x, y), (x, 1 - y), (1 - x, 1 - y)]
        return [pltpu.make_async_remote_copy(
            src_ref=ins[a].at[2 * px + py], dst_ref=outs[a].at[j], send_sem=sems[0].at[a * 3 + j],
            recv_sem=sems[1].at[a * 3 + j], device_id=(px, py, c), device_id_type=MESH)
            for a in range(n) for j, (px, py) in enumerate(chips)]

    def start(ins, outs, sems):
        for cp in copies(ins, outs, sems):
            cp.start()

    def finish(ins, outs, sems):
        cps = copies(ins, outs, sems)
        for cp in cps:
            cp.wait_recv()
        for cp in cps:
            cp.wait_send()

    outs = [jax.ShapeDtypeStruct((3,) + p.shape[1:], p.dtype) for p in parts]
    return _Comm(list(parts), outs, (3 * n, 3 * n), start, finish)


def _join_comms(comms):
    if len(comms) == 1:
        return comms[0]

    def split(refs, counts):
        out, at = [], 0
        for n in counts:
            out.append(refs[at:at + n])
            at += n
        return out

    def each(method):
        def run(ins, outs, sems):
            parts = zip(comms, split(ins, [len(c.ins) for c in comms]), split(outs, [len(c.outs) for c in comms]),
                        split(sems, [len(c.n_sems) for c in comms]))
            for c, c_ins, c_outs, c_sems in parts:
                getattr(c, method)(c_ins, c_outs, c_sems)
        return run

    return _Comm([a for c in comms for a in c.ins], [o for c in comms for o in c.outs],
                 tuple(k for c in comms for k in c.n_sems), each("start"), each("finish"))


def _exchange(comm, *, name):
    n_ci, n_co = len(comm.ins), len(comm.outs)

    def kern(*refs):
        c_ins, c_outs, sems = refs[:n_ci], refs[n_ci:n_ci + n_co], refs[n_ci + n_co:]
        comm.start(c_ins, c_outs, sems)
        comm.finish(c_ins, c_outs, sems)

    return pl.pallas_call(kern, name=name, in_specs=[ANY] * n_ci, out_specs=[ANY] * n_co, out_shape=comm.outs,
                          scratch_shapes=comm.sem_shapes())(*comm.ins)


def _call(body, *, name, grid, in_specs, out_specs, out_shape, scratch, args, plan=None):
    comm = plan.comm(name) if plan is not None else None
    if comm is None:
        return list(pl.pallas_call(functools.partial(body), name=name, grid=grid, in_specs=in_specs,
                                   out_specs=out_specs, out_shape=out_shape, scratch_shapes=scratch,
                                   compiler_params=_params(len(grid)))(*args))
    n_in, n_out, n_scr, n_ci, n_co = len(in_specs), len(out_specs), len(scratch), len(comm.ins), len(comm.outs)

    def kern(*refs):
        ins, c_ins, refs = refs[:n_in], refs[n_in:n_in + n_ci], refs[n_in + n_ci:]
        outs, c_outs, refs = refs[:n_out], refs[n_out:n_out + n_co], refs[n_out + n_co:]
        scr, sems = refs[:n_scr], refs[n_scr:]
        ids = [pl.program_id(ax) for ax in range(len(grid))]
        first = functools.reduce(jnp.logical_and, [i == 0 for i in ids])
        last = functools.reduce(jnp.logical_and, [i == g - 1 for i, g in zip(ids, grid)])

        @pl.when(first)
        def _():
            comm.start(c_ins, c_outs, sems)
        body(*ins, *outs, *scr)

        @pl.when(last)
        def _():
            comm.finish(c_ins, c_outs, sems)

    res = pl.pallas_call(kern, name=name, grid=grid, in_specs=list(in_specs) + [ANY] * n_ci,
                         out_specs=list(out_specs) + [ANY] * n_co, out_shape=list(out_shape) + comm.outs,
                         scratch_shapes=list(scratch) + comm.sem_shapes(),
                         compiler_params=_params(len(grid)))(*args, *comm.ins)
    plan.landed(name, list(res[n_out:]))
    return list(res[:n_out])


def _mm_body(dims, has_add, *refs):
    if has_add:
        a_ref, b_ref, add_ref, o_ref = refs
        total = _bdot(a_ref[...], b_ref[...], dims) + add_ref[...]
    else:
        a_ref, b_ref, o_ref = refs
        total = _bdot(a_ref[...], b_ref[...], dims)
    o_ref[...] = total.astype(o_ref.dtype)


def _mm_nt_body(j, n, dy_ref, w_ref, o_ref):
    total = _bdot(dy_ref[:, 0:n], w_ref[0], NT)
    for jj in range(1, j):
        total = total + _bdot(dy_ref[:, jj * n:(jj + 1) * n], w_ref[jj], NT)
    o_ref[...] = total.astype(o_ref.dtype)


def _mm_nn(a, w3, *, name, out_dtype=BF16, add=None, tm=1024, tn=None, out3=False, w_t=False, plan=None):
    m, kk = a.shape
    j, n = w3.shape[0], w3.shape[1 if w_t else 2]
    tm, tn = min(tm, m), n if tn is None else tn
    n_t = n // tn
    in_specs = [pl.BlockSpec((tm, kk), lambda i, jj: (i, 0)),
                pl.BlockSpec((None, tn, kk), lambda i, jj: (jj // n_t, jj % n_t, 0)) if w_t else
                pl.BlockSpec((None, kk, tn), lambda i, jj: (jj // n_t, 0, jj % n_t))]
    args = [a, w3]
    if add is not None:
        in_specs.append(pl.BlockSpec((tm, tn), lambda i, jj: (i, jj)))
        args.append(add)
    if out3:
        out_spec = pl.BlockSpec((None, tm, tn), lambda i, jj: (jj // n_t, i, jj % n_t))
        out_shape = jax.ShapeDtypeStruct((j, m, n), out_dtype)
    else:
        out_spec = pl.BlockSpec((tm, tn), lambda i, jj: (i, jj))
        out_shape = jax.ShapeDtypeStruct((m, j * n), out_dtype)
    return _call(
        functools.partial(_mm_body, NT if w_t else NN, add is not None), name=name, grid=(m // tm, j * n_t),
        in_specs=in_specs, out_specs=[out_spec], out_shape=[out_shape], scratch=[], args=args, plan=plan)[0]


def _mm_gathering(a, shard, *, name, out3=False, w_t=False, tm=1024):
    m, kk = a.shape
    n = shard.shape[0 if w_t else 1]
    tm = min(tm, m)
    n_i = m // tm
    fetch_at = min(1, n_i - 1)

    def body(a_ref, shard_ref, o_ref, w_all, w_vmem, send_sems, recv_sems, copy_sems):
        jj, i = pl.program_id(0), pl.program_id(1)
        x, y, c = _place()
        me, sibling = (x, y, c), (x, y, 1 - c)
        chips = _neighbours(x, y, c)
        sibling_chips = [chips[1], chips[0], chips[2]]

        def rows(block):
            return w_all.at[4 * block[0] + 2 * block[1] + block[2]]

        def remote(k, block, to, from_shard=False):
            return pltpu.make_async_remote_copy(
                src_ref=shard_ref if from_shard else rows(block), dst_ref=rows(block), send_sem=send_sems.at[k],
                recv_sem=recv_sems.at[k], device_id=to, device_id_type=MESH)

        def load(step, src):
            return pltpu.make_async_copy(src, w_vmem.at[step % 2], copy_sems.at[1 + step % 2])

        own = [remote(0, me, sibling, True), remote(1, me, (*chips[0], c), True), remote(2, me, (*chips[1], c), True),
               remote(3, (*chips[0], c), (*chips[1], c))]
        passed = [remote(4 + j, (*chip, c), sibling) for j, chip in enumerate(chips)]
        local = pltpu.make_async_copy(shard_ref, rows(me), copy_sems.at[0])

        @pl.when(jnp.logical_and(i == 0, jj == 0))
        def _():
            local.start()
            own[0].start()
            own[1].start()
            load(0, shard_ref).start()

        def arrivals():
            yield 1, (lambda: remote(0, sibling, me).wait_recv()), sibling
            for j, chip in enumerate(chips):
                def landed(j=j, chip=chip):
                    if j < 2:
                        own[1 + j].wait_send()
                        own[2 + j].start()
                    remote(1 + j, (*chip, c), me).wait_recv()
                    passed[j].start()
                yield 2 + 2 * j, landed, (*chip, c)
                block = (*sibling_chips[j], 1 - c)
                yield 3 + 2 * j, (lambda j=j, block=block: remote(4 + j, block, me).wait_recv()), block

        for step, wait_for_it, block in arrivals():
            @pl.when(jnp.logical_and(i == fetch_at, jj == step - 1))
            def _():
                wait_for_it()
                load(step, rows(block)).start()

        for step in range(N_DEV):
            @pl.when(jnp.logical_and(i == 0, jj == step))
            def _():
                load(step, rows(me)).wait()

        o_ref[...] = _bdot(a_ref[...], w_vmem[lax.rem(jj, 2)], NT if w_t else NN).astype(o_ref.dtype)

        @pl.when(jnp.logical_and(i == n_i - 1, jj == N_DEV - 1))
        def _():
            for cp in [own[0], own[3]] + passed:
                cp.wait_send()
            local.wait()

    def swept(jj):
        x, y, c = _place()
        first, second = 2 + 2 * c, 4 - 2 * c
        flips = (0b000, 0b001, first, second + 1, second, first + 1, 0b110, 0b111)
        return jnp.bitwise_xor(4 * x + 2 * y + c, sum(jnp.where(jj == k, f, 0) for k, f in enumerate(flips)))

    if out3:
        out_spec = pl.BlockSpec((None, tm, n), lambda jj, i: (swept(jj), i, 0))
        out_shape = jax.ShapeDtypeStruct((N_DEV, m, n), BF16)
    else:
        out_spec = pl.BlockSpec((tm, n), lambda jj, i: (i, swept(jj)))
        out_shape = jax.ShapeDtypeStruct((m, N_DEV * n), BF16)
    return pl.pallas_call(
        body, name=name, grid=(N_DEV, n_i),
        in_specs=[pl.BlockSpec((tm, kk), lambda jj, i: (i, 0)), ANY], out_specs=[out_spec, ANY],
        scratch_shapes=[pltpu.VMEM((2,) + shard.shape, shard.dtype), pltpu.SemaphoreType.DMA((7,)),
                        pltpu.SemaphoreType.DMA((7,)), pltpu.SemaphoreType.DMA((3,))],
        out_shape=[out_shape, jax.ShapeDtypeStruct((N_DEV,) + shard.shape, shard.dtype)],
        compiler_params=_params(2))(a, shard)


def _sigmoid(v):
    return 0.5 * jnp.tanh(0.5 * v) + 0.5


def _resident(w):
    return pl.BlockSpec(w.shape, lambda i: (0,) * w.ndim, pipeline_mode=pl.Buffered(1))


def _ffn_out_loss(gu3, w3, add, g, target, *, name, tm=512):
    j2, m, n = gu3.shape
    j = j2 // 2
    nn = w3.shape[2]
    tm = min(tm, m)

    def body(gu_ref, w_ref, add_ref, g_ref, t_ref, dx_ref, dxb_ref, dg_ref, loss_ref, act_ref):
        i = pl.program_id(0)
        xv = add_ref[...]
        for jj in range(j):
            gate = gu_ref[0, jj].astype(F32)
            act = (gate * _sigmoid(gate) * gu_ref[1, jj].astype(F32)).astype(BF16)
            act_ref[jj] = act
            xv = xv + _bdot(act, w_ref[jj], NN)
        gv = g_ref[...]
        r = lax.rsqrt(jnp.mean(xv * xv, axis=-1, keepdims=True) + NORM_EPS)
        xhat = xv * r
        err = xhat * gv - t_ref[...]
        _acc_rows(i, loss_ref, 0.5 * jnp.sum(jnp.mean(err * err, axis=-1, keepdims=True), axis=0, keepdims=True))
        dy = err * (1.0 / nn)
        dxhat = dy * gv
        dx = r * (dxhat - xhat * jnp.mean(dxhat * xhat, axis=-1, keepdims=True))
        dx_ref[...] = dx
        dxb_ref[...] = dx.astype(BF16)
        _acc_rows(i, dg_ref, jnp.sum(dy * xhat, axis=0, keepdims=True))

    row = pl.BlockSpec((tm, nn), lambda i: (i, 0))
    return _call(body, name=name, grid=(m // tm,),
                 in_specs=[pl.BlockSpec((2, j, tm, n), lambda i: (0, 0, i, 0)), _resident(w3),
                           row, pl.BlockSpec(g.shape, lambda i: (0, 0)), row],
                 out_specs=[row, row, pl.BlockSpec((8, nn), lambda i: (0, 0)), pl.BlockSpec((8, LANES), lambda i: (0, 0)),
                            pl.BlockSpec((j, tm, n), lambda i: (0, i, 0))],
                 out_shape=[jax.ShapeDtypeStruct((m, nn), F32), jax.ShapeDtypeStruct((m, nn), BF16),
                            jax.ShapeDtypeStruct((8, nn), F32), jax.ShapeDtypeStruct((8, LANES), F32),
                            jax.ShapeDtypeStruct((j, m, n), BF16)],
                 scratch=[], args=[gu3.reshape(2, j, m, n), w3, add, g, target])


def _ffn_out_bwd(dy, w3, gu3, *, name, tm=1024):
    m, nn = dy.shape
    j, n, _ = w3.shape
    tm = min(tm, m)

    def body(dy_ref, w_ref, gu_ref, dgu_ref):
        da = _bdot(dy_ref[...], w_ref[...], NT)
        gate = gu_ref[0].astype(F32)
        up = gu_ref[1].astype(F32)
        sg = _sigmoid(gate)
        silu = gate * sg
        dgu_ref[0] = (da * up * (sg + silu * (1.0 - sg))).astype(BF16)
        dgu_ref[1] = (da * silu).astype(BF16)

    out = _call(body, name=name, grid=(m // tm, j),
                in_specs=[pl.BlockSpec((tm, nn), lambda i, jj: (i, 0)),
                          pl.BlockSpec((None, n, nn), lambda i, jj: (jj, 0, 0)),
                          pl.BlockSpec((2, None, tm, n), lambda i, jj: (0, jj, i, 0))],
                out_specs=[pl.BlockSpec((2, None, tm, n), lambda i, jj: (0, jj, i, 0))],
                out_shape=[jax.ShapeDtypeStruct((2, j, m, n), BF16)], scratch=[],
                args=[dy, w3, gu3.reshape(2, j, m, n)])[0]
    return out.reshape(2 * j, m, n)


def _rms_fwd_tail(xv, g_ref, h_ref):
    r = lax.rsqrt(jnp.mean(xv * xv, axis=-1, keepdims=True) + NORM_EPS)
    h_ref[...] = (xv * r * g_ref[...]).astype(BF16)


def _rms_bwd_tail(i, dh, x_ref, g_ref, dres_ref, dx_ref, dxb_ref, dg_ref):
    xv = x_ref[...]
    r = lax.rsqrt(jnp.mean(xv * xv, axis=-1, keepdims=True) + NORM_EPS)
    xhat = xv * r
    dxhat = dh * g_ref[...]
    dx = r * (dxhat - xhat * jnp.mean(dxhat * xhat, axis=-1, keepdims=True))
    if dres_ref is not None:
        dx = dx + dres_ref[...]
    dx_ref[...] = dx
    dxb_ref[...] = dx.astype(BF16)
    _acc_rows(i, dg_ref, jnp.sum(dh * xhat, axis=0, keepdims=True))


def _mm_nt_rms(dy, w3, x, g, dres, *, name, dy3=False, w_nn=False, tm=512, plan=None):
    j = w3.shape[0]
    m, kk = x.shape
    n = dy.shape[2] if dy3 else dy.shape[1] // j
    tm = min(tm, m)

    def body(dy_ref, w_ref, x_ref, g_ref, *rest):
        dres_ref = rest[0] if dres is not None else None
        dx_ref, dxb_ref, dg_ref = rest[-3:]
        dh = None
        for jj in range(j):
            piece = dy_ref[jj] if dy3 else dy_ref[:, jj * n:(jj + 1) * n]
            part = _bdot(piece, w_ref[jj], NN if w_nn else NT)
            dh = part if dh is None else dh + part
        _rms_bwd_tail(pl.program_id(0), dh, x_ref, g_ref, dres_ref, dx_ref, dxb_ref, dg_ref)

    row = pl.BlockSpec((tm, kk), lambda i: (i, 0))
    in_specs = [pl.BlockSpec((j, tm, n), lambda i: (0, i, 0)) if dy3 else pl.BlockSpec((tm, j * n), lambda i: (i, 0)),
                _resident(w3), row, pl.BlockSpec(g.shape, lambda i: (0, 0))]
    args = [dy, w3, x, g]
    if dres is not None:
        in_specs.append(row)
        args.append(dres)
    return _call(body, name=name, grid=(m // tm,), in_specs=in_specs,
                 out_specs=[row, row, pl.BlockSpec((8, kk), lambda i: (0, 0))],
                 out_shape=[jax.ShapeDtypeStruct((m, kk), F32), jax.ShapeDtypeStruct((m, kk), BF16),
                            jax.ShapeDtypeStruct((8, kk), F32)], scratch=[], args=args, plan=plan)


def _mix_out(o_a, y_b, proj, w_a, w_b, w, x, g, *, name, tm=512, plan=None):
    s, c = o_a.shape
    d = w.shape[1]
    tm = min(tm, s)

    def body(oa_ref, yb_ref, ga_ref, gb_ref, wa_ref, wb_ref, w_ref, x_ref, g_ref, x1_ref, h_ref, merged_ref, a_ref, b_ref):
        a_ref[...] = _bdot(oa_ref[...], wa_ref[...], NN).astype(BF16)
        b_ref[...] = _bdot(yb_ref[...], wb_ref[...], NN).astype(BF16)
        merged = (_sigmoid(ga_ref[...].astype(F32)) * a_ref[...].astype(F32)
                  + _sigmoid(gb_ref[...].astype(F32)) * b_ref[...].astype(F32)).astype(BF16)
        merged_ref[...] = merged
        xv = _bdot(merged, w_ref[...], NN) + x_ref[...]
        x1_ref[...] = xv
        _rms_fwd_tail(xv, g_ref, h_ref)

    row = pl.BlockSpec((tm, d), lambda i: (i, 0))
    narrow = pl.BlockSpec((tm, c), lambda i: (i, 0))
    whole = lambda arr: pl.BlockSpec(arr.shape, lambda i: (0,) * arr.ndim)
    return _call(body, name=name, grid=(s // tm,),
                 in_specs=[narrow, narrow, pl.BlockSpec((tm, d), lambda i: (i, 3)), pl.BlockSpec((tm, d), lambda i: (i, 4)),
                           whole(w_a), whole(w_b), whole(w), row, whole(g)],
                 out_specs=[row] * 5,
                 out_shape=[jax.ShapeDtypeStruct((s, d), F32)] + [jax.ShapeDtypeStruct((s, d), BF16)] * 4,
                 scratch=[], args=[o_a, y_b, proj, proj, w_a, w_b, w, x, g], plan=plan)


def _mm_tn_a3(a3, dy, *, name):
    j, t, n = a3.shape
    nn = dy.shape[1]
    return _call(functools.partial(_mm_body, TN, False), name=name, grid=(j,),
                 in_specs=[pl.BlockSpec((None, t, n), lambda jj: (jj, 0, 0)), pl.BlockSpec((t, nn), lambda jj: (0, 0))],
                 out_specs=[pl.BlockSpec((None, n, nn), lambda jj: (jj, 0, 0))],
                 out_shape=[jax.ShapeDtypeStruct((j, n, nn), BF16)], scratch=[], args=[a3, dy])[0]


def _mm_nt(dy, w3, *, name, out_dtype=BF16, tm=512, tn=1024, plan=None):
    m = dy.shape[0]
    j, kk, n = w3.shape
    tm, tn = min(tm, m), min(tn, kk)
    return _call(
        functools.partial(_mm_nt_body, j, n), name=name,
        grid=(m // tm, kk // tn),
        in_specs=[pl.BlockSpec((tm, j * n), lambda i, q: (i, 0)),
                  pl.BlockSpec((j, tn, n), lambda i, q: (0, q, 0))],
        out_specs=[pl.BlockSpec((tm, tn), lambda i, q: (i, q))],
        out_shape=[jax.ShapeDtypeStruct((m, kk), out_dtype)], scratch=[], args=[dy, w3], plan=plan)[0]


def _mm_tn(a, dy, n, *, name, out_dtype=BF16, tm=512, tn=None, plan=None):
    t, kk = a.shape
    j = dy.shape[1] // n
    tm, tn = min(tm, kk), n if tn is None else tn
    n_t = n // tn
    return _call(
        functools.partial(_mm_body, TN, False), name=name,
        grid=(kk // tm, j * n_t),
        in_specs=[pl.BlockSpec((t, tm), lambda i, jj: (0, i)),
                  pl.BlockSpec((t, tn), lambda i, jj: (0, jj))],
        out_specs=[pl.BlockSpec((None, tm, tn), lambda i, jj: (jj // n_t, i, jj % n_t))],
        out_shape=[jax.ShapeDtypeStruct((j, kk, n), out_dtype)], scratch=[], args=[a, dy], plan=plan)[0]


def _rows(body, ins, outs, *, n_rows, tm, name, plan=None):
    tm = min(tm, n_rows)
    n_steps = n_rows // tm
    in_specs, args = [], []
    for arr, kind, width, block in ins:
        if kind == "row":
            in_specs.append(pl.BlockSpec((tm, width), functools.partial(lambda i, b: (i, b), b=block)))
        elif kind == "prev":
            in_specs.append(pl.BlockSpec((tm, width), functools.partial(lambda i, b: (jnp.maximum(i - 1, 0), b), b=block)))
        elif kind == "next":
            in_specs.append(pl.BlockSpec((tm, width), functools.partial(lambda i, b: (jnp.minimum(i + 1, n_steps - 1), b), b=block)))
        else:
            in_specs.append(pl.BlockSpec(arr.shape, functools.partial(lambda i, nd: (0,) * nd, nd=arr.ndim)))
        args.append(arr)
    out_specs, out_shape = [], []
    for shape, dtype, kind in outs:
        if kind == "row":
            out_specs.append(pl.BlockSpec((tm, shape[1]), lambda i: (i, 0)))
        else:
            out_specs.append(pl.BlockSpec(shape, functools.partial(lambda i, nd: (0,) * nd, nd=len(shape))))
        out_shape.append(jax.ShapeDtypeStruct(shape, dtype))

    def kern(*refs):
        body(pl.program_id(0), n_steps, *refs)

    return _call(kern, name=name, grid=(n_steps,), in_specs=in_specs, out_specs=out_specs, out_shape=out_shape,
                 scratch=[], args=args, plan=plan)


def _acc_rows(i, ref, value):
    @pl.when(i == 0)
    def _():
        ref[...] = jnp.zeros_like(ref)
    ref[...] += jnp.broadcast_to(value, ref.shape)


def _rms_fwd(x, g, *, name, tm=512):
    s, d = x.shape

    def body(i, n, x_ref, g_ref, h_ref):
        _rms_fwd_tail(x_ref[...], g_ref, h_ref)

    return _rows(body, [(x, "row", d, 0), (g, "full", 0, 0)], [((s, d), BF16, "row")], n_rows=s, tm=tm, name=name)[0]


def _rms_fwd_both(x, g, *, name, tm=512):
    s, d = x.shape
    tm = min(tm, s)

    def body(x_ref, g_ref, h_ref, ht_ref):
        xv = x_ref[...]
        h = xv * lax.rsqrt(jnp.mean(xv * xv, axis=-1, keepdims=True) + NORM_EPS) * g_ref[...]
        h_ref[...] = h.astype(BF16)
        ht_ref[...] = h.T.astype(BF16)

    return _call(body, name=name, grid=(s // tm,),
                 in_specs=[pl.BlockSpec((tm, d), lambda i: (i, 0)), pl.BlockSpec(g.shape, lambda i: (0, 0))],
                 out_specs=[pl.BlockSpec((tm, d), lambda i: (i, 0)), pl.BlockSpec((d, tm), lambda i: (0, i))],
                 out_shape=[jax.ShapeDtypeStruct((s, d), BF16), jax.ShapeDtypeStruct((d, s), BF16)],
                 scratch=[], args=[x, g])


def _mm_rows(at, dy, n, *, first, rows, name, plan=None):
    t = at.shape[1]
    j = dy.shape[1] // n
    return _call(functools.partial(_mm_body, NN, False), name=name, grid=(j,),
                 in_specs=[pl.BlockSpec((rows, t), lambda jj: (first // rows, 0)), pl.BlockSpec((t, n), lambda jj: (0, jj))],
                 out_specs=[pl.BlockSpec((None, rows, n), lambda jj: (jj, 0, 0))],
                 out_shape=[jax.ShapeDtypeStruct((j, rows, n), BF16)], scratch=[], args=[at, dy], plan=plan)[0]


def _rms_bwd(x, g, dh, dres, *, name, tm=512, plan=None):
    s, d = x.shape

    def body(i, n, x_ref, g_ref, dh_ref, dres_ref, dx_ref, dxb_ref, dg_ref):
        _rms_bwd_tail(i, dh_ref[...].astype(F32), x_ref, g_ref, dres_ref, dx_ref, dxb_ref, dg_ref)

    return _rows(body, [(x, "row", d, 0), (g, "full", 0, 0), (dh, "row", d, 0), (dres, "row", d, 0)],
                 [((s, d), F32, "row"), ((s, d), BF16, "row"), ((8, d), F32, "acc")],
                 n_rows=s, tm=tm, name=name, plan=plan)


def _mix_out_bwd(dx1b, w, br_a, br_b, proj, w_a, w_b, *, name, tm=512, plan=None):
    s, d = br_a.shape
    c = w_a.shape[0]
    tm = min(tm, s)

    def body(dy_ref, w_ref, a_ref, b_ref, ga_ref, gb_ref, wa_ref, wb_ref, da_ref, db_ref, dg_ref, doa_ref, dyb_ref):
        dm = _bdot(dy_ref[...], w_ref[...], NT)
        sa = _sigmoid(ga_ref[...].astype(F32))
        sb = _sigmoid(gb_ref[...].astype(F32))
        da_ref[...] = (dm * sa).astype(BF16)
        db_ref[...] = (dm * sb).astype(BF16)
        dg_ref[:, :d] = (dm * a_ref[...].astype(F32) * sa * (1.0 - sa)).astype(BF16)
        dg_ref[:, d:] = (dm * b_ref[...].astype(F32) * sb * (1.0 - sb)).astype(BF16)
        doa_ref[...] = _bdot(da_ref[...], wa_ref[...], NT).astype(BF16)
        dyb_ref[...] = _bdot(db_ref[...], wb_ref[...], NT).astype(BF16)

    row = pl.BlockSpec((tm, d), lambda i: (i, 0))
    narrow = pl.BlockSpec((tm, c), lambda i: (i, 0))
    whole = lambda arr: pl.BlockSpec(arr.shape, lambda i: (0,) * arr.ndim)
    return _call(body, name=name, grid=(s // tm,),
                 in_specs=[row, whole(w), row, row, pl.BlockSpec((tm, d), lambda i: (i, 3)),
                           pl.BlockSpec((tm, d), lambda i: (i, 4)), whole(w_a), whole(w_b)],
                 out_specs=[row, row, pl.BlockSpec((tm, 2 * d), lambda i: (i, 0)), narrow, narrow],
                 out_shape=[jax.ShapeDtypeStruct((s, d), BF16), jax.ShapeDtypeStruct((s, d), BF16),
                            jax.ShapeDtypeStruct((s, 2 * d), BF16), jax.ShapeDtypeStruct((s, c), BF16),
                            jax.ShapeDtypeStruct((s, c), BF16)],
                 scratch=[], args=[dx1b, w, br_a, br_b, proj, proj, w_a, w_b], plan=plan)


def _shift_down(cur, prev, k, first):
    row = lax.broadcasted_iota(jnp.int32, cur.shape, 0)
    out = jnp.where(row >= k, pltpu.roll(cur, k, 0), pltpu.roll(prev, k, 0))
    return jnp.where(jnp.logical_and(first, row < k), 0.0, out)


def _shift_up(cur, nxt, k, last):
    tm = cur.shape[0]
    row = lax.broadcasted_iota(jnp.int32, cur.shape, 0)
    out = jnp.where(row < tm - k, pltpu.roll(cur, tm - k, 0), pltpu.roll(nxt, tm - k, 0))
    return jnp.where(jnp.logical_and(last, row >= tm - k), 0.0, out)


def _conv_fwd(proj, conv_w, *, name, tm=512):
    s = proj.shape[0]
    c = CONV_WIDTH

    def body(i, n, u_ref, gb_ref, gc_ref, up_ref, gcp_ref, w_ref, y_ref):
        cu = gc_ref[...].astype(F32) * u_ref[...].astype(F32)
        cup = gcp_ref[...].astype(F32) * up_ref[...].astype(F32)
        first = i == 0
        y = (w_ref[0:1, :] * _shift_down(cu, cup, 2, first) + w_ref[1:2, :] * _shift_down(cu, cup, 1, first)
             + w_ref[2:3, :] * cu)
        y_ref[...] = (gb_ref[...].astype(F32) * y).astype(BF16)

    return _rows(body, [(proj, "row", c, 3), (proj, "row", c, 4), (proj, "row", c, 5),
                        (proj, "prev", c, 3), (proj, "prev", c, 5), (conv_w, "full", 0, 0)],
                 [((s, c), BF16, "row")], n_rows=s, tm=tm, name=name)[0]


def _conv_bwd(dy_b, proj, conv_w, *, name, tm=512, plan=None):
    s = proj.shape[0]
    c = CONV_WIDTH

    def body(i, n, dy_ref, u_ref, gb_ref, gc_ref, up_ref, gcp_ref, dyn_ref, gbn_ref, w_ref, d_ref, dw_ref):
        first, last = i == 0, i == n - 1
        u = u_ref[...].astype(F32)
        gb = gb_ref[...].astype(F32)
        gc = gc_ref[...].astype(F32)
        cu = gc * u
        cup = gcp_ref[...].astype(F32) * up_ref[...].astype(F32)
        cu1 = _shift_down(cu, cup, 1, first)
        cu2 = _shift_down(cu, cup, 2, first)
        conv = w_ref[0:1, :] * cu2 + w_ref[1:2, :] * cu1 + w_ref[2:3, :] * cu
        dy = dy_ref[...].astype(F32)
        dyc = dy * gb
        dycn = dyn_ref[...].astype(F32) * gbn_ref[...].astype(F32)
        dcu = (w_ref[2:3, :] * dyc + w_ref[1:2, :] * _shift_up(dyc, dycn, 1, last)
               + w_ref[0:1, :] * _shift_up(dyc, dycn, 2, last))
        d_ref[:, 0:c] = (dcu * gc).astype(BF16)
        d_ref[:, c:2 * c] = (dy * conv).astype(BF16)
        d_ref[:, 2 * c:3 * c] = (dcu * u).astype(BF16)
        row = lax.broadcasted_iota(jnp.int32, (8, c), 0)
        dw = (jnp.where(row == 0, jnp.sum(dyc * cu2, axis=0, keepdims=True), 0.0)
              + jnp.where(row == 1, jnp.sum(dyc * cu1, axis=0, keepdims=True), 0.0)
              + jnp.where(row == 2, jnp.sum(dyc * cu, axis=0, keepdims=True), 0.0))

        @pl.when(first)
        def _():
            dw_ref[...] = jnp.zeros_like(dw_ref)
        dw_ref[...] += dw

    return _rows(body, [(dy_b, "row", c, 0), (proj, "row", c, 3), (proj, "row", c, 4), (proj, "row", c, 5),
                        (proj, "prev", c, 3), (proj, "prev", c, 5), (dy_b, "next", c, 0), (proj, "next", c, 4),
                        (conv_w, "full", 0, 0)],
                 [((s, 3 * c), BF16, "row"), ((8, c), F32, "acc")], n_rows=s, tm=tm, name=name, plan=plan)


def _mem_probs(q, k, scale):
    sc = _bdot(q, k, NT) * scale
    sc = sc - jnp.max(sc, axis=-1, keepdims=True)
    p = jnp.exp(sc)
    return p / jnp.sum(p, axis=-1, keepdims=True)


def _mem_sublayer(hq, w_q, kv, w_o, x, g, *, name, tm=512, plan=None):
    s, d = hq.shape
    hd = d // MEM_HEADS
    scale = 1.0 / math.sqrt(hd)
    tm = min(tm, s)

    def body(hq_ref, wq_ref, kv_ref, wo_ref, x_ref, g_ref, q_ref, o_ref, x2_ref, h_ref):
        q_ref[...] = _bdot(hq_ref[...], wq_ref[...], NN).astype(BF16)
        for h in range(MEM_HEADS):
            cols = slice(h * hd, (h + 1) * hd)
            p = _mem_probs(q_ref[:, cols], kv_ref[:, cols], scale)
            o_ref[:, cols] = _bdot(p, kv_ref[:, d + h * hd:d + (h + 1) * hd], NN).astype(BF16)
        xv = _bdot(o_ref[...], wo_ref[...], NN) + x_ref[...]
        x2_ref[...] = xv
        _rms_fwd_tail(xv, g_ref, h_ref)

    row = pl.BlockSpec((tm, d), lambda i: (i, 0))
    whole = lambda a: pl.BlockSpec(a.shape, lambda i: (0,) * a.ndim)
    return _call(body, name=name, grid=(s // tm,),
                 in_specs=[row, whole(w_q), whole(kv), whole(w_o), row, whole(g)], out_specs=[row] * 4,
                 out_shape=[jax.ShapeDtypeStruct((s, d), BF16), jax.ShapeDtypeStruct((s, d), BF16),
                            jax.ShapeDtypeStruct((s, d), F32), jax.ShapeDtypeStruct((s, d), BF16)],
                 scratch=[], args=[hq, w_q, kv, w_o, x, g], plan=plan)


def _mem_sublayer_bwd(dx2b, dx2, x, g, qm, kv, w_q, w_o, *, name, tm=512, plan=None):
    s, d = qm.shape
    hd = d // MEM_HEADS
    scale = 1.0 / math.sqrt(hd)
    tm = min(tm, s)

    def body(dyb_ref, dres_ref, x_ref, g_ref, q_ref, kv_ref, wq_ref, wo_ref, dx_ref, dxb_ref, dg_ref, dq_ref, dkv_ref):
        i = pl.program_id(0)

        @pl.when(i == 0)
        def _():
            dkv_ref[...] = jnp.zeros_like(dkv_ref)
        dom = _bdot(dyb_ref[...], wo_ref[...], NT).astype(BF16)
        for h in range(MEM_HEADS):
            cols = slice(h * hd, (h + 1) * hd)
            vcols = slice(d + h * hd, d + (h + 1) * hd)
            q, k, v, do = q_ref[:, cols], kv_ref[:, cols], kv_ref[:, vcols], dom[:, cols]
            p = _mem_probs(q, k, scale)
            dp = _bdot(do, v, NT)
            ds = p * (dp - jnp.sum(dp * p, axis=-1, keepdims=True)) * scale
            dq_ref[:, cols] = _bdot(ds, k, NN).astype(BF16)
            dkv_ref[:, cols] += _bdot(ds, q, TN)
            dkv_ref[:, vcols] += _bdot(p, do, TN)
        dh = _bdot(dq_ref[...], wq_ref[...], NT)
        _rms_bwd_tail(i, dh, x_ref, g_ref, dres_ref, dx_ref, dxb_ref, dg_ref)

    row = pl.BlockSpec((tm, d), lambda i: (i, 0))
    whole = lambda a: pl.BlockSpec(a.shape, lambda i: (0,) * a.ndim)
    return _call(body, name=name, grid=(s // tm,),
                 in_specs=[row, row, row, whole(g), row, whole(kv), whole(w_q), whole(w_o)],
                 out_specs=[row, row, pl.BlockSpec((8, d), lambda i: (0, 0)), row, whole(kv)],
                 out_shape=[jax.ShapeDtypeStruct((s, d), F32), jax.ShapeDtypeStruct((s, d), BF16),
                            jax.ShapeDtypeStruct((8, d), F32), jax.ShapeDtypeStruct((s, d), BF16),
                            jax.ShapeDtypeStruct(kv.shape, F32)],
                 scratch=[], args=[dx2b, dx2, x, g, qm, kv, w_q, w_o], plan=plan)


def _sb_consts(t):
    row = lax.broadcasted_iota(jnp.int32, (t, t), 0)
    col = lax.broadcasted_iota(jnp.int32, (t, t), 1)
    lane = lax.broadcasted_iota(jnp.int32, (t, LANES), 1)
    return row, col, lane < SB_HEAD_DIM


def _sb_logits(q, k):
    z2 = jnp.minimum(_bdot(q, k, NT) * LOG2_E, SB_CLAMP)
    return z2, jnp.exp2(z2)


def _tri_sum(v, tri):
    hi = v.astype(BF16)
    lo = (v - hi.astype(F32)).astype(BF16)
    return _bdot(hi, tri, NN) + _bdot(lo, tri, NN)


def _sb_fwd(proj, *, name, plan=None):
    s = proj.shape[0]
    t, nh = SB_TILE, SB_STEP_HEADS
    n_q = s // t
    scale = 1.0 / math.sqrt(SB_HEAD_DIM)

    def body(q_ref, k_ref, v_ref, o_ref, c_ref, first_ref, acc_ref, c_scr):
        i = pl.program_id(1)
        row, col, head0 = _sb_consts(t)
        later = (row > col).astype(BF16)
        valid = col < row
        lanes = lambda h: slice((h // 2) * LANES, (h // 2 + 1) * LANES)
        q = [jnp.where(head0 == (h % 2 == 0), q_ref[:, lanes(h)] * scale, 0) for h in range(nh)]

        def tiles(kbs, diag_first, carry):
            rows = [pl.ds(pl.multiple_of(kb * t, t), t) for kb in kbs]
            jobs = [(n, h) for n in range(len(kbs)) for h in range(nh)]
            masked = lambda n: diag_first and n == 0
            zs = {(n, h): _sb_logits(q[h], k_ref[rows[n], lanes(h)]) for n, h in jobs}
            fail = {j: jnp.log2(1.0 + zs[j][1]) for j in jobs}
            fail = {j: jnp.where(valid, fail[j], 0.0) if masked(j[0]) else fail[j] for j in jobs}
            cum = {j: _tri_sum(fail[j], later) for j in jobs}
            run, before = list(carry), {}
            for n, h in jobs:
                before[n, h] = run[h]
                run[h] = run[h] + cum[n, h][:, 0:1] + fail[n, h][:, 0:1]
            w = {j: jnp.exp2(zs[j][0] - fail[j] - cum[j] - before[j]) for j in jobs}
            w = {j: jnp.where(valid, w[j], 0.0) if masked(j[0]) else w[j] for j in jobs}
            for n, h in jobs:
                acc_ref[h] += _bdot(w[n, h], v_ref[rows[n], lanes(h)], NN)
            return tuple(run)

        acc_ref[...] = jnp.zeros_like(acc_ref)
        zero = (jnp.zeros((t, 1), F32),) * nh

        def alive(carry):
            return (functools.reduce(jnp.minimum, [jnp.min(c) for c in carry]) < SB_DEAD).astype(jnp.int32)

        def step(state):
            new = tiles([state[0]], False, state[2:])
            return (state[0] - 1, alive(new)) + new

        @pl.when(i == 0)
        def _():
            for h, c in enumerate(tiles([i], True, zero)):
                c_scr[h] = c

        @pl.when(i > 0)
        def _():
            for h, c in enumerate(tiles([i, i - 1], True, zero)):
                c_scr[h] = c
        carry = tuple(c_scr[h] for h in range(nh))
        state = lax.while_loop(lambda st: jnp.logical_and(st[0] >= 0, st[1] > 0), step, (i - 2, alive(carry)) + carry)
        for b in range(nh // 2):
            o_ref[:, b * LANES:(b + 1) * LANES] = jnp.where(head0, acc_ref[2 * b], acc_ref[2 * b + 1]).astype(BF16)
        head = lax.broadcasted_iota(jnp.int32, (t, nh), 1)
        c_ref[...] = sum(jnp.where(head == h, state[2 + h], 0.0) for h in range(nh))
        first_ref[pl.program_id(0), i] = (jnp.maximum(state[0], -1) + 1).astype(F32)

    n_p, width = SB_HEADS // nh, nh * SB_HEAD_DIM
    k_blk, v_blk = SB_WIDTH // width, 2 * SB_WIDTH // width
    return _call(
        body, name=name, grid=(n_p, n_q),
        in_specs=[pl.BlockSpec((t, width), lambda p, i: (i, p)),
                  pl.BlockSpec((s, width), lambda p, i: (0, k_blk + p)),
                  pl.BlockSpec((s, width), lambda p, i: (0, v_blk + p))],
        out_specs=[pl.BlockSpec((t, width), lambda p, i: (i, p)),
                   pl.BlockSpec((None, t, nh), lambda p, i: (p, i, 0)),
                   pl.BlockSpec(memory_space=pltpu.SMEM)],
        out_shape=[jax.ShapeDtypeStruct((s, SB_WIDTH), BF16), jax.ShapeDtypeStruct((n_p, s, nh), F32),
                   jax.ShapeDtypeStruct((n_p, n_q), F32)],
        scratch=[pltpu.VMEM((nh, t, LANES), F32), pltpu.VMEM((nh, t, 1), F32)], args=[proj, proj, proj], plan=plan)


def _sb_bwd(proj, do_a, ctot, first, *, name, plan=None):
    s = proj.shape[0]
    t, nh = SB_TILE, SB_STEP_HEADS
    n_q = s // t
    scale = 1.0 / math.sqrt(SB_HEAD_DIM)

    def body(q_ref, k_ref, v_ref, do_ref, c_ref, first_ref, dq_ref, dk_ref, dv_ref, dq_acc, dk_acc, dv_acc):
        i = pl.program_id(1)
        kb0 = jnp.clip(first_ref[pl.program_id(0), i].astype(jnp.int32), 0, i)
        row, col, head0 = _sb_consts(t)
        upto = (row <= col).astype(BF16)
        before = (row < col).astype(BF16)
        valid = col < row
        lanes = lambda h: slice((h // 2) * LANES, (h // 2 + 1) * LANES)
        q2 = [jnp.where(head0 == (h % 2 == 0), q_ref[:, lanes(h)] * scale, 0) for h in range(nh)]
        do2 = [jnp.where(head0 == (h % 2 == 0), do_ref[:, lanes(h)], 0) for h in range(nh)]
        ctot2 = [c_ref[:, h:h + 1] for h in range(nh)]

        @pl.when(i == 0)
        def _():
            dk_acc[...] = jnp.zeros_like(dk_acc)
            dv_acc[...] = jnp.zeros_like(dv_acc)
        dq_acc[...] = jnp.zeros_like(dq_acc)

        def tiles(kbs, diag_last, carry):
            rows = [pl.ds(pl.multiple_of(kb * t, t), t) for kb in kbs]
            kt = {(n, h): k_ref[rows[n], lanes(h)] for n in range(len(kbs)) for h in range(nh)}
            jobs = list(kt)
            masked = lambda n: diag_last and n == len(kbs) - 1
            t_last = slice(t - 1, t)
            zs = {(n, h): _sb_logits(q2[h], kt[n, h]) for n, h in jobs}
            dw = {(n, h): _bdot(do2[h], v_ref[rows[n], lanes(h)], NT) for n, h in jobs}
            fail = {j: jnp.log2(1.0 + zs[j][1]) for j in jobs}
            fail = {j: jnp.where(valid, fail[j], 0.0) if masked(j[0]) else fail[j] for j in jobs}
            cum = {j: _tri_sum(fail[j], upto) for j in jobs}
            miss = {j: jnp.exp2(-fail[j]) for j in jobs}
            beta = {j: zs[j][1] * miss[j] for j in jobs}
            fail_run, fail_before = list(carry[0::2]), {}
            for n, h in jobs:
                fail_before[n, h] = fail_run[h]
                fail_run[h] = fail_run[h] + cum[n, h][:, t_last]
            w = {(n, h): beta[n, h] * jnp.exp2(fail_before[n, h] + cum[n, h] - ctot2[h]) for n, h in jobs}
            w = {j: jnp.where(valid, w[j], 0.0) if masked(j[0]) else w[j] for j in jobs}
            g = {j: w[j] * dw[j] for j in jobs}
            g_local = {j: _bdot(g[j], before, NN) for j in jobs}
            for n, h in jobs:
                dv_acc[rows[n], lanes(h)] += _bdot(w[n, h], do2[h], TN)
            g_run, dz = list(carry[1::2]), {}
            for n, h in jobs:
                g_sum = g_run[h] + g_local[n, h]
                dz[n, h] = g[n, h] * miss[n, h] - beta[n, h] * g_sum
                g_run[h] = g_sum[:, t_last] + g[n, h][:, t_last]
            dz = {j: jnp.where(valid, dz[j], 0.0) if masked(j[0]) else dz[j] for j in jobs}
            for n, h in jobs:
                dq_acc[h] += _bdot(dz[n, h], kt[n, h], NN)
                dk_acc[rows[n], lanes(h)] += _bdot(dz[n, h], q2[h], TN)
            return tuple(v for pair in zip(fail_run, g_run) for v in pair)

        zero = jnp.zeros((t, 1), F32)
        carry = lax.fori_loop(kb0, i - 1, lambda n, c: tiles([n], False, c), (zero,) * (2 * nh))

        @pl.when(i == 0)
        def _():
            tiles([i], True, carry)

        @pl.when(i > 0)
        def _():
            tiles([i - 1, i], True, carry)
        for b in range(nh // 2):
            dq_ref[:, b * LANES:(b + 1) * LANES] = (jnp.where(head0, dq_acc[2 * b], dq_acc[2 * b + 1])
                                                    * scale).astype(BF16)

        @pl.when(i == n_q - 1)
        def _():
            dk_ref[...] = dk_acc[...].astype(BF16)
            dv_ref[...] = dv_acc[...].astype(BF16)

    n_p, width = SB_HEADS // nh, nh * SB_HEAD_DIM
    k_blk, v_blk = SB_WIDTH // width, 2 * SB_WIDTH // width
    outs = _call(
        body, name=name, grid=(n_p, n_q),
        in_specs=[pl.BlockSpec((t, width), lambda p, i: (i, p)),
                  pl.BlockSpec((s, width), lambda p, i: (0, k_blk + p)),
                  pl.BlockSpec((s, width), lambda p, i: (0, v_blk + p)),
                  pl.BlockSpec((t, width), lambda p, i: (i, p)),
                  pl.BlockSpec((None, t, nh), lambda p, i: (p, i, 0)),
                  pl.BlockSpec(memory_space=pltpu.SMEM)],
        out_specs=[pl.BlockSpec((t, width), lambda p, i: (i, p)),
                   pl.BlockSpec((s, width), lambda p, i: (0, p)),
                   pl.BlockSpec((s, width), lambda p, i: (0, p))],
        out_shape=[jax.ShapeDtypeStruct((s, SB_WIDTH), BF16)] * 3,
        scratch=[pltpu.VMEM((nh, t, LANES), F32), pltpu.VMEM((s, width), F32), pltpu.VMEM((s, width), F32)],
        args=[proj, proj, proj, do_a, ctot, first], plan=plan)
    return jnp.concatenate(outs, axis=1)


def _mm_gathered(a, key, plan, *, name, out3=False, w_t=False):
    src = plan.gathering(key)
    if src is None:
        return _mm_nn(a, plan.weight(key), name=name, out3=out3, w_t=w_t, plan=plan)
    out, w_all = _mm_gathering(a, src, name=name, out3=out3, w_t=w_t)
    plan.set_weight(key, w_all)
    return out


def _local_step(x, mem, target, gains, plan):
    g_mix, g_memq, g_memkv, g_ffn, g_fin = gains
    d = x.shape[1]

    h0, h0_t = _rms_fwd_both(x, g_mix, name="rms_mix")
    proj = _mm_gathered(h0, "in", plan, name="mm_in")
    w_in = plan.weight("in")
    o_a, ctot, first = _sb_fwd(proj, name="sb_fwd", plan=plan)
    conv_w = plan.weight("conv")
    y_b = _conv_fwd(proj, conv_w, name="conv_fwd")
    w_a, w_b, w_mix = plan.weight("a"), plan.weight("b"), plan.weight("mix")
    x1, hq, merged, br_a, br_b = _mix_out(o_a, y_b, proj, w_a[0], w_b[0], w_mix[0], x, g_memq, name="mm_mix", plan=plan)
    w_mq, w_kv, w_mo = plan.weight("mq")[0], plan.weight("kv"), plan.weight("mo")[0]
    mn = _rms_fwd(mem, g_memkv, name="rms_memkv")
    kv = _mm_nn(mn, w_kv, name="mm_memkv")
    qm, om, x2, hf = _mem_sublayer(hq, w_mq, kv, w_mo, x1, g_ffn, name="mem_sublayer", plan=plan)
    gu = _mm_gathered(hf, "fi", plan, name="mm_ffn_in", out3=True, w_t=True)
    w_fi, w_fo = plan.weight("fi"), plan.weight("fo")
    dx3, dx3b, dg_fin, loss, act = _ffn_out_loss(gu, w_fo, x2, g_fin, target, name="mm_ffn_out")

    plan.grad("fo", _mm_tn_a3(act, dx3b, name="mm_d_w_ffn_out"))
    dgu = _ffn_out_bwd(dx3b, w_fo, gu, name="mm_d_act")
    plan.grad("fi", _mm_tn_a3(dgu, hf, name="mm_d_w_ffn_in"))
    dx2, dx2b, dg_ffn = _mm_nt_rms(dgu, w_fi, x2, g_ffn, dx3, name="mm_d_hf", dy3=True, w_nn=True, plan=plan)

    plan.grad("mo", _mm_tn(om, dx2b, d, name="mm_d_w_memo"))
    dx1, dx1b, dg_memq, dqm, dkv = _mem_sublayer_bwd(dx2b, dx2, x1, g_memq, qm, kv, w_mq, w_mo, name="mem_sublayer_bwd",
                                                    plan=plan)
    plan.grad("mq", _mm_tn(hq, dqm, d, name="mm_d_w_memq"))
    plan.grad("kv", _mm_tn(mn, dkv, w_kv.shape[2], name="mm_d_w_memkv"))
    _, _, dg_memkv = _mm_nt_rms(dkv, w_kv, mem, g_memkv, None, name="mm_d_mn")

    plan.grad("mix", _mm_tn(merged, dx1b, d, name="mm_d_w_mix"))
    dbr_a, dbr_b, dgab, do_a, dy_b = _mix_out_bwd(dx1b, w_mix[0], br_a, br_b, proj, w_a[0], w_b[0], name="mm_d_merged",
                                                 plan=plan)
    plan.grad("a", _mm_tn(o_a, dbr_a, d, name="mm_d_w_branch_a"))
    plan.grad("b", _mm_tn(y_b, dbr_b, d, name="mm_d_w_branch_b"))
    dconv, dconv_w = _conv_bwd(dy_b, proj, conv_w, name="conv_bwd", plan=plan)
    dqkv = _sb_bwd(proj, do_a, ctot, first, name="sb_bwd", plan=plan)
    dproj = jnp.concatenate([dqkv, dconv, dgab], axis=1)
    rows_in1 = d // IN_SPLIT[1] * (IN_SPLIT[1] - IN_SPLIT[0])
    plan.grad("in0", _mm_rows(h0_t, dproj, w_in.shape[2], first=0, rows=d - rows_in1, name="mm_d_w_in0"))
    plan.grad("in1", _mm_rows(h0_t, dproj, w_in.shape[2], first=d - rows_in1, rows=rows_in1, name="mm_d_w_in1",
                              plan=plan))
    dh0 = _mm_nt(dproj, w_in, name="mm_d_h0", out_dtype=F32, plan=plan)
    dx0, _, dg_mix = _rms_bwd(x, g_mix, dh0, dx1, name="rms_mix_bwd", plan=plan)

    return dx0, (dg_mix, dg_memq, dg_memkv, dg_ffn, dg_fin, dconv_w, loss)


def _row_tile(a, target=512):
    tm = min(a, target)
    while a % tm:
        tm -= 8
    return tm


def _sum_with_sibling(parts, recvs, core, *, name):
    n = len(parts)

    def body(core_ref, *refs):
        for p_ref, r_ref, o_ref in zip(refs[:n], refs[n:2 * n], refs[2 * n:]):
            o_ref[...] = (p_ref[...].astype(F32) + r_ref[...].astype(F32)).astype(o_ref.dtype)

    mine = [pl.BlockSpec((None,) + p.shape[1:], lambda q, core_ref: (2 * q + core_ref[0], 0, 0)) for p in parts]
    other = [pl.BlockSpec((None,) + p.shape[1:], lambda q, core_ref: (q, 0, 0)) for p in parts]
    return pl.pallas_call(
        body, name=name,
        grid_spec=pltpu.PrefetchScalarGridSpec(num_scalar_prefetch=1, grid=(N_CHIP,), in_specs=mine + other,
                                               out_specs=other),
        out_shape=[jax.ShapeDtypeStruct((N_CHIP,) + p.shape[1:], p.dtype) for p in parts],
        compiler_params=_params(1))(core, *parts, *recvs)


def _adam_math(wv, g, m, v):
    m = ADAM_B1 * m + (1.0 - ADAM_B1) * g
    v = ADAM_B2 * v + (1.0 - ADAM_B2) * (g * g)
    m_hat = m / (1.0 - ADAM_B1 ** ADAM_STEP)
    v_hat = v / (1.0 - ADAM_B2 ** ADAM_STEP)
    delta = -ADAM_LR * (m_hat / (jnp.sqrt(v_hat) + ADAM_EPS) + ADAM_WD * wv)
    return delta, m, v


def _adam_sharded(wv, m, v, own, recv, chip, *, name):
    a, b = wv.shape
    tm = _row_tile(a)

    def body(chip_ref, w_ref, m_ref, v_ref, own_ref, recv_ref, g_ref, d_ref, nm_ref, nv_ref):
        g = own_ref[...].astype(F32)
        for j in range(3):
            g = g + recv_ref[j].astype(F32)
        delta, nm, nv = _adam_math(w_ref[...], g, m_ref[...], v_ref[...])
        g_ref[...] = g
        d_ref[...] = delta
        nm_ref[...] = nm
        nv_ref[...] = nv

    tile = pl.BlockSpec((tm, b), lambda i, chip_ref: (i, 0))
    return pl.pallas_call(
        body, name=name,
        grid_spec=pltpu.PrefetchScalarGridSpec(
            num_scalar_prefetch=1, grid=(a // tm,),
            in_specs=[tile, tile, tile,
                      pl.BlockSpec((None, tm, b), lambda i, chip_ref: (chip_ref[0], i, 0)),
                      pl.BlockSpec((3, tm, b), lambda i, chip_ref: (0, i, 0))],
            out_specs=[tile] * 4),
        out_shape=[jax.ShapeDtypeStruct((a, b), F32)] * 4, compiler_params=_params(1))(chip, wv, m, v, own, recv)


def _sum_devices(gathered, *, name):
    _, r, c = gathered.shape

    def body(g_ref, o_ref):
        total = g_ref[0]
        for j in range(1, N_DEV):
            total = total + g_ref[j]
        o_ref[...] = total

    return pl.pallas_call(body, name=name, out_shape=jax.ShapeDtypeStruct((r, c), F32))(gathered)


def _adam_small(wv, g, m, v, *, name):
    def body(w_ref, g_ref, m_ref, v_ref, d_ref, nm_ref, nv_ref):
        delta, nm, nv = _adam_math(w_ref[...], g_ref[...], m_ref[...], v_ref[...])
        d_ref[...] = delta
        nm_ref[...] = nm
        nv_ref[...] = nv

    return pl.pallas_call(body, name=name, out_shape=[jax.ShapeDtypeStruct(wv.shape, F32)] * 3)(wv, g, m, v)


BIG = ("in", "a", "b", "mix", "mq", "kv", "mo", "fi", "fo")
ROW_SHARDED = ("mix", "mq", "mo")
UNSHARDED = ("a", "b")
FFN_GROUPS = 4
IN_SPLIT = (3, 4)
SMALL_ROWS = 16


class _Plan:
    FUSED = ("in",)
    GATHER_ON = {"sb_fwd": ("a", "b", "mix", "mq", "mo", "conv", "fi0"), "mm_mix": ("kv",), "mem_sublayer": ("fi1",),
                 "mm_ffn_in": ("fo",)}
    SIBLING_ON = {"mm_d_hf": ("fo", "fi"), "mm_d_merged": ("mo", "mq", "kv"), "conv_bwd": ("mix", "a", "b"),
                  "mm_d_w_in1": ("in0",), "mm_d_h0": ("in1",)}
    CHIPS_ON = {"mem_sublayer_bwd": ("fo",), "sb_bwd": ("fi", "mo", "mq", "kv", "mix", "a", "b"), "mm_d_h0": ("in0",),
                "rms_mix_bwd": ("in1",)}

    def __init__(self, shards, core):
        self.shards, self.core = shards, core
        self.w, self.parts, self.chip_sums, self.from_chips = {}, {}, {}, {}

    def gathering(self, k):
        return self.shards[k] if k in self.FUSED else None

    def comm(self, name):
        comms = []
        if name in self.GATHER_ON:
            comms.append(_gather_comm([self.shards[k] for k in self.GATHER_ON[name]]))
        if name in self.SIBLING_ON:
            comms.append(_sibling_comm([self.parts[k] for k in self.SIBLING_ON[name]]))
        if name in self.CHIPS_ON:
            comms.append(_chips_comm([self.chip_sums[k] for k in self.CHIPS_ON[name]]))
        return _join_comms(comms) if comms else None

    def landed(self, name, outs):
        outs = list(outs)
        for k in self.GATHER_ON.get(name, ()):
            self.set_weight(k, outs.pop(0))
        keys = self.SIBLING_ON.get(name, ())
        if keys:
            sums = _sum_with_sibling([self.parts[k] for k in keys], [outs.pop(0) for _ in keys], self.core,
                                     name="sum_with_sibling_" + "_".join(keys))
            self.chip_sums.update(zip(keys, sums))
        for k in self.CHIPS_ON.get(name, ()):
            self.from_chips[k] = outs.pop(0)

    def set_weight(self, k, gathered):
        _, a, b = gathered.shape
        if k in ROW_SHARDED:
            gathered = gathered.reshape(1, N_DEV * a, b)
        elif k in UNSHARDED:
            gathered = jnp.transpose(gathered, (1, 0, 2)).reshape(1, a, N_DEV * b)
        elif k == "fo":
            gathered = gathered.reshape(FFN_GROUPS, N_DEV * a // FFN_GROUPS, b)
        elif k == "conv":
            n_conv = CONV_WIDTH // N_DEV
            gathered = jnp.transpose(gathered[:, :3, :n_conv], (1, 0, 2)).reshape(3, CONV_WIDTH)
        self.w[k] = gathered
        if k == "fi1":
            self.w["fi"] = jnp.concatenate([self.w["fi0"], gathered], axis=2)

    def weight(self, k):
        return self.w[k]

    def grad(self, k, g):
        _, a, b = g.shape
        if k in ROW_SHARDED:
            g = g.reshape(N_DEV, a // N_DEV, b)
        elif k in UNSHARDED:
            g = jnp.transpose(g.reshape(a, N_DEV, b // N_DEV), (1, 0, 2))
        elif k == "fo":
            g = g.reshape(N_DEV, FFN_GROUPS * a // N_DEV, b)
        self.parts[k] = g


def kernel(x, mem, norm_mix, w_in, conv_w, w_branch_a, w_branch_b, w_mix_out, norm_mem_q, norm_mem_kv, w_mem_q, w_mem_kv, w_mem_o, norm_ffn, w_ffn_in, w_ffn_out, norm_final, loss_target, m_norm_mix, m_w_in, m_conv_w, m_w_branch_a, m_w_branch_b, m_w_mix_out, m_norm_mem_q, m_norm_mem_kv, m_w_mem_q, m_w_mem_kv, m_w_mem_o, m_norm_ffn, m_w_ffn_in, m_w_ffn_out, m_norm_final, v_norm_mix, v_w_in, v_conv_w, v_w_branch_a, v_w_branch_b, v_w_mix_out, v_norm_mem_q, v_norm_mem_kv, v_w_mem_q, v_w_mem_kv, v_w_mem_o, v_norm_ffn, v_w_ffn_in, v_w_ffn_out, v_norm_final):
    d = x.shape[-1]
    xi, yi, ci = lax.axis_index("x"), lax.axis_index("y"), lax.axis_index("c")
    chip = jnp.reshape(2 * xi + yi, (1,)).astype(jnp.int32)
    dev = 4 * xi + 2 * yi + ci

    big_w = dict(zip(BIG, (w_in, w_branch_a, w_branch_b, w_mix_out, w_mem_q, w_mem_kv, w_mem_o, w_ffn_in, w_ffn_out)))
    big_m = dict(zip(BIG, (m_w_in, m_w_branch_a, m_w_branch_b, m_w_mix_out, m_w_mem_q, m_w_mem_kv, m_w_mem_o, m_w_ffn_in, m_w_ffn_out)))
    big_v = dict(zip(BIG, (v_w_in, v_w_branch_a, v_w_branch_b, v_w_mix_out, v_w_mem_q, v_w_mem_kv, v_w_mem_o, v_w_ffn_in, v_w_ffn_out)))

    flip = lambda t, k: jnp.transpose(t) if k == "fi" else t
    shards = {k: flip(big_w[k][0], k).astype(BF16) for k in BIG}
    shards["fi0"], shards["fi1"] = shards["fi"][:, :d // 2], shards["fi"][:, d // 2:]
    n_conv = conv_w.shape[-1]
    shards["conv"] = jnp.zeros((8, LANES), F32).at[:3, :n_conv].set(conv_w[0])
    plan = _Plan(shards, jnp.reshape(ci, (1,)).astype(jnp.int32))

    gains = (norm_mix, norm_mem_q, norm_mem_kv, norm_ffn, norm_final.reshape(1, d))
    dx0, small = _local_step(x[0], mem[0], loss_target[0], gains, plan)

    grads, deltas, new_m, new_v = {}, {}, {}, {}
    for k in BIG:
        lead = big_w[k].shape
        wv, mv, vv = flip(big_w[k][0], k), flip(big_m[k][0], k), flip(big_v[k][0], k)
        if k == "in":
            half = wv.shape[0] * IN_SPLIT[0] // IN_SPLIT[1]
            lo = _adam_sharded(wv[:half], mv[:half], vv[:half], plan.chip_sums["in0"], plan.from_chips["in0"], chip,
                               name="adam_in0")
            hi = _adam_sharded(wv[half:], mv[half:], vv[half:], plan.chip_sums["in1"], plan.from_chips["in1"], chip,
                               name="adam_in1")
            outs = [jnp.concatenate(pair, axis=0) for pair in zip(lo, hi)]
        else:
            outs = _adam_sharded(wv, mv, vv, plan.chip_sums[k], plan.from_chips[k], chip, name="adam_" + k)
        grads[k], deltas[k], new_m[k], new_v[k] = (flip(t, k).reshape(lead) for t in outs)

    dg_mix, dg_memq, dg_memkv, dg_ffn, dg_fin, dconv_w, loss = small
    conv_rows = jnp.zeros((3, d), F32).at[:, :CONV_WIDTH].set(dconv_w[:3])
    block = jnp.concatenate([dg_mix[:1], dg_memq[:1], dg_memkv[:1], dg_ffn[:1], dg_fin[:1], conv_rows,
                             jnp.broadcast_to(loss[:1, :1], (1, d)), jnp.zeros((SMALL_ROWS - 9, d), F32)], axis=0)
    total = _sum_devices(_exchange(_gather_comm([block]), name="gather_small")[0], name="sum_small")
    g_conv = lax.dynamic_slice(total[5:8, :CONV_WIDTH], (0, dev * n_conv), (3, n_conv))
    small_w = [norm_mix, norm_mem_q, norm_mem_kv, norm_ffn, norm_final.reshape(1, d), conv_w[0]]
    small_m = [m_norm_mix, m_norm_mem_q, m_norm_mem_kv, m_norm_ffn, m_norm_final.reshape(1, d), m_conv_w[0]]
    small_v = [v_norm_mix, v_norm_mem_q, v_norm_mem_kv, v_norm_ffn, v_norm_final.reshape(1, d), v_conv_w[0]]
    small_g = [total[0:1], total[1:2], total[2:3], total[3:4], total[4:5], g_conv]
    small_names = ["norm_mix", "norm_mem_q", "norm_mem_kv", "norm_ffn", "norm_final", "conv_w"]
    sg, sd, sm, sv = {}, {}, {}, {}
    for nme, wv, g, m, v in zip(small_names, small_w, small_g, small_m, small_v):
        dl, nm, nv = _adam_small(wv, g, m, v, name="adam_" + nme)
        shape = norm_final.shape if nme == "norm_final" else (conv_w.shape if nme == "conv_w" else wv.shape)
        sg[nme], sd[nme], sm[nme], sv[nme] = (t.reshape(shape) for t in (g, dl, nm, nv))

    def ordered(big, sml):
        return (sml["norm_mix"], big["in"], sml["conv_w"], big["a"], big["b"], big["mix"], sml["norm_mem_q"],
                sml["norm_mem_kv"], big["mq"], big["kv"], big["mo"], sml["norm_ffn"], big["fi"], big["fo"],
                sml["norm_final"])

    loss_out = total[8, 0]
    grad_x = dx0.reshape(x.shape)
    return (loss_out, grad_x, *ordered(grads, sg), *ordered(deltas, sd), *ordered(new_m, sm), *ordered(new_v, sv))
```

```python
import functools
import math

import jax
import jax.numpy as jnp
from jax import lax
from jax.experimental import pallas as pl
from jax.experimental.pallas import tpu as pltpu

F32 = jnp.float32
BF16 = jnp.bfloat16
MESH = pl.DeviceIdType.MESH

N_DEV = 8
N_CHIP = 4
NORM_EPS = 1e-6
SB_HEADS = 8
SB_HEAD_DIM = 64
SB_WIDTH = SB_HEADS * SB_HEAD_DIM
CONV_WIDTH = 512
MEM_HEADS = 4
ADAM_LR = 0.001
ADAM_B1 = 0.9
ADAM_B2 = 0.999
ADAM_EPS = 1e-08
ADAM_WD = 0.01
ADAM_STEP = 10

LANES = 128
VMEM_LIMIT_BYTES = 52 * 1024 * 1024
SB_TILE = 256
SB_STEP_HEADS = 4
SB_DEAD = 159.0
SB_CLAMP = 126.0
LOG2_E = 1.4426950408889634

ANY = pl.BlockSpec(memory_space=pl.ANY)


def _params(n_grid):
    return pltpu.CompilerParams(dimension_semantics=("arbitrary",) * n_grid, vmem_limit_bytes=VMEM_LIMIT_BYTES)


def _bdot(a, b, dims):
    return lax.dot_general(a.astype(BF16), b.astype(BF16), (dims, ((), ())), preferred_element_type=F32)


NN = ((1,), (0,))
NT = ((1,), (1,))
TN = ((0,), (0,))


class _Comm:
    def __init__(self, ins, outs, n_sems, start, finish):
        self.ins, self.outs, self.n_sems, self.start, self.finish = ins, outs, n_sems, start, finish

    def sem_shapes(self):
        return [pltpu.SemaphoreType.DMA((k,)) for k in self.n_sems]


def _place():
    return lax.axis_index("x"), lax.axis_index("y"), lax.axis_index("c")


def _neighbours(x, y, c):
    return [(jnp.bitwise_xor(x, c), jnp.bitwise_xor(y, 1 - c)), (jnp.bitwise_xor(x, 1 - c), jnp.bitwise_xor(y, c)),
            (1 - x, 1 - y)]


def _gather_comm(shards):
    n = len(shards)

    def copies(ins, outs, sems):
        send_sems, recv_sems, _ = sems
        x, y, c = _place()
        chips = [(1 - x, y), (x, 1 - y), (1 - x, 1 - y)]

        def copy(a, k, block, to, from_shard=False):
            dst = outs[a].at[4 * block[0] + 2 * block[1] + block[2]]
            return pltpu.make_async_remote_copy(
                src_ref=ins[a] if from_shard else dst, dst_ref=dst, send_sem=send_sems.at[a * 7 + k],
                recv_sem=recv_sems.at[a * 7 + k], device_id=to, device_id_type=MESH)

        me, sibling = (x, y, c), (x, y, 1 - c)
        own = [[copy(a, 0, me, sibling, True)] + [copy(a, 1 + j, me, (*chip, c), True) for j, chip in enumerate(chips)]
               for a in range(n)]
        landed = [[copy(a, 1 + j, (*chip, c), me) for j, chip in enumerate(chips)] for a in range(n)]
        passed = [[copy(a, 4 + j, (*chip, c), sibling) for j, chip in enumerate(chips)] for a in range(n)]
        from_sibling = [[copy(a, 0, sibling, me)] + [copy(a, 4 + j, (*chip, 1 - c), me) for j, chip in enumerate(chips)]
                        for a in range(n)]
        local = [pltpu.make_async_copy(ins[a], outs[a].at[4 * x + 2 * y + c], sems[2].at[a]) for a in range(n)]
        return own, landed, passed, from_sibling, local

    def start(ins, outs, sems):
        own, _, _, _, local = copies(ins, outs, sems)
        for a in range(n):
            local[a].start()
            for cp in own[a]:
                cp.start()

    def finish(ins, outs, sems):
        own, landed, passed, from_sibling, local = copies(ins, outs, sems)
        for a in range(n):
            for arrived, onward in zip(landed[a], passed[a]):
                arrived.wait_recv()
                onward.start()
        for a in range(n):
            for cp in from_sibling[a]:
                cp.wait_recv()
        for a in range(n):
            for cp in own[a] + passed[a]:
                cp.wait_send()
            local[a].wait()

    outs = [jax.ShapeDtypeStruct((N_DEV,) + s.shape, s.dtype) for s in shards]
    return _Comm(list(shards), outs, (7 * n, 7 * n, n), start, finish)


def _sibling_comm(parts):
    n = len(parts)

    def copies(ins, outs, sems):
        x, y, c = _place()
        return [pltpu.make_async_remote_copy(
            src_ref=ins[a].at[2 * q + 1 - c], dst_ref=outs[a].at[q], send_sem=sems[0].at[a * N_CHIP + q],
            recv_sem=sems[1].at[a * N_CHIP + q], device_id=(x, y, 1 - c), device_id_type=MESH)
            for a in range(n) for q in range(N_CHIP)]

    def start(ins, outs, sems):
        for cp in copies(ins, outs, sems):
            cp.start()

    def finish(ins, outs, sems):
        cps = copies(ins, outs, sems)
        for cp in cps:
            cp.wait_recv()
        for cp in cps:
            cp.wait_send()

    outs = [jax.ShapeDtypeStruct((N_CHIP,) + p.shape[1:], p.dtype) for p in parts]
    return _Comm(list(parts), outs, (N_CHIP * n, N_CHIP * n), start, finish)


def _chips_comm(parts):
    n = len(parts)

    def copies(ins, outs, sems):
        x, y, c = _place()
        chips = [(1 - x, y), (x, 1 - y), (1 - x, 1 - y)]
        return [pltpu.make_async_remote_copy(
            src_ref=ins[a].at[2 * px + py], dst_ref=outs[a].at[j], send_sem=sems[0].at[a * 3 + j],
            recv_sem=sems[1].at[a * 3 + j], device_id=(px, py, c), device_id_type=MESH)
            for a in range(n) for j, (px, py) in enumerate(chips)]

    def start(ins, outs, sems):
        for cp in copies(ins, outs, sems):
            cp.start()

    def finish(ins, outs, sems):
        cps = copies(ins, outs, sems)
        for cp in cps:
            cp.wait_recv()
        for cp in cps:
            cp.wait_send()

    outs = [jax.ShapeDtypeStruct((3,) + p.shape[1:], p.dtype) for p in parts]
    return _Comm(list(parts), outs, (3 * n, 3 * n), start, finish)


def _join_comms(comms):
    if len(comms) == 1:
        return comms[0]

    def split(refs, counts):
        out, at = [], 0
        for n in counts:
            out.append(refs[at:at + n])
            at += n
        return out

    def each(method):
        def run(ins, outs, sems):
            parts = zip(comms, split(ins, [len(c.ins) for c in comms]), split(outs, [len(c.outs) for c in comms]),
                        split(sems, [len(c.n_sems) for c in comms]))
            for c, c_ins, c_outs, c_sems in parts:
                getattr(c, method)(c_ins, c_outs, c_sems)
        return run

    return _Comm([a for c in comms for a in c.ins], [o for c in comms for o in c.outs],
                 tuple(k for c in comms for k in c.n_sems), each("start"), each("finish"))


def _exchange(comm, *, name):
    n_ci, n_co = len(comm.ins), len(comm.outs)

    def kern(*refs):
        c_ins, c_outs, sems = refs[:n_ci], refs[n_ci:n_ci + n_co], refs[n_ci + n_co:]
        comm.start(c_ins, c_outs, sems)
        comm.finish(c_ins, c_outs, sems)

    return pl.pallas_call(kern, name=name, in_specs=[ANY] * n_ci, out_specs=[ANY] * n_co, out_shape=comm.outs,
                          scratch_shapes=comm.sem_shapes())(*comm.ins)


def _call(body, *, name, grid, in_specs, out_specs, out_shape, scratch, args, plan=None):
    comm = plan.comm(name) if plan is not None else None
    if comm is None:
        return list(pl.pallas_call(functools.partial(body), name=name, grid=grid, in_specs=in_specs,
                                   out_specs=out_specs, out_shape=out_shape, scratch_shapes=scratch,
                                   compiler_params=_params(len(grid)))(*args))
    n_in, n_out, n_scr, n_ci, n_co = len(in_specs), len(out_specs), len(scratch), len(comm.ins), len(comm.outs)

    def kern(*refs):
        ins, c_ins, refs = refs[:n_in], refs[n_in:n_in + n_ci], refs[n_in + n_ci:]
        outs, c_outs, refs = refs[:n_out], refs[n_out:n_out + n_co], refs[n_out + n_co:]
        scr, sems = refs[:n_scr], refs[n_scr:]
        ids = [pl.program_id(ax) for ax in range(len(grid))]
        first = functools.reduce(jnp.logical_and, [i == 0 for i in ids])
        last = functools.reduce(jnp.logical_and, [i == g - 1 for i, g in zip(ids, grid)])

        @pl.when(first)
        def _():
            comm.start(c_ins, c_outs, sems)
        body(*ins, *outs, *scr)

        @pl.when(last)
        def _():
            comm.finish(c_ins, c_outs, sems)

    res = pl.pallas_call(kern, name=name, grid=grid, in_specs=list(in_specs) + [ANY] * n_ci,
                         out_specs=list(out_specs) + [ANY] * n_co, out_shape=list(out_shape) + comm.outs,
                         scratch_shapes=list(scratch) + comm.sem_shapes(),
                         compiler_params=_params(len(grid)))(*args, *comm.ins)
    plan.landed(name, list(res[n_out:]))
    return list(res[:n_out])


def _mm_body(dims, has_add, *refs):
    if has_add:
        a_ref, b_ref, add_ref, o_ref = refs
        total = _bdot(a_ref[...], b_ref[...], dims) + add_ref[...]
    else:
        a_ref, b_ref, o_ref = refs
        total = _bdot(a_ref[...], b_ref[...], dims)
    o_ref[...] = total.astype(o_ref.dtype)


def _mm_nt_body(j, n, dy_ref, w_ref, o_ref):
    total = _bdot(dy_ref[:, 0:n], w_ref[0], NT)
    for jj in range(1, j):
        total = total + _bdot(dy_ref[:, jj * n:(jj + 1) * n], w_ref[jj], NT)
    o_ref[...] = total.astype(o_ref.dtype)


def _mm_nn(a, w3, *, name, out_dtype=BF16, add=None, tm=1024, tn=None, out3=False, w_t=False, plan=None):
    m, kk = a.shape
    j, n = w3.shape[0], w3.shape[1 if w_t else 2]
    tm, tn = min(tm, m), n if tn is None else tn
    n_t = n // tn
    in_specs = [pl.BlockSpec((tm, kk), lambda i, jj: (i, 0)),
                pl.BlockSpec((None, tn, kk), lambda i, jj: (jj // n_t, jj % n_t, 0)) if w_t else
                pl.BlockSpec((None, kk, tn), lambda i, jj: (jj // n_t, 0, jj % n_t))]
    args = [a, w3]
    if add is not None:
        in_specs.append(pl.BlockSpec((tm, tn), lambda i, jj: (i, jj)))
        args.append(add)
    if out3:
        out_spec = pl.BlockSpec((None, tm, tn), lambda i, jj: (jj // n_t, i, jj % n_t))
        out_shape = jax.ShapeDtypeStruct((j, m, n), out_dtype)
    else:
        out_spec = pl.BlockSpec((tm, tn), lambda i, jj: (i, jj))
        out_shape = jax.ShapeDtypeStruct((m, j * n), out_dtype)
    return _call(
        functools.partial(_mm_body, NT if w_t else NN, add is not None), name=name, grid=(m // tm, j * n_t),
        in_specs=in_specs, out_specs=[out_spec], out_shape=[out_shape], scratch=[], args=args, plan=plan)[0]


def _mm_gathering(a, shard, *, name, out3=False, w_t=False, tm=1024):
    m, kk = a.shape
    n = shard.shape[0 if w_t else 1]
    tm = min(tm, m)
    n_i = m // tm
    fetch_at = min(1, n_i - 1)

    def body(a_ref, shard_ref, o_ref, w_all, w_vmem, send_sems, recv_sems, copy_sems):
        jj, i = pl.program_id(0), pl.program_id(1)
        x, y, c = _place()
        me, sibling = (x, y, c), (x, y, 1 - c)
        chips = _neighbours(x, y, c)
        sibling_chips = [chips[1], chips[0], chips[2]]

        def rows(block):
            return w_all.at[4 * block[0] + 2 * block[1] + block[2]]

        def remote(k, block, to, from_shard=False):
            return pltpu.make_async_remote_copy(
                src_ref=shard_ref if from_shard else rows(block), dst_ref=rows(block), send_sem=send_sems.at[k],
                recv_sem=recv_sems.at[k], device_id=to, device_id_type=MESH)

        def load(step, src):
            return pltpu.make_async_copy(src, w_vmem.at[step % 2], copy_sems.at[1 + step % 2])

        own = [remote(0, me, sibling, True), remote(1, me, (*chips[0], c), True), remote(2, me, (*chips[1], c), True),
               remote(3, (*chips[0], c), (*chips[1], c))]
        passed = [remote(4 + j, (*chip, c), sibling) for j, chip in enumerate(chips)]
        local = pltpu.make_async_copy(shard_ref, rows(me), copy_sems.at[0])

        @pl.when(jnp.logical_and(i == 0, jj == 0))
        def _():
            local.start()
            own[0].start()
            own[1].start()
            load(0, shard_ref).start()

        def arrivals():
            yield 1, (lambda: remote(0, sibling, me).wait_recv()), sibling
            for j, chip in enumerate(chips):
                def landed(j=j, chip=chip):
                    if j < 2:
                        own[1 + j].wait_send()
                        own[2 + j].start()
                    remote(1 + j, (*chip, c), me).wait_recv()
                    passed[j].start()
                yield 2 + 2 * j, landed, (*chip, c)
                block = (*sibling_chips[j], 1 - c)
                yield 3 + 2 * j, (lambda j=j, block=block: remote(4 + j, block, me).wait_recv()), block

        for step, wait_for_it, block in arrivals():
            @pl.when(jnp.logical_and(i == fetch_at, jj == step - 1))
            def _():
                wait_for_it()
                load(step, rows(block)).start()

        for step in range(N_DEV):
            @pl.when(jnp.logical_and(i == 0, jj == step))
            def _():
                load(step, rows(me)).wait()

        o_ref[...] = _bdot(a_ref[...], w_vmem[lax.rem(jj, 2)], NT if w_t else NN).astype(o_ref.dtype)

        @pl.when(jnp.logical_and(i == n_i - 1, jj == N_DEV - 1))
        def _():
            for cp in [own[0], own[3]] + passed:
                cp.wait_send()
            local.wait()

    def swept(jj):
        x, y, c = _place()
        first, second = 2 + 2 * c, 4 - 2 * c
        flips = (0b000, 0b001, first, second + 1, second, first + 1, 0b110, 0b111)
        return jnp.bitwise_xor(4 * x + 2 * y + c, sum(jnp.where(jj == k, f, 0) for k, f in enumerate(flips)))

    if out3:
        out_spec = pl.BlockSpec((None, tm, n), lambda jj, i: (swept(jj), i, 0))
        out_shape = jax.ShapeDtypeStruct((N_DEV, m, n), BF16)
    else:
        out_spec = pl.BlockSpec((tm, n), lambda jj, i: (i, swept(jj)))
        out_shape = jax.ShapeDtypeStruct((m, N_DEV * n), BF16)
    return pl.pallas_call(
        body, name=name, grid=(N_DEV, n_i),
        in_specs=[pl.BlockSpec((tm, kk), lambda jj, i: (i, 0)), ANY], out_specs=[out_spec, ANY],
        scratch_shapes=[pltpu.VMEM((2,) + shard.shape, shard.dtype), pltpu.SemaphoreType.DMA((7,)),
                        pltpu.SemaphoreType.DMA((7,)), pltpu.SemaphoreType.DMA((3,))],
        out_shape=[out_shape, jax.ShapeDtypeStruct((N_DEV,) + shard.shape, shard.dtype)],
        compiler_params=_params(2))(a, shard)


def _sigmoid(v):
    return 0.5 * jnp.tanh(0.5 * v) + 0.5


def _resident(w):
    return pl.BlockSpec(w.shape, lambda i: (0,) * w.ndim, pipeline_mode=pl.Buffered(1))


def _ffn_out_loss(gu3, w3, add, g, target, *, name, tm=512):
    j2, m, n = gu3.shape
    j = j2 // 2
    nn = w3.shape[2]
    tm = min(tm, m)

    def body(gu_ref, w_ref, add_ref, g_ref, t_ref, dx_ref, dxb_ref, dg_ref, loss_ref, act_ref):
        i = pl.program_id(0)
        xv = add_ref[...]
        for jj in range(j):
            gate = gu_ref[0, jj].astype(F32)
            act = (gate * _sigmoid(gate) * gu_ref[1, jj].astype(F32)).astype(BF16)
            act_ref[jj] = act
            xv = xv + _bdot(act, w_ref[jj], NN)
        gv = g_ref[...]
        r = lax.rsqrt(jnp.mean(xv * xv, axis=-1, keepdims=True) + NORM_EPS)
        xhat = xv * r
        err = xhat * gv - t_ref[...]
        _acc_rows(i, loss_ref, 0.5 * jnp.sum(jnp.mean(err * err, axis=-1, keepdims=True), axis=0, keepdims=True))
        dy = err * (1.0 / nn)
        dxhat = dy * gv
        dx = r * (dxhat - xhat * jnp.mean(dxhat * xhat, axis=-1, keepdims=True))
        dx_ref[...] = dx
        dxb_ref[...] = dx.astype(BF16)
        _acc_rows(i, dg_ref, jnp.sum(dy * xhat, axis=0, keepdims=True))

    row = pl.BlockSpec((tm, nn), lambda i: (i, 0))
    return _call(body, name=name, grid=(m // tm,),
                 in_specs=[pl.BlockSpec((2, j, tm, n), lambda i: (0, 0, i, 0)), _resident(w3),
                           row, pl.BlockSpec(g.shape, lambda i: (0, 0)), row],
                 out_specs=[row, row, pl.BlockSpec((8, nn), lambda i: (0, 0)), pl.BlockSpec((8, LANES), lambda i: (0, 0)),
                            pl.BlockSpec((j, tm, n), lambda i: (0, i, 0))],
                 out_shape=[jax.ShapeDtypeStruct((m, nn), F32), jax.ShapeDtypeStruct((m, nn), BF16),
                            jax.ShapeDtypeStruct((8, nn), F32), jax.ShapeDtypeStruct((8, LANES), F32),
                            jax.ShapeDtypeStruct((j, m, n), BF16)],
                 scratch=[], args=[gu3.reshape(2, j, m, n), w3, add, g, target])


def _ffn_out_bwd(dy, w3, gu3, *, name, tm=1024):
    m, nn = dy.shape
    j, n, _ = w3.shape
    tm = min(tm, m)

    def body(dy_ref, w_ref, gu_ref, dgu_ref):
        da = _bdot(dy_ref[...], w_ref[...], NT)
        gate = gu_ref[0].astype(F32)
        up = gu_ref[1].astype(F32)
        sg = _sigmoid(gate)
        silu = gate * sg
        dgu_ref[0] = (da * up * (sg + silu * (1.0 - sg))).astype(BF16)
        dgu_ref[1] = (da * silu).astype(BF16)

    out = _call(body, name=name, grid=(m // tm, j),
                in_specs=[pl.BlockSpec((tm, nn), lambda i, jj: (i, 0)),
                          pl.BlockSpec((None, n, nn), lambda i, jj: (jj, 0, 0)),
                          pl.BlockSpec((2, None, tm, n), lambda i, jj: (0, jj, i, 0))],
                out_specs=[pl.BlockSpec((2, None, tm, n), lambda i, jj: (0, jj, i, 0))],
                out_shape=[jax.ShapeDtypeStruct((2, j, m, n), BF16)], scratch=[],
                args=[dy, w3, gu3.reshape(2, j, m, n)])[0]
    return out.reshape(2 * j, m, n)


def _rms_fwd_tail(xv, g_ref, h_ref):
    r = lax.rsqrt(jnp.mean(xv * xv, axis=-1, keepdims=True) + NORM_EPS)
    h_ref[...] = (xv * r * g_ref[...]).astype(BF16)


def _rms_bwd_tail(i, dh, x_ref, g_ref, dres_ref, dx_ref, dxb_ref, dg_ref):
    xv = x_ref[...]
    r = lax.rsqrt(jnp.mean(xv * xv, axis=-1, keepdims=True) + NORM_EPS)
    xhat = xv * r
    dxhat = dh * g_ref[...]
    dx = r * (dxhat - xhat * jnp.mean(dxhat * xhat, axis=-1, keepdims=True))
    if dres_ref is not None:
        dx = dx + dres_ref[...]
    dx_ref[...] = dx
    dxb_ref[...] = dx.astype(BF16)
    _acc_rows(i, dg_ref, jnp.sum(dh * xhat, axis=0, keepdims=True))


def _mm_nt_rms(dy, w3, x, g, dres, *, name, dy3=False, w_nn=False, tm=512, plan=None):
    j = w3.shape[0]
    m, kk = x.shape
    n = dy.shape[2] if dy3 else dy.shape[1] // j
    tm = min(tm, m)

    def body(dy_ref, w_ref, x_ref, g_ref, *rest):
        dres_ref = rest[0] if dres is not None else None
        dx_ref, dxb_ref, dg_ref = rest[-3:]
        dh = None
        for jj in range(j):
            piece = dy_ref[jj] if dy3 else dy_ref[:, jj * n:(jj + 1) * n]
            part = _bdot(piece, w_ref[jj], NN if w_nn else NT)
            dh = part if dh is None else dh + part
        _rms_bwd_tail(pl.program_id(0), dh, x_ref, g_ref, dres_ref, dx_ref, dxb_ref, dg_ref)

    row = pl.BlockSpec((tm, kk), lambda i: (i, 0))
    in_specs = [pl.BlockSpec((j, tm, n), lambda i: (0, i, 0)) if dy3 else pl.BlockSpec((tm, j * n), lambda i: (i, 0)),
                _resident(w3), row, pl.BlockSpec(g.shape, lambda i: (0, 0))]
    args = [dy, w3, x, g]
    if dres is not None:
        in_specs.append(row)
        args.append(dres)
    return _call(body, name=name, grid=(m // tm,), in_specs=in_specs,
                 out_specs=[row, row, pl.BlockSpec((8, kk), lambda i: (0, 0))],
                 out_shape=[jax.ShapeDtypeStruct((m, kk), F32), jax.ShapeDtypeStruct((m, kk), BF16),
                            jax.ShapeDtypeStruct((8, kk), F32)], scratch=[], args=args, plan=plan)


def _mix_out(o_a, y_b, proj, w_a, w_b, w, x, g, *, name, tm=512, plan=None):
    s, c = o_a.shape
    d = w.shape[1]
    tm = min(tm, s)

    def body(oa_ref, yb_ref, ga_ref, gb_ref, wa_ref, wb_ref, w_ref, x_ref, g_ref, x1_ref, h_ref, merged_ref, a_ref, b_ref):
        a_ref[...] = _bdot(oa_ref[...], wa_ref[...], NN).astype(BF16)
        b_ref[...] = _bdot(yb_ref[...], wb_ref[...], NN).astype(BF16)
        merged = (_sigmoid(ga_ref[...].astype(F32)) * a_ref[...].astype(F32)
                  + _sigmoid(gb_ref[...].astype(F32)) * b_ref[...].astype(F32)).astype(BF16)
        merged_ref[...] = merged
        xv = _bdot(merged, w_ref[...], NN) + x_ref[...]
        x1_ref[...] = xv
        _rms_fwd_tail(xv, g_ref, h_ref)

    row = pl.BlockSpec((tm, d), lambda i: (i, 0))
    narrow = pl.BlockSpec((tm, c), lambda i: (i, 0))
    whole = lambda arr: pl.BlockSpec(arr.shape, lambda i: (0,) * arr.ndim)
    return _call(body, name=name, grid=(s // tm,),
                 in_specs=[narrow, narrow, pl.BlockSpec((tm, d), lambda i: (i, 3)), pl.BlockSpec((tm, d), lambda i: (i, 4)),
                           whole(w_a), whole(w_b), whole(w), row, whole(g)],
                 out_specs=[row] * 5,
                 out_shape=[jax.ShapeDtypeStruct((s, d), F32)] + [jax.ShapeDtypeStruct((s, d), BF16)] * 4,
                 scratch=[], args=[o_a, y_b, proj, proj, w_a, w_b, w, x, g], plan=plan)


def _mm_tn_a3(a3, dy, *, name):
    j, t, n = a3.shape
    nn = dy.shape[1]
    return _call(functools.partial(_mm_body, TN, False), name=name, grid=(j,),
                 in_specs=[pl.BlockSpec((None, t, n), lambda jj: (jj, 0, 0)), pl.BlockSpec((t, nn), lambda jj: (0, 0))],
                 out_specs=[pl.BlockSpec((None, n, nn), lambda jj: (jj, 0, 0))],
                 out_shape=[jax.ShapeDtypeStruct((j, n, nn), BF16)], scratch=[], args=[a3, dy])[0]


def _mm_nt(dy, w3, *, name, out_dtype=BF16, tm=512, tn=1024, plan=None):
    m = dy.shape[0]
    j, kk, n = w3.shape
    tm, tn = min(tm, m), min(tn, kk)
    return _call(
        functools.partial(_mm_nt_body, j, n), name=name,
        grid=(m // tm, kk // tn),
        in_specs=[pl.BlockSpec((tm, j * n), lambda i, q: (i, 0)),
                  pl.BlockSpec((j, tn, n), lambda i, q: (0, q, 0))],
        out_specs=[pl.BlockSpec((tm, tn), lambda i, q: (i, q))],
        out_shape=[jax.ShapeDtypeStruct((m, kk), out_dtype)], scratch=[], args=[dy, w3], plan=plan)[0]


def _mm_tn(a, dy, n, *, name, out_dtype=BF16, tm=512, tn=None, k_tiles=None, plan=None):
    t, kk = a.shape
    j = dy.shape[1] // n
    tm, tn = min(tm, kk), n if tn is None else tn
    n_t = n // tn
    first, count = (0, kk // tm) if k_tiles is None else k_tiles
    return _call(
        functools.partial(_mm_body, TN, False), name=name,
        grid=(count, j * n_t),
        in_specs=[pl.BlockSpec((t, tm), lambda i, jj: (0, first + i)),
                  pl.BlockSpec((t, tn), lambda i, jj: (0, jj))],
        out_specs=[pl.BlockSpec((None, tm, tn), lambda i, jj: (jj // n_t, i, jj % n_t))],
        out_shape=[jax.ShapeDtypeStruct((j, count * tm, n), out_dtype)], scratch=[], args=[a, dy], plan=plan)[0]


def _rows(body, ins, outs, *, n_rows, tm, name, plan=None):
    tm = min(tm, n_rows)
    n_steps = n_rows // tm
    in_specs, args = [], []
    for arr, kind, width, block in ins:
        if kind == "row":
            in_specs.append(pl.BlockSpec((tm, width), functools.partial(lambda i, b: (i, b), b=block)))
        elif kind == "prev":
            in_specs.append(pl.BlockSpec((tm, width), functools.partial(lambda i, b: (jnp.maximum(i - 1, 0), b), b=block)))
        elif kind == "next":
            in_specs.append(pl.BlockSpec((tm, width), functools.partial(lambda i, b: (jnp.minimum(i + 1, n_steps - 1), b), b=block)))
        else:
            in_specs.append(pl.BlockSpec(arr.shape, functools.partial(lambda i, nd: (0,) * nd, nd=arr.ndim)))
        args.append(arr)
    out_specs, out_shape = [], []
    for shape, dtype, kind in outs:
        if kind == "row":
            out_specs.append(pl.BlockSpec((tm, shape[1]), lambda i: (i, 0)))
        else:
            out_specs.append(pl.BlockSpec(shape, functools.partial(lambda i, nd: (0,) * nd, nd=len(shape))))
        out_shape.append(jax.ShapeDtypeStruct(shape, dtype))

    def kern(*refs):
        body(pl.program_id(0), n_steps, *refs)

    return _call(kern, name=name, grid=(n_steps,), in_specs=in_specs, out_specs=out_specs, out_shape=out_shape,
                 scratch=[], args=args, plan=plan)


def _acc_rows(i, ref, value):
    @pl.when(i == 0)
    def _():
        ref[...] = jnp.zeros_like(ref)
    ref[...] += jnp.broadcast_to(value, ref.shape)


def _rms_fwd(x, g, *, name, tm=512):
    s, d = x.shape

    def body(i, n, x_ref, g_ref, h_ref):
        _rms_fwd_tail(x_ref[...], g_ref, h_ref)

    return _rows(body, [(x, "row", d, 0), (g, "full", 0, 0)], [((s, d), BF16, "row")], n_rows=s, tm=tm, name=name)[0]


def _rms_bwd(x, g, dh, dres, *, name, tm=512, plan=None):
    s, d = x.shape

    def body(i, n, x_ref, g_ref, dh_ref, dres_ref, dx_ref, dxb_ref, dg_ref):
        _rms_bwd_tail(i, dh_ref[...].astype(F32), x_ref, g_ref, dres_ref, dx_ref, dxb_ref, dg_ref)

    return _rows(body, [(x, "row", d, 0), (g, "full", 0, 0), (dh, "row", d, 0), (dres, "row", d, 0)],
                 [((s, d), F32, "row"), ((s, d), BF16, "row"), ((8, d), F32, "acc")],
                 n_rows=s, tm=tm, name=name, plan=plan)


def _mix_out_bwd(dx1b, w, br_a, br_b, proj, w_a, w_b, *, name, tm=512, plan=None):
    s, d = br_a.shape
    c = w_a.shape[0]
    tm = min(tm, s)

    def body(dy_ref, w_ref, a_ref, b_ref, ga_ref, gb_ref, wa_ref, wb_ref, da_ref, db_ref, dg_ref, doa_ref, dyb_ref):
        dm = _bdot(dy_ref[...], w_ref[...], NT)
        sa = _sigmoid(ga_ref[...].astype(F32))
        sb = _sigmoid(gb_ref[...].astype(F32))
        da_ref[...] = (dm * sa).astype(BF16)
        db_ref[...] = (dm * sb).astype(BF16)
        dg_ref[:, :d] = (dm * a_ref[...].astype(F32) * sa * (1.0 - sa)).astype(BF16)
        dg_ref[:, d:] = (dm * b_ref[...].astype(F32) * sb * (1.0 - sb)).astype(BF16)
        doa_ref[...] = _bdot(da_ref[...], wa_ref[...], NT).astype(BF16)
        dyb_ref[...] = _bdot(db_ref[...], wb_ref[...], NT).astype(BF16)

    row = pl.BlockSpec((tm, d), lambda i: (i, 0))
    narrow = pl.BlockSpec((tm, c), lambda i: (i, 0))
    whole = lambda arr: pl.BlockSpec(arr.shape, lambda i: (0,) * arr.ndim)
    return _call(body, name=name, grid=(s // tm,),
                 in_specs=[row, whole(w), row, row, pl.BlockSpec((tm, d), lambda i: (i, 3)),
                           pl.BlockSpec((tm, d), lambda i: (i, 4)), whole(w_a), whole(w_b)],
                 out_specs=[row, row, pl.BlockSpec((tm, 2 * d), lambda i: (i, 0)), narrow, narrow],
                 out_shape=[jax.ShapeDtypeStruct((s, d), BF16), jax.ShapeDtypeStruct((s, d), BF16),
                            jax.ShapeDtypeStruct((s, 2 * d), BF16), jax.ShapeDtypeStruct((s, c), BF16),
                            jax.ShapeDtypeStruct((s, c), BF16)],
                 scratch=[], args=[dx1b, w, br_a, br_b, proj, proj, w_a, w_b], plan=plan)


def _shift_down(cur, prev, k, first):
    row = lax.broadcasted_iota(jnp.int32, cur.shape, 0)
    out = jnp.where(row >= k, pltpu.roll(cur, k, 0), pltpu.roll(prev, k, 0))
    return jnp.where(jnp.logical_and(first, row < k), 0.0, out)


def _shift_up(cur, nxt, k, last):
    tm = cur.shape[0]
    row = lax.broadcasted_iota(jnp.int32, cur.shape, 0)
    out = jnp.where(row < tm - k, pltpu.roll(cur, tm - k, 0), pltpu.roll(nxt, tm - k, 0))
    return jnp.where(jnp.logical_and(last, row >= tm - k), 0.0, out)


def _conv_fwd(proj, conv_w, *, name, tm=512):
    s = proj.shape[0]
    c = CONV_WIDTH

    def body(i, n, u_ref, gb_ref, gc_ref, up_ref, gcp_ref, w_ref, y_ref):
        cu = gc_ref[...].astype(F32) * u_ref[...].astype(F32)
        cup = gcp_ref[...].astype(F32) * up_ref[...].astype(F32)
        first = i == 0
        y = (w_ref[0:1, :] * _shift_down(cu, cup, 2, first) + w_ref[1:2, :] * _shift_down(cu, cup, 1, first)
             + w_ref[2:3, :] * cu)
        y_ref[...] = (gb_ref[...].astype(F32) * y).astype(BF16)

    return _rows(body, [(proj, "row", c, 3), (proj, "row", c, 4), (proj, "row", c, 5),
                        (proj, "prev", c, 3), (proj, "prev", c, 5), (conv_w, "full", 0, 0)],
                 [((s, c), BF16, "row")], n_rows=s, tm=tm, name=name)[0]


def _conv_bwd(dy_b, proj, conv_w, *, name, tm=512, plan=None):
    s = proj.shape[0]
    c = CONV_WIDTH

    def body(i, n, dy_ref, u_ref, gb_ref, gc_ref, up_ref, gcp_ref, dyn_ref, gbn_ref, w_ref, d_ref, dw_ref):
        first, last = i == 0, i == n - 1
        u = u_ref[...].astype(F32)
        gb = gb_ref[...].astype(F32)
        gc = gc_ref[...].astype(F32)
        cu = gc * u
        cup = gcp_ref[...].astype(F32) * up_ref[...].astype(F32)
        cu1 = _shift_down(cu, cup, 1, first)
        cu2 = _shift_down(cu, cup, 2, first)
        conv = w_ref[0:1, :] * cu2 + w_ref[1:2, :] * cu1 + w_ref[2:3, :] * cu
        dy = dy_ref[...].astype(F32)
        dyc = dy * gb
        dycn = dyn_ref[...].astype(F32) * gbn_ref[...].astype(F32)
        dcu = (w_ref[2:3, :] * dyc + w_ref[1:2, :] * _shift_up(dyc, dycn, 1, last)
               + w_ref[0:1, :] * _shift_up(dyc, dycn, 2, last))
        d_ref[:, 0:c] = (dcu * gc).astype(BF16)
        d_ref[:, c:2 * c] = (dy * conv).astype(BF16)
        d_ref[:, 2 * c:3 * c] = (dcu * u).astype(BF16)
        row = lax.broadcasted_iota(jnp.int32, (8, c), 0)
        dw = (jnp.where(row == 0, jnp.sum(dyc * cu2, axis=0, keepdims=True), 0.0)
              + jnp.where(row == 1, jnp.sum(dyc * cu1, axis=0, keepdims=True), 0.0)
              + jnp.where(row == 2, jnp.sum(dyc * cu, axis=0, keepdims=True), 0.0))

        @pl.when(first)
        def _():
            dw_ref[...] = jnp.zeros_like(dw_ref)
        dw_ref[...] += dw

    return _rows(body, [(dy_b, "row", c, 0), (proj, "row", c, 3), (proj, "row", c, 4), (proj, "row", c, 5),
                        (proj, "prev", c, 3), (proj, "prev", c, 5), (dy_b, "next", c, 0), (proj, "next", c, 4),
                        (conv_w, "full", 0, 0)],
                 [((s, 3 * c), BF16, "row"), ((8, c), F32, "acc")], n_rows=s, tm=tm, name=name, plan=plan)


def _mem_probs(q, k, scale):
    sc = _bdot(q, k, NT) * scale
    sc = sc - jnp.max(sc, axis=-1, keepdims=True)
    p = jnp.exp(sc)
    return p / jnp.sum(p, axis=-1, keepdims=True)


def _mem_sublayer(hq, w_q, kv, w_o, x, g, *, name, tm=512, plan=None):
    s, d = hq.shape
    hd = d // MEM_HEADS
    scale = 1.0 / math.sqrt(hd)
    tm = min(tm, s)

    def body(hq_ref, wq_ref, kv_ref, wo_ref, x_ref, g_ref, q_ref, o_ref, x2_ref, h_ref):
        q_ref[...] = _bdot(hq_ref[...], wq_ref[...], NN).astype(BF16)
        for h in range(MEM_HEADS):
            cols = slice(h * hd, (h + 1) * hd)
            p = _mem_probs(q_ref[:, cols], kv_ref[:, cols], scale)
            o_ref[:, cols] = _bdot(p, kv_ref[:, d + h * hd:d + (h + 1) * hd], NN).astype(BF16)
        xv = _bdot(o_ref[...], wo_ref[...], NN) + x_ref[...]
        x2_ref[...] = xv
        _rms_fwd_tail(xv, g_ref, h_ref)

    row = pl.BlockSpec((tm, d), lambda i: (i, 0))
    whole = lambda a: pl.BlockSpec(a.shape, lambda i: (0,) * a.ndim)
    return _call(body, name=name, grid=(s // tm,),
                 in_specs=[row, whole(w_q), whole(kv), whole(w_o), row, whole(g)], out_specs=[row] * 4,
                 out_shape=[jax.ShapeDtypeStruct((s, d), BF16), jax.ShapeDtypeStruct((s, d), BF16),
                            jax.ShapeDtypeStruct((s, d), F32), jax.ShapeDtypeStruct((s, d), BF16)],
                 scratch=[], args=[hq, w_q, kv, w_o, x, g], plan=plan)


def _mem_sublayer_bwd(dx2b, dx2, x, g, qm, kv, w_q, w_o, *, name, tm=512, plan=None):
    s, d = qm.shape
    hd = d // MEM_HEADS
    scale = 1.0 / math.sqrt(hd)
    tm = min(tm, s)

    def body(dyb_ref, dres_ref, x_ref, g_ref, q_ref, kv_ref, wq_ref, wo_ref, dx_ref, dxb_ref, dg_ref, dq_ref, dkv_ref):
        i = pl.program_id(0)

        @pl.when(i == 0)
        def _():
            dkv_ref[...] = jnp.zeros_like(dkv_ref)
        dom = _bdot(dyb_ref[...], wo_ref[...], NT).astype(BF16)
        for h in range(MEM_HEADS):
            cols = slice(h * hd, (h + 1) * hd)
            vcols = slice(d + h * hd, d + (h + 1) * hd)
            q, k, v, do = q_ref[:, cols], kv_ref[:, cols], kv_ref[:, vcols], dom[:, cols]
            p = _mem_probs(q, k, scale)
            dp = _bdot(do, v, NT)
            ds = p * (dp - jnp.sum(dp * p, axis=-1, keepdims=True)) * scale
            dq_ref[:, cols] = _bdot(ds, k, NN).astype(BF16)
            dkv_ref[:, cols] += _bdot(ds, q, TN)
            dkv_ref[:, vcols] += _bdot(p, do, TN)
        dh = _bdot(dq_ref[...], wq_ref[...], NT)
        _rms_bwd_tail(i, dh, x_ref, g_ref, dres_ref, dx_ref, dxb_ref, dg_ref)

    row = pl.BlockSpec((tm, d), lambda i: (i, 0))
    whole = lambda a: pl.BlockSpec(a.shape, lambda i: (0,) * a.ndim)
    return _call(body, name=name, grid=(s // tm,),
                 in_specs=[row, row, row, whole(g), row, whole(kv), whole(w_q), whole(w_o)],
                 out_specs=[row, row, pl.BlockSpec((8, d), lambda i: (0, 0)), row, whole(kv)],
                 out_shape=[jax.ShapeDtypeStruct((s, d), F32), jax.ShapeDtypeStruct((s, d), BF16),
                            jax.ShapeDtypeStruct((8, d), F32), jax.ShapeDtypeStruct((s, d), BF16),
                            jax.ShapeDtypeStruct(kv.shape, F32)],
                 scratch=[], args=[dx2b, dx2, x, g, qm, kv, w_q, w_o], plan=plan)


def _sb_consts(t):
    row = lax.broadcasted_iota(jnp.int32, (t, t), 0)
    col = lax.broadcasted_iota(jnp.int32, (t, t), 1)
    lane = lax.broadcasted_iota(jnp.int32, (t, LANES), 1)
    return row, col, lane < SB_HEAD_DIM


def _sb_logits(q, k):
    z2 = jnp.minimum(_bdot(q, k, NT) * LOG2_E, SB_CLAMP)
    return z2, jnp.exp2(z2)


def _tri_sum(v, tri):
    hi = v.astype(BF16)
    lo = (v - hi.astype(F32)).astype(BF16)
    return _bdot(hi, tri, NN) + _bdot(lo, tri, NN)


def _sb_fwd(proj, *, name, plan=None):
    s = proj.shape[0]
    t, nh = SB_TILE, SB_STEP_HEADS
    n_q = s // t
    scale = 1.0 / math.sqrt(SB_HEAD_DIM)

    def body(q_ref, k_ref, v_ref, o_ref, c_ref, first_ref, acc_ref, c_scr):
        i = pl.program_id(1)
        row, col, head0 = _sb_consts(t)
        later = (row > col).astype(BF16)
        valid = col < row
        lanes = lambda h: slice((h // 2) * LANES, (h // 2 + 1) * LANES)
        q = [jnp.where(head0 == (h % 2 == 0), q_ref[:, lanes(h)] * scale, 0) for h in range(nh)]

        def tiles(kbs, diag_first, carry):
            rows = [pl.ds(pl.multiple_of(kb * t, t), t) for kb in kbs]
            jobs = [(n, h) for n in range(len(kbs)) for h in range(nh)]
            masked = lambda n: diag_first and n == 0
            zs = {(n, h): _sb_logits(q[h], k_ref[rows[n], lanes(h)]) for n, h in jobs}
            fail = {j: jnp.log2(1.0 + zs[j][1]) for j in jobs}
            fail = {j: jnp.where(valid, fail[j], 0.0) if masked(j[0]) else fail[j] for j in jobs}
            cum = {j: _tri_sum(fail[j], later) for j in jobs}
            run, before = list(carry), {}
            for n, h in jobs:
                before[n, h] = run[h]
                run[h] = run[h] + cum[n, h][:, 0:1] + fail[n, h][:, 0:1]
            w = {j: jnp.exp2(zs[j][0] - fail[j] - cum[j] - before[j]) for j in jobs}
            w = {j: jnp.where(valid, w[j], 0.0) if masked(j[0]) else w[j] for j in jobs}
            for n, h in jobs:
                acc_ref[h] += _bdot(w[n, h], v_ref[rows[n], lanes(h)], NN)
            return tuple(run)

        acc_ref[...] = jnp.zeros_like(acc_ref)
        zero = (jnp.zeros((t, 1), F32),) * nh

        def alive(carry):
            return (functools.reduce(jnp.minimum, [jnp.min(c) for c in carry]) < SB_DEAD).astype(jnp.int32)

        def step(state):
            new = tiles([state[0]], False, state[2:])
            return (state[0] - 1, alive(new)) + new

        @pl.when(i == 0)
        def _():
            for h, c in enumerate(tiles([i], True, zero)):
                c_scr[h] = c

        @pl.when(i > 0)
        def _():
            for h, c in enumerate(tiles([i, i - 1], True, zero)):
                c_scr[h] = c
        carry = tuple(c_scr[h] for h in range(nh))
        state = lax.while_loop(lambda st: jnp.logical_and(st[0] >= 0, st[1] > 0), step, (i - 2, alive(carry)) + carry)
        for b in range(nh // 2):
            o_ref[:, b * LANES:(b + 1) * LANES] = jnp.where(head0, acc_ref[2 * b], acc_ref[2 * b + 1]).astype(BF16)
        head = lax.broadcasted_iota(jnp.int32, (t, nh), 1)
        c_ref[...] = sum(jnp.where(head == h, state[2 + h], 0.0) for h in range(nh))
        first_ref[pl.program_id(0), i] = (jnp.maximum(state[0], -1) + 1).astype(F32)

    n_p, width = SB_HEADS // nh, nh * SB_HEAD_DIM
    k_blk, v_blk = SB_WIDTH // width, 2 * SB_WIDTH // width
    return _call(
        body, name=name, grid=(n_p, n_q),
        in_specs=[pl.BlockSpec((t, width), lambda p, i: (i, p)),
                  pl.BlockSpec((s, width), lambda p, i: (0, k_blk + p)),
                  pl.BlockSpec((s, width), lambda p, i: (0, v_blk + p))],
        out_specs=[pl.BlockSpec((t, width), lambda p, i: (i, p)),
                   pl.BlockSpec((None, t, nh), lambda p, i: (p, i, 0)),
                   pl.BlockSpec(memory_space=pltpu.SMEM)],
        out_shape=[jax.ShapeDtypeStruct((s, SB_WIDTH), BF16), jax.ShapeDtypeStruct((n_p, s, nh), F32),
                   jax.ShapeDtypeStruct((n_p, n_q), F32)],
        scratch=[pltpu.VMEM((nh, t, LANES), F32), pltpu.VMEM((nh, t, 1), F32)], args=[proj, proj, proj], plan=plan)


def _sb_bwd(proj, do_a, ctot, first, *, name, plan=None):
    s = proj.shape[0]
    t, nh = SB_TILE, SB_STEP_HEADS
    n_q = s // t
    scale = 1.0 / math.sqrt(SB_HEAD_DIM)

    def body(q_ref, k_ref, v_ref, do_ref, c_ref, first_ref, dq_ref, dk_ref, dv_ref, dq_acc, dk_acc, dv_acc):
        i = pl.program_id(1)
        kb0 = jnp.clip(first_ref[pl.program_id(0), i].astype(jnp.int32), 0, i)
        row, col, head0 = _sb_consts(t)
        upto = (row <= col).astype(BF16)
        before = (row < col).astype(BF16)
        valid = col < row
        lanes = lambda h: slice((h // 2) * LANES, (h // 2 + 1) * LANES)
        q2 = [jnp.where(head0 == (h % 2 == 0), q_ref[:, lanes(h)] * scale, 0) for h in range(nh)]
        do2 = [jnp.where(head0 == (h % 2 == 0), do_ref[:, lanes(h)], 0) for h in range(nh)]
        ctot2 = [c_ref[:, h:h + 1] for h in range(nh)]

        @pl.when(i == 0)
        def _():
            dk_acc[...] = jnp.zeros_like(dk_acc)
            dv_acc[...] = jnp.zeros_like(dv_acc)
        dq_acc[...] = jnp.zeros_like(dq_acc)

        def tiles(kbs, diag_last, carry):
            rows = [pl.ds(pl.multiple_of(kb * t, t), t) for kb in kbs]
            kt = {(n, h): k_ref[rows[n], lanes(h)] for n in range(len(kbs)) for h in range(nh)}
            jobs = list(kt)
            masked = lambda n: diag_last and n == len(kbs) - 1
            t_last = slice(t - 1, t)
            zs = {(n, h): _sb_logits(q2[h], kt[n, h]) for n, h in jobs}
            dw = {(n, h): _bdot(do2[h], v_ref[rows[n], lanes(h)], NT) for n, h in jobs}
            fail = {j: jnp.log2(1.0 + zs[j][1]) for j in jobs}
            fail = {j: jnp.where(valid, fail[j], 0.0) if masked(j[0]) else fail[j] for j in jobs}
            cum = {j: _tri_sum(fail[j], upto) for j in jobs}
            miss = {j: jnp.exp2(-fail[j]) for j in jobs}
            beta = {j: zs[j][1] * miss[j] for j in jobs}
            fail_run, fail_before = list(carry[0::2]), {}
            for n, h in jobs:
                fail_before[n, h] = fail_run[h]
                fail_run[h] = fail_run[h] + cum[n, h][:, t_last]
            w = {(n, h): beta[n, h] * jnp.exp2(fail_before[n, h] + cum[n, h] - ctot2[h]) for n, h in jobs}
            w = {j: jnp.where(valid, w[j], 0.0) if masked(j[0]) else w[j] for j in jobs}
            g = {j: w[j] * dw[j] for j in jobs}
            g_local = {j: _bdot(g[j], before, NN) for j in jobs}
            for n, h in jobs:
                dv_acc[rows[n], lanes(h)] += _bdot(w[n, h], do2[h], TN)
            g_run, dz = list(carry[1::2]), {}
            for n, h in jobs:
                g_sum = g_run[h] + g_local[n, h]
                dz[n, h] = g[n, h] * miss[n, h] - beta[n, h] * g_sum
                g_run[h] = g_sum[:, t_last] + g[n, h][:, t_last]
            dz = {j: jnp.where(valid, dz[j], 0.0) if masked(j[0]) else dz[j] for j in jobs}
            for n, h in jobs:
                dq_acc[h] += _bdot(dz[n, h], kt[n, h], NN)
                dk_acc[rows[n], lanes(h)] += _bdot(dz[n, h], q2[h], TN)
            return tuple(v for pair in zip(fail_run, g_run) for v in pair)

        zero = jnp.zeros((t, 1), F32)
        carry = lax.fori_loop(kb0, i - 1, lambda n, c: tiles([n], False, c), (zero,) * (2 * nh))

        @pl.when(i == 0)
        def _():
            tiles([i], True, carry)

        @pl.when(i > 0)
        def _():
            tiles([i - 1, i], True, carry)
        for b in range(nh // 2):
            dq_ref[:, b * LANES:(b + 1) * LANES] = (jnp.where(head0, dq_acc[2 * b], dq_acc[2 * b + 1])
                                                    * scale).astype(BF16)

        @pl.when(i == n_q - 1)
        def _():
            dk_ref[...] = dk_acc[...].astype(BF16)
            dv_ref[...] = dv_acc[...].astype(BF16)

    n_p, width = SB_HEADS // nh, nh * SB_HEAD_DIM
    k_blk, v_blk = SB_WIDTH // width, 2 * SB_WIDTH // width
    outs = _call(
        body, name=name, grid=(n_p, n_q),
        in_specs=[pl.BlockSpec((t, width), lambda p, i: (i, p)),
                  pl.BlockSpec((s, width), lambda p, i: (0, k_blk + p)),
                  pl.BlockSpec((s, width), lambda p, i: (0, v_blk + p)),
                  pl.BlockSpec((t, width), lambda p, i: (i, p)),
                  pl.BlockSpec((None, t, nh), lambda p, i: (p, i, 0)),
                  pl.BlockSpec(memory_space=pltpu.SMEM)],
        out_specs=[pl.BlockSpec((t, width), lambda p, i: (i, p)),
                   pl.BlockSpec((s, width), lambda p, i: (0, p)),
                   pl.BlockSpec((s, width), lambda p, i: (0, p))],
        out_shape=[jax.ShapeDtypeStruct((s, SB_WIDTH), BF16)] * 3,
        scratch=[pltpu.VMEM((nh, t, LANES), F32), pltpu.VMEM((s, width), F32), pltpu.VMEM((s, width), F32)],
        args=[proj, proj, proj, do_a, ctot, first], plan=plan)
    return jnp.concatenate(outs, axis=1)


def _mm_gathered(a, key, plan, *, name, out3=False, w_t=False):
    src = plan.gathering(key)
    if src is None:
        return _mm_nn(a, plan.weight(key), name=name, out3=out3, w_t=w_t, plan=plan)
    out, w_all = _mm_gathering(a, src, name=name, out3=out3, w_t=w_t)
    plan.set_weight(key, w_all)
    return out


def _local_step(x, mem, target, gains, plan):
    g_mix, g_memq, g_memkv, g_ffn, g_fin = gains
    d = x.shape[1]

    h0 = _rms_fwd(x, g_mix, name="rms_mix")
    proj = _mm_gathered(h0, "in", plan, name="mm_in")
    w_in = plan.weight("in")
    o_a, ctot, first = _sb_fwd(proj, name="sb_fwd", plan=plan)
    conv_w = plan.weight("conv")
    y_b = _conv_fwd(proj, conv_w, name="conv_fwd")
    w_a, w_b, w_mix = plan.weight("a"), plan.weight("b"), plan.weight("mix")
    x1, hq, merged, br_a, br_b = _mix_out(o_a, y_b, proj, w_a[0], w_b[0], w_mix[0], x, g_memq, name="mm_mix", plan=plan)
    w_mq, w_kv, w_mo = plan.weight("mq")[0], plan.weight("kv"), plan.weight("mo")[0]
    mn = _rms_fwd(mem, g_memkv, name="rms_memkv")
    kv = _mm_nn(mn, w_kv, name="mm_memkv")
    qm, om, x2, hf = _mem_sublayer(hq, w_mq, kv, w_mo, x1, g_ffn, name="mem_sublayer", plan=plan)
    gu = _mm_gathered(hf, "fi", plan, name="mm_ffn_in", out3=True, w_t=True)
    w_fi, w_fo = plan.weight("fi"), plan.weight("fo")
    dx3, dx3b, dg_fin, loss, act = _ffn_out_loss(gu, w_fo, x2, g_fin, target, name="mm_ffn_out")

    plan.grad("fo", _mm_tn_a3(act, dx3b, name="mm_d_w_ffn_out"))
    dgu = _ffn_out_bwd(dx3b, w_fo, gu, name="mm_d_act")
    plan.grad("fi", _mm_tn_a3(dgu, hf, name="mm_d_w_ffn_in"))
    dx2, dx2b, dg_ffn = _mm_nt_rms(dgu, w_fi, x2, g_ffn, dx3, name="mm_d_hf", dy3=True, w_nn=True, plan=plan)

    plan.grad("mo", _mm_tn(om, dx2b, d, name="mm_d_w_memo"))
    dx1, dx1b, dg_memq, dqm, dkv = _mem_sublayer_bwd(dx2b, dx2, x1, g_memq, qm, kv, w_mq, w_mo, name="mem_sublayer_bwd",
                                                    plan=plan)
    plan.grad("mq", _mm_tn(hq, dqm, d, name="mm_d_w_memq"))
    plan.grad("kv", _mm_tn(mn, dkv, w_kv.shape[2], name="mm_d_w_memkv"))
    _, _, dg_memkv = _mm_nt_rms(dkv, w_kv, mem, g_memkv, None, name="mm_d_mn")

    plan.grad("mix", _mm_tn(merged, dx1b, d, name="mm_d_w_mix"))
    dbr_a, dbr_b, dgab, do_a, dy_b = _mix_out_bwd(dx1b, w_mix[0], br_a, br_b, proj, w_a[0], w_b[0], name="mm_d_merged",
                                                 plan=plan)
    plan.grad("a", _mm_tn(o_a, dbr_a, d, name="mm_d_w_branch_a"))
    plan.grad("b", _mm_tn(y_b, dbr_b, d, name="mm_d_w_branch_b"))
    dconv, dconv_w = _conv_bwd(dy_b, proj, conv_w, name="conv_bwd", plan=plan)
    dqkv = _sb_bwd(proj, do_a, ctot, first, name="sb_bwd", plan=plan)
    dproj = jnp.concatenate([dqkv, dconv, dgab], axis=1)
    rows_in1 = d // IN_SPLIT[1] * (IN_SPLIT[1] - IN_SPLIT[0])
    plan.grad("in0", _mm_tn(h0, dproj, w_in.shape[2], name="mm_d_w_in0", tm=d - rows_in1, k_tiles=(0, 1)))
    plan.grad("in1", _mm_tn(h0, dproj, w_in.shape[2], name="mm_d_w_in1", tm=rows_in1,
                            k_tiles=(d // rows_in1 - 1, 1), plan=plan))
    dh0 = _mm_nt(dproj, w_in, name="mm_d_h0", out_dtype=F32, plan=plan)
    dx0, _, dg_mix = _rms_bwd(x, g_mix, dh0, dx1, name="rms_mix_bwd", plan=plan)

    return dx0, (dg_mix, dg_memq, dg_memkv, dg_ffn, dg_fin, dconv_w, loss)


def _row_tile(a, target=512):
    tm = min(a, target)
    while a % tm:
        tm -= 8
    return tm


def _sum_with_sibling(parts, recvs, core, *, name):
    n = len(parts)

    def body(core_ref, *refs):
        for p_ref, r_ref, o_ref in zip(refs[:n], refs[n:2 * n], refs[2 * n:]):
            o_ref[...] = (p_ref[...].astype(F32) + r_ref[...].astype(F32)).astype(o_ref.dtype)

    mine = [pl.BlockSpec((None,) + p.shape[1:], lambda q, core_ref: (2 * q + core_ref[0], 0, 0)) for p in parts]
    other = [pl.BlockSpec((None,) + p.shape[1:], lambda q, core_ref: (q, 0, 0)) for p in parts]
    return pl.pallas_call(
        body, name=name,
        grid_spec=pltpu.PrefetchScalarGridSpec(num_scalar_prefetch=1, grid=(N_CHIP,), in_specs=mine + other,
                                               out_specs=other),
        out_shape=[jax.ShapeDtypeStruct((N_CHIP,) + p.shape[1:], p.dtype) for p in parts],
        compiler_params=_params(1))(core, *parts, *recvs)


def _adam_math(wv, g, m, v):
    m = ADAM_B1 * m + (1.0 - ADAM_B1) * g
    v = ADAM_B2 * v + (1.0 - ADAM_B2) * (g * g)
    m_hat = m / (1.0 - ADAM_B1 ** ADAM_STEP)
    v_hat = v / (1.0 - ADAM_B2 ** ADAM_STEP)
    delta = -ADAM_LR * (m_hat / (jnp.sqrt(v_hat) + ADAM_EPS) + ADAM_WD * wv)
    return delta, m, v


def _adam_sharded(wv, m, v, own, recv, chip, *, name):
    a, b = wv.shape
    tm = _row_tile(a)

    def body(chip_ref, w_ref, m_ref, v_ref, own_ref, recv_ref, g_ref, d_ref, nm_ref, nv_ref):
        g = own_ref[...].astype(F32)
        for j in range(3):
            g = g + recv_ref[j].astype(F32)
        delta, nm, nv = _adam_math(w_ref[...], g, m_ref[...], v_ref[...])
        g_ref[...] = g
        d_ref[...] = delta
        nm_ref[...] = nm
        nv_ref[...] = nv

    tile = pl.BlockSpec((tm, b), lambda i, chip_ref: (i, 0))
    return pl.pallas_call(
        body, name=name,
        grid_spec=pltpu.PrefetchScalarGridSpec(
            num_scalar_prefetch=1, grid=(a // tm,),
            in_specs=[tile, tile, tile,
                      pl.BlockSpec((None, tm, b), lambda i, chip_ref: (chip_ref[0], i, 0)),
                      pl.BlockSpec((3, tm, b), lambda i, chip_ref: (0, i, 0))],
            out_specs=[tile] * 4),
        out_shape=[jax.ShapeDtypeStruct((a, b), F32)] * 4, compiler_params=_params(1))(chip, wv, m, v, own, recv)


def _sum_devices(gathered, *, name):
    _, r, c = gathered.shape

    def body(g_ref, o_ref):
        total = g_ref[0]
        for j in range(1, N_DEV):
            total = total + g_ref[j]
        o_ref[...] = total

    return pl.pallas_call(body, name=name, out_shape=jax.ShapeDtypeStruct((r, c), F32))(gathered)


def _adam_small(wv, g, m, v, *, name):
    def body(w_ref, g_ref, m_ref, v_ref, d_ref, nm_ref, nv_ref):
        delta, nm, nv = _adam_math(w_ref[...], g_ref[...], m_ref[...], v_ref[...])
        d_ref[...] = delta
        nm_ref[...] = nm
        nv_ref[...] = nv

    return pl.pallas_call(body, name=name, out_shape=[jax.ShapeDtypeStruct(wv.shape, F32)] * 3)(wv, g, m, v)


BIG = ("in", "a", "b", "mix", "mq", "kv", "mo", "fi", "fo")
ROW_SHARDED = ("mix", "mq", "mo")
UNSHARDED = ("a", "b")
FFN_GROUPS = 4
IN_SPLIT = (3, 4)
SMALL_ROWS = 16


class _Plan:
    FUSED = ("in",)
    GATHER_ON = {"sb_fwd": ("a", "b", "mix", "mq", "mo", "conv", "fi0"), "mm_mix": ("kv", "fi1"), "mem_sublayer": ("fi2",),
                 "mm_ffn_in": ("fo",)}
    SIBLING_ON = {"mm_d_hf": ("fo", "fi"), "mm_d_merged": ("mo", "mq", "kv"), "conv_bwd": ("mix", "a", "b"),
                  "mm_d_w_in1": ("in0",), "mm_d_h0": ("in1",)}
    CHIPS_ON = {"mem_sublayer_bwd": ("fo",), "sb_bwd": ("fi", "mo", "mq", "kv", "mix", "a", "b"), "mm_d_h0": ("in0",),
                "rms_mix_bwd": ("in1",)}

    def __init__(self, shards, core):
        self.shards, self.core = shards, core
        self.w, self.parts, self.chip_sums, self.from_chips = {}, {}, {}, {}

    def gathering(self, k):
        return self.shards[k] if k in self.FUSED else None

    def comm(self, name):
        comms = []
        if name in self.GATHER_ON:
            comms.append(_gather_comm([self.shards[k] for k in self.GATHER_ON[name]]))
        if name in self.SIBLING_ON:
            comms.append(_sibling_comm([self.parts[k] for k in self.SIBLING_ON[name]]))
        if name in self.CHIPS_ON:
            comms.append(_chips_comm([self.chip_sums[k] for k in self.CHIPS_ON[name]]))
        return _join_comms(comms) if comms else None

    def landed(self, name, outs):
        outs = list(outs)
        for k in self.GATHER_ON.get(name, ()):
            self.set_weight(k, outs.pop(0))
        keys = self.SIBLING_ON.get(name, ())
        if keys:
            sums = _sum_with_sibling([self.parts[k] for k in keys], [outs.pop(0) for _ in keys], self.core,
                                     name="sum_with_sibling_" + "_".join(keys))
            self.chip_sums.update(zip(keys, sums))
        for k in self.CHIPS_ON.get(name, ()):
            self.from_chips[k] = outs.pop(0)

    def set_weight(self, k, gathered):
        _, a, b = gathered.shape
        if k in ROW_SHARDED:
            gathered = gathered.reshape(1, N_DEV * a, b)
        elif k in UNSHARDED:
            gathered = jnp.transpose(gathered, (1, 0, 2)).reshape(1, a, N_DEV * b)
        elif k == "fo":
            gathered = gathered.reshape(FFN_GROUPS, N_DEV * a // FFN_GROUPS, b)
        elif k == "conv":
            n_conv = CONV_WIDTH // N_DEV
            gathered = jnp.transpose(gathered[:, :3, :n_conv], (1, 0, 2)).reshape(3, CONV_WIDTH)
        self.w[k] = gathered
        if k == "fi2":
            self.w["fi"] = jnp.concatenate([self.w["fi0"], self.w["fi1"], gathered], axis=2)

    def weight(self, k):
        return self.w[k]

    def grad(self, k, g):
        _, a, b = g.shape
        if k in ROW_SHARDED:
            g = g.reshape(N_DEV, a // N_DEV, b)
        elif k in UNSHARDED:
            g = jnp.transpose(g.reshape(a, N_DEV, b // N_DEV), (1, 0, 2))
        elif k == "fo":
            g = g.reshape(N_DEV, FFN_GROUPS * a // N_DEV, b)
        self.parts[k] = g


def kernel(x, mem, norm_mix, w_in, conv_w, w_branch_a, w_branch_b, w_mix_out, norm_mem_q, norm_mem_kv, w_mem_q, w_mem_kv, w_mem_o, norm_ffn, w_ffn_in, w_ffn_out, norm_final, loss_target, m_norm_mix, m_w_in, m_conv_w, m_w_branch_a, m_w_branch_b, m_w_mix_out, m_norm_mem_q, m_norm_mem_kv, m_w_mem_q, m_w_mem_kv, m_w_mem_o, m_norm_ffn, m_w_ffn_in, m_w_ffn_out, m_norm_final, v_norm_mix, v_w_in, v_conv_w, v_w_branch_a, v_w_branch_b, v_w_mix_out, v_norm_mem_q, v_norm_mem_kv, v_w_mem_q, v_w_mem_kv, v_w_mem_o, v_norm_ffn, v_w_ffn_in, v_w_ffn_out, v_norm_final):
    d = x.shape[-1]
    xi, yi, ci = lax.axis_index("x"), lax.axis_index("y"), lax.axis_index("c")
    chip = jnp.reshape(2 * xi + yi, (1,)).astype(jnp.int32)
    dev = 4 * xi + 2 * yi + ci

    big_w = dict(zip(BIG, (w_in, w_branch_a, w_branch_b, w_mix_out, w_mem_q, w_mem_kv, w_mem_o, w_ffn_in, w_ffn_out)))
    big_m = dict(zip(BIG, (m_w_in, m_w_branch_a, m_w_branch_b, m_w_mix_out, m_w_mem_q, m_w_mem_kv, m_w_mem_o, m_w_ffn_in, m_w_ffn_out)))
    big_v = dict(zip(BIG, (v_w_in, v_w_branch_a, v_w_branch_b, v_w_mix_out, v_w_mem_q, v_w_mem_kv, v_w_mem_o, v_w_ffn_in, v_w_ffn_out)))

    flip = lambda t, k: jnp.transpose(t) if k == "fi" else t
    shards = {k: flip(big_w[k][0], k).astype(BF16) for k in BIG}
    shards["fi0"], shards["fi1"], shards["fi2"] = (shards["fi"][:, :d // 2], shards["fi"][:, d // 2:3 * d // 4],
                                                   shards["fi"][:, 3 * d // 4:])
    n_conv = conv_w.shape[-1]
    shards["conv"] = jnp.zeros((8, LANES), F32).at[:3, :n_conv].set(conv_w[0])
    plan = _Plan(shards, jnp.reshape(ci, (1,)).astype(jnp.int32))

    gains = (norm_mix, norm_mem_q, norm_mem_kv, norm_ffn, norm_final.reshape(1, d))
    dx0, small = _local_step(x[0], mem[0], loss_target[0], gains, plan)

    grads, deltas, new_m, new_v = {}, {}, {}, {}
    for k in BIG:
        lead = big_w[k].shape
        wv, mv, vv = flip(big_w[k][0], k), flip(big_m[k][0], k), flip(big_v[k][0], k)
        if k == "in":
            half = wv.shape[0] * IN_SPLIT[0] // IN_SPLIT[1]
            lo = _adam_sharded(wv[:half], mv[:half], vv[:half], plan.chip_sums["in0"], plan.from_chips["in0"], chip,
                               name="adam_in0")
            hi = _adam_sharded(wv[half:], mv[half:], vv[half:], plan.chip_sums["in1"], plan.from_chips["in1"], chip,
                               name="adam_in1")
            outs = [jnp.concatenate(pair, axis=0) for pair in zip(lo, hi)]
        else:
            outs = _adam_sharded(wv, mv, vv, plan.chip_sums[k], plan.from_chips[k], chip, name="adam_" + k)
        grads[k], deltas[k], new_m[k], new_v[k] = (flip(t, k).reshape(lead) for t in outs)

    dg_mix, dg_memq, dg_memkv, dg_ffn, dg_fin, dconv_w, loss = small
    conv_rows = jnp.zeros((3, d), F32).at[:, :CONV_WIDTH].set(dconv_w[:3])
    block = jnp.concatenate([dg_mix[:1], dg_memq[:1], dg_memkv[:1], dg_ffn[:1], dg_fin[:1], conv_rows,
                             jnp.broadcast_to(loss[:1, :1], (1, d)), jnp.zeros((SMALL_ROWS - 9, d), F32)], axis=0)
    total = _sum_devices(_exchange(_gather_comm([block]), name="gather_small")[0], name="sum_small")
    g_conv = lax.dynamic_slice(total[5:8, :CONV_WIDTH], (0, dev * n_conv), (3, n_conv))
    small_w = [norm_mix, norm_mem_q, norm_mem_kv, norm_ffn, norm_final.reshape(1, d), conv_w[0]]
    small_m = [m_norm_mix, m_norm_mem_q, m_norm_mem_kv, m_norm_ffn, m_norm_final.reshape(1, d), m_conv_w[0]]
    small_v = [v_norm_mix, v_norm_mem_q, v_norm_mem_kv, v_norm_ffn, v_norm_final.reshape(1, d), v_conv_w[0]]
    small_g = [total[0:1], total[1:2], total[2:3], total[3:4], total[4:5], g_conv]
    small_names = ["norm_mix", "norm_mem_q", "norm_mem_kv", "norm_ffn", "norm_final", "conv_w"]
    sg, sd, sm, sv = {}, {}, {}, {}
    for nme, wv, g, m, v in zip(small_names, small_w, small_g, small_m, small_v):
        dl, nm, nv = _adam_small(wv, g, m, v, name="adam_" + nme)
        shape = norm_final.shape if nme == "norm_final" else (conv_w.shape if nme == "conv_w" else wv.shape)
        sg[nme], sd[nme], sm[nme], sv[nme] = (t.reshape(shape) for t in (g, dl, nm, nv))

    def ordered(big, sml):
        return (sml["norm_mix"], big["in"], sml["conv_w"], big["a"], big["b"], big["mix"], sml["norm_mem_q"],
                sml["norm_mem_kv"], big["mq"], big["kv"], big["mo"], sml["norm_ffn"], big["fi"], big["fo"],
                sml["norm_final"])

    loss_out = total[8, 0]
    grad_x = dx0.reshape(x.shape)
    return (loss_out, grad_x, *ordered(grads, sg), *ordered(deltas, sd), *ordered(new_m, sm), *ordered(new_v, sv))
```

```python
import functools
import math

import jax
import jax.numpy as jnp
from jax import lax
from jax.experimental import pallas as pl
from jax.experimental.pallas import tpu as pltpu

F32 = jnp.float32
BF16 = jnp.bfloat16
MESH = pl.DeviceIdType.MESH

N_DEV = 8
N_CHIP = 4
NORM_EPS = 1e-6
SB_HEADS = 8
SB_HEAD_DIM = 64
SB_WIDTH = SB_HEADS * SB_HEAD_DIM
CONV_WIDTH = 512
MEM_HEADS = 4
ADAM_LR = 0.001
ADAM_B1 = 0.9
ADAM_B2 = 0.999
ADAM_EPS = 1e-08
ADAM_WD = 0.01
ADAM_STEP = 10

LANES = 128
VMEM_LIMIT_BYTES = 52 * 1024 * 1024
SB_TILE = 256
SB_STEP_HEADS = 4
SB_DEAD = 159.0
SB_CLAMP = 126.0
LOG2_E = 1.4426950408889634

ANY = pl.BlockSpec(memory_space=pl.ANY)


def _params(n_grid):
    return pltpu.CompilerParams(dimension_semantics=("arbitrary",) * n_grid, vmem_limit_bytes=VMEM_LIMIT_BYTES)


def _bdot(a, b, dims):
    return lax.dot_general(a.astype(BF16), b.astype(BF16), (dims, ((), ())), preferred_element_type=F32)


NN = ((1,), (0,))
NT = ((1,), (1,))
TN = ((0,), (0,))


class _Comm:
    def __init__(self, ins, outs, n_sems, start, finish, late=None):
        self.ins, self.outs, self.n_sems, self.start, self.finish = ins, outs, n_sems, start, finish
        self.late = late if late is not None else (lambda ins, outs, sems: None)

    def sem_shapes(self):
        return [pltpu.SemaphoreType.DMA((k,)) for k in self.n_sems]


def _place():
    return lax.axis_index("x"), lax.axis_index("y"), lax.axis_index("c")


def _neighbours(x, y, c):
    return [(jnp.bitwise_xor(x, c), jnp.bitwise_xor(y, 1 - c)), (jnp.bitwise_xor(x, 1 - c), jnp.bitwise_xor(y, c)),
            (1 - x, 1 - y)]


def _gather_comm(shards):
    n = len(shards)

    def copies(ins, outs, sems):
        send_sems, recv_sems, _ = sems
        x, y, c = _place()
        chips = [(1 - x, y), (x, 1 - y), (1 - x, 1 - y)]

        def copy(a, k, block, to, from_shard=False):
            dst = outs[a].at[4 * block[0] + 2 * block[1] + block[2]]
            return pltpu.make_async_remote_copy(
                src_ref=ins[a] if from_shard else dst, dst_ref=dst, send_sem=send_sems.at[a * 7 + k],
                recv_sem=recv_sems.at[a * 7 + k], device_id=to, device_id_type=MESH)

        me, sibling = (x, y, c), (x, y, 1 - c)
        own = [[copy(a, 0, me, sibling, True)] + [copy(a, 1 + j, me, (*chip, c), True) for j, chip in enumerate(chips)]
               for a in range(n)]
        landed = [[copy(a, 1 + j, (*chip, c), me) for j, chip in enumerate(chips)] for a in range(n)]
        passed = [[copy(a, 4 + j, (*chip, c), sibling) for j, chip in enumerate(chips)] for a in range(n)]
        from_sibling = [[copy(a, 0, sibling, me)] + [copy(a, 4 + j, (*chip, 1 - c), me) for j, chip in enumerate(chips)]
                        for a in range(n)]
        local = [pltpu.make_async_copy(ins[a], outs[a].at[4 * x + 2 * y + c], sems[2].at[a]) for a in range(n)]
        return own, landed, passed, from_sibling, local

    def start(ins, outs, sems):
        own, _, _, _, local = copies(ins, outs, sems)
        for a in range(n):
            local[a].start()
            for cp in own[a]:
                cp.start()

    def late(ins, outs, sems):
        _, landed, passed, _, _ = copies(ins, outs, sems)
        for a in range(n):
            for arrived, onward in zip(landed[a], passed[a]):
                arrived.wait_recv()
                onward.start()

    def finish(ins, outs, sems):
        own, _, passed, from_sibling, local = copies(ins, outs, sems)
        for a in range(n):
            for cp in from_sibling[a]:
                cp.wait_recv()
        for a in range(n):
            for cp in own[a] + passed[a]:
                cp.wait_send()
            local[a].wait()

    outs = [jax.ShapeDtypeStruct((N_DEV,) + s.shape, s.dtype) for s in shards]
    return _Comm(list(shards), outs, (7 * n, 7 * n, n), start, finish, late)


def _sibling_comm(parts):
    n = len(parts)

    def copies(ins, outs, sems):
        x, y, c = _place()
        return [pltpu.make_async_remote_copy(
            src_ref=ins[a].at[2 * q + 1 - c], dst_ref=outs[a].at[q], send_sem=sems[0].at[a * N_CHIP + q],
            recv_sem=sems[1].at[a * N_CHIP + q], device_id=(x, y, 1 - c), device_id_type=MESH)
            for a in range(n) for q in range(N_CHIP)]

    def start(ins, outs, sems):
        for cp in copies(ins, outs, sems):
            cp.start()

    def finish(ins, outs, sems):
        cps = copies(ins, outs, sems)
        for cp in cps:
            cp.wait_recv()
        for cp in cps:
            cp.wait_send()

    outs = [jax.ShapeDtypeStruct((N_CHIP,) + p.shape[1:], p.dtype) for p in parts]
    return _Comm(list(parts), outs, (N_CHIP * n, N_CHIP * n), start, finish)


def _chips_comm(parts):
    n = len(parts)

    def copies(ins, outs, sems):
        x, y, c = _place()
        chips = [(1 - x, y), (x, 1 - y), (1 - x, 1 - y)]
        return [pltpu.make_async_remote_copy(
            src_ref=ins[a].at[2 * px + py], dst_ref=outs[a].at[j], send_sem=sems[0].at[a * 3 + j],
            recv_sem=sems[1].at[a * 3 + j], device_id=(px, py, c), device_id_type=MESH)
            for a in range(n) for j, (px, py) in enumerate(chips)]

    def start(ins, outs, sems):
        for cp in copies(ins, outs, sems):
            cp.start()

    def finish(ins, outs, sems):
        cps = copies(ins, outs, sems)
        for cp in cps:
            cp.wait_recv()
        for cp in cps:
            cp.wait_send()

    outs = [jax.ShapeDtypeStruct((3,) + p.shape[1:], p.dtype) for p in parts]
    return _Comm(list(parts), outs, (3 * n, 3 * n), start, finish)


def _join_comms(comms):
    if len(comms) == 1:
        return comms[0]

    def split(refs, counts):
        out, at = [], 0
        for n in counts:
            out.append(refs[at:at + n])
            at += n
        return out

    def each(method):
        def run(ins, outs, sems):
            parts = zip(comms, split(ins, [len(c.ins) for c in comms]), split(outs, [len(c.outs) for c in comms]),
                        split(sems, [len(c.n_sems) for c in comms]))
            for c, c_ins, c_outs, c_sems in parts:
                getattr(c, method)(c_ins, c_outs, c_sems)
        return run

    return _Comm([a for c in comms for a in c.ins], [o for c in comms for o in c.outs],
                 tuple(k for c in comms for k in c.n_sems), each("start"), each("finish"), each("late"))


_LATE_STEP = (7, 8)


def _exchange(comm, *, name):
    n_ci, n_co = len(comm.ins), len(comm.outs)

    def kern(*refs):
        c_ins, c_outs, sems = refs[:n_ci], refs[n_ci:n_ci + n_co], refs[n_ci + n_co:]
        comm.start(c_ins, c_outs, sems)
        comm.late(c_ins, c_outs, sems)
        comm.finish(c_ins, c_outs, sems)

    return pl.pallas_call(kern, name=name, in_specs=[ANY] * n_ci, out_specs=[ANY] * n_co, out_shape=comm.outs,
                          scratch_shapes=comm.sem_shapes())(*comm.ins)


def _call(body, *, name, grid, in_specs, out_specs, out_shape, scratch, args, plan=None):
    comm = plan.comm(name) if plan is not None else None
    if comm is None:
        return list(pl.pallas_call(functools.partial(body), name=name, grid=grid, in_specs=in_specs,
                                   out_specs=out_specs, out_shape=out_shape, scratch_shapes=scratch,
                                   compiler_params=_params(len(grid)))(*args))
    n_in, n_out, n_scr, n_ci, n_co = len(in_specs), len(out_specs), len(scratch), len(comm.ins), len(comm.outs)

    def kern(*refs):
        ins, c_ins, refs = refs[:n_in], refs[n_in:n_in + n_ci], refs[n_in + n_ci:]
        outs, c_outs, refs = refs[:n_out], refs[n_out:n_out + n_co], refs[n_out + n_co:]
        scr, sems = refs[:n_scr], refs[n_scr:]
        ids = [pl.program_id(ax) for ax in range(len(grid))]
        step = functools.reduce(lambda at, ig: at * ig[1] + ig[0], zip(ids, grid), 0)
        n_steps = math.prod(grid)

        @pl.when(step == 0)
        def _():
            comm.start(c_ins, c_outs, sems)

        @pl.when(step == n_steps * _LATE_STEP[0] // _LATE_STEP[1])
        def _():
            comm.late(c_ins, c_outs, sems)
        body(*ins, *outs, *scr)

        @pl.when(step == n_steps - 1)
        def _():
            comm.finish(c_ins, c_outs, sems)

    res = pl.pallas_call(kern, name=name, grid=grid, in_specs=list(in_specs) + [ANY] * n_ci,
                         out_specs=list(out_specs) + [ANY] * n_co, out_shape=list(out_shape) + comm.outs,
                         scratch_shapes=list(scratch) + comm.sem_shapes(),
                         compiler_params=_params(len(grid)))(*args, *comm.ins)
    plan.landed(name, list(res[n_out:]))
    return list(res[:n_out])


def _mm_body(dims, has_add, *refs):
    if has_add:
        a_ref, b_ref, add_ref, o_ref = refs
        total = _bdot(a_ref[...], b_ref[...], dims) + add_ref[...]
    else:
        a_ref, b_ref, o_ref = refs
        total = _bdot(a_ref[...], b_ref[...], dims)
    o_ref[...] = total.astype(o_ref.dtype)


def _mm_nt_body(j, n, dy_ref, w_ref, o_ref):
    total = _bdot(dy_ref[:, 0:n], w_ref[0], NT)
    for jj in range(1, j):
        total = total + _bdot(dy_ref[:, jj * n:(jj + 1) * n], w_ref[jj], NT)
    o_ref[...] = total.astype(o_ref.dtype)


def _mm_nn(a, w3, *, name, out_dtype=BF16, add=None, tm=1024, tn=None, out3=False, w_t=False, plan=None):
    m, kk = a.shape
    j, n = w3.shape[0], w3.shape[1 if w_t else 2]
    tm, tn = min(tm, m), n if tn is None else tn
    n_t = n // tn
    in_specs = [pl.BlockSpec((tm, kk), lambda i, jj: (i, 0)),
                pl.BlockSpec((None, tn, kk), lambda i, jj: (jj // n_t, jj % n_t, 0)) if w_t else
                pl.BlockSpec((None, kk, tn), lambda i, jj: (jj // n_t, 0, jj % n_t))]
    args = [a, w3]
    if add is not None:
        in_specs.append(pl.BlockSpec((tm, tn), lambda i, jj: (i, jj)))
        args.append(add)
    if out3:
        out_spec = pl.BlockSpec((None, tm, tn), lambda i, jj: (jj // n_t, i, jj % n_t))
        out_shape = jax.ShapeDtypeStruct((j, m, n), out_dtype)
    else:
        out_spec = pl.BlockSpec((tm, tn), lambda i, jj: (i, jj))
        out_shape = jax.ShapeDtypeStruct((m, j * n), out_dtype)
    return _call(
        functools.partial(_mm_body, NT if w_t else NN, add is not None), name=name, grid=(m // tm, j * n_t),
        in_specs=in_specs, out_specs=[out_spec], out_shape=[out_shape], scratch=[], args=args, plan=plan)[0]


def _mm_gathering(a, shard, *, name, out3=False, w_t=False, tm=1024):
    m, kk = a.shape
    n = shard.shape[0 if w_t else 1]
    tm = min(tm, m)
    n_i = m // tm
    fetch_at = min(1, n_i - 1)

    def body(a_ref, shard_ref, o_ref, w_all, w_vmem, send_sems, recv_sems, copy_sems):
        jj, i = pl.program_id(0), pl.program_id(1)
        x, y, c = _place()
        me, sibling = (x, y, c), (x, y, 1 - c)
        chips = _neighbours(x, y, c)
        sibling_chips = [chips[1], chips[0], chips[2]]

        def rows(block):
            return w_all.at[4 * block[0] + 2 * block[1] + block[2]]

        def remote(k, block, to, from_shard=False):
            return pltpu.make_async_remote_copy(
                src_ref=shard_ref if from_shard else rows(block), dst_ref=rows(block), send_sem=send_sems.at[k],
                recv_sem=recv_sems.at[k], device_id=to, device_id_type=MESH)

        def load(step, src):
            return pltpu.make_async_copy(src, w_vmem.at[step % 2], copy_sems.at[1 + step % 2])

        own = [remote(0, me, sibling, True), remote(1, me, (*chips[0], c), True), remote(2, me, (*chips[1], c), True),
               remote(3, (*chips[0], c), (*chips[1], c))]
        passed = [remote(4 + j, (*chip, c), sibling) for j, chip in enumerate(chips)]
        local = pltpu.make_async_copy(shard_ref, rows(me), copy_sems.at[0])

        @pl.when(jnp.logical_and(i == 0, jj == 0))
        def _():
            local.start()
            own[0].start()
            own[1].start()
            load(0, shard_ref).start()

        def arrivals():
            yield 1, (lambda: remote(0, sibling, me).wait_recv()), sibling
            for j, chip in enumerate(chips):
                def landed(j=j, chip=chip):
                    if j < 2:
                        own[1 + j].wait_send()
                        own[2 + j].start()
                    remote(1 + j, (*chip, c), me).wait_recv()
                    passed[j].start()
                yield 2 + 2 * j, landed, (*chip, c)
                block = (*sibling_chips[j], 1 - c)
                yield 3 + 2 * j, (lambda j=j, block=block: remote(4 + j, block, me).wait_recv()), block

        for step, wait_for_it, block in arrivals():
            @pl.when(jnp.logical_and(i == fetch_at, jj == step - 1))
            def _():
                wait_for_it()
                load(step, rows(block)).start()

        for step in range(N_DEV):
            @pl.when(jnp.logical_and(i == 0, jj == step))
            def _():
                load(step, rows(me)).wait()

        o_ref[...] = _bdot(a_ref[...], w_vmem[lax.rem(jj, 2)], NT if w_t else NN).astype(o_ref.dtype)

        @pl.when(jnp.logical_and(i == n_i - 1, jj == N_DEV - 1))
        def _():
            for cp in [own[0], own[3]] + passed:
                cp.wait_send()
            local.wait()

    def swept(jj):
        x, y, c = _place()
        first, second = 2 + 2 * c, 4 - 2 * c
        flips = (0b000, 0b001, first, second + 1, second, first + 1, 0b110, 0b111)
        return jnp.bitwise_xor(4 * x + 2 * y + c, sum(jnp.where(jj == k, f, 0) for k, f in enumerate(flips)))

    if out3:
        out_spec = pl.BlockSpec((None, tm, n), lambda jj, i: (swept(jj), i, 0))
        out_shape = jax.ShapeDtypeStruct((N_DEV, m, n), BF16)
    else:
        out_spec = pl.BlockSpec((tm, n), lambda jj, i: (i, swept(jj)))
        out_shape = jax.ShapeDtypeStruct((m, N_DEV * n), BF16)
    return pl.pallas_call(
        body, name=name, grid=(N_DEV, n_i),
        in_specs=[pl.BlockSpec((tm, kk), lambda jj, i: (i, 0)), ANY], out_specs=[out_spec, ANY],
        scratch_shapes=[pltpu.VMEM((2,) + shard.shape, shard.dtype), pltpu.SemaphoreType.DMA((7,)),
                        pltpu.SemaphoreType.DMA((7,)), pltpu.SemaphoreType.DMA((3,))],
        out_shape=[out_shape, jax.ShapeDtypeStruct((N_DEV,) + shard.shape, shard.dtype)],
        compiler_params=_params(2))(a, shard)


def _sigmoid(v):
    return 0.5 * jnp.tanh(0.5 * v) + 0.5


def _resident(w):
    return pl.BlockSpec(w.shape, lambda i: (0,) * w.ndim, pipeline_mode=pl.Buffered(1))


def _ffn_out_loss(gu3, w3, add, g, target, *, name, tm=512):
    j2, m, n = gu3.shape
    j = j2 // 2
    nn = w3.shape[2]
    tm = min(tm, m)

    def body(gu_ref, w_ref, add_ref, g_ref, t_ref, dx_ref, dxb_ref, dg_ref, loss_ref, act_ref):
        i = pl.program_id(0)
        xv = add_ref[...]
        for jj in range(j):
            gate = gu_ref[0, jj].astype(F32)
            act = (gate * _sigmoid(gate) * gu_ref[1, jj].astype(F32)).astype(BF16)
            act_ref[jj] = act
            xv = xv + _bdot(act, w_ref[jj], NN)
        gv = g_ref[...]
        r = lax.rsqrt(jnp.mean(xv * xv, axis=-1, keepdims=True) + NORM_EPS)
        xhat = xv * r
        err = xhat * gv - t_ref[...]
        _acc_rows(i, loss_ref, 0.5 * jnp.sum(jnp.mean(err * err, axis=-1, keepdims=True), axis=0, keepdims=True))
        dy = err * (1.0 / nn)
        dxhat = dy * gv
        dx = r * (dxhat - xhat * jnp.mean(dxhat * xhat, axis=-1, keepdims=True))
        dx_ref[...] = dx
        dxb_ref[...] = dx.astype(BF16)
        _acc_rows(i, dg_ref, jnp.sum(dy * xhat, axis=0, keepdims=True))

    row = pl.BlockSpec((tm, nn), lambda i: (i, 0))
    return _call(body, name=name, grid=(m // tm,),
                 in_specs=[pl.BlockSpec((2, j, tm, n), lambda i: (0, 0, i, 0)), _resident(w3),
                           row, pl.BlockSpec(g.shape, lambda i: (0, 0)), row],
                 out_specs=[row, row, pl.BlockSpec((8, nn), lambda i: (0, 0)), pl.BlockSpec((8, LANES), lambda i: (0, 0)),
                            pl.BlockSpec((j, tm, n), lambda i: (0, i, 0))],
                 out_shape=[jax.ShapeDtypeStruct((m, nn), F32), jax.ShapeDtypeStruct((m, nn), BF16),
                            jax.ShapeDtypeStruct((8, nn), F32), jax.ShapeDtypeStruct((8, LANES), F32),
                            jax.ShapeDtypeStruct((j, m, n), BF16)],
                 scratch=[], args=[gu3.reshape(2, j, m, n), w3, add, g, target])


def _ffn_out_bwd(dy, w3, gu3, *, name, tm=1024):
    m, nn = dy.shape
    j, n, _ = w3.shape
    tm = min(tm, m)

    def body(dy_ref, w_ref, gu_ref, dgu_ref):
        da = _bdot(dy_ref[...], w_ref[...], NT)
        gate = gu_ref[0].astype(F32)
        up = gu_ref[1].astype(F32)
        sg = _sigmoid(gate)
        silu = gate * sg
        dgu_ref[0] = (da * up * (sg + silu * (1.0 - sg))).astype(BF16)
        dgu_ref[1] = (da * silu).astype(BF16)

    out = _call(body, name=name, grid=(m // tm, j),
                in_specs=[pl.BlockSpec((tm, nn), lambda i, jj: (i, 0)),
                          pl.BlockSpec((None, n, nn), lambda i, jj: (jj, 0, 0)),
                          pl.BlockSpec((2, None, tm, n), lambda i, jj: (0, jj, i, 0))],
                out_specs=[pl.BlockSpec((2, None, tm, n), lambda i, jj: (0, jj, i, 0))],
                out_shape=[jax.ShapeDtypeStruct((2, j, m, n), BF16)], scratch=[],
                args=[dy, w3, gu3.reshape(2, j, m, n)])[0]
    return out.reshape(2 * j, m, n)


def _rms_fwd_tail(xv, g_ref, h_ref):
    r = lax.rsqrt(jnp.mean(xv * xv, axis=-1, keepdims=True) + NORM_EPS)
    h_ref[...] = (xv * r * g_ref[...]).astype(BF16)


def _rms_bwd_tail(i, dh, x_ref, g_ref, dres_ref, dx_ref, dxb_ref, dg_ref):
    xv = x_ref[...]
    r = lax.rsqrt(jnp.mean(xv * xv, axis=-1, keepdims=True) + NORM_EPS)
    xhat = xv * r
    dxhat = dh * g_ref[...]
    dx = r * (dxhat - xhat * jnp.mean(dxhat * xhat, axis=-1, keepdims=True))
    if dres_ref is not None:
        dx = dx + dres_ref[...]
    dx_ref[...] = dx
    dxb_ref[...] = dx.astype(BF16)
    _acc_rows(i, dg_ref, jnp.sum(dh * xhat, axis=0, keepdims=True))


def _mm_nt_rms(dy, w3, x, g, dres, *, name, dy3=False, w_nn=False, tm=512, plan=None):
    j = w3.shape[0]
    m, kk = x.shape
    n = dy.shape[2] if dy3 else dy.shape[1] // j
    tm = min(tm, m)

    def body(dy_ref, w_ref, x_ref, g_ref, *rest):
        dres_ref = rest[0] if dres is not None else None
        dx_ref, dxb_ref, dg_ref = rest[-3:]
        dh = None
        for jj in range(j):
            piece = dy_ref[jj] if dy3 else dy_ref[:, jj * n:(jj + 1) * n]
            part = _bdot(piece, w_ref[jj], NN if w_nn else NT)
            dh = part if dh is None else dh + part
        _rms_bwd_tail(pl.program_id(0), dh, x_ref, g_ref, dres_ref, dx_ref, dxb_ref, dg_ref)

    row = pl.BlockSpec((tm, kk), lambda i: (i, 0))
    in_specs = [pl.BlockSpec((j, tm, n), lambda i: (0, i, 0)) if dy3 else pl.BlockSpec((tm, j * n), lambda i: (i, 0)),
                _resident(w3), row, pl.BlockSpec(g.shape, lambda i: (0, 0))]
    args = [dy, w3, x, g]
    if dres is not None:
        in_specs.append(row)
        args.append(dres)
    return _call(body, name=name, grid=(m // tm,), in_specs=in_specs,
                 out_specs=[row, row, pl.BlockSpec((8, kk), lambda i: (0, 0))],
                 out_shape=[jax.ShapeDtypeStruct((m, kk), F32), jax.ShapeDtypeStruct((m, kk), BF16),
                            jax.ShapeDtypeStruct((8, kk), F32)], scratch=[], args=args, plan=plan)


def _mix_out(o_a, y_b, proj, w_a, w_b, w, x, g, *, name, tm=512, plan=None):
    s, c = o_a.shape
    d = w.shape[1]
    tm = min(tm, s)

    def body(oa_ref, yb_ref, ga_ref, gb_ref, wa_ref, wb_ref, w_ref, x_ref, g_ref, x1_ref, h_ref, merged_ref, a_ref, b_ref):
        a_ref[...] = _bdot(oa_ref[...], wa_ref[...], NN).astype(BF16)
        b_ref[...] = _bdot(yb_ref[...], wb_ref[...], NN).astype(BF16)
        merged = (_sigmoid(ga_ref[...].astype(F32)) * a_ref[...].astype(F32)
                  + _sigmoid(gb_ref[...].astype(F32)) * b_ref[...].astype(F32)).astype(BF16)
        merged_ref[...] = merged
        xv = _bdot(merged, w_ref[...], NN) + x_ref[...]
        x1_ref[...] = xv
        _rms_fwd_tail(xv, g_ref, h_ref)

    row = pl.BlockSpec((tm, d), lambda i: (i, 0))
    narrow = pl.BlockSpec((tm, c), lambda i: (i, 0))
    whole = lambda arr: pl.BlockSpec(arr.shape, lambda i: (0,) * arr.ndim)
    return _call(body, name=name, grid=(s // tm,),
                 in_specs=[narrow, narrow, pl.BlockSpec((tm, d), lambda i: (i, 3)), pl.BlockSpec((tm, d), lambda i: (i, 4)),
                           whole(w_a), whole(w_b), whole(w), row, whole(g)],
                 out_specs=[row] * 5,
                 out_shape=[jax.ShapeDtypeStruct((s, d), F32)] + [jax.ShapeDtypeStruct((s, d), BF16)] * 4,
                 scratch=[], args=[o_a, y_b, proj, proj, w_a, w_b, w, x, g], plan=plan)


def _mm_tn_a3(a3, dy, *, name):
    j, t, n = a3.shape
    nn = dy.shape[1]
    return _call(functools.partial(_mm_body, TN, False), name=name, grid=(j,),
                 in_specs=[pl.BlockSpec((None, t, n), lambda jj: (jj, 0, 0)), pl.BlockSpec((t, nn), lambda jj: (0, 0))],
                 out_specs=[pl.BlockSpec((None, n, nn), lambda jj: (jj, 0, 0))],
                 out_shape=[jax.ShapeDtypeStruct((j, n, nn), BF16)], scratch=[], args=[a3, dy])[0]


def _mm_nt(dy, w3, *, name, out_dtype=BF16, tm=512, tn=1024, plan=None):
    m = dy.shape[0]
    j, kk, n = w3.shape
    tm, tn = min(tm, m), min(tn, kk)
    return _call(
        functools.partial(_mm_nt_body, j, n), name=name,
        grid=(m // tm, kk // tn),
        in_specs=[pl.BlockSpec((tm, j * n), lambda i, q: (i, 0)),
                  pl.BlockSpec((j, tn, n), lambda i, q: (0, q, 0))],
        out_specs=[pl.BlockSpec((tm, tn), lambda i, q: (i, q))],
        out_shape=[jax.ShapeDtypeStruct((m, kk), out_dtype)], scratch=[], args=[dy, w3], plan=plan)[0]


def _mm_tn(a, dy, n, *, name, out_dtype=BF16, tm=512, tn=None, k_tiles=None, plan=None):
    t, kk = a.shape
    j = dy.shape[1] // n
    tm, tn = min(tm, kk), n if tn is None else tn
    n_t = n // tn
    first, count = (0, kk // tm) if k_tiles is None else k_tiles
    return _call(
        functools.partial(_mm_body, TN, False), name=name,
        grid=(count, j * n_t),
        in_specs=[pl.BlockSpec((t, tm), lambda i, jj: (0, first + i)),
                  pl.BlockSpec((t, tn), lambda i, jj: (0, jj))],
        out_specs=[pl.BlockSpec((None, tm, tn), lambda i, jj: (jj // n_t, i, jj % n_t))],
        out_shape=[jax.ShapeDtypeStruct((j, count * tm, n), out_dtype)], scratch=[], args=[a, dy], plan=plan)[0]


def _rows(body, ins, outs, *, n_rows, tm, name, plan=None):
    tm = min(tm, n_rows)
    n_steps = n_rows // tm
    in_specs, args = [], []
    for arr, kind, width, block in ins:
        if kind == "row":
            in_specs.append(pl.BlockSpec((tm, width), functools.partial(lambda i, b: (i, b), b=block)))
        elif kind == "prev":
            in_specs.append(pl.BlockSpec((tm, width), functools.partial(lambda i, b: (jnp.maximum(i - 1, 0), b), b=block)))
        elif kind == "next":
            in_specs.append(pl.BlockSpec((tm, width), functools.partial(lambda i, b: (jnp.minimum(i + 1, n_steps - 1), b), b=block)))
        else:
            in_specs.append(pl.BlockSpec(arr.shape, functools.partial(lambda i, nd: (0,) * nd, nd=arr.ndim)))
        args.append(arr)
    out_specs, out_shape = [], []
    for shape, dtype, kind in outs:
        if kind == "row":
            out_specs.append(pl.BlockSpec((tm, shape[1]), lambda i: (i, 0)))
        else:
            out_specs.append(pl.BlockSpec(shape, functools.partial(lambda i, nd: (0,) * nd, nd=len(shape))))
        out_shape.append(jax.ShapeDtypeStruct(shape, dtype))

    def kern(*refs):
        body(pl.program_id(0), n_steps, *refs)

    return _call(kern, name=name, grid=(n_steps,), in_specs=in_specs, out_specs=out_specs, out_shape=out_shape,
                 scratch=[], args=args, plan=plan)


def _acc_rows(i, ref, value):
    @pl.when(i == 0)
    def _():
        ref[...] = jnp.zeros_like(ref)
    ref[...] += jnp.broadcast_to(value, ref.shape)


def _rms_fwd(x, g, *, name, tm=512):
    s, d = x.shape

    def body(i, n, x_ref, g_ref, h_ref):
        _rms_fwd_tail(x_ref[...], g_ref, h_ref)

    return _rows(body, [(x, "row", d, 0), (g, "full", 0, 0)], [((s, d), BF16, "row")], n_rows=s, tm=tm, name=name)[0]


def _rms_bwd(x, g, dh, dres, *, name, tm=512, plan=None):
    s, d = x.shape

    def body(i, n, x_ref, g_ref, dh_ref, dres_ref, dx_ref, dxb_ref, dg_ref):
        _rms_bwd_tail(i, dh_ref[...].astype(F32), x_ref, g_ref, dres_ref, dx_ref, dxb_ref, dg_ref)

    return _rows(body, [(x, "row", d, 0), (g, "full", 0, 0), (dh, "row", d, 0), (dres, "row", d, 0)],
                 [((s, d), F32, "row"), ((s, d), BF16, "row"), ((8, d), F32, "acc")],
                 n_rows=s, tm=tm, name=name, plan=plan)


def _mix_out_bwd(dx1b, w, br_a, br_b, proj, w_a, w_b, *, name, tm=512, plan=None):
    s, d = br_a.shape
    c = w_a.shape[0]
    tm = min(tm, s)

    def body(dy_ref, w_ref, a_ref, b_ref, ga_ref, gb_ref, wa_ref, wb_ref, da_ref, db_ref, dg_ref, doa_ref, dyb_ref):
        dm = _bdot(dy_ref[...], w_ref[...], NT)
        sa = _sigmoid(ga_ref[...].astype(F32))
        sb = _sigmoid(gb_ref[...].astype(F32))
        da_ref[...] = (dm * sa).astype(BF16)
        db_ref[...] = (dm * sb).astype(BF16)
        dg_ref[:, :d] = (dm * a_ref[...].astype(F32) * sa * (1.0 - sa)).astype(BF16)
        dg_ref[:, d:] = (dm * b_ref[...].astype(F32) * sb * (1.0 - sb)).astype(BF16)
        doa_ref[...] = _bdot(da_ref[...], wa_ref[...], NT).astype(BF16)
        dyb_ref[...] = _bdot(db_ref[...], wb_ref[...], NT).astype(BF16)

    row = pl.BlockSpec((tm, d), lambda i: (i, 0))
    narrow = pl.BlockSpec((tm, c), lambda i: (i, 0))
    whole = lambda arr: pl.BlockSpec(arr.shape, lambda i: (0,) * arr.ndim)
    return _call(body, name=name, grid=(s // tm,),
                 in_specs=[row, whole(w), row, row, pl.BlockSpec((tm, d), lambda i: (i, 3)),
                           pl.BlockSpec((tm, d), lambda i: (i, 4)), whole(w_a), whole(w_b)],
                 out_specs=[row, row, pl.BlockSpec((tm, 2 * d), lambda i: (i, 0)), narrow, narrow],
                 out_shape=[jax.ShapeDtypeStruct((s, d), BF16), jax.ShapeDtypeStruct((s, d), BF16),
                            jax.ShapeDtypeStruct((s, 2 * d), BF16), jax.ShapeDtypeStruct((s, c), BF16),
                            jax.ShapeDtypeStruct((s, c), BF16)],
                 scratch=[], args=[dx1b, w, br_a, br_b, proj, proj, w_a, w_b], plan=plan)


def _shift_down(cur, prev, k, first):
    row = lax.broadcasted_iota(jnp.int32, cur.shape, 0)
    out = jnp.where(row >= k, pltpu.roll(cur, k, 0), pltpu.roll(prev, k, 0))
    return jnp.where(jnp.logical_and(first, row < k), 0.0, out)


def _shift_up(cur, nxt, k, last):
    tm = cur.shape[0]
    row = lax.broadcasted_iota(jnp.int32, cur.shape, 0)
    out = jnp.where(row < tm - k, pltpu.roll(cur, tm - k, 0), pltpu.roll(nxt, tm - k, 0))
    return jnp.where(jnp.logical_and(last, row >= tm - k), 0.0, out)


def _conv_fwd(proj, conv_w, *, name, tm=512):
    s = proj.shape[0]
    c = CONV_WIDTH

    def body(i, n, u_ref, gb_ref, gc_ref, up_ref, gcp_ref, w_ref, y_ref):
        cu = gc_ref[...].astype(F32) * u_ref[...].astype(F32)
        cup = gcp_ref[...].astype(F32) * up_ref[...].astype(F32)
        first = i == 0
        y = (w_ref[0:1, :] * _shift_down(cu, cup, 2, first) + w_ref[1:2, :] * _shift_down(cu, cup, 1, first)
             + w_ref[2:3, :] * cu)
        y_ref[...] = (gb_ref[...].astype(F32) * y).astype(BF16)

    return _rows(body, [(proj, "row", c, 3), (proj, "row", c, 4), (proj, "row", c, 5),
                        (proj, "prev", c, 3), (proj, "prev", c, 5), (conv_w, "full", 0, 0)],
                 [((s, c), BF16, "row")], n_rows=s, tm=tm, name=name)[0]


def _conv_bwd(dy_b, proj, conv_w, *, name, tm=512, plan=None):
    s = proj.shape[0]
    c = CONV_WIDTH

    def body(i, n, dy_ref, u_ref, gb_ref, gc_ref, up_ref, gcp_ref, dyn_ref, gbn_ref, w_ref, d_ref, dw_ref):
        first, last = i == 0, i == n - 1
        u = u_ref[...].astype(F32)
        gb = gb_ref[...].astype(F32)
        gc = gc_ref[...].astype(F32)
        cu = gc * u
        cup = gcp_ref[...].astype(F32) * up_ref[...].astype(F32)
        cu1 = _shift_down(cu, cup, 1, first)
        cu2 = _shift_down(cu, cup, 2, first)
        conv = w_ref[0:1, :] * cu2 + w_ref[1:2, :] * cu1 + w_ref[2:3, :] * cu
        dy = dy_ref[...].astype(F32)
        dyc = dy * gb
        dycn = dyn_ref[...].astype(F32) * gbn_ref[...].astype(F32)
        dcu = (w_ref[2:3, :] * dyc + w_ref[1:2, :] * _shift_up(dyc, dycn, 1, last)
               + w_ref[0:1, :] * _shift_up(dyc, dycn, 2, last))
        d_ref[:, 0:c] = (dcu * gc).astype(BF16)
        d_ref[:, c:2 * c] = (dy * conv).astype(BF16)
        d_ref[:, 2 * c:3 * c] = (dcu * u).astype(BF16)
        row = lax.broadcasted_iota(jnp.int32, (8, c), 0)
        dw = (jnp.where(row == 0, jnp.sum(dyc * cu2, axis=0, keepdims=True), 0.0)
              + jnp.where(row == 1, jnp.sum(dyc * cu1, axis=0, keepdims=True), 0.0)
              + jnp.where(row == 2, jnp.sum(dyc * cu, axis=0, keepdims=True), 0.0))

        @pl.when(first)
        def _():
            dw_ref[...] = jnp.zeros_like(dw_ref)
        dw_ref[...] += dw

    return _rows(body, [(dy_b, "row", c, 0), (proj, "row", c, 3), (proj, "row", c, 4), (proj, "row", c, 5),
                        (proj, "prev", c, 3), (proj, "prev", c, 5), (dy_b, "next", c, 0), (proj, "next", c, 4),
                        (conv_w, "full", 0, 0)],
                 [((s, 3 * c), BF16, "row"), ((8, c), F32, "acc")], n_rows=s, tm=tm, name=name, plan=plan)


def _mem_probs(q, k, scale):
    sc = _bdot(q, k, NT) * scale
    sc = sc - jnp.max(sc, axis=-1, keepdims=True)
    p = jnp.exp(sc)
    return p / jnp.sum(p, axis=-1, keepdims=True)


def _mem_sublayer(hq, w_q, kv, w_o, x, g, *, name, tm=512, plan=None):
    s, d = hq.shape
    hd = d // MEM_HEADS
    scale = 1.0 / math.sqrt(hd)
    tm = min(tm, s)

    def body(hq_ref, wq_ref, kv_ref, wo_ref, x_ref, g_ref, q_ref, o_ref, x2_ref, h_ref):
        q_ref[...] = _bdot(hq_ref[...], wq_ref[...], NN).astype(BF16)
        for h in range(MEM_HEADS):
            cols = slice(h * hd, (h + 1) * hd)
            p = _mem_probs(q_ref[:, cols], kv_ref[:, cols], scale)
            o_ref[:, cols] = _bdot(p, kv_ref[:, d + h * hd:d + (h + 1) * hd], NN).astype(BF16)
        xv = _bdot(o_ref[...], wo_ref[...], NN) + x_ref[...]
        x2_ref[...] = xv
        _rms_fwd_tail(xv, g_ref, h_ref)

    row = pl.BlockSpec((tm, d), lambda i: (i, 0))
    whole = lambda a: pl.BlockSpec(a.shape, lambda i: (0,) * a.ndim)
    return _call(body, name=name, grid=(s // tm,),
                 in_specs=[row, whole(w_q), whole(kv), whole(w_o), row, whole(g)], out_specs=[row] * 4,
                 out_shape=[jax.ShapeDtypeStruct((s, d), BF16), jax.ShapeDtypeStruct((s, d), BF16),
                            jax.ShapeDtypeStruct((s, d), F32), jax.ShapeDtypeStruct((s, d), BF16)],
                 scratch=[], args=[hq, w_q, kv, w_o, x, g], plan=plan)


def _mem_sublayer_bwd(dx2b, dx2, x, g, qm, kv, w_q, w_o, *, name, tm=512, plan=None):
    s, d = qm.shape
    hd = d // MEM_HEADS
    scale = 1.0 / math.sqrt(hd)
    tm = min(tm, s)

    def body(dyb_ref, dres_ref, x_ref, g_ref, q_ref, kv_ref, wq_ref, wo_ref, dx_ref, dxb_ref, dg_ref, dq_ref, dkv_ref):
        i = pl.program_id(0)

        @pl.when(i == 0)
        def _():
            dkv_ref[...] = jnp.zeros_like(dkv_ref)
        dom = _bdot(dyb_ref[...], wo_ref[...], NT).astype(BF16)
        for h in range(MEM_HEADS):
            cols = slice(h * hd, (h + 1) * hd)
            vcols = slice(d + h * hd, d + (h + 1) * hd)
            q, k, v, do = q_ref[:, cols], kv_ref[:, cols], kv_ref[:, vcols], dom[:, cols]
            p = _mem_probs(q, k, scale)
            dp = _bdot(do, v, NT)
            ds = p * (dp - jnp.sum(dp * p, axis=-1, keepdims=True)) * scale
            dq_ref[:, cols] = _bdot(ds, k, NN).astype(BF16)
            dkv_ref[:, cols] += _bdot(ds, q, TN)
            dkv_ref[:, vcols] += _bdot(p, do, TN)
        dh = _bdot(dq_ref[...], wq_ref[...], NT)
        _rms_bwd_tail(i, dh, x_ref, g_ref, dres_ref, dx_ref, dxb_ref, dg_ref)

    row = pl.BlockSpec((tm, d), lambda i: (i, 0))
    whole = lambda a: pl.BlockSpec(a.shape, lambda i: (0,) * a.ndim)
    return _call(body, name=name, grid=(s // tm,),
                 in_specs=[row, row, row, whole(g), row, whole(kv), whole(w_q), whole(w_o)],
                 out_specs=[row, row, pl.BlockSpec((8, d), lambda i: (0, 0)), row, whole(kv)],
                 out_shape=[jax.ShapeDtypeStruct((s, d), F32), jax.ShapeDtypeStruct((s, d), BF16),
                            jax.ShapeDtypeStruct((8, d), F32), jax.ShapeDtypeStruct((s, d), BF16),
                            jax.ShapeDtypeStruct(kv.shape, F32)],
                 scratch=[], args=[dx2b, dx2, x, g, qm, kv, w_q, w_o], plan=plan)


def _sb_consts(t):
    row = lax.broadcasted_iota(jnp.int32, (t, t), 0)
    col = lax.broadcasted_iota(jnp.int32, (t, t), 1)
    lane = lax.broadcasted_iota(jnp.int32, (t, LANES), 1)
    return row, col, lane < SB_HEAD_DIM


def _sb_logits(q, k):
    z2 = jnp.minimum(_bdot(q, k, NT) * LOG2_E, SB_CLAMP)
    return z2, jnp.exp2(z2)


def _tri_sum(v, tri):
    hi = v.astype(BF16)
    lo = (v - hi.astype(F32)).astype(BF16)
    return _bdot(hi, tri, NN) + _bdot(lo, tri, NN)


def _sb_fwd(proj, *, name, plan=None):
    s = proj.shape[0]
    t, nh = SB_TILE, SB_STEP_HEADS
    n_q = s // t
    scale = 1.0 / math.sqrt(SB_HEAD_DIM)

    def body(q_ref, k_ref, v_ref, o_ref, c_ref, first_ref, acc_ref, c_scr):
        i = pl.program_id(1)
        row, col, head0 = _sb_consts(t)
        later = (row > col).astype(BF16)
        valid = col < row
        lanes = lambda h: slice((h // 2) * LANES, (h // 2 + 1) * LANES)
        q = [jnp.where(head0 == (h % 2 == 0), q_ref[:, lanes(h)] * scale, 0) for h in range(nh)]

        def tiles(kbs, diag_first, carry):
            rows = [pl.ds(pl.multiple_of(kb * t, t), t) for kb in kbs]
            jobs = [(n, h) for n in range(len(kbs)) for h in range(nh)]
            masked = lambda n: diag_first and n == 0
            zs = {(n, h): _sb_logits(q[h], k_ref[rows[n], lanes(h)]) for n, h in jobs}
            fail = {j: jnp.log2(1.0 + zs[j][1]) for j in jobs}
            fail = {j: jnp.where(valid, fail[j], 0.0) if masked(j[0]) else fail[j] for j in jobs}
            cum = {j: _tri_sum(fail[j], later) for j in jobs}
            run, before = list(carry), {}
            for n, h in jobs:
                before[n, h] = run[h]
                run[h] = run[h] + cum[n, h][:, 0:1] + fail[n, h][:, 0:1]
            w = {j: jnp.exp2(zs[j][0] - fail[j] - cum[j] - before[j]) for j in jobs}
            w = {j: jnp.where(valid, w[j], 0.0) if masked(j[0]) else w[j] for j in jobs}
            for n, h in jobs:
                acc_ref[h] += _bdot(w[n, h], v_ref[rows[n], lanes(h)], NN)
            return tuple(run)

        acc_ref[...] = jnp.zeros_like(acc_ref)
        zero = (jnp.zeros((t, 1), F32),) * nh

        def alive(carry):
            return (functools.reduce(jnp.minimum, [jnp.min(c) for c in carry]) < SB_DEAD).astype(jnp.int32)

        def step(state):
            new = tiles([state[0]], False, state[2:])
            return (state[0] - 1, alive(new)) + new

        @pl.when(i == 0)
        def _():
            for h, c in enumerate(tiles([i], True, zero)):
                c_scr[h] = c

        @pl.when(i > 0)
        def _():
            for h, c in enumerate(tiles([i, i - 1], True, zero)):
                c_scr[h] = c
        carry = tuple(c_scr[h] for h in range(nh))
        state = lax.while_loop(lambda st: jnp.logical_and(st[0] >= 0, st[1] > 0), step, (i - 2, alive(carry)) + carry)
        for b in range(nh // 2):
            o_ref[:, b * LANES:(b + 1) * LANES] = jnp.where(head0, acc_ref[2 * b], acc_ref[2 * b + 1]).astype(BF16)
        head = lax.broadcasted_iota(jnp.int32, (t, nh), 1)
        c_ref[...] = sum(jnp.where(head == h, state[2 + h], 0.0) for h in range(nh))
        first_ref[pl.program_id(0), i] = (jnp.maximum(state[0], -1) + 1).astype(F32)

    n_p, width = SB_HEADS // nh, nh * SB_HEAD_DIM
    k_blk, v_blk = SB_WIDTH // width, 2 * SB_WIDTH // width
    return _call(
        body, name=name, grid=(n_p, n_q),
        in_specs=[pl.BlockSpec((t, width), lambda p, i: (i, p)),
                  pl.BlockSpec((s, width), lambda p, i: (0, k_blk + p)),
                  pl.BlockSpec((s, width), lambda p, i: (0, v_blk + p))],
        out_specs=[pl.BlockSpec((t, width), lambda p, i: (i, p)),
                   pl.BlockSpec((None, t, nh), lambda p, i: (p, i, 0)),
                   pl.BlockSpec(memory_space=pltpu.SMEM)],
        out_shape=[jax.ShapeDtypeStruct((s, SB_WIDTH), BF16), jax.ShapeDtypeStruct((n_p, s, nh), F32),
                   jax.ShapeDtypeStruct((n_p, n_q), F32)],
        scratch=[pltpu.VMEM((nh, t, LANES), F32), pltpu.VMEM((nh, t, 1), F32)], args=[proj, proj, proj], plan=plan)


def _sb_bwd(proj, do_a, ctot, first, *, name, plan=None):
    s = proj.shape[0]
    t, nh = SB_TILE, SB_STEP_HEADS
    n_q = s // t
    scale = 1.0 / math.sqrt(SB_HEAD_DIM)

    def body(q_ref, k_ref, v_ref, do_ref, c_ref, first_ref, dq_ref, dk_ref, dv_ref, dq_acc, dk_acc, dv_acc):
        i = pl.program_id(1)
        kb0 = jnp.clip(first_ref[pl.program_id(0), i].astype(jnp.int32), 0, i)
        row, col, head0 = _sb_consts(t)
        upto = (row <= col).astype(BF16)
        before = (row < col).astype(BF16)
        valid = col < row
        lanes = lambda h: slice((h // 2) * LANES, (h // 2 + 1) * LANES)
        q2 = [jnp.where(head0 == (h % 2 == 0), q_ref[:, lanes(h)] * scale, 0) for h in range(nh)]
        do2 = [jnp.where(head0 == (h % 2 == 0), do_ref[:, lanes(h)], 0) for h in range(nh)]
        ctot2 = [c_ref[:, h:h + 1] for h in range(nh)]

        @pl.when(i == 0)
        def _():
            dk_acc[...] = jnp.zeros_like(dk_acc)
            dv_acc[...] = jnp.zeros_like(dv_acc)
        dq_acc[...] = jnp.zeros_like(dq_acc)

        def tiles(kbs, diag_last, carry):
            rows = [pl.ds(pl.multiple_of(kb * t, t), t) for kb in kbs]
            kt = {(n, h): k_ref[rows[n], lanes(h)] for n in range(len(kbs)) for h in range(nh)}
            jobs = list(kt)
            masked = lambda n: diag_last and n == len(kbs) - 1
            t_last = slice(t - 1, t)
            zs = {(n, h): _sb_logits(q2[h], kt[n, h]) for n, h in jobs}
            dw = {(n, h): _bdot(do2[h], v_ref[rows[n], lanes(h)], NT) for n, h in jobs}
            fail = {j: jnp.log2(1.0 + zs[j][1]) for j in jobs}
            fail = {j: jnp.where(valid, fail[j], 0.0) if masked(j[0]) else fail[j] for j in jobs}
            cum = {j: _tri_sum(fail[j], upto) for j in jobs}
            miss = {j: jnp.exp2(-fail[j]) for j in jobs}
            beta = {j: zs[j][1] * miss[j] for j in jobs}
            fail_run, fail_before = list(carry[0::2]), {}
            for n, h in jobs:
                fail_before[n, h] = fail_run[h]
                fail_run[h] = fail_run[h] + cum[n, h][:, t_last]
            w = {(n, h): beta[n, h] * jnp.exp2(fail_before[n, h] + cum[n, h] - ctot2[h]) for n, h in jobs}
            w = {j: jnp.where(valid, w[j], 0.0) if masked(j[0]) else w[j] for j in jobs}
            g = {j: w[j] * dw[j] for j in jobs}
            g_local = {j: _bdot(g[j], before, NN) for j in jobs}
            for n, h in jobs:
                dv_acc[rows[n], lanes(h)] += _bdot(w[n, h], do2[h], TN)
            g_run, dz = list(carry[1::2]), {}
            for n, h in jobs:
                g_sum = g_run[h] + g_local[n, h]
                dz[n, h] = g[n, h] * miss[n, h] - beta[n, h] * g_sum
                g_run[h] = g_sum[:, t_last] + g[n, h][:, t_last]
            dz = {j: jnp.where(valid, dz[j], 0.0) if masked(j[0]) else dz[j] for j in jobs}
            for n, h in jobs:
                dq_acc[h] += _bdot(dz[n, h], kt[n, h], NN)
                dk_acc[rows[n], lanes(h)] += _bdot(dz[n, h], q2[h], TN)
            return tuple(v for pair in zip(fail_run, g_run) for v in pair)

        zero = jnp.zeros((t, 1), F32)
        carry = lax.fori_loop(kb0, i - 1, lambda n, c: tiles([n], False, c), (zero,) * (2 * nh))

        @pl.when(i == 0)
        def _():
            tiles([i], True, carry)

        @pl.when(i > 0)
        def _():
            tiles([i - 1, i], True, carry)
        for b in range(nh // 2):
            dq_ref[:, b * LANES:(b + 1) * LANES] = (jnp.where(head0, dq_acc[2 * b], dq_acc[2 * b + 1])
                                                    * scale).astype(BF16)

        @pl.when(i == n_q - 1)
        def _():
            dk_ref[...] = dk_acc[...].astype(BF16)
            dv_ref[...] = dv_acc[...].astype(BF16)

    n_p, width = SB_HEADS // nh, nh * SB_HEAD_DIM
    k_blk, v_blk = SB_WIDTH // width, 2 * SB_WIDTH // width
    outs = _call(
        body, name=name, grid=(n_p, n_q),
        in_specs=[pl.BlockSpec((t, width), lambda p, i: (i, p)),
                  pl.BlockSpec((s, width), lambda p, i: (0, k_blk + p)),
                  pl.BlockSpec((s, width), lambda p, i: (0, v_blk + p)),
                  pl.BlockSpec((t, width), lambda p, i: (i, p)),
                  pl.BlockSpec((None, t, nh), lambda p, i: (p, i, 0)),
                  pl.BlockSpec(memory_space=pltpu.SMEM)],
        out_specs=[pl.BlockSpec((t, width), lambda p, i: (i, p)),
                   pl.BlockSpec((s, width), lambda p, i: (0, p)),
                   pl.BlockSpec((s, width), lambda p, i: (0, p))],
        out_shape=[jax.ShapeDtypeStruct((s, SB_WIDTH), BF16)] * 3,
        scratch=[pltpu.VMEM((nh, t, LANES), F32), pltpu.VMEM((s, width), F32), pltpu.VMEM((s, width), F32)],
        args=[proj, proj, proj, do_a, ctot, first], plan=plan)
    return jnp.concatenate(outs, axis=1)


def _mm_gathered(a, key, plan, *, name, out3=False, w_t=False):
    src = plan.gathering(key)
    if src is None:
        return _mm_nn(a, plan.weight(key), name=name, out3=out3, w_t=w_t, plan=plan)
    out, w_all = _mm_gathering(a, src, name=name, out3=out3, w_t=w_t)
    plan.set_weight(key, w_all)
    return out


def _local_step(x, mem, target, gains, plan):
    g_mix, g_memq, g_memkv, g_ffn, g_fin = gains
    d = x.shape[1]

    h0 = _rms_fwd(x, g_mix, name="rms_mix")
    proj = _mm_gathered(h0, "in", plan, name="mm_in")
    w_in = plan.weight("in")
    o_a, ctot, first = _sb_fwd(proj, name="sb_fwd", plan=plan)
    conv_w = plan.weight("conv")
    y_b = _conv_fwd(proj, conv_w, name="conv_fwd")
    w_a, w_b, w_mix = plan.weight("a"), plan.weight("b"), plan.weight("mix")
    x1, hq, merged, br_a, br_b = _mix_out(o_a, y_b, proj, w_a[0], w_b[0], w_mix[0], x, g_memq, name="mm_mix", plan=plan)
    w_mq, w_kv, w_mo = plan.weight("mq")[0], plan.weight("kv"), plan.weight("mo")[0]
    mn = _rms_fwd(mem, g_memkv, name="rms_memkv")
    kv = _mm_nn(mn, w_kv, name="mm_memkv")
    qm, om, x2, hf = _mem_sublayer(hq, w_mq, kv, w_mo, x1, g_ffn, name="mem_sublayer", plan=plan)
    gu = _mm_gathered(hf, "fi", plan, name="mm_ffn_in", out3=True, w_t=True)
    w_fi, w_fo = plan.weight("fi"), plan.weight("fo")
    dx3, dx3b, dg_fin, loss, act = _ffn_out_loss(gu, w_fo, x2, g_fin, target, name="mm_ffn_out")

    plan.grad("fo", _mm_tn_a3(act, dx3b, name="mm_d_w_ffn_out"))
    dgu = _ffn_out_bwd(dx3b, w_fo, gu, name="mm_d_act")
    plan.grad("fi", _mm_tn_a3(dgu, hf, name="mm_d_w_ffn_in"))
    dx2, dx2b, dg_ffn = _mm_nt_rms(dgu, w_fi, x2, g_ffn, dx3, name="mm_d_hf", dy3=True, w_nn=True, plan=plan)

    plan.grad("mo", _mm_tn(om, dx2b, d, name="mm_d_w_memo"))
    dx1, dx1b, dg_memq, dqm, dkv = _mem_sublayer_bwd(dx2b, dx2, x1, g_memq, qm, kv, w_mq, w_mo, name="mem_sublayer_bwd",
                                                    plan=plan)
    plan.grad("mq", _mm_tn(hq, dqm, d, name="mm_d_w_memq"))
    plan.grad("kv", _mm_tn(mn, dkv, w_kv.shape[2], name="mm_d_w_memkv"))
    _, _, dg_memkv = _mm_nt_rms(dkv, w_kv, mem, g_memkv, None, name="mm_d_mn")

    plan.grad("mix", _mm_tn(merged, dx1b, d, name="mm_d_w_mix"))
    dbr_a, dbr_b, dgab, do_a, dy_b = _mix_out_bwd(dx1b, w_mix[0], br_a, br_b, proj, w_a[0], w_b[0], name="mm_d_merged",
                                                 plan=plan)
    plan.grad("a", _mm_tn(o_a, dbr_a, d, name="mm_d_w_branch_a"))
    plan.grad("b", _mm_tn(y_b, dbr_b, d, name="mm_d_w_branch_b"))
    dconv, dconv_w = _conv_bwd(dy_b, proj, conv_w, name="conv_bwd", plan=plan)
    dqkv = _sb_bwd(proj, do_a, ctot, first, name="sb_bwd", plan=plan)
    dproj = jnp.concatenate([dqkv, dconv, dgab], axis=1)
    rows_in1 = d // IN_SPLIT[1] * (IN_SPLIT[1] - IN_SPLIT[0])
    plan.grad("in0", _mm_tn(h0, dproj, w_in.shape[2], name="mm_d_w_in0", tm=d - rows_in1, k_tiles=(0, 1)))
    plan.grad("in1", _mm_tn(h0, dproj, w_in.shape[2], name="mm_d_w_in1", tm=rows_in1,
                            k_tiles=(d // rows_in1 - 1, 1), plan=plan))
    dh0 = _mm_nt(dproj, w_in, name="mm_d_h0", out_dtype=F32, plan=plan)
    dx0, _, dg_mix = _rms_bwd(x, g_mix, dh0, dx1, name="rms_mix_bwd", plan=plan)

    return dx0, (dg_mix, dg_memq, dg_memkv, dg_ffn, dg_fin, dconv_w, loss)


def _row_tile(a, target=512):
    tm = min(a, target)
    while a % tm:
        tm -= 8
    return tm


def _sum_with_sibling(parts, recvs, core, *, name):
    n = len(parts)

    def body(core_ref, *refs):
        for p_ref, r_ref, o_ref in zip(refs[:n], refs[n:2 * n], refs[2 * n:]):
            o_ref[...] = (p_ref[...].astype(F32) + r_ref[...].astype(F32)).astype(o_ref.dtype)

    mine = [pl.BlockSpec((None,) + p.shape[1:], lambda q, core_ref: (2 * q + core_ref[0], 0, 0)) for p in parts]
    other = [pl.BlockSpec((None,) + p.shape[1:], lambda q, core_ref: (q, 0, 0)) for p in parts]
    return pl.pallas_call(
        body, name=name,
        grid_spec=pltpu.PrefetchScalarGridSpec(num_scalar_prefetch=1, grid=(N_CHIP,), in_specs=mine + other,
                                               out_specs=other),
        out_shape=[jax.ShapeDtypeStruct((N_CHIP,) + p.shape[1:], p.dtype) for p in parts],
        compiler_params=_params(1))(core, *parts, *recvs)


def _adam_math(wv, g, m, v):
    m = ADAM_B1 * m + (1.0 - ADAM_B1) * g
    v = ADAM_B2 * v + (1.0 - ADAM_B2) * (g * g)
    m_hat = m / (1.0 - ADAM_B1 ** ADAM_STEP)
    v_hat = v / (1.0 - ADAM_B2 ** ADAM_STEP)
    delta = -ADAM_LR * (m_hat / (jnp.sqrt(v_hat) + ADAM_EPS) + ADAM_WD * wv)
    return delta, m, v


def _adam_sharded(wv, m, v, own, recv, chip, *, name):
    a, b = wv.shape
    tm = _row_tile(a)

    def body(chip_ref, w_ref, m_ref, v_ref, own_ref, recv_ref, g_ref, d_ref, nm_ref, nv_ref):
        g = own_ref[...].astype(F32)
        for j in range(3):
            g = g + recv_ref[j].astype(F32)
        delta, nm, nv = _adam_math(w_ref[...], g, m_ref[...], v_ref[...])
        g_ref[...] = g
        d_ref[...] = delta
        nm_ref[...] = nm
        nv_ref[...] = nv

    tile = pl.BlockSpec((tm, b), lambda i, chip_ref: (i, 0))
    return pl.pallas_call(
        body, name=name,
        grid_spec=pltpu.PrefetchScalarGridSpec(
            num_scalar_prefetch=1, grid=(a // tm,),
            in_specs=[tile, tile, tile,
                      pl.BlockSpec((None, tm, b), lambda i, chip_ref: (chip_ref[0], i, 0)),
                      pl.BlockSpec((3, tm, b), lambda i, chip_ref: (0, i, 0))],
            out_specs=[tile] * 4),
        out_shape=[jax.ShapeDtypeStruct((a, b), F32)] * 4, compiler_params=_params(1))(chip, wv, m, v, own, recv)


def _sum_devices(gathered, *, name):
    _, r, c = gathered.shape

    def body(g_ref, o_ref):
        total = g_ref[0]
        for j in range(1, N_DEV):
            total = total + g_ref[j]
        o_ref[...] = total

    return pl.pallas_call(body, name=name, out_shape=jax.ShapeDtypeStruct((r, c), F32))(gathered)


def _adam_small(wv, g, m, v, *, name):
    def body(w_ref, g_ref, m_ref, v_ref, d_ref, nm_ref, nv_ref):
        delta, nm, nv = _adam_math(w_ref[...], g_ref[...], m_ref[...], v_ref[...])
        d_ref[...] = delta
        nm_ref[...] = nm
        nv_ref[...] = nv

    return pl.pallas_call(body, name=name, out_shape=[jax.ShapeDtypeStruct(wv.shape, F32)] * 3)(wv, g, m, v)


BIG = ("in", "a", "b", "mix", "mq", "kv", "mo", "fi", "fo")
ROW_SHARDED = ("mix", "mq", "mo")
UNSHARDED = ("a", "b")
FFN_GROUPS = 4
IN_SPLIT = (3, 4)
SMALL_ROWS = 16


class _Plan:
    FUSED = ("in",)
    GATHER_ON = {"sb_fwd": ("a", "b", "mix", "mq", "mo", "conv", "fi0"), "mm_mix": ("kv",), "mem_sublayer": ("fi1",),
                 "mm_ffn_in": ("fo",)}
    SIBLING_ON = {"mm_d_hf": ("fo", "fi"), "mm_d_merged": ("mo", "mq", "kv"), "conv_bwd": ("mix", "a", "b"),
                  "mm_d_w_in1": ("in0",), "mm_d_h0": ("in1",)}
    CHIPS_ON = {"mem_sublayer_bwd": ("fo",), "sb_bwd": ("fi", "mo", "mq", "kv", "mix", "a", "b"), "mm_d_h0": ("in0",),
                "rms_mix_bwd": ("in1",)}

    def __init__(self, shards, core):
        self.shards, self.core = shards, core
        self.w, self.parts, self.chip_sums, self.from_chips = {}, {}, {}, {}

    def gathering(self, k):
        return self.shards[k] if k in self.FUSED else None

    def comm(self, name):
        comms = []
        if name in self.GATHER_ON:
            comms.append(_gather_comm([self.shards[k] for k in self.GATHER_ON[name]]))
        if name in self.SIBLING_ON:
            comms.append(_sibling_comm([self.parts[k] for k in self.SIBLING_ON[name]]))
        if name in self.CHIPS_ON:
            comms.append(_chips_comm([self.chip_sums[k] for k in self.CHIPS_ON[name]]))
        return _join_comms(comms) if comms else None

    def landed(self, name, outs):
        outs = list(outs)
        for k in self.GATHER_ON.get(name, ()):
            self.set_weight(k, outs.pop(0))
        keys = self.SIBLING_ON.get(name, ())
        if keys:
            sums = _sum_with_sibling([self.parts[k] for k in keys], [outs.pop(0) for _ in keys], self.core,
                                     name="sum_with_sibling_" + "_".join(keys))
            self.chip_sums.update(zip(keys, sums))
        for k in self.CHIPS_ON.get(name, ()):
            self.from_chips[k] = outs.pop(0)

    def set_weight(self, k, gathered):
        _, a, b = gathered.shape
        if k in ROW_SHARDED:
            gathered = gathered.reshape(1, N_DEV * a, b)
        elif k in UNSHARDED:
            gathered = jnp.transpose(gathered, (1, 0, 2)).reshape(1, a, N_DEV * b)
        elif k == "fo":
            gathered = gathered.reshape(FFN_GROUPS, N_DEV * a // FFN_GROUPS, b)
        elif k == "conv":
            n_conv = CONV_WIDTH // N_DEV
            gathered = jnp.transpose(gathered[:, :3, :n_conv], (1, 0, 2)).reshape(3, CONV_WIDTH)
        self.w[k] = gathered
        if k == "fi1":
            self.w["fi"] = jnp.concatenate([self.w["fi0"], gathered], axis=2)

    def weight(self, k):
        return self.w[k]

    def grad(self, k, g):
        _, a, b = g.shape
        if k in ROW_SHARDED:
            g = g.reshape(N_DEV, a // N_DEV, b)
        elif k in UNSHARDED:
            g = jnp.transpose(g.reshape(a, N_DEV, b // N_DEV), (1, 0, 2))
        elif k == "fo":
            g = g.reshape(N_DEV, FFN_GROUPS * a // N_DEV, b)
        self.parts[k] = g


def kernel(x, mem, norm_mix, w_in, conv_w, w_branch_a, w_branch_b, w_mix_out, norm_mem_q, norm_mem_kv, w_mem_q, w_mem_kv, w_mem_o, norm_ffn, w_ffn_in, w_ffn_out, norm_final, loss_target, m_norm_mix, m_w_in, m_conv_w, m_w_branch_a, m_w_branch_b, m_w_mix_out, m_norm_mem_q, m_norm_mem_kv, m_w_mem_q, m_w_mem_kv, m_w_mem_o, m_norm_ffn, m_w_ffn_in, m_w_ffn_out, m_norm_final, v_norm_mix, v_w_in, v_conv_w, v_w_branch_a, v_w_branch_b, v_w_mix_out, v_norm_mem_q, v_norm_mem_kv, v_w_mem_q, v_w_mem_kv, v_w_mem_o, v_norm_ffn, v_w_ffn_in, v_w_ffn_out, v_norm_final):
    d = x.shape[-1]
    xi, yi, ci = lax.axis_index("x"), lax.axis_index("y"), lax.axis_index("c")
    chip = jnp.reshape(2 * xi + yi, (1,)).astype(jnp.int32)
    dev = 4 * xi + 2 * yi + ci

    big_w = dict(zip(BIG, (w_in, w_branch_a, w_branch_b, w_mix_out, w_mem_q, w_mem_kv, w_mem_o, w_ffn_in, w_ffn_out)))
    big_m = dict(zip(BIG, (m_w_in, m_w_branch_a, m_w_branch_b, m_w_mix_out, m_w_mem_q, m_w_mem_kv, m_w_mem_o, m_w_ffn_in, m_w_ffn_out)))
    big_v = dict(zip(BIG, (v_w_in, v_w_branch_a, v_w_branch_b, v_w_mix_out, v_w_mem_q, v_w_mem_kv, v_w_mem_o, v_w_ffn_in, v_w_ffn_out)))

    flip = lambda t, k: jnp.transpose(t) if k == "fi" else t
    shards = {k: flip(big_w[k][0], k).astype(BF16) for k in BIG}
    shards["fi0"], shards["fi1"] = shards["fi"][:, :d // 2], shards["fi"][:, d // 2:]
    n_conv = conv_w.shape[-1]
    shards["conv"] = jnp.zeros((8, LANES), F32).at[:3, :n_conv].set(conv_w[0])
    plan = _Plan(shards, jnp.reshape(ci, (1,)).astype(jnp.int32))

    gains = (norm_mix, norm_mem_q, norm_mem_kv, norm_ffn, norm_final.reshape(1, d))
    dx0, small = _local_step(x[0], mem[0], loss_target[0], gains, plan)

    grads, deltas, new_m, new_v = {}, {}, {}, {}
    for k in BIG:
        lead = big_w[k].shape
        wv, mv, vv = flip(big_w[k][0], k), flip(big_m[k][0], k), flip(big_v[k][0], k)
        if k == "in":
            half = wv.shape[0] * IN_SPLIT[0] // IN_SPLIT[1]
            lo = _adam_sharded(wv[:half], mv[:half], vv[:half], plan.chip_sums["in0"], plan.from_chips["in0"], chip,
                               name="adam_in0")
            hi = _adam_sharded(wv[half:], mv[half:], vv[half:], plan.chip_sums["in1"], plan.from_chips["in1"], chip,
                               name="adam_in1")
            outs = [jnp.concatenate(pair, axis=0) for pair in zip(lo, hi)]
        else:
            outs = _adam_sharded(wv, mv, vv, plan.chip_sums[k], plan.from_chips[k], chip, name="adam_" + k)
        grads[k], deltas[k], new_m[k], new_v[k] = (flip(t, k).reshape(lead) for t in outs)

    dg_mix, dg_memq, dg_memkv, dg_ffn, dg_fin, dconv_w, loss = small
    conv_rows = jnp.zeros((3, d), F32).at[:, :CONV_WIDTH].set(dconv_w[:3])
    block = jnp.concatenate([dg_mix[:1], dg_memq[:1], dg_memkv[:1], dg_ffn[:1], dg_fin[:1], conv_rows,
                             jnp.broadcast_to(loss[:1, :1], (1, d)), jnp.zeros((SMALL_ROWS - 9, d), F32)], axis=0)
    total = _sum_devices(_exchange(_gather_comm([block]), name="gather_small")[0], name="sum_small")
    g_conv = lax.dynamic_slice(total[5:8, :CONV_WIDTH], (0, dev * n_conv), (3, n_conv))
    small_w = [norm_mix, norm_mem_q, norm_mem_kv, norm_ffn, norm_final.reshape(1, d), conv_w[0]]
    small_m = [m_norm_mix, m_norm_mem_q, m_norm_mem_kv, m_norm_ffn, m_norm_final.reshape(1, d), m_conv_w[0]]
    small_v = [v_norm_mix, v_norm_mem_q, v_norm_mem_kv, v_norm_ffn, v_norm_final.reshape(1, d), v_conv_w[0]]
    small_g = [total[0:1], total[1:2], total[2:3], total[3:4], total[4:5], g_conv]
    small_names = ["norm_mix", "norm_mem_q", "norm_mem_kv", "norm_ffn", "norm_final", "conv_w"]
    sg, sd, sm, sv = {}, {}, {}, {}
    for nme, wv, g, m, v in zip(small_names, small_w, small_g, small_m, small_v):
        dl, nm, nv = _adam_small(wv, g, m, v, name="adam_" + nme)
        shape = norm_final.shape if nme == "norm_final" else (conv_w.shape if nme == "conv_w" else wv.shape)
        sg[nme], sd[nme], sm[nme], sv[nme] = (t.reshape(shape) for t in (g, dl, nm, nv))

    def ordered(big, sml):
        return (sml["norm_mix"], big["in"], sml["conv_w"], big["a"], big["b"], big["mix"], sml["norm_mem_q"],
                sml["norm_mem_kv"], big["mq"], big["kv"], big["mo"], sml["norm_ffn"], big["fi"], big["fo"],
                sml["norm_final"])

    loss_out = total[8, 0]
    grad_x = dx0.reshape(x.shape)
    return (loss_out, grad_x, *ordered(grads, sg), *ordered(deltas, sd), *ordered(new_m, sm), *ordered(new_v, sv))
```

```python
import functools
import math

import jax
import jax.numpy as jnp
from jax import lax
from jax.experimental import pallas as pl
from jax.experimental.pallas import tpu as pltpu

F32 = jnp.float32
BF16 = jnp.bfloat16
MESH = pl.DeviceIdType.MESH

N_DEV = 8
N_CHIP = 4
NORM_EPS = 1e-6
SB_HEADS = 8
SB_HEAD_DIM = 64
SB_WIDTH = SB_HEADS * SB_HEAD_DIM
CONV_WIDTH = 512
MEM_HEADS = 4
ADAM_LR = 0.001
ADAM_B1 = 0.9
ADAM_B2 = 0.999
ADAM_EPS = 1e-08
ADAM_WD = 0.01
ADAM_STEP = 10

LANES = 128
VMEM_LIMIT_BYTES = 52 * 1024 * 1024
SB_TILE = 256
SB_STEP_HEADS = 4
SB_DEAD = 159.0
SB_CLAMP = 126.0
LOG2_E = 1.4426950408889634

ANY = pl.BlockSpec(memory_space=pl.ANY)


def _params(n_grid):
    return pltpu.CompilerParams(dimension_semantics=("arbitrary",) * n_grid, vmem_limit_bytes=VMEM_LIMIT_BYTES)


def _bdot(a, b, dims):
    return lax.dot_general(a.astype(BF16), b.astype(BF16), (dims, ((), ())), preferred_element_type=F32)


NN = ((1,), (0,))
NT = ((1,), (1,))
TN = ((0,), (0,))


class _Comm:
    def __init__(self, ins, outs, n_sems, start, finish, late=None):
        self.ins, self.outs, self.n_sems, self.start, self.finish = ins, outs, n_sems, start, finish
        self.late = late if late is not None else (lambda ins, outs, sems: None)
        self.late_at = (1, 1)

    def sem_shapes(self):
        return [pltpu.SemaphoreType.DMA((k,)) for k in self.n_sems]


def _place():
    return lax.axis_index("x"), lax.axis_index("y"), lax.axis_index("c")


def _neighbours(x, y, c):
    return [(jnp.bitwise_xor(x, c), jnp.bitwise_xor(y, 1 - c)), (jnp.bitwise_xor(x, 1 - c), jnp.bitwise_xor(y, c)),
            (1 - x, 1 - y)]


def _gather_comm(shards):
    n = len(shards)

    def copies(ins, outs, sems):
        send_sems, recv_sems, _ = sems
        x, y, c = _place()
        chips = [(1 - x, y), (x, 1 - y), (1 - x, 1 - y)]

        def copy(a, k, block, to, from_shard=False):
            dst = outs[a].at[4 * block[0] + 2 * block[1] + block[2]]
            return pltpu.make_async_remote_copy(
                src_ref=ins[a] if from_shard else dst, dst_ref=dst, send_sem=send_sems.at[a * 7 + k],
                recv_sem=recv_sems.at[a * 7 + k], device_id=to, device_id_type=MESH)

        me, sibling = (x, y, c), (x, y, 1 - c)
        own = [[copy(a, 0, me, sibling, True)] + [copy(a, 1 + j, me, (*chip, c), True) for j, chip in enumerate(chips)]
               for a in range(n)]
        landed = [[copy(a, 1 + j, (*chip, c), me) for j, chip in enumerate(chips)] for a in range(n)]
        passed = [[copy(a, 4 + j, (*chip, c), sibling) for j, chip in enumerate(chips)] for a in range(n)]
        from_sibling = [[copy(a, 0, sibling, me)] + [copy(a, 4 + j, (*chip, 1 - c), me) for j, chip in enumerate(chips)]
                        for a in range(n)]
        local = [pltpu.make_async_copy(ins[a], outs[a].at[4 * x + 2 * y + c], sems[2].at[a]) for a in range(n)]
        return own, landed, passed, from_sibling, local

    def start(ins, outs, sems):
        own, _, _, _, local = copies(ins, outs, sems)
        for a in range(n):
            local[a].start()
            for cp in own[a]:
                cp.start()

    def late(ins, outs, sems):
        _, landed, passed, _, _ = copies(ins, outs, sems)
        for a in range(n):
            for arrived, onward in zip(landed[a], passed[a]):
                arrived.wait_recv()
                onward.start()

    def finish(ins, outs, sems):
        own, _, passed, from_sibling, local = copies(ins, outs, sems)
        for a in range(n):
            for cp in from_sibling[a]:
                cp.wait_recv()
        for a in range(n):
            for cp in own[a] + passed[a]:
                cp.wait_send()
            local[a].wait()

    outs = [jax.ShapeDtypeStruct((N_DEV,) + s.shape, s.dtype) for s in shards]
    return _Comm(list(shards), outs, (7 * n, 7 * n, n), start, finish, late)


def _sibling_comm(parts):
    n = len(parts)

    def copies(ins, outs, sems):
        x, y, c = _place()
        return [pltpu.make_async_remote_copy(
            src_ref=ins[a].at[2 * q + 1 - c], dst_ref=outs[a].at[q], send_sem=sems[0].at[a * N_CHIP + q],
            recv_sem=sems[1].at[a * N_CHIP + q], device_id=(x, y, 1 - c), device_id_type=MESH)
            for a in range(n) for q in range(N_CHIP)]

    def start(ins, outs, sems):
        for cp in copies(ins, outs, sems):
            cp.start()

    def finish(ins, outs, sems):
        cps = copies(ins, outs, sems)
        for cp in cps:
            cp.wait_recv()
        for cp in cps:
            cp.wait_send()

    outs = [jax.ShapeDtypeStruct((N_CHIP,) + p.shape[1:], p.dtype) for p in parts]
    return _Comm(list(parts), outs, (N_CHIP * n, N_CHIP * n), start, finish)


def _chips_comm(parts):
    n = len(parts)

    def copies(ins, outs, sems):
        x, y, c = _place()
        chips = [(1 - x, y), (x, 1 - y), (1 - x, 1 - y)]
        return [pltpu.make_async_remote_copy(
            src_ref=ins[a].at[2 * px + py], dst_ref=outs[a].at[j], send_sem=sems[0].at[a * 3 + j],
            recv_sem=sems[1].at[a * 3 + j], device_id=(px, py, c), device_id_type=MESH)
            for a in range(n) for j, (px, py) in enumerate(chips)]

    def start(ins, outs, sems):
        for cp in copies(ins, outs, sems):
            cp.start()

    def finish(ins, outs, sems):
        cps = copies(ins, outs, sems)
        for cp in cps:
            cp.wait_recv()
        for cp in cps:
            cp.wait_send()

    outs = [jax.ShapeDtypeStruct((3,) + p.shape[1:], p.dtype) for p in parts]
    return _Comm(list(parts), outs, (3 * n, 3 * n), start, finish)


def _join_comms(comms):
    if len(comms) == 1:
        return comms[0]

    def split(refs, counts):
        out, at = [], 0
        for n in counts:
            out.append(refs[at:at + n])
            at += n
        return out

    def each(method):
        def run(ins, outs, sems):
            parts = zip(comms, split(ins, [len(c.ins) for c in comms]), split(outs, [len(c.outs) for c in comms]),
                        split(sems, [len(c.n_sems) for c in comms]))
            for c, c_ins, c_outs, c_sems in parts:
                getattr(c, method)(c_ins, c_outs, c_sems)
        return run

    return _Comm([a for c in comms for a in c.ins], [o for c in comms for o in c.outs],
                 tuple(k for c in comms for k in c.n_sems), each("start"), each("finish"), each("late"))


def _exchange(comm, *, name):
    n_ci, n_co = len(comm.ins), len(comm.outs)

    def kern(*refs):
        c_ins, c_outs, sems = refs[:n_ci], refs[n_ci:n_ci + n_co], refs[n_ci + n_co:]
        comm.start(c_ins, c_outs, sems)
        comm.late(c_ins, c_outs, sems)
        comm.finish(c_ins, c_outs, sems)

    return pl.pallas_call(kern, name=name, in_specs=[ANY] * n_ci, out_specs=[ANY] * n_co, out_shape=comm.outs,
                          scratch_shapes=comm.sem_shapes())(*comm.ins)


def _call(body, *, name, grid, in_specs, out_specs, out_shape, scratch, args, plan=None):
    comm = plan.comm(name) if plan is not None else None
    if comm is None:
        return list(pl.pallas_call(functools.partial(body), name=name, grid=grid, in_specs=in_specs,
                                   out_specs=out_specs, out_shape=out_shape, scratch_shapes=scratch,
                                   compiler_params=_params(len(grid)))(*args))
    n_in, n_out, n_scr, n_ci, n_co = len(in_specs), len(out_specs), len(scratch), len(comm.ins), len(comm.outs)

    def kern(*refs):
        ins, c_ins, refs = refs[:n_in], refs[n_in:n_in + n_ci], refs[n_in + n_ci:]
        outs, c_outs, refs = refs[:n_out], refs[n_out:n_out + n_co], refs[n_out + n_co:]
        scr, sems = refs[:n_scr], refs[n_scr:]
        ids = [pl.program_id(ax) for ax in range(len(grid))]
        step = functools.reduce(lambda at, ig: at * ig[1] + ig[0], zip(ids, grid), 0)
        n_steps = math.prod(grid)

        @pl.when(step == 0)
        def _():
            comm.start(c_ins, c_outs, sems)

        @pl.when(step == min(n_steps * comm.late_at[0] // comm.late_at[1], n_steps - 1))
        def _():
            comm.late(c_ins, c_outs, sems)
        body(*ins, *outs, *scr)

        @pl.when(step == n_steps - 1)
        def _():
            comm.finish(c_ins, c_outs, sems)

    res = pl.pallas_call(kern, name=name, grid=grid, in_specs=list(in_specs) + [ANY] * n_ci,
                         out_specs=list(out_specs) + [ANY] * n_co, out_shape=list(out_shape) + comm.outs,
                         scratch_shapes=list(scratch) + comm.sem_shapes(),
                         compiler_params=_params(len(grid)))(*args, *comm.ins)
    plan.landed(name, list(res[n_out:]))
    return list(res[:n_out])


def _mm_body(dims, has_add, *refs):
    if has_add:
        a_ref, b_ref, add_ref, o_ref = refs
        total = _bdot(a_ref[...], b_ref[...], dims) + add_ref[...]
    else:
        a_ref, b_ref, o_ref = refs
        total = _bdot(a_ref[...], b_ref[...], dims)
    o_ref[...] = total.astype(o_ref.dtype)


def _mm_nt_body(j, n, dy_ref, w_ref, o_ref):
    total = _bdot(dy_ref[:, 0:n], w_ref[0], NT)
    for jj in range(1, j):
        total = total + _bdot(dy_ref[:, jj * n:(jj + 1) * n], w_ref[jj], NT)
    o_ref[...] = total.astype(o_ref.dtype)


def _mm_nn(a, w3, *, name, out_dtype=BF16, add=None, tm=1024, tn=None, out3=False, w_t=False, plan=None):
    m, kk = a.shape
    j, n = w3.shape[0], w3.shape[1 if w_t else 2]
    tm, tn = min(tm, m), n if tn is None else tn
    n_t = n // tn
    in_specs = [pl.BlockSpec((tm, kk), lambda i, jj: (i, 0)),
                pl.BlockSpec((None, tn, kk), lambda i, jj: (jj // n_t, jj % n_t, 0)) if w_t else
                pl.BlockSpec((None, kk, tn), lambda i, jj: (jj // n_t, 0, jj % n_t))]
    args = [a, w3]
    if add is not None:
        in_specs.append(pl.BlockSpec((tm, tn), lambda i, jj: (i, jj)))
        args.append(add)
    if out3:
        out_spec = pl.BlockSpec((None, tm, tn), lambda i, jj: (jj // n_t, i, jj % n_t))
        out_shape = jax.ShapeDtypeStruct((j, m, n), out_dtype)
    else:
        out_spec = pl.BlockSpec((tm, tn), lambda i, jj: (i, jj))
        out_shape = jax.ShapeDtypeStruct((m, j * n), out_dtype)
    return _call(
        functools.partial(_mm_body, NT if w_t else NN, add is not None), name=name, grid=(m // tm, j * n_t),
        in_specs=in_specs, out_specs=[out_spec], out_shape=[out_shape], scratch=[], args=args, plan=plan)[0]


def _mm_gathering(a, shard, *, name, out3=False, w_t=False, tm=1024):
    m, kk = a.shape
    n = shard.shape[0 if w_t else 1]
    tm = min(tm, m)
    n_i = m // tm
    fetch_at = min(1, n_i - 1)

    def body(a_ref, shard_ref, o_ref, w_all, w_vmem, send_sems, recv_sems, copy_sems):
        jj, i = pl.program_id(0), pl.program_id(1)
        x, y, c = _place()
        me, sibling = (x, y, c), (x, y, 1 - c)
        chips = _neighbours(x, y, c)
        sibling_chips = [chips[1], chips[0], chips[2]]

        def rows(block):
            return w_all.at[4 * block[0] + 2 * block[1] + block[2]]

        def remote(k, block, to, from_shard=False):
            return pltpu.make_async_remote_copy(
                src_ref=shard_ref if from_shard else rows(block), dst_ref=rows(block), send_sem=send_sems.at[k],
                recv_sem=recv_sems.at[k], device_id=to, device_id_type=MESH)

        def load(step, src):
            return pltpu.make_async_copy(src, w_vmem.at[step % 2], copy_sems.at[1 + step % 2])

        own = [remote(0, me, sibling, True), remote(1, me, (*chips[0], c), True), remote(2, me, (*chips[1], c), True),
               remote(3, (*chips[0], c), (*chips[1], c))]
        passed = [remote(4 + j, (*chip, c), sibling) for j, chip in enumerate(chips)]
        local = pltpu.make_async_copy(shard_ref, rows(me), copy_sems.at[0])

        @pl.when(jnp.logical_and(i == 0, jj == 0))
        def _():
            local.start()
            own[0].start()
            own[1].start()
            load(0, shard_ref).start()

        def arrivals():
            yield 1, (lambda: remote(0, sibling, me).wait_recv()), sibling
            for j, chip in enumerate(chips):
                def landed(j=j, chip=chip):
                    if j < 2:
                        own[1 + j].wait_send()
                        own[2 + j].start()
                    remote(1 + j, (*chip, c), me).wait_recv()
                    passed[j].start()
                yield 2 + 2 * j, landed, (*chip, c)
                block = (*sibling_chips[j], 1 - c)
                yield 3 + 2 * j, (lambda j=j, block=block: remote(4 + j, block, me).wait_recv()), block

        for step, wait_for_it, block in arrivals():
            @pl.when(jnp.logical_and(i == fetch_at, jj == step - 1))
            def _():
                wait_for_it()
                load(step, rows(block)).start()

        for step in range(N_DEV):
            @pl.when(jnp.logical_and(i == 0, jj == step))
            def _():
                load(step, rows(me)).wait()

        o_ref[...] = _bdot(a_ref[...], w_vmem[lax.rem(jj, 2)], NT if w_t else NN).astype(o_ref.dtype)

        @pl.when(jnp.logical_and(i == n_i - 1, jj == N_DEV - 1))
        def _():
            for cp in [own[0], own[3]] + passed:
                cp.wait_send()
            local.wait()

    def swept(jj):
        x, y, c = _place()
        first, second = 2 + 2 * c, 4 - 2 * c
        flips = (0b000, 0b001, first, second + 1, second, first + 1, 0b110, 0b111)
        return jnp.bitwise_xor(4 * x + 2 * y + c, sum(jnp.where(jj == k, f, 0) for k, f in enumerate(flips)))

    if out3:
        out_spec = pl.BlockSpec((None, tm, n), lambda jj, i: (swept(jj), i, 0))
        out_shape = jax.ShapeDtypeStruct((N_DEV, m, n), BF16)
    else:
        out_spec = pl.BlockSpec((tm, n), lambda jj, i: (i, swept(jj)))
        out_shape = jax.ShapeDtypeStruct((m, N_DEV * n), BF16)
    return pl.pallas_call(
        body, name=name, grid=(N_DEV, n_i),
        in_specs=[pl.BlockSpec((tm, kk), lambda jj, i: (i, 0)), ANY], out_specs=[out_spec, ANY],
        scratch_shapes=[pltpu.VMEM((2,) + shard.shape, shard.dtype), pltpu.SemaphoreType.DMA((7,)),
                        pltpu.SemaphoreType.DMA((7,)), pltpu.SemaphoreType.DMA((3,))],
        out_shape=[out_shape, jax.ShapeDtypeStruct((N_DEV,) + shard.shape, shard.dtype)],
        compiler_params=_params(2))(a, shard)


def _sigmoid(v):
    return 0.5 * jnp.tanh(0.5 * v) + 0.5


def _resident(w):
    return pl.BlockSpec(w.shape, lambda i: (0,) * w.ndim, pipeline_mode=pl.Buffered(1))


def _ffn_out_loss(gu3, w3, add, g, target, *, name, tm=512):
    j2, m, n = gu3.shape
    j = j2 // 2
    nn = w3.shape[2]
    tm = min(tm, m)

    def body(gu_ref, w_ref, add_ref, g_ref, t_ref, dx_ref, dxb_ref, dg_ref, loss_ref, act_ref):
        i = pl.program_id(0)
        xv = add_ref[...]
        for jj in range(j):
            gate = gu_ref[0, jj].astype(F32)
            act = (gate * _sigmoid(gate) * gu_ref[1, jj].astype(F32)).astype(BF16)
            act_ref[jj] = act
            xv = xv + _bdot(act, w_ref[jj], NN)
        gv = g_ref[...]
        r = lax.rsqrt(jnp.mean(xv * xv, axis=-1, keepdims=True) + NORM_EPS)
        xhat = xv * r
        err = xhat * gv - t_ref[...]
        _acc_rows(i, loss_ref, 0.5 * jnp.sum(jnp.mean(err * err, axis=-1, keepdims=True), axis=0, keepdims=True))
        dy = err * (1.0 / nn)
        dxhat = dy * gv
        dx = r * (dxhat - xhat * jnp.mean(dxhat * xhat, axis=-1, keepdims=True))
        dx_ref[...] = dx
        dxb_ref[...] = dx.astype(BF16)
        _acc_rows(i, dg_ref, jnp.sum(dy * xhat, axis=0, keepdims=True))

    row = pl.BlockSpec((tm, nn), lambda i: (i, 0))
    return _call(body, name=name, grid=(m // tm,),
                 in_specs=[pl.BlockSpec((2, j, tm, n), lambda i: (0, 0, i, 0)), _resident(w3),
                           row, pl.BlockSpec(g.shape, lambda i: (0, 0)), row],
                 out_specs=[row, row, pl.BlockSpec((8, nn), lambda i: (0, 0)), pl.BlockSpec((8, LANES), lambda i: (0, 0)),
                            pl.BlockSpec((j, tm, n), lambda i: (0, i, 0))],
                 out_shape=[jax.ShapeDtypeStruct((m, nn), F32), jax.ShapeDtypeStruct((m, nn), BF16),
                            jax.ShapeDtypeStruct((8, nn), F32), jax.ShapeDtypeStruct((8, LANES), F32),
                            jax.ShapeDtypeStruct((j, m, n), BF16)],
                 scratch=[], args=[gu3.reshape(2, j, m, n), w3, add, g, target])


def _ffn_out_bwd(dy, w3, gu3, *, name, tm=1024):
    m, nn = dy.shape
    j, n, _ = w3.shape
    tm = min(tm, m)

    def body(dy_ref, w_ref, gu_ref, dgu_ref):
        da = _bdot(dy_ref[...], w_ref[...], NT)
        gate = gu_ref[0].astype(F32)
        up = gu_ref[1].astype(F32)
        sg = _sigmoid(gate)
        silu = gate * sg
        dgu_ref[0] = (da * up * (sg + silu * (1.0 - sg))).astype(BF16)
        dgu_ref[1] = (da * silu).astype(BF16)

    out = _call(body, name=name, grid=(m // tm, j),
                in_specs=[pl.BlockSpec((tm, nn), lambda i, jj: (i, 0)),
                          pl.BlockSpec((None, n, nn), lambda i, jj: (jj, 0, 0)),
                          pl.BlockSpec((2, None, tm, n), lambda i, jj: (0, jj, i, 0))],
                out_specs=[pl.BlockSpec((2, None, tm, n), lambda i, jj: (0, jj, i, 0))],
                out_shape=[jax.ShapeDtypeStruct((2, j, m, n), BF16)], scratch=[],
                args=[dy, w3, gu3.reshape(2, j, m, n)])[0]
    return out.reshape(2 * j, m, n)


def _rms_fwd_tail(xv, g_ref, h_ref):
    r = lax.rsqrt(jnp.mean(xv * xv, axis=-1, keepdims=True) + NORM_EPS)
    h_ref[...] = (xv * r * g_ref[...]).astype(BF16)


def _rms_bwd_tail(i, dh, x_ref, g_ref, dres_ref, dx_ref, dxb_ref, dg_ref):
    xv = x_ref[...]
    r = lax.rsqrt(jnp.mean(xv * xv, axis=-1, keepdims=True) + NORM_EPS)
    xhat = xv * r
    dxhat = dh * g_ref[...]
    dx = r * (dxhat - xhat * jnp.mean(dxhat * xhat, axis=-1, keepdims=True))
    if dres_ref is not None:
        dx = dx + dres_ref[...]
    dx_ref[...] = dx
    dxb_ref[...] = dx.astype(BF16)
    _acc_rows(i, dg_ref, jnp.sum(dh * xhat, axis=0, keepdims=True))


def _mm_nt_rms(dy, w3, x, g, dres, *, name, dy3=False, w_nn=False, tm=512, plan=None):
    j = w3.shape[0]
    m, kk = x.shape
    n = dy.shape[2] if dy3 else dy.shape[1] // j
    tm = min(tm, m)

    def body(dy_ref, w_ref, x_ref, g_ref, *rest):
        dres_ref = rest[0] if dres is not None else None
        dx_ref, dxb_ref, dg_ref = rest[-3:]
        dh = None
        for jj in range(j):
            piece = dy_ref[jj] if dy3 else dy_ref[:, jj * n:(jj + 1) * n]
            part = _bdot(piece, w_ref[jj], NN if w_nn else NT)
            dh = part if dh is None else dh + part
        _rms_bwd_tail(pl.program_id(0), dh, x_ref, g_ref, dres_ref, dx_ref, dxb_ref, dg_ref)

    row = pl.BlockSpec((tm, kk), lambda i: (i, 0))
    in_specs = [pl.BlockSpec((j, tm, n), lambda i: (0, i, 0)) if dy3 else pl.BlockSpec((tm, j * n), lambda i: (i, 0)),
                _resident(w3), row, pl.BlockSpec(g.shape, lambda i: (0, 0))]
    args = [dy, w3, x, g]
    if dres is not None:
        in_specs.append(row)
        args.append(dres)
    return _call(body, name=name, grid=(m // tm,), in_specs=in_specs,
                 out_specs=[row, row, pl.BlockSpec((8, kk), lambda i: (0, 0))],
                 out_shape=[jax.ShapeDtypeStruct((m, kk), F32), jax.ShapeDtypeStruct((m, kk), BF16),
                            jax.ShapeDtypeStruct((8, kk), F32)], scratch=[], args=args, plan=plan)


def _mix_out(o_a, y_b, proj, w_a, w_b, w, x, g, *, name, tm=512, plan=None):
    s, c = o_a.shape
    d = w.shape[1]
    tm = min(tm, s)

    def body(oa_ref, yb_ref, ga_ref, gb_ref, wa_ref, wb_ref, w_ref, x_ref, g_ref, x1_ref, h_ref, merged_ref, a_ref, b_ref):
        a_ref[...] = _bdot(oa_ref[...], wa_ref[...], NN).astype(BF16)
        b_ref[...] = _bdot(yb_ref[...], wb_ref[...], NN).astype(BF16)
        merged = (_sigmoid(ga_ref[...].astype(F32)) * a_ref[...].astype(F32)
                  + _sigmoid(gb_ref[...].astype(F32)) * b_ref[...].astype(F32)).astype(BF16)
        merged_ref[...] = merged
        xv = _bdot(merged, w_ref[...], NN) + x_ref[...]
        x1_ref[...] = xv
        _rms_fwd_tail(xv, g_ref, h_ref)

    row = pl.BlockSpec((tm, d), lambda i: (i, 0))
    narrow = pl.BlockSpec((tm, c), lambda i: (i, 0))
    whole = lambda arr: pl.BlockSpec(arr.shape, lambda i: (0,) * arr.ndim)
    return _call(body, name=name, grid=(s // tm,),
                 in_specs=[narrow, narrow, pl.BlockSpec((tm, d), lambda i: (i, 3)), pl.BlockSpec((tm, d), lambda i: (i, 4)),
                           whole(w_a), whole(w_b), whole(w), row, whole(g)],
                 out_specs=[row] * 5,
                 out_shape=[jax.ShapeDtypeStruct((s, d), F32)] + [jax.ShapeDtypeStruct((s, d), BF16)] * 4,
                 scratch=[], args=[o_a, y_b, proj, proj, w_a, w_b, w, x, g], plan=plan)


def _mm_tn_a3(a3, dy, *, name):
    j, t, n = a3.shape
    nn = dy.shape[1]
    return _call(functools.partial(_mm_body, TN, False), name=name, grid=(j,),
                 in_specs=[pl.BlockSpec((None, t, n), lambda jj: (jj, 0, 0)), pl.BlockSpec((t, nn), lambda jj: (0, 0))],
                 out_specs=[pl.BlockSpec((None, n, nn), lambda jj: (jj, 0, 0))],
                 out_shape=[jax.ShapeDtypeStruct((j, n, nn), BF16)], scratch=[], args=[a3, dy])[0]


def _mm_nt(dy, w3, *, name, out_dtype=BF16, tm=512, tn=1024, plan=None):
    m = dy.shape[0]
    j, kk, n = w3.shape
    tm, tn = min(tm, m), min(tn, kk)
    return _call(
        functools.partial(_mm_nt_body, j, n), name=name,
        grid=(m // tm, kk // tn),
        in_specs=[pl.BlockSpec((tm, j * n), lambda i, q: (i, 0)),
                  pl.BlockSpec((j, tn, n), lambda i, q: (0, q, 0))],
        out_specs=[pl.BlockSpec((tm, tn), lambda i, q: (i, q))],
        out_shape=[jax.ShapeDtypeStruct((m, kk), out_dtype)], scratch=[], args=[dy, w3], plan=plan)[0]


def _mm_tn(a, dy, n, *, name, out_dtype=BF16, tm=512, tn=None, k_tiles=None, plan=None):
    t, kk = a.shape
    j = dy.shape[1] // n
    tm, tn = min(tm, kk), n if tn is None else tn
    n_t = n // tn
    first, count = (0, kk // tm) if k_tiles is None else k_tiles
    return _call(
        functools.partial(_mm_body, TN, False), name=name,
        grid=(count, j * n_t),
        in_specs=[pl.BlockSpec((t, tm), lambda i, jj: (0, first + i)),
                  pl.BlockSpec((t, tn), lambda i, jj: (0, jj))],
        out_specs=[pl.BlockSpec((None, tm, tn), lambda i, jj: (jj // n_t, i, jj % n_t))],
        out_shape=[jax.ShapeDtypeStruct((j, count * tm, n), out_dtype)], scratch=[], args=[a, dy], plan=plan)[0]


def _rows(body, ins, outs, *, n_rows, tm, name, plan=None):
    tm = min(tm, n_rows)
    n_steps = n_rows // tm
    in_specs, args = [], []
    for arr, kind, width, block in ins:
        if kind == "row":
            in_specs.append(pl.BlockSpec((tm, width), functools.partial(lambda i, b: (i, b), b=block)))
        elif kind == "prev":
            in_specs.append(pl.BlockSpec((tm, width), functools.partial(lambda i, b: (jnp.maximum(i - 1, 0), b), b=block)))
        elif kind == "next":
            in_specs.append(pl.BlockSpec((tm, width), functools.partial(lambda i, b: (jnp.minimum(i + 1, n_steps - 1), b), b=block)))
        else:
            in_specs.append(pl.BlockSpec(arr.shape, functools.partial(lambda i, nd: (0,) * nd, nd=arr.ndim)))
        args.append(arr)
    out_specs, out_shape = [], []
    for shape, dtype, kind in outs:
        if kind == "row":
            out_specs.append(pl.BlockSpec((tm, shape[1]), lambda i: (i, 0)))
        else:
            out_specs.append(pl.BlockSpec(shape, functools.partial(lambda i, nd: (0,) * nd, nd=len(shape))))
        out_shape.append(jax.ShapeDtypeStruct(shape, dtype))

    def kern(*refs):
        body(pl.program_id(0), n_steps, *refs)

    return _call(kern, name=name, grid=(n_steps,), in_specs=in_specs, out_specs=out_specs, out_shape=out_shape,
                 scratch=[], args=args, plan=plan)


def _acc_rows(i, ref, value):
    @pl.when(i == 0)
    def _():
        ref[...] = jnp.zeros_like(ref)
    ref[...] += jnp.broadcast_to(value, ref.shape)


def _rms_fwd(x, g, *, name, tm=512):
    s, d = x.shape

    def body(i, n, x_ref, g_ref, h_ref):
        _rms_fwd_tail(x_ref[...], g_ref, h_ref)

    return _rows(body, [(x, "row", d, 0), (g, "full", 0, 0)], [((s, d), BF16, "row")], n_rows=s, tm=tm, name=name)[0]


def _rms_bwd(x, g, dh, dres, *, name, tm=512, plan=None):
    s, d = x.shape

    def body(i, n, x_ref, g_ref, dh_ref, dres_ref, dx_ref, dxb_ref, dg_ref):
        _rms_bwd_tail(i, dh_ref[...].astype(F32), x_ref, g_ref, dres_ref, dx_ref, dxb_ref, dg_ref)

    return _rows(body, [(x, "row", d, 0), (g, "full", 0, 0), (dh, "row", d, 0), (dres, "row", d, 0)],
                 [((s, d), F32, "row"), ((s, d), BF16, "row"), ((8, d), F32, "acc")],
                 n_rows=s, tm=tm, name=name, plan=plan)


def _mix_out_bwd(dx1b, w, br_a, br_b, proj, w_a, w_b, *, name, tm=512, plan=None):
    s, d = br_a.shape
    c = w_a.shape[0]
    tm = min(tm, s)

    def body(dy_ref, w_ref, a_ref, b_ref, ga_ref, gb_ref, wa_ref, wb_ref, da_ref, db_ref, dg_ref, doa_ref, dyb_ref):
        dm = _bdot(dy_ref[...], w_ref[...], NT)
        sa = _sigmoid(ga_ref[...].astype(F32))
        sb = _sigmoid(gb_ref[...].astype(F32))
        da_ref[...] = (dm * sa).astype(BF16)
        db_ref[...] = (dm * sb).astype(BF16)
        dg_ref[:, :d] = (dm * a_ref[...].astype(F32) * sa * (1.0 - sa)).astype(BF16)
        dg_ref[:, d:] = (dm * b_ref[...].astype(F32) * sb * (1.0 - sb)).astype(BF16)
        doa_ref[...] = _bdot(da_ref[...], wa_ref[...], NT).astype(BF16)
        dyb_ref[...] = _bdot(db_ref[...], wb_ref[...], NT).astype(BF16)

    row = pl.BlockSpec((tm, d), lambda i: (i, 0))
    narrow = pl.BlockSpec((tm, c), lambda i: (i, 0))
    whole = lambda arr: pl.BlockSpec(arr.shape, lambda i: (0,) * arr.ndim)
    return _call(body, name=name, grid=(s // tm,),
                 in_specs=[row, whole(w), row, row, pl.BlockSpec((tm, d), lambda i: (i, 3)),
                           pl.BlockSpec((tm, d), lambda i: (i, 4)), whole(w_a), whole(w_b)],
                 out_specs=[row, row, pl.BlockSpec((tm, 2 * d), lambda i: (i, 0)), narrow, narrow],
                 out_shape=[jax.ShapeDtypeStruct((s, d), BF16), jax.ShapeDtypeStruct((s, d), BF16),
                            jax.ShapeDtypeStruct((s, 2 * d), BF16), jax.ShapeDtypeStruct((s, c), BF16),
                            jax.ShapeDtypeStruct((s, c), BF16)],
                 scratch=[], args=[dx1b, w, br_a, br_b, proj, proj, w_a, w_b], plan=plan)


def _shift_down(cur, prev, k, first):
    row = lax.broadcasted_iota(jnp.int32, cur.shape, 0)
    out = jnp.where(row >= k, pltpu.roll(cur, k, 0), pltpu.roll(prev, k, 0))
    return jnp.where(jnp.logical_and(first, row < k), 0.0, out)


def _shift_up(cur, nxt, k, last):
    tm = cur.shape[0]
    row = lax.broadcasted_iota(jnp.int32, cur.shape, 0)
    out = jnp.where(row < tm - k, pltpu.roll(cur, tm - k, 0), pltpu.roll(nxt, tm - k, 0))
    return jnp.where(jnp.logical_and(last, row >= tm - k), 0.0, out)


def _conv_fwd(proj, conv_w, *, name, tm=512):
    s = proj.shape[0]
    c = CONV_WIDTH

    def body(i, n, u_ref, gb_ref, gc_ref, up_ref, gcp_ref, w_ref, y_ref):
        cu = gc_ref[...].astype(F32) * u_ref[...].astype(F32)
        cup = gcp_ref[...].astype(F32) * up_ref[...].astype(F32)
        first = i == 0
        y = (w_ref[0:1, :] * _shift_down(cu, cup, 2, first) + w_ref[1:2, :] * _shift_down(cu, cup, 1, first)
             + w_ref[2:3, :] * cu)
        y_ref[...] = (gb_ref[...].astype(F32) * y).astype(BF16)

    return _rows(body, [(proj, "row", c, 3), (proj, "row", c, 4), (proj, "row", c, 5),
                        (proj, "prev", c, 3), (proj, "prev", c, 5), (conv_w, "full", 0, 0)],
                 [((s, c), BF16, "row")], n_rows=s, tm=tm, name=name)[0]


def _conv_bwd(dy_b, proj, conv_w, *, name, tm=512, plan=None):
    s = proj.shape[0]
    c = CONV_WIDTH

    def body(i, n, dy_ref, u_ref, gb_ref, gc_ref, up_ref, gcp_ref, dyn_ref, gbn_ref, w_ref, d_ref, dw_ref):
        first, last = i == 0, i == n - 1
        u = u_ref[...].astype(F32)
        gb = gb_ref[...].astype(F32)
        gc = gc_ref[...].astype(F32)
        cu = gc * u
        cup = gcp_ref[...].astype(F32) * up_ref[...].astype(F32)
        cu1 = _shift_down(cu, cup, 1, first)
        cu2 = _shift_down(cu, cup, 2, first)
        conv = w_ref[0:1, :] * cu2 + w_ref[1:2, :] * cu1 + w_ref[2:3, :] * cu
        dy = dy_ref[...].astype(F32)
        dyc = dy * gb
        dycn = dyn_ref[...].astype(F32) * gbn_ref[...].astype(F32)
        dcu = (w_ref[2:3, :] * dyc + w_ref[1:2, :] * _shift_up(dyc, dycn, 1, last)
               + w_ref[0:1, :] * _shift_up(dyc, dycn, 2, last))
        d_ref[:, 0:c] = (dcu * gc).astype(BF16)
        d_ref[:, c:2 * c] = (dy * conv).astype(BF16)
        d_ref[:, 2 * c:3 * c] = (dcu * u).astype(BF16)
        row = lax.broadcasted_iota(jnp.int32, (8, c), 0)
        dw = (jnp.where(row == 0, jnp.sum(dyc * cu2, axis=0, keepdims=True), 0.0)
              + jnp.where(row == 1, jnp.sum(dyc * cu1, axis=0, keepdims=True), 0.0)
              + jnp.where(row == 2, jnp.sum(dyc * cu, axis=0, keepdims=True), 0.0))

        @pl.when(first)
        def _():
            dw_ref[...] = jnp.zeros_like(dw_ref)
        dw_ref[...] += dw

    return _rows(body, [(dy_b, "row", c, 0), (proj, "row", c, 3), (proj, "row", c, 4), (proj, "row", c, 5),
                        (proj, "prev", c, 3), (proj, "prev", c, 5), (dy_b, "next", c, 0), (proj, "next", c, 4),
                        (conv_w, "full", 0, 0)],
                 [((s, 3 * c), BF16, "row"), ((8, c), F32, "acc")], n_rows=s, tm=tm, name=name, plan=plan)


def _mem_probs(q, k, scale):
    sc = _bdot(q, k, NT) * scale
    sc = sc - jnp.max(sc, axis=-1, keepdims=True)
    p = jnp.exp(sc)
    return p / jnp.sum(p, axis=-1, keepdims=True)


def _mem_sublayer(hq, w_q, kv, w_o, x, g, *, name, tm=512, plan=None):
    s, d = hq.shape
    hd = d // MEM_HEADS
    scale = 1.0 / math.sqrt(hd)
    tm = min(tm, s)

    def body(hq_ref, wq_ref, kv_ref, wo_ref, x_ref, g_ref, q_ref, o_ref, x2_ref, h_ref):
        q_ref[...] = _bdot(hq_ref[...], wq_ref[...], NN).astype(BF16)
        for h in range(MEM_HEADS):
            cols = slice(h * hd, (h + 1) * hd)
            p = _mem_probs(q_ref[:, cols], kv_ref[:, cols], scale)
            o_ref[:, cols] = _bdot(p, kv_ref[:, d + h * hd:d + (h + 1) * hd], NN).astype(BF16)
        xv = _bdot(o_ref[...], wo_ref[...], NN) + x_ref[...]
        x2_ref[...] = xv
        _rms_fwd_tail(xv, g_ref, h_ref)

    row = pl.BlockSpec((tm, d), lambda i: (i, 0))
    whole = lambda a: pl.BlockSpec(a.shape, lambda i: (0,) * a.ndim)
    return _call(body, name=name, grid=(s // tm,),
                 in_specs=[row, whole(w_q), whole(kv), whole(w_o), row, whole(g)], out_specs=[row] * 4,
                 out_shape=[jax.ShapeDtypeStruct((s, d), BF16), jax.ShapeDtypeStruct((s, d), BF16),
                            jax.ShapeDtypeStruct((s, d), F32), jax.ShapeDtypeStruct((s, d), BF16)],
                 scratch=[], args=[hq, w_q, kv, w_o, x, g], plan=plan)


def _mem_sublayer_bwd(dx2b, dx2, x, g, qm, kv, w_q, w_o, *, name, tm=512, plan=None):
    s, d = qm.shape
    hd = d // MEM_HEADS
    scale = 1.0 / math.sqrt(hd)
    tm = min(tm, s)

    def body(dyb_ref, dres_ref, x_ref, g_ref, q_ref, kv_ref, wq_ref, wo_ref, dx_ref, dxb_ref, dg_ref, dq_ref, dkv_ref):
        i = pl.program_id(0)

        @pl.when(i == 0)
        def _():
            dkv_ref[...] = jnp.zeros_like(dkv_ref)
        dom = _bdot(dyb_ref[...], wo_ref[...], NT).astype(BF16)
        for h in range(MEM_HEADS):
            cols = slice(h * hd, (h + 1) * hd)
            vcols = slice(d + h * hd, d + (h + 1) * hd)
            q, k, v, do = q_ref[:, cols], kv_ref[:, cols], kv_ref[:, vcols], dom[:, cols]
            p = _mem_probs(q, k, scale)
            dp = _bdot(do, v, NT)
            ds = p * (dp - jnp.sum(dp * p, axis=-1, keepdims=True)) * scale
            dq_ref[:, cols] = _bdot(ds, k, NN).astype(BF16)
            dkv_ref[:, cols] += _bdot(ds, q, TN)
            dkv_ref[:, vcols] += _bdot(p, do, TN)
        dh = _bdot(dq_ref[...], wq_ref[...], NT)
        _rms_bwd_tail(i, dh, x_ref, g_ref, dres_ref, dx_ref, dxb_ref, dg_ref)

    row = pl.BlockSpec((tm, d), lambda i: (i, 0))
    whole = lambda a: pl.BlockSpec(a.shape, lambda i: (0,) * a.ndim)
    return _call(body, name=name, grid=(s // tm,),
                 in_specs=[row, row, row, whole(g), row, whole(kv), whole(w_q), whole(w_o)],
                 out_specs=[row, row, pl.BlockSpec((8, d), lambda i: (0, 0)), row, whole(kv)],
                 out_shape=[jax.ShapeDtypeStruct((s, d), F32), jax.ShapeDtypeStruct((s, d), BF16),
                            jax.ShapeDtypeStruct((8, d), F32), jax.ShapeDtypeStruct((s, d), BF16),
                            jax.ShapeDtypeStruct(kv.shape, F32)],
                 scratch=[], args=[dx2b, dx2, x, g, qm, kv, w_q, w_o], plan=plan)


def _sb_consts(t):
    row = lax.broadcasted_iota(jnp.int32, (t, t), 0)
    col = lax.broadcasted_iota(jnp.int32, (t, t), 1)
    lane = lax.broadcasted_iota(jnp.int32, (t, LANES), 1)
    return row, col, lane < SB_HEAD_DIM


def _sb_logits(q, k):
    z2 = jnp.minimum(_bdot(q, k, NT) * LOG2_E, SB_CLAMP)
    return z2, jnp.exp2(z2)


def _tri_sum(v, tri):
    hi = v.astype(BF16)
    lo = (v - hi.astype(F32)).astype(BF16)
    return _bdot(hi, tri, NN) + _bdot(lo, tri, NN)


def _sb_fwd(proj, *, name, plan=None):
    s = proj.shape[0]
    t, nh = SB_TILE, SB_STEP_HEADS
    n_q = s // t
    scale = 1.0 / math.sqrt(SB_HEAD_DIM)

    def body(q_ref, k_ref, v_ref, o_ref, c_ref, first_ref, acc_ref, c_scr):
        i = pl.program_id(1)
        row, col, head0 = _sb_consts(t)
        later = (row > col).astype(BF16)
        valid = col < row
        lanes = lambda h: slice((h // 2) * LANES, (h // 2 + 1) * LANES)
        q = [jnp.where(head0 == (h % 2 == 0), q_ref[:, lanes(h)] * scale, 0) for h in range(nh)]

        def tiles(kbs, diag_first, carry):
            rows = [pl.ds(pl.multiple_of(kb * t, t), t) for kb in kbs]
            jobs = [(n, h) for n in range(len(kbs)) for h in range(nh)]
            masked = lambda n: diag_first and n == 0
            zs = {(n, h): _sb_logits(q[h], k_ref[rows[n], lanes(h)]) for n, h in jobs}
            fail = {j: jnp.log2(1.0 + zs[j][1]) for j in jobs}
            fail = {j: jnp.where(valid, fail[j], 0.0) if masked(j[0]) else fail[j] for j in jobs}
            cum = {j: _tri_sum(fail[j], later) for j in jobs}
            run, before = list(carry), {}
            for n, h in jobs:
                before[n, h] = run[h]
                run[h] = run[h] + cum[n, h][:, 0:1] + fail[n, h][:, 0:1]
            w = {j: jnp.exp2(zs[j][0] - fail[j] - cum[j] - before[j]) for j in jobs}
            w = {j: jnp.where(valid, w[j], 0.0) if masked(j[0]) else w[j] for j in jobs}
            for n, h in jobs:
                acc_ref[h] += _bdot(w[n, h], v_ref[rows[n], lanes(h)], NN)
            return tuple(run)

        acc_ref[...] = jnp.zeros_like(acc_ref)
        zero = (jnp.zeros((t, 1), F32),) * nh

        def alive(carry):
            return (functools.reduce(jnp.minimum, [jnp.min(c) for c in carry]) < SB_DEAD).astype(jnp.int32)

        def step(state):
            new = tiles([state[0]], False, state[2:])
            return (state[0] - 1, alive(new)) + new

        @pl.when(i == 0)
        def _():
            for h, c in enumerate(tiles([i], True, zero)):
                c_scr[h] = c

        @pl.when(i > 0)
        def _():
            for h, c in enumerate(tiles([i, i - 1], True, zero)):
                c_scr[h] = c
        carry = tuple(c_scr[h] for h in range(nh))
        state = lax.while_loop(lambda st: jnp.logical_and(st[0] >= 0, st[1] > 0), step, (i - 2, alive(carry)) + carry)
        for b in range(nh // 2):
            o_ref[:, b * LANES:(b + 1) * LANES] = jnp.where(head0, acc_ref[2 * b], acc_ref[2 * b + 1]).astype(BF16)
        head = lax.broadcasted_iota(jnp.int32, (t, nh), 1)
        c_ref[...] = sum(jnp.where(head == h, state[2 + h], 0.0) for h in range(nh))
        first_ref[pl.program_id(0), i] = (jnp.maximum(state[0], -1) + 1).astype(F32)

    n_p, width = SB_HEADS // nh, nh * SB_HEAD_DIM
    k_blk, v_blk = SB_WIDTH // width, 2 * SB_WIDTH // width
    return _call(
        body, name=name, grid=(n_p, n_q),
        in_specs=[pl.BlockSpec((t, width), lambda p, i: (i, p)),
                  pl.BlockSpec((s, width), lambda p, i: (0, k_blk + p)),
                  pl.BlockSpec((s, width), lambda p, i: (0, v_blk + p))],
        out_specs=[pl.BlockSpec((t, width), lambda p, i: (i, p)),
                   pl.BlockSpec((None, t, nh), lambda p, i: (p, i, 0)),
                   pl.BlockSpec(memory_space=pltpu.SMEM)],
        out_shape=[jax.ShapeDtypeStruct((s, SB_WIDTH), BF16), jax.ShapeDtypeStruct((n_p, s, nh), F32),
                   jax.ShapeDtypeStruct((n_p, n_q), F32)],
        scratch=[pltpu.VMEM((nh, t, LANES), F32), pltpu.VMEM((nh, t, 1), F32)], args=[proj, proj, proj], plan=plan)


def _sb_bwd(proj, do_a, ctot, first, *, name, plan=None):
    s = proj.shape[0]
    t, nh = SB_TILE, SB_STEP_HEADS
    n_q = s // t
    scale = 1.0 / math.sqrt(SB_HEAD_DIM)

    def body(q_ref, k_ref, v_ref, do_ref, c_ref, first_ref, dq_ref, dk_ref, dv_ref, dq_acc, dk_acc, dv_acc):
        i = pl.program_id(1)
        kb0 = jnp.clip(first_ref[pl.program_id(0), i].astype(jnp.int32), 0, i)
        row, col, head0 = _sb_consts(t)
        upto = (row <= col).astype(BF16)
        before = (row < col).astype(BF16)
        valid = col < row
        lanes = lambda h: slice((h // 2) * LANES, (h // 2 + 1) * LANES)
        q2 = [jnp.where(head0 == (h % 2 == 0), q_ref[:, lanes(h)] * scale, 0) for h in range(nh)]
        do2 = [jnp.where(head0 == (h % 2 == 0), do_ref[:, lanes(h)], 0) for h in range(nh)]
        ctot2 = [c_ref[:, h:h + 1] for h in range(nh)]

        @pl.when(i == 0)
        def _():
            dk_acc[...] = jnp.zeros_like(dk_acc)
            dv_acc[...] = jnp.zeros_like(dv_acc)
        dq_acc[...] = jnp.zeros_like(dq_acc)

        def tiles(kbs, diag_last, carry):
            rows = [pl.ds(pl.multiple_of(kb * t, t), t) for kb in kbs]
            kt = {(n, h): k_ref[rows[n], lanes(h)] for n in range(len(kbs)) for h in range(nh)}
            jobs = list(kt)
            masked = lambda n: diag_last and n == len(kbs) - 1
            t_last = slice(t - 1, t)
            zs = {(n, h): _sb_logits(q2[h], kt[n, h]) for n, h in jobs}
            dw = {(n, h): _bdot(do2[h], v_ref[rows[n], lanes(h)], NT) for n, h in jobs}
            fail = {j: jnp.log2(1.0 + zs[j][1]) for j in jobs}
            fail = {j: jnp.where(valid, fail[j], 0.0) if masked(j[0]) else fail[j] for j in jobs}
            cum = {j: _tri_sum(fail[j], upto) for j in jobs}
            miss = {j: jnp.exp2(-fail[j]) for j in jobs}
            beta = {j: zs[j][1] * miss[j] for j in jobs}
            fail_run, fail_before = list(carry[0::2]), {}
            for n, h in jobs:
                fail_before[n, h] = fail_run[h]
                fail_run[h] = fail_run[h] + cum[n, h][:, t_last]
            w = {(n, h): beta[n, h] * jnp.exp2(fail_before[n, h] + cum[n, h] - ctot2[h]) for n, h in jobs}
            w = {j: jnp.where(valid, w[j], 0.0) if masked(j[0]) else w[j] for j in jobs}
            g = {j: w[j] * dw[j] for j in jobs}
            g_local = {j: _bdot(g[j], before, NN) for j in jobs}
            for n, h in jobs:
                dv_acc[rows[n], lanes(h)] += _bdot(w[n, h], do2[h], TN)
            g_run, dz = list(carry[1::2]), {}
            for n, h in jobs:
                g_sum = g_run[h] + g_local[n, h]
                dz[n, h] = g[n, h] * miss[n, h] - beta[n, h] * g_sum
                g_run[h] = g_sum[:, t_last] + g[n, h][:, t_last]
            dz = {j: jnp.where(valid, dz[j], 0.0) if masked(j[0]) else dz[j] for j in jobs}
            for n, h in jobs:
                dq_acc[h] += _bdot(dz[n, h], kt[n, h], NN)
                dk_acc[rows[n], lanes(h)] += _bdot(dz[n, h], q2[h], TN)
            return tuple(v for pair in zip(fail_run, g_run) for v in pair)

        zero = jnp.zeros((t, 1), F32)
        carry = lax.fori_loop(kb0, i - 1, lambda n, c: tiles([n], False, c), (zero,) * (2 * nh))

        @pl.when(i == 0)
        def _():
            tiles([i], True, carry)

        @pl.when(i > 0)
        def _():
            tiles([i - 1, i], True, carry)
        for b in range(nh // 2):
            dq_ref[:, b * LANES:(b + 1) * LANES] = (jnp.where(head0, dq_acc[2 * b], dq_acc[2 * b + 1])
                                                    * scale).astype(BF16)

        @pl.when(i == n_q - 1)
        def _():
            dk_ref[...] = dk_acc[...].astype(BF16)
            dv_ref[...] = dv_acc[...].astype(BF16)

    n_p, width = SB_HEADS // nh, nh * SB_HEAD_DIM
    k_blk, v_blk = SB_WIDTH // width, 2 * SB_WIDTH // width
    outs = _call(
        body, name=name, grid=(n_p, n_q),
        in_specs=[pl.BlockSpec((t, width), lambda p, i: (i, p)),
                  pl.BlockSpec((s, width), lambda p, i: (0, k_blk + p)),
                  pl.BlockSpec((s, width), lambda p, i: (0, v_blk + p)),
                  pl.BlockSpec((t, width), lambda p, i: (i, p)),
                  pl.BlockSpec((None, t, nh), lambda p, i: (p, i, 0)),
                  pl.BlockSpec(memory_space=pltpu.SMEM)],
        out_specs=[pl.BlockSpec((t, width), lambda p, i: (i, p)),
                   pl.BlockSpec((s, width), lambda p, i: (0, p)),
                   pl.BlockSpec((s, width), lambda p, i: (0, p))],
        out_shape=[jax.ShapeDtypeStruct((s, SB_WIDTH), BF16)] * 3,
        scratch=[pltpu.VMEM((nh, t, LANES), F32), pltpu.VMEM((s, width), F32), pltpu.VMEM((s, width), F32)],
        args=[proj, proj, proj, do_a, ctot, first], plan=plan)
    return jnp.concatenate(outs, axis=1)


def _mm_gathered(a, key, plan, *, name, out3=False, w_t=False):
    src = plan.gathering(key)
    if src is None:
        return _mm_nn(a, plan.weight(key), name=name, out3=out3, w_t=w_t, plan=plan)
    out, w_all = _mm_gathering(a, src, name=name, out3=out3, w_t=w_t)
    plan.set_weight(key, w_all)
    return out


def _local_step(x, mem, target, gains, plan):
    g_mix, g_memq, g_memkv, g_ffn, g_fin = gains
    d = x.shape[1]

    h0 = _rms_fwd(x, g_mix, name="rms_mix")
    proj = _mm_gathered(h0, "in", plan, name="mm_in")
    w_in = plan.weight("in")
    o_a, ctot, first = _sb_fwd(proj, name="sb_fwd", plan=plan)
    conv_w = plan.weight("conv")
    y_b = _conv_fwd(proj, conv_w, name="conv_fwd")
    w_a, w_b, w_mix = plan.weight("a"), plan.weight("b"), plan.weight("mix")
    x1, hq, merged, br_a, br_b = _mix_out(o_a, y_b, proj, w_a[0], w_b[0], w_mix[0], x, g_memq, name="mm_mix", plan=plan)
    w_mq, w_kv, w_mo = plan.weight("mq")[0], plan.weight("kv"), plan.weight("mo")[0]
    mn = _rms_fwd(mem, g_memkv, name="rms_memkv")
    kv = _mm_nn(mn, w_kv, name="mm_memkv")
    qm, om, x2, hf = _mem_sublayer(hq, w_mq, kv, w_mo, x1, g_ffn, name="mem_sublayer", plan=plan)
    gu = _mm_gathered(hf, "fi", plan, name="mm_ffn_in", out3=True, w_t=True)
    w_fi, w_fo = plan.weight("fi"), plan.weight("fo")
    dx3, dx3b, dg_fin, loss, act = _ffn_out_loss(gu, w_fo, x2, g_fin, target, name="mm_ffn_out")

    plan.grad("fo", _mm_tn_a3(act, dx3b, name="mm_d_w_ffn_out"))
    dgu = _ffn_out_bwd(dx3b, w_fo, gu, name="mm_d_act")
    plan.grad("fi", _mm_tn_a3(dgu, hf, name="mm_d_w_ffn_in"))
    dx2, dx2b, dg_ffn = _mm_nt_rms(dgu, w_fi, x2, g_ffn, dx3, name="mm_d_hf", dy3=True, w_nn=True, plan=plan)

    plan.grad("mo", _mm_tn(om, dx2b, d, name="mm_d_w_memo"))
    dx1, dx1b, dg_memq, dqm, dkv = _mem_sublayer_bwd(dx2b, dx2, x1, g_memq, qm, kv, w_mq, w_mo, name="mem_sublayer_bwd",
                                                    plan=plan)
    plan.grad("mq", _mm_tn(hq, dqm, d, name="mm_d_w_memq"))
    plan.grad("kv", _mm_tn(mn, dkv, w_kv.shape[2], name="mm_d_w_memkv"))
    _, _, dg_memkv = _mm_nt_rms(dkv, w_kv, mem, g_memkv, None, name="mm_d_mn")

    plan.grad("mix", _mm_tn(merged, dx1b, d, name="mm_d_w_mix"))
    dbr_a, dbr_b, dgab, do_a, dy_b = _mix_out_bwd(dx1b, w_mix[0], br_a, br_b, proj, w_a[0], w_b[0], name="mm_d_merged",
                                                 plan=plan)
    plan.grad("a", _mm_tn(o_a, dbr_a, d, name="mm_d_w_branch_a"))
    plan.grad("b", _mm_tn(y_b, dbr_b, d, name="mm_d_w_branch_b"))
    dconv, dconv_w = _conv_bwd(dy_b, proj, conv_w, name="conv_bwd", plan=plan)
    dqkv = _sb_bwd(proj, do_a, ctot, first, name="sb_bwd", plan=plan)
    dproj = jnp.concatenate([dqkv, dconv, dgab], axis=1)
    rows_in1 = d // IN_SPLIT[1] * (IN_SPLIT[1] - IN_SPLIT[0])
    plan.grad("in0", _mm_tn(h0, dproj, w_in.shape[2], name="mm_d_w_in0", tm=d - rows_in1, k_tiles=(0, 1)))
    plan.grad("in1", _mm_tn(h0, dproj, w_in.shape[2], name="mm_d_w_in1", tm=rows_in1,
                            k_tiles=(d // rows_in1 - 1, 1), plan=plan))
    dh0 = _mm_nt(dproj, w_in, name="mm_d_h0", out_dtype=F32, plan=plan)
    dx0, _, dg_mix = _rms_bwd(x, g_mix, dh0, dx1, name="rms_mix_bwd", plan=plan)

    return dx0, (dg_mix, dg_memq, dg_memkv, dg_ffn, dg_fin, dconv_w, loss)


def _row_tile(a, target=512):
    tm = min(a, target)
    while a % tm:
        tm -= 8
    return tm


def _sum_with_sibling(parts, recvs, core, *, name):
    n = len(parts)

    def body(core_ref, *refs):
        for p_ref, r_ref, o_ref in zip(refs[:n], refs[n:2 * n], refs[2 * n:]):
            o_ref[...] = (p_ref[...].astype(F32) + r_ref[...].astype(F32)).astype(o_ref.dtype)

    mine = [pl.BlockSpec((None,) + p.shape[1:], lambda q, core_ref: (2 * q + core_ref[0], 0, 0)) for p in parts]
    other = [pl.BlockSpec((None,) + p.shape[1:], lambda q, core_ref: (q, 0, 0)) for p in parts]
    return pl.pallas_call(
        body, name=name,
        grid_spec=pltpu.PrefetchScalarGridSpec(num_scalar_prefetch=1, grid=(N_CHIP,), in_specs=mine + other,
                                               out_specs=other),
        out_shape=[jax.ShapeDtypeStruct((N_CHIP,) + p.shape[1:], p.dtype) for p in parts],
        compiler_params=_params(1))(core, *parts, *recvs)


def _adam_math(wv, g, m, v):
    m = ADAM_B1 * m + (1.0 - ADAM_B1) * g
    v = ADAM_B2 * v + (1.0 - ADAM_B2) * (g * g)
    m_hat = m / (1.0 - ADAM_B1 ** ADAM_STEP)
    v_hat = v / (1.0 - ADAM_B2 ** ADAM_STEP)
    delta = -ADAM_LR * (m_hat / (jnp.sqrt(v_hat) + ADAM_EPS) + ADAM_WD * wv)
    return delta, m, v


def _adam_sharded(wv, m, v, own, recv, chip, *, name):
    a, b = wv.shape
    tm = _row_tile(a)

    def body(chip_ref, w_ref, m_ref, v_ref, own_ref, recv_ref, g_ref, d_ref, nm_ref, nv_ref):
        g = own_ref[...].astype(F32)
        for j in range(3):
            g = g + recv_ref[j].astype(F32)
        delta, nm, nv = _adam_math(w_ref[...], g, m_ref[...], v_ref[...])
        g_ref[...] = g
        d_ref[...] = delta
        nm_ref[...] = nm
        nv_ref[...] = nv

    tile = pl.BlockSpec((tm, b), lambda i, chip_ref: (i, 0))
    return pl.pallas_call(
        body, name=name,
        grid_spec=pltpu.PrefetchScalarGridSpec(
            num_scalar_prefetch=1, grid=(a // tm,),
            in_specs=[tile, tile, tile,
                      pl.BlockSpec((None, tm, b), lambda i, chip_ref: (chip_ref[0], i, 0)),
                      pl.BlockSpec((3, tm, b), lambda i, chip_ref: (0, i, 0))],
            out_specs=[tile] * 4),
        out_shape=[jax.ShapeDtypeStruct((a, b), F32)] * 4, compiler_params=_params(1))(chip, wv, m, v, own, recv)


def _sum_devices(gathered, *, name):
    _, r, c = gathered.shape

    def body(g_ref, o_ref):
        total = g_ref[0]
        for j in range(1, N_DEV):
            total = total + g_ref[j]
        o_ref[...] = total

    return pl.pallas_call(body, name=name, out_shape=jax.ShapeDtypeStruct((r, c), F32))(gathered)


def _adam_small(wv, g, m, v, *, name):
    def body(w_ref, g_ref, m_ref, v_ref, d_ref, nm_ref, nv_ref):
        delta, nm, nv = _adam_math(w_ref[...], g_ref[...], m_ref[...], v_ref[...])
        d_ref[...] = delta
        nm_ref[...] = nm
        nv_ref[...] = nv

    return pl.pallas_call(body, name=name, out_shape=[jax.ShapeDtypeStruct(wv.shape, F32)] * 3)(wv, g, m, v)


BIG = ("in", "a", "b", "mix", "mq", "kv", "mo", "fi", "fo")
ROW_SHARDED = ("mix", "mq", "mo")
UNSHARDED = ("a", "b")
FFN_GROUPS = 4
IN_SPLIT = (3, 4)
SMALL_ROWS = 16


class _Plan:
    FUSED = ("in",)
    GATHER_ON = {"sb_fwd": ("a", "b", "mix", "mq", "mo", "conv", "fi0"), "mm_mix": ("kv",), "mem_sublayer": ("fi1",),
                 "mm_ffn_in": ("fo",)}
    SIBLING_ON = {"mm_d_hf": ("fo", "fi"), "mm_d_merged": ("mo", "mq", "kv"), "conv_bwd": ("mix", "a", "b"),
                  "mm_d_w_in1": ("in0",), "mm_d_h0": ("in1",)}
    LATE_AT = {"mm_mix": (7, 8), "mm_ffn_in": (7, 8)}
    CHIPS_ON = {"mem_sublayer_bwd": ("fo",), "sb_bwd": ("fi", "mo", "mq", "kv", "mix", "a", "b"), "mm_d_h0": ("in0",),
                "rms_mix_bwd": ("in1",)}

    def __init__(self, shards, core):
        self.shards, self.core = shards, core
        self.w, self.parts, self.chip_sums, self.from_chips = {}, {}, {}, {}

    def gathering(self, k):
        return self.shards[k] if k in self.FUSED else None

    def comm(self, name):
        comms = []
        if name in self.GATHER_ON:
            comms.append(_gather_comm([self.shards[k] for k in self.GATHER_ON[name]]))
        if name in self.SIBLING_ON:
            comms.append(_sibling_comm([self.parts[k] for k in self.SIBLING_ON[name]]))
        if name in self.CHIPS_ON:
            comms.append(_chips_comm([self.chip_sums[k] for k in self.CHIPS_ON[name]]))
        if not comms:
            return None
        comm = _join_comms(comms)
        comm.late_at = self.LATE_AT.get(name, comm.late_at)
        return comm

    def landed(self, name, outs):
        outs = list(outs)
        for k in self.GATHER_ON.get(name, ()):
            self.set_weight(k, outs.pop(0))
        keys = self.SIBLING_ON.get(name, ())
        if keys:
            sums = _sum_with_sibling([self.parts[k] for k in keys], [outs.pop(0) for _ in keys], self.core,
                                     name="sum_with_sibling_" + "_".join(keys))
            self.chip_sums.update(zip(keys, sums))
        for k in self.CHIPS_ON.get(name, ()):
            self.from_chips[k] = outs.pop(0)

    def set_weight(self, k, gathered):
        _, a, b = gathered.shape
        if k in ROW_SHARDED:
            gathered = gathered.reshape(1, N_DEV * a, b)
        elif k in UNSHARDED:
            gathered = jnp.transpose(gathered, (1, 0, 2)).reshape(1, a, N_DEV * b)
        elif k == "fo":
            gathered = gathered.reshape(FFN_GROUPS, N_DEV * a // FFN_GROUPS, b)
        elif k == "conv":
            n_conv = CONV_WIDTH // N_DEV
            gathered = jnp.transpose(gathered[:, :3, :n_conv], (1, 0, 2)).reshape(3, CONV_WIDTH)
        self.w[k] = gathered
        if k == "fi1":
            self.w["fi"] = jnp.concatenate([self.w["fi0"], gathered], axis=2)

    def weight(self, k):
        return self.w[k]

    def grad(self, k, g):
        _, a, b = g.shape
        if k in ROW_SHARDED:
            g = g.reshape(N_DEV, a // N_DEV, b)
        elif k in UNSHARDED:
            g = jnp.transpose(g.reshape(a, N_DEV, b // N_DEV), (1, 0, 2))
        elif k == "fo":
            g = g.reshape(N_DEV, FFN_GROUPS * a // N_DEV, b)
        self.parts[k] = g


def kernel(x, mem, norm_mix, w_in, conv_w, w_branch_a, w_branch_b, w_mix_out, norm_mem_q, norm_mem_kv, w_mem_q, w_mem_kv, w_mem_o, norm_ffn, w_ffn_in, w_ffn_out, norm_final, loss_target, m_norm_mix, m_w_in, m_conv_w, m_w_branch_a, m_w_branch_b, m_w_mix_out, m_norm_mem_q, m_norm_mem_kv, m_w_mem_q, m_w_mem_kv, m_w_mem_o, m_norm_ffn, m_w_ffn_in, m_w_ffn_out, m_norm_final, v_norm_mix, v_w_in, v_conv_w, v_w_branch_a, v_w_branch_b, v_w_mix_out, v_norm_mem_q, v_norm_mem_kv, v_w_mem_q, v_w_mem_kv, v_w_mem_o, v_norm_ffn, v_w_ffn_in, v_w_ffn_out, v_norm_final):
    d = x.shape[-1]
    xi, yi, ci = lax.axis_index("x"), lax.axis_index("y"), lax.axis_index("c")
    chip = jnp.reshape(2 * xi + yi, (1,)).astype(jnp.int32)
    dev = 4 * xi + 2 * yi + ci

    big_w = dict(zip(BIG, (w_in, w_branch_a, w_branch_b, w_mix_out, w_mem_q, w_mem_kv, w_mem_o, w_ffn_in, w_ffn_out)))
    big_m = dict(zip(BIG, (m_w_in, m_w_branch_a, m_w_branch_b, m_w_mix_out, m_w_mem_q, m_w_mem_kv, m_w_mem_o, m_w_ffn_in, m_w_ffn_out)))
    big_v = dict(zip(BIG, (v_w_in, v_w_branch_a, v_w_branch_b, v_w_mix_out, v_w_mem_q, v_w_mem_kv, v_w_mem_o, v_w_ffn_in, v_w_ffn_out)))

    flip = lambda t, k: jnp.transpose(t) if k == "fi" else t
    shards = {k: flip(big_w[k][0], k).astype(BF16) for k in BIG}
    shards["fi0"], shards["fi1"] = shards["fi"][:, :d // 2], shards["fi"][:, d // 2:]
    n_conv = conv_w.shape[-1]
    shards["conv"] = jnp.zeros((8, LANES), F32).at[:3, :n_conv].set(conv_w[0])
    plan = _Plan(shards, jnp.reshape(ci, (1,)).astype(jnp.int32))

    gains = (norm_mix, norm_mem_q, norm_mem_kv, norm_ffn, norm_final.reshape(1, d))
    dx0, small = _local_step(x[0], mem[0], loss_target[0], gains, plan)

    grads, deltas, new_m, new_v = {}, {}, {}, {}
    for k in BIG:
        lead = big_w[k].shape
        wv, mv, vv = flip(big_w[k][0], k), flip(big_m[k][0], k), flip(big_v[k][0], k)
        if k == "in":
            half = wv.shape[0] * IN_SPLIT[0] // IN_SPLIT[1]
            lo = _adam_sharded(wv[:half], mv[:half], vv[:half], plan.chip_sums["in0"], plan.from_chips["in0"], chip,
                               name="adam_in0")
            hi = _adam_sharded(wv[half:], mv[half:], vv[half:], plan.chip_sums["in1"], plan.from_chips["in1"], chip,
                               name="adam_in1")
            outs = [jnp.concatenate(pair, axis=0) for pair in zip(lo, hi)]
        else:
            outs = _adam_sharded(wv, mv, vv, plan.chip_sums[k], plan.from_chips[k], chip, name="adam_" + k)
        grads[k], deltas[k], new_m[k], new_v[k] = (flip(t, k).reshape(lead) for t in outs)

    dg_mix, dg_memq, dg_memkv, dg_ffn, dg_fin, dconv_w, loss = small
    conv_rows = jnp.zeros((3, d), F32).at[:, :CONV_WIDTH].set(dconv_w[:3])
    block = jnp.concatenate([dg_mix[:1], dg_memq[:1], dg_memkv[:1], dg_ffn[:1], dg_fin[:1], conv_rows,
                             jnp.broadcast_to(loss[:1, :1], (1, d)), jnp.zeros((SMALL_ROWS - 9, d), F32)], axis=0)
    total = _sum_devices(_exchange(_gather_comm([block]), name="gather_small")[0], name="sum_small")
    g_conv = lax.dynamic_slice(total[5:8, :CONV_WIDTH], (0, dev * n_conv), (3, n_conv))
    small_w = [norm_mix, norm_mem_q, norm_mem_kv, norm_ffn, norm_final.reshape(1, d), conv_w[0]]
    small_m = [m_norm_mix, m_norm_mem_q, m_norm_mem_kv, m_norm_ffn, m_norm_final.reshape(1, d), m_conv_w[0]]
    small_v = [v_norm_mix, v_norm_mem_q, v_norm_mem_kv, v_norm_ffn, v_norm_final.reshape(1, d), v_conv_w[0]]
    small_g = [total[0:1], total[1:2], total[2:3], total[3:4], total[4:5], g_conv]
    small_names = ["norm_mix", "norm_mem_q", "norm_mem_kv", "norm_ffn", "norm_final", "conv_w"]
    sg, sd, sm, sv = {}, {}, {}, {}
    for nme, wv, g, m, v in zip(small_names, small_w, small_g, small_m, small_v):
        dl, nm, nv = _adam_small(wv, g, m, v, name="adam_" + nme)
        shape = norm_final.shape if nme == "norm_final" else (conv_w.shape if nme == "conv_w" else wv.shape)
        sg[nme], sd[nme], sm[nme], sv[nme] = (t.reshape(shape) for t in (g, dl, nm, nv))

    def ordered(big, sml):
        return (sml["norm_mix"], big["in"], sml["conv_w"], big["a"], big["b"], big["mix"], sml["norm_mem_q"],
                sml["norm_mem_kv"], big["mq"], big["kv"], big["mo"], sml["norm_ffn"], big["fi"], big["fo"],
                sml["norm_final"])

    loss_out = total[8, 0]
    grad_x = dx0.reshape(x.shape)
    return (loss_out, grad_x, *ordered(grads, sg), *ordered(deltas, sd), *ordered(new_m, sm), *ordered(new_v, sv))
```

```python
import functools
import math

import jax
import jax.numpy as jnp
from jax import lax
from jax.experimental import pallas as pl
from jax.experimental.pallas import tpu as pltpu

F32 = jnp.float32
BF16 = jnp.bfloat16
MESH = pl.DeviceIdType.MESH

N_DEV = 8
N_CHIP = 4
NORM_EPS = 1e-6
SB_HEADS = 8
SB_HEAD_DIM = 64
SB_WIDTH = SB_HEADS * SB_HEAD_DIM
CONV_WIDTH = 512
MEM_HEADS = 4
ADAM_LR = 0.001
ADAM_B1 = 0.9
ADAM_B2 = 0.999
ADAM_EPS = 1e-08
ADAM_WD = 0.01
ADAM_STEP = 10

LANES = 128
VMEM_LIMIT_BYTES = 52 * 1024 * 1024
SB_TILE = 256
SB_STEP_HEADS = 4
SB_DEAD = 159.0
SB_CLAMP = 126.0
LOG2_E = 1.4426950408889634

ANY = pl.BlockSpec(memory_space=pl.ANY)


def _params(n_grid):
    return pltpu.CompilerParams(dimension_semantics=("arbitrary",) * n_grid, vmem_limit_bytes=VMEM_LIMIT_BYTES)


def _bdot(a, b, dims):
    return lax.dot_general(a.astype(BF16), b.astype(BF16), (dims, ((), ())), preferred_element_type=F32)


NN = ((1,), (0,))
NT = ((1,), (1,))
TN = ((0,), (0,))


class _Comm:
    def __init__(self, ins, outs, n_sems, start, finish, late=None):
        self.ins, self.outs, self.n_sems, self.start, self.finish = ins, outs, n_sems, start, finish
        self.late = late if late is not None else (lambda ins, outs, sems: None)
        self.late_at = (1, 1)

    def sem_shapes(self):
        return [pltpu.SemaphoreType.DMA((k,)) for k in self.n_sems]


def _place():
    return lax.axis_index("x"), lax.axis_index("y"), lax.axis_index("c")


def _neighbours(x, y, c):
    return [(jnp.bitwise_xor(x, c), jnp.bitwise_xor(y, 1 - c)), (jnp.bitwise_xor(x, 1 - c), jnp.bitwise_xor(y, c)),
            (1 - x, 1 - y)]


def _gather_comm(shards):
    n = len(shards)

    def copies(ins, outs, sems):
        send_sems, recv_sems, _ = sems
        x, y, c = _place()
        chips = [(1 - x, y), (x, 1 - y), (1 - x, 1 - y)]

        def copy(a, k, block, to, from_shard=False):
            dst = outs[a].at[4 * block[0] + 2 * block[1] + block[2]]
            return pltpu.make_async_remote_copy(
                src_ref=ins[a] if from_shard else dst, dst_ref=dst, send_sem=send_sems.at[a * 7 + k],
                recv_sem=recv_sems.at[a * 7 + k], device_id=to, device_id_type=MESH)

        me, sibling = (x, y, c), (x, y, 1 - c)
        own = [[copy(a, 0, me, sibling, True)] + [copy(a, 1 + j, me, (*chip, c), True) for j, chip in enumerate(chips)]
               for a in range(n)]
        landed = [[copy(a, 1 + j, (*chip, c), me) for j, chip in enumerate(chips)] for a in range(n)]
        passed = [[copy(a, 4 + j, (*chip, c), sibling) for j, chip in enumerate(chips)] for a in range(n)]
        from_sibling = [[copy(a, 0, sibling, me)] + [copy(a, 4 + j, (*chip, 1 - c), me) for j, chip in enumerate(chips)]
                        for a in range(n)]
        local = [pltpu.make_async_copy(ins[a], outs[a].at[4 * x + 2 * y + c], sems[2].at[a]) for a in range(n)]
        return own, landed, passed, from_sibling, local

    def start(ins, outs, sems):
        own, _, _, _, local = copies(ins, outs, sems)
        for a in range(n):
            local[a].start()
            for cp in own[a]:
                cp.start()

    def late(ins, outs, sems):
        _, landed, passed, _, _ = copies(ins, outs, sems)
        for a in range(n):
            for arrived, onward in zip(landed[a], passed[a]):
                arrived.wait_recv()
                onward.start()

    def finish(ins, outs, sems):
        own, _, passed, from_sibling, local = copies(ins, outs, sems)
        for a in range(n):
            for cp in from_sibling[a]:
                cp.wait_recv()
        for a in range(n):
            for cp in own[a] + passed[a]:
                cp.wait_send()
            local[a].wait()

    outs = [jax.ShapeDtypeStruct((N_DEV,) + s.shape, s.dtype) for s in shards]
    return _Comm(list(shards), outs, (7 * n, 7 * n, n), start, finish, late)


def _sibling_comm(parts):
    n = len(parts)

    def copies(ins, outs, sems):
        x, y, c = _place()
        return [pltpu.make_async_remote_copy(
            src_ref=ins[a].at[2 * q + 1 - c], dst_ref=outs[a].at[q], send_sem=sems[0].at[a * N_CHIP + q],
            recv_sem=sems[1].at[a * N_CHIP + q], device_id=(x, y, 1 - c), device_id_type=MESH)
            for a in range(n) for q in range(N_CHIP)]

    def start(ins, outs, sems):
        for cp in copies(ins, outs, sems):
            cp.start()

    def finish(ins, outs, sems):
        cps = copies(ins, outs, sems)
        for cp in cps:
            cp.wait_recv()
        for cp in cps:
            cp.wait_send()

    outs = [jax.ShapeDtypeStruct((N_CHIP,) + p.shape[1:], p.dtype) for p in parts]
    return _Comm(list(parts), outs, (N_CHIP * n, N_CHIP * n), start, finish)


def _chips_comm(parts):
    n = len(parts)

    def copies(ins, outs, sems):
        x, y, c = _place()
        chips = [(1 - x, y), (x, 1 - y), (1 - x, 1 - y)]
        return [pltpu.make_async_remote_copy(
            src_ref=ins[a].at[2 * px + py], dst_ref=outs[a].at[j], send_sem=sems[0].at[a * 3 + j],
            recv_sem=sems[1].at[a * 3 + j], device_id=(px, py, c), device_id_type=MESH)
            for a in range(n) for j, (px, py) in enumerate(chips)]

    def start(ins, outs, sems):
        for cp in copies(ins, outs, sems):
            cp.start()

    def finish(ins, outs, sems):
        cps = copies(ins, outs, sems)
        for cp in cps:
            cp.wait_recv()
        for cp in cps:
            cp.wait_send()

    outs = [jax.ShapeDtypeStruct((3,) + p.shape[1:], p.dtype) for p in parts]
    return _Comm(list(parts), outs, (3 * n, 3 * n), start, finish)


def _join_comms(comms):
    if len(comms) == 1:
        return comms[0]

    def split(refs, counts):
        out, at = [], 0
        for n in counts:
            out.append(refs[at:at + n])
            at += n
        return out

    def each(method):
        def run(ins, outs, sems):
            parts = zip(comms, split(ins, [len(c.ins) for c in comms]), split(outs, [len(c.outs) for c in comms]),
                        split(sems, [len(c.n_sems) for c in comms]))
            for c, c_ins, c_outs, c_sems in parts:
                getattr(c, method)(c_ins, c_outs, c_sems)
        return run

    return _Comm([a for c in comms for a in c.ins], [o for c in comms for o in c.outs],
                 tuple(k for c in comms for k in c.n_sems), each("start"), each("finish"), each("late"))


def _exchange(comm, *, name):
    n_ci, n_co = len(comm.ins), len(comm.outs)

    def kern(*refs):
        c_ins, c_outs, sems = refs[:n_ci], refs[n_ci:n_ci + n_co], refs[n_ci + n_co:]
        comm.start(c_ins, c_outs, sems)
        comm.late(c_ins, c_outs, sems)
        comm.finish(c_ins, c_outs, sems)

    return pl.pallas_call(kern, name=name, in_specs=[ANY] * n_ci, out_specs=[ANY] * n_co, out_shape=comm.outs,
                          scratch_shapes=comm.sem_shapes())(*comm.ins)


def _call(body, *, name, grid, in_specs, out_specs, out_shape, scratch, args, plan=None):
    comm = plan.comm(name) if plan is not None else None
    if comm is None:
        return list(pl.pallas_call(functools.partial(body), name=name, grid=grid, in_specs=in_specs,
                                   out_specs=out_specs, out_shape=out_shape, scratch_shapes=scratch,
                                   compiler_params=_params(len(grid)))(*args))
    n_in, n_out, n_scr, n_ci, n_co = len(in_specs), len(out_specs), len(scratch), len(comm.ins), len(comm.outs)

    def kern(*refs):
        ins, c_ins, refs = refs[:n_in], refs[n_in:n_in + n_ci], refs[n_in + n_ci:]
        outs, c_outs, refs = refs[:n_out], refs[n_out:n_out + n_co], refs[n_out + n_co:]
        scr, sems = refs[:n_scr], refs[n_scr:]
        ids = [pl.program_id(ax) for ax in range(len(grid))]
        step = functools.reduce(lambda at, ig: at * ig[1] + ig[0], zip(ids, grid), 0)
        n_steps = math.prod(grid)

        @pl.when(step == 0)
        def _():
            comm.start(c_ins, c_outs, sems)

        @pl.when(step == min(n_steps * comm.late_at[0] // comm.late_at[1], n_steps - 1))
        def _():
            comm.late(c_ins, c_outs, sems)
        body(*ins, *outs, *scr)

        @pl.when(step == n_steps - 1)
        def _():
            comm.finish(c_ins, c_outs, sems)

    res = pl.pallas_call(kern, name=name, grid=grid, in_specs=list(in_specs) + [ANY] * n_ci,
                         out_specs=list(out_specs) + [ANY] * n_co, out_shape=list(out_shape) + comm.outs,
                         scratch_shapes=list(scratch) + comm.sem_shapes(),
                         compiler_params=_params(len(grid)))(*args, *comm.ins)
    plan.landed(name, list(res[n_out:]))
    return list(res[:n_out])


def _mm_body(dims, has_add, *refs):
    if has_add:
        a_ref, b_ref, add_ref, o_ref = refs
        total = _bdot(a_ref[...], b_ref[...], dims) + add_ref[...]
    else:
        a_ref, b_ref, o_ref = refs
        total = _bdot(a_ref[...], b_ref[...], dims)
    o_ref[...] = total.astype(o_ref.dtype)


def _mm_nt_body(j, n, dy_ref, w_ref, o_ref):
    total = _bdot(dy_ref[:, 0:n], w_ref[0], NT)
    for jj in range(1, j):
        total = total + _bdot(dy_ref[:, jj * n:(jj + 1) * n], w_ref[jj], NT)
    o_ref[...] = total.astype(o_ref.dtype)


def _mm_nn(a, w3, *, name, out_dtype=BF16, add=None, tm=1024, tn=None, out3=False, w_t=False, plan=None):
    m, kk = a.shape
    j, n = w3.shape[0], w3.shape[1 if w_t else 2]
    tm, tn = min(tm, m), n if tn is None else tn
    n_t = n // tn
    in_specs = [pl.BlockSpec((tm, kk), lambda i, jj: (i, 0)),
                pl.BlockSpec((None, tn, kk), lambda i, jj: (jj // n_t, jj % n_t, 0)) if w_t else
                pl.BlockSpec((None, kk, tn), lambda i, jj: (jj // n_t, 0, jj % n_t))]
    args = [a, w3]
    if add is not None:
        in_specs.append(pl.BlockSpec((tm, tn), lambda i, jj: (i, jj)))
        args.append(add)
    if out3:
        out_spec = pl.BlockSpec((None, tm, tn), lambda i, jj: (jj // n_t, i, jj % n_t))
        out_shape = jax.ShapeDtypeStruct((j, m, n), out_dtype)
    else:
        out_spec = pl.BlockSpec((tm, tn), lambda i, jj: (i, jj))
        out_shape = jax.ShapeDtypeStruct((m, j * n), out_dtype)
    return _call(
        functools.partial(_mm_body, NT if w_t else NN, add is not None), name=name, grid=(m // tm, j * n_t),
        in_specs=in_specs, out_specs=[out_spec], out_shape=[out_shape], scratch=[], args=args, plan=plan)[0]


def _mm_gathering(a, shard, *, name, out3=False, w_t=False, tm=1024):
    m, kk = a.shape
    n = shard.shape[0 if w_t else 1]
    tm = min(tm, m)
    n_i = m // tm
    fetch_at = min(1, n_i - 1)

    def body(a_ref, shard_ref, o_ref, w_all, w_vmem, send_sems, recv_sems, copy_sems):
        jj, i = pl.program_id(0), pl.program_id(1)
        x, y, c = _place()
        me, sibling = (x, y, c), (x, y, 1 - c)
        chips = _neighbours(x, y, c)
        sibling_chips = [chips[1], chips[0], chips[2]]

        def rows(block):
            return w_all.at[4 * block[0] + 2 * block[1] + block[2]]

        def remote(k, block, to, from_shard=False):
            return pltpu.make_async_remote_copy(
                src_ref=shard_ref if from_shard else rows(block), dst_ref=rows(block), send_sem=send_sems.at[k],
                recv_sem=recv_sems.at[k], device_id=to, device_id_type=MESH)

        def load(step, src):
            return pltpu.make_async_copy(src, w_vmem.at[step % 2], copy_sems.at[1 + step % 2])

        own = [remote(0, me, sibling, True), remote(1, me, (*chips[0], c), True), remote(2, me, (*chips[1], c), True),
               remote(3, (*chips[0], c), (*chips[1], c))]
        passed = [remote(4 + j, (*chip, c), sibling) for j, chip in enumerate(chips)]
        local = pltpu.make_async_copy(shard_ref, rows(me), copy_sems.at[0])

        @pl.when(jnp.logical_and(i == 0, jj == 0))
        def _():
            local.start()
            own[0].start()
            own[1].start()
            load(0, shard_ref).start()

        def arrivals():
            yield 1, (lambda: remote(0, sibling, me).wait_recv()), sibling
            for j, chip in enumerate(chips):
                def landed(j=j, chip=chip):
                    if j < 2:
                        own[1 + j].wait_send()
                        own[2 + j].start()
                    remote(1 + j, (*chip, c), me).wait_recv()
                    passed[j].start()
                yield 2 + 2 * j, landed, (*chip, c)
                block = (*sibling_chips[j], 1 - c)
                yield 3 + 2 * j, (lambda j=j, block=block: remote(4 + j, block, me).wait_recv()), block

        for step, wait_for_it, block in arrivals():
            @pl.when(jnp.logical_and(i == fetch_at, jj == step - 1))
            def _():
                wait_for_it()
                load(step, rows(block)).start()

        for step in range(N_DEV):
            @pl.when(jnp.logical_and(i == 0, jj == step))
            def _():
                load(step, rows(me)).wait()

        o_ref[...] = _bdot(a_ref[...], w_vmem[lax.rem(jj, 2)], NT if w_t else NN).astype(o_ref.dtype)

        @pl.when(jnp.logical_and(i == n_i - 1, jj == N_DEV - 1))
        def _():
            for cp in [own[0], own[3]] + passed:
                cp.wait_send()
            local.wait()

    def swept(jj):
        x, y, c = _place()
        first, second = 2 + 2 * c, 4 - 2 * c
        flips = (0b000, 0b001, first, second + 1, second, first + 1, 0b110, 0b111)
        return jnp.bitwise_xor(4 * x + 2 * y + c, sum(jnp.where(jj == k, f, 0) for k, f in enumerate(flips)))

    if out3:
        out_spec = pl.BlockSpec((None, tm, n), lambda jj, i: (swept(jj), i, 0))
        out_shape = jax.ShapeDtypeStruct((N_DEV, m, n), BF16)
    else:
        out_spec = pl.BlockSpec((tm, n), lambda jj, i: (i, swept(jj)))
        out_shape = jax.ShapeDtypeStruct((m, N_DEV * n), BF16)
    return pl.pallas_call(
        body, name=name, grid=(N_DEV, n_i),
        in_specs=[pl.BlockSpec((tm, kk), lambda jj, i: (i, 0)), ANY], out_specs=[out_spec, ANY],
        scratch_shapes=[pltpu.VMEM((2,) + shard.shape, shard.dtype), pltpu.SemaphoreType.DMA((7,)),
                        pltpu.SemaphoreType.DMA((7,)), pltpu.SemaphoreType.DMA((3,))],
        out_shape=[out_shape, jax.ShapeDtypeStruct((N_DEV,) + shard.shape, shard.dtype)],
        compiler_params=_params(2))(a, shard)


def _sigmoid(v):
    return 0.5 * jnp.tanh(0.5 * v) + 0.5


def _resident(w):
    return pl.BlockSpec(w.shape, lambda i: (0,) * w.ndim, pipeline_mode=pl.Buffered(1))


def _ffn_out_loss(gu3, w3, add, g, target, *, name, tm=512):
    j2, m, n = gu3.shape
    j = j2 // 2
    nn = w3.shape[2]
    tm = min(tm, m)

    def body(gu_ref, w_ref, add_ref, g_ref, t_ref, dx_ref, dxb_ref, dg_ref, loss_ref, act_ref):
        i = pl.program_id(0)
        xv = add_ref[...]
        for jj in range(j):
            gate = gu_ref[0, jj].astype(F32)
            act = (gate * _sigmoid(gate) * gu_ref[1, jj].astype(F32)).astype(BF16)
            act_ref[jj] = act
            xv = xv + _bdot(act, w_ref[jj], NN)
        gv = g_ref[...]
        r = lax.rsqrt(jnp.mean(xv * xv, axis=-1, keepdims=True) + NORM_EPS)
        xhat = xv * r
        err = xhat * gv - t_ref[...]
        _acc_rows(i, loss_ref, 0.5 * jnp.sum(jnp.mean(err * err, axis=-1, keepdims=True), axis=0, keepdims=True))
        dy = err * (1.0 / nn)
        dxhat = dy * gv
        dx = r * (dxhat - xhat * jnp.mean(dxhat * xhat, axis=-1, keepdims=True))
        dx_ref[...] = dx
        dxb_ref[...] = dx.astype(BF16)
        _acc_rows(i, dg_ref, jnp.sum(dy * xhat, axis=0, keepdims=True))

    row = pl.BlockSpec((tm, nn), lambda i: (i, 0))
    return _call(body, name=name, grid=(m // tm,),
                 in_specs=[pl.BlockSpec((2, j, tm, n), lambda i: (0, 0, i, 0)), _resident(w3),
                           row, pl.BlockSpec(g.shape, lambda i: (0, 0)), row],
                 out_specs=[row, row, pl.BlockSpec((8, nn), lambda i: (0, 0)), pl.BlockSpec((8, LANES), lambda i: (0, 0)),
                            pl.BlockSpec((j, tm, n), lambda i: (0, i, 0))],
                 out_shape=[jax.ShapeDtypeStruct((m, nn), F32), jax.ShapeDtypeStruct((m, nn), BF16),
                            jax.ShapeDtypeStruct((8, nn), F32), jax.ShapeDtypeStruct((8, LANES), F32),
                            jax.ShapeDtypeStruct((j, m, n), BF16)],
                 scratch=[], args=[gu3.reshape(2, j, m, n), w3, add, g, target])


def _ffn_out_bwd(dy, w3, gu3, *, name, tm=1024):
    m, nn = dy.shape
    j, n, _ = w3.shape
    tm = min(tm, m)

    def body(dy_ref, w_ref, gu_ref, dgu_ref):
        da = _bdot(dy_ref[...], w_ref[...], NT)
        gate = gu_ref[0].astype(F32)
        up = gu_ref[1].astype(F32)
        sg = _sigmoid(gate)
        silu = gate * sg
        dgu_ref[0] = (da * up * (sg + silu * (1.0 - sg))).astype(BF16)
        dgu_ref[1] = (da * silu).astype(BF16)

    out = _call(body, name=name, grid=(m // tm, j),
                in_specs=[pl.BlockSpec((tm, nn), lambda i, jj: (i, 0)),
                          pl.BlockSpec((None, n, nn), lambda i, jj: (jj, 0, 0)),
                          pl.BlockSpec((2, None, tm, n), lambda i, jj: (0, jj, i, 0))],
                out_specs=[pl.BlockSpec((2, None, tm, n), lambda i, jj: (0, jj, i, 0))],
                out_shape=[jax.ShapeDtypeStruct((2, j, m, n), BF16)], scratch=[],
                args=[dy, w3, gu3.reshape(2, j, m, n)])[0]
    return out.reshape(2 * j, m, n)


def _rms_fwd_tail(xv, g_ref, h_ref):
    r = lax.rsqrt(jnp.mean(xv * xv, axis=-1, keepdims=True) + NORM_EPS)
    h_ref[...] = (xv * r * g_ref[...]).astype(BF16)


def _rms_bwd_tail(i, dh, x_ref, g_ref, dres_ref, dx_ref, dxb_ref, dg_ref):
    xv = x_ref[...]
    r = lax.rsqrt(jnp.mean(xv * xv, axis=-1, keepdims=True) + NORM_EPS)
    xhat = xv * r
    dxhat = dh * g_ref[...]
    dx = r * (dxhat - xhat * jnp.mean(dxhat * xhat, axis=-1, keepdims=True))
    if dres_ref is not None:
        dx = dx + dres_ref[...]
    dx_ref[...] = dx
    dxb_ref[...] = dx.astype(BF16)
    _acc_rows(i, dg_ref, jnp.sum(dh * xhat, axis=0, keepdims=True))


def _mm_nt_rms(dy, w3, x, g, dres, *, name, dy3=False, w_nn=False, tm=512, plan=None):
    j = w3.shape[0]
    m, kk = x.shape
    n = dy.shape[2] if dy3 else dy.shape[1] // j
    tm = min(tm, m)

    def body(dy_ref, w_ref, x_ref, g_ref, *rest):
        dres_ref = rest[0] if dres is not None else None
        dx_ref, dxb_ref, dg_ref = rest[-3:]
        dh = None
        for jj in range(j):
            piece = dy_ref[jj] if dy3 else dy_ref[:, jj * n:(jj + 1) * n]
            part = _bdot(piece, w_ref[jj], NN if w_nn else NT)
            dh = part if dh is None else dh + part
        _rms_bwd_tail(pl.program_id(0), dh, x_ref, g_ref, dres_ref, dx_ref, dxb_ref, dg_ref)

    row = pl.BlockSpec((tm, kk), lambda i: (i, 0))
    in_specs = [pl.BlockSpec((j, tm, n), lambda i: (0, i, 0)) if dy3 else pl.BlockSpec((tm, j * n), lambda i: (i, 0)),
                _resident(w3), row, pl.BlockSpec(g.shape, lambda i: (0, 0))]
    args = [dy, w3, x, g]
    if dres is not None:
        in_specs.append(row)
        args.append(dres)
    return _call(body, name=name, grid=(m // tm,), in_specs=in_specs,
                 out_specs=[row, row, pl.BlockSpec((8, kk), lambda i: (0, 0))],
                 out_shape=[jax.ShapeDtypeStruct((m, kk), F32), jax.ShapeDtypeStruct((m, kk), BF16),
                            jax.ShapeDtypeStruct((8, kk), F32)], scratch=[], args=args, plan=plan)


def _mix_out(o_a, y_b, proj, w_a, w_b, w, x, g, *, name, tm=512, plan=None):
    s, c = o_a.shape
    d = w.shape[1]
    tm = min(tm, s)

    def body(oa_ref, yb_ref, ga_ref, gb_ref, wa_ref, wb_ref, w_ref, x_ref, g_ref, x1_ref, h_ref, merged_ref, a_ref, b_ref):
        a_ref[...] = _bdot(oa_ref[...], wa_ref[...], NN).astype(BF16)
        b_ref[...] = _bdot(yb_ref[...], wb_ref[...], NN).astype(BF16)
        merged = (_sigmoid(ga_ref[...].astype(F32)) * a_ref[...].astype(F32)
                  + _sigmoid(gb_ref[...].astype(F32)) * b_ref[...].astype(F32)).astype(BF16)
        merged_ref[...] = merged
        xv = _bdot(merged, w_ref[...], NN) + x_ref[...]
        x1_ref[...] = xv
        _rms_fwd_tail(xv, g_ref, h_ref)

    row = pl.BlockSpec((tm, d), lambda i: (i, 0))
    narrow = pl.BlockSpec((tm, c), lambda i: (i, 0))
    whole = lambda arr: pl.BlockSpec(arr.shape, lambda i: (0,) * arr.ndim)
    return _call(body, name=name, grid=(s // tm,),
                 in_specs=[narrow, narrow, pl.BlockSpec((tm, d), lambda i: (i, 3)), pl.BlockSpec((tm, d), lambda i: (i, 4)),
                           whole(w_a), whole(w_b), whole(w), row, whole(g)],
                 out_specs=[row] * 5,
                 out_shape=[jax.ShapeDtypeStruct((s, d), F32)] + [jax.ShapeDtypeStruct((s, d), BF16)] * 4,
                 scratch=[], args=[o_a, y_b, proj, proj, w_a, w_b, w, x, g], plan=plan)


def _mm_tn_a3(a3, dy, *, name):
    j, t, n = a3.shape
    nn = dy.shape[1]
    return _call(functools.partial(_mm_body, TN, False), name=name, grid=(j,),
                 in_specs=[pl.BlockSpec((None, t, n), lambda jj: (jj, 0, 0)), pl.BlockSpec((t, nn), lambda jj: (0, 0))],
                 out_specs=[pl.BlockSpec((None, n, nn), lambda jj: (jj, 0, 0))],
                 out_shape=[jax.ShapeDtypeStruct((j, n, nn), BF16)], scratch=[], args=[a3, dy])[0]


def _mm_nt(dy, w3, *, name, out_dtype=BF16, tm=512, tn=1024, plan=None):
    m = dy.shape[0]
    j, kk, n = w3.shape
    tm, tn = min(tm, m), min(tn, kk)
    return _call(
        functools.partial(_mm_nt_body, j, n), name=name,
        grid=(m // tm, kk // tn),
        in_specs=[pl.BlockSpec((tm, j * n), lambda i, q: (i, 0)),
                  pl.BlockSpec((j, tn, n), lambda i, q: (0, q, 0))],
        out_specs=[pl.BlockSpec((tm, tn), lambda i, q: (i, q))],
        out_shape=[jax.ShapeDtypeStruct((m, kk), out_dtype)], scratch=[], args=[dy, w3], plan=plan)[0]


def _mm_tn(a, dy, n, *, name, out_dtype=BF16, tm=512, tn=None, k_tiles=None, plan=None):
    t, kk = a.shape
    j = dy.shape[1] // n
    tm, tn = min(tm, kk), n if tn is None else tn
    n_t = n // tn
    first, count = (0, kk // tm) if k_tiles is None else k_tiles
    return _call(
        functools.partial(_mm_body, TN, False), name=name,
        grid=(count, j * n_t),
        in_specs=[pl.BlockSpec((t, tm), lambda i, jj: (0, first + i)),
                  pl.BlockSpec((t, tn), lambda i, jj: (0, jj))],
        out_specs=[pl.BlockSpec((None, tm, tn), lambda i, jj: (jj // n_t, i, jj % n_t))],
        out_shape=[jax.ShapeDtypeStruct((j, count * tm, n), out_dtype)], scratch=[], args=[a, dy], plan=plan)[0]


def _rows(body, ins, outs, *, n_rows, tm, name, plan=None):
    tm = min(tm, n_rows)
    n_steps = n_rows // tm
    in_specs, args = [], []
    for arr, kind, width, block in ins:
        if kind == "row":
            in_specs.append(pl.BlockSpec((tm, width), functools.partial(lambda i, b: (i, b), b=block)))
        elif kind == "prev":
            in_specs.append(pl.BlockSpec((tm, width), functools.partial(lambda i, b: (jnp.maximum(i - 1, 0), b), b=block)))
        elif kind == "next":
            in_specs.append(pl.BlockSpec((tm, width), functools.partial(lambda i, b: (jnp.minimum(i + 1, n_steps - 1), b), b=block)))
        else:
            in_specs.append(pl.BlockSpec(arr.shape, functools.partial(lambda i, nd: (0,) * nd, nd=arr.ndim)))
        args.append(arr)
    out_specs, out_shape = [], []
    for shape, dtype, kind in outs:
        if kind == "row":
            out_specs.append(pl.BlockSpec((tm, shape[1]), lambda i: (i, 0)))
        else:
            out_specs.append(pl.BlockSpec(shape, functools.partial(lambda i, nd: (0,) * nd, nd=len(shape))))
        out_shape.append(jax.ShapeDtypeStruct(shape, dtype))

    def kern(*refs):
        body(pl.program_id(0), n_steps, *refs)

    return _call(kern, name=name, grid=(n_steps,), in_specs=in_specs, out_specs=out_specs, out_shape=out_shape,
                 scratch=[], args=args, plan=plan)


def _acc_rows(i, ref, value):
    @pl.when(i == 0)
    def _():
        ref[...] = jnp.zeros_like(ref)
    ref[...] += jnp.broadcast_to(value, ref.shape)


def _rms_fwd(x, g, *, name, tm=512):
    s, d = x.shape

    def body(i, n, x_ref, g_ref, h_ref):
        _rms_fwd_tail(x_ref[...], g_ref, h_ref)

    return _rows(body, [(x, "row", d, 0), (g, "full", 0, 0)], [((s, d), BF16, "row")], n_rows=s, tm=tm, name=name)[0]


def _rms_bwd(x, g, dh, dres, *, name, tm=512, plan=None):
    s, d = x.shape

    def body(i, n, x_ref, g_ref, dh_ref, dres_ref, dx_ref, dxb_ref, dg_ref):
        _rms_bwd_tail(i, dh_ref[...].astype(F32), x_ref, g_ref, dres_ref, dx_ref, dxb_ref, dg_ref)

    return _rows(body, [(x, "row", d, 0), (g, "full", 0, 0), (dh, "row", d, 0), (dres, "row", d, 0)],
                 [((s, d), F32, "row"), ((s, d), BF16, "row"), ((8, d), F32, "acc")],
                 n_rows=s, tm=tm, name=name, plan=plan)


def _mix_out_bwd(dx1b, w, br_a, br_b, proj, w_a, w_b, *, name, tm=512, plan=None):
    s, d = br_a.shape
    c = w_a.shape[0]
    tm = min(tm, s)

    def body(dy_ref, w_ref, a_ref, b_ref, ga_ref, gb_ref, wa_ref, wb_ref, da_ref, db_ref, dg_ref, doa_ref, dyb_ref):
        dm = _bdot(dy_ref[...], w_ref[...], NT)
        sa = _sigmoid(ga_ref[...].astype(F32))
        sb = _sigmoid(gb_ref[...].astype(F32))
        da_ref[...] = (dm * sa).astype(BF16)
        db_ref[...] = (dm * sb).astype(BF16)
        dg_ref[:, :d] = (dm * a_ref[...].astype(F32) * sa * (1.0 - sa)).astype(BF16)
        dg_ref[:, d:] = (dm * b_ref[...].astype(F32) * sb * (1.0 - sb)).astype(BF16)
        doa_ref[...] = _bdot(da_ref[...], wa_ref[...], NT).astype(BF16)
        dyb_ref[...] = _bdot(db_ref[...], wb_ref[...], NT).astype(BF16)

    row = pl.BlockSpec((tm, d), lambda i: (i, 0))
    narrow = pl.BlockSpec((tm, c), lambda i: (i, 0))
    whole = lambda arr: pl.BlockSpec(arr.shape, lambda i: (0,) * arr.ndim)
    return _call(body, name=name, grid=(s // tm,),
                 in_specs=[row, whole(w), row, row, pl.BlockSpec((tm, d), lambda i: (i, 3)),
                           pl.BlockSpec((tm, d), lambda i: (i, 4)), whole(w_a), whole(w_b)],
                 out_specs=[row, row, pl.BlockSpec((tm, 2 * d), lambda i: (i, 0)), narrow, narrow],
                 out_shape=[jax.ShapeDtypeStruct((s, d), BF16), jax.ShapeDtypeStruct((s, d), BF16),
                            jax.ShapeDtypeStruct((s, 2 * d), BF16), jax.ShapeDtypeStruct((s, c), BF16),
                            jax.ShapeDtypeStruct((s, c), BF16)],
                 scratch=[], args=[dx1b, w, br_a, br_b, proj, proj, w_a, w_b], plan=plan)


def _shift_down(cur, prev, k, first):
    row = lax.broadcasted_iota(jnp.int32, cur.shape, 0)
    out = jnp.where(row >= k, pltpu.roll(cur, k, 0), pltpu.roll(prev, k, 0))
    return jnp.where(jnp.logical_and(first, row < k), 0.0, out)


def _shift_up(cur, nxt, k, last):
    tm = cur.shape[0]
    row = lax.broadcasted_iota(jnp.int32, cur.shape, 0)
    out = jnp.where(row < tm - k, pltpu.roll(cur, tm - k, 0), pltpu.roll(nxt, tm - k, 0))
    return jnp.where(jnp.logical_and(last, row >= tm - k), 0.0, out)


def _conv_fwd(proj, conv_w, *, name, tm=512):
    s = proj.shape[0]
    c = CONV_WIDTH

    def body(i, n, u_ref, gb_ref, gc_ref, up_ref, gcp_ref, w_ref, y_ref):
        cu = gc_ref[...].astype(F32) * u_ref[...].astype(F32)
        cup = gcp_ref[...].astype(F32) * up_ref[...].astype(F32)
        first = i == 0
        y = (w_ref[0:1, :] * _shift_down(cu, cup, 2, first) + w_ref[1:2, :] * _shift_down(cu, cup, 1, first)
             + w_ref[2:3, :] * cu)
        y_ref[...] = (gb_ref[...].astype(F32) * y).astype(BF16)

    return _rows(body, [(proj, "row", c, 3), (proj, "row", c, 4), (proj, "row", c, 5),
                        (proj, "prev", c, 3), (proj, "prev", c, 5), (conv_w, "full", 0, 0)],
                 [((s, c), BF16, "row")], n_rows=s, tm=tm, name=name)[0]


def _conv_bwd(dy_b, proj, conv_w, *, name, tm=512, plan=None):
    s = proj.shape[0]
    c = CONV_WIDTH

    def body(i, n, dy_ref, u_ref, gb_ref, gc_ref, up_ref, gcp_ref, dyn_ref, gbn_ref, w_ref, d_ref, dw_ref):
        first, last = i == 0, i == n - 1
        u = u_ref[...].astype(F32)
        gb = gb_ref[...].astype(F32)
        gc = gc_ref[...].astype(F32)
        cu = gc * u
        cup = gcp_ref[...].astype(F32) * up_ref[...].astype(F32)
        cu1 = _shift_down(cu, cup, 1, first)
        cu2 = _shift_down(cu, cup, 2, first)
        conv = w_ref[0:1, :] * cu2 + w_ref[1:2, :] * cu1 + w_ref[2:3, :] * cu
        dy = dy_ref[...].astype(F32)
        dyc = dy * gb
        dycn = dyn_ref[...].astype(F32) * gbn_ref[...].astype(F32)
        dcu = (w_ref[2:3, :] * dyc + w_ref[1:2, :] * _shift_up(dyc, dycn, 1, last)
               + w_ref[0:1, :] * _shift_up(dyc, dycn, 2, last))
        d_ref[:, 0:c] = (dcu * gc).astype(BF16)
        d_ref[:, c:2 * c] = (dy * conv).astype(BF16)
        d_ref[:, 2 * c:3 * c] = (dcu * u).astype(BF16)
        row = lax.broadcasted_iota(jnp.int32, (8, c), 0)
        dw = (jnp.where(row == 0, jnp.sum(dyc * cu2, axis=0, keepdims=True), 0.0)
              + jnp.where(row == 1, jnp.sum(dyc * cu1, axis=0, keepdims=True), 0.0)
              + jnp.where(row == 2, jnp.sum(dyc * cu, axis=0, keepdims=True), 0.0))

        @pl.when(first)
        def _():
            dw_ref[...] = jnp.zeros_like(dw_ref)
        dw_ref[...] += dw

    return _rows(body, [(dy_b, "row", c, 0), (proj, "row", c, 3), (proj, "row", c, 4), (proj, "row", c, 5),
                        (proj, "prev", c, 3), (proj, "prev", c, 5), (dy_b, "next", c, 0), (proj, "next", c, 4),
                        (conv_w, "full", 0, 0)],
                 [((s, 3 * c), BF16, "row"), ((8, c), F32, "acc")], n_rows=s, tm=tm, name=name, plan=plan)


def _mem_probs(q, k, scale):
    sc = _bdot(q, k, NT) * scale
    sc = sc - jnp.max(sc, axis=-1, keepdims=True)
    p = jnp.exp(sc)
    return p / jnp.sum(p, axis=-1, keepdims=True)


def _mem_sublayer(hq, w_q, kv, w_o, x, g, *, name, tm=512, plan=None):
    s, d = hq.shape
    hd = d // MEM_HEADS
    scale = 1.0 / math.sqrt(hd)
    tm = min(tm, s)

    def body(hq_ref, wq_ref, kv_ref, wo_ref, x_ref, g_ref, q_ref, o_ref, x2_ref, h_ref):
        q_ref[...] = _bdot(hq_ref[...], wq_ref[...], NN).astype(BF16)
        for h in range(MEM_HEADS):
            cols = slice(h * hd, (h + 1) * hd)
            p = _mem_probs(q_ref[:, cols], kv_ref[:, cols], scale)
            o_ref[:, cols] = _bdot(p, kv_ref[:, d + h * hd:d + (h + 1) * hd], NN).astype(BF16)
        xv = _bdot(o_ref[...], wo_ref[...], NN) + x_ref[...]
        x2_ref[...] = xv
        _rms_fwd_tail(xv, g_ref, h_ref)

    row = pl.BlockSpec((tm, d), lambda i: (i, 0))
    whole = lambda a: pl.BlockSpec(a.shape, lambda i: (0,) * a.ndim)
    return _call(body, name=name, grid=(s // tm,),
                 in_specs=[row, whole(w_q), whole(kv), whole(w_o), row, whole(g)], out_specs=[row] * 4,
                 out_shape=[jax.ShapeDtypeStruct((s, d), BF16), jax.ShapeDtypeStruct((s, d), BF16),
                            jax.ShapeDtypeStruct((s, d), F32), jax.ShapeDtypeStruct((s, d), BF16)],
                 scratch=[], args=[hq, w_q, kv, w_o, x, g], plan=plan)


def _mem_sublayer_bwd(dx2b, dx2, x, g, qm, kv, w_q, w_o, *, name, tm=512, plan=None):
    s, d = qm.shape
    hd = d // MEM_HEADS
    scale = 1.0 / math.sqrt(hd)
    tm = min(tm, s)

    def body(dyb_ref, dres_ref, x_ref, g_ref, q_ref, kv_ref, wq_ref, wo_ref, dx_ref, dxb_ref, dg_ref, dq_ref, dkv_ref):
        i = pl.program_id(0)

        @pl.when(i == 0)
        def _():
            dkv_ref[...] = jnp.zeros_like(dkv_ref)
        dom = _bdot(dyb_ref[...], wo_ref[...], NT).astype(BF16)
        for h in range(MEM_HEADS):
            cols = slice(h * hd, (h + 1) * hd)
            vcols = slice(d + h * hd, d + (h + 1) * hd)
            q, k, v, do = q_ref[:, cols], kv_ref[:, cols], kv_ref[:, vcols], dom[:, cols]
            p = _mem_probs(q, k, scale)
            dp = _bdot(do, v, NT)
            ds = p * (dp - jnp.sum(dp * p, axis=-1, keepdims=True)) * scale
            dq_ref[:, cols] = _bdot(ds, k, NN).astype(BF16)
            dkv_ref[:, cols] += _bdot(ds, q, TN)
            dkv_ref[:, vcols] += _bdot(p, do, TN)
        dh = _bdot(dq_ref[...], wq_ref[...], NT)
        _rms_bwd_tail(i, dh, x_ref, g_ref, dres_ref, dx_ref, dxb_ref, dg_ref)

    row = pl.BlockSpec((tm, d), lambda i: (i, 0))
    whole = lambda a: pl.BlockSpec(a.shape, lambda i: (0,) * a.ndim)
    return _call(body, name=name, grid=(s // tm,),
                 in_specs=[row, row, row, whole(g), row, whole(kv), whole(w_q), whole(w_o)],
                 out_specs=[row, row, pl.BlockSpec((8, d), lambda i: (0, 0)), row, whole(kv)],
                 out_shape=[jax.ShapeDtypeStruct((s, d), F32), jax.ShapeDtypeStruct((s, d), BF16),
                            jax.ShapeDtypeStruct((8, d), F32), jax.ShapeDtypeStruct((s, d), BF16),
                            jax.ShapeDtypeStruct(kv.shape, F32)],
                 scratch=[], args=[dx2b, dx2, x, g, qm, kv, w_q, w_o], plan=plan)


def _sb_consts(t):
    row = lax.broadcasted_iota(jnp.int32, (t, t), 0)
    col = lax.broadcasted_iota(jnp.int32, (t, t), 1)
    lane = lax.broadcasted_iota(jnp.int32, (t, LANES), 1)
    return row, col, lane < SB_HEAD_DIM


def _sb_logits(q, k):
    z2 = jnp.minimum(_bdot(q, k, NT) * LOG2_E, SB_CLAMP)
    return z2, jnp.exp2(z2)


def _tri_sum(v, tri):
    hi = v.astype(BF16)
    lo = (v - hi.astype(F32)).astype(BF16)
    return _bdot(hi, tri, NN) + _bdot(lo, tri, NN)


def _sb_fwd(proj, *, name, plan=None):
    s = proj.shape[0]
    t, nh = SB_TILE, SB_STEP_HEADS
    n_q = s // t
    scale = 1.0 / math.sqrt(SB_HEAD_DIM)

    def body(q_ref, k_ref, v_ref, o_ref, c_ref, first_ref, acc_ref, c_scr):
        i = pl.program_id(1)
        row, col, head0 = _sb_consts(t)
        later = (row > col).astype(BF16)
        valid = col < row
        lanes = lambda h: slice((h // 2) * LANES, (h // 2 + 1) * LANES)
        q = [jnp.where(head0 == (h % 2 == 0), q_ref[:, lanes(h)] * scale, 0) for h in range(nh)]

        def tiles(kbs, diag_first, carry):
            rows = [pl.ds(pl.multiple_of(kb * t, t), t) for kb in kbs]
            jobs = [(n, h) for n in range(len(kbs)) for h in range(nh)]
            masked = lambda n: diag_first and n == 0
            zs = {(n, h): _sb_logits(q[h], k_ref[rows[n], lanes(h)]) for n, h in jobs}
            fail = {j: jnp.log2(1.0 + zs[j][1]) for j in jobs}
            fail = {j: jnp.where(valid, fail[j], 0.0) if masked(j[0]) else fail[j] for j in jobs}
            cum = {j: _tri_sum(fail[j], later) for j in jobs}
            run, before = list(carry), {}
            for n, h in jobs:
                before[n, h] = run[h]
                run[h] = run[h] + cum[n, h][:, 0:1] + fail[n, h][:, 0:1]
            w = {j: jnp.exp2(zs[j][0] - fail[j] - cum[j] - before[j]) for j in jobs}
            w = {j: jnp.where(valid, w[j], 0.0) if masked(j[0]) else w[j] for j in jobs}
            for n, h in jobs:
                acc_ref[h] += _bdot(w[n, h], v_ref[rows[n], lanes(h)], NN)
            return tuple(run)

        acc_ref[...] = jnp.zeros_like(acc_ref)
        zero = (jnp.zeros((t, 1), F32),) * nh

        def alive(carry):
            return (functools.reduce(jnp.minimum, [jnp.min(c) for c in carry]) < SB_DEAD).astype(jnp.int32)

        def step(state):
            new = tiles([state[0]], False, state[2:])
            return (state[0] - 1, alive(new)) + new

        @pl.when(i == 0)
        def _():
            for h, c in enumerate(tiles([i], True, zero)):
                c_scr[h] = c

        @pl.when(i > 0)
        def _():
            for h, c in enumerate(tiles([i, i - 1], True, zero)):
                c_scr[h] = c
        carry = tuple(c_scr[h] for h in range(nh))
        state = lax.while_loop(lambda st: jnp.logical_and(st[0] >= 0, st[1] > 0), step, (i - 2, alive(carry)) + carry)
        for b in range(nh // 2):
            o_ref[:, b * LANES:(b + 1) * LANES] = jnp.where(head0, acc_ref[2 * b], acc_ref[2 * b + 1]).astype(BF16)
        head = lax.broadcasted_iota(jnp.int32, (t, nh), 1)
        c_ref[...] = sum(jnp.where(head == h, state[2 + h], 0.0) for h in range(nh))
        first_ref[pl.program_id(0), i] = (jnp.maximum(state[0], -1) + 1).astype(F32)

    n_p, width = SB_HEADS // nh, nh * SB_HEAD_DIM
    k_blk, v_blk = SB_WIDTH // width, 2 * SB_WIDTH // width
    return _call(
        body, name=name, grid=(n_p, n_q),
        in_specs=[pl.BlockSpec((t, width), lambda p, i: (i, p)),
                  pl.BlockSpec((s, width), lambda p, i: (0, k_blk + p)),
                  pl.BlockSpec((s, width), lambda p, i: (0, v_blk + p))],
        out_specs=[pl.BlockSpec((t, width), lambda p, i: (i, p)),
                   pl.BlockSpec((None, t, nh), lambda p, i: (p, i, 0)),
                   pl.BlockSpec(memory_space=pltpu.SMEM)],
        out_shape=[jax.ShapeDtypeStruct((s, SB_WIDTH), BF16), jax.ShapeDtypeStruct((n_p, s, nh), F32),
                   jax.ShapeDtypeStruct((n_p, n_q), F32)],
        scratch=[pltpu.VMEM((nh, t, LANES), F32), pltpu.VMEM((nh, t, 1), F32)], args=[proj, proj, proj], plan=plan)


def _sb_bwd(proj, do_a, ctot, first, *, name, plan=None):
    s = proj.shape[0]
    t, nh = SB_TILE, SB_STEP_HEADS
    n_q = s // t
    scale = 1.0 / math.sqrt(SB_HEAD_DIM)

    def body(q_ref, k_ref, v_ref, do_ref, c_ref, first_ref, dq_ref, dk_ref, dv_ref, dq_acc, dk_acc, dv_acc):
        i = pl.program_id(1)
        kb0 = jnp.clip(first_ref[pl.program_id(0), i].astype(jnp.int32), 0, i)
        row, col, head0 = _sb_consts(t)
        upto = (row <= col).astype(BF16)
        before = (row < col).astype(BF16)
        valid = col < row
        lanes = lambda h: slice((h // 2) * LANES, (h // 2 + 1) * LANES)
        q2 = [jnp.where(head0 == (h % 2 == 0), q_ref[:, lanes(h)] * scale, 0) for h in range(nh)]
        do2 = [jnp.where(head0 == (h % 2 == 0), do_ref[:, lanes(h)], 0) for h in range(nh)]
        ctot2 = [c_ref[:, h:h + 1] for h in range(nh)]

        @pl.when(i == 0)
        def _():
            dk_acc[...] = jnp.zeros_like(dk_acc)
            dv_acc[...] = jnp.zeros_like(dv_acc)
        dq_acc[...] = jnp.zeros_like(dq_acc)

        def tiles(kbs, diag_last, carry):
            rows = [pl.ds(pl.multiple_of(kb * t, t), t) for kb in kbs]
            kt = {(n, h): k_ref[rows[n], lanes(h)] for n in range(len(kbs)) for h in range(nh)}
            jobs = list(kt)
            masked = lambda n: diag_last and n == len(kbs) - 1
            t_last = slice(t - 1, t)
            zs = {(n, h): _sb_logits(q2[h], kt[n, h]) for n, h in jobs}
            dw = {(n, h): _bdot(do2[h], v_ref[rows[n], lanes(h)], NT) for n, h in jobs}
            fail = {j: jnp.log2(1.0 + zs[j][1]) for j in jobs}
            fail = {j: jnp.where(valid, fail[j], 0.0) if masked(j[0]) else fail[j] for j in jobs}
            cum = {j: _tri_sum(fail[j], upto) for j in jobs}
            miss = {j: jnp.exp2(-fail[j]) for j in jobs}
            beta = {j: zs[j][1] * miss[j] for j in jobs}
            fail_run, fail_before = list(carry[0::2]), {}
            for n, h in jobs:
                fail_before[n, h] = fail_run[h]
                fail_run[h] = fail_run[h] + cum[n, h][:, t_last]
            w = {(n, h): beta[n, h] * jnp.exp2(fail_before[n, h] + cum[n, h] - ctot2[h]) for n, h in jobs}
            w = {j: jnp.where(valid, w[j], 0.0) if masked(j[0]) else w[j] for j in jobs}
            g = {j: w[j] * dw[j] for j in jobs}
            g_local = {j: _bdot(g[j], before, NN) for j in jobs}
            for n, h in jobs:
                dv_acc[rows[n], lanes(h)] += _bdot(w[n, h], do2[h], TN)
            g_run, dz = list(carry[1::2]), {}
            for n, h in jobs:
                g_sum = g_run[h] + g_local[n, h]
                dz[n, h] = g[n, h] * miss[n, h] - beta[n, h] * g_sum
                g_run[h] = g_sum[:, t_last] + g[n, h][:, t_last]
            dz = {j: jnp.where(valid, dz[j], 0.0) if masked(j[0]) else dz[j] for j in jobs}
            for n, h in jobs:
                dq_acc[h] += _bdot(dz[n, h], kt[n, h], NN)
                dk_acc[rows[n], lanes(h)] += _bdot(dz[n, h], q2[h], TN)
            return tuple(v for pair in zip(fail_run, g_run) for v in pair)

        zero = jnp.zeros((t, 1), F32)
        carry = lax.fori_loop(kb0, i - 1, lambda n, c: tiles([n], False, c), (zero,) * (2 * nh))

        @pl.when(i == 0)
        def _():
            tiles([i], True, carry)

        @pl.when(i > 0)
        def _():
            tiles([i - 1, i], True, carry)
        for b in range(nh // 2):
            dq_ref[:, b * LANES:(b + 1) * LANES] = (jnp.where(head0, dq_acc[2 * b], dq_acc[2 * b + 1])
                                                    * scale).astype(BF16)

        @pl.when(i == n_q - 1)
        def _():
            dk_ref[...] = dk_acc[...].astype(BF16)
            dv_ref[...] = dv_acc[...].astype(BF16)

    n_p, width = SB_HEADS // nh, nh * SB_HEAD_DIM
    k_blk, v_blk = SB_WIDTH // width, 2 * SB_WIDTH // width
    outs = _call(
        body, name=name, grid=(n_p, n_q),
        in_specs=[pl.BlockSpec((t, width), lambda p, i: (i, p)),
                  pl.BlockSpec((s, width), lambda p, i: (0, k_blk + p)),
                  pl.BlockSpec((s, width), lambda p, i: (0, v_blk + p)),
                  pl.BlockSpec((t, width), lambda p, i: (i, p)),
                  pl.BlockSpec((None, t, nh), lambda p, i: (p, i, 0)),
                  pl.BlockSpec(memory_space=pltpu.SMEM)],
        out_specs=[pl.BlockSpec((t, width), lambda p, i: (i, p)),
                   pl.BlockSpec((s, width), lambda p, i: (0, p)),
                   pl.BlockSpec((s, width), lambda p, i: (0, p))],
        out_shape=[jax.ShapeDtypeStruct((s, SB_WIDTH), BF16)] * 3,
        scratch=[pltpu.VMEM((nh, t, LANES), F32), pltpu.VMEM((s, width), F32), pltpu.VMEM((s, width), F32)],
        args=[proj, proj, proj, do_a, ctot, first], plan=plan)
    return jnp.concatenate(outs, axis=1)


def _mm_gathered(a, key, plan, *, name, out3=False, w_t=False):
    src = plan.gathering(key)
    if src is None:
        return _mm_nn(a, plan.weight(key), name=name, out3=out3, w_t=w_t, plan=plan)
    out, w_all = _mm_gathering(a, src, name=name, out3=out3, w_t=w_t)
    plan.set_weight(key, w_all)
    return out


def _local_step(x, mem, target, gains, plan):
    g_mix, g_memq, g_memkv, g_ffn, g_fin = gains
    d = x.shape[1]

    h0 = _rms_fwd(x, g_mix, name="rms_mix")
    proj = _mm_gathered(h0, "in", plan, name="mm_in")
    w_in = plan.weight("in")
    o_a, ctot, first = _sb_fwd(proj, name="sb_fwd", plan=plan)
    conv_w = plan.weight("conv")
    y_b = _conv_fwd(proj, conv_w, name="conv_fwd")
    w_a, w_b, w_mix = plan.weight("a"), plan.weight("b"), plan.weight("mix")
    x1, hq, merged, br_a, br_b = _mix_out(o_a, y_b, proj, w_a[0], w_b[0], w_mix[0], x, g_memq, name="mm_mix", plan=plan)
    w_mq, w_kv, w_mo = plan.weight("mq")[0], plan.weight("kv"), plan.weight("mo")[0]
    mn = _rms_fwd(mem, g_memkv, name="rms_memkv")
    kv = _mm_nn(mn, w_kv, name="mm_memkv")
    qm, om, x2, hf = _mem_sublayer(hq, w_mq, kv, w_mo, x1, g_ffn, name="mem_sublayer", plan=plan)
    gu = _mm_gathered(hf, "fi", plan, name="mm_ffn_in", out3=True, w_t=True)
    w_fi, w_fo = plan.weight("fi"), plan.weight("fo")
    dx3, dx3b, dg_fin, loss, act = _ffn_out_loss(gu, w_fo, x2, g_fin, target, name="mm_ffn_out")

    plan.grad("fo", _mm_tn_a3(act, dx3b, name="mm_d_w_ffn_out"))
    dgu = _ffn_out_bwd(dx3b, w_fo, gu, name="mm_d_act")
    plan.grad("fi", _mm_tn_a3(dgu, hf, name="mm_d_w_ffn_in"))
    dx2, dx2b, dg_ffn = _mm_nt_rms(dgu, w_fi, x2, g_ffn, dx3, name="mm_d_hf", dy3=True, w_nn=True, plan=plan)

    plan.grad("mo", _mm_tn(om, dx2b, d, name="mm_d_w_memo"))
    dx1, dx1b, dg_memq, dqm, dkv = _mem_sublayer_bwd(dx2b, dx2, x1, g_memq, qm, kv, w_mq, w_mo, name="mem_sublayer_bwd",
                                                    plan=plan)
    plan.grad("mq", _mm_tn(hq, dqm, d, name="mm_d_w_memq"))
    plan.grad("kv", _mm_tn(mn, dkv, w_kv.shape[2], name="mm_d_w_memkv"))
    _, _, dg_memkv = _mm_nt_rms(dkv, w_kv, mem, g_memkv, None, name="mm_d_mn")

    plan.grad("mix", _mm_tn(merged, dx1b, d, name="mm_d_w_mix"))
    dbr_a, dbr_b, dgab, do_a, dy_b = _mix_out_bwd(dx1b, w_mix[0], br_a, br_b, proj, w_a[0], w_b[0], name="mm_d_merged",
                                                 plan=plan)
    plan.grad("a", _mm_tn(o_a, dbr_a, d, name="mm_d_w_branch_a"))
    plan.grad("b", _mm_tn(y_b, dbr_b, d, name="mm_d_w_branch_b"))
    dconv, dconv_w = _conv_bwd(dy_b, proj, conv_w, name="conv_bwd", plan=plan)
    dqkv = _sb_bwd(proj, do_a, ctot, first, name="sb_bwd", plan=plan)
    dproj = jnp.concatenate([dqkv, dconv, dgab], axis=1)
    rows_in1 = d // IN_SPLIT[1] * (IN_SPLIT[1] - IN_SPLIT[0])
    plan.grad("in0", _mm_tn(h0, dproj, w_in.shape[2], name="mm_d_w_in0", tm=d - rows_in1, k_tiles=(0, 1)))
    plan.grad("in1", _mm_tn(h0, dproj, w_in.shape[2], name="mm_d_w_in1", tm=rows_in1,
                            k_tiles=(d // rows_in1 - 1, 1), plan=plan))
    dh0 = _mm_nt(dproj, w_in, name="mm_d_h0", out_dtype=F32, plan=plan)
    dx0, _, dg_mix = _rms_bwd(x, g_mix, dh0, dx1, name="rms_mix_bwd", plan=plan)

    return dx0, (dg_mix, dg_memq, dg_memkv, dg_ffn, dg_fin, dconv_w, loss)


def _row_tile(a, target=512):
    tm = min(a, target)
    while a % tm:
        tm -= 8
    return tm


def _sum_with_sibling(parts, recvs, core, *, name):
    n = len(parts)

    def body(core_ref, *refs):
        for p_ref, r_ref, o_ref in zip(refs[:n], refs[n:2 * n], refs[2 * n:]):
            o_ref[...] = (p_ref[...].astype(F32) + r_ref[...].astype(F32)).astype(o_ref.dtype)

    mine = [pl.BlockSpec((None,) + p.shape[1:], lambda q, core_ref: (2 * q + core_ref[0], 0, 0)) for p in parts]
    other = [pl.BlockSpec((None,) + p.shape[1:], lambda q, core_ref: (q, 0, 0)) for p in parts]
    return pl.pallas_call(
        body, name=name,
        grid_spec=pltpu.PrefetchScalarGridSpec(num_scalar_prefetch=1, grid=(N_CHIP,), in_specs=mine + other,
                                               out_specs=other),
        out_shape=[jax.ShapeDtypeStruct((N_CHIP,) + p.shape[1:], p.dtype) for p in parts],
        compiler_params=_params(1))(core, *parts, *recvs)


def _adam_math(wv, g, m, v):
    m = ADAM_B1 * m + (1.0 - ADAM_B1) * g
    v = ADAM_B2 * v + (1.0 - ADAM_B2) * (g * g)
    m_hat = m / (1.0 - ADAM_B1 ** ADAM_STEP)
    v_hat = v / (1.0 - ADAM_B2 ** ADAM_STEP)
    delta = -ADAM_LR * (m_hat / (jnp.sqrt(v_hat) + ADAM_EPS) + ADAM_WD * wv)
    return delta, m, v


def _adam_sharded(ws, ms, vs, owns, recvs, chip, *, name):
    n = len(ws)
    a, b = ws[0].shape
    tm = _row_tile(a)

    def body(chip_ref, *refs):
        ins, outs = refs[:5 * n], refs[5 * n:]
        for k in range(n):
            w_ref, m_ref, v_ref, own_ref, recv_ref = ins[k::n]
            g = own_ref[...].astype(F32)
            for j in range(3):
                g = g + recv_ref[j].astype(F32)
            delta, nm, nv = _adam_math(w_ref[...], g, m_ref[...], v_ref[...])
            for o_ref, value in zip(outs[k::n], (g, delta, nm, nv)):
                o_ref[...] = value

    tile = pl.BlockSpec((tm, b), lambda i, chip_ref: (i, 0))
    res = pl.pallas_call(
        body, name=name,
        grid_spec=pltpu.PrefetchScalarGridSpec(
            num_scalar_prefetch=1, grid=(a // tm,),
            in_specs=[tile] * (3 * n) + [pl.BlockSpec((None, tm, b), lambda i, chip_ref: (chip_ref[0], i, 0))] * n
            + [pl.BlockSpec((3, tm, b), lambda i, chip_ref: (0, i, 0))] * n,
            out_specs=[tile] * (4 * n)),
        out_shape=[jax.ShapeDtypeStruct((a, b), F32)] * (4 * n),
        compiler_params=_params(1))(chip, *ws, *ms, *vs, *owns, *recvs)
    return [res[k::n] for k in range(n)]


def _sum_devices(gathered, *, name):
    _, r, c = gathered.shape

    def body(g_ref, o_ref):
        total = g_ref[0]
        for j in range(1, N_DEV):
            total = total + g_ref[j]
        o_ref[...] = total

    return pl.pallas_call(body, name=name, out_shape=jax.ShapeDtypeStruct((r, c), F32))(gathered)


def _adam_small(wv, g, m, v, *, name):
    def body(w_ref, g_ref, m_ref, v_ref, d_ref, nm_ref, nv_ref):
        delta, nm, nv = _adam_math(w_ref[...], g_ref[...], m_ref[...], v_ref[...])
        d_ref[...] = delta
        nm_ref[...] = nm
        nv_ref[...] = nv

    return pl.pallas_call(body, name=name, out_shape=[jax.ShapeDtypeStruct(wv.shape, F32)] * 3)(wv, g, m, v)


BIG = ("in", "a", "b", "mix", "mq", "kv", "mo", "fi", "fo")
ROW_SHARDED = ("mix", "mq", "mo")
UNSHARDED = ("a", "b")
FFN_GROUPS = 4
IN_SPLIT = (3, 4)
SMALL_ROWS = 16


class _Plan:
    FUSED = ("in",)
    GATHER_ON = {"sb_fwd": ("a", "b", "mix", "mq", "mo", "conv", "fi0"), "mm_mix": ("kv",), "mem_sublayer": ("fi1",),
                 "mm_ffn_in": ("fo",)}
    SIBLING_ON = {"mm_d_hf": ("fo", "fi"), "mm_d_merged": ("mo", "mq", "kv"), "conv_bwd": ("mix", "a", "b"),
                  "mm_d_w_in1": ("in0",), "mm_d_h0": ("in1",)}
    LATE_AT = {"mm_mix": (7, 8), "mm_ffn_in": (7, 8)}
    CHIPS_ON = {"mem_sublayer_bwd": ("fo",), "sb_bwd": ("fi", "mo", "mq", "kv", "mix", "a", "b"), "mm_d_h0": ("in0",),
                "rms_mix_bwd": ("in1",)}

    def __init__(self, shards, core):
        self.shards, self.core = shards, core
        self.w, self.parts, self.chip_sums, self.from_chips = {}, {}, {}, {}

    def gathering(self, k):
        return self.shards[k] if k in self.FUSED else None

    def comm(self, name):
        comms = []
        if name in self.GATHER_ON:
            comms.append(_gather_comm([self.shards[k] for k in self.GATHER_ON[name]]))
        if name in self.SIBLING_ON:
            comms.append(_sibling_comm([self.parts[k] for k in self.SIBLING_ON[name]]))
        if name in self.CHIPS_ON:
            comms.append(_chips_comm([self.chip_sums[k] for k in self.CHIPS_ON[name]]))
        if not comms:
            return None
        comm = _join_comms(comms)
        comm.late_at = self.LATE_AT.get(name, comm.late_at)
        return comm

    def landed(self, name, outs):
        outs = list(outs)
        for k in self.GATHER_ON.get(name, ()):
            self.set_weight(k, outs.pop(0))
        keys = self.SIBLING_ON.get(name, ())
        if keys:
            sums = _sum_with_sibling([self.parts[k] for k in keys], [outs.pop(0) for _ in keys], self.core,
                                     name="sum_with_sibling_" + "_".join(keys))
            self.chip_sums.update(zip(keys, sums))
        for k in self.CHIPS_ON.get(name, ()):
            self.from_chips[k] = outs.pop(0)

    def set_weight(self, k, gathered):
        _, a, b = gathered.shape
        if k in ROW_SHARDED:
            gathered = gathered.reshape(1, N_DEV * a, b)
        elif k in UNSHARDED:
            gathered = jnp.transpose(gathered, (1, 0, 2)).reshape(1, a, N_DEV * b)
        elif k == "fo":
            gathered = gathered.reshape(FFN_GROUPS, N_DEV * a // FFN_GROUPS, b)
        elif k == "conv":
            n_conv = CONV_WIDTH // N_DEV
            gathered = jnp.transpose(gathered[:, :3, :n_conv], (1, 0, 2)).reshape(3, CONV_WIDTH)
        self.w[k] = gathered
        if k == "fi1":
            self.w["fi"] = jnp.concatenate([self.w["fi0"], gathered], axis=2)

    def weight(self, k):
        return self.w[k]

    def grad(self, k, g):
        _, a, b = g.shape
        if k in ROW_SHARDED:
            g = g.reshape(N_DEV, a // N_DEV, b)
        elif k in UNSHARDED:
            g = jnp.transpose(g.reshape(a, N_DEV, b // N_DEV), (1, 0, 2))
        elif k == "fo":
            g = g.reshape(N_DEV, FFN_GROUPS * a // N_DEV, b)
        self.parts[k] = g


def kernel(x, mem, norm_mix, w_in, conv_w, w_branch_a, w_branch_b, w_mix_out, norm_mem_q, norm_mem_kv, w_mem_q, w_mem_kv, w_mem_o, norm_ffn, w_ffn_in, w_ffn_out, norm_final, loss_target, m_norm_mix, m_w_in, m_conv_w, m_w_branch_a, m_w_branch_b, m_w_mix_out, m_norm_mem_q, m_norm_mem_kv, m_w_mem_q, m_w_mem_kv, m_w_mem_o, m_norm_ffn, m_w_ffn_in, m_w_ffn_out, m_norm_final, v_norm_mix, v_w_in, v_conv_w, v_w_branch_a, v_w_branch_b, v_w_mix_out, v_norm_mem_q, v_norm_mem_kv, v_w_mem_q, v_w_mem_kv, v_w_mem_o, v_norm_ffn, v_w_ffn_in, v_w_ffn_out, v_norm_final):
    d = x.shape[-1]
    xi, yi, ci = lax.axis_index("x"), lax.axis_index("y"), lax.axis_index("c")
    chip = jnp.reshape(2 * xi + yi, (1,)).astype(jnp.int32)
    dev = 4 * xi + 2 * yi + ci

    big_w = dict(zip(BIG, (w_in, w_branch_a, w_branch_b, w_mix_out, w_mem_q, w_mem_kv, w_mem_o, w_ffn_in, w_ffn_out)))
    big_m = dict(zip(BIG, (m_w_in, m_w_branch_a, m_w_branch_b, m_w_mix_out, m_w_mem_q, m_w_mem_kv, m_w_mem_o, m_w_ffn_in, m_w_ffn_out)))
    big_v = dict(zip(BIG, (v_w_in, v_w_branch_a, v_w_branch_b, v_w_mix_out, v_w_mem_q, v_w_mem_kv, v_w_mem_o, v_w_ffn_in, v_w_ffn_out)))

    flip = lambda t, k: jnp.transpose(t) if k == "fi" else t
    shards = {k: flip(big_w[k][0], k).astype(BF16) for k in BIG}
    shards["fi0"], shards["fi1"] = shards["fi"][:, :d // 2], shards["fi"][:, d // 2:]
    n_conv = conv_w.shape[-1]
    shards["conv"] = jnp.zeros((8, LANES), F32).at[:3, :n_conv].set(conv_w[0])
    plan = _Plan(shards, jnp.reshape(ci, (1,)).astype(jnp.int32))

    gains = (norm_mix, norm_mem_q, norm_mem_kv, norm_ffn, norm_final.reshape(1, d))
    dx0, small = _local_step(x[0], mem[0], loss_target[0], gains, plan)

    grads, deltas, new_m, new_v = {}, {}, {}, {}
    def adam(keys, rows=None, part=None):
        cut = (lambda t: t) if rows is None else (lambda t: t[rows])
        return _adam_sharded(*[[cut(flip(src[k][0], k)) for k in keys] for src in (big_w, big_m, big_v)],
                             [plan.chip_sums[part or k] for k in keys], [plan.from_chips[part or k] for k in keys],
                             chip, name="adam_" + "_".join(keys) + (part or "")[2:])

    results = {}
    for keys in (("a", "b"), ("mix", "mq", "mo"), ("kv",), ("fi",), ("fo",)):
        results.update(zip(keys, adam(keys)))
    half = big_w["in"].shape[1] * IN_SPLIT[0] // IN_SPLIT[1]
    lo, hi = adam(("in",), slice(0, half), "in0")[0], adam(("in",), slice(half, None), "in1")[0]
    results["in"] = [jnp.concatenate(pair, axis=0) for pair in zip(lo, hi)]
    for k in BIG:
        grads[k], deltas[k], new_m[k], new_v[k] = (flip(t, k).reshape(big_w[k].shape) for t in results[k])

    dg_mix, dg_memq, dg_memkv, dg_ffn, dg_fin, dconv_w, loss = small
    conv_rows = jnp.zeros((3, d), F32).at[:, :CONV_WIDTH].set(dconv_w[:3])
    block = jnp.concatenate([dg_mix[:1], dg_memq[:1], dg_memkv[:1], dg_ffn[:1], dg_fin[:1], conv_rows,
                             jnp.broadcast_to(loss[:1, :1], (1, d)), jnp.zeros((SMALL_ROWS - 9, d), F32)], axis=0)
    total = _sum_devices(_exchange(_gather_comm([block]), name="gather_small")[0], name="sum_small")
    g_conv = lax.dynamic_slice(total[5:8, :CONV_WIDTH], (0, dev * n_conv), (3, n_conv))
    small_w = [norm_mix, norm_mem_q, norm_mem_kv, norm_ffn, norm_final.reshape(1, d), conv_w[0]]
    small_m = [m_norm_mix, m_norm_mem_q, m_norm_mem_kv, m_norm_ffn, m_norm_final.reshape(1, d), m_conv_w[0]]
    small_v = [v_norm_mix, v_norm_mem_q, v_norm_mem_kv, v_norm_ffn, v_norm_final.reshape(1, d), v_conv_w[0]]
    small_g = [total[0:1], total[1:2], total[2:3], total[3:4], total[4:5], g_conv]
    small_names = ["norm_mix", "norm_mem_q", "norm_mem_kv", "norm_ffn", "norm_final", "conv_w"]
    sg, sd, sm, sv = {}, {}, {}, {}
    for nme, wv, g, m, v in zip(small_names, small_w, small_g, small_m, small_v):
        dl, nm, nv = _adam_small(wv, g, m, v, name="adam_" + nme)
        shape = norm_final.shape if nme == "norm_final" else (conv_w.shape if nme == "conv_w" else wv.shape)
        sg[nme], sd[nme], sm[nme], sv[nme] = (t.reshape(shape) for t in (g, dl, nm, nv))

    def ordered(big, sml):
        return (sml["norm_mix"], big["in"], sml["conv_w"], big["a"], big["b"], big["mix"], sml["norm_mem_q"],
                sml["norm_mem_kv"], big["mq"], big["kv"], big["mo"], sml["norm_ffn"], big["fi"], big["fo"],
                sml["norm_final"])

    loss_out = total[8, 0]
    grad_x = dx0.reshape(x.shape)
    return (loss_out, grad_x, *ordered(grads, sg), *ordered(deltas, sd), *ordered(new_m, sm), *ordered(new_v, sv))
```

```python
import functools
import math

import jax
import jax.numpy as jnp
from jax import lax
from jax.experimental import pallas as pl
from jax.experimental.pallas import tpu as pltpu

F32 = jnp.float32
BF16 = jnp.bfloat16
MESH = pl.DeviceIdType.MESH

N_DEV = 8
N_CHIP = 4
NORM_EPS = 1e-6
SB_HEADS = 8
SB_HEAD_DIM = 64
SB_WIDTH = SB_HEADS * SB_HEAD_DIM
CONV_WIDTH = 512
MEM_HEADS = 4
ADAM_LR = 0.001
ADAM_B1 = 0.9
ADAM_B2 = 0.999
ADAM_EPS = 1e-08
ADAM_WD = 0.01
ADAM_STEP = 10

LANES = 128
VMEM_LIMIT_BYTES = 52 * 1024 * 1024
SB_TILE = 256
SB_STEP_HEADS = 4
SB_DEAD = 159.0
SB_CLAMP = 126.0
LOG2_E = 1.4426950408889634

ANY = pl.BlockSpec(memory_space=pl.ANY)


def _params(n_grid):
    return pltpu.CompilerParams(dimension_semantics=("arbitrary",) * n_grid, vmem_limit_bytes=VMEM_LIMIT_BYTES)


def _bdot(a, b, dims):
    return lax.dot_general(a.astype(BF16), b.astype(BF16), (dims, ((), ())), preferred_element_type=F32)


NN = ((1,), (0,))
NT = ((1,), (1,))
TN = ((0,), (0,))


class _Comm:
    def __init__(self, ins, outs, n_sems, start, finish, late=None):
        self.ins, self.outs, self.n_sems, self.start, self.finish = ins, outs, n_sems, start, finish
        self.late = late if late is not None else (lambda ins, outs, sems: None)
        self.late_at = (1, 1)

    def sem_shapes(self):
        return [pltpu.SemaphoreType.DMA((k,)) for k in self.n_sems]


def _place():
    return lax.axis_index("x"), lax.axis_index("y"), lax.axis_index("c")


def _neighbours(x, y, c):
    return [(jnp.bitwise_xor(x, c), jnp.bitwise_xor(y, 1 - c)), (jnp.bitwise_xor(x, 1 - c), jnp.bitwise_xor(y, c)),
            (1 - x, 1 - y)]


def _gather_comm(shards):
    n = len(shards)

    def copies(ins, outs, sems):
        send_sems, recv_sems, _ = sems
        x, y, c = _place()
        chips = [(1 - x, y), (x, 1 - y), (1 - x, 1 - y)]

        def copy(a, k, block, to, from_shard=False):
            dst = outs[a].at[4 * block[0] + 2 * block[1] + block[2]]
            return pltpu.make_async_remote_copy(
                src_ref=ins[a] if from_shard else dst, dst_ref=dst, send_sem=send_sems.at[a * 7 + k],
                recv_sem=recv_sems.at[a * 7 + k], device_id=to, device_id_type=MESH)

        me, sibling = (x, y, c), (x, y, 1 - c)
        own = [[copy(a, 0, me, sibling, True)] + [copy(a, 1 + j, me, (*chip, c), True) for j, chip in enumerate(chips)]
               for a in range(n)]
        landed = [[copy(a, 1 + j, (*chip, c), me) for j, chip in enumerate(chips)] for a in range(n)]
        passed = [[copy(a, 4 + j, (*chip, c), sibling) for j, chip in enumerate(chips)] for a in range(n)]
        from_sibling = [[copy(a, 0, sibling, me)] + [copy(a, 4 + j, (*chip, 1 - c), me) for j, chip in enumerate(chips)]
                        for a in range(n)]
        local = [pltpu.make_async_copy(ins[a], outs[a].at[4 * x + 2 * y + c], sems[2].at[a]) for a in range(n)]
        return own, landed, passed, from_sibling, local

    def start(ins, outs, sems):
        own, _, _, _, local = copies(ins, outs, sems)
        for a in range(n):
            local[a].start()
            for cp in own[a]:
                cp.start()

    def late(ins, outs, sems):
        _, landed, passed, _, _ = copies(ins, outs, sems)
        for a in range(n):
            for arrived, onward in zip(landed[a], passed[a]):
                arrived.wait_recv()
                onward.start()

    def finish(ins, outs, sems):
        own, _, passed, from_sibling, local = copies(ins, outs, sems)
        for a in range(n):
            for cp in from_sibling[a]:
                cp.wait_recv()
        for a in range(n):
            for cp in own[a] + passed[a]:
                cp.wait_send()
            local[a].wait()

    outs = [jax.ShapeDtypeStruct((N_DEV,) + s.shape, s.dtype) for s in shards]
    return _Comm(list(shards), outs, (7 * n, 7 * n, n), start, finish, late)


def _sibling_comm(parts):
    n = len(parts)

    def copies(ins, outs, sems):
        x, y, c = _place()
        return [pltpu.make_async_remote_copy(
            src_ref=ins[a].at[2 * q + 1 - c], dst_ref=outs[a].at[q], send_sem=sems[0].at[a * N_CHIP + q],
            recv_sem=sems[1].at[a * N_CHIP + q], device_id=(x, y, 1 - c), device_id_type=MESH)
            for a in range(n) for q in range(N_CHIP)]

    def start(ins, outs, sems):
        for cp in copies(ins, outs, sems):
            cp.start()

    def finish(ins, outs, sems):
        cps = copies(ins, outs, sems)
        for cp in cps:
            cp.wait_recv()
        for cp in cps:
            cp.wait_send()

    outs = [jax.ShapeDtypeStruct((N_CHIP,) + p.shape[1:], p.dtype) for p in parts]
    return _Comm(list(parts), outs, (N_CHIP * n, N_CHIP * n), start, finish)


def _chips_comm(parts):
    n = len(parts)

    def copies(ins, outs, sems):
        x, y, c = _place()
        chips = [(1 - x, y), (x, 1 - y), (1 - x, 1 - y)]
        return [pltpu.make_async_remote_copy(
            src_ref=ins[a].at[2 * px + py], dst_ref=outs[a].at[j], send_sem=sems[0].at[a * 3 + j],
            recv_sem=sems[1].at[a * 3 + j], device_id=(px, py, c), device_id_type=MESH)
            for a in range(n) for j, (px, py) in enumerate(chips)]

    def start(ins, outs, sems):
        for cp in copies(ins, outs, sems):
            cp.start()

    def finish(ins, outs, sems):
        cps = copies(ins, outs, sems)
        for cp in cps:
            cp.wait_recv()
        for cp in cps:
            cp.wait_send()

    outs = [jax.ShapeDtypeStruct((3,) + p.shape[1:], p.dtype) for p in parts]
    return _Comm(list(parts), outs, (3 * n, 3 * n), start, finish)


def _join_comms(comms):
    if len(comms) == 1:
        return comms[0]

    def split(refs, counts):
        out, at = [], 0
        for n in counts:
            out.append(refs[at:at + n])
            at += n
        return out

    def each(method):
        def run(ins, outs, sems):
            parts = zip(comms, split(ins, [len(c.ins) for c in comms]), split(outs, [len(c.outs) for c in comms]),
                        split(sems, [len(c.n_sems) for c in comms]))
            for c, c_ins, c_outs, c_sems in parts:
                getattr(c, method)(c_ins, c_outs, c_sems)
        return run

    return _Comm([a for c in comms for a in c.ins], [o for c in comms for o in c.outs],
                 tuple(k for c in comms for k in c.n_sems), each("start"), each("finish"), each("late"))


def _exchange(comm, *, name):
    n_ci, n_co = len(comm.ins), len(comm.outs)

    def kern(*refs):
        c_ins, c_outs, sems = refs[:n_ci], refs[n_ci:n_ci + n_co], refs[n_ci + n_co:]
        comm.start(c_ins, c_outs, sems)
        comm.late(c_ins, c_outs, sems)
        comm.finish(c_ins, c_outs, sems)

    return pl.pallas_call(kern, name=name, in_specs=[ANY] * n_ci, out_specs=[ANY] * n_co, out_shape=comm.outs,
                          scratch_shapes=comm.sem_shapes())(*comm.ins)


def _call(body, *, name, grid, in_specs, out_specs, out_shape, scratch, args, plan=None):
    comm = plan.comm(name) if plan is not None else None
    if comm is None:
        return list(pl.pallas_call(functools.partial(body), name=name, grid=grid, in_specs=in_specs,
                                   out_specs=out_specs, out_shape=out_shape, scratch_shapes=scratch,
                                   compiler_params=_params(len(grid)))(*args))
    n_in, n_out, n_scr, n_ci, n_co = len(in_specs), len(out_specs), len(scratch), len(comm.ins), len(comm.outs)

    def kern(*refs):
        ins, c_ins, refs = refs[:n_in], refs[n_in:n_in + n_ci], refs[n_in + n_ci:]
        outs, c_outs, refs = refs[:n_out], refs[n_out:n_out + n_co], refs[n_out + n_co:]
        scr, sems = refs[:n_scr], refs[n_scr:]
        ids = [pl.program_id(ax) for ax in range(len(grid))]
        step = functools.reduce(lambda at, ig: at * ig[1] + ig[0], zip(ids, grid), 0)
        n_steps = math.prod(grid)

        @pl.when(step == 0)
        def _():
            comm.start(c_ins, c_outs, sems)

        @pl.when(step == min(n_steps * comm.late_at[0] // comm.late_at[1], n_steps - 1))
        def _():
            comm.late(c_ins, c_outs, sems)
        body(*ins, *outs, *scr)

        @pl.when(step == n_steps - 1)
        def _():
            comm.finish(c_ins, c_outs, sems)

    res = pl.pallas_call(kern, name=name, grid=grid, in_specs=list(in_specs) + [ANY] * n_ci,
                         out_specs=list(out_specs) + [ANY] * n_co, out_shape=list(out_shape) + comm.outs,
                         scratch_shapes=list(scratch) + comm.sem_shapes(),
                         compiler_params=_params(len(grid)))(*args, *comm.ins)
    plan.landed(name, list(res[n_out:]))
    return list(res[:n_out])


def _mm_body(dims, has_add, *refs):
    if has_add:
        a_ref, b_ref, add_ref, o_ref = refs
        total = _bdot(a_ref[...], b_ref[...], dims) + add_ref[...]
    else:
        a_ref, b_ref, o_ref = refs
        total = _bdot(a_ref[...], b_ref[...], dims)
    o_ref[...] = total.astype(o_ref.dtype)


def _mm_nt_body(j, n, dy_ref, w_ref, o_ref):
    total = _bdot(dy_ref[:, 0:n], w_ref[0], NT)
    for jj in range(1, j):
        total = total + _bdot(dy_ref[:, jj * n:(jj + 1) * n], w_ref[jj], NT)
    o_ref[...] = total.astype(o_ref.dtype)


def _mm_nn(a, w3, *, name, out_dtype=BF16, add=None, tm=1024, tn=None, out3=False, w_t=False, plan=None):
    m, kk = a.shape
    j, n = w3.shape[0], w3.shape[1 if w_t else 2]
    tm, tn = min(tm, m), n if tn is None else tn
    n_t = n // tn
    in_specs = [pl.BlockSpec((tm, kk), lambda i, jj: (i, 0)),
                pl.BlockSpec((None, tn, kk), lambda i, jj: (jj // n_t, jj % n_t, 0)) if w_t else
                pl.BlockSpec((None, kk, tn), lambda i, jj: (jj // n_t, 0, jj % n_t))]
    args = [a, w3]
    if add is not None:
        in_specs.append(pl.BlockSpec((tm, tn), lambda i, jj: (i, jj)))
        args.append(add)
    if out3:
        out_spec = pl.BlockSpec((None, tm, tn), lambda i, jj: (jj // n_t, i, jj % n_t))
        out_shape = jax.ShapeDtypeStruct((j, m, n), out_dtype)
    else:
        out_spec = pl.BlockSpec((tm, tn), lambda i, jj: (i, jj))
        out_shape = jax.ShapeDtypeStruct((m, j * n), out_dtype)
    return _call(
        functools.partial(_mm_body, NT if w_t else NN, add is not None), name=name, grid=(m // tm, j * n_t),
        in_specs=in_specs, out_specs=[out_spec], out_shape=[out_shape], scratch=[], args=args, plan=plan)[0]


def _mm_gathering(a, shard, *, name, out3=False, w_t=False, tm=1024):
    m, kk = a.shape
    n = shard.shape[0 if w_t else 1]
    tm = min(tm, m)
    n_i = m // tm
    fetch_at = min(1, n_i - 1)

    def body(a_ref, shard_ref, o_ref, w_all, w_vmem, send_sems, recv_sems, copy_sems):
        jj, i = pl.program_id(0), pl.program_id(1)
        x, y, c = _place()
        me, sibling = (x, y, c), (x, y, 1 - c)
        chips = _neighbours(x, y, c)
        sibling_chips = [chips[1], chips[0], chips[2]]

        def rows(block):
            return w_all.at[4 * block[0] + 2 * block[1] + block[2]]

        def remote(k, block, to, from_shard=False):
            return pltpu.make_async_remote_copy(
                src_ref=shard_ref if from_shard else rows(block), dst_ref=rows(block), send_sem=send_sems.at[k],
                recv_sem=recv_sems.at[k], device_id=to, device_id_type=MESH)

        def load(step, src):
            return pltpu.make_async_copy(src, w_vmem.at[step % 2], copy_sems.at[1 + step % 2])

        own = [remote(0, me, sibling, True), remote(1, me, (*chips[0], c), True), remote(2, me, (*chips[1], c), True),
               remote(3, (*chips[0], c), (*chips[1], c))]
        passed = [remote(4 + j, (*chip, c), sibling) for j, chip in enumerate(chips)]
        local = pltpu.make_async_copy(shard_ref, rows(me), copy_sems.at[0])

        @pl.when(jnp.logical_and(i == 0, jj == 0))
        def _():
            local.start()
            own[0].start()
            own[1].start()
            load(0, shard_ref).start()

        def arrivals():
            yield 1, (lambda: remote(0, sibling, me).wait_recv()), sibling
            for j, chip in enumerate(chips):
                def landed(j=j, chip=chip):
                    if j < 2:
                        own[1 + j].wait_send()
                        own[2 + j].start()
                    remote(1 + j, (*chip, c), me).wait_recv()
                    passed[j].start()
                yield 2 + 2 * j, landed, (*chip, c)
                block = (*sibling_chips[j], 1 - c)
                yield 3 + 2 * j, (lambda j=j, block=block: remote(4 + j, block, me).wait_recv()), block

        for step, wait_for_it, block in arrivals():
            @pl.when(jnp.logical_and(i == fetch_at, jj == step - 1))
            def _():
                wait_for_it()
                load(step, rows(block)).start()

        for step in range(N_DEV):
            @pl.when(jnp.logical_and(i == 0, jj == step))
            def _():
                load(step, rows(me)).wait()

        o_ref[...] = _bdot(a_ref[...], w_vmem[lax.rem(jj, 2)], NT if w_t else NN).astype(o_ref.dtype)

        @pl.when(jnp.logical_and(i == n_i - 1, jj == N_DEV - 1))
        def _():
            for cp in [own[0], own[3]] + passed:
                cp.wait_send()
            local.wait()

    def swept(jj):
        x, y, c = _place()
        first, second = 2 + 2 * c, 4 - 2 * c
        flips = (0b000, 0b001, first, second + 1, second, first + 1, 0b110, 0b111)
        return jnp.bitwise_xor(4 * x + 2 * y + c, sum(jnp.where(jj == k, f, 0) for k, f in enumerate(flips)))

    if out3:
        out_spec = pl.BlockSpec((None, tm, n), lambda jj, i: (swept(jj), i, 0))
        out_shape = jax.ShapeDtypeStruct((N_DEV, m, n), BF16)
    else:
        out_spec = pl.BlockSpec((tm, n), lambda jj, i: (i, swept(jj)))
        out_shape = jax.ShapeDtypeStruct((m, N_DEV * n), BF16)
    return pl.pallas_call(
        body, name=name, grid=(N_DEV, n_i),
        in_specs=[pl.BlockSpec((tm, kk), lambda jj, i: (i, 0)), ANY], out_specs=[out_spec, ANY],
        scratch_shapes=[pltpu.VMEM((2,) + shard.shape, shard.dtype), pltpu.SemaphoreType.DMA((7,)),
                        pltpu.SemaphoreType.DMA((7,)), pltpu.SemaphoreType.DMA((3,))],
        out_shape=[out_shape, jax.ShapeDtypeStruct((N_DEV,) + shard.shape, shard.dtype)],
        compiler_params=_params(2))(a, shard)


def _sigmoid(v):
    return 0.5 * jnp.tanh(0.5 * v) + 0.5


def _resident(w):
    return pl.BlockSpec(w.shape, lambda i: (0,) * w.ndim, pipeline_mode=pl.Buffered(1))


def _ffn_out_loss(gu3, w3, add, g, target, *, name, tm=512):
    j2, m, n = gu3.shape
    j = j2 // 2
    nn = w3.shape[2]
    tm = min(tm, m)

    def body(gu_ref, w_ref, add_ref, g_ref, t_ref, dx_ref, dxb_ref, dg_ref, loss_ref, act_ref):
        i = pl.program_id(0)
        xv = add_ref[...]
        for jj in range(j):
            gate = gu_ref[0, jj].astype(F32)
            act = (gate * _sigmoid(gate) * gu_ref[1, jj].astype(F32)).astype(BF16)
            act_ref[jj] = act
            xv = xv + _bdot(act, w_ref[jj], NN)
        gv = g_ref[...]
        r = lax.rsqrt(jnp.mean(xv * xv, axis=-1, keepdims=True) + NORM_EPS)
        xhat = xv * r
        err = xhat * gv - t_ref[...]
        _acc_rows(i, loss_ref, 0.5 * jnp.sum(jnp.mean(err * err, axis=-1, keepdims=True), axis=0, keepdims=True))
        dy = err * (1.0 / nn)
        dxhat = dy * gv
        dx = r * (dxhat - xhat * jnp.mean(dxhat * xhat, axis=-1, keepdims=True))
        dx_ref[...] = dx
        dxb_ref[...] = dx.astype(BF16)
        _acc_rows(i, dg_ref, jnp.sum(dy * xhat, axis=0, keepdims=True))

    row = pl.BlockSpec((tm, nn), lambda i: (i, 0))
    return _call(body, name=name, grid=(m // tm,),
                 in_specs=[pl.BlockSpec((2, j, tm, n), lambda i: (0, 0, i, 0)), _resident(w3),
                           row, pl.BlockSpec(g.shape, lambda i: (0, 0)), row],
                 out_specs=[row, row, pl.BlockSpec((8, nn), lambda i: (0, 0)), pl.BlockSpec((8, LANES), lambda i: (0, 0)),
                            pl.BlockSpec((j, tm, n), lambda i: (0, i, 0))],
                 out_shape=[jax.ShapeDtypeStruct((m, nn), F32), jax.ShapeDtypeStruct((m, nn), BF16),
                            jax.ShapeDtypeStruct((8, nn), F32), jax.ShapeDtypeStruct((8, LANES), F32),
                            jax.ShapeDtypeStruct((j, m, n), BF16)],
                 scratch=[], args=[gu3.reshape(2, j, m, n), w3, add, g, target])


def _ffn_out_bwd(dy, w3, gu3, *, name, tm=1024):
    m, nn = dy.shape
    j, n, _ = w3.shape
    tm = min(tm, m)

    def body(dy_ref, w_ref, gu_ref, dgu_ref):
        da = _bdot(dy_ref[...], w_ref[...], NT)
        gate = gu_ref[0].astype(F32)
        up = gu_ref[1].astype(F32)
        sg = _sigmoid(gate)
        silu = gate * sg
        dgu_ref[0] = (da * up * (sg + silu * (1.0 - sg))).astype(BF16)
        dgu_ref[1] = (da * silu).astype(BF16)

    out = _call(body, name=name, grid=(m // tm, j),
                in_specs=[pl.BlockSpec((tm, nn), lambda i, jj: (i, 0)),
                          pl.BlockSpec((None, n, nn), lambda i, jj: (jj, 0, 0)),
                          pl.BlockSpec((2, None, tm, n), lambda i, jj: (0, jj, i, 0))],
                out_specs=[pl.BlockSpec((2, None, tm, n), lambda i, jj: (0, jj, i, 0))],
                out_shape=[jax.ShapeDtypeStruct((2, j, m, n), BF16)], scratch=[],
                args=[dy, w3, gu3.reshape(2, j, m, n)])[0]
    return out.reshape(2 * j, m, n)


def _rms_fwd_tail(xv, g_ref, h_ref):
    r = lax.rsqrt(jnp.mean(xv * xv, axis=-1, keepdims=True) + NORM_EPS)
    h_ref[...] = (xv * r * g_ref[...]).astype(BF16)


def _rms_bwd_tail(i, dh, x_ref, g_ref, dres_ref, dx_ref, dxb_ref, dg_ref):
    xv = x_ref[...]
    r = lax.rsqrt(jnp.mean(xv * xv, axis=-1, keepdims=True) + NORM_EPS)
    xhat = xv * r
    dxhat = dh * g_ref[...]
    dx = r * (dxhat - xhat * jnp.mean(dxhat * xhat, axis=-1, keepdims=True))
    if dres_ref is not None:
        dx = dx + dres_ref[...]
    dx_ref[...] = dx
    dxb_ref[...] = dx.astype(BF16)
    _acc_rows(i, dg_ref, jnp.sum(dh * xhat, axis=0, keepdims=True))


def _mm_nt_rms(dy, w3, x, g, dres, *, name, dy3=False, w_nn=False, tm=512, plan=None):
    j = w3.shape[0]
    m, kk = x.shape
    n = dy.shape[2] if dy3 else dy.shape[1] // j
    tm = min(tm, m)

    def body(dy_ref, w_ref, x_ref, g_ref, *rest):
        dres_ref = rest[0] if dres is not None else None
        dx_ref, dxb_ref, dg_ref = rest[-3:]
        dh = None
        for jj in range(j):
            piece = dy_ref[jj] if dy3 else dy_ref[:, jj * n:(jj + 1) * n]
            part = _bdot(piece, w_ref[jj], NN if w_nn else NT)
            dh = part if dh is None else dh + part
        _rms_bwd_tail(pl.program_id(0), dh, x_ref, g_ref, dres_ref, dx_ref, dxb_ref, dg_ref)

    row = pl.BlockSpec((tm, kk), lambda i: (i, 0))
    in_specs = [pl.BlockSpec((j, tm, n), lambda i: (0, i, 0)) if dy3 else pl.BlockSpec((tm, j * n), lambda i: (i, 0)),
                _resident(w3), row, pl.BlockSpec(g.shape, lambda i: (0, 0))]
    args = [dy, w3, x, g]
    if dres is not None:
        in_specs.append(row)
        args.append(dres)
    return _call(body, name=name, grid=(m // tm,), in_specs=in_specs,
                 out_specs=[row, row, pl.BlockSpec((8, kk), lambda i: (0, 0))],
                 out_shape=[jax.ShapeDtypeStruct((m, kk), F32), jax.ShapeDtypeStruct((m, kk), BF16),
                            jax.ShapeDtypeStruct((8, kk), F32)], scratch=[], args=args, plan=plan)


def _mix_out(o_a, y_b, proj, w_a, w_b, w, x, g, *, name, tm=512, plan=None):
    s, c = o_a.shape
    d = w.shape[1]
    tm = min(tm, s)

    def body(oa_ref, yb_ref, ga_ref, gb_ref, wa_ref, wb_ref, w_ref, x_ref, g_ref, x1_ref, h_ref, merged_ref, a_ref, b_ref):
        a_ref[...] = _bdot(oa_ref[...], wa_ref[...], NN).astype(BF16)
        b_ref[...] = _bdot(yb_ref[...], wb_ref[...], NN).astype(BF16)
        merged = (_sigmoid(ga_ref[...].astype(F32)) * a_ref[...].astype(F32)
                  + _sigmoid(gb_ref[...].astype(F32)) * b_ref[...].astype(F32)).astype(BF16)
        merged_ref[...] = merged
        xv = _bdot(merged, w_ref[...], NN) + x_ref[...]
        x1_ref[...] = xv
        _rms_fwd_tail(xv, g_ref, h_ref)

    row = pl.BlockSpec((tm, d), lambda i: (i, 0))
    narrow = pl.BlockSpec((tm, c), lambda i: (i, 0))
    whole = lambda arr: pl.BlockSpec(arr.shape, lambda i: (0,) * arr.ndim)
    return _call(body, name=name, grid=(s // tm,),
                 in_specs=[narrow, narrow, pl.BlockSpec((tm, d), lambda i: (i, 3)), pl.BlockSpec((tm, d), lambda i: (i, 4)),
                           whole(w_a), whole(w_b), whole(w), row, whole(g)],
                 out_specs=[row] * 5,
                 out_shape=[jax.ShapeDtypeStruct((s, d), F32)] + [jax.ShapeDtypeStruct((s, d), BF16)] * 4,
                 scratch=[], args=[o_a, y_b, proj, proj, w_a, w_b, w, x, g], plan=plan)


def _mm_tn_a3(a3, dy, *, name):
    j, t, n = a3.shape
    nn = dy.shape[1]
    return _call(functools.partial(_mm_body, TN, False), name=name, grid=(j,),
                 in_specs=[pl.BlockSpec((None, t, n), lambda jj: (jj, 0, 0)), pl.BlockSpec((t, nn), lambda jj: (0, 0))],
                 out_specs=[pl.BlockSpec((None, n, nn), lambda jj: (jj, 0, 0))],
                 out_shape=[jax.ShapeDtypeStruct((j, n, nn), BF16)], scratch=[], args=[a3, dy])[0]


def _mm_nt(dy, w3, *, name, out_dtype=BF16, tm=512, tn=1024, plan=None):
    m = dy.shape[0]
    j, kk, n = w3.shape
    tm, tn = min(tm, m), min(tn, kk)
    return _call(
        functools.partial(_mm_nt_body, j, n), name=name,
        grid=(m // tm, kk // tn),
        in_specs=[pl.BlockSpec((tm, j * n), lambda i, q: (i, 0)),
                  pl.BlockSpec((j, tn, n), lambda i, q: (0, q, 0))],
        out_specs=[pl.BlockSpec((tm, tn), lambda i, q: (i, q))],
        out_shape=[jax.ShapeDtypeStruct((m, kk), out_dtype)], scratch=[], args=[dy, w3], plan=plan)[0]


def _mm_tn(a, dy, n, *, name, out_dtype=BF16, tm=512, tn=None, k_tiles=None, plan=None):
    t, kk = a.shape
    j = dy.shape[1] // n
    tm, tn = min(tm, kk), n if tn is None else tn
    n_t = n // tn
    first, count = (0, kk // tm) if k_tiles is None else k_tiles
    return _call(
        functools.partial(_mm_body, TN, False), name=name,
        grid=(count, j * n_t),
        in_specs=[pl.BlockSpec((t, tm), lambda i, jj: (0, first + i)),
                  pl.BlockSpec((t, tn), lambda i, jj: (0, jj))],
        out_specs=[pl.BlockSpec((None, tm, tn), lambda i, jj: (jj // n_t, i, jj % n_t))],
        out_shape=[jax.ShapeDtypeStruct((j, count * tm, n), out_dtype)], scratch=[], args=[a, dy], plan=plan)[0]


def _rows(body, ins, outs, *, n_rows, tm, name, plan=None):
    tm = min(tm, n_rows)
    n_steps = n_rows // tm
    in_specs, args = [], []
    for arr, kind, width, block in ins:
        if kind == "row":
            in_specs.append(pl.BlockSpec((tm, width), functools.partial(lambda i, b: (i, b), b=block)))
        elif kind == "prev":
            in_specs.append(pl.BlockSpec((tm, width), functools.partial(lambda i, b: (jnp.maximum(i - 1, 0), b), b=block)))
        elif kind == "next":
            in_specs.append(pl.BlockSpec((tm, width), functools.partial(lambda i, b: (jnp.minimum(i + 1, n_steps - 1), b), b=block)))
        else:
            in_specs.append(pl.BlockSpec(arr.shape, functools.partial(lambda i, nd: (0,) * nd, nd=arr.ndim)))
        args.append(arr)
    out_specs, out_shape = [], []
    for shape, dtype, kind in outs:
        if kind == "row":
            out_specs.append(pl.BlockSpec((tm, shape[1]), lambda i: (i, 0)))
        else:
            out_specs.append(pl.BlockSpec(shape, functools.partial(lambda i, nd: (0,) * nd, nd=len(shape))))
        out_shape.append(jax.ShapeDtypeStruct(shape, dtype))

    def kern(*refs):
        body(pl.program_id(0), n_steps, *refs)

    return _call(kern, name=name, grid=(n_steps,), in_specs=in_specs, out_specs=out_specs, out_shape=out_shape,
                 scratch=[], args=args, plan=plan)


def _acc_rows(i, ref, value):
    @pl.when(i == 0)
    def _():
        ref[...] = jnp.zeros_like(ref)
    ref[...] += jnp.broadcast_to(value, ref.shape)


def _rms_fwd(x, g, *, name, tm=512):
    s, d = x.shape

    def body(i, n, x_ref, g_ref, h_ref):
        _rms_fwd_tail(x_ref[...], g_ref, h_ref)

    return _rows(body, [(x, "row", d, 0), (g, "full", 0, 0)], [((s, d), BF16, "row")], n_rows=s, tm=tm, name=name)[0]


def _rms_bwd(x, g, dh, dres, *, name, tm=512, plan=None):
    s, d = x.shape

    def body(i, n, x_ref, g_ref, dh_ref, dres_ref, dx_ref, dxb_ref, dg_ref):
        _rms_bwd_tail(i, dh_ref[...].astype(F32), x_ref, g_ref, dres_ref, dx_ref, dxb_ref, dg_ref)

    return _rows(body, [(x, "row", d, 0), (g, "full", 0, 0), (dh, "row", d, 0), (dres, "row", d, 0)],
                 [((s, d), F32, "row"), ((s, d), BF16, "row"), ((8, d), F32, "acc")],
                 n_rows=s, tm=tm, name=name, plan=plan)


def _mix_out_bwd(dx1b, w, br_a, br_b, proj, w_a, w_b, *, name, tm=512, plan=None):
    s, d = br_a.shape
    c = w_a.shape[0]
    tm = min(tm, s)

    def body(dy_ref, w_ref, a_ref, b_ref, ga_ref, gb_ref, wa_ref, wb_ref, da_ref, db_ref, dg_ref, doa_ref, dyb_ref):
        dm = _bdot(dy_ref[...], w_ref[...], NT)
        sa = _sigmoid(ga_ref[...].astype(F32))
        sb = _sigmoid(gb_ref[...].astype(F32))
        da_ref[...] = (dm * sa).astype(BF16)
        db_ref[...] = (dm * sb).astype(BF16)
        dg_ref[:, :d] = (dm * a_ref[...].astype(F32) * sa * (1.0 - sa)).astype(BF16)
        dg_ref[:, d:] = (dm * b_ref[...].astype(F32) * sb * (1.0 - sb)).astype(BF16)
        doa_ref[...] = _bdot(da_ref[...], wa_ref[...], NT).astype(BF16)
        dyb_ref[...] = _bdot(db_ref[...], wb_ref[...], NT).astype(BF16)

    row = pl.BlockSpec((tm, d), lambda i: (i, 0))
    narrow = pl.BlockSpec((tm, c), lambda i: (i, 0))
    whole = lambda arr: pl.BlockSpec(arr.shape, lambda i: (0,) * arr.ndim)
    return _call(body, name=name, grid=(s // tm,),
                 in_specs=[row, whole(w), row, row, pl.BlockSpec((tm, d), lambda i: (i, 3)),
                           pl.BlockSpec((tm, d), lambda i: (i, 4)), whole(w_a), whole(w_b)],
                 out_specs=[row, row, pl.BlockSpec((tm, 2 * d), lambda i: (i, 0)), narrow, narrow],
                 out_shape=[jax.ShapeDtypeStruct((s, d), BF16), jax.ShapeDtypeStruct((s, d), BF16),
                            jax.ShapeDtypeStruct((s, 2 * d), BF16), jax.ShapeDtypeStruct((s, c), BF16),
                            jax.ShapeDtypeStruct((s, c), BF16)],
                 scratch=[], args=[dx1b, w, br_a, br_b, proj, proj, w_a, w_b], plan=plan)


def _shift_down(cur, prev, k, first):
    row = lax.broadcasted_iota(jnp.int32, cur.shape, 0)
    out = jnp.where(row >= k, pltpu.roll(cur, k, 0), pltpu.roll(prev, k, 0))
    return jnp.where(jnp.logical_and(first, row < k), 0.0, out)


def _shift_up(cur, nxt, k, last):
    tm = cur.shape[0]
    row = lax.broadcasted_iota(jnp.int32, cur.shape, 0)
    out = jnp.where(row < tm - k, pltpu.roll(cur, tm - k, 0), pltpu.roll(nxt, tm - k, 0))
    return jnp.where(jnp.logical_and(last, row >= tm - k), 0.0, out)


def _conv_fwd(proj, conv_w, *, name, tm=512):
    s = proj.shape[0]
    c = CONV_WIDTH

    def body(i, n, u_ref, gb_ref, gc_ref, up_ref, gcp_ref, w_ref, y_ref):
        cu = gc_ref[...].astype(F32) * u_ref[...].astype(F32)
        cup = gcp_ref[...].astype(F32) * up_ref[...].astype(F32)
        first = i == 0
        y = (w_ref[0:1, :] * _shift_down(cu, cup, 2, first) + w_ref[1:2, :] * _shift_down(cu, cup, 1, first)
             + w_ref[2:3, :] * cu)
        y_ref[...] = (gb_ref[...].astype(F32) * y).astype(BF16)

    return _rows(body, [(proj, "row", c, 3), (proj, "row", c, 4), (proj, "row", c, 5),
                        (proj, "prev", c, 3), (proj, "prev", c, 5), (conv_w, "full", 0, 0)],
                 [((s, c), BF16, "row")], n_rows=s, tm=tm, name=name)[0]


def _conv_bwd(dy_b, proj, conv_w, *, name, tm=512, plan=None):
    s = proj.shape[0]
    c = CONV_WIDTH

    def body(i, n, dy_ref, u_ref, gb_ref, gc_ref, up_ref, gcp_ref, dyn_ref, gbn_ref, w_ref, d_ref, dw_ref):
        first, last = i == 0, i == n - 1
        u = u_ref[...].astype(F32)
        gb = gb_ref[...].astype(F32)
        gc = gc_ref[...].astype(F32)
        cu = gc * u
        cup = gcp_ref[...].astype(F32) * up_ref[...].astype(F32)
        cu1 = _shift_down(cu, cup, 1, first)
        cu2 = _shift_down(cu, cup, 2, first)
        conv = w_ref[0:1, :] * cu2 + w_ref[1:2, :] * cu1 + w_ref[2:3, :] * cu
        dy = dy_ref[...].astype(F32)
        dyc = dy * gb
        dycn = dyn_ref[...].astype(F32) * gbn_ref[...].astype(F32)
        dcu = (w_ref[2:3, :] * dyc + w_ref[1:2, :] * _shift_up(dyc, dycn, 1, last)
               + w_ref[0:1, :] * _shift_up(dyc, dycn, 2, last))
        d_ref[:, 0:c] = (dcu * gc).astype(BF16)
        d_ref[:, c:2 * c] = (dy * conv).astype(BF16)
        d_ref[:, 2 * c:3 * c] = (dcu * u).astype(BF16)
        row = lax.broadcasted_iota(jnp.int32, (8, c), 0)
        dw = (jnp.where(row == 0, jnp.sum(dyc * cu2, axis=0, keepdims=True), 0.0)
              + jnp.where(row == 1, jnp.sum(dyc * cu1, axis=0, keepdims=True), 0.0)
              + jnp.where(row == 2, jnp.sum(dyc * cu, axis=0, keepdims=True), 0.0))

        @pl.when(first)
        def _():
            dw_ref[...] = jnp.zeros_like(dw_ref)
        dw_ref[...] += dw

    return _rows(body, [(dy_b, "row", c, 0), (proj, "row", c, 3), (proj, "row", c, 4), (proj, "row", c, 5),
                        (proj, "prev", c, 3), (proj, "prev", c, 5), (dy_b, "next", c, 0), (proj, "next", c, 4),
                        (conv_w, "full", 0, 0)],
                 [((s, 3 * c), BF16, "row"), ((8, c), F32, "acc")], n_rows=s, tm=tm, name=name, plan=plan)


def _mem_probs(q, k, scale):
    sc = _bdot(q, k, NT) * scale
    sc = sc - jnp.max(sc, axis=-1, keepdims=True)
    p = jnp.exp(sc)
    return p / jnp.sum(p, axis=-1, keepdims=True)


def _mem_sublayer(hq, w_q, kv, w_o, x, g, *, name, tm=512, plan=None):
    s, d = hq.shape
    hd = d // MEM_HEADS
    scale = 1.0 / math.sqrt(hd)
    tm = min(tm, s)

    def body(hq_ref, wq_ref, kv_ref, wo_ref, x_ref, g_ref, q_ref, o_ref, x2_ref, h_ref):
        q_ref[...] = _bdot(hq_ref[...], wq_ref[...], NN).astype(BF16)
        for h in range(MEM_HEADS):
            cols = slice(h * hd, (h + 1) * hd)
            p = _mem_probs(q_ref[:, cols], kv_ref[:, cols], scale)
            o_ref[:, cols] = _bdot(p, kv_ref[:, d + h * hd:d + (h + 1) * hd], NN).astype(BF16)
        xv = _bdot(o_ref[...], wo_ref[...], NN) + x_ref[...]
        x2_ref[...] = xv
        _rms_fwd_tail(xv, g_ref, h_ref)

    row = pl.BlockSpec((tm, d), lambda i: (i, 0))
    whole = lambda a: pl.BlockSpec(a.shape, lambda i: (0,) * a.ndim)
    return _call(body, name=name, grid=(s // tm,),
                 in_specs=[row, whole(w_q), whole(kv), whole(w_o), row, whole(g)], out_specs=[row] * 4,
                 out_shape=[jax.ShapeDtypeStruct((s, d), BF16), jax.ShapeDtypeStruct((s, d), BF16),
                            jax.ShapeDtypeStruct((s, d), F32), jax.ShapeDtypeStruct((s, d), BF16)],
                 scratch=[], args=[hq, w_q, kv, w_o, x, g], plan=plan)


def _mem_sublayer_bwd(dx2b, dx2, x, g, qm, kv, w_q, w_o, *, name, tm=512, plan=None):
    s, d = qm.shape
    hd = d // MEM_HEADS
    scale = 1.0 / math.sqrt(hd)
    tm = min(tm, s)

    def body(dyb_ref, dres_ref, x_ref, g_ref, q_ref, kv_ref, wq_ref, wo_ref, dx_ref, dxb_ref, dg_ref, dq_ref, dkv_ref):
        i = pl.program_id(0)

        @pl.when(i == 0)
        def _():
            dkv_ref[...] = jnp.zeros_like(dkv_ref)
        dom = _bdot(dyb_ref[...], wo_ref[...], NT).astype(BF16)
        for h in range(MEM_HEADS):
            cols = slice(h * hd, (h + 1) * hd)
            vcols = slice(d + h * hd, d + (h + 1) * hd)
            q, k, v, do = q_ref[:, cols], kv_ref[:, cols], kv_ref[:, vcols], dom[:, cols]
            p = _mem_probs(q, k, scale)
            dp = _bdot(do, v, NT)
            ds = p * (dp - jnp.sum(dp * p, axis=-1, keepdims=True)) * scale
            dq_ref[:, cols] = _bdot(ds, k, NN).astype(BF16)
            dkv_ref[:, cols] += _bdot(ds, q, TN)
            dkv_ref[:, vcols] += _bdot(p, do, TN)
        dh = _bdot(dq_ref[...], wq_ref[...], NT)
        _rms_bwd_tail(i, dh, x_ref, g_ref, dres_ref, dx_ref, dxb_ref, dg_ref)

    row = pl.BlockSpec((tm, d), lambda i: (i, 0))
    whole = lambda a: pl.BlockSpec(a.shape, lambda i: (0,) * a.ndim)
    return _call(body, name=name, grid=(s // tm,),
                 in_specs=[row, row, row, whole(g), row, whole(kv), whole(w_q), whole(w_o)],
                 out_specs=[row, row, pl.BlockSpec((8, d), lambda i: (0, 0)), row, whole(kv)],
                 out_shape=[jax.ShapeDtypeStruct((s, d), F32), jax.ShapeDtypeStruct((s, d), BF16),
                            jax.ShapeDtypeStruct((8, d), F32), jax.ShapeDtypeStruct((s, d), BF16),
                            jax.ShapeDtypeStruct(kv.shape, F32)],
                 scratch=[], args=[dx2b, dx2, x, g, qm, kv, w_q, w_o], plan=plan)


def _sb_consts(t):
    row = lax.broadcasted_iota(jnp.int32, (t, t), 0)
    col = lax.broadcasted_iota(jnp.int32, (t, t), 1)
    lane = lax.broadcasted_iota(jnp.int32, (t, LANES), 1)
    return row, col, lane < SB_HEAD_DIM


def _sb_logits(q, k):
    z2 = jnp.minimum(_bdot(q, k, NT) * LOG2_E, SB_CLAMP)
    return z2, jnp.exp2(z2)


def _tri_sum(v, tri):
    hi = v.astype(BF16)
    lo = (v - hi.astype(F32)).astype(BF16)
    return _bdot(hi, tri, NN) + _bdot(lo, tri, NN)


def _sb_fwd(proj, *, name, plan=None):
    s = proj.shape[0]
    t, nh = SB_TILE, SB_STEP_HEADS
    n_q = s // t
    scale = 1.0 / math.sqrt(SB_HEAD_DIM)

    def body(q_ref, k_ref, v_ref, o_ref, c_ref, first_ref, acc_ref, c_scr):
        i = pl.program_id(1)
        row, col, head0 = _sb_consts(t)
        later = (row > col).astype(BF16)
        valid = col < row
        lanes = lambda h: slice((h // 2) * LANES, (h // 2 + 1) * LANES)
        q = [jnp.where(head0 == (h % 2 == 0), q_ref[:, lanes(h)] * scale, 0) for h in range(nh)]

        def tiles(kbs, diag_first, carry):
            rows = [pl.ds(pl.multiple_of(kb * t, t), t) for kb in kbs]
            jobs = [(n, h) for n in range(len(kbs)) for h in range(nh)]
            masked = lambda n: diag_first and n == 0
            zs = {(n, h): _sb_logits(q[h], k_ref[rows[n], lanes(h)]) for n, h in jobs}
            fail = {j: jnp.log2(1.0 + zs[j][1]) for j in jobs}
            fail = {j: jnp.where(valid, fail[j], 0.0) if masked(j[0]) else fail[j] for j in jobs}
            cum = {j: _tri_sum(fail[j], later) for j in jobs}
            run, before = list(carry), {}
            for n, h in jobs:
                before[n, h] = run[h]
                run[h] = run[h] + cum[n, h][:, 0:1] + fail[n, h][:, 0:1]
            w = {j: jnp.exp2(zs[j][0] - fail[j] - cum[j] - before[j]) for j in jobs}
            w = {j: jnp.where(valid, w[j], 0.0) if masked(j[0]) else w[j] for j in jobs}
            for n, h in jobs:
                acc_ref[h] += _bdot(w[n, h], v_ref[rows[n], lanes(h)], NN)
            return tuple(run)

        acc_ref[...] = jnp.zeros_like(acc_ref)
        zero = (jnp.zeros((t, 1), F32),) * nh

        def alive(carry):
            return (functools.reduce(jnp.minimum, [jnp.min(c) for c in carry]) < SB_DEAD).astype(jnp.int32)

        def step(state):
            new = tiles([state[0]], False, state[2:])
            return (state[0] - 1, alive(new)) + new

        @pl.when(i == 0)
        def _():
            for h, c in enumerate(tiles([i], True, zero)):
                c_scr[h] = c

        @pl.when(i > 0)
        def _():
            for h, c in enumerate(tiles([i, i - 1], True, zero)):
                c_scr[h] = c
        carry = tuple(c_scr[h] for h in range(nh))
        state = lax.while_loop(lambda st: jnp.logical_and(st[0] >= 0, st[1] > 0), step, (i - 2, alive(carry)) + carry)
        for b in range(nh // 2):
            o_ref[:, b * LANES:(b + 1) * LANES] = jnp.where(head0, acc_ref[2 * b], acc_ref[2 * b + 1]).astype(BF16)
        head = lax.broadcasted_iota(jnp.int32, (t, nh), 1)
        c_ref[...] = sum(jnp.where(head == h, state[2 + h], 0.0) for h in range(nh))
        first_ref[pl.program_id(0), i] = (jnp.maximum(state[0], -1) + 1).astype(F32)

    n_p, width = SB_HEADS // nh, nh * SB_HEAD_DIM
    k_blk, v_blk = SB_WIDTH // width, 2 * SB_WIDTH // width
    return _call(
        body, name=name, grid=(n_p, n_q),
        in_specs=[pl.BlockSpec((t, width), lambda p, i: (i, p)),
                  pl.BlockSpec((s, width), lambda p, i: (0, k_blk + p)),
                  pl.BlockSpec((s, width), lambda p, i: (0, v_blk + p))],
        out_specs=[pl.BlockSpec((t, width), lambda p, i: (i, p)),
                   pl.BlockSpec((None, t, nh), lambda p, i: (p, i, 0)),
                   pl.BlockSpec(memory_space=pltpu.SMEM)],
        out_shape=[jax.ShapeDtypeStruct((s, SB_WIDTH), BF16), jax.ShapeDtypeStruct((n_p, s, nh), F32),
                   jax.ShapeDtypeStruct((n_p, n_q), F32)],
        scratch=[pltpu.VMEM((nh, t, LANES), F32), pltpu.VMEM((nh, t, 1), F32)], args=[proj, proj, proj], plan=plan)


def _sb_bwd(proj, do_a, ctot, first, *, name, plan=None):
    s = proj.shape[0]
    t, nh = SB_TILE, SB_STEP_HEADS
    n_q = s // t
    scale = 1.0 / math.sqrt(SB_HEAD_DIM)

    def body(q_ref, k_ref, v_ref, do_ref, c_ref, first_ref, dq_ref, dk_ref, dv_ref, dq_acc, dk_acc, dv_acc):
        i = pl.program_id(1)
        kb0 = jnp.clip(first_ref[pl.program_id(0), i].astype(jnp.int32), 0, i)
        row, col, head0 = _sb_consts(t)
        upto = (row <= col).astype(BF16)
        before = (row < col).astype(BF16)
        valid = col < row
        lanes = lambda h: slice((h // 2) * LANES, (h // 2 + 1) * LANES)
        q2 = [jnp.where(head0 == (h % 2 == 0), q_ref[:, lanes(h)] * scale, 0) for h in range(nh)]
        do2 = [jnp.where(head0 == (h % 2 == 0), do_ref[:, lanes(h)], 0) for h in range(nh)]
        ctot2 = [c_ref[:, h:h + 1] for h in range(nh)]

        @pl.when(i == 0)
        def _():
            dk_acc[...] = jnp.zeros_like(dk_acc)
            dv_acc[...] = jnp.zeros_like(dv_acc)
        dq_acc[...] = jnp.zeros_like(dq_acc)

        def tiles(kbs, diag_last, carry):
            rows = [pl.ds(pl.multiple_of(kb * t, t), t) for kb in kbs]
            kt = {(n, h): k_ref[rows[n], lanes(h)] for n in range(len(kbs)) for h in range(nh)}
            jobs = list(kt)
            masked = lambda n: diag_last and n == len(kbs) - 1
            t_last = slice(t - 1, t)
            zs = {(n, h): _sb_logits(q2[h], kt[n, h]) for n, h in jobs}
            dw = {(n, h): _bdot(do2[h], v_ref[rows[n], lanes(h)], NT) for n, h in jobs}
            fail = {j: jnp.log2(1.0 + zs[j][1]) for j in jobs}
            fail = {j: jnp.where(valid, fail[j], 0.0) if masked(j[0]) else fail[j] for j in jobs}
            cum = {j: _tri_sum(fail[j], upto) for j in jobs}
            miss = {j: jnp.exp2(-fail[j]) for j in jobs}
            beta = {j: zs[j][1] * miss[j] for j in jobs}
            fail_run, fail_before = list(carry[0::2]), {}
            for n, h in jobs:
                fail_before[n, h] = fail_run[h]
                fail_run[h] = fail_run[h] + cum[n, h][:, t_last]
            w = {(n, h): beta[n, h] * jnp.exp2(fail_before[n, h] + cum[n, h] - ctot2[h]) for n, h in jobs}
            w = {j: jnp.where(valid, w[j], 0.0) if masked(j[0]) else w[j] for j in jobs}
            g = {j: w[j] * dw[j] for j in jobs}
            g_local = {j: _bdot(g[j], before, NN) for j in jobs}
            for n, h in jobs:
                dv_acc[rows[n], lanes(h)] += _bdot(w[n, h], do2[h], TN)
            g_run, dz = list(carry[1::2]), {}
            for n, h in jobs:
                g_sum = g_run[h] + g_local[n, h]
                dz[n, h] = g[n, h] * miss[n, h] - beta[n, h] * g_sum
                g_run[h] = g_sum[:, t_last] + g[n, h][:, t_last]
            dz = {j: jnp.where(valid, dz[j], 0.0) if masked(j[0]) else dz[j] for j in jobs}
            for n, h in jobs:
                dq_acc[h] += _bdot(dz[n, h], kt[n, h], NN)
                dk_acc[rows[n], lanes(h)] += _bdot(dz[n, h], q2[h], TN)
            return tuple(v for pair in zip(fail_run, g_run) for v in pair)

        zero = jnp.zeros((t, 1), F32)
        carry = lax.fori_loop(kb0, i - 1, lambda n, c: tiles([n], False, c), (zero,) * (2 * nh))

        @pl.when(i == 0)
        def _():
            tiles([i], True, carry)

        @pl.when(i > 0)
        def _():
            tiles([i - 1, i], True, carry)
        for b in range(nh // 2):
            dq_ref[:, b * LANES:(b + 1) * LANES] = (jnp.where(head0, dq_acc[2 * b], dq_acc[2 * b + 1])
                                                    * scale).astype(BF16)

        @pl.when(i == n_q - 1)
        def _():
            dk_ref[...] = dk_acc[...].astype(BF16)
            dv_ref[...] = dv_acc[...].astype(BF16)

    n_p, width = SB_HEADS // nh, nh * SB_HEAD_DIM
    k_blk, v_blk = SB_WIDTH // width, 2 * SB_WIDTH // width
    outs = _call(
        body, name=name, grid=(n_p, n_q),
        in_specs=[pl.BlockSpec((t, width), lambda p, i: (i, p)),
                  pl.BlockSpec((s, width), lambda p, i: (0, k_blk + p)),
                  pl.BlockSpec((s, width), lambda p, i: (0, v_blk + p)),
                  pl.BlockSpec((t, width), lambda p, i: (i, p)),
                  pl.BlockSpec((None, t, nh), lambda p, i: (p, i, 0)),
                  pl.BlockSpec(memory_space=pltpu.SMEM)],
        out_specs=[pl.BlockSpec((t, width), lambda p, i: (i, p)),
                   pl.BlockSpec((s, width), lambda p, i: (0, p)),
                   pl.BlockSpec((s, width), lambda p, i: (0, p))],
        out_shape=[jax.ShapeDtypeStruct((s, SB_WIDTH), BF16)] * 3,
        scratch=[pltpu.VMEM((nh, t, LANES), F32), pltpu.VMEM((s, width), F32), pltpu.VMEM((s, width), F32)],
        args=[proj, proj, proj, do_a, ctot, first], plan=plan)
    return jnp.concatenate(outs, axis=1)


def _mm_gathered(a, key, plan, *, name, out3=False, w_t=False):
    src = plan.gathering(key)
    if src is None:
        return _mm_nn(a, plan.weight(key), name=name, out3=out3, w_t=w_t, plan=plan)
    out, w_all = _mm_gathering(a, src, name=name, out3=out3, w_t=w_t)
    plan.set_weight(key, w_all)
    return out


def _local_step(x, mem, target, gains, plan):
    g_mix, g_memq, g_memkv, g_ffn, g_fin = gains
    d = x.shape[1]

    h0 = _rms_fwd(x, g_mix, name="rms_mix")
    proj = _mm_gathered(h0, "in", plan, name="mm_in")
    w_in = plan.weight("in")
    o_a, ctot, first = _sb_fwd(proj, name="sb_fwd", plan=plan)
    conv_w = plan.weight("conv")
    y_b = _conv_fwd(proj, conv_w, name="conv_fwd")
    w_a, w_b, w_mix = plan.weight("a"), plan.weight("b"), plan.weight("mix")
    x1, hq, merged, br_a, br_b = _mix_out(o_a, y_b, proj, w_a[0], w_b[0], w_mix[0], x, g_memq, name="mm_mix", plan=plan)
    w_mq, w_kv, w_mo = plan.weight("mq")[0], plan.weight("kv"), plan.weight("mo")[0]
    mn = _rms_fwd(mem, g_memkv, name="rms_memkv")
    kv = _mm_nn(mn, w_kv, name="mm_memkv")
    qm, om, x2, hf = _mem_sublayer(hq, w_mq, kv, w_mo, x1, g_ffn, name="mem_sublayer", plan=plan)
    gu = _mm_gathered(hf, "fi", plan, name="mm_ffn_in", out3=True, w_t=True)
    w_fi, w_fo = plan.weight("fi"), plan.weight("fo")
    dx3, dx3b, dg_fin, loss, act = _ffn_out_loss(gu, w_fo, x2, g_fin, target, name="mm_ffn_out")

    plan.grad("fo", _mm_tn_a3(act, dx3b, name="mm_d_w_ffn_out"))
    dgu = _ffn_out_bwd(dx3b, w_fo, gu, name="mm_d_act")
    plan.grad("fi", _mm_tn_a3(dgu, hf, name="mm_d_w_ffn_in"))
    dx2, dx2b, dg_ffn = _mm_nt_rms(dgu, w_fi, x2, g_ffn, dx3, name="mm_d_hf", dy3=True, w_nn=True, plan=plan)

    plan.grad("mo", _mm_tn(om, dx2b, d, name="mm_d_w_memo"))
    dx1, dx1b, dg_memq, dqm, dkv = _mem_sublayer_bwd(dx2b, dx2, x1, g_memq, qm, kv, w_mq, w_mo, name="mem_sublayer_bwd",
                                                    plan=plan)
    plan.grad("mq", _mm_tn(hq, dqm, d, name="mm_d_w_memq"))
    plan.grad("kv", _mm_tn(mn, dkv, w_kv.shape[2], name="mm_d_w_memkv"))
    _, _, dg_memkv = _mm_nt_rms(dkv, w_kv, mem, g_memkv, None, name="mm_d_mn")

    plan.grad("mix", _mm_tn(merged, dx1b, d, name="mm_d_w_mix"))
    dbr_a, dbr_b, dgab, do_a, dy_b = _mix_out_bwd(dx1b, w_mix[0], br_a, br_b, proj, w_a[0], w_b[0], name="mm_d_merged",
                                                 plan=plan)
    plan.grad("a", _mm_tn(o_a, dbr_a, d, name="mm_d_w_branch_a"))
    plan.grad("b", _mm_tn(y_b, dbr_b, d, name="mm_d_w_branch_b"))
    dconv, dconv_w = _conv_bwd(dy_b, proj, conv_w, name="conv_bwd", plan=plan)
    dqkv = _sb_bwd(proj, do_a, ctot, first, name="sb_bwd", plan=plan)
    dproj = jnp.concatenate([dqkv, dconv, dgab], axis=1)
    rows_in1 = d // IN_SPLIT[1] * (IN_SPLIT[1] - IN_SPLIT[0])
    plan.grad("in0", _mm_tn(h0, dproj, w_in.shape[2], name="mm_d_w_in0", tm=d - rows_in1, k_tiles=(0, 1)))
    plan.grad("in1", _mm_tn(h0, dproj, w_in.shape[2], name="mm_d_w_in1", tm=rows_in1,
                            k_tiles=(d // rows_in1 - 1, 1), plan=plan))
    dh0 = _mm_nt(dproj, w_in, name="mm_d_h0", out_dtype=F32, plan=plan)
    dx0, _, dg_mix = _rms_bwd(x, g_mix, dh0, dx1, name="rms_mix_bwd", plan=plan)

    return dx0, (dg_mix, dg_memq, dg_memkv, dg_ffn, dg_fin, dconv_w, loss)


def _row_tile(a, target=512):
    tm = min(a, target)
    while a % tm:
        tm -= 8
    return tm


def _sum_with_sibling(parts, recvs, core, *, name):
    n = len(parts)

    def body(core_ref, *refs):
        for p_ref, r_ref, o_ref in zip(refs[:n], refs[n:2 * n], refs[2 * n:]):
            o_ref[...] = (p_ref[...].astype(F32) + r_ref[...].astype(F32)).astype(o_ref.dtype)

    mine = [pl.BlockSpec((None,) + p.shape[1:], lambda q, core_ref: (2 * q + core_ref[0], 0, 0)) for p in parts]
    other = [pl.BlockSpec((None,) + p.shape[1:], lambda q, core_ref: (q, 0, 0)) for p in parts]
    return pl.pallas_call(
        body, name=name,
        grid_spec=pltpu.PrefetchScalarGridSpec(num_scalar_prefetch=1, grid=(N_CHIP,), in_specs=mine + other,
                                               out_specs=other),
        out_shape=[jax.ShapeDtypeStruct((N_CHIP,) + p.shape[1:], p.dtype) for p in parts],
        compiler_params=_params(1))(core, *parts, *recvs)


def _adam_math(wv, g, m, v):
    m = ADAM_B1 * m + (1.0 - ADAM_B1) * g
    v = ADAM_B2 * v + (1.0 - ADAM_B2) * (g * g)
    m_hat = m / (1.0 - ADAM_B1 ** ADAM_STEP)
    v_hat = v / (1.0 - ADAM_B2 ** ADAM_STEP)
    delta = -ADAM_LR * (m_hat / (jnp.sqrt(v_hat) + ADAM_EPS) + ADAM_WD * wv)
    return delta, m, v


def _adam_sharded(ws, ms, vs, owns, recvs, chip, *, name):
    n = len(ws)
    a, b = ws[0].shape
    tm = _row_tile(a)

    def body(chip_ref, *refs):
        ins, outs = refs[:5 * n], refs[5 * n:]
        for k in range(n):
            w_ref, m_ref, v_ref, own_ref, recv_ref = ins[k::n]
            g = own_ref[...].astype(F32)
            for j in range(3):
                g = g + recv_ref[j].astype(F32)
            delta, nm, nv = _adam_math(w_ref[...], g, m_ref[...], v_ref[...])
            for o_ref, value in zip(outs[k::n], (g, delta, nm, nv)):
                o_ref[...] = value

    tile = pl.BlockSpec((tm, b), lambda i, chip_ref: (i, 0))
    res = pl.pallas_call(
        body, name=name,
        grid_spec=pltpu.PrefetchScalarGridSpec(
            num_scalar_prefetch=1, grid=(a // tm,),
            in_specs=[tile] * (3 * n) + [pl.BlockSpec((None, tm, b), lambda i, chip_ref: (chip_ref[0], i, 0))] * n
            + [pl.BlockSpec((3, tm, b), lambda i, chip_ref: (0, i, 0))] * n,
            out_specs=[tile] * (4 * n)),
        out_shape=[jax.ShapeDtypeStruct((a, b), F32)] * (4 * n),
        compiler_params=_params(1))(chip, *ws, *ms, *vs, *owns, *recvs)
    return [res[k::n] for k in range(n)]


def _sum_devices(gathered, *, name):
    _, r, c = gathered.shape

    def body(g_ref, o_ref):
        total = g_ref[0]
        for j in range(1, N_DEV):
            total = total + g_ref[j]
        o_ref[...] = total

    return pl.pallas_call(body, name=name, out_shape=jax.ShapeDtypeStruct((r, c), F32))(gathered)


def _adam_small(ws, gs, ms, vs, *, name):
    n = len(ws)

    def body(*refs):
        ins, outs = refs[:4 * n], refs[4 * n:]
        for k in range(n):
            w_ref, g_ref, m_ref, v_ref = ins[k::n]
            for o_ref, value in zip(outs[k::n], _adam_math(w_ref[...], g_ref[...], m_ref[...], v_ref[...])):
                o_ref[...] = value

    res = pl.pallas_call(body, name=name, out_shape=[jax.ShapeDtypeStruct(w.shape, F32) for w in ws] * 3)(
        *ws, *gs, *ms, *vs)
    return [res[k::n] for k in range(n)]


BIG = ("in", "a", "b", "mix", "mq", "kv", "mo", "fi", "fo")
ROW_SHARDED = ("mix", "mq", "mo")
UNSHARDED = ("a", "b")
FFN_GROUPS = 4
IN_SPLIT = (3, 4)
SMALL_ROWS = 16


class _Plan:
    FUSED = ("in",)
    GATHER_ON = {"sb_fwd": ("a", "b", "mix", "mq", "mo", "conv", "fi0"), "mm_mix": ("kv",), "mem_sublayer": ("fi1",),
                 "mm_ffn_in": ("fo",)}
    SIBLING_ON = {"mm_d_hf": ("fo", "fi"), "mm_d_merged": ("mo", "mq", "kv"), "conv_bwd": ("mix", "a", "b"),
                  "mm_d_w_in1": ("in0",), "mm_d_h0": ("in1",)}
    LATE_AT = {"mm_mix": (7, 8), "mm_ffn_in": (7, 8)}
    CHIPS_ON = {"mem_sublayer_bwd": ("fo",), "sb_bwd": ("fi", "mo", "mq", "kv", "mix", "a", "b"), "mm_d_h0": ("in0",),
                "rms_mix_bwd": ("in1",)}

    def __init__(self, shards, core):
        self.shards, self.core = shards, core
        self.w, self.parts, self.chip_sums, self.from_chips = {}, {}, {}, {}

    def gathering(self, k):
        return self.shards[k] if k in self.FUSED else None

    def comm(self, name):
        comms = []
        if name in self.GATHER_ON:
            comms.append(_gather_comm([self.shards[k] for k in self.GATHER_ON[name]]))
        if name in self.SIBLING_ON:
            comms.append(_sibling_comm([self.parts[k] for k in self.SIBLING_ON[name]]))
        if name in self.CHIPS_ON:
            comms.append(_chips_comm([self.chip_sums[k] for k in self.CHIPS_ON[name]]))
        if not comms:
            return None
        comm = _join_comms(comms)
        comm.late_at = self.LATE_AT.get(name, comm.late_at)
        return comm

    def landed(self, name, outs):
        outs = list(outs)
        for k in self.GATHER_ON.get(name, ()):
            self.set_weight(k, outs.pop(0))
        keys = self.SIBLING_ON.get(name, ())
        if keys:
            sums = _sum_with_sibling([self.parts[k] for k in keys], [outs.pop(0) for _ in keys], self.core,
                                     name="sum_with_sibling_" + "_".join(keys))
            self.chip_sums.update(zip(keys, sums))
        for k in self.CHIPS_ON.get(name, ()):
            self.from_chips[k] = outs.pop(0)

    def set_weight(self, k, gathered):
        _, a, b = gathered.shape
        if k in ROW_SHARDED:
            gathered = gathered.reshape(1, N_DEV * a, b)
        elif k in UNSHARDED:
            gathered = jnp.transpose(gathered, (1, 0, 2)).reshape(1, a, N_DEV * b)
        elif k == "fo":
            gathered = gathered.reshape(FFN_GROUPS, N_DEV * a // FFN_GROUPS, b)
        elif k == "conv":
            n_conv = CONV_WIDTH // N_DEV
            gathered = jnp.transpose(gathered[:, :3, :n_conv], (1, 0, 2)).reshape(3, CONV_WIDTH)
        self.w[k] = gathered
        if k == "fi1":
            self.w["fi"] = jnp.concatenate([self.w["fi0"], gathered], axis=2)

    def weight(self, k):
        return self.w[k]

    def grad(self, k, g):
        _, a, b = g.shape
        if k in ROW_SHARDED:
            g = g.reshape(N_DEV, a // N_DEV, b)
        elif k in UNSHARDED:
            g = jnp.transpose(g.reshape(a, N_DEV, b // N_DEV), (1, 0, 2))
        elif k == "fo":
            g = g.reshape(N_DEV, FFN_GROUPS * a // N_DEV, b)
        self.parts[k] = g


def kernel(x, mem, norm_mix, w_in, conv_w, w_branch_a, w_branch_b, w_mix_out, norm_mem_q, norm_mem_kv, w_mem_q, w_mem_kv, w_mem_o, norm_ffn, w_ffn_in, w_ffn_out, norm_final, loss_target, m_norm_mix, m_w_in, m_conv_w, m_w_branch_a, m_w_branch_b, m_w_mix_out, m_norm_mem_q, m_norm_mem_kv, m_w_mem_q, m_w_mem_kv, m_w_mem_o, m_norm_ffn, m_w_ffn_in, m_w_ffn_out, m_norm_final, v_norm_mix, v_w_in, v_conv_w, v_w_branch_a, v_w_branch_b, v_w_mix_out, v_norm_mem_q, v_norm_mem_kv, v_w_mem_q, v_w_mem_kv, v_w_mem_o, v_norm_ffn, v_w_ffn_in, v_w_ffn_out, v_norm_final):
    d = x.shape[-1]
    xi, yi, ci = lax.axis_index("x"), lax.axis_index("y"), lax.axis_index("c")
    chip = jnp.reshape(2 * xi + yi, (1,)).astype(jnp.int32)
    dev = 4 * xi + 2 * yi + ci

    big_w = dict(zip(BIG, (w_in, w_branch_a, w_branch_b, w_mix_out, w_mem_q, w_mem_kv, w_mem_o, w_ffn_in, w_ffn_out)))
    big_m = dict(zip(BIG, (m_w_in, m_w_branch_a, m_w_branch_b, m_w_mix_out, m_w_mem_q, m_w_mem_kv, m_w_mem_o, m_w_ffn_in, m_w_ffn_out)))
    big_v = dict(zip(BIG, (v_w_in, v_w_branch_a, v_w_branch_b, v_w_mix_out, v_w_mem_q, v_w_mem_kv, v_w_mem_o, v_w_ffn_in, v_w_ffn_out)))

    flip = lambda t, k: jnp.transpose(t) if k == "fi" else t
    shards = {k: flip(big_w[k][0], k).astype(BF16) for k in BIG}
    shards["fi0"], shards["fi1"] = shards["fi"][:, :d // 2], shards["fi"][:, d // 2:]
    n_conv = conv_w.shape[-1]
    shards["conv"] = jnp.zeros((8, LANES), F32).at[:3, :n_conv].set(conv_w[0])
    plan = _Plan(shards, jnp.reshape(ci, (1,)).astype(jnp.int32))

    gains = (norm_mix, norm_mem_q, norm_mem_kv, norm_ffn, norm_final.reshape(1, d))
    dx0, small = _local_step(x[0], mem[0], loss_target[0], gains, plan)

    grads, deltas, new_m, new_v = {}, {}, {}, {}
    def adam(keys, rows=None, part=None):
        cut = (lambda t: t) if rows is None else (lambda t: t[rows])
        return _adam_sharded(*[[cut(flip(src[k][0], k)) for k in keys] for src in (big_w, big_m, big_v)],
                             [plan.chip_sums[part or k] for k in keys], [plan.from_chips[part or k] for k in keys],
                             chip, name="adam_" + "_".join(keys) + (part or "")[2:])

    results = {}
    for keys in (("a", "b"), ("mix", "mq", "mo"), ("kv",), ("fi",), ("fo",)):
        results.update(zip(keys, adam(keys)))
    half = big_w["in"].shape[1] * IN_SPLIT[0] // IN_SPLIT[1]
    lo, hi = adam(("in",), slice(0, half), "in0")[0], adam(("in",), slice(half, None), "in1")[0]
    results["in"] = [jnp.concatenate(pair, axis=0) for pair in zip(lo, hi)]
    for k in BIG:
        grads[k], deltas[k], new_m[k], new_v[k] = (flip(t, k).reshape(big_w[k].shape) for t in results[k])

    dg_mix, dg_memq, dg_memkv, dg_ffn, dg_fin, dconv_w, loss = small
    conv_rows = jnp.zeros((3, d), F32).at[:, :CONV_WIDTH].set(dconv_w[:3])
    block = jnp.concatenate([dg_mix[:1], dg_memq[:1], dg_memkv[:1], dg_ffn[:1], dg_fin[:1], conv_rows,
                             jnp.broadcast_to(loss[:1, :1], (1, d)), jnp.zeros((SMALL_ROWS - 9, d), F32)], axis=0)
    total = _sum_devices(_exchange(_gather_comm([block]), name="gather_small")[0], name="sum_small")
    g_conv = lax.dynamic_slice(total[5:8, :CONV_WIDTH], (0, dev * n_conv), (3, n_conv))
    small_w = [norm_mix, norm_mem_q, norm_mem_kv, norm_ffn, norm_final.reshape(1, d), conv_w[0]]
    small_m = [m_norm_mix, m_norm_mem_q, m_norm_mem_kv, m_norm_ffn, m_norm_final.reshape(1, d), m_conv_w[0]]
    small_v = [v_norm_mix, v_norm_mem_q, v_norm_mem_kv, v_norm_ffn, v_norm_final.reshape(1, d), v_conv_w[0]]
    small_g = [total[0:1], total[1:2], total[2:3], total[3:4], total[4:5], g_conv]
    small_names = ["norm_mix", "norm_mem_q", "norm_mem_kv", "norm_ffn", "norm_final", "conv_w"]
    sg, sd, sm, sv = {}, {}, {}, {}
    small_out = _adam_small(small_w, small_g, small_m, small_v, name="adam_small")
    for nme, wv, g, (dl, nm, nv) in zip(small_names, small_w, small_g, small_out):
        shape = norm_final.shape if nme == "norm_final" else (conv_w.shape if nme == "conv_w" else wv.shape)
        sg[nme], sd[nme], sm[nme], sv[nme] = (t.reshape(shape) for t in (g, dl, nm, nv))

    def ordered(big, sml):
        return (sml["norm_mix"], big["in"], sml["conv_w"], big["a"], big["b"], big["mix"], sml["norm_mem_q"],
                sml["norm_mem_kv"], big["mq"], big["kv"], big["mo"], sml["norm_ffn"], big["fi"], big["fo"],
                sml["norm_final"])

    loss_out = total[8, 0]
    grad_x = dx0.reshape(x.shape)
    return (loss_out, grad_x, *ordered(grads, sg), *ordered(deltas, sd), *ordered(new_m, sm), *ordered(new_v, sv))
```

```python
import functools
import math

import jax
import jax.numpy as jnp
from jax import lax
from jax.experimental import pallas as pl
from jax.experimental.pallas import tpu as pltpu

F32 = jnp.float32
BF16 = jnp.bfloat16
MESH = pl.DeviceIdType.MESH

N_DEV = 8
N_CHIP = 4
NORM_EPS = 1e-6
SB_HEADS = 8
SB_HEAD_DIM = 64
SB_WIDTH = SB_HEADS * SB_HEAD_DIM
CONV_WIDTH = 512
MEM_HEADS = 4
ADAM_LR = 0.001
ADAM_B1 = 0.9
ADAM_B2 = 0.999
ADAM_EPS = 1e-08
ADAM_WD = 0.01
ADAM_STEP = 10

LANES = 128
VMEM_LIMIT_BYTES = 52 * 1024 * 1024
SB_TILE = 256
SB_STEP_HEADS = 4
SB_DEAD = 159.0
SB_CLAMP = 126.0
LOG2_E = 1.4426950408889634

ANY = pl.BlockSpec(memory_space=pl.ANY)


def _params(n_grid):
    return pltpu.CompilerParams(dimension_semantics=("arbitrary",) * n_grid, vmem_limit_bytes=VMEM_LIMIT_BYTES)


def _bdot(a, b, dims):
    return lax.dot_general(a.astype(BF16), b.astype(BF16), (dims, ((), ())), preferred_element_type=F32)


NN = ((1,), (0,))
NT = ((1,), (1,))
TN = ((0,), (0,))


class _Comm:
    def __init__(self, ins, outs, n_sems, start, finish, late=None):
        self.ins, self.outs, self.n_sems, self.start, self.finish = ins, outs, n_sems, start, finish
        self.late = late if late is not None else (lambda ins, outs, sems: None)
        self.late_at = (1, 1)

    def sem_shapes(self):
        return [pltpu.SemaphoreType.DMA((k,)) for k in self.n_sems]


def _place():
    return lax.axis_index("x"), lax.axis_index("y"), lax.axis_index("c")


def _neighbours(x, y, c):
    return [(jnp.bitwise_xor(x, c), jnp.bitwise_xor(y, 1 - c)), (jnp.bitwise_xor(x, 1 - c), jnp.bitwise_xor(y, c)),
            (1 - x, 1 - y)]


def _gather_comm(shards):
    n = len(shards)

    def copies(ins, outs, sems):
        send_sems, recv_sems, _ = sems
        x, y, c = _place()
        chips = [(1 - x, y), (x, 1 - y), (1 - x, 1 - y)]

        def copy(a, k, block, to, from_shard=False):
            dst = outs[a].at[4 * block[0] + 2 * block[1] + block[2]]
            return pltpu.make_async_remote_copy(
                src_ref=ins[a] if from_shard else dst, dst_ref=dst, send_sem=send_sems.at[a * 7 + k],
                recv_sem=recv_sems.at[a * 7 + k], device_id=to, device_id_type=MESH)

        me, sibling = (x, y, c), (x, y, 1 - c)
        own = [[copy(a, 0, me, sibling, True)] + [copy(a, 1 + j, me, (*chip, c), True) for j, chip in enumerate(chips)]
               for a in range(n)]
        landed = [[copy(a, 1 + j, (*chip, c), me) for j, chip in enumerate(chips)] for a in range(n)]
        passed = [[copy(a, 4 + j, (*chip, c), sibling) for j, chip in enumerate(chips)] for a in range(n)]
        from_sibling = [[copy(a, 0, sibling, me)] + [copy(a, 4 + j, (*chip, 1 - c), me) for j, chip in enumerate(chips)]
                        for a in range(n)]
        local = [pltpu.make_async_copy(ins[a], outs[a].at[4 * x + 2 * y + c], sems[2].at[a]) for a in range(n)]
        return own, landed, passed, from_sibling, local

    def start(ins, outs, sems):
        own, _, _, _, local = copies(ins, outs, sems)
        for a in range(n):
            local[a].start()
            for cp in own[a]:
                cp.start()

    def late(ins, outs, sems):
        _, landed, passed, _, _ = copies(ins, outs, sems)
        for a in range(n):
            for arrived, onward in zip(landed[a], passed[a]):
                arrived.wait_recv()
                onward.start()

    def finish(ins, outs, sems):
        own, _, passed, from_sibling, local = copies(ins, outs, sems)
        for a in range(n):
            for cp in from_sibling[a]:
                cp.wait_recv()
        for a in range(n):
            for cp in own[a] + passed[a]:
                cp.wait_send()
            local[a].wait()

    outs = [jax.ShapeDtypeStruct((N_DEV,) + s.shape, s.dtype) for s in shards]
    return _Comm(list(shards), outs, (7 * n, 7 * n, n), start, finish, late)


def _sibling_comm(parts):
    n = len(parts)

    def copies(ins, outs, sems):
        x, y, c = _place()
        return [pltpu.make_async_remote_copy(
            src_ref=ins[a].at[2 * q + 1 - c], dst_ref=outs[a].at[q], send_sem=sems[0].at[a * N_CHIP + q],
            recv_sem=sems[1].at[a * N_CHIP + q], device_id=(x, y, 1 - c), device_id_type=MESH)
            for a in range(n) for q in range(N_CHIP)]

    def start(ins, outs, sems):
        for cp in copies(ins, outs, sems):
            cp.start()

    def finish(ins, outs, sems):
        cps = copies(ins, outs, sems)
        for cp in cps:
            cp.wait_recv()
        for cp in cps:
            cp.wait_send()

    outs = [jax.ShapeDtypeStruct((N_CHIP,) + p.shape[1:], p.dtype) for p in parts]
    return _Comm(list(parts), outs, (N_CHIP * n, N_CHIP * n), start, finish)


def _chips_comm(parts):
    n = len(parts)

    def copies(ins, outs, sems):
        x, y, c = _place()
        chips = [(1 - x, y), (x, 1 - y), (1 - x, 1 - y)]
        return [pltpu.make_async_remote_copy(
            src_ref=ins[a].at[2 * px + py], dst_ref=outs[a].at[j], send_sem=sems[0].at[a * 3 + j],
            recv_sem=sems[1].at[a * 3 + j], device_id=(px, py, c), device_id_type=MESH)
            for a in range(n) for j, (px, py) in enumerate(chips)]

    def start(ins, outs, sems):
        for cp in copies(ins, outs, sems):
            cp.start()

    def finish(ins, outs, sems):
        cps = copies(ins, outs, sems)
        for cp in cps:
            cp.wait_recv()
        for cp in cps:
            cp.wait_send()

    outs = [jax.ShapeDtypeStruct((3,) + p.shape[1:], p.dtype) for p in parts]
    return _Comm(list(parts), outs, (3 * n, 3 * n), start, finish)


def _join_comms(comms):
    if len(comms) == 1:
        return comms[0]

    def split(refs, counts):
        out, at = [], 0
        for n in counts:
            out.append(refs[at:at + n])
            at += n
        return out

    def each(method):
        def run(ins, outs, sems):
            parts = zip(comms, split(ins, [len(c.ins) for c in comms]), split(outs, [len(c.outs) for c in comms]),
                        split(sems, [len(c.n_sems) for c in comms]))
            for c, c_ins, c_outs, c_sems in parts:
                getattr(c, method)(c_ins, c_outs, c_sems)
        return run

    return _Comm([a for c in comms for a in c.ins], [o for c in comms for o in c.outs],
                 tuple(k for c in comms for k in c.n_sems), each("start"), each("finish"), each("late"))


def _exchange(comm, *, name):
    n_ci, n_co = len(comm.ins), len(comm.outs)

    def kern(*refs):
        c_ins, c_outs, sems = refs[:n_ci], refs[n_ci:n_ci + n_co], refs[n_ci + n_co:]
        comm.start(c_ins, c_outs, sems)
        comm.late(c_ins, c_outs, sems)
        comm.finish(c_ins, c_outs, sems)

    return pl.pallas_call(kern, name=name, in_specs=[ANY] * n_ci, out_specs=[ANY] * n_co, out_shape=comm.outs,
                          scratch_shapes=comm.sem_shapes())(*comm.ins)


def _call(body, *, name, grid, in_specs, out_specs, out_shape, scratch, args, plan=None):
    comm = plan.comm(name) if plan is not None else None
    if comm is None:
        return list(pl.pallas_call(functools.partial(body), name=name, grid=grid, in_specs=in_specs,
                                   out_specs=out_specs, out_shape=out_shape, scratch_shapes=scratch,
                                   compiler_params=_params(len(grid)))(*args))
    n_in, n_out, n_scr, n_ci, n_co = len(in_specs), len(out_specs), len(scratch), len(comm.ins), len(comm.outs)

    def kern(*refs):
        ins, c_ins, refs = refs[:n_in], refs[n_in:n_in + n_ci], refs[n_in + n_ci:]
        outs, c_outs, refs = refs[:n_out], refs[n_out:n_out + n_co], refs[n_out + n_co:]
        scr, sems = refs[:n_scr], refs[n_scr:]
        ids = [pl.program_id(ax) for ax in range(len(grid))]
        step = functools.reduce(lambda at, ig: at * ig[1] + ig[0], zip(ids, grid), 0)
        n_steps = math.prod(grid)

        @pl.when(step == 0)
        def _():
            comm.start(c_ins, c_outs, sems)

        @pl.when(step == min(n_steps * comm.late_at[0] // comm.late_at[1], n_steps - 1))
        def _():
            comm.late(c_ins, c_outs, sems)
        body(*ins, *outs, *scr)

        @pl.when(step == n_steps - 1)
        def _():
            comm.finish(c_ins, c_outs, sems)

    res = pl.pallas_call(kern, name=name, grid=grid, in_specs=list(in_specs) + [ANY] * n_ci,
                         out_specs=list(out_specs) + [ANY] * n_co, out_shape=list(out_shape) + comm.outs,
                         scratch_shapes=list(scratch) + comm.sem_shapes(),
                         compiler_params=_params(len(grid)))(*args, *comm.ins)
    plan.landed(name, list(res[n_out:]))
    return list(res[:n_out])


def _mm_body(dims, has_add, *refs):
    if has_add:
        a_ref, b_ref, add_ref, o_ref = refs
        total = _bdot(a_ref[...], b_ref[...], dims) + add_ref[...]
    else:
        a_ref, b_ref, o_ref = refs
        total = _bdot(a_ref[...], b_ref[...], dims)
    o_ref[...] = total.astype(o_ref.dtype)


def _mm_nt_body(j, n, dy_ref, w_ref, o_ref):
    total = _bdot(dy_ref[:, 0:n], w_ref[0], NT)
    for jj in range(1, j):
        total = total + _bdot(dy_ref[:, jj * n:(jj + 1) * n], w_ref[jj], NT)
    o_ref[...] = total.astype(o_ref.dtype)


def _mm_nn(a, w3, *, name, out_dtype=BF16, add=None, tm=1024, tn=None, out3=False, w_t=False, plan=None):
    m, kk = a.shape
    j, n = w3.shape[0], w3.shape[1 if w_t else 2]
    tm, tn = min(tm, m), n if tn is None else tn
    n_t = n // tn
    in_specs = [pl.BlockSpec((tm, kk), lambda i, jj: (i, 0)),
                pl.BlockSpec((None, tn, kk), lambda i, jj: (jj // n_t, jj % n_t, 0)) if w_t else
                pl.BlockSpec((None, kk, tn), lambda i, jj: (jj // n_t, 0, jj % n_t))]
    args = [a, w3]
    if add is not None:
        in_specs.append(pl.BlockSpec((tm, tn), lambda i, jj: (i, jj)))
        args.append(add)
    if out3:
        out_spec = pl.BlockSpec((None, tm, tn), lambda i, jj: (jj // n_t, i, jj % n_t))
        out_shape = jax.ShapeDtypeStruct((j, m, n), out_dtype)
    else:
        out_spec = pl.BlockSpec((tm, tn), lambda i, jj: (i, jj))
        out_shape = jax.ShapeDtypeStruct((m, j * n), out_dtype)
    return _call(
        functools.partial(_mm_body, NT if w_t else NN, add is not None), name=name, grid=(m // tm, j * n_t),
        in_specs=in_specs, out_specs=[out_spec], out_shape=[out_shape], scratch=[], args=args, plan=plan)[0]


def _mm_gathering(a, shard, *, name, out3=False, w_t=False, tm=1024):
    m, kk = a.shape
    n = shard.shape[0 if w_t else 1]
    tm = min(tm, m)
    n_i = m // tm
    fetch_at = min(1, n_i - 1)

    def body(a_ref, shard_ref, o_ref, w_all, w_vmem, send_sems, recv_sems, copy_sems):
        jj, i = pl.program_id(0), pl.program_id(1)
        x, y, c = _place()
        me, sibling = (x, y, c), (x, y, 1 - c)
        chips = _neighbours(x, y, c)
        sibling_chips = [chips[1], chips[0], chips[2]]

        def rows(block):
            return w_all.at[4 * block[0] + 2 * block[1] + block[2]]

        def remote(k, block, to, from_shard=False):
            return pltpu.make_async_remote_copy(
                src_ref=shard_ref if from_shard else rows(block), dst_ref=rows(block), send_sem=send_sems.at[k],
                recv_sem=recv_sems.at[k], device_id=to, device_id_type=MESH)

        def load(step, src):
            return pltpu.make_async_copy(src, w_vmem.at[step % 2], copy_sems.at[1 + step % 2])

        own = [remote(0, me, sibling, True), remote(1, me, (*chips[0], c), True), remote(2, me, (*chips[1], c), True),
               remote(3, (*chips[0], c), (*chips[1], c))]
        passed = [remote(4 + j, (*chip, c), sibling) for j, chip in enumerate(chips)]
        local = pltpu.make_async_copy(shard_ref, rows(me), copy_sems.at[0])

        @pl.when(jnp.logical_and(i == 0, jj == 0))
        def _():
            local.start()
            own[0].start()
            own[1].start()
            load(0, shard_ref).start()

        def arrivals():
            yield 1, (lambda: remote(0, sibling, me).wait_recv()), sibling
            for j, chip in enumerate(chips):
                def landed(j=j, chip=chip):
                    if j < 2:
                        own[1 + j].wait_send()
                        own[2 + j].start()
                    remote(1 + j, (*chip, c), me).wait_recv()
                    passed[j].start()
                yield 2 + 2 * j, landed, (*chip, c)
                block = (*sibling_chips[j], 1 - c)
                yield 3 + 2 * j, (lambda j=j, block=block: remote(4 + j, block, me).wait_recv()), block

        for step, wait_for_it, block in arrivals():
            @pl.when(jnp.logical_and(i == fetch_at, jj == step - 1))
            def _():
                wait_for_it()
                load(step, rows(block)).start()

        for step in range(N_DEV):
            @pl.when(jnp.logical_and(i == 0, jj == step))
            def _():
                load(step, rows(me)).wait()

        o_ref[...] = _bdot(a_ref[...], w_vmem[lax.rem(jj, 2)], NT if w_t else NN).astype(o_ref.dtype)

        @pl.when(jnp.logical_and(i == n_i - 1, jj == N_DEV - 1))
        def _():
            for cp in [own[0], own[3]] + passed:
                cp.wait_send()
            local.wait()

    def swept(jj):
        x, y, c = _place()
        first, second = 2 + 2 * c, 4 - 2 * c
        flips = (0b000, 0b001, first, second + 1, second, first + 1, 0b110, 0b111)
        return jnp.bitwise_xor(4 * x + 2 * y + c, sum(jnp.where(jj == k, f, 0) for k, f in enumerate(flips)))

    if out3:
        out_spec = pl.BlockSpec((None, tm, n), lambda jj, i: (swept(jj), i, 0))
        out_shape = jax.ShapeDtypeStruct((N_DEV, m, n), BF16)
    else:
        out_spec = pl.BlockSpec((tm, n), lambda jj, i: (i, swept(jj)))
        out_shape = jax.ShapeDtypeStruct((m, N_DEV * n), BF16)
    return pl.pallas_call(
        body, name=name, grid=(N_DEV, n_i),
        in_specs=[pl.BlockSpec((tm, kk), lambda jj, i: (i, 0)), ANY], out_specs=[out_spec, ANY],
        scratch_shapes=[pltpu.VMEM((2,) + shard.shape, shard.dtype), pltpu.SemaphoreType.DMA((7,)),
                        pltpu.SemaphoreType.DMA((7,)), pltpu.SemaphoreType.DMA((3,))],
        out_shape=[out_shape, jax.ShapeDtypeStruct((N_DEV,) + shard.shape, shard.dtype)],
        compiler_params=_params(2))(a, shard)


def _sigmoid(v):
    return 0.5 * jnp.tanh(0.5 * v) + 0.5


def _resident(w):
    return pl.BlockSpec(w.shape, lambda i: (0,) * w.ndim, pipeline_mode=pl.Buffered(1))


def _ffn_out_loss(gu3, w3, add, g, target, *, name, tm=512):
    j2, m, n = gu3.shape
    j = j2 // 2
    nn = w3.shape[2]
    tm = min(tm, m)

    def body(gu_ref, w_ref, add_ref, g_ref, t_ref, dx_ref, dxb_ref, dg_ref, loss_ref, act_ref):
        i = pl.program_id(0)
        xv = add_ref[...]
        for jj in range(j):
            gate = gu_ref[0, jj].astype(F32)
            act = (gate * _sigmoid(gate) * gu_ref[1, jj].astype(F32)).astype(BF16)
            act_ref[jj] = act
            xv = xv + _bdot(act, w_ref[jj], NN)
        gv = g_ref[...]
        r = lax.rsqrt(jnp.mean(xv * xv, axis=-1, keepdims=True) + NORM_EPS)
        xhat = xv * r
        err = xhat * gv - t_ref[...]
        _acc_rows(i, loss_ref, 0.5 * jnp.sum(jnp.mean(err * err, axis=-1, keepdims=True), axis=0, keepdims=True))
        dy = err * (1.0 / nn)
        dxhat = dy * gv
        dx = r * (dxhat - xhat * jnp.mean(dxhat * xhat, axis=-1, keepdims=True))
        dx_ref[...] = dx
        dxb_ref[...] = dx.astype(BF16)
        _acc_rows(i, dg_ref, jnp.sum(dy * xhat, axis=0, keepdims=True))

    row = pl.BlockSpec((tm, nn), lambda i: (i, 0))
    return _call(body, name=name, grid=(m // tm,),
                 in_specs=[pl.BlockSpec((2, j, tm, n), lambda i: (0, 0, i, 0)), _resident(w3),
                           row, pl.BlockSpec(g.shape, lambda i: (0, 0)), row],
                 out_specs=[row, row, pl.BlockSpec((8, nn), lambda i: (0, 0)), pl.BlockSpec((8, LANES), lambda i: (0, 0)),
                            pl.BlockSpec((j, tm, n), lambda i: (0, i, 0))],
                 out_shape=[jax.ShapeDtypeStruct((m, nn), F32), jax.ShapeDtypeStruct((m, nn), BF16),
                            jax.ShapeDtypeStruct((8, nn), F32), jax.ShapeDtypeStruct((8, LANES), F32),
                            jax.ShapeDtypeStruct((j, m, n), BF16)],
                 scratch=[], args=[gu3.reshape(2, j, m, n), w3, add, g, target])


def _ffn_out_bwd(dy, w3, gu3, *, name, tm=1024):
    m, nn = dy.shape
    j, n, _ = w3.shape
    tm = min(tm, m)

    def body(dy_ref, w_ref, gu_ref, dgu_ref):
        da = _bdot(dy_ref[...], w_ref[...], NT)
        gate = gu_ref[0].astype(F32)
        up = gu_ref[1].astype(F32)
        sg = _sigmoid(gate)
        silu = gate * sg
        dgu_ref[0] = (da * up * (sg + silu * (1.0 - sg))).astype(BF16)
        dgu_ref[1] = (da * silu).astype(BF16)

    out = _call(body, name=name, grid=(m // tm, j),
                in_specs=[pl.BlockSpec((tm, nn), lambda i, jj: (i, 0)),
                          pl.BlockSpec((None, n, nn), lambda i, jj: (jj, 0, 0)),
                          pl.BlockSpec((2, None, tm, n), lambda i, jj: (0, jj, i, 0))],
                out_specs=[pl.BlockSpec((2, None, tm, n), lambda i, jj: (0, jj, i, 0))],
                out_shape=[jax.ShapeDtypeStruct((2, j, m, n), BF16)], scratch=[],
                args=[dy, w3, gu3.reshape(2, j, m, n)])[0]
    return out.reshape(2 * j, m, n)


def _rms_fwd_tail(xv, g_ref, h_ref):
    r = lax.rsqrt(jnp.mean(xv * xv, axis=-1, keepdims=True) + NORM_EPS)
    h_ref[...] = (xv * r * g_ref[...]).astype(BF16)


def _rms_bwd_tail(i, dh, x_ref, g_ref, dres_ref, dx_ref, dxb_ref, dg_ref):
    xv = x_ref[...]
    r = lax.rsqrt(jnp.mean(xv * xv, axis=-1, keepdims=True) + NORM_EPS)
    xhat = xv * r
    dxhat = dh * g_ref[...]
    dx = r * (dxhat - xhat * jnp.mean(dxhat * xhat, axis=-1, keepdims=True))
    if dres_ref is not None:
        dx = dx + dres_ref[...]
    dx_ref[...] = dx
    dxb_ref[...] = dx.astype(BF16)
    _acc_rows(i, dg_ref, jnp.sum(dh * xhat, axis=0, keepdims=True))


def _mm_nt_rms(dy, w3, x, g, dres, *, name, dy3=False, w_nn=False, tm=512, plan=None):
    j = w3.shape[0]
    m, kk = x.shape
    n = dy.shape[2] if dy3 else dy.shape[1] // j
    tm = min(tm, m)

    def body(dy_ref, w_ref, x_ref, g_ref, *rest):
        dres_ref = rest[0] if dres is not None else None
        dx_ref, dxb_ref, dg_ref = rest[-3:]
        dh = None
        for jj in range(j):
            piece = dy_ref[jj] if dy3 else dy_ref[:, jj * n:(jj + 1) * n]
            part = _bdot(piece, w_ref[jj], NN if w_nn else NT)
            dh = part if dh is None else dh + part
        _rms_bwd_tail(pl.program_id(0), dh, x_ref, g_ref, dres_ref, dx_ref, dxb_ref, dg_ref)

    row = pl.BlockSpec((tm, kk), lambda i: (i, 0))
    in_specs = [pl.BlockSpec((j, tm, n), lambda i: (0, i, 0)) if dy3 else pl.BlockSpec((tm, j * n), lambda i: (i, 0)),
                _resident(w3), row, pl.BlockSpec(g.shape, lambda i: (0, 0))]
    args = [dy, w3, x, g]
    if dres is not None:
        in_specs.append(row)
        args.append(dres)
    return _call(body, name=name, grid=(m // tm,), in_specs=in_specs,
                 out_specs=[row, row, pl.BlockSpec((8, kk), lambda i: (0, 0))],
                 out_shape=[jax.ShapeDtypeStruct((m, kk), F32), jax.ShapeDtypeStruct((m, kk), BF16),
                            jax.ShapeDtypeStruct((8, kk), F32)], scratch=[], args=args, plan=plan)


def _mix_out(o_a, y_b, proj, w_a, w_b, w, x, g, *, name, tm=512, plan=None):
    s, c = o_a.shape
    d = w.shape[1]
    tm = min(tm, s)

    def body(oa_ref, yb_ref, ga_ref, gb_ref, wa_ref, wb_ref, w_ref, x_ref, g_ref, x1_ref, h_ref, merged_ref, a_ref, b_ref):
        a_ref[...] = _bdot(oa_ref[...], wa_ref[...], NN).astype(BF16)
        b_ref[...] = _bdot(yb_ref[...], wb_ref[...], NN).astype(BF16)
        merged = (_sigmoid(ga_ref[...].astype(F32)) * a_ref[...].astype(F32)
                  + _sigmoid(gb_ref[...].astype(F32)) * b_ref[...].astype(F32)).astype(BF16)
        merged_ref[...] = merged
        xv = _bdot(merged, w_ref[...], NN) + x_ref[...]
        x1_ref[...] = xv
        _rms_fwd_tail(xv, g_ref, h_ref)

    row = pl.BlockSpec((tm, d), lambda i: (i, 0))
    narrow = pl.BlockSpec((tm, c), lambda i: (i, 0))
    whole = lambda arr: pl.BlockSpec(arr.shape, lambda i: (0,) * arr.ndim)
    return _call(body, name=name, grid=(s // tm,),
                 in_specs=[narrow, narrow, pl.BlockSpec((tm, d), lambda i: (i, 3)), pl.BlockSpec((tm, d), lambda i: (i, 4)),
                           whole(w_a), whole(w_b), whole(w), row, whole(g)],
                 out_specs=[row] * 5,
                 out_shape=[jax.ShapeDtypeStruct((s, d), F32)] + [jax.ShapeDtypeStruct((s, d), BF16)] * 4,
                 scratch=[], args=[o_a, y_b, proj, proj, w_a, w_b, w, x, g], plan=plan)


def _mm_tn_a3(a3, dy, *, name):
    j, t, n = a3.shape
    nn = dy.shape[1]
    return _call(functools.partial(_mm_body, TN, False), name=name, grid=(j,),
                 in_specs=[pl.BlockSpec((None, t, n), lambda jj: (jj, 0, 0)), pl.BlockSpec((t, nn), lambda jj: (0, 0))],
                 out_specs=[pl.BlockSpec((None, n, nn), lambda jj: (jj, 0, 0))],
                 out_shape=[jax.ShapeDtypeStruct((j, n, nn), BF16)], scratch=[], args=[a3, dy])[0]


def _mm_nt(dy, w3, *, name, out_dtype=BF16, tm=512, tn=1024, plan=None):
    m = dy.shape[0]
    j, kk, n = w3.shape
    tm, tn = min(tm, m), min(tn, kk)
    return _call(
        functools.partial(_mm_nt_body, j, n), name=name,
        grid=(m // tm, kk // tn),
        in_specs=[pl.BlockSpec((tm, j * n), lambda i, q: (i, 0)),
                  pl.BlockSpec((j, tn, n), lambda i, q: (0, q, 0))],
        out_specs=[pl.BlockSpec((tm, tn), lambda i, q: (i, q))],
        out_shape=[jax.ShapeDtypeStruct((m, kk), out_dtype)], scratch=[], args=[dy, w3], plan=plan)[0]


def _mm_nt_gain(dy, w3, x, *, name, tm=512, plan=None):
    m, kk = x.shape
    j, _, n = w3.shape
    tm = min(tm, m)

    def body(dy_ref, w_ref, x_ref, o_ref, dg_ref):
        _mm_nt_body(j, n, dy_ref, w_ref, o_ref)
        xv = x_ref[...]
        xhat = xv * lax.rsqrt(jnp.mean(xv * xv, axis=-1, keepdims=True) + NORM_EPS)
        _acc_rows(pl.program_id(0), dg_ref, jnp.sum(o_ref[...] * xhat, axis=0, keepdims=True))

    row = pl.BlockSpec((tm, kk), lambda i: (i, 0))
    return _call(body, name=name, grid=(m // tm,),
                 in_specs=[pl.BlockSpec((tm, j * n), lambda i: (i, 0)), pl.BlockSpec(w3.shape, lambda i: (0, 0, 0)), row],
                 out_specs=[row, pl.BlockSpec((8, kk), lambda i: (0, 0))],
                 out_shape=[jax.ShapeDtypeStruct((m, kk), F32), jax.ShapeDtypeStruct((8, kk), F32)],
                 scratch=[], args=[dy, w3, x], plan=plan)


def _mm_tn(a, dy, n, *, name, out_dtype=BF16, tm=512, tn=None, k_tiles=None, plan=None):
    t, kk = a.shape
    j = dy.shape[1] // n
    tm, tn = min(tm, kk), n if tn is None else tn
    n_t = n // tn
    first, count = (0, kk // tm) if k_tiles is None else k_tiles
    return _call(
        functools.partial(_mm_body, TN, False), name=name,
        grid=(count, j * n_t),
        in_specs=[pl.BlockSpec((t, tm), lambda i, jj: (0, first + i)),
                  pl.BlockSpec((t, tn), lambda i, jj: (0, jj))],
        out_specs=[pl.BlockSpec((None, tm, tn), lambda i, jj: (jj // n_t, i, jj % n_t))],
        out_shape=[jax.ShapeDtypeStruct((j, count * tm, n), out_dtype)], scratch=[], args=[a, dy], plan=plan)[0]


def _rows(body, ins, outs, *, n_rows, tm, name, plan=None):
    tm = min(tm, n_rows)
    n_steps = n_rows // tm
    in_specs, args = [], []
    for arr, kind, width, block in ins:
        if kind == "row":
            in_specs.append(pl.BlockSpec((tm, width), functools.partial(lambda i, b: (i, b), b=block)))
        elif kind == "prev":
            in_specs.append(pl.BlockSpec((tm, width), functools.partial(lambda i, b: (jnp.maximum(i - 1, 0), b), b=block)))
        elif kind == "next":
            in_specs.append(pl.BlockSpec((tm, width), functools.partial(lambda i, b: (jnp.minimum(i + 1, n_steps - 1), b), b=block)))
        else:
            in_specs.append(pl.BlockSpec(arr.shape, functools.partial(lambda i, nd: (0,) * nd, nd=arr.ndim)))
        args.append(arr)
    out_specs, out_shape = [], []
    for shape, dtype, kind in outs:
        if kind == "row":
            out_specs.append(pl.BlockSpec((tm, shape[1]), lambda i: (i, 0)))
        else:
            out_specs.append(pl.BlockSpec(shape, functools.partial(lambda i, nd: (0,) * nd, nd=len(shape))))
        out_shape.append(jax.ShapeDtypeStruct(shape, dtype))

    def kern(*refs):
        body(pl.program_id(0), n_steps, *refs)

    return _call(kern, name=name, grid=(n_steps,), in_specs=in_specs, out_specs=out_specs, out_shape=out_shape,
                 scratch=[], args=args, plan=plan)


def _acc_rows(i, ref, value):
    @pl.when(i == 0)
    def _():
        ref[...] = jnp.zeros_like(ref)
    ref[...] += jnp.broadcast_to(value, ref.shape)


def _rms_fwd(x, g, *, name, tm=512):
    s, d = x.shape

    def body(i, n, x_ref, g_ref, h_ref):
        _rms_fwd_tail(x_ref[...], g_ref, h_ref)

    return _rows(body, [(x, "row", d, 0), (g, "full", 0, 0)], [((s, d), BF16, "row")], n_rows=s, tm=tm, name=name)[0]


def _rms_bwd(x, g, dh, dres, *, name, tm=512, plan=None):
    s, d = x.shape

    def body(i, n, x_ref, g_ref, dh_ref, dres_ref, dx_ref, dxb_ref, dg_ref):
        _rms_bwd_tail(i, dh_ref[...].astype(F32), x_ref, g_ref, dres_ref, dx_ref, dxb_ref, dg_ref)

    return _rows(body, [(x, "row", d, 0), (g, "full", 0, 0), (dh, "row", d, 0), (dres, "row", d, 0)],
                 [((s, d), F32, "row"), ((s, d), BF16, "row"), ((8, d), F32, "acc")],
                 n_rows=s, tm=tm, name=name, plan=plan)


def _mix_out_bwd(dx1b, w, br_a, br_b, proj, w_a, w_b, *, name, tm=512, plan=None):
    s, d = br_a.shape
    c = w_a.shape[0]
    tm = min(tm, s)

    def body(dy_ref, w_ref, a_ref, b_ref, ga_ref, gb_ref, wa_ref, wb_ref, da_ref, db_ref, dg_ref, doa_ref, dyb_ref):
        dm = _bdot(dy_ref[...], w_ref[...], NT)
        sa = _sigmoid(ga_ref[...].astype(F32))
        sb = _sigmoid(gb_ref[...].astype(F32))
        da_ref[...] = (dm * sa).astype(BF16)
        db_ref[...] = (dm * sb).astype(BF16)
        dg_ref[:, :d] = (dm * a_ref[...].astype(F32) * sa * (1.0 - sa)).astype(BF16)
        dg_ref[:, d:] = (dm * b_ref[...].astype(F32) * sb * (1.0 - sb)).astype(BF16)
        doa_ref[...] = _bdot(da_ref[...], wa_ref[...], NT).astype(BF16)
        dyb_ref[...] = _bdot(db_ref[...], wb_ref[...], NT).astype(BF16)

    row = pl.BlockSpec((tm, d), lambda i: (i, 0))
    narrow = pl.BlockSpec((tm, c), lambda i: (i, 0))
    whole = lambda arr: pl.BlockSpec(arr.shape, lambda i: (0,) * arr.ndim)
    return _call(body, name=name, grid=(s // tm,),
                 in_specs=[row, whole(w), row, row, pl.BlockSpec((tm, d), lambda i: (i, 3)),
                           pl.BlockSpec((tm, d), lambda i: (i, 4)), whole(w_a), whole(w_b)],
                 out_specs=[row, row, pl.BlockSpec((tm, 2 * d), lambda i: (i, 0)), narrow, narrow],
                 out_shape=[jax.ShapeDtypeStruct((s, d), BF16), jax.ShapeDtypeStruct((s, d), BF16),
                            jax.ShapeDtypeStruct((s, 2 * d), BF16), jax.ShapeDtypeStruct((s, c), BF16),
                            jax.ShapeDtypeStruct((s, c), BF16)],
                 scratch=[], args=[dx1b, w, br_a, br_b, proj, proj, w_a, w_b], plan=plan)


def _shift_down(cur, prev, k, first):
    row = lax.broadcasted_iota(jnp.int32, cur.shape, 0)
    out = jnp.where(row >= k, pltpu.roll(cur, k, 0), pltpu.roll(prev, k, 0))
    return jnp.where(jnp.logical_and(first, row < k), 0.0, out)


def _shift_up(cur, nxt, k, last):
    tm = cur.shape[0]
    row = lax.broadcasted_iota(jnp.int32, cur.shape, 0)
    out = jnp.where(row < tm - k, pltpu.roll(cur, tm - k, 0), pltpu.roll(nxt, tm - k, 0))
    return jnp.where(jnp.logical_and(last, row >= tm - k), 0.0, out)


def _conv_fwd(proj, conv_w, *, name, tm=512):
    s = proj.shape[0]
    c = CONV_WIDTH

    def body(i, n, u_ref, gb_ref, gc_ref, up_ref, gcp_ref, w_ref, y_ref):
        cu = gc_ref[...].astype(F32) * u_ref[...].astype(F32)
        cup = gcp_ref[...].astype(F32) * up_ref[...].astype(F32)
        first = i == 0
        y = (w_ref[0:1, :] * _shift_down(cu, cup, 2, first) + w_ref[1:2, :] * _shift_down(cu, cup, 1, first)
             + w_ref[2:3, :] * cu)
        y_ref[...] = (gb_ref[...].astype(F32) * y).astype(BF16)

    return _rows(body, [(proj, "row", c, 3), (proj, "row", c, 4), (proj, "row", c, 5),
                        (proj, "prev", c, 3), (proj, "prev", c, 5), (conv_w, "full", 0, 0)],
                 [((s, c), BF16, "row")], n_rows=s, tm=tm, name=name)[0]


def _conv_bwd(dy_b, proj, conv_w, *, name, tm=512, plan=None):
    s = proj.shape[0]
    c = CONV_WIDTH

    def body(i, n, dy_ref, u_ref, gb_ref, gc_ref, up_ref, gcp_ref, dyn_ref, gbn_ref, w_ref, d_ref, dw_ref):
        first, last = i == 0, i == n - 1
        u = u_ref[...].astype(F32)
        gb = gb_ref[...].astype(F32)
        gc = gc_ref[...].astype(F32)
        cu = gc * u
        cup = gcp_ref[...].astype(F32) * up_ref[...].astype(F32)
        cu1 = _shift_down(cu, cup, 1, first)
        cu2 = _shift_down(cu, cup, 2, first)
        conv = w_ref[0:1, :] * cu2 + w_ref[1:2, :] * cu1 + w_ref[2:3, :] * cu
        dy = dy_ref[...].astype(F32)
        dyc = dy * gb
        dycn = dyn_ref[...].astype(F32) * gbn_ref[...].astype(F32)
        dcu = (w_ref[2:3, :] * dyc + w_ref[1:2, :] * _shift_up(dyc, dycn, 1, last)
               + w_ref[0:1, :] * _shift_up(dyc, dycn, 2, last))
        d_ref[:, 0:c] = (dcu * gc).astype(BF16)
        d_ref[:, c:2 * c] = (dy * conv).astype(BF16)
        d_ref[:, 2 * c:3 * c] = (dcu * u).astype(BF16)
        row = lax.broadcasted_iota(jnp.int32, (8, c), 0)
        dw = (jnp.where(row == 0, jnp.sum(dyc * cu2, axis=0, keepdims=True), 0.0)
              + jnp.where(row == 1, jnp.sum(dyc * cu1, axis=0, keepdims=True), 0.0)
              + jnp.where(row == 2, jnp.sum(dyc * cu, axis=0, keepdims=True), 0.0))

        @pl.when(first)
        def _():
            dw_ref[...] = jnp.zeros_like(dw_ref)
        dw_ref[...] += dw

    return _rows(body, [(dy_b, "row", c, 0), (proj, "row", c, 3), (proj, "row", c, 4), (proj, "row", c, 5),
                        (proj, "prev", c, 3), (proj, "prev", c, 5), (dy_b, "next", c, 0), (proj, "next", c, 4),
                        (conv_w, "full", 0, 0)],
                 [((s, 3 * c), BF16, "row"), ((8, c), F32, "acc")], n_rows=s, tm=tm, name=name, plan=plan)


def _mem_probs(q, k, scale):
    sc = _bdot(q, k, NT) * scale
    sc = sc - jnp.max(sc, axis=-1, keepdims=True)
    p = jnp.exp(sc)
    return p / jnp.sum(p, axis=-1, keepdims=True)


def _mem_sublayer(hq, w_q, kv, w_o, x, g, *, name, tm=512, plan=None):
    s, d = hq.shape
    hd = d // MEM_HEADS
    scale = 1.0 / math.sqrt(hd)
    tm = min(tm, s)

    def body(hq_ref, wq_ref, kv_ref, wo_ref, x_ref, g_ref, q_ref, o_ref, x2_ref, h_ref):
        q_ref[...] = _bdot(hq_ref[...], wq_ref[...], NN).astype(BF16)
        for h in range(MEM_HEADS):
            cols = slice(h * hd, (h + 1) * hd)
            p = _mem_probs(q_ref[:, cols], kv_ref[:, cols], scale)
            o_ref[:, cols] = _bdot(p, kv_ref[:, d + h * hd:d + (h + 1) * hd], NN).astype(BF16)
        xv = _bdot(o_ref[...], wo_ref[...], NN) + x_ref[...]
        x2_ref[...] = xv
        _rms_fwd_tail(xv, g_ref, h_ref)

    row = pl.BlockSpec((tm, d), lambda i: (i, 0))
    whole = lambda a: pl.BlockSpec(a.shape, lambda i: (0,) * a.ndim)
    return _call(body, name=name, grid=(s // tm,),
                 in_specs=[row, whole(w_q), whole(kv), whole(w_o), row, whole(g)], out_specs=[row] * 4,
                 out_shape=[jax.ShapeDtypeStruct((s, d), BF16), jax.ShapeDtypeStruct((s, d), BF16),
                            jax.ShapeDtypeStruct((s, d), F32), jax.ShapeDtypeStruct((s, d), BF16)],
                 scratch=[], args=[hq, w_q, kv, w_o, x, g], plan=plan)


def _mem_sublayer_bwd(dx2b, dx2, x, g, qm, kv, w_q, w_o, *, name, tm=512, plan=None):
    s, d = qm.shape
    hd = d // MEM_HEADS
    scale = 1.0 / math.sqrt(hd)
    tm = min(tm, s)

    def body(dyb_ref, dres_ref, x_ref, g_ref, q_ref, kv_ref, wq_ref, wo_ref, dx_ref, dxb_ref, dg_ref, dq_ref, dkv_ref):
        i = pl.program_id(0)

        @pl.when(i == 0)
        def _():
            dkv_ref[...] = jnp.zeros_like(dkv_ref)
        dom = _bdot(dyb_ref[...], wo_ref[...], NT).astype(BF16)
        for h in range(MEM_HEADS):
            cols = slice(h * hd, (h + 1) * hd)
            vcols = slice(d + h * hd, d + (h + 1) * hd)
            q, k, v, do = q_ref[:, cols], kv_ref[:, cols], kv_ref[:, vcols], dom[:, cols]
            p = _mem_probs(q, k, scale)
            dp = _bdot(do, v, NT)
            ds = p * (dp - jnp.sum(dp * p, axis=-1, keepdims=True)) * scale
            dq_ref[:, cols] = _bdot(ds, k, NN).astype(BF16)
            dkv_ref[:, cols] += _bdot(ds, q, TN)
            dkv_ref[:, vcols] += _bdot(p, do, TN)
        dh = _bdot(dq_ref[...], wq_ref[...], NT)
        _rms_bwd_tail(i, dh, x_ref, g_ref, dres_ref, dx_ref, dxb_ref, dg_ref)

    row = pl.BlockSpec((tm, d), lambda i: (i, 0))
    whole = lambda a: pl.BlockSpec(a.shape, lambda i: (0,) * a.ndim)
    return _call(body, name=name, grid=(s // tm,),
                 in_specs=[row, row, row, whole(g), row, whole(kv), whole(w_q), whole(w_o)],
                 out_specs=[row, row, pl.BlockSpec((8, d), lambda i: (0, 0)), row, whole(kv)],
                 out_shape=[jax.ShapeDtypeStruct((s, d), F32), jax.ShapeDtypeStruct((s, d), BF16),
                            jax.ShapeDtypeStruct((8, d), F32), jax.ShapeDtypeStruct((s, d), BF16),
                            jax.ShapeDtypeStruct(kv.shape, F32)],
                 scratch=[], args=[dx2b, dx2, x, g, qm, kv, w_q, w_o], plan=plan)


def _sb_consts(t):
    row = lax.broadcasted_iota(jnp.int32, (t, t), 0)
    col = lax.broadcasted_iota(jnp.int32, (t, t), 1)
    lane = lax.broadcasted_iota(jnp.int32, (t, LANES), 1)
    return row, col, lane < SB_HEAD_DIM


def _sb_logits(q, k):
    z2 = jnp.minimum(_bdot(q, k, NT) * LOG2_E, SB_CLAMP)
    return z2, jnp.exp2(z2)


def _tri_sum(v, tri):
    hi = v.astype(BF16)
    lo = (v - hi.astype(F32)).astype(BF16)
    return _bdot(hi, tri, NN) + _bdot(lo, tri, NN)


def _sb_fwd(proj, *, name, plan=None):
    s = proj.shape[0]
    t, nh = SB_TILE, SB_STEP_HEADS
    n_q = s // t
    scale = 1.0 / math.sqrt(SB_HEAD_DIM)

    def body(q_ref, k_ref, v_ref, o_ref, c_ref, first_ref, acc_ref, c_scr):
        i = pl.program_id(1)
        row, col, head0 = _sb_consts(t)
        later = (row > col).astype(BF16)
        valid = col < row
        lanes = lambda h: slice((h // 2) * LANES, (h // 2 + 1) * LANES)
        q = [jnp.where(head0 == (h % 2 == 0), q_ref[:, lanes(h)] * scale, 0) for h in range(nh)]

        def tiles(kbs, diag_first, carry):
            rows = [pl.ds(pl.multiple_of(kb * t, t), t) for kb in kbs]
            jobs = [(n, h) for n in range(len(kbs)) for h in range(nh)]
            masked = lambda n: diag_first and n == 0
            zs = {(n, h): _sb_logits(q[h], k_ref[rows[n], lanes(h)]) for n, h in jobs}
            fail = {j: jnp.log2(1.0 + zs[j][1]) for j in jobs}
            fail = {j: jnp.where(valid, fail[j], 0.0) if masked(j[0]) else fail[j] for j in jobs}
            cum = {j: _tri_sum(fail[j], later) for j in jobs}
            run, before = list(carry), {}
            for n, h in jobs:
                before[n, h] = run[h]
                run[h] = run[h] + cum[n, h][:, 0:1] + fail[n, h][:, 0:1]
            w = {j: jnp.exp2(zs[j][0] - fail[j] - cum[j] - before[j]) for j in jobs}
            w = {j: jnp.where(valid, w[j], 0.0) if masked(j[0]) else w[j] for j in jobs}
            for n, h in jobs:
                acc_ref[h] += _bdot(w[n, h], v_ref[rows[n], lanes(h)], NN)
            return tuple(run)

        acc_ref[...] = jnp.zeros_like(acc_ref)
        zero = (jnp.zeros((t, 1), F32),) * nh

        def alive(carry):
            return (functools.reduce(jnp.minimum, [jnp.min(c) for c in carry]) < SB_DEAD).astype(jnp.int32)

        def step(state):
            new = tiles([state[0]], False, state[2:])
            return (state[0] - 1, alive(new)) + new

        @pl.when(i == 0)
        def _():
            for h, c in enumerate(tiles([i], True, zero)):
                c_scr[h] = c

        @pl.when(i > 0)
        def _():
            for h, c in enumerate(tiles([i, i - 1], True, zero)):
                c_scr[h] = c
        carry = tuple(c_scr[h] for h in range(nh))
        state = lax.while_loop(lambda st: jnp.logical_and(st[0] >= 0, st[1] > 0), step, (i - 2, alive(carry)) + carry)
        for b in range(nh // 2):
            o_ref[:, b * LANES:(b + 1) * LANES] = jnp.where(head0, acc_ref[2 * b], acc_ref[2 * b + 1]).astype(BF16)
        head = lax.broadcasted_iota(jnp.int32, (t, nh), 1)
        c_ref[...] = sum(jnp.where(head == h, state[2 + h], 0.0) for h in range(nh))
        first_ref[pl.program_id(0), i] = (jnp.maximum(state[0], -1) + 1).astype(F32)

    n_p, width = SB_HEADS // nh, nh * SB_HEAD_DIM
    k_blk, v_blk = SB_WIDTH // width, 2 * SB_WIDTH // width
    return _call(
        body, name=name, grid=(n_p, n_q),
        in_specs=[pl.BlockSpec((t, width), lambda p, i: (i, p)),
                  pl.BlockSpec((s, width), lambda p, i: (0, k_blk + p)),
                  pl.BlockSpec((s, width), lambda p, i: (0, v_blk + p))],
        out_specs=[pl.BlockSpec((t, width), lambda p, i: (i, p)),
                   pl.BlockSpec((None, t, nh), lambda p, i: (p, i, 0)),
                   pl.BlockSpec(memory_space=pltpu.SMEM)],
        out_shape=[jax.ShapeDtypeStruct((s, SB_WIDTH), BF16), jax.ShapeDtypeStruct((n_p, s, nh), F32),
                   jax.ShapeDtypeStruct((n_p, n_q), F32)],
        scratch=[pltpu.VMEM((nh, t, LANES), F32), pltpu.VMEM((nh, t, 1), F32)], args=[proj, proj, proj], plan=plan)


def _sb_bwd(proj, do_a, ctot, first, *, name, plan=None):
    s = proj.shape[0]
    t, nh = SB_TILE, SB_STEP_HEADS
    n_q = s // t
    scale = 1.0 / math.sqrt(SB_HEAD_DIM)

    def body(q_ref, k_ref, v_ref, do_ref, c_ref, first_ref, dq_ref, dk_ref, dv_ref, dq_acc, dk_acc, dv_acc):
        i = pl.program_id(1)
        kb0 = jnp.clip(first_ref[pl.program_id(0), i].astype(jnp.int32), 0, i)
        row, col, head0 = _sb_consts(t)
        upto = (row <= col).astype(BF16)
        before = (row < col).astype(BF16)
        valid = col < row
        lanes = lambda h: slice((h // 2) * LANES, (h // 2 + 1) * LANES)
        q2 = [jnp.where(head0 == (h % 2 == 0), q_ref[:, lanes(h)] * scale, 0) for h in range(nh)]
        do2 = [jnp.where(head0 == (h % 2 == 0), do_ref[:, lanes(h)], 0) for h in range(nh)]
        ctot2 = [c_ref[:, h:h + 1] for h in range(nh)]

        @pl.when(i == 0)
        def _():
            dk_acc[...] = jnp.zeros_like(dk_acc)
            dv_acc[...] = jnp.zeros_like(dv_acc)
        dq_acc[...] = jnp.zeros_like(dq_acc)

        def tiles(kbs, diag_last, carry):
            rows = [pl.ds(pl.multiple_of(kb * t, t), t) for kb in kbs]
            kt = {(n, h): k_ref[rows[n], lanes(h)] for n in range(len(kbs)) for h in range(nh)}
            jobs = list(kt)
            masked = lambda n: diag_last and n == len(kbs) - 1
            t_last = slice(t - 1, t)
            zs = {(n, h): _sb_logits(q2[h], kt[n, h]) for n, h in jobs}
            dw = {(n, h): _bdot(do2[h], v_ref[rows[n], lanes(h)], NT) for n, h in jobs}
            fail = {j: jnp.log2(1.0 + zs[j][1]) for j in jobs}
            fail = {j: jnp.where(valid, fail[j], 0.0) if masked(j[0]) else fail[j] for j in jobs}
            cum = {j: _tri_sum(fail[j], upto) for j in jobs}
            miss = {j: jnp.exp2(-fail[j]) for j in jobs}
            beta = {j: zs[j][1] * miss[j] for j in jobs}
            fail_run, fail_before = list(carry[0::2]), {}
            for n, h in jobs:
                fail_before[n, h] = fail_run[h]
                fail_run[h] = fail_run[h] + cum[n, h][:, t_last]
            w = {(n, h): beta[n, h] * jnp.exp2(fail_before[n, h] + cum[n, h] - ctot2[h]) for n, h in jobs}
            w = {j: jnp.where(valid, w[j], 0.0) if masked(j[0]) else w[j] for j in jobs}
            g = {j: w[j] * dw[j] for j in jobs}
            g_local = {j: _bdot(g[j], before, NN) for j in jobs}
            for n, h in jobs:
                dv_acc[rows[n], lanes(h)] += _bdot(w[n, h], do2[h], TN)
            g_run, dz = list(carry[1::2]), {}
            for n, h in jobs:
                g_sum = g_run[h] + g_local[n, h]
                dz[n, h] = g[n, h] * miss[n, h] - beta[n, h] * g_sum
                g_run[h] = g_sum[:, t_last] + g[n, h][:, t_last]
            dz = {j: jnp.where(valid, dz[j], 0.0) if masked(j[0]) else dz[j] for j in jobs}
            for n, h in jobs:
                dq_acc[h] += _bdot(dz[n, h], kt[n, h], NN)
                dk_acc[rows[n], lanes(h)] += _bdot(dz[n, h], q2[h], TN)
            return tuple(v for pair in zip(fail_run, g_run) for v in pair)

        zero = jnp.zeros((t, 1), F32)
        carry = lax.fori_loop(kb0, i - 1, lambda n, c: tiles([n], False, c), (zero,) * (2 * nh))

        @pl.when(i == 0)
        def _():
            tiles([i], True, carry)

        @pl.when(i > 0)
        def _():
            tiles([i - 1, i], True, carry)
        for b in range(nh // 2):
            dq_ref[:, b * LANES:(b + 1) * LANES] = (jnp.where(head0, dq_acc[2 * b], dq_acc[2 * b + 1])
                                                    * scale).astype(BF16)

        @pl.when(i == n_q - 1)
        def _():
            dk_ref[...] = dk_acc[...].astype(BF16)
            dv_ref[...] = dv_acc[...].astype(BF16)

    n_p, width = SB_HEADS // nh, nh * SB_HEAD_DIM
    k_blk, v_blk = SB_WIDTH // width, 2 * SB_WIDTH // width
    outs = _call(
        body, name=name, grid=(n_p, n_q),
        in_specs=[pl.BlockSpec((t, width), lambda p, i: (i, p)),
                  pl.BlockSpec((s, width), lambda p, i: (0, k_blk + p)),
                  pl.BlockSpec((s, width), lambda p, i: (0, v_blk + p)),
                  pl.BlockSpec((t, width), lambda p, i: (i, p)),
                  pl.BlockSpec((None, t, nh), lambda p, i: (p, i, 0)),
                  pl.BlockSpec(memory_space=pltpu.SMEM)],
        out_specs=[pl.BlockSpec((t, width), lambda p, i: (i, p)),
                   pl.BlockSpec((s, width), lambda p, i: (0, p)),
                   pl.BlockSpec((s, width), lambda p, i: (0, p))],
        out_shape=[jax.ShapeDtypeStruct((s, SB_WIDTH), BF16)] * 3,
        scratch=[pltpu.VMEM((nh, t, LANES), F32), pltpu.VMEM((s, width), F32), pltpu.VMEM((s, width), F32)],
        args=[proj, proj, proj, do_a, ctot, first], plan=plan)
    return jnp.concatenate(outs, axis=1)


def _mm_gathered(a, key, plan, *, name, out3=False, w_t=False):
    src = plan.gathering(key)
    if src is None:
        return _mm_nn(a, plan.weight(key), name=name, out3=out3, w_t=w_t, plan=plan)
    out, w_all = _mm_gathering(a, src, name=name, out3=out3, w_t=w_t)
    plan.set_weight(key, w_all)
    return out


def _local_step(x, mem, target, gains, plan):
    g_mix, g_memq, g_memkv, g_ffn, g_fin = gains
    d = x.shape[1]

    h0 = _rms_fwd(x, g_mix, name="rms_mix")
    proj = _mm_gathered(h0, "in", plan, name="mm_in")
    w_in = plan.weight("in")
    o_a, ctot, first = _sb_fwd(proj, name="sb_fwd", plan=plan)
    conv_w = plan.weight("conv")
    y_b = _conv_fwd(proj, conv_w, name="conv_fwd")
    w_a, w_b, w_mix = plan.weight("a"), plan.weight("b"), plan.weight("mix")
    x1, hq, merged, br_a, br_b = _mix_out(o_a, y_b, proj, w_a[0], w_b[0], w_mix[0], x, g_memq, name="mm_mix", plan=plan)
    w_mq, w_kv, w_mo = plan.weight("mq")[0], plan.weight("kv"), plan.weight("mo")[0]
    mn = _rms_fwd(mem, g_memkv, name="rms_memkv")
    kv = _mm_nn(mn, w_kv, name="mm_memkv")
    qm, om, x2, hf = _mem_sublayer(hq, w_mq, kv, w_mo, x1, g_ffn, name="mem_sublayer", plan=plan)
    gu = _mm_gathered(hf, "fi", plan, name="mm_ffn_in", out3=True, w_t=True)
    w_fi, w_fo = plan.weight("fi"), plan.weight("fo")
    dx3, dx3b, dg_fin, loss, act = _ffn_out_loss(gu, w_fo, x2, g_fin, target, name="mm_ffn_out")

    plan.grad("fo", _mm_tn_a3(act, dx3b, name="mm_d_w_ffn_out"))
    dgu = _ffn_out_bwd(dx3b, w_fo, gu, name="mm_d_act")
    plan.grad("fi", _mm_tn_a3(dgu, hf, name="mm_d_w_ffn_in"))
    dx2, dx2b, dg_ffn = _mm_nt_rms(dgu, w_fi, x2, g_ffn, dx3, name="mm_d_hf", dy3=True, w_nn=True, plan=plan)

    plan.grad("mo", _mm_tn(om, dx2b, d, name="mm_d_w_memo"))
    dx1, dx1b, dg_memq, dqm, dkv = _mem_sublayer_bwd(dx2b, dx2, x1, g_memq, qm, kv, w_mq, w_mo, name="mem_sublayer_bwd",
                                                    plan=plan)
    plan.grad("mq", _mm_tn(hq, dqm, d, name="mm_d_w_memq"))
    plan.grad("kv", _mm_tn(mn, dkv, w_kv.shape[2], name="mm_d_w_memkv"))
    _, _, dg_memkv = _mm_nt_rms(dkv, w_kv, mem, g_memkv, None, name="mm_d_mn")

    plan.grad("mix", _mm_tn(merged, dx1b, d, name="mm_d_w_mix"))
    dbr_a, dbr_b, dgab, do_a, dy_b = _mix_out_bwd(dx1b, w_mix[0], br_a, br_b, proj, w_a[0], w_b[0], name="mm_d_merged",
                                                 plan=plan)
    plan.grad("a", _mm_tn(o_a, dbr_a, d, name="mm_d_w_branch_a"))
    plan.grad("b", _mm_tn(y_b, dbr_b, d, name="mm_d_w_branch_b"))
    dconv, dconv_w = _conv_bwd(dy_b, proj, conv_w, name="conv_bwd", plan=plan)
    dqkv = _sb_bwd(proj, do_a, ctot, first, name="sb_bwd", plan=plan)
    dproj = jnp.concatenate([dqkv, dconv, dgab], axis=1)
    rows_in1 = d // IN_SPLIT[1] * (IN_SPLIT[1] - IN_SPLIT[0])
    plan.grad("in0", _mm_tn(h0, dproj, w_in.shape[2], name="mm_d_w_in0", tm=d - rows_in1, k_tiles=(0, 1)))
    plan.grad("in1", _mm_tn(h0, dproj, w_in.shape[2], name="mm_d_w_in1", tm=rows_in1,
                            k_tiles=(d // rows_in1 - 1, 1), plan=plan))
    dh0, dg_mix = _mm_nt_gain(dproj, w_in, x, name="mm_d_h0", plan=plan)
    conv_rows = jnp.zeros((3, d), F32).at[:, :CONV_WIDTH].set(dconv_w[:3])
    plan.offer("small", jnp.concatenate(
        [dg_mix[:1], dg_memq[:1], dg_memkv[:1], dg_ffn[:1], dg_fin[:1], conv_rows,
         jnp.broadcast_to(loss[:1, :1], (1, d)), jnp.zeros((SMALL_ROWS - 9, d), F32)], axis=0))
    dx0, _, _ = _rms_bwd(x, g_mix, dh0, dx1, name="rms_mix_bwd", plan=plan)
    return dx0


def _row_tile(a, target=512):
    tm = min(a, target)
    while a % tm:
        tm -= 8
    return tm


def _sum_with_sibling(parts, recvs, core, *, name):
    n = len(parts)

    def body(core_ref, *refs):
        for p_ref, r_ref, o_ref in zip(refs[:n], refs[n:2 * n], refs[2 * n:]):
            o_ref[...] = (p_ref[...].astype(F32) + r_ref[...].astype(F32)).astype(o_ref.dtype)

    mine = [pl.BlockSpec((None,) + p.shape[1:], lambda q, core_ref: (2 * q + core_ref[0], 0, 0)) for p in parts]
    other = [pl.BlockSpec((None,) + p.shape[1:], lambda q, core_ref: (q, 0, 0)) for p in parts]
    return pl.pallas_call(
        body, name=name,
        grid_spec=pltpu.PrefetchScalarGridSpec(num_scalar_prefetch=1, grid=(N_CHIP,), in_specs=mine + other,
                                               out_specs=other),
        out_shape=[jax.ShapeDtypeStruct((N_CHIP,) + p.shape[1:], p.dtype) for p in parts],
        compiler_params=_params(1))(core, *parts, *recvs)


def _adam_math(wv, g, m, v):
    m = ADAM_B1 * m + (1.0 - ADAM_B1) * g
    v = ADAM_B2 * v + (1.0 - ADAM_B2) * (g * g)
    m_hat = m / (1.0 - ADAM_B1 ** ADAM_STEP)
    v_hat = v / (1.0 - ADAM_B2 ** ADAM_STEP)
    delta = -ADAM_LR * (m_hat / (jnp.sqrt(v_hat) + ADAM_EPS) + ADAM_WD * wv)
    return delta, m, v


def _adam_sharded(ws, ms, vs, owns, recvs, chip, *, name):
    n = len(ws)
    a, b = ws[0].shape
    tm = _row_tile(a)

    def body(chip_ref, *refs):
        ins, outs = refs[:5 * n], refs[5 * n:]
        for k in range(n):
            w_ref, m_ref, v_ref, own_ref, recv_ref = ins[k::n]
            g = own_ref[...].astype(F32)
            for j in range(3):
                g = g + recv_ref[j].astype(F32)
            delta, nm, nv = _adam_math(w_ref[...], g, m_ref[...], v_ref[...])
            for o_ref, value in zip(outs[k::n], (g, delta, nm, nv)):
                o_ref[...] = value

    tile = pl.BlockSpec((tm, b), lambda i, chip_ref: (i, 0))
    res = pl.pallas_call(
        body, name=name,
        grid_spec=pltpu.PrefetchScalarGridSpec(
            num_scalar_prefetch=1, grid=(a // tm,),
            in_specs=[tile] * (3 * n) + [pl.BlockSpec((None, tm, b), lambda i, chip_ref: (chip_ref[0], i, 0))] * n
            + [pl.BlockSpec((3, tm, b), lambda i, chip_ref: (0, i, 0))] * n,
            out_specs=[tile] * (4 * n)),
        out_shape=[jax.ShapeDtypeStruct((a, b), F32)] * (4 * n),
        compiler_params=_params(1))(chip, *ws, *ms, *vs, *owns, *recvs)
    return [res[k::n] for k in range(n)]


def _sum_devices(gathered, *, name):
    _, r, c = gathered.shape

    def body(g_ref, o_ref):
        total = g_ref[0]
        for j in range(1, N_DEV):
            total = total + g_ref[j]
        o_ref[...] = total

    return pl.pallas_call(body, name=name, out_shape=jax.ShapeDtypeStruct((r, c), F32))(gathered)


def _adam_small(ws, gs, ms, vs, *, name):
    n = len(ws)

    def body(*refs):
        ins, outs = refs[:4 * n], refs[4 * n:]
        for k in range(n):
            w_ref, g_ref, m_ref, v_ref = ins[k::n]
            for o_ref, value in zip(outs[k::n], _adam_math(w_ref[...], g_ref[...], m_ref[...], v_ref[...])):
                o_ref[...] = value

    res = pl.pallas_call(body, name=name, out_shape=[jax.ShapeDtypeStruct(w.shape, F32) for w in ws] * 3)(
        *ws, *gs, *ms, *vs)
    return [res[k::n] for k in range(n)]


BIG = ("in", "a", "b", "mix", "mq", "kv", "mo", "fi", "fo")
ROW_SHARDED = ("mix", "mq", "mo")
UNSHARDED = ("a", "b")
FFN_GROUPS = 4
IN_SPLIT = (3, 4)
SMALL_ROWS = 16


class _Plan:
    FUSED = ("in",)
    GATHER_ON = {"sb_fwd": ("a", "b", "mix", "mq", "mo", "conv", "fi0"), "mm_mix": ("kv",), "mem_sublayer": ("fi1",),
                 "mm_ffn_in": ("fo",), "rms_mix_bwd": ("small",)}
    SIBLING_ON = {"mm_d_hf": ("fo", "fi"), "mm_d_merged": ("mo", "mq", "kv"), "conv_bwd": ("mix", "a", "b"),
                  "mm_d_w_in1": ("in0",), "mm_d_h0": ("in1",)}
    LATE_AT = {"mm_mix": (7, 8), "mm_ffn_in": (7, 8)}
    CHIPS_ON = {"mem_sublayer_bwd": ("fo",), "sb_bwd": ("fi", "mo", "mq", "kv", "mix", "a", "b"), "mm_d_h0": ("in0",),
                "rms_mix_bwd": ("in1",)}

    def __init__(self, shards, core):
        self.shards, self.core = shards, core
        self.w, self.parts, self.chip_sums, self.from_chips = {}, {}, {}, {}

    def gathering(self, k):
        return self.shards[k] if k in self.FUSED else None

    def offer(self, k, block):
        self.shards[k] = block

    def comm(self, name):
        comms = []
        if name in self.GATHER_ON:
            comms.append(_gather_comm([self.shards[k] for k in self.GATHER_ON[name]]))
        if name in self.SIBLING_ON:
            comms.append(_sibling_comm([self.parts[k] for k in self.SIBLING_ON[name]]))
        if name in self.CHIPS_ON:
            comms.append(_chips_comm([self.chip_sums[k] for k in self.CHIPS_ON[name]]))
        if not comms:
            return None
        comm = _join_comms(comms)
        comm.late_at = self.LATE_AT.get(name, comm.late_at)
        return comm

    def landed(self, name, outs):
        outs = list(outs)
        for k in self.GATHER_ON.get(name, ()):
            self.set_weight(k, outs.pop(0))
        keys = self.SIBLING_ON.get(name, ())
        if keys:
            sums = _sum_with_sibling([self.parts[k] for k in keys], [outs.pop(0) for _ in keys], self.core,
                                     name="sum_with_sibling_" + "_".join(keys))
            self.chip_sums.update(zip(keys, sums))
        for k in self.CHIPS_ON.get(name, ()):
            self.from_chips[k] = outs.pop(0)

    def set_weight(self, k, gathered):
        _, a, b = gathered.shape
        if k in ROW_SHARDED:
            gathered = gathered.reshape(1, N_DEV * a, b)
        elif k in UNSHARDED:
            gathered = jnp.transpose(gathered, (1, 0, 2)).reshape(1, a, N_DEV * b)
        elif k == "fo":
            gathered = gathered.reshape(FFN_GROUPS, N_DEV * a // FFN_GROUPS, b)
        elif k == "conv":
            n_conv = CONV_WIDTH // N_DEV
            gathered = jnp.transpose(gathered[:, :3, :n_conv], (1, 0, 2)).reshape(3, CONV_WIDTH)
        self.w[k] = gathered
        if k == "fi1":
            self.w["fi"] = jnp.concatenate([self.w["fi0"], gathered], axis=2)

    def weight(self, k):
        return self.w[k]

    def grad(self, k, g):
        _, a, b = g.shape
        if k in ROW_SHARDED:
            g = g.reshape(N_DEV, a // N_DEV, b)
        elif k in UNSHARDED:
            g = jnp.transpose(g.reshape(a, N_DEV, b // N_DEV), (1, 0, 2))
        elif k == "fo":
            g = g.reshape(N_DEV, FFN_GROUPS * a // N_DEV, b)
        self.parts[k] = g


def kernel(x, mem, norm_mix, w_in, conv_w, w_branch_a, w_branch_b, w_mix_out, norm_mem_q, norm_mem_kv, w_mem_q, w_mem_kv, w_mem_o, norm_ffn, w_ffn_in, w_ffn_out, norm_final, loss_target, m_norm_mix, m_w_in, m_conv_w, m_w_branch_a, m_w_branch_b, m_w_mix_out, m_norm_mem_q, m_norm_mem_kv, m_w_mem_q, m_w_mem_kv, m_w_mem_o, m_norm_ffn, m_w_ffn_in, m_w_ffn_out, m_norm_final, v_norm_mix, v_w_in, v_conv_w, v_w_branch_a, v_w_branch_b, v_w_mix_out, v_norm_mem_q, v_norm_mem_kv, v_w_mem_q, v_w_mem_kv, v_w_mem_o, v_norm_ffn, v_w_ffn_in, v_w_ffn_out, v_norm_final):
    d = x.shape[-1]
    xi, yi, ci = lax.axis_index("x"), lax.axis_index("y"), lax.axis_index("c")
    chip = jnp.reshape(2 * xi + yi, (1,)).astype(jnp.int32)
    dev = 4 * xi + 2 * yi + ci

    big_w = dict(zip(BIG, (w_in, w_branch_a, w_branch_b, w_mix_out, w_mem_q, w_mem_kv, w_mem_o, w_ffn_in, w_ffn_out)))
    big_m = dict(zip(BIG, (m_w_in, m_w_branch_a, m_w_branch_b, m_w_mix_out, m_w_mem_q, m_w_mem_kv, m_w_mem_o, m_w_ffn_in, m_w_ffn_out)))
    big_v = dict(zip(BIG, (v_w_in, v_w_branch_a, v_w_branch_b, v_w_mix_out, v_w_mem_q, v_w_mem_kv, v_w_mem_o, v_w_ffn_in, v_w_ffn_out)))

    flip = lambda t, k: jnp.transpose(t) if k == "fi" else t
    shards = {k: flip(big_w[k][0], k).astype(BF16) for k in BIG}
    shards["fi0"], shards["fi1"] = shards["fi"][:, :d // 2], shards["fi"][:, d // 2:]
    n_conv = conv_w.shape[-1]
    shards["conv"] = jnp.zeros((8, LANES), F32).at[:3, :n_conv].set(conv_w[0])
    plan = _Plan(shards, jnp.reshape(ci, (1,)).astype(jnp.int32))

    gains = (norm_mix, norm_mem_q, norm_mem_kv, norm_ffn, norm_final.reshape(1, d))
    dx0 = _local_step(x[0], mem[0], loss_target[0], gains, plan)

    grads, deltas, new_m, new_v = {}, {}, {}, {}
    def adam(keys, rows=None, part=None):
        cut = (lambda t: t) if rows is None else (lambda t: t[rows])
        return _adam_sharded(*[[cut(flip(src[k][0], k)) for k in keys] for src in (big_w, big_m, big_v)],
                             [plan.chip_sums[part or k] for k in keys], [plan.from_chips[part or k] for k in keys],
                             chip, name="adam_" + "_".join(keys) + (part or "")[2:])

    results = {}
    for keys in (("a", "b"), ("mix", "mq", "mo"), ("kv",), ("fi",), ("fo",)):
        results.update(zip(keys, adam(keys)))
    half = big_w["in"].shape[1] * IN_SPLIT[0] // IN_SPLIT[1]
    lo, hi = adam(("in",), slice(0, half), "in0")[0], adam(("in",), slice(half, None), "in1")[0]
    results["in"] = [jnp.concatenate(pair, axis=0) for pair in zip(lo, hi)]
    for k in BIG:
        grads[k], deltas[k], new_m[k], new_v[k] = (flip(t, k).reshape(big_w[k].shape) for t in results[k])

    total = _sum_devices(plan.weight("small"), name="sum_small")
    g_conv = lax.dynamic_slice(total[5:8, :CONV_WIDTH], (0, dev * n_conv), (3, n_conv))
    small_w = [norm_mix, norm_mem_q, norm_mem_kv, norm_ffn, norm_final.reshape(1, d), conv_w[0]]
    small_m = [m_norm_mix, m_norm_mem_q, m_norm_mem_kv, m_norm_ffn, m_norm_final.reshape(1, d), m_conv_w[0]]
    small_v = [v_norm_mix, v_norm_mem_q, v_norm_mem_kv, v_norm_ffn, v_norm_final.reshape(1, d), v_conv_w[0]]
    small_g = [total[0:1], total[1:2], total[2:3], total[3:4], total[4:5], g_conv]
    small_names = ["norm_mix", "norm_mem_q", "norm_mem_kv", "norm_ffn", "norm_final", "conv_w"]
    sg, sd, sm, sv = {}, {}, {}, {}
    small_out = _adam_small(small_w, small_g, small_m, small_v, name="adam_small")
    for nme, wv, g, (dl, nm, nv) in zip(small_names, small_w, small_g, small_out):
        shape = norm_final.shape if nme == "norm_final" else (conv_w.shape if nme == "conv_w" else wv.shape)
        sg[nme], sd[nme], sm[nme], sv[nme] = (t.reshape(shape) for t in (g, dl, nm, nv))

    def ordered(big, sml):
        return (sml["norm_mix"], big["in"], sml["conv_w"], big["a"], big["b"], big["mix"], sml["norm_mem_q"],
                sml["norm_mem_kv"], big["mq"], big["kv"], big["mo"], sml["norm_ffn"], big["fi"], big["fo"],
                sml["norm_final"])

    loss_out = total[8, 0]
    grad_x = dx0.reshape(x.shape)
    return (loss_out, grad_x, *ordered(grads, sg), *ordered(deltas, sd), *ordered(new_m, sm), *ordered(new_v, sv))
```

```python
import functools
import math

import jax
import jax.numpy as jnp
from jax import lax
from jax.experimental import pallas as pl
from jax.experimental.pallas import tpu as pltpu

F32 = jnp.float32
BF16 = jnp.bfloat16
MESH = pl.DeviceIdType.MESH

N_DEV = 8
N_CHIP = 4
NORM_EPS = 1e-6
SB_HEADS = 8
SB_HEAD_DIM = 64
SB_WIDTH = SB_HEADS * SB_HEAD_DIM
CONV_WIDTH = 512
MEM_HEADS = 4
ADAM_LR = 0.001
ADAM_B1 = 0.9
ADAM_B2 = 0.999
ADAM_EPS = 1e-08
ADAM_WD = 0.01
ADAM_STEP = 10

LANES = 128
VMEM_LIMIT_BYTES = 52 * 1024 * 1024
SB_TILE = 256
SB_STEP_HEADS = 4
SB_DEAD = 159.0
SB_CLAMP = 126.0
LOG2_E = 1.4426950408889634

ANY = pl.BlockSpec(memory_space=pl.ANY)


def _params(n_grid):
    return pltpu.CompilerParams(dimension_semantics=("arbitrary",) * n_grid, vmem_limit_bytes=VMEM_LIMIT_BYTES)


def _bdot(a, b, dims):
    return lax.dot_general(a.astype(BF16), b.astype(BF16), (dims, ((), ())), preferred_element_type=F32)


NN = ((1,), (0,))
NT = ((1,), (1,))
TN = ((0,), (0,))


class _Comm:
    def __init__(self, ins, outs, n_sems, start, finish, late=None):
        self.ins, self.outs, self.n_sems, self.start, self.finish = ins, outs, n_sems, start, finish
        self.late = late if late is not None else (lambda ins, outs, sems: None)
        self.late_at = (1, 1)

    def sem_shapes(self):
        return [pltpu.SemaphoreType.DMA((k,)) for k in self.n_sems]


def _place():
    return lax.axis_index("x"), lax.axis_index("y"), lax.axis_index("c")


def _neighbours(x, y, c):
    return [(jnp.bitwise_xor(x, c), jnp.bitwise_xor(y, 1 - c)), (jnp.bitwise_xor(x, 1 - c), jnp.bitwise_xor(y, c)),
            (1 - x, 1 - y)]


def _gather_comm(shards):
    n = len(shards)

    def copies(ins, outs, sems):
        send_sems, recv_sems, _ = sems
        x, y, c = _place()
        chips = [(1 - x, y), (x, 1 - y), (1 - x, 1 - y)]

        def copy(a, k, block, to, from_shard=False):
            dst = outs[a].at[4 * block[0] + 2 * block[1] + block[2]]
            return pltpu.make_async_remote_copy(
                src_ref=ins[a] if from_shard else dst, dst_ref=dst, send_sem=send_sems.at[a * 7 + k],
                recv_sem=recv_sems.at[a * 7 + k], device_id=to, device_id_type=MESH)

        me, sibling = (x, y, c), (x, y, 1 - c)
        own = [[copy(a, 0, me, sibling, True)] + [copy(a, 1 + j, me, (*chip, c), True) for j, chip in enumerate(chips)]
               for a in range(n)]
        landed = [[copy(a, 1 + j, (*chip, c), me) for j, chip in enumerate(chips)] for a in range(n)]
        passed = [[copy(a, 4 + j, (*chip, c), sibling) for j, chip in enumerate(chips)] for a in range(n)]
        from_sibling = [[copy(a, 0, sibling, me)] + [copy(a, 4 + j, (*chip, 1 - c), me) for j, chip in enumerate(chips)]
                        for a in range(n)]
        local = [pltpu.make_async_copy(ins[a], outs[a].at[4 * x + 2 * y + c], sems[2].at[a]) for a in range(n)]
        return own, landed, passed, from_sibling, local

    def start(ins, outs, sems):
        own, _, _, _, local = copies(ins, outs, sems)
        for a in range(n):
            local[a].start()
            for cp in own[a]:
                cp.start()

    def late(ins, outs, sems):
        _, landed, passed, _, _ = copies(ins, outs, sems)
        for a in range(n):
            for arrived, onward in zip(landed[a], passed[a]):
                arrived.wait_recv()
                onward.start()

    def finish(ins, outs, sems):
        own, _, passed, from_sibling, local = copies(ins, outs, sems)
        for a in range(n):
            for cp in from_sibling[a]:
                cp.wait_recv()
        for a in range(n):
            for cp in own[a] + passed[a]:
                cp.wait_send()
            local[a].wait()

    outs = [jax.ShapeDtypeStruct((N_DEV,) + s.shape, s.dtype) for s in shards]
    return _Comm(list(shards), outs, (7 * n, 7 * n, n), start, finish, late)


def _sibling_comm(parts):
    n = len(parts)

    def copies(ins, outs, sems):
        x, y, c = _place()
        return [pltpu.make_async_remote_copy(
            src_ref=ins[a].at[2 * q + 1 - c], dst_ref=outs[a].at[q], send_sem=sems[0].at[a * N_CHIP + q],
            recv_sem=sems[1].at[a * N_CHIP + q], device_id=(x, y, 1 - c), device_id_type=MESH)
            for a in range(n) for q in range(N_CHIP)]

    def start(ins, outs, sems):
        for cp in copies(ins, outs, sems):
            cp.start()

    def finish(ins, outs, sems):
        cps = copies(ins, outs, sems)
        for cp in cps:
            cp.wait_recv()
        for cp in cps:
            cp.wait_send()

    outs = [jax.ShapeDtypeStruct((N_CHIP,) + p.shape[1:], p.dtype) for p in parts]
    return _Comm(list(parts), outs, (N_CHIP * n, N_CHIP * n), start, finish)


def _chips_comm(parts):
    n = len(parts)

    def copies(ins, outs, sems):
        x, y, c = _place()
        chips = [(1 - x, y), (x, 1 - y), (1 - x, 1 - y)]
        return [pltpu.make_async_remote_copy(
            src_ref=ins[a].at[2 * px + py], dst_ref=outs[a].at[j], send_sem=sems[0].at[a * 3 + j],
            recv_sem=sems[1].at[a * 3 + j], device_id=(px, py, c), device_id_type=MESH)
            for a in range(n) for j, (px, py) in enumerate(chips)]

    def start(ins, outs, sems):
        for cp in copies(ins, outs, sems):
            cp.start()

    def finish(ins, outs, sems):
        cps = copies(ins, outs, sems)
        for cp in cps:
            cp.wait_recv()
        for cp in cps:
            cp.wait_send()

    outs = [jax.ShapeDtypeStruct((3,) + p.shape[1:], p.dtype) for p in parts]
    return _Comm(list(parts), outs, (3 * n, 3 * n), start, finish)


def _join_comms(comms):
    if len(comms) == 1:
        return comms[0]

    def split(refs, counts):
        out, at = [], 0
        for n in counts:
            out.append(refs[at:at + n])
            at += n
        return out

    def each(method):
        def run(ins, outs, sems):
            parts = zip(comms, split(ins, [len(c.ins) for c in comms]), split(outs, [len(c.outs) for c in comms]),
                        split(sems, [len(c.n_sems) for c in comms]))
            for c, c_ins, c_outs, c_sems in parts:
                getattr(c, method)(c_ins, c_outs, c_sems)
        return run

    return _Comm([a for c in comms for a in c.ins], [o for c in comms for o in c.outs],
                 tuple(k for c in comms for k in c.n_sems), each("start"), each("finish"), each("late"))


def _exchange(comm, *, name):
    n_ci, n_co = len(comm.ins), len(comm.outs)

    def kern(*refs):
        c_ins, c_outs, sems = refs[:n_ci], refs[n_ci:n_ci + n_co], refs[n_ci + n_co:]
        comm.start(c_ins, c_outs, sems)
        comm.late(c_ins, c_outs, sems)
        comm.finish(c_ins, c_outs, sems)

    return pl.pallas_call(kern, name=name, in_specs=[ANY] * n_ci, out_specs=[ANY] * n_co, out_shape=comm.outs,
                          scratch_shapes=comm.sem_shapes())(*comm.ins)


def _call(body, *, name, grid, in_specs, out_specs, out_shape, scratch, args, plan=None):
    comm = plan.comm(name) if plan is not None else None
    if comm is None:
        return list(pl.pallas_call(functools.partial(body), name=name, grid=grid, in_specs=in_specs,
                                   out_specs=out_specs, out_shape=out_shape, scratch_shapes=scratch,
                                   compiler_params=_params(len(grid)))(*args))
    n_in, n_out, n_scr, n_ci, n_co = len(in_specs), len(out_specs), len(scratch), len(comm.ins), len(comm.outs)

    def kern(*refs):
        ins, c_ins, refs = refs[:n_in], refs[n_in:n_in + n_ci], refs[n_in + n_ci:]
        outs, c_outs, refs = refs[:n_out], refs[n_out:n_out + n_co], refs[n_out + n_co:]
        scr, sems = refs[:n_scr], refs[n_scr:]
        ids = [pl.program_id(ax) for ax in range(len(grid))]
        step = functools.reduce(lambda at, ig: at * ig[1] + ig[0], zip(ids, grid), 0)
        n_steps = math.prod(grid)

        @pl.when(step == 0)
        def _():
            comm.start(c_ins, c_outs, sems)

        @pl.when(step == min(n_steps * comm.late_at[0] // comm.late_at[1], n_steps - 1))
        def _():
            comm.late(c_ins, c_outs, sems)
        body(*ins, *outs, *scr)

        @pl.when(step == n_steps - 1)
        def _():
            comm.finish(c_ins, c_outs, sems)

    res = pl.pallas_call(kern, name=name, grid=grid, in_specs=list(in_specs) + [ANY] * n_ci,
                         out_specs=list(out_specs) + [ANY] * n_co, out_shape=list(out_shape) + comm.outs,
                         scratch_shapes=list(scratch) + comm.sem_shapes(),
                         compiler_params=_params(len(grid)))(*args, *comm.ins)
    plan.landed(name, list(res[n_out:]))
    return list(res[:n_out])


def _mm_body(dims, has_add, *refs):
    if has_add:
        a_ref, b_ref, add_ref, o_ref = refs
        total = _bdot(a_ref[...], b_ref[...], dims) + add_ref[...]
    else:
        a_ref, b_ref, o_ref = refs
        total = _bdot(a_ref[...], b_ref[...], dims)
    o_ref[...] = total.astype(o_ref.dtype)


def _mm_nt_body(j, n, dy_ref, w_ref, o_ref):
    total = _bdot(dy_ref[:, 0:n], w_ref[0], NT)
    for jj in range(1, j):
        total = total + _bdot(dy_ref[:, jj * n:(jj + 1) * n], w_ref[jj], NT)
    o_ref[...] = total.astype(o_ref.dtype)


def _mm_nn(a, w3, *, name, out_dtype=BF16, add=None, tm=1024, tn=None, out3=False, w_t=False, plan=None):
    m, kk = a.shape
    j, n = w3.shape[0], w3.shape[1 if w_t else 2]
    tm, tn = min(tm, m), n if tn is None else tn
    n_t = n // tn
    in_specs = [pl.BlockSpec((tm, kk), lambda i, jj: (i, 0)),
                pl.BlockSpec((None, tn, kk), lambda i, jj: (jj // n_t, jj % n_t, 0)) if w_t else
                pl.BlockSpec((None, kk, tn), lambda i, jj: (jj // n_t, 0, jj % n_t))]
    args = [a, w3]
    if add is not None:
        in_specs.append(pl.BlockSpec((tm, tn), lambda i, jj: (i, jj)))
        args.append(add)
    if out3:
        out_spec = pl.BlockSpec((None, tm, tn), lambda i, jj: (jj // n_t, i, jj % n_t))
        out_shape = jax.ShapeDtypeStruct((j, m, n), out_dtype)
    else:
        out_spec = pl.BlockSpec((tm, tn), lambda i, jj: (i, jj))
        out_shape = jax.ShapeDtypeStruct((m, j * n), out_dtype)
    return _call(
        functools.partial(_mm_body, NT if w_t else NN, add is not None), name=name, grid=(m // tm, j * n_t),
        in_specs=in_specs, out_specs=[out_spec], out_shape=[out_shape], scratch=[], args=args, plan=plan)[0]


def _mm_gathering(a, shard, *, name, out3=False, w_t=False, tm=1024):
    m, kk = a.shape
    n = shard.shape[0 if w_t else 1]
    tm = min(tm, m)
    n_i = m // tm
    fetch_at = min(1, n_i - 1)

    def body(a_ref, shard_ref, o_ref, w_all, w_vmem, send_sems, recv_sems, copy_sems):
        jj, i = pl.program_id(0), pl.program_id(1)
        x, y, c = _place()
        me, sibling = (x, y, c), (x, y, 1 - c)
        chips = _neighbours(x, y, c)
        sibling_chips = [chips[1], chips[0], chips[2]]

        def rows(block):
            return w_all.at[4 * block[0] + 2 * block[1] + block[2]]

        def remote(k, block, to, from_shard=False):
            return pltpu.make_async_remote_copy(
                src_ref=shard_ref if from_shard else rows(block), dst_ref=rows(block), send_sem=send_sems.at[k],
                recv_sem=recv_sems.at[k], device_id=to, device_id_type=MESH)

        def load(step, src):
            return pltpu.make_async_copy(src, w_vmem.at[step % 2], copy_sems.at[1 + step % 2])

        own = [remote(0, me, sibling, True), remote(1, me, (*chips[0], c), True), remote(2, me, (*chips[1], c), True),
               remote(3, (*chips[0], c), (*chips[1], c))]
        passed = [remote(4 + j, (*chip, c), sibling) for j, chip in enumerate(chips)]
        local = pltpu.make_async_copy(shard_ref, rows(me), copy_sems.at[0])

        @pl.when(jnp.logical_and(i == 0, jj == 0))
        def _():
            local.start()
            own[0].start()
            own[1].start()
            load(0, shard_ref).start()

        def arrivals():
            yield 1, (lambda: remote(0, sibling, me).wait_recv()), sibling
            for j, chip in enumerate(chips):
                def landed(j=j, chip=chip):
                    if j < 2:
                        own[1 + j].wait_send()
                        own[2 + j].start()
                    remote(1 + j, (*chip, c), me).wait_recv()
                    passed[j].start()
                yield 2 + 2 * j, landed, (*chip, c)
                block = (*sibling_chips[j], 1 - c)
                yield 3 + 2 * j, (lambda j=j, block=block: remote(4 + j, block, me).wait_recv()), block

        for step, wait_for_it, block in arrivals():
            @pl.when(jnp.logical_and(i == fetch_at, jj == step - 1))
            def _():
                wait_for_it()
                load(step, rows(block)).start()

        for step in range(N_DEV):
            @pl.when(jnp.logical_and(i == 0, jj == step))
            def _():
                load(step, rows(me)).wait()

        o_ref[...] = _bdot(a_ref[...], w_vmem[lax.rem(jj, 2)], NT if w_t else NN).astype(o_ref.dtype)

        @pl.when(jnp.logical_and(i == n_i - 1, jj == N_DEV - 1))
        def _():
            for cp in [own[0], own[3]] + passed:
                cp.wait_send()
            local.wait()

    def swept(jj):
        x, y, c = _place()
        first, second = 2 + 2 * c, 4 - 2 * c
        flips = (0b000, 0b001, first, second + 1, second, first + 1, 0b110, 0b111)
        return jnp.bitwise_xor(4 * x + 2 * y + c, sum(jnp.where(jj == k, f, 0) for k, f in enumerate(flips)))

    if out3:
        out_spec = pl.BlockSpec((None, tm, n), lambda jj, i: (swept(jj), i, 0))
        out_shape = jax.ShapeDtypeStruct((N_DEV, m, n), BF16)
    else:
        out_spec = pl.BlockSpec((tm, n), lambda jj, i: (i, swept(jj)))
        out_shape = jax.ShapeDtypeStruct((m, N_DEV * n), BF16)
    return pl.pallas_call(
        body, name=name, grid=(N_DEV, n_i),
        in_specs=[pl.BlockSpec((tm, kk), lambda jj, i: (i, 0)), ANY], out_specs=[out_spec, ANY],
        scratch_shapes=[pltpu.VMEM((2,) + shard.shape, shard.dtype), pltpu.SemaphoreType.DMA((7,)),
                        pltpu.SemaphoreType.DMA((7,)), pltpu.SemaphoreType.DMA((3,))],
        out_shape=[out_shape, jax.ShapeDtypeStruct((N_DEV,) + shard.shape, shard.dtype)],
        compiler_params=_params(2))(a, shard)


def _sigmoid(v):
    return 0.5 * jnp.tanh(0.5 * v) + 0.5


def _resident(w):
    return pl.BlockSpec(w.shape, lambda i: (0,) * w.ndim, pipeline_mode=pl.Buffered(1))


def _ffn_out_loss(gu3, w3, add, g, target, *, name, tm=512):
    j2, m, n = gu3.shape
    j = j2 // 2
    nn = w3.shape[2]
    tm = min(tm, m)

    def body(gu_ref, w_ref, add_ref, g_ref, t_ref, dx_ref, dxb_ref, dg_ref, loss_ref, act_ref):
        i = pl.program_id(0)
        xv = add_ref[...]
        for jj in range(j):
            gate = gu_ref[0, jj].astype(F32)
            act = (gate * _sigmoid(gate) * gu_ref[1, jj].astype(F32)).astype(BF16)
            act_ref[jj] = act
            xv = xv + _bdot(act, w_ref[jj], NN)
        gv = g_ref[...]
        r = lax.rsqrt(jnp.mean(xv * xv, axis=-1, keepdims=True) + NORM_EPS)
        xhat = xv * r
        err = xhat * gv - t_ref[...]
        _acc_rows(i, loss_ref, 0.5 * jnp.sum(jnp.mean(err * err, axis=-1, keepdims=True), axis=0, keepdims=True))
        dy = err * (1.0 / nn)
        dxhat = dy * gv
        dx = r * (dxhat - xhat * jnp.mean(dxhat * xhat, axis=-1, keepdims=True))
        dx_ref[...] = dx
        dxb_ref[...] = dx.astype(BF16)
        _acc_rows(i, dg_ref, jnp.sum(dy * xhat, axis=0, keepdims=True))

    row = pl.BlockSpec((tm, nn), lambda i: (i, 0))
    return _call(body, name=name, grid=(m // tm,),
                 in_specs=[pl.BlockSpec((2, j, tm, n), lambda i: (0, 0, i, 0)), _resident(w3),
                           row, pl.BlockSpec(g.shape, lambda i: (0, 0)), row],
                 out_specs=[row, row, pl.BlockSpec((8, nn), lambda i: (0, 0)), pl.BlockSpec((8, LANES), lambda i: (0, 0)),
                            pl.BlockSpec((j, tm, n), lambda i: (0, i, 0))],
                 out_shape=[jax.ShapeDtypeStruct((m, nn), F32), jax.ShapeDtypeStruct((m, nn), BF16),
                            jax.ShapeDtypeStruct((8, nn), F32), jax.ShapeDtypeStruct((8, LANES), F32),
                            jax.ShapeDtypeStruct((j, m, n), BF16)],
                 scratch=[], args=[gu3.reshape(2, j, m, n), w3, add, g, target])


def _ffn_out_bwd(dy, w3, gu3, *, name, tm=1024):
    m, nn = dy.shape
    j, n, _ = w3.shape
    tm = min(tm, m)

    def body(dy_ref, w_ref, gu_ref, dgu_ref):
        da = _bdot(dy_ref[...], w_ref[...], NT)
        gate = gu_ref[0].astype(F32)
        up = gu_ref[1].astype(F32)
        sg = _sigmoid(gate)
        silu = gate * sg
        dgu_ref[0] = (da * up * (sg + silu * (1.0 - sg))).astype(BF16)
        dgu_ref[1] = (da * silu).astype(BF16)

    out = _call(body, name=name, grid=(m // tm, j),
                in_specs=[pl.BlockSpec((tm, nn), lambda i, jj: (i, 0)),
                          pl.BlockSpec((None, n, nn), lambda i, jj: (jj, 0, 0)),
                          pl.BlockSpec((2, None, tm, n), lambda i, jj: (0, jj, i, 0))],
                out_specs=[pl.BlockSpec((2, None, tm, n), lambda i, jj: (0, jj, i, 0))],
                out_shape=[jax.ShapeDtypeStruct((2, j, m, n), BF16)], scratch=[],
                args=[dy, w3, gu3.reshape(2, j, m, n)])[0]
    return out.reshape(2 * j, m, n)


def _rms_fwd_tail(xv, g_ref, h_ref):
    r = lax.rsqrt(jnp.mean(xv * xv, axis=-1, keepdims=True) + NORM_EPS)
    h_ref[...] = (xv * r * g_ref[...]).astype(BF16)


def _rms_bwd_tail(i, dh, x_ref, g_ref, dres_ref, dx_ref, dxb_ref, dg_ref):
    xv = x_ref[...]
    r = lax.rsqrt(jnp.mean(xv * xv, axis=-1, keepdims=True) + NORM_EPS)
    xhat = xv * r
    dxhat = dh * g_ref[...]
    dx = r * (dxhat - xhat * jnp.mean(dxhat * xhat, axis=-1, keepdims=True))
    if dres_ref is not None:
        dx = dx + dres_ref[...]
    dx_ref[...] = dx
    dxb_ref[...] = dx.astype(BF16)
    _acc_rows(i, dg_ref, jnp.sum(dh * xhat, axis=0, keepdims=True))


def _mm_nt_rms(dy, w3, x, g, dres, *, name, dy3=False, w_nn=False, tm=512, plan=None):
    j = w3.shape[0]
    m, kk = x.shape
    n = dy.shape[2] if dy3 else dy.shape[1] // j
    tm = min(tm, m)

    def body(dy_ref, w_ref, x_ref, g_ref, *rest):
        dres_ref = rest[0] if dres is not None else None
        dx_ref, dxb_ref, dg_ref = rest[-3:]
        dh = None
        for jj in range(j):
            piece = dy_ref[jj] if dy3 else dy_ref[:, jj * n:(jj + 1) * n]
            part = _bdot(piece, w_ref[jj], NN if w_nn else NT)
            dh = part if dh is None else dh + part
        _rms_bwd_tail(pl.program_id(0), dh, x_ref, g_ref, dres_ref, dx_ref, dxb_ref, dg_ref)

    row = pl.BlockSpec((tm, kk), lambda i: (i, 0))
    in_specs = [pl.BlockSpec((j, tm, n), lambda i: (0, i, 0)) if dy3 else pl.BlockSpec((tm, j * n), lambda i: (i, 0)),
                _resident(w3), row, pl.BlockSpec(g.shape, lambda i: (0, 0))]
    args = [dy, w3, x, g]
    if dres is not None:
        in_specs.append(row)
        args.append(dres)
    return _call(body, name=name, grid=(m // tm,), in_specs=in_specs,
                 out_specs=[row, row, pl.BlockSpec((8, kk), lambda i: (0, 0))],
                 out_shape=[jax.ShapeDtypeStruct((m, kk), F32), jax.ShapeDtypeStruct((m, kk), BF16),
                            jax.ShapeDtypeStruct((8, kk), F32)], scratch=[], args=args, plan=plan)


def _mix_out(o_a, y_b, proj, w_a, w_b, w, x, g, *, name, tm=512, plan=None):
    s, c = o_a.shape
    d = w.shape[1]
    tm = min(tm, s)

    def body(oa_ref, yb_ref, ga_ref, gb_ref, wa_ref, wb_ref, w_ref, x_ref, g_ref, x1_ref, h_ref, merged_ref, a_ref, b_ref):
        a_ref[...] = _bdot(oa_ref[...], wa_ref[...], NN).astype(BF16)
        b_ref[...] = _bdot(yb_ref[...], wb_ref[...], NN).astype(BF16)
        merged = (_sigmoid(ga_ref[...].astype(F32)) * a_ref[...].astype(F32)
                  + _sigmoid(gb_ref[...].astype(F32)) * b_ref[...].astype(F32)).astype(BF16)
        merged_ref[...] = merged
        xv = _bdot(merged, w_ref[...], NN) + x_ref[...]
        x1_ref[...] = xv
        _rms_fwd_tail(xv, g_ref, h_ref)

    row = pl.BlockSpec((tm, d), lambda i: (i, 0))
    narrow = pl.BlockSpec((tm, c), lambda i: (i, 0))
    whole = lambda arr: pl.BlockSpec(arr.shape, lambda i: (0,) * arr.ndim)
    return _call(body, name=name, grid=(s // tm,),
                 in_specs=[narrow, narrow, pl.BlockSpec((tm, d), lambda i: (i, 3)), pl.BlockSpec((tm, d), lambda i: (i, 4)),
                           whole(w_a), whole(w_b), whole(w), row, whole(g)],
                 out_specs=[row] * 5,
                 out_shape=[jax.ShapeDtypeStruct((s, d), F32)] + [jax.ShapeDtypeStruct((s, d), BF16)] * 4,
                 scratch=[], args=[o_a, y_b, proj, proj, w_a, w_b, w, x, g], plan=plan)


def _mm_tn_a3(a3, dy, *, name):
    j, t, n = a3.shape
    nn = dy.shape[1]
    return _call(functools.partial(_mm_body, TN, False), name=name, grid=(j,),
                 in_specs=[pl.BlockSpec((None, t, n), lambda jj: (jj, 0, 0)), pl.BlockSpec((t, nn), lambda jj: (0, 0))],
                 out_specs=[pl.BlockSpec((None, n, nn), lambda jj: (jj, 0, 0))],
                 out_shape=[jax.ShapeDtypeStruct((j, n, nn), BF16)], scratch=[], args=[a3, dy])[0]


def _mm_nt(dy, w3, *, name, out_dtype=BF16, tm=512, tn=1024, plan=None):
    m = dy.shape[0]
    j, kk, n = w3.shape
    tm, tn = min(tm, m), min(tn, kk)
    return _call(
        functools.partial(_mm_nt_body, j, n), name=name,
        grid=(m // tm, kk // tn),
        in_specs=[pl.BlockSpec((tm, j * n), lambda i, q: (i, 0)),
                  pl.BlockSpec((j, tn, n), lambda i, q: (0, q, 0))],
        out_specs=[pl.BlockSpec((tm, tn), lambda i, q: (i, q))],
        out_shape=[jax.ShapeDtypeStruct((m, kk), out_dtype)], scratch=[], args=[dy, w3], plan=plan)[0]


def _mm_nt_gain(dy, w3, x, *, name, tm=512, plan=None):
    m, kk = x.shape
    j, _, n = w3.shape
    tm = min(tm, m)

    def body(dy_ref, w_ref, x_ref, o_ref, dg_ref):
        _mm_nt_body(j, n, dy_ref, w_ref, o_ref)
        xv = x_ref[...]
        xhat = xv * lax.rsqrt(jnp.mean(xv * xv, axis=-1, keepdims=True) + NORM_EPS)
        _acc_rows(pl.program_id(0), dg_ref, jnp.sum(o_ref[...] * xhat, axis=0, keepdims=True))

    row = pl.BlockSpec((tm, kk), lambda i: (i, 0))
    return _call(body, name=name, grid=(m // tm,),
                 in_specs=[pl.BlockSpec((tm, j * n), lambda i: (i, 0)), pl.BlockSpec(w3.shape, lambda i: (0, 0, 0)), row],
                 out_specs=[row, pl.BlockSpec((8, kk), lambda i: (0, 0))],
                 out_shape=[jax.ShapeDtypeStruct((m, kk), F32), jax.ShapeDtypeStruct((8, kk), F32)],
                 scratch=[], args=[dy, w3, x], plan=plan)


def _mm_tn(a, dy, n, *, name, out_dtype=BF16, tm=512, tn=None, k_tiles=None, plan=None):
    t, kk = a.shape
    j = dy.shape[1] // n
    tm, tn = min(tm, kk), n if tn is None else tn
    n_t = n // tn
    first, count = (0, kk // tm) if k_tiles is None else k_tiles
    return _call(
        functools.partial(_mm_body, TN, False), name=name,
        grid=(count, j * n_t),
        in_specs=[pl.BlockSpec((t, tm), lambda i, jj: (0, first + i)),
                  pl.BlockSpec((t, tn), lambda i, jj: (0, jj))],
        out_specs=[pl.BlockSpec((None, tm, tn), lambda i, jj: (jj // n_t, i, jj % n_t))],
        out_shape=[jax.ShapeDtypeStruct((j, count * tm, n), out_dtype)], scratch=[], args=[a, dy], plan=plan)[0]


def _rows(body, ins, outs, *, n_rows, tm, name, plan=None):
    tm = min(tm, n_rows)
    n_steps = n_rows // tm
    in_specs, args = [], []
    for arr, kind, width, block in ins:
        if kind == "row":
            in_specs.append(pl.BlockSpec((tm, width), functools.partial(lambda i, b: (i, b), b=block)))
        elif kind == "prev":
            in_specs.append(pl.BlockSpec((tm, width), functools.partial(lambda i, b: (jnp.maximum(i - 1, 0), b), b=block)))
        elif kind == "next":
            in_specs.append(pl.BlockSpec((tm, width), functools.partial(lambda i, b: (jnp.minimum(i + 1, n_steps - 1), b), b=block)))
        else:
            in_specs.append(pl.BlockSpec(arr.shape, functools.partial(lambda i, nd: (0,) * nd, nd=arr.ndim)))
        args.append(arr)
    out_specs, out_shape = [], []
    for shape, dtype, kind in outs:
        if kind == "row":
            out_specs.append(pl.BlockSpec((tm, shape[1]), lambda i: (i, 0)))
        else:
            out_specs.append(pl.BlockSpec(shape, functools.partial(lambda i, nd: (0,) * nd, nd=len(shape))))
        out_shape.append(jax.ShapeDtypeStruct(shape, dtype))

    def kern(*refs):
        body(pl.program_id(0), n_steps, *refs)

    return _call(kern, name=name, grid=(n_steps,), in_specs=in_specs, out_specs=out_specs, out_shape=out_shape,
                 scratch=[], args=args, plan=plan)


def _acc_rows(i, ref, value):
    @pl.when(i == 0)
    def _():
        ref[...] = jnp.zeros_like(ref)
    ref[...] += jnp.broadcast_to(value, ref.shape)


def _rms_fwd(x, g, *, name, tm=512):
    s, d = x.shape

    def body(i, n, x_ref, g_ref, h_ref):
        _rms_fwd_tail(x_ref[...], g_ref, h_ref)

    return _rows(body, [(x, "row", d, 0), (g, "full", 0, 0)], [((s, d), BF16, "row")], n_rows=s, tm=tm, name=name)[0]


def _rms_bwd(x, g, dh, dres, *, name, tm=512, plan=None):
    s, d = x.shape

    def body(i, n, x_ref, g_ref, dh_ref, dres_ref, dx_ref, dxb_ref, dg_ref):
        _rms_bwd_tail(i, dh_ref[...].astype(F32), x_ref, g_ref, dres_ref, dx_ref, dxb_ref, dg_ref)

    return _rows(body, [(x, "row", d, 0), (g, "full", 0, 0), (dh, "row", d, 0), (dres, "row", d, 0)],
                 [((s, d), F32, "row"), ((s, d), BF16, "row"), ((8, d), F32, "acc")],
                 n_rows=s, tm=tm, name=name, plan=plan)


def _mix_out_bwd(dx1b, w, br_a, br_b, proj, w_a, w_b, *, name, tm=512, plan=None):
    s, d = br_a.shape
    c = w_a.shape[0]
    tm = min(tm, s)

    def body(dy_ref, w_ref, a_ref, b_ref, ga_ref, gb_ref, wa_ref, wb_ref, da_ref, db_ref, dg_ref, doa_ref, dyb_ref):
        dm = _bdot(dy_ref[...], w_ref[...], NT)
        sa = _sigmoid(ga_ref[...].astype(F32))
        sb = _sigmoid(gb_ref[...].astype(F32))
        da_ref[...] = (dm * sa).astype(BF16)
        db_ref[...] = (dm * sb).astype(BF16)
        dg_ref[:, :d] = (dm * a_ref[...].astype(F32) * sa * (1.0 - sa)).astype(BF16)
        dg_ref[:, d:] = (dm * b_ref[...].astype(F32) * sb * (1.0 - sb)).astype(BF16)
        doa_ref[...] = _bdot(da_ref[...], wa_ref[...], NT).astype(BF16)
        dyb_ref[...] = _bdot(db_ref[...], wb_ref[...], NT).astype(BF16)

    row = pl.BlockSpec((tm, d), lambda i: (i, 0))
    narrow = pl.BlockSpec((tm, c), lambda i: (i, 0))
    whole = lambda arr: pl.BlockSpec(arr.shape, lambda i: (0,) * arr.ndim)
    return _call(body, name=name, grid=(s // tm,),
                 in_specs=[row, whole(w), row, row, pl.BlockSpec((tm, d), lambda i: (i, 3)),
                           pl.BlockSpec((tm, d), lambda i: (i, 4)), whole(w_a), whole(w_b)],
                 out_specs=[row, row, pl.BlockSpec((tm, 2 * d), lambda i: (i, 0)), narrow, narrow],
                 out_shape=[jax.ShapeDtypeStruct((s, d), BF16), jax.ShapeDtypeStruct((s, d), BF16),
                            jax.ShapeDtypeStruct((s, 2 * d), BF16), jax.ShapeDtypeStruct((s, c), BF16),
                            jax.ShapeDtypeStruct((s, c), BF16)],
                 scratch=[], args=[dx1b, w, br_a, br_b, proj, proj, w_a, w_b], plan=plan)


def _shift_down(cur, prev, k, first):
    row = lax.broadcasted_iota(jnp.int32, cur.shape, 0)
    out = jnp.where(row >= k, pltpu.roll(cur, k, 0), pltpu.roll(prev, k, 0))
    return jnp.where(jnp.logical_and(first, row < k), 0.0, out)


def _shift_up(cur, nxt, k, last):
    tm = cur.shape[0]
    row = lax.broadcasted_iota(jnp.int32, cur.shape, 0)
    out = jnp.where(row < tm - k, pltpu.roll(cur, tm - k, 0), pltpu.roll(nxt, tm - k, 0))
    return jnp.where(jnp.logical_and(last, row >= tm - k), 0.0, out)


def _conv_fwd(proj, conv_w, *, name, tm=512):
    s = proj.shape[0]
    c = CONV_WIDTH

    def body(i, n, u_ref, gb_ref, gc_ref, up_ref, gcp_ref, w_ref, y_ref):
        cu = gc_ref[...].astype(F32) * u_ref[...].astype(F32)
        cup = gcp_ref[...].astype(F32) * up_ref[...].astype(F32)
        first = i == 0
        y = (w_ref[0:1, :] * _shift_down(cu, cup, 2, first) + w_ref[1:2, :] * _shift_down(cu, cup, 1, first)
             + w_ref[2:3, :] * cu)
        y_ref[...] = (gb_ref[...].astype(F32) * y).astype(BF16)

    return _rows(body, [(proj, "row", c, 3), (proj, "row", c, 4), (proj, "row", c, 5),
                        (proj, "prev", c, 3), (proj, "prev", c, 5), (conv_w, "full", 0, 0)],
                 [((s, c), BF16, "row")], n_rows=s, tm=tm, name=name)[0]


def _conv_bwd(dy_b, proj, conv_w, *, name, tm=512, plan=None):
    s = proj.shape[0]
    c = CONV_WIDTH

    def body(i, n, dy_ref, u_ref, gb_ref, gc_ref, up_ref, gcp_ref, dyn_ref, gbn_ref, w_ref, d_ref, dw_ref):
        first, last = i == 0, i == n - 1
        u = u_ref[...].astype(F32)
        gb = gb_ref[...].astype(F32)
        gc = gc_ref[...].astype(F32)
        cu = gc * u
        cup = gcp_ref[...].astype(F32) * up_ref[...].astype(F32)
        cu1 = _shift_down(cu, cup, 1, first)
        cu2 = _shift_down(cu, cup, 2, first)
        conv = w_ref[0:1, :] * cu2 + w_ref[1:2, :] * cu1 + w_ref[2:3, :] * cu
        dy = dy_ref[...].astype(F32)
        dyc = dy * gb
        dycn = dyn_ref[...].astype(F32) * gbn_ref[...].astype(F32)
        dcu = (w_ref[2:3, :] * dyc + w_ref[1:2, :] * _shift_up(dyc, dycn, 1, last)
               + w_ref[0:1, :] * _shift_up(dyc, dycn, 2, last))
        d_ref[:, 0:c] = (dcu * gc).astype(BF16)
        d_ref[:, c:2 * c] = (dy * conv).astype(BF16)
        d_ref[:, 2 * c:3 * c] = (dcu * u).astype(BF16)
        row = lax.broadcasted_iota(jnp.int32, (8, c), 0)
        dw = (jnp.where(row == 0, jnp.sum(dyc * cu2, axis=0, keepdims=True), 0.0)
              + jnp.where(row == 1, jnp.sum(dyc * cu1, axis=0, keepdims=True), 0.0)
              + jnp.where(row == 2, jnp.sum(dyc * cu, axis=0, keepdims=True), 0.0))

        @pl.when(first)
        def _():
            dw_ref[...] = jnp.zeros_like(dw_ref)
        dw_ref[...] += dw

    return _rows(body, [(dy_b, "row", c, 0), (proj, "row", c, 3), (proj, "row", c, 4), (proj, "row", c, 5),
                        (proj, "prev", c, 3), (proj, "prev", c, 5), (dy_b, "next", c, 0), (proj, "next", c, 4),
                        (conv_w, "full", 0, 0)],
                 [((s, 3 * c), BF16, "row"), ((8, c), F32, "acc")], n_rows=s, tm=tm, name=name, plan=plan)


def _mem_probs(q, k, scale):
    sc = _bdot(q, k, NT) * scale
    sc = sc - jnp.max(sc, axis=-1, keepdims=True)
    p = jnp.exp(sc)
    return p / jnp.sum(p, axis=-1, keepdims=True)


def _mem_sublayer(hq, w_q, kv, w_o, x, g, *, name, tm=512, plan=None):
    s, d = hq.shape
    hd = d // MEM_HEADS
    scale = 1.0 / math.sqrt(hd)
    tm = min(tm, s)

    def body(hq_ref, wq_ref, kv_ref, wo_ref, x_ref, g_ref, q_ref, o_ref, x2_ref, h_ref):
        q_ref[...] = _bdot(hq_ref[...], wq_ref[...], NN).astype(BF16)
        for h in range(MEM_HEADS):
            cols = slice(h * hd, (h + 1) * hd)
            p = _mem_probs(q_ref[:, cols], kv_ref[:, cols], scale)
            o_ref[:, cols] = _bdot(p, kv_ref[:, d + h * hd:d + (h + 1) * hd], NN).astype(BF16)
        xv = _bdot(o_ref[...], wo_ref[...], NN) + x_ref[...]
        x2_ref[...] = xv
        _rms_fwd_tail(xv, g_ref, h_ref)

    row = pl.BlockSpec((tm, d), lambda i: (i, 0))
    whole = lambda a: pl.BlockSpec(a.shape, lambda i: (0,) * a.ndim)
    return _call(body, name=name, grid=(s // tm,),
                 in_specs=[row, whole(w_q), whole(kv), whole(w_o), row, whole(g)], out_specs=[row] * 4,
                 out_shape=[jax.ShapeDtypeStruct((s, d), BF16), jax.ShapeDtypeStruct((s, d), BF16),
                            jax.ShapeDtypeStruct((s, d), F32), jax.ShapeDtypeStruct((s, d), BF16)],
                 scratch=[], args=[hq, w_q, kv, w_o, x, g], plan=plan)


def _mem_sublayer_bwd(dx2b, dx2, x, g, qm, kv, w_q, w_o, *, name, tm=512, plan=None):
    s, d = qm.shape
    hd = d // MEM_HEADS
    scale = 1.0 / math.sqrt(hd)
    tm = min(tm, s)

    def body(dyb_ref, dres_ref, x_ref, g_ref, q_ref, kv_ref, wq_ref, wo_ref, dx_ref, dxb_ref, dg_ref, dq_ref, dkv_ref):
        i = pl.program_id(0)

        @pl.when(i == 0)
        def _():
            dkv_ref[...] = jnp.zeros_like(dkv_ref)
        dom = _bdot(dyb_ref[...], wo_ref[...], NT).astype(BF16)
        for h in range(MEM_HEADS):
            cols = slice(h * hd, (h + 1) * hd)
            vcols = slice(d + h * hd, d + (h + 1) * hd)
            q, k, v, do = q_ref[:, cols], kv_ref[:, cols], kv_ref[:, vcols], dom[:, cols]
            p = _mem_probs(q, k, scale)
            dp = _bdot(do, v, NT)
            ds = p * (dp - jnp.sum(dp * p, axis=-1, keepdims=True)) * scale
            dq_ref[:, cols] = _bdot(ds, k, NN).astype(BF16)
            dkv_ref[:, cols] += _bdot(ds, q, TN)
            dkv_ref[:, vcols] += _bdot(p, do, TN)
        dh = _bdot(dq_ref[...], wq_ref[...], NT)
        _rms_bwd_tail(i, dh, x_ref, g_ref, dres_ref, dx_ref, dxb_ref, dg_ref)

    row = pl.BlockSpec((tm, d), lambda i: (i, 0))
    whole = lambda a: pl.BlockSpec(a.shape, lambda i: (0,) * a.ndim)
    return _call(body, name=name, grid=(s // tm,),
                 in_specs=[row, row, row, whole(g), row, whole(kv), whole(w_q), whole(w_o)],
                 out_specs=[row, row, pl.BlockSpec((8, d), lambda i: (0, 0)), row, whole(kv)],
                 out_shape=[jax.ShapeDtypeStruct((s, d), F32), jax.ShapeDtypeStruct((s, d), BF16),
                            jax.ShapeDtypeStruct((8, d), F32), jax.ShapeDtypeStruct((s, d), BF16),
                            jax.ShapeDtypeStruct(kv.shape, F32)],
                 scratch=[], args=[dx2b, dx2, x, g, qm, kv, w_q, w_o], plan=plan)


def _sb_consts(t):
    row = lax.broadcasted_iota(jnp.int32, (t, t), 0)
    col = lax.broadcasted_iota(jnp.int32, (t, t), 1)
    lane = lax.broadcasted_iota(jnp.int32, (t, LANES), 1)
    return row, col, lane < SB_HEAD_DIM


def _sb_logits(q, k):
    z2 = jnp.minimum(_bdot(q, k, NT) * LOG2_E, SB_CLAMP)
    return z2, jnp.exp2(z2)


def _tri_sum(v, tri):
    hi = v.astype(BF16)
    lo = (v - hi.astype(F32)).astype(BF16)
    return _bdot(hi, tri, NN) + _bdot(lo, tri, NN)


def _sb_fwd(proj, *, name, plan=None):
    s = proj.shape[0]
    t, nh = SB_TILE, SB_STEP_HEADS
    n_q = s // t
    scale = 1.0 / math.sqrt(SB_HEAD_DIM)

    def body(q_ref, k_ref, v_ref, o_ref, c_ref, first_ref, acc_ref, c_scr):
        i = pl.program_id(1)
        row, col, head0 = _sb_consts(t)
        later = (row > col).astype(BF16)
        valid = col < row
        lanes = lambda h: slice((h // 2) * LANES, (h // 2 + 1) * LANES)
        q = [jnp.where(head0 == (h % 2 == 0), q_ref[:, lanes(h)] * scale, 0) for h in range(nh)]

        def tiles(kbs, diag_first, carry):
            rows = [pl.ds(pl.multiple_of(kb * t, t), t) for kb in kbs]
            jobs = [(n, h) for n in range(len(kbs)) for h in range(nh)]
            masked = lambda n: diag_first and n == 0
            zs = {(n, h): _sb_logits(q[h], k_ref[rows[n], lanes(h)]) for n, h in jobs}
            fail = {j: jnp.log2(1.0 + zs[j][1]) for j in jobs}
            fail = {j: jnp.where(valid, fail[j], 0.0) if masked(j[0]) else fail[j] for j in jobs}
            cum = {j: _tri_sum(fail[j], later) for j in jobs}
            run, before = list(carry), {}
            for n, h in jobs:
                before[n, h] = run[h]
                run[h] = run[h] + cum[n, h][:, 0:1] + fail[n, h][:, 0:1]
            w = {j: jnp.exp2(zs[j][0] - fail[j] - cum[j] - before[j]) for j in jobs}
            w = {j: jnp.where(valid, w[j], 0.0) if masked(j[0]) else w[j] for j in jobs}
            for n, h in jobs:
                acc_ref[h] += _bdot(w[n, h], v_ref[rows[n], lanes(h)], NN)
            return tuple(run)

        acc_ref[...] = jnp.zeros_like(acc_ref)
        zero = (jnp.zeros((t, 1), F32),) * nh

        def alive(carry):
            return (functools.reduce(jnp.minimum, [jnp.min(c) for c in carry]) < SB_DEAD).astype(jnp.int32)

        def step(state):
            new = tiles([state[0]], False, state[2:])
            return (state[0] - 1, alive(new)) + new

        @pl.when(i == 0)
        def _():
            for h, c in enumerate(tiles([i], True, zero)):
                c_scr[h] = c

        @pl.when(i > 0)
        def _():
            for h, c in enumerate(tiles([i, i - 1], True, zero)):
                c_scr[h] = c
        carry = tuple(c_scr[h] for h in range(nh))
        state = lax.while_loop(lambda st: jnp.logical_and(st[0] >= 0, st[1] > 0), step, (i - 2, alive(carry)) + carry)
        for b in range(nh // 2):
            o_ref[:, b * LANES:(b + 1) * LANES] = jnp.where(head0, acc_ref[2 * b], acc_ref[2 * b + 1]).astype(BF16)
        head = lax.broadcasted_iota(jnp.int32, (t, nh), 1)
        c_ref[...] = sum(jnp.where(head == h, state[2 + h], 0.0) for h in range(nh))
        first_ref[pl.program_id(0), i] = (jnp.maximum(state[0], -1) + 1).astype(F32)

    n_p, width = SB_HEADS // nh, nh * SB_HEAD_DIM
    k_blk, v_blk = SB_WIDTH // width, 2 * SB_WIDTH // width
    return _call(
        body, name=name, grid=(n_p, n_q),
        in_specs=[pl.BlockSpec((t, width), lambda p, i: (i, p)),
                  pl.BlockSpec((s, width), lambda p, i: (0, k_blk + p)),
                  pl.BlockSpec((s, width), lambda p, i: (0, v_blk + p))],
        out_specs=[pl.BlockSpec((t, width), lambda p, i: (i, p)),
                   pl.BlockSpec((None, t, nh), lambda p, i: (p, i, 0)),
                   pl.BlockSpec(memory_space=pltpu.SMEM)],
        out_shape=[jax.ShapeDtypeStruct((s, SB_WIDTH), BF16), jax.ShapeDtypeStruct((n_p, s, nh), F32),
                   jax.ShapeDtypeStruct((n_p, n_q), F32)],
        scratch=[pltpu.VMEM((nh, t, LANES), F32), pltpu.VMEM((nh, t, 1), F32)], args=[proj, proj, proj], plan=plan)


def _sb_bwd(proj, do_a, ctot, first, *, name, plan=None):
    s = proj.shape[0]
    t, nh = SB_TILE, SB_STEP_HEADS
    n_q = s // t
    scale = 1.0 / math.sqrt(SB_HEAD_DIM)

    def body(q_ref, k_ref, v_ref, do_ref, c_ref, first_ref, dq_ref, dk_ref, dv_ref, dq_acc, dk_acc, dv_acc):
        i = pl.program_id(1)
        kb0 = jnp.clip(first_ref[pl.program_id(0), i].astype(jnp.int32), 0, i)
        row, col, head0 = _sb_consts(t)
        upto = (row <= col).astype(BF16)
        before = (row < col).astype(BF16)
        valid = col < row
        lanes = lambda h: slice((h // 2) * LANES, (h // 2 + 1) * LANES)
        q2 = [jnp.where(head0 == (h % 2 == 0), q_ref[:, lanes(h)] * scale, 0) for h in range(nh)]
        do2 = [jnp.where(head0 == (h % 2 == 0), do_ref[:, lanes(h)], 0) for h in range(nh)]
        ctot2 = [c_ref[:, h:h + 1] for h in range(nh)]

        @pl.when(i == 0)
        def _():
            dk_acc[...] = jnp.zeros_like(dk_acc)
            dv_acc[...] = jnp.zeros_like(dv_acc)
        dq_acc[...] = jnp.zeros_like(dq_acc)

        def tiles(kbs, diag_last, carry):
            rows = [pl.ds(pl.multiple_of(kb * t, t), t) for kb in kbs]
            kt = {(n, h): k_ref[rows[n], lanes(h)] for n in range(len(kbs)) for h in range(nh)}
            jobs = list(kt)
            masked = lambda n: diag_last and n == len(kbs) - 1
            t_last = slice(t - 1, t)
            zs = {(n, h): _sb_logits(q2[h], kt[n, h]) for n, h in jobs}
            dw = {(n, h): _bdot(do2[h], v_ref[rows[n], lanes(h)], NT) for n, h in jobs}
            fail = {j: jnp.log2(1.0 + zs[j][1]) for j in jobs}
            fail = {j: jnp.where(valid, fail[j], 0.0) if masked(j[0]) else fail[j] for j in jobs}
            cum = {j: _tri_sum(fail[j], upto) for j in jobs}
            miss = {j: jnp.exp2(-fail[j]) for j in jobs}
            beta = {j: zs[j][1] * miss[j] for j in jobs}
            fail_run, fail_before = list(carry[0::2]), {}
            for n, h in jobs:
                fail_before[n, h] = fail_run[h]
                fail_run[h] = fail_run[h] + cum[n, h][:, t_last]
            w = {(n, h): beta[n, h] * jnp.exp2(fail_before[n, h] + cum[n, h] - ctot2[h]) for n, h in jobs}
            w = {j: jnp.where(valid, w[j], 0.0) if masked(j[0]) else w[j] for j in jobs}
            g = {j: w[j] * dw[j] for j in jobs}
            g_local = {j: _bdot(g[j], before, NN) for j in jobs}
            for n, h in jobs:
                dv_acc[rows[n], lanes(h)] += _bdot(w[n, h], do2[h], TN)
            g_run, dz = list(carry[1::2]), {}
            for n, h in jobs:
                g_sum = g_run[h] + g_local[n, h]
                dz[n, h] = g[n, h] * miss[n, h] - beta[n, h] * g_sum
                g_run[h] = g_sum[:, t_last] + g[n, h][:, t_last]
            dz = {j: jnp.where(valid, dz[j], 0.0) if masked(j[0]) else dz[j] for j in jobs}
            for n, h in jobs:
                dq_acc[h] += _bdot(dz[n, h], kt[n, h], NN)
                dk_acc[rows[n], lanes(h)] += _bdot(dz[n, h], q2[h], TN)
            return tuple(v for pair in zip(fail_run, g_run) for v in pair)

        zero = jnp.zeros((t, 1), F32)
        carry = lax.fori_loop(kb0, i - 1, lambda n, c: tiles([n], False, c), (zero,) * (2 * nh))

        @pl.when(i == 0)
        def _():
            tiles([i], True, carry)

        @pl.when(i > 0)
        def _():
            tiles([i - 1, i], True, carry)
        for b in range(nh // 2):
            dq_ref[:, b * LANES:(b + 1) * LANES] = (jnp.where(head0, dq_acc[2 * b], dq_acc[2 * b + 1])
                                                    * scale).astype(BF16)

        @pl.when(i == n_q - 1)
        def _():
            dk_ref[...] = dk_acc[...].astype(BF16)
            dv_ref[...] = dv_acc[...].astype(BF16)

    n_p, width = SB_HEADS // nh, nh * SB_HEAD_DIM
    k_blk, v_blk = SB_WIDTH // width, 2 * SB_WIDTH // width
    outs = _call(
        body, name=name, grid=(n_p, n_q),
        in_specs=[pl.BlockSpec((t, width), lambda p, i: (i, p)),
                  pl.BlockSpec((s, width), lambda p, i: (0, k_blk + p)),
                  pl.BlockSpec((s, width), lambda p, i: (0, v_blk + p)),
                  pl.BlockSpec((t, width), lambda p, i: (i, p)),
                  pl.BlockSpec((None, t, nh), lambda p, i: (p, i, 0)),
                  pl.BlockSpec(memory_space=pltpu.SMEM)],
        out_specs=[pl.BlockSpec((t, width), lambda p, i: (i, p)),
                   pl.BlockSpec((s, width), lambda p, i: (0, p)),
                   pl.BlockSpec((s, width), lambda p, i: (0, p))],
        out_shape=[jax.ShapeDtypeStruct((s, SB_WIDTH), BF16)] * 3,
        scratch=[pltpu.VMEM((nh, t, LANES), F32), pltpu.VMEM((s, width), F32), pltpu.VMEM((s, width), F32)],
        args=[proj, proj, proj, do_a, ctot, first], plan=plan)
    return jnp.concatenate(outs, axis=1)


def _mm_gathered(a, key, plan, *, name, out3=False, w_t=False):
    src = plan.gathering(key)
    if src is None:
        return _mm_nn(a, plan.weight(key), name=name, out3=out3, w_t=w_t, plan=plan)
    out, w_all = _mm_gathering(a, src, name=name, out3=out3, w_t=w_t)
    plan.set_weight(key, w_all)
    return out


def _local_step(x, mem, target, gains, plan):
    g_mix, g_memq, g_memkv, g_ffn, g_fin = gains
    d = x.shape[1]

    h0 = _rms_fwd(x, g_mix, name="rms_mix")
    proj = _mm_gathered(h0, "in", plan, name="mm_in")
    w_in = plan.weight("in")
    o_a, ctot, first = _sb_fwd(proj, name="sb_fwd", plan=plan)
    conv_w = plan.weight("conv")
    y_b = _conv_fwd(proj, conv_w, name="conv_fwd")
    w_a, w_b, w_mix = plan.weight("a"), plan.weight("b"), plan.weight("mix")
    x1, hq, merged, br_a, br_b = _mix_out(o_a, y_b, proj, w_a[0], w_b[0], w_mix[0], x, g_memq, name="mm_mix", plan=plan)
    w_mq, w_kv, w_mo = plan.weight("mq")[0], plan.weight("kv"), plan.weight("mo")[0]
    mn = _rms_fwd(mem, g_memkv, name="rms_memkv")
    kv = _mm_nn(mn, w_kv, name="mm_memkv")
    qm, om, x2, hf = _mem_sublayer(hq, w_mq, kv, w_mo, x1, g_ffn, name="mem_sublayer", plan=plan)
    gu = _mm_gathered(hf, "fi", plan, name="mm_ffn_in", out3=True, w_t=True)
    w_fi, w_fo = plan.weight("fi"), plan.weight("fo")
    dx3, dx3b, dg_fin, loss, act = _ffn_out_loss(gu, w_fo, x2, g_fin, target, name="mm_ffn_out")

    plan.grad("fo", _mm_tn_a3(act, dx3b, name="mm_d_w_ffn_out"))
    dgu = _ffn_out_bwd(dx3b, w_fo, gu, name="mm_d_act")
    plan.grad("fi", _mm_tn_a3(dgu, hf, name="mm_d_w_ffn_in"))
    dx2, dx2b, dg_ffn = _mm_nt_rms(dgu, w_fi, x2, g_ffn, dx3, name="mm_d_hf", dy3=True, w_nn=True, plan=plan)

    plan.grad("mo", _mm_tn(om, dx2b, d, name="mm_d_w_memo"))
    dx1, dx1b, dg_memq, dqm, dkv = _mem_sublayer_bwd(dx2b, dx2, x1, g_memq, qm, kv, w_mq, w_mo, name="mem_sublayer_bwd",
                                                    plan=plan)
    plan.grad("mq", _mm_tn(hq, dqm, d, name="mm_d_w_memq"))
    plan.grad("kv", _mm_tn(mn, dkv, w_kv.shape[2], name="mm_d_w_memkv"))
    _, _, dg_memkv = _mm_nt_rms(dkv, w_kv, mem, g_memkv, None, name="mm_d_mn")

    plan.grad("mix", _mm_tn(merged, dx1b, d, name="mm_d_w_mix"))
    dbr_a, dbr_b, dgab, do_a, dy_b = _mix_out_bwd(dx1b, w_mix[0], br_a, br_b, proj, w_a[0], w_b[0], name="mm_d_merged",
                                                 plan=plan)
    plan.grad("a", _mm_tn(o_a, dbr_a, d, name="mm_d_w_branch_a"))
    plan.grad("b", _mm_tn(y_b, dbr_b, d, name="mm_d_w_branch_b"))
    dconv, dconv_w = _conv_bwd(dy_b, proj, conv_w, name="conv_bwd", plan=plan)
    dqkv = _sb_bwd(proj, do_a, ctot, first, name="sb_bwd", plan=plan)
    dproj = jnp.concatenate([dqkv, dconv, dgab], axis=1)
    rows_in1 = d // IN_SPLIT[1] * (IN_SPLIT[1] - IN_SPLIT[0])
    plan.grad("in0", _mm_tn(h0, dproj, w_in.shape[2], name="mm_d_w_in0", tm=d - rows_in1, k_tiles=(0, 1)))
    plan.grad("in1", _mm_tn(h0, dproj, w_in.shape[2], name="mm_d_w_in1", tm=rows_in1,
                            k_tiles=(d // rows_in1 - 1, 1), plan=plan))
    dh0, dg_mix = _mm_nt_gain(dproj, w_in, x, name="mm_d_h0", plan=plan)
    conv_rows = jnp.zeros((3, d), F32).at[:, :CONV_WIDTH].set(dconv_w[:3])
    plan.offer("small", jnp.concatenate(
        [dg_mix[:1], dg_memq[:1], dg_memkv[:1], dg_ffn[:1], dg_fin[:1], conv_rows,
         jnp.broadcast_to(loss[:1, :1], (1, d)), jnp.zeros((SMALL_ROWS - 9, d), F32)], axis=0))
    dx0, _, _ = _rms_bwd(x, g_mix, dh0, dx1, name="rms_mix_bwd", plan=plan)
    return dx0


def _row_tile(a, target=512):
    tm = min(a, target)
    while a % tm:
        tm -= 8
    return tm


def _sum_with_sibling(parts, recvs, core, *, name):
    n = len(parts)

    def body(core_ref, *refs):
        for p_ref, r_ref, o_ref in zip(refs[:n], refs[n:2 * n], refs[2 * n:]):
            o_ref[...] = (p_ref[...].astype(F32) + r_ref[...].astype(F32)).astype(o_ref.dtype)

    mine = [pl.BlockSpec((None,) + p.shape[1:], lambda q, core_ref: (2 * q + core_ref[0], 0, 0)) for p in parts]
    other = [pl.BlockSpec((None,) + p.shape[1:], lambda q, core_ref: (q, 0, 0)) for p in parts]
    return pl.pallas_call(
        body, name=name,
        grid_spec=pltpu.PrefetchScalarGridSpec(num_scalar_prefetch=1, grid=(N_CHIP,), in_specs=mine + other,
                                               out_specs=other),
        out_shape=[jax.ShapeDtypeStruct((N_CHIP,) + p.shape[1:], p.dtype) for p in parts],
        compiler_params=_params(1))(core, *parts, *recvs)


def _adam_math(wv, g, m, v):
    m = ADAM_B1 * m + (1.0 - ADAM_B1) * g
    v = ADAM_B2 * v + (1.0 - ADAM_B2) * (g * g)
    m_hat = m / (1.0 - ADAM_B1 ** ADAM_STEP)
    v_hat = v / (1.0 - ADAM_B2 ** ADAM_STEP)
    delta = -ADAM_LR * (m_hat / (jnp.sqrt(v_hat) + ADAM_EPS) + ADAM_WD * wv)
    return delta, m, v


def _adam_sharded(ws, ms, vs, owns, recvs, chip, *, name):
    n = len(ws)
    a, b = ws[0].shape
    tm = _row_tile(a)

    def body(chip_ref, *refs):
        ins, outs = refs[:5 * n], refs[5 * n:]
        for k in range(n):
            w_ref, m_ref, v_ref, own_ref, recv_ref = ins[k::n]
            g = own_ref[...].astype(F32)
            for j in range(3):
                g = g + recv_ref[j].astype(F32)
            delta, nm, nv = _adam_math(w_ref[...], g, m_ref[...], v_ref[...])
            for o_ref, value in zip(outs[k::n], (g, delta, nm, nv)):
                o_ref[...] = value

    tile = pl.BlockSpec((tm, b), lambda i, chip_ref: (i, 0))
    res = pl.pallas_call(
        body, name=name,
        grid_spec=pltpu.PrefetchScalarGridSpec(
            num_scalar_prefetch=1, grid=(a // tm,),
            in_specs=[tile] * (3 * n) + [pl.BlockSpec((None, tm, b), lambda i, chip_ref: (chip_ref[0], i, 0))] * n
            + [pl.BlockSpec((3, tm, b), lambda i, chip_ref: (0, i, 0))] * n,
            out_specs=[tile] * (4 * n)),
        out_shape=[jax.ShapeDtypeStruct((a, b), F32)] * (4 * n),
        compiler_params=_params(1))(chip, *ws, *ms, *vs, *owns, *recvs)
    return [res[k::n] for k in range(n)]


def _sum_devices(gathered, *, name):
    _, r, c = gathered.shape

    def body(g_ref, o_ref):
        total = g_ref[0]
        for j in range(1, N_DEV):
            total = total + g_ref[j]
        o_ref[...] = total

    return pl.pallas_call(body, name=name, out_shape=jax.ShapeDtypeStruct((r, c), F32))(gathered)


def _adam_small(ws, gs, ms, vs, *, name):
    n = len(ws)

    def body(*refs):
        ins, outs = refs[:4 * n], refs[4 * n:]
        for k in range(n):
            w_ref, g_ref, m_ref, v_ref = ins[k::n]
            for o_ref, value in zip(outs[k::n], _adam_math(w_ref[...], g_ref[...], m_ref[...], v_ref[...])):
                o_ref[...] = value

    res = pl.pallas_call(body, name=name, out_shape=[jax.ShapeDtypeStruct(w.shape, F32) for w in ws] * 3)(
        *ws, *gs, *ms, *vs)
    return [res[k::n] for k in range(n)]


BIG = ("in", "a", "b", "mix", "mq", "kv", "mo", "fi", "fo")
ROW_SHARDED = ("mix", "mq", "mo")
UNSHARDED = ("a", "b")
FFN_GROUPS = 4
IN_SPLIT = (3, 4)
SMALL_ROWS = 16


class _Plan:
    FUSED = ("in",)
    GATHER_ON = {"sb_fwd": ("a", "b", "mix", "mq", "mo", "conv", "fi0"), "mm_mix": ("kv",), "mem_sublayer": ("fi1",),
                 "mm_ffn_in": ("fo",), "rms_mix_bwd": ("small",)}
    SIBLING_ON = {"mm_d_hf": ("fo", "fi"), "mm_d_merged": ("mo", "mq", "kv"), "conv_bwd": ("mix", "a", "b"),
                  "mm_d_w_in1": ("in0",), "mm_d_h0": ("in1",)}
    LATE_AT = {"mm_mix": (7, 8), "mm_ffn_in": (7, 8), "sb_fwd": (15, 16)}
    CHIPS_ON = {"mem_sublayer_bwd": ("fo",), "sb_bwd": ("fi", "mo", "mq", "kv", "mix", "a", "b"), "mm_d_h0": ("in0",),
                "rms_mix_bwd": ("in1",)}

    def __init__(self, shards, core):
        self.shards, self.core = shards, core
        self.w, self.parts, self.chip_sums, self.from_chips = {}, {}, {}, {}

    def gathering(self, k):
        return self.shards[k] if k in self.FUSED else None

    def offer(self, k, block):
        self.shards[k] = block

    def comm(self, name):
        comms = []
        if name in self.GATHER_ON:
            comms.append(_gather_comm([self.shards[k] for k in self.GATHER_ON[name]]))
        if name in self.SIBLING_ON:
            comms.append(_sibling_comm([self.parts[k] for k in self.SIBLING_ON[name]]))
        if name in self.CHIPS_ON:
            comms.append(_chips_comm([self.chip_sums[k] for k in self.CHIPS_ON[name]]))
        if not comms:
            return None
        comm = _join_comms(comms)
        comm.late_at = self.LATE_AT.get(name, comm.late_at)
        return comm

    def landed(self, name, outs):
        outs = list(outs)
        for k in self.GATHER_ON.get(name, ()):
            self.set_weight(k, outs.pop(0))
        keys = self.SIBLING_ON.get(name, ())
        if keys:
            sums = _sum_with_sibling([self.parts[k] for k in keys], [outs.pop(0) for _ in keys], self.core,
                                     name="sum_with_sibling_" + "_".join(keys))
            self.chip_sums.update(zip(keys, sums))
        for k in self.CHIPS_ON.get(name, ()):
            self.from_chips[k] = outs.pop(0)

    def set_weight(self, k, gathered):
        _, a, b = gathered.shape
        if k in ROW_SHARDED:
            gathered = gathered.reshape(1, N_DEV * a, b)
        elif k in UNSHARDED:
            gathered = jnp.transpose(gathered, (1, 0, 2)).reshape(1, a, N_DEV * b)
        elif k == "fo":
            gathered = gathered.reshape(FFN_GROUPS, N_DEV * a // FFN_GROUPS, b)
        elif k == "conv":
            n_conv = CONV_WIDTH // N_DEV
            gathered = jnp.transpose(gathered[:, :3, :n_conv], (1, 0, 2)).reshape(3, CONV_WIDTH)
        self.w[k] = gathered
        if k == "fi1":
            self.w["fi"] = jnp.concatenate([self.w["fi0"], gathered], axis=2)

    def weight(self, k):
        return self.w[k]

    def grad(self, k, g):
        _, a, b = g.shape
        if k in ROW_SHARDED:
            g = g.reshape(N_DEV, a // N_DEV, b)
        elif k in UNSHARDED:
            g = jnp.transpose(g.reshape(a, N_DEV, b // N_DEV), (1, 0, 2))
        elif k == "fo":
            g = g.reshape(N_DEV, FFN_GROUPS * a // N_DEV, b)
        self.parts[k] = g


def kernel(x, mem, norm_mix, w_in, conv_w, w_branch_a, w_branch_b, w_mix_out, norm_mem_q, norm_mem_kv, w_mem_q, w_mem_kv, w_mem_o, norm_ffn, w_ffn_in, w_ffn_out, norm_final, loss_target, m_norm_mix, m_w_in, m_conv_w, m_w_branch_a, m_w_branch_b, m_w_mix_out, m_norm_mem_q, m_norm_mem_kv, m_w_mem_q, m_w_mem_kv, m_w_mem_o, m_norm_ffn, m_w_ffn_in, m_w_ffn_out, m_norm_final, v_norm_mix, v_w_in, v_conv_w, v_w_branch_a, v_w_branch_b, v_w_mix_out, v_norm_mem_q, v_norm_mem_kv, v_w_mem_q, v_w_mem_kv, v_w_mem_o, v_norm_ffn, v_w_ffn_in, v_w_ffn_out, v_norm_final):
    d = x.shape[-1]
    xi, yi, ci = lax.axis_index("x"), lax.axis_index("y"), lax.axis_index("c")
    chip = jnp.reshape(2 * xi + yi, (1,)).astype(jnp.int32)
    dev = 4 * xi + 2 * yi + ci

    big_w = dict(zip(BIG, (w_in, w_branch_a, w_branch_b, w_mix_out, w_mem_q, w_mem_kv, w_mem_o, w_ffn_in, w_ffn_out)))
    big_m = dict(zip(BIG, (m_w_in, m_w_branch_a, m_w_branch_b, m_w_mix_out, m_w_mem_q, m_w_mem_kv, m_w_mem_o, m_w_ffn_in, m_w_ffn_out)))
    big_v = dict(zip(BIG, (v_w_in, v_w_branch_a, v_w_branch_b, v_w_mix_out, v_w_mem_q, v_w_mem_kv, v_w_mem_o, v_w_ffn_in, v_w_ffn_out)))

    flip = lambda t, k: jnp.transpose(t) if k == "fi" else t
    shards = {k: flip(big_w[k][0], k).astype(BF16) for k in BIG}
    shards["fi0"], shards["fi1"] = shards["fi"][:, :d // 2], shards["fi"][:, d // 2:]
    n_conv = conv_w.shape[-1]
    shards["conv"] = jnp.zeros((8, LANES), F32).at[:3, :n_conv].set(conv_w[0])
    plan = _Plan(shards, jnp.reshape(ci, (1,)).astype(jnp.int32))

    gains = (norm_mix, norm_mem_q, norm_mem_kv, norm_ffn, norm_final.reshape(1, d))
    dx0 = _local_step(x[0], mem[0], loss_target[0], gains, plan)

    grads, deltas, new_m, new_v = {}, {}, {}, {}
    def adam(keys, rows=None, part=None):
        cut = (lambda t: t) if rows is None else (lambda t: t[rows])
        return _adam_sharded(*[[cut(flip(src[k][0], k)) for k in keys] for src in (big_w, big_m, big_v)],
                             [plan.chip_sums[part or k] for k in keys], [plan.from_chips[part or k] for k in keys],
                             chip, name="adam_" + "_".join(keys) + (part or "")[2:])

    results = {}
    for keys in (("a", "b"), ("mix", "mq", "mo"), ("kv",), ("fi",), ("fo",)):
        results.update(zip(keys, adam(keys)))
    half = big_w["in"].shape[1] * IN_SPLIT[0] // IN_SPLIT[1]
    lo, hi = adam(("in",), slice(0, half), "in0")[0], adam(("in",), slice(half, None), "in1")[0]
    results["in"] = [jnp.concatenate(pair, axis=0) for pair in zip(lo, hi)]
    for k in BIG:
        grads[k], deltas[k], new_m[k], new_v[k] = (flip(t, k).reshape(big_w[k].shape) for t in results[k])

    total = _sum_devices(plan.weight("small"), name="sum_small")
    g_conv = lax.dynamic_slice(total[5:8, :CONV_WIDTH], (0, dev * n_conv), (3, n_conv))
    small_w = [norm_mix, norm_mem_q, norm_mem_kv, norm_ffn, norm_final.reshape(1, d), conv_w[0]]
    small_m = [m_norm_mix, m_norm_mem_q, m_norm_mem_kv, m_norm_ffn, m_norm_final.reshape(1, d), m_conv_w[0]]
    small_v = [v_norm_mix, v_norm_mem_q, v_norm_mem_kv, v_norm_ffn, v_norm_final.reshape(1, d), v_conv_w[0]]
    small_g = [total[0:1], total[1:2], total[2:3], total[3:4], total[4:5], g_conv]
    small_names = ["norm_mix", "norm_mem_q", "norm_mem_kv", "norm_ffn", "norm_final", "conv_w"]
    sg, sd, sm, sv = {}, {}, {}, {}
    small_out = _adam_small(small_w, small_g, small_m, small_v, name="adam_small")
    for nme, wv, g, (dl, nm, nv) in zip(small_names, small_w, small_g, small_out):
        shape = norm_final.shape if nme == "norm_final" else (conv_w.shape if nme == "conv_w" else wv.shape)
        sg[nme], sd[nme], sm[nme], sv[nme] = (t.reshape(shape) for t in (g, dl, nm, nv))

    def ordered(big, sml):
        return (sml["norm_mix"], big["in"], sml["conv_w"], big["a"], big["b"], big["mix"], sml["norm_mem_q"],
                sml["norm_mem_kv"], big["mq"], big["kv"], big["mo"], sml["norm_ffn"], big["fi"], big["fo"],
                sml["norm_final"])

    loss_out = total[8, 0]
    grad_x = dx0.reshape(x.shape)
    return (loss_out, grad_x, *ordered(grads, sg), *ordered(deltas, sd), *ordered(new_m, sm), *ordered(new_v, sv))
```
